```python
import math
import jax, jax.numpy as jnp
from jax import lax
import numpy as np

D_MODEL = 1024
BATCH = 16
SEQ = 2048
DEPTH = 1

HEAD_DIM = 64
RET_HEADS = 8
ATTN_HEADS = 8
ATTN_KV_HEADS = 2
ATTN_GROUP = ATTN_HEADS // ATTN_KV_HEADS
RET_WIDTH = RET_HEADS * HEAD_DIM
ATTN_WIDTH = ATTN_HEADS * HEAD_DIM
KV_WIDTH = ATTN_KV_HEADS * HEAD_DIM
MIX_WIDTH = RET_WIDTH + ATTN_WIDTH
IN_PROJ_WIDTH = 4 * RET_WIDTH + ATTN_WIDTH + 2 * KV_WIDTH
CHUNK = 128
WINDOW = 128
BLOCK = 128
PLE_DIM = 256
FFN_HIDDEN = ((8 * D_MODEL + 767) // 768) * 256
ALPHA = (2.0 * DEPTH) ** 0.25
BETA = (8.0 * DEPTH) ** -0.25
LN_EPS = 1e-5
GN_EPS = 1e-5
NEG_INF = -1e30

kernel_name = "hybrid_retention_swa_deepnorm_encoder"


def _layer_norm(x, gain, bias):
    xf = x.astype(jnp.float32)
    mu = jnp.mean(xf, axis=-1, keepdims=True)
    var = jnp.mean(jnp.square(xf - mu), axis=-1, keepdims=True)
    return (xf - mu) * lax.rsqrt(var + LN_EPS) * gain.astype(jnp.float32) + bias.astype(jnp.float32)


def _retention_one_direction(q, k, v, log_gamma):
    b, h, s, d = q.shape
    n = s // CHUNK
    q = q.reshape(b, h, n, CHUNK, d)
    k = k.reshape(b, h, n, CHUNK, d)
    v = v.reshape(b, h, n, CHUNK, d)
    idx = jnp.arange(CHUNK, dtype=jnp.float32)
    diff = idx[:, None] - idx[None, :]
    lg = log_gamma[:, None, None]
    decay_in = jnp.where(diff >= 0, jnp.exp(lg * jnp.maximum(diff, 0.0)), 0.0)
    k_decay = jnp.exp(log_gamma[:, None] * (CHUNK - 1.0 - idx)[None, :])
    q_decay = jnp.exp(log_gamma[:, None] * (idx + 1.0)[None, :])
    chunk_decay = jnp.exp(log_gamma * CHUNK)
    scores = jnp.einsum('bhncd,bhnsd->bhncs', q, k) * decay_in[None, :, None]
    y_inner = jnp.einsum('bhncs,bhnse->bhnce', scores, v)
    kv = jnp.einsum('bhnsd,bhnse->bhnde', k * k_decay[None, :, None, :, None], v)

    def step(state, kv_n):
        return state * chunk_decay[None, :, None, None] + kv_n, state

    _, r_prev = lax.scan(step, jnp.zeros((b, h, d, d), q.dtype), jnp.moveaxis(kv, 2, 0))
    r_prev = jnp.moveaxis(r_prev, 0, 2)
    y_cross = jnp.einsum('bhncd,bhnde->bhnce', q * q_decay[None, :, None, :, None], r_prev)
    return (y_inner + y_cross).reshape(b, h, s, d)


def _bidirectional_retention(q, k, v, log_gamma_fwd, log_gamma_bwd, gn_gain, gate):
    b, s, _ = q.shape
    to_heads = lambda t: jnp.transpose(t.astype(jnp.float32).reshape(b, s, RET_HEADS, HEAD_DIM), (0, 2, 1, 3))
    qh, kh, vh = to_heads(q), to_heads(k) * (HEAD_DIM ** -0.5), to_heads(v)
    y_f = _retention_one_direction(qh, kh, vh, log_gamma_fwd)
    flip = lambda t: jnp.flip(t, axis=2)
    y_b = flip(_retention_one_direction(flip(qh), flip(kh), flip(vh), log_gamma_bwd))
    y = y_f + y_b
    mu = jnp.mean(y, axis=-1, keepdims=True)
    var = jnp.mean(jnp.square(y - mu), axis=-1, keepdims=True)
    y = (y - mu) * lax.rsqrt(var + GN_EPS)
    y = jnp.transpose(y, (0, 2, 1, 3)).reshape(b, s, RET_WIDTH) * gn_gain.astype(jnp.float32)
    return y * jax.nn.silu(gate.astype(jnp.float32))


def _alibi_slopes(n_heads):
    return 2.0 ** (-8.0 * (jnp.arange(n_heads, dtype=jnp.float32) + 1.0) / n_heads)


def _windowed_gqa(q, k, v, sink):
    b, s, _ = q.shape
    n = s // BLOCK
    qb = q.astype(jnp.float32).reshape(b, n, BLOCK, ATTN_KV_HEADS, ATTN_GROUP, HEAD_DIM)

    def neighbour_blocks(t):
        tp = jnp.pad(t.astype(jnp.float32).reshape(b, s, ATTN_KV_HEADS, HEAD_DIM),
                     ((0, 0), (BLOCK, BLOCK), (0, 0), (0, 0)))
        tp = tp.reshape(b, n + 2, BLOCK, ATTN_KV_HEADS, HEAD_DIM)
        return jnp.concatenate([tp[:, :-2], tp[:, 1:-1], tp[:, 2:]], axis=2)

    kb, vb = neighbour_blocks(k), neighbour_blocks(v)
    scores = jnp.einsum('bnqhgd,bnshd->bnhgqs', qb, kb) * (HEAD_DIM ** -0.5)
    qi = jnp.arange(BLOCK)
    kj = jnp.arange(3 * BLOCK)
    dist = jnp.abs(kj[None, :] - BLOCK - qi[:, None])
    key_pos = jnp.arange(n)[:, None] * BLOCK - BLOCK + kj[None, :]
    valid = (dist <= WINDOW)[None] & ((key_pos >= 0) & (key_pos < s))[:, None, :]
    slopes = _alibi_slopes(ATTN_HEADS).reshape(ATTN_KV_HEADS, ATTN_GROUP)
    alibi = -slopes[:, :, None, None] * dist.astype(jnp.float32)[None, None]
    scores = jnp.where(valid[None, :, None, None], scores + alibi[None, None], NEG_INF)
    sink_l = sink.astype(jnp.float32).reshape(ATTN_KV_HEADS, ATTN_GROUP)[None, None, :, :, None, None]
    m = jnp.maximum(jnp.max(scores, axis=-1, keepdims=True), sink_l)
    e = jnp.exp(scores - m)
    denom = jnp.sum(e, axis=-1, keepdims=True) + jnp.exp(sink_l - m)
    probs = e / denom
    out = jnp.einsum('bnhgqs,bnshd->bnqhgd', probs, vb)
    return out.reshape(b, s, ATTN_WIDTH)


def _fwd_setup_inputs(seed: int = 0) -> dict:
    key = jax.random.key(seed)
    ks = jax.random.split(key, 18)
    nrm = lambda k, shape, scale: jax.random.normal(k, shape, jnp.float32) * scale
    base_log2 = -5.0 - jnp.arange(RET_HEADS, dtype=jnp.float32)
    return {
        "x": nrm(ks[0], (BATCH, SEQ, D_MODEL), 1.0),
        "p": nrm(ks[1], (DEPTH, BATCH, SEQ, PLE_DIM), 1.0),
        "w_in": nrm(ks[2], (DEPTH, D_MODEL, IN_PROJ_WIDTH), D_MODEL ** -0.5),
        "ret_decay_fwd": base_log2[None] + nrm(ks[3], (DEPTH, RET_HEADS), 0.1),
        "ret_decay_bwd": base_log2[None] + nrm(ks[4], (DEPTH, RET_HEADS), 0.1),
        "ret_gn_gain": 1.0 + nrm(ks[5], (DEPTH, RET_WIDTH), 0.02),
        "attn_sink": nrm(ks[6], (DEPTH, ATTN_HEADS), 0.5),
        "w_out": nrm(ks[7], (DEPTH, MIX_WIDTH, D_MODEL), BETA * MIX_WIDTH ** -0.5),
        "ln1_gain": 1.0 + nrm(ks[8], (DEPTH, D_MODEL), 0.02),
        "ln1_bias": nrm(ks[9], (DEPTH, D_MODEL), 0.02),
        "w_ffn_gate": nrm(ks[10], (DEPTH, D_MODEL, FFN_HIDDEN), D_MODEL ** -0.5),
        "w_ffn_up": nrm(ks[11], (DEPTH, D_MODEL, FFN_HIDDEN), D_MODEL ** -0.5),
        "w_ffn_down": nrm(ks[12], (DEPTH, FFN_HIDDEN, D_MODEL), BETA * FFN_HIDDEN ** -0.5),
        "w_ple_proj": nrm(ks[13], (DEPTH, PLE_DIM, D_MODEL), BETA * PLE_DIM ** -0.5),
        "w_ple_gate": nrm(ks[14], (DEPTH, D_MODEL, D_MODEL), D_MODEL ** -0.5),
        "ln2_gain": 1.0 + nrm(ks[15], (DEPTH, D_MODEL), 0.02),
        "ln2_bias": nrm(ks[16], (DEPTH, D_MODEL), 0.02),
    }


def _fwd_reference(x, p, w_in, ret_decay_fwd, ret_decay_bwd, ret_gn_gain, attn_sink, w_out,
              ln1_gain, ln1_bias, w_ffn_gate, w_ffn_up, w_ffn_down, w_ple_proj, w_ple_gate,
              ln2_gain, ln2_bias):
    split_points = [RET_WIDTH, 2 * RET_WIDTH, 3 * RET_WIDTH, 4 * RET_WIDTH,
                    4 * RET_WIDTH + ATTN_WIDTH, 4 * RET_WIDTH + ATTN_WIDTH + KV_WIDTH]
    h = x.astype(jnp.float32)
    for i in range(DEPTH):
        u = jnp.einsum('bsd,de->bse', h, w_in[i].astype(jnp.float32))
        rq, rk, rv, rg, aq, ak, av = jnp.split(u, split_points, axis=-1)
        lg_f = jnp.log1p(-jnp.exp2(ret_decay_fwd[i].astype(jnp.float32)))
        lg_b = jnp.log1p(-jnp.exp2(ret_decay_bwd[i].astype(jnp.float32)))
        y_ret = _bidirectional_retention(rq, rk, rv, lg_f, lg_b, ret_gn_gain[i], rg)
        y_att = _windowed_gqa(aq, ak, av, attn_sink[i])
        mix = jnp.einsum('bse,ed->bsd', jnp.concatenate([y_ret, y_att], axis=-1),
                         w_out[i].astype(jnp.float32))
        h = _layer_norm(ALPHA * h + mix, ln1_gain[i], ln1_bias[i])
        g = jnp.einsum('bsd,df->bsf', h, w_ffn_gate[i].astype(jnp.float32))
        up = jnp.einsum('bsd,df->bsf', h, w_ffn_up[i].astype(jnp.float32))
        ffn = jnp.einsum('bsf,fd->bsd', jax.nn.silu(g) * up, w_ffn_down[i].astype(jnp.float32))
        ple = jnp.einsum('bsr,rd->bsd', p[i].astype(jnp.float32), w_ple_proj[i].astype(jnp.float32))
        ple_gate = jax.nn.sigmoid(jnp.einsum('bsd,de->bse', h, w_ple_gate[i].astype(jnp.float32)))
        h = _layer_norm(ALPHA * h + ffn + ple_gate * ple, ln2_gain[i], ln2_bias[i])
    return h.astype(x.dtype)


import jax as _jax
import jax.numpy as _jnp

TWIN_FORMAT = 'train_step'
FWD_PARAMS = ['x', 'p', 'w_in', 'ret_decay_fwd', 'ret_decay_bwd', 'ret_gn_gain', 'attn_sink', 'w_out', 'ln1_gain', 'ln1_bias', 'w_ffn_gate', 'w_ffn_up', 'w_ffn_down', 'w_ple_proj', 'w_ple_gate', 'ln2_gain', 'ln2_bias']
TWIN_WEIGHTS = ['w_in', 'ret_decay_fwd', 'ret_decay_bwd', 'ret_gn_gain', 'attn_sink', 'w_out', 'ln1_gain', 'ln1_bias', 'w_ffn_gate', 'w_ffn_up', 'w_ffn_down', 'w_ple_proj', 'w_ple_gate', 'ln2_gain', 'ln2_bias']
TWIN_DIFF_INPUT = 'x'
TWIN_INPUTS = ['x', 'p', 'w_in', 'ret_decay_fwd', 'ret_decay_bwd', 'ret_gn_gain', 'attn_sink', 'w_out', 'ln1_gain', 'ln1_bias', 'w_ffn_gate', 'w_ffn_up', 'w_ffn_down', 'w_ple_proj', 'w_ple_gate', 'ln2_gain', 'ln2_bias', 'loss_target', 'm_w_in', 'm_ret_decay_fwd', 'm_ret_decay_bwd', 'm_ret_gn_gain', 'm_attn_sink', 'm_w_out', 'm_ln1_gain', 'm_ln1_bias', 'm_w_ffn_gate', 'm_w_ffn_up', 'm_w_ffn_down', 'm_w_ple_proj', 'm_w_ple_gate', 'm_ln2_gain', 'm_ln2_bias', 'v_w_in', 'v_ret_decay_fwd', 'v_ret_decay_bwd', 'v_ret_gn_gain', 'v_attn_sink', 'v_w_out', 'v_ln1_gain', 'v_ln1_bias', 'v_w_ffn_gate', 'v_w_ffn_up', 'v_w_ffn_down', 'v_w_ple_proj', 'v_w_ple_gate', 'v_ln2_gain', 'v_ln2_bias']
TWIN_OUTPUTS = ['loss', 'grad_x', 'grad_w_in', 'grad_ret_decay_fwd', 'grad_ret_decay_bwd', 'grad_ret_gn_gain', 'grad_attn_sink', 'grad_w_out', 'grad_ln1_gain', 'grad_ln1_bias', 'grad_w_ffn_gate', 'grad_w_ffn_up', 'grad_w_ffn_down', 'grad_w_ple_proj', 'grad_w_ple_gate', 'grad_ln2_gain', 'grad_ln2_bias', 'delta_w_in', 'delta_ret_decay_fwd', 'delta_ret_decay_bwd', 'delta_ret_gn_gain', 'delta_attn_sink', 'delta_w_out', 'delta_ln1_gain', 'delta_ln1_bias', 'delta_w_ffn_gate', 'delta_w_ffn_up', 'delta_w_ffn_down', 'delta_w_ple_proj', 'delta_w_ple_gate', 'delta_ln2_gain', 'delta_ln2_bias', 'new_m_w_in', 'new_m_ret_decay_fwd', 'new_m_ret_decay_bwd', 'new_m_ret_gn_gain', 'new_m_attn_sink', 'new_m_w_out', 'new_m_ln1_gain', 'new_m_ln1_bias', 'new_m_w_ffn_gate', 'new_m_w_ffn_up', 'new_m_w_ffn_down', 'new_m_w_ple_proj', 'new_m_w_ple_gate', 'new_m_ln2_gain', 'new_m_ln2_bias', 'new_v_w_in', 'new_v_ret_decay_fwd', 'new_v_ret_decay_bwd', 'new_v_ret_gn_gain', 'new_v_attn_sink', 'new_v_w_out', 'new_v_ln1_gain', 'new_v_ln1_bias', 'new_v_w_ffn_gate', 'new_v_w_ffn_up', 'new_v_w_ffn_down', 'new_v_w_ple_proj', 'new_v_w_ple_gate', 'new_v_ln2_gain', 'new_v_ln2_bias']
TWIN_LEAF_KINDS = {'loss': 'loss', 'grad_x': 'grad_x', 'grad_w_in': 'grad_w', 'grad_ret_decay_fwd': 'grad_w', 'grad_ret_decay_bwd': 'grad_w', 'grad_ret_gn_gain': 'grad_w', 'grad_attn_sink': 'grad_w', 'grad_w_out': 'grad_w', 'grad_ln1_gain': 'grad_w', 'grad_ln1_bias': 'grad_w', 'grad_w_ffn_gate': 'grad_w', 'grad_w_ffn_up': 'grad_w', 'grad_w_ffn_down': 'grad_w', 'grad_w_ple_proj': 'grad_w', 'grad_w_ple_gate': 'grad_w', 'grad_ln2_gain': 'grad_w', 'grad_ln2_bias': 'grad_w', 'delta_w_in': 'delta_w', 'delta_ret_decay_fwd': 'delta_w', 'delta_ret_decay_bwd': 'delta_w', 'delta_ret_gn_gain': 'delta_w', 'delta_attn_sink': 'delta_w', 'delta_w_out': 'delta_w', 'delta_ln1_gain': 'delta_w', 'delta_ln1_bias': 'delta_w', 'delta_w_ffn_gate': 'delta_w', 'delta_w_ffn_up': 'delta_w', 'delta_w_ffn_down': 'delta_w', 'delta_w_ple_proj': 'delta_w', 'delta_w_ple_gate': 'delta_w', 'delta_ln2_gain': 'delta_w', 'delta_ln2_bias': 'delta_w', 'new_m_w_in': 'new_m', 'new_m_ret_decay_fwd': 'new_m', 'new_m_ret_decay_bwd': 'new_m', 'new_m_ret_gn_gain': 'new_m', 'new_m_attn_sink': 'new_m', 'new_m_w_out': 'new_m', 'new_m_ln1_gain': 'new_m', 'new_m_ln1_bias': 'new_m', 'new_m_w_ffn_gate': 'new_m', 'new_m_w_ffn_up': 'new_m', 'new_m_w_ffn_down': 'new_m', 'new_m_w_ple_proj': 'new_m', 'new_m_w_ple_gate': 'new_m', 'new_m_ln2_gain': 'new_m', 'new_m_ln2_bias': 'new_m', 'new_v_w_in': 'new_v', 'new_v_ret_decay_fwd': 'new_v', 'new_v_ret_decay_bwd': 'new_v', 'new_v_ret_gn_gain': 'new_v', 'new_v_attn_sink': 'new_v', 'new_v_w_out': 'new_v', 'new_v_ln1_gain': 'new_v', 'new_v_ln1_bias': 'new_v', 'new_v_w_ffn_gate': 'new_v', 'new_v_w_ffn_up': 'new_v', 'new_v_w_ffn_down': 'new_v', 'new_v_w_ple_proj': 'new_v', 'new_v_w_ple_gate': 'new_v', 'new_v_ln2_gain': 'new_v', 'new_v_ln2_bias': 'new_v'}


def _forward(args):
    return _fwd_reference(*[args[k] for k in FWD_PARAMS])


def _output_shape():
    out = _jax.eval_shape(lambda: _forward(_fwd_setup_inputs(0)))
    return out.shape, out.dtype

N_MICROBATCH = 1
ADAM_LR = 0.001
ADAM_B1 = 0.9
ADAM_B2 = 0.999
ADAM_EPS = 1e-08
ADAM_WD = 0.01
ADAM_STEP = 10
PER_EXAMPLE_BATCH_AXIS = {'x': 0, 'p': 1, 'loss_target': 0}
SHARED_INPUTS = []
_WEIGHT_DTYPES = {'w_in': _jnp.float32, 'ret_decay_fwd': _jnp.float32, 'ret_decay_bwd': _jnp.float32, 'ret_gn_gain': _jnp.float32, 'attn_sink': _jnp.float32, 'w_out': _jnp.float32, 'ln1_gain': _jnp.float32, 'ln1_bias': _jnp.float32, 'w_ffn_gate': _jnp.float32, 'w_ffn_up': _jnp.float32, 'w_ffn_down': _jnp.float32, 'w_ple_proj': _jnp.float32, 'w_ple_gate': _jnp.float32, 'ln2_gain': _jnp.float32, 'ln2_bias': _jnp.float32}
MOMENT_SCALE = {'w_in': 4.717129e-02, 'ret_decay_fwd': 1.110379e-01, 'ret_decay_bwd': 5.574479e-02, 'ret_gn_gain': 5.171904e-02, 'attn_sink': 2.815120e-02, 'w_out': 6.663984e-02, 'ln1_gain': 8.678218e-01, 'ln1_bias': 3.956444e-01, 'w_ffn_gate': 3.055913e-02, 'w_ffn_up': 2.950797e-02, 'w_ffn_down': 8.243783e-02, 'w_ple_proj': 7.499717e-02, 'w_ple_gate': 1.740530e-02, 'ln2_gain': 3.197777e+01, 'ln2_bias': 6.438237e-01}


def _to_microbatches(a, axis):
    t = _jnp.moveaxis(a, axis, 0)
    t = t.reshape((N_MICROBATCH, t.shape[0] // N_MICROBATCH) + t.shape[1:])
    return _jnp.moveaxis(t, 1, axis + 1)


def setup_inputs(seed: int = 0) -> dict:
    inp = _fwd_setup_inputs(seed)
    key = _jax.random.fold_in(_jax.random.key(seed), 7919)
    shape, _ = _output_shape()
    out = dict(inp)
    out["loss_target"] = _jax.random.normal(_jax.random.fold_in(key, 0), shape, _jnp.float32)
    for i, name in enumerate(TWIN_WEIGHTS):
        w = inp[name].astype(_jnp.float32)
        if MOMENT_SCALE is None:
            s = _jnp.sqrt(_jnp.mean(_jnp.square(w)) + 1e-30)
        else:
            s = MOMENT_SCALE[name]
        km, kv = _jax.random.split(_jax.random.fold_in(key, i + 1))
        out[name] = w
        out["m_" + name] = s * _jax.random.normal(km, w.shape, _jnp.float32)
        out["v_" + name] = (s * s) * _jax.random.uniform(kv, w.shape, _jnp.float32, 0.5, 1.5)
    if N_MICROBATCH > 1:
        for name, axis in PER_EXAMPLE_BATCH_AXIS.items():
            out[name] = _to_microbatches(out[name], axis)
    return {'x': out['x'], 'p': out['p'], 'w_in': out['w_in'], 'ret_decay_fwd': out['ret_decay_fwd'], 'ret_decay_bwd': out['ret_decay_bwd'], 'ret_gn_gain': out['ret_gn_gain'], 'attn_sink': out['attn_sink'], 'w_out': out['w_out'], 'ln1_gain': out['ln1_gain'], 'ln1_bias': out['ln1_bias'], 'w_ffn_gate': out['w_ffn_gate'], 'w_ffn_up': out['w_ffn_up'], 'w_ffn_down': out['w_ffn_down'], 'w_ple_proj': out['w_ple_proj'], 'w_ple_gate': out['w_ple_gate'], 'ln2_gain': out['ln2_gain'], 'ln2_bias': out['ln2_bias'], 'loss_target': out['loss_target'], 'm_w_in': out['m_w_in'], 'm_ret_decay_fwd': out['m_ret_decay_fwd'], 'm_ret_decay_bwd': out['m_ret_decay_bwd'], 'm_ret_gn_gain': out['m_ret_gn_gain'], 'm_attn_sink': out['m_attn_sink'], 'm_w_out': out['m_w_out'], 'm_ln1_gain': out['m_ln1_gain'], 'm_ln1_bias': out['m_ln1_bias'], 'm_w_ffn_gate': out['m_w_ffn_gate'], 'm_w_ffn_up': out['m_w_ffn_up'], 'm_w_ffn_down': out['m_w_ffn_down'], 'm_w_ple_proj': out['m_w_ple_proj'], 'm_w_ple_gate': out['m_w_ple_gate'], 'm_ln2_gain': out['m_ln2_gain'], 'm_ln2_bias': out['m_ln2_bias'], 'v_w_in': out['v_w_in'], 'v_ret_decay_fwd': out['v_ret_decay_fwd'], 'v_ret_decay_bwd': out['v_ret_decay_bwd'], 'v_ret_gn_gain': out['v_ret_gn_gain'], 'v_attn_sink': out['v_attn_sink'], 'v_w_out': out['v_w_out'], 'v_ln1_gain': out['v_ln1_gain'], 'v_ln1_bias': out['v_ln1_bias'], 'v_w_ffn_gate': out['v_w_ffn_gate'], 'v_w_ffn_up': out['v_w_ffn_up'], 'v_w_ffn_down': out['v_w_ffn_down'], 'v_w_ple_proj': out['v_w_ple_proj'], 'v_w_ple_gate': out['v_w_ple_gate'], 'v_ln2_gain': out['v_ln2_gain'], 'v_ln2_bias': out['v_ln2_bias']}


def _loss(weights, diff, rest, loss_target):
    with _jax.named_scope("forward"):
        args = {**rest, TWIN_DIFF_INPUT: diff, **{k: w.astype(_WEIGHT_DTYPES[k]) for k, w in weights.items()}}
        y = _forward(args)
    with _jax.named_scope("loss_head"):
        err = _jnp.square(y.astype(_jnp.float32) - loss_target)
        return 0.5 * _jnp.sum(_jnp.mean(err, axis=-1)) if err.ndim else 0.5 * err


def _adamw(w, g, m, v):
    m = ADAM_B1 * m + (1.0 - ADAM_B1) * g
    v = ADAM_B2 * v + (1.0 - ADAM_B2) * _jnp.square(g)
    m_hat = m / (1.0 - ADAM_B1 ** ADAM_STEP)
    v_hat = v / (1.0 - ADAM_B2 ** ADAM_STEP)
    delta = -ADAM_LR * (m_hat / (_jnp.sqrt(v_hat) + ADAM_EPS) + ADAM_WD * w)
    return delta, m, v


def reference(x, p, w_in, ret_decay_fwd, ret_decay_bwd, ret_gn_gain, attn_sink, w_out, ln1_gain, ln1_bias, w_ffn_gate, w_ffn_up, w_ffn_down, w_ple_proj, w_ple_gate, ln2_gain, ln2_bias, loss_target, m_w_in, m_ret_decay_fwd, m_ret_decay_bwd, m_ret_gn_gain, m_attn_sink, m_w_out, m_ln1_gain, m_ln1_bias, m_w_ffn_gate, m_w_ffn_up, m_w_ffn_down, m_w_ple_proj, m_w_ple_gate, m_ln2_gain, m_ln2_bias, v_w_in, v_ret_decay_fwd, v_ret_decay_bwd, v_ret_gn_gain, v_attn_sink, v_w_out, v_ln1_gain, v_ln1_bias, v_w_ffn_gate, v_w_ffn_up, v_w_ffn_down, v_w_ple_proj, v_w_ple_gate, v_ln2_gain, v_ln2_bias):
    given = dict(x=x, p=p, w_in=w_in, ret_decay_fwd=ret_decay_fwd, ret_decay_bwd=ret_decay_bwd, ret_gn_gain=ret_gn_gain, attn_sink=attn_sink, w_out=w_out, ln1_gain=ln1_gain, ln1_bias=ln1_bias, w_ffn_gate=w_ffn_gate, w_ffn_up=w_ffn_up, w_ffn_down=w_ffn_down, w_ple_proj=w_ple_proj, w_ple_gate=w_ple_gate, ln2_gain=ln2_gain, ln2_bias=ln2_bias, loss_target=loss_target, m_w_in=m_w_in, m_ret_decay_fwd=m_ret_decay_fwd, m_ret_decay_bwd=m_ret_decay_bwd, m_ret_gn_gain=m_ret_gn_gain, m_attn_sink=m_attn_sink, m_w_out=m_w_out, m_ln1_gain=m_ln1_gain, m_ln1_bias=m_ln1_bias, m_w_ffn_gate=m_w_ffn_gate, m_w_ffn_up=m_w_ffn_up, m_w_ffn_down=m_w_ffn_down, m_w_ple_proj=m_w_ple_proj, m_w_ple_gate=m_w_ple_gate, m_ln2_gain=m_ln2_gain, m_ln2_bias=m_ln2_bias, v_w_in=v_w_in, v_ret_decay_fwd=v_ret_decay_fwd, v_ret_decay_bwd=v_ret_decay_bwd, v_ret_gn_gain=v_ret_gn_gain, v_attn_sink=v_attn_sink, v_w_out=v_w_out, v_ln1_gain=v_ln1_gain, v_ln1_bias=v_ln1_bias, v_w_ffn_gate=v_w_ffn_gate, v_w_ffn_up=v_w_ffn_up, v_w_ffn_down=v_w_ffn_down, v_w_ple_proj=v_w_ple_proj, v_w_ple_gate=v_w_ple_gate, v_ln2_gain=v_ln2_gain, v_ln2_bias=v_ln2_bias)
    weights = {n: given[n] for n in TWIN_WEIGHTS}
    shared = {n: given[n] for n in SHARED_INPUTS}
    per_example = {n: given[n] for n in ['x', 'p']}
    grad_fn = _jax.value_and_grad(_loss, argnums=(0, 1))

    def one_microbatch(ex, loss_target):
        ex = dict(ex)
        diff = ex.pop(TWIN_DIFF_INPUT)
        return grad_fn(weights, diff, {**shared, **ex}, loss_target)

    if N_MICROBATCH == 1:
        loss, (grad_w, grad_x) = one_microbatch(per_example, given["loss_target"])
    else:
        def body(carry, xs):
            loss_sum, grad_sum = carry
            l_k, (gw_k, gx_k) = one_microbatch(xs[0], xs[1])
            with _jax.named_scope("update"):
                return (loss_sum + l_k, _jax.tree.map(_jnp.add, grad_sum, gw_k)), gx_k

        init = (_jnp.zeros((), _jnp.float32), _jax.tree.map(_jnp.zeros_like, weights))
        (loss, grad_w), grad_x = _jax.lax.scan(body, init, (per_example, given["loss_target"]))
    with _jax.named_scope("update"):
        delta_w, new_m, new_v = {}, {}, {}
        for n in TWIN_WEIGHTS:
            delta_w[n], new_m[n], new_v[n] = _adamw(weights[n], grad_w[n], given["m_" + n], given["v_" + n])
    return (loss, grad_x, *[grad_w[n] for n in TWIN_WEIGHTS], *[delta_w[n] for n in TWIN_WEIGHTS],
            *[new_m[n] for n in TWIN_WEIGHTS], *[new_v[n] for n in TWIN_WEIGHTS])
```

```python
import functools
import math

import jax
import jax.numpy as jnp
from jax import lax
from jax.experimental import pallas as pl
from jax.experimental.pallas import tpu as pltpu

F32 = jnp.float32
BF16 = jnp.bfloat16

D_MODEL = 1024
HEAD_DIM = 64
RET_HEADS = 8
ATTN_HEADS = 8
RET_WIDTH = 512
ATTN_WIDTH = 512
KV_WIDTH = 128
IN_WIDTH = 2816
FFN = 2816
N_SHARD = 4
FFN_SHARD = FFN // N_SHARD
PLE_DIM = 256
CHUNK = 128
LANES = 128
ALPHA = 2.0 ** 0.25
LN_EPS = 1e-5
GN_EPS = 1e-5
NEG_INF = -1e30
ADAM_LR = 0.001
ADAM_B1 = 0.9
ADAM_B2 = 0.999
ADAM_EPS = 1e-08
ADAM_WD = 0.01
ADAM_STEP = 10
VMEM_LIMIT = 56 * 1024 * 1024
MESH = pl.DeviceIdType.MESH

CB_RQ, CB_RK, CB_RV, CB_RG, CB_AQ, CB_AK, CB_AV = 0, 4, 8, 12, 16, 20, 21


def _dot(a, b):
    return jnp.dot(a, b, preferred_element_type=F32)


def _dot_nt(a, b):
    return lax.dot_general(a, b, (((1,), (1,)), ((), ())), preferred_element_type=F32)


def _dot_tn(a, b):
    return lax.dot_general(a, b, (((0,), (0,)), ((), ())), preferred_element_type=F32)


def _sigmoid(x):
    return 1.0 / (1.0 + jnp.exp(-x))


def _params(*sem, vmem=None):
    return pltpu.CompilerParams(dimension_semantics=tuple(sem) if sem else None, vmem_limit_bytes=vmem)


def _head_mean(x, m0):
    s0 = jnp.sum(jnp.where(m0, x, 0.0), axis=1, keepdims=True)
    s1 = jnp.sum(jnp.where(m0, 0.0, x), axis=1, keepdims=True)
    return jnp.where(m0, s0, s1) * (1.0 / HEAD_DIM)


def _inproj(x2d, w_in):
    t = x2d.shape[0]
    tm = 512
    nb = 256

    def body(x_ref, w_ref, o_ref):
        xb = x_ref[...].astype(BF16)
        for n in range(0, IN_WIDTH, nb):
            o_ref[:, n:n + nb] = _dot(xb, w_ref[:, n:n + nb]).astype(BF16)

    return pl.pallas_call(
        body, name="inproj", grid=(t // tm,),
        in_specs=[pl.BlockSpec((tm, D_MODEL), lambda i: (i, 0)),
                  pl.BlockSpec((D_MODEL, IN_WIDTH), lambda i: (0, 0))],
        out_specs=pl.BlockSpec((tm, IN_WIDTH), lambda i: (i, 0)),
        out_shape=jax.ShapeDtypeStruct((t, IN_WIDTH), BF16),
        compiler_params=_params("parallel", vmem=VMEM_LIMIT),
    )(x2d, w_in)


def _outproj_ln1(y_ret, y_att, x2d, w_out, gain, bias):
    t = x2d.shape[0]
    tm = 512

    def body(yr_ref, ya_ref, x_ref, w_ref, g_ref, b_ref, zh_ref, r_ref, hb_ref):
        mix = _dot(yr_ref[...], w_ref[0:RET_WIDTH, :]) + _dot(ya_ref[...], w_ref[RET_WIDTH:, :])
        z = ALPHA * x_ref[...] + mix
        mu = jnp.mean(z, axis=1, keepdims=True)
        zc = z - mu
        var = jnp.mean(zc * zc, axis=1, keepdims=True)
        r = lax.rsqrt(var + LN_EPS)
        zh = zc * r
        zh_ref[...] = zh
        r_ref[...] = r
        hb_ref[...] = (zh * g_ref[...] + b_ref[...]).astype(BF16)

    row = lambda w: pl.BlockSpec((tm, w), lambda i: (i, 0))
    const = lambda s: pl.BlockSpec(s, lambda i: (0, 0))
    return pl.pallas_call(
        body, name="outproj_ln1", grid=(t // tm,),
        in_specs=[row(RET_WIDTH), row(ATTN_WIDTH), row(D_MODEL), const((D_MODEL, D_MODEL)),
                  const((1, D_MODEL)), const((1, D_MODEL))],
        out_specs=[row(D_MODEL), row(1), row(D_MODEL)],
        out_shape=[jax.ShapeDtypeStruct((t, D_MODEL), F32), jax.ShapeDtypeStruct((t, 1), F32),
                   jax.ShapeDtypeStruct((t, D_MODEL), BF16)],
        compiler_params=_params("parallel", vmem=VMEM_LIMIT),
    )(y_ret, y_att, x2d, w_out, gain, bias)


def _load_resident(step, pairs):
    @pl.when(step == 0)
    def _():
        for src, dst in pairs:
            pltpu.sync_copy(src, dst)


def _ffn_fwd(zh1, hb, p2d, tgt, g1, b1, g2, b2, wg4, wu4, wd4, wpe, wpg):
    t = zh1.shape[0]
    tm = 256

    def body(zh_ref, hb_ref, p_ref, t_ref, g1_ref, b1_ref, g2_ref, b2_ref,
             wg_hbm, wu_hbm, wd_hbm, wpe_hbm, wpg_hbm,
             dz_ref, gs_ref, us_ref, pg_ref, ple_ref, loss_ref, dg2_ref, db2_ref,
             wg, wu, wd, wpe, wpg):
        step = pl.program_id(0)
        _load_resident(step, [(wg_hbm, wg), (wu_hbm, wu), (wd_hbm, wd), (wpe_hbm, wpe), (wpg_hbm, wpg)])

        @pl.when(step == 0)
        def _():
            loss_ref[...] = jnp.zeros_like(loss_ref)
            dg2_ref[...] = jnp.zeros_like(dg2_ref)
            db2_ref[...] = jnp.zeros_like(db2_ref)

        h1 = zh_ref[...] * g1_ref[...] + b1_ref[...]
        hbv = hb_ref[...]
        ffn = jnp.zeros((tm, D_MODEL), F32)
        for j in range(N_SHARD):
            gj = _dot(hbv, wg[j])
            uj = _dot(hbv, wu[j])
            gs_ref[j] = gj.astype(BF16)
            us_ref[j] = uj.astype(BF16)
            act = (gj * _sigmoid(gj) * uj).astype(BF16)
            ffn = ffn + _dot(act, wd[j])
        ple = _dot(p_ref[...].astype(BF16), wpe[...])
        pg = _sigmoid(_dot(hbv, wpg[...]))
        pg_ref[...] = pg.astype(BF16)
        ple_ref[...] = ple.astype(BF16)
        z2 = ALPHA * h1 + ffn + pg * ple
        mu = jnp.mean(z2, axis=1, keepdims=True)
        zc = z2 - mu
        var = jnp.mean(zc * zc, axis=1, keepdims=True)
        r = lax.rsqrt(var + LN_EPS)
        zh2 = zc * r
        err = zh2 * g2_ref[...] + b2_ref[...] - t_ref[...]
        loss_ref[...] += jnp.sum(err * err)
        dy = err * (1.0 / D_MODEL)
        dg2_ref[...] += jnp.sum(dy * zh2, axis=0, keepdims=True)
        db2_ref[...] += jnp.sum(dy, axis=0, keepdims=True)
        dzh = dy * g2_ref[...]
        m1 = jnp.mean(dzh, axis=1, keepdims=True)
        m2 = jnp.mean(dzh * zh2, axis=1, keepdims=True)
        dz_ref[...] = r * (dzh - m1 - zh2 * m2)

    row = lambda w: pl.BlockSpec((tm, w), lambda i: (i, 0))
    const = lambda s: pl.BlockSpec(s, lambda i: (0, 0))
    sh = pl.BlockSpec((N_SHARD, tm, FFN_SHARD), lambda i: (0, i, 0))
    hbm = pl.BlockSpec(memory_space=pl.ANY)
    return pl.pallas_call(
        body, name="ffn_fwd", grid=(t // tm,),
        in_specs=[row(D_MODEL), row(D_MODEL), row(PLE_DIM), row(D_MODEL),
                  const((1, D_MODEL)), const((1, D_MODEL)), const((1, D_MODEL)), const((1, D_MODEL)),
                  hbm, hbm, hbm, hbm, hbm],
        out_specs=[row(D_MODEL), sh, sh, row(D_MODEL), row(D_MODEL),
                   const((8, LANES)), const((1, D_MODEL)), const((1, D_MODEL))],
        out_shape=[jax.ShapeDtypeStruct((t, D_MODEL), F32),
                   jax.ShapeDtypeStruct((N_SHARD, t, FFN_SHARD), BF16),
                   jax.ShapeDtypeStruct((N_SHARD, t, FFN_SHARD), BF16),
                   jax.ShapeDtypeStruct((t, D_MODEL), BF16), jax.ShapeDtypeStruct((t, D_MODEL), BF16),
                   jax.ShapeDtypeStruct((8, LANES), F32),
                   jax.ShapeDtypeStruct((1, D_MODEL), F32), jax.ShapeDtypeStruct((1, D_MODEL), F32)],
        scratch_shapes=[pltpu.VMEM(wg4.shape, BF16), pltpu.VMEM(wu4.shape, BF16), pltpu.VMEM(wd4.shape, BF16),
                        pltpu.VMEM(wpe.shape, BF16), pltpu.VMEM(wpg.shape, BF16)],
        compiler_params=_params("arbitrary", vmem=VMEM_LIMIT),
    )(zh1, hb, p2d, tgt, g1, b1, g2, b2, wg4, wu4, wd4, wpe, wpg)


def _ret_tables(lgf, lgb):
    c = CHUNK
    row = lax.broadcasted_iota(jnp.int32, (c, LANES), 0).astype(F32)
    ii = lax.broadcasted_iota(jnp.int32, (c, c), 0).astype(F32)
    jj = lax.broadcasted_iota(jnp.int32, (c, c), 1).astype(F32)
    diff = ii - jj
    dmats = []
    for h in range(2):
        lf = lgf[:, h * HEAD_DIM:h * HEAD_DIM + 1]
        lb = lgb[:, h * HEAD_DIM:h * HEAD_DIM + 1]
        dmats.append(jnp.where(diff > 0, jnp.exp(lf * jnp.maximum(diff, 0.0)),
                               jnp.where(diff < 0, jnp.exp(lb * jnp.maximum(-diff, 0.0)), 2.0)))
    tab = dict(
        qdec_f=jnp.exp(lgf * (row + 1.0)), kdec_f=jnp.exp(lgf * (c - 1.0 - row)),
        qdec_b=jnp.exp(lgb * (c - row)), kdec_b=jnp.exp(lgb * row),
        cdec_f=jnp.exp(lgf * c), cdec_b=jnp.exp(lgb * c),
        d0=dmats[0], d1=dmats[1], row=row, diff=diff)
    r = lax.broadcasted_iota(jnp.int32, (LANES, LANES), 0) < HEAD_DIM
    cc = lax.broadcasted_iota(jnp.int32, (LANES, LANES), 1) < HEAD_DIM
    tab["bd"] = r == cc
    tab["m0"] = lax.broadcasted_iota(jnp.int32, (c, LANES), 1) < HEAD_DIM
    return tab


def _ret_specs(s):
    blk = lambda cb: pl.BlockSpec((None, s, LANES), lambda b, p, cb=cb: (b, 0, cb + p))
    lane = pl.BlockSpec((None, 1, LANES), lambda b, p: (p, 0, 0))
    gain = pl.BlockSpec((1, LANES), lambda b, p: (0, p))
    pair = pl.BlockSpec((None, s, LANES), lambda b, p: (b, 0, p))
    return blk, lane, gain, pair


def _ret_fwd(u3, lgf_l, lgb_l, gn_gain):
    bsz, s, _ = u3.shape
    n_chunk = s // CHUNK
    c = CHUNK

    def body(q_ref, k_ref, v_ref, g_ref, lgf_ref, lgb_ref, gain_ref, y_ref, o_ref, rb_ref):
        tb = _ret_tables(lgf_ref[...], lgb_ref[...])
        m0, bd = tb["m0"], tb["bd"]
        gain = gain_ref[...]

        def back_states(i, rb):
            n = n_chunk - 1 - i
            sl = pl.ds(pl.multiple_of(n * c, c), c)
            rb_ref[n] = rb
            kb = (k_ref[sl, :].astype(F32) * tb["kdec_b"]).astype(BF16)
            return rb * tb["cdec_b"] + jnp.where(bd, _dot_tn(kb, v_ref[sl, :]), 0.0)

        lax.fori_loop(0, n_chunk, back_states, jnp.zeros((LANES, LANES), F32))

        def chunk(n, rf):
            sl = pl.ds(pl.multiple_of(n * c, c), c)
            q = q_ref[sl, :].astype(F32) * 0.125
            k = k_ref[sl, :]
            v = v_ref[sl, :]
            kf32 = k.astype(F32)
            vf32 = v.astype(F32)
            q0 = jnp.where(m0, q, 0.0).astype(BF16)
            q1 = jnp.where(m0, 0.0, q).astype(BF16)
            a0 = (_dot_nt(q0, k) * tb["d0"]).astype(BF16)
            a1 = (_dot_nt(q1, k) * tb["d1"]).astype(BF16)
            v0 = jnp.where(m0, vf32, 0.0).astype(BF16)
            v1 = jnp.where(m0, 0.0, vf32).astype(BF16)
            y = _dot(a0, v0) + _dot(a1, v1)
            y = y + _dot((q * tb["qdec_f"]).astype(BF16), rf.astype(BF16))
            y = y + _dot((q * tb["qdec_b"]).astype(BF16), rb_ref[n].astype(BF16))
            rf_new = rf * tb["cdec_f"] + jnp.where(bd, _dot_tn((kf32 * tb["kdec_f"]).astype(BF16), v), 0.0)
            mu = _head_mean(y, m0)
            yc = y - mu
            var = _head_mean(yc * yc, m0)
            yh = yc * lax.rsqrt(var + GN_EPS)
            g = g_ref[sl, :].astype(F32)
            y_ref[sl, :] = y
            o_ref[sl, :] = (yh * gain * (g * _sigmoid(g))).astype(BF16)
            return rf_new

        lax.fori_loop(0, n_chunk, chunk, jnp.zeros((LANES, LANES), F32))

    blk, lane, gain, pair = _ret_specs(s)
    return pl.pallas_call(
        body, name="ret_fwd", grid=(bsz, 4),
        in_specs=[blk(CB_RQ), blk(CB_RK), blk(CB_RV), blk(CB_RG), lane, lane, gain],
        out_specs=[pair, pair],
        out_shape=[jax.ShapeDtypeStruct((bsz, s, RET_WIDTH), F32), jax.ShapeDtypeStruct((bsz, s, RET_WIDTH), BF16)],
        scratch_shapes=[pltpu.VMEM((n_chunk, LANES, LANES), F32)],
        compiler_params=_params("parallel", "parallel", vmem=VMEM_LIMIT),
    )(u3, u3, u3, u3, lgf_l, lgb_l, gn_gain)


def _ret_bwd(u3, y_pre, d_o, lgf_l, lgb_l, gn_gain):
    bsz, s, _ = u3.shape
    n_chunk = s // CHUNK
    c = CHUNK

    def body(q_ref, k_ref, v_ref, g_ref, y_ref, do_ref, lgf_ref, lgb_ref, gain_ref,
             dq_ref, dk_ref, dv_ref, dg_ref, part_ref,
             rb_ref, rf_ref, dy_ref, dk_acc, dv_acc, af0, af1, ab0, ab1, vec_ref):
        tb = _ret_tables(lgf_ref[...], lgb_ref[...])
        m0, bd, row = tb["m0"], tb["bd"], tb["row"]
        gain = gain_ref[...]
        wf = jnp.maximum(tb["diff"], 0.0)
        wb = jnp.maximum(-tb["diff"], 0.0)
        zero_state = jnp.zeros((LANES, LANES), F32)
        for ref in (af0, af1, ab0, ab1):
            ref[...] = jnp.zeros_like(ref)
        vec_ref[...] = jnp.zeros_like(vec_ref)

        def back_states(i, rb):
            n = n_chunk - 1 - i
            sl = pl.ds(pl.multiple_of(n * c, c), c)
            rb_ref[n] = rb
            kb = (k_ref[sl, :].astype(F32) * tb["kdec_b"]).astype(BF16)
            return rb * tb["cdec_b"] + jnp.where(bd, _dot_tn(kb, v_ref[sl, :]), 0.0)

        lax.fori_loop(0, n_chunk, back_states, zero_state)

        def sweep_fwd(n, carry):
            rf, gb = carry
            sl = pl.ds(pl.multiple_of(n * c, c), c)
            rf_ref[n] = rf
            rbn = rb_ref[n]
            q = q_ref[sl, :].astype(F32) * 0.125
            k = k_ref[sl, :]
            v = v_ref[sl, :]
            kf32 = k.astype(F32)
            vf32 = v.astype(F32)
            y = y_ref[sl, :]
            do = do_ref[sl, :].astype(F32)
            g = g_ref[sl, :].astype(F32)
            mu = _head_mean(y, m0)
            yc = y - mu
            rstd = lax.rsqrt(_head_mean(yc * yc, m0) + GN_EPS)
            yh = yc * rstd
            sg = _sigmoid(g)
            sil = g * sg
            dyh = do * gain * sil
            dg_ref[sl, :] = (do * yh * gain * sg * (1.0 + g * (1.0 - sg))).astype(BF16)
            dgain = jnp.sum(do * yh * sil, axis=0, keepdims=True)
            dy = rstd * (dyh - _head_mean(dyh, m0) - yh * _head_mean(dyh * yh, m0))
            dyb = dy.astype(BF16)
            dy_ref[sl, :] = dyb
            dy0 = jnp.where(m0, dy, 0.0).astype(BF16)
            dy1 = jnp.where(m0, 0.0, dy).astype(BF16)
            q0 = jnp.where(m0, q, 0.0).astype(BF16)
            q1 = jnp.where(m0, 0.0, q).astype(BF16)
            k0 = jnp.where(m0, kf32, 0.0).astype(BF16)
            k1 = jnp.where(m0, 0.0, kf32).astype(BF16)
            a0 = _dot_nt(q0, k) * tb["d0"]
            a1 = _dot_nt(q1, k) * tb["d1"]
            da0 = _dot_nt(dy0, v)
            da1 = _dot_nt(dy1, v)
            pr0 = da0 * a0
            pr1 = da1 * a1
            af0[...] += pr0 * wf
            ab0[...] += pr0 * wb
            af1[...] += pr1 * wf
            ab1[...] += pr1 * wb
            ds0 = (da0 * tb["d0"]).astype(BF16)
            ds1 = (da1 * tb["d1"]).astype(BF16)
            dq = _dot(ds0, k0) + _dot(ds1, k1)
            dk = _dot_tn(ds0, q0) + _dot_tn(ds1, q1)
            dv = _dot_tn(a0.astype(BF16), dy0) + _dot_tn(a1.astype(BF16), dy1)
            qf = (q * tb["qdec_f"]).astype(BF16)
            qb = (q * tb["qdec_b"]).astype(BF16)
            rfb = rf.astype(BF16)
            rbb = rbn.astype(BF16)
            ycf = _dot(qf, rfb)
            ycb = _dot(qb, rbb)
            dq = dq + _dot_nt(dyb, rfb) * tb["qdec_f"] + _dot_nt(dyb, rbb) * tb["qdec_b"]
            dlf = jnp.sum((row + 1.0) * ycf * dy, axis=0, keepdims=True)
            dlb = jnp.sum((c - row) * ycb * dy, axis=0, keepdims=True)
            gbb = gb.astype(BF16)
            kbd = kf32 * tb["kdec_b"]
            dkb = _dot_nt(v, gbb)
            dk = dk + dkb * tb["kdec_b"]
            dv = dv + _dot(kbd.astype(BF16), gbb)
            dlb = dlb + jnp.sum(row * kbd * dkb, axis=0, keepdims=True)
            dlb = dlb + c * tb["cdec_b"] * jnp.sum(gb * rbn, axis=0, keepdims=True)
            gb_new = jnp.where(bd, _dot_tn(qb, dyb), 0.0) + tb["cdec_b"] * gb
            rf_new = rf * tb["cdec_f"] + jnp.where(bd, _dot_tn((kf32 * tb["kdec_f"]).astype(BF16), v), 0.0)
            dq_ref[sl, :] = (dq * 0.125).astype(BF16)
            dk_acc[sl, :] = dk
            dv_acc[sl, :] = dv
            vec_ref[0:1, :] += dlf
            vec_ref[1:2, :] += dlb
            vec_ref[6:7, :] += dgain
            return rf_new, gb_new

        lax.fori_loop(0, n_chunk, sweep_fwd, (zero_state, zero_state))

        def sweep_bwd(i, gf):
            n = n_chunk - 1 - i
            sl = pl.ds(pl.multiple_of(n * c, c), c)
            q = q_ref[sl, :].astype(F32) * 0.125
            kf32 = k_ref[sl, :].astype(F32)
            v = v_ref[sl, :]
            dyb = dy_ref[sl, :]
            gfb = gf.astype(BF16)
            kfd = kf32 * tb["kdec_f"]
            dkf = _dot_nt(v, gfb)
            dk_ref[sl, :] = (dk_acc[sl, :] + dkf * tb["kdec_f"]).astype(BF16)
            dv_ref[sl, :] = (dv_acc[sl, :] + _dot(kfd.astype(BF16), gfb)).astype(BF16)
            dlf = jnp.sum((c - 1.0 - row) * kfd * dkf, axis=0, keepdims=True)
            dlf = dlf + c * tb["cdec_f"] * jnp.sum(gf * rf_ref[n], axis=0, keepdims=True)
            vec_ref[0:1, :] += dlf
            qf = (q * tb["qdec_f"]).astype(BF16)
            return jnp.where(bd, _dot_tn(qf, dyb), 0.0) + tb["cdec_f"] * gf

        lax.fori_loop(0, n_chunk, sweep_bwd, zero_state)
        vec_ref[2:3, :] = jnp.sum(af0[...], axis=0, keepdims=True)
        vec_ref[3:4, :] = jnp.sum(af1[...], axis=0, keepdims=True)
        vec_ref[4:5, :] = jnp.sum(ab0[...], axis=0, keepdims=True)
        vec_ref[5:6, :] = jnp.sum(ab1[...], axis=0, keepdims=True)
        part_ref[...] = vec_ref[...]

    blk, lane, gain, pair = _ret_specs(s)
    out_bf = jax.ShapeDtypeStruct((bsz, s, RET_WIDTH), BF16)
    return pl.pallas_call(
        body, name="ret_bwd", grid=(bsz, 4),
        in_specs=[blk(CB_RQ), blk(CB_RK), blk(CB_RV), blk(CB_RG), pair, pair, lane, lane, gain],
        out_specs=[pair, pair, pair, pair, pl.BlockSpec((None, None, 8, LANES), lambda b, p: (b, p, 0, 0))],
        out_shape=[out_bf, out_bf, out_bf, out_bf, jax.ShapeDtypeStruct((bsz, 4, 8, LANES), F32)],
        scratch_shapes=[pltpu.VMEM((n_chunk, LANES, LANES), F32), pltpu.VMEM((n_chunk, LANES, LANES), F32),
                        pltpu.VMEM((s, LANES), BF16), pltpu.VMEM((s, LANES), F32), pltpu.VMEM((s, LANES), F32),
                        pltpu.VMEM((c, c), F32), pltpu.VMEM((c, c), F32), pltpu.VMEM((c, c), F32),
                        pltpu.VMEM((c, c), F32), pltpu.VMEM((8, LANES), F32)],
        compiler_params=_params("parallel", "parallel", vmem=VMEM_LIMIT),
    )(u3, u3, u3, u3, y_pre, d_o, lgf_l, lgb_l, gn_gain)


def _attn_window_tables(n, s):
    qi = lax.broadcasted_iota(jnp.int32, (CHUNK, 3 * CHUNK), 0)
    kj = lax.broadcasted_iota(jnp.int32, (CHUNK, 3 * CHUNK), 1)
    dist = jnp.abs(kj - CHUNK - qi)
    kpos = n * CHUNK - CHUNK + kj
    valid = (dist <= CHUNK) & (kpos >= 0) & (kpos < s)
    return dist.astype(F32), valid


def _dup_kv_head(x, g):
    lane = lax.broadcasted_iota(jnp.int32, x.shape, 1)
    keep = (lane < HEAD_DIM) == (g == 0)
    xf = x.astype(F32)
    return jnp.where(keep, xf, pltpu.roll(xf, HEAD_DIM, 1))


def _attn_specs(s):
    q = pl.BlockSpec((None, s, 2 * LANES), lambda b, g: (b, 0, CB_AQ // 2 + g))
    k = pl.BlockSpec((None, s, LANES), lambda b, g: (b, 0, CB_AK))
    v = pl.BlockSpec((None, s, LANES), lambda b, g: (b, 0, CB_AV))
    grp = pl.BlockSpec((None, s, 2 * LANES), lambda b, g: (b, 0, g))
    smem = pl.BlockSpec(memory_space=pltpu.SMEM)
    return q, k, v, grp, smem


def _fill_padded(dst_ref, val, s):
    dst_ref[0:CHUNK, :] = jnp.zeros((CHUNK, LANES), dst_ref.dtype)
    dst_ref[CHUNK:CHUNK + s, :] = val.astype(dst_ref.dtype)
    dst_ref[CHUNK + s:2 * CHUNK + s, :] = jnp.zeros((CHUNK, LANES), dst_ref.dtype)


def _attn_probs(qh, kw, slope, snk, dist, valid):
    sc = _dot_nt(qh, kw)
    sc = jnp.where(valid, sc - slope * dist, NEG_INF)
    m = jnp.maximum(jnp.max(sc, axis=1, keepdims=True), snk)
    e = jnp.exp(sc - m)
    es = jnp.exp(snk - m)
    inv = 1.0 / (jnp.sum(e, axis=1, keepdims=True) + es)
    return e * inv, es * inv


def _attn_fwd(u3, slopes, sink):
    bsz, s, _ = u3.shape
    n_blk = s // CHUNK

    def body(slope_ref, sink_ref, q_ref, k_ref, v_ref, o_ref, kp_ref, vp_ref):
        g = pl.program_id(1)
        _fill_padded(kp_ref, _dup_kv_head(k_ref[...], g), s)
        _fill_padded(vp_ref, _dup_kv_head(v_ref[...], g), s)
        m0 = lax.broadcasted_iota(jnp.int32, (CHUNK, LANES), 1) < HEAD_DIM
        m0w = lax.broadcasted_iota(jnp.int32, (3 * CHUNK, LANES), 1) < HEAD_DIM

        def blk(n, carry):
            r0 = pl.multiple_of(n * CHUNK, CHUNK)
            kw = kp_ref[pl.ds(r0, 3 * CHUNK), :]
            vw = vp_ref[pl.ds(r0, 3 * CHUNK), :].astype(F32)
            vws = (jnp.where(m0w, vw, 0.0).astype(BF16), jnp.where(m0w, 0.0, vw).astype(BF16))
            dist, valid = _attn_window_tables(n, s)
            for pr in range(2):
                qp = q_ref[pl.ds(r0, CHUNK), pr * LANES:(pr + 1) * LANES].astype(F32) * 0.125
                out = jnp.zeros((CHUNK, LANES), F32)
                for hh in range(2):
                    h = g * 4 + pr * 2 + hh
                    qh = (jnp.where(m0, qp, 0.0) if hh == 0 else jnp.where(m0, 0.0, qp)).astype(BF16)
                    p, _ = _attn_probs(qh, kw, slope_ref[h], sink_ref[h], dist, valid)
                    out = out + _dot(p.astype(BF16), vws[hh])
                o_ref[pl.ds(r0, CHUNK), pr * LANES:(pr + 1) * LANES] = out.astype(BF16)
            return carry

        lax.fori_loop(0, n_blk, blk, 0)

    q, k, v, grp, smem = _attn_specs(s)
    return pl.pallas_call(
        body, name="attn_fwd", grid=(bsz, 2),
        in_specs=[smem, smem, q, k, v],
        out_specs=grp,
        out_shape=jax.ShapeDtypeStruct((bsz, s, ATTN_WIDTH), BF16),
        scratch_shapes=[pltpu.VMEM((s + 2 * CHUNK, LANES), BF16), pltpu.VMEM((s + 2 * CHUNK, LANES), BF16)],
        compiler_params=_params("parallel", "parallel", vmem=VMEM_LIMIT),
    )(slopes, sink, u3, u3, u3)


def _attn_bwd(u3, d_o, slopes, sink):
    bsz, s, _ = u3.shape
    n_blk = s // CHUNK

    def body(slope_ref, sink_ref, q_ref, k_ref, v_ref, do_ref, dq_ref, dkv_ref, ds_ref,
             kp_ref, vp_ref, dk_acc, dv_acc):
        g = pl.program_id(1)
        _fill_padded(kp_ref, _dup_kv_head(k_ref[...], g), s)
        _fill_padded(vp_ref, _dup_kv_head(v_ref[...], g), s)
        dk_acc[...] = jnp.zeros_like(dk_acc)
        dv_acc[...] = jnp.zeros_like(dv_acc)
        m0 = lax.broadcasted_iota(jnp.int32, (CHUNK, LANES), 1) < HEAD_DIM

        def blk(n, dsink):
            r0 = pl.multiple_of(n * CHUNK, CHUNK)
            win = pl.ds(r0, 3 * CHUNK)
            kw = kp_ref[win, :]
            vw = vp_ref[win, :]
            dist, valid = _attn_window_tables(n, s)
            dkw = jnp.zeros((3 * CHUNK, LANES), F32)
            dvw = jnp.zeros((3 * CHUNK, LANES), F32)
            new_dsink = []
            for pr in range(2):
                cols = slice(pr * LANES, (pr + 1) * LANES)
                qp = q_ref[pl.ds(r0, CHUNK), cols].astype(F32) * 0.125
                dop = do_ref[pl.ds(r0, CHUNK), cols].astype(F32)
                dq = jnp.zeros((CHUNK, LANES), F32)
                for hh in range(2):
                    h = g * 4 + pr * 2 + hh
                    sel = (lambda a: jnp.where(m0, a, 0.0)) if hh == 0 else (lambda a: jnp.where(m0, 0.0, a))
                    qh = sel(qp).astype(BF16)
                    doh = sel(dop).astype(BF16)
                    p, ps = _attn_probs(qh, kw, slope_ref[h], sink_ref[h], dist, valid)
                    dp = _dot_nt(doh, vw)
                    delta = jnp.sum(p * dp, axis=1, keepdims=True)
                    dsc = (p * (dp - delta)).astype(BF16)
                    dsh = jnp.sum(ps * delta, axis=0, keepdims=True)
                    new_dsink.append(dsink[pr * 2 + hh] - jnp.broadcast_to(dsh, (1, LANES)))
                    dq = dq + sel(_dot(dsc, kw))
                    dkw = dkw + _dot_tn(dsc, qh)
                    dvw = dvw + _dot_tn(p.astype(BF16), doh)
                dq_ref[pl.ds(r0, CHUNK), cols] = (dq * 0.125).astype(BF16)
            dk_acc[win, :] += dkw
            dv_acc[win, :] += dvw
            return tuple(new_dsink)

        dsink = lax.fori_loop(0, n_blk, blk, tuple(jnp.zeros((1, LANES), F32) for _ in range(4)))
        dk = dk_acc[CHUNK:CHUNK + s, :]
        dv = dv_acc[CHUNK:CHUNK + s, :]
        lane = lax.broadcasted_iota(jnp.int32, (s, LANES), 1)
        fold = lambda a: a + pltpu.roll(a, HEAD_DIM, 1)
        dkv_ref[...] = jnp.where(lane < HEAD_DIM, fold(dk), fold(dv)).astype(BF16)
        ds_ref[...] = jnp.zeros_like(ds_ref)
        for i in range(4):
            ds_ref[i:i + 1, :] = dsink[i]

    q, k, v, grp, smem = _attn_specs(s)
    return pl.pallas_call(
        body, name="attn_bwd", grid=(bsz, 2),
        in_specs=[smem, smem, q, k, v, grp],
        out_specs=[grp, pl.BlockSpec((None, s, LANES), lambda b, g: (b, 0, g)),
                   pl.BlockSpec((None, None, 8, LANES), lambda b, g: (b, g, 0, 0))],
        out_shape=[jax.ShapeDtypeStruct((bsz, s, ATTN_WIDTH), BF16), jax.ShapeDtypeStruct((bsz, s, 2 * LANES), BF16),
                   jax.ShapeDtypeStruct((bsz, 2, 8, LANES), F32)],
        scratch_shapes=[pltpu.VMEM((s + 2 * CHUNK, LANES), BF16), pltpu.VMEM((s + 2 * CHUNK, LANES), BF16),
                        pltpu.VMEM((s + 2 * CHUNK, LANES), F32), pltpu.VMEM((s + 2 * CHUNK, LANES), F32)],
        compiler_params=_params("parallel", "parallel", vmem=VMEM_LIMIT),
    )(slopes, sink, u3, u3, u3, d_o)


def _ffn_bwd(dz2, gs, us, pg, ple, zh1, r1, g1, wg4, wu4, wd4, wpg, w_out):
    t = dz2.shape[0]
    tm = 256

    def body(dz_ref, gs_ref, us_ref, pg_ref, ple_ref, zh_ref, r_ref, g1_ref,
             wg_hbm, wu_hbm, wd_hbm, wpg_hbm, wo_hbm,
             dgs_ref, dus_ref, dsp_ref, dple_ref, dz1_ref, dyr_ref, dya_ref, dg1_ref, db1_ref,
             wg, wu, wd, wpg, wo):
        step = pl.program_id(0)
        _load_resident(step, [(wg_hbm, wg), (wu_hbm, wu), (wd_hbm, wd), (wpg_hbm, wpg), (wo_hbm, wo)])

        @pl.when(step == 0)
        def _():
            dg1_ref[...] = jnp.zeros_like(dg1_ref)
            db1_ref[...] = jnp.zeros_like(db1_ref)

        dz = dz_ref[...]
        dzb = dz.astype(BF16)
        dh = ALPHA * dz
        for j in range(N_SHARD):
            da = _dot_nt(dzb, wd[j])
            gj = gs_ref[j].astype(F32)
            uj = us_ref[j].astype(F32)
            sg = _sigmoid(gj)
            dgj = (da * uj * sg * (1.0 + gj * (1.0 - sg))).astype(BF16)
            duj = (da * gj * sg).astype(BF16)
            dgs_ref[j] = dgj
            dus_ref[j] = duj
            dh = dh + _dot_nt(dgj, wg[j]) + _dot_nt(duj, wu[j])
        pgv = pg_ref[...].astype(F32)
        plev = ple_ref[...].astype(F32)
        dple_ref[...] = (dz * pgv).astype(BF16)
        dsp = (dz * plev * pgv * (1.0 - pgv)).astype(BF16)
        dsp_ref[...] = dsp
        dh = dh + _dot_nt(dsp, wpg[...])
        zh = zh_ref[...]
        dg1_ref[...] += jnp.sum(dh * zh, axis=0, keepdims=True)
        db1_ref[...] += jnp.sum(dh, axis=0, keepdims=True)
        dzh = dh * g1_ref[...]
        m1 = jnp.mean(dzh, axis=1, keepdims=True)
        m2 = jnp.mean(dzh * zh, axis=1, keepdims=True)
        dz1 = r_ref[...] * (dzh - m1 - zh * m2)
        dz1_ref[...] = dz1
        dyc = _dot_nt(dz1.astype(BF16), wo[...])
        dyr_ref[...] = dyc[:, 0:RET_WIDTH].astype(BF16)
        dya_ref[...] = dyc[:, RET_WIDTH:].astype(BF16)

    row = lambda w: pl.BlockSpec((tm, w), lambda i: (i, 0))
    const = lambda s: pl.BlockSpec(s, lambda i: (0, 0))
    sh = pl.BlockSpec((N_SHARD, tm, FFN_SHARD), lambda i: (0, i, 0))
    hbm = pl.BlockSpec(memory_space=pl.ANY)
    sh_shape = jax.ShapeDtypeStruct((N_SHARD, t, FFN_SHARD), BF16)
    return pl.pallas_call(
        body, name="ffn_bwd", grid=(t // tm,),
        in_specs=[row(D_MODEL), sh, sh, row(D_MODEL), row(D_MODEL), row(D_MODEL), row(1), const((1, D_MODEL)),
                  hbm, hbm, hbm, hbm, hbm],
        out_specs=[sh, sh, row(D_MODEL), row(D_MODEL), row(D_MODEL), row(RET_WIDTH), row(ATTN_WIDTH),
                   const((1, D_MODEL)), const((1, D_MODEL))],
        out_shape=[sh_shape, sh_shape, jax.ShapeDtypeStruct((t, D_MODEL), BF16),
                   jax.ShapeDtypeStruct((t, D_MODEL), BF16), jax.ShapeDtypeStruct((t, D_MODEL), F32),
                   jax.ShapeDtypeStruct((t, RET_WIDTH), BF16), jax.ShapeDtypeStruct((t, ATTN_WIDTH), BF16),
                   jax.ShapeDtypeStruct((1, D_MODEL), F32), jax.ShapeDtypeStruct((1, D_MODEL), F32)],
        scratch_shapes=[pltpu.VMEM(wg4.shape, BF16), pltpu.VMEM(wu4.shape, BF16), pltpu.VMEM(wd4.shape, BF16),
                        pltpu.VMEM(wpg.shape, BF16), pltpu.VMEM(w_out.shape, BF16)],
        compiler_params=_params("arbitrary", vmem=VMEM_LIMIT),
    )(dz2, gs, us, pg, ple, zh1, r1, g1, wg4, wu4, wd4, wpg, w_out)


def _wgrad(a, b, name):
    t, m = a.shape
    n = b.shape[1]
    tm = min(m, 512)
    tk = min(t, 512)

    def body(a_ref, b_ref, o_ref):
        @pl.when(pl.program_id(1) == 0)
        def _():
            o_ref[...] = jnp.zeros_like(o_ref)

        o_ref[...] += _dot_tn(a_ref[...].astype(BF16), b_ref[...].astype(BF16))

    return pl.pallas_call(
        body, name=name, grid=(m // tm, t // tk),
        in_specs=[pl.BlockSpec((tk, tm), lambda i, k: (k, i)), pl.BlockSpec((tk, n), lambda i, k: (k, 0))],
        out_specs=pl.BlockSpec((tm, n), lambda i, k: (i, 0)),
        out_shape=jax.ShapeDtypeStruct((m, n), F32),
        compiler_params=_params("parallel", "arbitrary", vmem=VMEM_LIMIT),
    )(a, b)


def _wgrad_gate_up(hb, dgs, dus):
    t = hb.shape[0]
    tm = 512
    tk = min(t, 512)

    def body(a_ref, bg_ref, bu_ref, og_ref, ou_ref):
        @pl.when(pl.program_id(2) == 0)
        def _():
            og_ref[...] = jnp.zeros_like(og_ref)
            ou_ref[...] = jnp.zeros_like(ou_ref)

        a = a_ref[...]
        og_ref[...] += _dot_tn(a, bg_ref[...])
        ou_ref[...] += _dot_tn(a, bu_ref[...])

    b_spec = pl.BlockSpec((None, tk, FFN_SHARD), lambda j, i, k: (j, k, 0))
    o_spec = pl.BlockSpec((None, tm, FFN_SHARD), lambda j, i, k: (j, i, 0))
    o_shape = jax.ShapeDtypeStruct((N_SHARD, D_MODEL, FFN_SHARD), F32)
    return pl.pallas_call(
        body, name="wgrad_gate_up", grid=(N_SHARD, D_MODEL // tm, t // tk),
        in_specs=[pl.BlockSpec((tk, tm), lambda j, i, k: (k, i)), b_spec, b_spec],
        out_specs=[o_spec, o_spec], out_shape=[o_shape, o_shape],
        compiler_params=_params("parallel", "parallel", "arbitrary", vmem=VMEM_LIMIT),
    )(hb, dgs, dus)


def _wgrad_down(gs, us, dz2):
    t = dz2.shape[0]
    tk = min(t, 512)

    def body(g_ref, u_ref, b_ref, o_ref):
        @pl.when(pl.program_id(1) == 0)
        def _():
            o_ref[...] = jnp.zeros_like(o_ref)

        gj = g_ref[...].astype(F32)
        act = (gj * _sigmoid(gj) * u_ref[...].astype(F32)).astype(BF16)
        o_ref[...] += _dot_tn(act, b_ref[...].astype(BF16))

    a_spec = pl.BlockSpec((None, tk, FFN_SHARD), lambda j, k: (j, k, 0))
    return pl.pallas_call(
        body, name="wgrad_down", grid=(N_SHARD, t // tk),
        in_specs=[a_spec, a_spec, pl.BlockSpec((tk, D_MODEL), lambda j, k: (k, 0))],
        out_specs=pl.BlockSpec((None, FFN_SHARD, D_MODEL), lambda j, k: (j, 0, 0)),
        out_shape=jax.ShapeDtypeStruct((N_SHARD, FFN_SHARD, D_MODEL), F32),
        compiler_params=_params("parallel", "arbitrary", vmem=VMEM_LIMIT),
    )(gs, us, dz2)


def _inproj_bwd(dz1, pieces, w_main, w_kv):
    t = dz1.shape[0]
    tm = 512

    def body(dz_ref, p0, p1, p2, p3, p4, pkv, wm_ref, wkv_ref, o_ref):
        acc = ALPHA * dz_ref[...]
        for i, ref in enumerate((p0, p1, p2, p3, p4)):
            acc = acc + _dot_nt(ref[...], wm_ref[:, i * 512:(i + 1) * 512])
        o_ref[...] = acc + _dot_nt(pkv[...], wkv_ref[...])

    row = lambda w: pl.BlockSpec((tm, w), lambda i: (i, 0))
    const = lambda s: pl.BlockSpec(s, lambda i: (0, 0))
    return pl.pallas_call(
        body, name="inproj_bwd", grid=(t // tm,),
        in_specs=[row(D_MODEL)] + [row(512)] * 5 + [row(256), const(w_main.shape), const(w_kv.shape)],
        out_specs=row(D_MODEL),
        out_shape=jax.ShapeDtypeStruct((t, D_MODEL), F32),
        compiler_params=_params("parallel", vmem=VMEM_LIMIT),
    )(dz1, *pieces, w_main, w_kv)


def _coords():
    return lax.axis_index("x"), lax.axis_index("y"), lax.axis_index("c")


def _chip_of(x, y, rel):
    return (1 - x if rel & 2 else x), (1 - y if rel & 1 else y)


def _all_gather_weights(shards):
    nw = len(shards)

    def body(*refs):
        ins, outs = refs[:nw], refs[nw:2 * nw]
        send, recv, fsend, frecv, lsem = refs[2 * nw:]
        x, y, c = _coords()
        me = 2 * x + y

        def half(w, chip, cc):
            h = shards[w].shape[0] // 2
            return outs[w].at[chip, pl.ds(cc * h, h), :]

        def chip_copy(w, rel):
            h = shards[w].shape[0] // 2
            kx, ky = _chip_of(x, y, rel)
            return pltpu.make_async_remote_copy(
                src_ref=ins[w].at[pl.ds(c * h, h), :], dst_ref=half(w, me, c),
                send_sem=send.at[w * 3 + rel - 1], recv_sem=recv.at[w * 3 + rel - 1],
                device_id=(kx, ky, c), device_id_type=MESH)

        def arrived(w, rel):
            kx, ky = _chip_of(x, y, rel)
            return pltpu.make_async_remote_copy(
                src_ref=half(w, 2 * kx + ky, c), dst_ref=half(w, 2 * kx + ky, c),
                send_sem=send.at[w * 3 + rel - 1], recv_sem=recv.at[w * 3 + rel - 1],
                device_id=(kx, ky, c), device_id_type=MESH)

        def pass_on(w, rel, cc):
            kx, ky = _chip_of(x, y, rel)
            return pltpu.make_async_remote_copy(
                src_ref=half(w, 2 * kx + ky, cc), dst_ref=half(w, 2 * kx + ky, cc),
                send_sem=fsend.at[w * 3 + rel - 1], recv_sem=frecv.at[w * 3 + rel - 1],
                device_id=(x, y, 1 - c), device_id_type=MESH)

        own = [pltpu.make_async_copy(ins[w], outs[w].at[me], lsem.at[w]) for w in range(nw)]
        for cp in own:
            cp.start()
        for rel in (1, 2, 3):
            for w in range(nw):
                chip_copy(w, rel).start()
        for rel in (1, 2, 3):
            for w in range(nw):
                arrived(w, rel).wait_recv()
                pass_on(w, rel, c).start()
        for rel in (1, 2, 3):
            for w in range(nw):
                pass_on(w, rel, 1 - c).wait_recv()
        for rel in (1, 2, 3):
            for w in range(nw):
                chip_copy(w, rel).wait_send()
                pass_on(w, rel, c).wait_send()
        for cp in own:
            cp.wait()

    hbm = pl.BlockSpec(memory_space=pl.ANY)
    return pl.pallas_call(
        body, name="gather_weights",
        in_specs=[hbm] * nw, out_specs=[hbm] * nw,
        out_shape=[jax.ShapeDtypeStruct((N_SHARD,) + s.shape, s.dtype) for s in shards],
        scratch_shapes=[pltpu.SemaphoreType.DMA((3 * nw,)), pltpu.SemaphoreType.DMA((3 * nw,)),
                        pltpu.SemaphoreType.DMA((3 * nw,)), pltpu.SemaphoreType.DMA((3 * nw,)),
                        pltpu.SemaphoreType.DMA((nw,))],
    )(*shards)


def _exchange_halves(parts):
    nw = len(parts)

    def body(*refs):
        ins, outs = refs[:nw], refs[nw:2 * nw]
        send, recv = refs[2 * nw:]
        x, y, c = _coords()
        copies = []
        for w in range(nw):
            h = parts[w].shape[1] // 2
            copies.append(pltpu.make_async_remote_copy(
                src_ref=ins[w].at[:, pl.ds((1 - c) * h, h), :], dst_ref=outs[w],
                send_sem=send.at[w], recv_sem=recv.at[w], device_id=(x, y, 1 - c), device_id_type=MESH))
        for cp in copies:
            cp.start()
        for cp in copies:
            cp.wait()

    hbm = pl.BlockSpec(memory_space=pl.ANY)
    return pl.pallas_call(
        body, name="exchange_halves",
        in_specs=[hbm] * nw, out_specs=[hbm] * nw,
        out_shape=[jax.ShapeDtypeStruct((N_SHARD, p.shape[1] // 2, p.shape[2]), F32) for p in parts],
        scratch_shapes=[pltpu.SemaphoreType.DMA((nw,)), pltpu.SemaphoreType.DMA((nw,))],
    )(*parts)


def _add_halves(parts, theirs, c_arr):
    nw = len(parts)
    split = 2

    def body(c_ref, *refs):
        ins, oth = refs[:nw], refs[nw:2 * nw]
        o32, o16 = refs[2 * nw:3 * nw], refs[3 * nw:]
        for w in range(nw):
            sm = ins[w][...] + oth[w][...]
            o32[w][...] = sm
            o16[w][...] = sm.astype(BF16)

    in_specs, oth_specs, out_specs, shapes32, shapes16 = [], [], [], [], []
    for p in parts:
        hb = p.shape[1] // 2 // split
        in_specs.append(pl.BlockSpec((None, hb, p.shape[2]), lambda j, i, c_ref: (j, c_ref[0] * split + i, 0)))
        oth_specs.append(pl.BlockSpec((None, hb, p.shape[2]), lambda j, i, c_ref: (j, i, 0)))
        shapes32.append(jax.ShapeDtypeStruct((N_SHARD, p.shape[1] // 2, p.shape[2]), F32))
        shapes16.append(jax.ShapeDtypeStruct((N_SHARD, p.shape[1] // 2, p.shape[2]), BF16))
    return pl.pallas_call(
        body, name="add_halves",
        grid_spec=pltpu.PrefetchScalarGridSpec(
            num_scalar_prefetch=1, grid=(N_SHARD, split),
            in_specs=in_specs + oth_specs, out_specs=oth_specs + oth_specs),
        out_shape=shapes32 + shapes16,
        compiler_params=_params("parallel", "parallel", vmem=VMEM_LIMIT),
    )(c_arr, *parts, *theirs)


def _exchange_chips(sums16):
    nw = len(sums16)

    def body(*refs):
        ins, outs = refs[:nw], refs[nw:2 * nw]
        send, recv = refs[2 * nw:]
        x, y, c = _coords()
        copies = []
        for rel in (1, 2, 3):
            kx, ky = _chip_of(x, y, rel)
            for w in range(nw):
                copies.append(pltpu.make_async_remote_copy(
                    src_ref=ins[w].at[2 * kx + ky], dst_ref=outs[w].at[rel - 1],
                    send_sem=send.at[w * 3 + rel - 1], recv_sem=recv.at[w * 3 + rel - 1],
                    device_id=(kx, ky, c), device_id_type=MESH))
        for cp in copies:
            cp.start()
        for cp in copies:
            cp.wait()

    hbm = pl.BlockSpec(memory_space=pl.ANY)
    return pl.pallas_call(
        body, name="exchange_chips",
        in_specs=[hbm] * nw, out_specs=[hbm] * nw,
        out_shape=[jax.ShapeDtypeStruct((3,) + s.shape[1:], BF16) for s in sums16],
        scratch_shapes=[pltpu.SemaphoreType.DMA((3 * nw,)), pltpu.SemaphoreType.DMA((3 * nw,))],
    )(*sums16)


def _add_chips(sums32, theirs, me_arr):
    nw = len(sums32)
    split = 2

    def body(me_ref, *refs):
        ins, oth, outs = refs[:nw], refs[nw:2 * nw], refs[2 * nw:]
        for w in range(nw):
            acc = ins[w][...]
            for r in range(3):
                acc = acc + oth[w][r].astype(F32)
            outs[w][...] = acc

    in_specs, oth_specs, out_specs, shapes = [], [], [], []
    for s in sums32:
        hb = s.shape[1] // split
        in_specs.append(pl.BlockSpec((None, hb, s.shape[2]), lambda i, me_ref: (me_ref[0], i, 0)))
        oth_specs.append(pl.BlockSpec((3, hb, s.shape[2]), lambda i, me_ref: (0, i, 0)))
        out_specs.append(pl.BlockSpec((hb, s.shape[2]), lambda i, me_ref: (i, 0)))
        shapes.append(jax.ShapeDtypeStruct(s.shape[1:], F32))
    return pl.pallas_call(
        body, name="add_chips",
        grid_spec=pltpu.PrefetchScalarGridSpec(
            num_scalar_prefetch=1, grid=(split,), in_specs=in_specs + oth_specs, out_specs=out_specs),
        out_shape=shapes,
        compiler_params=_params("parallel", vmem=VMEM_LIMIT),
    )(me_arr, *sums32, *theirs)


def _join_halves(halves):
    nw = len(halves)

    def body(*refs):
        ins, outs = refs[:nw], refs[nw:2 * nw]
        send, recv, lsem = refs[2 * nw:]
        x, y, c = _coords()
        local, remote = [], []
        for w in range(nw):
            h = halves[w].shape[0]
            local.append(pltpu.make_async_copy(ins[w], outs[w].at[pl.ds(c * h, h), :], lsem.at[w]))
            remote.append(pltpu.make_async_remote_copy(
                src_ref=ins[w], dst_ref=outs[w].at[pl.ds(c * h, h), :],
                send_sem=send.at[w], recv_sem=recv.at[w], device_id=(x, y, 1 - c), device_id_type=MESH))
        for cp in local + remote:
            cp.start()
        for w in range(nw):
            h = halves[w].shape[0]
            pltpu.make_async_remote_copy(
                src_ref=ins[w], dst_ref=outs[w].at[pl.ds((1 - c) * h, h), :],
                send_sem=send.at[w], recv_sem=recv.at[w], device_id=(x, y, 1 - c), device_id_type=MESH).wait_recv()
            remote[w].wait_send()
            local[w].wait()

    hbm = pl.BlockSpec(memory_space=pl.ANY)
    return pl.pallas_call(
        body, name="join_halves",
        in_specs=[hbm] * nw, out_specs=[hbm] * nw,
        out_shape=[jax.ShapeDtypeStruct((2 * h.shape[0], h.shape[1]), F32) for h in halves],
        scratch_shapes=[pltpu.SemaphoreType.DMA((nw,)), pltpu.SemaphoreType.DMA((nw,)),
                        pltpu.SemaphoreType.DMA((nw,))],
    )(*halves)


def _adamw_math(w, g, m, v):
    m = ADAM_B1 * m + (1.0 - ADAM_B1) * g
    v = ADAM_B2 * v + (1.0 - ADAM_B2) * (g * g)
    m_hat = m / (1.0 - ADAM_B1 ** ADAM_STEP)
    v_hat = v / (1.0 - ADAM_B2 ** ADAM_STEP)
    delta = -ADAM_LR * (m_hat / (jnp.sqrt(v_hat) + ADAM_EPS) + ADAM_WD * w)
    return delta, m, v


def _adamw(ws, gs, ms, vs):
    nw = len(ws)
    split = 8

    def body(*refs):
        w_r, g_r, m_r, v_r = (refs[i * nw:(i + 1) * nw] for i in range(4))
        d_o, m_o, v_o = (refs[(4 + i) * nw:(5 + i) * nw] for i in range(3))
        for k in range(nw):
            d, m, v = _adamw_math(w_r[k][...], g_r[k][...], m_r[k][...], v_r[k][...])
            d_o[k][...] = d
            m_o[k][...] = m
            v_o[k][...] = v

    specs = [pl.BlockSpec((w.shape[0] // split, w.shape[1]), lambda i: (i, 0)) for w in ws]
    shapes = [jax.ShapeDtypeStruct(w.shape, F32) for w in ws]
    outs = pl.pallas_call(
        body, name="adamw", grid=(split,),
        in_specs=specs * 4, out_specs=specs * 3, out_shape=shapes * 3,
        compiler_params=_params("parallel", vmem=VMEM_LIMIT),
    )(*ws, *gs, *ms, *vs)
    return outs[:nw], outs[nw:2 * nw], outs[2 * nw:]


SMALL_ROWS = 40


def _small_allreduce_adamw(part, w, m, v):
    def body(part_ref, w_ref, m_ref, v_ref, g_out, d_out, m_out, v_out, all_ref, send, recv):
        x, y, c = _coords()
        me = 4 * x + 2 * y + c
        all_ref[me] = part_ref[...]
        copies = []
        for rel in range(1, 8):
            px = 1 - x if rel & 4 else x
            py = 1 - y if rel & 2 else y
            pc = 1 - c if rel & 1 else c
            copies.append(pltpu.make_async_remote_copy(
                src_ref=part_ref, dst_ref=all_ref.at[me],
                send_sem=send.at[rel - 1], recv_sem=recv.at[rel - 1], device_id=(px, py, pc), device_id_type=MESH))
        for cp in copies:
            cp.start()
        for cp in copies:
            cp.wait()
        g = all_ref[0]
        for k in range(1, 8):
            g = g + all_ref[k]
        d, mn, vn = _adamw_math(w_ref[...], g, m_ref[...], v_ref[...])
        g_out[...] = g
        d_out[...] = d
        m_out[...] = mn
        v_out[...] = vn

    vm = pl.BlockSpec(memory_space=pltpu.VMEM)
    shape = jax.ShapeDtypeStruct((SMALL_ROWS, LANES), F32)
    return pl.pallas_call(
        body, name="small_allreduce_adamw",
        in_specs=[vm] * 4, out_specs=[vm] * 4, out_shape=[shape] * 4,
        scratch_shapes=[pltpu.VMEM((8, SMALL_ROWS, LANES), F32),
                        pltpu.SemaphoreType.DMA((7,)), pltpu.SemaphoreType.DMA((7,))],
    )(part, w, m, v)


SMALL_NAMES = ("ret_decay_fwd", "ret_decay_bwd", "attn_sink", "ret_gn_gain",
               "ln1_gain", "ln1_bias", "ln2_gain", "ln2_bias")


def _pack_small(vals):
    head = jnp.concatenate([vals["ret_decay_fwd"].reshape(8), vals["ret_decay_bwd"].reshape(8),
                            vals["attn_sink"].reshape(8), jnp.zeros((LANES - 24,), F32)]).reshape(1, LANES)
    rows = [head, vals["ret_gn_gain"].reshape(4, LANES)]
    rows += [vals[n].reshape(8, LANES) for n in ("ln1_gain", "ln1_bias", "ln2_gain", "ln2_bias")]
    rows.append(jnp.zeros((SMALL_ROWS - 37, LANES), F32))
    return jnp.concatenate(rows, axis=0)


def _unpack_small(packed):
    out = {"ret_decay_fwd": packed[0, 0:8].reshape(1, 8), "ret_decay_bwd": packed[0, 8:16].reshape(1, 8),
           "attn_sink": packed[0, 16:24].reshape(1, 8), "ret_gn_gain": packed[1:5].reshape(1, RET_WIDTH)}
    for i, n in enumerate(("ln1_gain", "ln1_bias", "ln2_gain", "ln2_bias")):
        out[n] = packed[5 + 8 * i:13 + 8 * i].reshape(1, D_MODEL)
    return out


def _local_step(x, p, tgt, wts, small):
    bsz, s, _ = x.shape
    t = bsz * s
    x2d = x.reshape(t, D_MODEL)
    p2d = p.reshape(t, PLE_DIM)
    tgt2d = tgt.reshape(t, D_MODEL)
    dec_f = small["ret_decay_fwd"].reshape(8)
    dec_b = small["ret_decay_bwd"].reshape(8)
    lg_f = jnp.log1p(-jnp.exp2(dec_f))
    lg_b = jnp.log1p(-jnp.exp2(dec_b))
    per_lane = lambda v: jnp.repeat(v, HEAD_DIM).reshape(4, 1, LANES)
    lgf_l, lgb_l = per_lane(lg_f), per_lane(lg_b)
    sink = small["attn_sink"].reshape(8)
    slopes = 2.0 ** (-(jnp.arange(8, dtype=F32) + 1.0))
    gn_gain = small["ret_gn_gain"]
    g1, b1, g2, b2 = (small[n] for n in ("ln1_gain", "ln1_bias", "ln2_gain", "ln2_bias"))

    u = _inproj(x2d, wts["w_in"])
    u3 = u.reshape(bsz, s, IN_WIDTH)
    y_pre, y_ret = _ret_fwd(u3, lgf_l, lgb_l, gn_gain)
    y_att = _attn_fwd(u3, slopes, sink)
    zh1, r1, hb = _outproj_ln1(y_ret.reshape(t, RET_WIDTH), y_att.reshape(t, ATTN_WIDTH), x2d, wts["w_out"], g1, b1)
    dz2, gs, us, pg, ple, sq, dg2, db2 = _ffn_fwd(zh1, hb, p2d, tgt2d, g1, b1, g2, b2, wts["gate4"], wts["up4"],
                                                 wts["down4"], wts["ple_proj"], wts["ple_gate"])
    dgs, dus, dsp, dple, dz1, dyr, dya, dg1, db1 = _ffn_bwd(dz2, gs, us, pg, ple, zh1, r1, g1, wts["gate4"],
                                                          wts["up4"], wts["down4"], wts["ple_gate"], wts["w_out"])
    d_gate4, d_up4 = _wgrad_gate_up(hb, dgs, dus)
    d_down4 = _wgrad_down(gs, us, dz2)
    d_ple_gate = _wgrad(hb, dsp, "wgrad_ple_gate")
    d_ple_proj = _wgrad(p2d, dple, "wgrad_ple_proj")
    d_w_out = jnp.concatenate([_wgrad(y_ret.reshape(t, RET_WIDTH), dz1, "wgrad_out_ret"),
                               _wgrad(y_att.reshape(t, ATTN_WIDTH), dz1, "wgrad_out_att")], axis=0)
    drq, drk, drv, drg, rpart = _ret_bwd(u3, y_pre, dyr.reshape(bsz, s, RET_WIDTH), lgf_l, lgb_l, gn_gain)
    daq, dakv, spart = _attn_bwd(u3, dya.reshape(bsz, s, ATTN_WIDTH), slopes, sink)
    pieces = [a.reshape(t, -1) for a in (drq, drk, drv, drg, daq, dakv)]
    w_in = wts["w_in"]
    kv0 = CB_AK * LANES
    kv_cols = [w_in[:, kv0 + o:kv0 + o + HEAD_DIM] for o in (0, 128, 64, 192)]
    w_kv = jnp.concatenate(kv_cols, axis=1)
    grad_x = _inproj_bwd(dz1, pieces, w_in[:, :kv0], w_kv).reshape(bsz, s, D_MODEL)
    d_in_main = [_wgrad(x2d, pc, "wgrad_in_%d" % i) for i, pc in enumerate(pieces[:5])]
    d_kv = _wgrad(x2d, pieces[5], "wgrad_in_kv")
    d_kv = jnp.concatenate([d_kv[:, o:o + HEAD_DIM] for o in (0, 128, 64, 192)], axis=1)
    d_w_in = jnp.concatenate(d_in_main + [d_kv], axis=1)

    rsum = jnp.sum(rpart, axis=0)
    lane_heads = lambda row: jnp.sum(row.reshape(4, 2, HEAD_DIM), axis=-1).reshape(8)
    dlg_f = lane_heads(rsum[:, 0, :]) + jnp.stack([jnp.sum(rsum[:, 2, :], -1), jnp.sum(rsum[:, 3, :], -1)], 1).reshape(8)
    dlg_b = lane_heads(rsum[:, 1, :]) + jnp.stack([jnp.sum(rsum[:, 4, :], -1), jnp.sum(rsum[:, 5, :], -1)], 1).reshape(8)
    chain = lambda d: -(math.log(2.0) * jnp.exp2(d)) / (1.0 - jnp.exp2(d))
    grads_small = {
        "ret_decay_fwd": (dlg_f * chain(dec_f)).reshape(1, 8),
        "ret_decay_bwd": (dlg_b * chain(dec_b)).reshape(1, 8),
        "attn_sink": jnp.sum(spart, axis=0)[:, 0:4, 0].reshape(1, 8),
        "ret_gn_gain": rsum[:, 6, :].reshape(1, RET_WIDTH),
        "ln1_gain": dg1, "ln1_bias": db1, "ln2_gain": dg2, "ln2_bias": db2,
    }
    grads_big = {
        "w_in": d_w_in.reshape(D_MODEL, N_SHARD, FFN_SHARD).transpose(1, 0, 2),
        "w_out": d_w_out.reshape(N_SHARD, D_MODEL // N_SHARD, D_MODEL),
        "w_ffn_gate": d_gate4, "w_ffn_up": d_up4, "w_ffn_down": d_down4,
        "w_ple_proj": d_ple_proj.reshape(PLE_DIM, N_SHARD, D_MODEL // N_SHARD).transpose(1, 0, 2),
        "w_ple_gate": d_ple_gate.reshape(N_SHARD, D_MODEL // N_SHARD, D_MODEL),
    }
    return sq[0, 0], grad_x, grads_big, grads_small


BIG_NAMES = ("w_in", "w_out", "w_ffn_gate", "w_ffn_up", "w_ffn_down", "w_ple_proj", "w_ple_gate")
WEIGHT_ORDER = ("w_in", "ret_decay_fwd", "ret_decay_bwd", "ret_gn_gain", "attn_sink", "w_out", "ln1_gain",
                "ln1_bias", "w_ffn_gate", "w_ffn_up", "w_ffn_down", "w_ple_proj", "w_ple_gate", "ln2_gain", "ln2_bias")


def _assemble_weights(gathered):
    cols = lambda a: a.transpose(1, 0, 2).reshape(a.shape[1], N_SHARD * a.shape[2])
    rows = lambda a: a.reshape(N_SHARD * a.shape[1], a.shape[2])
    return {"w_in": cols(gathered["w_in"]), "w_out": rows(gathered["w_out"]),
            "gate4": gathered["w_ffn_gate"], "up4": gathered["w_ffn_up"], "down4": gathered["w_ffn_down"],
            "ple_proj": cols(gathered["w_ple_proj"]), "ple_gate": rows(gathered["w_ple_gate"])}


def kernel(x, p, w_in, ret_decay_fwd, ret_decay_bwd, ret_gn_gain, attn_sink, w_out, ln1_gain, ln1_bias, w_ffn_gate, w_ffn_up, w_ffn_down, w_ple_proj, w_ple_gate, ln2_gain, ln2_bias, loss_target, m_w_in, m_ret_decay_fwd, m_ret_decay_bwd, m_ret_gn_gain, m_attn_sink, m_w_out, m_ln1_gain, m_ln1_bias, m_w_ffn_gate, m_w_ffn_up, m_w_ffn_down, m_w_ple_proj, m_w_ple_gate, m_ln2_gain, m_ln2_bias, v_w_in, v_ret_decay_fwd, v_ret_decay_bwd, v_ret_gn_gain, v_attn_sink, v_w_out, v_ln1_gain, v_ln1_bias, v_w_ffn_gate, v_w_ffn_up, v_w_ffn_down, v_w_ple_proj, v_w_ple_gate, v_ln2_gain, v_ln2_bias):
    w = dict(w_in=w_in, ret_decay_fwd=ret_decay_fwd, ret_decay_bwd=ret_decay_bwd, ret_gn_gain=ret_gn_gain,
             attn_sink=attn_sink, w_out=w_out, ln1_gain=ln1_gain, ln1_bias=ln1_bias, w_ffn_gate=w_ffn_gate,
             w_ffn_up=w_ffn_up, w_ffn_down=w_ffn_down, w_ple_proj=w_ple_proj, w_ple_gate=w_ple_gate,
             ln2_gain=ln2_gain, ln2_bias=ln2_bias)
    m = dict(w_in=m_w_in, ret_decay_fwd=m_ret_decay_fwd, ret_decay_bwd=m_ret_decay_bwd, ret_gn_gain=m_ret_gn_gain,
             attn_sink=m_attn_sink, w_out=m_w_out, ln1_gain=m_ln1_gain, ln1_bias=m_ln1_bias, w_ffn_gate=m_w_ffn_gate,
             w_ffn_up=m_w_ffn_up, w_ffn_down=m_w_ffn_down, w_ple_proj=m_w_ple_proj, w_ple_gate=m_w_ple_gate,
             ln2_gain=m_ln2_gain, ln2_bias=m_ln2_bias)
    v = dict(w_in=v_w_in, ret_decay_fwd=v_ret_decay_fwd, ret_decay_bwd=v_ret_decay_bwd, ret_gn_gain=v_ret_gn_gain,
             attn_sink=v_attn_sink, w_out=v_w_out, ln1_gain=v_ln1_gain, ln1_bias=v_ln1_bias, w_ffn_gate=v_w_ffn_gate,
             w_ffn_up=v_w_ffn_up, w_ffn_down=v_w_ffn_down, w_ple_proj=v_w_ple_proj, w_ple_gate=v_w_ple_gate,
             ln2_gain=v_ln2_gain, ln2_bias=v_ln2_bias)
    big = lambda d: [d[n][0] for n in BIG_NAMES]
    small = lambda d: {n: d[n] for n in SMALL_NAMES}

    gathered = _all_gather_weights([a.astype(BF16) for a in big(w)])
    wts = _assemble_weights(dict(zip(BIG_NAMES, gathered)))
    sq, grad_x, grads_big, grads_small = _local_step(x, p[0], loss_target, wts, small(w))

    c_arr = lax.axis_index("c").astype(jnp.int32).reshape(1)
    me_arr = (2 * lax.axis_index("x") + lax.axis_index("y")).astype(jnp.int32).reshape(1)
    parts = [grads_big[n] for n in BIG_NAMES]
    theirs = _exchange_halves(parts)
    sums = _add_halves(parts, theirs, c_arr)
    sums32, sums16 = sums[:len(parts)], sums[len(parts):]
    from_chips = _exchange_chips(sums16)
    halves = _add_chips(sums32, from_chips, me_arr)
    g_big = _join_halves(halves)
    d_big, m_big, v_big = _adamw(big(w), g_big, big(m), big(v))

    g_s, d_s, m_s, v_s = _small_allreduce_adamw(_pack_small(grads_small), _pack_small(small(w)),
                                                _pack_small(small(m)), _pack_small(small(v)))
    loss = lax.psum(sq, ("x", "y", "c")) * (0.5 / D_MODEL)

    def tree(bigs, packed):
        out = dict(zip(BIG_NAMES, [a[None] for a in bigs]))
        out.update(_unpack_small(packed))
        return [out[n] for n in WEIGHT_ORDER]

    return (loss, grad_x, *tree(g_big, g_s), *tree(d_big, d_s), *tree(m_big, m_s), *tree(v_big, v_s))
```

```python
import functools
import math

import jax
import jax.numpy as jnp
from jax import lax
from jax.experimental import pallas as pl
from jax.experimental.pallas import tpu as pltpu

F32 = jnp.float32
BF16 = jnp.bfloat16

D_MODEL = 1024
HEAD_DIM = 64
RET_HEADS = 8
ATTN_HEADS = 8
RET_WIDTH = 512
ATTN_WIDTH = 512
KV_WIDTH = 128
IN_WIDTH = 2816
FFN = 2816
N_SHARD = 4
FFN_SHARD = FFN // N_SHARD
PLE_DIM = 256
CHUNK = 128
LANES = 128
ALPHA = 2.0 ** 0.25
LN_EPS = 1e-5
GN_EPS = 1e-5
NEG_INF = -1e30
ADAM_LR = 0.001
ADAM_B1 = 0.9
ADAM_B2 = 0.999
ADAM_EPS = 1e-08
ADAM_WD = 0.01
ADAM_STEP = 10
VMEM_LIMIT = 56 * 1024 * 1024
MESH = pl.DeviceIdType.MESH

CB_RQ, CB_RK, CB_RV, CB_RG, CB_AQ, CB_AK, CB_AV = 0, 4, 8, 12, 16, 20, 21


def _dot(a, b):
    return jnp.dot(a, b, preferred_element_type=F32)


def _dot_nt(a, b):
    return lax.dot_general(a, b, (((1,), (1,)), ((), ())), preferred_element_type=F32)


def _dot_tn(a, b):
    return lax.dot_general(a, b, (((0,), (0,)), ((), ())), preferred_element_type=F32)


def _sigmoid(x):
    return 1.0 / (1.0 + jnp.exp(-x))


def _params(*sem, vmem=None):
    return pltpu.CompilerParams(dimension_semantics=tuple(sem) if sem else None, vmem_limit_bytes=vmem)


def _head_mean(x, m0):
    s0 = jnp.sum(jnp.where(m0, x, 0.0), axis=1, keepdims=True)
    s1 = jnp.sum(jnp.where(m0, 0.0, x), axis=1, keepdims=True)
    return jnp.where(m0, s0, s1) * (1.0 / HEAD_DIM)


def _inproj(x2d, w_in_t):
    t = x2d.shape[0]
    tm = 512
    nb = 256

    def body(x_ref, w_ref, o_ref):
        xb = x_ref[...].astype(BF16)
        for n in range(0, IN_WIDTH, nb):
            o_ref[:, n:n + nb] = _dot_nt(xb, w_ref[n:n + nb, :]).astype(BF16)

    return pl.pallas_call(
        body, name="inproj", grid=(t // tm,),
        in_specs=[pl.BlockSpec((tm, D_MODEL), lambda i: (i, 0)),
                  pl.BlockSpec((IN_WIDTH, D_MODEL), lambda i: (0, 0))],
        out_specs=pl.BlockSpec((tm, IN_WIDTH), lambda i: (i, 0)),
        out_shape=jax.ShapeDtypeStruct((t, IN_WIDTH), BF16),
        compiler_params=_params("parallel", vmem=VMEM_LIMIT),
    )(x2d, w_in_t)


def _outproj_ln1(y_ret, y_att, x2d, w_out, gain, bias):
    t = x2d.shape[0]
    tm = 512

    def body(yr_ref, ya_ref, x_ref, w_ref, g_ref, b_ref, zh_ref, r_ref, hb_ref):
        mix = _dot(yr_ref[...], w_ref[0:RET_WIDTH, :]) + _dot(ya_ref[...], w_ref[RET_WIDTH:, :])
        z = ALPHA * x_ref[...] + mix
        mu = jnp.mean(z, axis=1, keepdims=True)
        zc = z - mu
        var = jnp.mean(zc * zc, axis=1, keepdims=True)
        r = lax.rsqrt(var + LN_EPS)
        zh = zc * r
        zh_ref[...] = zh
        r_ref[...] = r
        hb_ref[...] = (zh * g_ref[...] + b_ref[...]).astype(BF16)

    row = lambda w: pl.BlockSpec((tm, w), lambda i: (i, 0))
    const = lambda s: pl.BlockSpec(s, lambda i: (0, 0))
    return pl.pallas_call(
        body, name="outproj_ln1", grid=(t // tm,),
        in_specs=[row(RET_WIDTH), row(ATTN_WIDTH), row(D_MODEL), const((D_MODEL, D_MODEL)),
                  const((1, D_MODEL)), const((1, D_MODEL))],
        out_specs=[row(D_MODEL), row(1), row(D_MODEL)],
        out_shape=[jax.ShapeDtypeStruct((t, D_MODEL), F32), jax.ShapeDtypeStruct((t, 1), F32),
                   jax.ShapeDtypeStruct((t, D_MODEL), BF16)],
        compiler_params=_params("parallel", vmem=VMEM_LIMIT),
    )(y_ret, y_att, x2d, w_out, gain, bias)


def _load_resident(step, pairs):
    @pl.when(step == 0)
    def _():
        for src, dst in pairs:
            pltpu.sync_copy(src, dst)


def _ffn_fwd(zh1, hb, p2d, tgt, g1, b1, g2, b2, wg4, wu4, wd4, wpe, wpg):
    t = zh1.shape[0]
    tm = 256

    def body(zh_ref, hb_ref, p_ref, t_ref, g1_ref, b1_ref, g2_ref, b2_ref,
             wg_hbm, wu_hbm, wd_hbm, wpe_hbm, wpg_hbm,
             dz_ref, gs_ref, us_ref, pg_ref, ple_ref, loss_ref, dg2_ref, db2_ref,
             wg, wu, wd, wpe, wpg):
        step = pl.program_id(0)
        _load_resident(step, [(wg_hbm, wg), (wu_hbm, wu), (wd_hbm, wd), (wpe_hbm, wpe), (wpg_hbm, wpg)])

        @pl.when(step == 0)
        def _():
            loss_ref[...] = jnp.zeros_like(loss_ref)
            dg2_ref[...] = jnp.zeros_like(dg2_ref)
            db2_ref[...] = jnp.zeros_like(db2_ref)

        h1 = zh_ref[...] * g1_ref[...] + b1_ref[...]
        hbv = hb_ref[...]
        ffn = jnp.zeros((tm, D_MODEL), F32)
        for j in range(N_SHARD):
            gj = _dot_nt(hbv, wg[j])
            uj = _dot_nt(hbv, wu[j])
            gs_ref[j] = gj.astype(BF16)
            us_ref[j] = uj.astype(BF16)
            act = (gj * _sigmoid(gj) * uj).astype(BF16)
            ffn = ffn + _dot(act, wd[j])
        ple = _dot(p_ref[...].astype(BF16), wpe[...])
        pg = _sigmoid(_dot(hbv, wpg[...]))
        pg_ref[...] = pg.astype(BF16)
        ple_ref[...] = ple.astype(BF16)
        z2 = ALPHA * h1 + ffn + pg * ple
        mu = jnp.mean(z2, axis=1, keepdims=True)
        zc = z2 - mu
        var = jnp.mean(zc * zc, axis=1, keepdims=True)
        r = lax.rsqrt(var + LN_EPS)
        zh2 = zc * r
        err = zh2 * g2_ref[...] + b2_ref[...] - t_ref[...]
        loss_ref[...] += jnp.sum(err * err)
        dy = err * (1.0 / D_MODEL)
        dg2_ref[...] += jnp.sum(dy * zh2, axis=0, keepdims=True)
        db2_ref[...] += jnp.sum(dy, axis=0, keepdims=True)
        dzh = dy * g2_ref[...]
        m1 = jnp.mean(dzh, axis=1, keepdims=True)
        m2 = jnp.mean(dzh * zh2, axis=1, keepdims=True)
        dz_ref[...] = r * (dzh - m1 - zh2 * m2)

    row = lambda w: pl.BlockSpec((tm, w), lambda i: (i, 0))
    const = lambda s: pl.BlockSpec(s, lambda i: (0, 0))
    sh = pl.BlockSpec((N_SHARD, tm, FFN_SHARD), lambda i: (0, i, 0))
    hbm = pl.BlockSpec(memory_space=pl.ANY)
    return pl.pallas_call(
        body, name="ffn_fwd", grid=(t // tm,),
        in_specs=[row(D_MODEL), row(D_MODEL), row(PLE_DIM), row(D_MODEL),
                  const((1, D_MODEL)), const((1, D_MODEL)), const((1, D_MODEL)), const((1, D_MODEL)),
                  hbm, hbm, hbm, hbm, hbm],
        out_specs=[row(D_MODEL), sh, sh, row(D_MODEL), row(D_MODEL),
                   const((8, LANES)), const((1, D_MODEL)), const((1, D_MODEL))],
        out_shape=[jax.ShapeDtypeStruct((t, D_MODEL), F32),
                   jax.ShapeDtypeStruct((N_SHARD, t, FFN_SHARD), BF16),
                   jax.ShapeDtypeStruct((N_SHARD, t, FFN_SHARD), BF16),
                   jax.ShapeDtypeStruct((t, D_MODEL), BF16), jax.ShapeDtypeStruct((t, D_MODEL), BF16),
                   jax.ShapeDtypeStruct((8, LANES), F32),
                   jax.ShapeDtypeStruct((1, D_MODEL), F32), jax.ShapeDtypeStruct((1, D_MODEL), F32)],
        scratch_shapes=[pltpu.VMEM(wg4.shape, BF16), pltpu.VMEM(wu4.shape, BF16), pltpu.VMEM(wd4.shape, BF16),
                        pltpu.VMEM(wpe.shape, BF16), pltpu.VMEM(wpg.shape, BF16)],
        compiler_params=_params("arbitrary", vmem=VMEM_LIMIT),
    )(zh1, hb, p2d, tgt, g1, b1, g2, b2, wg4, wu4, wd4, wpe, wpg)


def _ret_tables(lgf, lgb):
    c = CHUNK
    row = lax.broadcasted_iota(jnp.int32, (c, LANES), 0).astype(F32)
    ii = lax.broadcasted_iota(jnp.int32, (c, c), 0).astype(F32)
    jj = lax.broadcasted_iota(jnp.int32, (c, c), 1).astype(F32)
    diff = ii - jj
    dmats = []
    for h in range(2):
        lf = lgf[:, h * HEAD_DIM:h * HEAD_DIM + 1]
        lb = lgb[:, h * HEAD_DIM:h * HEAD_DIM + 1]
        dmats.append(jnp.where(diff > 0, jnp.exp(lf * jnp.maximum(diff, 0.0)),
                               jnp.where(diff < 0, jnp.exp(lb * jnp.maximum(-diff, 0.0)), 2.0)))
    tab = dict(
        qdec_f=jnp.exp(lgf * (row + 1.0)), kdec_f=jnp.exp(lgf * (c - 1.0 - row)),
        qdec_b=jnp.exp(lgb * (c - row)), kdec_b=jnp.exp(lgb * row),
        cdec_f=jnp.exp(lgf * c), cdec_b=jnp.exp(lgb * c),
        d0=dmats[0], d1=dmats[1], row=row, diff=diff)
    r = lax.broadcasted_iota(jnp.int32, (LANES, LANES), 0) < HEAD_DIM
    cc = lax.broadcasted_iota(jnp.int32, (LANES, LANES), 1) < HEAD_DIM
    tab["bd"] = r == cc
    tab["m0"] = lax.broadcasted_iota(jnp.int32, (c, LANES), 1) < HEAD_DIM
    return tab


def _ret_specs(s):
    blk = lambda cb: pl.BlockSpec((None, s, LANES), lambda b, p, cb=cb: (b, 0, cb + p))
    lane = pl.BlockSpec((None, 1, LANES), lambda b, p: (p, 0, 0))
    gain = pl.BlockSpec((1, LANES), lambda b, p: (0, p))
    pair = pl.BlockSpec((None, s, LANES), lambda b, p: (b, 0, p))
    return blk, lane, gain, pair


def _ret_fwd(u3, lgf_l, lgb_l, gn_gain):
    bsz, s, _ = u3.shape
    n_chunk = s // CHUNK
    c = CHUNK

    def body(q_ref, k_ref, v_ref, g_ref, lgf_ref, lgb_ref, gain_ref, y_ref, o_ref, rb_ref):
        tb = _ret_tables(lgf_ref[...], lgb_ref[...])
        m0, bd = tb["m0"], tb["bd"]
        gain = gain_ref[...]

        def back_states(i, rb):
            n = n_chunk - 1 - i
            sl = pl.ds(pl.multiple_of(n * c, c), c)
            rb_ref[n] = rb
            kb = (k_ref[sl, :].astype(F32) * tb["kdec_b"]).astype(BF16)
            return rb * tb["cdec_b"] + jnp.where(bd, _dot_tn(kb, v_ref[sl, :]), 0.0)

        lax.fori_loop(0, n_chunk, back_states, jnp.zeros((LANES, LANES), F32))

        def chunk(n, rf):
            sl = pl.ds(pl.multiple_of(n * c, c), c)
            q = q_ref[sl, :].astype(F32) * 0.125
            k = k_ref[sl, :]
            v = v_ref[sl, :]
            kf32 = k.astype(F32)
            vf32 = v.astype(F32)
            q0 = jnp.where(m0, q, 0.0).astype(BF16)
            q1 = jnp.where(m0, 0.0, q).astype(BF16)
            a0 = (_dot_nt(q0, k) * tb["d0"]).astype(BF16)
            a1 = (_dot_nt(q1, k) * tb["d1"]).astype(BF16)
            v0 = jnp.where(m0, vf32, 0.0).astype(BF16)
            v1 = jnp.where(m0, 0.0, vf32).astype(BF16)
            y = _dot(a0, v0) + _dot(a1, v1)
            y = y + _dot((q * tb["qdec_f"]).astype(BF16), rf.astype(BF16))
            y = y + _dot((q * tb["qdec_b"]).astype(BF16), rb_ref[n].astype(BF16))
            rf_new = rf * tb["cdec_f"] + jnp.where(bd, _dot_tn((kf32 * tb["kdec_f"]).astype(BF16), v), 0.0)
            mu = _head_mean(y, m0)
            yc = y - mu
            var = _head_mean(yc * yc, m0)
            yh = yc * lax.rsqrt(var + GN_EPS)
            g = g_ref[sl, :].astype(F32)
            y_ref[sl, :] = y
            o_ref[sl, :] = (yh * gain * (g * _sigmoid(g))).astype(BF16)
            return rf_new

        lax.fori_loop(0, n_chunk, chunk, jnp.zeros((LANES, LANES), F32))

    blk, lane, gain, pair = _ret_specs(s)
    return pl.pallas_call(
        body, name="ret_fwd", grid=(bsz, 4),
        in_specs=[blk(CB_RQ), blk(CB_RK), blk(CB_RV), blk(CB_RG), lane, lane, gain],
        out_specs=[pair, pair],
        out_shape=[jax.ShapeDtypeStruct((bsz, s, RET_WIDTH), F32), jax.ShapeDtypeStruct((bsz, s, RET_WIDTH), BF16)],
        scratch_shapes=[pltpu.VMEM((n_chunk, LANES, LANES), F32)],
        compiler_params=_params("parallel", "parallel", vmem=VMEM_LIMIT),
    )(u3, u3, u3, u3, lgf_l, lgb_l, gn_gain)


def _ret_bwd(u3, y_pre, d_o, lgf_l, lgb_l, gn_gain):
    bsz, s, _ = u3.shape
    n_chunk = s // CHUNK
    c = CHUNK

    def body(q_ref, k_ref, v_ref, g_ref, y_ref, do_ref, lgf_ref, lgb_ref, gain_ref,
             dq_ref, dk_ref, dv_ref, dg_ref, part_ref,
             rb_ref, rf_ref, dy_ref, dk_acc, dv_acc, af0, af1, ab0, ab1, vec_ref):
        tb = _ret_tables(lgf_ref[...], lgb_ref[...])
        m0, bd, row = tb["m0"], tb["bd"], tb["row"]
        gain = gain_ref[...]
        wf = jnp.maximum(tb["diff"], 0.0)
        wb = jnp.maximum(-tb["diff"], 0.0)
        zero_state = jnp.zeros((LANES, LANES), F32)
        for ref in (af0, af1, ab0, ab1):
            ref[...] = jnp.zeros_like(ref)
        vec_ref[...] = jnp.zeros_like(vec_ref)

        def back_states(i, rb):
            n = n_chunk - 1 - i
            sl = pl.ds(pl.multiple_of(n * c, c), c)
            rb_ref[n] = rb
            kb = (k_ref[sl, :].astype(F32) * tb["kdec_b"]).astype(BF16)
            return rb * tb["cdec_b"] + jnp.where(bd, _dot_tn(kb, v_ref[sl, :]), 0.0)

        lax.fori_loop(0, n_chunk, back_states, zero_state)

        def sweep_fwd(n, carry):
            rf, gb = carry
            sl = pl.ds(pl.multiple_of(n * c, c), c)
            rf_ref[n] = rf
            rbn = rb_ref[n]
            q = q_ref[sl, :].astype(F32) * 0.125
            k = k_ref[sl, :]
            v = v_ref[sl, :]
            kf32 = k.astype(F32)
            vf32 = v.astype(F32)
            y = y_ref[sl, :]
            do = do_ref[sl, :].astype(F32)
            g = g_ref[sl, :].astype(F32)
            mu = _head_mean(y, m0)
            yc = y - mu
            rstd = lax.rsqrt(_head_mean(yc * yc, m0) + GN_EPS)
            yh = yc * rstd
            sg = _sigmoid(g)
            sil = g * sg
            dyh = do * gain * sil
            dg_ref[sl, :] = (do * yh * gain * sg * (1.0 + g * (1.0 - sg))).astype(BF16)
            dgain = jnp.sum(do * yh * sil, axis=0, keepdims=True)
            dy = rstd * (dyh - _head_mean(dyh, m0) - yh * _head_mean(dyh * yh, m0))
            dyb = dy.astype(BF16)
            dy_ref[sl, :] = dyb
            dy0 = jnp.where(m0, dy, 0.0).astype(BF16)
            dy1 = jnp.where(m0, 0.0, dy).astype(BF16)
            q0 = jnp.where(m0, q, 0.0).astype(BF16)
            q1 = jnp.where(m0, 0.0, q).astype(BF16)
            k0 = jnp.where(m0, kf32, 0.0).astype(BF16)
            k1 = jnp.where(m0, 0.0, kf32).astype(BF16)
            a0 = _dot_nt(q0, k) * tb["d0"]
            a1 = _dot_nt(q1, k) * tb["d1"]
            da0 = _dot_nt(dy0, v)
            da1 = _dot_nt(dy1, v)
            pr0 = da0 * a0
            pr1 = da1 * a1
            af0[...] += pr0 * wf
            ab0[...] += pr0 * wb
            af1[...] += pr1 * wf
            ab1[...] += pr1 * wb
            ds0 = (da0 * tb["d0"]).astype(BF16)
            ds1 = (da1 * tb["d1"]).astype(BF16)
            dq = _dot(ds0, k0) + _dot(ds1, k1)
            dk = _dot_tn(ds0, q0) + _dot_tn(ds1, q1)
            dv = _dot_tn(a0.astype(BF16), dy0) + _dot_tn(a1.astype(BF16), dy1)
            qf = (q * tb["qdec_f"]).astype(BF16)
            qb = (q * tb["qdec_b"]).astype(BF16)
            rfb = rf.astype(BF16)
            rbb = rbn.astype(BF16)
            ycf = _dot(qf, rfb)
            ycb = _dot(qb, rbb)
            dq = dq + _dot_nt(dyb, rfb) * tb["qdec_f"] + _dot_nt(dyb, rbb) * tb["qdec_b"]
            dlf = jnp.sum((row + 1.0) * ycf * dy, axis=0, keepdims=True)
            dlb = jnp.sum((c - row) * ycb * dy, axis=0, keepdims=True)
            gbb = gb.astype(BF16)
            kbd = kf32 * tb["kdec_b"]
            dkb = _dot_nt(v, gbb)
            dk = dk + dkb * tb["kdec_b"]
            dv = dv + _dot(kbd.astype(BF16), gbb)
            dlb = dlb + jnp.sum(row * kbd * dkb, axis=0, keepdims=True)
            dlb = dlb + c * tb["cdec_b"] * jnp.sum(gb * rbn, axis=0, keepdims=True)
            gb_new = jnp.where(bd, _dot_tn(qb, dyb), 0.0) + tb["cdec_b"] * gb
            rf_new = rf * tb["cdec_f"] + jnp.where(bd, _dot_tn((kf32 * tb["kdec_f"]).astype(BF16), v), 0.0)
            dq_ref[sl, :] = (dq * 0.125).astype(BF16)
            dk_acc[sl, :] = dk
            dv_acc[sl, :] = dv
            vec_ref[0:1, :] += dlf
            vec_ref[1:2, :] += dlb
            vec_ref[6:7, :] += dgain
            return rf_new, gb_new

        lax.fori_loop(0, n_chunk, sweep_fwd, (zero_state, zero_state))

        def sweep_bwd(i, gf):
            n = n_chunk - 1 - i
            sl = pl.ds(pl.multiple_of(n * c, c), c)
            q = q_ref[sl, :].astype(F32) * 0.125
            kf32 = k_ref[sl, :].astype(F32)
            v = v_ref[sl, :]
            dyb = dy_ref[sl, :]
            gfb = gf.astype(BF16)
            kfd = kf32 * tb["kdec_f"]
            dkf = _dot_nt(v, gfb)
            dk_ref[sl, :] = (dk_acc[sl, :] + dkf * tb["kdec_f"]).astype(BF16)
            dv_ref[sl, :] = (dv_acc[sl, :] + _dot(kfd.astype(BF16), gfb)).astype(BF16)
            dlf = jnp.sum((c - 1.0 - row) * kfd * dkf, axis=0, keepdims=True)
            dlf = dlf + c * tb["cdec_f"] * jnp.sum(gf * rf_ref[n], axis=0, keepdims=True)
            vec_ref[0:1, :] += dlf
            qf = (q * tb["qdec_f"]).astype(BF16)
            return jnp.where(bd, _dot_tn(qf, dyb), 0.0) + tb["cdec_f"] * gf

        lax.fori_loop(0, n_chunk, sweep_bwd, zero_state)
        vec_ref[2:3, :] = jnp.sum(af0[...], axis=0, keepdims=True)
        vec_ref[3:4, :] = jnp.sum(af1[...], axis=0, keepdims=True)
        vec_ref[4:5, :] = jnp.sum(ab0[...], axis=0, keepdims=True)
        vec_ref[5:6, :] = jnp.sum(ab1[...], axis=0, keepdims=True)
        part_ref[...] = vec_ref[...]

    blk, lane, gain, pair = _ret_specs(s)
    out_bf = jax.ShapeDtypeStruct((bsz, s, RET_WIDTH), BF16)
    return pl.pallas_call(
        body, name="ret_bwd", grid=(bsz, 4),
        in_specs=[blk(CB_RQ), blk(CB_RK), blk(CB_RV), blk(CB_RG), pair, pair, lane, lane, gain],
        out_specs=[pair, pair, pair, pair, pl.BlockSpec((None, None, 8, LANES), lambda b, p: (b, p, 0, 0))],
        out_shape=[out_bf, out_bf, out_bf, out_bf, jax.ShapeDtypeStruct((bsz, 4, 8, LANES), F32)],
        scratch_shapes=[pltpu.VMEM((n_chunk, LANES, LANES), F32), pltpu.VMEM((n_chunk, LANES, LANES), F32),
                        pltpu.VMEM((s, LANES), BF16), pltpu.VMEM((s, LANES), F32), pltpu.VMEM((s, LANES), F32),
                        pltpu.VMEM((c, c), F32), pltpu.VMEM((c, c), F32), pltpu.VMEM((c, c), F32),
                        pltpu.VMEM((c, c), F32), pltpu.VMEM((8, LANES), F32)],
        compiler_params=_params("parallel", "parallel", vmem=VMEM_LIMIT),
    )(u3, u3, u3, u3, y_pre, d_o, lgf_l, lgb_l, gn_gain)


def _attn_window_tables(n, s):
    qi = lax.broadcasted_iota(jnp.int32, (CHUNK, 3 * CHUNK), 0)
    kj = lax.broadcasted_iota(jnp.int32, (CHUNK, 3 * CHUNK), 1)
    dist = jnp.abs(kj - CHUNK - qi)
    kpos = n * CHUNK - CHUNK + kj
    valid = (dist <= CHUNK) & (kpos >= 0) & (kpos < s)
    return dist.astype(F32), valid


def _dup_kv_head(x, g):
    lane = lax.broadcasted_iota(jnp.int32, x.shape, 1)
    keep = (lane < HEAD_DIM) == (g == 0)
    xf = x.astype(F32)
    return jnp.where(keep, xf, pltpu.roll(xf, HEAD_DIM, 1))


def _attn_specs(s):
    q = pl.BlockSpec((None, s, 2 * LANES), lambda b, g: (b, 0, CB_AQ // 2 + g))
    k = pl.BlockSpec((None, s, LANES), lambda b, g: (b, 0, CB_AK))
    v = pl.BlockSpec((None, s, LANES), lambda b, g: (b, 0, CB_AV))
    grp = pl.BlockSpec((None, s, 2 * LANES), lambda b, g: (b, 0, g))
    smem = pl.BlockSpec(memory_space=pltpu.SMEM)
    return q, k, v, grp, smem


def _fill_padded(dst_ref, val, s):
    dst_ref[0:CHUNK, :] = jnp.zeros((CHUNK, LANES), dst_ref.dtype)
    dst_ref[CHUNK:CHUNK + s, :] = val.astype(dst_ref.dtype)
    dst_ref[CHUNK + s:2 * CHUNK + s, :] = jnp.zeros((CHUNK, LANES), dst_ref.dtype)


def _attn_probs(qh, kw, slope, snk, dist, valid):
    sc = _dot_nt(qh, kw)
    sc = jnp.where(valid, sc - slope * dist, NEG_INF)
    m = jnp.maximum(jnp.max(sc, axis=1, keepdims=True), snk)
    e = jnp.exp(sc - m)
    es = jnp.exp(snk - m)
    inv = 1.0 / (jnp.sum(e, axis=1, keepdims=True) + es)
    return e * inv, es * inv


def _attn_fwd(u3, slopes, sink):
    bsz, s, _ = u3.shape
    n_blk = s // CHUNK

    def body(slope_ref, sink_ref, q_ref, k_ref, v_ref, o_ref, kp_ref, vp_ref):
        g = pl.program_id(1)
        _fill_padded(kp_ref, _dup_kv_head(k_ref[...], g), s)
        _fill_padded(vp_ref, _dup_kv_head(v_ref[...], g), s)
        m0 = lax.broadcasted_iota(jnp.int32, (CHUNK, LANES), 1) < HEAD_DIM
        m0w = lax.broadcasted_iota(jnp.int32, (3 * CHUNK, LANES), 1) < HEAD_DIM

        def blk(n, carry):
            r0 = pl.multiple_of(n * CHUNK, CHUNK)
            kw = kp_ref[pl.ds(r0, 3 * CHUNK), :]
            vw = vp_ref[pl.ds(r0, 3 * CHUNK), :].astype(F32)
            vws = (jnp.where(m0w, vw, 0.0).astype(BF16), jnp.where(m0w, 0.0, vw).astype(BF16))
            dist, valid = _attn_window_tables(n, s)
            for pr in range(2):
                qp = q_ref[pl.ds(r0, CHUNK), pr * LANES:(pr + 1) * LANES].astype(F32) * 0.125
                out = jnp.zeros((CHUNK, LANES), F32)
                for hh in range(2):
                    h = g * 4 + pr * 2 + hh
                    qh = (jnp.where(m0, qp, 0.0) if hh == 0 else jnp.where(m0, 0.0, qp)).astype(BF16)
                    p, _ = _attn_probs(qh, kw, slope_ref[h], sink_ref[h], dist, valid)
                    out = out + _dot(p.astype(BF16), vws[hh])
                o_ref[pl.ds(r0, CHUNK), pr * LANES:(pr + 1) * LANES] = out.astype(BF16)
            return carry

        lax.fori_loop(0, n_blk, blk, 0)

    q, k, v, grp, smem = _attn_specs(s)
    return pl.pallas_call(
        body, name="attn_fwd", grid=(bsz, 2),
        in_specs=[smem, smem, q, k, v],
        out_specs=grp,
        out_shape=jax.ShapeDtypeStruct((bsz, s, ATTN_WIDTH), BF16),
        scratch_shapes=[pltpu.VMEM((s + 2 * CHUNK, LANES), BF16), pltpu.VMEM((s + 2 * CHUNK, LANES), BF16)],
        compiler_params=_params("parallel", "parallel", vmem=VMEM_LIMIT),
    )(slopes, sink, u3, u3, u3)


def _attn_bwd(u3, d_o, slopes, sink):
    bsz, s, _ = u3.shape
    n_blk = s // CHUNK

    def body(slope_ref, sink_ref, q_ref, k_ref, v_ref, do_ref, dq_ref, dkv_ref, ds_ref,
             kp_ref, vp_ref, dk_acc, dv_acc):
        g = pl.program_id(1)
        _fill_padded(kp_ref, _dup_kv_head(k_ref[...], g), s)
        _fill_padded(vp_ref, _dup_kv_head(v_ref[...], g), s)
        dk_acc[...] = jnp.zeros_like(dk_acc)
        dv_acc[...] = jnp.zeros_like(dv_acc)
        m0 = lax.broadcasted_iota(jnp.int32, (CHUNK, LANES), 1) < HEAD_DIM

        def blk(n, dsink):
            r0 = pl.multiple_of(n * CHUNK, CHUNK)
            win = pl.ds(r0, 3 * CHUNK)
            kw = kp_ref[win, :]
            vw = vp_ref[win, :]
            dist, valid = _attn_window_tables(n, s)
            dkw = jnp.zeros((3 * CHUNK, LANES), F32)
            dvw = jnp.zeros((3 * CHUNK, LANES), F32)
            new_dsink = []
            for pr in range(2):
                cols = slice(pr * LANES, (pr + 1) * LANES)
                qp = q_ref[pl.ds(r0, CHUNK), cols].astype(F32) * 0.125
                dop = do_ref[pl.ds(r0, CHUNK), cols].astype(F32)
                dq = jnp.zeros((CHUNK, LANES), F32)
                for hh in range(2):
                    h = g * 4 + pr * 2 + hh
                    sel = (lambda a: jnp.where(m0, a, 0.0)) if hh == 0 else (lambda a: jnp.where(m0, 0.0, a))
                    qh = sel(qp).astype(BF16)
                    doh = sel(dop).astype(BF16)
                    p, ps = _attn_probs(qh, kw, slope_ref[h], sink_ref[h], dist, valid)
                    dp = _dot_nt(doh, vw)
                    delta = jnp.sum(p * dp, axis=1, keepdims=True)
                    dsc = (p * (dp - delta)).astype(BF16)
                    dsh = jnp.sum(ps * delta, axis=0, keepdims=True)
                    new_dsink.append(dsink[pr * 2 + hh] - jnp.broadcast_to(dsh, (1, LANES)))
                    dq = dq + sel(_dot(dsc, kw))
                    dkw = dkw + _dot_tn(dsc, qh)
                    dvw = dvw + _dot_tn(p.astype(BF16), doh)
                dq_ref[pl.ds(r0, CHUNK), cols] = (dq * 0.125).astype(BF16)
            dk_acc[win, :] += dkw
            dv_acc[win, :] += dvw
            return tuple(new_dsink)

        dsink = lax.fori_loop(0, n_blk, blk, tuple(jnp.zeros((1, LANES), F32) for _ in range(4)))
        dk = dk_acc[CHUNK:CHUNK + s, :]
        dv = dv_acc[CHUNK:CHUNK + s, :]
        lane = lax.broadcasted_iota(jnp.int32, (s, LANES), 1)
        fold = lambda a: a + pltpu.roll(a, HEAD_DIM, 1)
        dkv_ref[...] = jnp.where(lane < HEAD_DIM, fold(dk), fold(dv)).astype(BF16)
        ds_ref[...] = jnp.zeros_like(ds_ref)
        for i in range(4):
            ds_ref[i:i + 1, :] = dsink[i]

    q, k, v, grp, smem = _attn_specs(s)
    return pl.pallas_call(
        body, name="attn_bwd", grid=(bsz, 2),
        in_specs=[smem, smem, q, k, v, grp],
        out_specs=[grp, pl.BlockSpec((None, s, LANES), lambda b, g: (b, 0, g)),
                   pl.BlockSpec((None, None, 8, LANES), lambda b, g: (b, g, 0, 0))],
        out_shape=[jax.ShapeDtypeStruct((bsz, s, ATTN_WIDTH), BF16), jax.ShapeDtypeStruct((bsz, s, 2 * LANES), BF16),
                   jax.ShapeDtypeStruct((bsz, 2, 8, LANES), F32)],
        scratch_shapes=[pltpu.VMEM((s + 2 * CHUNK, LANES), BF16), pltpu.VMEM((s + 2 * CHUNK, LANES), BF16),
                        pltpu.VMEM((s + 2 * CHUNK, LANES), F32), pltpu.VMEM((s + 2 * CHUNK, LANES), F32)],
        compiler_params=_params("parallel", "parallel", vmem=VMEM_LIMIT),
    )(slopes, sink, u3, u3, u3, d_o)


def _ffn_bwd(dz2, gs, us, pg, ple, zh1, r1, g1, wg4, wu4, wd4, wpg, w_out):
    t = dz2.shape[0]
    tm = 256

    def body(dz_ref, gs_ref, us_ref, pg_ref, ple_ref, zh_ref, r_ref, g1_ref,
             wg_hbm, wu_hbm, wd_hbm, wpg_hbm, wo_hbm,
             dgs_ref, dus_ref, dsp_ref, dple_ref, dz1_ref, dyr_ref, dya_ref, dg1_ref, db1_ref,
             wg, wu, wd, wpg, wo):
        step = pl.program_id(0)
        _load_resident(step, [(wg_hbm, wg), (wu_hbm, wu), (wd_hbm, wd), (wpg_hbm, wpg), (wo_hbm, wo)])

        @pl.when(step == 0)
        def _():
            dg1_ref[...] = jnp.zeros_like(dg1_ref)
            db1_ref[...] = jnp.zeros_like(db1_ref)

        dz = dz_ref[...]
        dzb = dz.astype(BF16)
        dh = ALPHA * dz
        for j in range(N_SHARD):
            da = _dot_nt(dzb, wd[j])
            gj = gs_ref[j].astype(F32)
            uj = us_ref[j].astype(F32)
            sg = _sigmoid(gj)
            dgj = (da * uj * sg * (1.0 + gj * (1.0 - sg))).astype(BF16)
            duj = (da * gj * sg).astype(BF16)
            dgs_ref[j] = dgj
            dus_ref[j] = duj
            dh = dh + _dot(dgj, wg[j]) + _dot(duj, wu[j])
        pgv = pg_ref[...].astype(F32)
        plev = ple_ref[...].astype(F32)
        dple_ref[...] = (dz * pgv).astype(BF16)
        dsp = (dz * plev * pgv * (1.0 - pgv)).astype(BF16)
        dsp_ref[...] = dsp
        dh = dh + _dot_nt(dsp, wpg[...])
        zh = zh_ref[...]
        dg1_ref[...] += jnp.sum(dh * zh, axis=0, keepdims=True)
        db1_ref[...] += jnp.sum(dh, axis=0, keepdims=True)
        dzh = dh * g1_ref[...]
        m1 = jnp.mean(dzh, axis=1, keepdims=True)
        m2 = jnp.mean(dzh * zh, axis=1, keepdims=True)
        dz1 = r_ref[...] * (dzh - m1 - zh * m2)
        dz1_ref[...] = dz1
        dyc = _dot_nt(dz1.astype(BF16), wo[...])
        dyr_ref[...] = dyc[:, 0:RET_WIDTH].astype(BF16)
        dya_ref[...] = dyc[:, RET_WIDTH:].astype(BF16)

    row = lambda w: pl.BlockSpec((tm, w), lambda i: (i, 0))
    const = lambda s: pl.BlockSpec(s, lambda i: (0, 0))
    sh = pl.BlockSpec((N_SHARD, tm, FFN_SHARD), lambda i: (0, i, 0))
    hbm = pl.BlockSpec(memory_space=pl.ANY)
    sh_shape = jax.ShapeDtypeStruct((N_SHARD, t, FFN_SHARD), BF16)
    return pl.pallas_call(
        body, name="ffn_bwd", grid=(t // tm,),
        in_specs=[row(D_MODEL), sh, sh, row(D_MODEL), row(D_MODEL), row(D_MODEL), row(1), const((1, D_MODEL)),
                  hbm, hbm, hbm, hbm, hbm],
        out_specs=[sh, sh, row(D_MODEL), row(D_MODEL), row(D_MODEL), row(RET_WIDTH), row(ATTN_WIDTH),
                   const((1, D_MODEL)), const((1, D_MODEL))],
        out_shape=[sh_shape, sh_shape, jax.ShapeDtypeStruct((t, D_MODEL), BF16),
                   jax.ShapeDtypeStruct((t, D_MODEL), BF16), jax.ShapeDtypeStruct((t, D_MODEL), F32),
                   jax.ShapeDtypeStruct((t, RET_WIDTH), BF16), jax.ShapeDtypeStruct((t, ATTN_WIDTH), BF16),
                   jax.ShapeDtypeStruct((1, D_MODEL), F32), jax.ShapeDtypeStruct((1, D_MODEL), F32)],
        scratch_shapes=[pltpu.VMEM(wg4.shape, BF16), pltpu.VMEM(wu4.shape, BF16), pltpu.VMEM(wd4.shape, BF16),
                        pltpu.VMEM(wpg.shape, BF16), pltpu.VMEM(w_out.shape, BF16)],
        compiler_params=_params("arbitrary", vmem=VMEM_LIMIT),
    )(dz2, gs, us, pg, ple, zh1, r1, g1, wg4, wu4, wd4, wpg, w_out)


def _wgrad_misc(y_ret, y_att, dz1, hb, dsp, p2d, dple):
    t = dz1.shape[0]
    tk = min(t, 512)

    def body(yr_ref, ya_ref, dz_ref, hb_ref, dsp_ref, p_ref, dple_ref, wo_ref, wpg_ref, wpe_ref):
        @pl.when(pl.program_id(0) == 0)
        def _():
            wo_ref[...] = jnp.zeros_like(wo_ref)
            wpg_ref[...] = jnp.zeros_like(wpg_ref)
            wpe_ref[...] = jnp.zeros_like(wpe_ref)

        dzb = dz_ref[...].astype(BF16)
        wo_ref[0:RET_WIDTH, :] += _dot_tn(yr_ref[...], dzb)
        wo_ref[RET_WIDTH:, :] += _dot_tn(ya_ref[...], dzb)
        wpg_ref[...] += _dot_tn(hb_ref[...], dsp_ref[...])
        wpe_ref[...] += _dot_tn(p_ref[...].astype(BF16), dple_ref[...])

    row = lambda w: pl.BlockSpec((tk, w), lambda k: (k, 0))
    const = lambda s: pl.BlockSpec(s, lambda k: (0, 0))
    return pl.pallas_call(
        body, name="wgrad_misc", grid=(t // tk,),
        in_specs=[row(RET_WIDTH), row(ATTN_WIDTH), row(D_MODEL), row(D_MODEL), row(D_MODEL), row(PLE_DIM),
                  row(D_MODEL)],
        out_specs=[const((D_MODEL, D_MODEL)), const((D_MODEL, D_MODEL)), const((PLE_DIM, D_MODEL))],
        out_shape=[jax.ShapeDtypeStruct((D_MODEL, D_MODEL), F32), jax.ShapeDtypeStruct((D_MODEL, D_MODEL), F32),
                   jax.ShapeDtypeStruct((PLE_DIM, D_MODEL), F32)],
        compiler_params=_params("arbitrary", vmem=VMEM_LIMIT),
    )(y_ret, y_att, dz1, hb, dsp, p2d, dple)


def _wgrad_ffn(gs, us, dgs, dus, hb, dz2):
    t = dz2.shape[0]
    tk = min(t, 512)

    def body(g_ref, u_ref, dg_ref, du_ref, hb_ref, dz_ref, og_ref, ou_ref, od_ref):
        @pl.when(pl.program_id(1) == 0)
        def _():
            og_ref[...] = jnp.zeros_like(og_ref)
            ou_ref[...] = jnp.zeros_like(ou_ref)
            od_ref[...] = jnp.zeros_like(od_ref)

        hbv = hb_ref[...]
        og_ref[...] += _dot_tn(dg_ref[...], hbv)
        ou_ref[...] += _dot_tn(du_ref[...], hbv)
        gj = g_ref[...].astype(F32)
        act = (gj * _sigmoid(gj) * u_ref[...].astype(F32)).astype(BF16)
        od_ref[...] += _dot_tn(act, dz_ref[...].astype(BF16))

    a_spec = pl.BlockSpec((None, tk, FFN_SHARD), lambda j, k: (j, k, 0))
    b_spec = pl.BlockSpec((tk, D_MODEL), lambda j, k: (k, 0))
    o_spec = pl.BlockSpec((None, FFN_SHARD, D_MODEL), lambda j, k: (j, 0, 0))
    o_shape = jax.ShapeDtypeStruct((N_SHARD, FFN_SHARD, D_MODEL), F32)
    return pl.pallas_call(
        body, name="wgrad_ffn", grid=(N_SHARD, t // tk),
        in_specs=[a_spec, a_spec, a_spec, a_spec, b_spec, b_spec],
        out_specs=[o_spec, o_spec, o_spec], out_shape=[o_shape, o_shape, o_shape],
        compiler_params=_params("parallel", "arbitrary", vmem=VMEM_LIMIT),
    )(gs, us, dgs, dus, hb, dz2)


def _wgrad_in(pieces, x2d):
    t = x2d.shape[0]
    tk = min(t, 512)

    def body(p0, p1, p2, p3, p4, pkv, x_ref, o_ref):
        @pl.when(pl.program_id(0) == 0)
        def _():
            o_ref[...] = jnp.zeros_like(o_ref)

        xb = x_ref[...].astype(BF16)
        for i, ref in enumerate((p0, p1, p2, p3, p4)):
            o_ref[i * 512:(i + 1) * 512, :] += _dot_tn(ref[...], xb)
        o_ref[2560:IN_WIDTH, :] += _dot_tn(pkv[...], xb)

    row = lambda w: pl.BlockSpec((tk, w), lambda k: (k, 0))
    return pl.pallas_call(
        body, name="wgrad_in", grid=(t // tk,),
        in_specs=[row(512)] * 5 + [row(256), row(D_MODEL)],
        out_specs=pl.BlockSpec((IN_WIDTH, D_MODEL), lambda k: (0, 0)),
        out_shape=jax.ShapeDtypeStruct((IN_WIDTH, D_MODEL), F32),
        compiler_params=_params("arbitrary", vmem=VMEM_LIMIT),
    )(*pieces, x2d)


def _inproj_bwd(dz1, pieces, w_main, w_kv):
    t = dz1.shape[0]
    tm = 512

    def body(dz_ref, p0, p1, p2, p3, p4, pkv, wm_ref, wkv_ref, o_ref):
        acc = ALPHA * dz_ref[...]
        for i, ref in enumerate((p0, p1, p2, p3, p4)):
            acc = acc + _dot(ref[...], wm_ref[i * 512:(i + 1) * 512, :])
        o_ref[...] = acc + _dot(pkv[...], wkv_ref[...])

    row = lambda w: pl.BlockSpec((tm, w), lambda i: (i, 0))
    const = lambda s: pl.BlockSpec(s, lambda i: (0, 0))
    return pl.pallas_call(
        body, name="inproj_bwd", grid=(t // tm,),
        in_specs=[row(D_MODEL)] + [row(512)] * 5 + [row(256), const(w_main.shape), const(w_kv.shape)],
        out_specs=row(D_MODEL),
        out_shape=jax.ShapeDtypeStruct((t, D_MODEL), F32),
        compiler_params=_params("parallel", vmem=VMEM_LIMIT),
    )(dz1, *pieces, w_main, w_kv)


def _coords():
    return lax.axis_index("x"), lax.axis_index("y"), lax.axis_index("c")


def _chip_of(x, y, rel):
    return (1 - x if rel & 2 else x), (1 - y if rel & 1 else y)


def _all_gather_weights(shards):
    nw = len(shards)

    def body(*refs):
        ins, outs = refs[:nw], refs[nw:2 * nw]
        send, recv, fsend, frecv, lsend, lrecv = refs[2 * nw:]
        x, y, c = _coords()
        me = 2 * x + y

        def half(w, chip, cc):
            h = shards[w].shape[0] // 2
            return outs[w].at[chip, pl.ds(cc * h, h), :]

        def chip_copy(w, rel):
            h = shards[w].shape[0] // 2
            kx, ky = _chip_of(x, y, rel)
            return pltpu.make_async_remote_copy(
                src_ref=ins[w].at[pl.ds(c * h, h), :], dst_ref=half(w, me, c),
                send_sem=send.at[w * 3 + rel - 1], recv_sem=recv.at[w * 3 + rel - 1],
                device_id=(kx, ky, c), device_id_type=MESH)

        def arrived(w, rel):
            kx, ky = _chip_of(x, y, rel)
            return pltpu.make_async_remote_copy(
                src_ref=half(w, 2 * kx + ky, c), dst_ref=half(w, 2 * kx + ky, c),
                send_sem=send.at[w * 3 + rel - 1], recv_sem=recv.at[w * 3 + rel - 1],
                device_id=(kx, ky, c), device_id_type=MESH)

        def pass_on(w, rel, cc):
            kx, ky = _chip_of(x, y, rel)
            return pltpu.make_async_remote_copy(
                src_ref=half(w, 2 * kx + ky, cc), dst_ref=half(w, 2 * kx + ky, cc),
                send_sem=fsend.at[w * 3 + rel - 1], recv_sem=frecv.at[w * 3 + rel - 1],
                device_id=(x, y, 1 - c), device_id_type=MESH)

        own = [pltpu.make_async_remote_copy(
            src_ref=ins[w], dst_ref=outs[w].at[me], send_sem=lsend.at[w], recv_sem=lrecv.at[w],
            device_id=(x, y, 1 - c), device_id_type=MESH) for w in range(nw)]
        for cp in own:
            cp.start()
        for rel in (1, 2, 3):
            for w in range(nw):
                chip_copy(w, rel).start()
        for rel in (1, 2, 3):
            for w in range(nw):
                arrived(w, rel).wait_recv()
                pass_on(w, rel, c).start()
        for rel in (1, 2, 3):
            for w in range(nw):
                pass_on(w, rel, 1 - c).wait_recv()
        for rel in (1, 2, 3):
            for w in range(nw):
                chip_copy(w, rel).wait_send()
                pass_on(w, rel, c).wait_send()
        for cp in own:
            cp.wait()

    hbm = pl.BlockSpec(memory_space=pl.ANY)
    return pl.pallas_call(
        body, name="gather_weights",
        in_specs=[hbm] * nw, out_specs=[hbm] * nw,
        out_shape=[jax.ShapeDtypeStruct((N_SHARD,) + s.shape, s.dtype) for s in shards],
        scratch_shapes=[pltpu.SemaphoreType.DMA((3 * nw,)), pltpu.SemaphoreType.DMA((3 * nw,)),
                        pltpu.SemaphoreType.DMA((3 * nw,)), pltpu.SemaphoreType.DMA((3 * nw,)),
                        pltpu.SemaphoreType.DMA((nw,)), pltpu.SemaphoreType.DMA((nw,))],
    )(*shards)


def _exchange_halves(parts):
    nw = len(parts)

    def body(*refs):
        ins, outs = refs[:nw], refs[nw:2 * nw]
        send, recv = refs[2 * nw:]
        x, y, c = _coords()
        copies = []
        for w in range(nw):
            h = parts[w].shape[1] // 2
            copies.append(pltpu.make_async_remote_copy(
                src_ref=ins[w].at[:, pl.ds((1 - c) * h, h), :], dst_ref=outs[w],
                send_sem=send.at[w], recv_sem=recv.at[w], device_id=(x, y, 1 - c), device_id_type=MESH))
        for cp in copies:
            cp.start()
        for cp in copies:
            cp.wait()

    hbm = pl.BlockSpec(memory_space=pl.ANY)
    return pl.pallas_call(
        body, name="exchange_halves",
        in_specs=[hbm] * nw, out_specs=[hbm] * nw,
        out_shape=[jax.ShapeDtypeStruct((N_SHARD, p.shape[1] // 2, p.shape[2]), F32) for p in parts],
        scratch_shapes=[pltpu.SemaphoreType.DMA((nw,)), pltpu.SemaphoreType.DMA((nw,))],
    )(*parts)


def _add_halves(parts, theirs, c_arr):
    nw = len(parts)
    split = 2

    def body(c_ref, *refs):
        ins, oth = refs[:nw], refs[nw:2 * nw]
        o32, o16 = refs[2 * nw:3 * nw], refs[3 * nw:]
        for w in range(nw):
            sm = ins[w][...] + oth[w][...]
            o32[w][...] = sm
            o16[w][...] = sm.astype(BF16)

    in_specs, oth_specs, out_specs, shapes32, shapes16 = [], [], [], [], []
    for p in parts:
        hb = p.shape[1] // 2 // split
        in_specs.append(pl.BlockSpec((None, hb, p.shape[2]), lambda j, i, c_ref: (j, c_ref[0] * split + i, 0)))
        oth_specs.append(pl.BlockSpec((None, hb, p.shape[2]), lambda j, i, c_ref: (j, i, 0)))
        shapes32.append(jax.ShapeDtypeStruct((N_SHARD, p.shape[1] // 2, p.shape[2]), F32))
        shapes16.append(jax.ShapeDtypeStruct((N_SHARD, p.shape[1] // 2, p.shape[2]), BF16))
    return pl.pallas_call(
        body, name="add_halves",
        grid_spec=pltpu.PrefetchScalarGridSpec(
            num_scalar_prefetch=1, grid=(N_SHARD, split),
            in_specs=in_specs + oth_specs, out_specs=oth_specs + oth_specs),
        out_shape=shapes32 + shapes16,
        compiler_params=_params("parallel", "parallel", vmem=VMEM_LIMIT),
    )(c_arr, *parts, *theirs)


def _exchange_chips(sums16):
    nw = len(sums16)

    def body(*refs):
        ins, outs = refs[:nw], refs[nw:2 * nw]
        send, recv = refs[2 * nw:]
        x, y, c = _coords()
        copies = []
        for rel in (1, 2, 3):
            kx, ky = _chip_of(x, y, rel)
            for w in range(nw):
                copies.append(pltpu.make_async_remote_copy(
                    src_ref=ins[w].at[2 * kx + ky], dst_ref=outs[w].at[rel - 1],
                    send_sem=send.at[w * 3 + rel - 1], recv_sem=recv.at[w * 3 + rel - 1],
                    device_id=(kx, ky, c), device_id_type=MESH))
        for cp in copies:
            cp.start()
        for cp in copies:
            cp.wait()

    hbm = pl.BlockSpec(memory_space=pl.ANY)
    return pl.pallas_call(
        body, name="exchange_chips",
        in_specs=[hbm] * nw, out_specs=[hbm] * nw,
        out_shape=[jax.ShapeDtypeStruct((3,) + s.shape[1:], BF16) for s in sums16],
        scratch_shapes=[pltpu.SemaphoreType.DMA((3 * nw,)), pltpu.SemaphoreType.DMA((3 * nw,))],
    )(*sums16)


def _add_chips(sums32, theirs, pos):
    nw = len(sums32)
    split = 2

    def body(pos_ref, *refs):
        ins, oth, outs = refs[:nw], refs[nw:2 * nw], refs[2 * nw:]
        for w in range(nw):
            acc = ins[w][...]
            for r in range(3):
                acc = acc + oth[w][r].astype(F32)
            outs[w][...] = acc

    in_specs, oth_specs, out_specs, shapes = [], [], [], []
    for s in sums32:
        hb = s.shape[1] // split
        in_specs.append(pl.BlockSpec((None, hb, s.shape[2]), lambda i, pos_ref: (pos_ref[0], i, 0)))
        oth_specs.append(pl.BlockSpec((3, hb, s.shape[2]), lambda i, pos_ref: (0, i, 0)))
        out_specs.append(pl.BlockSpec((hb, s.shape[2]), lambda i, pos_ref: (pos_ref[1] * split + i, 0)))
        shapes.append(jax.ShapeDtypeStruct((2 * s.shape[1], s.shape[2]), F32))
    return pl.pallas_call(
        body, name="add_chips",
        grid_spec=pltpu.PrefetchScalarGridSpec(
            num_scalar_prefetch=1, grid=(split,), in_specs=in_specs + oth_specs, out_specs=out_specs),
        out_shape=shapes,
        compiler_params=_params("parallel", vmem=VMEM_LIMIT),
    )(pos, *sums32, *theirs)


def _join_halves(shards):
    nw = len(shards)

    def body(*refs):
        outs = refs[nw:2 * nw]
        send, recv = refs[2 * nw:]
        x, y, c = _coords()

        def copy(w, cc):
            h = shards[w].shape[0] // 2
            rows = outs[w].at[pl.ds(cc * h, h), :]
            return pltpu.make_async_remote_copy(
                src_ref=rows, dst_ref=rows, send_sem=send.at[w], recv_sem=recv.at[w],
                device_id=(x, y, 1 - c), device_id_type=MESH)

        for w in range(nw):
            copy(w, c).start()
        for w in range(nw):
            copy(w, 1 - c).wait_recv()
            copy(w, c).wait_send()

    hbm = pl.BlockSpec(memory_space=pl.ANY)
    return pl.pallas_call(
        body, name="join_halves",
        in_specs=[hbm] * nw, out_specs=[hbm] * nw,
        out_shape=[jax.ShapeDtypeStruct(s.shape, F32) for s in shards],
        input_output_aliases={w: w for w in range(nw)},
        scratch_shapes=[pltpu.SemaphoreType.DMA((nw,)), pltpu.SemaphoreType.DMA((nw,))],
    )(*shards)


def _adamw_math(w, g, m, v):
    m = ADAM_B1 * m + (1.0 - ADAM_B1) * g
    v = ADAM_B2 * v + (1.0 - ADAM_B2) * (g * g)
    m_hat = m / (1.0 - ADAM_B1 ** ADAM_STEP)
    v_hat = v / (1.0 - ADAM_B2 ** ADAM_STEP)
    delta = -ADAM_LR * (m_hat / (jnp.sqrt(v_hat) + ADAM_EPS) + ADAM_WD * w)
    return delta, m, v


def _adamw(ws, gs, ms, vs):
    nw = len(ws)
    split = 8

    def body(*refs):
        w_r, g_r, m_r, v_r = (refs[i * nw:(i + 1) * nw] for i in range(4))
        d_o, m_o, v_o = (refs[(4 + i) * nw:(5 + i) * nw] for i in range(3))
        for k in range(nw):
            d, m, v = _adamw_math(w_r[k][...], g_r[k][...], m_r[k][...], v_r[k][...])
            d_o[k][...] = d
            m_o[k][...] = m
            v_o[k][...] = v

    specs = [pl.BlockSpec((w.shape[0] // split, w.shape[1]), lambda i: (i, 0)) for w in ws]
    shapes = [jax.ShapeDtypeStruct(w.shape, F32) for w in ws]
    outs = pl.pallas_call(
        body, name="adamw", grid=(split,),
        in_specs=specs * 4, out_specs=specs * 3, out_shape=shapes * 3,
        compiler_params=_params("parallel", vmem=VMEM_LIMIT),
    )(*ws, *gs, *ms, *vs)
    return outs[:nw], outs[nw:2 * nw], outs[2 * nw:]


SMALL_ROWS = 8
SMALL_COLS = D_MODEL
LOSS_COL = RET_WIDTH + 24


def _small_allreduce_adamw(part, w, m, v):
    def body(part_ref, w_ref, m_ref, v_ref, g_out, d_out, m_out, v_out, all_ref, send, recv):
        x, y, c = _coords()
        me = 4 * x + 2 * y + c
        all_ref[me] = part_ref[...]
        copies = []
        for rel in range(1, 8):
            px = 1 - x if rel & 4 else x
            py = 1 - y if rel & 2 else y
            pc = 1 - c if rel & 1 else c
            copies.append(pltpu.make_async_remote_copy(
                src_ref=part_ref, dst_ref=all_ref.at[me],
                send_sem=send.at[rel - 1], recv_sem=recv.at[rel - 1], device_id=(px, py, pc), device_id_type=MESH))
        for cp in copies:
            cp.start()
        for cp in copies:
            cp.wait()
        g = all_ref[0]
        for k in range(1, 8):
            g = g + all_ref[k]
        d, mn, vn = _adamw_math(w_ref[...], g, m_ref[...], v_ref[...])
        g_out[...] = g
        d_out[...] = d
        m_out[...] = mn
        v_out[...] = vn

    vm = pl.BlockSpec(memory_space=pltpu.VMEM)
    shape = jax.ShapeDtypeStruct((SMALL_ROWS, SMALL_COLS), F32)
    return pl.pallas_call(
        body, name="small_allreduce_adamw",
        in_specs=[vm] * 4, out_specs=[vm] * 4, out_shape=[shape] * 4,
        scratch_shapes=[pltpu.VMEM((8, SMALL_ROWS, SMALL_COLS), F32),
                        pltpu.SemaphoreType.DMA((7,)), pltpu.SemaphoreType.DMA((7,))],
    )(part, w, m, v)


SMALL_NAMES = ("ret_decay_fwd", "ret_decay_bwd", "attn_sink", "ret_gn_gain",
               "ln1_gain", "ln1_bias", "ln2_gain", "ln2_bias")


LN_NAMES = ("ln1_gain", "ln1_bias", "ln2_gain", "ln2_bias")


def _pack_small(vals, extra=None):
    tail = jnp.zeros((1, 1), F32) if extra is None else extra.reshape(1, 1)
    row4 = jnp.concatenate([vals["ret_gn_gain"], vals["ret_decay_fwd"], vals["ret_decay_bwd"], vals["attn_sink"],
                            tail, jnp.zeros((1, SMALL_COLS - LOSS_COL - 1), F32)], axis=1)
    rows = [vals[n] for n in LN_NAMES] + [row4, jnp.zeros((SMALL_ROWS - 5, SMALL_COLS), F32)]
    return jnp.concatenate(rows, axis=0)


def _unpack_small(packed):
    out = {n: packed[i:i + 1] for i, n in enumerate(LN_NAMES)}
    o = RET_WIDTH
    out.update(ret_gn_gain=packed[4:5, 0:o], ret_decay_fwd=packed[4:5, o:o + 8],
               ret_decay_bwd=packed[4:5, o + 8:o + 16], attn_sink=packed[4:5, o + 16:o + 24])
    return out


def _local_step(x, p, tgt, wts, small):
    bsz, s, _ = x.shape
    t = bsz * s
    x2d = x.reshape(t, D_MODEL)
    p2d = p.reshape(t, PLE_DIM)
    tgt2d = tgt.reshape(t, D_MODEL)
    dec_f = small["ret_decay_fwd"].reshape(8)
    dec_b = small["ret_decay_bwd"].reshape(8)
    lg_f = jnp.log1p(-jnp.exp2(dec_f))
    lg_b = jnp.log1p(-jnp.exp2(dec_b))
    per_lane = lambda v: jnp.repeat(v, HEAD_DIM).reshape(4, 1, LANES)
    lgf_l, lgb_l = per_lane(lg_f), per_lane(lg_b)
    sink = small["attn_sink"].reshape(8)
    slopes = 2.0 ** (-(jnp.arange(8, dtype=F32) + 1.0))
    gn_gain = small["ret_gn_gain"]
    g1, b1, g2, b2 = (small[n] for n in ("ln1_gain", "ln1_bias", "ln2_gain", "ln2_bias"))

    u = _inproj(x2d, wts["w_in_t"])
    u3 = u.reshape(bsz, s, IN_WIDTH)
    y_pre, y_ret = _ret_fwd(u3, lgf_l, lgb_l, gn_gain)
    y_att = _attn_fwd(u3, slopes, sink)
    zh1, r1, hb = _outproj_ln1(y_ret.reshape(t, RET_WIDTH), y_att.reshape(t, ATTN_WIDTH), x2d, wts["w_out"], g1, b1)
    dz2, gs, us, pg, ple, sq, dg2, db2 = _ffn_fwd(zh1, hb, p2d, tgt2d, g1, b1, g2, b2, wts["gate4"], wts["up4"],
                                                 wts["down4"], wts["ple_proj"], wts["ple_gate"])
    dgs, dus, dsp, dple, dz1, dyr, dya, dg1, db1 = _ffn_bwd(dz2, gs, us, pg, ple, zh1, r1, g1, wts["gate4"],
                                                          wts["up4"], wts["down4"], wts["ple_gate"], wts["w_out"])
    d_gate4, d_up4, d_down4 = _wgrad_ffn(gs, us, dgs, dus, hb, dz2)
    d_w_out, d_ple_gate, d_ple_proj = _wgrad_misc(y_ret.reshape(t, RET_WIDTH), y_att.reshape(t, ATTN_WIDTH), dz1,
                                                  hb, dsp, p2d, dple)
    drq, drk, drv, drg, rpart = _ret_bwd(u3, y_pre, dyr.reshape(bsz, s, RET_WIDTH), lgf_l, lgb_l, gn_gain)
    daq, dakv, spart = _attn_bwd(u3, dya.reshape(bsz, s, ATTN_WIDTH), slopes, sink)
    pieces = [a.reshape(t, -1) for a in (drq, drk, drv, drg, daq, dakv)]
    w_in_t = wts["w_in_t"]
    kv0 = CB_AK * LANES
    kv_order = (0, 128, 64, 192)
    w_kv = jnp.concatenate([w_in_t[kv0 + o:kv0 + o + HEAD_DIM] for o in kv_order], axis=0)
    grad_x = _inproj_bwd(dz1, pieces, w_in_t[:kv0], w_kv).reshape(bsz, s, D_MODEL)
    d_in = _wgrad_in(pieces, x2d)
    d_w_in_t = jnp.concatenate([d_in[:kv0]] + [d_in[kv0 + o:kv0 + o + HEAD_DIM] for o in kv_order], axis=0)

    rsum = jnp.sum(rpart, axis=0)
    lane_heads = lambda row: jnp.sum(row.reshape(4, 2, HEAD_DIM), axis=-1).reshape(8)
    dlg_f = lane_heads(rsum[:, 0, :]) + jnp.stack([jnp.sum(rsum[:, 2, :], -1), jnp.sum(rsum[:, 3, :], -1)], 1).reshape(8)
    dlg_b = lane_heads(rsum[:, 1, :]) + jnp.stack([jnp.sum(rsum[:, 4, :], -1), jnp.sum(rsum[:, 5, :], -1)], 1).reshape(8)
    chain = lambda d: -(math.log(2.0) * jnp.exp2(d)) / (1.0 - jnp.exp2(d))
    grads_small = {
        "ret_decay_fwd": (dlg_f * chain(dec_f)).reshape(1, 8),
        "ret_decay_bwd": (dlg_b * chain(dec_b)).reshape(1, 8),
        "attn_sink": jnp.sum(spart, axis=0)[:, 0:4, 0].reshape(1, 8),
        "ret_gn_gain": rsum[:, 6, :].reshape(1, RET_WIDTH),
        "ln1_gain": dg1, "ln1_bias": db1, "ln2_gain": dg2, "ln2_bias": db2,
    }
    grads_big = {
        "w_in": d_w_in_t.reshape(N_SHARD, FFN_SHARD, D_MODEL),
        "w_out": d_w_out.reshape(N_SHARD, D_MODEL // N_SHARD, D_MODEL),
        "w_ffn_gate": d_gate4, "w_ffn_up": d_up4, "w_ffn_down": d_down4,
        "w_ple_proj": d_ple_proj.reshape(PLE_DIM, N_SHARD, D_MODEL // N_SHARD).transpose(1, 0, 2),
        "w_ple_gate": d_ple_gate.reshape(N_SHARD, D_MODEL // N_SHARD, D_MODEL),
    }
    return sq[0, 0], grad_x, grads_big, grads_small


BIG_NAMES = ("w_in", "w_out", "w_ffn_gate", "w_ffn_up", "w_ffn_down", "w_ple_proj", "w_ple_gate")
TRANSPOSED = ("w_in", "w_ffn_gate", "w_ffn_up")
WEIGHT_ORDER = ("w_in", "ret_decay_fwd", "ret_decay_bwd", "ret_gn_gain", "attn_sink", "w_out", "ln1_gain",
                "ln1_bias", "w_ffn_gate", "w_ffn_up", "w_ffn_down", "w_ple_proj", "w_ple_gate", "ln2_gain", "ln2_bias")


def _shard_rows(name, a):
    return jnp.swapaxes(a[0], 0, 1) if name in TRANSPOSED else a[0]


def _unshard_rows(name, a):
    return (jnp.swapaxes(a, 0, 1) if name in TRANSPOSED else a)[None]


def _assemble_weights(gathered):
    cols = lambda a: a.transpose(1, 0, 2).reshape(a.shape[1], N_SHARD * a.shape[2])
    rows = lambda a: a.reshape(N_SHARD * a.shape[1], a.shape[2])
    return {"w_in_t": rows(gathered["w_in"]), "w_out": rows(gathered["w_out"]),
            "gate4": gathered["w_ffn_gate"], "up4": gathered["w_ffn_up"], "down4": gathered["w_ffn_down"],
            "ple_proj": cols(gathered["w_ple_proj"]), "ple_gate": rows(gathered["w_ple_gate"])}


def kernel(x, p, w_in, ret_decay_fwd, ret_decay_bwd, ret_gn_gain, attn_sink, w_out, ln1_gain, ln1_bias, w_ffn_gate, w_ffn_up, w_ffn_down, w_ple_proj, w_ple_gate, ln2_gain, ln2_bias, loss_target, m_w_in, m_ret_decay_fwd, m_ret_decay_bwd, m_ret_gn_gain, m_attn_sink, m_w_out, m_ln1_gain, m_ln1_bias, m_w_ffn_gate, m_w_ffn_up, m_w_ffn_down, m_w_ple_proj, m_w_ple_gate, m_ln2_gain, m_ln2_bias, v_w_in, v_ret_decay_fwd, v_ret_decay_bwd, v_ret_gn_gain, v_attn_sink, v_w_out, v_ln1_gain, v_ln1_bias, v_w_ffn_gate, v_w_ffn_up, v_w_ffn_down, v_w_ple_proj, v_w_ple_gate, v_ln2_gain, v_ln2_bias):
    w = dict(w_in=w_in, ret_decay_fwd=ret_decay_fwd, ret_decay_bwd=ret_decay_bwd, ret_gn_gain=ret_gn_gain,
             attn_sink=attn_sink, w_out=w_out, ln1_gain=ln1_gain, ln1_bias=ln1_bias, w_ffn_gate=w_ffn_gate,
             w_ffn_up=w_ffn_up, w_ffn_down=w_ffn_down, w_ple_proj=w_ple_proj, w_ple_gate=w_ple_gate,
             ln2_gain=ln2_gain, ln2_bias=ln2_bias)
    m = dict(w_in=m_w_in, ret_decay_fwd=m_ret_decay_fwd, ret_decay_bwd=m_ret_decay_bwd, ret_gn_gain=m_ret_gn_gain,
             attn_sink=m_attn_sink, w_out=m_w_out, ln1_gain=m_ln1_gain, ln1_bias=m_ln1_bias, w_ffn_gate=m_w_ffn_gate,
             w_ffn_up=m_w_ffn_up, w_ffn_down=m_w_ffn_down, w_ple_proj=m_w_ple_proj, w_ple_gate=m_w_ple_gate,
             ln2_gain=m_ln2_gain, ln2_bias=m_ln2_bias)
    v = dict(w_in=v_w_in, ret_decay_fwd=v_ret_decay_fwd, ret_decay_bwd=v_ret_decay_bwd, ret_gn_gain=v_ret_gn_gain,
             attn_sink=v_attn_sink, w_out=v_w_out, ln1_gain=v_ln1_gain, ln1_bias=v_ln1_bias, w_ffn_gate=v_w_ffn_gate,
             w_ffn_up=v_w_ffn_up, w_ffn_down=v_w_ffn_down, w_ple_proj=v_w_ple_proj, w_ple_gate=v_w_ple_gate,
             ln2_gain=v_ln2_gain, ln2_bias=v_ln2_bias)
    big = lambda d: [_shard_rows(n, d[n]) for n in BIG_NAMES]
    small = lambda d: {n: d[n] for n in SMALL_NAMES}

    gathered = _all_gather_weights([a.astype(BF16) for a in big(w)])
    wts = _assemble_weights(dict(zip(BIG_NAMES, gathered)))
    sq, grad_x, grads_big, grads_small = _local_step(x, p[0], loss_target, wts, small(w))

    chip = 2 * lax.axis_index("x") + lax.axis_index("y")
    core = lax.axis_index("c")
    c_arr = core.astype(jnp.int32).reshape(1)
    pos = jnp.stack([chip, core]).astype(jnp.int32)
    parts = [grads_big[n] for n in BIG_NAMES]
    theirs = _exchange_halves(parts)
    sums = _add_halves(parts, theirs, c_arr)
    sums32, sums16 = sums[:len(parts)], sums[len(parts):]
    from_chips = _exchange_chips(sums16)
    g_big = _join_halves(_add_chips(sums32, from_chips, pos))
    d_big, m_big, v_big = _adamw(big(w), g_big, big(m), big(v))

    g_s, d_s, m_s, v_s = _small_allreduce_adamw(_pack_small(grads_small, sq), _pack_small(small(w)),
                                                _pack_small(small(m)), _pack_small(small(v)))
    loss = g_s[4, LOSS_COL] * (0.5 / D_MODEL)

    def tree(bigs, packed):
        out = {n: _unshard_rows(n, a) for n, a in zip(BIG_NAMES, bigs)}
        out.update(_unpack_small(packed))
        return [out[n] for n in WEIGHT_ORDER]

    return (loss, grad_x, *tree(g_big, g_s), *tree(d_big, d_s), *tree(m_big, m_s), *tree(v_big, v_s))
```

```python
import functools
import math

import jax
import jax.numpy as jnp
from jax import lax
from jax.experimental import pallas as pl
from jax.experimental.pallas import tpu as pltpu

F32 = jnp.float32
BF16 = jnp.bfloat16

D_MODEL = 1024
HEAD_DIM = 64
RET_HEADS = 8
ATTN_HEADS = 8
RET_WIDTH = 512
ATTN_WIDTH = 512
KV_WIDTH = 128
IN_WIDTH = 2816
FFN = 2816
N_SHARD = 4
FFN_SHARD = FFN // N_SHARD
PLE_DIM = 256
CHUNK = 128
LANES = 128
ALPHA = 2.0 ** 0.25
LN_EPS = 1e-5
GN_EPS = 1e-5
NEG_INF = -1e30
ADAM_LR = 0.001
ADAM_B1 = 0.9
ADAM_B2 = 0.999
ADAM_EPS = 1e-08
ADAM_WD = 0.01
ADAM_STEP = 10
VMEM_LIMIT = 56 * 1024 * 1024
MESH = pl.DeviceIdType.MESH

CB_RQ, CB_RK, CB_RV, CB_RG, CB_AQ, CB_AK, CB_AV = 0, 4, 8, 12, 16, 20, 21


def _dot(a, b):
    return jnp.dot(a, b, preferred_element_type=F32)


def _dot_nt(a, b):
    return lax.dot_general(a, b, (((1,), (1,)), ((), ())), preferred_element_type=F32)


def _dot_tn(a, b):
    return lax.dot_general(a, b, (((0,), (0,)), ((), ())), preferred_element_type=F32)


def _sigmoid(x):
    return 1.0 / (1.0 + jnp.exp(-x))


def _params(*sem, vmem=None):
    return pltpu.CompilerParams(dimension_semantics=tuple(sem) if sem else None, vmem_limit_bytes=vmem)


class _Rider:
    def __init__(self, ins, out_shapes, sems, start, finish, aliases=None):
        self.ins, self.out_shapes, self.sems = list(ins), list(out_shapes), list(sems)
        self.start, self.finish, self.aliases = start, finish, dict(aliases or {})


def _hosted_call(body, name, grid, in_specs, out_specs, out_shape, scratch_shapes, operands, rider=None):
    n_in, n_out, n_scr = len(in_specs), len(out_specs), len(scratch_shapes)
    if rider is None:
        return pl.pallas_call(
            body, name=name, grid=grid, in_specs=in_specs, out_specs=out_specs, out_shape=out_shape,
            scratch_shapes=scratch_shapes,
            compiler_params=_params(*(["parallel"] * len(grid)), vmem=VMEM_LIMIT))(*operands)
    r_in, r_out = len(rider.ins), len(rider.out_shapes)

    def full_body(*refs):
        main_in, rin = refs[:n_in], refs[n_in:n_in + r_in]
        o0 = n_in + r_in
        main_out, rout = refs[o0:o0 + n_out], refs[o0 + n_out:o0 + n_out + r_out]
        s0 = o0 + n_out + r_out
        main_scr, rsem = refs[s0:s0 + n_scr], refs[s0 + n_scr:]
        first = functools.reduce(jnp.logical_and, [pl.program_id(a) == 0 for a in range(len(grid))])
        last = functools.reduce(jnp.logical_and, [pl.program_id(a) == g - 1 for a, g in enumerate(grid)])

        @pl.when(first)
        def _():
            rider.start(rin, rout, rsem)

        body(*main_in, *main_out, *main_scr)

        @pl.when(last)
        def _():
            rider.finish(rin, rout, rsem)

    hbm = pl.BlockSpec(memory_space=pl.ANY)
    return pl.pallas_call(
        full_body, name=name, grid=grid,
        in_specs=list(in_specs) + [hbm] * r_in, out_specs=list(out_specs) + [hbm] * r_out,
        out_shape=list(out_shape) + rider.out_shapes,
        scratch_shapes=list(scratch_shapes) + rider.sems,
        input_output_aliases={n_in + i: n_out + o for i, o in rider.aliases.items()},
        compiler_params=_params(*(["arbitrary"] * len(grid)), vmem=VMEM_LIMIT),
    )(*operands, *rider.ins)


def _head_mean(x, m0):
    s0 = jnp.sum(jnp.where(m0, x, 0.0), axis=1, keepdims=True)
    s1 = jnp.sum(jnp.where(m0, 0.0, x), axis=1, keepdims=True)
    return jnp.where(m0, s0, s1) * (1.0 / HEAD_DIM)


def _inproj(x2d, w_in_t):
    t = x2d.shape[0]
    tm = 512
    nb = 256

    def body(x_ref, w_ref, o_ref):
        xb = x_ref[...].astype(BF16)
        for n in range(0, IN_WIDTH, nb):
            o_ref[:, n:n + nb] = _dot_nt(xb, w_ref[n:n + nb, :]).astype(BF16)

    return pl.pallas_call(
        body, name="inproj", grid=(t // tm,),
        in_specs=[pl.BlockSpec((tm, D_MODEL), lambda i: (i, 0)),
                  pl.BlockSpec((IN_WIDTH, D_MODEL), lambda i: (0, 0))],
        out_specs=pl.BlockSpec((tm, IN_WIDTH), lambda i: (i, 0)),
        out_shape=jax.ShapeDtypeStruct((t, IN_WIDTH), BF16),
        compiler_params=_params("parallel", vmem=VMEM_LIMIT),
    )(x2d, w_in_t)


def _outproj_ln1(y_ret, y_att, x2d, w_out, gain, bias):
    t = x2d.shape[0]
    tm = 512

    def body(yr_ref, ya_ref, x_ref, w_ref, g_ref, b_ref, zh_ref, r_ref, hb_ref):
        mix = _dot(yr_ref[...], w_ref[0:RET_WIDTH, :]) + _dot(ya_ref[...], w_ref[RET_WIDTH:, :])
        z = ALPHA * x_ref[...] + mix
        mu = jnp.mean(z, axis=1, keepdims=True)
        zc = z - mu
        var = jnp.mean(zc * zc, axis=1, keepdims=True)
        r = lax.rsqrt(var + LN_EPS)
        zh = zc * r
        zh_ref[...] = zh
        r_ref[...] = r
        hb_ref[...] = (zh * g_ref[...] + b_ref[...]).astype(BF16)

    row = lambda w: pl.BlockSpec((tm, w), lambda i: (i, 0))
    const = lambda s: pl.BlockSpec(s, lambda i: (0, 0))
    return pl.pallas_call(
        body, name="outproj_ln1", grid=(t // tm,),
        in_specs=[row(RET_WIDTH), row(ATTN_WIDTH), row(D_MODEL), const((D_MODEL, D_MODEL)),
                  const((1, D_MODEL)), const((1, D_MODEL))],
        out_specs=[row(D_MODEL), row(1), row(D_MODEL)],
        out_shape=[jax.ShapeDtypeStruct((t, D_MODEL), F32), jax.ShapeDtypeStruct((t, 1), F32),
                   jax.ShapeDtypeStruct((t, D_MODEL), BF16)],
        compiler_params=_params("parallel", vmem=VMEM_LIMIT),
    )(y_ret, y_att, x2d, w_out, gain, bias)


def _load_resident(step, pairs):
    @pl.when(step == 0)
    def _():
        for src, dst in pairs:
            pltpu.sync_copy(src, dst)


def _ffn_fwd(zh1, hb, p2d, tgt, g1, b1, g2, b2, wg4, wu4, wd4, wpe, wpg):
    t = zh1.shape[0]
    tm = 256

    def body(zh_ref, hb_ref, p_ref, t_ref, g1_ref, b1_ref, g2_ref, b2_ref,
             wg_hbm, wu_hbm, wd_hbm, wpe_hbm, wpg_hbm,
             dz_ref, gs_ref, us_ref, pg_ref, ple_ref, loss_ref, dg2_ref, db2_ref,
             wg, wu, wd, wpe, wpg):
        step = pl.program_id(0)
        _load_resident(step, [(wg_hbm, wg), (wu_hbm, wu), (wd_hbm, wd), (wpe_hbm, wpe), (wpg_hbm, wpg)])

        @pl.when(step == 0)
        def _():
            loss_ref[...] = jnp.zeros_like(loss_ref)
            dg2_ref[...] = jnp.zeros_like(dg2_ref)
            db2_ref[...] = jnp.zeros_like(db2_ref)

        h1 = zh_ref[...] * g1_ref[...] + b1_ref[...]
        hbv = hb_ref[...]
        ffn = jnp.zeros((tm, D_MODEL), F32)
        for j in range(N_SHARD):
            gj = _dot_nt(hbv, wg[j])
            uj = _dot_nt(hbv, wu[j])
            gs_ref[j] = gj.astype(BF16)
            us_ref[j] = uj.astype(BF16)
            act = (gj * _sigmoid(gj) * uj).astype(BF16)
            ffn = ffn + _dot(act, wd[j])
        ple = _dot(p_ref[...].astype(BF16), wpe[...])
        pg = _sigmoid(_dot(hbv, wpg[...]))
        pg_ref[...] = pg.astype(BF16)
        ple_ref[...] = ple.astype(BF16)
        z2 = ALPHA * h1 + ffn + pg * ple
        mu = jnp.mean(z2, axis=1, keepdims=True)
        zc = z2 - mu
        var = jnp.mean(zc * zc, axis=1, keepdims=True)
        r = lax.rsqrt(var + LN_EPS)
        zh2 = zc * r
        err = zh2 * g2_ref[...] + b2_ref[...] - t_ref[...]
        loss_ref[...] += jnp.sum(err * err)
        dy = err * (1.0 / D_MODEL)
        dg2_ref[...] += jnp.sum(dy * zh2, axis=0, keepdims=True)
        db2_ref[...] += jnp.sum(dy, axis=0, keepdims=True)
        dzh = dy * g2_ref[...]
        m1 = jnp.mean(dzh, axis=1, keepdims=True)
        m2 = jnp.mean(dzh * zh2, axis=1, keepdims=True)
        dz_ref[...] = r * (dzh - m1 - zh2 * m2)

    row = lambda w: pl.BlockSpec((tm, w), lambda i: (i, 0))
    const = lambda s: pl.BlockSpec(s, lambda i: (0, 0))
    sh = pl.BlockSpec((N_SHARD, tm, FFN_SHARD), lambda i: (0, i, 0))
    hbm = pl.BlockSpec(memory_space=pl.ANY)
    return pl.pallas_call(
        body, name="ffn_fwd", grid=(t // tm,),
        in_specs=[row(D_MODEL), row(D_MODEL), row(PLE_DIM), row(D_MODEL),
                  const((1, D_MODEL)), const((1, D_MODEL)), const((1, D_MODEL)), const((1, D_MODEL)),
                  hbm, hbm, hbm, hbm, hbm],
        out_specs=[row(D_MODEL), sh, sh, row(D_MODEL), row(D_MODEL),
                   const((8, LANES)), const((1, D_MODEL)), const((1, D_MODEL))],
        out_shape=[jax.ShapeDtypeStruct((t, D_MODEL), F32),
                   jax.ShapeDtypeStruct((N_SHARD, t, FFN_SHARD), BF16),
                   jax.ShapeDtypeStruct((N_SHARD, t, FFN_SHARD), BF16),
                   jax.ShapeDtypeStruct((t, D_MODEL), BF16), jax.ShapeDtypeStruct((t, D_MODEL), BF16),
                   jax.ShapeDtypeStruct((8, LANES), F32),
                   jax.ShapeDtypeStruct((1, D_MODEL), F32), jax.ShapeDtypeStruct((1, D_MODEL), F32)],
        scratch_shapes=[pltpu.VMEM(wg4.shape, BF16), pltpu.VMEM(wu4.shape, BF16), pltpu.VMEM(wd4.shape, BF16),
                        pltpu.VMEM(wpe.shape, BF16), pltpu.VMEM(wpg.shape, BF16)],
        compiler_params=_params("arbitrary", vmem=VMEM_LIMIT),
    )(zh1, hb, p2d, tgt, g1, b1, g2, b2, wg4, wu4, wd4, wpe, wpg)


def _ret_tables(lgf, lgb):
    c = CHUNK
    row = lax.broadcasted_iota(jnp.int32, (c, LANES), 0).astype(F32)
    ii = lax.broadcasted_iota(jnp.int32, (c, c), 0).astype(F32)
    jj = lax.broadcasted_iota(jnp.int32, (c, c), 1).astype(F32)
    diff = ii - jj
    dmats = []
    for h in range(2):
        lf = lgf[:, h * HEAD_DIM:h * HEAD_DIM + 1]
        lb = lgb[:, h * HEAD_DIM:h * HEAD_DIM + 1]
        dmats.append(jnp.where(diff > 0, jnp.exp(lf * jnp.maximum(diff, 0.0)),
                               jnp.where(diff < 0, jnp.exp(lb * jnp.maximum(-diff, 0.0)), 2.0)))
    tab = dict(
        qdec_f=jnp.exp(lgf * (row + 1.0)), kdec_f=jnp.exp(lgf * (c - 1.0 - row)),
        qdec_b=jnp.exp(lgb * (c - row)), kdec_b=jnp.exp(lgb * row),
        cdec_f=jnp.exp(lgf * c), cdec_b=jnp.exp(lgb * c),
        d0=dmats[0], d1=dmats[1], row=row, diff=diff)
    r = lax.broadcasted_iota(jnp.int32, (LANES, LANES), 0) < HEAD_DIM
    cc = lax.broadcasted_iota(jnp.int32, (LANES, LANES), 1) < HEAD_DIM
    tab["bd"] = r == cc
    tab["m0"] = lax.broadcasted_iota(jnp.int32, (c, LANES), 1) < HEAD_DIM
    return tab


def _ret_specs(s):
    blk = lambda cb: pl.BlockSpec((None, s, LANES), lambda b, p, cb=cb: (b, 0, cb + p))
    lane = pl.BlockSpec((None, 1, LANES), lambda b, p: (p, 0, 0))
    gain = pl.BlockSpec((1, LANES), lambda b, p: (0, p))
    pair = pl.BlockSpec((None, s, LANES), lambda b, p: (b, 0, p))
    return blk, lane, gain, pair


def _ret_fwd(u3, lgf_l, lgb_l, gn_gain, rider=None):
    bsz, s, _ = u3.shape
    n_chunk = s // CHUNK
    c = CHUNK

    def body(q_ref, k_ref, v_ref, g_ref, lgf_ref, lgb_ref, gain_ref, y_ref, o_ref, rb_ref):
        tb = _ret_tables(lgf_ref[...], lgb_ref[...])
        m0, bd = tb["m0"], tb["bd"]
        gain = gain_ref[...]

        def back_states(i, rb):
            n = n_chunk - 1 - i
            sl = pl.ds(pl.multiple_of(n * c, c), c)
            rb_ref[n] = rb
            kb = (k_ref[sl, :].astype(F32) * tb["kdec_b"]).astype(BF16)
            return rb * tb["cdec_b"] + jnp.where(bd, _dot_tn(kb, v_ref[sl, :]), 0.0)

        lax.fori_loop(0, n_chunk, back_states, jnp.zeros((LANES, LANES), F32))

        def chunk(n, rf):
            sl = pl.ds(pl.multiple_of(n * c, c), c)
            q = q_ref[sl, :].astype(F32) * 0.125
            k = k_ref[sl, :]
            v = v_ref[sl, :]
            kf32 = k.astype(F32)
            vf32 = v.astype(F32)
            q0 = jnp.where(m0, q, 0.0).astype(BF16)
            q1 = jnp.where(m0, 0.0, q).astype(BF16)
            a0 = (_dot_nt(q0, k) * tb["d0"]).astype(BF16)
            a1 = (_dot_nt(q1, k) * tb["d1"]).astype(BF16)
            v0 = jnp.where(m0, vf32, 0.0).astype(BF16)
            v1 = jnp.where(m0, 0.0, vf32).astype(BF16)
            y = _dot(a0, v0) + _dot(a1, v1)
            y = y + _dot((q * tb["qdec_f"]).astype(BF16), rf.astype(BF16))
            y = y + _dot((q * tb["qdec_b"]).astype(BF16), rb_ref[n].astype(BF16))
            rf_new = rf * tb["cdec_f"] + jnp.where(bd, _dot_tn((kf32 * tb["kdec_f"]).astype(BF16), v), 0.0)
            mu = _head_mean(y, m0)
            yc = y - mu
            var = _head_mean(yc * yc, m0)
            yh = yc * lax.rsqrt(var + GN_EPS)
            g = g_ref[sl, :].astype(F32)
            y_ref[sl, :] = y
            o_ref[sl, :] = (yh * gain * (g * _sigmoid(g))).astype(BF16)
            return rf_new

        lax.fori_loop(0, n_chunk, chunk, jnp.zeros((LANES, LANES), F32))

    blk, lane, gain, pair = _ret_specs(s)
    return _hosted_call(
        body, "ret_fwd", (bsz, 4),
        in_specs=[blk(CB_RQ), blk(CB_RK), blk(CB_RV), blk(CB_RG), lane, lane, gain],
        out_specs=[pair, pair],
        out_shape=[jax.ShapeDtypeStruct((bsz, s, RET_WIDTH), F32), jax.ShapeDtypeStruct((bsz, s, RET_WIDTH), BF16)],
        scratch_shapes=[pltpu.VMEM((n_chunk, LANES, LANES), F32)],
        operands=(u3, u3, u3, u3, lgf_l, lgb_l, gn_gain), rider=rider)


def _ret_bwd(u3, y_pre, d_o, lgf_l, lgb_l, gn_gain, rider=None):
    bsz, s, _ = u3.shape
    n_chunk = s // CHUNK
    c = CHUNK

    def body(q_ref, k_ref, v_ref, g_ref, y_ref, do_ref, lgf_ref, lgb_ref, gain_ref,
             dq_ref, dk_ref, dv_ref, dg_ref, part_ref,
             rb_ref, rf_ref, dy_ref, dk_acc, dv_acc, af0, af1, ab0, ab1, vec_ref):
        tb = _ret_tables(lgf_ref[...], lgb_ref[...])
        m0, bd, row = tb["m0"], tb["bd"], tb["row"]
        gain = gain_ref[...]
        wf = jnp.maximum(tb["diff"], 0.0)
        wb = jnp.maximum(-tb["diff"], 0.0)
        zero_state = jnp.zeros((LANES, LANES), F32)
        for ref in (af0, af1, ab0, ab1):
            ref[...] = jnp.zeros_like(ref)
        vec_ref[...] = jnp.zeros_like(vec_ref)

        def back_states(i, rb):
            n = n_chunk - 1 - i
            sl = pl.ds(pl.multiple_of(n * c, c), c)
            rb_ref[n] = rb
            kb = (k_ref[sl, :].astype(F32) * tb["kdec_b"]).astype(BF16)
            return rb * tb["cdec_b"] + jnp.where(bd, _dot_tn(kb, v_ref[sl, :]), 0.0)

        lax.fori_loop(0, n_chunk, back_states, zero_state)

        def sweep_fwd(n, carry):
            rf, gb = carry
            sl = pl.ds(pl.multiple_of(n * c, c), c)
            rf_ref[n] = rf
            rbn = rb_ref[n]
            q = q_ref[sl, :].astype(F32) * 0.125
            k = k_ref[sl, :]
            v = v_ref[sl, :]
            kf32 = k.astype(F32)
            vf32 = v.astype(F32)
            y = y_ref[sl, :]
            do = do_ref[sl, :].astype(F32)
            g = g_ref[sl, :].astype(F32)
            mu = _head_mean(y, m0)
            yc = y - mu
            rstd = lax.rsqrt(_head_mean(yc * yc, m0) + GN_EPS)
            yh = yc * rstd
            sg = _sigmoid(g)
            sil = g * sg
            dyh = do * gain * sil
            dg_ref[sl, :] = (do * yh * gain * sg * (1.0 + g * (1.0 - sg))).astype(BF16)
            dgain = jnp.sum(do * yh * sil, axis=0, keepdims=True)
            dy = rstd * (dyh - _head_mean(dyh, m0) - yh * _head_mean(dyh * yh, m0))
            dyb = dy.astype(BF16)
            dy_ref[sl, :] = dyb
            dy0 = jnp.where(m0, dy, 0.0).astype(BF16)
            dy1 = jnp.where(m0, 0.0, dy).astype(BF16)
            q0 = jnp.where(m0, q, 0.0).astype(BF16)
            q1 = jnp.where(m0, 0.0, q).astype(BF16)
            k0 = jnp.where(m0, kf32, 0.0).astype(BF16)
            k1 = jnp.where(m0, 0.0, kf32).astype(BF16)
            a0 = _dot_nt(q0, k) * tb["d0"]
            a1 = _dot_nt(q1, k) * tb["d1"]
            da0 = _dot_nt(dy0, v)
            da1 = _dot_nt(dy1, v)
            pr0 = da0 * a0
            pr1 = da1 * a1
            af0[...] += pr0 * wf
            ab0[...] += pr0 * wb
            af1[...] += pr1 * wf
            ab1[...] += pr1 * wb
            ds0 = (da0 * tb["d0"]).astype(BF16)
            ds1 = (da1 * tb["d1"]).astype(BF16)
            dq = _dot(ds0, k0) + _dot(ds1, k1)
            dk = _dot_tn(ds0, q0) + _dot_tn(ds1, q1)
            dv = _dot_tn(a0.astype(BF16), dy0) + _dot_tn(a1.astype(BF16), dy1)
            qf = (q * tb["qdec_f"]).astype(BF16)
            qb = (q * tb["qdec_b"]).astype(BF16)
            rfb = rf.astype(BF16)
            rbb = rbn.astype(BF16)
            ycf = _dot(qf, rfb)
            ycb = _dot(qb, rbb)
            dq = dq + _dot_nt(dyb, rfb) * tb["qdec_f"] + _dot_nt(dyb, rbb) * tb["qdec_b"]
            dlf = jnp.sum((row + 1.0) * ycf * dy, axis=0, keepdims=True)
            dlb = jnp.sum((c - row) * ycb * dy, axis=0, keepdims=True)
            gbb = gb.astype(BF16)
            kbd = kf32 * tb["kdec_b"]
            dkb = _dot_nt(v, gbb)
            dk = dk + dkb * tb["kdec_b"]
            dv = dv + _dot(kbd.astype(BF16), gbb)
            dlb = dlb + jnp.sum(row * kbd * dkb, axis=0, keepdims=True)
            dlb = dlb + c * tb["cdec_b"] * jnp.sum(gb * rbn, axis=0, keepdims=True)
            gb_new = jnp.where(bd, _dot_tn(qb, dyb), 0.0) + tb["cdec_b"] * gb
            rf_new = rf * tb["cdec_f"] + jnp.where(bd, _dot_tn((kf32 * tb["kdec_f"]).astype(BF16), v), 0.0)
            dq_ref[sl, :] = (dq * 0.125).astype(BF16)
            dk_acc[sl, :] = dk
            dv_acc[sl, :] = dv
            vec_ref[0:1, :] += dlf
            vec_ref[1:2, :] += dlb
            vec_ref[6:7, :] += dgain
            return rf_new, gb_new

        lax.fori_loop(0, n_chunk, sweep_fwd, (zero_state, zero_state))

        def sweep_bwd(i, gf):
            n = n_chunk - 1 - i
            sl = pl.ds(pl.multiple_of(n * c, c), c)
            q = q_ref[sl, :].astype(F32) * 0.125
            kf32 = k_ref[sl, :].astype(F32)
            v = v_ref[sl, :]
            dyb = dy_ref[sl, :]
            gfb = gf.astype(BF16)
            kfd = kf32 * tb["kdec_f"]
            dkf = _dot_nt(v, gfb)
            dk_ref[sl, :] = (dk_acc[sl, :] + dkf * tb["kdec_f"]).astype(BF16)
            dv_ref[sl, :] = (dv_acc[sl, :] + _dot(kfd.astype(BF16), gfb)).astype(BF16)
            dlf = jnp.sum((c - 1.0 - row) * kfd * dkf, axis=0, keepdims=True)
            dlf = dlf + c * tb["cdec_f"] * jnp.sum(gf * rf_ref[n], axis=0, keepdims=True)
            vec_ref[0:1, :] += dlf
            qf = (q * tb["qdec_f"]).astype(BF16)
            return jnp.where(bd, _dot_tn(qf, dyb), 0.0) + tb["cdec_f"] * gf

        lax.fori_loop(0, n_chunk, sweep_bwd, zero_state)
        vec_ref[2:3, :] = jnp.sum(af0[...], axis=0, keepdims=True)
        vec_ref[3:4, :] = jnp.sum(af1[...], axis=0, keepdims=True)
        vec_ref[4:5, :] = jnp.sum(ab0[...], axis=0, keepdims=True)
        vec_ref[5:6, :] = jnp.sum(ab1[...], axis=0, keepdims=True)
        part_ref[...] = vec_ref[...]

    blk, lane, gain, pair = _ret_specs(s)
    out_bf = jax.ShapeDtypeStruct((bsz, s, RET_WIDTH), BF16)
    return _hosted_call(
        body, "ret_bwd", (bsz, 4),
        in_specs=[blk(CB_RQ), blk(CB_RK), blk(CB_RV), blk(CB_RG), pair, pair, lane, lane, gain],
        out_specs=[pair, pair, pair, pair, pl.BlockSpec((None, None, 8, LANES), lambda b, p: (b, p, 0, 0))],
        out_shape=[out_bf, out_bf, out_bf, out_bf, jax.ShapeDtypeStruct((bsz, 4, 8, LANES), F32)],
        scratch_shapes=[pltpu.VMEM((n_chunk, LANES, LANES), F32), pltpu.VMEM((n_chunk, LANES, LANES), F32),
                        pltpu.VMEM((s, LANES), BF16), pltpu.VMEM((s, LANES), F32), pltpu.VMEM((s, LANES), F32),
                        pltpu.VMEM((c, c), F32), pltpu.VMEM((c, c), F32), pltpu.VMEM((c, c), F32),
                        pltpu.VMEM((c, c), F32), pltpu.VMEM((8, LANES), F32)],
        operands=(u3, u3, u3, u3, y_pre, d_o, lgf_l, lgb_l, gn_gain), rider=rider)


def _attn_window_tables(n, s):
    qi = lax.broadcasted_iota(jnp.int32, (CHUNK, 3 * CHUNK), 0)
    kj = lax.broadcasted_iota(jnp.int32, (CHUNK, 3 * CHUNK), 1)
    dist = jnp.abs(kj - CHUNK - qi)
    kpos = n * CHUNK - CHUNK + kj
    valid = (dist <= CHUNK) & (kpos >= 0) & (kpos < s)
    return dist.astype(F32), valid


def _dup_kv_head(x, g):
    lane = lax.broadcasted_iota(jnp.int32, x.shape, 1)
    keep = (lane < HEAD_DIM) == (g == 0)
    xf = x.astype(F32)
    return jnp.where(keep, xf, pltpu.roll(xf, HEAD_DIM, 1))


def _attn_specs(s):
    q = pl.BlockSpec((None, s, 2 * LANES), lambda b, g: (b, 0, CB_AQ // 2 + g))
    k = pl.BlockSpec((None, s, LANES), lambda b, g: (b, 0, CB_AK))
    v = pl.BlockSpec((None, s, LANES), lambda b, g: (b, 0, CB_AV))
    grp = pl.BlockSpec((None, s, 2 * LANES), lambda b, g: (b, 0, g))
    smem = pl.BlockSpec(memory_space=pltpu.SMEM)
    return q, k, v, grp, smem


def _fill_padded(dst_ref, val, s):
    dst_ref[0:CHUNK, :] = jnp.zeros((CHUNK, LANES), dst_ref.dtype)
    dst_ref[CHUNK:CHUNK + s, :] = val.astype(dst_ref.dtype)
    dst_ref[CHUNK + s:2 * CHUNK + s, :] = jnp.zeros((CHUNK, LANES), dst_ref.dtype)


def _attn_probs(qh, kw, slope, snk, dist, valid):
    sc = _dot_nt(qh, kw)
    sc = jnp.where(valid, sc - slope * dist, NEG_INF)
    m = jnp.maximum(jnp.max(sc, axis=1, keepdims=True), snk)
    e = jnp.exp(sc - m)
    es = jnp.exp(snk - m)
    inv = 1.0 / (jnp.sum(e, axis=1, keepdims=True) + es)
    return e * inv, es * inv


def _attn_fwd(u3, slopes, sink, rider=None):
    bsz, s, _ = u3.shape
    n_blk = s // CHUNK

    def body(slope_ref, sink_ref, q_ref, k_ref, v_ref, o_ref, kp_ref, vp_ref):
        g = pl.program_id(1)
        _fill_padded(kp_ref, _dup_kv_head(k_ref[...], g), s)
        _fill_padded(vp_ref, _dup_kv_head(v_ref[...], g), s)
        m0 = lax.broadcasted_iota(jnp.int32, (CHUNK, LANES), 1) < HEAD_DIM
        m0w = lax.broadcasted_iota(jnp.int32, (3 * CHUNK, LANES), 1) < HEAD_DIM

        def blk(n, carry):
            r0 = pl.multiple_of(n * CHUNK, CHUNK)
            kw = kp_ref[pl.ds(r0, 3 * CHUNK), :]
            vw = vp_ref[pl.ds(r0, 3 * CHUNK), :].astype(F32)
            vws = (jnp.where(m0w, vw, 0.0).astype(BF16), jnp.where(m0w, 0.0, vw).astype(BF16))
            dist, valid = _attn_window_tables(n, s)
            for pr in range(2):
                qp = q_ref[pl.ds(r0, CHUNK), pr * LANES:(pr + 1) * LANES].astype(F32) * 0.125
                out = jnp.zeros((CHUNK, LANES), F32)
                for hh in range(2):
                    h = g * 4 + pr * 2 + hh
                    qh = (jnp.where(m0, qp, 0.0) if hh == 0 else jnp.where(m0, 0.0, qp)).astype(BF16)
                    p, _ = _attn_probs(qh, kw, slope_ref[h], sink_ref[h], dist, valid)
                    out = out + _dot(p.astype(BF16), vws[hh])
                o_ref[pl.ds(r0, CHUNK), pr * LANES:(pr + 1) * LANES] = out.astype(BF16)
            return carry

        lax.fori_loop(0, n_blk, blk, 0)

    q, k, v, grp, smem = _attn_specs(s)
    return _hosted_call(
        body, "attn_fwd", (bsz, 2),
        in_specs=[smem, smem, q, k, v],
        out_specs=[grp],
        out_shape=[jax.ShapeDtypeStruct((bsz, s, ATTN_WIDTH), BF16)],
        scratch_shapes=[pltpu.VMEM((s + 2 * CHUNK, LANES), BF16), pltpu.VMEM((s + 2 * CHUNK, LANES), BF16)],
        operands=(slopes, sink, u3, u3, u3), rider=rider)


def _attn_bwd(u3, d_o, slopes, sink, rider=None):
    bsz, s, _ = u3.shape
    n_blk = s // CHUNK

    def body(slope_ref, sink_ref, q_ref, k_ref, v_ref, do_ref, dq_ref, dkv_ref, ds_ref,
             kp_ref, vp_ref, dk_acc, dv_acc):
        g = pl.program_id(1)
        _fill_padded(kp_ref, _dup_kv_head(k_ref[...], g), s)
        _fill_padded(vp_ref, _dup_kv_head(v_ref[...], g), s)
        dk_acc[...] = jnp.zeros_like(dk_acc)
        dv_acc[...] = jnp.zeros_like(dv_acc)
        m0 = lax.broadcasted_iota(jnp.int32, (CHUNK, LANES), 1) < HEAD_DIM

        def blk(n, dsink):
            r0 = pl.multiple_of(n * CHUNK, CHUNK)
            win = pl.ds(r0, 3 * CHUNK)
            kw = kp_ref[win, :]
            vw = vp_ref[win, :]
            dist, valid = _attn_window_tables(n, s)
            dkw = jnp.zeros((3 * CHUNK, LANES), F32)
            dvw = jnp.zeros((3 * CHUNK, LANES), F32)
            new_dsink = []
            for pr in range(2):
                cols = slice(pr * LANES, (pr + 1) * LANES)
                qp = q_ref[pl.ds(r0, CHUNK), cols].astype(F32) * 0.125
                dop = do_ref[pl.ds(r0, CHUNK), cols].astype(F32)
                dq = jnp.zeros((CHUNK, LANES), F32)
                for hh in range(2):
                    h = g * 4 + pr * 2 + hh
                    sel = (lambda a: jnp.where(m0, a, 0.0)) if hh == 0 else (lambda a: jnp.where(m0, 0.0, a))
                    qh = sel(qp).astype(BF16)
                    doh = sel(dop).astype(BF16)
                    p, ps = _attn_probs(qh, kw, slope_ref[h], sink_ref[h], dist, valid)
                    dp = _dot_nt(doh, vw)
                    delta = jnp.sum(p * dp, axis=1, keepdims=True)
                    dsc = (p * (dp - delta)).astype(BF16)
                    dsh = jnp.sum(ps * delta, axis=0, keepdims=True)
                    new_dsink.append(dsink[pr * 2 + hh] - jnp.broadcast_to(dsh, (1, LANES)))
                    dq = dq + sel(_dot(dsc, kw))
                    dkw = dkw + _dot_tn(dsc, qh)
                    dvw = dvw + _dot_tn(p.astype(BF16), doh)
                dq_ref[pl.ds(r0, CHUNK), cols] = (dq * 0.125).astype(BF16)
            dk_acc[win, :] += dkw
            dv_acc[win, :] += dvw
            return tuple(new_dsink)

        dsink = lax.fori_loop(0, n_blk, blk, tuple(jnp.zeros((1, LANES), F32) for _ in range(4)))
        dk = dk_acc[CHUNK:CHUNK + s, :]
        dv = dv_acc[CHUNK:CHUNK + s, :]
        lane = lax.broadcasted_iota(jnp.int32, (s, LANES), 1)
        fold = lambda a: a + pltpu.roll(a, HEAD_DIM, 1)
        dkv_ref[...] = jnp.where(lane < HEAD_DIM, fold(dk), fold(dv)).astype(BF16)
        ds_ref[...] = jnp.zeros_like(ds_ref)
        for i in range(4):
            ds_ref[i:i + 1, :] = dsink[i]

    q, k, v, grp, smem = _attn_specs(s)
    return _hosted_call(
        body, "attn_bwd", (bsz, 2),
        in_specs=[smem, smem, q, k, v, grp],
        out_specs=[grp, pl.BlockSpec((None, s, LANES), lambda b, g: (b, 0, g)),
                   pl.BlockSpec((None, None, 8, LANES), lambda b, g: (b, g, 0, 0))],
        out_shape=[jax.ShapeDtypeStruct((bsz, s, ATTN_WIDTH), BF16), jax.ShapeDtypeStruct((bsz, s, 2 * LANES), BF16),
                   jax.ShapeDtypeStruct((bsz, 2, 8, LANES), F32)],
        scratch_shapes=[pltpu.VMEM((s + 2 * CHUNK, LANES), BF16), pltpu.VMEM((s + 2 * CHUNK, LANES), BF16),
                        pltpu.VMEM((s + 2 * CHUNK, LANES), F32), pltpu.VMEM((s + 2 * CHUNK, LANES), F32)],
        operands=(slopes, sink, u3, u3, u3, d_o), rider=rider)


def _ffn_bwd(dz2, gs, us, pg, ple, zh1, r1, g1, wg4, wu4, wd4, wpg, w_out):
    t = dz2.shape[0]
    tm = 256

    def body(dz_ref, gs_ref, us_ref, pg_ref, ple_ref, zh_ref, r_ref, g1_ref,
             wg_hbm, wu_hbm, wd_hbm, wpg_hbm, wo_hbm,
             dgs_ref, dus_ref, dsp_ref, dple_ref, dz1_ref, dyr_ref, dya_ref, dg1_ref, db1_ref,
             wg, wu, wd, wpg, wo):
        step = pl.program_id(0)
        _load_resident(step, [(wg_hbm, wg), (wu_hbm, wu), (wd_hbm, wd), (wpg_hbm, wpg), (wo_hbm, wo)])

        @pl.when(step == 0)
        def _():
            dg1_ref[...] = jnp.zeros_like(dg1_ref)
            db1_ref[...] = jnp.zeros_like(db1_ref)

        dz = dz_ref[...]
        dzb = dz.astype(BF16)
        dh = ALPHA * dz
        for j in range(N_SHARD):
            da = _dot_nt(dzb, wd[j])
            gj = gs_ref[j].astype(F32)
            uj = us_ref[j].astype(F32)
            sg = _sigmoid(gj)
            dgj = (da * uj * sg * (1.0 + gj * (1.0 - sg))).astype(BF16)
            duj = (da * gj * sg).astype(BF16)
            dgs_ref[j] = dgj
            dus_ref[j] = duj
            dh = dh + _dot(dgj, wg[j]) + _dot(duj, wu[j])
        pgv = pg_ref[...].astype(F32)
        plev = ple_ref[...].astype(F32)
        dple_ref[...] = (dz * pgv).astype(BF16)
        dsp = (dz * plev * pgv * (1.0 - pgv)).astype(BF16)
        dsp_ref[...] = dsp
        dh = dh + _dot_nt(dsp, wpg[...])
        zh = zh_ref[...]
        dg1_ref[...] += jnp.sum(dh * zh, axis=0, keepdims=True)
        db1_ref[...] += jnp.sum(dh, axis=0, keepdims=True)
        dzh = dh * g1_ref[...]
        m1 = jnp.mean(dzh, axis=1, keepdims=True)
        m2 = jnp.mean(dzh * zh, axis=1, keepdims=True)
        dz1 = r_ref[...] * (dzh - m1 - zh * m2)
        dz1_ref[...] = dz1
        dyc = _dot_nt(dz1.astype(BF16), wo[...])
        dyr_ref[...] = dyc[:, 0:RET_WIDTH].astype(BF16)
        dya_ref[...] = dyc[:, RET_WIDTH:].astype(BF16)

    row = lambda w: pl.BlockSpec((tm, w), lambda i: (i, 0))
    const = lambda s: pl.BlockSpec(s, lambda i: (0, 0))
    sh = pl.BlockSpec((N_SHARD, tm, FFN_SHARD), lambda i: (0, i, 0))
    hbm = pl.BlockSpec(memory_space=pl.ANY)
    sh_shape = jax.ShapeDtypeStruct((N_SHARD, t, FFN_SHARD), BF16)
    return pl.pallas_call(
        body, name="ffn_bwd", grid=(t // tm,),
        in_specs=[row(D_MODEL), sh, sh, row(D_MODEL), row(D_MODEL), row(D_MODEL), row(1), const((1, D_MODEL)),
                  hbm, hbm, hbm, hbm, hbm],
        out_specs=[sh, sh, row(D_MODEL), row(D_MODEL), row(D_MODEL), row(RET_WIDTH), row(ATTN_WIDTH),
                   const((1, D_MODEL)), const((1, D_MODEL))],
        out_shape=[sh_shape, sh_shape, jax.ShapeDtypeStruct((t, D_MODEL), BF16),
                   jax.ShapeDtypeStruct((t, D_MODEL), BF16), jax.ShapeDtypeStruct((t, D_MODEL), F32),
                   jax.ShapeDtypeStruct((t, RET_WIDTH), BF16), jax.ShapeDtypeStruct((t, ATTN_WIDTH), BF16),
                   jax.ShapeDtypeStruct((1, D_MODEL), F32), jax.ShapeDtypeStruct((1, D_MODEL), F32)],
        scratch_shapes=[pltpu.VMEM(wg4.shape, BF16), pltpu.VMEM(wu4.shape, BF16), pltpu.VMEM(wd4.shape, BF16),
                        pltpu.VMEM(wpg.shape, BF16), pltpu.VMEM(w_out.shape, BF16)],
        compiler_params=_params("arbitrary", vmem=VMEM_LIMIT),
    )(dz2, gs, us, pg, ple, zh1, r1, g1, wg4, wu4, wd4, wpg, w_out)


def _wgrad_misc(y_ret, y_att, dz1, hb, dsp, p2d, dple):
    t = dz1.shape[0]
    tk = min(t, 512)

    def body(yr_ref, ya_ref, dz_ref, hb_ref, dsp_ref, p_ref, dple_ref, wo_ref, wpg_ref, wpe_ref):
        @pl.when(pl.program_id(0) == 0)
        def _():
            wo_ref[...] = jnp.zeros_like(wo_ref)
            wpg_ref[...] = jnp.zeros_like(wpg_ref)
            wpe_ref[...] = jnp.zeros_like(wpe_ref)

        dzb = dz_ref[...].astype(BF16)
        wo_ref[0:RET_WIDTH, :] += _dot_tn(yr_ref[...], dzb)
        wo_ref[RET_WIDTH:, :] += _dot_tn(ya_ref[...], dzb)
        wpg_ref[...] += _dot_tn(hb_ref[...], dsp_ref[...])
        wpe_ref[...] += _dot_tn(p_ref[...].astype(BF16), dple_ref[...])

    row = lambda w: pl.BlockSpec((tk, w), lambda k: (k, 0))
    const = lambda s: pl.BlockSpec(s, lambda k: (0, 0))
    return pl.pallas_call(
        body, name="wgrad_misc", grid=(t // tk,),
        in_specs=[row(RET_WIDTH), row(ATTN_WIDTH), row(D_MODEL), row(D_MODEL), row(D_MODEL), row(PLE_DIM),
                  row(D_MODEL)],
        out_specs=[const((D_MODEL, D_MODEL)), const((D_MODEL, D_MODEL)), const((PLE_DIM, D_MODEL))],
        out_shape=[jax.ShapeDtypeStruct((D_MODEL, D_MODEL), F32), jax.ShapeDtypeStruct((D_MODEL, D_MODEL), F32),
                   jax.ShapeDtypeStruct((PLE_DIM, D_MODEL), F32)],
        compiler_params=_params("arbitrary", vmem=VMEM_LIMIT),
    )(y_ret, y_att, dz1, hb, dsp, p2d, dple)


def _wgrad_ffn(gs, us, dgs, dus, hb, dz2):
    t = dz2.shape[0]
    tk = min(t, 512)

    def body(g_ref, u_ref, dg_ref, du_ref, hb_ref, dz_ref, og_ref, ou_ref, od_ref):
        @pl.when(pl.program_id(1) == 0)
        def _():
            og_ref[...] = jnp.zeros_like(og_ref)
            ou_ref[...] = jnp.zeros_like(ou_ref)
            od_ref[...] = jnp.zeros_like(od_ref)

        hbv = hb_ref[...]
        og_ref[...] += _dot_tn(dg_ref[...], hbv)
        ou_ref[...] += _dot_tn(du_ref[...], hbv)
        gj = g_ref[...].astype(F32)
        act = (gj * _sigmoid(gj) * u_ref[...].astype(F32)).astype(BF16)
        od_ref[...] += _dot_tn(act, dz_ref[...].astype(BF16))

    a_spec = pl.BlockSpec((None, tk, FFN_SHARD), lambda j, k: (j, k, 0))
    b_spec = pl.BlockSpec((tk, D_MODEL), lambda j, k: (k, 0))
    o_spec = pl.BlockSpec((None, FFN_SHARD, D_MODEL), lambda j, k: (j, 0, 0))
    o_shape = jax.ShapeDtypeStruct((N_SHARD, FFN_SHARD, D_MODEL), F32)
    return pl.pallas_call(
        body, name="wgrad_ffn", grid=(N_SHARD, t // tk),
        in_specs=[a_spec, a_spec, a_spec, a_spec, b_spec, b_spec],
        out_specs=[o_spec, o_spec, o_spec], out_shape=[o_shape, o_shape, o_shape],
        compiler_params=_params("parallel", "arbitrary", vmem=VMEM_LIMIT),
    )(gs, us, dgs, dus, hb, dz2)


def _wgrad_in(pieces, x2d):
    t = x2d.shape[0]
    tk = min(t, 512)

    def body(p0, p1, p2, p3, p4, pkv, x_ref, o_ref):
        @pl.when(pl.program_id(0) == 0)
        def _():
            o_ref[...] = jnp.zeros_like(o_ref)

        xb = x_ref[...].astype(BF16)
        for i, ref in enumerate((p0, p1, p2, p3, p4)):
            o_ref[i * 512:(i + 1) * 512, :] += _dot_tn(ref[...], xb)
        o_ref[2560:IN_WIDTH, :] += _dot_tn(pkv[...], xb)

    row = lambda w: pl.BlockSpec((tk, w), lambda k: (k, 0))
    return pl.pallas_call(
        body, name="wgrad_in", grid=(t // tk,),
        in_specs=[row(512)] * 5 + [row(256), row(D_MODEL)],
        out_specs=pl.BlockSpec((IN_WIDTH, D_MODEL), lambda k: (0, 0)),
        out_shape=jax.ShapeDtypeStruct((IN_WIDTH, D_MODEL), F32),
        compiler_params=_params("arbitrary", vmem=VMEM_LIMIT),
    )(*pieces, x2d)


def _inproj_bwd(dz1, pieces, w_main, w_kv):
    t = dz1.shape[0]
    tm = 512

    def body(dz_ref, p0, p1, p2, p3, p4, pkv, wm_ref, wkv_ref, o_ref):
        acc = ALPHA * dz_ref[...]
        for i, ref in enumerate((p0, p1, p2, p3, p4)):
            acc = acc + _dot(ref[...], wm_ref[i * 512:(i + 1) * 512, :])
        o_ref[...] = acc + _dot(pkv[...], wkv_ref[...])

    row = lambda w: pl.BlockSpec((tm, w), lambda i: (i, 0))
    const = lambda s: pl.BlockSpec(s, lambda i: (0, 0))
    return pl.pallas_call(
        body, name="inproj_bwd", grid=(t // tm,),
        in_specs=[row(D_MODEL)] + [row(512)] * 5 + [row(256), const(w_main.shape), const(w_kv.shape)],
        out_specs=row(D_MODEL),
        out_shape=jax.ShapeDtypeStruct((t, D_MODEL), F32),
        compiler_params=_params("parallel", vmem=VMEM_LIMIT),
    )(dz1, *pieces, w_main, w_kv)


def _coords():
    return lax.axis_index("x"), lax.axis_index("y"), lax.axis_index("c")


def _chip_of(x, y, rel):
    return (1 - x if rel & 2 else x), (1 - y if rel & 1 else y)


def _all_gather_weights(shards):
    first = _gather_chips_rider(shards)
    second = _gather_pass_rider([jax.ShapeDtypeStruct((N_SHARD,) + s.shape, s.dtype) for s in shards], chained=True)
    return _run_riders("gather_weights", shards, first.out_shapes, [first, second])


def _run_riders(name, ins, out_shapes, riders):
    n_in, n_out = len(ins), len(out_shapes)

    def body(*refs):
        in_refs, out_refs = refs[:n_in], refs[n_in:n_in + n_out]
        k = n_in + n_out
        for r in riders:
            sems = refs[k:k + len(r.sems)]
            k += len(r.sems)
            r.start(in_refs, out_refs, sems)
            r.finish(in_refs, out_refs, sems)

    hbm = pl.BlockSpec(memory_space=pl.ANY)
    return pl.pallas_call(
        body, name=name, in_specs=[hbm] * n_in, out_specs=[hbm] * n_out, out_shape=list(out_shapes),
        scratch_shapes=[s for r in riders for s in r.sems],
    )(*ins)


def _gather_half(outs, w, chip, cc):
    h = outs[w].shape[1] // 2
    return outs[w].at[chip, pl.ds(cc * h, h), :]


def _gather_chips_rider(shards):
    nw = len(shards)

    def copies(ins, outs, sems):
        send, recv, lsend, lrecv = sems
        x, y, c = _coords()
        me = 2 * x + y
        own = [pltpu.make_async_remote_copy(
            src_ref=ins[w], dst_ref=outs[w].at[me], send_sem=lsend.at[w], recv_sem=lrecv.at[w],
            device_id=(x, y, 1 - c), device_id_type=MESH) for w in range(nw)]
        out, arrive = [], []
        for rel in (1, 2, 3):
            kx, ky = _chip_of(x, y, rel)
            for w in range(nw):
                h = shards[w].shape[0] // 2
                sem = dict(send_sem=send.at[w * 3 + rel - 1], recv_sem=recv.at[w * 3 + rel - 1],
                           device_id=(kx, ky, c), device_id_type=MESH)
                out.append(pltpu.make_async_remote_copy(
                    src_ref=ins[w].at[pl.ds(c * h, h), :], dst_ref=_gather_half(outs, w, me, c), **sem))
                theirs = _gather_half(outs, w, 2 * kx + ky, c)
                arrive.append(pltpu.make_async_remote_copy(src_ref=theirs, dst_ref=theirs, **sem))
        return own, out, arrive

    def start(ins, outs, sems):
        own, out, _ = copies(ins, outs, sems)
        for cp in own + out:
            cp.start()

    def finish(ins, outs, sems):
        own, out, arrive = copies(ins, outs, sems)
        for cp in arrive:
            cp.wait_recv()
        for cp in out:
            cp.wait_send()
        for cp in own:
            cp.wait()

    dma = pltpu.SemaphoreType.DMA
    return _Rider(shards, [jax.ShapeDtypeStruct((N_SHARD,) + s.shape, s.dtype) for s in shards],
                  [dma((3 * nw,)), dma((3 * nw,)), dma((nw,)), dma((nw,))], start, finish)


def _gather_pass_rider(gathered, chained=False):
    nw = len(gathered)

    def copies(outs, sems, cc):
        send, recv = sems
        x, y, c = _coords()
        res = []
        for rel in (1, 2, 3):
            kx, ky = _chip_of(x, y, rel)
            for w in range(nw):
                rows = _gather_half(outs, w, 2 * kx + ky, cc)
                res.append(pltpu.make_async_remote_copy(
                    src_ref=rows, dst_ref=rows, send_sem=send.at[w * 3 + rel - 1], recv_sem=recv.at[w * 3 + rel - 1],
                    device_id=(x, y, 1 - c), device_id_type=MESH))
        return res

    def start(ins, outs, sems):
        for cp in copies(outs, sems, lax.axis_index("c")):
            cp.start()

    def finish(ins, outs, sems):
        c = lax.axis_index("c")
        for cp in copies(outs, sems, 1 - c):
            cp.wait_recv()
        for cp in copies(outs, sems, c):
            cp.wait_send()

    dma = pltpu.SemaphoreType.DMA
    shapes = [jax.ShapeDtypeStruct(g.shape, g.dtype) for g in gathered]
    if chained:
        return _Rider([], [], [dma((3 * nw,)), dma((3 * nw,))], start, finish)
    return _Rider(gathered, shapes, [dma((3 * nw,)), dma((3 * nw,))], start, finish,
                  aliases={w: w for w in range(nw)})


def _exchange_halves_rider(parts):
    nw = len(parts)

    def copies(ins, outs, sems):
        send, recv = sems
        x, y, c = _coords()
        res = []
        for w in range(nw):
            h = parts[w].shape[1] // 2
            res.append(pltpu.make_async_remote_copy(
                src_ref=ins[w].at[:, pl.ds((1 - c) * h, h), :], dst_ref=outs[w],
                send_sem=send.at[w], recv_sem=recv.at[w], device_id=(x, y, 1 - c), device_id_type=MESH))
        return res

    def start(ins, outs, sems):
        for cp in copies(ins, outs, sems):
            cp.start()

    def finish(ins, outs, sems):
        for cp in copies(ins, outs, sems):
            cp.wait()

    dma = pltpu.SemaphoreType.DMA
    return _Rider(parts, [jax.ShapeDtypeStruct((N_SHARD, p.shape[1] // 2, p.shape[2]), F32) for p in parts],
                  [dma((nw,)), dma((nw,))], start, finish)


def _exchange_halves(parts):
    r = _exchange_halves_rider(parts)
    return _run_riders("exchange_halves", parts, r.out_shapes, [r])


def _add_halves(parts, theirs, c_arr):
    nw = len(parts)
    split = 2

    def body(c_ref, *refs):
        ins, oth = refs[:nw], refs[nw:2 * nw]
        o32, o16 = refs[2 * nw:3 * nw], refs[3 * nw:]
        for w in range(nw):
            sm = ins[w][...] + oth[w][...]
            o32[w][...] = sm
            o16[w][...] = sm.astype(BF16)

    in_specs, oth_specs, out_specs, shapes32, shapes16 = [], [], [], [], []
    for p in parts:
        hb = p.shape[1] // 2 // split
        in_specs.append(pl.BlockSpec((None, hb, p.shape[2]), lambda j, i, c_ref: (j, c_ref[0] * split + i, 0)))
        oth_specs.append(pl.BlockSpec((None, hb, p.shape[2]), lambda j, i, c_ref: (j, i, 0)))
        shapes32.append(jax.ShapeDtypeStruct((N_SHARD, p.shape[1] // 2, p.shape[2]), F32))
        shapes16.append(jax.ShapeDtypeStruct((N_SHARD, p.shape[1] // 2, p.shape[2]), BF16))
    return pl.pallas_call(
        body, name="add_halves",
        grid_spec=pltpu.PrefetchScalarGridSpec(
            num_scalar_prefetch=1, grid=(N_SHARD, split),
            in_specs=in_specs + oth_specs, out_specs=oth_specs + oth_specs),
        out_shape=shapes32 + shapes16,
        compiler_params=_params("parallel", "parallel", vmem=VMEM_LIMIT),
    )(c_arr, *parts, *theirs)


def _exchange_chips_rider(sums16):
    nw = len(sums16)

    def copies(ins, outs, sems):
        send, recv = sems
        x, y, c = _coords()
        res = []
        for rel in (1, 2, 3):
            kx, ky = _chip_of(x, y, rel)
            for w in range(nw):
                res.append(pltpu.make_async_remote_copy(
                    src_ref=ins[w].at[2 * kx + ky], dst_ref=outs[w].at[rel - 1],
                    send_sem=send.at[w * 3 + rel - 1], recv_sem=recv.at[w * 3 + rel - 1],
                    device_id=(kx, ky, c), device_id_type=MESH))
        return res

    def start(ins, outs, sems):
        for cp in copies(ins, outs, sems):
            cp.start()

    def finish(ins, outs, sems):
        for cp in copies(ins, outs, sems):
            cp.wait()

    dma = pltpu.SemaphoreType.DMA
    return _Rider(sums16, [jax.ShapeDtypeStruct((3,) + s.shape[1:], BF16) for s in sums16],
                  [dma((3 * nw,)), dma((3 * nw,))], start, finish)


def _exchange_chips(sums16):
    r = _exchange_chips_rider(sums16)
    return _run_riders("exchange_chips", sums16, r.out_shapes, [r])


def _add_chips(sums32, theirs, pos):
    nw = len(sums32)
    split = 2

    def body(pos_ref, *refs):
        ins, oth, outs = refs[:nw], refs[nw:2 * nw], refs[2 * nw:]
        for w in range(nw):
            acc = ins[w][...]
            for r in range(3):
                acc = acc + oth[w][r].astype(F32)
            outs[w][...] = acc

    in_specs, oth_specs, out_specs, shapes = [], [], [], []
    for s in sums32:
        hb = s.shape[1] // split
        in_specs.append(pl.BlockSpec((None, hb, s.shape[2]), lambda i, pos_ref: (pos_ref[0], i, 0)))
        oth_specs.append(pl.BlockSpec((3, hb, s.shape[2]), lambda i, pos_ref: (0, i, 0)))
        out_specs.append(pl.BlockSpec((hb, s.shape[2]), lambda i, pos_ref: (pos_ref[1] * split + i, 0)))
        shapes.append(jax.ShapeDtypeStruct((2 * s.shape[1], s.shape[2]), F32))
    return pl.pallas_call(
        body, name="add_chips",
        grid_spec=pltpu.PrefetchScalarGridSpec(
            num_scalar_prefetch=1, grid=(split,), in_specs=in_specs + oth_specs, out_specs=out_specs),
        out_shape=shapes,
        compiler_params=_params("parallel", vmem=VMEM_LIMIT),
    )(pos, *sums32, *theirs)


def _join_halves(shards):
    nw = len(shards)

    def body(*refs):
        outs = refs[nw:2 * nw]
        send, recv = refs[2 * nw:]
        x, y, c = _coords()

        def copy(w, cc):
            h = shards[w].shape[0] // 2
            rows = outs[w].at[pl.ds(cc * h, h), :]
            return pltpu.make_async_remote_copy(
                src_ref=rows, dst_ref=rows, send_sem=send.at[w], recv_sem=recv.at[w],
                device_id=(x, y, 1 - c), device_id_type=MESH)

        for w in range(nw):
            copy(w, c).start()
        for w in range(nw):
            copy(w, 1 - c).wait_recv()
            copy(w, c).wait_send()

    hbm = pl.BlockSpec(memory_space=pl.ANY)
    return pl.pallas_call(
        body, name="join_halves",
        in_specs=[hbm] * nw, out_specs=[hbm] * nw,
        out_shape=[jax.ShapeDtypeStruct(s.shape, F32) for s in shards],
        input_output_aliases={w: w for w in range(nw)},
        scratch_shapes=[pltpu.SemaphoreType.DMA((nw,)), pltpu.SemaphoreType.DMA((nw,))],
    )(*shards)


def _adamw_math(w, g, m, v):
    m = ADAM_B1 * m + (1.0 - ADAM_B1) * g
    v = ADAM_B2 * v + (1.0 - ADAM_B2) * (g * g)
    m_hat = m / (1.0 - ADAM_B1 ** ADAM_STEP)
    v_hat = v / (1.0 - ADAM_B2 ** ADAM_STEP)
    delta = -ADAM_LR * (m_hat / (jnp.sqrt(v_hat) + ADAM_EPS) + ADAM_WD * w)
    return delta, m, v


def _adamw(ws, gs, ms, vs):
    nw = len(ws)
    split = 8

    def body(*refs):
        w_r, g_r, m_r, v_r = (refs[i * nw:(i + 1) * nw] for i in range(4))
        d_o, m_o, v_o = (refs[(4 + i) * nw:(5 + i) * nw] for i in range(3))
        for k in range(nw):
            d, m, v = _adamw_math(w_r[k][...], g_r[k][...], m_r[k][...], v_r[k][...])
            d_o[k][...] = d
            m_o[k][...] = m
            v_o[k][...] = v

    specs = [pl.BlockSpec((w.shape[0] // split, w.shape[1]), lambda i: (i, 0)) for w in ws]
    shapes = [jax.ShapeDtypeStruct(w.shape, F32) for w in ws]
    outs = pl.pallas_call(
        body, name="adamw", grid=(split,),
        in_specs=specs * 4, out_specs=specs * 3, out_shape=shapes * 3,
        compiler_params=_params("parallel", vmem=VMEM_LIMIT),
    )(*ws, *gs, *ms, *vs)
    return outs[:nw], outs[nw:2 * nw], outs[2 * nw:]


SMALL_ROWS = 8
SMALL_COLS = D_MODEL
LOSS_COL = RET_WIDTH + 24


def _small_allreduce_adamw(part, w, m, v):
    def body(part_ref, w_ref, m_ref, v_ref, g_out, d_out, m_out, v_out, all_ref, send, recv):
        x, y, c = _coords()
        me = 4 * x + 2 * y + c
        all_ref[me] = part_ref[...]
        copies = []
        for rel in range(1, 8):
            px = 1 - x if rel & 4 else x
            py = 1 - y if rel & 2 else y
            pc = 1 - c if rel & 1 else c
            copies.append(pltpu.make_async_remote_copy(
                src_ref=part_ref, dst_ref=all_ref.at[me],
                send_sem=send.at[rel - 1], recv_sem=recv.at[rel - 1], device_id=(px, py, pc), device_id_type=MESH))
        for cp in copies:
            cp.start()
        for cp in copies:
            cp.wait()
        g = all_ref[0]
        for k in range(1, 8):
            g = g + all_ref[k]
        d, mn, vn = _adamw_math(w_ref[...], g, m_ref[...], v_ref[...])
        g_out[...] = g
        d_out[...] = d
        m_out[...] = mn
        v_out[...] = vn

    vm = pl.BlockSpec(memory_space=pltpu.VMEM)
    shape = jax.ShapeDtypeStruct((SMALL_ROWS, SMALL_COLS), F32)
    return pl.pallas_call(
        body, name="small_allreduce_adamw",
        in_specs=[vm] * 4, out_specs=[vm] * 4, out_shape=[shape] * 4,
        scratch_shapes=[pltpu.VMEM((8, SMALL_ROWS, SMALL_COLS), F32),
                        pltpu.SemaphoreType.DMA((7,)), pltpu.SemaphoreType.DMA((7,))],
    )(part, w, m, v)


SMALL_NAMES = ("ret_decay_fwd", "ret_decay_bwd", "attn_sink", "ret_gn_gain",
               "ln1_gain", "ln1_bias", "ln2_gain", "ln2_bias")


LN_NAMES = ("ln1_gain", "ln1_bias", "ln2_gain", "ln2_bias")


def _pack_small(vals, extra=None):
    tail = jnp.zeros((1, 1), F32) if extra is None else extra.reshape(1, 1)
    row4 = jnp.concatenate([vals["ret_gn_gain"], vals["ret_decay_fwd"], vals["ret_decay_bwd"], vals["attn_sink"],
                            tail, jnp.zeros((1, SMALL_COLS - LOSS_COL - 1), F32)], axis=1)
    rows = [vals[n] for n in LN_NAMES] + [row4, jnp.zeros((SMALL_ROWS - 5, SMALL_COLS), F32)]
    return jnp.concatenate(rows, axis=0)


def _unpack_small(packed):
    out = {n: packed[i:i + 1] for i, n in enumerate(LN_NAMES)}
    o = RET_WIDTH
    out.update(ret_gn_gain=packed[4:5, 0:o], ret_decay_fwd=packed[4:5, o:o + 8],
               ret_decay_bwd=packed[4:5, o + 8:o + 16], attn_sink=packed[4:5, o + 16:o + 24])
    return out


def _local_step(x, p, tgt, w_in_t, rest, small, core=None):
    bsz, s, _ = x.shape
    t = bsz * s
    x2d = x.reshape(t, D_MODEL)
    p2d = p.reshape(t, PLE_DIM)
    tgt2d = tgt.reshape(t, D_MODEL)
    dec_f = small["ret_decay_fwd"].reshape(8)
    dec_b = small["ret_decay_bwd"].reshape(8)
    lg_f = jnp.log1p(-jnp.exp2(dec_f))
    lg_b = jnp.log1p(-jnp.exp2(dec_b))
    per_lane = lambda v: jnp.repeat(v, HEAD_DIM).reshape(4, 1, LANES)
    lgf_l, lgb_l = per_lane(lg_f), per_lane(lg_b)
    sink = small["attn_sink"].reshape(8)
    slopes = 2.0 ** (-(jnp.arange(8, dtype=F32) + 1.0))
    gn_gain = small["ret_gn_gain"]
    g1, b1, g2, b2 = (small[n] for n in ("ln1_gain", "ln1_bias", "ln2_gain", "ln2_bias"))

    u = _inproj(x2d, w_in_t)
    u3 = u.reshape(bsz, s, IN_WIDTH)
    if core is None:
        wts = rest
        y_pre, y_ret = _ret_fwd(u3, lgf_l, lgb_l, gn_gain)
        (y_att,) = _attn_fwd(u3, slopes, sink)
    else:
        y_pre, y_ret, *from_chips = _ret_fwd(u3, lgf_l, lgb_l, gn_gain, rider=_gather_chips_rider(rest))
        y_att, *gathered = _attn_fwd(u3, slopes, sink, rider=_gather_pass_rider(from_chips))
        wts = _assemble_weights(dict(zip(REST_NAMES, gathered)))
    zh1, r1, hb = _outproj_ln1(y_ret.reshape(t, RET_WIDTH), y_att.reshape(t, ATTN_WIDTH), x2d, wts["w_out"], g1, b1)
    dz2, gs, us, pg, ple, sq, dg2, db2 = _ffn_fwd(zh1, hb, p2d, tgt2d, g1, b1, g2, b2, wts["gate4"], wts["up4"],
                                                 wts["down4"], wts["ple_proj"], wts["ple_gate"])
    dgs, dus, dsp, dple, dz1, dyr, dya, dg1, db1 = _ffn_bwd(dz2, gs, us, pg, ple, zh1, r1, g1, wts["gate4"],
                                                          wts["up4"], wts["down4"], wts["ple_gate"], wts["w_out"])
    d_gate4, d_up4, d_down4 = _wgrad_ffn(gs, us, dgs, dus, hb, dz2)
    d_w_out, d_ple_gate, d_ple_proj = _wgrad_misc(y_ret.reshape(t, RET_WIDTH), y_att.reshape(t, ATTN_WIDTH), dz1,
                                                  hb, dsp, p2d, dple)
    grads_rest = [
        d_w_out.reshape(N_SHARD, D_MODEL // N_SHARD, D_MODEL), d_gate4, d_up4, d_down4,
        d_ple_proj.reshape(PLE_DIM, N_SHARD, D_MODEL // N_SHARD).transpose(1, 0, 2),
        d_ple_gate.reshape(N_SHARD, D_MODEL // N_SHARD, D_MODEL)]
    dyr3, dya3 = dyr.reshape(bsz, s, RET_WIDTH), dya.reshape(bsz, s, ATTN_WIDTH)
    if core is None:
        drq, drk, drv, drg, rpart = _ret_bwd(u3, y_pre, dyr3, lgf_l, lgb_l, gn_gain)
        daq, dakv, spart = _attn_bwd(u3, dya3, slopes, sink)
    else:
        nr = len(grads_rest)
        drq, drk, drv, drg, rpart, *theirs = _ret_bwd(u3, y_pre, dyr3, lgf_l, lgb_l, gn_gain,
                                                      rider=_exchange_halves_rider(grads_rest))
        sums = _add_halves(grads_rest, theirs, core)
        daq, dakv, spart, *from_chips = _attn_bwd(u3, dya3, slopes, sink, rider=_exchange_chips_rider(sums[nr:]))
        grads_rest = (sums[:nr], from_chips)
    pieces = [a.reshape(t, -1) for a in (drq, drk, drv, drg, daq, dakv)]
    kv0 = CB_AK * LANES
    kv_order = (0, 128, 64, 192)
    w_kv = jnp.concatenate([w_in_t[kv0 + o:kv0 + o + HEAD_DIM] for o in kv_order], axis=0)
    grad_x = _inproj_bwd(dz1, pieces, w_in_t[:kv0], w_kv).reshape(bsz, s, D_MODEL)
    d_in = _wgrad_in(pieces, x2d)
    d_w_in_t = jnp.concatenate([d_in[:kv0]] + [d_in[kv0 + o:kv0 + o + HEAD_DIM] for o in kv_order], axis=0)

    rsum = jnp.sum(rpart, axis=0)
    lane_heads = lambda row: jnp.sum(row.reshape(4, 2, HEAD_DIM), axis=-1).reshape(8)
    dlg_f = lane_heads(rsum[:, 0, :]) + jnp.stack([jnp.sum(rsum[:, 2, :], -1), jnp.sum(rsum[:, 3, :], -1)], 1).reshape(8)
    dlg_b = lane_heads(rsum[:, 1, :]) + jnp.stack([jnp.sum(rsum[:, 4, :], -1), jnp.sum(rsum[:, 5, :], -1)], 1).reshape(8)
    chain = lambda d: -(math.log(2.0) * jnp.exp2(d)) / (1.0 - jnp.exp2(d))
    grads_small = {
        "ret_decay_fwd": (dlg_f * chain(dec_f)).reshape(1, 8),
        "ret_decay_bwd": (dlg_b * chain(dec_b)).reshape(1, 8),
        "attn_sink": jnp.sum(spart, axis=0)[:, 0:4, 0].reshape(1, 8),
        "ret_gn_gain": rsum[:, 6, :].reshape(1, RET_WIDTH),
        "ln1_gain": dg1, "ln1_bias": db1, "ln2_gain": dg2, "ln2_bias": db2,
    }
    return sq[0, 0], grad_x, d_w_in_t.reshape(N_SHARD, FFN_SHARD, D_MODEL), grads_rest, grads_small


BIG_NAMES = ("w_in", "w_out", "w_ffn_gate", "w_ffn_up", "w_ffn_down", "w_ple_proj", "w_ple_gate")
REST_NAMES = BIG_NAMES[1:]
TRANSPOSED = ("w_in", "w_ffn_gate", "w_ffn_up")
WEIGHT_ORDER = ("w_in", "ret_decay_fwd", "ret_decay_bwd", "ret_gn_gain", "attn_sink", "w_out", "ln1_gain",
                "ln1_bias", "w_ffn_gate", "w_ffn_up", "w_ffn_down", "w_ple_proj", "w_ple_gate", "ln2_gain", "ln2_bias")


def _shard_rows(name, a):
    return jnp.swapaxes(a[0], 0, 1) if name in TRANSPOSED else a[0]


def _unshard_rows(name, a):
    return (jnp.swapaxes(a, 0, 1) if name in TRANSPOSED else a)[None]


def _assemble_weights(gathered):
    cols = lambda a: a.transpose(1, 0, 2).reshape(a.shape[1], N_SHARD * a.shape[2])
    rows = lambda a: a.reshape(N_SHARD * a.shape[1], a.shape[2])
    return {"w_out": rows(gathered["w_out"]),
            "gate4": gathered["w_ffn_gate"], "up4": gathered["w_ffn_up"], "down4": gathered["w_ffn_down"],
            "ple_proj": cols(gathered["w_ple_proj"]), "ple_gate": rows(gathered["w_ple_gate"])}


def kernel(x, p, w_in, ret_decay_fwd, ret_decay_bwd, ret_gn_gain, attn_sink, w_out, ln1_gain, ln1_bias, w_ffn_gate, w_ffn_up, w_ffn_down, w_ple_proj, w_ple_gate, ln2_gain, ln2_bias, loss_target, m_w_in, m_ret_decay_fwd, m_ret_decay_bwd, m_ret_gn_gain, m_attn_sink, m_w_out, m_ln1_gain, m_ln1_bias, m_w_ffn_gate, m_w_ffn_up, m_w_ffn_down, m_w_ple_proj, m_w_ple_gate, m_ln2_gain, m_ln2_bias, v_w_in, v_ret_decay_fwd, v_ret_decay_bwd, v_ret_gn_gain, v_attn_sink, v_w_out, v_ln1_gain, v_ln1_bias, v_w_ffn_gate, v_w_ffn_up, v_w_ffn_down, v_w_ple_proj, v_w_ple_gate, v_ln2_gain, v_ln2_bias):
    w = dict(w_in=w_in, ret_decay_fwd=ret_decay_fwd, ret_decay_bwd=ret_decay_bwd, ret_gn_gain=ret_gn_gain,
             attn_sink=attn_sink, w_out=w_out, ln1_gain=ln1_gain, ln1_bias=ln1_bias, w_ffn_gate=w_ffn_gate,
             w_ffn_up=w_ffn_up, w_ffn_down=w_ffn_down, w_ple_proj=w_ple_proj, w_ple_gate=w_ple_gate,
             ln2_gain=ln2_gain, ln2_bias=ln2_bias)
    m = dict(w_in=m_w_in, ret_decay_fwd=m_ret_decay_fwd, ret_decay_bwd=m_ret_decay_bwd, ret_gn_gain=m_ret_gn_gain,
             attn_sink=m_attn_sink, w_out=m_w_out, ln1_gain=m_ln1_gain, ln1_bias=m_ln1_bias, w_ffn_gate=m_w_ffn_gate,
             w_ffn_up=m_w_ffn_up, w_ffn_down=m_w_ffn_down, w_ple_proj=m_w_ple_proj, w_ple_gate=m_w_ple_gate,
             ln2_gain=m_ln2_gain, ln2_bias=m_ln2_bias)
    v = dict(w_in=v_w_in, ret_decay_fwd=v_ret_decay_fwd, ret_decay_bwd=v_ret_decay_bwd, ret_gn_gain=v_ret_gn_gain,
             attn_sink=v_attn_sink, w_out=v_w_out, ln1_gain=v_ln1_gain, ln1_bias=v_ln1_bias, w_ffn_gate=v_w_ffn_gate,
             w_ffn_up=v_w_ffn_up, w_ffn_down=v_w_ffn_down, w_ple_proj=v_w_ple_proj, w_ple_gate=v_w_ple_gate,
             ln2_gain=v_ln2_gain, ln2_bias=v_ln2_bias)
    big = lambda d: [_shard_rows(n, d[n]) for n in BIG_NAMES]
    small = lambda d: {n: d[n] for n in SMALL_NAMES}

    chip = 2 * lax.axis_index("x") + lax.axis_index("y")
    core = lax.axis_index("c")
    c_arr = core.astype(jnp.int32).reshape(1)
    pos = jnp.stack([chip, core]).astype(jnp.int32)

    shards = [a.astype(BF16) for a in big(w)]
    (w_in4,) = _all_gather_weights(shards[:1])
    w_in_t = w_in4.reshape(IN_WIDTH, D_MODEL)
    sq, grad_x, d_in, (sums32, from_chips), grads_small = _local_step(x, p[0], loss_target, w_in_t, shards[1:],
                                                                     small(w), core=c_arr)
    their_in = _exchange_halves([d_in])
    sum32_in, sum16_in = _add_halves([d_in], their_in, c_arr)
    chips_in = _exchange_chips([sum16_in])
    g_big = _join_halves(_add_chips([sum32_in] + list(sums32), list(chips_in) + list(from_chips), pos))
    d_big, m_big, v_big = _adamw(big(w), g_big, big(m), big(v))

    g_s, d_s, m_s, v_s = _small_allreduce_adamw(_pack_small(grads_small, sq), _pack_small(small(w)),
                                                _pack_small(small(m)), _pack_small(small(v)))
    loss = g_s[4, LOSS_COL] * (0.5 / D_MODEL)

    def tree(bigs, packed):
        out = {n: _unshard_rows(n, a) for n, a in zip(BIG_NAMES, bigs)}
        out.update(_unpack_small(packed))
        return [out[n] for n in WEIGHT_ORDER]

    return (loss, grad_x, *tree(g_big, g_s), *tree(d_big, d_s), *tree(m_big, m_s), *tree(v_big, v_s))
```

```python
import functools
import math

import jax
import jax.numpy as jnp
from jax import lax
from jax.experimental import pallas as pl
from jax.experimental.pallas import tpu as pltpu

F32 = jnp.float32
BF16 = jnp.bfloat16

D_MODEL = 1024
HEAD_DIM = 64
RET_HEADS = 8
ATTN_HEADS = 8
RET_WIDTH = 512
ATTN_WIDTH = 512
KV_WIDTH = 128
IN_WIDTH = 2816
FFN = 2816
N_SHARD = 4
FFN_SHARD = FFN // N_SHARD
PLE_DIM = 256
CHUNK = 128
LANES = 128
ALPHA = 2.0 ** 0.25
LN_EPS = 1e-5
GN_EPS = 1e-5
NEG_INF = -1e30
ADAM_LR = 0.001
ADAM_B1 = 0.9
ADAM_B2 = 0.999
ADAM_EPS = 1e-08
ADAM_WD = 0.01
ADAM_STEP = 10
VMEM_LIMIT = 56 * 1024 * 1024
MESH = pl.DeviceIdType.MESH

CB_RQ, CB_RK, CB_RV, CB_RG, CB_AQ, CB_AK, CB_AV = 0, 4, 8, 12, 16, 20, 21


def _dot(a, b):
    return jnp.dot(a, b, preferred_element_type=F32)


def _dot_nt(a, b):
    return lax.dot_general(a, b, (((1,), (1,)), ((), ())), preferred_element_type=F32)


def _dot_tn(a, b):
    return lax.dot_general(a, b, (((0,), (0,)), ((), ())), preferred_element_type=F32)


def _sigmoid(x):
    return 1.0 / (1.0 + jnp.exp(-x))


def _params(*sem, vmem=None):
    return pltpu.CompilerParams(dimension_semantics=tuple(sem) if sem else None, vmem_limit_bytes=vmem)


class _Rider:
    def __init__(self, ins, out_shapes, sems, start, finish, aliases=None):
        self.ins, self.out_shapes, self.sems = list(ins), list(out_shapes), list(sems)
        self.start, self.finish, self.aliases = start, finish, dict(aliases or {})


def _hosted_call(body, name, grid, in_specs, out_specs, out_shape, scratch_shapes, operands, rider=None):
    n_in, n_out, n_scr = len(in_specs), len(out_specs), len(scratch_shapes)
    if rider is None:
        return pl.pallas_call(
            body, name=name, grid=grid, in_specs=in_specs, out_specs=out_specs, out_shape=out_shape,
            scratch_shapes=scratch_shapes,
            compiler_params=_params(*(["parallel"] * len(grid)), vmem=VMEM_LIMIT))(*operands)
    r_in, r_out = len(rider.ins), len(rider.out_shapes)

    def full_body(*refs):
        main_in, rin = refs[:n_in], refs[n_in:n_in + r_in]
        o0 = n_in + r_in
        main_out, rout = refs[o0:o0 + n_out], refs[o0 + n_out:o0 + n_out + r_out]
        s0 = o0 + n_out + r_out
        main_scr, rsem = refs[s0:s0 + n_scr], refs[s0 + n_scr:]
        first = functools.reduce(jnp.logical_and, [pl.program_id(a) == 0 for a in range(len(grid))])
        last = functools.reduce(jnp.logical_and, [pl.program_id(a) == g - 1 for a, g in enumerate(grid)])

        @pl.when(first)
        def _():
            rider.start(rin, rout, rsem)

        body(*main_in, *main_out, *main_scr)

        @pl.when(last)
        def _():
            rider.finish(rin, rout, rsem)

    hbm = pl.BlockSpec(memory_space=pl.ANY)
    return pl.pallas_call(
        full_body, name=name, grid=grid,
        in_specs=list(in_specs) + [hbm] * r_in, out_specs=list(out_specs) + [hbm] * r_out,
        out_shape=list(out_shape) + rider.out_shapes,
        scratch_shapes=list(scratch_shapes) + rider.sems,
        input_output_aliases={n_in + i: n_out + o for i, o in rider.aliases.items()},
        compiler_params=_params(*(["arbitrary"] * len(grid)), vmem=VMEM_LIMIT),
    )(*operands, *rider.ins)


def _head_mean(x, m0):
    s0 = jnp.sum(jnp.where(m0, x, 0.0), axis=1, keepdims=True)
    s1 = jnp.sum(jnp.where(m0, 0.0, x), axis=1, keepdims=True)
    return jnp.where(m0, s0, s1) * (1.0 / HEAD_DIM)


def _inproj(x2d, w_in_t):
    t = x2d.shape[0]
    tm = 512
    nb = 256

    def body(x_ref, w_ref, o_ref):
        xb = x_ref[...].astype(BF16)
        for n in range(0, IN_WIDTH, nb):
            o_ref[:, n:n + nb] = _dot_nt(xb, w_ref[n:n + nb, :]).astype(BF16)

    return pl.pallas_call(
        body, name="inproj", grid=(t // tm,),
        in_specs=[pl.BlockSpec((tm, D_MODEL), lambda i: (i, 0)),
                  pl.BlockSpec((IN_WIDTH, D_MODEL), lambda i: (0, 0))],
        out_specs=pl.BlockSpec((tm, IN_WIDTH), lambda i: (i, 0)),
        out_shape=jax.ShapeDtypeStruct((t, IN_WIDTH), BF16),
        compiler_params=_params("parallel", vmem=VMEM_LIMIT),
    )(x2d, w_in_t)


def _outproj_ln1(y_ret, y_att, x2d, w_out, gain, bias):
    t = x2d.shape[0]
    tm = 512

    def body(yr_ref, ya_ref, x_ref, w_ref, g_ref, b_ref, zh_ref, r_ref, hb_ref):
        mix = _dot(yr_ref[...], w_ref[0:RET_WIDTH, :]) + _dot(ya_ref[...], w_ref[RET_WIDTH:, :])
        z = ALPHA * x_ref[...] + mix
        mu = jnp.mean(z, axis=1, keepdims=True)
        zc = z - mu
        var = jnp.mean(zc * zc, axis=1, keepdims=True)
        r = lax.rsqrt(var + LN_EPS)
        zh = zc * r
        zh_ref[...] = zh
        r_ref[...] = r
        hb_ref[...] = (zh * g_ref[...] + b_ref[...]).astype(BF16)

    row = lambda w: pl.BlockSpec((tm, w), lambda i: (i, 0))
    const = lambda s: pl.BlockSpec(s, lambda i: (0, 0))
    return pl.pallas_call(
        body, name="outproj_ln1", grid=(t // tm,),
        in_specs=[row(RET_WIDTH), row(ATTN_WIDTH), row(D_MODEL), const((D_MODEL, D_MODEL)),
                  const((1, D_MODEL)), const((1, D_MODEL))],
        out_specs=[row(D_MODEL), row(1), row(D_MODEL)],
        out_shape=[jax.ShapeDtypeStruct((t, D_MODEL), F32), jax.ShapeDtypeStruct((t, 1), F32),
                   jax.ShapeDtypeStruct((t, D_MODEL), BF16)],
        compiler_params=_params("parallel", vmem=VMEM_LIMIT),
    )(y_ret, y_att, x2d, w_out, gain, bias)


def _load_resident(step, pairs):
    @pl.when(step == 0)
    def _():
        for src, dst in pairs:
            pltpu.sync_copy(src, dst)


def _ffn_fwd(zh1, hb, p2d, tgt, g1, b1, g2, b2, wg4, wu4, wd4, wpe, wpg):
    t = zh1.shape[0]
    tm = 256

    def body(zh_ref, hb_ref, p_ref, t_ref, g1_ref, b1_ref, g2_ref, b2_ref,
             wg_hbm, wu_hbm, wd_hbm, wpe_hbm, wpg_hbm,
             dz_ref, gs_ref, us_ref, pg_ref, ple_ref, loss_ref, dg2_ref, db2_ref,
             wg, wu, wd, wpe, wpg):
        step = pl.program_id(0)
        _load_resident(step, [(wg_hbm, wg), (wu_hbm, wu), (wd_hbm, wd), (wpe_hbm, wpe), (wpg_hbm, wpg)])

        @pl.when(step == 0)
        def _():
            loss_ref[...] = jnp.zeros_like(loss_ref)
            dg2_ref[...] = jnp.zeros_like(dg2_ref)
            db2_ref[...] = jnp.zeros_like(db2_ref)

        h1 = zh_ref[...] * g1_ref[...] + b1_ref[...]
        hbv = hb_ref[...]
        ffn = jnp.zeros((tm, D_MODEL), F32)
        for j in range(N_SHARD):
            gj = _dot_nt(hbv, wg[j])
            uj = _dot_nt(hbv, wu[j])
            gs_ref[j] = gj.astype(BF16)
            us_ref[j] = uj.astype(BF16)
            act = (gj * _sigmoid(gj) * uj).astype(BF16)
            ffn = ffn + _dot(act, wd[j])
        ple = _dot(p_ref[...].astype(BF16), wpe[...])
        pg = _sigmoid(_dot(hbv, wpg[...]))
        pg_ref[...] = pg.astype(BF16)
        ple_ref[...] = ple.astype(BF16)
        z2 = ALPHA * h1 + ffn + pg * ple
        mu = jnp.mean(z2, axis=1, keepdims=True)
        zc = z2 - mu
        var = jnp.mean(zc * zc, axis=1, keepdims=True)
        r = lax.rsqrt(var + LN_EPS)
        zh2 = zc * r
        err = zh2 * g2_ref[...] + b2_ref[...] - t_ref[...]
        loss_ref[...] += jnp.sum(err * err)
        dy = err * (1.0 / D_MODEL)
        dg2_ref[...] += jnp.sum(dy * zh2, axis=0, keepdims=True)
        db2_ref[...] += jnp.sum(dy, axis=0, keepdims=True)
        dzh = dy * g2_ref[...]
        m1 = jnp.mean(dzh, axis=1, keepdims=True)
        m2 = jnp.mean(dzh * zh2, axis=1, keepdims=True)
        dz_ref[...] = r * (dzh - m1 - zh2 * m2)

    row = lambda w: pl.BlockSpec((tm, w), lambda i: (i, 0))
    const = lambda s: pl.BlockSpec(s, lambda i: (0, 0))
    sh = pl.BlockSpec((N_SHARD, tm, FFN_SHARD), lambda i: (0, i, 0))
    hbm = pl.BlockSpec(memory_space=pl.ANY)
    return pl.pallas_call(
        body, name="ffn_fwd", grid=(t // tm,),
        in_specs=[row(D_MODEL), row(D_MODEL), row(PLE_DIM), row(D_MODEL),
                  const((1, D_MODEL)), const((1, D_MODEL)), const((1, D_MODEL)), const((1, D_MODEL)),
                  hbm, hbm, hbm, hbm, hbm],
        out_specs=[row(D_MODEL), sh, sh, row(D_MODEL), row(D_MODEL),
                   const((8, LANES)), const((1, D_MODEL)), const((1, D_MODEL))],
        out_shape=[jax.ShapeDtypeStruct((t, D_MODEL), F32),
                   jax.ShapeDtypeStruct((N_SHARD, t, FFN_SHARD), BF16),
                   jax.ShapeDtypeStruct((N_SHARD, t, FFN_SHARD), BF16),
                   jax.ShapeDtypeStruct((t, D_MODEL), BF16), jax.ShapeDtypeStruct((t, D_MODEL), BF16),
                   jax.ShapeDtypeStruct((8, LANES), F32),
                   jax.ShapeDtypeStruct((1, D_MODEL), F32), jax.ShapeDtypeStruct((1, D_MODEL), F32)],
        scratch_shapes=[pltpu.VMEM(wg4.shape, BF16), pltpu.VMEM(wu4.shape, BF16), pltpu.VMEM(wd4.shape, BF16),
                        pltpu.VMEM(wpe.shape, BF16), pltpu.VMEM(wpg.shape, BF16)],
        compiler_params=_params("arbitrary", vmem=VMEM_LIMIT),
    )(zh1, hb, p2d, tgt, g1, b1, g2, b2, wg4, wu4, wd4, wpe, wpg)


def _ret_tables(lgf, lgb):
    c = CHUNK
    row = lax.broadcasted_iota(jnp.int32, (c, LANES), 0).astype(F32)
    ii = lax.broadcasted_iota(jnp.int32, (c, c), 0).astype(F32)
    jj = lax.broadcasted_iota(jnp.int32, (c, c), 1).astype(F32)
    diff = ii - jj
    dmats = []
    for h in range(2):
        lf = lgf[:, h * HEAD_DIM:h * HEAD_DIM + 1]
        lb = lgb[:, h * HEAD_DIM:h * HEAD_DIM + 1]
        dmats.append(jnp.where(diff > 0, jnp.exp(lf * jnp.maximum(diff, 0.0)),
                               jnp.where(diff < 0, jnp.exp(lb * jnp.maximum(-diff, 0.0)), 2.0)))
    tab = dict(
        qdec_f=jnp.exp(lgf * (row + 1.0)), kdec_f=jnp.exp(lgf * (c - 1.0 - row)),
        qdec_b=jnp.exp(lgb * (c - row)), kdec_b=jnp.exp(lgb * row),
        cdec_f=jnp.exp(lgf * c), cdec_b=jnp.exp(lgb * c),
        d0=dmats[0], d1=dmats[1], row=row, diff=diff)
    r = lax.broadcasted_iota(jnp.int32, (LANES, LANES), 0) < HEAD_DIM
    cc = lax.broadcasted_iota(jnp.int32, (LANES, LANES), 1) < HEAD_DIM
    tab["bd"] = r == cc
    tab["m0"] = lax.broadcasted_iota(jnp.int32, (c, LANES), 1) < HEAD_DIM
    return tab


def _ret_specs(bsz, s):
    blk = lambda cb: pl.BlockSpec((bsz, s, LANES), lambda p, cb=cb: (0, 0, cb + p))
    lane = pl.BlockSpec((None, 1, LANES), lambda p: (p, 0, 0))
    gain = pl.BlockSpec((1, LANES), lambda p: (0, p))
    pair = pl.BlockSpec((bsz, s, LANES), lambda p: (0, 0, p))
    return blk, lane, gain, pair


def _ret_kv_states(tb, k_ref, v_ref, rb_ref, kvf_ref, n_chunk):
    c = CHUNK
    bsz = k_ref.shape[0]
    bd = tb["bd"]

    def step(i, rbs):
        n = n_chunk - 1 - i
        sl = pl.ds(pl.multiple_of(n * c, c), c)
        kfb = []
        for b in range(bsz):
            k32 = k_ref[b, sl, :].astype(F32)
            kfb.append(jnp.concatenate([k32 * tb["kdec_f"], k32 * tb["kdec_b"]], axis=1).astype(BF16))
        kvs = [_dot_tn(kfb[b], v_ref[b, sl, :]) for b in range(bsz)]
        new = []
        for b in range(bsz):
            rb_ref[b, n] = rbs[b]
            kvf_ref[b, n] = jnp.where(bd, kvs[b][0:LANES], 0.0)
            new.append(rbs[b] * tb["cdec_b"] + jnp.where(bd, kvs[b][LANES:], 0.0))
        return tuple(new)

    lax.fori_loop(0, n_chunk, step, tuple(jnp.zeros((LANES, LANES), F32) for _ in range(bsz)))


def _split_rows(x, m0):
    return jnp.concatenate([jnp.where(m0, x, 0.0), jnp.where(m0, 0.0, x)], axis=0).astype(BF16)


def _ret_fwd(u3, lgf_l, lgb_l, gn_gain, rider=None):
    bsz, s, _ = u3.shape
    n_chunk = s // CHUNK
    c = CHUNK

    def body(q_ref, k_ref, v_ref, g_ref, lgf_ref, lgb_ref, gain_ref, y_ref, o_ref, rb_ref, kvf_ref):
        tb = _ret_tables(lgf_ref[...], lgb_ref[...])
        m0 = tb["m0"]
        gain = gain_ref[...]
        rows = range(bsz)
        _ret_kv_states(tb, k_ref, v_ref, rb_ref, kvf_ref, n_chunk)

        def chunk(n, rfs):
            sl = pl.ds(pl.multiple_of(n * c, c), c)
            qs = [q_ref[b, sl, :].astype(F32) * 0.125 for b in rows]
            s01 = [_dot_nt(_split_rows(qs[b], m0), k_ref[b, sl, :]) for b in rows]
            ys = []
            for b in rows:
                lhs = jnp.concatenate([s01[b][0:c] * tb["d0"], s01[b][c:] * tb["d1"],
                                       qs[b] * tb["qdec_f"], qs[b] * tb["qdec_b"]], axis=1).astype(BF16)
                rhs = jnp.concatenate([_split_rows(v_ref[b, sl, :].astype(F32), m0),
                                       rfs[b].astype(BF16), rb_ref[b, n].astype(BF16)], axis=0)
                ys.append(_dot(lhs, rhs))
            new = []
            for b in rows:
                y = ys[b]
                mu = _head_mean(y, m0)
                yc = y - mu
                var = _head_mean(yc * yc, m0)
                yh = yc * lax.rsqrt(var + GN_EPS)
                g = g_ref[b, sl, :].astype(F32)
                y_ref[b, sl, :] = y
                o_ref[b, sl, :] = (yh * gain * (g * _sigmoid(g))).astype(BF16)
                new.append(rfs[b] * tb["cdec_f"] + kvf_ref[b, n])
            return tuple(new)

        lax.fori_loop(0, n_chunk, chunk, tuple(jnp.zeros((LANES, LANES), F32) for _ in rows))

    blk, lane, gain, pair = _ret_specs(bsz, s)
    state = pltpu.VMEM((bsz, n_chunk, LANES, LANES), F32)
    return _hosted_call(
        body, "ret_fwd", (4,),
        in_specs=[blk(CB_RQ), blk(CB_RK), blk(CB_RV), blk(CB_RG), lane, lane, gain],
        out_specs=[pair, pair],
        out_shape=[jax.ShapeDtypeStruct((bsz, s, RET_WIDTH), F32), jax.ShapeDtypeStruct((bsz, s, RET_WIDTH), BF16)],
        scratch_shapes=[state, state],
        operands=(u3, u3, u3, u3, lgf_l, lgb_l, gn_gain), rider=rider)


def _ret_bwd(u3, y_pre, d_o, lgf_l, lgb_l, gn_gain, rider=None):
    bsz, s, _ = u3.shape
    n_chunk = s // CHUNK
    c = CHUNK

    def body(q_ref, k_ref, v_ref, g_ref, y_ref, do_ref, lgf_ref, lgb_ref, gain_ref,
             dq_ref, dk_ref, dv_ref, dg_ref, part_ref,
             rb_ref, kvf_ref, rf_ref, dirf_ref, dy_ref, dk_acc, dv_acc, af0, af1, ab0, ab1, vec_ref):
        tb = _ret_tables(lgf_ref[...], lgb_ref[...])
        m0, bd, row = tb["m0"], tb["bd"], tb["row"]
        gain = gain_ref[...]
        wf = jnp.maximum(tb["diff"], 0.0)
        wb = jnp.maximum(-tb["diff"], 0.0)
        rows = range(bsz)
        zero_states = tuple(jnp.zeros((LANES, LANES), F32) for _ in rows)
        for ref in (af0, af1, ab0, ab1):
            ref[...] = jnp.zeros_like(ref)
        vec_ref[...] = jnp.zeros_like(vec_ref)
        _ret_kv_states(tb, k_ref, v_ref, rb_ref, kvf_ref, n_chunk)

        def sweep_fwd(n, carry):
            rfs, gbs = carry
            sl = pl.ds(pl.multiple_of(n * c, c), c)
            qs, ks, vs, dys, dybs, q01, k01, dy01 = [], [], [], [], [], [], [], []
            dgain = jnp.zeros((1, LANES), F32)
            for b in rows:
                q = q_ref[b, sl, :].astype(F32) * 0.125
                k = k_ref[b, sl, :]
                y = y_ref[b, sl, :]
                do = do_ref[b, sl, :].astype(F32)
                g = g_ref[b, sl, :].astype(F32)
                mu = _head_mean(y, m0)
                yc = y - mu
                rstd = lax.rsqrt(_head_mean(yc * yc, m0) + GN_EPS)
                yh = yc * rstd
                sg = _sigmoid(g)
                sil = g * sg
                dyh = do * gain * sil
                dg_ref[b, sl, :] = (do * yh * gain * sg * (1.0 + g * (1.0 - sg))).astype(BF16)
                dgain = dgain + jnp.sum(do * yh * sil, axis=0, keepdims=True)
                dy = rstd * (dyh - _head_mean(dyh, m0) - yh * _head_mean(dyh * yh, m0))
                dyb = dy.astype(BF16)
                dy_ref[b, sl, :] = dyb
                rf_ref[b, n] = rfs[b]
                qs.append(q)
                ks.append(k)
                vs.append(v_ref[b, sl, :])
                dys.append(dy)
                dybs.append(dyb)
                q01.append(_split_rows(q, m0))
                k01.append(_split_rows(k.astype(F32), m0))
                dy01.append(_split_rows(dy, m0))
            s01 = [_dot_nt(q01[b], ks[b]) for b in rows]
            da01 = [_dot_nt(dy01[b], vs[b]) for b in rows]
            rbn = [rb_ref[b, n] for b in rows]
            states = [jnp.concatenate([rfs[b], rbn[b]], axis=0).astype(BF16) for b in rows]
            dqc = [_dot_nt(dybs[b], states[b]) for b in rows]
            gbb = [gbs[b].astype(BF16) for b in rows]
            dkb = [_dot_nt(vs[b], gbb[b]) for b in rows]
            qfb = [jnp.concatenate([qs[b] * tb["qdec_f"], qs[b] * tb["qdec_b"]], axis=1) for b in rows]
            direct = [_dot_tn(qfb[b].astype(BF16), dybs[b]) for b in rows]
            ds_cat, ds_rows, a_rows = [], [], []
            for b in rows:
                a0 = s01[b][0:c] * tb["d0"]
                a1 = s01[b][c:] * tb["d1"]
                pr0 = da01[b][0:c] * a0
                pr1 = da01[b][c:] * a1
                af0[...] += pr0 * wf
                ab0[...] += pr0 * wb
                af1[...] += pr1 * wf
                ab1[...] += pr1 * wb
                ds0 = da01[b][0:c] * tb["d0"]
                ds1 = da01[b][c:] * tb["d1"]
                ds_cat.append(jnp.concatenate([ds0, ds1], axis=1).astype(BF16))
                ds_rows.append(jnp.concatenate([ds0, ds1], axis=0).astype(BF16))
                a_rows.append(jnp.concatenate([a0, a1], axis=0).astype(BF16))
            kbd = [ks[b].astype(F32) * tb["kdec_b"] for b in rows]
            dq_in = [_dot(ds_cat[b], k01[b]) for b in rows]
            dk_in = [_dot_tn(ds_rows[b], q01[b]) for b in rows]
            dv_in = [_dot_tn(a_rows[b], dy01[b]) for b in rows]
            dv_gb = [_dot(kbd[b].astype(BF16), gbb[b]) for b in rows]
            new_rf, new_gb = [], []
            dlf = jnp.zeros((1, LANES), F32)
            dlb = jnp.zeros((1, LANES), F32)
            for b in rows:
                dqf, dqb = dqc[b][:, 0:LANES], dqc[b][:, LANES:]
                qf, qb = qfb[b][:, 0:LANES], qfb[b][:, LANES:]
                dq = dq_in[b] + dqf * tb["qdec_f"] + dqb * tb["qdec_b"]
                dq_ref[b, sl, :] = (dq * 0.125).astype(BF16)
                dk_acc[b, sl, :] = dk_in[b] + dkb[b] * tb["kdec_b"]
                dv_acc[b, sl, :] = dv_in[b] + dv_gb[b]
                dlf = dlf + jnp.sum((row + 1.0) * qf * dqf, axis=0, keepdims=True)
                dlb = dlb + jnp.sum((c - row) * qb * dqb + row * kbd[b] * dkb[b], axis=0, keepdims=True)
                dlb = dlb + c * tb["cdec_b"] * jnp.sum(gbs[b] * rbn[b], axis=0, keepdims=True)
                dirf_ref[b, n] = jnp.where(bd, direct[b][0:LANES], 0.0)
                new_gb.append(jnp.where(bd, direct[b][LANES:], 0.0) + tb["cdec_b"] * gbs[b])
                new_rf.append(rfs[b] * tb["cdec_f"] + kvf_ref[b, n])
            vec_ref[0:1, :] += dlf
            vec_ref[1:2, :] += dlb
            vec_ref[6:7, :] += dgain
            return tuple(new_rf), tuple(new_gb)

        lax.fori_loop(0, n_chunk, sweep_fwd, (zero_states, zero_states))

        def sweep_bwd(i, gfs):
            n = n_chunk - 1 - i
            sl = pl.ds(pl.multiple_of(n * c, c), c)
            gfb = [gfs[b].astype(BF16) for b in rows]
            kfd = [k_ref[b, sl, :].astype(F32) * tb["kdec_f"] for b in rows]
            dkf = [_dot_nt(v_ref[b, sl, :], gfb[b]) for b in rows]
            dvf = [_dot(kfd[b].astype(BF16), gfb[b]) for b in rows]
            new = []
            dlf = jnp.zeros((1, LANES), F32)
            for b in rows:
                dk_ref[b, sl, :] = (dk_acc[b, sl, :] + dkf[b] * tb["kdec_f"]).astype(BF16)
                dv_ref[b, sl, :] = (dv_acc[b, sl, :] + dvf[b]).astype(BF16)
                dlf = dlf + jnp.sum((c - 1.0 - row) * kfd[b] * dkf[b], axis=0, keepdims=True)
                dlf = dlf + c * tb["cdec_f"] * jnp.sum(gfs[b] * rf_ref[b, n], axis=0, keepdims=True)
                new.append(dirf_ref[b, n] + tb["cdec_f"] * gfs[b])
            vec_ref[0:1, :] += dlf
            return tuple(new)

        lax.fori_loop(0, n_chunk, sweep_bwd, zero_states)
        vec_ref[2:3, :] = jnp.sum(af0[...], axis=0, keepdims=True)
        vec_ref[3:4, :] = jnp.sum(af1[...], axis=0, keepdims=True)
        vec_ref[4:5, :] = jnp.sum(ab0[...], axis=0, keepdims=True)
        vec_ref[5:6, :] = jnp.sum(ab1[...], axis=0, keepdims=True)
        part_ref[...] = vec_ref[...]

    blk, lane, gain, pair = _ret_specs(bsz, s)
    out_bf = jax.ShapeDtypeStruct((bsz, s, RET_WIDTH), BF16)
    state = pltpu.VMEM((bsz, n_chunk, LANES, LANES), F32)
    return _hosted_call(
        body, "ret_bwd", (4,),
        in_specs=[blk(CB_RQ), blk(CB_RK), blk(CB_RV), blk(CB_RG), pair, pair, lane, lane, gain],
        out_specs=[pair, pair, pair, pair, pl.BlockSpec((None, 8, LANES), lambda p: (p, 0, 0))],
        out_shape=[out_bf, out_bf, out_bf, out_bf, jax.ShapeDtypeStruct((4, 8, LANES), F32)],
        scratch_shapes=[state, state, state, state,
                        pltpu.VMEM((bsz, s, LANES), BF16), pltpu.VMEM((bsz, s, LANES), F32),
                        pltpu.VMEM((bsz, s, LANES), F32),
                        pltpu.VMEM((c, c), F32), pltpu.VMEM((c, c), F32), pltpu.VMEM((c, c), F32),
                        pltpu.VMEM((c, c), F32), pltpu.VMEM((8, LANES), F32)],
        operands=(u3, u3, u3, u3, y_pre, d_o, lgf_l, lgb_l, gn_gain), rider=rider)


def _attn_window_tables(n, s):
    qi = lax.broadcasted_iota(jnp.int32, (CHUNK, 3 * CHUNK), 0)
    kj = lax.broadcasted_iota(jnp.int32, (CHUNK, 3 * CHUNK), 1)
    dist = jnp.abs(kj - CHUNK - qi)
    kpos = n * CHUNK - CHUNK + kj
    valid = (dist <= CHUNK) & (kpos >= 0) & (kpos < s)
    return dist.astype(F32), valid


def _dup_kv_head(x, g):
    lane = lax.broadcasted_iota(jnp.int32, x.shape, 1)
    keep = (lane < HEAD_DIM) == (g == 0)
    xf = x.astype(F32)
    return jnp.where(keep, xf, pltpu.roll(xf, HEAD_DIM, 1))


def _attn_specs(s):
    q = pl.BlockSpec((None, s, 2 * LANES), lambda b, g: (b, 0, CB_AQ // 2 + g))
    k = pl.BlockSpec((None, s, LANES), lambda b, g: (b, 0, CB_AK))
    v = pl.BlockSpec((None, s, LANES), lambda b, g: (b, 0, CB_AV))
    grp = pl.BlockSpec((None, s, 2 * LANES), lambda b, g: (b, 0, g))
    smem = pl.BlockSpec(memory_space=pltpu.SMEM)
    return q, k, v, grp, smem


def _fill_padded(dst_ref, val, s):
    dst_ref[0:CHUNK, :] = jnp.zeros((CHUNK, LANES), dst_ref.dtype)
    dst_ref[CHUNK:CHUNK + s, :] = val.astype(dst_ref.dtype)
    dst_ref[CHUNK + s:2 * CHUNK + s, :] = jnp.zeros((CHUNK, LANES), dst_ref.dtype)


def _attn_probs(sc, slope, snk, dist, valid):
    sc = jnp.where(valid, sc - slope * dist, NEG_INF)
    m = jnp.maximum(jnp.max(sc, axis=1, keepdims=True), snk)
    e = jnp.exp(sc - m)
    es = jnp.exp(snk - m)
    inv = 1.0 / (jnp.sum(e, axis=1, keepdims=True) + es)
    return e * inv, es * inv


def _stack_heads(x2, m0):
    parts = []
    for pr in range(2):
        xp = x2[:, pr * LANES:(pr + 1) * LANES]
        parts += [jnp.where(m0, xp, 0.0), jnp.where(m0, 0.0, xp)]
    return jnp.concatenate(parts, axis=0).astype(BF16)


def _unstack_pair(x_all, pr, m0):
    return jnp.where(m0, x_all[(2 * pr) * CHUNK:(2 * pr + 1) * CHUNK], x_all[(2 * pr + 1) * CHUNK:(2 * pr + 2) * CHUNK])


def _attn_fwd(u3, slopes, sink, rider=None):
    bsz, s, _ = u3.shape
    n_blk = s // CHUNK

    def body(slope_ref, sink_ref, q_ref, k_ref, v_ref, o_ref, kp_ref, vp_ref):
        g = pl.program_id(1)
        _fill_padded(kp_ref, _dup_kv_head(k_ref[...], g), s)
        _fill_padded(vp_ref, _dup_kv_head(v_ref[...], g), s)
        m0 = lax.broadcasted_iota(jnp.int32, (CHUNK, LANES), 1) < HEAD_DIM

        def blk(n, carry):
            r0 = pl.multiple_of(n * CHUNK, CHUNK)
            kw = kp_ref[pl.ds(r0, 3 * CHUNK), :]
            vw = vp_ref[pl.ds(r0, 3 * CHUNK), :]
            dist, valid = _attn_window_tables(n, s)
            q_all = _stack_heads(q_ref[pl.ds(r0, CHUNK), :].astype(F32) * 0.125, m0)
            sc_all = _dot_nt(q_all, kw)
            probs = []
            for i in range(4):
                p, _ = _attn_probs(sc_all[i * CHUNK:(i + 1) * CHUNK], slope_ref[g * 4 + i], sink_ref[g * 4 + i],
                                   dist, valid)
                probs.append(p.astype(BF16))
            out_all = _dot(jnp.concatenate(probs, axis=0), vw)
            for pr in range(2):
                o_ref[pl.ds(r0, CHUNK), pr * LANES:(pr + 1) * LANES] = _unstack_pair(out_all, pr, m0).astype(BF16)
            return carry

        lax.fori_loop(0, n_blk, blk, 0)

    q, k, v, grp, smem = _attn_specs(s)
    return _hosted_call(
        body, "attn_fwd", (bsz, 2),
        in_specs=[smem, smem, q, k, v],
        out_specs=[grp],
        out_shape=[jax.ShapeDtypeStruct((bsz, s, ATTN_WIDTH), BF16)],
        scratch_shapes=[pltpu.VMEM((s + 2 * CHUNK, LANES), BF16), pltpu.VMEM((s + 2 * CHUNK, LANES), BF16)],
        operands=(slopes, sink, u3, u3, u3), rider=rider)


def _attn_bwd(u3, d_o, slopes, sink, rider=None):
    bsz, s, _ = u3.shape
    n_blk = s // CHUNK

    def body(slope_ref, sink_ref, q_ref, k_ref, v_ref, do_ref, dq_ref, dkv_ref, ds_ref,
             kp_ref, vp_ref, dk_acc, dv_acc):
        g = pl.program_id(1)
        _fill_padded(kp_ref, _dup_kv_head(k_ref[...], g), s)
        _fill_padded(vp_ref, _dup_kv_head(v_ref[...], g), s)
        dk_acc[...] = jnp.zeros_like(dk_acc)
        dv_acc[...] = jnp.zeros_like(dv_acc)
        m0 = lax.broadcasted_iota(jnp.int32, (CHUNK, LANES), 1) < HEAD_DIM

        def blk(n, dsink):
            r0 = pl.multiple_of(n * CHUNK, CHUNK)
            win = pl.ds(r0, 3 * CHUNK)
            kw = kp_ref[win, :]
            vw = vp_ref[win, :]
            dist, valid = _attn_window_tables(n, s)
            q_all = _stack_heads(q_ref[pl.ds(r0, CHUNK), :].astype(F32) * 0.125, m0)
            do_all = _stack_heads(do_ref[pl.ds(r0, CHUNK), :].astype(F32), m0)
            sc_all = _dot_nt(q_all, kw)
            dp_all = _dot_nt(do_all, vw)
            new_dsink, probs, dscs = [], [], []
            for i in range(4):
                rows = slice(i * CHUNK, (i + 1) * CHUNK)
                p, ps = _attn_probs(sc_all[rows], slope_ref[g * 4 + i], sink_ref[g * 4 + i], dist, valid)
                dp = dp_all[rows]
                delta = jnp.sum(p * dp, axis=1, keepdims=True)
                dscs.append((p * (dp - delta)).astype(BF16))
                probs.append(p.astype(BF16))
                dsh = jnp.sum(ps * delta, axis=0, keepdims=True)
                new_dsink.append(dsink[i] - jnp.broadcast_to(dsh, (1, LANES)))
            dsc_all = jnp.concatenate(dscs, axis=0)
            dq_all = _dot(dsc_all, kw)
            dk_acc[win, :] += _dot_tn(dsc_all, q_all)
            dv_acc[win, :] += _dot_tn(jnp.concatenate(probs, axis=0), do_all)
            for pr in range(2):
                dq_ref[pl.ds(r0, CHUNK), pr * LANES:(pr + 1) * LANES] = (
                    _unstack_pair(dq_all, pr, m0) * 0.125).astype(BF16)
            return tuple(new_dsink)

        dsink = lax.fori_loop(0, n_blk, blk, tuple(jnp.zeros((1, LANES), F32) for _ in range(4)))
        dk = dk_acc[CHUNK:CHUNK + s, :]
        dv = dv_acc[CHUNK:CHUNK + s, :]
        lane = lax.broadcasted_iota(jnp.int32, (s, LANES), 1)
        fold = lambda a: a + pltpu.roll(a, HEAD_DIM, 1)
        dkv_ref[...] = jnp.where(lane < HEAD_DIM, fold(dk), fold(dv)).astype(BF16)
        ds_ref[...] = jnp.zeros_like(ds_ref)
        for i in range(4):
            ds_ref[i:i + 1, :] = dsink[i]

    q, k, v, grp, smem = _attn_specs(s)
    return _hosted_call(
        body, "attn_bwd", (bsz, 2),
        in_specs=[smem, smem, q, k, v, grp],
        out_specs=[grp, pl.BlockSpec((None, s, LANES), lambda b, g: (b, 0, g)),
                   pl.BlockSpec((None, None, 8, LANES), lambda b, g: (b, g, 0, 0))],
        out_shape=[jax.ShapeDtypeStruct((bsz, s, ATTN_WIDTH), BF16), jax.ShapeDtypeStruct((bsz, s, 2 * LANES), BF16),
                   jax.ShapeDtypeStruct((bsz, 2, 8, LANES), F32)],
        scratch_shapes=[pltpu.VMEM((s + 2 * CHUNK, LANES), BF16), pltpu.VMEM((s + 2 * CHUNK, LANES), BF16),
                        pltpu.VMEM((s + 2 * CHUNK, LANES), F32), pltpu.VMEM((s + 2 * CHUNK, LANES), F32)],
        operands=(slopes, sink, u3, u3, u3, d_o), rider=rider)


def _ffn_bwd(dz2, gs, us, pg, ple, zh1, r1, g1, wg4, wu4, wd4, wpg, w_out):
    t = dz2.shape[0]
    tm = 256

    def body(dz_ref, gs_ref, us_ref, pg_ref, ple_ref, zh_ref, r_ref, g1_ref,
             wg_hbm, wu_hbm, wd_hbm, wpg_hbm, wo_hbm,
             dgs_ref, dus_ref, dsp_ref, dple_ref, dz1_ref, dyr_ref, dya_ref, dg1_ref, db1_ref,
             wg, wu, wd, wpg, wo):
        step = pl.program_id(0)
        _load_resident(step, [(wg_hbm, wg), (wu_hbm, wu), (wd_hbm, wd), (wpg_hbm, wpg), (wo_hbm, wo)])

        @pl.when(step == 0)
        def _():
            dg1_ref[...] = jnp.zeros_like(dg1_ref)
            db1_ref[...] = jnp.zeros_like(db1_ref)

        dz = dz_ref[...]
        dzb = dz.astype(BF16)
        dh = ALPHA * dz
        for j in range(N_SHARD):
            da = _dot_nt(dzb, wd[j])
            gj = gs_ref[j].astype(F32)
            uj = us_ref[j].astype(F32)
            sg = _sigmoid(gj)
            dgj = (da * uj * sg * (1.0 + gj * (1.0 - sg))).astype(BF16)
            duj = (da * gj * sg).astype(BF16)
            dgs_ref[j] = dgj
            dus_ref[j] = duj
            dh = dh + _dot(dgj, wg[j]) + _dot(duj, wu[j])
        pgv = pg_ref[...].astype(F32)
        plev = ple_ref[...].astype(F32)
        dple_ref[...] = (dz * pgv).astype(BF16)
        dsp = (dz * plev * pgv * (1.0 - pgv)).astype(BF16)
        dsp_ref[...] = dsp
        dh = dh + _dot_nt(dsp, wpg[...])
        zh = zh_ref[...]
        dg1_ref[...] += jnp.sum(dh * zh, axis=0, keepdims=True)
        db1_ref[...] += jnp.sum(dh, axis=0, keepdims=True)
        dzh = dh * g1_ref[...]
        m1 = jnp.mean(dzh, axis=1, keepdims=True)
        m2 = jnp.mean(dzh * zh, axis=1, keepdims=True)
        dz1 = r_ref[...] * (dzh - m1 - zh * m2)
        dz1_ref[...] = dz1
        dyc = _dot_nt(dz1.astype(BF16), wo[...])
        dyr_ref[...] = dyc[:, 0:RET_WIDTH].astype(BF16)
        dya_ref[...] = dyc[:, RET_WIDTH:].astype(BF16)

    row = lambda w: pl.BlockSpec((tm, w), lambda i: (i, 0))
    const = lambda s: pl.BlockSpec(s, lambda i: (0, 0))
    sh = pl.BlockSpec((N_SHARD, tm, FFN_SHARD), lambda i: (0, i, 0))
    hbm = pl.BlockSpec(memory_space=pl.ANY)
    sh_shape = jax.ShapeDtypeStruct((N_SHARD, t, FFN_SHARD), BF16)
    return pl.pallas_call(
        body, name="ffn_bwd", grid=(t // tm,),
        in_specs=[row(D_MODEL), sh, sh, row(D_MODEL), row(D_MODEL), row(D_MODEL), row(1), const((1, D_MODEL)),
                  hbm, hbm, hbm, hbm, hbm],
        out_specs=[sh, sh, row(D_MODEL), row(D_MODEL), row(D_MODEL), row(RET_WIDTH), row(ATTN_WIDTH),
                   const((1, D_MODEL)), const((1, D_MODEL))],
        out_shape=[sh_shape, sh_shape, jax.ShapeDtypeStruct((t, D_MODEL), BF16),
                   jax.ShapeDtypeStruct((t, D_MODEL), BF16), jax.ShapeDtypeStruct((t, D_MODEL), F32),
                   jax.ShapeDtypeStruct((t, RET_WIDTH), BF16), jax.ShapeDtypeStruct((t, ATTN_WIDTH), BF16),
                   jax.ShapeDtypeStruct((1, D_MODEL), F32), jax.ShapeDtypeStruct((1, D_MODEL), F32)],
        scratch_shapes=[pltpu.VMEM(wg4.shape, BF16), pltpu.VMEM(wu4.shape, BF16), pltpu.VMEM(wd4.shape, BF16),
                        pltpu.VMEM(wpg.shape, BF16), pltpu.VMEM(w_out.shape, BF16)],
        compiler_params=_params("arbitrary", vmem=VMEM_LIMIT),
    )(dz2, gs, us, pg, ple, zh1, r1, g1, wg4, wu4, wd4, wpg, w_out)


def _wgrad_misc(y_ret, y_att, dz1, hb, dsp, p2d, dple):
    t = dz1.shape[0]
    tk = min(t, 512)

    def body(yr_ref, ya_ref, dz_ref, hb_ref, dsp_ref, p_ref, dple_ref, wo_ref, wpg_ref, wpe_ref):
        @pl.when(pl.program_id(0) == 0)
        def _():
            wo_ref[...] = jnp.zeros_like(wo_ref)
            wpg_ref[...] = jnp.zeros_like(wpg_ref)
            wpe_ref[...] = jnp.zeros_like(wpe_ref)

        dzb = dz_ref[...].astype(BF16)
        wo_ref[0:RET_WIDTH, :] += _dot_tn(yr_ref[...], dzb)
        wo_ref[RET_WIDTH:, :] += _dot_tn(ya_ref[...], dzb)
        wpg_ref[...] += _dot_tn(hb_ref[...], dsp_ref[...])
        wpe_ref[...] += _dot_tn(p_ref[...].astype(BF16), dple_ref[...])

    row = lambda w: pl.BlockSpec((tk, w), lambda k: (k, 0))
    const = lambda s: pl.BlockSpec(s, lambda k: (0, 0))
    return pl.pallas_call(
        body, name="wgrad_misc", grid=(t // tk,),
        in_specs=[row(RET_WIDTH), row(ATTN_WIDTH), row(D_MODEL), row(D_MODEL), row(D_MODEL), row(PLE_DIM),
                  row(D_MODEL)],
        out_specs=[const((D_MODEL, D_MODEL)), const((D_MODEL, D_MODEL)), const((PLE_DIM, D_MODEL))],
        out_shape=[jax.ShapeDtypeStruct((D_MODEL, D_MODEL), F32), jax.ShapeDtypeStruct((D_MODEL, D_MODEL), F32),
                   jax.ShapeDtypeStruct((PLE_DIM, D_MODEL), F32)],
        compiler_params=_params("arbitrary", vmem=VMEM_LIMIT),
    )(y_ret, y_att, dz1, hb, dsp, p2d, dple)


def _wgrad_ffn(gs, us, dgs, dus, hb, dz2):
    t = dz2.shape[0]
    tk = min(t, 512)

    def body(g_ref, u_ref, dg_ref, du_ref, hb_ref, dz_ref, og_ref, ou_ref, od_ref):
        @pl.when(pl.program_id(1) == 0)
        def _():
            og_ref[...] = jnp.zeros_like(og_ref)
            ou_ref[...] = jnp.zeros_like(ou_ref)
            od_ref[...] = jnp.zeros_like(od_ref)

        hbv = hb_ref[...]
        og_ref[...] += _dot_tn(dg_ref[...], hbv)
        ou_ref[...] += _dot_tn(du_ref[...], hbv)
        gj = g_ref[...].astype(F32)
        act = (gj * _sigmoid(gj) * u_ref[...].astype(F32)).astype(BF16)
        od_ref[...] += _dot_tn(act, dz_ref[...].astype(BF16))

    a_spec = pl.BlockSpec((None, tk, FFN_SHARD), lambda j, k: (j, k, 0))
    b_spec = pl.BlockSpec((tk, D_MODEL), lambda j, k: (k, 0))
    o_spec = pl.BlockSpec((None, FFN_SHARD, D_MODEL), lambda j, k: (j, 0, 0))
    o_shape = jax.ShapeDtypeStruct((N_SHARD, FFN_SHARD, D_MODEL), F32)
    return pl.pallas_call(
        body, name="wgrad_ffn", grid=(N_SHARD, t // tk),
        in_specs=[a_spec, a_spec, a_spec, a_spec, b_spec, b_spec],
        out_specs=[o_spec, o_spec, o_spec], out_shape=[o_shape, o_shape, o_shape],
        compiler_params=_params("parallel", "arbitrary", vmem=VMEM_LIMIT),
    )(gs, us, dgs, dus, hb, dz2)


def _wgrad_in(pieces, x2d):
    t = x2d.shape[0]
    tk = min(t, 512)

    def body(p0, p1, p2, p3, p4, pkv, x_ref, o_ref):
        @pl.when(pl.program_id(0) == 0)
        def _():
            o_ref[...] = jnp.zeros_like(o_ref)

        xb = x_ref[...].astype(BF16)
        for i, ref in enumerate((p0, p1, p2, p3, p4)):
            o_ref[i * 512:(i + 1) * 512, :] += _dot_tn(ref[...], xb)
        o_ref[2560:IN_WIDTH, :] += _dot_tn(pkv[...], xb)

    row = lambda w: pl.BlockSpec((tk, w), lambda k: (k, 0))
    return pl.pallas_call(
        body, name="wgrad_in", grid=(t // tk,),
        in_specs=[row(512)] * 5 + [row(256), row(D_MODEL)],
        out_specs=pl.BlockSpec((IN_WIDTH, D_MODEL), lambda k: (0, 0)),
        out_shape=jax.ShapeDtypeStruct((IN_WIDTH, D_MODEL), F32),
        compiler_params=_params("arbitrary", vmem=VMEM_LIMIT),
    )(*pieces, x2d)


def _inproj_bwd(dz1, pieces, w_main, w_kv):
    t = dz1.shape[0]
    tm = 512

    def body(dz_ref, p0, p1, p2, p3, p4, pkv, wm_ref, wkv_ref, o_ref):
        acc = ALPHA * dz_ref[...]
        for i, ref in enumerate((p0, p1, p2, p3, p4)):
            acc = acc + _dot(ref[...], wm_ref[i * 512:(i + 1) * 512, :])
        o_ref[...] = acc + _dot(pkv[...], wkv_ref[...])

    row = lambda w: pl.BlockSpec((tm, w), lambda i: (i, 0))
    const = lambda s: pl.BlockSpec(s, lambda i: (0, 0))
    return pl.pallas_call(
        body, name="inproj_bwd", grid=(t // tm,),
        in_specs=[row(D_MODEL)] + [row(512)] * 5 + [row(256), const(w_main.shape), const(w_kv.shape)],
        out_specs=row(D_MODEL),
        out_shape=jax.ShapeDtypeStruct((t, D_MODEL), F32),
        compiler_params=_params("parallel", vmem=VMEM_LIMIT),
    )(dz1, *pieces, w_main, w_kv)


def _coords():
    return lax.axis_index("x"), lax.axis_index("y"), lax.axis_index("c")


def _chip_of(x, y, rel):
    return (1 - x if rel & 2 else x), (1 - y if rel & 1 else y)


def _all_gather_weights(shards):
    first = _gather_chips_rider(shards)
    second = _gather_pass_rider([jax.ShapeDtypeStruct((N_SHARD,) + s.shape, s.dtype) for s in shards], chained=True)
    return _run_riders("gather_weights", shards, first.out_shapes, [first, second])


def _run_riders(name, ins, out_shapes, riders):
    n_in, n_out = len(ins), len(out_shapes)

    def body(*refs):
        in_refs, out_refs = refs[:n_in], refs[n_in:n_in + n_out]
        k = n_in + n_out
        for r in riders:
            sems = refs[k:k + len(r.sems)]
            k += len(r.sems)
            r.start(in_refs, out_refs, sems)
            r.finish(in_refs, out_refs, sems)

    hbm = pl.BlockSpec(memory_space=pl.ANY)
    return pl.pallas_call(
        body, name=name, in_specs=[hbm] * n_in, out_specs=[hbm] * n_out, out_shape=list(out_shapes),
        scratch_shapes=[s for r in riders for s in r.sems],
    )(*ins)


def _gather_half(outs, w, chip, cc):
    h = outs[w].shape[1] // 2
    return outs[w].at[chip, pl.ds(cc * h, h), :]


def _gather_chips_rider(shards):
    nw = len(shards)

    def copies(ins, outs, sems):
        send, recv, lsend, lrecv = sems
        x, y, c = _coords()
        me = 2 * x + y
        own = [pltpu.make_async_remote_copy(
            src_ref=ins[w], dst_ref=outs[w].at[me], send_sem=lsend.at[w], recv_sem=lrecv.at[w],
            device_id=(x, y, 1 - c), device_id_type=MESH) for w in range(nw)]
        out, arrive = [], []
        for rel in (1, 2, 3):
            kx, ky = _chip_of(x, y, rel)
            for w in range(nw):
                h = shards[w].shape[0] // 2
                sem = dict(send_sem=send.at[w * 3 + rel - 1], recv_sem=recv.at[w * 3 + rel - 1],
                           device_id=(kx, ky, c), device_id_type=MESH)
                out.append(pltpu.make_async_remote_copy(
                    src_ref=ins[w].at[pl.ds(c * h, h), :], dst_ref=_gather_half(outs, w, me, c), **sem))
                theirs = _gather_half(outs, w, 2 * kx + ky, c)
                arrive.append(pltpu.make_async_remote_copy(src_ref=theirs, dst_ref=theirs, **sem))
        return own, out, arrive

    def start(ins, outs, sems):
        own, out, _ = copies(ins, outs, sems)
        for cp in own + out:
            cp.start()

    def finish(ins, outs, sems):
        own, out, arrive = copies(ins, outs, sems)
        for cp in arrive:
            cp.wait_recv()
        for cp in out:
            cp.wait_send()
        for cp in own:
            cp.wait()

    dma = pltpu.SemaphoreType.DMA
    return _Rider(shards, [jax.ShapeDtypeStruct((N_SHARD,) + s.shape, s.dtype) for s in shards],
                  [dma((3 * nw,)), dma((3 * nw,)), dma((nw,)), dma((nw,))], start, finish)


def _gather_pass_rider(gathered, chained=False):
    nw = len(gathered)

    def copies(outs, sems, cc):
        send, recv = sems
        x, y, c = _coords()
        res = []
        for rel in (1, 2, 3):
            kx, ky = _chip_of(x, y, rel)
            for w in range(nw):
                rows = _gather_half(outs, w, 2 * kx + ky, cc)
                res.append(pltpu.make_async_remote_copy(
                    src_ref=rows, dst_ref=rows, send_sem=send.at[w * 3 + rel - 1], recv_sem=recv.at[w * 3 + rel - 1],
                    device_id=(x, y, 1 - c), device_id_type=MESH))
        return res

    def start(ins, outs, sems):
        for cp in copies(outs, sems, lax.axis_index("c")):
            cp.start()

    def finish(ins, outs, sems):
        c = lax.axis_index("c")
        for cp in copies(outs, sems, 1 - c):
            cp.wait_recv()
        for cp in copies(outs, sems, c):
            cp.wait_send()

    dma = pltpu.SemaphoreType.DMA
    shapes = [jax.ShapeDtypeStruct(g.shape, g.dtype) for g in gathered]
    if chained:
        return _Rider([], [], [dma((3 * nw,)), dma((3 * nw,))], start, finish)
    return _Rider(gathered, shapes, [dma((3 * nw,)), dma((3 * nw,))], start, finish,
                  aliases={w: w for w in range(nw)})


def _exchange_halves_rider(parts):
    nw = len(parts)

    def copies(ins, outs, sems):
        send, recv = sems
        x, y, c = _coords()
        res = []
        for w in range(nw):
            h = parts[w].shape[1] // 2
            res.append(pltpu.make_async_remote_copy(
                src_ref=ins[w].at[:, pl.ds((1 - c) * h, h), :], dst_ref=outs[w],
                send_sem=send.at[w], recv_sem=recv.at[w], device_id=(x, y, 1 - c), device_id_type=MESH))
        return res

    def start(ins, outs, sems):
        for cp in copies(ins, outs, sems):
            cp.start()

    def finish(ins, outs, sems):
        for cp in copies(ins, outs, sems):
            cp.wait()

    dma = pltpu.SemaphoreType.DMA
    return _Rider(parts, [jax.ShapeDtypeStruct((N_SHARD, p.shape[1] // 2, p.shape[2]), F32) for p in parts],
                  [dma((nw,)), dma((nw,))], start, finish)


def _exchange_halves(parts):
    r = _exchange_halves_rider(parts)
    return _run_riders("exchange_halves", parts, r.out_shapes, [r])


def _add_halves(parts, theirs, c_arr):
    nw = len(parts)
    split = 2

    def body(c_ref, *refs):
        ins, oth = refs[:nw], refs[nw:2 * nw]
        o32, o16 = refs[2 * nw:3 * nw], refs[3 * nw:]
        for w in range(nw):
            sm = ins[w][...] + oth[w][...]
            o32[w][...] = sm
            o16[w][...] = sm.astype(BF16)

    in_specs, oth_specs, out_specs, shapes32, shapes16 = [], [], [], [], []
    for p in parts:
        hb = p.shape[1] // 2 // split
        in_specs.append(pl.BlockSpec((None, hb, p.shape[2]), lambda j, i, c_ref: (j, c_ref[0] * split + i, 0)))
        oth_specs.append(pl.BlockSpec((None, hb, p.shape[2]), lambda j, i, c_ref: (j, i, 0)))
        shapes32.append(jax.ShapeDtypeStruct((N_SHARD, p.shape[1] // 2, p.shape[2]), F32))
        shapes16.append(jax.ShapeDtypeStruct((N_SHARD, p.shape[1] // 2, p.shape[2]), BF16))
    return pl.pallas_call(
        body, name="add_halves",
        grid_spec=pltpu.PrefetchScalarGridSpec(
            num_scalar_prefetch=1, grid=(N_SHARD, split),
            in_specs=in_specs + oth_specs, out_specs=oth_specs + oth_specs),
        out_shape=shapes32 + shapes16,
        compiler_params=_params("parallel", "parallel", vmem=VMEM_LIMIT),
    )(c_arr, *parts, *theirs)


def _exchange_chips_rider(sums16):
    nw = len(sums16)

    def copies(ins, outs, sems):
        send, recv = sems
        x, y, c = _coords()
        res = []
        for rel in (1, 2, 3):
            kx, ky = _chip_of(x, y, rel)
            for w in range(nw):
                res.append(pltpu.make_async_remote_copy(
                    src_ref=ins[w].at[2 * kx + ky], dst_ref=outs[w].at[rel - 1],
                    send_sem=send.at[w * 3 + rel - 1], recv_sem=recv.at[w * 3 + rel - 1],
                    device_id=(kx, ky, c), device_id_type=MESH))
        return res

    def start(ins, outs, sems):
        for cp in copies(ins, outs, sems):
            cp.start()

    def finish(ins, outs, sems):
        for cp in copies(ins, outs, sems):
            cp.wait()

    dma = pltpu.SemaphoreType.DMA
    return _Rider(sums16, [jax.ShapeDtypeStruct((3,) + s.shape[1:], BF16) for s in sums16],
                  [dma((3 * nw,)), dma((3 * nw,))], start, finish)


def _exchange_chips(sums16):
    r = _exchange_chips_rider(sums16)
    return _run_riders("exchange_chips", sums16, r.out_shapes, [r])


def _add_chips(sums32, theirs, pos):
    nw = len(sums32)
    split = 2

    def body(pos_ref, *refs):
        ins, oth, outs = refs[:nw], refs[nw:2 * nw], refs[2 * nw:]
        for w in range(nw):
            acc = ins[w][...]
            for r in range(3):
                acc = acc + oth[w][r].astype(F32)
            outs[w][...] = acc

    in_specs, oth_specs, out_specs, shapes = [], [], [], []
    for s in sums32:
        hb = s.shape[1] // split
        in_specs.append(pl.BlockSpec((None, hb, s.shape[2]), lambda i, pos_ref: (pos_ref[0], i, 0)))
        oth_specs.append(pl.BlockSpec((3, hb, s.shape[2]), lambda i, pos_ref: (0, i, 0)))
        out_specs.append(pl.BlockSpec((hb, s.shape[2]), lambda i, pos_ref: (pos_ref[1] * split + i, 0)))
        shapes.append(jax.ShapeDtypeStruct((2 * s.shape[1], s.shape[2]), F32))
    return pl.pallas_call(
        body, name="add_chips",
        grid_spec=pltpu.PrefetchScalarGridSpec(
            num_scalar_prefetch=1, grid=(split,), in_specs=in_specs + oth_specs, out_specs=out_specs),
        out_shape=shapes,
        compiler_params=_params("parallel", vmem=VMEM_LIMIT),
    )(pos, *sums32, *theirs)


def _join_halves(shards):
    nw = len(shards)

    def body(*refs):
        outs = refs[nw:2 * nw]
        send, recv = refs[2 * nw:]
        x, y, c = _coords()

        def copy(w, cc):
            h = shards[w].shape[0] // 2
            rows = outs[w].at[pl.ds(cc * h, h), :]
            return pltpu.make_async_remote_copy(
                src_ref=rows, dst_ref=rows, send_sem=send.at[w], recv_sem=recv.at[w],
                device_id=(x, y, 1 - c), device_id_type=MESH)

        for w in range(nw):
            copy(w, c).start()
        for w in range(nw):
            copy(w, 1 - c).wait_recv()
            copy(w, c).wait_send()

    hbm = pl.BlockSpec(memory_space=pl.ANY)
    return pl.pallas_call(
        body, name="join_halves",
        in_specs=[hbm] * nw, out_specs=[hbm] * nw,
        out_shape=[jax.ShapeDtypeStruct(s.shape, F32) for s in shards],
        input_output_aliases={w: w for w in range(nw)},
        scratch_shapes=[pltpu.SemaphoreType.DMA((nw,)), pltpu.SemaphoreType.DMA((nw,))],
    )(*shards)


def _adamw_math(w, g, m, v):
    m = ADAM_B1 * m + (1.0 - ADAM_B1) * g
    v = ADAM_B2 * v + (1.0 - ADAM_B2) * (g * g)
    m_hat = m / (1.0 - ADAM_B1 ** ADAM_STEP)
    v_hat = v / (1.0 - ADAM_B2 ** ADAM_STEP)
    delta = -ADAM_LR * (m_hat / (jnp.sqrt(v_hat) + ADAM_EPS) + ADAM_WD * w)
    return delta, m, v


def _adamw(ws, gs, ms, vs):
    nw = len(ws)
    split = 8

    def body(*refs):
        w_r, g_r, m_r, v_r = (refs[i * nw:(i + 1) * nw] for i in range(4))
        d_o, m_o, v_o = (refs[(4 + i) * nw:(5 + i) * nw] for i in range(3))
        for k in range(nw):
            d, m, v = _adamw_math(w_r[k][...], g_r[k][...], m_r[k][...], v_r[k][...])
            d_o[k][...] = d
            m_o[k][...] = m
            v_o[k][...] = v

    specs = [pl.BlockSpec((w.shape[0] // split, w.shape[1]), lambda i: (i, 0)) for w in ws]
    shapes = [jax.ShapeDtypeStruct(w.shape, F32) for w in ws]
    outs = pl.pallas_call(
        body, name="adamw", grid=(split,),
        in_specs=specs * 4, out_specs=specs * 3, out_shape=shapes * 3,
        compiler_params=_params("parallel", vmem=VMEM_LIMIT),
    )(*ws, *gs, *ms, *vs)
    return outs[:nw], outs[nw:2 * nw], outs[2 * nw:]


SMALL_ROWS = 8
SMALL_COLS = D_MODEL
LOSS_COL = RET_WIDTH + 24


def _small_allreduce_adamw(part, w, m, v):
    def body(part_ref, w_ref, m_ref, v_ref, g_out, d_out, m_out, v_out, all_ref, send, recv):
        x, y, c = _coords()
        me = 4 * x + 2 * y + c
        all_ref[me] = part_ref[...]
        copies = []
        for rel in range(1, 8):
            px = 1 - x if rel & 4 else x
            py = 1 - y if rel & 2 else y
            pc = 1 - c if rel & 1 else c
            copies.append(pltpu.make_async_remote_copy(
                src_ref=part_ref, dst_ref=all_ref.at[me],
                send_sem=send.at[rel - 1], recv_sem=recv.at[rel - 1], device_id=(px, py, pc), device_id_type=MESH))
        for cp in copies:
            cp.start()
        for cp in copies:
            cp.wait()
        g = all_ref[0]
        for k in range(1, 8):
            g = g + all_ref[k]
        d, mn, vn = _adamw_math(w_ref[...], g, m_ref[...], v_ref[...])
        g_out[...] = g
        d_out[...] = d
        m_out[...] = mn
        v_out[...] = vn

    vm = pl.BlockSpec(memory_space=pltpu.VMEM)
    shape = jax.ShapeDtypeStruct((SMALL_ROWS, SMALL_COLS), F32)
    return pl.pallas_call(
        body, name="small_allreduce_adamw",
        in_specs=[vm] * 4, out_specs=[vm] * 4, out_shape=[shape] * 4,
        scratch_shapes=[pltpu.VMEM((8, SMALL_ROWS, SMALL_COLS), F32),
                        pltpu.SemaphoreType.DMA((7,)), pltpu.SemaphoreType.DMA((7,))],
    )(part, w, m, v)


SMALL_NAMES = ("ret_decay_fwd", "ret_decay_bwd", "attn_sink", "ret_gn_gain",
               "ln1_gain", "ln1_bias", "ln2_gain", "ln2_bias")


LN_NAMES = ("ln1_gain", "ln1_bias", "ln2_gain", "ln2_bias")


def _pack_small(vals, extra=None):
    tail = jnp.zeros((1, 1), F32) if extra is None else extra.reshape(1, 1)
    row4 = jnp.concatenate([vals["ret_gn_gain"], vals["ret_decay_fwd"], vals["ret_decay_bwd"], vals["attn_sink"],
                            tail, jnp.zeros((1, SMALL_COLS - LOSS_COL - 1), F32)], axis=1)
    rows = [vals[n] for n in LN_NAMES] + [row4, jnp.zeros((SMALL_ROWS - 5, SMALL_COLS), F32)]
    return jnp.concatenate(rows, axis=0)


def _unpack_small(packed):
    out = {n: packed[i:i + 1] for i, n in enumerate(LN_NAMES)}
    o = RET_WIDTH
    out.update(ret_gn_gain=packed[4:5, 0:o], ret_decay_fwd=packed[4:5, o:o + 8],
               ret_decay_bwd=packed[4:5, o + 8:o + 16], attn_sink=packed[4:5, o + 16:o + 24])
    return out


def _local_step(x, p, tgt, w_in_t, rest, small, core=None):
    bsz, s, _ = x.shape
    t = bsz * s
    x2d = x.reshape(t, D_MODEL)
    p2d = p.reshape(t, PLE_DIM)
    tgt2d = tgt.reshape(t, D_MODEL)
    dec_f = small["ret_decay_fwd"].reshape(8)
    dec_b = small["ret_decay_bwd"].reshape(8)
    lg_f = jnp.log1p(-jnp.exp2(dec_f))
    lg_b = jnp.log1p(-jnp.exp2(dec_b))
    per_lane = lambda v: jnp.repeat(v, HEAD_DIM).reshape(4, 1, LANES)
    lgf_l, lgb_l = per_lane(lg_f), per_lane(lg_b)
    sink = small["attn_sink"].reshape(8)
    slopes = 2.0 ** (-(jnp.arange(8, dtype=F32) + 1.0))
    gn_gain = small["ret_gn_gain"]
    g1, b1, g2, b2 = (small[n] for n in ("ln1_gain", "ln1_bias", "ln2_gain", "ln2_bias"))

    u = _inproj(x2d, w_in_t)
    u3 = u.reshape(bsz, s, IN_WIDTH)
    if core is None:
        wts = rest
        y_pre, y_ret = _ret_fwd(u3, lgf_l, lgb_l, gn_gain)
        (y_att,) = _attn_fwd(u3, slopes, sink)
    else:
        y_pre, y_ret, *from_chips = _ret_fwd(u3, lgf_l, lgb_l, gn_gain, rider=_gather_chips_rider(rest))
        y_att, *gathered = _attn_fwd(u3, slopes, sink, rider=_gather_pass_rider(from_chips))
        wts = _assemble_weights(dict(zip(REST_NAMES, gathered)))
    zh1, r1, hb = _outproj_ln1(y_ret.reshape(t, RET_WIDTH), y_att.reshape(t, ATTN_WIDTH), x2d, wts["w_out"], g1, b1)
    dz2, gs, us, pg, ple, sq, dg2, db2 = _ffn_fwd(zh1, hb, p2d, tgt2d, g1, b1, g2, b2, wts["gate4"], wts["up4"],
                                                 wts["down4"], wts["ple_proj"], wts["ple_gate"])
    dgs, dus, dsp, dple, dz1, dyr, dya, dg1, db1 = _ffn_bwd(dz2, gs, us, pg, ple, zh1, r1, g1, wts["gate4"],
                                                          wts["up4"], wts["down4"], wts["ple_gate"], wts["w_out"])
    d_gate4, d_up4, d_down4 = _wgrad_ffn(gs, us, dgs, dus, hb, dz2)
    d_w_out, d_ple_gate, d_ple_proj = _wgrad_misc(y_ret.reshape(t, RET_WIDTH), y_att.reshape(t, ATTN_WIDTH), dz1,
                                                  hb, dsp, p2d, dple)
    grads_rest = [
        d_w_out.reshape(N_SHARD, D_MODEL // N_SHARD, D_MODEL), d_gate4, d_up4, d_down4,
        d_ple_proj.reshape(PLE_DIM, N_SHARD, D_MODEL // N_SHARD).transpose(1, 0, 2),
        d_ple_gate.reshape(N_SHARD, D_MODEL // N_SHARD, D_MODEL)]
    dyr3, dya3 = dyr.reshape(bsz, s, RET_WIDTH), dya.reshape(bsz, s, ATTN_WIDTH)
    if core is None:
        drq, drk, drv, drg, rpart = _ret_bwd(u3, y_pre, dyr3, lgf_l, lgb_l, gn_gain)
        daq, dakv, spart = _attn_bwd(u3, dya3, slopes, sink)
    else:
        nr = len(grads_rest)
        drq, drk, drv, drg, rpart, *theirs = _ret_bwd(u3, y_pre, dyr3, lgf_l, lgb_l, gn_gain,
                                                      rider=_exchange_halves_rider(grads_rest))
        sums = _add_halves(grads_rest, theirs, core)
        daq, dakv, spart, *from_chips = _attn_bwd(u3, dya3, slopes, sink, rider=_exchange_chips_rider(sums[nr:]))
        grads_rest = (sums[:nr], from_chips)
    pieces = [a.reshape(t, -1) for a in (drq, drk, drv, drg, daq, dakv)]
    kv0 = CB_AK * LANES
    kv_order = (0, 128, 64, 192)
    w_kv = jnp.concatenate([w_in_t[kv0 + o:kv0 + o + HEAD_DIM] for o in kv_order], axis=0)
    grad_x = _inproj_bwd(dz1, pieces, w_in_t[:kv0], w_kv).reshape(bsz, s, D_MODEL)
    d_in = _wgrad_in(pieces, x2d)
    d_w_in_t = jnp.concatenate([d_in[:kv0]] + [d_in[kv0 + o:kv0 + o + HEAD_DIM] for o in kv_order], axis=0)

    rsum = rpart
    lane_heads = lambda row: jnp.sum(row.reshape(4, 2, HEAD_DIM), axis=-1).reshape(8)
    dlg_f = lane_heads(rsum[:, 0, :]) + jnp.stack([jnp.sum(rsum[:, 2, :], -1), jnp.sum(rsum[:, 3, :], -1)], 1).reshape(8)
    dlg_b = lane_heads(rsum[:, 1, :]) + jnp.stack([jnp.sum(rsum[:, 4, :], -1), jnp.sum(rsum[:, 5, :], -1)], 1).reshape(8)
    chain = lambda d: -(math.log(2.0) * jnp.exp2(d)) / (1.0 - jnp.exp2(d))
    grads_small = {
        "ret_decay_fwd": (dlg_f * chain(dec_f)).reshape(1, 8),
        "ret_decay_bwd": (dlg_b * chain(dec_b)).reshape(1, 8),
        "attn_sink": jnp.sum(spart, axis=0)[:, 0:4, 0].reshape(1, 8),
        "ret_gn_gain": rsum[:, 6, :].reshape(1, RET_WIDTH),
        "ln1_gain": dg1, "ln1_bias": db1, "ln2_gain": dg2, "ln2_bias": db2,
    }
    return sq[0, 0], grad_x, d_w_in_t.reshape(N_SHARD, FFN_SHARD, D_MODEL), grads_rest, grads_small


BIG_NAMES = ("w_in", "w_out", "w_ffn_gate", "w_ffn_up", "w_ffn_down", "w_ple_proj", "w_ple_gate")
REST_NAMES = BIG_NAMES[1:]
TRANSPOSED = ("w_in", "w_ffn_gate", "w_ffn_up")
WEIGHT_ORDER = ("w_in", "ret_decay_fwd", "ret_decay_bwd", "ret_gn_gain", "attn_sink", "w_out", "ln1_gain",
                "ln1_bias", "w_ffn_gate", "w_ffn_up", "w_ffn_down", "w_ple_proj", "w_ple_gate", "ln2_gain", "ln2_bias")


def _shard_rows(name, a):
    return jnp.swapaxes(a[0], 0, 1) if name in TRANSPOSED else a[0]


def _unshard_rows(name, a):
    return (jnp.swapaxes(a, 0, 1) if name in TRANSPOSED else a)[None]


def _assemble_weights(gathered):
    cols = lambda a: a.transpose(1, 0, 2).reshape(a.shape[1], N_SHARD * a.shape[2])
    rows = lambda a: a.reshape(N_SHARD * a.shape[1], a.shape[2])
    return {"w_out": rows(gathered["w_out"]),
            "gate4": gathered["w_ffn_gate"], "up4": gathered["w_ffn_up"], "down4": gathered["w_ffn_down"],
            "ple_proj": cols(gathered["w_ple_proj"]), "ple_gate": rows(gathered["w_ple_gate"])}


def kernel(x, p, w_in, ret_decay_fwd, ret_decay_bwd, ret_gn_gain, attn_sink, w_out, ln1_gain, ln1_bias, w_ffn_gate, w_ffn_up, w_ffn_down, w_ple_proj, w_ple_gate, ln2_gain, ln2_bias, loss_target, m_w_in, m_ret_decay_fwd, m_ret_decay_bwd, m_ret_gn_gain, m_attn_sink, m_w_out, m_ln1_gain, m_ln1_bias, m_w_ffn_gate, m_w_ffn_up, m_w_ffn_down, m_w_ple_proj, m_w_ple_gate, m_ln2_gain, m_ln2_bias, v_w_in, v_ret_decay_fwd, v_ret_decay_bwd, v_ret_gn_gain, v_attn_sink, v_w_out, v_ln1_gain, v_ln1_bias, v_w_ffn_gate, v_w_ffn_up, v_w_ffn_down, v_w_ple_proj, v_w_ple_gate, v_ln2_gain, v_ln2_bias):
    w = dict(w_in=w_in, ret_decay_fwd=ret_decay_fwd, ret_decay_bwd=ret_decay_bwd, ret_gn_gain=ret_gn_gain,
             attn_sink=attn_sink, w_out=w_out, ln1_gain=ln1_gain, ln1_bias=ln1_bias, w_ffn_gate=w_ffn_gate,
             w_ffn_up=w_ffn_up, w_ffn_down=w_ffn_down, w_ple_proj=w_ple_proj, w_ple_gate=w_ple_gate,
             ln2_gain=ln2_gain, ln2_bias=ln2_bias)
    m = dict(w_in=m_w_in, ret_decay_fwd=m_ret_decay_fwd, ret_decay_bwd=m_ret_decay_bwd, ret_gn_gain=m_ret_gn_gain,
             attn_sink=m_attn_sink, w_out=m_w_out, ln1_gain=m_ln1_gain, ln1_bias=m_ln1_bias, w_ffn_gate=m_w_ffn_gate,
             w_ffn_up=m_w_ffn_up, w_ffn_down=m_w_ffn_down, w_ple_proj=m_w_ple_proj, w_ple_gate=m_w_ple_gate,
             ln2_gain=m_ln2_gain, ln2_bias=m_ln2_bias)
    v = dict(w_in=v_w_in, ret_decay_fwd=v_ret_decay_fwd, ret_decay_bwd=v_ret_decay_bwd, ret_gn_gain=v_ret_gn_gain,
             attn_sink=v_attn_sink, w_out=v_w_out, ln1_gain=v_ln1_gain, ln1_bias=v_ln1_bias, w_ffn_gate=v_w_ffn_gate,
             w_ffn_up=v_w_ffn_up, w_ffn_down=v_w_ffn_down, w_ple_proj=v_w_ple_proj, w_ple_gate=v_w_ple_gate,
             ln2_gain=v_ln2_gain, ln2_bias=v_ln2_bias)
    big = lambda d: [_shard_rows(n, d[n]) for n in BIG_NAMES]
    small = lambda d: {n: d[n] for n in SMALL_NAMES}

    chip = 2 * lax.axis_index("x") + lax.axis_index("y")
    core = lax.axis_index("c")
    c_arr = core.astype(jnp.int32).reshape(1)
    pos = jnp.stack([chip, core]).astype(jnp.int32)

    shards = [a.astype(BF16) for a in big(w)]
    (w_in4,) = _all_gather_weights(shards[:1])
    w_in_t = w_in4.reshape(IN_WIDTH, D_MODEL)
    sq, grad_x, d_in, (sums32, from_chips), grads_small = _local_step(x, p[0], loss_target, w_in_t, shards[1:],
                                                                     small(w), core=c_arr)
    their_in = _exchange_halves([d_in])
    sum32_in, sum16_in = _add_halves([d_in], their_in, c_arr)
    chips_in = _exchange_chips([sum16_in])
    g_big = _join_halves(_add_chips([sum32_in] + list(sums32), list(chips_in) + list(from_chips), pos))
    d_big, m_big, v_big = _adamw(big(w), g_big, big(m), big(v))

    g_s, d_s, m_s, v_s = _small_allreduce_adamw(_pack_small(grads_small, sq), _pack_small(small(w)),
                                                _pack_small(small(m)), _pack_small(small(v)))
    loss = g_s[4, LOSS_COL] * (0.5 / D_MODEL)

    def tree(bigs, packed):
        out = {n: _unshard_rows(n, a) for n, a in zip(BIG_NAMES, bigs)}
        out.update(_unpack_small(packed))
        return [out[n] for n in WEIGHT_ORDER]

    return (loss, grad_x, *tree(g_big, g_s), *tree(d_big, d_s), *tree(m_big, m_s), *tree(v_big, v_s))
```

```python
import functools
import math

import jax
import jax.numpy as jnp
from jax import lax
from jax.experimental import pallas as pl
from jax.experimental.pallas import tpu as pltpu

F32 = jnp.float32
BF16 = jnp.bfloat16

D_MODEL = 1024
HEAD_DIM = 64
RET_HEADS = 8
ATTN_HEADS = 8
RET_WIDTH = 512
ATTN_WIDTH = 512
KV_WIDTH = 128
IN_WIDTH = 2816
FFN = 2816
N_SHARD = 4
FFN_SHARD = FFN // N_SHARD
PLE_DIM = 256
CHUNK = 128
LANES = 128
ALPHA = 2.0 ** 0.25
LN_EPS = 1e-5
GN_EPS = 1e-5
NEG_INF = -1e30
ADAM_LR = 0.001
ADAM_B1 = 0.9
ADAM_B2 = 0.999
ADAM_EPS = 1e-08
ADAM_WD = 0.01
ADAM_STEP = 10
VMEM_LIMIT = 56 * 1024 * 1024
MESH = pl.DeviceIdType.MESH

CB_RQ, CB_RK, CB_RV, CB_RG, CB_AQ, CB_AK, CB_AV = 0, 4, 8, 12, 16, 20, 21


def _dot(a, b):
    return jnp.dot(a, b, preferred_element_type=F32)


def _dot_nt(a, b):
    return lax.dot_general(a, b, (((1,), (1,)), ((), ())), preferred_element_type=F32)


def _dot_tn(a, b):
    return lax.dot_general(a, b, (((0,), (0,)), ((), ())), preferred_element_type=F32)


def _sigmoid(x):
    return 1.0 / (1.0 + jnp.exp(-x))


def _params(*sem, vmem=None):
    return pltpu.CompilerParams(dimension_semantics=tuple(sem) if sem else None, vmem_limit_bytes=vmem)


class _Rider:
    def __init__(self, ins, out_shapes, sems, start, finish, aliases=None):
        self.ins, self.out_shapes, self.sems = list(ins), list(out_shapes), list(sems)
        self.start, self.finish, self.aliases = start, finish, dict(aliases or {})


def _merge_riders(riders):
    riders = [r for r in riders if r is not None]
    if len(riders) == 1:
        return riders[0]
    bounds, aliases = [], {}
    i0 = o0 = s0 = 0
    for r in riders:
        bounds.append((i0, o0, s0))
        aliases.update({i0 + i: o0 + o for i, o in r.aliases.items()})
        i0, o0, s0 = i0 + len(r.ins), o0 + len(r.out_shapes), s0 + len(r.sems)

    def each(method):
        def run(ins, outs, sems):
            for r, (i, o, s) in zip(riders, bounds):
                getattr(r, method)(ins[i:i + len(r.ins)], outs[o:o + len(r.out_shapes)], sems[s:s + len(r.sems)])
        return run

    return _Rider([a for r in riders for a in r.ins], [a for r in riders for a in r.out_shapes],
                  [a for r in riders for a in r.sems], each("start"), each("finish"), aliases)


def _hosted_call(body, name, grid, in_specs, out_specs, out_shape, scratch_shapes, operands, rider=None,
                 semantics=None):
    n_in, n_out, n_scr = len(in_specs), len(out_specs), len(scratch_shapes)
    if rider is None:
        return pl.pallas_call(
            body, name=name, grid=grid, in_specs=in_specs, out_specs=out_specs, out_shape=out_shape,
            scratch_shapes=scratch_shapes,
            compiler_params=_params(*(semantics or ["parallel"] * len(grid)), vmem=VMEM_LIMIT))(*operands)
    r_in, r_out = len(rider.ins), len(rider.out_shapes)

    def full_body(*refs):
        main_in, rin = refs[:n_in], refs[n_in:n_in + r_in]
        o0 = n_in + r_in
        main_out, rout = refs[o0:o0 + n_out], refs[o0 + n_out:o0 + n_out + r_out]
        s0 = o0 + n_out + r_out
        main_scr, rsem = refs[s0:s0 + n_scr], refs[s0 + n_scr:]
        first = functools.reduce(jnp.logical_and, [pl.program_id(a) == 0 for a in range(len(grid))])
        last = functools.reduce(jnp.logical_and, [pl.program_id(a) == g - 1 for a, g in enumerate(grid)])

        @pl.when(first)
        def _():
            rider.start(rin, rout, rsem)

        body(*main_in, *main_out, *main_scr)

        @pl.when(last)
        def _():
            rider.finish(rin, rout, rsem)

    hbm = pl.BlockSpec(memory_space=pl.ANY)
    return pl.pallas_call(
        full_body, name=name, grid=grid,
        in_specs=list(in_specs) + [hbm] * r_in, out_specs=list(out_specs) + [hbm] * r_out,
        out_shape=list(out_shape) + rider.out_shapes,
        scratch_shapes=list(scratch_shapes) + rider.sems,
        input_output_aliases={n_in + i: n_out + o for i, o in rider.aliases.items()},
        compiler_params=_params(*(["arbitrary"] * len(grid)), vmem=VMEM_LIMIT),
    )(*operands, *rider.ins)


def _head_mean(x, m0):
    s0 = jnp.sum(jnp.where(m0, x, 0.0), axis=1, keepdims=True)
    s1 = jnp.sum(jnp.where(m0, 0.0, x), axis=1, keepdims=True)
    return jnp.where(m0, s0, s1) * (1.0 / HEAD_DIM)


def _inproj(x2d, w_in_t, rider=None):
    t = x2d.shape[0]
    tm = 512
    nb = 256

    def body(x_ref, w_ref, o_ref):
        xb = x_ref[...].astype(BF16)
        for n in range(0, IN_WIDTH, nb):
            o_ref[:, n:n + nb] = _dot_nt(xb, w_ref[n:n + nb, :]).astype(BF16)

    return _hosted_call(
        body, "inproj", (t // tm,),
        in_specs=[pl.BlockSpec((tm, D_MODEL), lambda i: (i, 0)),
                  pl.BlockSpec((IN_WIDTH, D_MODEL), lambda i: (0, 0))],
        out_specs=[pl.BlockSpec((tm, IN_WIDTH), lambda i: (i, 0))],
        out_shape=[jax.ShapeDtypeStruct((t, IN_WIDTH), BF16)],
        scratch_shapes=[], operands=(x2d, w_in_t), rider=rider)


def _outproj_ln1(y_ret, y_att, x2d, w_out, gain, bias, rider=None):
    t = x2d.shape[0]
    tm = 512

    def body(yr_ref, ya_ref, x_ref, w_ref, g_ref, b_ref, zh_ref, r_ref, hb_ref):
        mix = _dot(yr_ref[...], w_ref[0:RET_WIDTH, :]) + _dot(ya_ref[...], w_ref[RET_WIDTH:, :])
        z = ALPHA * x_ref[...] + mix
        mu = jnp.mean(z, axis=1, keepdims=True)
        zc = z - mu
        var = jnp.mean(zc * zc, axis=1, keepdims=True)
        r = lax.rsqrt(var + LN_EPS)
        zh = zc * r
        zh_ref[...] = zh
        r_ref[...] = r
        hb_ref[...] = (zh * g_ref[...] + b_ref[...]).astype(BF16)

    row = lambda w: pl.BlockSpec((tm, w), lambda i: (i, 0))
    const = lambda s: pl.BlockSpec(s, lambda i: (0, 0))
    return _hosted_call(
        body, "outproj_ln1", (t // tm,),
        in_specs=[row(RET_WIDTH), row(ATTN_WIDTH), row(D_MODEL), const((D_MODEL, D_MODEL)),
                  const((1, D_MODEL)), const((1, D_MODEL))],
        out_specs=[row(D_MODEL), row(1), row(D_MODEL)],
        out_shape=[jax.ShapeDtypeStruct((t, D_MODEL), F32), jax.ShapeDtypeStruct((t, 1), F32),
                   jax.ShapeDtypeStruct((t, D_MODEL), BF16)],
        scratch_shapes=[], operands=(y_ret, y_att, x2d, w_out, gain, bias), rider=rider)


def _load_resident(step, pairs):
    @pl.when(step == 0)
    def _():
        for src, dst in pairs:
            pltpu.sync_copy(src, dst)


def _ffn_fwd(zh1, hb, p2d, tgt, g1, b1, g2, b2, wg4, wu4, wd4, wpe, wpg):
    t = zh1.shape[0]
    tm = 256

    def body(zh_ref, hb_ref, p_ref, t_ref, g1_ref, b1_ref, g2_ref, b2_ref,
             wg_hbm, wu_hbm, wd_hbm, wpe_hbm, wpg_hbm,
             dz_ref, gs_ref, us_ref, pg_ref, ple_ref, loss_ref, dg2_ref, db2_ref,
             wg, wu, wd, wpe, wpg):
        step = pl.program_id(0)
        _load_resident(step, [(wg_hbm, wg), (wu_hbm, wu), (wd_hbm, wd), (wpe_hbm, wpe), (wpg_hbm, wpg)])

        @pl.when(step == 0)
        def _():
            loss_ref[...] = jnp.zeros_like(loss_ref)
            dg2_ref[...] = jnp.zeros_like(dg2_ref)
            db2_ref[...] = jnp.zeros_like(db2_ref)

        h1 = zh_ref[...] * g1_ref[...] + b1_ref[...]
        hbv = hb_ref[...]
        ffn = jnp.zeros((tm, D_MODEL), F32)
        for j in range(N_SHARD):
            gj = _dot_nt(hbv, wg[j])
            uj = _dot_nt(hbv, wu[j])
            gs_ref[j] = gj.astype(BF16)
            us_ref[j] = uj.astype(BF16)
            act = (gj * _sigmoid(gj) * uj).astype(BF16)
            ffn = ffn + _dot(act, wd[j])
        ple = _dot(p_ref[...].astype(BF16), wpe[...])
        pg = _sigmoid(_dot(hbv, wpg[...]))
        pg_ref[...] = pg.astype(BF16)
        ple_ref[...] = ple.astype(BF16)
        z2 = ALPHA * h1 + ffn + pg * ple
        mu = jnp.mean(z2, axis=1, keepdims=True)
        zc = z2 - mu
        var = jnp.mean(zc * zc, axis=1, keepdims=True)
        r = lax.rsqrt(var + LN_EPS)
        zh2 = zc * r
        err = zh2 * g2_ref[...] + b2_ref[...] - t_ref[...]
        loss_ref[...] += jnp.sum(err * err)
        dy = err * (1.0 / D_MODEL)
        dg2_ref[...] += jnp.sum(dy * zh2, axis=0, keepdims=True)
        db2_ref[...] += jnp.sum(dy, axis=0, keepdims=True)
        dzh = dy * g2_ref[...]
        m1 = jnp.mean(dzh, axis=1, keepdims=True)
        m2 = jnp.mean(dzh * zh2, axis=1, keepdims=True)
        dz_ref[...] = r * (dzh - m1 - zh2 * m2)

    row = lambda w: pl.BlockSpec((tm, w), lambda i: (i, 0))
    const = lambda s: pl.BlockSpec(s, lambda i: (0, 0))
    sh = pl.BlockSpec((N_SHARD, tm, FFN_SHARD), lambda i: (0, i, 0))
    hbm = pl.BlockSpec(memory_space=pl.ANY)
    return pl.pallas_call(
        body, name="ffn_fwd", grid=(t // tm,),
        in_specs=[row(D_MODEL), row(D_MODEL), row(PLE_DIM), row(D_MODEL),
                  const((1, D_MODEL)), const((1, D_MODEL)), const((1, D_MODEL)), const((1, D_MODEL)),
                  hbm, hbm, hbm, hbm, hbm],
        out_specs=[row(D_MODEL), sh, sh, row(D_MODEL), row(D_MODEL),
                   const((8, LANES)), const((1, D_MODEL)), const((1, D_MODEL))],
        out_shape=[jax.ShapeDtypeStruct((t, D_MODEL), F32),
                   jax.ShapeDtypeStruct((N_SHARD, t, FFN_SHARD), BF16),
                   jax.ShapeDtypeStruct((N_SHARD, t, FFN_SHARD), BF16),
                   jax.ShapeDtypeStruct((t, D_MODEL), BF16), jax.ShapeDtypeStruct((t, D_MODEL), BF16),
                   jax.ShapeDtypeStruct((8, LANES), F32),
                   jax.ShapeDtypeStruct((1, D_MODEL), F32), jax.ShapeDtypeStruct((1, D_MODEL), F32)],
        scratch_shapes=[pltpu.VMEM(wg4.shape, BF16), pltpu.VMEM(wu4.shape, BF16), pltpu.VMEM(wd4.shape, BF16),
                        pltpu.VMEM(wpe.shape, BF16), pltpu.VMEM(wpg.shape, BF16)],
        compiler_params=_params("arbitrary", vmem=VMEM_LIMIT),
    )(zh1, hb, p2d, tgt, g1, b1, g2, b2, wg4, wu4, wd4, wpe, wpg)


def _ret_tables(lgf, lgb):
    c = CHUNK
    row = lax.broadcasted_iota(jnp.int32, (c, LANES), 0).astype(F32)
    ii = lax.broadcasted_iota(jnp.int32, (c, c), 0).astype(F32)
    jj = lax.broadcasted_iota(jnp.int32, (c, c), 1).astype(F32)
    diff = ii - jj
    dmats = []
    for h in range(2):
        lf = lgf[:, h * HEAD_DIM:h * HEAD_DIM + 1]
        lb = lgb[:, h * HEAD_DIM:h * HEAD_DIM + 1]
        dmats.append(jnp.where(diff > 0, jnp.exp(lf * jnp.maximum(diff, 0.0)),
                               jnp.where(diff < 0, jnp.exp(lb * jnp.maximum(-diff, 0.0)), 2.0)))
    tab = dict(
        qdec_f=jnp.exp(lgf * (row + 1.0)), kdec_f=jnp.exp(lgf * (c - 1.0 - row)),
        qdec_b=jnp.exp(lgb * (c - row)), kdec_b=jnp.exp(lgb * row),
        cdec_f=jnp.exp(lgf * c), cdec_b=jnp.exp(lgb * c),
        d0=dmats[0], d1=dmats[1], row=row, diff=diff)
    r = lax.broadcasted_iota(jnp.int32, (LANES, LANES), 0) < HEAD_DIM
    cc = lax.broadcasted_iota(jnp.int32, (LANES, LANES), 1) < HEAD_DIM
    tab["bd"] = r == cc
    tab["m0"] = lax.broadcasted_iota(jnp.int32, (c, LANES), 1) < HEAD_DIM
    return tab


def _ret_specs(bsz, s):
    blk = lambda cb: pl.BlockSpec((bsz, s, LANES), lambda p, cb=cb: (0, 0, cb + p))
    lane = pl.BlockSpec((None, 1, LANES), lambda p: (p, 0, 0))
    gain = pl.BlockSpec((1, LANES), lambda p: (0, p))
    pair = pl.BlockSpec((bsz, s, LANES), lambda p: (0, 0, p))
    return blk, lane, gain, pair


def _ret_kv_states(tb, k_ref, v_ref, rb_ref, kvf_ref, n_chunk):
    c = CHUNK
    bsz = k_ref.shape[0]
    bd = tb["bd"]

    def step(i, rbs):
        n = n_chunk - 1 - i
        sl = pl.ds(pl.multiple_of(n * c, c), c)
        kfb = []
        for b in range(bsz):
            k32 = k_ref[b, sl, :].astype(F32)
            kfb.append(jnp.concatenate([k32 * tb["kdec_f"], k32 * tb["kdec_b"]], axis=1).astype(BF16))
        kvs = [_dot_tn(kfb[b], v_ref[b, sl, :]) for b in range(bsz)]
        new = []
        for b in range(bsz):
            rb_ref[b, n] = rbs[b]
            kvf_ref[b, n] = jnp.where(bd, kvs[b][0:LANES], 0.0)
            new.append(rbs[b] * tb["cdec_b"] + jnp.where(bd, kvs[b][LANES:], 0.0))
        return tuple(new)

    lax.fori_loop(0, n_chunk, step, tuple(jnp.zeros((LANES, LANES), F32) for _ in range(bsz)))


def _split_rows(x, m0):
    return jnp.concatenate([jnp.where(m0, x, 0.0), jnp.where(m0, 0.0, x)], axis=0).astype(BF16)


def _ret_fwd(u3, lgf_l, lgb_l, gn_gain, rider=None):
    bsz, s, _ = u3.shape
    n_chunk = s // CHUNK
    c = CHUNK

    def body(q_ref, k_ref, v_ref, g_ref, lgf_ref, lgb_ref, gain_ref, y_ref, o_ref, rb_ref, kvf_ref):
        tb = _ret_tables(lgf_ref[...], lgb_ref[...])
        m0 = tb["m0"]
        gain = gain_ref[...]
        rows = range(bsz)
        _ret_kv_states(tb, k_ref, v_ref, rb_ref, kvf_ref, n_chunk)

        def chunk(n, rfs):
            sl = pl.ds(pl.multiple_of(n * c, c), c)
            qs = [q_ref[b, sl, :].astype(F32) * 0.125 for b in rows]
            s01 = [_dot_nt(_split_rows(qs[b], m0), k_ref[b, sl, :]) for b in rows]
            ys = []
            for b in rows:
                lhs = jnp.concatenate([s01[b][0:c] * tb["d0"], s01[b][c:] * tb["d1"],
                                       qs[b] * tb["qdec_f"], qs[b] * tb["qdec_b"]], axis=1).astype(BF16)
                rhs = jnp.concatenate([_split_rows(v_ref[b, sl, :].astype(F32), m0),
                                       rfs[b].astype(BF16), rb_ref[b, n].astype(BF16)], axis=0)
                ys.append(_dot(lhs, rhs))
            new = []
            for b in rows:
                y = ys[b]
                mu = _head_mean(y, m0)
                yc = y - mu
                var = _head_mean(yc * yc, m0)
                yh = yc * lax.rsqrt(var + GN_EPS)
                g = g_ref[b, sl, :].astype(F32)
                y_ref[b, sl, :] = y
                o_ref[b, sl, :] = (yh * gain * (g * _sigmoid(g))).astype(BF16)
                new.append(rfs[b] * tb["cdec_f"] + kvf_ref[b, n])
            return tuple(new)

        lax.fori_loop(0, n_chunk, chunk, tuple(jnp.zeros((LANES, LANES), F32) for _ in rows))

    blk, lane, gain, pair = _ret_specs(bsz, s)
    state = pltpu.VMEM((bsz, n_chunk, LANES, LANES), F32)
    return _hosted_call(
        body, "ret_fwd", (4,),
        in_specs=[blk(CB_RQ), blk(CB_RK), blk(CB_RV), blk(CB_RG), lane, lane, gain],
        out_specs=[pair, pair],
        out_shape=[jax.ShapeDtypeStruct((bsz, s, RET_WIDTH), F32), jax.ShapeDtypeStruct((bsz, s, RET_WIDTH), BF16)],
        scratch_shapes=[state, state],
        operands=(u3, u3, u3, u3, lgf_l, lgb_l, gn_gain), rider=rider)


def _ret_bwd(u3, y_pre, d_o, lgf_l, lgb_l, gn_gain, rider=None):
    bsz, s, _ = u3.shape
    n_chunk = s // CHUNK
    c = CHUNK

    def body(q_ref, k_ref, v_ref, g_ref, y_ref, do_ref, lgf_ref, lgb_ref, gain_ref,
             dq_ref, dk_ref, dv_ref, dg_ref, part_ref,
             rb_ref, kvf_ref, rf_ref, dirf_ref, dy_ref, dk_acc, dv_acc, af0, af1, ab0, ab1, vec_ref):
        tb = _ret_tables(lgf_ref[...], lgb_ref[...])
        m0, bd, row = tb["m0"], tb["bd"], tb["row"]
        gain = gain_ref[...]
        wf = jnp.maximum(tb["diff"], 0.0)
        wb = jnp.maximum(-tb["diff"], 0.0)
        rows = range(bsz)
        zero_states = tuple(jnp.zeros((LANES, LANES), F32) for _ in rows)
        for ref in (af0, af1, ab0, ab1):
            ref[...] = jnp.zeros_like(ref)
        vec_ref[...] = jnp.zeros_like(vec_ref)
        _ret_kv_states(tb, k_ref, v_ref, rb_ref, kvf_ref, n_chunk)

        def sweep_fwd(n, carry):
            rfs, gbs = carry
            sl = pl.ds(pl.multiple_of(n * c, c), c)
            qs, ks, vs, dys, dybs, q01, k01, dy01 = [], [], [], [], [], [], [], []
            dgain = jnp.zeros((1, LANES), F32)
            for b in rows:
                q = q_ref[b, sl, :].astype(F32) * 0.125
                k = k_ref[b, sl, :]
                y = y_ref[b, sl, :]
                do = do_ref[b, sl, :].astype(F32)
                g = g_ref[b, sl, :].astype(F32)
                mu = _head_mean(y, m0)
                yc = y - mu
                rstd = lax.rsqrt(_head_mean(yc * yc, m0) + GN_EPS)
                yh = yc * rstd
                sg = _sigmoid(g)
                sil = g * sg
                dyh = do * gain * sil
                dg_ref[b, sl, :] = (do * yh * gain * sg * (1.0 + g * (1.0 - sg))).astype(BF16)
                dgain = dgain + jnp.sum(do * yh * sil, axis=0, keepdims=True)
                dy = rstd * (dyh - _head_mean(dyh, m0) - yh * _head_mean(dyh * yh, m0))
                dyb = dy.astype(BF16)
                dy_ref[b, sl, :] = dyb
                rf_ref[b, n] = rfs[b]
                qs.append(q)
                ks.append(k)
                vs.append(v_ref[b, sl, :])
                dys.append(dy)
                dybs.append(dyb)
                q01.append(_split_rows(q, m0))
                k01.append(_split_rows(k.astype(F32), m0))
                dy01.append(_split_rows(dy, m0))
            s01 = [_dot_nt(q01[b], ks[b]) for b in rows]
            da01 = [_dot_nt(dy01[b], vs[b]) for b in rows]
            rbn = [rb_ref[b, n] for b in rows]
            states = [jnp.concatenate([rfs[b], rbn[b]], axis=0).astype(BF16) for b in rows]
            dqc = [_dot_nt(dybs[b], states[b]) for b in rows]
            gbb = [gbs[b].astype(BF16) for b in rows]
            dkb = [_dot_nt(vs[b], gbb[b]) for b in rows]
            qfb = [jnp.concatenate([qs[b] * tb["qdec_f"], qs[b] * tb["qdec_b"]], axis=1) for b in rows]
            direct = [_dot_tn(qfb[b].astype(BF16), dybs[b]) for b in rows]
            ds_cat, ds_rows, a_rows = [], [], []
            for b in rows:
                a0 = s01[b][0:c] * tb["d0"]
                a1 = s01[b][c:] * tb["d1"]
                pr0 = da01[b][0:c] * a0
                pr1 = da01[b][c:] * a1
                af0[...] += pr0 * wf
                ab0[...] += pr0 * wb
                af1[...] += pr1 * wf
                ab1[...] += pr1 * wb
                ds0 = da01[b][0:c] * tb["d0"]
                ds1 = da01[b][c:] * tb["d1"]
                ds_cat.append(jnp.concatenate([ds0, ds1], axis=1).astype(BF16))
                ds_rows.append(jnp.concatenate([ds0, ds1], axis=0).astype(BF16))
                a_rows.append(jnp.concatenate([a0, a1], axis=0).astype(BF16))
            kbd = [ks[b].astype(F32) * tb["kdec_b"] for b in rows]
            dq_in = [_dot(ds_cat[b], k01[b]) for b in rows]
            dk_in = [_dot_tn(ds_rows[b], q01[b]) for b in rows]
            dv_in = [_dot_tn(a_rows[b], dy01[b]) for b in rows]
            dv_gb = [_dot(kbd[b].astype(BF16), gbb[b]) for b in rows]
            new_rf, new_gb = [], []
            dlf = jnp.zeros((1, LANES), F32)
            dlb = jnp.zeros((1, LANES), F32)
            for b in rows:
                dqf, dqb = dqc[b][:, 0:LANES], dqc[b][:, LANES:]
                qf, qb = qfb[b][:, 0:LANES], qfb[b][:, LANES:]
                dq = dq_in[b] + dqf * tb["qdec_f"] + dqb * tb["qdec_b"]
                dq_ref[b, sl, :] = (dq * 0.125).astype(BF16)
                dk_acc[b, sl, :] = dk_in[b] + dkb[b] * tb["kdec_b"]
                dv_acc[b, sl, :] = dv_in[b] + dv_gb[b]
                dlf = dlf + jnp.sum((row + 1.0) * qf * dqf, axis=0, keepdims=True)
                dlb = dlb + jnp.sum((c - row) * qb * dqb + row * kbd[b] * dkb[b], axis=0, keepdims=True)
                dlb = dlb + c * tb["cdec_b"] * jnp.sum(gbs[b] * rbn[b], axis=0, keepdims=True)
                dirf_ref[b, n] = jnp.where(bd, direct[b][0:LANES], 0.0)
                new_gb.append(jnp.where(bd, direct[b][LANES:], 0.0) + tb["cdec_b"] * gbs[b])
                new_rf.append(rfs[b] * tb["cdec_f"] + kvf_ref[b, n])
            vec_ref[0:1, :] += dlf
            vec_ref[1:2, :] += dlb
            vec_ref[6:7, :] += dgain
            return tuple(new_rf), tuple(new_gb)

        lax.fori_loop(0, n_chunk, sweep_fwd, (zero_states, zero_states))

        def sweep_bwd(i, gfs):
            n = n_chunk - 1 - i
            sl = pl.ds(pl.multiple_of(n * c, c), c)
            gfb = [gfs[b].astype(BF16) for b in rows]
            kfd = [k_ref[b, sl, :].astype(F32) * tb["kdec_f"] for b in rows]
            dkf = [_dot_nt(v_ref[b, sl, :], gfb[b]) for b in rows]
            dvf = [_dot(kfd[b].astype(BF16), gfb[b]) for b in rows]
            new = []
            dlf = jnp.zeros((1, LANES), F32)
            for b in rows:
                dk_ref[b, sl, :] = (dk_acc[b, sl, :] + dkf[b] * tb["kdec_f"]).astype(BF16)
                dv_ref[b, sl, :] = (dv_acc[b, sl, :] + dvf[b]).astype(BF16)
                dlf = dlf + jnp.sum((c - 1.0 - row) * kfd[b] * dkf[b], axis=0, keepdims=True)
                dlf = dlf + c * tb["cdec_f"] * jnp.sum(gfs[b] * rf_ref[b, n], axis=0, keepdims=True)
                new.append(dirf_ref[b, n] + tb["cdec_f"] * gfs[b])
            vec_ref[0:1, :] += dlf
            return tuple(new)

        lax.fori_loop(0, n_chunk, sweep_bwd, zero_states)
        vec_ref[2:3, :] = jnp.sum(af0[...], axis=0, keepdims=True)
        vec_ref[3:4, :] = jnp.sum(af1[...], axis=0, keepdims=True)
        vec_ref[4:5, :] = jnp.sum(ab0[...], axis=0, keepdims=True)
        vec_ref[5:6, :] = jnp.sum(ab1[...], axis=0, keepdims=True)
        part_ref[...] = vec_ref[...]

    blk, lane, gain, pair = _ret_specs(bsz, s)
    out_bf = jax.ShapeDtypeStruct((bsz, s, RET_WIDTH), BF16)
    state = pltpu.VMEM((bsz, n_chunk, LANES, LANES), F32)
    return _hosted_call(
        body, "ret_bwd", (4,),
        in_specs=[blk(CB_RQ), blk(CB_RK), blk(CB_RV), blk(CB_RG), pair, pair, lane, lane, gain],
        out_specs=[pair, pair, pair, pair, pl.BlockSpec((None, 8, LANES), lambda p: (p, 0, 0))],
        out_shape=[out_bf, out_bf, out_bf, out_bf, jax.ShapeDtypeStruct((4, 8, LANES), F32)],
        scratch_shapes=[state, state, state, state,
                        pltpu.VMEM((bsz, s, LANES), BF16), pltpu.VMEM((bsz, s, LANES), F32),
                        pltpu.VMEM((bsz, s, LANES), F32),
                        pltpu.VMEM((c, c), F32), pltpu.VMEM((c, c), F32), pltpu.VMEM((c, c), F32),
                        pltpu.VMEM((c, c), F32), pltpu.VMEM((8, LANES), F32)],
        operands=(u3, u3, u3, u3, y_pre, d_o, lgf_l, lgb_l, gn_gain), rider=rider)


def _attn_window_tables(n, s):
    qi = lax.broadcasted_iota(jnp.int32, (CHUNK, 3 * CHUNK), 0)
    kj = lax.broadcasted_iota(jnp.int32, (CHUNK, 3 * CHUNK), 1)
    dist = jnp.abs(kj - CHUNK - qi)
    kpos = n * CHUNK - CHUNK + kj
    valid = (dist <= CHUNK) & (kpos >= 0) & (kpos < s)
    return dist.astype(F32), valid


def _dup_kv_head(x, g):
    lane = lax.broadcasted_iota(jnp.int32, x.shape, 1)
    keep = (lane < HEAD_DIM) == (g == 0)
    xf = x.astype(F32)
    return jnp.where(keep, xf, pltpu.roll(xf, HEAD_DIM, 1))


def _attn_specs(s):
    q = pl.BlockSpec((None, s, 2 * LANES), lambda b, g: (b, 0, CB_AQ // 2 + g))
    k = pl.BlockSpec((None, s, LANES), lambda b, g: (b, 0, CB_AK))
    v = pl.BlockSpec((None, s, LANES), lambda b, g: (b, 0, CB_AV))
    grp = pl.BlockSpec((None, s, 2 * LANES), lambda b, g: (b, 0, g))
    smem = pl.BlockSpec(memory_space=pltpu.SMEM)
    return q, k, v, grp, smem


def _fill_padded(dst_ref, val, s):
    dst_ref[0:CHUNK, :] = jnp.zeros((CHUNK, LANES), dst_ref.dtype)
    dst_ref[CHUNK:CHUNK + s, :] = val.astype(dst_ref.dtype)
    dst_ref[CHUNK + s:2 * CHUNK + s, :] = jnp.zeros((CHUNK, LANES), dst_ref.dtype)


def _attn_probs(sc, slope, snk, dist, valid):
    sc = jnp.where(valid, sc - slope * dist, NEG_INF)
    m = jnp.maximum(jnp.max(sc, axis=1, keepdims=True), snk)
    e = jnp.exp(sc - m)
    es = jnp.exp(snk - m)
    inv = 1.0 / (jnp.sum(e, axis=1, keepdims=True) + es)
    return e * inv, es * inv


def _stack_heads(x2, m0):
    parts = []
    for pr in range(2):
        xp = x2[:, pr * LANES:(pr + 1) * LANES]
        parts += [jnp.where(m0, xp, 0.0), jnp.where(m0, 0.0, xp)]
    return jnp.concatenate(parts, axis=0).astype(BF16)


def _unstack_pair(x_all, pr, m0):
    return jnp.where(m0, x_all[(2 * pr) * CHUNK:(2 * pr + 1) * CHUNK], x_all[(2 * pr + 1) * CHUNK:(2 * pr + 2) * CHUNK])


def _attn_fwd(u3, slopes, sink, rider=None):
    bsz, s, _ = u3.shape
    n_blk = s // CHUNK

    def body(slope_ref, sink_ref, q_ref, k_ref, v_ref, o_ref, kp_ref, vp_ref):
        g = pl.program_id(1)
        _fill_padded(kp_ref, _dup_kv_head(k_ref[...], g), s)
        _fill_padded(vp_ref, _dup_kv_head(v_ref[...], g), s)
        m0 = lax.broadcasted_iota(jnp.int32, (CHUNK, LANES), 1) < HEAD_DIM

        def blk(n, carry):
            r0 = pl.multiple_of(n * CHUNK, CHUNK)
            kw = kp_ref[pl.ds(r0, 3 * CHUNK), :]
            vw = vp_ref[pl.ds(r0, 3 * CHUNK), :]
            dist, valid = _attn_window_tables(n, s)
            q_all = _stack_heads(q_ref[pl.ds(r0, CHUNK), :].astype(F32) * 0.125, m0)
            sc_all = _dot_nt(q_all, kw)
            probs = []
            for i in range(4):
                p, _ = _attn_probs(sc_all[i * CHUNK:(i + 1) * CHUNK], slope_ref[g * 4 + i], sink_ref[g * 4 + i],
                                   dist, valid)
                probs.append(p.astype(BF16))
            out_all = _dot(jnp.concatenate(probs, axis=0), vw)
            for pr in range(2):
                o_ref[pl.ds(r0, CHUNK), pr * LANES:(pr + 1) * LANES] = _unstack_pair(out_all, pr, m0).astype(BF16)
            return carry

        lax.fori_loop(0, n_blk, blk, 0)

    q, k, v, grp, smem = _attn_specs(s)
    return _hosted_call(
        body, "attn_fwd", (bsz, 2),
        in_specs=[smem, smem, q, k, v],
        out_specs=[grp],
        out_shape=[jax.ShapeDtypeStruct((bsz, s, ATTN_WIDTH), BF16)],
        scratch_shapes=[pltpu.VMEM((s + 2 * CHUNK, LANES), BF16), pltpu.VMEM((s + 2 * CHUNK, LANES), BF16)],
        operands=(slopes, sink, u3, u3, u3), rider=rider)


def _attn_bwd(u3, d_o, slopes, sink, rider=None):
    bsz, s, _ = u3.shape
    n_blk = s // CHUNK

    def body(slope_ref, sink_ref, q_ref, k_ref, v_ref, do_ref, dq_ref, dkv_ref, ds_ref,
             kp_ref, vp_ref, dk_acc, dv_acc):
        g = pl.program_id(1)
        _fill_padded(kp_ref, _dup_kv_head(k_ref[...], g), s)
        _fill_padded(vp_ref, _dup_kv_head(v_ref[...], g), s)
        dk_acc[...] = jnp.zeros_like(dk_acc)
        dv_acc[...] = jnp.zeros_like(dv_acc)
        m0 = lax.broadcasted_iota(jnp.int32, (CHUNK, LANES), 1) < HEAD_DIM

        def blk(n, dsink):
            r0 = pl.multiple_of(n * CHUNK, CHUNK)
            win = pl.ds(r0, 3 * CHUNK)
            kw = kp_ref[win, :]
            vw = vp_ref[win, :]
            dist, valid = _attn_window_tables(n, s)
            q_all = _stack_heads(q_ref[pl.ds(r0, CHUNK), :].astype(F32) * 0.125, m0)
            do_all = _stack_heads(do_ref[pl.ds(r0, CHUNK), :].astype(F32), m0)
            sc_all = _dot_nt(q_all, kw)
            dp_all = _dot_nt(do_all, vw)
            new_dsink, probs, dscs = [], [], []
            for i in range(4):
                rows = slice(i * CHUNK, (i + 1) * CHUNK)
                p, ps = _attn_probs(sc_all[rows], slope_ref[g * 4 + i], sink_ref[g * 4 + i], dist, valid)
                dp = dp_all[rows]
                delta = jnp.sum(p * dp, axis=1, keepdims=True)
                dscs.append((p * (dp - delta)).astype(BF16))
                probs.append(p.astype(BF16))
                dsh = jnp.sum(ps * delta, axis=0, keepdims=True)
                new_dsink.append(dsink[i] - jnp.broadcast_to(dsh, (1, LANES)))
            dsc_all = jnp.concatenate(dscs, axis=0)
            dq_all = _dot(dsc_all, kw)
            dk_acc[win, :] += _dot_tn(dsc_all, q_all)
            dv_acc[win, :] += _dot_tn(jnp.concatenate(probs, axis=0), do_all)
            for pr in range(2):
                dq_ref[pl.ds(r0, CHUNK), pr * LANES:(pr + 1) * LANES] = (
                    _unstack_pair(dq_all, pr, m0) * 0.125).astype(BF16)
            return tuple(new_dsink)

        dsink = lax.fori_loop(0, n_blk, blk, tuple(jnp.zeros((1, LANES), F32) for _ in range(4)))
        dk = dk_acc[CHUNK:CHUNK + s, :]
        dv = dv_acc[CHUNK:CHUNK + s, :]
        lane = lax.broadcasted_iota(jnp.int32, (s, LANES), 1)
        fold = lambda a: a + pltpu.roll(a, HEAD_DIM, 1)
        dkv_ref[...] = jnp.where(lane < HEAD_DIM, fold(dk), fold(dv)).astype(BF16)
        ds_ref[...] = jnp.zeros_like(ds_ref)
        for i in range(4):
            ds_ref[i:i + 1, :] = dsink[i]

    q, k, v, grp, smem = _attn_specs(s)
    return _hosted_call(
        body, "attn_bwd", (bsz, 2),
        in_specs=[smem, smem, q, k, v, grp],
        out_specs=[grp, pl.BlockSpec((None, s, LANES), lambda b, g: (b, 0, g)),
                   pl.BlockSpec((None, None, 8, LANES), lambda b, g: (b, g, 0, 0))],
        out_shape=[jax.ShapeDtypeStruct((bsz, s, ATTN_WIDTH), BF16), jax.ShapeDtypeStruct((bsz, s, 2 * LANES), BF16),
                   jax.ShapeDtypeStruct((bsz, 2, 8, LANES), F32)],
        scratch_shapes=[pltpu.VMEM((s + 2 * CHUNK, LANES), BF16), pltpu.VMEM((s + 2 * CHUNK, LANES), BF16),
                        pltpu.VMEM((s + 2 * CHUNK, LANES), F32), pltpu.VMEM((s + 2 * CHUNK, LANES), F32)],
        operands=(slopes, sink, u3, u3, u3, d_o), rider=rider)


def _ffn_bwd(dz2, gs, us, pg, ple, zh1, r1, g1, wg4, wu4, wd4, wpg, w_out):
    t = dz2.shape[0]
    tm = 256

    def body(dz_ref, gs_ref, us_ref, pg_ref, ple_ref, zh_ref, r_ref, g1_ref,
             wg_hbm, wu_hbm, wd_hbm, wpg_hbm, wo_hbm,
             dgs_ref, dus_ref, dsp_ref, dple_ref, dz1_ref, dyr_ref, dya_ref, dg1_ref, db1_ref,
             wg, wu, wd, wpg, wo):
        step = pl.program_id(0)
        _load_resident(step, [(wg_hbm, wg), (wu_hbm, wu), (wd_hbm, wd), (wpg_hbm, wpg), (wo_hbm, wo)])

        @pl.when(step == 0)
        def _():
            dg1_ref[...] = jnp.zeros_like(dg1_ref)
            db1_ref[...] = jnp.zeros_like(db1_ref)

        dz = dz_ref[...]
        dzb = dz.astype(BF16)
        dh = ALPHA * dz
        for j in range(N_SHARD):
            da = _dot_nt(dzb, wd[j])
            gj = gs_ref[j].astype(F32)
            uj = us_ref[j].astype(F32)
            sg = _sigmoid(gj)
            dgj = (da * uj * sg * (1.0 + gj * (1.0 - sg))).astype(BF16)
            duj = (da * gj * sg).astype(BF16)
            dgs_ref[j] = dgj
            dus_ref[j] = duj
            dh = dh + _dot(dgj, wg[j]) + _dot(duj, wu[j])
        pgv = pg_ref[...].astype(F32)
        plev = ple_ref[...].astype(F32)
        dple_ref[...] = (dz * pgv).astype(BF16)
        dsp = (dz * plev * pgv * (1.0 - pgv)).astype(BF16)
        dsp_ref[...] = dsp
        dh = dh + _dot_nt(dsp, wpg[...])
        zh = zh_ref[...]
        dg1_ref[...] += jnp.sum(dh * zh, axis=0, keepdims=True)
        db1_ref[...] += jnp.sum(dh, axis=0, keepdims=True)
        dzh = dh * g1_ref[...]
        m1 = jnp.mean(dzh, axis=1, keepdims=True)
        m2 = jnp.mean(dzh * zh, axis=1, keepdims=True)
        dz1 = r_ref[...] * (dzh - m1 - zh * m2)
        dz1_ref[...] = dz1
        dyc = _dot_nt(dz1.astype(BF16), wo[...])
        dyr_ref[...] = dyc[:, 0:RET_WIDTH].astype(BF16)
        dya_ref[...] = dyc[:, RET_WIDTH:].astype(BF16)

    row = lambda w: pl.BlockSpec((tm, w), lambda i: (i, 0))
    const = lambda s: pl.BlockSpec(s, lambda i: (0, 0))
    sh = pl.BlockSpec((N_SHARD, tm, FFN_SHARD), lambda i: (0, i, 0))
    hbm = pl.BlockSpec(memory_space=pl.ANY)
    sh_shape = jax.ShapeDtypeStruct((N_SHARD, t, FFN_SHARD), BF16)
    return pl.pallas_call(
        body, name="ffn_bwd", grid=(t // tm,),
        in_specs=[row(D_MODEL), sh, sh, row(D_MODEL), row(D_MODEL), row(D_MODEL), row(1), const((1, D_MODEL)),
                  hbm, hbm, hbm, hbm, hbm],
        out_specs=[sh, sh, row(D_MODEL), row(D_MODEL), row(D_MODEL), row(RET_WIDTH), row(ATTN_WIDTH),
                   const((1, D_MODEL)), const((1, D_MODEL))],
        out_shape=[sh_shape, sh_shape, jax.ShapeDtypeStruct((t, D_MODEL), BF16),
                   jax.ShapeDtypeStruct((t, D_MODEL), BF16), jax.ShapeDtypeStruct((t, D_MODEL), F32),
                   jax.ShapeDtypeStruct((t, RET_WIDTH), BF16), jax.ShapeDtypeStruct((t, ATTN_WIDTH), BF16),
                   jax.ShapeDtypeStruct((1, D_MODEL), F32), jax.ShapeDtypeStruct((1, D_MODEL), F32)],
        scratch_shapes=[pltpu.VMEM(wg4.shape, BF16), pltpu.VMEM(wu4.shape, BF16), pltpu.VMEM(wd4.shape, BF16),
                        pltpu.VMEM(wpg.shape, BF16), pltpu.VMEM(w_out.shape, BF16)],
        compiler_params=_params("arbitrary", vmem=VMEM_LIMIT),
    )(dz2, gs, us, pg, ple, zh1, r1, g1, wg4, wu4, wd4, wpg, w_out)


def _wgrad_misc(y_ret, y_att, dz1, hb, dsp, p2d, dple, rider=None):
    t = dz1.shape[0]
    tk = min(t, 512)

    def body(yr_ref, ya_ref, dz_ref, hb_ref, dsp_ref, p_ref, dple_ref, wo_ref, wpg_ref, wpe_ref):
        @pl.when(pl.program_id(0) == 0)
        def _():
            wo_ref[...] = jnp.zeros_like(wo_ref)
            wpg_ref[...] = jnp.zeros_like(wpg_ref)
            wpe_ref[...] = jnp.zeros_like(wpe_ref)

        dzb = dz_ref[...].astype(BF16)
        wo_ref[0:RET_WIDTH, :] += _dot_tn(yr_ref[...], dzb)
        wo_ref[RET_WIDTH:, :] += _dot_tn(ya_ref[...], dzb)
        wpg_ref[...] += _dot_tn(hb_ref[...], dsp_ref[...])
        wpe_ref[...] += _dot_tn(p_ref[...].astype(BF16), dple_ref[...])

    row = lambda w: pl.BlockSpec((tk, w), lambda k: (k, 0))
    const = lambda s: pl.BlockSpec(s, lambda k: (0, 0))
    return _hosted_call(
        body, "wgrad_misc", (t // tk,),
        in_specs=[row(RET_WIDTH), row(ATTN_WIDTH), row(D_MODEL), row(D_MODEL), row(D_MODEL), row(PLE_DIM),
                  row(D_MODEL)],
        out_specs=[const((D_MODEL, D_MODEL)), const((D_MODEL, D_MODEL)), const((PLE_DIM, D_MODEL))],
        out_shape=[jax.ShapeDtypeStruct((D_MODEL, D_MODEL), F32), jax.ShapeDtypeStruct((D_MODEL, D_MODEL), F32),
                   jax.ShapeDtypeStruct((PLE_DIM, D_MODEL), F32)],
        scratch_shapes=[], operands=(y_ret, y_att, dz1, hb, dsp, p2d, dple), rider=rider, semantics=["arbitrary"])


def _wgrad_ffn(gs, us, dgs, dus, hb, dz2):
    t = dz2.shape[0]
    tk = min(t, 512)

    def body(g_ref, u_ref, dg_ref, du_ref, hb_ref, dz_ref, og_ref, ou_ref, od_ref):
        @pl.when(pl.program_id(1) == 0)
        def _():
            og_ref[...] = jnp.zeros_like(og_ref)
            ou_ref[...] = jnp.zeros_like(ou_ref)
            od_ref[...] = jnp.zeros_like(od_ref)

        hbv = hb_ref[...]
        og_ref[...] += _dot_tn(dg_ref[...], hbv)
        ou_ref[...] += _dot_tn(du_ref[...], hbv)
        gj = g_ref[...].astype(F32)
        act = (gj * _sigmoid(gj) * u_ref[...].astype(F32)).astype(BF16)
        od_ref[...] += _dot_tn(act, dz_ref[...].astype(BF16))

    a_spec = pl.BlockSpec((None, tk, FFN_SHARD), lambda j, k: (j, k, 0))
    b_spec = pl.BlockSpec((tk, D_MODEL), lambda j, k: (k, 0))
    o_spec = pl.BlockSpec((None, FFN_SHARD, D_MODEL), lambda j, k: (j, 0, 0))
    o_shape = jax.ShapeDtypeStruct((N_SHARD, FFN_SHARD, D_MODEL), F32)
    return pl.pallas_call(
        body, name="wgrad_ffn", grid=(N_SHARD, t // tk),
        in_specs=[a_spec, a_spec, a_spec, a_spec, b_spec, b_spec],
        out_specs=[o_spec, o_spec, o_spec], out_shape=[o_shape, o_shape, o_shape],
        compiler_params=_params("parallel", "arbitrary", vmem=VMEM_LIMIT),
    )(gs, us, dgs, dus, hb, dz2)


KV_ORDER = (0, 128, 64, 192)


def _wgrad_in(pieces, x2d):
    t = x2d.shape[0]
    tk = min(t, 512)
    kv0 = CB_AK * LANES

    def body(p0, p1, p2, p3, p4, pkv, x_ref, o_ref):
        @pl.when(pl.program_id(0) == 0)
        def _():
            o_ref[...] = jnp.zeros_like(o_ref)

        xb = x_ref[...].astype(BF16)
        for i, ref in enumerate((p0, p1, p2, p3, p4)):
            o_ref[i * 512:(i + 1) * 512, :] += _dot_tn(ref[...], xb)
        dkv = _dot_tn(pkv[...], xb)
        for i, o in enumerate(KV_ORDER):
            o_ref[kv0 + o:kv0 + o + HEAD_DIM, :] += dkv[i * HEAD_DIM:(i + 1) * HEAD_DIM]

    row = lambda w: pl.BlockSpec((tk, w), lambda k: (k, 0))
    return pl.pallas_call(
        body, name="wgrad_in", grid=(t // tk,),
        in_specs=[row(512)] * 5 + [row(256), row(D_MODEL)],
        out_specs=pl.BlockSpec((IN_WIDTH, D_MODEL), lambda k: (0, 0)),
        out_shape=jax.ShapeDtypeStruct((IN_WIDTH, D_MODEL), F32),
        compiler_params=_params("arbitrary", vmem=VMEM_LIMIT),
    )(*pieces, x2d)


def _inproj_bwd(dz1, pieces, w_main, w_kv, rider=None):
    t = dz1.shape[0]
    tm = 512

    def body(dz_ref, p0, p1, p2, p3, p4, pkv, wm_ref, wkv_ref, o_ref):
        acc = ALPHA * dz_ref[...]
        for i, ref in enumerate((p0, p1, p2, p3, p4)):
            acc = acc + _dot(ref[...], wm_ref[i * 512:(i + 1) * 512, :])
        o_ref[...] = acc + _dot(pkv[...], wkv_ref[...])

    row = lambda w: pl.BlockSpec((tm, w), lambda i: (i, 0))
    const = lambda s: pl.BlockSpec(s, lambda i: (0, 0))
    return _hosted_call(
        body, "inproj_bwd", (t // tm,),
        in_specs=[row(D_MODEL)] + [row(512)] * 5 + [row(256), const(w_main.shape), const(w_kv.shape)],
        out_specs=[row(D_MODEL)],
        out_shape=[jax.ShapeDtypeStruct((t, D_MODEL), F32)],
        scratch_shapes=[], operands=(dz1, *pieces, w_main, w_kv), rider=rider)


def _coords():
    return lax.axis_index("x"), lax.axis_index("y"), lax.axis_index("c")


def _chip_of(x, y, rel):
    return (1 - x if rel & 2 else x), (1 - y if rel & 1 else y)


def _all_gather_weights(shards):
    first = _gather_chips_rider(shards)
    second = _gather_pass_rider([jax.ShapeDtypeStruct((N_SHARD,) + s.shape, s.dtype) for s in shards], chained=True)
    return _run_riders("gather_weights", shards, first.out_shapes, [first, second])


def _run_riders(name, ins, out_shapes, riders):
    n_in, n_out = len(ins), len(out_shapes)

    def body(*refs):
        in_refs, out_refs = refs[:n_in], refs[n_in:n_in + n_out]
        k = n_in + n_out
        for r in riders:
            sems = refs[k:k + len(r.sems)]
            k += len(r.sems)
            r.start(in_refs, out_refs, sems)
            r.finish(in_refs, out_refs, sems)

    hbm = pl.BlockSpec(memory_space=pl.ANY)
    return pl.pallas_call(
        body, name=name, in_specs=[hbm] * n_in, out_specs=[hbm] * n_out, out_shape=list(out_shapes),
        scratch_shapes=[s for r in riders for s in r.sems],
    )(*ins)


def _gather_half(outs, w, chip, cc):
    h = outs[w].shape[1] // 2
    return outs[w].at[chip, pl.ds(cc * h, h), :]


def _gather_chips_rider(shards):
    nw = len(shards)

    def copies(ins, outs, sems):
        send, recv, lsend, lrecv = sems
        x, y, c = _coords()
        me = 2 * x + y
        own = [pltpu.make_async_remote_copy(
            src_ref=ins[w], dst_ref=outs[w].at[me], send_sem=lsend.at[w], recv_sem=lrecv.at[w],
            device_id=(x, y, 1 - c), device_id_type=MESH) for w in range(nw)]
        out, arrive = [], []
        for rel in (1, 2, 3):
            kx, ky = _chip_of(x, y, rel)
            for w in range(nw):
                h = shards[w].shape[0] // 2
                sem = dict(send_sem=send.at[w * 3 + rel - 1], recv_sem=recv.at[w * 3 + rel - 1],
                           device_id=(kx, ky, c), device_id_type=MESH)
                out.append(pltpu.make_async_remote_copy(
                    src_ref=ins[w].at[pl.ds(c * h, h), :], dst_ref=_gather_half(outs, w, me, c), **sem))
                theirs = _gather_half(outs, w, 2 * kx + ky, c)
                arrive.append(pltpu.make_async_remote_copy(src_ref=theirs, dst_ref=theirs, **sem))
        return own, out, arrive

    def start(ins, outs, sems):
        own, out, _ = copies(ins, outs, sems)
        for cp in own + out:
            cp.start()

    def finish(ins, outs, sems):
        own, out, arrive = copies(ins, outs, sems)
        for cp in arrive:
            cp.wait_recv()
        for cp in out:
            cp.wait_send()
        for cp in own:
            cp.wait()

    dma = pltpu.SemaphoreType.DMA
    return _Rider(shards, [jax.ShapeDtypeStruct((N_SHARD,) + s.shape, s.dtype) for s in shards],
                  [dma((3 * nw,)), dma((3 * nw,)), dma((nw,)), dma((nw,))], start, finish)


def _gather_pass_rider(gathered, chained=False):
    nw = len(gathered)

    def copies(outs, sems, cc):
        send, recv = sems
        x, y, c = _coords()
        res = []
        for rel in (1, 2, 3):
            kx, ky = _chip_of(x, y, rel)
            for w in range(nw):
                rows = _gather_half(outs, w, 2 * kx + ky, cc)
                res.append(pltpu.make_async_remote_copy(
                    src_ref=rows, dst_ref=rows, send_sem=send.at[w * 3 + rel - 1], recv_sem=recv.at[w * 3 + rel - 1],
                    device_id=(x, y, 1 - c), device_id_type=MESH))
        return res

    def start(ins, outs, sems):
        for cp in copies(outs, sems, lax.axis_index("c")):
            cp.start()

    def finish(ins, outs, sems):
        c = lax.axis_index("c")
        for cp in copies(outs, sems, 1 - c):
            cp.wait_recv()
        for cp in copies(outs, sems, c):
            cp.wait_send()

    dma = pltpu.SemaphoreType.DMA
    shapes = [jax.ShapeDtypeStruct(g.shape, g.dtype) for g in gathered]
    if chained:
        return _Rider([], [], [dma((3 * nw,)), dma((3 * nw,))], start, finish)
    return _Rider(gathered, shapes, [dma((3 * nw,)), dma((3 * nw,))], start, finish,
                  aliases={w: w for w in range(nw)})


def _exchange_halves_rider(parts):
    nw = len(parts)

    def copies(ins, outs, sems):
        send, recv = sems
        x, y, c = _coords()
        res = []
        for w in range(nw):
            h = parts[w].shape[1] // 2
            res.append(pltpu.make_async_remote_copy(
                src_ref=ins[w].at[:, pl.ds((1 - c) * h, h), :], dst_ref=outs[w],
                send_sem=send.at[w], recv_sem=recv.at[w], device_id=(x, y, 1 - c), device_id_type=MESH))
        return res

    def start(ins, outs, sems):
        for cp in copies(ins, outs, sems):
            cp.start()

    def finish(ins, outs, sems):
        for cp in copies(ins, outs, sems):
            cp.wait()

    dma = pltpu.SemaphoreType.DMA
    return _Rider(parts, [jax.ShapeDtypeStruct((N_SHARD, p.shape[1] // 2, p.shape[2]), F32) for p in parts],
                  [dma((nw,)), dma((nw,))], start, finish)


def _add_halves(parts, theirs, c_arr):
    nw = len(parts)
    split = 2

    def body(c_ref, *refs):
        ins, oth = refs[:nw], refs[nw:2 * nw]
        o32, o16 = refs[2 * nw:3 * nw], refs[3 * nw:]
        for w in range(nw):
            sm = ins[w][...] + oth[w][...]
            o32[w][...] = sm
            o16[w][...] = sm.astype(BF16)

    in_specs, oth_specs, out_specs, shapes32, shapes16 = [], [], [], [], []
    for p in parts:
        hb = p.shape[1] // 2 // split
        in_specs.append(pl.BlockSpec((None, hb, p.shape[2]), lambda j, i, c_ref: (j, c_ref[0] * split + i, 0)))
        oth_specs.append(pl.BlockSpec((None, hb, p.shape[2]), lambda j, i, c_ref: (j, i, 0)))
        shapes32.append(jax.ShapeDtypeStruct((N_SHARD, p.shape[1] // 2, p.shape[2]), F32))
        shapes16.append(jax.ShapeDtypeStruct((N_SHARD, p.shape[1] // 2, p.shape[2]), BF16))
    return pl.pallas_call(
        body, name="add_halves",
        grid_spec=pltpu.PrefetchScalarGridSpec(
            num_scalar_prefetch=1, grid=(N_SHARD, split),
            in_specs=in_specs + oth_specs, out_specs=oth_specs + oth_specs),
        out_shape=shapes32 + shapes16,
        compiler_params=_params("parallel", "parallel", vmem=VMEM_LIMIT),
    )(c_arr, *parts, *theirs)


def _exchange_chips_rider(sums16):
    nw = len(sums16)

    def copies(ins, outs, sems):
        send, recv = sems
        x, y, c = _coords()
        res = []
        for rel in (1, 2, 3):
            kx, ky = _chip_of(x, y, rel)
            for w in range(nw):
                res.append(pltpu.make_async_remote_copy(
                    src_ref=ins[w].at[2 * kx + ky], dst_ref=outs[w].at[rel - 1],
                    send_sem=send.at[w * 3 + rel - 1], recv_sem=recv.at[w * 3 + rel - 1],
                    device_id=(kx, ky, c), device_id_type=MESH))
        return res

    def start(ins, outs, sems):
        for cp in copies(ins, outs, sems):
            cp.start()

    def finish(ins, outs, sems):
        for cp in copies(ins, outs, sems):
            cp.wait()

    dma = pltpu.SemaphoreType.DMA
    return _Rider(sums16, [jax.ShapeDtypeStruct((3,) + s.shape[1:], BF16) for s in sums16],
                  [dma((3 * nw,)), dma((3 * nw,))], start, finish)


def _exchange_chips(sums16):
    r = _exchange_chips_rider(sums16)
    return _run_riders("exchange_chips", sums16, r.out_shapes, [r])


def _add_chips(sums32, theirs, pos):
    nw = len(sums32)
    split = 2

    def body(pos_ref, *refs):
        ins, oth, outs = refs[:nw], refs[nw:2 * nw], refs[2 * nw:]
        for w in range(nw):
            acc = ins[w][...]
            for r in range(3):
                acc = acc + oth[w][r].astype(F32)
            outs[w][...] = acc

    in_specs, oth_specs, out_specs, shapes = [], [], [], []
    for s in sums32:
        hb = s.shape[1] // split
        in_specs.append(pl.BlockSpec((None, hb, s.shape[2]), lambda i, pos_ref: (pos_ref[0], i, 0)))
        oth_specs.append(pl.BlockSpec((3, hb, s.shape[2]), lambda i, pos_ref: (0, i, 0)))
        out_specs.append(pl.BlockSpec((hb, s.shape[2]), lambda i, pos_ref: (pos_ref[1] * split + i, 0)))
        shapes.append(jax.ShapeDtypeStruct((2 * s.shape[1], s.shape[2]), F32))
    return pl.pallas_call(
        body, name="add_chips",
        grid_spec=pltpu.PrefetchScalarGridSpec(
            num_scalar_prefetch=1, grid=(split,), in_specs=in_specs + oth_specs, out_specs=out_specs),
        out_shape=shapes,
        compiler_params=_params("parallel", vmem=VMEM_LIMIT),
    )(pos, *sums32, *theirs)


def _join_halves(shards):
    nw = len(shards)

    def body(*refs):
        outs = refs[nw:2 * nw]
        send, recv = refs[2 * nw:]
        x, y, c = _coords()

        def copy(w, cc):
            h = shards[w].shape[0] // 2
            rows = outs[w].at[pl.ds(cc * h, h), :]
            return pltpu.make_async_remote_copy(
                src_ref=rows, dst_ref=rows, send_sem=send.at[w], recv_sem=recv.at[w],
                device_id=(x, y, 1 - c), device_id_type=MESH)

        for w in range(nw):
            copy(w, c).start()
        for w in range(nw):
            copy(w, 1 - c).wait_recv()
            copy(w, c).wait_send()

    hbm = pl.BlockSpec(memory_space=pl.ANY)
    return pl.pallas_call(
        body, name="join_halves",
        in_specs=[hbm] * nw, out_specs=[hbm] * nw,
        out_shape=[jax.ShapeDtypeStruct(s.shape, F32) for s in shards],
        input_output_aliases={w: w for w in range(nw)},
        scratch_shapes=[pltpu.SemaphoreType.DMA((nw,)), pltpu.SemaphoreType.DMA((nw,))],
    )(*shards)


def _adamw_math(w, g, m, v):
    m = ADAM_B1 * m + (1.0 - ADAM_B1) * g
    v = ADAM_B2 * v + (1.0 - ADAM_B2) * (g * g)
    m_hat = m / (1.0 - ADAM_B1 ** ADAM_STEP)
    v_hat = v / (1.0 - ADAM_B2 ** ADAM_STEP)
    delta = -ADAM_LR * (m_hat / (jnp.sqrt(v_hat) + ADAM_EPS) + ADAM_WD * w)
    return delta, m, v


def _adamw(ws, gs, ms, vs):
    nw = len(ws)
    split = 8

    def body(*refs):
        w_r, g_r, m_r, v_r = (refs[i * nw:(i + 1) * nw] for i in range(4))
        d_o, m_o, v_o = (refs[(4 + i) * nw:(5 + i) * nw] for i in range(3))
        for k in range(nw):
            d, m, v = _adamw_math(w_r[k][...], g_r[k][...], m_r[k][...], v_r[k][...])
            d_o[k][...] = d
            m_o[k][...] = m
            v_o[k][...] = v

    specs = [pl.BlockSpec((w.shape[0] // split, w.shape[1]), lambda i: (i, 0)) for w in ws]
    shapes = [jax.ShapeDtypeStruct(w.shape, F32) for w in ws]
    outs = pl.pallas_call(
        body, name="adamw", grid=(split,),
        in_specs=specs * 4, out_specs=specs * 3, out_shape=shapes * 3,
        compiler_params=_params("parallel", vmem=VMEM_LIMIT),
    )(*ws, *gs, *ms, *vs)
    return outs[:nw], outs[nw:2 * nw], outs[2 * nw:]


SMALL_ROWS = 8
SMALL_COLS = D_MODEL
LOSS_COL = RET_WIDTH + 24


def _small_allreduce_adamw(part, w, m, v):
    def body(part_ref, w_ref, m_ref, v_ref, g_out, d_out, m_out, v_out, all_ref, send, recv):
        x, y, c = _coords()
        me = 4 * x + 2 * y + c
        all_ref[me] = part_ref[...]
        copies = []
        for rel in range(1, 8):
            px = 1 - x if rel & 4 else x
            py = 1 - y if rel & 2 else y
            pc = 1 - c if rel & 1 else c
            copies.append(pltpu.make_async_remote_copy(
                src_ref=part_ref, dst_ref=all_ref.at[me],
                send_sem=send.at[rel - 1], recv_sem=recv.at[rel - 1], device_id=(px, py, pc), device_id_type=MESH))
        for cp in copies:
            cp.start()
        for cp in copies:
            cp.wait()
        g = all_ref[0]
        for k in range(1, 8):
            g = g + all_ref[k]
        d, mn, vn = _adamw_math(w_ref[...], g, m_ref[...], v_ref[...])
        g_out[...] = g
        d_out[...] = d
        m_out[...] = mn
        v_out[...] = vn

    vm = pl.BlockSpec(memory_space=pltpu.VMEM)
    shape = jax.ShapeDtypeStruct((SMALL_ROWS, SMALL_COLS), F32)
    return pl.pallas_call(
        body, name="small_allreduce_adamw",
        in_specs=[vm] * 4, out_specs=[vm] * 4, out_shape=[shape] * 4,
        scratch_shapes=[pltpu.VMEM((8, SMALL_ROWS, SMALL_COLS), F32),
                        pltpu.SemaphoreType.DMA((7,)), pltpu.SemaphoreType.DMA((7,))],
    )(part, w, m, v)


SMALL_NAMES = ("ret_decay_fwd", "ret_decay_bwd", "attn_sink", "ret_gn_gain",
               "ln1_gain", "ln1_bias", "ln2_gain", "ln2_bias")


LN_NAMES = ("ln1_gain", "ln1_bias", "ln2_gain", "ln2_bias")


def _pack_small(vals, extra=None):
    tail = jnp.zeros((1, 1), F32) if extra is None else extra.reshape(1, 1)
    row4 = jnp.concatenate([vals["ret_gn_gain"], vals["ret_decay_fwd"], vals["ret_decay_bwd"], vals["attn_sink"],
                            tail, jnp.zeros((1, SMALL_COLS - LOSS_COL - 1), F32)], axis=1)
    rows = [vals[n] for n in LN_NAMES] + [row4, jnp.zeros((SMALL_ROWS - 5, SMALL_COLS), F32)]
    return jnp.concatenate(rows, axis=0)


def _unpack_small(packed):
    out = {n: packed[i:i + 1] for i, n in enumerate(LN_NAMES)}
    o = RET_WIDTH
    out.update(ret_gn_gain=packed[4:5, 0:o], ret_decay_fwd=packed[4:5, o:o + 8],
               ret_decay_bwd=packed[4:5, o + 8:o + 16], attn_sink=packed[4:5, o + 16:o + 24])
    return out


def _local_step(x, p, tgt, w_in_t, rest, small, core=None):
    bsz, s, _ = x.shape
    t = bsz * s
    x2d = x.reshape(t, D_MODEL)
    p2d = p.reshape(t, PLE_DIM)
    tgt2d = tgt.reshape(t, D_MODEL)
    dec_f = small["ret_decay_fwd"].reshape(8)
    dec_b = small["ret_decay_bwd"].reshape(8)
    lg_f = jnp.log1p(-jnp.exp2(dec_f))
    lg_b = jnp.log1p(-jnp.exp2(dec_b))
    per_lane = lambda v: jnp.repeat(v, HEAD_DIM).reshape(4, 1, LANES)
    lgf_l, lgb_l = per_lane(lg_f), per_lane(lg_b)
    sink = small["attn_sink"].reshape(8)
    slopes = 2.0 ** (-(jnp.arange(8, dtype=F32) + 1.0))
    gn_gain = small["ret_gn_gain"]
    g1, b1, g2, b2 = (small[n] for n in ("ln1_gain", "ln1_bias", "ln2_gain", "ln2_bias"))

    dist = core is not None
    chips = lambda names: _gather_chips_rider([rest[REST_NAMES.index(n)] for n in names])
    first, second, third = ("w_ffn_up",), ("w_out", "w_ffn_gate", "w_ple_proj", "w_ple_gate"), ("w_ffn_down",)
    u, *c1 = _inproj(x2d, w_in_t, rider=chips(first) if dist else None)
    u3 = u.reshape(bsz, s, IN_WIDTH)
    y_pre, y_ret, *o2 = _ret_fwd(u3, lgf_l, lgb_l, gn_gain,
                                 rider=_merge_riders([_gather_pass_rider(c1), chips(second)]) if dist else None)
    y_att, *o3 = _attn_fwd(u3, slopes, sink, rider=_merge_riders(
        [_gather_pass_rider(o2[len(first):]), chips(third)]) if dist else None)
    gathered = dict(zip(first, o2[:len(first)]))
    gathered.update(zip(second, o3[:len(second)]))
    w_out = _assemble_weights({"w_out": gathered["w_out"]})["w_out"] if dist else rest["w_out"]
    zh1, r1, hb, *o4 = _outproj_ln1(y_ret.reshape(t, RET_WIDTH), y_att.reshape(t, ATTN_WIDTH), x2d, w_out, g1, b1,
                                    rider=_gather_pass_rider(o3[len(second):]) if dist else None)
    gathered.update(zip(third, o4))
    wts = _assemble_weights(gathered) if dist else rest
    dz2, gs, us, pg, ple, sq, dg2, db2 = _ffn_fwd(zh1, hb, p2d, tgt2d, g1, b1, g2, b2, wts["gate4"], wts["up4"],
                                                 wts["down4"], wts["ple_proj"], wts["ple_gate"])
    dgs, dus, dsp, dple, dz1, dyr, dya, dg1, db1 = _ffn_bwd(dz2, gs, us, pg, ple, zh1, r1, g1, wts["gate4"],
                                                          wts["up4"], wts["down4"], wts["ple_gate"], wts["w_out"])
    ffn_parts = list(_wgrad_ffn(gs, us, dgs, dus, hb, dz2))
    d_w_out, d_ple_gate, d_ple_proj, *th_ffn = _wgrad_misc(
        y_ret.reshape(t, RET_WIDTH), y_att.reshape(t, ATTN_WIDTH), dz1, hb, dsp, p2d, dple,
        rider=_exchange_halves_rider(ffn_parts) if dist else None)
    misc_parts = [d_w_out.reshape(N_SHARD, D_MODEL // N_SHARD, D_MODEL),
                  d_ple_proj.reshape(PLE_DIM, N_SHARD, D_MODEL // N_SHARD).transpose(1, 0, 2),
                  d_ple_gate.reshape(N_SHARD, D_MODEL // N_SHARD, D_MODEL)]
    dyr3, dya3 = dyr.reshape(bsz, s, RET_WIDTH), dya.reshape(bsz, s, ATTN_WIDTH)
    if dist:
        s_ffn = _add_halves(ffn_parts, th_ffn, core)
        drq, drk, drv, drg, rpart, *o5 = _ret_bwd(u3, y_pre, dyr3, lgf_l, lgb_l, gn_gain, rider=_merge_riders(
            [_exchange_chips_rider(s_ffn[3:5]), _exchange_halves_rider(misc_parts)]))
        s_misc = _add_halves(misc_parts, o5[2:], core)
        daq, dakv, spart, *o6 = _attn_bwd(u3, dya3, slopes, sink,
                                          rider=_exchange_chips_rider([s_ffn[5]] + list(s_misc[3:])))
    else:
        drq, drk, drv, drg, rpart = _ret_bwd(u3, y_pre, dyr3, lgf_l, lgb_l, gn_gain)
        daq, dakv, spart = _attn_bwd(u3, dya3, slopes, sink)
    pieces = [a.reshape(t, -1) for a in (drq, drk, drv, drg, daq, dakv)]
    kv0 = CB_AK * LANES
    w_kv = jnp.concatenate([w_in_t[kv0 + o:kv0 + o + HEAD_DIM] for o in KV_ORDER], axis=0)
    d_in = _wgrad_in(pieces, x2d).reshape(N_SHARD, FFN_SHARD, D_MODEL)
    grad_x, *th_in = _inproj_bwd(dz1, pieces, w_in_t[:kv0], w_kv,
                                 rider=_exchange_halves_rider([d_in]) if dist else None)
    grad_x = grad_x.reshape(bsz, s, D_MODEL)
    if dist:
        s_in = _add_halves([d_in], th_in, core)
        sums32 = [s_misc[0], s_ffn[0], s_ffn[1], s_ffn[2], s_misc[1], s_misc[2]]
        from_chips = [o6[1], o5[0], o5[1], o6[0], o6[2], o6[3]]
        grads_in, grads_rest = (s_in[0], s_in[1]), (sums32, from_chips)
    else:
        grads_in = d_in
        grads_rest = [misc_parts[0]] + ffn_parts + misc_parts[1:]

    rsum = rpart
    lane_heads = lambda row: jnp.sum(row.reshape(4, 2, HEAD_DIM), axis=-1).reshape(8)
    dlg_f = lane_heads(rsum[:, 0, :]) + jnp.stack([jnp.sum(rsum[:, 2, :], -1), jnp.sum(rsum[:, 3, :], -1)], 1).reshape(8)
    dlg_b = lane_heads(rsum[:, 1, :]) + jnp.stack([jnp.sum(rsum[:, 4, :], -1), jnp.sum(rsum[:, 5, :], -1)], 1).reshape(8)
    chain = lambda d: -(math.log(2.0) * jnp.exp2(d)) / (1.0 - jnp.exp2(d))
    grads_small = {
        "ret_decay_fwd": (dlg_f * chain(dec_f)).reshape(1, 8),
        "ret_decay_bwd": (dlg_b * chain(dec_b)).reshape(1, 8),
        "attn_sink": jnp.sum(spart, axis=0)[:, 0:4, 0].reshape(1, 8),
        "ret_gn_gain": rsum[:, 6, :].reshape(1, RET_WIDTH),
        "ln1_gain": dg1, "ln1_bias": db1, "ln2_gain": dg2, "ln2_bias": db2,
    }
    return sq[0, 0], grad_x, grads_in, grads_rest, grads_small


BIG_NAMES = ("w_in", "w_out", "w_ffn_gate", "w_ffn_up", "w_ffn_down", "w_ple_proj", "w_ple_gate")
REST_NAMES = BIG_NAMES[1:]
TRANSPOSED = ("w_in", "w_ffn_gate", "w_ffn_up")
WEIGHT_ORDER = ("w_in", "ret_decay_fwd", "ret_decay_bwd", "ret_gn_gain", "attn_sink", "w_out", "ln1_gain",
                "ln1_bias", "w_ffn_gate", "w_ffn_up", "w_ffn_down", "w_ple_proj", "w_ple_gate", "ln2_gain", "ln2_bias")


def _shard_rows(name, a):
    return jnp.swapaxes(a[0], 0, 1) if name in TRANSPOSED else a[0]


def _unshard_rows(name, a):
    return (jnp.swapaxes(a, 0, 1) if name in TRANSPOSED else a)[None]


def _assemble_weights(gathered):
    cols = lambda a: a.transpose(1, 0, 2).reshape(a.shape[1], N_SHARD * a.shape[2])
    rows = lambda a: a.reshape(N_SHARD * a.shape[1], a.shape[2])
    same = lambda a: a
    layout = {"w_out": ("w_out", rows), "w_ffn_gate": ("gate4", same), "w_ffn_up": ("up4", same),
              "w_ffn_down": ("down4", same), "w_ple_proj": ("ple_proj", cols), "w_ple_gate": ("ple_gate", rows)}
    return {layout[n][0]: layout[n][1](a) for n, a in gathered.items()}


def kernel(x, p, w_in, ret_decay_fwd, ret_decay_bwd, ret_gn_gain, attn_sink, w_out, ln1_gain, ln1_bias, w_ffn_gate, w_ffn_up, w_ffn_down, w_ple_proj, w_ple_gate, ln2_gain, ln2_bias, loss_target, m_w_in, m_ret_decay_fwd, m_ret_decay_bwd, m_ret_gn_gain, m_attn_sink, m_w_out, m_ln1_gain, m_ln1_bias, m_w_ffn_gate, m_w_ffn_up, m_w_ffn_down, m_w_ple_proj, m_w_ple_gate, m_ln2_gain, m_ln2_bias, v_w_in, v_ret_decay_fwd, v_ret_decay_bwd, v_ret_gn_gain, v_attn_sink, v_w_out, v_ln1_gain, v_ln1_bias, v_w_ffn_gate, v_w_ffn_up, v_w_ffn_down, v_w_ple_proj, v_w_ple_gate, v_ln2_gain, v_ln2_bias):
    w = dict(w_in=w_in, ret_decay_fwd=ret_decay_fwd, ret_decay_bwd=ret_decay_bwd, ret_gn_gain=ret_gn_gain,
             attn_sink=attn_sink, w_out=w_out, ln1_gain=ln1_gain, ln1_bias=ln1_bias, w_ffn_gate=w_ffn_gate,
             w_ffn_up=w_ffn_up, w_ffn_down=w_ffn_down, w_ple_proj=w_ple_proj, w_ple_gate=w_ple_gate,
             ln2_gain=ln2_gain, ln2_bias=ln2_bias)
    m = dict(w_in=m_w_in, ret_decay_fwd=m_ret_decay_fwd, ret_decay_bwd=m_ret_decay_bwd, ret_gn_gain=m_ret_gn_gain,
             attn_sink=m_attn_sink, w_out=m_w_out, ln1_gain=m_ln1_gain, ln1_bias=m_ln1_bias, w_ffn_gate=m_w_ffn_gate,
             w_ffn_up=m_w_ffn_up, w_ffn_down=m_w_ffn_down, w_ple_proj=m_w_ple_proj, w_ple_gate=m_w_ple_gate,
             ln2_gain=m_ln2_gain, ln2_bias=m_ln2_bias)
    v = dict(w_in=v_w_in, ret_decay_fwd=v_ret_decay_fwd, ret_decay_bwd=v_ret_decay_bwd, ret_gn_gain=v_ret_gn_gain,
             attn_sink=v_attn_sink, w_out=v_w_out, ln1_gain=v_ln1_gain, ln1_bias=v_ln1_bias, w_ffn_gate=v_w_ffn_gate,
             w_ffn_up=v_w_ffn_up, w_ffn_down=v_w_ffn_down, w_ple_proj=v_w_ple_proj, w_ple_gate=v_w_ple_gate,
             ln2_gain=v_ln2_gain, ln2_bias=v_ln2_bias)
    big = lambda d: [_shard_rows(n, d[n]) for n in BIG_NAMES]
    small = lambda d: {n: d[n] for n in SMALL_NAMES}

    chip = 2 * lax.axis_index("x") + lax.axis_index("y")
    core = lax.axis_index("c")
    c_arr = core.astype(jnp.int32).reshape(1)
    pos = jnp.stack([chip, core]).astype(jnp.int32)

    shards = [a.astype(BF16) for a in big(w)]
    (w_in4,) = _all_gather_weights(shards[:1])
    w_in_t = w_in4.reshape(IN_WIDTH, D_MODEL)
    sq, grad_x, (sum32_in, sum16_in), (sums32, from_chips), grads_small = _local_step(
        x, p[0], loss_target, w_in_t, shards[1:], small(w), core=c_arr)
    chips_in = _exchange_chips([sum16_in])
    g_big = _join_halves(_add_chips([sum32_in] + list(sums32), list(chips_in) + list(from_chips), pos))
    d_big, m_big, v_big = _adamw(big(w), g_big, big(m), big(v))

    g_s, d_s, m_s, v_s = _small_allreduce_adamw(_pack_small(grads_small, sq), _pack_small(small(w)),
                                                _pack_small(small(m)), _pack_small(small(v)))
    loss = g_s[4, LOSS_COL] * (0.5 / D_MODEL)

    def tree(bigs, packed):
        out = {n: _unshard_rows(n, a) for n, a in zip(BIG_NAMES, bigs)}
        out.update(_unpack_small(packed))
        return [out[n] for n in WEIGHT_ORDER]

    return (loss, grad_x, *tree(g_big, g_s), *tree(d_big, d_s), *tree(m_big, m_s), *tree(v_big, v_s))
```

```python
import functools
import math

import jax
import jax.numpy as jnp
from jax import lax
from jax.experimental import pallas as pl
from jax.experimental.pallas import tpu as pltpu

F32 = jnp.float32
BF16 = jnp.bfloat16

D_MODEL = 1024
HEAD_DIM = 64
RET_HEADS = 8
ATTN_HEADS = 8
RET_WIDTH = 512
ATTN_WIDTH = 512
KV_WIDTH = 128
IN_WIDTH = 2816
FFN = 2816
N_SHARD = 4
FFN_SHARD = FFN // N_SHARD
PLE_DIM = 256
CHUNK = 128
LANES = 128
ALPHA = 2.0 ** 0.25
LN_EPS = 1e-5
GN_EPS = 1e-5
NEG_INF = -1e30
ADAM_LR = 0.001
ADAM_B1 = 0.9
ADAM_B2 = 0.999
ADAM_EPS = 1e-08
ADAM_WD = 0.01
ADAM_STEP = 10
VMEM_LIMIT = 56 * 1024 * 1024
MESH = pl.DeviceIdType.MESH

CB_RQ, CB_RK, CB_RV, CB_RG, CB_AQ, CB_AK, CB_AV = 0, 4, 8, 12, 16, 20, 21


def _dot(a, b):
    return jnp.dot(a, b, preferred_element_type=F32)


def _dot_nt(a, b):
    return lax.dot_general(a, b, (((1,), (1,)), ((), ())), preferred_element_type=F32)


def _dot_tn(a, b):
    return lax.dot_general(a, b, (((0,), (0,)), ((), ())), preferred_element_type=F32)


def _sigmoid(x):
    return 1.0 / (1.0 + jnp.exp(-x))


def _params(*sem, vmem=None):
    return pltpu.CompilerParams(dimension_semantics=tuple(sem) if sem else None, vmem_limit_bytes=vmem)


class _Rider:
    def __init__(self, ins, out_shapes, sems, start, finish, aliases=None):
        self.ins, self.out_shapes, self.sems = list(ins), list(out_shapes), list(sems)
        self.start, self.finish, self.aliases = start, finish, dict(aliases or {})


def _merge_riders(riders):
    riders = [r for r in riders if r is not None]
    if len(riders) == 1:
        return riders[0]
    bounds, aliases = [], {}
    i0 = o0 = s0 = 0
    for r in riders:
        bounds.append((i0, o0, s0))
        aliases.update({i0 + i: o0 + o for i, o in r.aliases.items()})
        i0, o0, s0 = i0 + len(r.ins), o0 + len(r.out_shapes), s0 + len(r.sems)

    def each(method):
        def run(ins, outs, sems):
            for r, (i, o, s) in zip(riders, bounds):
                getattr(r, method)(ins[i:i + len(r.ins)], outs[o:o + len(r.out_shapes)], sems[s:s + len(r.sems)])
        return run

    return _Rider([a for r in riders for a in r.ins], [a for r in riders for a in r.out_shapes],
                  [a for r in riders for a in r.sems], each("start"), each("finish"), aliases)


def _hosted_call(body, name, grid, in_specs, out_specs, out_shape, scratch_shapes, operands, rider=None,
                 semantics=None):
    n_in, n_out, n_scr = len(in_specs), len(out_specs), len(scratch_shapes)
    if rider is None:
        return pl.pallas_call(
            body, name=name, grid=grid, in_specs=in_specs, out_specs=out_specs, out_shape=out_shape,
            scratch_shapes=scratch_shapes,
            compiler_params=_params(*(semantics or ["parallel"] * len(grid)), vmem=VMEM_LIMIT))(*operands)
    r_in, r_out = len(rider.ins), len(rider.out_shapes)

    def full_body(*refs):
        main_in, rin = refs[:n_in], refs[n_in:n_in + r_in]
        o0 = n_in + r_in
        main_out, rout = refs[o0:o0 + n_out], refs[o0 + n_out:o0 + n_out + r_out]
        s0 = o0 + n_out + r_out
        main_scr, rsem = refs[s0:s0 + n_scr], refs[s0 + n_scr:]
        first = functools.reduce(jnp.logical_and, [pl.program_id(a) == 0 for a in range(len(grid))])
        last = functools.reduce(jnp.logical_and, [pl.program_id(a) == g - 1 for a, g in enumerate(grid)])

        @pl.when(first)
        def _():
            rider.start(rin, rout, rsem)

        body(*main_in, *main_out, *main_scr)

        @pl.when(last)
        def _():
            rider.finish(rin, rout, rsem)

    hbm = pl.BlockSpec(memory_space=pl.ANY)
    return pl.pallas_call(
        full_body, name=name, grid=grid,
        in_specs=list(in_specs) + [hbm] * r_in, out_specs=list(out_specs) + [hbm] * r_out,
        out_shape=list(out_shape) + rider.out_shapes,
        scratch_shapes=list(scratch_shapes) + rider.sems,
        input_output_aliases={n_in + i: n_out + o for i, o in rider.aliases.items()},
        compiler_params=_params(*(["arbitrary"] * len(grid)), vmem=VMEM_LIMIT),
    )(*operands, *rider.ins)


def _head_mean(x, m0):
    s0 = jnp.sum(jnp.where(m0, x, 0.0), axis=1, keepdims=True)
    s1 = jnp.sum(jnp.where(m0, 0.0, x), axis=1, keepdims=True)
    return jnp.where(m0, s0, s1) * (1.0 / HEAD_DIM)


def _inproj(x2d, w_in_t, rider=None):
    t = x2d.shape[0]
    tm = 512
    nb = 256

    def body(x_ref, w_ref, o_ref):
        xb = x_ref[...].astype(BF16)
        for n in range(0, IN_WIDTH, nb):
            o_ref[:, n:n + nb] = _dot_nt(xb, w_ref[n:n + nb, :]).astype(BF16)

    return _hosted_call(
        body, "inproj", (t // tm,),
        in_specs=[pl.BlockSpec((tm, D_MODEL), lambda i: (i, 0)),
                  pl.BlockSpec((IN_WIDTH, D_MODEL), lambda i: (0, 0))],
        out_specs=[pl.BlockSpec((tm, IN_WIDTH), lambda i: (i, 0))],
        out_shape=[jax.ShapeDtypeStruct((t, IN_WIDTH), BF16)],
        scratch_shapes=[], operands=(x2d, w_in_t), rider=rider)


def _outproj_ln1(y_ret, y_att, x2d, w_out, gain, bias, rider=None):
    t = x2d.shape[0]
    tm = 512

    def body(yr_ref, ya_ref, x_ref, w_ref, g_ref, b_ref, zh_ref, r_ref, hb_ref):
        mix = _dot(yr_ref[...], w_ref[0:RET_WIDTH, :]) + _dot(ya_ref[...], w_ref[RET_WIDTH:, :])
        z = ALPHA * x_ref[...] + mix
        mu = jnp.mean(z, axis=1, keepdims=True)
        zc = z - mu
        var = jnp.mean(zc * zc, axis=1, keepdims=True)
        r = lax.rsqrt(var + LN_EPS)
        zh = zc * r
        zh_ref[...] = zh
        r_ref[...] = r
        hb_ref[...] = (zh * g_ref[...] + b_ref[...]).astype(BF16)

    row = lambda w: pl.BlockSpec((tm, w), lambda i: (i, 0))
    const = lambda s: pl.BlockSpec(s, lambda i: (0, 0))
    return _hosted_call(
        body, "outproj_ln1", (t // tm,),
        in_specs=[row(RET_WIDTH), row(ATTN_WIDTH), row(D_MODEL), const((D_MODEL, D_MODEL)),
                  const((1, D_MODEL)), const((1, D_MODEL))],
        out_specs=[row(D_MODEL), row(1), row(D_MODEL)],
        out_shape=[jax.ShapeDtypeStruct((t, D_MODEL), F32), jax.ShapeDtypeStruct((t, 1), F32),
                   jax.ShapeDtypeStruct((t, D_MODEL), BF16)],
        scratch_shapes=[], operands=(y_ret, y_att, x2d, w_out, gain, bias), rider=rider)


def _load_resident(step, pairs):
    @pl.when(step == 0)
    def _():
        for src, dst in pairs:
            pltpu.sync_copy(src, dst)


def _ffn_fwd(zh1, hb, p2d, tgt, g1, b1, g2, b2, wg4, wu4, wd4, wpe, wpg):
    t = zh1.shape[0]
    tm = 256

    def body(zh_ref, hb_ref, p_ref, t_ref, g1_ref, b1_ref, g2_ref, b2_ref,
             wg_hbm, wu_hbm, wd_hbm, wpe_hbm, wpg_hbm,
             dz_ref, gs_ref, us_ref, pg_ref, ple_ref, loss_ref, dg2_ref, db2_ref,
             wg, wu, wd, wpe, wpg):
        step = pl.program_id(0)
        _load_resident(step, [(wg_hbm, wg), (wu_hbm, wu), (wd_hbm, wd), (wpe_hbm, wpe), (wpg_hbm, wpg)])

        @pl.when(step == 0)
        def _():
            loss_ref[...] = jnp.zeros_like(loss_ref)
            dg2_ref[...] = jnp.zeros_like(dg2_ref)
            db2_ref[...] = jnp.zeros_like(db2_ref)

        h1 = zh_ref[...] * g1_ref[...] + b1_ref[...]
        hbv = hb_ref[...]
        ffn = jnp.zeros((tm, D_MODEL), F32)
        for j in range(N_SHARD):
            gj = _dot_nt(hbv, wg[j])
            uj = _dot_nt(hbv, wu[j])
            gs_ref[j] = gj.astype(BF16)
            us_ref[j] = uj.astype(BF16)
            act = (gj * _sigmoid(gj) * uj).astype(BF16)
            ffn = ffn + _dot(act, wd[j])
        ple = _dot(p_ref[...].astype(BF16), wpe[...])
        pg = _sigmoid(_dot(hbv, wpg[...]))
        pg_ref[...] = pg.astype(BF16)
        ple_ref[...] = ple.astype(BF16)
        z2 = ALPHA * h1 + ffn + pg * ple
        mu = jnp.mean(z2, axis=1, keepdims=True)
        zc = z2 - mu
        var = jnp.mean(zc * zc, axis=1, keepdims=True)
        r = lax.rsqrt(var + LN_EPS)
        zh2 = zc * r
        err = zh2 * g2_ref[...] + b2_ref[...] - t_ref[...]
        loss_ref[...] += jnp.sum(err * err)
        dy = err * (1.0 / D_MODEL)
        dg2_ref[...] += jnp.sum(dy * zh2, axis=0, keepdims=True)
        db2_ref[...] += jnp.sum(dy, axis=0, keepdims=True)
        dzh = dy * g2_ref[...]
        m1 = jnp.mean(dzh, axis=1, keepdims=True)
        m2 = jnp.mean(dzh * zh2, axis=1, keepdims=True)
        dz_ref[...] = r * (dzh - m1 - zh2 * m2)

    row = lambda w: pl.BlockSpec((tm, w), lambda i: (i, 0))
    const = lambda s: pl.BlockSpec(s, lambda i: (0, 0))
    sh = pl.BlockSpec((N_SHARD, tm, FFN_SHARD), lambda i: (0, i, 0))
    hbm = pl.BlockSpec(memory_space=pl.ANY)
    return pl.pallas_call(
        body, name="ffn_fwd", grid=(t // tm,),
        in_specs=[row(D_MODEL), row(D_MODEL), row(PLE_DIM), row(D_MODEL),
                  const((1, D_MODEL)), const((1, D_MODEL)), const((1, D_MODEL)), const((1, D_MODEL)),
                  hbm, hbm, hbm, hbm, hbm],
        out_specs=[row(D_MODEL), sh, sh, row(D_MODEL), row(D_MODEL),
                   const((8, LANES)), const((1, D_MODEL)), const((1, D_MODEL))],
        out_shape=[jax.ShapeDtypeStruct((t, D_MODEL), F32),
                   jax.ShapeDtypeStruct((N_SHARD, t, FFN_SHARD), BF16),
                   jax.ShapeDtypeStruct((N_SHARD, t, FFN_SHARD), BF16),
                   jax.ShapeDtypeStruct((t, D_MODEL), BF16), jax.ShapeDtypeStruct((t, D_MODEL), BF16),
                   jax.ShapeDtypeStruct((8, LANES), F32),
                   jax.ShapeDtypeStruct((1, D_MODEL), F32), jax.ShapeDtypeStruct((1, D_MODEL), F32)],
        scratch_shapes=[pltpu.VMEM(wg4.shape, BF16), pltpu.VMEM(wu4.shape, BF16), pltpu.VMEM(wd4.shape, BF16),
                        pltpu.VMEM(wpe.shape, BF16), pltpu.VMEM(wpg.shape, BF16)],
        compiler_params=_params("arbitrary", vmem=VMEM_LIMIT),
    )(zh1, hb, p2d, tgt, g1, b1, g2, b2, wg4, wu4, wd4, wpe, wpg)


def _ret_tables(lgf, lgb):
    c = CHUNK
    row = lax.broadcasted_iota(jnp.int32, (c, LANES), 0).astype(F32)
    ii = lax.broadcasted_iota(jnp.int32, (c, c), 0).astype(F32)
    jj = lax.broadcasted_iota(jnp.int32, (c, c), 1).astype(F32)
    diff = ii - jj
    dmats = []
    for h in range(2):
        lf = lgf[:, h * HEAD_DIM:h * HEAD_DIM + 1]
        lb = lgb[:, h * HEAD_DIM:h * HEAD_DIM + 1]
        dmats.append(jnp.where(diff > 0, jnp.exp(lf * jnp.maximum(diff, 0.0)),
                               jnp.where(diff < 0, jnp.exp(lb * jnp.maximum(-diff, 0.0)), 2.0)))
    tab = dict(
        qdec_f=jnp.exp(lgf * (row + 1.0)), kdec_f=jnp.exp(lgf * (c - 1.0 - row)),
        qdec_b=jnp.exp(lgb * (c - row)), kdec_b=jnp.exp(lgb * row),
        cdec_f=jnp.exp(lgf * c), cdec_b=jnp.exp(lgb * c),
        d0=dmats[0], d1=dmats[1], row=row, diff=diff)
    r = lax.broadcasted_iota(jnp.int32, (LANES, LANES), 0) < HEAD_DIM
    cc = lax.broadcasted_iota(jnp.int32, (LANES, LANES), 1) < HEAD_DIM
    tab["bd"] = r == cc
    tab["m0"] = lax.broadcasted_iota(jnp.int32, (c, LANES), 1) < HEAD_DIM
    return tab


def _ret_specs(bsz, s):
    blk = lambda cb: pl.BlockSpec((bsz, s, LANES), lambda p, cb=cb: (0, 0, cb + p))
    lane = pl.BlockSpec((None, 1, LANES), lambda p: (p, 0, 0))
    gain = pl.BlockSpec((1, LANES), lambda p: (0, p))
    pair = pl.BlockSpec((bsz, s, LANES), lambda p: (0, 0, p))
    return blk, lane, gain, pair


def _ret_kv_states(tb, k_ref, v_ref, rb_ref, kvf_ref, n_chunk):
    c = CHUNK
    bsz = k_ref.shape[0]
    bd = tb["bd"]

    def step(i, rbs):
        n = n_chunk - 1 - i
        sl = pl.ds(pl.multiple_of(n * c, c), c)
        kfb = []
        for b in range(bsz):
            k32 = k_ref[b, sl, :].astype(F32)
            kfb.append(jnp.concatenate([k32 * tb["kdec_f"], k32 * tb["kdec_b"]], axis=1).astype(BF16))
        kvs = [_dot_tn(kfb[b], v_ref[b, sl, :]) for b in range(bsz)]
        new = []
        for b in range(bsz):
            rb_ref[b, n] = rbs[b]
            kvf_ref[b, n] = jnp.where(bd, kvs[b][0:LANES], 0.0)
            new.append(rbs[b] * tb["cdec_b"] + jnp.where(bd, kvs[b][LANES:], 0.0))
        return tuple(new)

    lax.fori_loop(0, n_chunk, step, tuple(jnp.zeros((LANES, LANES), F32) for _ in range(bsz)))


def _split_rows(x, m0):
    return jnp.concatenate([jnp.where(m0, x, 0.0), jnp.where(m0, 0.0, x)], axis=0).astype(BF16)


def _ret_fwd(u3, lgf_l, lgb_l, gn_gain, rider=None):
    bsz, s, _ = u3.shape
    n_chunk = s // CHUNK
    c = CHUNK

    def body(q_ref, k_ref, v_ref, g_ref, lgf_ref, lgb_ref, gain_ref, y_ref, o_ref, rb_ref, kvf_ref):
        tb = _ret_tables(lgf_ref[...], lgb_ref[...])
        m0 = tb["m0"]
        gain = gain_ref[...]
        rows = range(bsz)
        _ret_kv_states(tb, k_ref, v_ref, rb_ref, kvf_ref, n_chunk)

        def chunk(n, rfs):
            sl = pl.ds(pl.multiple_of(n * c, c), c)
            qs = [q_ref[b, sl, :].astype(F32) * 0.125 for b in rows]
            s01 = [_dot_nt(_split_rows(qs[b], m0), k_ref[b, sl, :]) for b in rows]
            ys = []
            for b in rows:
                lhs = jnp.concatenate([s01[b][0:c] * tb["d0"], s01[b][c:] * tb["d1"],
                                       qs[b] * tb["qdec_f"], qs[b] * tb["qdec_b"]], axis=1).astype(BF16)
                rhs = jnp.concatenate([_split_rows(v_ref[b, sl, :].astype(F32), m0),
                                       rfs[b].astype(BF16), rb_ref[b, n].astype(BF16)], axis=0)
                ys.append(_dot(lhs, rhs))
            new = []
            for b in rows:
                y = ys[b]
                mu = _head_mean(y, m0)
                yc = y - mu
                var = _head_mean(yc * yc, m0)
                yh = yc * lax.rsqrt(var + GN_EPS)
                g = g_ref[b, sl, :].astype(F32)
                y_ref[b, sl, :] = y
                o_ref[b, sl, :] = (yh * gain * (g * _sigmoid(g))).astype(BF16)
                new.append(rfs[b] * tb["cdec_f"] + kvf_ref[b, n])
            return tuple(new)

        lax.fori_loop(0, n_chunk, chunk, tuple(jnp.zeros((LANES, LANES), F32) for _ in rows))

    blk, lane, gain, pair = _ret_specs(bsz, s)
    state = pltpu.VMEM((bsz, n_chunk, LANES, LANES), F32)
    return _hosted_call(
        body, "ret_fwd", (4,),
        in_specs=[blk(CB_RQ), blk(CB_RK), blk(CB_RV), blk(CB_RG), lane, lane, gain],
        out_specs=[pair, pair],
        out_shape=[jax.ShapeDtypeStruct((bsz, s, RET_WIDTH), F32), jax.ShapeDtypeStruct((bsz, s, RET_WIDTH), BF16)],
        scratch_shapes=[state, state],
        operands=(u3, u3, u3, u3, lgf_l, lgb_l, gn_gain), rider=rider)


def _ret_bwd(u3, y_pre, d_o, lgf_l, lgb_l, gn_gain, rider=None):
    bsz, s, _ = u3.shape
    n_chunk = s // CHUNK
    c = CHUNK

    def body(q_ref, k_ref, v_ref, g_ref, y_ref, do_ref, lgf_ref, lgb_ref, gain_ref,
             dq_ref, dk_ref, dv_ref, dg_ref, part_ref,
             rb_ref, kvf_ref, rf_ref, dirf_ref, dy_ref, dk_acc, dv_acc, pa0, pa1, vec_ref):
        tb = _ret_tables(lgf_ref[...], lgb_ref[...])
        m0, bd, row = tb["m0"], tb["bd"], tb["row"]
        gain = gain_ref[...]
        wf = jnp.maximum(tb["diff"], 0.0)
        wb = jnp.maximum(-tb["diff"], 0.0)
        rows = range(bsz)
        zero_states = tuple(jnp.zeros((LANES, LANES), F32) for _ in rows)
        for ref in (pa0, pa1):
            ref[...] = jnp.zeros_like(ref)
        vec_ref[...] = jnp.zeros_like(vec_ref)
        _ret_kv_states(tb, k_ref, v_ref, rb_ref, kvf_ref, n_chunk)

        def sweep_fwd(n, carry):
            rfs, gbs = carry
            sl = pl.ds(pl.multiple_of(n * c, c), c)
            qs, ks, vs, dys, dybs, q01, k01, dy01 = [], [], [], [], [], [], [], []
            dgain = jnp.zeros((1, LANES), F32)
            for b in rows:
                q = q_ref[b, sl, :].astype(F32) * 0.125
                k = k_ref[b, sl, :]
                y = y_ref[b, sl, :]
                do = do_ref[b, sl, :].astype(F32)
                g = g_ref[b, sl, :].astype(F32)
                mu = _head_mean(y, m0)
                yc = y - mu
                rstd = lax.rsqrt(_head_mean(yc * yc, m0) + GN_EPS)
                yh = yc * rstd
                sg = _sigmoid(g)
                sil = g * sg
                dyh = do * gain * sil
                dg_ref[b, sl, :] = (do * yh * gain * sg * (1.0 + g * (1.0 - sg))).astype(BF16)
                dgain = dgain + jnp.sum(do * yh * sil, axis=0, keepdims=True)
                dy = rstd * (dyh - _head_mean(dyh, m0) - yh * _head_mean(dyh * yh, m0))
                dyb = dy.astype(BF16)
                dy_ref[b, sl, :] = dyb
                rf_ref[b, n] = rfs[b]
                qs.append(q)
                ks.append(k)
                vs.append(v_ref[b, sl, :])
                dys.append(dy)
                dybs.append(dyb)
                q01.append(_split_rows(q, m0))
                k01.append(_split_rows(k.astype(F32), m0))
                dy01.append(_split_rows(dy, m0))
            s01 = [_dot_nt(q01[b], ks[b]) for b in rows]
            da01 = [_dot_nt(dy01[b], vs[b]) for b in rows]
            rbn = [rb_ref[b, n] for b in rows]
            states = [jnp.concatenate([rfs[b], rbn[b]], axis=0).astype(BF16) for b in rows]
            dqc = [_dot_nt(dybs[b], states[b]) for b in rows]
            gbb = [gbs[b].astype(BF16) for b in rows]
            dkb = [_dot_nt(vs[b], gbb[b]) for b in rows]
            qfb = [jnp.concatenate([qs[b] * tb["qdec_f"], qs[b] * tb["qdec_b"]], axis=1) for b in rows]
            direct = [_dot_tn(qfb[b].astype(BF16), dybs[b]) for b in rows]
            ds_cat, ds_rows, a_rows = [], [], []
            for b in rows:
                a0 = s01[b][0:c] * tb["d0"]
                a1 = s01[b][c:] * tb["d1"]
                pa0[...] += da01[b][0:c] * a0
                pa1[...] += da01[b][c:] * a1
                ds0 = da01[b][0:c] * tb["d0"]
                ds1 = da01[b][c:] * tb["d1"]
                ds_cat.append(jnp.concatenate([ds0, ds1], axis=1).astype(BF16))
                ds_rows.append(jnp.concatenate([ds0, ds1], axis=0).astype(BF16))
                a_rows.append(jnp.concatenate([a0, a1], axis=0).astype(BF16))
            kbd = [ks[b].astype(F32) * tb["kdec_b"] for b in rows]
            dq_in = [_dot(ds_cat[b], k01[b]) for b in rows]
            dk_in = [_dot_tn(ds_rows[b], q01[b]) for b in rows]
            dv_in = [_dot_tn(a_rows[b], dy01[b]) for b in rows]
            dv_gb = [_dot(kbd[b].astype(BF16), gbb[b]) for b in rows]
            new_rf, new_gb = [], []
            dlf = jnp.zeros((1, LANES), F32)
            dlb = jnp.zeros((1, LANES), F32)
            for b in rows:
                dqf, dqb = dqc[b][:, 0:LANES], dqc[b][:, LANES:]
                qf, qb = qfb[b][:, 0:LANES], qfb[b][:, LANES:]
                dq = dq_in[b] + dqf * tb["qdec_f"] + dqb * tb["qdec_b"]
                dq_ref[b, sl, :] = (dq * 0.125).astype(BF16)
                dk_acc[b, sl, :] = dk_in[b] + dkb[b] * tb["kdec_b"]
                dv_acc[b, sl, :] = dv_in[b] + dv_gb[b]
                dlf = dlf + jnp.sum((row + 1.0) * qf * dqf, axis=0, keepdims=True)
                dlb = dlb + jnp.sum((c - row) * qb * dqb + row * kbd[b] * dkb[b], axis=0, keepdims=True)
                dlb = dlb + c * tb["cdec_b"] * jnp.sum(gbs[b] * rbn[b], axis=0, keepdims=True)
                dirf_ref[b, n] = jnp.where(bd, direct[b][0:LANES], 0.0)
                new_gb.append(jnp.where(bd, direct[b][LANES:], 0.0) + tb["cdec_b"] * gbs[b])
                new_rf.append(rfs[b] * tb["cdec_f"] + kvf_ref[b, n])
            vec_ref[0:1, :] += dlf
            vec_ref[1:2, :] += dlb
            vec_ref[6:7, :] += dgain
            return tuple(new_rf), tuple(new_gb)

        lax.fori_loop(0, n_chunk, sweep_fwd, (zero_states, zero_states))

        def sweep_bwd(i, gfs):
            n = n_chunk - 1 - i
            sl = pl.ds(pl.multiple_of(n * c, c), c)
            gfb = [gfs[b].astype(BF16) for b in rows]
            kfd = [k_ref[b, sl, :].astype(F32) * tb["kdec_f"] for b in rows]
            dkf = [_dot_nt(v_ref[b, sl, :], gfb[b]) for b in rows]
            dvf = [_dot(kfd[b].astype(BF16), gfb[b]) for b in rows]
            new = []
            dlf = jnp.zeros((1, LANES), F32)
            for b in rows:
                dk_ref[b, sl, :] = (dk_acc[b, sl, :] + dkf[b] * tb["kdec_f"]).astype(BF16)
                dv_ref[b, sl, :] = (dv_acc[b, sl, :] + dvf[b]).astype(BF16)
                dlf = dlf + jnp.sum((c - 1.0 - row) * kfd[b] * dkf[b], axis=0, keepdims=True)
                dlf = dlf + c * tb["cdec_f"] * jnp.sum(gfs[b] * rf_ref[b, n], axis=0, keepdims=True)
                new.append(dirf_ref[b, n] + tb["cdec_f"] * gfs[b])
            vec_ref[0:1, :] += dlf
            return tuple(new)

        lax.fori_loop(0, n_chunk, sweep_bwd, zero_states)
        vec_ref[2:3, :] = jnp.sum(pa0[...] * wf, axis=0, keepdims=True)
        vec_ref[3:4, :] = jnp.sum(pa1[...] * wf, axis=0, keepdims=True)
        vec_ref[4:5, :] = jnp.sum(pa0[...] * wb, axis=0, keepdims=True)
        vec_ref[5:6, :] = jnp.sum(pa1[...] * wb, axis=0, keepdims=True)
        part_ref[...] = vec_ref[...]

    blk, lane, gain, pair = _ret_specs(bsz, s)
    out_bf = jax.ShapeDtypeStruct((bsz, s, RET_WIDTH), BF16)
    state = pltpu.VMEM((bsz, n_chunk, LANES, LANES), F32)
    return _hosted_call(
        body, "ret_bwd", (4,),
        in_specs=[blk(CB_RQ), blk(CB_RK), blk(CB_RV), blk(CB_RG), pair, pair, lane, lane, gain],
        out_specs=[pair, pair, pair, pair, pl.BlockSpec((None, 8, LANES), lambda p: (p, 0, 0))],
        out_shape=[out_bf, out_bf, out_bf, out_bf, jax.ShapeDtypeStruct((4, 8, LANES), F32)],
        scratch_shapes=[state, state, state, state,
                        pltpu.VMEM((bsz, s, LANES), BF16), pltpu.VMEM((bsz, s, LANES), F32),
                        pltpu.VMEM((bsz, s, LANES), F32),
                        pltpu.VMEM((c, c), F32), pltpu.VMEM((c, c), F32), pltpu.VMEM((8, LANES), F32)],
        operands=(u3, u3, u3, u3, y_pre, d_o, lgf_l, lgb_l, gn_gain), rider=rider)


def _attn_window_tables(n, s):
    qi = lax.broadcasted_iota(jnp.int32, (CHUNK, 3 * CHUNK), 0)
    kj = lax.broadcasted_iota(jnp.int32, (CHUNK, 3 * CHUNK), 1)
    dist = jnp.abs(kj - CHUNK - qi)
    kpos = n * CHUNK - CHUNK + kj
    valid = (dist <= CHUNK) & (kpos >= 0) & (kpos < s)
    return dist.astype(F32), valid


def _dup_kv_head(x, g):
    lane = lax.broadcasted_iota(jnp.int32, x.shape, 1)
    keep = (lane < HEAD_DIM) == (g == 0)
    xf = x.astype(F32)
    return jnp.where(keep, xf, pltpu.roll(xf, HEAD_DIM, 1))


def _attn_specs(s):
    q = pl.BlockSpec((None, s, 2 * LANES), lambda b, g: (b, 0, CB_AQ // 2 + g))
    k = pl.BlockSpec((None, s, LANES), lambda b, g: (b, 0, CB_AK))
    v = pl.BlockSpec((None, s, LANES), lambda b, g: (b, 0, CB_AV))
    grp = pl.BlockSpec((None, s, 2 * LANES), lambda b, g: (b, 0, g))
    smem = pl.BlockSpec(memory_space=pltpu.SMEM)
    return q, k, v, grp, smem


def _fill_padded(dst_ref, val, s):
    dst_ref[0:CHUNK, :] = jnp.zeros((CHUNK, LANES), dst_ref.dtype)
    dst_ref[CHUNK:CHUNK + s, :] = val.astype(dst_ref.dtype)
    dst_ref[CHUNK + s:2 * CHUNK + s, :] = jnp.zeros((CHUNK, LANES), dst_ref.dtype)


def _attn_probs(sc, slope, snk, dist, valid):
    sc = jnp.where(valid, sc - slope * dist, NEG_INF)
    m = jnp.maximum(jnp.max(sc, axis=1, keepdims=True), snk)
    e = jnp.exp(sc - m)
    es = jnp.exp(snk - m)
    inv = 1.0 / (jnp.sum(e, axis=1, keepdims=True) + es)
    return e * inv, es * inv


def _stack_heads(x2, m0):
    parts = []
    for pr in range(2):
        xp = x2[:, pr * LANES:(pr + 1) * LANES]
        parts += [jnp.where(m0, xp, 0.0), jnp.where(m0, 0.0, xp)]
    return jnp.concatenate(parts, axis=0).astype(BF16)


def _unstack_pair(x_all, pr, m0):
    return jnp.where(m0, x_all[(2 * pr) * CHUNK:(2 * pr + 1) * CHUNK], x_all[(2 * pr + 1) * CHUNK:(2 * pr + 2) * CHUNK])


def _attn_fwd(u3, slopes, sink, rider=None):
    bsz, s, _ = u3.shape
    n_blk = s // CHUNK

    def body(slope_ref, sink_ref, q_ref, k_ref, v_ref, o_ref, kp_ref, vp_ref):
        g = pl.program_id(1)
        _fill_padded(kp_ref, _dup_kv_head(k_ref[...], g), s)
        _fill_padded(vp_ref, _dup_kv_head(v_ref[...], g), s)
        m0 = lax.broadcasted_iota(jnp.int32, (CHUNK, LANES), 1) < HEAD_DIM

        def blk(n, carry):
            r0 = pl.multiple_of(n * CHUNK, CHUNK)
            kw = kp_ref[pl.ds(r0, 3 * CHUNK), :]
            vw = vp_ref[pl.ds(r0, 3 * CHUNK), :]
            dist, valid = _attn_window_tables(n, s)
            q_all = _stack_heads(q_ref[pl.ds(r0, CHUNK), :].astype(F32) * 0.125, m0)
            sc_all = _dot_nt(q_all, kw)
            probs = []
            for i in range(4):
                p, _ = _attn_probs(sc_all[i * CHUNK:(i + 1) * CHUNK], slope_ref[g * 4 + i], sink_ref[g * 4 + i],
                                   dist, valid)
                probs.append(p.astype(BF16))
            out_all = _dot(jnp.concatenate(probs, axis=0), vw)
            for pr in range(2):
                o_ref[pl.ds(r0, CHUNK), pr * LANES:(pr + 1) * LANES] = _unstack_pair(out_all, pr, m0).astype(BF16)
            return carry

        lax.fori_loop(0, n_blk, blk, 0)

    q, k, v, grp, smem = _attn_specs(s)
    return _hosted_call(
        body, "attn_fwd", (bsz, 2),
        in_specs=[smem, smem, q, k, v],
        out_specs=[grp],
        out_shape=[jax.ShapeDtypeStruct((bsz, s, ATTN_WIDTH), BF16)],
        scratch_shapes=[pltpu.VMEM((s + 2 * CHUNK, LANES), BF16), pltpu.VMEM((s + 2 * CHUNK, LANES), BF16)],
        operands=(slopes, sink, u3, u3, u3), rider=rider)


def _attn_bwd(u3, d_o, slopes, sink, rider=None):
    bsz, s, _ = u3.shape
    n_blk = s // CHUNK

    def body(slope_ref, sink_ref, q_ref, k_ref, v_ref, do_ref, dq_ref, dkv_ref, ds_ref,
             kp_ref, vp_ref, dk_acc, dv_acc):
        g = pl.program_id(1)
        _fill_padded(kp_ref, _dup_kv_head(k_ref[...], g), s)
        _fill_padded(vp_ref, _dup_kv_head(v_ref[...], g), s)
        dk_acc[...] = jnp.zeros_like(dk_acc)
        dv_acc[...] = jnp.zeros_like(dv_acc)
        m0 = lax.broadcasted_iota(jnp.int32, (CHUNK, LANES), 1) < HEAD_DIM

        def blk(n, dsink):
            r0 = pl.multiple_of(n * CHUNK, CHUNK)
            win = pl.ds(r0, 3 * CHUNK)
            kw = kp_ref[win, :]
            vw = vp_ref[win, :]
            dist, valid = _attn_window_tables(n, s)
            q_all = _stack_heads(q_ref[pl.ds(r0, CHUNK), :].astype(F32) * 0.125, m0)
            do_all = _stack_heads(do_ref[pl.ds(r0, CHUNK), :].astype(F32), m0)
            sc_all = _dot_nt(q_all, kw)
            dp_all = _dot_nt(do_all, vw)
            new_dsink, probs, dscs = [], [], []
            for i in range(4):
                rows = slice(i * CHUNK, (i + 1) * CHUNK)
                p, ps = _attn_probs(sc_all[rows], slope_ref[g * 4 + i], sink_ref[g * 4 + i], dist, valid)
                dp = dp_all[rows]
                delta = jnp.sum(p * dp, axis=1, keepdims=True)
                dscs.append((p * (dp - delta)).astype(BF16))
                probs.append(p.astype(BF16))
                dsh = jnp.sum(ps * delta, axis=0, keepdims=True)
                new_dsink.append(dsink[i] - jnp.broadcast_to(dsh, (1, LANES)))
            dsc_all = jnp.concatenate(dscs, axis=0)
            dq_all = _dot(dsc_all, kw)
            dk_acc[win, :] += _dot_tn(dsc_all, q_all)
            dv_acc[win, :] += _dot_tn(jnp.concatenate(probs, axis=0), do_all)
            for pr in range(2):
                dq_ref[pl.ds(r0, CHUNK), pr * LANES:(pr + 1) * LANES] = (
                    _unstack_pair(dq_all, pr, m0) * 0.125).astype(BF16)
            return tuple(new_dsink)

        dsink = lax.fori_loop(0, n_blk, blk, tuple(jnp.zeros((1, LANES), F32) for _ in range(4)))
        dk = dk_acc[CHUNK:CHUNK + s, :]
        dv = dv_acc[CHUNK:CHUNK + s, :]
        lane = lax.broadcasted_iota(jnp.int32, (s, LANES), 1)
        fold = lambda a: a + pltpu.roll(a, HEAD_DIM, 1)
        dkv_ref[...] = jnp.where(lane < HEAD_DIM, fold(dk), fold(dv)).astype(BF16)
        ds_ref[...] = jnp.zeros_like(ds_ref)
        for i in range(4):
            ds_ref[i:i + 1, :] = dsink[i]

    q, k, v, grp, smem = _attn_specs(s)
    return _hosted_call(
        body, "attn_bwd", (bsz, 2),
        in_specs=[smem, smem, q, k, v, grp],
        out_specs=[grp, pl.BlockSpec((None, s, LANES), lambda b, g: (b, 0, g)),
                   pl.BlockSpec((None, None, 8, LANES), lambda b, g: (b, g, 0, 0))],
        out_shape=[jax.ShapeDtypeStruct((bsz, s, ATTN_WIDTH), BF16), jax.ShapeDtypeStruct((bsz, s, 2 * LANES), BF16),
                   jax.ShapeDtypeStruct((bsz, 2, 8, LANES), F32)],
        scratch_shapes=[pltpu.VMEM((s + 2 * CHUNK, LANES), BF16), pltpu.VMEM((s + 2 * CHUNK, LANES), BF16),
                        pltpu.VMEM((s + 2 * CHUNK, LANES), F32), pltpu.VMEM((s + 2 * CHUNK, LANES), F32)],
        operands=(slopes, sink, u3, u3, u3, d_o), rider=rider)


def _ffn_bwd(dz2, gs, us, pg, ple, zh1, r1, g1, wg4, wu4, wd4, wpg, w_out):
    t = dz2.shape[0]
    tm = 256

    def body(dz_ref, gs_ref, us_ref, pg_ref, ple_ref, zh_ref, r_ref, g1_ref,
             wg_hbm, wu_hbm, wd_hbm, wpg_hbm, wo_hbm,
             dgs_ref, dus_ref, dsp_ref, dple_ref, dz1_ref, dyr_ref, dya_ref, dg1_ref, db1_ref,
             wg, wu, wd, wpg, wo):
        step = pl.program_id(0)
        _load_resident(step, [(wg_hbm, wg), (wu_hbm, wu), (wd_hbm, wd), (wpg_hbm, wpg), (wo_hbm, wo)])

        @pl.when(step == 0)
        def _():
            dg1_ref[...] = jnp.zeros_like(dg1_ref)
            db1_ref[...] = jnp.zeros_like(db1_ref)

        dz = dz_ref[...]
        dzb = dz.astype(BF16)
        dh = ALPHA * dz
        for j in range(N_SHARD):
            da = _dot_nt(dzb, wd[j])
            gj = gs_ref[j].astype(F32)
            uj = us_ref[j].astype(F32)
            sg = _sigmoid(gj)
            dgj = (da * uj * sg * (1.0 + gj * (1.0 - sg))).astype(BF16)
            duj = (da * gj * sg).astype(BF16)
            dgs_ref[j] = dgj
            dus_ref[j] = duj
            dh = dh + _dot(dgj, wg[j]) + _dot(duj, wu[j])
        pgv = pg_ref[...].astype(F32)
        plev = ple_ref[...].astype(F32)
        dple_ref[...] = (dz * pgv).astype(BF16)
        dsp = (dz * plev * pgv * (1.0 - pgv)).astype(BF16)
        dsp_ref[...] = dsp
        dh = dh + _dot_nt(dsp, wpg[...])
        zh = zh_ref[...]
        dg1_ref[...] += jnp.sum(dh * zh, axis=0, keepdims=True)
        db1_ref[...] += jnp.sum(dh, axis=0, keepdims=True)
        dzh = dh * g1_ref[...]
        m1 = jnp.mean(dzh, axis=1, keepdims=True)
        m2 = jnp.mean(dzh * zh, axis=1, keepdims=True)
        dz1 = r_ref[...] * (dzh - m1 - zh * m2)
        dz1_ref[...] = dz1
        dyc = _dot_nt(dz1.astype(BF16), wo[...])
        dyr_ref[...] = dyc[:, 0:RET_WIDTH].astype(BF16)
        dya_ref[...] = dyc[:, RET_WIDTH:].astype(BF16)

    row = lambda w: pl.BlockSpec((tm, w), lambda i: (i, 0))
    const = lambda s: pl.BlockSpec(s, lambda i: (0, 0))
    sh = pl.BlockSpec((N_SHARD, tm, FFN_SHARD), lambda i: (0, i, 0))
    hbm = pl.BlockSpec(memory_space=pl.ANY)
    sh_shape = jax.ShapeDtypeStruct((N_SHARD, t, FFN_SHARD), BF16)
    return pl.pallas_call(
        body, name="ffn_bwd", grid=(t // tm,),
        in_specs=[row(D_MODEL), sh, sh, row(D_MODEL), row(D_MODEL), row(D_MODEL), row(1), const((1, D_MODEL)),
                  hbm, hbm, hbm, hbm, hbm],
        out_specs=[sh, sh, row(D_MODEL), row(D_MODEL), row(D_MODEL), row(RET_WIDTH), row(ATTN_WIDTH),
                   const((1, D_MODEL)), const((1, D_MODEL))],
        out_shape=[sh_shape, sh_shape, jax.ShapeDtypeStruct((t, D_MODEL), BF16),
                   jax.ShapeDtypeStruct((t, D_MODEL), BF16), jax.ShapeDtypeStruct((t, D_MODEL), F32),
                   jax.ShapeDtypeStruct((t, RET_WIDTH), BF16), jax.ShapeDtypeStruct((t, ATTN_WIDTH), BF16),
                   jax.ShapeDtypeStruct((1, D_MODEL), F32), jax.ShapeDtypeStruct((1, D_MODEL), F32)],
        scratch_shapes=[pltpu.VMEM(wg4.shape, BF16), pltpu.VMEM(wu4.shape, BF16), pltpu.VMEM(wd4.shape, BF16),
                        pltpu.VMEM(wpg.shape, BF16), pltpu.VMEM(w_out.shape, BF16)],
        compiler_params=_params("arbitrary", vmem=VMEM_LIMIT),
    )(dz2, gs, us, pg, ple, zh1, r1, g1, wg4, wu4, wd4, wpg, w_out)


def _wgrad_misc(y_ret, y_att, dz1, hb, dsp, p2d, dple, rider=None):
    t = dz1.shape[0]
    tk = min(t, 512)

    def body(yr_ref, ya_ref, dz_ref, hb_ref, dsp_ref, p_ref, dple_ref, wo_ref, wpg_ref, wpe_ref):
        @pl.when(pl.program_id(0) == 0)
        def _():
            wo_ref[...] = jnp.zeros_like(wo_ref)
            wpg_ref[...] = jnp.zeros_like(wpg_ref)
            wpe_ref[...] = jnp.zeros_like(wpe_ref)

        dzb = dz_ref[...].astype(BF16)
        wo_ref[0:RET_WIDTH, :] += _dot_tn(yr_ref[...], dzb)
        wo_ref[RET_WIDTH:, :] += _dot_tn(ya_ref[...], dzb)
        wpg_ref[...] += _dot_tn(hb_ref[...], dsp_ref[...])
        wpe_ref[...] += _dot_tn(p_ref[...].astype(BF16), dple_ref[...])

    row = lambda w: pl.BlockSpec((tk, w), lambda k: (k, 0))
    const = lambda s: pl.BlockSpec(s, lambda k: (0, 0))
    return _hosted_call(
        body, "wgrad_misc", (t // tk,),
        in_specs=[row(RET_WIDTH), row(ATTN_WIDTH), row(D_MODEL), row(D_MODEL), row(D_MODEL), row(PLE_DIM),
                  row(D_MODEL)],
        out_specs=[const((D_MODEL, D_MODEL)), const((D_MODEL, D_MODEL)), const((PLE_DIM, D_MODEL))],
        out_shape=[jax.ShapeDtypeStruct((D_MODEL, D_MODEL), F32), jax.ShapeDtypeStruct((D_MODEL, D_MODEL), F32),
                   jax.ShapeDtypeStruct((PLE_DIM, D_MODEL), F32)],
        scratch_shapes=[], operands=(y_ret, y_att, dz1, hb, dsp, p2d, dple), rider=rider, semantics=["arbitrary"])


def _wgrad_ffn(gs, us, dgs, dus, hb, dz2):
    t = dz2.shape[0]
    tk = min(t, 512)

    def body(g_ref, u_ref, dg_ref, du_ref, hb_ref, dz_ref, og_ref, ou_ref, od_ref):
        @pl.when(pl.program_id(1) == 0)
        def _():
            og_ref[...] = jnp.zeros_like(og_ref)
            ou_ref[...] = jnp.zeros_like(ou_ref)
            od_ref[...] = jnp.zeros_like(od_ref)

        hbv = hb_ref[...]
        og_ref[...] += _dot_tn(dg_ref[...], hbv)
        ou_ref[...] += _dot_tn(du_ref[...], hbv)
        gj = g_ref[...].astype(F32)
        act = (gj * _sigmoid(gj) * u_ref[...].astype(F32)).astype(BF16)
        od_ref[...] += _dot_tn(act, dz_ref[...].astype(BF16))

    a_spec = pl.BlockSpec((None, tk, FFN_SHARD), lambda j, k: (j, k, 0))
    b_spec = pl.BlockSpec((tk, D_MODEL), lambda j, k: (k, 0))
    o_spec = pl.BlockSpec((None, FFN_SHARD, D_MODEL), lambda j, k: (j, 0, 0))
    o_shape = jax.ShapeDtypeStruct((N_SHARD, FFN_SHARD, D_MODEL), F32)
    return pl.pallas_call(
        body, name="wgrad_ffn", grid=(N_SHARD, t // tk),
        in_specs=[a_spec, a_spec, a_spec, a_spec, b_spec, b_spec],
        out_specs=[o_spec, o_spec, o_spec], out_shape=[o_shape, o_shape, o_shape],
        compiler_params=_params("parallel", "arbitrary", vmem=VMEM_LIMIT),
    )(gs, us, dgs, dus, hb, dz2)


KV_ORDER = (0, 128, 64, 192)


def _wgrad_in(pieces, x2d):
    t = x2d.shape[0]
    tk = min(t, 512)
    kv0 = CB_AK * LANES

    def body(p0, p1, p2, p3, p4, pkv, x_ref, o_ref):
        @pl.when(pl.program_id(0) == 0)
        def _():
            o_ref[...] = jnp.zeros_like(o_ref)

        xb = x_ref[...].astype(BF16)
        for i, ref in enumerate((p0, p1, p2, p3, p4)):
            o_ref[i * 512:(i + 1) * 512, :] += _dot_tn(ref[...], xb)
        dkv = _dot_tn(pkv[...], xb)
        for i, o in enumerate(KV_ORDER):
            o_ref[kv0 + o:kv0 + o + HEAD_DIM, :] += dkv[i * HEAD_DIM:(i + 1) * HEAD_DIM]

    row = lambda w: pl.BlockSpec((tk, w), lambda k: (k, 0))
    return pl.pallas_call(
        body, name="wgrad_in", grid=(t // tk,),
        in_specs=[row(512)] * 5 + [row(256), row(D_MODEL)],
        out_specs=pl.BlockSpec((IN_WIDTH, D_MODEL), lambda k: (0, 0)),
        out_shape=jax.ShapeDtypeStruct((IN_WIDTH, D_MODEL), F32),
        compiler_params=_params("arbitrary", vmem=VMEM_LIMIT),
    )(*pieces, x2d)


def _inproj_bwd(dz1, pieces, w_main, w_kv, rider=None):
    t = dz1.shape[0]
    tm = 512

    def body(dz_ref, p0, p1, p2, p3, p4, pkv, wm_ref, wkv_ref, o_ref):
        acc = ALPHA * dz_ref[...]
        for i, ref in enumerate((p0, p1, p2, p3, p4)):
            acc = acc + _dot(ref[...], wm_ref[i * 512:(i + 1) * 512, :])
        o_ref[...] = acc + _dot(pkv[...], wkv_ref[...])

    row = lambda w: pl.BlockSpec((tm, w), lambda i: (i, 0))
    const = lambda s: pl.BlockSpec(s, lambda i: (0, 0))
    return _hosted_call(
        body, "inproj_bwd", (t // tm,),
        in_specs=[row(D_MODEL)] + [row(512)] * 5 + [row(256), const(w_main.shape), const(w_kv.shape)],
        out_specs=[row(D_MODEL)],
        out_shape=[jax.ShapeDtypeStruct((t, D_MODEL), F32)],
        scratch_shapes=[], operands=(dz1, *pieces, w_main, w_kv), rider=rider)


def _coords():
    return lax.axis_index("x"), lax.axis_index("y"), lax.axis_index("c")


def _chip_of(x, y, rel):
    return (1 - x if rel & 2 else x), (1 - y if rel & 1 else y)


def _all_gather_weights(shards):
    first = _gather_chips_rider(shards)
    second = _gather_pass_rider([jax.ShapeDtypeStruct((N_SHARD,) + s.shape, s.dtype) for s in shards], chained=True)
    return _run_riders("gather_weights", shards, first.out_shapes, [first, second])


def _run_riders(name, ins, out_shapes, riders):
    n_in, n_out = len(ins), len(out_shapes)

    def body(*refs):
        in_refs, out_refs = refs[:n_in], refs[n_in:n_in + n_out]
        k = n_in + n_out
        for r in riders:
            sems = refs[k:k + len(r.sems)]
            k += len(r.sems)
            r.start(in_refs, out_refs, sems)
            r.finish(in_refs, out_refs, sems)

    hbm = pl.BlockSpec(memory_space=pl.ANY)
    return pl.pallas_call(
        body, name=name, in_specs=[hbm] * n_in, out_specs=[hbm] * n_out, out_shape=list(out_shapes),
        scratch_shapes=[s for r in riders for s in r.sems],
    )(*ins)


def _gather_half(outs, w, chip, cc):
    h = outs[w].shape[1] // 2
    return outs[w].at[chip, pl.ds(cc * h, h), :]


def _gather_chips_rider(shards):
    nw = len(shards)

    def copies(ins, outs, sems):
        send, recv, lsend, lrecv = sems
        x, y, c = _coords()
        me = 2 * x + y
        own = [pltpu.make_async_remote_copy(
            src_ref=ins[w], dst_ref=outs[w].at[me], send_sem=lsend.at[w], recv_sem=lrecv.at[w],
            device_id=(x, y, 1 - c), device_id_type=MESH) for w in range(nw)]
        out, arrive = [], []
        for rel in (1, 2, 3):
            kx, ky = _chip_of(x, y, rel)
            for w in range(nw):
                h = shards[w].shape[0] // 2
                sem = dict(send_sem=send.at[w * 3 + rel - 1], recv_sem=recv.at[w * 3 + rel - 1],
                           device_id=(kx, ky, c), device_id_type=MESH)
                out.append(pltpu.make_async_remote_copy(
                    src_ref=ins[w].at[pl.ds(c * h, h), :], dst_ref=_gather_half(outs, w, me, c), **sem))
                theirs = _gather_half(outs, w, 2 * kx + ky, c)
                arrive.append(pltpu.make_async_remote_copy(src_ref=theirs, dst_ref=theirs, **sem))
        return own, out, arrive

    def start(ins, outs, sems):
        own, out, _ = copies(ins, outs, sems)
        for cp in own + out:
            cp.start()

    def finish(ins, outs, sems):
        own, out, arrive = copies(ins, outs, sems)
        for cp in arrive:
            cp.wait_recv()
        for cp in out:
            cp.wait_send()
        for cp in own:
            cp.wait()

    dma = pltpu.SemaphoreType.DMA
    return _Rider(shards, [jax.ShapeDtypeStruct((N_SHARD,) + s.shape, s.dtype) for s in shards],
                  [dma((3 * nw,)), dma((3 * nw,)), dma((nw,)), dma((nw,))], start, finish)


def _gather_pass_rider(gathered, chained=False):
    nw = len(gathered)

    def copies(outs, sems, cc):
        send, recv = sems
        x, y, c = _coords()
        res = []
        for rel in (1, 2, 3):
            kx, ky = _chip_of(x, y, rel)
            for w in range(nw):
                rows = _gather_half(outs, w, 2 * kx + ky, cc)
                res.append(pltpu.make_async_remote_copy(
                    src_ref=rows, dst_ref=rows, send_sem=send.at[w * 3 + rel - 1], recv_sem=recv.at[w * 3 + rel - 1],
                    device_id=(x, y, 1 - c), device_id_type=MESH))
        return res

    def start(ins, outs, sems):
        for cp in copies(outs, sems, lax.axis_index("c")):
            cp.start()

    def finish(ins, outs, sems):
        c = lax.axis_index("c")
        for cp in copies(outs, sems, 1 - c):
            cp.wait_recv()
        for cp in copies(outs, sems, c):
            cp.wait_send()

    dma = pltpu.SemaphoreType.DMA
    shapes = [jax.ShapeDtypeStruct(g.shape, g.dtype) for g in gathered]
    if chained:
        return _Rider([], [], [dma((3 * nw,)), dma((3 * nw,))], start, finish)
    return _Rider(gathered, shapes, [dma((3 * nw,)), dma((3 * nw,))], start, finish,
                  aliases={w: w for w in range(nw)})


def _exchange_halves_rider(parts):
    nw = len(parts)

    def copies(ins, outs, sems):
        send, recv = sems
        x, y, c = _coords()
        res = []
        for w in range(nw):
            h = parts[w].shape[1] // 2
            res.append(pltpu.make_async_remote_copy(
                src_ref=ins[w].at[:, pl.ds((1 - c) * h, h), :], dst_ref=outs[w],
                send_sem=send.at[w], recv_sem=recv.at[w], device_id=(x, y, 1 - c), device_id_type=MESH))
        return res

    def start(ins, outs, sems):
        for cp in copies(ins, outs, sems):
            cp.start()

    def finish(ins, outs, sems):
        for cp in copies(ins, outs, sems):
            cp.wait()

    dma = pltpu.SemaphoreType.DMA
    return _Rider(parts, [jax.ShapeDtypeStruct((N_SHARD, p.shape[1] // 2, p.shape[2]), F32) for p in parts],
                  [dma((nw,)), dma((nw,))], start, finish)


def _add_halves(parts, theirs, c_arr):
    nw = len(parts)
    split = 2

    def body(c_ref, *refs):
        ins, oth = refs[:nw], refs[nw:2 * nw]
        o32, o16 = refs[2 * nw:3 * nw], refs[3 * nw:]
        for w in range(nw):
            sm = ins[w][...] + oth[w][...]
            o32[w][...] = sm
            o16[w][...] = sm.astype(BF16)

    in_specs, oth_specs, out_specs, shapes32, shapes16 = [], [], [], [], []
    for p in parts:
        hb = p.shape[1] // 2 // split
        in_specs.append(pl.BlockSpec((None, hb, p.shape[2]), lambda j, i, c_ref: (j, c_ref[0] * split + i, 0)))
        oth_specs.append(pl.BlockSpec((None, hb, p.shape[2]), lambda j, i, c_ref: (j, i, 0)))
        shapes32.append(jax.ShapeDtypeStruct((N_SHARD, p.shape[1] // 2, p.shape[2]), F32))
        shapes16.append(jax.ShapeDtypeStruct((N_SHARD, p.shape[1] // 2, p.shape[2]), BF16))
    return pl.pallas_call(
        body, name="add_halves",
        grid_spec=pltpu.PrefetchScalarGridSpec(
            num_scalar_prefetch=1, grid=(N_SHARD, split),
            in_specs=in_specs + oth_specs, out_specs=oth_specs + oth_specs),
        out_shape=shapes32 + shapes16,
        compiler_params=_params("parallel", "parallel", vmem=VMEM_LIMIT),
    )(c_arr, *parts, *theirs)


def _exchange_chips_rider(sums16):
    nw = len(sums16)

    def copies(ins, outs, sems):
        send, recv = sems
        x, y, c = _coords()
        res = []
        for rel in (1, 2, 3):
            kx, ky = _chip_of(x, y, rel)
            for w in range(nw):
                res.append(pltpu.make_async_remote_copy(
                    src_ref=ins[w].at[2 * kx + ky], dst_ref=outs[w].at[rel - 1],
                    send_sem=send.at[w * 3 + rel - 1], recv_sem=recv.at[w * 3 + rel - 1],
                    device_id=(kx, ky, c), device_id_type=MESH))
        return res

    def start(ins, outs, sems):
        for cp in copies(ins, outs, sems):
            cp.start()

    def finish(ins, outs, sems):
        for cp in copies(ins, outs, sems):
            cp.wait()

    dma = pltpu.SemaphoreType.DMA
    return _Rider(sums16, [jax.ShapeDtypeStruct((3,) + s.shape[1:], BF16) for s in sums16],
                  [dma((3 * nw,)), dma((3 * nw,))], start, finish)


def _add_chips(sums32, theirs, pos):
    nw = len(sums32)
    split = 2

    def body(pos_ref, *refs):
        ins, oth, outs = refs[:nw], refs[nw:2 * nw], refs[2 * nw:]
        for w in range(nw):
            acc = ins[w][...]
            for r in range(3):
                acc = acc + oth[w][r].astype(F32)
            outs[w][...] = acc

    in_specs, oth_specs, out_specs, shapes = [], [], [], []
    for s in sums32:
        hb = s.shape[1] // split
        in_specs.append(pl.BlockSpec((None, hb, s.shape[2]), lambda i, pos_ref: (pos_ref[0], i, 0)))
        oth_specs.append(pl.BlockSpec((3, hb, s.shape[2]), lambda i, pos_ref: (0, i, 0)))
        out_specs.append(pl.BlockSpec((hb, s.shape[2]), lambda i, pos_ref: (pos_ref[1] * split + i, 0)))
        shapes.append(jax.ShapeDtypeStruct((2 * s.shape[1], s.shape[2]), F32))
    return pl.pallas_call(
        body, name="add_chips",
        grid_spec=pltpu.PrefetchScalarGridSpec(
            num_scalar_prefetch=1, grid=(split,), in_specs=in_specs + oth_specs, out_specs=out_specs),
        out_shape=shapes,
        compiler_params=_params("parallel", vmem=VMEM_LIMIT),
    )(pos, *sums32, *theirs)


def _join_halves(shards):
    nw = len(shards)

    def body(*refs):
        outs = refs[nw:2 * nw]
        send, recv = refs[2 * nw:]
        x, y, c = _coords()

        def copy(w, cc):
            h = shards[w].shape[0] // 2
            rows = outs[w].at[pl.ds(cc * h, h), :]
            return pltpu.make_async_remote_copy(
                src_ref=rows, dst_ref=rows, send_sem=send.at[w], recv_sem=recv.at[w],
                device_id=(x, y, 1 - c), device_id_type=MESH)

        for w in range(nw):
            copy(w, c).start()
        for w in range(nw):
            copy(w, 1 - c).wait_recv()
            copy(w, c).wait_send()

    hbm = pl.BlockSpec(memory_space=pl.ANY)
    return pl.pallas_call(
        body, name="join_halves",
        in_specs=[hbm] * nw, out_specs=[hbm] * nw,
        out_shape=[jax.ShapeDtypeStruct(s.shape, F32) for s in shards],
        input_output_aliases={w: w for w in range(nw)},
        scratch_shapes=[pltpu.SemaphoreType.DMA((nw,)), pltpu.SemaphoreType.DMA((nw,))],
    )(*shards)


def _adamw_math(w, g, m, v):
    m = ADAM_B1 * m + (1.0 - ADAM_B1) * g
    v = ADAM_B2 * v + (1.0 - ADAM_B2) * (g * g)
    m_hat = m / (1.0 - ADAM_B1 ** ADAM_STEP)
    v_hat = v / (1.0 - ADAM_B2 ** ADAM_STEP)
    delta = -ADAM_LR * (m_hat / (jnp.sqrt(v_hat) + ADAM_EPS) + ADAM_WD * w)
    return delta, m, v


def _adamw(ws, gs, ms, vs):
    nw = len(ws)
    split = 8

    def body(*refs):
        w_r, g_r, m_r, v_r = (refs[i * nw:(i + 1) * nw] for i in range(4))
        d_o, m_o, v_o = (refs[(4 + i) * nw:(5 + i) * nw] for i in range(3))
        for k in range(nw):
            d, m, v = _adamw_math(w_r[k][...], g_r[k][...], m_r[k][...], v_r[k][...])
            d_o[k][...] = d
            m_o[k][...] = m
            v_o[k][...] = v

    specs = [pl.BlockSpec((w.shape[0] // split, w.shape[1]), lambda i: (i, 0)) for w in ws]
    shapes = [jax.ShapeDtypeStruct(w.shape, F32) for w in ws]
    outs = pl.pallas_call(
        body, name="adamw", grid=(split,),
        in_specs=specs * 4, out_specs=specs * 3, out_shape=shapes * 3,
        compiler_params=_params("parallel", vmem=VMEM_LIMIT),
    )(*ws, *gs, *ms, *vs)
    return outs[:nw], outs[nw:2 * nw], outs[2 * nw:]


SMALL_ROWS = 8
SMALL_COLS = D_MODEL
LOSS_COL = RET_WIDTH + 24


def _small_allreduce_adamw(part, w, m, v, rider=None):
    def body(part_ref, w_ref, m_ref, v_ref, g_out, d_out, m_out, v_out, all_ref, send, recv):
        x, y, c = _coords()
        me = 4 * x + 2 * y + c
        all_ref[me] = part_ref[...]
        copies = []
        for rel in range(1, 8):
            px = 1 - x if rel & 4 else x
            py = 1 - y if rel & 2 else y
            pc = 1 - c if rel & 1 else c
            copies.append(pltpu.make_async_remote_copy(
                src_ref=part_ref, dst_ref=all_ref.at[me],
                send_sem=send.at[rel - 1], recv_sem=recv.at[rel - 1], device_id=(px, py, pc), device_id_type=MESH))
        for cp in copies:
            cp.start()
        for cp in copies:
            cp.wait()
        g = all_ref[0]
        for k in range(1, 8):
            g = g + all_ref[k]
        d, mn, vn = _adamw_math(w_ref[...], g, m_ref[...], v_ref[...])
        g_out[...] = g
        d_out[...] = d
        m_out[...] = mn
        v_out[...] = vn

    vm = pl.BlockSpec(memory_space=pltpu.VMEM)
    shape = jax.ShapeDtypeStruct((SMALL_ROWS, SMALL_COLS), F32)
    return _hosted_call(
        body, "small_allreduce_adamw", (1,),
        in_specs=[vm] * 4, out_specs=[vm] * 4, out_shape=[shape] * 4,
        scratch_shapes=[pltpu.VMEM((8, SMALL_ROWS, SMALL_COLS), F32),
                        pltpu.SemaphoreType.DMA((7,)), pltpu.SemaphoreType.DMA((7,))],
        operands=(part, w, m, v), rider=rider, semantics=["arbitrary"])


SMALL_NAMES = ("ret_decay_fwd", "ret_decay_bwd", "attn_sink", "ret_gn_gain",
               "ln1_gain", "ln1_bias", "ln2_gain", "ln2_bias")


LN_NAMES = ("ln1_gain", "ln1_bias", "ln2_gain", "ln2_bias")


def _pack_small(vals, extra=None):
    tail = jnp.zeros((1, 1), F32) if extra is None else extra.reshape(1, 1)
    row4 = jnp.concatenate([vals["ret_gn_gain"], vals["ret_decay_fwd"], vals["ret_decay_bwd"], vals["attn_sink"],
                            tail, jnp.zeros((1, SMALL_COLS - LOSS_COL - 1), F32)], axis=1)
    rows = [vals[n] for n in LN_NAMES] + [row4, jnp.zeros((SMALL_ROWS - 5, SMALL_COLS), F32)]
    return jnp.concatenate(rows, axis=0)


def _unpack_small(packed):
    out = {n: packed[i:i + 1] for i, n in enumerate(LN_NAMES)}
    o = RET_WIDTH
    out.update(ret_gn_gain=packed[4:5, 0:o], ret_decay_fwd=packed[4:5, o:o + 8],
               ret_decay_bwd=packed[4:5, o + 8:o + 16], attn_sink=packed[4:5, o + 16:o + 24])
    return out


def _local_step(x, p, tgt, w_in_t, rest, small, core=None, small_state=None):
    bsz, s, _ = x.shape
    t = bsz * s
    x2d = x.reshape(t, D_MODEL)
    p2d = p.reshape(t, PLE_DIM)
    tgt2d = tgt.reshape(t, D_MODEL)
    dec_f = small["ret_decay_fwd"].reshape(8)
    dec_b = small["ret_decay_bwd"].reshape(8)
    lg_f = jnp.log1p(-jnp.exp2(dec_f))
    lg_b = jnp.log1p(-jnp.exp2(dec_b))
    per_lane = lambda v: jnp.repeat(v, HEAD_DIM).reshape(4, 1, LANES)
    lgf_l, lgb_l = per_lane(lg_f), per_lane(lg_b)
    sink = small["attn_sink"].reshape(8)
    slopes = 2.0 ** (-(jnp.arange(8, dtype=F32) + 1.0))
    gn_gain = small["ret_gn_gain"]
    g1, b1, g2, b2 = (small[n] for n in ("ln1_gain", "ln1_bias", "ln2_gain", "ln2_bias"))

    dist = core is not None
    chips = lambda names: _gather_chips_rider([rest[REST_NAMES.index(n)] for n in names])
    first, second, third = ("w_ffn_up",), ("w_out", "w_ffn_gate", "w_ple_proj", "w_ple_gate"), ("w_ffn_down",)
    u, *c1 = _inproj(x2d, w_in_t, rider=chips(first) if dist else None)
    u3 = u.reshape(bsz, s, IN_WIDTH)
    y_pre, y_ret, *o2 = _ret_fwd(u3, lgf_l, lgb_l, gn_gain,
                                 rider=_merge_riders([_gather_pass_rider(c1), chips(second)]) if dist else None)
    y_att, *o3 = _attn_fwd(u3, slopes, sink, rider=_merge_riders(
        [_gather_pass_rider(o2[len(first):]), chips(third)]) if dist else None)
    gathered = dict(zip(first, o2[:len(first)]))
    gathered.update(zip(second, o3[:len(second)]))
    w_out = _assemble_weights({"w_out": gathered["w_out"]})["w_out"] if dist else rest["w_out"]
    zh1, r1, hb, *o4 = _outproj_ln1(y_ret.reshape(t, RET_WIDTH), y_att.reshape(t, ATTN_WIDTH), x2d, w_out, g1, b1,
                                    rider=_gather_pass_rider(o3[len(second):]) if dist else None)
    gathered.update(zip(third, o4))
    wts = _assemble_weights(gathered) if dist else rest
    dz2, gs, us, pg, ple, sq, dg2, db2 = _ffn_fwd(zh1, hb, p2d, tgt2d, g1, b1, g2, b2, wts["gate4"], wts["up4"],
                                                 wts["down4"], wts["ple_proj"], wts["ple_gate"])
    dgs, dus, dsp, dple, dz1, dyr, dya, dg1, db1 = _ffn_bwd(dz2, gs, us, pg, ple, zh1, r1, g1, wts["gate4"],
                                                          wts["up4"], wts["down4"], wts["ple_gate"], wts["w_out"])
    ffn_parts = list(_wgrad_ffn(gs, us, dgs, dus, hb, dz2))
    d_w_out, d_ple_gate, d_ple_proj, *th_ffn = _wgrad_misc(
        y_ret.reshape(t, RET_WIDTH), y_att.reshape(t, ATTN_WIDTH), dz1, hb, dsp, p2d, dple,
        rider=_exchange_halves_rider(ffn_parts) if dist else None)
    misc_parts = [d_w_out.reshape(N_SHARD, D_MODEL // N_SHARD, D_MODEL),
                  d_ple_proj.reshape(PLE_DIM, N_SHARD, D_MODEL // N_SHARD).transpose(1, 0, 2),
                  d_ple_gate.reshape(N_SHARD, D_MODEL // N_SHARD, D_MODEL)]
    dyr3, dya3 = dyr.reshape(bsz, s, RET_WIDTH), dya.reshape(bsz, s, ATTN_WIDTH)
    if dist:
        s_ffn = _add_halves(ffn_parts, th_ffn, core)
        drq, drk, drv, drg, rpart, *o5 = _ret_bwd(u3, y_pre, dyr3, lgf_l, lgb_l, gn_gain, rider=_merge_riders(
            [_exchange_chips_rider(s_ffn[3:5]), _exchange_halves_rider(misc_parts)]))
        s_misc = _add_halves(misc_parts, o5[2:], core)
        daq, dakv, spart, *o6 = _attn_bwd(u3, dya3, slopes, sink,
                                          rider=_exchange_chips_rider([s_ffn[5]] + list(s_misc[3:])))
    else:
        drq, drk, drv, drg, rpart = _ret_bwd(u3, y_pre, dyr3, lgf_l, lgb_l, gn_gain)
        daq, dakv, spart = _attn_bwd(u3, dya3, slopes, sink)
    pieces = [a.reshape(t, -1) for a in (drq, drk, drv, drg, daq, dakv)]
    kv0 = CB_AK * LANES
    w_kv = jnp.concatenate([w_in_t[kv0 + o:kv0 + o + HEAD_DIM] for o in KV_ORDER], axis=0)
    d_in = _wgrad_in(pieces, x2d).reshape(N_SHARD, FFN_SHARD, D_MODEL)

    rsum = rpart
    lane_heads = lambda row: jnp.sum(row.reshape(4, 2, HEAD_DIM), axis=-1).reshape(8)
    dlg_f = lane_heads(rsum[:, 0, :]) + jnp.stack([jnp.sum(rsum[:, 2, :], -1), jnp.sum(rsum[:, 3, :], -1)], 1).reshape(8)
    dlg_b = lane_heads(rsum[:, 1, :]) + jnp.stack([jnp.sum(rsum[:, 4, :], -1), jnp.sum(rsum[:, 5, :], -1)], 1).reshape(8)
    chain = lambda d: -(math.log(2.0) * jnp.exp2(d)) / (1.0 - jnp.exp2(d))
    grads_small = {
        "ret_decay_fwd": (dlg_f * chain(dec_f)).reshape(1, 8),
        "ret_decay_bwd": (dlg_b * chain(dec_b)).reshape(1, 8),
        "attn_sink": jnp.sum(spart, axis=0)[:, 0:4, 0].reshape(1, 8),
        "ret_gn_gain": rsum[:, 6, :].reshape(1, RET_WIDTH),
        "ln1_gain": dg1, "ln1_bias": db1, "ln2_gain": dg2, "ln2_bias": db2,
    }
    if not dist:
        grad_x, = _inproj_bwd(dz1, pieces, w_in_t[:kv0], w_kv)
        grads_rest = [misc_parts[0]] + ffn_parts + misc_parts[1:]
        return sq[0, 0], grad_x.reshape(bsz, s, D_MODEL), d_in, grads_rest, grads_small
    *small_out, th_in = _small_allreduce_adamw(_pack_small(grads_small, sq[0, 0]), *small_state,
                                               rider=_exchange_halves_rider([d_in]))
    s_in = _add_halves([d_in], [th_in], core)
    grad_x, chips_in = _inproj_bwd(dz1, pieces, w_in_t[:kv0], w_kv, rider=_exchange_chips_rider([s_in[1]]))
    sums32 = [s_in[0], s_misc[0], s_ffn[0], s_ffn[1], s_ffn[2], s_misc[1], s_misc[2]]
    from_chips = [chips_in, o6[1], o5[0], o5[1], o6[0], o6[2], o6[3]]
    return grad_x.reshape(bsz, s, D_MODEL), sums32, from_chips, small_out


BIG_NAMES = ("w_in", "w_out", "w_ffn_gate", "w_ffn_up", "w_ffn_down", "w_ple_proj", "w_ple_gate")
REST_NAMES = BIG_NAMES[1:]
TRANSPOSED = ("w_in", "w_ffn_gate", "w_ffn_up")
WEIGHT_ORDER = ("w_in", "ret_decay_fwd", "ret_decay_bwd", "ret_gn_gain", "attn_sink", "w_out", "ln1_gain",
                "ln1_bias", "w_ffn_gate", "w_ffn_up", "w_ffn_down", "w_ple_proj", "w_ple_gate", "ln2_gain", "ln2_bias")


def _shard_rows(name, a):
    return jnp.swapaxes(a[0], 0, 1) if name in TRANSPOSED else a[0]


def _unshard_rows(name, a):
    return (jnp.swapaxes(a, 0, 1) if name in TRANSPOSED else a)[None]


def _assemble_weights(gathered):
    cols = lambda a: a.transpose(1, 0, 2).reshape(a.shape[1], N_SHARD * a.shape[2])
    rows = lambda a: a.reshape(N_SHARD * a.shape[1], a.shape[2])
    same = lambda a: a
    layout = {"w_out": ("w_out", rows), "w_ffn_gate": ("gate4", same), "w_ffn_up": ("up4", same),
              "w_ffn_down": ("down4", same), "w_ple_proj": ("ple_proj", cols), "w_ple_gate": ("ple_gate", rows)}
    return {layout[n][0]: layout[n][1](a) for n, a in gathered.items()}


def kernel(x, p, w_in, ret_decay_fwd, ret_decay_bwd, ret_gn_gain, attn_sink, w_out, ln1_gain, ln1_bias, w_ffn_gate, w_ffn_up, w_ffn_down, w_ple_proj, w_ple_gate, ln2_gain, ln2_bias, loss_target, m_w_in, m_ret_decay_fwd, m_ret_decay_bwd, m_ret_gn_gain, m_attn_sink, m_w_out, m_ln1_gain, m_ln1_bias, m_w_ffn_gate, m_w_ffn_up, m_w_ffn_down, m_w_ple_proj, m_w_ple_gate, m_ln2_gain, m_ln2_bias, v_w_in, v_ret_decay_fwd, v_ret_decay_bwd, v_ret_gn_gain, v_attn_sink, v_w_out, v_ln1_gain, v_ln1_bias, v_w_ffn_gate, v_w_ffn_up, v_w_ffn_down, v_w_ple_proj, v_w_ple_gate, v_ln2_gain, v_ln2_bias):
    w = dict(w_in=w_in, ret_decay_fwd=ret_decay_fwd, ret_decay_bwd=ret_decay_bwd, ret_gn_gain=ret_gn_gain,
             attn_sink=attn_sink, w_out=w_out, ln1_gain=ln1_gain, ln1_bias=ln1_bias, w_ffn_gate=w_ffn_gate,
             w_ffn_up=w_ffn_up, w_ffn_down=w_ffn_down, w_ple_proj=w_ple_proj, w_ple_gate=w_ple_gate,
             ln2_gain=ln2_gain, ln2_bias=ln2_bias)
    m = dict(w_in=m_w_in, ret_decay_fwd=m_ret_decay_fwd, ret_decay_bwd=m_ret_decay_bwd, ret_gn_gain=m_ret_gn_gain,
             attn_sink=m_attn_sink, w_out=m_w_out, ln1_gain=m_ln1_gain, ln1_bias=m_ln1_bias, w_ffn_gate=m_w_ffn_gate,
             w_ffn_up=m_w_ffn_up, w_ffn_down=m_w_ffn_down, w_ple_proj=m_w_ple_proj, w_ple_gate=m_w_ple_gate,
             ln2_gain=m_ln2_gain, ln2_bias=m_ln2_bias)
    v = dict(w_in=v_w_in, ret_decay_fwd=v_ret_decay_fwd, ret_decay_bwd=v_ret_decay_bwd, ret_gn_gain=v_ret_gn_gain,
             attn_sink=v_attn_sink, w_out=v_w_out, ln1_gain=v_ln1_gain, ln1_bias=v_ln1_bias, w_ffn_gate=v_w_ffn_gate,
             w_ffn_up=v_w_ffn_up, w_ffn_down=v_w_ffn_down, w_ple_proj=v_w_ple_proj, w_ple_gate=v_w_ple_gate,
             ln2_gain=v_ln2_gain, ln2_bias=v_ln2_bias)
    big = lambda d: [_shard_rows(n, d[n]) for n in BIG_NAMES]
    small = lambda d: {n: d[n] for n in SMALL_NAMES}

    chip = 2 * lax.axis_index("x") + lax.axis_index("y")
    core = lax.axis_index("c")
    c_arr = core.astype(jnp.int32).reshape(1)
    pos = jnp.stack([chip, core]).astype(jnp.int32)

    shards = [a.astype(BF16) for a in big(w)]
    (w_in4,) = _all_gather_weights(shards[:1])
    w_in_t = w_in4.reshape(IN_WIDTH, D_MODEL)
    grad_x, sums32, from_chips, (g_s, d_s, m_s, v_s) = _local_step(
        x, p[0], loss_target, w_in_t, shards[1:], small(w), core=c_arr,
        small_state=(_pack_small(small(w)), _pack_small(small(m)), _pack_small(small(v))))
    g_big = _join_halves(_add_chips(sums32, from_chips, pos))
    d_big, m_big, v_big = _adamw(big(w), g_big, big(m), big(v))
    loss = g_s[4, LOSS_COL] * (0.5 / D_MODEL)

    def tree(bigs, packed):
        out = {n: _unshard_rows(n, a) for n, a in zip(BIG_NAMES, bigs)}
        out.update(_unpack_small(packed))
        return [out[n] for n in WEIGHT_ORDER]

    return (loss, grad_x, *tree(g_big, g_s), *tree(d_big, d_s), *tree(m_big, m_s), *tree(v_big, v_s))
```

```python
import functools
import math

import jax
import jax.numpy as jnp
from jax import lax
from jax.experimental import pallas as pl
from jax.experimental.pallas import tpu as pltpu

F32 = jnp.float32
BF16 = jnp.bfloat16

D_MODEL = 1024
HEAD_DIM = 64
RET_HEADS = 8
ATTN_HEADS = 8
RET_WIDTH = 512
ATTN_WIDTH = 512
KV_WIDTH = 128
IN_WIDTH = 2816
FFN = 2816
N_SHARD = 4
FFN_SHARD = FFN // N_SHARD
PLE_DIM = 256
CHUNK = 128
LANES = 128
ALPHA = 2.0 ** 0.25
LN_EPS = 1e-5
GN_EPS = 1e-5
NEG_INF = -1e30
ADAM_LR = 0.001
ADAM_B1 = 0.9
ADAM_B2 = 0.999
ADAM_EPS = 1e-08
ADAM_WD = 0.01
ADAM_STEP = 10
VMEM_LIMIT = 56 * 1024 * 1024
MESH = pl.DeviceIdType.MESH

CB_RQ, CB_RK, CB_RV, CB_RG, CB_AQ, CB_AK, CB_AV = 0, 4, 8, 12, 16, 20, 21


def _dot(a, b):
    return jnp.dot(a, b, preferred_element_type=F32)


def _dot_nt(a, b):
    return lax.dot_general(a, b, (((1,), (1,)), ((), ())), preferred_element_type=F32)


def _dot_tn(a, b):
    return lax.dot_general(a, b, (((0,), (0,)), ((), ())), preferred_element_type=F32)


def _sigmoid(x):
    return 1.0 / (1.0 + jnp.exp(-x))


def _params(*sem, vmem=None):
    return pltpu.CompilerParams(dimension_semantics=tuple(sem) if sem else None, vmem_limit_bytes=vmem)


class _Rider:
    def __init__(self, ins, out_shapes, sems, start, finish, aliases=None):
        self.ins, self.out_shapes, self.sems = list(ins), list(out_shapes), list(sems)
        self.start, self.finish, self.aliases = start, finish, dict(aliases or {})


def _merge_riders(riders):
    riders = [r for r in riders if r is not None]
    if len(riders) == 1:
        return riders[0]
    bounds, aliases = [], {}
    i0 = o0 = s0 = 0
    for r in riders:
        bounds.append((i0, o0, s0))
        aliases.update({i0 + i: o0 + o for i, o in r.aliases.items()})
        i0, o0, s0 = i0 + len(r.ins), o0 + len(r.out_shapes), s0 + len(r.sems)

    def each(method):
        def run(ins, outs, sems):
            for r, (i, o, s) in zip(riders, bounds):
                getattr(r, method)(ins[i:i + len(r.ins)], outs[o:o + len(r.out_shapes)], sems[s:s + len(r.sems)])
        return run

    return _Rider([a for r in riders for a in r.ins], [a for r in riders for a in r.out_shapes],
                  [a for r in riders for a in r.sems], each("start"), each("finish"), aliases)


def _hosted_call(body, name, grid, in_specs, out_specs, out_shape, scratch_shapes, operands, rider=None,
                 semantics=None):
    n_in, n_out, n_scr = len(in_specs), len(out_specs), len(scratch_shapes)
    if rider is None:
        return pl.pallas_call(
            body, name=name, grid=grid, in_specs=in_specs, out_specs=out_specs, out_shape=out_shape,
            scratch_shapes=scratch_shapes,
            compiler_params=_params(*(semantics or ["parallel"] * len(grid)), vmem=VMEM_LIMIT))(*operands)
    r_in, r_out = len(rider.ins), len(rider.out_shapes)

    def full_body(*refs):
        main_in, rin = refs[:n_in], refs[n_in:n_in + r_in]
        o0 = n_in + r_in
        main_out, rout = refs[o0:o0 + n_out], refs[o0 + n_out:o0 + n_out + r_out]
        s0 = o0 + n_out + r_out
        main_scr, rsem = refs[s0:s0 + n_scr], refs[s0 + n_scr:]
        first = functools.reduce(jnp.logical_and, [pl.program_id(a) == 0 for a in range(len(grid))])
        last = functools.reduce(jnp.logical_and, [pl.program_id(a) == g - 1 for a, g in enumerate(grid)])

        @pl.when(first)
        def _():
            rider.start(rin, rout, rsem)

        body(*main_in, *main_out, *main_scr)

        @pl.when(last)
        def _():
            rider.finish(rin, rout, rsem)

    hbm = pl.BlockSpec(memory_space=pl.ANY)
    return pl.pallas_call(
        full_body, name=name, grid=grid,
        in_specs=list(in_specs) + [hbm] * r_in, out_specs=list(out_specs) + [hbm] * r_out,
        out_shape=list(out_shape) + rider.out_shapes,
        scratch_shapes=list(scratch_shapes) + rider.sems,
        input_output_aliases={n_in + i: n_out + o for i, o in rider.aliases.items()},
        compiler_params=_params(*(["arbitrary"] * len(grid)), vmem=VMEM_LIMIT),
    )(*operands, *rider.ins)


def _head_mean(x, m0):
    s0 = jnp.sum(jnp.where(m0, x, 0.0), axis=1, keepdims=True)
    s1 = jnp.sum(jnp.where(m0, 0.0, x), axis=1, keepdims=True)
    return jnp.where(m0, s0, s1) * (1.0 / HEAD_DIM)


def _inproj(x2d, w_in_t, rider=None):
    t = x2d.shape[0]
    tm = 512
    nb = 256

    def body(x_ref, w_ref, o_ref):
        xb = x_ref[...].astype(BF16)
        for n in range(0, IN_WIDTH, nb):
            o_ref[:, n:n + nb] = _dot_nt(xb, w_ref[n:n + nb, :]).astype(BF16)

    return _hosted_call(
        body, "inproj", (t // tm,),
        in_specs=[pl.BlockSpec((tm, D_MODEL), lambda i: (i, 0)),
                  pl.BlockSpec((IN_WIDTH, D_MODEL), lambda i: (0, 0))],
        out_specs=[pl.BlockSpec((tm, IN_WIDTH), lambda i: (i, 0))],
        out_shape=[jax.ShapeDtypeStruct((t, IN_WIDTH), BF16)],
        scratch_shapes=[], operands=(x2d, w_in_t), rider=rider)


def _outproj_ln1(y_ret, y_att, x2d, w_out, gain, bias, rider=None):
    t = x2d.shape[0]
    tm = 512

    def body(yr_ref, ya_ref, x_ref, w_ref, g_ref, b_ref, zh_ref, r_ref, hb_ref):
        mix = _dot(yr_ref[...], w_ref[0:RET_WIDTH, :]) + _dot(ya_ref[...], w_ref[RET_WIDTH:, :])
        z = ALPHA * x_ref[...] + mix
        mu = jnp.mean(z, axis=1, keepdims=True)
        zc = z - mu
        var = jnp.mean(zc * zc, axis=1, keepdims=True)
        r = lax.rsqrt(var + LN_EPS)
        zh = zc * r
        zh_ref[...] = zh
        r_ref[...] = r
        hb_ref[...] = (zh * g_ref[...] + b_ref[...]).astype(BF16)

    row = lambda w: pl.BlockSpec((tm, w), lambda i: (i, 0))
    const = lambda s: pl.BlockSpec(s, lambda i: (0, 0))
    return _hosted_call(
        body, "outproj_ln1", (t // tm,),
        in_specs=[row(RET_WIDTH), row(ATTN_WIDTH), row(D_MODEL), const((D_MODEL, D_MODEL)),
                  const((1, D_MODEL)), const((1, D_MODEL))],
        out_specs=[row(D_MODEL), row(1), row(D_MODEL)],
        out_shape=[jax.ShapeDtypeStruct((t, D_MODEL), F32), jax.ShapeDtypeStruct((t, 1), F32),
                   jax.ShapeDtypeStruct((t, D_MODEL), BF16)],
        scratch_shapes=[], operands=(y_ret, y_att, x2d, w_out, gain, bias), rider=rider)


def _load_resident(step, pairs):
    @pl.when(step == 0)
    def _():
        for src, dst in pairs:
            pltpu.sync_copy(src, dst)


def _ffn_fwd(zh1, hb, p2d, tgt, g1, b1, g2, b2, wg4, wu4, wd4, wpe, wpg):
    t = zh1.shape[0]
    tm = 256

    def body(zh_ref, hb_ref, p_ref, t_ref, g1_ref, b1_ref, g2_ref, b2_ref,
             wg_hbm, wu_hbm, wd_hbm, wpe_hbm, wpg_hbm,
             dz_ref, dzb_ref, gs_ref, us_ref, act_ref, pg_ref, ple_ref, loss_ref, dg2_ref, db2_ref,
             wg, wu, wd, wpe, wpg):
        step = pl.program_id(0)
        _load_resident(step, [(wg_hbm, wg), (wu_hbm, wu), (wd_hbm, wd), (wpe_hbm, wpe), (wpg_hbm, wpg)])

        @pl.when(step == 0)
        def _():
            loss_ref[...] = jnp.zeros_like(loss_ref)
            dg2_ref[...] = jnp.zeros_like(dg2_ref)
            db2_ref[...] = jnp.zeros_like(db2_ref)

        h1 = zh_ref[...] * g1_ref[...] + b1_ref[...]
        hbv = hb_ref[...]
        ffn = jnp.zeros((tm, D_MODEL), F32)
        acts = []
        for j in range(N_SHARD + 1):
            if j < N_SHARD:
                gj = _dot_nt(hbv, wg[j])
                uj = _dot_nt(hbv, wu[j])
                gs_ref[j] = gj.astype(BF16)
                us_ref[j] = uj.astype(BF16)
                acts.append((gj * _sigmoid(gj) * uj).astype(BF16))
                act_ref[j] = acts[j]
            if j > 0:
                ffn = ffn + _dot(acts[j - 1], wd[j - 1])
        ple = _dot(p_ref[...].astype(BF16), wpe[...])
        pg = _sigmoid(_dot(hbv, wpg[...]))
        pg_ref[...] = pg.astype(BF16)
        ple_ref[...] = ple.astype(BF16)
        z2 = ALPHA * h1 + ffn + pg * ple
        mu = jnp.mean(z2, axis=1, keepdims=True)
        zc = z2 - mu
        var = jnp.mean(zc * zc, axis=1, keepdims=True)
        r = lax.rsqrt(var + LN_EPS)
        zh2 = zc * r
        err = zh2 * g2_ref[...] + b2_ref[...] - t_ref[...]
        loss_ref[...] += jnp.sum(err * err)
        dy = err * (1.0 / D_MODEL)
        dg2_ref[...] += jnp.sum(dy * zh2, axis=0, keepdims=True)
        db2_ref[...] += jnp.sum(dy, axis=0, keepdims=True)
        dzh = dy * g2_ref[...]
        m1 = jnp.mean(dzh, axis=1, keepdims=True)
        m2 = jnp.mean(dzh * zh2, axis=1, keepdims=True)
        dz2 = r * (dzh - m1 - zh2 * m2)
        dz_ref[...] = dz2
        dzb_ref[...] = dz2.astype(BF16)

    row = lambda w: pl.BlockSpec((tm, w), lambda i: (i, 0))
    const = lambda s: pl.BlockSpec(s, lambda i: (0, 0))
    sh = pl.BlockSpec((N_SHARD, tm, FFN_SHARD), lambda i: (0, i, 0))
    sh_shape = jax.ShapeDtypeStruct((N_SHARD, t, FFN_SHARD), BF16)
    hbm = pl.BlockSpec(memory_space=pl.ANY)
    return pl.pallas_call(
        body, name="ffn_fwd", grid=(t // tm,),
        in_specs=[row(D_MODEL), row(D_MODEL), row(PLE_DIM), row(D_MODEL),
                  const((1, D_MODEL)), const((1, D_MODEL)), const((1, D_MODEL)), const((1, D_MODEL)),
                  hbm, hbm, hbm, hbm, hbm],
        out_specs=[row(D_MODEL), row(D_MODEL), sh, sh, sh, row(D_MODEL), row(D_MODEL),
                   const((8, LANES)), const((1, D_MODEL)), const((1, D_MODEL))],
        out_shape=[jax.ShapeDtypeStruct((t, D_MODEL), F32), jax.ShapeDtypeStruct((t, D_MODEL), BF16),
                   sh_shape, sh_shape, sh_shape,
                   jax.ShapeDtypeStruct((t, D_MODEL), BF16), jax.ShapeDtypeStruct((t, D_MODEL), BF16),
                   jax.ShapeDtypeStruct((8, LANES), F32),
                   jax.ShapeDtypeStruct((1, D_MODEL), F32), jax.ShapeDtypeStruct((1, D_MODEL), F32)],
        scratch_shapes=[pltpu.VMEM(wg4.shape, BF16), pltpu.VMEM(wu4.shape, BF16), pltpu.VMEM(wd4.shape, BF16),
                        pltpu.VMEM(wpe.shape, BF16), pltpu.VMEM(wpg.shape, BF16)],
        compiler_params=_params("arbitrary", vmem=VMEM_LIMIT),
    )(zh1, hb, p2d, tgt, g1, b1, g2, b2, wg4, wu4, wd4, wpe, wpg)


def _ret_tables(lgf, lgb):
    c = CHUNK
    row = lax.broadcasted_iota(jnp.int32, (c, LANES), 0).astype(F32)
    ii = lax.broadcasted_iota(jnp.int32, (c, c), 0).astype(F32)
    jj = lax.broadcasted_iota(jnp.int32, (c, c), 1).astype(F32)
    diff = ii - jj
    dmats = []
    for h in range(2):
        lf = lgf[:, h * HEAD_DIM:h * HEAD_DIM + 1]
        lb = lgb[:, h * HEAD_DIM:h * HEAD_DIM + 1]
        dmats.append(jnp.where(diff > 0, jnp.exp(lf * jnp.maximum(diff, 0.0)),
                               jnp.where(diff < 0, jnp.exp(lb * jnp.maximum(-diff, 0.0)), 2.0)))
    tab = dict(
        qdec_f=jnp.exp(lgf * (row + 1.0)), kdec_f=jnp.exp(lgf * (c - 1.0 - row)),
        qdec_b=jnp.exp(lgb * (c - row)), kdec_b=jnp.exp(lgb * row),
        cdec_f=jnp.exp(lgf * c), cdec_b=jnp.exp(lgb * c),
        d0=dmats[0], d1=dmats[1], row=row, diff=diff)
    r = lax.broadcasted_iota(jnp.int32, (LANES, LANES), 0) < HEAD_DIM
    cc = lax.broadcasted_iota(jnp.int32, (LANES, LANES), 1) < HEAD_DIM
    tab["bd"] = r == cc
    tab["m0"] = lax.broadcasted_iota(jnp.int32, (c, LANES), 1) < HEAD_DIM
    return tab


def _ret_specs(bsz, s):
    blk = lambda cb: pl.BlockSpec((bsz, s, LANES), lambda p, cb=cb: (0, 0, cb + p))
    lane = pl.BlockSpec((None, 1, LANES), lambda p: (p, 0, 0))
    gain = pl.BlockSpec((1, LANES), lambda p: (0, p))
    pair = pl.BlockSpec((bsz, s, LANES), lambda p: (0, 0, p))
    return blk, lane, gain, pair


def _ret_kv_states(tb, k_ref, v_ref, rb_ref, kvf_ref, n_chunk):
    c = CHUNK
    bsz = k_ref.shape[0]
    bd = tb["bd"]

    def step(i, rbs):
        n = n_chunk - 1 - i
        sl = pl.ds(pl.multiple_of(n * c, c), c)
        kfb = []
        for b in range(bsz):
            k32 = k_ref[b, sl, :].astype(F32)
            kfb.append(jnp.concatenate([k32 * tb["kdec_f"], k32 * tb["kdec_b"]], axis=1).astype(BF16))
        kvs = [_dot_tn(kfb[b], v_ref[b, sl, :]) for b in range(bsz)]
        new = []
        for b in range(bsz):
            rb_ref[b, n] = rbs[b]
            kvf_ref[b, n] = jnp.where(bd, kvs[b][0:LANES], 0.0)
            new.append(rbs[b] * tb["cdec_b"] + jnp.where(bd, kvs[b][LANES:], 0.0))
        return tuple(new)

    lax.fori_loop(0, n_chunk, step, tuple(jnp.zeros((LANES, LANES), F32) for _ in range(bsz)))


def _split_rows(x, m0):
    return jnp.concatenate([jnp.where(m0, x, 0.0), jnp.where(m0, 0.0, x)], axis=0).astype(BF16)


def _ret_fwd(u3, lgf_l, lgb_l, gn_gain, rider=None):
    bsz, s, _ = u3.shape
    n_chunk = s // CHUNK
    c = CHUNK

    def body(q_ref, k_ref, v_ref, g_ref, lgf_ref, lgb_ref, gain_ref, y_ref, o_ref, rb_ref, kvf_ref):
        tb = _ret_tables(lgf_ref[...], lgb_ref[...])
        m0 = tb["m0"]
        gain = gain_ref[...]
        rows = range(bsz)
        _ret_kv_states(tb, k_ref, v_ref, rb_ref, kvf_ref, n_chunk)

        def chunk(n, rfs):
            sl = pl.ds(pl.multiple_of(n * c, c), c)
            qs = [q_ref[b, sl, :].astype(F32) * 0.125 for b in rows]
            s01 = [_dot_nt(_split_rows(qs[b], m0), k_ref[b, sl, :]) for b in rows]
            ys = []
            for b in rows:
                lhs = jnp.concatenate([s01[b][0:c] * tb["d0"], s01[b][c:] * tb["d1"],
                                       qs[b] * tb["qdec_f"], qs[b] * tb["qdec_b"]], axis=1).astype(BF16)
                rhs = jnp.concatenate([_split_rows(v_ref[b, sl, :].astype(F32), m0),
                                       rfs[b].astype(BF16), rb_ref[b, n].astype(BF16)], axis=0)
                ys.append(_dot(lhs, rhs))
            new = []
            for b in rows:
                y = ys[b]
                mu = _head_mean(y, m0)
                yc = y - mu
                var = _head_mean(yc * yc, m0)
                yh = yc * lax.rsqrt(var + GN_EPS)
                g = g_ref[b, sl, :].astype(F32)
                y_ref[b, sl, :] = y
                o_ref[b, sl, :] = (yh * gain * (g * _sigmoid(g))).astype(BF16)
                new.append(rfs[b] * tb["cdec_f"] + kvf_ref[b, n])
            return tuple(new)

        lax.fori_loop(0, n_chunk, chunk, tuple(jnp.zeros((LANES, LANES), F32) for _ in rows))

    blk, lane, gain, pair = _ret_specs(bsz, s)
    state = pltpu.VMEM((bsz, n_chunk, LANES, LANES), F32)
    return _hosted_call(
        body, "ret_fwd", (4,),
        in_specs=[blk(CB_RQ), blk(CB_RK), blk(CB_RV), blk(CB_RG), lane, lane, gain],
        out_specs=[pair, pair],
        out_shape=[jax.ShapeDtypeStruct((bsz, s, RET_WIDTH), F32), jax.ShapeDtypeStruct((bsz, s, RET_WIDTH), BF16)],
        scratch_shapes=[state, state],
        operands=(u3, u3, u3, u3, lgf_l, lgb_l, gn_gain), rider=rider)


def _ret_bwd(u3, y_pre, d_o, lgf_l, lgb_l, gn_gain, rider=None):
    bsz, s, _ = u3.shape
    n_chunk = s // CHUNK
    c = CHUNK

    def body(q_ref, k_ref, v_ref, g_ref, y_ref, do_ref, lgf_ref, lgb_ref, gain_ref,
             dq_ref, dk_ref, dv_ref, dg_ref, part_ref,
             rb_ref, kvf_ref, rf_ref, dirf_ref, dy_ref, dk_acc, dv_acc, pa0, pa1, vec_ref):
        tb = _ret_tables(lgf_ref[...], lgb_ref[...])
        m0, bd, row = tb["m0"], tb["bd"], tb["row"]
        gain = gain_ref[...]
        wf = jnp.maximum(tb["diff"], 0.0)
        wb = jnp.maximum(-tb["diff"], 0.0)
        rows = range(bsz)
        zero_states = tuple(jnp.zeros((LANES, LANES), F32) for _ in rows)
        for ref in (pa0, pa1):
            ref[...] = jnp.zeros_like(ref)
        vec_ref[...] = jnp.zeros_like(vec_ref)
        _ret_kv_states(tb, k_ref, v_ref, rb_ref, kvf_ref, n_chunk)

        def sweep_fwd(n, carry):
            rfs, gbs = carry
            sl = pl.ds(pl.multiple_of(n * c, c), c)
            qs, ks, vs, dys, dybs, q01, k01, dy01 = [], [], [], [], [], [], [], []
            dgain = jnp.zeros((1, LANES), F32)
            for b in rows:
                q = q_ref[b, sl, :].astype(F32) * 0.125
                k = k_ref[b, sl, :]
                y = y_ref[b, sl, :]
                do = do_ref[b, sl, :].astype(F32)
                g = g_ref[b, sl, :].astype(F32)
                mu = _head_mean(y, m0)
                yc = y - mu
                rstd = lax.rsqrt(_head_mean(yc * yc, m0) + GN_EPS)
                yh = yc * rstd
                sg = _sigmoid(g)
                sil = g * sg
                dyh = do * gain * sil
                dg_ref[b, sl, :] = (do * yh * gain * sg * (1.0 + g * (1.0 - sg))).astype(BF16)
                dgain = dgain + jnp.sum(do * yh * sil, axis=0, keepdims=True)
                dy = rstd * (dyh - _head_mean(dyh, m0) - yh * _head_mean(dyh * yh, m0))
                dyb = dy.astype(BF16)
                dy_ref[b, sl, :] = dyb
                rf_ref[b, n] = rfs[b]
                qs.append(q)
                ks.append(k)
                vs.append(v_ref[b, sl, :])
                dys.append(dy)
                dybs.append(dyb)
                q01.append(_split_rows(q, m0))
                k01.append(_split_rows(k.astype(F32), m0))
                dy01.append(_split_rows(dy, m0))
            s01 = [_dot_nt(q01[b], ks[b]) for b in rows]
            da01 = [_dot_nt(dy01[b], vs[b]) for b in rows]
            rbn = [rb_ref[b, n] for b in rows]
            states = [jnp.concatenate([rfs[b], rbn[b]], axis=0).astype(BF16) for b in rows]
            dqc = [_dot_nt(dybs[b], states[b]) for b in rows]
            gbb = [gbs[b].astype(BF16) for b in rows]
            dkb = [_dot_nt(vs[b], gbb[b]) for b in rows]
            qfb = [jnp.concatenate([qs[b] * tb["qdec_f"], qs[b] * tb["qdec_b"]], axis=1) for b in rows]
            direct = [_dot_tn(qfb[b].astype(BF16), dybs[b]) for b in rows]
            ds_cat, ds_rows, a_rows = [], [], []
            for b in rows:
                a0 = s01[b][0:c] * tb["d0"]
                a1 = s01[b][c:] * tb["d1"]
                pa0[...] += da01[b][0:c] * a0
                pa1[...] += da01[b][c:] * a1
                ds0 = da01[b][0:c] * tb["d0"]
                ds1 = da01[b][c:] * tb["d1"]
                ds_cat.append(jnp.concatenate([ds0, ds1], axis=1).astype(BF16))
                ds_rows.append(jnp.concatenate([ds0, ds1], axis=0).astype(BF16))
                a_rows.append(jnp.concatenate([a0, a1], axis=0).astype(BF16))
            kbd = [ks[b].astype(F32) * tb["kdec_b"] for b in rows]
            dq_in = [_dot(ds_cat[b], k01[b]) for b in rows]
            dk_in = [_dot_tn(ds_rows[b], q01[b]) for b in rows]
            dv_in = [_dot_tn(a_rows[b], dy01[b]) for b in rows]
            dv_gb = [_dot(kbd[b].astype(BF16), gbb[b]) for b in rows]
            new_rf, new_gb = [], []
            dlf = jnp.zeros((1, LANES), F32)
            dlb = jnp.zeros((1, LANES), F32)
            for b in rows:
                dqf, dqb = dqc[b][:, 0:LANES], dqc[b][:, LANES:]
                qf, qb = qfb[b][:, 0:LANES], qfb[b][:, LANES:]
                dq = dq_in[b] + dqf * tb["qdec_f"] + dqb * tb["qdec_b"]
                dq_ref[b, sl, :] = (dq * 0.125).astype(BF16)
                dk_acc[b, sl, :] = dk_in[b] + dkb[b] * tb["kdec_b"]
                dv_acc[b, sl, :] = dv_in[b] + dv_gb[b]
                dlf = dlf + jnp.sum((row + 1.0) * qf * dqf, axis=0, keepdims=True)
                dlb = dlb + jnp.sum((c - row) * qb * dqb + row * kbd[b] * dkb[b], axis=0, keepdims=True)
                dlb = dlb + c * tb["cdec_b"] * jnp.sum(gbs[b] * rbn[b], axis=0, keepdims=True)
                dirf_ref[b, n] = jnp.where(bd, direct[b][0:LANES], 0.0)
                new_gb.append(jnp.where(bd, direct[b][LANES:], 0.0) + tb["cdec_b"] * gbs[b])
                new_rf.append(rfs[b] * tb["cdec_f"] + kvf_ref[b, n])
            vec_ref[0:1, :] += dlf
            vec_ref[1:2, :] += dlb
            vec_ref[6:7, :] += dgain
            return tuple(new_rf), tuple(new_gb)

        lax.fori_loop(0, n_chunk, sweep_fwd, (zero_states, zero_states))

        def sweep_bwd(i, gfs):
            n = n_chunk - 1 - i
            sl = pl.ds(pl.multiple_of(n * c, c), c)
            gfb = [gfs[b].astype(BF16) for b in rows]
            kfd = [k_ref[b, sl, :].astype(F32) * tb["kdec_f"] for b in rows]
            dkf = [_dot_nt(v_ref[b, sl, :], gfb[b]) for b in rows]
            dvf = [_dot(kfd[b].astype(BF16), gfb[b]) for b in rows]
            new = []
            dlf = jnp.zeros((1, LANES), F32)
            for b in rows:
                dk_ref[b, sl, :] = (dk_acc[b, sl, :] + dkf[b] * tb["kdec_f"]).astype(BF16)
                dv_ref[b, sl, :] = (dv_acc[b, sl, :] + dvf[b]).astype(BF16)
                dlf = dlf + jnp.sum((c - 1.0 - row) * kfd[b] * dkf[b], axis=0, keepdims=True)
                dlf = dlf + c * tb["cdec_f"] * jnp.sum(gfs[b] * rf_ref[b, n], axis=0, keepdims=True)
                new.append(dirf_ref[b, n] + tb["cdec_f"] * gfs[b])
            vec_ref[0:1, :] += dlf
            return tuple(new)

        lax.fori_loop(0, n_chunk, sweep_bwd, zero_states)
        vec_ref[2:3, :] = jnp.sum(pa0[...] * wf, axis=0, keepdims=True)
        vec_ref[3:4, :] = jnp.sum(pa1[...] * wf, axis=0, keepdims=True)
        vec_ref[4:5, :] = jnp.sum(pa0[...] * wb, axis=0, keepdims=True)
        vec_ref[5:6, :] = jnp.sum(pa1[...] * wb, axis=0, keepdims=True)
        part_ref[...] = vec_ref[...]

    blk, lane, gain, pair = _ret_specs(bsz, s)
    out_bf = jax.ShapeDtypeStruct((bsz, s, RET_WIDTH), BF16)
    state = pltpu.VMEM((bsz, n_chunk, LANES, LANES), F32)
    return _hosted_call(
        body, "ret_bwd", (4,),
        in_specs=[blk(CB_RQ), blk(CB_RK), blk(CB_RV), blk(CB_RG), pair, pair, lane, lane, gain],
        out_specs=[pair, pair, pair, pair, pl.BlockSpec((None, 8, LANES), lambda p: (p, 0, 0))],
        out_shape=[out_bf, out_bf, out_bf, out_bf, jax.ShapeDtypeStruct((4, 8, LANES), F32)],
        scratch_shapes=[state, state, state, state,
                        pltpu.VMEM((bsz, s, LANES), BF16), pltpu.VMEM((bsz, s, LANES), F32),
                        pltpu.VMEM((bsz, s, LANES), F32),
                        pltpu.VMEM((c, c), F32), pltpu.VMEM((c, c), F32), pltpu.VMEM((8, LANES), F32)],
        operands=(u3, u3, u3, u3, y_pre, d_o, lgf_l, lgb_l, gn_gain), rider=rider)


def _attn_window_tables(n, s):
    qi = lax.broadcasted_iota(jnp.int32, (CHUNK, 3 * CHUNK), 0)
    kj = lax.broadcasted_iota(jnp.int32, (CHUNK, 3 * CHUNK), 1)
    dist = jnp.abs(kj - CHUNK - qi)
    kpos = n * CHUNK - CHUNK + kj
    valid = (dist <= CHUNK) & (kpos >= 0) & (kpos < s)
    return dist.astype(F32), valid


def _dup_kv_head(x, g):
    lane = lax.broadcasted_iota(jnp.int32, x.shape, 1)
    keep = (lane < HEAD_DIM) == (g == 0)
    xf = x.astype(F32)
    return jnp.where(keep, xf, pltpu.roll(xf, HEAD_DIM, 1))


def _attn_specs(s):
    q = pl.BlockSpec((None, s, 2 * LANES), lambda b, g: (b, 0, CB_AQ // 2 + g))
    k = pl.BlockSpec((None, s, LANES), lambda b, g: (b, 0, CB_AK))
    v = pl.BlockSpec((None, s, LANES), lambda b, g: (b, 0, CB_AV))
    grp = pl.BlockSpec((None, s, 2 * LANES), lambda b, g: (b, 0, g))
    smem = pl.BlockSpec(memory_space=pltpu.SMEM)
    return q, k, v, grp, smem


def _fill_padded(dst_ref, val, s):
    dst_ref[0:CHUNK, :] = jnp.zeros((CHUNK, LANES), dst_ref.dtype)
    dst_ref[CHUNK:CHUNK + s, :] = val.astype(dst_ref.dtype)
    dst_ref[CHUNK + s:2 * CHUNK + s, :] = jnp.zeros((CHUNK, LANES), dst_ref.dtype)


def _attn_probs(sc, slope, snk, dist, valid):
    sc = jnp.where(valid, sc - slope * dist, NEG_INF)
    m = jnp.maximum(jnp.max(sc, axis=1, keepdims=True), snk)
    e = jnp.exp(sc - m)
    es = jnp.exp(snk - m)
    inv = 1.0 / (jnp.sum(e, axis=1, keepdims=True) + es)
    return e * inv, es * inv


def _stack_heads(x2, m0):
    parts = []
    for pr in range(2):
        xp = x2[:, pr * LANES:(pr + 1) * LANES]
        parts += [jnp.where(m0, xp, 0.0), jnp.where(m0, 0.0, xp)]
    return jnp.concatenate(parts, axis=0).astype(BF16)


def _unstack_pair(x_all, pr, m0):
    return jnp.where(m0, x_all[(2 * pr) * CHUNK:(2 * pr + 1) * CHUNK], x_all[(2 * pr + 1) * CHUNK:(2 * pr + 2) * CHUNK])


def _attn_fwd(u3, slopes, sink, rider=None):
    bsz, s, _ = u3.shape
    n_blk = s // CHUNK

    def body(slope_ref, sink_ref, q_ref, k_ref, v_ref, o_ref, kp_ref, vp_ref):
        g = pl.program_id(1)
        _fill_padded(kp_ref, _dup_kv_head(k_ref[...], g), s)
        _fill_padded(vp_ref, _dup_kv_head(v_ref[...], g), s)
        m0 = lax.broadcasted_iota(jnp.int32, (CHUNK, LANES), 1) < HEAD_DIM

        def blk(n, carry):
            r0 = pl.multiple_of(n * CHUNK, CHUNK)
            kw = kp_ref[pl.ds(r0, 3 * CHUNK), :]
            vw = vp_ref[pl.ds(r0, 3 * CHUNK), :]
            dist, valid = _attn_window_tables(n, s)
            q_all = _stack_heads(q_ref[pl.ds(r0, CHUNK), :].astype(F32) * 0.125, m0)
            sc_all = _dot_nt(q_all, kw)
            probs = []
            for i in range(4):
                p, _ = _attn_probs(sc_all[i * CHUNK:(i + 1) * CHUNK], slope_ref[g * 4 + i], sink_ref[g * 4 + i],
                                   dist, valid)
                probs.append(p.astype(BF16))
            out_all = _dot(jnp.concatenate(probs, axis=0), vw)
            for pr in range(2):
                o_ref[pl.ds(r0, CHUNK), pr * LANES:(pr + 1) * LANES] = _unstack_pair(out_all, pr, m0).astype(BF16)
            return carry

        lax.fori_loop(0, n_blk, blk, 0)

    q, k, v, grp, smem = _attn_specs(s)
    return _hosted_call(
        body, "attn_fwd", (bsz, 2),
        in_specs=[smem, smem, q, k, v],
        out_specs=[grp],
        out_shape=[jax.ShapeDtypeStruct((bsz, s, ATTN_WIDTH), BF16)],
        scratch_shapes=[pltpu.VMEM((s + 2 * CHUNK, LANES), BF16), pltpu.VMEM((s + 2 * CHUNK, LANES), BF16)],
        operands=(slopes, sink, u3, u3, u3), rider=rider)


def _attn_bwd(u3, d_o, slopes, sink, rider=None):
    bsz, s, _ = u3.shape
    n_blk = s // CHUNK

    def body(slope_ref, sink_ref, q_ref, k_ref, v_ref, do_ref, dq_ref, dkv_ref, ds_ref,
             kp_ref, vp_ref, dk_acc, dv_acc):
        g = pl.program_id(1)
        _fill_padded(kp_ref, _dup_kv_head(k_ref[...], g), s)
        _fill_padded(vp_ref, _dup_kv_head(v_ref[...], g), s)
        dk_acc[...] = jnp.zeros_like(dk_acc)
        dv_acc[...] = jnp.zeros_like(dv_acc)
        m0 = lax.broadcasted_iota(jnp.int32, (CHUNK, LANES), 1) < HEAD_DIM

        def blk(n, dsink):
            r0 = pl.multiple_of(n * CHUNK, CHUNK)
            win = pl.ds(r0, 3 * CHUNK)
            kw = kp_ref[win, :]
            vw = vp_ref[win, :]
            dist, valid = _attn_window_tables(n, s)
            q_all = _stack_heads(q_ref[pl.ds(r0, CHUNK), :].astype(F32) * 0.125, m0)
            do_all = _stack_heads(do_ref[pl.ds(r0, CHUNK), :].astype(F32), m0)
            sc_all = _dot_nt(q_all, kw)
            dp_all = _dot_nt(do_all, vw)
            new_dsink, probs, dscs = [], [], []
            for i in range(4):
                rows = slice(i * CHUNK, (i + 1) * CHUNK)
                p, ps = _attn_probs(sc_all[rows], slope_ref[g * 4 + i], sink_ref[g * 4 + i], dist, valid)
                dp = dp_all[rows]
                delta = jnp.sum(p * dp, axis=1, keepdims=True)
                dscs.append((p * (dp - delta)).astype(BF16))
                probs.append(p.astype(BF16))
                dsh = jnp.sum(ps * delta, axis=0, keepdims=True)
                new_dsink.append(dsink[i] - jnp.broadcast_to(dsh, (1, LANES)))
            dsc_all = jnp.concatenate(dscs, axis=0)
            dq_all = _dot(dsc_all, kw)
            dk_acc[win, :] += _dot_tn(dsc_all, q_all)
            dv_acc[win, :] += _dot_tn(jnp.concatenate(probs, axis=0), do_all)
            for pr in range(2):
                dq_ref[pl.ds(r0, CHUNK), pr * LANES:(pr + 1) * LANES] = (
                    _unstack_pair(dq_all, pr, m0) * 0.125).astype(BF16)
            return tuple(new_dsink)

        dsink = lax.fori_loop(0, n_blk, blk, tuple(jnp.zeros((1, LANES), F32) for _ in range(4)))
        dk = dk_acc[CHUNK:CHUNK + s, :]
        dv = dv_acc[CHUNK:CHUNK + s, :]
        lane = lax.broadcasted_iota(jnp.int32, (s, LANES), 1)
        fold = lambda a: a + pltpu.roll(a, HEAD_DIM, 1)
        dkv_ref[...] = jnp.where(lane < HEAD_DIM, fold(dk), fold(dv)).astype(BF16)
        ds_ref[...] = jnp.zeros_like(ds_ref)
        for i in range(4):
            ds_ref[i:i + 1, :] = dsink[i]

    q, k, v, grp, smem = _attn_specs(s)
    return _hosted_call(
        body, "attn_bwd", (bsz, 2),
        in_specs=[smem, smem, q, k, v, grp],
        out_specs=[grp, pl.BlockSpec((None, s, LANES), lambda b, g: (b, 0, g)),
                   pl.BlockSpec((None, None, 8, LANES), lambda b, g: (b, g, 0, 0))],
        out_shape=[jax.ShapeDtypeStruct((bsz, s, ATTN_WIDTH), BF16), jax.ShapeDtypeStruct((bsz, s, 2 * LANES), BF16),
                   jax.ShapeDtypeStruct((bsz, 2, 8, LANES), F32)],
        scratch_shapes=[pltpu.VMEM((s + 2 * CHUNK, LANES), BF16), pltpu.VMEM((s + 2 * CHUNK, LANES), BF16),
                        pltpu.VMEM((s + 2 * CHUNK, LANES), F32), pltpu.VMEM((s + 2 * CHUNK, LANES), F32)],
        operands=(slopes, sink, u3, u3, u3, d_o), rider=rider)


def _ffn_bwd(dz2, gs, us, pg, ple, zh1, r1, g1, wg4, wu4, wd4, wpg, w_out):
    t = dz2.shape[0]
    tm = 256

    def body(dz_ref, gs_ref, us_ref, pg_ref, ple_ref, zh_ref, r_ref, g1_ref,
             wg_hbm, wu_hbm, wd_hbm, wpg_hbm, wo_hbm,
             dgs_ref, dus_ref, dsp_ref, dple_ref, dz1_ref, dyr_ref, dya_ref, dg1_ref, db1_ref,
             wg, wu, wd, wpg, wo):
        step = pl.program_id(0)
        _load_resident(step, [(wg_hbm, wg), (wu_hbm, wu), (wd_hbm, wd), (wpg_hbm, wpg), (wo_hbm, wo)])

        @pl.when(step == 0)
        def _():
            dg1_ref[...] = jnp.zeros_like(dg1_ref)
            db1_ref[...] = jnp.zeros_like(db1_ref)

        dz = dz_ref[...]
        dzb = dz.astype(BF16)
        dh = ALPHA * dz
        pending = []
        for j in range(N_SHARD + 1):
            if j < N_SHARD:
                da = _dot_nt(dzb, wd[j])
                gj = gs_ref[j].astype(F32)
                uj = us_ref[j].astype(F32)
                sg = _sigmoid(gj)
                dgj = (da * uj * sg * (1.0 + gj * (1.0 - sg))).astype(BF16)
                duj = (da * gj * sg).astype(BF16)
                dgs_ref[j] = dgj
                dus_ref[j] = duj
                pending.append((dgj, duj))
            if j > 0:
                dgp, dup = pending[j - 1]
                dh = dh + _dot(dgp, wg[j - 1]) + _dot(dup, wu[j - 1])
        pgv = pg_ref[...].astype(F32)
        plev = ple_ref[...].astype(F32)
        dple_ref[...] = (dz * pgv).astype(BF16)
        dsp = (dz * plev * pgv * (1.0 - pgv)).astype(BF16)
        dsp_ref[...] = dsp
        dh = dh + _dot_nt(dsp, wpg[...])
        zh = zh_ref[...]
        dg1_ref[...] += jnp.sum(dh * zh, axis=0, keepdims=True)
        db1_ref[...] += jnp.sum(dh, axis=0, keepdims=True)
        dzh = dh * g1_ref[...]
        m1 = jnp.mean(dzh, axis=1, keepdims=True)
        m2 = jnp.mean(dzh * zh, axis=1, keepdims=True)
        dz1 = r_ref[...] * (dzh - m1 - zh * m2)
        dz1_ref[...] = dz1
        dyc = _dot_nt(dz1.astype(BF16), wo[...])
        dyr_ref[...] = dyc[:, 0:RET_WIDTH].astype(BF16)
        dya_ref[...] = dyc[:, RET_WIDTH:].astype(BF16)

    row = lambda w: pl.BlockSpec((tm, w), lambda i: (i, 0))
    const = lambda s: pl.BlockSpec(s, lambda i: (0, 0))
    sh = pl.BlockSpec((N_SHARD, tm, FFN_SHARD), lambda i: (0, i, 0))
    hbm = pl.BlockSpec(memory_space=pl.ANY)
    sh_shape = jax.ShapeDtypeStruct((N_SHARD, t, FFN_SHARD), BF16)
    return pl.pallas_call(
        body, name="ffn_bwd", grid=(t // tm,),
        in_specs=[row(D_MODEL), sh, sh, row(D_MODEL), row(D_MODEL), row(D_MODEL), row(1), const((1, D_MODEL)),
                  hbm, hbm, hbm, hbm, hbm],
        out_specs=[sh, sh, row(D_MODEL), row(D_MODEL), row(D_MODEL), row(RET_WIDTH), row(ATTN_WIDTH),
                   const((1, D_MODEL)), const((1, D_MODEL))],
        out_shape=[sh_shape, sh_shape, jax.ShapeDtypeStruct((t, D_MODEL), BF16),
                   jax.ShapeDtypeStruct((t, D_MODEL), BF16), jax.ShapeDtypeStruct((t, D_MODEL), F32),
                   jax.ShapeDtypeStruct((t, RET_WIDTH), BF16), jax.ShapeDtypeStruct((t, ATTN_WIDTH), BF16),
                   jax.ShapeDtypeStruct((1, D_MODEL), F32), jax.ShapeDtypeStruct((1, D_MODEL), F32)],
        scratch_shapes=[pltpu.VMEM(wg4.shape, BF16), pltpu.VMEM(wu4.shape, BF16), pltpu.VMEM(wd4.shape, BF16),
                        pltpu.VMEM(wpg.shape, BF16), pltpu.VMEM(w_out.shape, BF16)],
        compiler_params=_params("arbitrary", vmem=VMEM_LIMIT),
    )(dz2, gs, us, pg, ple, zh1, r1, g1, wg4, wu4, wd4, wpg, w_out)


def _wgrad_misc(y_ret, y_att, dz1, hb, dsp, p2d, dple, rider=None):
    t = dz1.shape[0]
    tk = min(t, 512)

    def body(yr_ref, ya_ref, dz_ref, hb_ref, dsp_ref, p_ref, dple_ref, wo_ref, wpg_ref, wpe_ref):
        @pl.when(pl.program_id(0) == 0)
        def _():
            wo_ref[...] = jnp.zeros_like(wo_ref)
            wpg_ref[...] = jnp.zeros_like(wpg_ref)
            wpe_ref[...] = jnp.zeros_like(wpe_ref)

        dzb = dz_ref[...].astype(BF16)
        wo_ref[0:RET_WIDTH, :] += _dot_tn(yr_ref[...], dzb)
        wo_ref[RET_WIDTH:, :] += _dot_tn(ya_ref[...], dzb)
        wpg_ref[...] += _dot_tn(hb_ref[...], dsp_ref[...])
        wpe_ref[...] += _dot_tn(p_ref[...].astype(BF16), dple_ref[...])

    row = lambda w: pl.BlockSpec((tk, w), lambda k: (k, 0))
    const = lambda s: pl.BlockSpec(s, lambda k: (0, 0))
    return _hosted_call(
        body, "wgrad_misc", (t // tk,),
        in_specs=[row(RET_WIDTH), row(ATTN_WIDTH), row(D_MODEL), row(D_MODEL), row(D_MODEL), row(PLE_DIM),
                  row(D_MODEL)],
        out_specs=[const((D_MODEL, D_MODEL)), const((D_MODEL, D_MODEL)), const((PLE_DIM, D_MODEL))],
        out_shape=[jax.ShapeDtypeStruct((D_MODEL, D_MODEL), F32), jax.ShapeDtypeStruct((D_MODEL, D_MODEL), F32),
                   jax.ShapeDtypeStruct((PLE_DIM, D_MODEL), F32)],
        scratch_shapes=[], operands=(y_ret, y_att, dz1, hb, dsp, p2d, dple), rider=rider, semantics=["arbitrary"])


def _wgrad_ffn(acts, dgs, dus, hb, dz2b):
    t = dz2b.shape[0]
    tk = min(t, 512)

    def body(act_ref, dg_ref, du_ref, hb_ref, dz_ref, og_ref, ou_ref, od_ref):
        @pl.when(pl.program_id(1) == 0)
        def _():
            og_ref[...] = jnp.zeros_like(og_ref)
            ou_ref[...] = jnp.zeros_like(ou_ref)
            od_ref[...] = jnp.zeros_like(od_ref)

        hbv = hb_ref[...]
        og_ref[...] += _dot_tn(dg_ref[...], hbv)
        ou_ref[...] += _dot_tn(du_ref[...], hbv)
        od_ref[...] += _dot_tn(act_ref[...], dz_ref[...])

    a_spec = pl.BlockSpec((None, tk, FFN_SHARD), lambda j, k: (j, k, 0))
    b_spec = pl.BlockSpec((tk, D_MODEL), lambda j, k: (k, 0))
    o_spec = pl.BlockSpec((None, FFN_SHARD, D_MODEL), lambda j, k: (j, 0, 0))
    o_shape = jax.ShapeDtypeStruct((N_SHARD, FFN_SHARD, D_MODEL), F32)
    return pl.pallas_call(
        body, name="wgrad_ffn", grid=(N_SHARD, t // tk),
        in_specs=[a_spec, a_spec, a_spec, b_spec, b_spec],
        out_specs=[o_spec, o_spec, o_spec], out_shape=[o_shape, o_shape, o_shape],
        compiler_params=_params("parallel", "arbitrary", vmem=VMEM_LIMIT),
    )(acts, dgs, dus, hb, dz2b)


KV_ORDER = (0, 128, 64, 192)


def _wgrad_in(pieces, x2d):
    t = x2d.shape[0]
    tk = min(t, 512)
    kv0 = CB_AK * LANES

    def body(p0, p1, p2, p3, p4, pkv, x_ref, o_ref):
        @pl.when(pl.program_id(0) == 0)
        def _():
            o_ref[...] = jnp.zeros_like(o_ref)

        xb = x_ref[...].astype(BF16)
        for i, ref in enumerate((p0, p1, p2, p3, p4)):
            o_ref[i * 512:(i + 1) * 512, :] += _dot_tn(ref[...], xb)
        dkv = _dot_tn(pkv[...], xb)
        for i, o in enumerate(KV_ORDER):
            o_ref[kv0 + o:kv0 + o + HEAD_DIM, :] += dkv[i * HEAD_DIM:(i + 1) * HEAD_DIM]

    row = lambda w: pl.BlockSpec((tk, w), lambda k: (k, 0))
    return pl.pallas_call(
        body, name="wgrad_in", grid=(t // tk,),
        in_specs=[row(512)] * 5 + [row(256), row(D_MODEL)],
        out_specs=pl.BlockSpec((IN_WIDTH, D_MODEL), lambda k: (0, 0)),
        out_shape=jax.ShapeDtypeStruct((IN_WIDTH, D_MODEL), F32),
        compiler_params=_params("arbitrary", vmem=VMEM_LIMIT),
    )(*pieces, x2d)


def _inproj_bwd(dz1, pieces, w_main, w_kv, rider=None):
    t = dz1.shape[0]
    tm = 512

    def body(dz_ref, p0, p1, p2, p3, p4, pkv, wm_ref, wkv_ref, o_ref):
        acc = ALPHA * dz_ref[...]
        for i, ref in enumerate((p0, p1, p2, p3, p4)):
            acc = acc + _dot(ref[...], wm_ref[i * 512:(i + 1) * 512, :])
        o_ref[...] = acc + _dot(pkv[...], wkv_ref[...])

    row = lambda w: pl.BlockSpec((tm, w), lambda i: (i, 0))
    const = lambda s: pl.BlockSpec(s, lambda i: (0, 0))
    return _hosted_call(
        body, "inproj_bwd", (t // tm,),
        in_specs=[row(D_MODEL)] + [row(512)] * 5 + [row(256), const(w_main.shape), const(w_kv.shape)],
        out_specs=[row(D_MODEL)],
        out_shape=[jax.ShapeDtypeStruct((t, D_MODEL), F32)],
        scratch_shapes=[], operands=(dz1, *pieces, w_main, w_kv), rider=rider)


def _coords():
    return lax.axis_index("x"), lax.axis_index("y"), lax.axis_index("c")


def _chip_of(x, y, rel):
    return (1 - x if rel & 2 else x), (1 - y if rel & 1 else y)


def _all_gather_weights(shards):
    first = _gather_chips_rider(shards)
    second = _gather_pass_rider([jax.ShapeDtypeStruct((N_SHARD,) + s.shape, s.dtype) for s in shards], chained=True)
    return _run_riders("gather_weights", shards, first.out_shapes, [first, second])


def _run_riders(name, ins, out_shapes, riders):
    n_in, n_out = len(ins), len(out_shapes)

    def body(*refs):
        in_refs, out_refs = refs[:n_in], refs[n_in:n_in + n_out]
        k = n_in + n_out
        for r in riders:
            sems = refs[k:k + len(r.sems)]
            k += len(r.sems)
            r.start(in_refs, out_refs, sems)
            r.finish(in_refs, out_refs, sems)

    hbm = pl.BlockSpec(memory_space=pl.ANY)
    return pl.pallas_call(
        body, name=name, in_specs=[hbm] * n_in, out_specs=[hbm] * n_out, out_shape=list(out_shapes),
        scratch_shapes=[s for r in riders for s in r.sems],
    )(*ins)


def _gather_half(outs, w, chip, cc):
    h = outs[w].shape[1] // 2
    return outs[w].at[chip, pl.ds(cc * h, h), :]


def _gather_chips_rider(shards):
    nw = len(shards)

    def copies(ins, outs, sems):
        send, recv, lsend, lrecv = sems
        x, y, c = _coords()
        me = 2 * x + y
        own = [pltpu.make_async_remote_copy(
            src_ref=ins[w], dst_ref=outs[w].at[me], send_sem=lsend.at[w], recv_sem=lrecv.at[w],
            device_id=(x, y, 1 - c), device_id_type=MESH) for w in range(nw)]
        out, arrive = [], []
        for rel in (1, 2, 3):
            kx, ky = _chip_of(x, y, rel)
            for w in range(nw):
                h = shards[w].shape[0] // 2
                sem = dict(send_sem=send.at[w * 3 + rel - 1], recv_sem=recv.at[w * 3 + rel - 1],
                           device_id=(kx, ky, c), device_id_type=MESH)
                out.append(pltpu.make_async_remote_copy(
                    src_ref=ins[w].at[pl.ds(c * h, h), :], dst_ref=_gather_half(outs, w, me, c), **sem))
                theirs = _gather_half(outs, w, 2 * kx + ky, c)
                arrive.append(pltpu.make_async_remote_copy(src_ref=theirs, dst_ref=theirs, **sem))
        return own, out, arrive

    def start(ins, outs, sems):
        own, out, _ = copies(ins, outs, sems)
        for cp in own + out:
            cp.start()

    def finish(ins, outs, sems):
        own, out, arrive = copies(ins, outs, sems)
        for cp in arrive:
            cp.wait_recv()
        for cp in out:
            cp.wait_send()
        for cp in own:
            cp.wait()

    dma = pltpu.SemaphoreType.DMA
    return _Rider(shards, [jax.ShapeDtypeStruct((N_SHARD,) + s.shape, s.dtype) for s in shards],
                  [dma((3 * nw,)), dma((3 * nw,)), dma((nw,)), dma((nw,))], start, finish)


def _gather_pass_rider(gathered, chained=False):
    nw = len(gathered)

    def copies(outs, sems, cc):
        send, recv = sems
        x, y, c = _coords()
        res = []
        for rel in (1, 2, 3):
            kx, ky = _chip_of(x, y, rel)
            for w in range(nw):
                rows = _gather_half(outs, w, 2 * kx + ky, cc)
                res.append(pltpu.make_async_remote_copy(
                    src_ref=rows, dst_ref=rows, send_sem=send.at[w * 3 + rel - 1], recv_sem=recv.at[w * 3 + rel - 1],
                    device_id=(x, y, 1 - c), device_id_type=MESH))
        return res

    def start(ins, outs, sems):
        for cp in copies(outs, sems, lax.axis_index("c")):
            cp.start()

    def finish(ins, outs, sems):
        c = lax.axis_index("c")
        for cp in copies(outs, sems, 1 - c):
            cp.wait_recv()
        for cp in copies(outs, sems, c):
            cp.wait_send()

    dma = pltpu.SemaphoreType.DMA
    shapes = [jax.ShapeDtypeStruct(g.shape, g.dtype) for g in gathered]
    if chained:
        return _Rider([], [], [dma((3 * nw,)), dma((3 * nw,))], start, finish)
    return _Rider(gathered, shapes, [dma((3 * nw,)), dma((3 * nw,))], start, finish,
                  aliases={w: w for w in range(nw)})


def _exchange_halves_rider(parts):
    nw = len(parts)

    def copies(ins, outs, sems):
        send, recv = sems
        x, y, c = _coords()
        res = []
        for w in range(nw):
            h = parts[w].shape[1] // 2
            res.append(pltpu.make_async_remote_copy(
                src_ref=ins[w].at[:, pl.ds((1 - c) * h, h), :], dst_ref=outs[w],
                send_sem=send.at[w], recv_sem=recv.at[w], device_id=(x, y, 1 - c), device_id_type=MESH))
        return res

    def start(ins, outs, sems):
        for cp in copies(ins, outs, sems):
            cp.start()

    def finish(ins, outs, sems):
        for cp in copies(ins, outs, sems):
            cp.wait()

    dma = pltpu.SemaphoreType.DMA
    return _Rider(parts, [jax.ShapeDtypeStruct((N_SHARD, p.shape[1] // 2, p.shape[2]), F32) for p in parts],
                  [dma((nw,)), dma((nw,))], start, finish)


def _add_halves(parts, theirs, pos):
    nw = len(parts)
    split = 2

    def body(pos_ref, *refs):
        ins, oth = refs[:nw], refs[nw:2 * nw]
        o32, o16 = refs[2 * nw:3 * nw], refs[3 * nw:]
        sums = [ins[w][...] + oth[w][...] for w in range(nw)]
        for w in range(nw):
            o16[w][...] = sums[w].astype(BF16)

        @pl.when(pl.program_id(1) == pos_ref[0])
        def _():
            for w in range(nw):
                o32[w][...] = sums[w]

    in_specs, oth_specs, o32_specs, shapes32, shapes16 = [], [], [], [], []
    for p in parts:
        hb = p.shape[1] // 2 // split
        blk = (None, hb, p.shape[2])
        in_specs.append(pl.BlockSpec(blk, lambda i, j, pos_ref: (j, pos_ref[1] * split + i, 0)))
        oth_specs.append(pl.BlockSpec(blk, lambda i, j, pos_ref: (j, i, 0)))
        o32_specs.append(pl.BlockSpec((hb, p.shape[2]), lambda i, j, pos_ref: (i, 0)))
        shapes32.append(jax.ShapeDtypeStruct((p.shape[1] // 2, p.shape[2]), F32))
        shapes16.append(jax.ShapeDtypeStruct((N_SHARD, p.shape[1] // 2, p.shape[2]), BF16))
    return pl.pallas_call(
        body, name="add_halves",
        grid_spec=pltpu.PrefetchScalarGridSpec(
            num_scalar_prefetch=1, grid=(split, N_SHARD),
            in_specs=in_specs + oth_specs, out_specs=o32_specs + oth_specs),
        out_shape=shapes32 + shapes16,
        compiler_params=_params("parallel", "arbitrary", vmem=VMEM_LIMIT),
    )(pos, *parts, *theirs)


def _exchange_chips_rider(sums16):
    nw = len(sums16)

    def copies(ins, outs, sems):
        send, recv = sems
        x, y, c = _coords()
        res = []
        for rel in (1, 2, 3):
            kx, ky = _chip_of(x, y, rel)
            for w in range(nw):
                res.append(pltpu.make_async_remote_copy(
                    src_ref=ins[w].at[2 * kx + ky], dst_ref=outs[w].at[rel - 1],
                    send_sem=send.at[w * 3 + rel - 1], recv_sem=recv.at[w * 3 + rel - 1],
                    device_id=(kx, ky, c), device_id_type=MESH))
        return res

    def start(ins, outs, sems):
        for cp in copies(ins, outs, sems):
            cp.start()

    def finish(ins, outs, sems):
        for cp in copies(ins, outs, sems):
            cp.wait()

    dma = pltpu.SemaphoreType.DMA
    return _Rider(sums16, [jax.ShapeDtypeStruct((3,) + s.shape[1:], BF16) for s in sums16],
                  [dma((3 * nw,)), dma((3 * nw,))], start, finish)


def _add_chips(sums32, theirs, pos):
    nw = len(sums32)
    split = 2

    def body(pos_ref, *refs):
        ins, oth, outs = refs[:nw], refs[nw:2 * nw], refs[2 * nw:]
        for w in range(nw):
            acc = ins[w][...]
            for r in range(3):
                acc = acc + oth[w][r].astype(F32)
            outs[w][...] = acc

    in_specs, oth_specs, out_specs, shapes = [], [], [], []
    for s in sums32:
        hb = s.shape[0] // split
        in_specs.append(pl.BlockSpec((hb, s.shape[1]), lambda i, pos_ref: (i, 0)))
        oth_specs.append(pl.BlockSpec((3, hb, s.shape[1]), lambda i, pos_ref: (0, i, 0)))
        out_specs.append(pl.BlockSpec((hb, s.shape[1]), lambda i, pos_ref: (pos_ref[1] * split + i, 0)))
        shapes.append(jax.ShapeDtypeStruct((2 * s.shape[0], s.shape[1]), F32))
    return pl.pallas_call(
        body, name="add_chips",
        grid_spec=pltpu.PrefetchScalarGridSpec(
            num_scalar_prefetch=1, grid=(split,), in_specs=in_specs + oth_specs, out_specs=out_specs),
        out_shape=shapes,
        compiler_params=_params("parallel", vmem=VMEM_LIMIT),
    )(pos, *sums32, *theirs)


def _join_halves(shards):
    nw = len(shards)

    def body(*refs):
        outs = refs[nw:2 * nw]
        send, recv = refs[2 * nw:]
        x, y, c = _coords()

        def copy(w, cc):
            h = shards[w].shape[0] // 2
            rows = outs[w].at[pl.ds(cc * h, h), :]
            return pltpu.make_async_remote_copy(
                src_ref=rows, dst_ref=rows, send_sem=send.at[w], recv_sem=recv.at[w],
                device_id=(x, y, 1 - c), device_id_type=MESH)

        for w in range(nw):
            copy(w, c).start()
        for w in range(nw):
            copy(w, 1 - c).wait_recv()
            copy(w, c).wait_send()

    hbm = pl.BlockSpec(memory_space=pl.ANY)
    return pl.pallas_call(
        body, name="join_halves",
        in_specs=[hbm] * nw, out_specs=[hbm] * nw,
        out_shape=[jax.ShapeDtypeStruct(s.shape, F32) for s in shards],
        input_output_aliases={w: w for w in range(nw)},
        scratch_shapes=[pltpu.SemaphoreType.DMA((nw,)), pltpu.SemaphoreType.DMA((nw,))],
    )(*shards)


def _adamw_math(w, g, m, v):
    m = ADAM_B1 * m + (1.0 - ADAM_B1) * g
    v = ADAM_B2 * v + (1.0 - ADAM_B2) * (g * g)
    m_hat = m / (1.0 - ADAM_B1 ** ADAM_STEP)
    v_hat = v / (1.0 - ADAM_B2 ** ADAM_STEP)
    delta = -ADAM_LR * (m_hat / (jnp.sqrt(v_hat) + ADAM_EPS) + ADAM_WD * w)
    return delta, m, v


def _adamw(ws, gs, ms, vs):
    nw = len(ws)
    split = 8

    def body(*refs):
        w_r, g_r, m_r, v_r = (refs[i * nw:(i + 1) * nw] for i in range(4))
        d_o, m_o, v_o = (refs[(4 + i) * nw:(5 + i) * nw] for i in range(3))
        for k in range(nw):
            d, m, v = _adamw_math(w_r[k][...], g_r[k][...], m_r[k][...], v_r[k][...])
            d_o[k][...] = d
            m_o[k][...] = m
            v_o[k][...] = v

    specs = [pl.BlockSpec((w.shape[0] // split, w.shape[1]), lambda i: (i, 0)) for w in ws]
    shapes = [jax.ShapeDtypeStruct(w.shape, F32) for w in ws]
    outs = pl.pallas_call(
        body, name="adamw", grid=(split,),
        in_specs=specs * 4, out_specs=specs * 3, out_shape=shapes * 3,
        compiler_params=_params("parallel", vmem=VMEM_LIMIT),
    )(*ws, *gs, *ms, *vs)
    return outs[:nw], outs[nw:2 * nw], outs[2 * nw:]


SMALL_ROWS = 8
SMALL_COLS = D_MODEL
LOSS_COL = RET_WIDTH + 24


def _small_allreduce_adamw(part, w, m, v, rider=None):
    def body(part_ref, w_ref, m_ref, v_ref, g_out, d_out, m_out, v_out, all_ref, send, recv):
        x, y, c = _coords()
        me = 4 * x + 2 * y + c
        all_ref[me] = part_ref[...]
        copies = []
        for rel in range(1, 8):
            px = 1 - x if rel & 4 else x
            py = 1 - y if rel & 2 else y
            pc = 1 - c if rel & 1 else c
            copies.append(pltpu.make_async_remote_copy(
                src_ref=part_ref, dst_ref=all_ref.at[me],
                send_sem=send.at[rel - 1], recv_sem=recv.at[rel - 1], device_id=(px, py, pc), device_id_type=MESH))
        for cp in copies:
            cp.start()
        for cp in copies:
            cp.wait()
        g = all_ref[0]
        for k in range(1, 8):
            g = g + all_ref[k]
        d, mn, vn = _adamw_math(w_ref[...], g, m_ref[...], v_ref[...])
        g_out[...] = g
        d_out[...] = d
        m_out[...] = mn
        v_out[...] = vn

    vm = pl.BlockSpec(memory_space=pltpu.VMEM)
    shape = jax.ShapeDtypeStruct((SMALL_ROWS, SMALL_COLS), F32)
    return _hosted_call(
        body, "small_allreduce_adamw", (1,),
        in_specs=[vm] * 4, out_specs=[vm] * 4, out_shape=[shape] * 4,
        scratch_shapes=[pltpu.VMEM((8, SMALL_ROWS, SMALL_COLS), F32),
                        pltpu.SemaphoreType.DMA((7,)), pltpu.SemaphoreType.DMA((7,))],
        operands=(part, w, m, v), rider=rider, semantics=["arbitrary"])


SMALL_NAMES = ("ret_decay_fwd", "ret_decay_bwd", "attn_sink", "ret_gn_gain",
               "ln1_gain", "ln1_bias", "ln2_gain", "ln2_bias")


LN_NAMES = ("ln1_gain", "ln1_bias", "ln2_gain", "ln2_bias")


def _pack_small(vals, extra=None):
    tail = jnp.zeros((1, 1), F32) if extra is None else extra.reshape(1, 1)
    row4 = jnp.concatenate([vals["ret_gn_gain"], vals["ret_decay_fwd"], vals["ret_decay_bwd"], vals["attn_sink"],
                            tail, jnp.zeros((1, SMALL_COLS - LOSS_COL - 1), F32)], axis=1)
    rows = [vals[n] for n in LN_NAMES] + [row4, jnp.zeros((SMALL_ROWS - 5, SMALL_COLS), F32)]
    return jnp.concatenate(rows, axis=0)


def _unpack_small(packed):
    out = {n: packed[i:i + 1] for i, n in enumerate(LN_NAMES)}
    o = RET_WIDTH
    out.update(ret_gn_gain=packed[4:5, 0:o], ret_decay_fwd=packed[4:5, o:o + 8],
               ret_decay_bwd=packed[4:5, o + 8:o + 16], attn_sink=packed[4:5, o + 16:o + 24])
    return out


def _local_step(x, p, tgt, w_in_t, rest, small, pos=None, small_state=None):
    bsz, s, _ = x.shape
    t = bsz * s
    x2d = x.reshape(t, D_MODEL)
    p2d = p.reshape(t, PLE_DIM)
    tgt2d = tgt.reshape(t, D_MODEL)
    dec_f = small["ret_decay_fwd"].reshape(8)
    dec_b = small["ret_decay_bwd"].reshape(8)
    lg_f = jnp.log1p(-jnp.exp2(dec_f))
    lg_b = jnp.log1p(-jnp.exp2(dec_b))
    per_lane = lambda v: jnp.repeat(v, HEAD_DIM).reshape(4, 1, LANES)
    lgf_l, lgb_l = per_lane(lg_f), per_lane(lg_b)
    sink = small["attn_sink"].reshape(8)
    slopes = 2.0 ** (-(jnp.arange(8, dtype=F32) + 1.0))
    gn_gain = small["ret_gn_gain"]
    g1, b1, g2, b2 = (small[n] for n in ("ln1_gain", "ln1_bias", "ln2_gain", "ln2_bias"))

    dist = pos is not None
    chips = lambda names: _gather_chips_rider([rest[REST_NAMES.index(n)] for n in names])
    first, second, third = ("w_ffn_up",), ("w_out", "w_ffn_gate", "w_ple_proj", "w_ple_gate"), ("w_ffn_down",)
    u, *c1 = _inproj(x2d, w_in_t, rider=chips(first) if dist else None)
    u3 = u.reshape(bsz, s, IN_WIDTH)
    y_pre, y_ret, *o2 = _ret_fwd(u3, lgf_l, lgb_l, gn_gain,
                                 rider=_merge_riders([_gather_pass_rider(c1), chips(second)]) if dist else None)
    y_att, *o3 = _attn_fwd(u3, slopes, sink, rider=_merge_riders(
        [_gather_pass_rider(o2[len(first):]), chips(third)]) if dist else None)
    gathered = dict(zip(first, o2[:len(first)]))
    gathered.update(zip(second, o3[:len(second)]))
    w_out = _assemble_weights({"w_out": gathered["w_out"]})["w_out"] if dist else rest["w_out"]
    zh1, r1, hb, *o4 = _outproj_ln1(y_ret.reshape(t, RET_WIDTH), y_att.reshape(t, ATTN_WIDTH), x2d, w_out, g1, b1,
                                    rider=_gather_pass_rider(o3[len(second):]) if dist else None)
    gathered.update(zip(third, o4))
    wts = _assemble_weights(gathered) if dist else rest
    dz2, dz2b, gs, us, acts, pg, ple, sq, dg2, db2 = _ffn_fwd(
        zh1, hb, p2d, tgt2d, g1, b1, g2, b2, wts["gate4"], wts["up4"], wts["down4"], wts["ple_proj"], wts["ple_gate"])
    dgs, dus, dsp, dple, dz1, dyr, dya, dg1, db1 = _ffn_bwd(dz2, gs, us, pg, ple, zh1, r1, g1, wts["gate4"],
                                                          wts["up4"], wts["down4"], wts["ple_gate"], wts["w_out"])
    ffn_parts = list(_wgrad_ffn(acts, dgs, dus, hb, dz2b))
    d_w_out, d_ple_gate, d_ple_proj, *th_ffn = _wgrad_misc(
        y_ret.reshape(t, RET_WIDTH), y_att.reshape(t, ATTN_WIDTH), dz1, hb, dsp, p2d, dple,
        rider=_exchange_halves_rider(ffn_parts) if dist else None)
    misc_parts = [d_w_out.reshape(N_SHARD, D_MODEL // N_SHARD, D_MODEL),
                  d_ple_proj.reshape(PLE_DIM, N_SHARD, D_MODEL // N_SHARD).transpose(1, 0, 2),
                  d_ple_gate.reshape(N_SHARD, D_MODEL // N_SHARD, D_MODEL)]
    dyr3, dya3 = dyr.reshape(bsz, s, RET_WIDTH), dya.reshape(bsz, s, ATTN_WIDTH)
    if dist:
        s_ffn = _add_halves(ffn_parts, th_ffn, pos)
        drq, drk, drv, drg, rpart, *o5 = _ret_bwd(u3, y_pre, dyr3, lgf_l, lgb_l, gn_gain, rider=_merge_riders(
            [_exchange_chips_rider(s_ffn[3:5]), _exchange_halves_rider(misc_parts)]))
        s_misc = _add_halves(misc_parts, o5[2:], pos)
        daq, dakv, spart, *o6 = _attn_bwd(u3, dya3, slopes, sink,
                                          rider=_exchange_chips_rider([s_ffn[5]] + list(s_misc[3:])))
    else:
        drq, drk, drv, drg, rpart = _ret_bwd(u3, y_pre, dyr3, lgf_l, lgb_l, gn_gain)
        daq, dakv, spart = _attn_bwd(u3, dya3, slopes, sink)
    pieces = [a.reshape(t, -1) for a in (drq, drk, drv, drg, daq, dakv)]
    kv0 = CB_AK * LANES
    w_kv = jnp.concatenate([w_in_t[kv0 + o:kv0 + o + HEAD_DIM] for o in KV_ORDER], axis=0)
    d_in = _wgrad_in(pieces, x2d).reshape(N_SHARD, FFN_SHARD, D_MODEL)

    rsum = rpart
    lane_heads = lambda row: jnp.sum(row.reshape(4, 2, HEAD_DIM), axis=-1).reshape(8)
    dlg_f = lane_heads(rsum[:, 0, :]) + jnp.stack([jnp.sum(rsum[:, 2, :], -1), jnp.sum(rsum[:, 3, :], -1)], 1).reshape(8)
    dlg_b = lane_heads(rsum[:, 1, :]) + jnp.stack([jnp.sum(rsum[:, 4, :], -1), jnp.sum(rsum[:, 5, :], -1)], 1).reshape(8)
    chain = lambda d: -(math.log(2.0) * jnp.exp2(d)) / (1.0 - jnp.exp2(d))
    grads_small = {
        "ret_decay_fwd": (dlg_f * chain(dec_f)).reshape(1, 8),
        "ret_decay_bwd": (dlg_b * chain(dec_b)).reshape(1, 8),
        "attn_sink": jnp.sum(spart, axis=0)[:, 0:4, 0].reshape(1, 8),
        "ret_gn_gain": rsum[:, 6, :].reshape(1, RET_WIDTH),
        "ln1_gain": dg1, "ln1_bias": db1, "ln2_gain": dg2, "ln2_bias": db2,
    }
    if not dist:
        grad_x, = _inproj_bwd(dz1, pieces, w_in_t[:kv0], w_kv)
        grads_rest = [misc_parts[0]] + ffn_parts + misc_parts[1:]
        return sq[0, 0], grad_x.reshape(bsz, s, D_MODEL), d_in, grads_rest, grads_small
    *small_out, th_in = _small_allreduce_adamw(_pack_small(grads_small, sq[0, 0]), *small_state,
                                               rider=_exchange_halves_rider([d_in]))
    s_in = _add_halves([d_in], [th_in], pos)
    grad_x, chips_in = _inproj_bwd(dz1, pieces, w_in_t[:kv0], w_kv, rider=_exchange_chips_rider([s_in[1]]))
    sums32 = [s_in[0], s_misc[0], s_ffn[0], s_ffn[1], s_ffn[2], s_misc[1], s_misc[2]]
    from_chips = [chips_in, o6[1], o5[0], o5[1], o6[0], o6[2], o6[3]]
    return grad_x.reshape(bsz, s, D_MODEL), sums32, from_chips, small_out


BIG_NAMES = ("w_in", "w_out", "w_ffn_gate", "w_ffn_up", "w_ffn_down", "w_ple_proj", "w_ple_gate")
REST_NAMES = BIG_NAMES[1:]
TRANSPOSED = ("w_in", "w_ffn_gate", "w_ffn_up")
WEIGHT_ORDER = ("w_in", "ret_decay_fwd", "ret_decay_bwd", "ret_gn_gain", "attn_sink", "w_out", "ln1_gain",
                "ln1_bias", "w_ffn_gate", "w_ffn_up", "w_ffn_down", "w_ple_proj", "w_ple_gate", "ln2_gain", "ln2_bias")


def _shard_rows(name, a):
    return jnp.swapaxes(a[0], 0, 1) if name in TRANSPOSED else a[0]


def _unshard_rows(name, a):
    return (jnp.swapaxes(a, 0, 1) if name in TRANSPOSED else a)[None]


def _assemble_weights(gathered):
    cols = lambda a: a.transpose(1, 0, 2).reshape(a.shape[1], N_SHARD * a.shape[2])
    rows = lambda a: a.reshape(N_SHARD * a.shape[1], a.shape[2])
    same = lambda a: a
    layout = {"w_out": ("w_out", rows), "w_ffn_gate": ("gate4", same), "w_ffn_up": ("up4", same),
              "w_ffn_down": ("down4", same), "w_ple_proj": ("ple_proj", cols), "w_ple_gate": ("ple_gate", rows)}
    return {layout[n][0]: layout[n][1](a) for n, a in gathered.items()}


def kernel(x, p, w_in, ret_decay_fwd, ret_decay_bwd, ret_gn_gain, attn_sink, w_out, ln1_gain, ln1_bias, w_ffn_gate, w_ffn_up, w_ffn_down, w_ple_proj, w_ple_gate, ln2_gain, ln2_bias, loss_target, m_w_in, m_ret_decay_fwd, m_ret_decay_bwd, m_ret_gn_gain, m_attn_sink, m_w_out, m_ln1_gain, m_ln1_bias, m_w_ffn_gate, m_w_ffn_up, m_w_ffn_down, m_w_ple_proj, m_w_ple_gate, m_ln2_gain, m_ln2_bias, v_w_in, v_ret_decay_fwd, v_ret_decay_bwd, v_ret_gn_gain, v_attn_sink, v_w_out, v_ln1_gain, v_ln1_bias, v_w_ffn_gate, v_w_ffn_up, v_w_ffn_down, v_w_ple_proj, v_w_ple_gate, v_ln2_gain, v_ln2_bias):
    w = dict(w_in=w_in, ret_decay_fwd=ret_decay_fwd, ret_decay_bwd=ret_decay_bwd, ret_gn_gain=ret_gn_gain,
             attn_sink=attn_sink, w_out=w_out, ln1_gain=ln1_gain, ln1_bias=ln1_bias, w_ffn_gate=w_ffn_gate,
             w_ffn_up=w_ffn_up, w_ffn_down=w_ffn_down, w_ple_proj=w_ple_proj, w_ple_gate=w_ple_gate,
             ln2_gain=ln2_gain, ln2_bias=ln2_bias)
    m = dict(w_in=m_w_in, ret_decay_fwd=m_ret_decay_fwd, ret_decay_bwd=m_ret_decay_bwd, ret_gn_gain=m_ret_gn_gain,
             attn_sink=m_attn_sink, w_out=m_w_out, ln1_gain=m_ln1_gain, ln1_bias=m_ln1_bias, w_ffn_gate=m_w_ffn_gate,
             w_ffn_up=m_w_ffn_up, w_ffn_down=m_w_ffn_down, w_ple_proj=m_w_ple_proj, w_ple_gate=m_w_ple_gate,
             ln2_gain=m_ln2_gain, ln2_bias=m_ln2_bias)
    v = dict(w_in=v_w_in, ret_decay_fwd=v_ret_decay_fwd, ret_decay_bwd=v_ret_decay_bwd, ret_gn_gain=v_ret_gn_gain,
             attn_sink=v_attn_sink, w_out=v_w_out, ln1_gain=v_ln1_gain, ln1_bias=v_ln1_bias, w_ffn_gate=v_w_ffn_gate,
             w_ffn_up=v_w_ffn_up, w_ffn_down=v_w_ffn_down, w_ple_proj=v_w_ple_proj, w_ple_gate=v_w_ple_gate,
             ln2_gain=v_ln2_gain, ln2_bias=v_ln2_bias)
    big = lambda d: [_shard_rows(n, d[n]) for n in BIG_NAMES]
    small = lambda d: {n: d[n] for n in SMALL_NAMES}

    chip = 2 * lax.axis_index("x") + lax.axis_index("y")
    pos = jnp.stack([chip, lax.axis_index("c")]).astype(jnp.int32)

    shards = [a.astype(BF16) for a in big(w)]
    (w_in4,) = _all_gather_weights(shards[:1])
    w_in_t = w_in4.reshape(IN_WIDTH, D_MODEL)
    grad_x, sums32, from_chips, (g_s, d_s, m_s, v_s) = _local_step(
        x, p[0], loss_target, w_in_t, shards[1:], small(w), pos=pos,
        small_state=(_pack_small(small(w)), _pack_small(small(m)), _pack_small(small(v))))
    g_big = _join_halves(_add_chips(sums32, from_chips, pos))
    d_big, m_big, v_big = _adamw(big(w), g_big, big(m), big(v))
    loss = g_s[4, LOSS_COL] * (0.5 / D_MODEL)

    def tree(bigs, packed):
        out = {n: _unshard_rows(n, a) for n, a in zip(BIG_NAMES, bigs)}
        out.update(_unpack_small(packed))
        return [out[n] for n in WEIGHT_ORDER]

    return (loss, grad_x, *tree(g_big, g_s), *tree(d_big, d_s), *tree(m_big, m_s), *tree(v_big, v_s))
```

```python
import functools
import math

import jax
import jax.numpy as jnp
from jax import lax
from jax.experimental import pallas as pl
from jax.experimental.pallas import tpu as pltpu

F32 = jnp.float32
BF16 = jnp.bfloat16

D_MODEL = 1024
HEAD_DIM = 64
RET_HEADS = 8
ATTN_HEADS = 8
RET_WIDTH = 512
ATTN_WIDTH = 512
KV_WIDTH = 128
IN_WIDTH = 2816
FFN = 2816
N_SHARD = 4
FFN_SHARD = FFN // N_SHARD
PLE_DIM = 256
CHUNK = 128
LANES = 128
ALPHA = 2.0 ** 0.25
LN_EPS = 1e-5
GN_EPS = 1e-5
NEG_INF = -1e30
ADAM_LR = 0.001
ADAM_B1 = 0.9
ADAM_B2 = 0.999
ADAM_EPS = 1e-08
ADAM_WD = 0.01
ADAM_STEP = 10
VMEM_LIMIT = 56 * 1024 * 1024
MESH = pl.DeviceIdType.MESH

CB_RQ, CB_RK, CB_RV, CB_RG, CB_AQ, CB_AK, CB_AV = 0, 4, 8, 12, 16, 20, 21


def _dot(a, b):
    return jnp.dot(a, b, preferred_element_type=F32)


def _dot_nt(a, b):
    return lax.dot_general(a, b, (((1,), (1,)), ((), ())), preferred_element_type=F32)


def _dot_tn(a, b):
    return lax.dot_general(a, b, (((0,), (0,)), ((), ())), preferred_element_type=F32)


def _sigmoid(x):
    return 1.0 / (1.0 + jnp.exp(-x))


def _params(*sem, vmem=None):
    return pltpu.CompilerParams(dimension_semantics=tuple(sem) if sem else None, vmem_limit_bytes=vmem)


class _Rider:
    def __init__(self, ins, out_shapes, sems, start, finish, aliases=None):
        self.ins, self.out_shapes, self.sems = list(ins), list(out_shapes), list(sems)
        self.start, self.finish, self.aliases = start, finish, dict(aliases or {})


def _merge_riders(riders):
    riders = [r for r in riders if r is not None]
    if len(riders) == 1:
        return riders[0]
    bounds, aliases = [], {}
    i0 = o0 = s0 = 0
    for r in riders:
        bounds.append((i0, o0, s0))
        aliases.update({i0 + i: o0 + o for i, o in r.aliases.items()})
        i0, o0, s0 = i0 + len(r.ins), o0 + len(r.out_shapes), s0 + len(r.sems)

    def each(method):
        def run(ins, outs, sems):
            for r, (i, o, s) in zip(riders, bounds):
                getattr(r, method)(ins[i:i + len(r.ins)], outs[o:o + len(r.out_shapes)], sems[s:s + len(r.sems)])
        return run

    return _Rider([a for r in riders for a in r.ins], [a for r in riders for a in r.out_shapes],
                  [a for r in riders for a in r.sems], each("start"), each("finish"), aliases)


def _hosted_call(body, name, grid, in_specs, out_specs, out_shape, scratch_shapes, operands, rider=None,
                 semantics=None):
    n_in, n_out, n_scr = len(in_specs), len(out_specs), len(scratch_shapes)
    if rider is None:
        return pl.pallas_call(
            body, name=name, grid=grid, in_specs=in_specs, out_specs=out_specs, out_shape=out_shape,
            scratch_shapes=scratch_shapes,
            compiler_params=_params(*(semantics or ["parallel"] * len(grid)), vmem=VMEM_LIMIT))(*operands)
    r_in, r_out = len(rider.ins), len(rider.out_shapes)

    def full_body(*refs):
        main_in, rin = refs[:n_in], refs[n_in:n_in + r_in]
        o0 = n_in + r_in
        main_out, rout = refs[o0:o0 + n_out], refs[o0 + n_out:o0 + n_out + r_out]
        s0 = o0 + n_out + r_out
        main_scr, rsem = refs[s0:s0 + n_scr], refs[s0 + n_scr:]
        first = functools.reduce(jnp.logical_and, [pl.program_id(a) == 0 for a in range(len(grid))])
        last = functools.reduce(jnp.logical_and, [pl.program_id(a) == g - 1 for a, g in enumerate(grid)])

        @pl.when(first)
        def _():
            rider.start(rin, rout, rsem)

        body(*main_in, *main_out, *main_scr)

        @pl.when(last)
        def _():
            rider.finish(rin, rout, rsem)

    hbm = pl.BlockSpec(memory_space=pl.ANY)
    return pl.pallas_call(
        full_body, name=name, grid=grid,
        in_specs=list(in_specs) + [hbm] * r_in, out_specs=list(out_specs) + [hbm] * r_out,
        out_shape=list(out_shape) + rider.out_shapes,
        scratch_shapes=list(scratch_shapes) + rider.sems,
        input_output_aliases={n_in + i: n_out + o for i, o in rider.aliases.items()},
        compiler_params=_params(*(["arbitrary"] * len(grid)), vmem=VMEM_LIMIT),
    )(*operands, *rider.ins)


def _head_mean(x, m0):
    s0 = jnp.sum(jnp.where(m0, x, 0.0), axis=1, keepdims=True)
    s1 = jnp.sum(jnp.where(m0, 0.0, x), axis=1, keepdims=True)
    return jnp.where(m0, s0, s1) * (1.0 / HEAD_DIM)


def _inproj(x2d, w_in_t, rider=None):
    t = x2d.shape[0]
    tm = 512
    nb = 256

    def body(x_ref, w_ref, o_ref):
        xb = x_ref[...].astype(BF16)
        for n in range(0, IN_WIDTH, nb):
            o_ref[:, n:n + nb] = _dot_nt(xb, w_ref[n:n + nb, :]).astype(BF16)

    return _hosted_call(
        body, "inproj", (t // tm,),
        in_specs=[pl.BlockSpec((tm, D_MODEL), lambda i: (i, 0)),
                  pl.BlockSpec((IN_WIDTH, D_MODEL), lambda i: (0, 0))],
        out_specs=[pl.BlockSpec((tm, IN_WIDTH), lambda i: (i, 0))],
        out_shape=[jax.ShapeDtypeStruct((t, IN_WIDTH), BF16)],
        scratch_shapes=[], operands=(x2d, w_in_t), rider=rider)


def _outproj_ln1(y_ret, y_att, x2d, w_out, gain, bias, rider=None):
    t = x2d.shape[0]
    tm = 512

    def body(yr_ref, ya_ref, x_ref, w_ref, g_ref, b_ref, zh_ref, r_ref, hb_ref):
        mix = _dot(yr_ref[...], w_ref[0:RET_WIDTH, :]) + _dot(ya_ref[...], w_ref[RET_WIDTH:, :])
        z = ALPHA * x_ref[...] + mix
        mu = jnp.mean(z, axis=1, keepdims=True)
        zc = z - mu
        var = jnp.mean(zc * zc, axis=1, keepdims=True)
        r = lax.rsqrt(var + LN_EPS)
        zh = zc * r
        zh_ref[...] = zh
        r_ref[...] = r
        hb_ref[...] = (zh * g_ref[...] + b_ref[...]).astype(BF16)

    row = lambda w: pl.BlockSpec((tm, w), lambda i: (i, 0))
    const = lambda s: pl.BlockSpec(s, lambda i: (0, 0))
    return _hosted_call(
        body, "outproj_ln1", (t // tm,),
        in_specs=[row(RET_WIDTH), row(ATTN_WIDTH), row(D_MODEL), const((D_MODEL, D_MODEL)),
                  const((1, D_MODEL)), const((1, D_MODEL))],
        out_specs=[row(D_MODEL), row(1), row(D_MODEL)],
        out_shape=[jax.ShapeDtypeStruct((t, D_MODEL), F32), jax.ShapeDtypeStruct((t, 1), F32),
                   jax.ShapeDtypeStruct((t, D_MODEL), BF16)],
        scratch_shapes=[], operands=(y_ret, y_att, x2d, w_out, gain, bias), rider=rider)


def _load_resident(step, pairs, sems):
    copies = [pltpu.make_async_copy(src, dst, sems.at[i]) for i, (src, dst) in enumerate(pairs)]

    @pl.when(step == 0)
    def _():
        for cp in copies:
            cp.start()
        for cp in copies:
            cp.wait()


def _ffn_fwd(zh1, hb, p2d, tgt, g1, b1, g2, b2, wg4, wu4, wd4, wpe, wpg):
    t = zh1.shape[0]
    tm = 256

    def body(zh_ref, hb_ref, p_ref, t_ref, g1_ref, b1_ref, g2_ref, b2_ref,
             wg_hbm, wu_hbm, wd_hbm, wpe_hbm, wpg_hbm,
             dz_ref, dzb_ref, gs_ref, us_ref, act_ref, pg_ref, ple_ref, loss_ref, dg2_ref, db2_ref,
             wg, wu, wd, wpe, wpg, wsem):
        step = pl.program_id(0)
        loads = [(h.at[j], v.at[j]) for j in range(N_SHARD) for h, v in ((wg_hbm, wg), (wu_hbm, wu), (wd_hbm, wd))]
        _load_resident(step, loads + [(wpe_hbm, wpe), (wpg_hbm, wpg)], wsem)

        @pl.when(step == 0)
        def _():
            loss_ref[...] = jnp.zeros_like(loss_ref)
            dg2_ref[...] = jnp.zeros_like(dg2_ref)
            db2_ref[...] = jnp.zeros_like(db2_ref)

        h1 = zh_ref[...] * g1_ref[...] + b1_ref[...]
        hbv = hb_ref[...]
        ffn = jnp.zeros((tm, D_MODEL), F32)
        acts = []
        for j in range(N_SHARD + 1):
            if j < N_SHARD:
                gj = _dot_nt(hbv, wg[j])
                uj = _dot_nt(hbv, wu[j])
                gs_ref[j] = gj.astype(BF16)
                us_ref[j] = uj.astype(BF16)
                acts.append((gj * _sigmoid(gj) * uj).astype(BF16))
                act_ref[j] = acts[j]
            if j > 0:
                ffn = ffn + _dot(acts[j - 1], wd[j - 1])
        ple = _dot(p_ref[...].astype(BF16), wpe[...])
        pg = _sigmoid(_dot(hbv, wpg[...]))
        pg_ref[...] = pg.astype(BF16)
        ple_ref[...] = ple.astype(BF16)
        z2 = ALPHA * h1 + ffn + pg * ple
        mu = jnp.mean(z2, axis=1, keepdims=True)
        zc = z2 - mu
        var = jnp.mean(zc * zc, axis=1, keepdims=True)
        r = lax.rsqrt(var + LN_EPS)
        zh2 = zc * r
        err = zh2 * g2_ref[...] + b2_ref[...] - t_ref[...]
        loss_ref[...] += jnp.sum(err * err)
        dy = err * (1.0 / D_MODEL)
        dg2_ref[...] += jnp.sum(dy * zh2, axis=0, keepdims=True)
        db2_ref[...] += jnp.sum(dy, axis=0, keepdims=True)
        dzh = dy * g2_ref[...]
        m1 = jnp.mean(dzh, axis=1, keepdims=True)
        m2 = jnp.mean(dzh * zh2, axis=1, keepdims=True)
        dz2 = r * (dzh - m1 - zh2 * m2)
        dz_ref[...] = dz2
        dzb_ref[...] = dz2.astype(BF16)

    row = lambda w: pl.BlockSpec((tm, w), lambda i: (i, 0))
    const = lambda s: pl.BlockSpec(s, lambda i: (0, 0))
    sh = pl.BlockSpec((N_SHARD, tm, FFN_SHARD), lambda i: (0, i, 0))
    sh_shape = jax.ShapeDtypeStruct((N_SHARD, t, FFN_SHARD), BF16)
    hbm = pl.BlockSpec(memory_space=pl.ANY)
    return pl.pallas_call(
        body, name="ffn_fwd", grid=(t // tm,),
        in_specs=[row(D_MODEL), row(D_MODEL), row(PLE_DIM), row(D_MODEL),
                  const((1, D_MODEL)), const((1, D_MODEL)), const((1, D_MODEL)), const((1, D_MODEL)),
                  hbm, hbm, hbm, hbm, hbm],
        out_specs=[row(D_MODEL), row(D_MODEL), sh, sh, sh, row(D_MODEL), row(D_MODEL),
                   const((8, LANES)), const((1, D_MODEL)), const((1, D_MODEL))],
        out_shape=[jax.ShapeDtypeStruct((t, D_MODEL), F32), jax.ShapeDtypeStruct((t, D_MODEL), BF16),
                   sh_shape, sh_shape, sh_shape,
                   jax.ShapeDtypeStruct((t, D_MODEL), BF16), jax.ShapeDtypeStruct((t, D_MODEL), BF16),
                   jax.ShapeDtypeStruct((8, LANES), F32),
                   jax.ShapeDtypeStruct((1, D_MODEL), F32), jax.ShapeDtypeStruct((1, D_MODEL), F32)],
        scratch_shapes=[pltpu.VMEM(wg4.shape, BF16), pltpu.VMEM(wu4.shape, BF16), pltpu.VMEM(wd4.shape, BF16),
                        pltpu.VMEM(wpe.shape, BF16), pltpu.VMEM(wpg.shape, BF16),
                        pltpu.SemaphoreType.DMA((3 * N_SHARD + 2,))],
        compiler_params=_params("arbitrary", vmem=VMEM_LIMIT),
    )(zh1, hb, p2d, tgt, g1, b1, g2, b2, wg4, wu4, wd4, wpe, wpg)


def _ret_tables(lgf, lgb):
    c = CHUNK
    row = lax.broadcasted_iota(jnp.int32, (c, LANES), 0).astype(F32)
    ii = lax.broadcasted_iota(jnp.int32, (c, c), 0).astype(F32)
    jj = lax.broadcasted_iota(jnp.int32, (c, c), 1).astype(F32)
    diff = ii - jj
    dmats = []
    for h in range(2):
        lf = lgf[:, h * HEAD_DIM:h * HEAD_DIM + 1]
        lb = lgb[:, h * HEAD_DIM:h * HEAD_DIM + 1]
        dmats.append(jnp.where(diff > 0, jnp.exp(lf * jnp.maximum(diff, 0.0)),
                               jnp.where(diff < 0, jnp.exp(lb * jnp.maximum(-diff, 0.0)), 2.0)))
    tab = dict(
        qdec_f=jnp.exp(lgf * (row + 1.0)), kdec_f=jnp.exp(lgf * (c - 1.0 - row)),
        qdec_b=jnp.exp(lgb * (c - row)), kdec_b=jnp.exp(lgb * row),
        cdec_f=jnp.exp(lgf * c), cdec_b=jnp.exp(lgb * c),
        d0=dmats[0], d1=dmats[1], row=row, diff=diff)
    r = lax.broadcasted_iota(jnp.int32, (LANES, LANES), 0) < HEAD_DIM
    cc = lax.broadcasted_iota(jnp.int32, (LANES, LANES), 1) < HEAD_DIM
    tab["bd"] = r == cc
    tab["m0"] = lax.broadcasted_iota(jnp.int32, (c, LANES), 1) < HEAD_DIM
    return tab


def _ret_specs(bsz, s):
    blk = lambda cb: pl.BlockSpec((bsz, s, LANES), lambda p, cb=cb: (0, 0, cb + p))
    lane = pl.BlockSpec((None, 1, LANES), lambda p: (p, 0, 0))
    gain = pl.BlockSpec((1, LANES), lambda p: (0, p))
    pair = pl.BlockSpec((bsz, s, LANES), lambda p: (0, 0, p))
    return blk, lane, gain, pair


def _ret_kv_states(tb, k_ref, v_ref, rb_ref, kvf_ref, n_chunk):
    c = CHUNK
    bsz = k_ref.shape[0]
    bd = tb["bd"]

    def step(i, rbs):
        n = n_chunk - 1 - i
        sl = pl.ds(pl.multiple_of(n * c, c), c)
        kfb = []
        for b in range(bsz):
            k32 = k_ref[b, sl, :].astype(F32)
            kfb.append(jnp.concatenate([k32 * tb["kdec_f"], k32 * tb["kdec_b"]], axis=1).astype(BF16))
        kvs = [_dot_tn(kfb[b], v_ref[b, sl, :]) for b in range(bsz)]
        new = []
        for b in range(bsz):
            rb_ref[b, n] = rbs[b]
            kvf_ref[b, n] = jnp.where(bd, kvs[b][0:LANES], 0.0)
            new.append(rbs[b] * tb["cdec_b"] + jnp.where(bd, kvs[b][LANES:], 0.0))
        return tuple(new)

    lax.fori_loop(0, n_chunk, step, tuple(jnp.zeros((LANES, LANES), F32) for _ in range(bsz)))


def _split_rows(x, m0):
    return jnp.concatenate([jnp.where(m0, x, 0.0), jnp.where(m0, 0.0, x)], axis=0).astype(BF16)


def _ret_fwd(u3, lgf_l, lgb_l, gn_gain, rider=None):
    bsz, s, _ = u3.shape
    n_chunk = s // CHUNK
    c = CHUNK

    def body(q_ref, k_ref, v_ref, g_ref, lgf_ref, lgb_ref, gain_ref, y_ref, o_ref, rb_ref, kvf_ref):
        tb = _ret_tables(lgf_ref[...], lgb_ref[...])
        m0 = tb["m0"]
        gain = gain_ref[...]
        rows = range(bsz)
        _ret_kv_states(tb, k_ref, v_ref, rb_ref, kvf_ref, n_chunk)

        def chunk(n, rfs):
            sl = pl.ds(pl.multiple_of(n * c, c), c)
            qs = [q_ref[b, sl, :].astype(F32) * 0.125 for b in rows]
            s01 = [_dot_nt(_split_rows(qs[b], m0), k_ref[b, sl, :]) for b in rows]
            ys = []
            for b in rows:
                lhs = jnp.concatenate([s01[b][0:c] * tb["d0"], s01[b][c:] * tb["d1"],
                                       qs[b] * tb["qdec_f"], qs[b] * tb["qdec_b"]], axis=1).astype(BF16)
                rhs = jnp.concatenate([_split_rows(v_ref[b, sl, :].astype(F32), m0),
                                       rfs[b].astype(BF16), rb_ref[b, n].astype(BF16)], axis=0)
                ys.append(_dot(lhs, rhs))
            new = []
            for b in rows:
                y = ys[b]
                mu = _head_mean(y, m0)
                yc = y - mu
                var = _head_mean(yc * yc, m0)
                yh = yc * lax.rsqrt(var + GN_EPS)
                g = g_ref[b, sl, :].astype(F32)
                y_ref[b, sl, :] = y
                o_ref[b, sl, :] = (yh * gain * (g * _sigmoid(g))).astype(BF16)
                new.append(rfs[b] * tb["cdec_f"] + kvf_ref[b, n])
            return tuple(new)

        lax.fori_loop(0, n_chunk, chunk, tuple(jnp.zeros((LANES, LANES), F32) for _ in rows))

    blk, lane, gain, pair = _ret_specs(bsz, s)
    state = pltpu.VMEM((bsz, n_chunk, LANES, LANES), F32)
    return _hosted_call(
        body, "ret_fwd", (4,),
        in_specs=[blk(CB_RQ), blk(CB_RK), blk(CB_RV), blk(CB_RG), lane, lane, gain],
        out_specs=[pair, pair],
        out_shape=[jax.ShapeDtypeStruct((bsz, s, RET_WIDTH), F32), jax.ShapeDtypeStruct((bsz, s, RET_WIDTH), BF16)],
        scratch_shapes=[state, state],
        operands=(u3, u3, u3, u3, lgf_l, lgb_l, gn_gain), rider=rider)


def _ret_bwd(u3, y_pre, d_o, lgf_l, lgb_l, gn_gain, rider=None):
    bsz, s, _ = u3.shape
    n_chunk = s // CHUNK
    c = CHUNK

    def body(q_ref, k_ref, v_ref, g_ref, y_ref, do_ref, lgf_ref, lgb_ref, gain_ref,
             dq_ref, dk_ref, dv_ref, dg_ref, part_ref,
             rb_ref, kvf_ref, rf_ref, dirf_ref, dy_ref, dk_acc, dv_acc, pa0, pa1, vec_ref):
        tb = _ret_tables(lgf_ref[...], lgb_ref[...])
        m0, bd, row = tb["m0"], tb["bd"], tb["row"]
        gain = gain_ref[...]
        wf = jnp.maximum(tb["diff"], 0.0)
        wb = jnp.maximum(-tb["diff"], 0.0)
        rows = range(bsz)
        zero_states = tuple(jnp.zeros((LANES, LANES), F32) for _ in rows)
        for ref in (pa0, pa1):
            ref[...] = jnp.zeros_like(ref)
        vec_ref[...] = jnp.zeros_like(vec_ref)
        _ret_kv_states(tb, k_ref, v_ref, rb_ref, kvf_ref, n_chunk)

        def sweep_fwd(n, carry):
            rfs, gbs = carry
            sl = pl.ds(pl.multiple_of(n * c, c), c)
            qs, ks, vs, dys, dybs, q01, k01, dy01 = [], [], [], [], [], [], [], []
            dgain = jnp.zeros((1, LANES), F32)
            for b in rows:
                q = q_ref[b, sl, :].astype(F32) * 0.125
                k = k_ref[b, sl, :]
                y = y_ref[b, sl, :]
                do = do_ref[b, sl, :].astype(F32)
                g = g_ref[b, sl, :].astype(F32)
                mu = _head_mean(y, m0)
                yc = y - mu
                rstd = lax.rsqrt(_head_mean(yc * yc, m0) + GN_EPS)
                yh = yc * rstd
                sg = _sigmoid(g)
                sil = g * sg
                dyh = do * gain * sil
                dg_ref[b, sl, :] = (do * yh * gain * sg * (1.0 + g * (1.0 - sg))).astype(BF16)
                dgain = dgain + jnp.sum(do * yh * sil, axis=0, keepdims=True)
                dy = rstd * (dyh - _head_mean(dyh, m0) - yh * _head_mean(dyh * yh, m0))
                dyb = dy.astype(BF16)
                dy_ref[b, sl, :] = dyb
                rf_ref[b, n] = rfs[b]
                qs.append(q)
                ks.append(k)
                vs.append(v_ref[b, sl, :])
                dys.append(dy)
                dybs.append(dyb)
                q01.append(_split_rows(q, m0))
                k01.append(_split_rows(k.astype(F32), m0))
                dy01.append(_split_rows(dy, m0))
            s01 = [_dot_nt(q01[b], ks[b]) for b in rows]
            da01 = [_dot_nt(dy01[b], vs[b]) for b in rows]
            rbn = [rb_ref[b, n] for b in rows]
            states = [jnp.concatenate([rfs[b], rbn[b]], axis=0).astype(BF16) for b in rows]
            dqc = [_dot_nt(dybs[b], states[b]) for b in rows]
            gbb = [gbs[b].astype(BF16) for b in rows]
            dkb = [_dot_nt(vs[b], gbb[b]) for b in rows]
            qfb = [jnp.concatenate([qs[b] * tb["qdec_f"], qs[b] * tb["qdec_b"]], axis=1) for b in rows]
            direct = [_dot_tn(qfb[b].astype(BF16), dybs[b]) for b in rows]
            ds_cat, ds_rows, a_rows = [], [], []
            for b in rows:
                a0 = s01[b][0:c] * tb["d0"]
                a1 = s01[b][c:] * tb["d1"]
                pa0[...] += da01[b][0:c] * a0
                pa1[...] += da01[b][c:] * a1
                ds0 = da01[b][0:c] * tb["d0"]
                ds1 = da01[b][c:] * tb["d1"]
                ds_cat.append(jnp.concatenate([ds0, ds1], axis=1).astype(BF16))
                ds_rows.append(jnp.concatenate([ds0, ds1], axis=0).astype(BF16))
                a_rows.append(jnp.concatenate([a0, a1], axis=0).astype(BF16))
            kbd = [ks[b].astype(F32) * tb["kdec_b"] for b in rows]
            dq_in = [_dot(ds_cat[b], k01[b]) for b in rows]
            dk_in = [_dot_tn(ds_rows[b], q01[b]) for b in rows]
            dv_in = [_dot_tn(a_rows[b], dy01[b]) for b in rows]
            dv_gb = [_dot(kbd[b].astype(BF16), gbb[b]) for b in rows]
            new_rf, new_gb = [], []
            dlf = jnp.zeros((1, LANES), F32)
            dlb = jnp.zeros((1, LANES), F32)
            for b in rows:
                dqf, dqb = dqc[b][:, 0:LANES], dqc[b][:, LANES:]
                qf, qb = qfb[b][:, 0:LANES], qfb[b][:, LANES:]
                dq = dq_in[b] + dqf * tb["qdec_f"] + dqb * tb["qdec_b"]
                dq_ref[b, sl, :] = (dq * 0.125).astype(BF16)
                dk_acc[b, sl, :] = dk_in[b] + dkb[b] * tb["kdec_b"]
                dv_acc[b, sl, :] = dv_in[b] + dv_gb[b]
                dlf = dlf + jnp.sum((row + 1.0) * qf * dqf, axis=0, keepdims=True)
                dlb = dlb + jnp.sum((c - row) * qb * dqb + row * kbd[b] * dkb[b], axis=0, keepdims=True)
                dlb = dlb + c * tb["cdec_b"] * jnp.sum(gbs[b] * rbn[b], axis=0, keepdims=True)
                dirf_ref[b, n] = jnp.where(bd, direct[b][0:LANES], 0.0)
                new_gb.append(jnp.where(bd, direct[b][LANES:], 0.0) + tb["cdec_b"] * gbs[b])
                new_rf.append(rfs[b] * tb["cdec_f"] + kvf_ref[b, n])
            vec_ref[0:1, :] += dlf
            vec_ref[1:2, :] += dlb
            vec_ref[6:7, :] += dgain
            return tuple(new_rf), tuple(new_gb)

        lax.fori_loop(0, n_chunk, sweep_fwd, (zero_states, zero_states))

        def sweep_bwd(i, gfs):
            n = n_chunk - 1 - i
            sl = pl.ds(pl.multiple_of(n * c, c), c)
            gfb = [gfs[b].astype(BF16) for b in rows]
            kfd = [k_ref[b, sl, :].astype(F32) * tb["kdec_f"] for b in rows]
            dkf = [_dot_nt(v_ref[b, sl, :], gfb[b]) for b in rows]
            dvf = [_dot(kfd[b].astype(BF16), gfb[b]) for b in rows]
            new = []
            dlf = jnp.zeros((1, LANES), F32)
            for b in rows:
                dk_ref[b, sl, :] = (dk_acc[b, sl, :] + dkf[b] * tb["kdec_f"]).astype(BF16)
                dv_ref[b, sl, :] = (dv_acc[b, sl, :] + dvf[b]).astype(BF16)
                dlf = dlf + jnp.sum((c - 1.0 - row) * kfd[b] * dkf[b], axis=0, keepdims=True)
                dlf = dlf + c * tb["cdec_f"] * jnp.sum(gfs[b] * rf_ref[b, n], axis=0, keepdims=True)
                new.append(dirf_ref[b, n] + tb["cdec_f"] * gfs[b])
            vec_ref[0:1, :] += dlf
            return tuple(new)

        lax.fori_loop(0, n_chunk, sweep_bwd, zero_states)
        vec_ref[2:3, :] = jnp.sum(pa0[...] * wf, axis=0, keepdims=True)
        vec_ref[3:4, :] = jnp.sum(pa1[...] * wf, axis=0, keepdims=True)
        vec_ref[4:5, :] = jnp.sum(pa0[...] * wb, axis=0, keepdims=True)
        vec_ref[5:6, :] = jnp.sum(pa1[...] * wb, axis=0, keepdims=True)
        part_ref[...] = vec_ref[...]

    blk, lane, gain, pair = _ret_specs(bsz, s)
    out_bf = jax.ShapeDtypeStruct((bsz, s, RET_WIDTH), BF16)
    state = pltpu.VMEM((bsz, n_chunk, LANES, LANES), F32)
    return _hosted_call(
        body, "ret_bwd", (4,),
        in_specs=[blk(CB_RQ), blk(CB_RK), blk(CB_RV), blk(CB_RG), pair, pair, lane, lane, gain],
        out_specs=[pair, pair, pair, pair, pl.BlockSpec((None, 8, LANES), lambda p: (p, 0, 0))],
        out_shape=[out_bf, out_bf, out_bf, out_bf, jax.ShapeDtypeStruct((4, 8, LANES), F32)],
        scratch_shapes=[state, state, state, state,
                        pltpu.VMEM((bsz, s, LANES), BF16), pltpu.VMEM((bsz, s, LANES), F32),
                        pltpu.VMEM((bsz, s, LANES), F32),
                        pltpu.VMEM((c, c), F32), pltpu.VMEM((c, c), F32), pltpu.VMEM((8, LANES), F32)],
        operands=(u3, u3, u3, u3, y_pre, d_o, lgf_l, lgb_l, gn_gain), rider=rider)


def _attn_window_tables(n, s):
    qi = lax.broadcasted_iota(jnp.int32, (CHUNK, 3 * CHUNK), 0)
    kj = lax.broadcasted_iota(jnp.int32, (CHUNK, 3 * CHUNK), 1)
    dist = jnp.abs(kj - CHUNK - qi)
    kpos = n * CHUNK - CHUNK + kj
    valid = (dist <= CHUNK) & (kpos >= 0) & (kpos < s)
    return dist.astype(F32), valid


def _dup_kv_head(x, g):
    lane = lax.broadcasted_iota(jnp.int32, x.shape, 1)
    keep = (lane < HEAD_DIM) == (g == 0)
    xf = x.astype(F32)
    return jnp.where(keep, xf, pltpu.roll(xf, HEAD_DIM, 1))


def _attn_specs(s):
    q = pl.BlockSpec((None, s, 2 * LANES), lambda b, g: (b, 0, CB_AQ // 2 + g))
    k = pl.BlockSpec((None, s, LANES), lambda b, g: (b, 0, CB_AK))
    v = pl.BlockSpec((None, s, LANES), lambda b, g: (b, 0, CB_AV))
    grp = pl.BlockSpec((None, s, 2 * LANES), lambda b, g: (b, 0, g))
    smem = pl.BlockSpec(memory_space=pltpu.SMEM)
    return q, k, v, grp, smem


def _fill_padded(dst_ref, val, s):
    dst_ref[0:CHUNK, :] = jnp.zeros((CHUNK, LANES), dst_ref.dtype)
    dst_ref[CHUNK:CHUNK + s, :] = val.astype(dst_ref.dtype)
    dst_ref[CHUNK + s:2 * CHUNK + s, :] = jnp.zeros((CHUNK, LANES), dst_ref.dtype)


def _attn_probs(sc, slope, snk, dist, valid):
    sc = jnp.where(valid, sc - slope * dist, NEG_INF)
    m = jnp.maximum(jnp.max(sc, axis=1, keepdims=True), snk)
    e = jnp.exp(sc - m)
    es = jnp.exp(snk - m)
    inv = 1.0 / (jnp.sum(e, axis=1, keepdims=True) + es)
    return e * inv, es * inv


def _stack_heads(x2, m0):
    parts = []
    for pr in range(2):
        xp = x2[:, pr * LANES:(pr + 1) * LANES]
        parts += [jnp.where(m0, xp, 0.0), jnp.where(m0, 0.0, xp)]
    return jnp.concatenate(parts, axis=0).astype(BF16)


def _unstack_pair(x_all, pr, m0):
    return jnp.where(m0, x_all[(2 * pr) * CHUNK:(2 * pr + 1) * CHUNK], x_all[(2 * pr + 1) * CHUNK:(2 * pr + 2) * CHUNK])


def _attn_fwd(u3, slopes, sink, rider=None):
    bsz, s, _ = u3.shape
    n_blk = s // CHUNK

    def body(slope_ref, sink_ref, q_ref, k_ref, v_ref, o_ref, kp_ref, vp_ref):
        g = pl.program_id(1)
        _fill_padded(kp_ref, _dup_kv_head(k_ref[...], g), s)
        _fill_padded(vp_ref, _dup_kv_head(v_ref[...], g), s)
        m0 = lax.broadcasted_iota(jnp.int32, (CHUNK, LANES), 1) < HEAD_DIM

        def blk(n, carry):
            r0 = pl.multiple_of(n * CHUNK, CHUNK)
            kw = kp_ref[pl.ds(r0, 3 * CHUNK), :]
            vw = vp_ref[pl.ds(r0, 3 * CHUNK), :]
            dist, valid = _attn_window_tables(n, s)
            q_all = _stack_heads(q_ref[pl.ds(r0, CHUNK), :].astype(F32) * 0.125, m0)
            sc_all = _dot_nt(q_all, kw)
            probs = []
            for i in range(4):
                p, _ = _attn_probs(sc_all[i * CHUNK:(i + 1) * CHUNK], slope_ref[g * 4 + i], sink_ref[g * 4 + i],
                                   dist, valid)
                probs.append(p.astype(BF16))
            out_all = _dot(jnp.concatenate(probs, axis=0), vw)
            for pr in range(2):
                o_ref[pl.ds(r0, CHUNK), pr * LANES:(pr + 1) * LANES] = _unstack_pair(out_all, pr, m0).astype(BF16)
            return carry

        lax.fori_loop(0, n_blk, blk, 0)

    q, k, v, grp, smem = _attn_specs(s)
    return _hosted_call(
        body, "attn_fwd", (bsz, 2),
        in_specs=[smem, smem, q, k, v],
        out_specs=[grp],
        out_shape=[jax.ShapeDtypeStruct((bsz, s, ATTN_WIDTH), BF16)],
        scratch_shapes=[pltpu.VMEM((s + 2 * CHUNK, LANES), BF16), pltpu.VMEM((s + 2 * CHUNK, LANES), BF16)],
        operands=(slopes, sink, u3, u3, u3), rider=rider)


def _attn_bwd(u3, d_o, slopes, sink, rider=None):
    bsz, s, _ = u3.shape
    n_blk = s // CHUNK

    def body(slope_ref, sink_ref, q_ref, k_ref, v_ref, do_ref, dq_ref, dkv_ref, ds_ref,
             kp_ref, vp_ref, dk_acc, dv_acc):
        g = pl.program_id(1)
        _fill_padded(kp_ref, _dup_kv_head(k_ref[...], g), s)
        _fill_padded(vp_ref, _dup_kv_head(v_ref[...], g), s)
        dk_acc[...] = jnp.zeros_like(dk_acc)
        dv_acc[...] = jnp.zeros_like(dv_acc)
        m0 = lax.broadcasted_iota(jnp.int32, (CHUNK, LANES), 1) < HEAD_DIM

        def blk(n, dsink):
            r0 = pl.multiple_of(n * CHUNK, CHUNK)
            win = pl.ds(r0, 3 * CHUNK)
            kw = kp_ref[win, :]
            vw = vp_ref[win, :]
            dist, valid = _attn_window_tables(n, s)
            q_all = _stack_heads(q_ref[pl.ds(r0, CHUNK), :].astype(F32) * 0.125, m0)
            do_all = _stack_heads(do_ref[pl.ds(r0, CHUNK), :].astype(F32), m0)
            sc_all = _dot_nt(q_all, kw)
            dp_all = _dot_nt(do_all, vw)
            new_dsink, probs, dscs = [], [], []
            for i in range(4):
                rows = slice(i * CHUNK, (i + 1) * CHUNK)
                p, ps = _attn_probs(sc_all[rows], slope_ref[g * 4 + i], sink_ref[g * 4 + i], dist, valid)
                dp = dp_all[rows]
                delta = jnp.sum(p * dp, axis=1, keepdims=True)
                dscs.append((p * (dp - delta)).astype(BF16))
                probs.append(p.astype(BF16))
                dsh = jnp.sum(ps * delta, axis=0, keepdims=True)
                new_dsink.append(dsink[i] - jnp.broadcast_to(dsh, (1, LANES)))
            dsc_all = jnp.concatenate(dscs, axis=0)
            dq_all = _dot(dsc_all, kw)
            dk_acc[win, :] += _dot_tn(dsc_all, q_all)
            dv_acc[win, :] += _dot_tn(jnp.concatenate(probs, axis=0), do_all)
            for pr in range(2):
                dq_ref[pl.ds(r0, CHUNK), pr * LANES:(pr + 1) * LANES] = (
                    _unstack_pair(dq_all, pr, m0) * 0.125).astype(BF16)
            return tuple(new_dsink)

        dsink = lax.fori_loop(0, n_blk, blk, tuple(jnp.zeros((1, LANES), F32) for _ in range(4)))
        dk = dk_acc[CHUNK:CHUNK + s, :]
        dv = dv_acc[CHUNK:CHUNK + s, :]
        lane = lax.broadcasted_iota(jnp.int32, (s, LANES), 1)
        fold = lambda a: a + pltpu.roll(a, HEAD_DIM, 1)
        dkv_ref[...] = jnp.where(lane < HEAD_DIM, fold(dk), fold(dv)).astype(BF16)
        ds_ref[...] = jnp.zeros_like(ds_ref)
        for i in range(4):
            ds_ref[i:i + 1, :] = dsink[i]

    q, k, v, grp, smem = _attn_specs(s)
    return _hosted_call(
        body, "attn_bwd", (bsz, 2),
        in_specs=[smem, smem, q, k, v, grp],
        out_specs=[grp, pl.BlockSpec((None, s, LANES), lambda b, g: (b, 0, g)),
                   pl.BlockSpec((None, None, 8, LANES), lambda b, g: (b, g, 0, 0))],
        out_shape=[jax.ShapeDtypeStruct((bsz, s, ATTN_WIDTH), BF16), jax.ShapeDtypeStruct((bsz, s, 2 * LANES), BF16),
                   jax.ShapeDtypeStruct((bsz, 2, 8, LANES), F32)],
        scratch_shapes=[pltpu.VMEM((s + 2 * CHUNK, LANES), BF16), pltpu.VMEM((s + 2 * CHUNK, LANES), BF16),
                        pltpu.VMEM((s + 2 * CHUNK, LANES), F32), pltpu.VMEM((s + 2 * CHUNK, LANES), F32)],
        operands=(slopes, sink, u3, u3, u3, d_o), rider=rider)


def _ffn_bwd(dz2, gs, us, pg, ple, zh1, r1, g1, wg4, wu4, wd4, wpg, w_out):
    t = dz2.shape[0]
    tm = 256

    def body(dz_ref, gs_ref, us_ref, pg_ref, ple_ref, zh_ref, r_ref, g1_ref,
             wg_hbm, wu_hbm, wd_hbm, wpg_hbm, wo_hbm,
             dgs_ref, dus_ref, dsp_ref, dple_ref, dz1_ref, dyr_ref, dya_ref, dg1_ref, db1_ref,
             wg, wu, wd, wpg, wo, wsem):
        step = pl.program_id(0)
        loads = [(h.at[j], v.at[j]) for j in range(N_SHARD) for h, v in ((wd_hbm, wd), (wg_hbm, wg), (wu_hbm, wu))]
        _load_resident(step, loads + [(wpg_hbm, wpg), (wo_hbm, wo)], wsem)

        @pl.when(step == 0)
        def _():
            dg1_ref[...] = jnp.zeros_like(dg1_ref)
            db1_ref[...] = jnp.zeros_like(db1_ref)

        dz = dz_ref[...]
        dzb = dz.astype(BF16)
        dh = ALPHA * dz
        pending = []
        for j in range(N_SHARD + 1):
            if j < N_SHARD:
                da = _dot_nt(dzb, wd[j])
                gj = gs_ref[j].astype(F32)
                uj = us_ref[j].astype(F32)
                sg = _sigmoid(gj)
                dgj = (da * uj * sg * (1.0 + gj * (1.0 - sg))).astype(BF16)
                duj = (da * gj * sg).astype(BF16)
                dgs_ref[j] = dgj
                dus_ref[j] = duj
                pending.append((dgj, duj))
            if j > 0:
                dgp, dup = pending[j - 1]
                dh = dh + _dot(dgp, wg[j - 1]) + _dot(dup, wu[j - 1])
        pgv = pg_ref[...].astype(F32)
        plev = ple_ref[...].astype(F32)
        dple_ref[...] = (dz * pgv).astype(BF16)
        dsp = (dz * plev * pgv * (1.0 - pgv)).astype(BF16)
        dsp_ref[...] = dsp
        dh = dh + _dot_nt(dsp, wpg[...])
        zh = zh_ref[...]
        dg1_ref[...] += jnp.sum(dh * zh, axis=0, keepdims=True)
        db1_ref[...] += jnp.sum(dh, axis=0, keepdims=True)
        dzh = dh * g1_ref[...]
        m1 = jnp.mean(dzh, axis=1, keepdims=True)
        m2 = jnp.mean(dzh * zh, axis=1, keepdims=True)
        dz1 = r_ref[...] * (dzh - m1 - zh * m2)
        dz1_ref[...] = dz1
        dyc = _dot_nt(dz1.astype(BF16), wo[...])
        dyr_ref[...] = dyc[:, 0:RET_WIDTH].astype(BF16)
        dya_ref[...] = dyc[:, RET_WIDTH:].astype(BF16)

    row = lambda w: pl.BlockSpec((tm, w), lambda i: (i, 0))
    const = lambda s: pl.BlockSpec(s, lambda i: (0, 0))
    sh = pl.BlockSpec((N_SHARD, tm, FFN_SHARD), lambda i: (0, i, 0))
    hbm = pl.BlockSpec(memory_space=pl.ANY)
    sh_shape = jax.ShapeDtypeStruct((N_SHARD, t, FFN_SHARD), BF16)
    return pl.pallas_call(
        body, name="ffn_bwd", grid=(t // tm,),
        in_specs=[row(D_MODEL), sh, sh, row(D_MODEL), row(D_MODEL), row(D_MODEL), row(1), const((1, D_MODEL)),
                  hbm, hbm, hbm, hbm, hbm],
        out_specs=[sh, sh, row(D_MODEL), row(D_MODEL), row(D_MODEL), row(RET_WIDTH), row(ATTN_WIDTH),
                   const((1, D_MODEL)), const((1, D_MODEL))],
        out_shape=[sh_shape, sh_shape, jax.ShapeDtypeStruct((t, D_MODEL), BF16),
                   jax.ShapeDtypeStruct((t, D_MODEL), BF16), jax.ShapeDtypeStruct((t, D_MODEL), F32),
                   jax.ShapeDtypeStruct((t, RET_WIDTH), BF16), jax.ShapeDtypeStruct((t, ATTN_WIDTH), BF16),
                   jax.ShapeDtypeStruct((1, D_MODEL), F32), jax.ShapeDtypeStruct((1, D_MODEL), F32)],
        scratch_shapes=[pltpu.VMEM(wg4.shape, BF16), pltpu.VMEM(wu4.shape, BF16), pltpu.VMEM(wd4.shape, BF16),
                        pltpu.VMEM(wpg.shape, BF16), pltpu.VMEM(w_out.shape, BF16),
                        pltpu.SemaphoreType.DMA((3 * N_SHARD + 2,))],
        compiler_params=_params("arbitrary", vmem=VMEM_LIMIT),
    )(dz2, gs, us, pg, ple, zh1, r1, g1, wg4, wu4, wd4, wpg, w_out)


def _wgrad_misc(y_ret, y_att, dz1, hb, dsp, p2d, dple, rider=None):
    t = dz1.shape[0]
    tk = min(t, 512)

    def body(yr_ref, ya_ref, dz_ref, hb_ref, dsp_ref, p_ref, dple_ref, wo_ref, wpg_ref, wpe_ref):
        @pl.when(pl.program_id(0) == 0)
        def _():
            wo_ref[...] = jnp.zeros_like(wo_ref)
            wpg_ref[...] = jnp.zeros_like(wpg_ref)
            wpe_ref[...] = jnp.zeros_like(wpe_ref)

        dzb = dz_ref[...].astype(BF16)
        wo_ref[0:RET_WIDTH, :] += _dot_tn(yr_ref[...], dzb)
        wo_ref[RET_WIDTH:, :] += _dot_tn(ya_ref[...], dzb)
        wpg_ref[...] += _dot_tn(hb_ref[...], dsp_ref[...])
        wpe_ref[...] += _dot_tn(p_ref[...].astype(BF16), dple_ref[...])

    row = lambda w: pl.BlockSpec((tk, w), lambda k: (k, 0))
    const = lambda s: pl.BlockSpec(s, lambda k: (0, 0))
    return _hosted_call(
        body, "wgrad_misc", (t // tk,),
        in_specs=[row(RET_WIDTH), row(ATTN_WIDTH), row(D_MODEL), row(D_MODEL), row(D_MODEL), row(PLE_DIM),
                  row(D_MODEL)],
        out_specs=[const((D_MODEL, D_MODEL)), const((D_MODEL, D_MODEL)), const((PLE_DIM, D_MODEL))],
        out_shape=[jax.ShapeDtypeStruct((D_MODEL, D_MODEL), F32), jax.ShapeDtypeStruct((D_MODEL, D_MODEL), F32),
                   jax.ShapeDtypeStruct((PLE_DIM, D_MODEL), F32)],
        scratch_shapes=[], operands=(y_ret, y_att, dz1, hb, dsp, p2d, dple), rider=rider, semantics=["arbitrary"])


def _wgrad_ffn(acts, dgs, dus, hb, dz2b):
    t = dz2b.shape[0]
    tk = min(t, 512)

    def body(act_ref, dg_ref, du_ref, hb_ref, dz_ref, og_ref, ou_ref, od_ref):
        @pl.when(pl.program_id(1) == 0)
        def _():
            og_ref[...] = jnp.zeros_like(og_ref)
            ou_ref[...] = jnp.zeros_like(ou_ref)
            od_ref[...] = jnp.zeros_like(od_ref)

        hbv = hb_ref[...]
        og_ref[...] += _dot_tn(dg_ref[...], hbv)
        ou_ref[...] += _dot_tn(du_ref[...], hbv)
        od_ref[...] += _dot_tn(act_ref[...], dz_ref[...])

    a_spec = pl.BlockSpec((None, tk, FFN_SHARD), lambda j, k: (j, k, 0))
    b_spec = pl.BlockSpec((tk, D_MODEL), lambda j, k: (k, 0))
    o_spec = pl.BlockSpec((None, FFN_SHARD, D_MODEL), lambda j, k: (j, 0, 0))
    o_shape = jax.ShapeDtypeStruct((N_SHARD, FFN_SHARD, D_MODEL), F32)
    return pl.pallas_call(
        body, name="wgrad_ffn", grid=(N_SHARD, t // tk),
        in_specs=[a_spec, a_spec, a_spec, b_spec, b_spec],
        out_specs=[o_spec, o_spec, o_spec], out_shape=[o_shape, o_shape, o_shape],
        compiler_params=_params("parallel", "arbitrary", vmem=VMEM_LIMIT),
    )(acts, dgs, dus, hb, dz2b)


KV_ORDER = (0, 128, 64, 192)


def _wgrad_in(pieces, x2d):
    t = x2d.shape[0]
    tk = min(t, 512)
    kv0 = CB_AK * LANES

    def body(p0, p1, p2, p3, p4, pkv, x_ref, o_ref):
        @pl.when(pl.program_id(0) == 0)
        def _():
            o_ref[...] = jnp.zeros_like(o_ref)

        xb = x_ref[...].astype(BF16)
        for i, ref in enumerate((p0, p1, p2, p3, p4)):
            o_ref[i * 512:(i + 1) * 512, :] += _dot_tn(ref[...], xb)
        dkv = _dot_tn(pkv[...], xb)
        for i, o in enumerate(KV_ORDER):
            o_ref[kv0 + o:kv0 + o + HEAD_DIM, :] += dkv[i * HEAD_DIM:(i + 1) * HEAD_DIM]

    row = lambda w: pl.BlockSpec((tk, w), lambda k: (k, 0))
    return pl.pallas_call(
        body, name="wgrad_in", grid=(t // tk,),
        in_specs=[row(512)] * 5 + [row(256), row(D_MODEL)],
        out_specs=pl.BlockSpec((IN_WIDTH, D_MODEL), lambda k: (0, 0)),
        out_shape=jax.ShapeDtypeStruct((IN_WIDTH, D_MODEL), F32),
        compiler_params=_params("arbitrary", vmem=VMEM_LIMIT),
    )(*pieces, x2d)


def _inproj_bwd(dz1, pieces, w_main, w_kv, rider=None):
    t = dz1.shape[0]
    tm = 512

    def body(dz_ref, p0, p1, p2, p3, p4, pkv, wm_ref, wkv_ref, o_ref):
        acc = ALPHA * dz_ref[...]
        for i, ref in enumerate((p0, p1, p2, p3, p4)):
            acc = acc + _dot(ref[...], wm_ref[i * 512:(i + 1) * 512, :])
        o_ref[...] = acc + _dot(pkv[...], wkv_ref[...])

    row = lambda w: pl.BlockSpec((tm, w), lambda i: (i, 0))
    const = lambda s: pl.BlockSpec(s, lambda i: (0, 0))
    return _hosted_call(
        body, "inproj_bwd", (t // tm,),
        in_specs=[row(D_MODEL)] + [row(512)] * 5 + [row(256), const(w_main.shape), const(w_kv.shape)],
        out_specs=[row(D_MODEL)],
        out_shape=[jax.ShapeDtypeStruct((t, D_MODEL), F32)],
        scratch_shapes=[], operands=(dz1, *pieces, w_main, w_kv), rider=rider)


def _coords():
    return lax.axis_index("x"), lax.axis_index("y"), lax.axis_index("c")


def _chip_of(x, y, rel):
    return (1 - x if rel & 2 else x), (1 - y if rel & 1 else y)


def _all_gather_weights(shards):
    first = _gather_chips_rider(shards)
    second = _gather_pass_rider([jax.ShapeDtypeStruct((N_SHARD,) + s.shape, s.dtype) for s in shards], chained=True)
    return _run_riders("gather_weights", shards, first.out_shapes, [first, second])


def _run_riders(name, ins, out_shapes, riders):
    n_in, n_out = len(ins), len(out_shapes)

    def body(*refs):
        in_refs, out_refs = refs[:n_in], refs[n_in:n_in + n_out]
        k = n_in + n_out
        for r in riders:
            sems = refs[k:k + len(r.sems)]
            k += len(r.sems)
            r.start(in_refs, out_refs, sems)
            r.finish(in_refs, out_refs, sems)

    hbm = pl.BlockSpec(memory_space=pl.ANY)
    return pl.pallas_call(
        body, name=name, in_specs=[hbm] * n_in, out_specs=[hbm] * n_out, out_shape=list(out_shapes),
        scratch_shapes=[s for r in riders for s in r.sems],
    )(*ins)


def _gather_half(outs, w, chip, cc):
    h = outs[w].shape[1] // 2
    return outs[w].at[chip, pl.ds(cc * h, h), :]


def _gather_chips_rider(shards):
    nw = len(shards)

    def copies(ins, outs, sems):
        send, recv, lsend, lrecv = sems
        x, y, c = _coords()
        me = 2 * x + y
        own = [pltpu.make_async_remote_copy(
            src_ref=ins[w], dst_ref=outs[w].at[me], send_sem=lsend.at[w], recv_sem=lrecv.at[w],
            device_id=(x, y, 1 - c), device_id_type=MESH) for w in range(nw)]
        out, arrive = [], []
        for rel in (1, 2, 3):
            kx, ky = _chip_of(x, y, rel)
            for w in range(nw):
                h = shards[w].shape[0] // 2
                sem = dict(send_sem=send.at[w * 3 + rel - 1], recv_sem=recv.at[w * 3 + rel - 1],
                           device_id=(kx, ky, c), device_id_type=MESH)
                out.append(pltpu.make_async_remote_copy(
                    src_ref=ins[w].at[pl.ds(c * h, h), :], dst_ref=_gather_half(outs, w, me, c), **sem))
                theirs = _gather_half(outs, w, 2 * kx + ky, c)
                arrive.append(pltpu.make_async_remote_copy(src_ref=theirs, dst_ref=theirs, **sem))
        return own, out, arrive

    def start(ins, outs, sems):
        own, out, _ = copies(ins, outs, sems)
        for cp in own + out:
            cp.start()

    def finish(ins, outs, sems):
        own, out, arrive = copies(ins, outs, sems)
        for cp in arrive:
            cp.wait_recv()
        for cp in out:
            cp.wait_send()
        for cp in own:
            cp.wait()

    dma = pltpu.SemaphoreType.DMA
    return _Rider(shards, [jax.ShapeDtypeStruct((N_SHARD,) + s.shape, s.dtype) for s in shards],
                  [dma((3 * nw,)), dma((3 * nw,)), dma((nw,)), dma((nw,))], start, finish)


def _gather_pass_rider(gathered, chained=False):
    nw = len(gathered)

    def copies(outs, sems, cc):
        send, recv = sems
        x, y, c = _coords()
        res = []
        for rel in (1, 2, 3):
            kx, ky = _chip_of(x, y, rel)
            for w in range(nw):
                rows = _gather_half(outs, w, 2 * kx + ky, cc)
                res.append(pltpu.make_async_remote_copy(
                    src_ref=rows, dst_ref=rows, send_sem=send.at[w * 3 + rel - 1], recv_sem=recv.at[w * 3 + rel - 1],
                    device_id=(x, y, 1 - c), device_id_type=MESH))
        return res

    def start(ins, outs, sems):
        for cp in copies(outs, sems, lax.axis_index("c")):
            cp.start()

    def finish(ins, outs, sems):
        c = lax.axis_index("c")
        for cp in copies(outs, sems, 1 - c):
            cp.wait_recv()
        for cp in copies(outs, sems, c):
            cp.wait_send()

    dma = pltpu.SemaphoreType.DMA
    shapes = [jax.ShapeDtypeStruct(g.shape, g.dtype) for g in gathered]
    if chained:
        return _Rider([], [], [dma((3 * nw,)), dma((3 * nw,))], start, finish)
    return _Rider(gathered, shapes, [dma((3 * nw,)), dma((3 * nw,))], start, finish,
                  aliases={w: w for w in range(nw)})


def _exchange_halves_rider(parts):
    nw = len(parts)

    def copies(ins, outs, sems):
        send, recv = sems
        x, y, c = _coords()
        res = []
        for w in range(nw):
            h = parts[w].shape[1] // 2
            res.append(pltpu.make_async_remote_copy(
                src_ref=ins[w].at[:, pl.ds((1 - c) * h, h), :], dst_ref=outs[w],
                send_sem=send.at[w], recv_sem=recv.at[w], device_id=(x, y, 1 - c), device_id_type=MESH))
        return res

    def start(ins, outs, sems):
        for cp in copies(ins, outs, sems):
            cp.start()

    def finish(ins, outs, sems):
        for cp in copies(ins, outs, sems):
            cp.wait()

    dma = pltpu.SemaphoreType.DMA
    return _Rider(parts, [jax.ShapeDtypeStruct((N_SHARD, p.shape[1] // 2, p.shape[2]), F32) for p in parts],
                  [dma((nw,)), dma((nw,))], start, finish)


def _add_halves(parts, theirs, pos):
    nw = len(parts)
    split = 2

    def body(pos_ref, *refs):
        ins, oth = refs[:nw], refs[nw:2 * nw]
        o32, o16 = refs[2 * nw:3 * nw], refs[3 * nw:]
        sums = [ins[w][...] + oth[w][...] for w in range(nw)]
        for w in range(nw):
            o16[w][...] = sums[w].astype(BF16)

        @pl.when(pl.program_id(1) == pos_ref[0])
        def _():
            for w in range(nw):
                o32[w][...] = sums[w]

    in_specs, oth_specs, o32_specs, shapes32, shapes16 = [], [], [], [], []
    for p in parts:
        hb = p.shape[1] // 2 // split
        blk = (None, hb, p.shape[2])
        in_specs.append(pl.BlockSpec(blk, lambda i, j, pos_ref: (j, pos_ref[1] * split + i, 0)))
        oth_specs.append(pl.BlockSpec(blk, lambda i, j, pos_ref: (j, i, 0)))
        o32_specs.append(pl.BlockSpec((hb, p.shape[2]), lambda i, j, pos_ref: (i, 0)))
        shapes32.append(jax.ShapeDtypeStruct((p.shape[1] // 2, p.shape[2]), F32))
        shapes16.append(jax.ShapeDtypeStruct((N_SHARD, p.shape[1] // 2, p.shape[2]), BF16))
    return pl.pallas_call(
        body, name="add_halves",
        grid_spec=pltpu.PrefetchScalarGridSpec(
            num_scalar_prefetch=1, grid=(split, N_SHARD),
            in_specs=in_specs + oth_specs, out_specs=o32_specs + oth_specs),
        out_shape=shapes32 + shapes16,
        compiler_params=_params("parallel", "arbitrary", vmem=VMEM_LIMIT),
    )(pos, *parts, *theirs)


def _exchange_chips_rider(sums16):
    nw = len(sums16)

    def copies(ins, outs, sems):
        send, recv = sems
        x, y, c = _coords()
        res = []
        for rel in (1, 2, 3):
            kx, ky = _chip_of(x, y, rel)
            for w in range(nw):
                res.append(pltpu.make_async_remote_copy(
                    src_ref=ins[w].at[2 * kx + ky], dst_ref=outs[w].at[rel - 1],
                    send_sem=send.at[w * 3 + rel - 1], recv_sem=recv.at[w * 3 + rel - 1],
                    device_id=(kx, ky, c), device_id_type=MESH))
        return res

    def start(ins, outs, sems):
        for cp in copies(ins, outs, sems):
            cp.start()

    def finish(ins, outs, sems):
        for cp in copies(ins, outs, sems):
            cp.wait()

    dma = pltpu.SemaphoreType.DMA
    return _Rider(sums16, [jax.ShapeDtypeStruct((3,) + s.shape[1:], BF16) for s in sums16],
                  [dma((3 * nw,)), dma((3 * nw,))], start, finish)


def _add_chips(sums32, theirs, pos):
    nw = len(sums32)
    split = 2

    def body(pos_ref, *refs):
        ins, oth, outs = refs[:nw], refs[nw:2 * nw], refs[2 * nw:]
        for w in range(nw):
            acc = ins[w][...]
            for r in range(3):
                acc = acc + oth[w][r].astype(F32)
            outs[w][...] = acc

    in_specs, oth_specs, out_specs, shapes = [], [], [], []
    for s in sums32:
        hb = s.shape[0] // split
        in_specs.append(pl.BlockSpec((hb, s.shape[1]), lambda i, pos_ref: (i, 0)))
        oth_specs.append(pl.BlockSpec((3, hb, s.shape[1]), lambda i, pos_ref: (0, i, 0)))
        out_specs.append(pl.BlockSpec((hb, s.shape[1]), lambda i, pos_ref: (pos_ref[1] * split + i, 0)))
        shapes.append(jax.ShapeDtypeStruct((2 * s.shape[0], s.shape[1]), F32))
    return pl.pallas_call(
        body, name="add_chips",
        grid_spec=pltpu.PrefetchScalarGridSpec(
            num_scalar_prefetch=1, grid=(split,), in_specs=in_specs + oth_specs, out_specs=out_specs),
        out_shape=shapes,
        compiler_params=_params("parallel", vmem=VMEM_LIMIT),
    )(pos, *sums32, *theirs)


def _join_halves(shards):
    nw = len(shards)

    def body(*refs):
        outs = refs[nw:2 * nw]
        send, recv = refs[2 * nw:]
        x, y, c = _coords()

        def copy(w, cc):
            h = shards[w].shape[0] // 2
            rows = outs[w].at[pl.ds(cc * h, h), :]
            return pltpu.make_async_remote_copy(
                src_ref=rows, dst_ref=rows, send_sem=send.at[w], recv_sem=recv.at[w],
                device_id=(x, y, 1 - c), device_id_type=MESH)

        for w in range(nw):
            copy(w, c).start()
        for w in range(nw):
            copy(w, 1 - c).wait_recv()
            copy(w, c).wait_send()

    hbm = pl.BlockSpec(memory_space=pl.ANY)
    return pl.pallas_call(
        body, name="join_halves",
        in_specs=[hbm] * nw, out_specs=[hbm] * nw,
        out_shape=[jax.ShapeDtypeStruct(s.shape, F32) for s in shards],
        input_output_aliases={w: w for w in range(nw)},
        scratch_shapes=[pltpu.SemaphoreType.DMA((nw,)), pltpu.SemaphoreType.DMA((nw,))],
    )(*shards)


def _adamw_math(w, g, m, v):
    m = ADAM_B1 * m + (1.0 - ADAM_B1) * g
    v = ADAM_B2 * v + (1.0 - ADAM_B2) * (g * g)
    m_hat = m / (1.0 - ADAM_B1 ** ADAM_STEP)
    v_hat = v / (1.0 - ADAM_B2 ** ADAM_STEP)
    delta = -ADAM_LR * (m_hat / (jnp.sqrt(v_hat) + ADAM_EPS) + ADAM_WD * w)
    return delta, m, v


def _adamw(ws, gs, ms, vs):
    nw = len(ws)
    split = 8

    def body(*refs):
        w_r, g_r, m_r, v_r = (refs[i * nw:(i + 1) * nw] for i in range(4))
        d_o, m_o, v_o = (refs[(4 + i) * nw:(5 + i) * nw] for i in range(3))
        for k in range(nw):
            d, m, v = _adamw_math(w_r[k][...], g_r[k][...], m_r[k][...], v_r[k][...])
            d_o[k][...] = d
            m_o[k][...] = m
            v_o[k][...] = v

    specs = [pl.BlockSpec((w.shape[0] // split, w.shape[1]), lambda i: (i, 0)) for w in ws]
    shapes = [jax.ShapeDtypeStruct(w.shape, F32) for w in ws]
    outs = pl.pallas_call(
        body, name="adamw", grid=(split,),
        in_specs=specs * 4, out_specs=specs * 3, out_shape=shapes * 3,
        compiler_params=_params("parallel", vmem=VMEM_LIMIT),
    )(*ws, *gs, *ms, *vs)
    return outs[:nw], outs[nw:2 * nw], outs[2 * nw:]


SMALL_ROWS = 8
SMALL_COLS = D_MODEL
LOSS_COL = RET_WIDTH + 24


def _small_allreduce_adamw(part, w, m, v, rider=None):
    def body(part_ref, w_ref, m_ref, v_ref, g_out, d_out, m_out, v_out, all_ref, send, recv):
        x, y, c = _coords()
        me = 4 * x + 2 * y + c
        all_ref[me] = part_ref[...]
        copies = []
        for rel in range(1, 8):
            px = 1 - x if rel & 4 else x
            py = 1 - y if rel & 2 else y
            pc = 1 - c if rel & 1 else c
            copies.append(pltpu.make_async_remote_copy(
                src_ref=part_ref, dst_ref=all_ref.at[me],
                send_sem=send.at[rel - 1], recv_sem=recv.at[rel - 1], device_id=(px, py, pc), device_id_type=MESH))
        for cp in copies:
            cp.start()
        for cp in copies:
            cp.wait()
        g = all_ref[0]
        for k in range(1, 8):
            g = g + all_ref[k]
        d, mn, vn = _adamw_math(w_ref[...], g, m_ref[...], v_ref[...])
        g_out[...] = g
        d_out[...] = d
        m_out[...] = mn
        v_out[...] = vn

    vm = pl.BlockSpec(memory_space=pltpu.VMEM)
    shape = jax.ShapeDtypeStruct((SMALL_ROWS, SMALL_COLS), F32)
    return _hosted_call(
        body, "small_allreduce_adamw", (1,),
        in_specs=[vm] * 4, out_specs=[vm] * 4, out_shape=[shape] * 4,
        scratch_shapes=[pltpu.VMEM((8, SMALL_ROWS, SMALL_COLS), F32),
                        pltpu.SemaphoreType.DMA((7,)), pltpu.SemaphoreType.DMA((7,))],
        operands=(part, w, m, v), rider=rider, semantics=["arbitrary"])


SMALL_NAMES = ("ret_decay_fwd", "ret_decay_bwd", "attn_sink", "ret_gn_gain",
               "ln1_gain", "ln1_bias", "ln2_gain", "ln2_bias")


LN_NAMES = ("ln1_gain", "ln1_bias", "ln2_gain", "ln2_bias")


def _pack_small(vals, extra=None):
    tail = jnp.zeros((1, 1), F32) if extra is None else extra.reshape(1, 1)
    row4 = jnp.concatenate([vals["ret_gn_gain"], vals["ret_decay_fwd"], vals["ret_decay_bwd"], vals["attn_sink"],
                            tail, jnp.zeros((1, SMALL_COLS - LOSS_COL - 1), F32)], axis=1)
    rows = [vals[n] for n in LN_NAMES] + [row4, jnp.zeros((SMALL_ROWS - 5, SMALL_COLS), F32)]
    return jnp.concatenate(rows, axis=0)


def _unpack_small(packed):
    out = {n: packed[i:i + 1] for i, n in enumerate(LN_NAMES)}
    o = RET_WIDTH
    out.update(ret_gn_gain=packed[4:5, 0:o], ret_decay_fwd=packed[4:5, o:o + 8],
               ret_decay_bwd=packed[4:5, o + 8:o + 16], attn_sink=packed[4:5, o + 16:o + 24])
    return out


def _local_step(x, p, tgt, w_in_t, rest, small, pos=None, small_state=None):
    bsz, s, _ = x.shape
    t = bsz * s
    x2d = x.reshape(t, D_MODEL)
    p2d = p.reshape(t, PLE_DIM)
    tgt2d = tgt.reshape(t, D_MODEL)
    dec_f = small["ret_decay_fwd"].reshape(8)
    dec_b = small["ret_decay_bwd"].reshape(8)
    lg_f = jnp.log1p(-jnp.exp2(dec_f))
    lg_b = jnp.log1p(-jnp.exp2(dec_b))
    per_lane = lambda v: jnp.repeat(v, HEAD_DIM).reshape(4, 1, LANES)
    lgf_l, lgb_l = per_lane(lg_f), per_lane(lg_b)
    sink = small["attn_sink"].reshape(8)
    slopes = 2.0 ** (-(jnp.arange(8, dtype=F32) + 1.0))
    gn_gain = small["ret_gn_gain"]
    g1, b1, g2, b2 = (small[n] for n in ("ln1_gain", "ln1_bias", "ln2_gain", "ln2_bias"))

    dist = pos is not None
    chips = lambda names: _gather_chips_rider([rest[REST_NAMES.index(n)] for n in names])
    first, second, third = ("w_ffn_up",), ("w_out", "w_ffn_gate", "w_ple_proj", "w_ple_gate"), ("w_ffn_down",)
    u, *c1 = _inproj(x2d, w_in_t, rider=chips(first) if dist else None)
    u3 = u.reshape(bsz, s, IN_WIDTH)
    y_pre, y_ret, *o2 = _ret_fwd(u3, lgf_l, lgb_l, gn_gain,
                                 rider=_merge_riders([_gather_pass_rider(c1), chips(second)]) if dist else None)
    y_att, *o3 = _attn_fwd(u3, slopes, sink, rider=_merge_riders(
        [_gather_pass_rider(o2[len(first):]), chips(third)]) if dist else None)
    gathered = dict(zip(first, o2[:len(first)]))
    gathered.update(zip(second, o3[:len(second)]))
    w_out = _assemble_weights({"w_out": gathered["w_out"]})["w_out"] if dist else rest["w_out"]
    zh1, r1, hb, *o4 = _outproj_ln1(y_ret.reshape(t, RET_WIDTH), y_att.reshape(t, ATTN_WIDTH), x2d, w_out, g1, b1,
                                    rider=_gather_pass_rider(o3[len(second):]) if dist else None)
    gathered.update(zip(third, o4))
    wts = _assemble_weights(gathered) if dist else rest
    dz2, dz2b, gs, us, acts, pg, ple, sq, dg2, db2 = _ffn_fwd(
        zh1, hb, p2d, tgt2d, g1, b1, g2, b2, wts["gate4"], wts["up4"], wts["down4"], wts["ple_proj"], wts["ple_gate"])
    dgs, dus, dsp, dple, dz1, dyr, dya, dg1, db1 = _ffn_bwd(dz2, gs, us, pg, ple, zh1, r1, g1, wts["gate4"],
                                                          wts["up4"], wts["down4"], wts["ple_gate"], wts["w_out"])
    ffn_parts = list(_wgrad_ffn(acts, dgs, dus, hb, dz2b))
    d_w_out, d_ple_gate, d_ple_proj, *th_ffn = _wgrad_misc(
        y_ret.reshape(t, RET_WIDTH), y_att.reshape(t, ATTN_WIDTH), dz1, hb, dsp, p2d, dple,
        rider=_exchange_halves_rider(ffn_parts) if dist else None)
    misc_parts = [d_w_out.reshape(N_SHARD, D_MODEL // N_SHARD, D_MODEL),
                  d_ple_proj.reshape(PLE_DIM, N_SHARD, D_MODEL // N_SHARD).transpose(1, 0, 2),
                  d_ple_gate.reshape(N_SHARD, D_MODEL // N_SHARD, D_MODEL)]
    dyr3, dya3 = dyr.reshape(bsz, s, RET_WIDTH), dya.reshape(bsz, s, ATTN_WIDTH)
    if dist:
        s_ffn = _add_halves(ffn_parts, th_ffn, pos)
        drq, drk, drv, drg, rpart, *o5 = _ret_bwd(u3, y_pre, dyr3, lgf_l, lgb_l, gn_gain, rider=_merge_riders(
            [_exchange_chips_rider(s_ffn[3:5]), _exchange_halves_rider(misc_parts)]))
        s_misc = _add_halves(misc_parts, o5[2:], pos)
        daq, dakv, spart, *o6 = _attn_bwd(u3, dya3, slopes, sink,
                                          rider=_exchange_chips_rider([s_ffn[5]] + list(s_misc[3:])))
    else:
        drq, drk, drv, drg, rpart = _ret_bwd(u3, y_pre, dyr3, lgf_l, lgb_l, gn_gain)
        daq, dakv, spart = _attn_bwd(u3, dya3, slopes, sink)
    pieces = [a.reshape(t, -1) for a in (drq, drk, drv, drg, daq, dakv)]
    kv0 = CB_AK * LANES
    w_kv = jnp.concatenate([w_in_t[kv0 + o:kv0 + o + HEAD_DIM] for o in KV_ORDER], axis=0)
    d_in = _wgrad_in(pieces, x2d).reshape(N_SHARD, FFN_SHARD, D_MODEL)

    rsum = rpart
    lane_heads = lambda row: jnp.sum(row.reshape(4, 2, HEAD_DIM), axis=-1).reshape(8)
    dlg_f = lane_heads(rsum[:, 0, :]) + jnp.stack([jnp.sum(rsum[:, 2, :], -1), jnp.sum(rsum[:, 3, :], -1)], 1).reshape(8)
    dlg_b = lane_heads(rsum[:, 1, :]) + jnp.stack([jnp.sum(rsum[:, 4, :], -1), jnp.sum(rsum[:, 5, :], -1)], 1).reshape(8)
    chain = lambda d: -(math.log(2.0) * jnp.exp2(d)) / (1.0 - jnp.exp2(d))
    grads_small = {
        "ret_decay_fwd": (dlg_f * chain(dec_f)).reshape(1, 8),
        "ret_decay_bwd": (dlg_b * chain(dec_b)).reshape(1, 8),
        "attn_sink": jnp.sum(spart, axis=0)[:, 0:4, 0].reshape(1, 8),
        "ret_gn_gain": rsum[:, 6, :].reshape(1, RET_WIDTH),
        "ln1_gain": dg1, "ln1_bias": db1, "ln2_gain": dg2, "ln2_bias": db2,
    }
    if not dist:
        grad_x, = _inproj_bwd(dz1, pieces, w_in_t[:kv0], w_kv)
        grads_rest = [misc_parts[0]] + ffn_parts + misc_parts[1:]
        return sq[0, 0], grad_x.reshape(bsz, s, D_MODEL), d_in, grads_rest, grads_small
    *small_out, th_in = _small_allreduce_adamw(_pack_small(grads_small, sq[0, 0]), *small_state,
                                               rider=_exchange_halves_rider([d_in]))
    s_in = _add_halves([d_in], [th_in], pos)
    grad_x, chips_in = _inproj_bwd(dz1, pieces, w_in_t[:kv0], w_kv, rider=_exchange_chips_rider([s_in[1]]))
    sums32 = [s_in[0], s_misc[0], s_ffn[0], s_ffn[1], s_ffn[2], s_misc[1], s_misc[2]]
    from_chips = [chips_in, o6[1], o5[0], o5[1], o6[0], o6[2], o6[3]]
    return grad_x.reshape(bsz, s, D_MODEL), sums32, from_chips, small_out


BIG_NAMES = ("w_in", "w_out", "w_ffn_gate", "w_ffn_up", "w_ffn_down", "w_ple_proj", "w_ple_gate")
REST_NAMES = BIG_NAMES[1:]
TRANSPOSED = ("w_in", "w_ffn_gate", "w_ffn_up")
WEIGHT_ORDER = ("w_in", "ret_decay_fwd", "ret_decay_bwd", "ret_gn_gain", "attn_sink", "w_out", "ln1_gain",
                "ln1_bias", "w_ffn_gate", "w_ffn_up", "w_ffn_down", "w_ple_proj", "w_ple_gate", "ln2_gain", "ln2_bias")


def _shard_rows(name, a):
    return jnp.swapaxes(a[0], 0, 1) if name in TRANSPOSED else a[0]


def _unshard_rows(name, a):
    return (jnp.swapaxes(a, 0, 1) if name in TRANSPOSED else a)[None]


def _assemble_weights(gathered):
    cols = lambda a: a.transpose(1, 0, 2).reshape(a.shape[1], N_SHARD * a.shape[2])
    rows = lambda a: a.reshape(N_SHARD * a.shape[1], a.shape[2])
    same = lambda a: a
    layout = {"w_out": ("w_out", rows), "w_ffn_gate": ("gate4", same), "w_ffn_up": ("up4", same),
              "w_ffn_down": ("down4", same), "w_ple_proj": ("ple_proj", cols), "w_ple_gate": ("ple_gate", rows)}
    return {layout[n][0]: layout[n][1](a) for n, a in gathered.items()}


def kernel(x, p, w_in, ret_decay_fwd, ret_decay_bwd, ret_gn_gain, attn_sink, w_out, ln1_gain, ln1_bias, w_ffn_gate, w_ffn_up, w_ffn_down, w_ple_proj, w_ple_gate, ln2_gain, ln2_bias, loss_target, m_w_in, m_ret_decay_fwd, m_ret_decay_bwd, m_ret_gn_gain, m_attn_sink, m_w_out, m_ln1_gain, m_ln1_bias, m_w_ffn_gate, m_w_ffn_up, m_w_ffn_down, m_w_ple_proj, m_w_ple_gate, m_ln2_gain, m_ln2_bias, v_w_in, v_ret_decay_fwd, v_ret_decay_bwd, v_ret_gn_gain, v_attn_sink, v_w_out, v_ln1_gain, v_ln1_bias, v_w_ffn_gate, v_w_ffn_up, v_w_ffn_down, v_w_ple_proj, v_w_ple_gate, v_ln2_gain, v_ln2_bias):
    w = dict(w_in=w_in, ret_decay_fwd=ret_decay_fwd, ret_decay_bwd=ret_decay_bwd, ret_gn_gain=ret_gn_gain,
             attn_sink=attn_sink, w_out=w_out, ln1_gain=ln1_gain, ln1_bias=ln1_bias, w_ffn_gate=w_ffn_gate,
             w_ffn_up=w_ffn_up, w_ffn_down=w_ffn_down, w_ple_proj=w_ple_proj, w_ple_gate=w_ple_gate,
             ln2_gain=ln2_gain, ln2_bias=ln2_bias)
    m = dict(w_in=m_w_in, ret_decay_fwd=m_ret_decay_fwd, ret_decay_bwd=m_ret_decay_bwd, ret_gn_gain=m_ret_gn_gain,
             attn_sink=m_attn_sink, w_out=m_w_out, ln1_gain=m_ln1_gain, ln1_bias=m_ln1_bias, w_ffn_gate=m_w_ffn_gate,
             w_ffn_up=m_w_ffn_up, w_ffn_down=m_w_ffn_down, w_ple_proj=m_w_ple_proj, w_ple_gate=m_w_ple_gate,
             ln2_gain=m_ln2_gain, ln2_bias=m_ln2_bias)
    v = dict(w_in=v_w_in, ret_decay_fwd=v_ret_decay_fwd, ret_decay_bwd=v_ret_decay_bwd, ret_gn_gain=v_ret_gn_gain,
             attn_sink=v_attn_sink, w_out=v_w_out, ln1_gain=v_ln1_gain, ln1_bias=v_ln1_bias, w_ffn_gate=v_w_ffn_gate,
             w_ffn_up=v_w_ffn_up, w_ffn_down=v_w_ffn_down, w_ple_proj=v_w_ple_proj, w_ple_gate=v_w_ple_gate,
             ln2_gain=v_ln2_gain, ln2_bias=v_ln2_bias)
    big = lambda d: [_shard_rows(n, d[n]) for n in BIG_NAMES]
    small = lambda d: {n: d[n] for n in SMALL_NAMES}

    chip = 2 * lax.axis_index("x") + lax.axis_index("y")
    pos = jnp.stack([chip, lax.axis_index("c")]).astype(jnp.int32)

    shards = [a.astype(BF16) for a in big(w)]
    (w_in4,) = _all_gather_weights(shards[:1])
    w_in_t = w_in4.reshape(IN_WIDTH, D_MODEL)
    grad_x, sums32, from_chips, (g_s, d_s, m_s, v_s) = _local_step(
        x, p[0], loss_target, w_in_t, shards[1:], small(w), pos=pos,
        small_state=(_pack_small(small(w)), _pack_small(small(m)), _pack_small(small(v))))
    g_big = _join_halves(_add_chips(sums32, from_chips, pos))
    d_big, m_big, v_big = _adamw(big(w), g_big, big(m), big(v))
    loss = g_s[4, LOSS_COL] * (0.5 / D_MODEL)

    def tree(bigs, packed):
        out = {n: _unshard_rows(n, a) for n, a in zip(BIG_NAMES, bigs)}
        out.update(_unpack_small(packed))
        return [out[n] for n in WEIGHT_ORDER]

    return (loss, grad_x, *tree(g_big, g_s), *tree(d_big, d_s), *tree(m_big, m_s), *tree(v_big, v_s))
```

```python
import functools
import math

import jax
import jax.numpy as jnp
from jax import lax
from jax.experimental import pallas as pl
from jax.experimental.pallas import tpu as pltpu

F32 = jnp.float32
BF16 = jnp.bfloat16

D_MODEL = 1024
HEAD_DIM = 64
RET_HEADS = 8
ATTN_HEADS = 8
RET_WIDTH = 512
ATTN_WIDTH = 512
KV_WIDTH = 128
IN_WIDTH = 2816
FFN = 2816
N_SHARD = 4
FFN_SHARD = FFN // N_SHARD
PLE_DIM = 256
CHUNK = 128
LANES = 128
ALPHA = 2.0 ** 0.25
LN_EPS = 1e-5
GN_EPS = 1e-5
NEG_INF = -1e30
ADAM_LR = 0.001
ADAM_B1 = 0.9
ADAM_B2 = 0.999
ADAM_EPS = 1e-08
ADAM_WD = 0.01
ADAM_STEP = 10
VMEM_LIMIT = 56 * 1024 * 1024
MESH = pl.DeviceIdType.MESH

CB_RQ, CB_RK, CB_RV, CB_RG, CB_AQ, CB_AK, CB_AV = 0, 4, 8, 12, 16, 20, 21


def _dot(a, b):
    return jnp.dot(a, b, preferred_element_type=F32)


def _dot_nt(a, b):
    return lax.dot_general(a, b, (((1,), (1,)), ((), ())), preferred_element_type=F32)


def _dot_tn(a, b):
    return lax.dot_general(a, b, (((0,), (0,)), ((), ())), preferred_element_type=F32)


def _sigmoid(x):
    return 1.0 / (1.0 + jnp.exp(-x))


def _params(*sem, vmem=None):
    return pltpu.CompilerParams(dimension_semantics=tuple(sem) if sem else None, vmem_limit_bytes=vmem)


class _Rider:
    def __init__(self, ins, out_shapes, sems, start, finish, aliases=None):
        self.ins, self.out_shapes, self.sems = list(ins), list(out_shapes), list(sems)
        self.start, self.finish, self.aliases = start, finish, dict(aliases or {})


def _merge_riders(riders):
    riders = [r for r in riders if r is not None]
    if len(riders) == 1:
        return riders[0]
    bounds, aliases = [], {}
    i0 = o0 = s0 = 0
    for r in riders:
        bounds.append((i0, o0, s0))
        aliases.update({i0 + i: o0 + o for i, o in r.aliases.items()})
        i0, o0, s0 = i0 + len(r.ins), o0 + len(r.out_shapes), s0 + len(r.sems)

    def each(method):
        def run(ins, outs, sems):
            for r, (i, o, s) in zip(riders, bounds):
                getattr(r, method)(ins[i:i + len(r.ins)], outs[o:o + len(r.out_shapes)], sems[s:s + len(r.sems)])
        return run

    return _Rider([a for r in riders for a in r.ins], [a for r in riders for a in r.out_shapes],
                  [a for r in riders for a in r.sems], each("start"), each("finish"), aliases)


def _hosted_call(body, name, grid, in_specs, out_specs, out_shape, scratch_shapes, operands, rider=None,
                 semantics=None):
    n_in, n_out, n_scr = len(in_specs), len(out_specs), len(scratch_shapes)
    if rider is None:
        return pl.pallas_call(
            body, name=name, grid=grid, in_specs=in_specs, out_specs=out_specs, out_shape=out_shape,
            scratch_shapes=scratch_shapes,
            compiler_params=_params(*(semantics or ["parallel"] * len(grid)), vmem=VMEM_LIMIT))(*operands)
    r_in, r_out = len(rider.ins), len(rider.out_shapes)

    def full_body(*refs):
        main_in, rin = refs[:n_in], refs[n_in:n_in + r_in]
        o0 = n_in + r_in
        main_out, rout = refs[o0:o0 + n_out], refs[o0 + n_out:o0 + n_out + r_out]
        s0 = o0 + n_out + r_out
        main_scr, rsem = refs[s0:s0 + n_scr], refs[s0 + n_scr:]
        first = functools.reduce(jnp.logical_and, [pl.program_id(a) == 0 for a in range(len(grid))])
        last = functools.reduce(jnp.logical_and, [pl.program_id(a) == g - 1 for a, g in enumerate(grid)])

        @pl.when(first)
        def _():
            rider.start(rin, rout, rsem)

        body(*main_in, *main_out, *main_scr)

        @pl.when(last)
        def _():
            rider.finish(rin, rout, rsem)

    hbm = pl.BlockSpec(memory_space=pl.ANY)
    return pl.pallas_call(
        full_body, name=name, grid=grid,
        in_specs=list(in_specs) + [hbm] * r_in, out_specs=list(out_specs) + [hbm] * r_out,
        out_shape=list(out_shape) + rider.out_shapes,
        scratch_shapes=list(scratch_shapes) + rider.sems,
        input_output_aliases={n_in + i: n_out + o for i, o in rider.aliases.items()},
        compiler_params=_params(*(["arbitrary"] * len(grid)), vmem=VMEM_LIMIT),
    )(*operands, *rider.ins)


def _head_mean(x, m0):
    s0 = jnp.sum(jnp.where(m0, x, 0.0), axis=1, keepdims=True)
    s1 = jnp.sum(jnp.where(m0, 0.0, x), axis=1, keepdims=True)
    return jnp.where(m0, s0, s1) * (1.0 / HEAD_DIM)


def _inproj(x2d, w_in_t, rider=None):
    t = x2d.shape[0]
    tm = 512
    nb = 256

    def body(x_ref, w_ref, o_ref):
        xb = x_ref[...].astype(BF16)
        for n in range(0, IN_WIDTH, nb):
            o_ref[:, n:n + nb] = _dot_nt(xb, w_ref[n:n + nb, :]).astype(BF16)

    return _hosted_call(
        body, "inproj", (t // tm,),
        in_specs=[pl.BlockSpec((tm, D_MODEL), lambda i: (i, 0)),
                  pl.BlockSpec((IN_WIDTH, D_MODEL), lambda i: (0, 0))],
        out_specs=[pl.BlockSpec((tm, IN_WIDTH), lambda i: (i, 0))],
        out_shape=[jax.ShapeDtypeStruct((t, IN_WIDTH), BF16)],
        scratch_shapes=[], operands=(x2d, w_in_t), rider=rider)


def _outproj_ln1(y_ret, y_att, x2d, w_out, gain, bias, rider=None):
    t = x2d.shape[0]
    tm = 512

    def body(yr_ref, ya_ref, x_ref, w_ref, g_ref, b_ref, zh_ref, r_ref, hb_ref):
        mix = _dot(yr_ref[...], w_ref[0:RET_WIDTH, :]) + _dot(ya_ref[...], w_ref[RET_WIDTH:, :])
        z = ALPHA * x_ref[...] + mix
        mu = jnp.mean(z, axis=1, keepdims=True)
        zc = z - mu
        var = jnp.mean(zc * zc, axis=1, keepdims=True)
        r = lax.rsqrt(var + LN_EPS)
        zh = zc * r
        zh_ref[...] = zh
        r_ref[...] = r
        hb_ref[...] = (zh * g_ref[...] + b_ref[...]).astype(BF16)

    row = lambda w: pl.BlockSpec((tm, w), lambda i: (i, 0))
    const = lambda s: pl.BlockSpec(s, lambda i: (0, 0))
    return _hosted_call(
        body, "outproj_ln1", (t // tm,),
        in_specs=[row(RET_WIDTH), row(ATTN_WIDTH), row(D_MODEL), const((D_MODEL, D_MODEL)),
                  const((1, D_MODEL)), const((1, D_MODEL))],
        out_specs=[row(D_MODEL), row(1), row(D_MODEL)],
        out_shape=[jax.ShapeDtypeStruct((t, D_MODEL), F32), jax.ShapeDtypeStruct((t, 1), F32),
                   jax.ShapeDtypeStruct((t, D_MODEL), BF16)],
        scratch_shapes=[], operands=(y_ret, y_att, x2d, w_out, gain, bias), rider=rider)


def _load_resident(step, pairs, sems):
    copies = [pltpu.make_async_copy(src, dst, sems.at[i]) for i, (src, dst) in enumerate(pairs)]

    @pl.when(step == 0)
    def _():
        for cp in copies:
            cp.start()
        for cp in copies:
            cp.wait()


def _ffn_fwd(zh1, hb, p2d, tgt, g1, b1, g2, b2, wg4, wu4, wd4, wpe, wpg):
    t = zh1.shape[0]
    tm = 256

    def body(zh_ref, hb_ref, p_ref, t_ref, g1_ref, b1_ref, g2_ref, b2_ref,
             wg_hbm, wu_hbm, wd_hbm, wpe_hbm, wpg_hbm,
             dz_ref, dzb_ref, gs_ref, us_ref, act_ref, pg_ref, ple_ref, loss_ref, dg2_ref, db2_ref,
             wg, wu, wd, wpe, wpg, wsem):
        step = pl.program_id(0)
        loads = [(h.at[j], v.at[j]) for j in range(N_SHARD) for h, v in ((wg_hbm, wg), (wu_hbm, wu), (wd_hbm, wd))]
        _load_resident(step, loads + [(wpe_hbm, wpe), (wpg_hbm, wpg)], wsem)

        @pl.when(step == 0)
        def _():
            loss_ref[...] = jnp.zeros_like(loss_ref)
            dg2_ref[...] = jnp.zeros_like(dg2_ref)
            db2_ref[...] = jnp.zeros_like(db2_ref)

        h1 = zh_ref[...] * g1_ref[...] + b1_ref[...]
        hbv = hb_ref[...]
        ffn = jnp.zeros((tm, D_MODEL), F32)
        acts = []
        for j in range(N_SHARD + 1):
            if j < N_SHARD:
                gj = _dot_nt(hbv, wg[j])
                uj = _dot_nt(hbv, wu[j])
                gs_ref[j] = gj.astype(BF16)
                us_ref[j] = uj.astype(BF16)
                acts.append((gj * _sigmoid(gj) * uj).astype(BF16))
                act_ref[j] = acts[j]
            if j > 0:
                ffn = ffn + _dot(acts[j - 1], wd[j - 1])
        ple = _dot(p_ref[...].astype(BF16), wpe[...])
        pg = _sigmoid(_dot(hbv, wpg[...]))
        pg_ref[...] = pg.astype(BF16)
        ple_ref[...] = ple.astype(BF16)
        z2 = ALPHA * h1 + ffn + pg * ple
        mu = jnp.mean(z2, axis=1, keepdims=True)
        zc = z2 - mu
        var = jnp.mean(zc * zc, axis=1, keepdims=True)
        r = lax.rsqrt(var + LN_EPS)
        zh2 = zc * r
        err = zh2 * g2_ref[...] + b2_ref[...] - t_ref[...]
        loss_ref[...] += jnp.sum(err * err)
        dy = err * (1.0 / D_MODEL)
        dg2_ref[...] += jnp.sum(dy * zh2, axis=0, keepdims=True)
        db2_ref[...] += jnp.sum(dy, axis=0, keepdims=True)
        dzh = dy * g2_ref[...]
        m1 = jnp.mean(dzh, axis=1, keepdims=True)
        m2 = jnp.mean(dzh * zh2, axis=1, keepdims=True)
        dz2 = r * (dzh - m1 - zh2 * m2)
        dz_ref[...] = dz2
        dzb_ref[...] = dz2.astype(BF16)

    row = lambda w: pl.BlockSpec((tm, w), lambda i: (i, 0))
    const = lambda s: pl.BlockSpec(s, lambda i: (0, 0))
    sh = pl.BlockSpec((N_SHARD, tm, FFN_SHARD), lambda i: (0, i, 0))
    sh_shape = jax.ShapeDtypeStruct((N_SHARD, t, FFN_SHARD), BF16)
    hbm = pl.BlockSpec(memory_space=pl.ANY)
    return pl.pallas_call(
        body, name="ffn_fwd", grid=(t // tm,),
        in_specs=[row(D_MODEL), row(D_MODEL), row(PLE_DIM), row(D_MODEL),
                  const((1, D_MODEL)), const((1, D_MODEL)), const((1, D_MODEL)), const((1, D_MODEL)),
                  hbm, hbm, hbm, hbm, hbm],
        out_specs=[row(D_MODEL), row(D_MODEL), sh, sh, sh, row(D_MODEL), row(D_MODEL),
                   const((8, LANES)), const((1, D_MODEL)), const((1, D_MODEL))],
        out_shape=[jax.ShapeDtypeStruct((t, D_MODEL), F32), jax.ShapeDtypeStruct((t, D_MODEL), BF16),
                   sh_shape, sh_shape, sh_shape,
                   jax.ShapeDtypeStruct((t, D_MODEL), BF16), jax.ShapeDtypeStruct((t, D_MODEL), BF16),
                   jax.ShapeDtypeStruct((8, LANES), F32),
                   jax.ShapeDtypeStruct((1, D_MODEL), F32), jax.ShapeDtypeStruct((1, D_MODEL), F32)],
        scratch_shapes=[pltpu.VMEM(wg4.shape, BF16), pltpu.VMEM(wu4.shape, BF16), pltpu.VMEM(wd4.shape, BF16),
                        pltpu.VMEM(wpe.shape, BF16), pltpu.VMEM(wpg.shape, BF16),
                        pltpu.SemaphoreType.DMA((3 * N_SHARD + 2,))],
        compiler_params=_params("arbitrary", vmem=VMEM_LIMIT),
    )(zh1, hb, p2d, tgt, g1, b1, g2, b2, wg4, wu4, wd4, wpe, wpg)


def _ret_tables(lgf, lgb):
    c = CHUNK
    row = lax.broadcasted_iota(jnp.int32, (c, LANES), 0).astype(F32)
    ii = lax.broadcasted_iota(jnp.int32, (c, c), 0).astype(F32)
    jj = lax.broadcasted_iota(jnp.int32, (c, c), 1).astype(F32)
    diff = ii - jj
    dmats = []
    for h in range(2):
        lf = lgf[:, h * HEAD_DIM:h * HEAD_DIM + 1]
        lb = lgb[:, h * HEAD_DIM:h * HEAD_DIM + 1]
        dmats.append(jnp.where(diff > 0, jnp.exp(lf * jnp.maximum(diff, 0.0)),
                               jnp.where(diff < 0, jnp.exp(lb * jnp.maximum(-diff, 0.0)), 2.0)))
    tab = dict(
        qdec_f=jnp.exp(lgf * (row + 1.0)), kdec_f=jnp.exp(lgf * (c - 1.0 - row)),
        qdec_b=jnp.exp(lgb * (c - row)), kdec_b=jnp.exp(lgb * row),
        cdec_f=jnp.exp(lgf * c), cdec_b=jnp.exp(lgb * c),
        d0=dmats[0], d1=dmats[1], row=row, diff=diff)
    r = lax.broadcasted_iota(jnp.int32, (LANES, LANES), 0) < HEAD_DIM
    cc = lax.broadcasted_iota(jnp.int32, (LANES, LANES), 1) < HEAD_DIM
    tab["bd"] = r == cc
    tab["m0"] = lax.broadcasted_iota(jnp.int32, (c, LANES), 1) < HEAD_DIM
    return tab


def _ret_specs(bsz, s):
    blk = lambda cb: pl.BlockSpec((bsz, s, LANES), lambda p, cb=cb: (0, 0, cb + p))
    lane = pl.BlockSpec((None, 1, LANES), lambda p: (p, 0, 0))
    gain = pl.BlockSpec((1, LANES), lambda p: (0, p))
    pair = pl.BlockSpec((bsz, s, LANES), lambda p: (0, 0, p))
    return blk, lane, gain, pair


def _ret_kv_states(tb, k_ref, v_ref, rb_ref, kvf_ref, n_chunk):
    c = CHUNK
    bsz = k_ref.shape[0]
    bd = tb["bd"]

    def step(i, rbs):
        n = n_chunk - 1 - i
        sl = pl.ds(pl.multiple_of(n * c, c), c)
        kfb = []
        for b in range(bsz):
            k32 = k_ref[b, sl, :].astype(F32)
            kfb.append(jnp.concatenate([k32 * tb["kdec_f"], k32 * tb["kdec_b"]], axis=1).astype(BF16))
        kvs = [_dot_tn(kfb[b], v_ref[b, sl, :]) for b in range(bsz)]
        new = []
        for b in range(bsz):
            rb_ref[b, n] = rbs[b]
            kvf_ref[b, n] = jnp.where(bd, kvs[b][0:LANES], 0.0)
            new.append(rbs[b] * tb["cdec_b"] + jnp.where(bd, kvs[b][LANES:], 0.0))
        return tuple(new)

    lax.fori_loop(0, n_chunk, step, tuple(jnp.zeros((LANES, LANES), F32) for _ in range(bsz)))


def _split_rows(x, m0):
    return jnp.concatenate([jnp.where(m0, x, 0.0), jnp.where(m0, 0.0, x)], axis=0).astype(BF16)


def _ret_fwd(u3, lgf_l, lgb_l, gn_gain, rider=None):
    bsz, s, _ = u3.shape
    n_chunk = s // CHUNK
    c = CHUNK

    def body(q_ref, k_ref, v_ref, g_ref, lgf_ref, lgb_ref, gain_ref, y_ref, o_ref, rb_ref, kvf_ref):
        tb = _ret_tables(lgf_ref[...], lgb_ref[...])
        m0 = tb["m0"]
        gain = gain_ref[...]
        rows = range(bsz)
        _ret_kv_states(tb, k_ref, v_ref, rb_ref, kvf_ref, n_chunk)

        def chunk(n, rfs):
            sl = pl.ds(pl.multiple_of(n * c, c), c)
            qs = [q_ref[b, sl, :].astype(F32) * 0.125 for b in rows]
            s01 = [_dot_nt(_split_rows(qs[b], m0), k_ref[b, sl, :]) for b in rows]
            ys = []
            for b in rows:
                lhs = jnp.concatenate([s01[b][0:c] * tb["d0"], s01[b][c:] * tb["d1"],
                                       qs[b] * tb["qdec_f"], qs[b] * tb["qdec_b"]], axis=1).astype(BF16)
                rhs = jnp.concatenate([_split_rows(v_ref[b, sl, :].astype(F32), m0),
                                       rfs[b].astype(BF16), rb_ref[b, n].astype(BF16)], axis=0)
                ys.append(_dot(lhs, rhs))
            new = []
            for b in rows:
                y = ys[b]
                mu = _head_mean(y, m0)
                yc = y - mu
                var = _head_mean(yc * yc, m0)
                yh = yc * lax.rsqrt(var + GN_EPS)
                g = g_ref[b, sl, :].astype(F32)
                y_ref[b, sl, :] = y
                o_ref[b, sl, :] = (yh * gain * (g * _sigmoid(g))).astype(BF16)
                new.append(rfs[b] * tb["cdec_f"] + kvf_ref[b, n])
            return tuple(new)

        lax.fori_loop(0, n_chunk, chunk, tuple(jnp.zeros((LANES, LANES), F32) for _ in rows))

    blk, lane, gain, pair = _ret_specs(bsz, s)
    state = pltpu.VMEM((bsz, n_chunk, LANES, LANES), F32)
    return _hosted_call(
        body, "ret_fwd", (4,),
        in_specs=[blk(CB_RQ), blk(CB_RK), blk(CB_RV), blk(CB_RG), lane, lane, gain],
        out_specs=[pair, pair],
        out_shape=[jax.ShapeDtypeStruct((bsz, s, RET_WIDTH), F32), jax.ShapeDtypeStruct((bsz, s, RET_WIDTH), BF16)],
        scratch_shapes=[state, state],
        operands=(u3, u3, u3, u3, lgf_l, lgb_l, gn_gain), rider=rider)


def _ret_bwd(u3, y_pre, d_o, lgf_l, lgb_l, gn_gain, rider=None):
    bsz, s, _ = u3.shape
    n_chunk = s // CHUNK
    c = CHUNK

    def body(q_ref, k_ref, v_ref, g_ref, y_ref, do_ref, lgf_ref, lgb_ref, gain_ref,
             dq_ref, dk_ref, dv_ref, dg_ref, part_ref,
             rb_ref, kvf_ref, rf_ref, dirf_ref, dy_ref, dk_acc, dv_acc, pa0, pa1, vec_ref):
        tb = _ret_tables(lgf_ref[...], lgb_ref[...])
        m0, bd, row = tb["m0"], tb["bd"], tb["row"]
        gain = gain_ref[...]
        wf = jnp.maximum(tb["diff"], 0.0)
        wb = jnp.maximum(-tb["diff"], 0.0)
        rows = range(bsz)
        zero_states = tuple(jnp.zeros((LANES, LANES), F32) for _ in rows)
        for ref in (pa0, pa1):
            ref[...] = jnp.zeros_like(ref)
        vec_ref[...] = jnp.zeros_like(vec_ref)
        _ret_kv_states(tb, k_ref, v_ref, rb_ref, kvf_ref, n_chunk)

        def sweep_fwd(n, carry):
            rfs, gbs = carry
            sl = pl.ds(pl.multiple_of(n * c, c), c)
            qs, ks, vs, dys, dybs, q01, k01, dy01 = [], [], [], [], [], [], [], []
            dgain = jnp.zeros((1, LANES), F32)
            for b in rows:
                q = q_ref[b, sl, :].astype(F32) * 0.125
                k = k_ref[b, sl, :]
                y = y_ref[b, sl, :]
                do = do_ref[b, sl, :].astype(F32)
                g = g_ref[b, sl, :].astype(F32)
                mu = _head_mean(y, m0)
                yc = y - mu
                rstd = lax.rsqrt(_head_mean(yc * yc, m0) + GN_EPS)
                yh = yc * rstd
                sg = _sigmoid(g)
                sil = g * sg
                dyh = do * gain * sil
                dg_ref[b, sl, :] = (do * yh * gain * sg * (1.0 + g * (1.0 - sg))).astype(BF16)
                dgain = dgain + jnp.sum(do * yh * sil, axis=0, keepdims=True)
                dy = rstd * (dyh - _head_mean(dyh, m0) - yh * _head_mean(dyh * yh, m0))
                dyb = dy.astype(BF16)
                dy_ref[b, sl, :] = dyb
                rf_ref[b, n] = rfs[b]
                qs.append(q)
                ks.append(k)
                vs.append(v_ref[b, sl, :])
                dys.append(dy)
                dybs.append(dyb)
                q01.append(_split_rows(q, m0))
                k01.append(_split_rows(k.astype(F32), m0))
                dy01.append(_split_rows(dy, m0))
            s01 = [_dot_nt(q01[b], ks[b]) for b in rows]
            da01 = [_dot_nt(dy01[b], vs[b]) for b in rows]
            rbn = [rb_ref[b, n] for b in rows]
            states = [jnp.concatenate([rfs[b], rbn[b]], axis=0).astype(BF16) for b in rows]
            dqc = [_dot_nt(dybs[b], states[b]) for b in rows]
            gbb = [gbs[b].astype(BF16) for b in rows]
            dkb = [_dot_nt(vs[b], gbb[b]) for b in rows]
            qfb = [jnp.concatenate([qs[b] * tb["qdec_f"], qs[b] * tb["qdec_b"]], axis=1) for b in rows]
            direct = [_dot_tn(qfb[b].astype(BF16), dybs[b]) for b in rows]
            ds_cat, ds_rows, a_rows = [], [], []
            for b in rows:
                a0 = s01[b][0:c] * tb["d0"]
                a1 = s01[b][c:] * tb["d1"]
                pa0[...] += da01[b][0:c] * a0
                pa1[...] += da01[b][c:] * a1
                ds0 = da01[b][0:c] * tb["d0"]
                ds1 = da01[b][c:] * tb["d1"]
                ds_cat.append(jnp.concatenate([ds0, ds1], axis=1).astype(BF16))
                ds_rows.append(jnp.concatenate([ds0, ds1], axis=0).astype(BF16))
                a_rows.append(jnp.concatenate([a0, a1], axis=0).astype(BF16))
            kbd = [ks[b].astype(F32) * tb["kdec_b"] for b in rows]
            dq_in = [_dot(ds_cat[b], k01[b]) for b in rows]
            dk_in = [_dot_tn(ds_rows[b], q01[b]) for b in rows]
            dv_in = [_dot_tn(a_rows[b], dy01[b]) for b in rows]
            dv_gb = [_dot(kbd[b].astype(BF16), gbb[b]) for b in rows]
            new_rf, new_gb = [], []
            dlf = jnp.zeros((1, LANES), F32)
            dlb = jnp.zeros((1, LANES), F32)
            for b in rows:
                dqf, dqb = dqc[b][:, 0:LANES], dqc[b][:, LANES:]
                qf, qb = qfb[b][:, 0:LANES], qfb[b][:, LANES:]
                dq = dq_in[b] + dqf * tb["qdec_f"] + dqb * tb["qdec_b"]
                dq_ref[b, sl, :] = (dq * 0.125).astype(BF16)
                dk_acc[b, sl, :] = dk_in[b] + dkb[b] * tb["kdec_b"]
                dv_acc[b, sl, :] = dv_in[b] + dv_gb[b]
                dlf = dlf + jnp.sum((row + 1.0) * qf * dqf, axis=0, keepdims=True)
                dlb = dlb + jnp.sum((c - row) * qb * dqb + row * kbd[b] * dkb[b], axis=0, keepdims=True)
                dlb = dlb + c * tb["cdec_b"] * jnp.sum(gbs[b] * rbn[b], axis=0, keepdims=True)
                dirf_ref[b, n] = jnp.where(bd, direct[b][0:LANES], 0.0)
                new_gb.append(jnp.where(bd, direct[b][LANES:], 0.0) + tb["cdec_b"] * gbs[b])
                new_rf.append(rfs[b] * tb["cdec_f"] + kvf_ref[b, n])
            vec_ref[0:1, :] += dlf
            vec_ref[1:2, :] += dlb
            vec_ref[6:7, :] += dgain
            return tuple(new_rf), tuple(new_gb)

        lax.fori_loop(0, n_chunk, sweep_fwd, (zero_states, zero_states))

        def sweep_bwd(i, gfs):
            n = n_chunk - 1 - i
            sl = pl.ds(pl.multiple_of(n * c, c), c)
            gfb = [gfs[b].astype(BF16) for b in rows]
            kfd = [k_ref[b, sl, :].astype(F32) * tb["kdec_f"] for b in rows]
            dkf = [_dot_nt(v_ref[b, sl, :], gfb[b]) for b in rows]
            dvf = [_dot(kfd[b].astype(BF16), gfb[b]) for b in rows]
            new = []
            dlf = jnp.zeros((1, LANES), F32)
            for b in rows:
                dk_ref[b, sl, :] = (dk_acc[b, sl, :] + dkf[b] * tb["kdec_f"]).astype(BF16)
                dv_ref[b, sl, :] = (dv_acc[b, sl, :] + dvf[b]).astype(BF16)
                dlf = dlf + jnp.sum((c - 1.0 - row) * kfd[b] * dkf[b], axis=0, keepdims=True)
                dlf = dlf + c * tb["cdec_f"] * jnp.sum(gfs[b] * rf_ref[b, n], axis=0, keepdims=True)
                new.append(dirf_ref[b, n] + tb["cdec_f"] * gfs[b])
            vec_ref[0:1, :] += dlf
            return tuple(new)

        lax.fori_loop(0, n_chunk, sweep_bwd, zero_states)
        vec_ref[2:3, :] = jnp.sum(pa0[...] * wf, axis=0, keepdims=True)
        vec_ref[3:4, :] = jnp.sum(pa1[...] * wf, axis=0, keepdims=True)
        vec_ref[4:5, :] = jnp.sum(pa0[...] * wb, axis=0, keepdims=True)
        vec_ref[5:6, :] = jnp.sum(pa1[...] * wb, axis=0, keepdims=True)
        part_ref[...] = vec_ref[...]

    blk, lane, gain, pair = _ret_specs(bsz, s)
    out_bf = jax.ShapeDtypeStruct((bsz, s, RET_WIDTH), BF16)
    state = pltpu.VMEM((bsz, n_chunk, LANES, LANES), F32)
    return _hosted_call(
        body, "ret_bwd", (4,),
        in_specs=[blk(CB_RQ), blk(CB_RK), blk(CB_RV), blk(CB_RG), pair, pair, lane, lane, gain],
        out_specs=[pair, pair, pair, pair, pl.BlockSpec((None, 8, LANES), lambda p: (p, 0, 0))],
        out_shape=[out_bf, out_bf, out_bf, out_bf, jax.ShapeDtypeStruct((4, 8, LANES), F32)],
        scratch_shapes=[state, state, state, state,
                        pltpu.VMEM((bsz, s, LANES), BF16), pltpu.VMEM((bsz, s, LANES), F32),
                        pltpu.VMEM((bsz, s, LANES), F32),
                        pltpu.VMEM((c, c), F32), pltpu.VMEM((c, c), F32), pltpu.VMEM((8, LANES), F32)],
        operands=(u3, u3, u3, u3, y_pre, d_o, lgf_l, lgb_l, gn_gain), rider=rider)


def _attn_window_tables(n, s):
    qi = lax.broadcasted_iota(jnp.int32, (CHUNK, 3 * CHUNK), 0)
    kj = lax.broadcasted_iota(jnp.int32, (CHUNK, 3 * CHUNK), 1)
    dist = jnp.abs(kj - CHUNK - qi)
    kpos = n * CHUNK - CHUNK + kj
    valid = (dist <= CHUNK) & (kpos >= 0) & (kpos < s)
    return dist.astype(F32), valid


def _dup_kv_head(x, g):
    lane = lax.broadcasted_iota(jnp.int32, x.shape, 1)
    keep = (lane < HEAD_DIM) == (g == 0)
    xf = x.astype(F32)
    return jnp.where(keep, xf, pltpu.roll(xf, HEAD_DIM, 1))


def _attn_specs(s):
    q = pl.BlockSpec((None, s, 2 * LANES), lambda b, g: (b, 0, CB_AQ // 2 + g))
    k = pl.BlockSpec((None, s, LANES), lambda b, g: (b, 0, CB_AK))
    v = pl.BlockSpec((None, s, LANES), lambda b, g: (b, 0, CB_AV))
    grp = pl.BlockSpec((None, s, 2 * LANES), lambda b, g: (b, 0, g))
    smem = pl.BlockSpec(memory_space=pltpu.SMEM)
    return q, k, v, grp, smem


def _fill_padded(dst_ref, val, s):
    dst_ref[0:CHUNK, :] = jnp.zeros((CHUNK, LANES), dst_ref.dtype)
    dst_ref[CHUNK:CHUNK + s, :] = val.astype(dst_ref.dtype)
    dst_ref[CHUNK + s:2 * CHUNK + s, :] = jnp.zeros((CHUNK, LANES), dst_ref.dtype)


def _attn_probs(sc, slope, snk, dist, valid):
    sc = jnp.where(valid, sc - slope * dist, NEG_INF)
    m = jnp.maximum(jnp.max(sc, axis=1, keepdims=True), snk)
    e = jnp.exp(sc - m)
    es = jnp.exp(snk - m)
    inv = 1.0 / (jnp.sum(e, axis=1, keepdims=True) + es)
    return e * inv, es * inv


def _stack_heads(x2, m0):
    parts = []
    for pr in range(2):
        xp = x2[:, pr * LANES:(pr + 1) * LANES]
        parts += [jnp.where(m0, xp, 0.0), jnp.where(m0, 0.0, xp)]
    return jnp.concatenate(parts, axis=0).astype(BF16)


def _unstack_pair(x_all, pr, m0):
    return jnp.where(m0, x_all[(2 * pr) * CHUNK:(2 * pr + 1) * CHUNK], x_all[(2 * pr + 1) * CHUNK:(2 * pr + 2) * CHUNK])


def _attn_saved_specs(bsz, n_blk):
    specs = [pl.BlockSpec((None, None, n_blk, 4 * CHUNK, w), lambda b, g: (b, g, 0, 0, 0)) for w in (3 * CHUNK, 1)]
    shapes = [jax.ShapeDtypeStruct((bsz, 2, n_blk, 4 * CHUNK, 3 * CHUNK), BF16),
              jax.ShapeDtypeStruct((bsz, 2, n_blk, 4 * CHUNK, 1), F32)]
    return specs, shapes


def _attn_fwd(u3, slopes, sink, rider=None):
    bsz, s, _ = u3.shape
    n_blk = s // CHUNK

    def body(slope_ref, sink_ref, q_ref, k_ref, v_ref, o_ref, p_ref, ps_ref, kp_ref, vp_ref):
        g = pl.program_id(1)
        _fill_padded(kp_ref, _dup_kv_head(k_ref[...], g), s)
        _fill_padded(vp_ref, _dup_kv_head(v_ref[...], g), s)
        m0 = lax.broadcasted_iota(jnp.int32, (CHUNK, LANES), 1) < HEAD_DIM

        def blk(n, carry):
            r0 = pl.multiple_of(n * CHUNK, CHUNK)
            kw = kp_ref[pl.ds(r0, 3 * CHUNK), :]
            vw = vp_ref[pl.ds(r0, 3 * CHUNK), :]
            dist, valid = _attn_window_tables(n, s)
            q_all = _stack_heads(q_ref[pl.ds(r0, CHUNK), :].astype(F32) * 0.125, m0)
            sc_all = _dot_nt(q_all, kw)
            probs, sinks = [], []
            for i in range(4):
                p, ps = _attn_probs(sc_all[i * CHUNK:(i + 1) * CHUNK], slope_ref[g * 4 + i], sink_ref[g * 4 + i],
                                    dist, valid)
                probs.append(p.astype(BF16))
                sinks.append(ps)
            p_all = jnp.concatenate(probs, axis=0)
            p_ref[n] = p_all
            ps_ref[n] = jnp.concatenate(sinks, axis=0)
            out_all = _dot(p_all, vw)
            for pr in range(2):
                o_ref[pl.ds(r0, CHUNK), pr * LANES:(pr + 1) * LANES] = _unstack_pair(out_all, pr, m0).astype(BF16)
            return carry

        lax.fori_loop(0, n_blk, blk, 0)

    q, k, v, grp, smem = _attn_specs(s)
    saved_specs, saved_shapes = _attn_saved_specs(bsz, n_blk)
    return _hosted_call(
        body, "attn_fwd", (bsz, 2),
        in_specs=[smem, smem, q, k, v],
        out_specs=[grp] + saved_specs,
        out_shape=[jax.ShapeDtypeStruct((bsz, s, ATTN_WIDTH), BF16)] + saved_shapes,
        scratch_shapes=[pltpu.VMEM((s + 2 * CHUNK, LANES), BF16), pltpu.VMEM((s + 2 * CHUNK, LANES), BF16)],
        operands=(slopes, sink, u3, u3, u3), rider=rider)


def _attn_bwd(u3, d_o, probs, sink_probs, rider=None):
    bsz, s, _ = u3.shape
    n_blk = s // CHUNK

    def body(q_ref, k_ref, v_ref, do_ref, p_ref, ps_ref, dq_ref, dkv_ref, ds_ref,
             kp_ref, vp_ref, dk_acc, dv_acc):
        g = pl.program_id(1)
        _fill_padded(kp_ref, _dup_kv_head(k_ref[...], g), s)
        _fill_padded(vp_ref, _dup_kv_head(v_ref[...], g), s)
        dk_acc[...] = jnp.zeros_like(dk_acc)
        dv_acc[...] = jnp.zeros_like(dv_acc)
        m0 = lax.broadcasted_iota(jnp.int32, (CHUNK, LANES), 1) < HEAD_DIM

        def blk(n, dsink):
            r0 = pl.multiple_of(n * CHUNK, CHUNK)
            win = pl.ds(r0, 3 * CHUNK)
            kw = kp_ref[win, :]
            vw = vp_ref[win, :]
            q_all = _stack_heads(q_ref[pl.ds(r0, CHUNK), :].astype(F32) * 0.125, m0)
            do_all = _stack_heads(do_ref[pl.ds(r0, CHUNK), :].astype(F32), m0)
            p_all = p_ref[n]
            ps_all = ps_ref[n]
            dp_all = _dot_nt(do_all, vw)
            new_dsink, dscs = [], []
            for i in range(4):
                rows = slice(i * CHUNK, (i + 1) * CHUNK)
                p = p_all[rows].astype(F32)
                dp = dp_all[rows]
                delta = jnp.sum(p * dp, axis=1, keepdims=True)
                dscs.append((p * (dp - delta)).astype(BF16))
                dsh = jnp.sum(ps_all[rows] * delta, axis=0, keepdims=True)
                new_dsink.append(dsink[i] - jnp.broadcast_to(dsh, (1, LANES)))
            dsc_all = jnp.concatenate(dscs, axis=0)
            dq_all = _dot(dsc_all, kw)
            dk_acc[win, :] += _dot_tn(dsc_all, q_all)
            dv_acc[win, :] += _dot_tn(p_all, do_all)
            for pr in range(2):
                dq_ref[pl.ds(r0, CHUNK), pr * LANES:(pr + 1) * LANES] = (
                    _unstack_pair(dq_all, pr, m0) * 0.125).astype(BF16)
            return tuple(new_dsink)

        dsink = lax.fori_loop(0, n_blk, blk, tuple(jnp.zeros((1, LANES), F32) for _ in range(4)))
        dk = dk_acc[CHUNK:CHUNK + s, :]
        dv = dv_acc[CHUNK:CHUNK + s, :]
        lane = lax.broadcasted_iota(jnp.int32, (s, LANES), 1)
        fold = lambda a: a + pltpu.roll(a, HEAD_DIM, 1)
        dkv_ref[...] = jnp.where(lane < HEAD_DIM, fold(dk), fold(dv)).astype(BF16)
        ds_ref[...] = jnp.zeros_like(ds_ref)
        for i in range(4):
            ds_ref[i:i + 1, :] = dsink[i]

    q, k, v, grp, _ = _attn_specs(s)
    return _hosted_call(
        body, "attn_bwd", (bsz, 2),
        in_specs=[q, k, v, grp] + _attn_saved_specs(bsz, n_blk)[0],
        out_specs=[grp, pl.BlockSpec((None, s, LANES), lambda b, g: (b, 0, g)),
                   pl.BlockSpec((None, None, 8, LANES), lambda b, g: (b, g, 0, 0))],
        out_shape=[jax.ShapeDtypeStruct((bsz, s, ATTN_WIDTH), BF16), jax.ShapeDtypeStruct((bsz, s, 2 * LANES), BF16),
                   jax.ShapeDtypeStruct((bsz, 2, 8, LANES), F32)],
        scratch_shapes=[pltpu.VMEM((s + 2 * CHUNK, LANES), BF16), pltpu.VMEM((s + 2 * CHUNK, LANES), BF16),
                        pltpu.VMEM((s + 2 * CHUNK, LANES), F32), pltpu.VMEM((s + 2 * CHUNK, LANES), F32)],
        operands=(u3, u3, u3, d_o, probs, sink_probs), rider=rider)


def _ffn_bwd(dz2, gs, us, pg, ple, zh1, r1, g1, wg4, wu4, wd4, wpg, w_out):
    t = dz2.shape[0]
    tm = 256

    def body(dz_ref, gs_ref, us_ref, pg_ref, ple_ref, zh_ref, r_ref, g1_ref,
             wg_hbm, wu_hbm, wd_hbm, wpg_hbm, wo_hbm,
             dgs_ref, dus_ref, dsp_ref, dple_ref, dz1_ref, dyr_ref, dya_ref, dg1_ref, db1_ref,
             wg, wu, wd, wpg, wo, wsem):
        step = pl.program_id(0)
        loads = [(h.at[j], v.at[j]) for j in range(N_SHARD) for h, v in ((wd_hbm, wd), (wg_hbm, wg), (wu_hbm, wu))]
        _load_resident(step, loads + [(wpg_hbm, wpg), (wo_hbm, wo)], wsem)

        @pl.when(step == 0)
        def _():
            dg1_ref[...] = jnp.zeros_like(dg1_ref)
            db1_ref[...] = jnp.zeros_like(db1_ref)

        dz = dz_ref[...]
        dzb = dz.astype(BF16)
        dh = ALPHA * dz
        pending = []
        for j in range(N_SHARD + 1):
            if j < N_SHARD:
                da = _dot_nt(dzb, wd[j])
                gj = gs_ref[j].astype(F32)
                uj = us_ref[j].astype(F32)
                sg = _sigmoid(gj)
                dgj = (da * uj * sg * (1.0 + gj * (1.0 - sg))).astype(BF16)
                duj = (da * gj * sg).astype(BF16)
                dgs_ref[j] = dgj
                dus_ref[j] = duj
                pending.append((dgj, duj))
            if j > 0:
                dgp, dup = pending[j - 1]
                dh = dh + _dot(dgp, wg[j - 1]) + _dot(dup, wu[j - 1])
        pgv = pg_ref[...].astype(F32)
        plev = ple_ref[...].astype(F32)
        dple_ref[...] = (dz * pgv).astype(BF16)
        dsp = (dz * plev * pgv * (1.0 - pgv)).astype(BF16)
        dsp_ref[...] = dsp
        dh = dh + _dot_nt(dsp, wpg[...])
        zh = zh_ref[...]
        dg1_ref[...] += jnp.sum(dh * zh, axis=0, keepdims=True)
        db1_ref[...] += jnp.sum(dh, axis=0, keepdims=True)
        dzh = dh * g1_ref[...]
        m1 = jnp.mean(dzh, axis=1, keepdims=True)
        m2 = jnp.mean(dzh * zh, axis=1, keepdims=True)
        dz1 = r_ref[...] * (dzh - m1 - zh * m2)
        dz1_ref[...] = dz1
        dyc = _dot_nt(dz1.astype(BF16), wo[...])
        dyr_ref[...] = dyc[:, 0:RET_WIDTH].astype(BF16)
        dya_ref[...] = dyc[:, RET_WIDTH:].astype(BF16)

    row = lambda w: pl.BlockSpec((tm, w), lambda i: (i, 0))
    const = lambda s: pl.BlockSpec(s, lambda i: (0, 0))
    sh = pl.BlockSpec((N_SHARD, tm, FFN_SHARD), lambda i: (0, i, 0))
    hbm = pl.BlockSpec(memory_space=pl.ANY)
    sh_shape = jax.ShapeDtypeStruct((N_SHARD, t, FFN_SHARD), BF16)
    return pl.pallas_call(
        body, name="ffn_bwd", grid=(t // tm,),
        in_specs=[row(D_MODEL), sh, sh, row(D_MODEL), row(D_MODEL), row(D_MODEL), row(1), const((1, D_MODEL)),
                  hbm, hbm, hbm, hbm, hbm],
        out_specs=[sh, sh, row(D_MODEL), row(D_MODEL), row(D_MODEL), row(RET_WIDTH), row(ATTN_WIDTH),
                   const((1, D_MODEL)), const((1, D_MODEL))],
        out_shape=[sh_shape, sh_shape, jax.ShapeDtypeStruct((t, D_MODEL), BF16),
                   jax.ShapeDtypeStruct((t, D_MODEL), BF16), jax.ShapeDtypeStruct((t, D_MODEL), F32),
                   jax.ShapeDtypeStruct((t, RET_WIDTH), BF16), jax.ShapeDtypeStruct((t, ATTN_WIDTH), BF16),
                   jax.ShapeDtypeStruct((1, D_MODEL), F32), jax.ShapeDtypeStruct((1, D_MODEL), F32)],
        scratch_shapes=[pltpu.VMEM(wg4.shape, BF16), pltpu.VMEM(wu4.shape, BF16), pltpu.VMEM(wd4.shape, BF16),
                        pltpu.VMEM(wpg.shape, BF16), pltpu.VMEM(w_out.shape, BF16),
                        pltpu.SemaphoreType.DMA((3 * N_SHARD + 2,))],
        compiler_params=_params("arbitrary", vmem=VMEM_LIMIT),
    )(dz2, gs, us, pg, ple, zh1, r1, g1, wg4, wu4, wd4, wpg, w_out)


def _wgrad_misc(y_ret, y_att, dz1, hb, dsp, p2d, dple, rider=None):
    t = dz1.shape[0]
    tk = min(t, 512)

    def body(yr_ref, ya_ref, dz_ref, hb_ref, dsp_ref, p_ref, dple_ref, wo_ref, wpg_ref, wpe_ref):
        @pl.when(pl.program_id(0) == 0)
        def _():
            wo_ref[...] = jnp.zeros_like(wo_ref)
            wpg_ref[...] = jnp.zeros_like(wpg_ref)
            wpe_ref[...] = jnp.zeros_like(wpe_ref)

        dzb = dz_ref[...].astype(BF16)
        wo_ref[0:RET_WIDTH, :] += _dot_tn(yr_ref[...], dzb)
        wo_ref[RET_WIDTH:, :] += _dot_tn(ya_ref[...], dzb)
        wpg_ref[...] += _dot_tn(hb_ref[...], dsp_ref[...])
        wpe_ref[...] += _dot_tn(p_ref[...].astype(BF16), dple_ref[...])

    row = lambda w: pl.BlockSpec((tk, w), lambda k: (k, 0))
    const = lambda s: pl.BlockSpec(s, lambda k: (0, 0))
    return _hosted_call(
        body, "wgrad_misc", (t // tk,),
        in_specs=[row(RET_WIDTH), row(ATTN_WIDTH), row(D_MODEL), row(D_MODEL), row(D_MODEL), row(PLE_DIM),
                  row(D_MODEL)],
        out_specs=[const((D_MODEL, D_MODEL)), const((D_MODEL, D_MODEL)), const((PLE_DIM, D_MODEL))],
        out_shape=[jax.ShapeDtypeStruct((D_MODEL, D_MODEL), F32), jax.ShapeDtypeStruct((D_MODEL, D_MODEL), F32),
                   jax.ShapeDtypeStruct((PLE_DIM, D_MODEL), F32)],
        scratch_shapes=[], operands=(y_ret, y_att, dz1, hb, dsp, p2d, dple), rider=rider, semantics=["arbitrary"])


def _wgrad_ffn(acts, dgs, dus, hb, dz2b):
    t = dz2b.shape[0]
    tk = min(t, 512)

    def body(act_ref, dg_ref, du_ref, hb_ref, dz_ref, og_ref, ou_ref, od_ref):
        @pl.when(pl.program_id(1) == 0)
        def _():
            og_ref[...] = jnp.zeros_like(og_ref)
            ou_ref[...] = jnp.zeros_like(ou_ref)
            od_ref[...] = jnp.zeros_like(od_ref)

        hbv = hb_ref[...]
        og_ref[...] += _dot_tn(dg_ref[...], hbv)
        ou_ref[...] += _dot_tn(du_ref[...], hbv)
        od_ref[...] += _dot_tn(act_ref[...], dz_ref[...])

    a_spec = pl.BlockSpec((None, tk, FFN_SHARD), lambda j, k: (j, k, 0))
    b_spec = pl.BlockSpec((tk, D_MODEL), lambda j, k: (k, 0))
    o_spec = pl.BlockSpec((None, FFN_SHARD, D_MODEL), lambda j, k: (j, 0, 0))
    o_shape = jax.ShapeDtypeStruct((N_SHARD, FFN_SHARD, D_MODEL), F32)
    return pl.pallas_call(
        body, name="wgrad_ffn", grid=(N_SHARD, t // tk),
        in_specs=[a_spec, a_spec, a_spec, b_spec, b_spec],
        out_specs=[o_spec, o_spec, o_spec], out_shape=[o_shape, o_shape, o_shape],
        compiler_params=_params("parallel", "arbitrary", vmem=VMEM_LIMIT),
    )(acts, dgs, dus, hb, dz2b)


KV_ORDER = (0, 128, 64, 192)


def _wgrad_in(pieces, x2d):
    t = x2d.shape[0]
    tk = min(t, 512)
    kv0 = CB_AK * LANES

    def body(p0, p1, p2, p3, p4, pkv, x_ref, o_ref):
        @pl.when(pl.program_id(0) == 0)
        def _():
            o_ref[...] = jnp.zeros_like(o_ref)

        xb = x_ref[...].astype(BF16)
        for i, ref in enumerate((p0, p1, p2, p3, p4)):
            o_ref[i * 512:(i + 1) * 512, :] += _dot_tn(ref[...], xb)
        dkv = _dot_tn(pkv[...], xb)
        for i, o in enumerate(KV_ORDER):
            o_ref[kv0 + o:kv0 + o + HEAD_DIM, :] += dkv[i * HEAD_DIM:(i + 1) * HEAD_DIM]

    row = lambda w: pl.BlockSpec((tk, w), lambda k: (k, 0))
    return pl.pallas_call(
        body, name="wgrad_in", grid=(t // tk,),
        in_specs=[row(512)] * 5 + [row(256), row(D_MODEL)],
        out_specs=pl.BlockSpec((IN_WIDTH, D_MODEL), lambda k: (0, 0)),
        out_shape=jax.ShapeDtypeStruct((IN_WIDTH, D_MODEL), F32),
        compiler_params=_params("arbitrary", vmem=VMEM_LIMIT),
    )(*pieces, x2d)


def _inproj_bwd(dz1, pieces, w_main, w_kv, rider=None):
    t = dz1.shape[0]
    tm = 512

    def body(dz_ref, p0, p1, p2, p3, p4, pkv, wm_ref, wkv_ref, o_ref):
        acc = ALPHA * dz_ref[...]
        for i, ref in enumerate((p0, p1, p2, p3, p4)):
            acc = acc + _dot(ref[...], wm_ref[i * 512:(i + 1) * 512, :])
        o_ref[...] = acc + _dot(pkv[...], wkv_ref[...])

    row = lambda w: pl.BlockSpec((tm, w), lambda i: (i, 0))
    const = lambda s: pl.BlockSpec(s, lambda i: (0, 0))
    return _hosted_call(
        body, "inproj_bwd", (t // tm,),
        in_specs=[row(D_MODEL)] + [row(512)] * 5 + [row(256), const(w_main.shape), const(w_kv.shape)],
        out_specs=[row(D_MODEL)],
        out_shape=[jax.ShapeDtypeStruct((t, D_MODEL), F32)],
        scratch_shapes=[], operands=(dz1, *pieces, w_main, w_kv), rider=rider)


def _coords():
    return lax.axis_index("x"), lax.axis_index("y"), lax.axis_index("c")


def _chip_of(x, y, rel):
    return (1 - x if rel & 2 else x), (1 - y if rel & 1 else y)


def _all_gather_weights(shards):
    first = _gather_chips_rider(shards)
    second = _gather_pass_rider([jax.ShapeDtypeStruct((N_SHARD,) + s.shape, s.dtype) for s in shards], chained=True)
    return _run_riders("gather_weights", shards, first.out_shapes, [first, second])


def _run_riders(name, ins, out_shapes, riders):
    n_in, n_out = len(ins), len(out_shapes)

    def body(*refs):
        in_refs, out_refs = refs[:n_in], refs[n_in:n_in + n_out]
        k = n_in + n_out
        for r in riders:
            sems = refs[k:k + len(r.sems)]
            k += len(r.sems)
            r.start(in_refs, out_refs, sems)
            r.finish(in_refs, out_refs, sems)

    hbm = pl.BlockSpec(memory_space=pl.ANY)
    return pl.pallas_call(
        body, name=name, in_specs=[hbm] * n_in, out_specs=[hbm] * n_out, out_shape=list(out_shapes),
        scratch_shapes=[s for r in riders for s in r.sems],
    )(*ins)


def _gather_half(outs, w, chip, cc):
    h = outs[w].shape[1] // 2
    return outs[w].at[chip, pl.ds(cc * h, h), :]


def _gather_chips_rider(shards):
    nw = len(shards)

    def copies(ins, outs, sems, arrivals):
        send, recv, lsend, lrecv = sems
        x, y, c = _coords()
        me = 2 * x + y
        own = [pltpu.make_async_remote_copy(
            src_ref=ins[w], dst_ref=outs[w].at[me], send_sem=lsend.at[w], recv_sem=lrecv.at[w],
            device_id=(x, y, 1 - c), device_id_type=MESH) for w in range(nw)]
        out, arrive = [], []
        for rel in (1, 2, 3):
            kx, ky = _chip_of(x, y, rel)
            for w in range(nw):
                h = shards[w].shape[0] // 2
                sem = dict(send_sem=send.at[w * 3 + rel - 1], recv_sem=recv.at[w * 3 + rel - 1],
                           device_id=(kx, ky, c), device_id_type=MESH)
                out.append(pltpu.make_async_remote_copy(
                    src_ref=ins[w].at[pl.ds(c * h, h), :], dst_ref=_gather_half(outs, w, me, c), **sem))
                if arrivals:
                    theirs = _gather_half(outs, w, 2 * kx + ky, c)
                    arrive.append(pltpu.make_async_remote_copy(src_ref=theirs, dst_ref=theirs, **sem))
        return own, out, arrive

    def start(ins, outs, sems):
        own, out, _ = copies(ins, outs, sems, arrivals=False)
        for cp in own + out:
            cp.start()

    def finish(ins, outs, sems):
        own, out, arrive = copies(ins, outs, sems, arrivals=True)
        for cp in arrive:
            cp.wait_recv()
        for cp in out:
            cp.wait_send()
        for cp in own:
            cp.wait()

    dma = pltpu.SemaphoreType.DMA
    return _Rider(shards, [jax.ShapeDtypeStruct((N_SHARD,) + s.shape, s.dtype) for s in shards],
                  [dma((3 * nw,)), dma((3 * nw,)), dma((nw,)), dma((nw,))], start, finish)


def _gather_pass_rider(gathered, chained=False):
    nw = len(gathered)

    def copies(outs, sems, cc):
        send, recv = sems
        x, y, c = _coords()
        res = []
        for rel in (1, 2, 3):
            kx, ky = _chip_of(x, y, rel)
            for w in range(nw):
                rows = _gather_half(outs, w, 2 * kx + ky, cc)
                res.append(pltpu.make_async_remote_copy(
                    src_ref=rows, dst_ref=rows, send_sem=send.at[w * 3 + rel - 1], recv_sem=recv.at[w * 3 + rel - 1],
                    device_id=(x, y, 1 - c), device_id_type=MESH))
        return res

    def start(ins, outs, sems):
        for cp in copies(outs, sems, lax.axis_index("c")):
            cp.start()

    def finish(ins, outs, sems):
        c = lax.axis_index("c")
        for cp in copies(outs, sems, 1 - c):
            cp.wait_recv()
        for cp in copies(outs, sems, c):
            cp.wait_send()

    dma = pltpu.SemaphoreType.DMA
    shapes = [jax.ShapeDtypeStruct(g.shape, g.dtype) for g in gathered]
    if chained:
        return _Rider([], [], [dma((3 * nw,)), dma((3 * nw,))], start, finish)
    return _Rider(gathered, shapes, [dma((3 * nw,)), dma((3 * nw,))], start, finish,
                  aliases={w: w for w in range(nw)})


def _exchange_halves_rider(parts):
    nw = len(parts)

    def copies(ins, outs, sems):
        send, recv = sems
        x, y, c = _coords()
        res = []
        for w in range(nw):
            h = parts[w].shape[1] // 2
            res.append(pltpu.make_async_remote_copy(
                src_ref=ins[w].at[:, pl.ds((1 - c) * h, h), :], dst_ref=outs[w],
                send_sem=send.at[w], recv_sem=recv.at[w], device_id=(x, y, 1 - c), device_id_type=MESH))
        return res

    def start(ins, outs, sems):
        for cp in copies(ins, outs, sems):
            cp.start()

    def finish(ins, outs, sems):
        for cp in copies(ins, outs, sems):
            cp.wait()

    dma = pltpu.SemaphoreType.DMA
    return _Rider(parts, [jax.ShapeDtypeStruct((N_SHARD, p.shape[1] // 2, p.shape[2]), F32) for p in parts],
                  [dma((nw,)), dma((nw,))], start, finish)


def _add_halves(parts, theirs, pos):
    nw = len(parts)
    split = 2

    def body(pos_ref, *refs):
        ins, oth = refs[:nw], refs[nw:2 * nw]
        o32, o16 = refs[2 * nw:3 * nw], refs[3 * nw:]
        sums = [ins[w][...] + oth[w][...] for w in range(nw)]
        for w in range(nw):
            o16[w][...] = sums[w].astype(BF16)

        @pl.when(pl.program_id(1) == pos_ref[0])
        def _():
            for w in range(nw):
                o32[w][...] = sums[w]

    in_specs, oth_specs, o32_specs, shapes32, shapes16 = [], [], [], [], []
    for p in parts:
        hb = p.shape[1] // 2 // split
        blk = (None, hb, p.shape[2])
        in_specs.append(pl.BlockSpec(blk, lambda i, j, pos_ref: (j, pos_ref[1] * split + i, 0)))
        oth_specs.append(pl.BlockSpec(blk, lambda i, j, pos_ref: (j, i, 0)))
        o32_specs.append(pl.BlockSpec((hb, p.shape[2]), lambda i, j, pos_ref: (i, 0)))
        shapes32.append(jax.ShapeDtypeStruct((p.shape[1] // 2, p.shape[2]), F32))
        shapes16.append(jax.ShapeDtypeStruct((N_SHARD, p.shape[1] // 2, p.shape[2]), BF16))
    return pl.pallas_call(
        body, name="add_halves",
        grid_spec=pltpu.PrefetchScalarGridSpec(
            num_scalar_prefetch=1, grid=(split, N_SHARD),
            in_specs=in_specs + oth_specs, out_specs=o32_specs + oth_specs),
        out_shape=shapes32 + shapes16,
        compiler_params=_params("parallel", "arbitrary", vmem=VMEM_LIMIT),
    )(pos, *parts, *theirs)


def _exchange_chips_rider(sums16):
    nw = len(sums16)

    def copies(ins, outs, sems):
        send, recv = sems
        x, y, c = _coords()
        res = []
        for rel in (1, 2, 3):
            kx, ky = _chip_of(x, y, rel)
            for w in range(nw):
                res.append(pltpu.make_async_remote_copy(
                    src_ref=ins[w].at[2 * kx + ky], dst_ref=outs[w].at[rel - 1],
                    send_sem=send.at[w * 3 + rel - 1], recv_sem=recv.at[w * 3 + rel - 1],
                    device_id=(kx, ky, c), device_id_type=MESH))
        return res

    def start(ins, outs, sems):
        for cp in copies(ins, outs, sems):
            cp.start()

    def finish(ins, outs, sems):
        for cp in copies(ins, outs, sems):
            cp.wait()

    dma = pltpu.SemaphoreType.DMA
    return _Rider(sums16, [jax.ShapeDtypeStruct((3,) + s.shape[1:], BF16) for s in sums16],
                  [dma((3 * nw,)), dma((3 * nw,))], start, finish)


def _add_chips(sums32, theirs, pos):
    nw = len(sums32)
    split = 2

    def body(pos_ref, *refs):
        ins, oth, outs = refs[:nw], refs[nw:2 * nw], refs[2 * nw:]
        for w in range(nw):
            acc = ins[w][...]
            for r in range(3):
                acc = acc + oth[w][r].astype(F32)
            outs[w][...] = acc

    in_specs, oth_specs, out_specs, shapes = [], [], [], []
    for s in sums32:
        hb = s.shape[0] // split
        in_specs.append(pl.BlockSpec((hb, s.shape[1]), lambda i, pos_ref: (i, 0)))
        oth_specs.append(pl.BlockSpec((3, hb, s.shape[1]), lambda i, pos_ref: (0, i, 0)))
        out_specs.append(pl.BlockSpec((hb, s.shape[1]), lambda i, pos_ref: (pos_ref[1] * split + i, 0)))
        shapes.append(jax.ShapeDtypeStruct((2 * s.shape[0], s.shape[1]), F32))
    return pl.pallas_call(
        body, name="add_chips",
        grid_spec=pltpu.PrefetchScalarGridSpec(
            num_scalar_prefetch=1, grid=(split,), in_specs=in_specs + oth_specs, out_specs=out_specs),
        out_shape=shapes,
        compiler_params=_params("parallel", vmem=VMEM_LIMIT),
    )(pos, *sums32, *theirs)


def _join_halves(shards):
    nw = len(shards)

    def body(*refs):
        outs = refs[nw:2 * nw]
        send, recv = refs[2 * nw:]
        x, y, c = _coords()

        def copy(w, cc):
            h = shards[w].shape[0] // 2
            rows = outs[w].at[pl.ds(cc * h, h), :]
            return pltpu.make_async_remote_copy(
                src_ref=rows, dst_ref=rows, send_sem=send.at[w], recv_sem=recv.at[w],
                device_id=(x, y, 1 - c), device_id_type=MESH)

        for w in range(nw):
            copy(w, c).start()
        for w in range(nw):
            copy(w, 1 - c).wait_recv()
            copy(w, c).wait_send()

    hbm = pl.BlockSpec(memory_space=pl.ANY)
    return pl.pallas_call(
        body, name="join_halves",
        in_specs=[hbm] * nw, out_specs=[hbm] * nw,
        out_shape=[jax.ShapeDtypeStruct(s.shape, F32) for s in shards],
        input_output_aliases={w: w for w in range(nw)},
        scratch_shapes=[pltpu.SemaphoreType.DMA((nw,)), pltpu.SemaphoreType.DMA((nw,))],
    )(*shards)


def _adamw_math(w, g, m, v):
    m = ADAM_B1 * m + (1.0 - ADAM_B1) * g
    v = ADAM_B2 * v + (1.0 - ADAM_B2) * (g * g)
    m_hat = m / (1.0 - ADAM_B1 ** ADAM_STEP)
    v_hat = v / (1.0 - ADAM_B2 ** ADAM_STEP)
    delta = -ADAM_LR * (m_hat / (jnp.sqrt(v_hat) + ADAM_EPS) + ADAM_WD * w)
    return delta, m, v


def _adamw(ws, gs, ms, vs):
    nw = len(ws)
    split = 8

    def body(*refs):
        w_r, g_r, m_r, v_r = (refs[i * nw:(i + 1) * nw] for i in range(4))
        g_o, d_o, m_o, v_o = (refs[(4 + i) * nw:(5 + i) * nw] for i in range(4))
        for k in range(nw):
            g = g_r[k][...]
            d, m, v = _adamw_math(w_r[k][...], g, m_r[k][...], v_r[k][...])
            g_o[k][...] = g
            d_o[k][...] = d
            m_o[k][...] = m
            v_o[k][...] = v

    specs = [pl.BlockSpec((w.shape[0] // split, w.shape[1]), lambda i: (i, 0)) for w in ws]
    shapes = [jax.ShapeDtypeStruct(w.shape, F32) for w in ws]
    outs = pl.pallas_call(
        body, name="adamw", grid=(split,),
        in_specs=specs * 4, out_specs=specs * 4, out_shape=shapes * 4,
        compiler_params=_params("parallel", vmem=VMEM_LIMIT),
    )(*ws, *gs, *ms, *vs)
    return outs[:nw], outs[nw:2 * nw], outs[2 * nw:3 * nw], outs[3 * nw:]


SMALL_ROWS = 8
SMALL_COLS = D_MODEL
LOSS_COL = RET_WIDTH + 24


def _small_allreduce_adamw(part, w, m, v, rider=None):
    def body(part_ref, w_ref, m_ref, v_ref, g_out, d_out, m_out, v_out, all_ref, send, recv):
        x, y, c = _coords()
        me = 4 * x + 2 * y + c
        all_ref[me] = part_ref[...]
        copies = []
        for rel in range(1, 8):
            px = 1 - x if rel & 4 else x
            py = 1 - y if rel & 2 else y
            pc = 1 - c if rel & 1 else c
            copies.append(pltpu.make_async_remote_copy(
                src_ref=part_ref, dst_ref=all_ref.at[me],
                send_sem=send.at[rel - 1], recv_sem=recv.at[rel - 1], device_id=(px, py, pc), device_id_type=MESH))
        for cp in copies:
            cp.start()
        for cp in copies:
            cp.wait()
        g = all_ref[0]
        for k in range(1, 8):
            g = g + all_ref[k]
        d, mn, vn = _adamw_math(w_ref[...], g, m_ref[...], v_ref[...])
        g_out[...] = g
        d_out[...] = d
        m_out[...] = mn
        v_out[...] = vn

    vm = pl.BlockSpec(memory_space=pltpu.VMEM)
    shape = jax.ShapeDtypeStruct((SMALL_ROWS, SMALL_COLS), F32)
    return _hosted_call(
        body, "small_allreduce_adamw", (1,),
        in_specs=[vm] * 4, out_specs=[vm] * 4, out_shape=[shape] * 4,
        scratch_shapes=[pltpu.VMEM((8, SMALL_ROWS, SMALL_COLS), F32),
                        pltpu.SemaphoreType.DMA((7,)), pltpu.SemaphoreType.DMA((7,))],
        operands=(part, w, m, v), rider=rider, semantics=["arbitrary"])


SMALL_NAMES = ("ret_decay_fwd", "ret_decay_bwd", "attn_sink", "ret_gn_gain",
               "ln1_gain", "ln1_bias", "ln2_gain", "ln2_bias")


LN_NAMES = ("ln1_gain", "ln1_bias", "ln2_gain", "ln2_bias")


def _pack_small(vals, extra=None):
    tail = jnp.zeros((1, 1), F32) if extra is None else extra.reshape(1, 1)
    row4 = jnp.concatenate([vals["ret_gn_gain"], vals["ret_decay_fwd"], vals["ret_decay_bwd"], vals["attn_sink"],
                            tail, jnp.zeros((1, SMALL_COLS - LOSS_COL - 1), F32)], axis=1)
    rows = [vals[n] for n in LN_NAMES] + [row4, jnp.zeros((SMALL_ROWS - 5, SMALL_COLS), F32)]
    return jnp.concatenate(rows, axis=0)


def _unpack_small(packed):
    out = {n: packed[i:i + 1] for i, n in enumerate(LN_NAMES)}
    o = RET_WIDTH
    out.update(ret_gn_gain=packed[4:5, 0:o], ret_decay_fwd=packed[4:5, o:o + 8],
               ret_decay_bwd=packed[4:5, o + 8:o + 16], attn_sink=packed[4:5, o + 16:o + 24])
    return out


def _local_step(x, p, tgt, w_in_t, rest, small, pos=None, small_state=None):
    bsz, s, _ = x.shape
    t = bsz * s
    x2d = x.reshape(t, D_MODEL)
    p2d = p.reshape(t, PLE_DIM)
    tgt2d = tgt.reshape(t, D_MODEL)
    dec_f = small["ret_decay_fwd"].reshape(8)
    dec_b = small["ret_decay_bwd"].reshape(8)
    lg_f = jnp.log1p(-jnp.exp2(dec_f))
    lg_b = jnp.log1p(-jnp.exp2(dec_b))
    per_lane = lambda v: jnp.repeat(v, HEAD_DIM).reshape(4, 1, LANES)
    lgf_l, lgb_l = per_lane(lg_f), per_lane(lg_b)
    sink = small["attn_sink"].reshape(8)
    slopes = 2.0 ** (-(jnp.arange(8, dtype=F32) + 1.0))
    gn_gain = small["ret_gn_gain"]
    g1, b1, g2, b2 = (small[n] for n in ("ln1_gain", "ln1_bias", "ln2_gain", "ln2_bias"))

    dist = pos is not None
    chips = lambda names: _gather_chips_rider([rest[REST_NAMES.index(n)] for n in names])
    first, second, third = ("w_ffn_up",), ("w_out", "w_ffn_gate", "w_ple_proj", "w_ple_gate"), ("w_ffn_down",)
    u, *c1 = _inproj(x2d, w_in_t, rider=chips(first) if dist else None)
    u3 = u.reshape(bsz, s, IN_WIDTH)
    y_pre, y_ret, *o2 = _ret_fwd(u3, lgf_l, lgb_l, gn_gain,
                                 rider=_merge_riders([_gather_pass_rider(c1), chips(second)]) if dist else None)
    y_att, att_p, att_ps, *o3 = _attn_fwd(u3, slopes, sink, rider=_merge_riders(
        [_gather_pass_rider(o2[len(first):]), chips(third)]) if dist else None)
    gathered = dict(zip(first, o2[:len(first)]))
    gathered.update(zip(second, o3[:len(second)]))
    w_out = _assemble_weights({"w_out": gathered["w_out"]})["w_out"] if dist else rest["w_out"]
    zh1, r1, hb, *o4 = _outproj_ln1(y_ret.reshape(t, RET_WIDTH), y_att.reshape(t, ATTN_WIDTH), x2d, w_out, g1, b1,
                                    rider=_gather_pass_rider(o3[len(second):]) if dist else None)
    gathered.update(zip(third, o4))
    wts = _assemble_weights(gathered) if dist else rest
    dz2, dz2b, gs, us, acts, pg, ple, sq, dg2, db2 = _ffn_fwd(
        zh1, hb, p2d, tgt2d, g1, b1, g2, b2, wts["gate4"], wts["up4"], wts["down4"], wts["ple_proj"], wts["ple_gate"])
    dgs, dus, dsp, dple, dz1, dyr, dya, dg1, db1 = _ffn_bwd(dz2, gs, us, pg, ple, zh1, r1, g1, wts["gate4"],
                                                          wts["up4"], wts["down4"], wts["ple_gate"], wts["w_out"])
    ffn_parts = list(_wgrad_ffn(acts, dgs, dus, hb, dz2b))
    d_w_out, d_ple_gate, d_ple_proj, *th_ffn = _wgrad_misc(
        y_ret.reshape(t, RET_WIDTH), y_att.reshape(t, ATTN_WIDTH), dz1, hb, dsp, p2d, dple,
        rider=_exchange_halves_rider(ffn_parts) if dist else None)
    misc_parts = [d_w_out.reshape(N_SHARD, D_MODEL // N_SHARD, D_MODEL),
                  d_ple_proj.reshape(PLE_DIM, N_SHARD, D_MODEL // N_SHARD).transpose(1, 0, 2),
                  d_ple_gate.reshape(N_SHARD, D_MODEL // N_SHARD, D_MODEL)]
    dyr3, dya3 = dyr.reshape(bsz, s, RET_WIDTH), dya.reshape(bsz, s, ATTN_WIDTH)
    if dist:
        s_ffn = _add_halves(ffn_parts, th_ffn, pos)
        drq, drk, drv, drg, rpart, *o5 = _ret_bwd(u3, y_pre, dyr3, lgf_l, lgb_l, gn_gain, rider=_merge_riders(
            [_exchange_chips_rider(s_ffn[3:5]), _exchange_halves_rider(misc_parts)]))
        s_misc = _add_halves(misc_parts, o5[2:], pos)
        daq, dakv, spart, *o6 = _attn_bwd(u3, dya3, att_p, att_ps,
                                          rider=_exchange_chips_rider([s_ffn[5]] + list(s_misc[3:])))
    else:
        drq, drk, drv, drg, rpart = _ret_bwd(u3, y_pre, dyr3, lgf_l, lgb_l, gn_gain)
        daq, dakv, spart = _attn_bwd(u3, dya3, att_p, att_ps)
    pieces = [a.reshape(t, -1) for a in (drq, drk, drv, drg, daq, dakv)]
    kv0 = CB_AK * LANES
    w_kv = jnp.concatenate([w_in_t[kv0 + o:kv0 + o + HEAD_DIM] for o in KV_ORDER], axis=0)
    d_in = _wgrad_in(pieces, x2d).reshape(N_SHARD, FFN_SHARD, D_MODEL)

    rsum = rpart
    lane_heads = lambda row: jnp.sum(row.reshape(4, 2, HEAD_DIM), axis=-1).reshape(8)
    dlg_f = lane_heads(rsum[:, 0, :]) + jnp.stack([jnp.sum(rsum[:, 2, :], -1), jnp.sum(rsum[:, 3, :], -1)], 1).reshape(8)
    dlg_b = lane_heads(rsum[:, 1, :]) + jnp.stack([jnp.sum(rsum[:, 4, :], -1), jnp.sum(rsum[:, 5, :], -1)], 1).reshape(8)
    chain = lambda d: -(math.log(2.0) * jnp.exp2(d)) / (1.0 - jnp.exp2(d))
    grads_small = {
        "ret_decay_fwd": (dlg_f * chain(dec_f)).reshape(1, 8),
        "ret_decay_bwd": (dlg_b * chain(dec_b)).reshape(1, 8),
        "attn_sink": jnp.sum(spart, axis=0)[:, 0:4, 0].reshape(1, 8),
        "ret_gn_gain": rsum[:, 6, :].reshape(1, RET_WIDTH),
        "ln1_gain": dg1, "ln1_bias": db1, "ln2_gain": dg2, "ln2_bias": db2,
    }
    if not dist:
        grad_x, = _inproj_bwd(dz1, pieces, w_in_t[:kv0], w_kv)
        grads_rest = [misc_parts[0]] + ffn_parts + misc_parts[1:]
        return sq[0, 0], grad_x.reshape(bsz, s, D_MODEL), d_in, grads_rest, grads_small
    *small_out, th_in = _small_allreduce_adamw(_pack_small(grads_small, sq[0, 0]), *small_state,
                                               rider=_exchange_halves_rider([d_in]))
    s_in = _add_halves([d_in], [th_in], pos)
    grad_x, chips_in = _inproj_bwd(dz1, pieces, w_in_t[:kv0], w_kv, rider=_exchange_chips_rider([s_in[1]]))
    sums32 = [s_in[0], s_misc[0], s_ffn[0], s_ffn[1], s_ffn[2], s_misc[1], s_misc[2]]
    from_chips = [chips_in, o6[1], o5[0], o5[1], o6[0], o6[2], o6[3]]
    return grad_x.reshape(bsz, s, D_MODEL), sums32, from_chips, small_out


BIG_NAMES = ("w_in", "w_out", "w_ffn_gate", "w_ffn_up", "w_ffn_down", "w_ple_proj", "w_ple_gate")
REST_NAMES = BIG_NAMES[1:]
TRANSPOSED = ("w_in", "w_ffn_gate", "w_ffn_up")
WEIGHT_ORDER = ("w_in", "ret_decay_fwd", "ret_decay_bwd", "ret_gn_gain", "attn_sink", "w_out", "ln1_gain",
                "ln1_bias", "w_ffn_gate", "w_ffn_up", "w_ffn_down", "w_ple_proj", "w_ple_gate", "ln2_gain", "ln2_bias")


def _shard_rows(name, a):
    return jnp.swapaxes(a[0], 0, 1) if name in TRANSPOSED else a[0]


def _unshard_rows(name, a):
    return (jnp.swapaxes(a, 0, 1) if name in TRANSPOSED else a)[None]


def _assemble_weights(gathered):
    cols = lambda a: a.transpose(1, 0, 2).reshape(a.shape[1], N_SHARD * a.shape[2])
    rows = lambda a: a.reshape(N_SHARD * a.shape[1], a.shape[2])
    same = lambda a: a
    layout = {"w_out": ("w_out", rows), "w_ffn_gate": ("gate4", same), "w_ffn_up": ("up4", same),
              "w_ffn_down": ("down4", same), "w_ple_proj": ("ple_proj", cols), "w_ple_gate": ("ple_gate", rows)}
    return {layout[n][0]: layout[n][1](a) for n, a in gathered.items()}


def kernel(x, p, w_in, ret_decay_fwd, ret_decay_bwd, ret_gn_gain, attn_sink, w_out, ln1_gain, ln1_bias, w_ffn_gate, w_ffn_up, w_ffn_down, w_ple_proj, w_ple_gate, ln2_gain, ln2_bias, loss_target, m_w_in, m_ret_decay_fwd, m_ret_decay_bwd, m_ret_gn_gain, m_attn_sink, m_w_out, m_ln1_gain, m_ln1_bias, m_w_ffn_gate, m_w_ffn_up, m_w_ffn_down, m_w_ple_proj, m_w_ple_gate, m_ln2_gain, m_ln2_bias, v_w_in, v_ret_decay_fwd, v_ret_decay_bwd, v_ret_gn_gain, v_attn_sink, v_w_out, v_ln1_gain, v_ln1_bias, v_w_ffn_gate, v_w_ffn_up, v_w_ffn_down, v_w_ple_proj, v_w_ple_gate, v_ln2_gain, v_ln2_bias):
    w = dict(w_in=w_in, ret_decay_fwd=ret_decay_fwd, ret_decay_bwd=ret_decay_bwd, ret_gn_gain=ret_gn_gain,
             attn_sink=attn_sink, w_out=w_out, ln1_gain=ln1_gain, ln1_bias=ln1_bias, w_ffn_gate=w_ffn_gate,
             w_ffn_up=w_ffn_up, w_ffn_down=w_ffn_down, w_ple_proj=w_ple_proj, w_ple_gate=w_ple_gate,
             ln2_gain=ln2_gain, ln2_bias=ln2_bias)
    m = dict(w_in=m_w_in, ret_decay_fwd=m_ret_decay_fwd, ret_decay_bwd=m_ret_decay_bwd, ret_gn_gain=m_ret_gn_gain,
             attn_sink=m_attn_sink, w_out=m_w_out, ln1_gain=m_ln1_gain, ln1_bias=m_ln1_bias, w_ffn_gate=m_w_ffn_gate,
             w_ffn_up=m_w_ffn_up, w_ffn_down=m_w_ffn_down, w_ple_proj=m_w_ple_proj, w_ple_gate=m_w_ple_gate,
             ln2_gain=m_ln2_gain, ln2_bias=m_ln2_bias)
    v = dict(w_in=v_w_in, ret_decay_fwd=v_ret_decay_fwd, ret_decay_bwd=v_ret_decay_bwd, ret_gn_gain=v_ret_gn_gain,
             attn_sink=v_attn_sink, w_out=v_w_out, ln1_gain=v_ln1_gain, ln1_bias=v_ln1_bias, w_ffn_gate=v_w_ffn_gate,
             w_ffn_up=v_w_ffn_up, w_ffn_down=v_w_ffn_down, w_ple_proj=v_w_ple_proj, w_ple_gate=v_w_ple_gate,
             ln2_gain=v_ln2_gain, ln2_bias=v_ln2_bias)
    big = lambda d: [_shard_rows(n, d[n]) for n in BIG_NAMES]
    small = lambda d: {n: d[n] for n in SMALL_NAMES}

    chip = 2 * lax.axis_index("x") + lax.axis_index("y")
    pos = jnp.stack([chip, lax.axis_index("c")]).astype(jnp.int32)

    shards = [a.astype(BF16) for a in big(w)]
    (w_in4,) = _all_gather_weights(shards[:1])
    w_in_t = w_in4.reshape(IN_WIDTH, D_MODEL)
    grad_x, sums32, from_chips, (g_s, d_s, m_s, v_s) = _local_step(
        x, p[0], loss_target, w_in_t, shards[1:], small(w), pos=pos,
        small_state=(_pack_small(small(w)), _pack_small(small(m)), _pack_small(small(v))))
    g_big, d_big, m_big, v_big = _adamw(big(w), _join_halves(_add_chips(sums32, from_chips, pos)), big(m), big(v))
    loss = g_s[4, LOSS_COL] * (0.5 / D_MODEL)

    def tree(bigs, packed):
        out = {n: _unshard_rows(n, a) for n, a in zip(BIG_NAMES, bigs)}
        out.update(_unpack_small(packed))
        return [out[n] for n in WEIGHT_ORDER]

    return (loss, grad_x, *tree(g_big, g_s), *tree(d_big, d_s), *tree(m_big, m_s), *tree(v_big, v_s))
```

```python
import functools
import math

import jax
import jax.numpy as jnp
from jax import lax
from jax.experimental import pallas as pl
from jax.experimental.pallas import tpu as pltpu

F32 = jnp.float32
BF16 = jnp.bfloat16

D_MODEL = 1024
HEAD_DIM = 64
RET_HEADS = 8
ATTN_HEADS = 8
RET_WIDTH = 512
ATTN_WIDTH = 512
KV_WIDTH = 128
IN_WIDTH = 2816
FFN = 2816
N_SHARD = 4
FFN_SHARD = FFN // N_SHARD
PLE_DIM = 256
CHUNK = 128
LANES = 128
ALPHA = 2.0 ** 0.25
LN_EPS = 1e-5
GN_EPS = 1e-5
NEG_INF = -1e30
ADAM_LR = 0.001
ADAM_B1 = 0.9
ADAM_B2 = 0.999
ADAM_EPS = 1e-08
ADAM_WD = 0.01
ADAM_STEP = 10
VMEM_LIMIT = 56 * 1024 * 1024
MESH = pl.DeviceIdType.MESH

CB_RQ, CB_RK, CB_RV, CB_RG, CB_AQ, CB_AK, CB_AV = 0, 4, 8, 12, 16, 20, 21


def _dot(a, b):
    return jnp.dot(a, b, preferred_element_type=F32)


def _dot_nt(a, b):
    return lax.dot_general(a, b, (((1,), (1,)), ((), ())), preferred_element_type=F32)


def _dot_tn(a, b):
    return lax.dot_general(a, b, (((0,), (0,)), ((), ())), preferred_element_type=F32)


def _sigmoid(x):
    return 1.0 / (1.0 + jnp.exp(-x))


def _params(*sem, vmem=None):
    return pltpu.CompilerParams(dimension_semantics=tuple(sem) if sem else None, vmem_limit_bytes=vmem)


class _Rider:
    def __init__(self, ins, out_shapes, sems, start, finish, aliases=None):
        self.ins, self.out_shapes, self.sems = list(ins), list(out_shapes), list(sems)
        self.start, self.finish, self.aliases = start, finish, dict(aliases or {})


def _merge_riders(riders):
    riders = [r for r in riders if r is not None]
    if len(riders) == 1:
        return riders[0]
    bounds, aliases = [], {}
    i0 = o0 = s0 = 0
    for r in riders:
        bounds.append((i0, o0, s0))
        aliases.update({i0 + i: o0 + o for i, o in r.aliases.items()})
        i0, o0, s0 = i0 + len(r.ins), o0 + len(r.out_shapes), s0 + len(r.sems)

    def each(method):
        def run(ins, outs, sems):
            for r, (i, o, s) in zip(riders, bounds):
                getattr(r, method)(ins[i:i + len(r.ins)], outs[o:o + len(r.out_shapes)], sems[s:s + len(r.sems)])
        return run

    return _Rider([a for r in riders for a in r.ins], [a for r in riders for a in r.out_shapes],
                  [a for r in riders for a in r.sems], each("start"), each("finish"), aliases)


def _hosted_call(body, name, grid, in_specs, out_specs, out_shape, scratch_shapes, operands, rider=None,
                 semantics=None):
    n_in, n_out, n_scr = len(in_specs), len(out_specs), len(scratch_shapes)
    if rider is None:
        return pl.pallas_call(
            body, name=name, grid=grid, in_specs=in_specs, out_specs=out_specs, out_shape=out_shape,
            scratch_shapes=scratch_shapes,
            compiler_params=_params(*(semantics or ["parallel"] * len(grid)), vmem=VMEM_LIMIT))(*operands)
    r_in, r_out = len(rider.ins), len(rider.out_shapes)

    def full_body(*refs):
        main_in, rin = refs[:n_in], refs[n_in:n_in + r_in]
        o0 = n_in + r_in
        main_out, rout = refs[o0:o0 + n_out], refs[o0 + n_out:o0 + n_out + r_out]
        s0 = o0 + n_out + r_out
        main_scr, rsem = refs[s0:s0 + n_scr], refs[s0 + n_scr:]
        first = functools.reduce(jnp.logical_and, [pl.program_id(a) == 0 for a in range(len(grid))])
        last = functools.reduce(jnp.logical_and, [pl.program_id(a) == g - 1 for a, g in enumerate(grid)])

        @pl.when(first)
        def _():
            rider.start(rin, rout, rsem)

        body(*main_in, *main_out, *main_scr)

        @pl.when(last)
        def _():
            rider.finish(rin, rout, rsem)

    hbm = pl.BlockSpec(memory_space=pl.ANY)
    return pl.pallas_call(
        full_body, name=name, grid=grid,
        in_specs=list(in_specs) + [hbm] * r_in, out_specs=list(out_specs) + [hbm] * r_out,
        out_shape=list(out_shape) + rider.out_shapes,
        scratch_shapes=list(scratch_shapes) + rider.sems,
        input_output_aliases={n_in + i: n_out + o for i, o in rider.aliases.items()},
        compiler_params=_params(*(["arbitrary"] * len(grid)), vmem=VMEM_LIMIT),
    )(*operands, *rider.ins)


def _head_mean(x, m0):
    s0 = jnp.sum(jnp.where(m0, x, 0.0), axis=1, keepdims=True)
    s1 = jnp.sum(jnp.where(m0, 0.0, x), axis=1, keepdims=True)
    return jnp.where(m0, s0, s1) * (1.0 / HEAD_DIM)


def _inproj(x2d, w_in_t, rider=None):
    t = x2d.shape[0]
    tm = 512
    nb = 256

    def body(x_ref, w_ref, o_ref):
        xb = x_ref[...].astype(BF16)
        for n in range(0, IN_WIDTH, nb):
            o_ref[:, n:n + nb] = _dot_nt(xb, w_ref[n:n + nb, :]).astype(BF16)

    return _hosted_call(
        body, "inproj", (t // tm,),
        in_specs=[pl.BlockSpec((tm, D_MODEL), lambda i: (i, 0)),
                  pl.BlockSpec((IN_WIDTH, D_MODEL), lambda i: (0, 0))],
        out_specs=[pl.BlockSpec((tm, IN_WIDTH), lambda i: (i, 0))],
        out_shape=[jax.ShapeDtypeStruct((t, IN_WIDTH), BF16)],
        scratch_shapes=[], operands=(x2d, w_in_t), rider=rider)


def _outproj_ln1(y_ret, y_att, x2d, w_out, gain, bias, rider=None):
    t = x2d.shape[0]
    tm = 512

    def body(yr_ref, ya_ref, x_ref, w_ref, g_ref, b_ref, zh_ref, r_ref, hb_ref):
        mix = _dot(yr_ref[...], w_ref[0:RET_WIDTH, :]) + _dot(ya_ref[...], w_ref[RET_WIDTH:, :])
        z = ALPHA * x_ref[...] + mix
        mu = jnp.mean(z, axis=1, keepdims=True)
        zc = z - mu
        var = jnp.mean(zc * zc, axis=1, keepdims=True)
        r = lax.rsqrt(var + LN_EPS)
        zh = zc * r
        zh_ref[...] = zh
        r_ref[...] = r
        hb_ref[...] = (zh * g_ref[...] + b_ref[...]).astype(BF16)

    row = lambda w: pl.BlockSpec((tm, w), lambda i: (i, 0))
    const = lambda s: pl.BlockSpec(s, lambda i: (0, 0))
    return _hosted_call(
        body, "outproj_ln1", (t // tm,),
        in_specs=[row(RET_WIDTH), row(ATTN_WIDTH), row(D_MODEL), const((D_MODEL, D_MODEL)),
                  const((1, D_MODEL)), const((1, D_MODEL))],
        out_specs=[row(D_MODEL), row(1), row(D_MODEL)],
        out_shape=[jax.ShapeDtypeStruct((t, D_MODEL), F32), jax.ShapeDtypeStruct((t, 1), F32),
                   jax.ShapeDtypeStruct((t, D_MODEL), BF16)],
        scratch_shapes=[], operands=(y_ret, y_att, x2d, w_out, gain, bias), rider=rider)


def _load_resident(step, pairs, sems):
    copies = [pltpu.make_async_copy(src, dst, sems.at[i]) for i, (src, dst) in enumerate(pairs)]

    @pl.when(step == 0)
    def _():
        for cp in copies:
            cp.start()
        for cp in copies:
            cp.wait()


def _ffn_fwd(zh1, hb, p2d, tgt, g1, b1, g2, b2, wg4, wu4, wd4, wpe, wpg):
    t = zh1.shape[0]
    tm = 256

    def body(zh_ref, hb_ref, p_ref, t_ref, g1_ref, b1_ref, g2_ref, b2_ref,
             wg_hbm, wu_hbm, wd_hbm, wpe_hbm, wpg_hbm,
             dz_ref, dzb_ref, gs_ref, us_ref, act_ref, pg_ref, ple_ref, loss_ref, dg2_ref, db2_ref,
             wg, wu, wd, wpe, wpg, wsem):
        step = pl.program_id(0)
        loads = [(h.at[j], v.at[j]) for j in range(N_SHARD) for h, v in ((wg_hbm, wg), (wu_hbm, wu), (wd_hbm, wd))]
        _load_resident(step, loads + [(wpe_hbm, wpe), (wpg_hbm, wpg)], wsem)

        @pl.when(step == 0)
        def _():
            loss_ref[...] = jnp.zeros_like(loss_ref)
            dg2_ref[...] = jnp.zeros_like(dg2_ref)
            db2_ref[...] = jnp.zeros_like(db2_ref)

        h1 = zh_ref[...] * g1_ref[...] + b1_ref[...]
        hbv = hb_ref[...]
        ffn = jnp.zeros((tm, D_MODEL), F32)
        acts = []
        for j in range(N_SHARD + 1):
            if j < N_SHARD:
                gj = _dot_nt(hbv, wg[j])
                uj = _dot_nt(hbv, wu[j])
                gs_ref[j] = gj.astype(BF16)
                us_ref[j] = uj.astype(BF16)
                acts.append((gj * _sigmoid(gj) * uj).astype(BF16))
                act_ref[j] = acts[j]
            if j > 0:
                ffn = ffn + _dot(acts[j - 1], wd[j - 1])
        ple = _dot(p_ref[...].astype(BF16), wpe[...])
        pg = _sigmoid(_dot(hbv, wpg[...]))
        pg_ref[...] = pg.astype(BF16)
        ple_ref[...] = ple.astype(BF16)
        z2 = ALPHA * h1 + ffn + pg * ple
        mu = jnp.mean(z2, axis=1, keepdims=True)
        zc = z2 - mu
        var = jnp.mean(zc * zc, axis=1, keepdims=True)
        r = lax.rsqrt(var + LN_EPS)
        zh2 = zc * r
        err = zh2 * g2_ref[...] + b2_ref[...] - t_ref[...]
        loss_ref[...] += jnp.sum(err * err)
        dy = err * (1.0 / D_MODEL)
        dg2_ref[...] += jnp.sum(dy * zh2, axis=0, keepdims=True)
        db2_ref[...] += jnp.sum(dy, axis=0, keepdims=True)
        dzh = dy * g2_ref[...]
        m1 = jnp.mean(dzh, axis=1, keepdims=True)
        m2 = jnp.mean(dzh * zh2, axis=1, keepdims=True)
        dz2 = r * (dzh - m1 - zh2 * m2)
        dz_ref[...] = dz2
        dzb_ref[...] = dz2.astype(BF16)

    row = lambda w: pl.BlockSpec((tm, w), lambda i: (i, 0))
    const = lambda s: pl.BlockSpec(s, lambda i: (0, 0))
    sh = pl.BlockSpec((N_SHARD, tm, FFN_SHARD), lambda i: (0, i, 0))
    sh_shape = jax.ShapeDtypeStruct((N_SHARD, t, FFN_SHARD), BF16)
    hbm = pl.BlockSpec(memory_space=pl.ANY)
    return pl.pallas_call(
        body, name="ffn_fwd", grid=(t // tm,),
        in_specs=[row(D_MODEL), row(D_MODEL), row(PLE_DIM), row(D_MODEL),
                  const((1, D_MODEL)), const((1, D_MODEL)), const((1, D_MODEL)), const((1, D_MODEL)),
                  hbm, hbm, hbm, hbm, hbm],
        out_specs=[row(D_MODEL), row(D_MODEL), sh, sh, sh, row(D_MODEL), row(D_MODEL),
                   const((8, LANES)), const((1, D_MODEL)), const((1, D_MODEL))],
        out_shape=[jax.ShapeDtypeStruct((t, D_MODEL), F32), jax.ShapeDtypeStruct((t, D_MODEL), BF16),
                   sh_shape, sh_shape, sh_shape,
                   jax.ShapeDtypeStruct((t, D_MODEL), BF16), jax.ShapeDtypeStruct((t, D_MODEL), BF16),
                   jax.ShapeDtypeStruct((8, LANES), F32),
                   jax.ShapeDtypeStruct((1, D_MODEL), F32), jax.ShapeDtypeStruct((1, D_MODEL), F32)],
        scratch_shapes=[pltpu.VMEM(wg4.shape, BF16), pltpu.VMEM(wu4.shape, BF16), pltpu.VMEM(wd4.shape, BF16),
                        pltpu.VMEM(wpe.shape, BF16), pltpu.VMEM(wpg.shape, BF16),
                        pltpu.SemaphoreType.DMA((3 * N_SHARD + 2,))],
        compiler_params=_params("arbitrary", vmem=VMEM_LIMIT),
    )(zh1, hb, p2d, tgt, g1, b1, g2, b2, wg4, wu4, wd4, wpe, wpg)


def _ret_tables(lgf, lgb):
    c = CHUNK
    row = lax.broadcasted_iota(jnp.int32, (c, LANES), 0).astype(F32)
    ii = lax.broadcasted_iota(jnp.int32, (c, c), 0).astype(F32)
    jj = lax.broadcasted_iota(jnp.int32, (c, c), 1).astype(F32)
    diff = ii - jj
    dmats = []
    for h in range(2):
        lf = lgf[:, h * HEAD_DIM:h * HEAD_DIM + 1]
        lb = lgb[:, h * HEAD_DIM:h * HEAD_DIM + 1]
        dmats.append(jnp.where(diff > 0, jnp.exp(lf * jnp.maximum(diff, 0.0)),
                               jnp.where(diff < 0, jnp.exp(lb * jnp.maximum(-diff, 0.0)), 2.0)))
    tab = dict(
        qdec_f=jnp.exp(lgf * (row + 1.0)), kdec_f=jnp.exp(lgf * (c - 1.0 - row)),
        qdec_b=jnp.exp(lgb * (c - row)), kdec_b=jnp.exp(lgb * row),
        cdec_f=jnp.exp(lgf * c), cdec_b=jnp.exp(lgb * c),
        d0=dmats[0], d1=dmats[1], row=row, diff=diff)
    r = lax.broadcasted_iota(jnp.int32, (LANES, LANES), 0) < HEAD_DIM
    cc = lax.broadcasted_iota(jnp.int32, (LANES, LANES), 1) < HEAD_DIM
    tab["bd"] = r == cc
    tab["m0"] = lax.broadcasted_iota(jnp.int32, (c, LANES), 1) < HEAD_DIM
    return tab


def _ret_specs(bsz, s):
    blk = lambda cb: pl.BlockSpec((bsz, s, LANES), lambda p, cb=cb: (0, 0, cb + p))
    lane = pl.BlockSpec((None, 1, LANES), lambda p: (p, 0, 0))
    gain = pl.BlockSpec((1, LANES), lambda p: (0, p))
    pair = pl.BlockSpec((bsz, s, LANES), lambda p: (0, 0, p))
    return blk, lane, gain, pair


def _ret_kv_states(tb, k_ref, v_ref, rb_ref, kvf_ref, n_chunk):
    c = CHUNK
    bsz = k_ref.shape[0]
    bd = tb["bd"]

    def step(i, rbs):
        n = n_chunk - 1 - i
        sl = pl.ds(pl.multiple_of(n * c, c), c)
        kfb = []
        for b in range(bsz):
            k32 = k_ref[b, sl, :].astype(F32)
            kfb.append(jnp.concatenate([k32 * tb["kdec_f"], k32 * tb["kdec_b"]], axis=1).astype(BF16))
        kvs = [_dot_tn(kfb[b], v_ref[b, sl, :]) for b in range(bsz)]
        new = []
        for b in range(bsz):
            rb_ref[b, n] = rbs[b]
            kvf_ref[b, n] = jnp.where(bd, kvs[b][0:LANES], 0.0)
            new.append(rbs[b] * tb["cdec_b"] + jnp.where(bd, kvs[b][LANES:], 0.0))
        return tuple(new)

    lax.fori_loop(0, n_chunk, step, tuple(jnp.zeros((LANES, LANES), F32) for _ in range(bsz)))


def _split_rows(x, m0):
    return jnp.concatenate([jnp.where(m0, x, 0.0), jnp.where(m0, 0.0, x)], axis=0).astype(BF16)


def _ret_fwd(u3, lgf_l, lgb_l, gn_gain, rider=None):
    bsz, s, _ = u3.shape
    n_chunk = s // CHUNK
    c = CHUNK

    def body(q_ref, k_ref, v_ref, g_ref, lgf_ref, lgb_ref, gain_ref, yh_ref, rstd_ref, o_ref, rb_ref, kvf_ref):
        tb = _ret_tables(lgf_ref[...], lgb_ref[...])
        m0 = tb["m0"]
        gain = gain_ref[...]
        rows = range(bsz)
        _ret_kv_states(tb, k_ref, v_ref, rb_ref, kvf_ref, n_chunk)

        def chunk(n, rfs):
            sl = pl.ds(pl.multiple_of(n * c, c), c)
            qs = [q_ref[b, sl, :].astype(F32) * 0.125 for b in rows]
            s01 = [_dot_nt(_split_rows(qs[b], m0), k_ref[b, sl, :]) for b in rows]
            ys = []
            for b in rows:
                lhs = jnp.concatenate([s01[b][0:c] * tb["d0"], s01[b][c:] * tb["d1"],
                                       qs[b] * tb["qdec_f"], qs[b] * tb["qdec_b"]], axis=1).astype(BF16)
                rhs = jnp.concatenate([_split_rows(v_ref[b, sl, :].astype(F32), m0),
                                       rfs[b].astype(BF16), rb_ref[b, n].astype(BF16)], axis=0)
                ys.append(_dot(lhs, rhs))
            new = []
            for b in rows:
                y = ys[b]
                mu = _head_mean(y, m0)
                yc = y - mu
                rstd = lax.rsqrt(_head_mean(yc * yc, m0) + GN_EPS)
                yh = yc * rstd
                g = g_ref[b, sl, :].astype(F32)
                yh_ref[b, sl, :] = yh
                rstd_ref[b, sl, :] = rstd
                o_ref[b, sl, :] = (yh * gain * (g * _sigmoid(g))).astype(BF16)
                new.append(rfs[b] * tb["cdec_f"] + kvf_ref[b, n])
            return tuple(new)

        lax.fori_loop(0, n_chunk, chunk, tuple(jnp.zeros((LANES, LANES), F32) for _ in rows))

    blk, lane, gain, pair = _ret_specs(bsz, s)
    state = pltpu.VMEM((bsz, n_chunk, LANES, LANES), F32)
    return _hosted_call(
        body, "ret_fwd", (4,),
        in_specs=[blk(CB_RQ), blk(CB_RK), blk(CB_RV), blk(CB_RG), lane, lane, gain],
        out_specs=[pair, pair, pair],
        out_shape=[jax.ShapeDtypeStruct((bsz, s, RET_WIDTH), F32), jax.ShapeDtypeStruct((bsz, s, RET_WIDTH), F32),
                   jax.ShapeDtypeStruct((bsz, s, RET_WIDTH), BF16)],
        scratch_shapes=[state, state],
        operands=(u3, u3, u3, u3, lgf_l, lgb_l, gn_gain), rider=rider)


def _ret_bwd(u3, y_hat, y_rstd, d_o, lgf_l, lgb_l, gn_gain, rider=None):
    bsz, s, _ = u3.shape
    n_chunk = s // CHUNK
    c = CHUNK

    def body(q_ref, k_ref, v_ref, g_ref, yh_ref, rstd_ref, do_ref, lgf_ref, lgb_ref, gain_ref,
             dq_ref, dk_ref, dv_ref, dg_ref, part_ref,
             rb_ref, kvf_ref, rf_ref, dirf_ref, dy_ref, dk_acc, dv_acc, pa0, pa1, vec_ref):
        tb = _ret_tables(lgf_ref[...], lgb_ref[...])
        m0, bd, row = tb["m0"], tb["bd"], tb["row"]
        gain = gain_ref[...]
        wf = jnp.maximum(tb["diff"], 0.0)
        wb = jnp.maximum(-tb["diff"], 0.0)
        rows = range(bsz)
        zero_states = tuple(jnp.zeros((LANES, LANES), F32) for _ in rows)
        for ref in (pa0, pa1):
            ref[...] = jnp.zeros_like(ref)
        vec_ref[...] = jnp.zeros_like(vec_ref)
        _ret_kv_states(tb, k_ref, v_ref, rb_ref, kvf_ref, n_chunk)

        def sweep_fwd(n, carry):
            rfs, gbs = carry
            sl = pl.ds(pl.multiple_of(n * c, c), c)
            qs, ks, vs, dys, dybs, q01, k01, dy01 = [], [], [], [], [], [], [], []
            dgain = jnp.zeros((1, LANES), F32)
            for b in rows:
                q = q_ref[b, sl, :].astype(F32) * 0.125
                k = k_ref[b, sl, :]
                yh = yh_ref[b, sl, :]
                rstd = rstd_ref[b, sl, :]
                do = do_ref[b, sl, :].astype(F32)
                g = g_ref[b, sl, :].astype(F32)
                sg = _sigmoid(g)
                sil = g * sg
                dyh = do * gain * sil
                dg_ref[b, sl, :] = (do * yh * gain * sg * (1.0 + g * (1.0 - sg))).astype(BF16)
                dgain = dgain + jnp.sum(do * yh * sil, axis=0, keepdims=True)
                dy = rstd * (dyh - _head_mean(dyh, m0) - yh * _head_mean(dyh * yh, m0))
                dyb = dy.astype(BF16)
                dy_ref[b, sl, :] = dyb
                rf_ref[b, n] = rfs[b]
                qs.append(q)
                ks.append(k)
                vs.append(v_ref[b, sl, :])
                dys.append(dy)
                dybs.append(dyb)
                q01.append(_split_rows(q, m0))
                k01.append(_split_rows(k.astype(F32), m0))
                dy01.append(_split_rows(dy, m0))
            s01 = [_dot_nt(q01[b], ks[b]) for b in rows]
            da01 = [_dot_nt(dy01[b], vs[b]) for b in rows]
            rbn = [rb_ref[b, n] for b in rows]
            states = [jnp.concatenate([rfs[b], rbn[b]], axis=0).astype(BF16) for b in rows]
            dqc = [_dot_nt(dybs[b], states[b]) for b in rows]
            gbb = [gbs[b].astype(BF16) for b in rows]
            dkb = [_dot_nt(vs[b], gbb[b]) for b in rows]
            qfb = [jnp.concatenate([qs[b] * tb["qdec_f"], qs[b] * tb["qdec_b"]], axis=1) for b in rows]
            direct = [_dot_tn(qfb[b].astype(BF16), dybs[b]) for b in rows]
            ds_cat, ds_rows, a_rows = [], [], []
            for b in rows:
                a0 = s01[b][0:c] * tb["d0"]
                a1 = s01[b][c:] * tb["d1"]
                pa0[...] += da01[b][0:c] * a0
                pa1[...] += da01[b][c:] * a1
                ds0 = da01[b][0:c] * tb["d0"]
                ds1 = da01[b][c:] * tb["d1"]
                ds_cat.append(jnp.concatenate([ds0, ds1], axis=1).astype(BF16))
                ds_rows.append(jnp.concatenate([ds0, ds1], axis=0).astype(BF16))
                a_rows.append(jnp.concatenate([a0, a1], axis=0).astype(BF16))
            kbd = [ks[b].astype(F32) * tb["kdec_b"] for b in rows]
            dq_in = [_dot(ds_cat[b], k01[b]) for b in rows]
            dk_in = [_dot_tn(ds_rows[b], q01[b]) for b in rows]
            dv_in = [_dot_tn(a_rows[b], dy01[b]) for b in rows]
            dv_gb = [_dot(kbd[b].astype(BF16), gbb[b]) for b in rows]
            new_rf, new_gb = [], []
            dlf = jnp.zeros((1, LANES), F32)
            dlb = jnp.zeros((1, LANES), F32)
            for b in rows:
                dqf, dqb = dqc[b][:, 0:LANES], dqc[b][:, LANES:]
                qf, qb = qfb[b][:, 0:LANES], qfb[b][:, LANES:]
                dq = dq_in[b] + dqf * tb["qdec_f"] + dqb * tb["qdec_b"]
                dq_ref[b, sl, :] = (dq * 0.125).astype(BF16)
                dk_acc[b, sl, :] = dk_in[b] + dkb[b] * tb["kdec_b"]
                dv_acc[b, sl, :] = dv_in[b] + dv_gb[b]
                dlf = dlf + jnp.sum((row + 1.0) * qf * dqf, axis=0, keepdims=True)
                dlb = dlb + jnp.sum((c - row) * qb * dqb + row * kbd[b] * dkb[b], axis=0, keepdims=True)
                dlb = dlb + c * tb["cdec_b"] * jnp.sum(gbs[b] * rbn[b], axis=0, keepdims=True)
                dirf_ref[b, n] = jnp.where(bd, direct[b][0:LANES], 0.0)
                new_gb.append(jnp.where(bd, direct[b][LANES:], 0.0) + tb["cdec_b"] * gbs[b])
                new_rf.append(rfs[b] * tb["cdec_f"] + kvf_ref[b, n])
            vec_ref[0:1, :] += dlf
            vec_ref[1:2, :] += dlb
            vec_ref[6:7, :] += dgain
            return tuple(new_rf), tuple(new_gb)

        lax.fori_loop(0, n_chunk, sweep_fwd, (zero_states, zero_states))

        def sweep_bwd(i, gfs):
            n = n_chunk - 1 - i
            sl = pl.ds(pl.multiple_of(n * c, c), c)
            gfb = [gfs[b].astype(BF16) for b in rows]
            kfd = [k_ref[b, sl, :].astype(F32) * tb["kdec_f"] for b in rows]
            dkf = [_dot_nt(v_ref[b, sl, :], gfb[b]) for b in rows]
            dvf = [_dot(kfd[b].astype(BF16), gfb[b]) for b in rows]
            new = []
            dlf = jnp.zeros((1, LANES), F32)
            for b in rows:
                dk_ref[b, sl, :] = (dk_acc[b, sl, :] + dkf[b] * tb["kdec_f"]).astype(BF16)
                dv_ref[b, sl, :] = (dv_acc[b, sl, :] + dvf[b]).astype(BF16)
                dlf = dlf + jnp.sum((c - 1.0 - row) * kfd[b] * dkf[b], axis=0, keepdims=True)
                dlf = dlf + c * tb["cdec_f"] * jnp.sum(gfs[b] * rf_ref[b, n], axis=0, keepdims=True)
                new.append(dirf_ref[b, n] + tb["cdec_f"] * gfs[b])
            vec_ref[0:1, :] += dlf
            return tuple(new)

        lax.fori_loop(0, n_chunk, sweep_bwd, zero_states)
        vec_ref[2:3, :] = jnp.sum(pa0[...] * wf, axis=0, keepdims=True)
        vec_ref[3:4, :] = jnp.sum(pa1[...] * wf, axis=0, keepdims=True)
        vec_ref[4:5, :] = jnp.sum(pa0[...] * wb, axis=0, keepdims=True)
        vec_ref[5:6, :] = jnp.sum(pa1[...] * wb, axis=0, keepdims=True)
        part_ref[...] = vec_ref[...]

    blk, lane, gain, pair = _ret_specs(bsz, s)
    out_bf = jax.ShapeDtypeStruct((bsz, s, RET_WIDTH), BF16)
    state = pltpu.VMEM((bsz, n_chunk, LANES, LANES), F32)
    return _hosted_call(
        body, "ret_bwd", (4,),
        in_specs=[blk(CB_RQ), blk(CB_RK), blk(CB_RV), blk(CB_RG), pair, pair, pair, lane, lane, gain],
        out_specs=[pair, pair, pair, pair, pl.BlockSpec((None, 8, LANES), lambda p: (p, 0, 0))],
        out_shape=[out_bf, out_bf, out_bf, out_bf, jax.ShapeDtypeStruct((4, 8, LANES), F32)],
        scratch_shapes=[state, state, state, state,
                        pltpu.VMEM((bsz, s, LANES), BF16), pltpu.VMEM((bsz, s, LANES), F32),
                        pltpu.VMEM((bsz, s, LANES), F32),
                        pltpu.VMEM((c, c), F32), pltpu.VMEM((c, c), F32), pltpu.VMEM((8, LANES), F32)],
        operands=(u3, u3, u3, u3, y_hat, y_rstd, d_o, lgf_l, lgb_l, gn_gain), rider=rider)


def _attn_window_tables(n, s):
    qi = lax.broadcasted_iota(jnp.int32, (CHUNK, 3 * CHUNK), 0)
    kj = lax.broadcasted_iota(jnp.int32, (CHUNK, 3 * CHUNK), 1)
    dist = jnp.abs(kj - CHUNK - qi)
    kpos = n * CHUNK - CHUNK + kj
    valid = (dist <= CHUNK) & (kpos >= 0) & (kpos < s)
    return dist.astype(F32), valid


def _dup_kv_head(x, g):
    lane = lax.broadcasted_iota(jnp.int32, x.shape, 1)
    keep = (lane < HEAD_DIM) == (g == 0)
    xf = x.astype(F32)
    return jnp.where(keep, xf, pltpu.roll(xf, HEAD_DIM, 1))


def _attn_specs(s):
    q = pl.BlockSpec((None, s, 2 * LANES), lambda b, g: (b, 0, CB_AQ // 2 + g))
    k = pl.BlockSpec((None, s, LANES), lambda b, g: (b, 0, CB_AK))
    v = pl.BlockSpec((None, s, LANES), lambda b, g: (b, 0, CB_AV))
    grp = pl.BlockSpec((None, s, 2 * LANES), lambda b, g: (b, 0, g))
    smem = pl.BlockSpec(memory_space=pltpu.SMEM)
    return q, k, v, grp, smem


def _fill_padded(dst_ref, val, s):
    dst_ref[0:CHUNK, :] = jnp.zeros((CHUNK, LANES), dst_ref.dtype)
    dst_ref[CHUNK:CHUNK + s, :] = val.astype(dst_ref.dtype)
    dst_ref[CHUNK + s:2 * CHUNK + s, :] = jnp.zeros((CHUNK, LANES), dst_ref.dtype)


def _attn_probs(sc, slope, snk, dist, valid):
    sc = jnp.where(valid, sc - slope * dist, NEG_INF)
    m = jnp.maximum(jnp.max(sc, axis=1, keepdims=True), snk)
    e = jnp.exp(sc - m)
    es = jnp.exp(snk - m)
    inv = 1.0 / (jnp.sum(e, axis=1, keepdims=True) + es)
    return e * inv, es * inv


def _stack_heads(x2, m0):
    parts = []
    for pr in range(2):
        xp = x2[:, pr * LANES:(pr + 1) * LANES]
        parts += [jnp.where(m0, xp, 0.0), jnp.where(m0, 0.0, xp)]
    return jnp.concatenate(parts, axis=0).astype(BF16)


def _unstack_pair(x_all, pr, m0):
    return jnp.where(m0, x_all[(2 * pr) * CHUNK:(2 * pr + 1) * CHUNK], x_all[(2 * pr + 1) * CHUNK:(2 * pr + 2) * CHUNK])


def _attn_saved_specs(bsz, n_blk):
    specs = [pl.BlockSpec((None, None, n_blk, 4 * CHUNK, w), lambda b, g: (b, g, 0, 0, 0)) for w in (3 * CHUNK, 1)]
    shapes = [jax.ShapeDtypeStruct((bsz, 2, n_blk, 4 * CHUNK, 3 * CHUNK), BF16),
              jax.ShapeDtypeStruct((bsz, 2, n_blk, 4 * CHUNK, 1), F32)]
    return specs, shapes


def _attn_fwd(u3, slopes, sink, rider=None):
    bsz, s, _ = u3.shape
    n_blk = s // CHUNK

    def body(slope_ref, sink_ref, q_ref, k_ref, v_ref, o_ref, p_ref, ps_ref, kp_ref, vp_ref):
        g = pl.program_id(1)
        _fill_padded(kp_ref, _dup_kv_head(k_ref[...], g), s)
        _fill_padded(vp_ref, _dup_kv_head(v_ref[...], g), s)
        m0 = lax.broadcasted_iota(jnp.int32, (CHUNK, LANES), 1) < HEAD_DIM

        def blk(n, carry):
            r0 = pl.multiple_of(n * CHUNK, CHUNK)
            kw = kp_ref[pl.ds(r0, 3 * CHUNK), :]
            vw = vp_ref[pl.ds(r0, 3 * CHUNK), :]
            dist, valid = _attn_window_tables(n, s)
            q_all = _stack_heads(q_ref[pl.ds(r0, CHUNK), :].astype(F32) * 0.125, m0)
            sc_all = _dot_nt(q_all, kw)
            probs, sinks = [], []
            for i in range(4):
                p, ps = _attn_probs(sc_all[i * CHUNK:(i + 1) * CHUNK], slope_ref[g * 4 + i], sink_ref[g * 4 + i],
                                    dist, valid)
                probs.append(p.astype(BF16))
                sinks.append(ps)
            p_all = jnp.concatenate(probs, axis=0)
            p_ref[n] = p_all
            ps_ref[n] = jnp.concatenate(sinks, axis=0)
            out_all = _dot(p_all, vw)
            for pr in range(2):
                o_ref[pl.ds(r0, CHUNK), pr * LANES:(pr + 1) * LANES] = _unstack_pair(out_all, pr, m0).astype(BF16)
            return carry

        lax.fori_loop(0, n_blk, blk, 0)

    q, k, v, grp, smem = _attn_specs(s)
    saved_specs, saved_shapes = _attn_saved_specs(bsz, n_blk)
    return _hosted_call(
        body, "attn_fwd", (bsz, 2),
        in_specs=[smem, smem, q, k, v],
        out_specs=[grp] + saved_specs,
        out_shape=[jax.ShapeDtypeStruct((bsz, s, ATTN_WIDTH), BF16)] + saved_shapes,
        scratch_shapes=[pltpu.VMEM((s + 2 * CHUNK, LANES), BF16), pltpu.VMEM((s + 2 * CHUNK, LANES), BF16)],
        operands=(slopes, sink, u3, u3, u3), rider=rider)


def _attn_bwd(u3, d_o, probs, sink_probs, rider=None):
    bsz, s, _ = u3.shape
    n_blk = s // CHUNK

    def body(q_ref, k_ref, v_ref, do_ref, p_ref, ps_ref, dq_ref, dkv_ref, ds_ref,
             kp_ref, vp_ref, dk_acc, dv_acc):
        g = pl.program_id(1)
        _fill_padded(kp_ref, _dup_kv_head(k_ref[...], g), s)
        _fill_padded(vp_ref, _dup_kv_head(v_ref[...], g), s)
        dk_acc[...] = jnp.zeros_like(dk_acc)
        dv_acc[...] = jnp.zeros_like(dv_acc)
        m0 = lax.broadcasted_iota(jnp.int32, (CHUNK, LANES), 1) < HEAD_DIM

        def blk(n, dsink):
            r0 = pl.multiple_of(n * CHUNK, CHUNK)
            win = pl.ds(r0, 3 * CHUNK)
            kw = kp_ref[win, :]
            vw = vp_ref[win, :]
            q_all = _stack_heads(q_ref[pl.ds(r0, CHUNK), :].astype(F32) * 0.125, m0)
            do_all = _stack_heads(do_ref[pl.ds(r0, CHUNK), :].astype(F32), m0)
            p_all = p_ref[n]
            ps_all = ps_ref[n]
            dp_all = _dot_nt(do_all, vw)
            new_dsink, dscs = [], []
            for i in range(4):
                rows = slice(i * CHUNK, (i + 1) * CHUNK)
                p = p_all[rows].astype(F32)
                dp = dp_all[rows]
                delta = jnp.sum(p * dp, axis=1, keepdims=True)
                dscs.append((p * (dp - delta)).astype(BF16))
                dsh = jnp.sum(ps_all[rows] * delta, axis=0, keepdims=True)
                new_dsink.append(dsink[i] - jnp.broadcast_to(dsh, (1, LANES)))
            dsc_all = jnp.concatenate(dscs, axis=0)
            dq_all = _dot(dsc_all, kw)
            dk_acc[win, :] += _dot_tn(dsc_all, q_all)
            dv_acc[win, :] += _dot_tn(p_all, do_all)
            for pr in range(2):
                dq_ref[pl.ds(r0, CHUNK), pr * LANES:(pr + 1) * LANES] = (
                    _unstack_pair(dq_all, pr, m0) * 0.125).astype(BF16)
            return tuple(new_dsink)

        dsink = lax.fori_loop(0, n_blk, blk, tuple(jnp.zeros((1, LANES), F32) for _ in range(4)))
        dk = dk_acc[CHUNK:CHUNK + s, :]
        dv = dv_acc[CHUNK:CHUNK + s, :]
        lane = lax.broadcasted_iota(jnp.int32, (s, LANES), 1)
        fold = lambda a: a + pltpu.roll(a, HEAD_DIM, 1)
        dkv_ref[...] = jnp.where(lane < HEAD_DIM, fold(dk), fold(dv)).astype(BF16)
        ds_ref[...] = jnp.zeros_like(ds_ref)
        for i in range(4):
            ds_ref[i:i + 1, :] = dsink[i]

    q, k, v, grp, _ = _attn_specs(s)
    return _hosted_call(
        body, "attn_bwd", (bsz, 2),
        in_specs=[q, k, v, grp] + _attn_saved_specs(bsz, n_blk)[0],
        out_specs=[grp, pl.BlockSpec((None, s, LANES), lambda b, g: (b, 0, g)),
                   pl.BlockSpec((None, None, 8, LANES), lambda b, g: (b, g, 0, 0))],
        out_shape=[jax.ShapeDtypeStruct((bsz, s, ATTN_WIDTH), BF16), jax.ShapeDtypeStruct((bsz, s, 2 * LANES), BF16),
                   jax.ShapeDtypeStruct((bsz, 2, 8, LANES), F32)],
        scratch_shapes=[pltpu.VMEM((s + 2 * CHUNK, LANES), BF16), pltpu.VMEM((s + 2 * CHUNK, LANES), BF16),
                        pltpu.VMEM((s + 2 * CHUNK, LANES), F32), pltpu.VMEM((s + 2 * CHUNK, LANES), F32)],
        operands=(u3, u3, u3, d_o, probs, sink_probs), rider=rider)


def _ffn_bwd(dz2, gs, us, pg, ple, zh1, r1, g1, wg4, wu4, wd4, wpg, w_out):
    t = dz2.shape[0]
    tm = 256

    def body(dz_ref, gs_ref, us_ref, pg_ref, ple_ref, zh_ref, r_ref, g1_ref,
             wg_hbm, wu_hbm, wd_hbm, wpg_hbm, wo_hbm,
             dgs_ref, dus_ref, dsp_ref, dple_ref, dz1_ref, dyr_ref, dya_ref, dg1_ref, db1_ref,
             wg, wu, wd, wpg, wo, wsem):
        step = pl.program_id(0)
        loads = [(h.at[j], v.at[j]) for j in range(N_SHARD) for h, v in ((wd_hbm, wd), (wg_hbm, wg), (wu_hbm, wu))]
        _load_resident(step, loads + [(wpg_hbm, wpg), (wo_hbm, wo)], wsem)

        @pl.when(step == 0)
        def _():
            dg1_ref[...] = jnp.zeros_like(dg1_ref)
            db1_ref[...] = jnp.zeros_like(db1_ref)

        dz = dz_ref[...]
        dzb = dz.astype(BF16)
        dh = ALPHA * dz
        pending = []
        for j in range(N_SHARD + 1):
            if j < N_SHARD:
                da = _dot_nt(dzb, wd[j])
                gj = gs_ref[j].astype(F32)
                uj = us_ref[j].astype(F32)
                sg = _sigmoid(gj)
                dgj = (da * uj * sg * (1.0 + gj * (1.0 - sg))).astype(BF16)
                duj = (da * gj * sg).astype(BF16)
                dgs_ref[j] = dgj
                dus_ref[j] = duj
                pending.append((dgj, duj))
            if j > 0:
                dgp, dup = pending[j - 1]
                dh = dh + _dot(dgp, wg[j - 1]) + _dot(dup, wu[j - 1])
        pgv = pg_ref[...].astype(F32)
        plev = ple_ref[...].astype(F32)
        dple_ref[...] = (dz * pgv).astype(BF16)
        dsp = (dz * plev * pgv * (1.0 - pgv)).astype(BF16)
        dsp_ref[...] = dsp
        dh = dh + _dot_nt(dsp, wpg[...])
        zh = zh_ref[...]
        dg1_ref[...] += jnp.sum(dh * zh, axis=0, keepdims=True)
        db1_ref[...] += jnp.sum(dh, axis=0, keepdims=True)
        dzh = dh * g1_ref[...]
        m1 = jnp.mean(dzh, axis=1, keepdims=True)
        m2 = jnp.mean(dzh * zh, axis=1, keepdims=True)
        dz1 = r_ref[...] * (dzh - m1 - zh * m2)
        dz1_ref[...] = dz1
        dyc = _dot_nt(dz1.astype(BF16), wo[...])
        dyr_ref[...] = dyc[:, 0:RET_WIDTH].astype(BF16)
        dya_ref[...] = dyc[:, RET_WIDTH:].astype(BF16)

    row = lambda w: pl.BlockSpec((tm, w), lambda i: (i, 0))
    const = lambda s: pl.BlockSpec(s, lambda i: (0, 0))
    sh = pl.BlockSpec((N_SHARD, tm, FFN_SHARD), lambda i: (0, i, 0))
    hbm = pl.BlockSpec(memory_space=pl.ANY)
    sh_shape = jax.ShapeDtypeStruct((N_SHARD, t, FFN_SHARD), BF16)
    return pl.pallas_call(
        body, name="ffn_bwd", grid=(t // tm,),
        in_specs=[row(D_MODEL), sh, sh, row(D_MODEL), row(D_MODEL), row(D_MODEL), row(1), const((1, D_MODEL)),
                  hbm, hbm, hbm, hbm, hbm],
        out_specs=[sh, sh, row(D_MODEL), row(D_MODEL), row(D_MODEL), row(RET_WIDTH), row(ATTN_WIDTH),
                   const((1, D_MODEL)), const((1, D_MODEL))],
        out_shape=[sh_shape, sh_shape, jax.ShapeDtypeStruct((t, D_MODEL), BF16),
                   jax.ShapeDtypeStruct((t, D_MODEL), BF16), jax.ShapeDtypeStruct((t, D_MODEL), F32),
                   jax.ShapeDtypeStruct((t, RET_WIDTH), BF16), jax.ShapeDtypeStruct((t, ATTN_WIDTH), BF16),
                   jax.ShapeDtypeStruct((1, D_MODEL), F32), jax.ShapeDtypeStruct((1, D_MODEL), F32)],
        scratch_shapes=[pltpu.VMEM(wg4.shape, BF16), pltpu.VMEM(wu4.shape, BF16), pltpu.VMEM(wd4.shape, BF16),
                        pltpu.VMEM(wpg.shape, BF16), pltpu.VMEM(w_out.shape, BF16),
                        pltpu.SemaphoreType.DMA((3 * N_SHARD + 2,))],
        compiler_params=_params("arbitrary", vmem=VMEM_LIMIT),
    )(dz2, gs, us, pg, ple, zh1, r1, g1, wg4, wu4, wd4, wpg, w_out)


def _wgrad_misc(y_ret, y_att, dz1, hb, dsp, p2d, dple, rider=None):
    t = dz1.shape[0]
    tk = min(t, 512)

    def body(yr_ref, ya_ref, dz_ref, hb_ref, dsp_ref, p_ref, dple_ref, wo_ref, wpg_ref, wpe_ref):
        @pl.when(pl.program_id(0) == 0)
        def _():
            wo_ref[...] = jnp.zeros_like(wo_ref)
            wpg_ref[...] = jnp.zeros_like(wpg_ref)
            wpe_ref[...] = jnp.zeros_like(wpe_ref)

        dzb = dz_ref[...].astype(BF16)
        wo_ref[0:RET_WIDTH, :] += _dot_tn(yr_ref[...], dzb)
        wo_ref[RET_WIDTH:, :] += _dot_tn(ya_ref[...], dzb)
        wpg_ref[...] += _dot_tn(hb_ref[...], dsp_ref[...])
        wpe_ref[...] += _dot_tn(p_ref[...].astype(BF16), dple_ref[...])

    row = lambda w: pl.BlockSpec((tk, w), lambda k: (k, 0))
    const = lambda s: pl.BlockSpec(s, lambda k: (0, 0))
    return _hosted_call(
        body, "wgrad_misc", (t // tk,),
        in_specs=[row(RET_WIDTH), row(ATTN_WIDTH), row(D_MODEL), row(D_MODEL), row(D_MODEL), row(PLE_DIM),
                  row(D_MODEL)],
        out_specs=[const((D_MODEL, D_MODEL)), const((D_MODEL, D_MODEL)), const((PLE_DIM, D_MODEL))],
        out_shape=[jax.ShapeDtypeStruct((D_MODEL, D_MODEL), F32), jax.ShapeDtypeStruct((D_MODEL, D_MODEL), F32),
                   jax.ShapeDtypeStruct((PLE_DIM, D_MODEL), F32)],
        scratch_shapes=[], operands=(y_ret, y_att, dz1, hb, dsp, p2d, dple), rider=rider, semantics=["arbitrary"])


def _wgrad_ffn(acts, dgs, dus, hb, dz2b):
    t = dz2b.shape[0]
    tk = min(t, 512)
    nk = t // tk

    def body(act_ref, dg_ref, du_ref, hb_ref, dz_ref, og_ref, ou_ref, od_ref, ogb_ref, oub_ref, odb_ref):
        @pl.when(pl.program_id(1) == 0)
        def _():
            og_ref[...] = jnp.zeros_like(og_ref)
            ou_ref[...] = jnp.zeros_like(ou_ref)
            od_ref[...] = jnp.zeros_like(od_ref)

        hbv = hb_ref[...]
        og_ref[...] += _dot_tn(dg_ref[...], hbv)
        ou_ref[...] += _dot_tn(du_ref[...], hbv)
        od_ref[...] += _dot_tn(act_ref[...], dz_ref[...])

        @pl.when(pl.program_id(1) == nk - 1)
        def _():
            ogb_ref[...] = og_ref[...].astype(BF16)
            oub_ref[...] = ou_ref[...].astype(BF16)
            odb_ref[...] = od_ref[...].astype(BF16)

    a_spec = pl.BlockSpec((None, tk, FFN_SHARD), lambda j, k: (j, k, 0))
    b_spec = pl.BlockSpec((tk, D_MODEL), lambda j, k: (k, 0))
    o_spec = pl.BlockSpec((None, FFN_SHARD, D_MODEL), lambda j, k: (j, 0, 0))
    shape = lambda dt: jax.ShapeDtypeStruct((N_SHARD, FFN_SHARD, D_MODEL), dt)
    return pl.pallas_call(
        body, name="wgrad_ffn", grid=(N_SHARD, nk),
        in_specs=[a_spec, a_spec, a_spec, b_spec, b_spec],
        out_specs=[o_spec] * 6, out_shape=[shape(F32)] * 3 + [shape(BF16)] * 3,
        compiler_params=_params("parallel", "arbitrary", vmem=VMEM_LIMIT),
    )(acts, dgs, dus, hb, dz2b)


KV_ORDER = (0, 128, 64, 192)


def _wgrad_in(pieces, x2d):
    t = x2d.shape[0]
    tk = min(t, 512)
    nk = t // tk
    kv0 = CB_AK * LANES

    def body(p0, p1, p2, p3, p4, pkv, x_ref, o_ref, ob_ref):
        @pl.when(pl.program_id(0) == 0)
        def _():
            o_ref[...] = jnp.zeros_like(o_ref)

        xb = x_ref[...].astype(BF16)
        for i, ref in enumerate((p0, p1, p2, p3, p4)):
            o_ref[i * 512:(i + 1) * 512, :] += _dot_tn(ref[...], xb)
        dkv = _dot_tn(pkv[...], xb)
        for i, o in enumerate(KV_ORDER):
            o_ref[kv0 + o:kv0 + o + HEAD_DIM, :] += dkv[i * HEAD_DIM:(i + 1) * HEAD_DIM]

        @pl.when(pl.program_id(0) == nk - 1)
        def _():
            ob_ref[...] = o_ref[...].astype(BF16)

    row = lambda w: pl.BlockSpec((tk, w), lambda k: (k, 0))
    whole = pl.BlockSpec((IN_WIDTH, D_MODEL), lambda k: (0, 0))
    return pl.pallas_call(
        body, name="wgrad_in", grid=(nk,),
        in_specs=[row(512)] * 5 + [row(256), row(D_MODEL)],
        out_specs=[whole, whole],
        out_shape=[jax.ShapeDtypeStruct((IN_WIDTH, D_MODEL), F32), jax.ShapeDtypeStruct((IN_WIDTH, D_MODEL), BF16)],
        compiler_params=_params("arbitrary", vmem=VMEM_LIMIT),
    )(*pieces, x2d)


def _inproj_bwd(dz1, pieces, w_main, w_kv, rider=None):
    t = dz1.shape[0]
    tm = 512

    def body(dz_ref, p0, p1, p2, p3, p4, pkv, wm_ref, wkv_ref, o_ref):
        acc = ALPHA * dz_ref[...]
        for i, ref in enumerate((p0, p1, p2, p3, p4)):
            acc = acc + _dot(ref[...], wm_ref[i * 512:(i + 1) * 512, :])
        o_ref[...] = acc + _dot(pkv[...], wkv_ref[...])

    row = lambda w: pl.BlockSpec((tm, w), lambda i: (i, 0))
    const = lambda s: pl.BlockSpec(s, lambda i: (0, 0))
    return _hosted_call(
        body, "inproj_bwd", (t // tm,),
        in_specs=[row(D_MODEL)] + [row(512)] * 5 + [row(256), const(w_main.shape), const(w_kv.shape)],
        out_specs=[row(D_MODEL)],
        out_shape=[jax.ShapeDtypeStruct((t, D_MODEL), F32)],
        scratch_shapes=[], operands=(dz1, *pieces, w_main, w_kv), rider=rider)


def _coords():
    return lax.axis_index("x"), lax.axis_index("y"), lax.axis_index("c")


def _chip_of(x, y, rel):
    return (1 - x if rel & 2 else x), (1 - y if rel & 1 else y)


def _all_gather_weights(shards):
    first = _gather_chips_rider(shards)
    second = _gather_pass_rider([jax.ShapeDtypeStruct((N_SHARD,) + s.shape, s.dtype) for s in shards], chained=True)
    return _run_riders("gather_weights", shards, first.out_shapes, [first, second])


def _run_riders(name, ins, out_shapes, riders):
    n_in, n_out = len(ins), len(out_shapes)

    def body(*refs):
        in_refs, out_refs = refs[:n_in], refs[n_in:n_in + n_out]
        k = n_in + n_out
        for r in riders:
            sems = refs[k:k + len(r.sems)]
            k += len(r.sems)
            r.start(in_refs, out_refs, sems)
            r.finish(in_refs, out_refs, sems)

    hbm = pl.BlockSpec(memory_space=pl.ANY)
    return pl.pallas_call(
        body, name=name, in_specs=[hbm] * n_in, out_specs=[hbm] * n_out, out_shape=list(out_shapes),
        scratch_shapes=[s for r in riders for s in r.sems],
    )(*ins)


def _gather_half(outs, w, chip, cc):
    h = outs[w].shape[1] // 2
    return outs[w].at[chip, pl.ds(cc * h, h), :]


def _gather_chips_rider(shards):
    nw = len(shards)

    def copies(ins, outs, sems, arrivals):
        send, recv, lsend, lrecv = sems
        x, y, c = _coords()
        me = 2 * x + y
        own = [pltpu.make_async_remote_copy(
            src_ref=ins[w], dst_ref=outs[w].at[me], send_sem=lsend.at[w], recv_sem=lrecv.at[w],
            device_id=(x, y, 1 - c), device_id_type=MESH) for w in range(nw)]
        out, arrive = [], []
        for rel in (1, 2, 3):
            kx, ky = _chip_of(x, y, rel)
            for w in range(nw):
                h = shards[w].shape[0] // 2
                sem = dict(send_sem=send.at[w * 3 + rel - 1], recv_sem=recv.at[w * 3 + rel - 1],
                           device_id=(kx, ky, c), device_id_type=MESH)
                out.append(pltpu.make_async_remote_copy(
                    src_ref=ins[w].at[pl.ds(c * h, h), :], dst_ref=_gather_half(outs, w, me, c), **sem))
                if arrivals:
                    theirs = _gather_half(outs, w, 2 * kx + ky, c)
                    arrive.append(pltpu.make_async_remote_copy(src_ref=theirs, dst_ref=theirs, **sem))
        return own, out, arrive

    def start(ins, outs, sems):
        own, out, _ = copies(ins, outs, sems, arrivals=False)
        for cp in own + out:
            cp.start()

    def finish(ins, outs, sems):
        own, out, arrive = copies(ins, outs, sems, arrivals=True)
        for cp in arrive:
            cp.wait_recv()
        for cp in out:
            cp.wait_send()
        for cp in own:
            cp.wait()

    dma = pltpu.SemaphoreType.DMA
    return _Rider(shards, [jax.ShapeDtypeStruct((N_SHARD,) + s.shape, s.dtype) for s in shards],
                  [dma((3 * nw,)), dma((3 * nw,)), dma((nw,)), dma((nw,))], start, finish)


def _gather_pass_rider(gathered, chained=False):
    nw = len(gathered)

    def copies(outs, sems, cc):
        send, recv = sems
        x, y, c = _coords()
        res = []
        for rel in (1, 2, 3):
            kx, ky = _chip_of(x, y, rel)
            for w in range(nw):
                rows = _gather_half(outs, w, 2 * kx + ky, cc)
                res.append(pltpu.make_async_remote_copy(
                    src_ref=rows, dst_ref=rows, send_sem=send.at[w * 3 + rel - 1], recv_sem=recv.at[w * 3 + rel - 1],
                    device_id=(x, y, 1 - c), device_id_type=MESH))
        return res

    def start(ins, outs, sems):
        for cp in copies(outs, sems, lax.axis_index("c")):
            cp.start()

    def finish(ins, outs, sems):
        c = lax.axis_index("c")
        for cp in copies(outs, sems, 1 - c):
            cp.wait_recv()
        for cp in copies(outs, sems, c):
            cp.wait_send()

    dma = pltpu.SemaphoreType.DMA
    shapes = [jax.ShapeDtypeStruct(g.shape, g.dtype) for g in gathered]
    if chained:
        return _Rider([], [], [dma((3 * nw,)), dma((3 * nw,))], start, finish)
    return _Rider(gathered, shapes, [dma((3 * nw,)), dma((3 * nw,))], start, finish,
                  aliases={w: w for w in range(nw)})


def _exchange_halves_rider(parts):
    nw = len(parts)

    def copies(ins, outs, sems):
        send, recv = sems
        x, y, c = _coords()
        res = []
        for w in range(nw):
            h = parts[w].shape[1] // 2
            res.append(pltpu.make_async_remote_copy(
                src_ref=ins[w].at[:, pl.ds((1 - c) * h, h), :], dst_ref=outs[w],
                send_sem=send.at[w], recv_sem=recv.at[w], device_id=(x, y, 1 - c), device_id_type=MESH))
        return res

    def start(ins, outs, sems):
        for cp in copies(ins, outs, sems):
            cp.start()

    def finish(ins, outs, sems):
        for cp in copies(ins, outs, sems):
            cp.wait()

    dma = pltpu.SemaphoreType.DMA
    return _Rider(parts, [jax.ShapeDtypeStruct((N_SHARD, p.shape[1] // 2, p.shape[2]), p.dtype) for p in parts],
                  [dma((nw,)), dma((nw,))], start, finish)


def _add_halves(parts, theirs, pos):
    nw = len(parts)
    split = 2

    def body(pos_ref, *refs):
        ins, oth = refs[:nw], refs[nw:2 * nw]
        o32, o16 = refs[2 * nw:3 * nw], refs[3 * nw:]
        sums = [ins[w][...] + oth[w][...].astype(F32) for w in range(nw)]
        for w in range(nw):
            o16[w][...] = sums[w].astype(BF16)

        @pl.when(pl.program_id(1) == pos_ref[0])
        def _():
            for w in range(nw):
                o32[w][...] = sums[w]

    in_specs, oth_specs, o32_specs, shapes32, shapes16 = [], [], [], [], []
    for p in parts:
        hb = p.shape[1] // 2 // split
        blk = (None, hb, p.shape[2])
        in_specs.append(pl.BlockSpec(blk, lambda i, j, pos_ref: (j, pos_ref[1] * split + i, 0)))
        oth_specs.append(pl.BlockSpec(blk, lambda i, j, pos_ref: (j, i, 0)))
        o32_specs.append(pl.BlockSpec((hb, p.shape[2]), lambda i, j, pos_ref: (i, 0)))
        shapes32.append(jax.ShapeDtypeStruct((p.shape[1] // 2, p.shape[2]), F32))
        shapes16.append(jax.ShapeDtypeStruct((N_SHARD, p.shape[1] // 2, p.shape[2]), BF16))
    return pl.pallas_call(
        body, name="add_halves",
        grid_spec=pltpu.PrefetchScalarGridSpec(
            num_scalar_prefetch=1, grid=(split, N_SHARD),
            in_specs=in_specs + oth_specs, out_specs=o32_specs + oth_specs),
        out_shape=shapes32 + shapes16,
        compiler_params=_params("parallel", "arbitrary", vmem=VMEM_LIMIT),
    )(pos, *parts, *theirs)


def _exchange_chips_rider(sums16):
    nw = len(sums16)

    def copies(ins, outs, sems):
        send, recv = sems
        x, y, c = _coords()
        res = []
        for rel in (1, 2, 3):
            kx, ky = _chip_of(x, y, rel)
            for w in range(nw):
                res.append(pltpu.make_async_remote_copy(
                    src_ref=ins[w].at[2 * kx + ky], dst_ref=outs[w].at[rel - 1],
                    send_sem=send.at[w * 3 + rel - 1], recv_sem=recv.at[w * 3 + rel - 1],
                    device_id=(kx, ky, c), device_id_type=MESH))
        return res

    def start(ins, outs, sems):
        for cp in copies(ins, outs, sems):
            cp.start()

    def finish(ins, outs, sems):
        for cp in copies(ins, outs, sems):
            cp.wait()

    dma = pltpu.SemaphoreType.DMA
    return _Rider(sums16, [jax.ShapeDtypeStruct((3,) + s.shape[1:], BF16) for s in sums16],
                  [dma((3 * nw,)), dma((3 * nw,))], start, finish)


def _add_chips(sums32, theirs, pos):
    nw = len(sums32)
    split = 2

    def body(pos_ref, *refs):
        ins, oth, outs = refs[:nw], refs[nw:2 * nw], refs[2 * nw:]
        for w in range(nw):
            acc = ins[w][...]
            for r in range(3):
                acc = acc + oth[w][r].astype(F32)
            outs[w][...] = acc

    in_specs, oth_specs, out_specs, shapes = [], [], [], []
    for s in sums32:
        hb = s.shape[0] // split
        in_specs.append(pl.BlockSpec((hb, s.shape[1]), lambda i, pos_ref: (i, 0)))
        oth_specs.append(pl.BlockSpec((3, hb, s.shape[1]), lambda i, pos_ref: (0, i, 0)))
        out_specs.append(pl.BlockSpec((hb, s.shape[1]), lambda i, pos_ref: (pos_ref[1] * split + i, 0)))
        shapes.append(jax.ShapeDtypeStruct((2 * s.shape[0], s.shape[1]), F32))
    return pl.pallas_call(
        body, name="add_chips",
        grid_spec=pltpu.PrefetchScalarGridSpec(
            num_scalar_prefetch=1, grid=(split,), in_specs=in_specs + oth_specs, out_specs=out_specs),
        out_shape=shapes,
        compiler_params=_params("parallel", vmem=VMEM_LIMIT),
    )(pos, *sums32, *theirs)


def _join_halves(shards):
    nw = len(shards)

    def body(*refs):
        outs = refs[nw:2 * nw]
        send, recv = refs[2 * nw:]
        x, y, c = _coords()

        def copy(w, cc):
            h = shards[w].shape[0] // 2
            rows = outs[w].at[pl.ds(cc * h, h), :]
            return pltpu.make_async_remote_copy(
                src_ref=rows, dst_ref=rows, send_sem=send.at[w], recv_sem=recv.at[w],
                device_id=(x, y, 1 - c), device_id_type=MESH)

        for w in range(nw):
            copy(w, c).start()
        for w in range(nw):
            copy(w, 1 - c).wait_recv()
            copy(w, c).wait_send()

    hbm = pl.BlockSpec(memory_space=pl.ANY)
    return pl.pallas_call(
        body, name="join_halves",
        in_specs=[hbm] * nw, out_specs=[hbm] * nw,
        out_shape=[jax.ShapeDtypeStruct(s.shape, F32) for s in shards],
        input_output_aliases={w: w for w in range(nw)},
        scratch_shapes=[pltpu.SemaphoreType.DMA((nw,)), pltpu.SemaphoreType.DMA((nw,))],
    )(*shards)


def _adamw_math(w, g, m, v):
    m = ADAM_B1 * m + (1.0 - ADAM_B1) * g
    v = ADAM_B2 * v + (1.0 - ADAM_B2) * (g * g)
    m_hat = m / (1.0 - ADAM_B1 ** ADAM_STEP)
    v_hat = v / (1.0 - ADAM_B2 ** ADAM_STEP)
    delta = -ADAM_LR * (m_hat / (jnp.sqrt(v_hat) + ADAM_EPS) + ADAM_WD * w)
    return delta, m, v


def _adamw(ws, gs, ms, vs):
    nw = len(ws)
    split = 8

    def body(*refs):
        w_r, g_r, m_r, v_r = (refs[i * nw:(i + 1) * nw] for i in range(4))
        g_o, d_o, m_o, v_o = (refs[(4 + i) * nw:(5 + i) * nw] for i in range(4))
        for k in range(nw):
            g = g_r[k][...]
            d, m, v = _adamw_math(w_r[k][...], g, m_r[k][...], v_r[k][...])
            g_o[k][...] = g
            d_o[k][...] = d
            m_o[k][...] = m
            v_o[k][...] = v

    specs = [pl.BlockSpec((w.shape[0] // split, w.shape[1]), lambda i: (i, 0)) for w in ws]
    shapes = [jax.ShapeDtypeStruct(w.shape, F32) for w in ws]
    outs = pl.pallas_call(
        body, name="adamw", grid=(split,),
        in_specs=specs * 4, out_specs=specs * 4, out_shape=shapes * 4,
        compiler_params=_params("parallel", vmem=VMEM_LIMIT),
    )(*ws, *gs, *ms, *vs)
    return outs[:nw], outs[nw:2 * nw], outs[2 * nw:3 * nw], outs[3 * nw:]


SMALL_ROWS = 8
SMALL_COLS = D_MODEL
LOSS_COL = RET_WIDTH + 24


def _small_allreduce_adamw(part, w, m, v, rider=None):
    def body(part_ref, w_ref, m_ref, v_ref, g_out, d_out, m_out, v_out, all_ref, send, recv):
        x, y, c = _coords()
        me = 4 * x + 2 * y + c
        all_ref[me] = part_ref[...]
        copies = []
        for rel in range(1, 8):
            px = 1 - x if rel & 4 else x
            py = 1 - y if rel & 2 else y
            pc = 1 - c if rel & 1 else c
            copies.append(pltpu.make_async_remote_copy(
                src_ref=part_ref, dst_ref=all_ref.at[me],
                send_sem=send.at[rel - 1], recv_sem=recv.at[rel - 1], device_id=(px, py, pc), device_id_type=MESH))
        for cp in copies:
            cp.start()
        for cp in copies:
            cp.wait()
        g = all_ref[0]
        for k in range(1, 8):
            g = g + all_ref[k]
        d, mn, vn = _adamw_math(w_ref[...], g, m_ref[...], v_ref[...])
        g_out[...] = g
        d_out[...] = d
        m_out[...] = mn
        v_out[...] = vn

    vm = pl.BlockSpec(memory_space=pltpu.VMEM)
    shape = jax.ShapeDtypeStruct((SMALL_ROWS, SMALL_COLS), F32)
    return _hosted_call(
        body, "small_allreduce_adamw", (1,),
        in_specs=[vm] * 4, out_specs=[vm] * 4, out_shape=[shape] * 4,
        scratch_shapes=[pltpu.VMEM((8, SMALL_ROWS, SMALL_COLS), F32),
                        pltpu.SemaphoreType.DMA((7,)), pltpu.SemaphoreType.DMA((7,))],
        operands=(part, w, m, v), rider=rider, semantics=["arbitrary"])


SMALL_NAMES = ("ret_decay_fwd", "ret_decay_bwd", "attn_sink", "ret_gn_gain",
               "ln1_gain", "ln1_bias", "ln2_gain", "ln2_bias")


LN_NAMES = ("ln1_gain", "ln1_bias", "ln2_gain", "ln2_bias")


def _pack_small(vals, extra=None):
    tail = jnp.zeros((1, 1), F32) if extra is None else extra.reshape(1, 1)
    row4 = jnp.concatenate([vals["ret_gn_gain"], vals["ret_decay_fwd"], vals["ret_decay_bwd"], vals["attn_sink"],
                            tail, jnp.zeros((1, SMALL_COLS - LOSS_COL - 1), F32)], axis=1)
    rows = [vals[n] for n in LN_NAMES] + [row4, jnp.zeros((SMALL_ROWS - 5, SMALL_COLS), F32)]
    return jnp.concatenate(rows, axis=0)


def _unpack_small(packed):
    out = {n: packed[i:i + 1] for i, n in enumerate(LN_NAMES)}
    o = RET_WIDTH
    out.update(ret_gn_gain=packed[4:5, 0:o], ret_decay_fwd=packed[4:5, o:o + 8],
               ret_decay_bwd=packed[4:5, o + 8:o + 16], attn_sink=packed[4:5, o + 16:o + 24])
    return out


def _local_step(x, p, tgt, w_in_t, rest, small, pos=None, small_state=None):
    bsz, s, _ = x.shape
    t = bsz * s
    x2d = x.reshape(t, D_MODEL)
    p2d = p.reshape(t, PLE_DIM)
    tgt2d = tgt.reshape(t, D_MODEL)
    dec_f = small["ret_decay_fwd"].reshape(8)
    dec_b = small["ret_decay_bwd"].reshape(8)
    lg_f = jnp.log1p(-jnp.exp2(dec_f))
    lg_b = jnp.log1p(-jnp.exp2(dec_b))
    per_lane = lambda v: jnp.repeat(v, HEAD_DIM).reshape(4, 1, LANES)
    lgf_l, lgb_l = per_lane(lg_f), per_lane(lg_b)
    sink = small["attn_sink"].reshape(8)
    slopes = 2.0 ** (-(jnp.arange(8, dtype=F32) + 1.0))
    gn_gain = small["ret_gn_gain"]
    g1, b1, g2, b2 = (small[n] for n in ("ln1_gain", "ln1_bias", "ln2_gain", "ln2_bias"))

    dist = pos is not None
    chips = lambda names: _gather_chips_rider([rest[REST_NAMES.index(n)] for n in names])
    first, second, third = ("w_ffn_up",), ("w_out", "w_ffn_gate", "w_ple_proj", "w_ple_gate"), ("w_ffn_down",)
    u, *c1 = _inproj(x2d, w_in_t, rider=chips(first) if dist else None)
    u3 = u.reshape(bsz, s, IN_WIDTH)
    y_hat, y_rstd, y_ret, *o2 = _ret_fwd(u3, lgf_l, lgb_l, gn_gain,
                                 rider=_merge_riders([_gather_pass_rider(c1), chips(second)]) if dist else None)
    y_att, att_p, att_ps, *o3 = _attn_fwd(u3, slopes, sink, rider=_merge_riders(
        [_gather_pass_rider(o2[len(first):]), chips(third)]) if dist else None)
    gathered = dict(zip(first, o2[:len(first)]))
    gathered.update(zip(second, o3[:len(second)]))
    w_out = _assemble_weights({"w_out": gathered["w_out"]})["w_out"] if dist else rest["w_out"]
    zh1, r1, hb, *o4 = _outproj_ln1(y_ret.reshape(t, RET_WIDTH), y_att.reshape(t, ATTN_WIDTH), x2d, w_out, g1, b1,
                                    rider=_gather_pass_rider(o3[len(second):]) if dist else None)
    gathered.update(zip(third, o4))
    wts = _assemble_weights(gathered) if dist else rest
    dz2, dz2b, gs, us, acts, pg, ple, sq, dg2, db2 = _ffn_fwd(
        zh1, hb, p2d, tgt2d, g1, b1, g2, b2, wts["gate4"], wts["up4"], wts["down4"], wts["ple_proj"], wts["ple_gate"])
    dgs, dus, dsp, dple, dz1, dyr, dya, dg1, db1 = _ffn_bwd(dz2, gs, us, pg, ple, zh1, r1, g1, wts["gate4"],
                                                          wts["up4"], wts["down4"], wts["ple_gate"], wts["w_out"])
    *ffn_parts, ffn_g16, ffn_u16, ffn_d16 = _wgrad_ffn(acts, dgs, dus, hb, dz2b)
    d_w_out, d_ple_gate, d_ple_proj, *th_ffn = _wgrad_misc(
        y_ret.reshape(t, RET_WIDTH), y_att.reshape(t, ATTN_WIDTH), dz1, hb, dsp, p2d, dple,
        rider=_exchange_halves_rider([ffn_g16, ffn_u16, ffn_d16]) if dist else None)
    misc_parts = [d_w_out.reshape(N_SHARD, D_MODEL // N_SHARD, D_MODEL),
                  d_ple_proj.reshape(PLE_DIM, N_SHARD, D_MODEL // N_SHARD).transpose(1, 0, 2),
                  d_ple_gate.reshape(N_SHARD, D_MODEL // N_SHARD, D_MODEL)]
    dyr3, dya3 = dyr.reshape(bsz, s, RET_WIDTH), dya.reshape(bsz, s, ATTN_WIDTH)
    if dist:
        s_ffn = _add_halves(ffn_parts, th_ffn, pos)
        drq, drk, drv, drg, rpart, *o5 = _ret_bwd(u3, y_hat, y_rstd, dyr3, lgf_l, lgb_l, gn_gain, rider=_merge_riders(
            [_exchange_chips_rider(s_ffn[3:5]), _exchange_halves_rider(misc_parts)]))
        s_misc = _add_halves(misc_parts, o5[2:], pos)
        daq, dakv, spart, *o6 = _attn_bwd(u3, dya3, att_p, att_ps,
                                          rider=_exchange_chips_rider([s_ffn[5]] + list(s_misc[3:])))
    else:
        drq, drk, drv, drg, rpart = _ret_bwd(u3, y_hat, y_rstd, dyr3, lgf_l, lgb_l, gn_gain)
        daq, dakv, spart = _attn_bwd(u3, dya3, att_p, att_ps)
    pieces = [a.reshape(t, -1) for a in (drq, drk, drv, drg, daq, dakv)]
    kv0 = CB_AK * LANES
    w_kv = jnp.concatenate([w_in_t[kv0 + o:kv0 + o + HEAD_DIM] for o in KV_ORDER], axis=0)
    d_in, d_in16 = (a.reshape(N_SHARD, FFN_SHARD, D_MODEL) for a in _wgrad_in(pieces, x2d))

    rsum = rpart
    lane_heads = lambda row: jnp.sum(row.reshape(4, 2, HEAD_DIM), axis=-1).reshape(8)
    dlg_f = lane_heads(rsum[:, 0, :]) + jnp.stack([jnp.sum(rsum[:, 2, :], -1), jnp.sum(rsum[:, 3, :], -1)], 1).reshape(8)
    dlg_b = lane_heads(rsum[:, 1, :]) + jnp.stack([jnp.sum(rsum[:, 4, :], -1), jnp.sum(rsum[:, 5, :], -1)], 1).reshape(8)
    chain = lambda d: -(math.log(2.0) * jnp.exp2(d)) / (1.0 - jnp.exp2(d))
    grads_small = {
        "ret_decay_fwd": (dlg_f * chain(dec_f)).reshape(1, 8),
        "ret_decay_bwd": (dlg_b * chain(dec_b)).reshape(1, 8),
        "attn_sink": jnp.sum(spart, axis=0)[:, 0:4, 0].reshape(1, 8),
        "ret_gn_gain": rsum[:, 6, :].reshape(1, RET_WIDTH),
        "ln1_gain": dg1, "ln1_bias": db1, "ln2_gain": dg2, "ln2_bias": db2,
    }
    if not dist:
        grad_x, = _inproj_bwd(dz1, pieces, w_in_t[:kv0], w_kv)
        grads_rest = [misc_parts[0]] + ffn_parts + misc_parts[1:]
        return sq[0, 0], grad_x.reshape(bsz, s, D_MODEL), d_in, grads_rest, grads_small
    *small_out, th_in = _small_allreduce_adamw(_pack_small(grads_small, sq[0, 0]), *small_state,
                                               rider=_exchange_halves_rider([d_in16]))
    s_in = _add_halves([d_in], [th_in], pos)
    grad_x, chips_in = _inproj_bwd(dz1, pieces, w_in_t[:kv0], w_kv, rider=_exchange_chips_rider([s_in[1]]))
    sums32 = [s_in[0], s_misc[0], s_ffn[0], s_ffn[1], s_ffn[2], s_misc[1], s_misc[2]]
    from_chips = [chips_in, o6[1], o5[0], o5[1], o6[0], o6[2], o6[3]]
    return grad_x.reshape(bsz, s, D_MODEL), sums32, from_chips, small_out


BIG_NAMES = ("w_in", "w_out", "w_ffn_gate", "w_ffn_up", "w_ffn_down", "w_ple_proj", "w_ple_gate")
REST_NAMES = BIG_NAMES[1:]
TRANSPOSED = ("w_in", "w_ffn_gate", "w_ffn_up")
WEIGHT_ORDER = ("w_in", "ret_decay_fwd", "ret_decay_bwd", "ret_gn_gain", "attn_sink", "w_out", "ln1_gain",
                "ln1_bias", "w_ffn_gate", "w_ffn_up", "w_ffn_down", "w_ple_proj", "w_ple_gate", "ln2_gain", "ln2_bias")


def _shard_rows(name, a):
    return jnp.swapaxes(a[0], 0, 1) if name in TRANSPOSED else a[0]


def _unshard_rows(name, a):
    return (jnp.swapaxes(a, 0, 1) if name in TRANSPOSED else a)[None]


def _assemble_weights(gathered):
    cols = lambda a: a.transpose(1, 0, 2).reshape(a.shape[1], N_SHARD * a.shape[2])
    rows = lambda a: a.reshape(N_SHARD * a.shape[1], a.shape[2])
    same = lambda a: a
    layout = {"w_out": ("w_out", rows), "w_ffn_gate": ("gate4", same), "w_ffn_up": ("up4", same),
              "w_ffn_down": ("down4", same), "w_ple_proj": ("ple_proj", cols), "w_ple_gate": ("ple_gate", rows)}
    return {layout[n][0]: layout[n][1](a) for n, a in gathered.items()}


def kernel(x, p, w_in, ret_decay_fwd, ret_decay_bwd, ret_gn_gain, attn_sink, w_out, ln1_gain, ln1_bias, w_ffn_gate, w_ffn_up, w_ffn_down, w_ple_proj, w_ple_gate, ln2_gain, ln2_bias, loss_target, m_w_in, m_ret_decay_fwd, m_ret_decay_bwd, m_ret_gn_gain, m_attn_sink, m_w_out, m_ln1_gain, m_ln1_bias, m_w_ffn_gate, m_w_ffn_up, m_w_ffn_down, m_w_ple_proj, m_w_ple_gate, m_ln2_gain, m_ln2_bias, v_w_in, v_ret_decay_fwd, v_ret_decay_bwd, v_ret_gn_gain, v_attn_sink, v_w_out, v_ln1_gain, v_ln1_bias, v_w_ffn_gate, v_w_ffn_up, v_w_ffn_down, v_w_ple_proj, v_w_ple_gate, v_ln2_gain, v_ln2_bias):
    w = dict(w_in=w_in, ret_decay_fwd=ret_decay_fwd, ret_decay_bwd=ret_decay_bwd, ret_gn_gain=ret_gn_gain,
             attn_sink=attn_sink, w_out=w_out, ln1_gain=ln1_gain, ln1_bias=ln1_bias, w_ffn_gate=w_ffn_gate,
             w_ffn_up=w_ffn_up, w_ffn_down=w_ffn_down, w_ple_proj=w_ple_proj, w_ple_gate=w_ple_gate,
             ln2_gain=ln2_gain, ln2_bias=ln2_bias)
    m = dict(w_in=m_w_in, ret_decay_fwd=m_ret_decay_fwd, ret_decay_bwd=m_ret_decay_bwd, ret_gn_gain=m_ret_gn_gain,
             attn_sink=m_attn_sink, w_out=m_w_out, ln1_gain=m_ln1_gain, ln1_bias=m_ln1_bias, w_ffn_gate=m_w_ffn_gate,
             w_ffn_up=m_w_ffn_up, w_ffn_down=m_w_ffn_down, w_ple_proj=m_w_ple_proj, w_ple_gate=m_w_ple_gate,
             ln2_gain=m_ln2_gain, ln2_bias=m_ln2_bias)
    v = dict(w_in=v_w_in, ret_decay_fwd=v_ret_decay_fwd, ret_decay_bwd=v_ret_decay_bwd, ret_gn_gain=v_ret_gn_gain,
             attn_sink=v_attn_sink, w_out=v_w_out, ln1_gain=v_ln1_gain, ln1_bias=v_ln1_bias, w_ffn_gate=v_w_ffn_gate,
             w_ffn_up=v_w_ffn_up, w_ffn_down=v_w_ffn_down, w_ple_proj=v_w_ple_proj, w_ple_gate=v_w_ple_gate,
             ln2_gain=v_ln2_gain, ln2_bias=v_ln2_bias)
    big = lambda d: [_shard_rows(n, d[n]) for n in BIG_NAMES]
    small = lambda d: {n: d[n] for n in SMALL_NAMES}

    chip = 2 * lax.axis_index("x") + lax.axis_index("y")
    pos = jnp.stack([chip, lax.axis_index("c")]).astype(jnp.int32)

    shards = [a.astype(BF16) for a in big(w)]
    (w_in4,) = _all_gather_weights(shards[:1])
    w_in_t = w_in4.reshape(IN_WIDTH, D_MODEL)
    grad_x, sums32, from_chips, (g_s, d_s, m_s, v_s) = _local_step(
        x, p[0], loss_target, w_in_t, shards[1:], small(w), pos=pos,
        small_state=(_pack_small(small(w)), _pack_small(small(m)), _pack_small(small(v))))
    g_big, d_big, m_big, v_big = _adamw(big(w), _join_halves(_add_chips(sums32, from_chips, pos)), big(m), big(v))
    loss = g_s[4, LOSS_COL] * (0.5 / D_MODEL)

    def tree(bigs, packed):
        out = {n: _unshard_rows(n, a) for n, a in zip(BIG_NAMES, bigs)}
        out.update(_unpack_small(packed))
        return [out[n] for n in WEIGHT_ORDER]

    return (loss, grad_x, *tree(g_big, g_s), *tree(d_big, d_s), *tree(m_big, m_s), *tree(v_big, v_s))
```

```python
import functools
import math

import jax
import jax.numpy as jnp
from jax import lax
from jax.experimental import pallas as pl
from jax.experimental.pallas import tpu as pltpu

F32 = jnp.float32
BF16 = jnp.bfloat16

D_MODEL = 1024
HEAD_DIM = 64
RET_HEADS = 8
ATTN_HEADS = 8
RET_WIDTH = 512
ATTN_WIDTH = 512
KV_WIDTH = 128
IN_WIDTH = 2816
FFN = 2816
N_SHARD = 4
FFN_SHARD = FFN // N_SHARD
PLE_DIM = 256
CHUNK = 128
LANES = 128
ALPHA = 2.0 ** 0.25
LN_EPS = 1e-5
GN_EPS = 1e-5
NEG_INF = -1e30
ADAM_LR = 0.001
ADAM_B1 = 0.9
ADAM_B2 = 0.999
ADAM_EPS = 1e-08
ADAM_WD = 0.01
ADAM_STEP = 10
VMEM_LIMIT = 56 * 1024 * 1024
MESH = pl.DeviceIdType.MESH

CB_RQ, CB_RK, CB_RV, CB_RG, CB_AQ, CB_AK, CB_AV = 0, 4, 8, 12, 16, 20, 21


def _dot(a, b):
    return jnp.dot(a, b, preferred_element_type=F32)


def _dot_nt(a, b):
    return lax.dot_general(a, b, (((1,), (1,)), ((), ())), preferred_element_type=F32)


def _dot_tn(a, b):
    return lax.dot_general(a, b, (((0,), (0,)), ((), ())), preferred_element_type=F32)


def _sigmoid(x):
    return 1.0 / (1.0 + jnp.exp(-x))


def _params(*sem, vmem=None):
    return pltpu.CompilerParams(dimension_semantics=tuple(sem) if sem else None, vmem_limit_bytes=vmem)


class _Rider:
    def __init__(self, ins, out_shapes, sems, start, finish, aliases=None):
        self.ins, self.out_shapes, self.sems = list(ins), list(out_shapes), list(sems)
        self.start, self.finish, self.aliases = start, finish, dict(aliases or {})


def _merge_riders(riders):
    riders = [r for r in riders if r is not None]
    if len(riders) == 1:
        return riders[0]
    bounds, aliases = [], {}
    i0 = o0 = s0 = 0
    for r in riders:
        bounds.append((i0, o0, s0))
        aliases.update({i0 + i: o0 + o for i, o in r.aliases.items()})
        i0, o0, s0 = i0 + len(r.ins), o0 + len(r.out_shapes), s0 + len(r.sems)

    def each(method):
        def run(ins, outs, sems):
            for r, (i, o, s) in zip(riders, bounds):
                getattr(r, method)(ins[i:i + len(r.ins)], outs[o:o + len(r.out_shapes)], sems[s:s + len(r.sems)])
        return run

    return _Rider([a for r in riders for a in r.ins], [a for r in riders for a in r.out_shapes],
                  [a for r in riders for a in r.sems], each("start"), each("finish"), aliases)


def _hosted_call(body, name, grid, in_specs, out_specs, out_shape, scratch_shapes, operands, rider=None,
                 semantics=None):
    n_in, n_out, n_scr = len(in_specs), len(out_specs), len(scratch_shapes)
    if rider is None:
        return pl.pallas_call(
            body, name=name, grid=grid, in_specs=in_specs, out_specs=out_specs, out_shape=out_shape,
            scratch_shapes=scratch_shapes,
            compiler_params=_params(*(semantics or ["parallel"] * len(grid)), vmem=VMEM_LIMIT))(*operands)
    r_in, r_out = len(rider.ins), len(rider.out_shapes)

    def full_body(*refs):
        main_in, rin = refs[:n_in], refs[n_in:n_in + r_in]
        o0 = n_in + r_in
        main_out, rout = refs[o0:o0 + n_out], refs[o0 + n_out:o0 + n_out + r_out]
        s0 = o0 + n_out + r_out
        main_scr, rsem = refs[s0:s0 + n_scr], refs[s0 + n_scr:]
        first = functools.reduce(jnp.logical_and, [pl.program_id(a) == 0 for a in range(len(grid))])
        last = functools.reduce(jnp.logical_and, [pl.program_id(a) == g - 1 for a, g in enumerate(grid)])

        @pl.when(first)
        def _():
            rider.start(rin, rout, rsem)

        body(*main_in, *main_out, *main_scr)

        @pl.when(last)
        def _():
            rider.finish(rin, rout, rsem)

    hbm = pl.BlockSpec(memory_space=pl.ANY)
    return pl.pallas_call(
        full_body, name=name, grid=grid,
        in_specs=list(in_specs) + [hbm] * r_in, out_specs=list(out_specs) + [hbm] * r_out,
        out_shape=list(out_shape) + rider.out_shapes,
        scratch_shapes=list(scratch_shapes) + rider.sems,
        input_output_aliases={n_in + i: n_out + o for i, o in rider.aliases.items()},
        compiler_params=_params(*(["arbitrary"] * len(grid)), vmem=VMEM_LIMIT),
    )(*operands, *rider.ins)


def _head_mean(x, m0):
    s0 = jnp.sum(jnp.where(m0, x, 0.0), axis=1, keepdims=True)
    s1 = jnp.sum(jnp.where(m0, 0.0, x), axis=1, keepdims=True)
    return jnp.where(m0, s0, s1) * (1.0 / HEAD_DIM)


def _inproj(x2d, w_in_t, rider=None):
    t = x2d.shape[0]
    tm = 512
    nb = 256

    def body(x_ref, w_ref, o_ref):
        xb = x_ref[...].astype(BF16)
        for n in range(0, IN_WIDTH, nb):
            o_ref[:, n:n + nb] = _dot_nt(xb, w_ref[n:n + nb, :]).astype(BF16)

    return _hosted_call(
        body, "inproj", (t // tm,),
        in_specs=[pl.BlockSpec((tm, D_MODEL), lambda i: (i, 0)),
                  pl.BlockSpec((IN_WIDTH, D_MODEL), lambda i: (0, 0))],
        out_specs=[pl.BlockSpec((tm, IN_WIDTH), lambda i: (i, 0))],
        out_shape=[jax.ShapeDtypeStruct((t, IN_WIDTH), BF16)],
        scratch_shapes=[], operands=(x2d, w_in_t), rider=rider)


def _outproj_ln1(y_ret, y_att, x2d, w_out, gain, bias, rider=None):
    t = x2d.shape[0]
    tm = 512

    def body(yr_ref, ya_ref, x_ref, w_ref, g_ref, b_ref, zh_ref, r_ref, hb_ref):
        mix = _dot(yr_ref[...], w_ref[0:RET_WIDTH, :]) + _dot(ya_ref[...], w_ref[RET_WIDTH:, :])
        z = ALPHA * x_ref[...] + mix
        mu = jnp.mean(z, axis=1, keepdims=True)
        zc = z - mu
        var = jnp.mean(zc * zc, axis=1, keepdims=True)
        r = lax.rsqrt(var + LN_EPS)
        zh = zc * r
        zh_ref[...] = zh
        r_ref[...] = r
        hb_ref[...] = (zh * g_ref[...] + b_ref[...]).astype(BF16)

    row = lambda w: pl.BlockSpec((tm, w), lambda i: (i, 0))
    const = lambda s: pl.BlockSpec(s, lambda i: (0, 0))
    return _hosted_call(
        body, "outproj_ln1", (t // tm,),
        in_specs=[row(RET_WIDTH), row(ATTN_WIDTH), row(D_MODEL), const((D_MODEL, D_MODEL)),
                  const((1, D_MODEL)), const((1, D_MODEL))],
        out_specs=[row(D_MODEL), row(1), row(D_MODEL)],
        out_shape=[jax.ShapeDtypeStruct((t, D_MODEL), F32), jax.ShapeDtypeStruct((t, 1), F32),
                   jax.ShapeDtypeStruct((t, D_MODEL), BF16)],
        scratch_shapes=[], operands=(y_ret, y_att, x2d, w_out, gain, bias), rider=rider)


def _load_resident(step, pairs, sems):
    copies = [pltpu.make_async_copy(src, dst, sems.at[i]) for i, (src, dst) in enumerate(pairs)]

    @pl.when(step == 0)
    def _():
        for cp in copies:
            cp.start()
        for cp in copies:
            cp.wait()


def _ffn_fwd(zh1, hb, p2d, tgt, g1, b1, g2, b2, wg4, wu4, wd4, wpe, wpg):
    t = zh1.shape[0]
    tm = 256

    def body(zh_ref, hb_ref, p_ref, t_ref, g1_ref, b1_ref, g2_ref, b2_ref,
             wg_hbm, wu_hbm, wd_hbm, wpe_hbm, wpg_hbm,
             dz_ref, dzb_ref, gs_ref, us_ref, act_ref, pg_ref, ple_ref, loss_ref, dg2_ref, db2_ref,
             wg, wu, wd, wpe, wpg, wsem):
        step = pl.program_id(0)
        loads = [(h.at[j], v.at[j]) for j in range(N_SHARD) for h, v in ((wg_hbm, wg), (wu_hbm, wu), (wd_hbm, wd))]
        _load_resident(step, loads + [(wpe_hbm, wpe), (wpg_hbm, wpg)], wsem)

        @pl.when(step == 0)
        def _():
            loss_ref[...] = jnp.zeros_like(loss_ref)
            dg2_ref[...] = jnp.zeros_like(dg2_ref)
            db2_ref[...] = jnp.zeros_like(db2_ref)

        h1 = zh_ref[...] * g1_ref[...] + b1_ref[...]
        hbv = hb_ref[...]
        ffn = jnp.zeros((tm, D_MODEL), F32)
        acts = []
        for j in range(N_SHARD + 1):
            if j < N_SHARD:
                gj = _dot_nt(hbv, wg[j])
                uj = _dot_nt(hbv, wu[j])
                gs_ref[j] = gj.astype(BF16)
                us_ref[j] = uj.astype(BF16)
                acts.append((gj * _sigmoid(gj) * uj).astype(BF16))
                act_ref[j] = acts[j]
            if j > 0:
                ffn = ffn + _dot(acts[j - 1], wd[j - 1])
        ple = _dot(p_ref[...].astype(BF16), wpe[...])
        pg = _sigmoid(_dot(hbv, wpg[...]))
        pg_ref[...] = pg.astype(BF16)
        ple_ref[...] = ple.astype(BF16)
        z2 = ALPHA * h1 + ffn + pg * ple
        mu = jnp.mean(z2, axis=1, keepdims=True)
        zc = z2 - mu
        var = jnp.mean(zc * zc, axis=1, keepdims=True)
        r = lax.rsqrt(var + LN_EPS)
        zh2 = zc * r
        err = zh2 * g2_ref[...] + b2_ref[...] - t_ref[...]
        loss_ref[...] += jnp.sum(err * err)
        dy = err * (1.0 / D_MODEL)
        dg2_ref[...] += jnp.sum(dy * zh2, axis=0, keepdims=True)
        db2_ref[...] += jnp.sum(dy, axis=0, keepdims=True)
        dzh = dy * g2_ref[...]
        m1 = jnp.mean(dzh, axis=1, keepdims=True)
        m2 = jnp.mean(dzh * zh2, axis=1, keepdims=True)
        dz2 = r * (dzh - m1 - zh2 * m2)
        dz_ref[...] = dz2
        dzb_ref[...] = dz2.astype(BF16)

    row = lambda w: pl.BlockSpec((tm, w), lambda i: (i, 0))
    const = lambda s: pl.BlockSpec(s, lambda i: (0, 0))
    sh = pl.BlockSpec((N_SHARD, tm, FFN_SHARD), lambda i: (0, i, 0))
    sh_shape = jax.ShapeDtypeStruct((N_SHARD, t, FFN_SHARD), BF16)
    hbm = pl.BlockSpec(memory_space=pl.ANY)
    return pl.pallas_call(
        body, name="ffn_fwd", grid=(t // tm,),
        in_specs=[row(D_MODEL), row(D_MODEL), row(PLE_DIM), row(D_MODEL),
                  const((1, D_MODEL)), const((1, D_MODEL)), const((1, D_MODEL)), const((1, D_MODEL)),
                  hbm, hbm, hbm, hbm, hbm],
        out_specs=[row(D_MODEL), row(D_MODEL), sh, sh, sh, row(D_MODEL), row(D_MODEL),
                   const((8, LANES)), const((1, D_MODEL)), const((1, D_MODEL))],
        out_shape=[jax.ShapeDtypeStruct((t, D_MODEL), F32), jax.ShapeDtypeStruct((t, D_MODEL), BF16),
                   sh_shape, sh_shape, sh_shape,
                   jax.ShapeDtypeStruct((t, D_MODEL), BF16), jax.ShapeDtypeStruct((t, D_MODEL), BF16),
                   jax.ShapeDtypeStruct((8, LANES), F32),
                   jax.ShapeDtypeStruct((1, D_MODEL), F32), jax.ShapeDtypeStruct((1, D_MODEL), F32)],
        scratch_shapes=[pltpu.VMEM(wg4.shape, BF16), pltpu.VMEM(wu4.shape, BF16), pltpu.VMEM(wd4.shape, BF16),
                        pltpu.VMEM(wpe.shape, BF16), pltpu.VMEM(wpg.shape, BF16),
                        pltpu.SemaphoreType.DMA((3 * N_SHARD + 2,))],
        compiler_params=_params("arbitrary", vmem=VMEM_LIMIT),
    )(zh1, hb, p2d, tgt, g1, b1, g2, b2, wg4, wu4, wd4, wpe, wpg)


def _ret_tables(lgf, lgb):
    c = CHUNK
    row = lax.broadcasted_iota(jnp.int32, (c, LANES), 0).astype(F32)
    ii = lax.broadcasted_iota(jnp.int32, (c, c), 0).astype(F32)
    jj = lax.broadcasted_iota(jnp.int32, (c, c), 1).astype(F32)
    diff = ii - jj
    dmats = []
    for h in range(2):
        lf = lgf[:, h * HEAD_DIM:h * HEAD_DIM + 1]
        lb = lgb[:, h * HEAD_DIM:h * HEAD_DIM + 1]
        dmats.append(jnp.where(diff > 0, jnp.exp(lf * jnp.maximum(diff, 0.0)),
                               jnp.where(diff < 0, jnp.exp(lb * jnp.maximum(-diff, 0.0)), 2.0)))
    tab = dict(
        qdec_f=jnp.exp(lgf * (row + 1.0)), kdec_f=jnp.exp(lgf * (c - 1.0 - row)),
        qdec_b=jnp.exp(lgb * (c - row)), kdec_b=jnp.exp(lgb * row),
        cdec_f=jnp.exp(lgf * c), cdec_b=jnp.exp(lgb * c),
        d0=dmats[0], d1=dmats[1], row=row, diff=diff)
    r = lax.broadcasted_iota(jnp.int32, (LANES, LANES), 0) < HEAD_DIM
    cc = lax.broadcasted_iota(jnp.int32, (LANES, LANES), 1) < HEAD_DIM
    tab["bd"] = r == cc
    tab["m0"] = lax.broadcasted_iota(jnp.int32, (c, LANES), 1) < HEAD_DIM
    return tab


def _ret_specs(bsz, s):
    blk = lambda cb: pl.BlockSpec((bsz, s, LANES), lambda p, cb=cb: (0, 0, cb + p))
    lane = pl.BlockSpec((None, 1, LANES), lambda p: (p, 0, 0))
    gain = pl.BlockSpec((1, LANES), lambda p: (0, p))
    pair = pl.BlockSpec((bsz, s, LANES), lambda p: (0, 0, p))
    return blk, lane, gain, pair


def _ret_state_spec(bsz, n_chunk):
    spec = pl.BlockSpec((None, bsz, n_chunk, LANES, LANES), lambda p: (p, 0, 0, 0, 0))
    return spec, jax.ShapeDtypeStruct((4, bsz, n_chunk, LANES, LANES), F32)


def _ret_kv_states(tb, k_ref, v_ref, rb_ref, kvf_ref, n_chunk):
    c = CHUNK
    bsz = k_ref.shape[0]
    bd = tb["bd"]

    def contributions(n, carry):
        sl = pl.ds(pl.multiple_of(n * c, c), c)
        kfb = []
        for b in range(bsz):
            k32 = k_ref[b, sl, :].astype(F32)
            kfb.append(jnp.concatenate([k32 * tb["kdec_f"], k32 * tb["kdec_b"]], axis=1).astype(BF16))
        kvs = [_dot_tn(kfb[b], v_ref[b, sl, :]) for b in range(bsz)]
        for b in range(bsz):
            kvf_ref[b, n] = jnp.where(bd, kvs[b][0:LANES], 0.0)
            rb_ref[b, n] = jnp.where(bd, kvs[b][LANES:], 0.0)
        return carry

    lax.fori_loop(0, n_chunk, contributions, 0, unroll=2)

    def recur(i, rbs):
        n = n_chunk - 1 - i
        new = []
        for b in range(bsz):
            own = rb_ref[b, n]
            rb_ref[b, n] = rbs[b]
            new.append(rbs[b] * tb["cdec_b"] + own)
        return tuple(new)

    lax.fori_loop(0, n_chunk, recur, tuple(jnp.zeros((LANES, LANES), F32) for _ in range(bsz)))


def _split_rows(x, m0):
    return jnp.concatenate([jnp.where(m0, x, 0.0), jnp.where(m0, 0.0, x)], axis=0).astype(BF16)


def _ret_fwd(u3, lgf_l, lgb_l, gn_gain, rider=None):
    bsz, s, _ = u3.shape
    n_chunk = s // CHUNK
    c = CHUNK

    def body(q_ref, k_ref, v_ref, g_ref, lgf_ref, lgb_ref, gain_ref, yh_ref, rstd_ref, o_ref, rb_ref, kvf_ref):
        tb = _ret_tables(lgf_ref[...], lgb_ref[...])
        m0 = tb["m0"]
        gain = gain_ref[...]
        rows = range(bsz)
        _ret_kv_states(tb, k_ref, v_ref, rb_ref, kvf_ref, n_chunk)

        def chunk(n, rfs):
            sl = pl.ds(pl.multiple_of(n * c, c), c)
            qs = [q_ref[b, sl, :].astype(F32) * 0.125 for b in rows]
            s01 = [_dot_nt(_split_rows(qs[b], m0), k_ref[b, sl, :]) for b in rows]
            ys = []
            for b in rows:
                lhs = jnp.concatenate([s01[b][0:c] * tb["d0"], s01[b][c:] * tb["d1"],
                                       qs[b] * tb["qdec_f"], qs[b] * tb["qdec_b"]], axis=1).astype(BF16)
                rhs = jnp.concatenate([_split_rows(v_ref[b, sl, :].astype(F32), m0),
                                       rfs[b].astype(BF16), rb_ref[b, n].astype(BF16)], axis=0)
                ys.append(_dot(lhs, rhs))
            new = []
            for b in rows:
                y = ys[b]
                mu = _head_mean(y, m0)
                yc = y - mu
                rstd = lax.rsqrt(_head_mean(yc * yc, m0) + GN_EPS)
                yh = yc * rstd
                g = g_ref[b, sl, :].astype(F32)
                yh_ref[b, sl, :] = yh
                rstd_ref[b, sl, :] = rstd
                o_ref[b, sl, :] = (yh * gain * (g * _sigmoid(g))).astype(BF16)
                new.append(rfs[b] * tb["cdec_f"] + kvf_ref[b, n])
            return tuple(new)

        lax.fori_loop(0, n_chunk, chunk, tuple(jnp.zeros((LANES, LANES), F32) for _ in rows))

    blk, lane, gain, pair = _ret_specs(bsz, s)
    state, state_shape = _ret_state_spec(bsz, n_chunk)
    return _hosted_call(
        body, "ret_fwd", (4,),
        in_specs=[blk(CB_RQ), blk(CB_RK), blk(CB_RV), blk(CB_RG), lane, lane, gain],
        out_specs=[pair, pair, pair, state, state],
        out_shape=[jax.ShapeDtypeStruct((bsz, s, RET_WIDTH), F32), jax.ShapeDtypeStruct((bsz, s, RET_WIDTH), F32),
                   jax.ShapeDtypeStruct((bsz, s, RET_WIDTH), BF16), state_shape, state_shape],
        scratch_shapes=[],
        operands=(u3, u3, u3, u3, lgf_l, lgb_l, gn_gain), rider=rider)


def _ret_bwd(u3, y_hat, y_rstd, states, d_o, lgf_l, lgb_l, gn_gain, rider=None):
    bsz, s, _ = u3.shape
    n_chunk = s // CHUNK
    c = CHUNK

    def body(q_ref, k_ref, v_ref, g_ref, yh_ref, rstd_ref, do_ref, lgf_ref, lgb_ref, gain_ref, rb_ref, kvf_ref,
             dq_ref, dk_ref, dv_ref, dg_ref, part_ref,
             rf_ref, dirf_ref, dy_ref, dk_acc, dv_acc, pa0, pa1, vec_ref):
        tb = _ret_tables(lgf_ref[...], lgb_ref[...])
        m0, bd, row = tb["m0"], tb["bd"], tb["row"]
        gain = gain_ref[...]
        wf = jnp.maximum(tb["diff"], 0.0)
        wb = jnp.maximum(-tb["diff"], 0.0)
        rows = range(bsz)
        zero_states = tuple(jnp.zeros((LANES, LANES), F32) for _ in rows)
        for ref in (pa0, pa1):
            ref[...] = jnp.zeros_like(ref)
        vec_ref[...] = jnp.zeros_like(vec_ref)

        def sweep_fwd(n, carry):
            rfs, gbs = carry
            sl = pl.ds(pl.multiple_of(n * c, c), c)
            qs, ks, vs, dys, dybs, q01, k01, dy01 = [], [], [], [], [], [], [], []
            dgain = jnp.zeros((1, LANES), F32)
            for b in rows:
                q = q_ref[b, sl, :].astype(F32) * 0.125
                k = k_ref[b, sl, :]
                yh = yh_ref[b, sl, :]
                rstd = rstd_ref[b, sl, :]
                do = do_ref[b, sl, :].astype(F32)
                g = g_ref[b, sl, :].astype(F32)
                sg = _sigmoid(g)
                sil = g * sg
                dyh = do * gain * sil
                dg_ref[b, sl, :] = (do * yh * gain * sg * (1.0 + g * (1.0 - sg))).astype(BF16)
                dgain = dgain + jnp.sum(do * yh * sil, axis=0, keepdims=True)
                dy = rstd * (dyh - _head_mean(dyh, m0) - yh * _head_mean(dyh * yh, m0))
                dyb = dy.astype(BF16)
                dy_ref[b, sl, :] = dyb
                rf_ref[b, n] = rfs[b]
                qs.append(q)
                ks.append(k)
                vs.append(v_ref[b, sl, :])
                dys.append(dy)
                dybs.append(dyb)
                q01.append(_split_rows(q, m0))
                k01.append(_split_rows(k.astype(F32), m0))
                dy01.append(_split_rows(dy, m0))
            s01 = [_dot_nt(q01[b], ks[b]) for b in rows]
            da01 = [_dot_nt(dy01[b], vs[b]) for b in rows]
            rbn = [rb_ref[b, n] for b in rows]
            states = [jnp.concatenate([rfs[b], rbn[b]], axis=0).astype(BF16) for b in rows]
            dqc = [_dot_nt(dybs[b], states[b]) for b in rows]
            gbb = [gbs[b].astype(BF16) for b in rows]
            dkb = [_dot_nt(vs[b], gbb[b]) for b in rows]
            qfb = [jnp.concatenate([qs[b] * tb["qdec_f"], qs[b] * tb["qdec_b"]], axis=1) for b in rows]
            direct = [_dot_tn(qfb[b].astype(BF16), dybs[b]) for b in rows]
            ds_cat, ds_rows, a_rows = [], [], []
            for b in rows:
                a0 = s01[b][0:c] * tb["d0"]
                a1 = s01[b][c:] * tb["d1"]
                pa0[...] += da01[b][0:c] * a0
                pa1[...] += da01[b][c:] * a1
                ds0 = da01[b][0:c] * tb["d0"]
                ds1 = da01[b][c:] * tb["d1"]
                ds_cat.append(jnp.concatenate([ds0, ds1], axis=1).astype(BF16))
                ds_rows.append(jnp.concatenate([ds0, ds1], axis=0).astype(BF16))
                a_rows.append(jnp.concatenate([a0, a1], axis=0).astype(BF16))
            kbd = [ks[b].astype(F32) * tb["kdec_b"] for b in rows]
            dq_in = [_dot(ds_cat[b], k01[b]) for b in rows]
            dk_in = [_dot_tn(ds_rows[b], q01[b]) for b in rows]
            dv_in = [_dot_tn(a_rows[b], dy01[b]) for b in rows]
            dv_gb = [_dot(kbd[b].astype(BF16), gbb[b]) for b in rows]
            new_rf, new_gb = [], []
            dlf = jnp.zeros((1, LANES), F32)
            dlb = jnp.zeros((1, LANES), F32)
            for b in rows:
                dqf, dqb = dqc[b][:, 0:LANES], dqc[b][:, LANES:]
                qf, qb = qfb[b][:, 0:LANES], qfb[b][:, LANES:]
                dq = dq_in[b] + dqf * tb["qdec_f"] + dqb * tb["qdec_b"]
                dq_ref[b, sl, :] = (dq * 0.125).astype(BF16)
                dk_acc[b, sl, :] = dk_in[b] + dkb[b] * tb["kdec_b"]
                dv_acc[b, sl, :] = dv_in[b] + dv_gb[b]
                dlf = dlf + jnp.sum((row + 1.0) * qf * dqf, axis=0, keepdims=True)
                dlb = dlb + jnp.sum((c - row) * qb * dqb + row * kbd[b] * dkb[b], axis=0, keepdims=True)
                dlb = dlb + c * tb["cdec_b"] * jnp.sum(gbs[b] * rbn[b], axis=0, keepdims=True)
                dirf_ref[b, n] = jnp.where(bd, direct[b][0:LANES], 0.0)
                new_gb.append(jnp.where(bd, direct[b][LANES:], 0.0) + tb["cdec_b"] * gbs[b])
                new_rf.append(rfs[b] * tb["cdec_f"] + kvf_ref[b, n])
            vec_ref[0:1, :] += dlf
            vec_ref[1:2, :] += dlb
            vec_ref[6:7, :] += dgain
            return tuple(new_rf), tuple(new_gb)

        lax.fori_loop(0, n_chunk, sweep_fwd, (zero_states, zero_states))

        def sweep_bwd(i, gfs):
            n = n_chunk - 1 - i
            sl = pl.ds(pl.multiple_of(n * c, c), c)
            gfb = [gfs[b].astype(BF16) for b in rows]
            kfd = [k_ref[b, sl, :].astype(F32) * tb["kdec_f"] for b in rows]
            dkf = [_dot_nt(v_ref[b, sl, :], gfb[b]) for b in rows]
            dvf = [_dot(kfd[b].astype(BF16), gfb[b]) for b in rows]
            new = []
            dlf = jnp.zeros((1, LANES), F32)
            for b in rows:
                dk_ref[b, sl, :] = (dk_acc[b, sl, :] + dkf[b] * tb["kdec_f"]).astype(BF16)
                dv_ref[b, sl, :] = (dv_acc[b, sl, :] + dvf[b]).astype(BF16)
                dlf = dlf + jnp.sum((c - 1.0 - row) * kfd[b] * dkf[b], axis=0, keepdims=True)
                dlf = dlf + c * tb["cdec_f"] * jnp.sum(gfs[b] * rf_ref[b, n], axis=0, keepdims=True)
                new.append(dirf_ref[b, n] + tb["cdec_f"] * gfs[b])
            vec_ref[0:1, :] += dlf
            return tuple(new)

        lax.fori_loop(0, n_chunk, sweep_bwd, zero_states, unroll=2)
        vec_ref[2:3, :] = jnp.sum(pa0[...] * wf, axis=0, keepdims=True)
        vec_ref[3:4, :] = jnp.sum(pa1[...] * wf, axis=0, keepdims=True)
        vec_ref[4:5, :] = jnp.sum(pa0[...] * wb, axis=0, keepdims=True)
        vec_ref[5:6, :] = jnp.sum(pa1[...] * wb, axis=0, keepdims=True)
        part_ref[...] = vec_ref[...]

    blk, lane, gain, pair = _ret_specs(bsz, s)
    out_bf = jax.ShapeDtypeStruct((bsz, s, RET_WIDTH), BF16)
    state = pltpu.VMEM((bsz, n_chunk, LANES, LANES), F32)
    saved = _ret_state_spec(bsz, n_chunk)[0]
    return _hosted_call(
        body, "ret_bwd", (4,),
        in_specs=[blk(CB_RQ), blk(CB_RK), blk(CB_RV), blk(CB_RG), pair, pair, pair, lane, lane, gain, saved, saved],
        out_specs=[pair, pair, pair, pair, pl.BlockSpec((None, 8, LANES), lambda p: (p, 0, 0))],
        out_shape=[out_bf, out_bf, out_bf, out_bf, jax.ShapeDtypeStruct((4, 8, LANES), F32)],
        scratch_shapes=[state, state,
                        pltpu.VMEM((bsz, s, LANES), BF16), pltpu.VMEM((bsz, s, LANES), F32),
                        pltpu.VMEM((bsz, s, LANES), F32),
                        pltpu.VMEM((c, c), F32), pltpu.VMEM((c, c), F32), pltpu.VMEM((8, LANES), F32)],
        operands=(u3, u3, u3, u3, y_hat, y_rstd, d_o, lgf_l, lgb_l, gn_gain, *states), rider=rider)


def _attn_window_tables(n, s):
    qi = lax.broadcasted_iota(jnp.int32, (CHUNK, 3 * CHUNK), 0)
    kj = lax.broadcasted_iota(jnp.int32, (CHUNK, 3 * CHUNK), 1)
    dist = jnp.abs(kj - CHUNK - qi)
    kpos = n * CHUNK - CHUNK + kj
    valid = (dist <= CHUNK) & (kpos >= 0) & (kpos < s)
    return dist.astype(F32), valid


def _dup_kv_head(x, g):
    lane = lax.broadcasted_iota(jnp.int32, x.shape, 1)
    keep = (lane < HEAD_DIM) == (g == 0)
    xf = x.astype(F32)
    return jnp.where(keep, xf, pltpu.roll(xf, HEAD_DIM, 1))


def _attn_specs(s):
    q = pl.BlockSpec((None, s, 2 * LANES), lambda b, g: (b, 0, CB_AQ // 2 + g))
    k = pl.BlockSpec((None, s, LANES), lambda b, g: (b, 0, CB_AK))
    v = pl.BlockSpec((None, s, LANES), lambda b, g: (b, 0, CB_AV))
    grp = pl.BlockSpec((None, s, 2 * LANES), lambda b, g: (b, 0, g))
    smem = pl.BlockSpec(memory_space=pltpu.SMEM)
    return q, k, v, grp, smem


def _fill_padded(dst_ref, val, s):
    dst_ref[0:CHUNK, :] = jnp.zeros((CHUNK, LANES), dst_ref.dtype)
    dst_ref[CHUNK:CHUNK + s, :] = val.astype(dst_ref.dtype)
    dst_ref[CHUNK + s:2 * CHUNK + s, :] = jnp.zeros((CHUNK, LANES), dst_ref.dtype)


def _attn_probs(sc, slope, snk, dist, valid):
    sc = jnp.where(valid, sc - slope * dist, NEG_INF)
    m = jnp.maximum(jnp.max(sc, axis=1, keepdims=True), snk)
    e = jnp.exp(sc - m)
    es = jnp.exp(snk - m)
    inv = 1.0 / (jnp.sum(e, axis=1, keepdims=True) + es)
    return e * inv, es * inv


def _stack_heads(x2, m0):
    parts = []
    for pr in range(2):
        xp = x2[:, pr * LANES:(pr + 1) * LANES]
        parts += [jnp.where(m0, xp, 0.0), jnp.where(m0, 0.0, xp)]
    return jnp.concatenate(parts, axis=0).astype(BF16)


def _unstack_pair(x_all, pr, m0):
    return jnp.where(m0, x_all[(2 * pr) * CHUNK:(2 * pr + 1) * CHUNK], x_all[(2 * pr + 1) * CHUNK:(2 * pr + 2) * CHUNK])


def _attn_saved_specs(bsz, n_blk):
    specs = [pl.BlockSpec((None, None, n_blk, 4 * CHUNK, w), lambda b, g: (b, g, 0, 0, 0)) for w in (3 * CHUNK, 1)]
    shapes = [jax.ShapeDtypeStruct((bsz, 2, n_blk, 4 * CHUNK, 3 * CHUNK), BF16),
              jax.ShapeDtypeStruct((bsz, 2, n_blk, 4 * CHUNK, 1), F32)]
    return specs, shapes


def _attn_fwd(u3, slopes, sink, rider=None):
    bsz, s, _ = u3.shape
    n_blk = s // CHUNK

    def body(slope_ref, sink_ref, q_ref, k_ref, v_ref, o_ref, p_ref, ps_ref, kp_ref, vp_ref):
        g = pl.program_id(1)
        _fill_padded(kp_ref, _dup_kv_head(k_ref[...], g), s)
        _fill_padded(vp_ref, _dup_kv_head(v_ref[...], g), s)
        m0 = lax.broadcasted_iota(jnp.int32, (CHUNK, LANES), 1) < HEAD_DIM

        def blk(n, carry):
            r0 = pl.multiple_of(n * CHUNK, CHUNK)
            kw = kp_ref[pl.ds(r0, 3 * CHUNK), :]
            vw = vp_ref[pl.ds(r0, 3 * CHUNK), :]
            dist, valid = _attn_window_tables(n, s)
            q_all = _stack_heads(q_ref[pl.ds(r0, CHUNK), :].astype(F32) * 0.125, m0)
            sc_all = _dot_nt(q_all, kw)
            probs, sinks = [], []
            for i in range(4):
                p, ps = _attn_probs(sc_all[i * CHUNK:(i + 1) * CHUNK], slope_ref[g * 4 + i], sink_ref[g * 4 + i],
                                    dist, valid)
                probs.append(p.astype(BF16))
                sinks.append(ps)
            p_all = jnp.concatenate(probs, axis=0)
            p_ref[n] = p_all
            ps_ref[n] = jnp.concatenate(sinks, axis=0)
            out_all = _dot(p_all, vw)
            for pr in range(2):
                o_ref[pl.ds(r0, CHUNK), pr * LANES:(pr + 1) * LANES] = _unstack_pair(out_all, pr, m0).astype(BF16)
            return carry

        lax.fori_loop(0, n_blk, blk, 0)

    q, k, v, grp, smem = _attn_specs(s)
    saved_specs, saved_shapes = _attn_saved_specs(bsz, n_blk)
    return _hosted_call(
        body, "attn_fwd", (bsz, 2),
        in_specs=[smem, smem, q, k, v],
        out_specs=[grp] + saved_specs,
        out_shape=[jax.ShapeDtypeStruct((bsz, s, ATTN_WIDTH), BF16)] + saved_shapes,
        scratch_shapes=[pltpu.VMEM((s + 2 * CHUNK, LANES), BF16), pltpu.VMEM((s + 2 * CHUNK, LANES), BF16)],
        operands=(slopes, sink, u3, u3, u3), rider=rider)


def _attn_bwd(u3, d_o, probs, sink_probs, rider=None):
    bsz, s, _ = u3.shape
    n_blk = s // CHUNK

    def body(q_ref, k_ref, v_ref, do_ref, p_ref, ps_ref, dq_ref, dkv_ref, ds_ref,
             kp_ref, vp_ref, dk_acc, dv_acc):
        g = pl.program_id(1)
        _fill_padded(kp_ref, _dup_kv_head(k_ref[...], g), s)
        _fill_padded(vp_ref, _dup_kv_head(v_ref[...], g), s)
        dk_acc[...] = jnp.zeros_like(dk_acc)
        dv_acc[...] = jnp.zeros_like(dv_acc)
        m0 = lax.broadcasted_iota(jnp.int32, (CHUNK, LANES), 1) < HEAD_DIM

        def blk(n, dsink):
            r0 = pl.multiple_of(n * CHUNK, CHUNK)
            win = pl.ds(r0, 3 * CHUNK)
            kw = kp_ref[win, :]
            vw = vp_ref[win, :]
            q_all = _stack_heads(q_ref[pl.ds(r0, CHUNK), :].astype(F32) * 0.125, m0)
            do_all = _stack_heads(do_ref[pl.ds(r0, CHUNK), :].astype(F32), m0)
            p_all = p_ref[n]
            ps_all = ps_ref[n]
            dp_all = _dot_nt(do_all, vw)
            new_dsink, dscs = [], []
            for i in range(4):
                rows = slice(i * CHUNK, (i + 1) * CHUNK)
                p = p_all[rows].astype(F32)
                dp = dp_all[rows]
                delta = jnp.sum(p * dp, axis=1, keepdims=True)
                dscs.append((p * (dp - delta)).astype(BF16))
                dsh = jnp.sum(ps_all[rows] * delta, axis=0, keepdims=True)
                new_dsink.append(dsink[i] - jnp.broadcast_to(dsh, (1, LANES)))
            dsc_all = jnp.concatenate(dscs, axis=0)
            dq_all = _dot(dsc_all, kw)
            dk_acc[win, :] += _dot_tn(dsc_all, q_all)
            dv_acc[win, :] += _dot_tn(p_all, do_all)
            for pr in range(2):
                dq_ref[pl.ds(r0, CHUNK), pr * LANES:(pr + 1) * LANES] = (
                    _unstack_pair(dq_all, pr, m0) * 0.125).astype(BF16)
            return tuple(new_dsink)

        dsink = lax.fori_loop(0, n_blk, blk, tuple(jnp.zeros((1, LANES), F32) for _ in range(4)))
        dk = dk_acc[CHUNK:CHUNK + s, :]
        dv = dv_acc[CHUNK:CHUNK + s, :]
        lane = lax.broadcasted_iota(jnp.int32, (s, LANES), 1)
        fold = lambda a: a + pltpu.roll(a, HEAD_DIM, 1)
        dkv_ref[...] = jnp.where(lane < HEAD_DIM, fold(dk), fold(dv)).astype(BF16)
        ds_ref[...] = jnp.zeros_like(ds_ref)
        for i in range(4):
            ds_ref[i:i + 1, :] = dsink[i]

    q, k, v, grp, _ = _attn_specs(s)
    return _hosted_call(
        body, "attn_bwd", (bsz, 2),
        in_specs=[q, k, v, grp] + _attn_saved_specs(bsz, n_blk)[0],
        out_specs=[grp, pl.BlockSpec((None, s, LANES), lambda b, g: (b, 0, g)),
                   pl.BlockSpec((None, None, 8, LANES), lambda b, g: (b, g, 0, 0))],
        out_shape=[jax.ShapeDtypeStruct((bsz, s, ATTN_WIDTH), BF16), jax.ShapeDtypeStruct((bsz, s, 2 * LANES), BF16),
                   jax.ShapeDtypeStruct((bsz, 2, 8, LANES), F32)],
        scratch_shapes=[pltpu.VMEM((s + 2 * CHUNK, LANES), BF16), pltpu.VMEM((s + 2 * CHUNK, LANES), BF16),
                        pltpu.VMEM((s + 2 * CHUNK, LANES), F32), pltpu.VMEM((s + 2 * CHUNK, LANES), F32)],
        operands=(u3, u3, u3, d_o, probs, sink_probs), rider=rider)


def _ffn_bwd(dz2, gs, us, pg, ple, zh1, r1, g1, wg4, wu4, wd4, wpg, w_out):
    t = dz2.shape[0]
    tm = 256

    def body(dz_ref, gs_ref, us_ref, pg_ref, ple_ref, zh_ref, r_ref, g1_ref,
             wg_hbm, wu_hbm, wd_hbm, wpg_hbm, wo_hbm,
             dgs_ref, dus_ref, dsp_ref, dple_ref, dz1_ref, dyr_ref, dya_ref, dg1_ref, db1_ref,
             wg, wu, wd, wpg, wo, wsem):
        step = pl.program_id(0)
        loads = [(h.at[j], v.at[j]) for j in range(N_SHARD) for h, v in ((wd_hbm, wd), (wg_hbm, wg), (wu_hbm, wu))]
        _load_resident(step, loads + [(wpg_hbm, wpg), (wo_hbm, wo)], wsem)

        @pl.when(step == 0)
        def _():
            dg1_ref[...] = jnp.zeros_like(dg1_ref)
            db1_ref[...] = jnp.zeros_like(db1_ref)

        dz = dz_ref[...]
        dzb = dz.astype(BF16)
        dh = ALPHA * dz
        pending = []
        for j in range(N_SHARD + 1):
            if j < N_SHARD:
                da = _dot_nt(dzb, wd[j])
                gj = gs_ref[j].astype(F32)
                uj = us_ref[j].astype(F32)
                sg = _sigmoid(gj)
                dgj = (da * uj * sg * (1.0 + gj * (1.0 - sg))).astype(BF16)
                duj = (da * gj * sg).astype(BF16)
                dgs_ref[j] = dgj
                dus_ref[j] = duj
                pending.append((dgj, duj))
            if j > 0:
                dgp, dup = pending[j - 1]
                dh = dh + _dot(dgp, wg[j - 1]) + _dot(dup, wu[j - 1])
        pgv = pg_ref[...].astype(F32)
        plev = ple_ref[...].astype(F32)
        dple_ref[...] = (dz * pgv).astype(BF16)
        dsp = (dz * plev * pgv * (1.0 - pgv)).astype(BF16)
        dsp_ref[...] = dsp
        dh = dh + _dot_nt(dsp, wpg[...])
        zh = zh_ref[...]
        dg1_ref[...] += jnp.sum(dh * zh, axis=0, keepdims=True)
        db1_ref[...] += jnp.sum(dh, axis=0, keepdims=True)
        dzh = dh * g1_ref[...]
        m1 = jnp.mean(dzh, axis=1, keepdims=True)
        m2 = jnp.mean(dzh * zh, axis=1, keepdims=True)
        dz1 = r_ref[...] * (dzh - m1 - zh * m2)
        dz1_ref[...] = dz1
        dyc = _dot_nt(dz1.astype(BF16), wo[...])
        dyr_ref[...] = dyc[:, 0:RET_WIDTH].astype(BF16)
        dya_ref[...] = dyc[:, RET_WIDTH:].astype(BF16)

    row = lambda w: pl.BlockSpec((tm, w), lambda i: (i, 0))
    const = lambda s: pl.BlockSpec(s, lambda i: (0, 0))
    sh = pl.BlockSpec((N_SHARD, tm, FFN_SHARD), lambda i: (0, i, 0))
    hbm = pl.BlockSpec(memory_space=pl.ANY)
    sh_shape = jax.ShapeDtypeStruct((N_SHARD, t, FFN_SHARD), BF16)
    return pl.pallas_call(
        body, name="ffn_bwd", grid=(t // tm,),
        in_specs=[row(D_MODEL), sh, sh, row(D_MODEL), row(D_MODEL), row(D_MODEL), row(1), const((1, D_MODEL)),
                  hbm, hbm, hbm, hbm, hbm],
        out_specs=[sh, sh, row(D_MODEL), row(D_MODEL), row(D_MODEL), row(RET_WIDTH), row(ATTN_WIDTH),
                   const((1, D_MODEL)), const((1, D_MODEL))],
        out_shape=[sh_shape, sh_shape, jax.ShapeDtypeStruct((t, D_MODEL), BF16),
                   jax.ShapeDtypeStruct((t, D_MODEL), BF16), jax.ShapeDtypeStruct((t, D_MODEL), F32),
                   jax.ShapeDtypeStruct((t, RET_WIDTH), BF16), jax.ShapeDtypeStruct((t, ATTN_WIDTH), BF16),
                   jax.ShapeDtypeStruct((1, D_MODEL), F32), jax.ShapeDtypeStruct((1, D_MODEL), F32)],
        scratch_shapes=[pltpu.VMEM(wg4.shape, BF16), pltpu.VMEM(wu4.shape, BF16), pltpu.VMEM(wd4.shape, BF16),
                        pltpu.VMEM(wpg.shape, BF16), pltpu.VMEM(w_out.shape, BF16),
                        pltpu.SemaphoreType.DMA((3 * N_SHARD + 2,))],
        compiler_params=_params("arbitrary", vmem=VMEM_LIMIT),
    )(dz2, gs, us, pg, ple, zh1, r1, g1, wg4, wu4, wd4, wpg, w_out)


def _wgrad_misc(y_ret, y_att, dz1, hb, dsp, p2d, dple, rider=None):
    t = dz1.shape[0]
    tk = min(t, 512)

    def body(yr_ref, ya_ref, dz_ref, hb_ref, dsp_ref, p_ref, dple_ref, wo_ref, wpg_ref, wpe_ref):
        @pl.when(pl.program_id(0) == 0)
        def _():
            wo_ref[...] = jnp.zeros_like(wo_ref)
            wpg_ref[...] = jnp.zeros_like(wpg_ref)
            wpe_ref[...] = jnp.zeros_like(wpe_ref)

        dzb = dz_ref[...].astype(BF16)
        wo_ref[0:RET_WIDTH, :] += _dot_tn(yr_ref[...], dzb)
        wo_ref[RET_WIDTH:, :] += _dot_tn(ya_ref[...], dzb)
        wpg_ref[...] += _dot_tn(hb_ref[...], dsp_ref[...])
        wpe_ref[...] += _dot_tn(p_ref[...].astype(BF16), dple_ref[...])

    row = lambda w: pl.BlockSpec((tk, w), lambda k: (k, 0))
    const = lambda s: pl.BlockSpec(s, lambda k: (0, 0))
    return _hosted_call(
        body, "wgrad_misc", (t // tk,),
        in_specs=[row(RET_WIDTH), row(ATTN_WIDTH), row(D_MODEL), row(D_MODEL), row(D_MODEL), row(PLE_DIM),
                  row(D_MODEL)],
        out_specs=[const((D_MODEL, D_MODEL)), const((D_MODEL, D_MODEL)), const((PLE_DIM, D_MODEL))],
        out_shape=[jax.ShapeDtypeStruct((D_MODEL, D_MODEL), F32), jax.ShapeDtypeStruct((D_MODEL, D_MODEL), F32),
                   jax.ShapeDtypeStruct((PLE_DIM, D_MODEL), F32)],
        scratch_shapes=[], operands=(y_ret, y_att, dz1, hb, dsp, p2d, dple), rider=rider, semantics=["arbitrary"])


def _wgrad_ffn(acts, dgs, dus, hb, dz2b):
    t = dz2b.shape[0]
    tk = min(t, 512)
    nk = t // tk

    def body(act_ref, dg_ref, du_ref, hb_ref, dz_ref, og_ref, ou_ref, od_ref):
        @pl.when(pl.program_id(1) == 0)
        def _():
            og_ref[...] = jnp.zeros_like(og_ref)
            ou_ref[...] = jnp.zeros_like(ou_ref)
            od_ref[...] = jnp.zeros_like(od_ref)

        hbv = hb_ref[...]
        og_ref[...] += _dot_tn(dg_ref[...], hbv)
        ou_ref[...] += _dot_tn(du_ref[...], hbv)
        od_ref[...] += _dot_tn(act_ref[...], dz_ref[...])

    a_spec = pl.BlockSpec((None, tk, FFN_SHARD), lambda j, k: (j, k, 0))
    b_spec = pl.BlockSpec((tk, D_MODEL), lambda j, k: (k, 0))
    o_spec = pl.BlockSpec((None, FFN_SHARD, D_MODEL), lambda j, k: (j, 0, 0))
    o_shape = jax.ShapeDtypeStruct((N_SHARD, FFN_SHARD, D_MODEL), F32)
    return pl.pallas_call(
        body, name="wgrad_ffn", grid=(N_SHARD, nk),
        in_specs=[a_spec, a_spec, a_spec, b_spec, b_spec],
        out_specs=[o_spec] * 3, out_shape=[o_shape] * 3,
        compiler_params=_params("parallel", "arbitrary", vmem=VMEM_LIMIT),
    )(acts, dgs, dus, hb, dz2b)


KV_ORDER = (0, 128, 64, 192)


def _wgrad_in(pieces, x2d):
    t = x2d.shape[0]
    tk = min(t, 512)
    nk = t // tk
    kv0 = CB_AK * LANES

    def body(p0, p1, p2, p3, p4, pkv, x_ref, o_ref):
        @pl.when(pl.program_id(0) == 0)
        def _():
            o_ref[...] = jnp.zeros_like(o_ref)

        xb = x_ref[...].astype(BF16)
        for i, ref in enumerate((p0, p1, p2, p3, p4)):
            o_ref[i * 512:(i + 1) * 512, :] += _dot_tn(ref[...], xb)
        dkv = _dot_tn(pkv[...], xb)
        for i, o in enumerate(KV_ORDER):
            o_ref[kv0 + o:kv0 + o + HEAD_DIM, :] += dkv[i * HEAD_DIM:(i + 1) * HEAD_DIM]

    row = lambda w: pl.BlockSpec((tk, w), lambda k: (k, 0))
    return pl.pallas_call(
        body, name="wgrad_in", grid=(nk,),
        in_specs=[row(512)] * 5 + [row(256), row(D_MODEL)],
        out_specs=pl.BlockSpec((IN_WIDTH, D_MODEL), lambda k: (0, 0)),
        out_shape=jax.ShapeDtypeStruct((IN_WIDTH, D_MODEL), F32),
        compiler_params=_params("arbitrary", vmem=VMEM_LIMIT),
    )(*pieces, x2d)


def _inproj_bwd(dz1, pieces, w_main, w_kv, rider=None):
    t = dz1.shape[0]
    tm = 512

    def body(dz_ref, p0, p1, p2, p3, p4, pkv, wm_ref, wkv_ref, o_ref):
        acc = ALPHA * dz_ref[...]
        for i, ref in enumerate((p0, p1, p2, p3, p4)):
            acc = acc + _dot(ref[...], wm_ref[i * 512:(i + 1) * 512, :])
        o_ref[...] = acc + _dot(pkv[...], wkv_ref[...])

    row = lambda w: pl.BlockSpec((tm, w), lambda i: (i, 0))
    const = lambda s: pl.BlockSpec(s, lambda i: (0, 0))
    return _hosted_call(
        body, "inproj_bwd", (t // tm,),
        in_specs=[row(D_MODEL)] + [row(512)] * 5 + [row(256), const(w_main.shape), const(w_kv.shape)],
        out_specs=[row(D_MODEL)],
        out_shape=[jax.ShapeDtypeStruct((t, D_MODEL), F32)],
        scratch_shapes=[], operands=(dz1, *pieces, w_main, w_kv), rider=rider)


def _coords():
    return lax.axis_index("x"), lax.axis_index("y"), lax.axis_index("c")


def _chip_of(x, y, rel):
    return (1 - x if rel & 2 else x), (1 - y if rel & 1 else y)


def _all_gather_weights(shards):
    first = _gather_chips_rider(shards)
    second = _gather_pass_rider([jax.ShapeDtypeStruct((N_SHARD,) + s.shape, s.dtype) for s in shards], chained=True)
    return _run_riders("gather_weights", shards, first.out_shapes, [first, second])


def _run_riders(name, ins, out_shapes, riders):
    n_in, n_out = len(ins), len(out_shapes)

    def body(*refs):
        in_refs, out_refs = refs[:n_in], refs[n_in:n_in + n_out]
        k = n_in + n_out
        for r in riders:
            sems = refs[k:k + len(r.sems)]
            k += len(r.sems)
            r.start(in_refs, out_refs, sems)
            r.finish(in_refs, out_refs, sems)

    hbm = pl.BlockSpec(memory_space=pl.ANY)
    return pl.pallas_call(
        body, name=name, in_specs=[hbm] * n_in, out_specs=[hbm] * n_out, out_shape=list(out_shapes),
        scratch_shapes=[s for r in riders for s in r.sems],
    )(*ins)


def _gather_half(outs, w, chip, cc):
    h = outs[w].shape[1] // 2
    return outs[w].at[chip, pl.ds(cc * h, h), :]


def _gather_chips_rider(shards):
    nw = len(shards)

    def copies(ins, outs, sems, arrivals):
        send, recv, lsend, lrecv = sems
        x, y, c = _coords()
        me = 2 * x + y
        own = [pltpu.make_async_remote_copy(
            src_ref=ins[w], dst_ref=outs[w].at[me], send_sem=lsend.at[w], recv_sem=lrecv.at[w],
            device_id=(x, y, 1 - c), device_id_type=MESH) for w in range(nw)]
        out, arrive = [], []
        for rel in (1, 2, 3):
            kx, ky = _chip_of(x, y, rel)
            for w in range(nw):
                h = shards[w].shape[0] // 2
                sem = dict(send_sem=send.at[w * 3 + rel - 1], recv_sem=recv.at[w * 3 + rel - 1],
                           device_id=(kx, ky, c), device_id_type=MESH)
                out.append(pltpu.make_async_remote_copy(
                    src_ref=ins[w].at[pl.ds(c * h, h), :], dst_ref=_gather_half(outs, w, me, c), **sem))
                if arrivals:
                    theirs = _gather_half(outs, w, 2 * kx + ky, c)
                    arrive.append(pltpu.make_async_remote_copy(src_ref=theirs, dst_ref=theirs, **sem))
        return own, out, arrive

    def start(ins, outs, sems):
        own, out, _ = copies(ins, outs, sems, arrivals=False)
        for cp in own + out:
            cp.start()

    def finish(ins, outs, sems):
        own, out, arrive = copies(ins, outs, sems, arrivals=True)
        for cp in arrive:
            cp.wait_recv()
        for cp in out:
            cp.wait_send()
        for cp in own:
            cp.wait()

    dma = pltpu.SemaphoreType.DMA
    return _Rider(shards, [jax.ShapeDtypeStruct((N_SHARD,) + s.shape, s.dtype) for s in shards],
                  [dma((3 * nw,)), dma((3 * nw,)), dma((nw,)), dma((nw,))], start, finish)


def _gather_pass_rider(gathered, chained=False):
    nw = len(gathered)

    def copies(outs, sems, cc):
        send, recv = sems
        x, y, c = _coords()
        res = []
        for rel in (1, 2, 3):
            kx, ky = _chip_of(x, y, rel)
            for w in range(nw):
                rows = _gather_half(outs, w, 2 * kx + ky, cc)
                res.append(pltpu.make_async_remote_copy(
                    src_ref=rows, dst_ref=rows, send_sem=send.at[w * 3 + rel - 1], recv_sem=recv.at[w * 3 + rel - 1],
                    device_id=(x, y, 1 - c), device_id_type=MESH))
        return res

    def start(ins, outs, sems):
        for cp in copies(outs, sems, lax.axis_index("c")):
            cp.start()

    def finish(ins, outs, sems):
        c = lax.axis_index("c")
        for cp in copies(outs, sems, 1 - c):
            cp.wait_recv()
        for cp in copies(outs, sems, c):
            cp.wait_send()

    dma = pltpu.SemaphoreType.DMA
    shapes = [jax.ShapeDtypeStruct(g.shape, g.dtype) for g in gathered]
    if chained:
        return _Rider([], [], [dma((3 * nw,)), dma((3 * nw,))], start, finish)
    return _Rider(gathered, shapes, [dma((3 * nw,)), dma((3 * nw,))], start, finish,
                  aliases={w: w for w in range(nw)})


def _exchange_halves_rider(parts):
    nw = len(parts)

    def copies(ins, outs, sems):
        send, recv = sems
        x, y, c = _coords()
        res = []
        for w in range(nw):
            h = parts[w].shape[1] // 2
            res.append(pltpu.make_async_remote_copy(
                src_ref=ins[w].at[:, pl.ds((1 - c) * h, h), :], dst_ref=outs[w],
                send_sem=send.at[w], recv_sem=recv.at[w], device_id=(x, y, 1 - c), device_id_type=MESH))
        return res

    def start(ins, outs, sems):
        for cp in copies(ins, outs, sems):
            cp.start()

    def finish(ins, outs, sems):
        for cp in copies(ins, outs, sems):
            cp.wait()

    dma = pltpu.SemaphoreType.DMA
    return _Rider(parts, [jax.ShapeDtypeStruct((N_SHARD, p.shape[1] // 2, p.shape[2]), p.dtype) for p in parts],
                  [dma((nw,)), dma((nw,))], start, finish)


def _add_halves(parts, theirs, pos):
    nw = len(parts)
    split = 2

    def body(pos_ref, *refs):
        ins, oth = refs[:nw], refs[nw:2 * nw]
        o32, o16 = refs[2 * nw:3 * nw], refs[3 * nw:]
        sums = [ins[w][...] + oth[w][...].astype(F32) for w in range(nw)]
        for w in range(nw):
            o16[w][...] = sums[w].astype(BF16)

        @pl.when(pl.program_id(1) == pos_ref[0])
        def _():
            for w in range(nw):
                o32[w][...] = sums[w]

    in_specs, oth_specs, o32_specs, shapes32, shapes16 = [], [], [], [], []
    for p in parts:
        hb = p.shape[1] // 2 // split
        blk = (None, hb, p.shape[2])
        in_specs.append(pl.BlockSpec(blk, lambda i, j, pos_ref: (j, pos_ref[1] * split + i, 0)))
        oth_specs.append(pl.BlockSpec(blk, lambda i, j, pos_ref: (j, i, 0)))
        o32_specs.append(pl.BlockSpec((hb, p.shape[2]), lambda i, j, pos_ref: (i, 0)))
        shapes32.append(jax.ShapeDtypeStruct((p.shape[1] // 2, p.shape[2]), F32))
        shapes16.append(jax.ShapeDtypeStruct((N_SHARD, p.shape[1] // 2, p.shape[2]), BF16))
    return pl.pallas_call(
        body, name="add_halves",
        grid_spec=pltpu.PrefetchScalarGridSpec(
            num_scalar_prefetch=1, grid=(split, N_SHARD),
            in_specs=in_specs + oth_specs, out_specs=o32_specs + oth_specs),
        out_shape=shapes32 + shapes16,
        compiler_params=_params("parallel", "arbitrary", vmem=VMEM_LIMIT),
    )(pos, *parts, *theirs)


def _exchange_chips_rider(sums16):
    nw = len(sums16)

    def copies(ins, outs, sems):
        send, recv = sems
        x, y, c = _coords()
        res = []
        for rel in (1, 2, 3):
            kx, ky = _chip_of(x, y, rel)
            for w in range(nw):
                res.append(pltpu.make_async_remote_copy(
                    src_ref=ins[w].at[2 * kx + ky], dst_ref=outs[w].at[rel - 1],
                    send_sem=send.at[w * 3 + rel - 1], recv_sem=recv.at[w * 3 + rel - 1],
                    device_id=(kx, ky, c), device_id_type=MESH))
        return res

    def start(ins, outs, sems):
        for cp in copies(ins, outs, sems):
            cp.start()

    def finish(ins, outs, sems):
        for cp in copies(ins, outs, sems):
            cp.wait()

    dma = pltpu.SemaphoreType.DMA
    return _Rider(sums16, [jax.ShapeDtypeStruct((3,) + s.shape[1:], BF16) for s in sums16],
                  [dma((3 * nw,)), dma((3 * nw,))], start, finish)


def _add_chips(sums32, theirs, pos):
    nw = len(sums32)
    split = 2

    def body(pos_ref, *refs):
        ins, oth, outs = refs[:nw], refs[nw:2 * nw], refs[2 * nw:]
        for w in range(nw):
            acc = ins[w][...]
            for r in range(3):
                acc = acc + oth[w][r].astype(F32)
            outs[w][...] = acc

    in_specs, oth_specs, out_specs, shapes = [], [], [], []
    for s in sums32:
        hb = s.shape[0] // split
        in_specs.append(pl.BlockSpec((hb, s.shape[1]), lambda i, pos_ref: (i, 0)))
        oth_specs.append(pl.BlockSpec((3, hb, s.shape[1]), lambda i, pos_ref: (0, i, 0)))
        out_specs.append(pl.BlockSpec((hb, s.shape[1]), lambda i, pos_ref: (pos_ref[1] * split + i, 0)))
        shapes.append(jax.ShapeDtypeStruct((2 * s.shape[0], s.shape[1]), F32))
    return pl.pallas_call(
        body, name="add_chips",
        grid_spec=pltpu.PrefetchScalarGridSpec(
            num_scalar_prefetch=1, grid=(split,), in_specs=in_specs + oth_specs, out_specs=out_specs),
        out_shape=shapes,
        compiler_params=_params("parallel", vmem=VMEM_LIMIT),
    )(pos, *sums32, *theirs)


def _join_halves(shards):
    nw = len(shards)

    def body(*refs):
        outs = refs[nw:2 * nw]
        send, recv = refs[2 * nw:]
        x, y, c = _coords()

        def copy(w, cc):
            h = shards[w].shape[0] // 2
            rows = outs[w].at[pl.ds(cc * h, h), :]
            return pltpu.make_async_remote_copy(
                src_ref=rows, dst_ref=rows, send_sem=send.at[w], recv_sem=recv.at[w],
                device_id=(x, y, 1 - c), device_id_type=MESH)

        for w in range(nw):
            copy(w, c).start()
        for w in range(nw):
            copy(w, 1 - c).wait_recv()
            copy(w, c).wait_send()

    hbm = pl.BlockSpec(memory_space=pl.ANY)
    return pl.pallas_call(
        body, name="join_halves",
        in_specs=[hbm] * nw, out_specs=[hbm] * nw,
        out_shape=[jax.ShapeDtypeStruct(s.shape, F32) for s in shards],
        input_output_aliases={w: w for w in range(nw)},
        scratch_shapes=[pltpu.SemaphoreType.DMA((nw,)), pltpu.SemaphoreType.DMA((nw,))],
    )(*shards)


def _adamw_math(w, g, m, v):
    m = ADAM_B1 * m + (1.0 - ADAM_B1) * g
    v = ADAM_B2 * v + (1.0 - ADAM_B2) * (g * g)
    m_hat = m / (1.0 - ADAM_B1 ** ADAM_STEP)
    v_hat = v / (1.0 - ADAM_B2 ** ADAM_STEP)
    delta = -ADAM_LR * (m_hat / (jnp.sqrt(v_hat) + ADAM_EPS) + ADAM_WD * w)
    return delta, m, v


def _adamw(ws, gs, ms, vs):
    nw = len(ws)
    split = 8

    def body(*refs):
        w_r, g_r, m_r, v_r = (refs[i * nw:(i + 1) * nw] for i in range(4))
        g_o, d_o, m_o, v_o = (refs[(4 + i) * nw:(5 + i) * nw] for i in range(4))
        for k in range(nw):
            g = g_r[k][...]
            d, m, v = _adamw_math(w_r[k][...], g, m_r[k][...], v_r[k][...])
            g_o[k][...] = g
            d_o[k][...] = d
            m_o[k][...] = m
            v_o[k][...] = v

    specs = [pl.BlockSpec((w.shape[0] // split, w.shape[1]), lambda i: (i, 0)) for w in ws]
    shapes = [jax.ShapeDtypeStruct(w.shape, F32) for w in ws]
    outs = pl.pallas_call(
        body, name="adamw", grid=(split,),
        in_specs=specs * 4, out_specs=specs * 4, out_shape=shapes * 4,
        compiler_params=_params("parallel", vmem=VMEM_LIMIT),
    )(*ws, *gs, *ms, *vs)
    return outs[:nw], outs[nw:2 * nw], outs[2 * nw:3 * nw], outs[3 * nw:]


SMALL_ROWS = 8
SMALL_COLS = D_MODEL
LOSS_COL = RET_WIDTH + 24


def _small_allreduce_adamw(part, w, m, v, rider=None):
    def body(part_ref, w_ref, m_ref, v_ref, g_out, d_out, m_out, v_out, all_ref, send, recv):
        x, y, c = _coords()
        me = 4 * x + 2 * y + c
        all_ref[me] = part_ref[...]
        copies = []
        for rel in range(1, 8):
            px = 1 - x if rel & 4 else x
            py = 1 - y if rel & 2 else y
            pc = 1 - c if rel & 1 else c
            copies.append(pltpu.make_async_remote_copy(
                src_ref=part_ref, dst_ref=all_ref.at[me],
                send_sem=send.at[rel - 1], recv_sem=recv.at[rel - 1], device_id=(px, py, pc), device_id_type=MESH))
        for cp in copies:
            cp.start()
        for cp in copies:
            cp.wait()
        g = all_ref[0]
        for k in range(1, 8):
            g = g + all_ref[k]
        d, mn, vn = _adamw_math(w_ref[...], g, m_ref[...], v_ref[...])
        g_out[...] = g
        d_out[...] = d
        m_out[...] = mn
        v_out[...] = vn

    vm = pl.BlockSpec(memory_space=pltpu.VMEM)
    shape = jax.ShapeDtypeStruct((SMALL_ROWS, SMALL_COLS), F32)
    return _hosted_call(
        body, "small_allreduce_adamw", (1,),
        in_specs=[vm] * 4, out_specs=[vm] * 4, out_shape=[shape] * 4,
        scratch_shapes=[pltpu.VMEM((8, SMALL_ROWS, SMALL_COLS), F32),
                        pltpu.SemaphoreType.DMA((7,)), pltpu.SemaphoreType.DMA((7,))],
        operands=(part, w, m, v), rider=rider, semantics=["arbitrary"])


SMALL_NAMES = ("ret_decay_fwd", "ret_decay_bwd", "attn_sink", "ret_gn_gain",
               "ln1_gain", "ln1_bias", "ln2_gain", "ln2_bias")


LN_NAMES = ("ln1_gain", "ln1_bias", "ln2_gain", "ln2_bias")


def _pack_small(vals, extra=None):
    tail = jnp.zeros((1, 1), F32) if extra is None else extra.reshape(1, 1)
    row4 = jnp.concatenate([vals["ret_gn_gain"], vals["ret_decay_fwd"], vals["ret_decay_bwd"], vals["attn_sink"],
                            tail, jnp.zeros((1, SMALL_COLS - LOSS_COL - 1), F32)], axis=1)
    rows = [vals[n] for n in LN_NAMES] + [row4, jnp.zeros((SMALL_ROWS - 5, SMALL_COLS), F32)]
    return jnp.concatenate(rows, axis=0)


def _unpack_small(packed):
    out = {n: packed[i:i + 1] for i, n in enumerate(LN_NAMES)}
    o = RET_WIDTH
    out.update(ret_gn_gain=packed[4:5, 0:o], ret_decay_fwd=packed[4:5, o:o + 8],
               ret_decay_bwd=packed[4:5, o + 8:o + 16], attn_sink=packed[4:5, o + 16:o + 24])
    return out


def _local_step(x, p, tgt, w_in_t, rest, small, pos=None, small_state=None):
    bsz, s, _ = x.shape
    t = bsz * s
    x2d = x.reshape(t, D_MODEL)
    p2d = p.reshape(t, PLE_DIM)
    tgt2d = tgt.reshape(t, D_MODEL)
    dec_f = small["ret_decay_fwd"].reshape(8)
    dec_b = small["ret_decay_bwd"].reshape(8)
    lg_f = jnp.log1p(-jnp.exp2(dec_f))
    lg_b = jnp.log1p(-jnp.exp2(dec_b))
    per_lane = lambda v: jnp.repeat(v, HEAD_DIM).reshape(4, 1, LANES)
    lgf_l, lgb_l = per_lane(lg_f), per_lane(lg_b)
    sink = small["attn_sink"].reshape(8)
    slopes = 2.0 ** (-(jnp.arange(8, dtype=F32) + 1.0))
    gn_gain = small["ret_gn_gain"]
    g1, b1, g2, b2 = (small[n] for n in ("ln1_gain", "ln1_bias", "ln2_gain", "ln2_bias"))

    dist = pos is not None
    chips = lambda names: _gather_chips_rider([rest[REST_NAMES.index(n)] for n in names])
    first, second, third = ("w_ffn_up",), ("w_out", "w_ffn_gate", "w_ple_proj", "w_ple_gate"), ("w_ffn_down",)
    u, *c1 = _inproj(x2d, w_in_t, rider=chips(first) if dist else None)
    u3 = u.reshape(bsz, s, IN_WIDTH)
    y_hat, y_rstd, y_ret, ret_rb, ret_kvf, *o2 = _ret_fwd(u3, lgf_l, lgb_l, gn_gain,
                                 rider=_merge_riders([_gather_pass_rider(c1), chips(second)]) if dist else None)
    y_att, att_p, att_ps, *o3 = _attn_fwd(u3, slopes, sink, rider=_merge_riders(
        [_gather_pass_rider(o2[len(first):]), chips(third)]) if dist else None)
    gathered = dict(zip(first, o2[:len(first)]))
    gathered.update(zip(second, o3[:len(second)]))
    w_out = _assemble_weights({"w_out": gathered["w_out"]})["w_out"] if dist else rest["w_out"]
    zh1, r1, hb, *o4 = _outproj_ln1(y_ret.reshape(t, RET_WIDTH), y_att.reshape(t, ATTN_WIDTH), x2d, w_out, g1, b1,
                                    rider=_gather_pass_rider(o3[len(second):]) if dist else None)
    gathered.update(zip(third, o4))
    wts = _assemble_weights(gathered) if dist else rest
    dz2, dz2b, gs, us, acts, pg, ple, sq, dg2, db2 = _ffn_fwd(
        zh1, hb, p2d, tgt2d, g1, b1, g2, b2, wts["gate4"], wts["up4"], wts["down4"], wts["ple_proj"], wts["ple_gate"])
    dgs, dus, dsp, dple, dz1, dyr, dya, dg1, db1 = _ffn_bwd(dz2, gs, us, pg, ple, zh1, r1, g1, wts["gate4"],
                                                          wts["up4"], wts["down4"], wts["ple_gate"], wts["w_out"])
    ffn_parts = list(_wgrad_ffn(acts, dgs, dus, hb, dz2b))
    d_w_out, d_ple_gate, d_ple_proj, *th_ffn = _wgrad_misc(
        y_ret.reshape(t, RET_WIDTH), y_att.reshape(t, ATTN_WIDTH), dz1, hb, dsp, p2d, dple,
        rider=_exchange_halves_rider(ffn_parts) if dist else None)
    misc_parts = [d_w_out.reshape(N_SHARD, D_MODEL // N_SHARD, D_MODEL),
                  d_ple_proj.reshape(PLE_DIM, N_SHARD, D_MODEL // N_SHARD).transpose(1, 0, 2),
                  d_ple_gate.reshape(N_SHARD, D_MODEL // N_SHARD, D_MODEL)]
    dyr3, dya3 = dyr.reshape(bsz, s, RET_WIDTH), dya.reshape(bsz, s, ATTN_WIDTH)
    if dist:
        s_ffn = _add_halves(ffn_parts, th_ffn, pos)
        drq, drk, drv, drg, rpart, *o5 = _ret_bwd(u3, y_hat, y_rstd, (ret_rb, ret_kvf), dyr3, lgf_l, lgb_l, gn_gain,
                                                  rider=_merge_riders(
            [_exchange_chips_rider(s_ffn[3:5]), _exchange_halves_rider(misc_parts)]))
        s_misc = _add_halves(misc_parts, o5[2:], pos)
        daq, dakv, spart, *o6 = _attn_bwd(u3, dya3, att_p, att_ps,
                                          rider=_exchange_chips_rider([s_ffn[5]] + list(s_misc[3:])))
    else:
        drq, drk, drv, drg, rpart = _ret_bwd(u3, y_hat, y_rstd, (ret_rb, ret_kvf), dyr3, lgf_l, lgb_l, gn_gain)
        daq, dakv, spart = _attn_bwd(u3, dya3, att_p, att_ps)
    pieces = [a.reshape(t, -1) for a in (drq, drk, drv, drg, daq, dakv)]
    kv0 = CB_AK * LANES
    w_kv = jnp.concatenate([w_in_t[kv0 + o:kv0 + o + HEAD_DIM] for o in KV_ORDER], axis=0)
    d_in = _wgrad_in(pieces, x2d).reshape(N_SHARD, FFN_SHARD, D_MODEL)

    rsum = rpart
    lane_heads = lambda row: jnp.sum(row.reshape(4, 2, HEAD_DIM), axis=-1).reshape(8)
    dlg_f = lane_heads(rsum[:, 0, :]) + jnp.stack([jnp.sum(rsum[:, 2, :], -1), jnp.sum(rsum[:, 3, :], -1)], 1).reshape(8)
    dlg_b = lane_heads(rsum[:, 1, :]) + jnp.stack([jnp.sum(rsum[:, 4, :], -1), jnp.sum(rsum[:, 5, :], -1)], 1).reshape(8)
    chain = lambda d: -(math.log(2.0) * jnp.exp2(d)) / (1.0 - jnp.exp2(d))
    grads_small = {
        "ret_decay_fwd": (dlg_f * chain(dec_f)).reshape(1, 8),
        "ret_decay_bwd": (dlg_b * chain(dec_b)).reshape(1, 8),
        "attn_sink": jnp.sum(spart, axis=0)[:, 0:4, 0].reshape(1, 8),
        "ret_gn_gain": rsum[:, 6, :].reshape(1, RET_WIDTH),
        "ln1_gain": dg1, "ln1_bias": db1, "ln2_gain": dg2, "ln2_bias": db2,
    }
    if not dist:
        grad_x, = _inproj_bwd(dz1, pieces, w_in_t[:kv0], w_kv)
        grads_rest = [misc_parts[0]] + ffn_parts + misc_parts[1:]
        return sq[0, 0], grad_x.reshape(bsz, s, D_MODEL), d_in, grads_rest, grads_small
    *small_out, th_in = _small_allreduce_adamw(_pack_small(grads_small, sq[0, 0]), *small_state,
                                               rider=_exchange_halves_rider([d_in]))
    s_in = _add_halves([d_in], [th_in], pos)
    grad_x, chips_in = _inproj_bwd(dz1, pieces, w_in_t[:kv0], w_kv, rider=_exchange_chips_rider([s_in[1]]))
    sums32 = [s_in[0], s_misc[0], s_ffn[0], s_ffn[1], s_ffn[2], s_misc[1], s_misc[2]]
    from_chips = [chips_in, o6[1], o5[0], o5[1], o6[0], o6[2], o6[3]]
    return grad_x.reshape(bsz, s, D_MODEL), sums32, from_chips, small_out


BIG_NAMES = ("w_in", "w_out", "w_ffn_gate", "w_ffn_up", "w_ffn_down", "w_ple_proj", "w_ple_gate")
REST_NAMES = BIG_NAMES[1:]
TRANSPOSED = ("w_in", "w_ffn_gate", "w_ffn_up")
WEIGHT_ORDER = ("w_in", "ret_decay_fwd", "ret_decay_bwd", "ret_gn_gain", "attn_sink", "w_out", "ln1_gain",
                "ln1_bias", "w_ffn_gate", "w_ffn_up", "w_ffn_down", "w_ple_proj", "w_ple_gate", "ln2_gain", "ln2_bias")


def _shard_rows(name, a):
    return jnp.swapaxes(a[0], 0, 1) if name in TRANSPOSED else a[0]


def _unshard_rows(name, a):
    return (jnp.swapaxes(a, 0, 1) if name in TRANSPOSED else a)[None]


def _assemble_weights(gathered):
    cols = lambda a: a.transpose(1, 0, 2).reshape(a.shape[1], N_SHARD * a.shape[2])
    rows = lambda a: a.reshape(N_SHARD * a.shape[1], a.shape[2])
    same = lambda a: a
    layout = {"w_out": ("w_out", rows), "w_ffn_gate": ("gate4", same), "w_ffn_up": ("up4", same),
              "w_ffn_down": ("down4", same), "w_ple_proj": ("ple_proj", cols), "w_ple_gate": ("ple_gate", rows)}
    return {layout[n][0]: layout[n][1](a) for n, a in gathered.items()}


def kernel(x, p, w_in, ret_decay_fwd, ret_decay_bwd, ret_gn_gain, attn_sink, w_out, ln1_gain, ln1_bias, w_ffn_gate, w_ffn_up, w_ffn_down, w_ple_proj, w_ple_gate, ln2_gain, ln2_bias, loss_target, m_w_in, m_ret_decay_fwd, m_ret_decay_bwd, m_ret_gn_gain, m_attn_sink, m_w_out, m_ln1_gain, m_ln1_bias, m_w_ffn_gate, m_w_ffn_up, m_w_ffn_down, m_w_ple_proj, m_w_ple_gate, m_ln2_gain, m_ln2_bias, v_w_in, v_ret_decay_fwd, v_ret_decay_bwd, v_ret_gn_gain, v_attn_sink, v_w_out, v_ln1_gain, v_ln1_bias, v_w_ffn_gate, v_w_ffn_up, v_w_ffn_down, v_w_ple_proj, v_w_ple_gate, v_ln2_gain, v_ln2_bias):
    w = dict(w_in=w_in, ret_decay_fwd=ret_decay_fwd, ret_decay_bwd=ret_decay_bwd, ret_gn_gain=ret_gn_gain,
             attn_sink=attn_sink, w_out=w_out, ln1_gain=ln1_gain, ln1_bias=ln1_bias, w_ffn_gate=w_ffn_gate,
             w_ffn_up=w_ffn_up, w_ffn_down=w_ffn_down, w_ple_proj=w_ple_proj, w_ple_gate=w_ple_gate,
             ln2_gain=ln2_gain, ln2_bias=ln2_bias)
    m = dict(w_in=m_w_in, ret_decay_fwd=m_ret_decay_fwd, ret_decay_bwd=m_ret_decay_bwd, ret_gn_gain=m_ret_gn_gain,
             attn_sink=m_attn_sink, w_out=m_w_out, ln1_gain=m_ln1_gain, ln1_bias=m_ln1_bias, w_ffn_gate=m_w_ffn_gate,
             w_ffn_up=m_w_ffn_up, w_ffn_down=m_w_ffn_down, w_ple_proj=m_w_ple_proj, w_ple_gate=m_w_ple_gate,
             ln2_gain=m_ln2_gain, ln2_bias=m_ln2_bias)
    v = dict(w_in=v_w_in, ret_decay_fwd=v_ret_decay_fwd, ret_decay_bwd=v_ret_decay_bwd, ret_gn_gain=v_ret_gn_gain,
             attn_sink=v_attn_sink, w_out=v_w_out, ln1_gain=v_ln1_gain, ln1_bias=v_ln1_bias, w_ffn_gate=v_w_ffn_gate,
             w_ffn_up=v_w_ffn_up, w_ffn_down=v_w_ffn_down, w_ple_proj=v_w_ple_proj, w_ple_gate=v_w_ple_gate,
             ln2_gain=v_ln2_gain, ln2_bias=v_ln2_bias)
    big = lambda d: [_shard_rows(n, d[n]) for n in BIG_NAMES]
    small = lambda d: {n: d[n] for n in SMALL_NAMES}

    chip = 2 * lax.axis_index("x") + lax.axis_index("y")
    pos = jnp.stack([chip, lax.axis_index("c")]).astype(jnp.int32)

    shards = [a.astype(BF16) for a in big(w)]
    (w_in4,) = _all_gather_weights(shards[:1])
    w_in_t = w_in4.reshape(IN_WIDTH, D_MODEL)
    grad_x, sums32, from_chips, (g_s, d_s, m_s, v_s) = _local_step(
        x, p[0], loss_target, w_in_t, shards[1:], small(w), pos=pos,
        small_state=(_pack_small(small(w)), _pack_small(small(m)), _pack_small(small(v))))
    g_big, d_big, m_big, v_big = _adamw(big(w), _join_halves(_add_chips(sums32, from_chips, pos)), big(m), big(v))
    loss = g_s[4, LOSS_COL] * (0.5 / D_MODEL)

    def tree(bigs, packed):
        out = {n: _unshard_rows(n, a) for n, a in zip(BIG_NAMES, bigs)}
        out.update(_unpack_small(packed))
        return [out[n] for n in WEIGHT_ORDER]

    return (loss, grad_x, *tree(g_big, g_s), *tree(d_big, d_s), *tree(m_big, m_s), *tree(v_big, v_s))
```

```python
import functools
import math

import jax
import jax.numpy as jnp
from jax import lax
from jax.experimental import pallas as pl
from jax.experimental.pallas import tpu as pltpu

F32 = jnp.float32
BF16 = jnp.bfloat16

D_MODEL = 1024
HEAD_DIM = 64
RET_HEADS = 8
ATTN_HEADS = 8
RET_WIDTH = 512
ATTN_WIDTH = 512
KV_WIDTH = 128
IN_WIDTH = 2816
FFN = 2816
N_SHARD = 4
FFN_SHARD = FFN // N_SHARD
PLE_DIM = 256
CHUNK = 128
LANES = 128
ALPHA = 2.0 ** 0.25
LN_EPS = 1e-5
GN_EPS = 1e-5
NEG_INF = -1e30
ADAM_LR = 0.001
ADAM_B1 = 0.9
ADAM_B2 = 0.999
ADAM_EPS = 1e-08
ADAM_WD = 0.01
ADAM_STEP = 10
VMEM_LIMIT = 56 * 1024 * 1024
MESH = pl.DeviceIdType.MESH

CB_RQ, CB_RK, CB_RV, CB_RG, CB_AQ, CB_AK, CB_AV = 0, 4, 8, 12, 16, 20, 21


def _dot(a, b):
    return jnp.dot(a, b, preferred_element_type=F32)


def _dot_nt(a, b):
    return lax.dot_general(a, b, (((1,), (1,)), ((), ())), preferred_element_type=F32)


def _dot_tn(a, b):
    return lax.dot_general(a, b, (((0,), (0,)), ((), ())), preferred_element_type=F32)


def _sigmoid(x):
    return 1.0 / (1.0 + jnp.exp(-x))


def _params(*sem, vmem=None):
    return pltpu.CompilerParams(dimension_semantics=tuple(sem) if sem else None, vmem_limit_bytes=vmem)


class _Rider:
    def __init__(self, ins, out_shapes, sems, start, finish, aliases=None):
        self.ins, self.out_shapes, self.sems = list(ins), list(out_shapes), list(sems)
        self.start, self.finish, self.aliases = start, finish, dict(aliases or {})


def _merge_riders(riders):
    riders = [r for r in riders if r is not None]
    if len(riders) == 1:
        return riders[0]
    bounds, aliases = [], {}
    i0 = o0 = s0 = 0
    for r in riders:
        bounds.append((i0, o0, s0))
        aliases.update({i0 + i: o0 + o for i, o in r.aliases.items()})
        i0, o0, s0 = i0 + len(r.ins), o0 + len(r.out_shapes), s0 + len(r.sems)

    def each(method):
        def run(ins, outs, sems):
            for r, (i, o, s) in zip(riders, bounds):
                getattr(r, method)(ins[i:i + len(r.ins)], outs[o:o + len(r.out_shapes)], sems[s:s + len(r.sems)])
        return run

    return _Rider([a for r in riders for a in r.ins], [a for r in riders for a in r.out_shapes],
                  [a for r in riders for a in r.sems], each("start"), each("finish"), aliases)


def _hosted_call(body, name, grid, in_specs, out_specs, out_shape, scratch_shapes, operands, rider=None,
                 semantics=None):
    n_in, n_out, n_scr = len(in_specs), len(out_specs), len(scratch_shapes)
    if rider is None:
        return pl.pallas_call(
            body, name=name, grid=grid, in_specs=in_specs, out_specs=out_specs, out_shape=out_shape,
            scratch_shapes=scratch_shapes,
            compiler_params=_params(*(semantics or ["parallel"] * len(grid)), vmem=VMEM_LIMIT))(*operands)
    r_in, r_out = len(rider.ins), len(rider.out_shapes)

    def full_body(*refs):
        main_in, rin = refs[:n_in], refs[n_in:n_in + r_in]
        o0 = n_in + r_in
        main_out, rout = refs[o0:o0 + n_out], refs[o0 + n_out:o0 + n_out + r_out]
        s0 = o0 + n_out + r_out
        main_scr, rsem = refs[s0:s0 + n_scr], refs[s0 + n_scr:]
        first = functools.reduce(jnp.logical_and, [pl.program_id(a) == 0 for a in range(len(grid))])
        last = functools.reduce(jnp.logical_and, [pl.program_id(a) == g - 1 for a, g in enumerate(grid)])

        @pl.when(first)
        def _():
            rider.start(rin, rout, rsem)

        body(*main_in, *main_out, *main_scr)

        @pl.when(last)
        def _():
            rider.finish(rin, rout, rsem)

    hbm = pl.BlockSpec(memory_space=pl.ANY)
    return pl.pallas_call(
        full_body, name=name, grid=grid,
        in_specs=list(in_specs) + [hbm] * r_in, out_specs=list(out_specs) + [hbm] * r_out,
        out_shape=list(out_shape) + rider.out_shapes,
        scratch_shapes=list(scratch_shapes) + rider.sems,
        input_output_aliases={n_in + i: n_out + o for i, o in rider.aliases.items()},
        compiler_params=_params(*(["arbitrary"] * len(grid)), vmem=VMEM_LIMIT),
    )(*operands, *rider.ins)


def _head_mean(x, m0):
    s0 = jnp.sum(jnp.where(m0, x, 0.0), axis=1, keepdims=True)
    s1 = jnp.sum(jnp.where(m0, 0.0, x), axis=1, keepdims=True)
    return jnp.where(m0, s0, s1) * (1.0 / HEAD_DIM)


def _inproj(x2d, w_in_t, rider=None):
    t = x2d.shape[0]
    tm = 512
    nb = 256

    def body(x_ref, w_ref, o_ref):
        xb = x_ref[...].astype(BF16)
        for n in range(0, IN_WIDTH, nb):
            o_ref[:, n:n + nb] = _dot_nt(xb, w_ref[n:n + nb, :]).astype(BF16)

    return _hosted_call(
        body, "inproj", (t // tm,),
        in_specs=[pl.BlockSpec((tm, D_MODEL), lambda i: (i, 0)),
                  pl.BlockSpec((IN_WIDTH, D_MODEL), lambda i: (0, 0))],
        out_specs=[pl.BlockSpec((tm, IN_WIDTH), lambda i: (i, 0))],
        out_shape=[jax.ShapeDtypeStruct((t, IN_WIDTH), BF16)],
        scratch_shapes=[], operands=(x2d, w_in_t), rider=rider)


def _outproj_ln1(y_ret, y_att, x2d, w_out, gain, bias, rider=None):
    t = x2d.shape[0]
    tm = 512

    def body(yr_ref, ya_ref, x_ref, w_ref, g_ref, b_ref, zh_ref, r_ref, hb_ref):
        mix = _dot(yr_ref[...], w_ref[0:RET_WIDTH, :]) + _dot(ya_ref[...], w_ref[RET_WIDTH:, :])
        z = ALPHA * x_ref[...] + mix
        mu = jnp.mean(z, axis=1, keepdims=True)
        zc = z - mu
        var = jnp.mean(zc * zc, axis=1, keepdims=True)
        r = lax.rsqrt(var + LN_EPS)
        zh = zc * r
        zh_ref[...] = zh
        r_ref[...] = r
        hb_ref[...] = (zh * g_ref[...] + b_ref[...]).astype(BF16)

    row = lambda w: pl.BlockSpec((tm, w), lambda i: (i, 0))
    const = lambda s: pl.BlockSpec(s, lambda i: (0, 0))
    return _hosted_call(
        body, "outproj_ln1", (t // tm,),
        in_specs=[row(RET_WIDTH), row(ATTN_WIDTH), row(D_MODEL), const((D_MODEL, D_MODEL)),
                  const((1, D_MODEL)), const((1, D_MODEL))],
        out_specs=[row(D_MODEL), row(1), row(D_MODEL)],
        out_shape=[jax.ShapeDtypeStruct((t, D_MODEL), F32), jax.ShapeDtypeStruct((t, 1), F32),
                   jax.ShapeDtypeStruct((t, D_MODEL), BF16)],
        scratch_shapes=[], operands=(y_ret, y_att, x2d, w_out, gain, bias), rider=rider)


def _load_resident(step, pairs, sems):
    copies = [pltpu.make_async_copy(src, dst, sems.at[i]) for i, (src, dst) in enumerate(pairs)]

    @pl.when(step == 0)
    def _():
        for cp in copies:
            cp.start()
        for cp in copies:
            cp.wait()


FFN_CHUNK = 256
N_FFN_CHUNK = FFN // FFN_CHUNK


def _resident_quarters(hbm, vmem):
    q = FFN // N_SHARD
    return [(hbm.at[pl.ds(j * q, q), :], vmem.at[pl.ds(j * q, q), :]) for j in range(N_SHARD)]


def _ffn_fwd(zh1, hb, p2d, tgt, g1, b1, g2, b2, wg4, wu4, wd4, wpe, wpg):
    t = zh1.shape[0]
    tm = 256
    wg_t, wu_t, wd_all = (w.reshape(FFN, D_MODEL) for w in (wg4, wu4, wd4))

    def body(zh_ref, hb_ref, p_ref, t_ref, g1_ref, b1_ref, g2_ref, b2_ref,
             wg_hbm, wu_hbm, wd_hbm, wpe_hbm, wpg_hbm,
             dz_ref, dzb_ref, gs_ref, us_ref, act_ref, pg_ref, ple_ref, loss_ref, dg2_ref, db2_ref,
             wg, wu, wd, wpe, wpg, wsem):
        step = pl.program_id(0)
        loads = _resident_quarters(wg_hbm, wg) + _resident_quarters(wu_hbm, wu) + _resident_quarters(wd_hbm, wd)
        _load_resident(step, loads + [(wpe_hbm, wpe), (wpg_hbm, wpg)], wsem)

        @pl.when(step == 0)
        def _():
            loss_ref[...] = jnp.zeros_like(loss_ref)
            dg2_ref[...] = jnp.zeros_like(dg2_ref)
            db2_ref[...] = jnp.zeros_like(db2_ref)

        h1 = zh_ref[...] * g1_ref[...] + b1_ref[...]
        hbv = hb_ref[...]
        ffn = jnp.zeros((tm, D_MODEL), F32)
        acts = []
        chunks = [slice(n * FFN_CHUNK, (n + 1) * FFN_CHUNK) for n in range(N_FFN_CHUNK)]
        for n in range(N_FFN_CHUNK + 1):
            if n < N_FFN_CHUNK:
                gj = _dot_nt(hbv, wg[chunks[n], :])
                uj = _dot_nt(hbv, wu[chunks[n], :])
                gs_ref[:, chunks[n]] = gj.astype(BF16)
                us_ref[:, chunks[n]] = uj.astype(BF16)
                acts.append((gj * _sigmoid(gj) * uj).astype(BF16))
                act_ref[:, chunks[n]] = acts[n]
            if n > 0:
                ffn = ffn + _dot(acts[n - 1], wd[chunks[n - 1], :])
        ple = _dot(p_ref[...].astype(BF16), wpe[...])
        pg = _sigmoid(_dot(hbv, wpg[...]))
        pg_ref[...] = pg.astype(BF16)
        ple_ref[...] = ple.astype(BF16)
        z2 = ALPHA * h1 + ffn + pg * ple
        mu = jnp.mean(z2, axis=1, keepdims=True)
        zc = z2 - mu
        var = jnp.mean(zc * zc, axis=1, keepdims=True)
        r = lax.rsqrt(var + LN_EPS)
        zh2 = zc * r
        err = zh2 * g2_ref[...] + b2_ref[...] - t_ref[...]
        loss_ref[...] += jnp.sum(err * err)
        dy = err * (1.0 / D_MODEL)
        dg2_ref[...] += jnp.sum(dy * zh2, axis=0, keepdims=True)
        db2_ref[...] += jnp.sum(dy, axis=0, keepdims=True)
        dzh = dy * g2_ref[...]
        m1 = jnp.mean(dzh, axis=1, keepdims=True)
        m2 = jnp.mean(dzh * zh2, axis=1, keepdims=True)
        dz2 = r * (dzh - m1 - zh2 * m2)
        dz_ref[...] = dz2
        dzb_ref[...] = dz2.astype(BF16)

    row = lambda w: pl.BlockSpec((tm, w), lambda i: (i, 0))
    const = lambda s: pl.BlockSpec(s, lambda i: (0, 0))
    hid_shape = jax.ShapeDtypeStruct((t, FFN), BF16)
    hbm = pl.BlockSpec(memory_space=pl.ANY)
    return pl.pallas_call(
        body, name="ffn_fwd", grid=(t // tm,),
        in_specs=[row(D_MODEL), row(D_MODEL), row(PLE_DIM), row(D_MODEL),
                  const((1, D_MODEL)), const((1, D_MODEL)), const((1, D_MODEL)), const((1, D_MODEL)),
                  hbm, hbm, hbm, hbm, hbm],
        out_specs=[row(D_MODEL), row(D_MODEL), row(FFN), row(FFN), row(FFN), row(D_MODEL), row(D_MODEL),
                   const((8, LANES)), const((1, D_MODEL)), const((1, D_MODEL))],
        out_shape=[jax.ShapeDtypeStruct((t, D_MODEL), F32), jax.ShapeDtypeStruct((t, D_MODEL), BF16),
                   hid_shape, hid_shape, hid_shape,
                   jax.ShapeDtypeStruct((t, D_MODEL), BF16), jax.ShapeDtypeStruct((t, D_MODEL), BF16),
                   jax.ShapeDtypeStruct((8, LANES), F32),
                   jax.ShapeDtypeStruct((1, D_MODEL), F32), jax.ShapeDtypeStruct((1, D_MODEL), F32)],
        scratch_shapes=[pltpu.VMEM((FFN, D_MODEL), BF16), pltpu.VMEM((FFN, D_MODEL), BF16),
                        pltpu.VMEM((FFN, D_MODEL), BF16),
                        pltpu.VMEM(wpe.shape, BF16), pltpu.VMEM(wpg.shape, BF16),
                        pltpu.SemaphoreType.DMA((3 * N_SHARD + 2,))],
        compiler_params=_params("arbitrary", vmem=VMEM_LIMIT),
    )(zh1, hb, p2d, tgt, g1, b1, g2, b2, wg_t, wu_t, wd_all, wpe, wpg)


def _ret_tables(lgf, lgb):
    c = CHUNK
    row = lax.broadcasted_iota(jnp.int32, (c, LANES), 0).astype(F32)
    ii = lax.broadcasted_iota(jnp.int32, (c, c), 0).astype(F32)
    jj = lax.broadcasted_iota(jnp.int32, (c, c), 1).astype(F32)
    diff = ii - jj
    dmats = []
    for h in range(2):
        lf = lgf[:, h * HEAD_DIM:h * HEAD_DIM + 1]
        lb = lgb[:, h * HEAD_DIM:h * HEAD_DIM + 1]
        dmats.append(jnp.where(diff > 0, jnp.exp(lf * jnp.maximum(diff, 0.0)),
                               jnp.where(diff < 0, jnp.exp(lb * jnp.maximum(-diff, 0.0)), 2.0)))
    tab = dict(
        qdec_f=jnp.exp(lgf * (row + 1.0)), kdec_f=jnp.exp(lgf * (c - 1.0 - row)),
        qdec_b=jnp.exp(lgb * (c - row)), kdec_b=jnp.exp(lgb * row),
        cdec_f=jnp.exp(lgf * c), cdec_b=jnp.exp(lgb * c),
        d0=dmats[0], d1=dmats[1], row=row, diff=diff)
    r = lax.broadcasted_iota(jnp.int32, (LANES, LANES), 0) < HEAD_DIM
    cc = lax.broadcasted_iota(jnp.int32, (LANES, LANES), 1) < HEAD_DIM
    tab["bd"] = r == cc
    tab["m0"] = lax.broadcasted_iota(jnp.int32, (c, LANES), 1) < HEAD_DIM
    return tab


def _ret_specs(bsz, s):
    blk = lambda cb: pl.BlockSpec((bsz, s, LANES), lambda p, cb=cb: (0, 0, cb + p))
    lane = pl.BlockSpec((None, 1, LANES), lambda p: (p, 0, 0))
    gain = pl.BlockSpec((1, LANES), lambda p: (0, p))
    pair = pl.BlockSpec((bsz, s, LANES), lambda p: (0, 0, p))
    return blk, lane, gain, pair


def _ret_state_spec(bsz, n_chunk):
    spec = pl.BlockSpec((None, bsz, n_chunk, LANES, LANES), lambda p: (p, 0, 0, 0, 0))
    return spec, jax.ShapeDtypeStruct((4, bsz, n_chunk, LANES, LANES), F32)


def _ret_kv_states(tb, k_ref, v_ref, rb_ref, kvf_ref, n_chunk):
    c = CHUNK
    bsz = k_ref.shape[0]
    bd = tb["bd"]

    def contributions(n, carry):
        sl = pl.ds(pl.multiple_of(n * c, c), c)
        kfb = []
        for b in range(bsz):
            k32 = k_ref[b, sl, :].astype(F32)
            kfb.append(jnp.concatenate([k32 * tb["kdec_f"], k32 * tb["kdec_b"]], axis=1).astype(BF16))
        kvs = [_dot_tn(kfb[b], v_ref[b, sl, :]) for b in range(bsz)]
        for b in range(bsz):
            kvf_ref[b, n] = jnp.where(bd, kvs[b][0:LANES], 0.0)
            rb_ref[b, n] = jnp.where(bd, kvs[b][LANES:], 0.0)
        return carry

    lax.fori_loop(0, n_chunk, contributions, 0, unroll=2)

    def recur(i, rbs):
        n = n_chunk - 1 - i
        new = []
        for b in range(bsz):
            own = rb_ref[b, n]
            rb_ref[b, n] = rbs[b]
            new.append(rbs[b] * tb["cdec_b"] + own)
        return tuple(new)

    lax.fori_loop(0, n_chunk, recur, tuple(jnp.zeros((LANES, LANES), F32) for _ in range(bsz)))


def _split_rows(x, m0):
    return jnp.concatenate([jnp.where(m0, x, 0.0), jnp.where(m0, 0.0, x)], axis=0).astype(BF16)


def _ret_fwd(u3, lgf_l, lgb_l, gn_gain, rider=None):
    bsz, s, _ = u3.shape
    n_chunk = s // CHUNK
    c = CHUNK

    def body(q_ref, k_ref, v_ref, g_ref, lgf_ref, lgb_ref, gain_ref, yh_ref, rstd_ref, o_ref, rb_ref, kvf_ref):
        tb = _ret_tables(lgf_ref[...], lgb_ref[...])
        m0 = tb["m0"]
        gain = gain_ref[...]
        rows = range(bsz)
        _ret_kv_states(tb, k_ref, v_ref, rb_ref, kvf_ref, n_chunk)

        def chunk(n, rfs):
            sl = pl.ds(pl.multiple_of(n * c, c), c)
            qs = [q_ref[b, sl, :].astype(F32) * 0.125 for b in rows]
            s01 = [_dot_nt(_split_rows(qs[b], m0), k_ref[b, sl, :]) for b in rows]
            ys = []
            for b in rows:
                lhs = jnp.concatenate([s01[b][0:c] * tb["d0"], s01[b][c:] * tb["d1"],
                                       qs[b] * tb["qdec_f"], qs[b] * tb["qdec_b"]], axis=1).astype(BF16)
                rhs = jnp.concatenate([_split_rows(v_ref[b, sl, :].astype(F32), m0),
                                       rfs[b].astype(BF16), rb_ref[b, n].astype(BF16)], axis=0)
                ys.append(_dot(lhs, rhs))
            new = []
            for b in rows:
                y = ys[b]
                mu = _head_mean(y, m0)
                yc = y - mu
                rstd = lax.rsqrt(_head_mean(yc * yc, m0) + GN_EPS)
                yh = yc * rstd
                g = g_ref[b, sl, :].astype(F32)
                yh_ref[b, sl, :] = yh
                rstd_ref[b, sl, :] = rstd
                o_ref[b, sl, :] = (yh * gain * (g * _sigmoid(g))).astype(BF16)
                new.append(rfs[b] * tb["cdec_f"] + kvf_ref[b, n])
            return tuple(new)

        lax.fori_loop(0, n_chunk, chunk, tuple(jnp.zeros((LANES, LANES), F32) for _ in rows))

    blk, lane, gain, pair = _ret_specs(bsz, s)
    state, state_shape = _ret_state_spec(bsz, n_chunk)
    return _hosted_call(
        body, "ret_fwd", (4,),
        in_specs=[blk(CB_RQ), blk(CB_RK), blk(CB_RV), blk(CB_RG), lane, lane, gain],
        out_specs=[pair, pair, pair, state, state],
        out_shape=[jax.ShapeDtypeStruct((bsz, s, RET_WIDTH), F32), jax.ShapeDtypeStruct((bsz, s, RET_WIDTH), F32),
                   jax.ShapeDtypeStruct((bsz, s, RET_WIDTH), BF16), state_shape, state_shape],
        scratch_shapes=[],
        operands=(u3, u3, u3, u3, lgf_l, lgb_l, gn_gain), rider=rider)


def _ret_bwd(u3, y_hat, y_rstd, states, d_o, lgf_l, lgb_l, gn_gain, rider=None):
    bsz, s, _ = u3.shape
    n_chunk = s // CHUNK
    c = CHUNK

    def body(q_ref, k_ref, v_ref, g_ref, yh_ref, rstd_ref, do_ref, lgf_ref, lgb_ref, gain_ref, rb_ref, kvf_ref,
             dq_ref, dk_ref, dv_ref, dg_ref, part_ref,
             rf_ref, dirf_ref, dy_ref, dk_acc, dv_acc, pa0, pa1, vec_ref):
        tb = _ret_tables(lgf_ref[...], lgb_ref[...])
        m0, bd, row = tb["m0"], tb["bd"], tb["row"]
        gain = gain_ref[...]
        wf = jnp.maximum(tb["diff"], 0.0)
        wb = jnp.maximum(-tb["diff"], 0.0)
        rows = range(bsz)
        zero_states = tuple(jnp.zeros((LANES, LANES), F32) for _ in rows)
        for ref in (pa0, pa1):
            ref[...] = jnp.zeros_like(ref)
        vec_ref[...] = jnp.zeros_like(vec_ref)

        def sweep_fwd(n, carry):
            rfs, gbs = carry
            sl = pl.ds(pl.multiple_of(n * c, c), c)
            qs, ks, vs, dys, dybs, q01, k01, dy01 = [], [], [], [], [], [], [], []
            dgain = jnp.zeros((1, LANES), F32)
            for b in rows:
                q = q_ref[b, sl, :].astype(F32) * 0.125
                k = k_ref[b, sl, :]
                yh = yh_ref[b, sl, :]
                rstd = rstd_ref[b, sl, :]
                do = do_ref[b, sl, :].astype(F32)
                g = g_ref[b, sl, :].astype(F32)
                sg = _sigmoid(g)
                sil = g * sg
                dyh = do * gain * sil
                dg_ref[b, sl, :] = (do * yh * gain * sg * (1.0 + g * (1.0 - sg))).astype(BF16)
                dgain = dgain + jnp.sum(do * yh * sil, axis=0, keepdims=True)
                dy = rstd * (dyh - _head_mean(dyh, m0) - yh * _head_mean(dyh * yh, m0))
                dyb = dy.astype(BF16)
                dy_ref[b, sl, :] = dyb
                rf_ref[b, n] = rfs[b]
                qs.append(q)
                ks.append(k)
                vs.append(v_ref[b, sl, :])
                dys.append(dy)
                dybs.append(dyb)
                q01.append(_split_rows(q, m0))
                k01.append(_split_rows(k.astype(F32), m0))
                dy01.append(_split_rows(dy, m0))
            s01 = [_dot_nt(q01[b], ks[b]) for b in rows]
            da01 = [_dot_nt(dy01[b], vs[b]) for b in rows]
            rbn = [rb_ref[b, n] for b in rows]
            states = [jnp.concatenate([rfs[b], rbn[b]], axis=0).astype(BF16) for b in rows]
            dqc = [_dot_nt(dybs[b], states[b]) for b in rows]
            gbb = [gbs[b].astype(BF16) for b in rows]
            dkb = [_dot_nt(vs[b], gbb[b]) for b in rows]
            qfb = [jnp.concatenate([qs[b] * tb["qdec_f"], qs[b] * tb["qdec_b"]], axis=1) for b in rows]
            direct = [_dot_tn(qfb[b].astype(BF16), dybs[b]) for b in rows]
            ds_cat, ds_rows, a_rows = [], [], []
            for b in rows:
                a0 = s01[b][0:c] * tb["d0"]
                a1 = s01[b][c:] * tb["d1"]
                pa0[...] += da01[b][0:c] * a0
                pa1[...] += da01[b][c:] * a1
                ds0 = da01[b][0:c] * tb["d0"]
                ds1 = da01[b][c:] * tb["d1"]
                ds_cat.append(jnp.concatenate([ds0, ds1], axis=1).astype(BF16))
                ds_rows.append(jnp.concatenate([ds0, ds1], axis=0).astype(BF16))
                a_rows.append(jnp.concatenate([a0, a1], axis=0).astype(BF16))
            kbd = [ks[b].astype(F32) * tb["kdec_b"] for b in rows]
            dq_in = [_dot(ds_cat[b], k01[b]) for b in rows]
            dk_in = [_dot_tn(ds_rows[b], q01[b]) for b in rows]
            dv_in = [_dot_tn(a_rows[b], dy01[b]) for b in rows]
            dv_gb = [_dot(kbd[b].astype(BF16), gbb[b]) for b in rows]
            new_rf, new_gb = [], []
            dlf = jnp.zeros((1, LANES), F32)
            dlb = jnp.zeros((1, LANES), F32)
            for b in rows:
                dqf, dqb = dqc[b][:, 0:LANES], dqc[b][:, LANES:]
                qf, qb = qfb[b][:, 0:LANES], qfb[b][:, LANES:]
                dq = dq_in[b] + dqf * tb["qdec_f"] + dqb * tb["qdec_b"]
                dq_ref[b, sl, :] = (dq * 0.125).astype(BF16)
                dk_acc[b, sl, :] = dk_in[b] + dkb[b] * tb["kdec_b"]
                dv_acc[b, sl, :] = dv_in[b] + dv_gb[b]
                dlf = dlf + jnp.sum((row + 1.0) * qf * dqf, axis=0, keepdims=True)
                dlb = dlb + jnp.sum((c - row) * qb * dqb + row * kbd[b] * dkb[b], axis=0, keepdims=True)
                dlb = dlb + c * tb["cdec_b"] * jnp.sum(gbs[b] * rbn[b], axis=0, keepdims=True)
                dirf_ref[b, n] = jnp.where(bd, direct[b][0:LANES], 0.0)
                new_gb.append(jnp.where(bd, direct[b][LANES:], 0.0) + tb["cdec_b"] * gbs[b])
                new_rf.append(rfs[b] * tb["cdec_f"] + kvf_ref[b, n])
            vec_ref[0:1, :] += dlf
            vec_ref[1:2, :] += dlb
            vec_ref[6:7, :] += dgain
            return tuple(new_rf), tuple(new_gb)

        lax.fori_loop(0, n_chunk, sweep_fwd, (zero_states, zero_states))

        def sweep_bwd(i, gfs):
            n = n_chunk - 1 - i
            sl = pl.ds(pl.multiple_of(n * c, c), c)
            gfb = [gfs[b].astype(BF16) for b in rows]
            kfd = [k_ref[b, sl, :].astype(F32) * tb["kdec_f"] for b in rows]
            dkf = [_dot_nt(v_ref[b, sl, :], gfb[b]) for b in rows]
            dvf = [_dot(kfd[b].astype(BF16), gfb[b]) for b in rows]
            new = []
            dlf = jnp.zeros((1, LANES), F32)
            for b in rows:
                dk_ref[b, sl, :] = (dk_acc[b, sl, :] + dkf[b] * tb["kdec_f"]).astype(BF16)
                dv_ref[b, sl, :] = (dv_acc[b, sl, :] + dvf[b]).astype(BF16)
                dlf = dlf + jnp.sum((c - 1.0 - row) * kfd[b] * dkf[b], axis=0, keepdims=True)
                dlf = dlf + c * tb["cdec_f"] * jnp.sum(gfs[b] * rf_ref[b, n], axis=0, keepdims=True)
                new.append(dirf_ref[b, n] + tb["cdec_f"] * gfs[b])
            vec_ref[0:1, :] += dlf
            return tuple(new)

        lax.fori_loop(0, n_chunk, sweep_bwd, zero_states, unroll=2)
        vec_ref[2:3, :] = jnp.sum(pa0[...] * wf, axis=0, keepdims=True)
        vec_ref[3:4, :] = jnp.sum(pa1[...] * wf, axis=0, keepdims=True)
        vec_ref[4:5, :] = jnp.sum(pa0[...] * wb, axis=0, keepdims=True)
        vec_ref[5:6, :] = jnp.sum(pa1[...] * wb, axis=0, keepdims=True)
        part_ref[...] = vec_ref[...]

    blk, lane, gain, pair = _ret_specs(bsz, s)
    out_bf = jax.ShapeDtypeStruct((bsz, s, RET_WIDTH), BF16)
    state = pltpu.VMEM((bsz, n_chunk, LANES, LANES), F32)
    saved = _ret_state_spec(bsz, n_chunk)[0]
    return _hosted_call(
        body, "ret_bwd", (4,),
        in_specs=[blk(CB_RQ), blk(CB_RK), blk(CB_RV), blk(CB_RG), pair, pair, pair, lane, lane, gain, saved, saved],
        out_specs=[pair, pair, pair, pair, pl.BlockSpec((None, 8, LANES), lambda p: (p, 0, 0))],
        out_shape=[out_bf, out_bf, out_bf, out_bf, jax.ShapeDtypeStruct((4, 8, LANES), F32)],
        scratch_shapes=[state, state,
                        pltpu.VMEM((bsz, s, LANES), BF16), pltpu.VMEM((bsz, s, LANES), F32),
                        pltpu.VMEM((bsz, s, LANES), F32),
                        pltpu.VMEM((c, c), F32), pltpu.VMEM((c, c), F32), pltpu.VMEM((8, LANES), F32)],
        operands=(u3, u3, u3, u3, y_hat, y_rstd, d_o, lgf_l, lgb_l, gn_gain, *states), rider=rider)


def _attn_window_tables(n, s):
    qi = lax.broadcasted_iota(jnp.int32, (CHUNK, 3 * CHUNK), 0)
    kj = lax.broadcasted_iota(jnp.int32, (CHUNK, 3 * CHUNK), 1)
    dist = jnp.abs(kj - CHUNK - qi)
    kpos = n * CHUNK - CHUNK + kj
    valid = (dist <= CHUNK) & (kpos >= 0) & (kpos < s)
    return dist.astype(F32), valid


def _dup_kv_head(x, g):
    lane = lax.broadcasted_iota(jnp.int32, x.shape, 1)
    keep = (lane < HEAD_DIM) == (g == 0)
    xf = x.astype(F32)
    return jnp.where(keep, xf, pltpu.roll(xf, HEAD_DIM, 1))


def _attn_specs(s):
    q = pl.BlockSpec((None, s, 2 * LANES), lambda b, g: (b, 0, CB_AQ // 2 + g))
    k = pl.BlockSpec((None, s, LANES), lambda b, g: (b, 0, CB_AK))
    v = pl.BlockSpec((None, s, LANES), lambda b, g: (b, 0, CB_AV))
    grp = pl.BlockSpec((None, s, 2 * LANES), lambda b, g: (b, 0, g))
    smem = pl.BlockSpec(memory_space=pltpu.SMEM)
    return q, k, v, grp, smem


def _fill_padded(dst_ref, val, s):
    dst_ref[0:CHUNK, :] = jnp.zeros((CHUNK, LANES), dst_ref.dtype)
    dst_ref[CHUNK:CHUNK + s, :] = val.astype(dst_ref.dtype)
    dst_ref[CHUNK + s:2 * CHUNK + s, :] = jnp.zeros((CHUNK, LANES), dst_ref.dtype)


def _attn_probs(sc, slope, snk, dist, valid):
    sc = jnp.where(valid, sc - slope * dist, NEG_INF)
    m = jnp.maximum(jnp.max(sc, axis=1, keepdims=True), snk)
    e = jnp.exp(sc - m)
    es = jnp.exp(snk - m)
    inv = 1.0 / (jnp.sum(e, axis=1, keepdims=True) + es)
    return e * inv, es * inv


def _stack_heads(x2, m0):
    parts = []
    for pr in range(2):
        xp = x2[:, pr * LANES:(pr + 1) * LANES]
        parts += [jnp.where(m0, xp, 0.0), jnp.where(m0, 0.0, xp)]
    return jnp.concatenate(parts, axis=0).astype(BF16)


def _unstack_pair(x_all, pr, m0):
    return jnp.where(m0, x_all[(2 * pr) * CHUNK:(2 * pr + 1) * CHUNK], x_all[(2 * pr + 1) * CHUNK:(2 * pr + 2) * CHUNK])


def _attn_saved_specs(bsz, n_blk):
    specs = [pl.BlockSpec((None, None, n_blk, 4 * CHUNK, w), lambda b, g: (b, g, 0, 0, 0)) for w in (3 * CHUNK, 1)]
    shapes = [jax.ShapeDtypeStruct((bsz, 2, n_blk, 4 * CHUNK, 3 * CHUNK), BF16),
              jax.ShapeDtypeStruct((bsz, 2, n_blk, 4 * CHUNK, 1), F32)]
    return specs, shapes


def _attn_fwd(u3, slopes, sink, rider=None):
    bsz, s, _ = u3.shape
    n_blk = s // CHUNK

    def body(slope_ref, sink_ref, q_ref, k_ref, v_ref, o_ref, p_ref, ps_ref, kp_ref, vp_ref):
        g = pl.program_id(1)
        _fill_padded(kp_ref, _dup_kv_head(k_ref[...], g), s)
        _fill_padded(vp_ref, _dup_kv_head(v_ref[...], g), s)
        m0 = lax.broadcasted_iota(jnp.int32, (CHUNK, LANES), 1) < HEAD_DIM

        def blk(n, carry):
            r0 = pl.multiple_of(n * CHUNK, CHUNK)
            kw = kp_ref[pl.ds(r0, 3 * CHUNK), :]
            vw = vp_ref[pl.ds(r0, 3 * CHUNK), :]
            dist, valid = _attn_window_tables(n, s)
            q_all = _stack_heads(q_ref[pl.ds(r0, CHUNK), :].astype(F32) * 0.125, m0)
            sc_all = _dot_nt(q_all, kw)
            probs, sinks = [], []
            for i in range(4):
                p, ps = _attn_probs(sc_all[i * CHUNK:(i + 1) * CHUNK], slope_ref[g * 4 + i], sink_ref[g * 4 + i],
                                    dist, valid)
                probs.append(p.astype(BF16))
                sinks.append(ps)
            p_all = jnp.concatenate(probs, axis=0)
            p_ref[n] = p_all
            ps_ref[n] = jnp.concatenate(sinks, axis=0)
            out_all = _dot(p_all, vw)
            for pr in range(2):
                o_ref[pl.ds(r0, CHUNK), pr * LANES:(pr + 1) * LANES] = _unstack_pair(out_all, pr, m0).astype(BF16)
            return carry

        lax.fori_loop(0, n_blk, blk, 0)

    q, k, v, grp, smem = _attn_specs(s)
    saved_specs, saved_shapes = _attn_saved_specs(bsz, n_blk)
    return _hosted_call(
        body, "attn_fwd", (bsz, 2),
        in_specs=[smem, smem, q, k, v],
        out_specs=[grp] + saved_specs,
        out_shape=[jax.ShapeDtypeStruct((bsz, s, ATTN_WIDTH), BF16)] + saved_shapes,
        scratch_shapes=[pltpu.VMEM((s + 2 * CHUNK, LANES), BF16), pltpu.VMEM((s + 2 * CHUNK, LANES), BF16)],
        operands=(slopes, sink, u3, u3, u3), rider=rider)


def _attn_bwd(u3, d_o, probs, sink_probs, rider=None):
    bsz, s, _ = u3.shape
    n_blk = s // CHUNK

    def body(q_ref, k_ref, v_ref, do_ref, p_ref, ps_ref, dq_ref, dkv_ref, ds_ref,
             kp_ref, vp_ref, dk_acc, dv_acc):
        g = pl.program_id(1)
        _fill_padded(kp_ref, _dup_kv_head(k_ref[...], g), s)
        _fill_padded(vp_ref, _dup_kv_head(v_ref[...], g), s)
        dk_acc[...] = jnp.zeros_like(dk_acc)
        dv_acc[...] = jnp.zeros_like(dv_acc)
        m0 = lax.broadcasted_iota(jnp.int32, (CHUNK, LANES), 1) < HEAD_DIM

        def blk(n, dsink):
            r0 = pl.multiple_of(n * CHUNK, CHUNK)
            win = pl.ds(r0, 3 * CHUNK)
            kw = kp_ref[win, :]
            vw = vp_ref[win, :]
            q_all = _stack_heads(q_ref[pl.ds(r0, CHUNK), :].astype(F32) * 0.125, m0)
            do_all = _stack_heads(do_ref[pl.ds(r0, CHUNK), :].astype(F32), m0)
            p_all = p_ref[n]
            ps_all = ps_ref[n]
            dp_all = _dot_nt(do_all, vw)
            new_dsink, dscs = [], []
            for i in range(4):
                rows = slice(i * CHUNK, (i + 1) * CHUNK)
                p = p_all[rows].astype(F32)
                dp = dp_all[rows]
                delta = jnp.sum(p * dp, axis=1, keepdims=True)
                dscs.append((p * (dp - delta)).astype(BF16))
                dsh = jnp.sum(ps_all[rows] * delta, axis=0, keepdims=True)
                new_dsink.append(dsink[i] - jnp.broadcast_to(dsh, (1, LANES)))
            dsc_all = jnp.concatenate(dscs, axis=0)
            dq_all = _dot(dsc_all, kw)
            dk_acc[win, :] += _dot_tn(dsc_all, q_all)
            dv_acc[win, :] += _dot_tn(p_all, do_all)
            for pr in range(2):
                dq_ref[pl.ds(r0, CHUNK), pr * LANES:(pr + 1) * LANES] = (
                    _unstack_pair(dq_all, pr, m0) * 0.125).astype(BF16)
            return tuple(new_dsink)

        dsink = lax.fori_loop(0, n_blk, blk, tuple(jnp.zeros((1, LANES), F32) for _ in range(4)))
        dk = dk_acc[CHUNK:CHUNK + s, :]
        dv = dv_acc[CHUNK:CHUNK + s, :]
        lane = lax.broadcasted_iota(jnp.int32, (s, LANES), 1)
        fold = lambda a: a + pltpu.roll(a, HEAD_DIM, 1)
        dkv_ref[...] = jnp.where(lane < HEAD_DIM, fold(dk), fold(dv)).astype(BF16)
        ds_ref[...] = jnp.zeros_like(ds_ref)
        for i in range(4):
            ds_ref[i:i + 1, :] = dsink[i]

    q, k, v, grp, _ = _attn_specs(s)
    return _hosted_call(
        body, "attn_bwd", (bsz, 2),
        in_specs=[q, k, v, grp] + _attn_saved_specs(bsz, n_blk)[0],
        out_specs=[grp, pl.BlockSpec((None, s, LANES), lambda b, g: (b, 0, g)),
                   pl.BlockSpec((None, None, 8, LANES), lambda b, g: (b, g, 0, 0))],
        out_shape=[jax.ShapeDtypeStruct((bsz, s, ATTN_WIDTH), BF16), jax.ShapeDtypeStruct((bsz, s, 2 * LANES), BF16),
                   jax.ShapeDtypeStruct((bsz, 2, 8, LANES), F32)],
        scratch_shapes=[pltpu.VMEM((s + 2 * CHUNK, LANES), BF16), pltpu.VMEM((s + 2 * CHUNK, LANES), BF16),
                        pltpu.VMEM((s + 2 * CHUNK, LANES), F32), pltpu.VMEM((s + 2 * CHUNK, LANES), F32)],
        operands=(u3, u3, u3, d_o, probs, sink_probs), rider=rider)


def _ffn_bwd(dz2, gs, us, pg, ple, zh1, r1, g1, wg4, wu4, wd4, wpg, w_out):
    t = dz2.shape[0]
    tm = 256
    wg_t, wu_t, wd_all = (w.reshape(FFN, D_MODEL) for w in (wg4, wu4, wd4))

    def body(dz_ref, gs_ref, us_ref, pg_ref, ple_ref, zh_ref, r_ref, g1_ref,
             wg_hbm, wu_hbm, wd_hbm, wpg_hbm, wo_hbm,
             dgs_ref, dus_ref, dsp_ref, dple_ref, dz1_ref, dyr_ref, dya_ref, dg1_ref, db1_ref,
             wg, wu, wd, wpg, wo, wsem):
        step = pl.program_id(0)
        loads = _resident_quarters(wd_hbm, wd) + _resident_quarters(wg_hbm, wg) + _resident_quarters(wu_hbm, wu)
        _load_resident(step, loads + [(wpg_hbm, wpg), (wo_hbm, wo)], wsem)

        @pl.when(step == 0)
        def _():
            dg1_ref[...] = jnp.zeros_like(dg1_ref)
            db1_ref[...] = jnp.zeros_like(db1_ref)

        dz = dz_ref[...]
        dzb = dz.astype(BF16)
        dh = ALPHA * dz
        pending = []
        chunks = [slice(n * FFN_CHUNK, (n + 1) * FFN_CHUNK) for n in range(N_FFN_CHUNK)]
        for n in range(N_FFN_CHUNK + 1):
            if n < N_FFN_CHUNK:
                da = _dot_nt(dzb, wd[chunks[n], :])
                gj = gs_ref[:, chunks[n]].astype(F32)
                uj = us_ref[:, chunks[n]].astype(F32)
                sg = _sigmoid(gj)
                dgj = (da * uj * sg * (1.0 + gj * (1.0 - sg))).astype(BF16)
                duj = (da * gj * sg).astype(BF16)
                dgs_ref[:, chunks[n]] = dgj
                dus_ref[:, chunks[n]] = duj
                pending.append((dgj, duj))
            if n > 0:
                dgp, dup = pending[n - 1]
                dh = dh + _dot(dgp, wg[chunks[n - 1], :]) + _dot(dup, wu[chunks[n - 1], :])
        pgv = pg_ref[...].astype(F32)
        plev = ple_ref[...].astype(F32)
        dple_ref[...] = (dz * pgv).astype(BF16)
        dsp = (dz * plev * pgv * (1.0 - pgv)).astype(BF16)
        dsp_ref[...] = dsp
        dh = dh + _dot_nt(dsp, wpg[...])
        zh = zh_ref[...]
        dg1_ref[...] += jnp.sum(dh * zh, axis=0, keepdims=True)
        db1_ref[...] += jnp.sum(dh, axis=0, keepdims=True)
        dzh = dh * g1_ref[...]
        m1 = jnp.mean(dzh, axis=1, keepdims=True)
        m2 = jnp.mean(dzh * zh, axis=1, keepdims=True)
        dz1 = r_ref[...] * (dzh - m1 - zh * m2)
        dz1_ref[...] = dz1
        dyc = _dot_nt(dz1.astype(BF16), wo[...])
        dyr_ref[...] = dyc[:, 0:RET_WIDTH].astype(BF16)
        dya_ref[...] = dyc[:, RET_WIDTH:].astype(BF16)

    row = lambda w: pl.BlockSpec((tm, w), lambda i: (i, 0))
    const = lambda s: pl.BlockSpec(s, lambda i: (0, 0))
    hbm = pl.BlockSpec(memory_space=pl.ANY)
    hid_shape = jax.ShapeDtypeStruct((t, FFN), BF16)
    return pl.pallas_call(
        body, name="ffn_bwd", grid=(t // tm,),
        in_specs=[row(D_MODEL), row(FFN), row(FFN), row(D_MODEL), row(D_MODEL), row(D_MODEL), row(1),
                  const((1, D_MODEL)), hbm, hbm, hbm, hbm, hbm],
        out_specs=[row(FFN), row(FFN), row(D_MODEL), row(D_MODEL), row(D_MODEL), row(RET_WIDTH), row(ATTN_WIDTH),
                   const((1, D_MODEL)), const((1, D_MODEL))],
        out_shape=[hid_shape, hid_shape, jax.ShapeDtypeStruct((t, D_MODEL), BF16),
                   jax.ShapeDtypeStruct((t, D_MODEL), BF16), jax.ShapeDtypeStruct((t, D_MODEL), F32),
                   jax.ShapeDtypeStruct((t, RET_WIDTH), BF16), jax.ShapeDtypeStruct((t, ATTN_WIDTH), BF16),
                   jax.ShapeDtypeStruct((1, D_MODEL), F32), jax.ShapeDtypeStruct((1, D_MODEL), F32)],
        scratch_shapes=[pltpu.VMEM((FFN, D_MODEL), BF16), pltpu.VMEM((FFN, D_MODEL), BF16),
                        pltpu.VMEM((FFN, D_MODEL), BF16),
                        pltpu.VMEM(wpg.shape, BF16), pltpu.VMEM(w_out.shape, BF16),
                        pltpu.SemaphoreType.DMA((3 * N_SHARD + 2,))],
        compiler_params=_params("arbitrary", vmem=VMEM_LIMIT),
    )(dz2, gs, us, pg, ple, zh1, r1, g1, wg_t, wu_t, wd_all, wpg, w_out)


def _wgrad_misc(y_ret, y_att, dz1, hb, dsp, p2d, dple, rider=None):
    t = dz1.shape[0]
    tk = min(t, 512)

    def body(yr_ref, ya_ref, dz_ref, hb_ref, dsp_ref, p_ref, dple_ref, wo_ref, wpg_ref, wpe_ref):
        @pl.when(pl.program_id(0) == 0)
        def _():
            wo_ref[...] = jnp.zeros_like(wo_ref)
            wpg_ref[...] = jnp.zeros_like(wpg_ref)
            wpe_ref[...] = jnp.zeros_like(wpe_ref)

        dzb = dz_ref[...].astype(BF16)
        wo_ref[0:RET_WIDTH, :] += _dot_tn(yr_ref[...], dzb)
        wo_ref[RET_WIDTH:, :] += _dot_tn(ya_ref[...], dzb)
        wpg_ref[...] += _dot_tn(hb_ref[...], dsp_ref[...])
        wpe_ref[...] += _dot_tn(p_ref[...].astype(BF16), dple_ref[...])

    row = lambda w: pl.BlockSpec((tk, w), lambda k: (k, 0))
    const = lambda s: pl.BlockSpec(s, lambda k: (0, 0))
    return _hosted_call(
        body, "wgrad_misc", (t // tk,),
        in_specs=[row(RET_WIDTH), row(ATTN_WIDTH), row(D_MODEL), row(D_MODEL), row(D_MODEL), row(PLE_DIM),
                  row(D_MODEL)],
        out_specs=[const((D_MODEL, D_MODEL)), const((D_MODEL, D_MODEL)), const((PLE_DIM, D_MODEL))],
        out_shape=[jax.ShapeDtypeStruct((D_MODEL, D_MODEL), F32), jax.ShapeDtypeStruct((D_MODEL, D_MODEL), F32),
                   jax.ShapeDtypeStruct((PLE_DIM, D_MODEL), F32)],
        scratch_shapes=[], operands=(y_ret, y_att, dz1, hb, dsp, p2d, dple), rider=rider, semantics=["arbitrary"])


def _wgrad_ffn(acts, dgs, dus, hb, dz2b):
    t = dz2b.shape[0]
    tk = min(t, 512)
    nk = t // tk

    def body(act_ref, dg_ref, du_ref, hb_ref, dz_ref, og_ref, ou_ref, od_ref):
        @pl.when(pl.program_id(1) == 0)
        def _():
            og_ref[...] = jnp.zeros_like(og_ref)
            ou_ref[...] = jnp.zeros_like(ou_ref)
            od_ref[...] = jnp.zeros_like(od_ref)

        hbv = hb_ref[...]
        og_ref[...] += _dot_tn(dg_ref[...], hbv)
        ou_ref[...] += _dot_tn(du_ref[...], hbv)
        od_ref[...] += _dot_tn(act_ref[...], dz_ref[...])

    half = FFN // 2
    a_spec = pl.BlockSpec((tk, half), lambda j, k: (k, j))
    b_spec = pl.BlockSpec((tk, D_MODEL), lambda j, k: (k, 0))
    o_spec = pl.BlockSpec((half, D_MODEL), lambda j, k: (j, 0))
    o_shape = jax.ShapeDtypeStruct((FFN, D_MODEL), F32)
    outs = pl.pallas_call(
        body, name="wgrad_ffn", grid=(2, nk),
        in_specs=[a_spec, a_spec, a_spec, b_spec, b_spec],
        out_specs=[o_spec] * 3, out_shape=[o_shape] * 3,
        compiler_params=_params("parallel", "arbitrary", vmem=VMEM_LIMIT),
    )(acts, dgs, dus, hb, dz2b)
    return [o.reshape(N_SHARD, FFN_SHARD, D_MODEL) for o in outs]


KV_ORDER = (0, 128, 64, 192)


def _wgrad_in(pieces, x2d):
    t = x2d.shape[0]
    tk = min(t, 512)
    nk = t // tk
    kv0 = CB_AK * LANES

    def body(p0, p1, p2, p3, p4, pkv, x_ref, o_ref):
        @pl.when(pl.program_id(0) == 0)
        def _():
            o_ref[...] = jnp.zeros_like(o_ref)

        xb = x_ref[...].astype(BF16)
        for i, ref in enumerate((p0, p1, p2, p3, p4)):
            o_ref[i * 512:(i + 1) * 512, :] += _dot_tn(ref[...], xb)
        dkv = _dot_tn(pkv[...], xb)
        for i, o in enumerate(KV_ORDER):
            o_ref[kv0 + o:kv0 + o + HEAD_DIM, :] += dkv[i * HEAD_DIM:(i + 1) * HEAD_DIM]

    row = lambda w: pl.BlockSpec((tk, w), lambda k: (k, 0))
    return pl.pallas_call(
        body, name="wgrad_in", grid=(nk,),
        in_specs=[row(512)] * 5 + [row(256), row(D_MODEL)],
        out_specs=pl.BlockSpec((IN_WIDTH, D_MODEL), lambda k: (0, 0)),
        out_shape=jax.ShapeDtypeStruct((IN_WIDTH, D_MODEL), F32),
        compiler_params=_params("arbitrary", vmem=VMEM_LIMIT),
    )(*pieces, x2d)


def _inproj_bwd(dz1, pieces, w_main, w_kv, rider=None):
    t = dz1.shape[0]
    tm = 512

    def body(dz_ref, p0, p1, p2, p3, p4, pkv, wm_ref, wkv_ref, o_ref):
        acc = ALPHA * dz_ref[...]
        for i, ref in enumerate((p0, p1, p2, p3, p4)):
            acc = acc + _dot(ref[...], wm_ref[i * 512:(i + 1) * 512, :])
        o_ref[...] = acc + _dot(pkv[...], wkv_ref[...])

    row = lambda w: pl.BlockSpec((tm, w), lambda i: (i, 0))
    const = lambda s: pl.BlockSpec(s, lambda i: (0, 0))
    return _hosted_call(
        body, "inproj_bwd", (t // tm,),
        in_specs=[row(D_MODEL)] + [row(512)] * 5 + [row(256), const(w_main.shape), const(w_kv.shape)],
        out_specs=[row(D_MODEL)],
        out_shape=[jax.ShapeDtypeStruct((t, D_MODEL), F32)],
        scratch_shapes=[], operands=(dz1, *pieces, w_main, w_kv), rider=rider)


def _coords():
    return lax.axis_index("x"), lax.axis_index("y"), lax.axis_index("c")


def _chip_of(x, y, rel):
    return (1 - x if rel & 2 else x), (1 - y if rel & 1 else y)


def _all_gather_weights(shards):
    first = _gather_chips_rider(shards)
    second = _gather_pass_rider([jax.ShapeDtypeStruct((N_SHARD,) + s.shape, s.dtype) for s in shards], chained=True)
    return _run_riders("gather_weights", shards, first.out_shapes, [first, second])


def _run_riders(name, ins, out_shapes, riders):
    n_in, n_out = len(ins), len(out_shapes)

    def body(*refs):
        in_refs, out_refs = refs[:n_in], refs[n_in:n_in + n_out]
        k = n_in + n_out
        for r in riders:
            sems = refs[k:k + len(r.sems)]
            k += len(r.sems)
            r.start(in_refs, out_refs, sems)
            r.finish(in_refs, out_refs, sems)

    hbm = pl.BlockSpec(memory_space=pl.ANY)
    return pl.pallas_call(
        body, name=name, in_specs=[hbm] * n_in, out_specs=[hbm] * n_out, out_shape=list(out_shapes),
        scratch_shapes=[s for r in riders for s in r.sems],
    )(*ins)


def _gather_half(outs, w, chip, cc):
    h = outs[w].shape[1] // 2
    return outs[w].at[chip, pl.ds(cc * h, h), :]


def _gather_chips_rider(shards):
    nw = len(shards)

    def copies(ins, outs, sems, arrivals):
        send, recv, lsend, lrecv = sems
        x, y, c = _coords()
        me = 2 * x + y
        own = [pltpu.make_async_remote_copy(
            src_ref=ins[w], dst_ref=outs[w].at[me], send_sem=lsend.at[w], recv_sem=lrecv.at[w],
            device_id=(x, y, 1 - c), device_id_type=MESH) for w in range(nw)]
        out, arrive = [], []
        for rel in (1, 2, 3):
            kx, ky = _chip_of(x, y, rel)
            for w in range(nw):
                h = shards[w].shape[0] // 2
                sem = dict(send_sem=send.at[w * 3 + rel - 1], recv_sem=recv.at[w * 3 + rel - 1],
                           device_id=(kx, ky, c), device_id_type=MESH)
                out.append(pltpu.make_async_remote_copy(
                    src_ref=ins[w].at[pl.ds(c * h, h), :], dst_ref=_gather_half(outs, w, me, c), **sem))
                if arrivals:
                    theirs = _gather_half(outs, w, 2 * kx + ky, c)
                    arrive.append(pltpu.make_async_remote_copy(src_ref=theirs, dst_ref=theirs, **sem))
        return own, out, arrive

    def start(ins, outs, sems):
        own, out, _ = copies(ins, outs, sems, arrivals=False)
        for cp in own + out:
            cp.start()

    def finish(ins, outs, sems):
        own, out, arrive = copies(ins, outs, sems, arrivals=True)
        for cp in arrive:
            cp.wait_recv()
        for cp in out:
            cp.wait_send()
        for cp in own:
            cp.wait()

    dma = pltpu.SemaphoreType.DMA
    return _Rider(shards, [jax.ShapeDtypeStruct((N_SHARD,) + s.shape, s.dtype) for s in shards],
                  [dma((3 * nw,)), dma((3 * nw,)), dma((nw,)), dma((nw,))], start, finish)


def _gather_pass_rider(gathered, chained=False):
    nw = len(gathered)

    def copies(outs, sems, cc):
        send, recv = sems
        x, y, c = _coords()
        res = []
        for rel in (1, 2, 3):
            kx, ky = _chip_of(x, y, rel)
            for w in range(nw):
                rows = _gather_half(outs, w, 2 * kx + ky, cc)
                res.append(pltpu.make_async_remote_copy(
                    src_ref=rows, dst_ref=rows, send_sem=send.at[w * 3 + rel - 1], recv_sem=recv.at[w * 3 + rel - 1],
                    device_id=(x, y, 1 - c), device_id_type=MESH))
        return res

    def start(ins, outs, sems):
        for cp in copies(outs, sems, lax.axis_index("c")):
            cp.start()

    def finish(ins, outs, sems):
        c = lax.axis_index("c")
        for cp in copies(outs, sems, 1 - c):
            cp.wait_recv()
        for cp in copies(outs, sems, c):
            cp.wait_send()

    dma = pltpu.SemaphoreType.DMA
    shapes = [jax.ShapeDtypeStruct(g.shape, g.dtype) for g in gathered]
    if chained:
        return _Rider([], [], [dma((3 * nw,)), dma((3 * nw,))], start, finish)
    return _Rider(gathered, shapes, [dma((3 * nw,)), dma((3 * nw,))], start, finish,
                  aliases={w: w for w in range(nw)})


def _exchange_halves_rider(parts):
    nw = len(parts)

    def copies(ins, outs, sems):
        send, recv = sems
        x, y, c = _coords()
        res = []
        for w in range(nw):
            h = parts[w].shape[1] // 2
            res.append(pltpu.make_async_remote_copy(
                src_ref=ins[w].at[:, pl.ds((1 - c) * h, h), :], dst_ref=outs[w],
                send_sem=send.at[w], recv_sem=recv.at[w], device_id=(x, y, 1 - c), device_id_type=MESH))
        return res

    def start(ins, outs, sems):
        for cp in copies(ins, outs, sems):
            cp.start()

    def finish(ins, outs, sems):
        for cp in copies(ins, outs, sems):
            cp.wait()

    dma = pltpu.SemaphoreType.DMA
    return _Rider(parts, [jax.ShapeDtypeStruct((N_SHARD, p.shape[1] // 2, p.shape[2]), p.dtype) for p in parts],
                  [dma((nw,)), dma((nw,))], start, finish)


def _add_halves(parts, theirs, pos):
    nw = len(parts)
    split = 2

    def body(pos_ref, *refs):
        ins, oth = refs[:nw], refs[nw:2 * nw]
        o32, o16 = refs[2 * nw:3 * nw], refs[3 * nw:]
        sums = [ins[w][...] + oth[w][...].astype(F32) for w in range(nw)]
        for w in range(nw):
            o16[w][...] = sums[w].astype(BF16)

        @pl.when(pl.program_id(1) == pos_ref[0])
        def _():
            for w in range(nw):
                o32[w][...] = sums[w]

    in_specs, oth_specs, o32_specs, shapes32, shapes16 = [], [], [], [], []
    for p in parts:
        hb = p.shape[1] // 2 // split
        blk = (None, hb, p.shape[2])
        in_specs.append(pl.BlockSpec(blk, lambda i, j, pos_ref: (j, pos_ref[1] * split + i, 0)))
        oth_specs.append(pl.BlockSpec(blk, lambda i, j, pos_ref: (j, i, 0)))
        o32_specs.append(pl.BlockSpec((hb, p.shape[2]), lambda i, j, pos_ref: (i, 0)))
        shapes32.append(jax.ShapeDtypeStruct((p.shape[1] // 2, p.shape[2]), F32))
        shapes16.append(jax.ShapeDtypeStruct((N_SHARD, p.shape[1] // 2, p.shape[2]), BF16))
    return pl.pallas_call(
        body, name="add_halves",
        grid_spec=pltpu.PrefetchScalarGridSpec(
            num_scalar_prefetch=1, grid=(split, N_SHARD),
            in_specs=in_specs + oth_specs, out_specs=o32_specs + oth_specs),
        out_shape=shapes32 + shapes16,
        compiler_params=_params("parallel", "arbitrary", vmem=VMEM_LIMIT),
    )(pos, *parts, *theirs)


def _exchange_chips_rider(sums16):
    nw = len(sums16)

    def copies(ins, outs, sems):
        send, recv = sems
        x, y, c = _coords()
        res = []
        for rel in (1, 2, 3):
            kx, ky = _chip_of(x, y, rel)
            for w in range(nw):
                res.append(pltpu.make_async_remote_copy(
                    src_ref=ins[w].at[2 * kx + ky], dst_ref=outs[w].at[rel - 1],
                    send_sem=send.at[w * 3 + rel - 1], recv_sem=recv.at[w * 3 + rel - 1],
                    device_id=(kx, ky, c), device_id_type=MESH))
        return res

    def start(ins, outs, sems):
        for cp in copies(ins, outs, sems):
            cp.start()

    def finish(ins, outs, sems):
        for cp in copies(ins, outs, sems):
            cp.wait()

    dma = pltpu.SemaphoreType.DMA
    return _Rider(sums16, [jax.ShapeDtypeStruct((3,) + s.shape[1:], BF16) for s in sums16],
                  [dma((3 * nw,)), dma((3 * nw,))], start, finish)


def _add_chips(sums32, theirs, pos):
    nw = len(sums32)
    split = 2

    def body(pos_ref, *refs):
        ins, oth, outs = refs[:nw], refs[nw:2 * nw], refs[2 * nw:]
        for w in range(nw):
            acc = ins[w][...]
            for r in range(3):
                acc = acc + oth[w][r].astype(F32)
            outs[w][...] = acc

    in_specs, oth_specs, out_specs, shapes = [], [], [], []
    for s in sums32:
        hb = s.shape[0] // split
        in_specs.append(pl.BlockSpec((hb, s.shape[1]), lambda i, pos_ref: (i, 0)))
        oth_specs.append(pl.BlockSpec((3, hb, s.shape[1]), lambda i, pos_ref: (0, i, 0)))
        out_specs.append(pl.BlockSpec((hb, s.shape[1]), lambda i, pos_ref: (pos_ref[1] * split + i, 0)))
        shapes.append(jax.ShapeDtypeStruct((2 * s.shape[0], s.shape[1]), F32))
    return pl.pallas_call(
        body, name="add_chips",
        grid_spec=pltpu.PrefetchScalarGridSpec(
            num_scalar_prefetch=1, grid=(split,), in_specs=in_specs + oth_specs, out_specs=out_specs),
        out_shape=shapes,
        compiler_params=_params("parallel", vmem=VMEM_LIMIT),
    )(pos, *sums32, *theirs)


def _join_halves(shards):
    nw = len(shards)

    def body(*refs):
        outs = refs[nw:2 * nw]
        send, recv = refs[2 * nw:]
        x, y, c = _coords()

        def copy(w, cc):
            h = shards[w].shape[0] // 2
            rows = outs[w].at[pl.ds(cc * h, h), :]
            return pltpu.make_async_remote_copy(
                src_ref=rows, dst_ref=rows, send_sem=send.at[w], recv_sem=recv.at[w],
                device_id=(x, y, 1 - c), device_id_type=MESH)

        for w in range(nw):
            copy(w, c).start()
        for w in range(nw):
            copy(w, 1 - c).wait_recv()
            copy(w, c).wait_send()

    hbm = pl.BlockSpec(memory_space=pl.ANY)
    return pl.pallas_call(
        body, name="join_halves",
        in_specs=[hbm] * nw, out_specs=[hbm] * nw,
        out_shape=[jax.ShapeDtypeStruct(s.shape, F32) for s in shards],
        input_output_aliases={w: w for w in range(nw)},
        scratch_shapes=[pltpu.SemaphoreType.DMA((nw,)), pltpu.SemaphoreType.DMA((nw,))],
    )(*shards)


def _adamw_math(w, g, m, v):
    m = ADAM_B1 * m + (1.0 - ADAM_B1) * g
    v = ADAM_B2 * v + (1.0 - ADAM_B2) * (g * g)
    m_hat = m / (1.0 - ADAM_B1 ** ADAM_STEP)
    v_hat = v / (1.0 - ADAM_B2 ** ADAM_STEP)
    delta = -ADAM_LR * (m_hat / (jnp.sqrt(v_hat) + ADAM_EPS) + ADAM_WD * w)
    return delta, m, v


def _adamw(ws, gs, ms, vs):
    nw = len(ws)
    split = 8

    def body(*refs):
        w_r, g_r, m_r, v_r = (refs[i * nw:(i + 1) * nw] for i in range(4))
        g_o, d_o, m_o, v_o = (refs[(4 + i) * nw:(5 + i) * nw] for i in range(4))
        for k in range(nw):
            g = g_r[k][...]
            d, m, v = _adamw_math(w_r[k][...], g, m_r[k][...], v_r[k][...])
            g_o[k][...] = g
            d_o[k][...] = d
            m_o[k][...] = m
            v_o[k][...] = v

    specs = [pl.BlockSpec((w.shape[0] // split, w.shape[1]), lambda i: (i, 0)) for w in ws]
    shapes = [jax.ShapeDtypeStruct(w.shape, F32) for w in ws]
    outs = pl.pallas_call(
        body, name="adamw", grid=(split,),
        in_specs=specs * 4, out_specs=specs * 4, out_shape=shapes * 4,
        compiler_params=_params("parallel", vmem=VMEM_LIMIT),
    )(*ws, *gs, *ms, *vs)
    return outs[:nw], outs[nw:2 * nw], outs[2 * nw:3 * nw], outs[3 * nw:]


SMALL_ROWS = 8
SMALL_COLS = D_MODEL
LOSS_COL = RET_WIDTH + 24


def _small_allreduce_adamw(part, w, m, v, rider=None):
    def body(part_ref, w_ref, m_ref, v_ref, g_out, d_out, m_out, v_out, all_ref, send, recv):
        x, y, c = _coords()
        me = 4 * x + 2 * y + c
        all_ref[me] = part_ref[...]
        copies = []
        for rel in range(1, 8):
            px = 1 - x if rel & 4 else x
            py = 1 - y if rel & 2 else y
            pc = 1 - c if rel & 1 else c
            copies.append(pltpu.make_async_remote_copy(
                src_ref=part_ref, dst_ref=all_ref.at[me],
                send_sem=send.at[rel - 1], recv_sem=recv.at[rel - 1], device_id=(px, py, pc), device_id_type=MESH))
        for cp in copies:
            cp.start()
        for cp in copies:
            cp.wait()
        g = all_ref[0]
        for k in range(1, 8):
            g = g + all_ref[k]
        d, mn, vn = _adamw_math(w_ref[...], g, m_ref[...], v_ref[...])
        g_out[...] = g
        d_out[...] = d
        m_out[...] = mn
        v_out[...] = vn

    vm = pl.BlockSpec(memory_space=pltpu.VMEM)
    shape = jax.ShapeDtypeStruct((SMALL_ROWS, SMALL_COLS), F32)
    return _hosted_call(
        body, "small_allreduce_adamw", (1,),
        in_specs=[vm] * 4, out_specs=[vm] * 4, out_shape=[shape] * 4,
        scratch_shapes=[pltpu.VMEM((8, SMALL_ROWS, SMALL_COLS), F32),
                        pltpu.SemaphoreType.DMA((7,)), pltpu.SemaphoreType.DMA((7,))],
        operands=(part, w, m, v), rider=rider, semantics=["arbitrary"])


SMALL_NAMES = ("ret_decay_fwd", "ret_decay_bwd", "attn_sink", "ret_gn_gain",
               "ln1_gain", "ln1_bias", "ln2_gain", "ln2_bias")


LN_NAMES = ("ln1_gain", "ln1_bias", "ln2_gain", "ln2_bias")


def _pack_small(vals, extra=None):
    tail = jnp.zeros((1, 1), F32) if extra is None else extra.reshape(1, 1)
    row4 = jnp.concatenate([vals["ret_gn_gain"], vals["ret_decay_fwd"], vals["ret_decay_bwd"], vals["attn_sink"],
                            tail, jnp.zeros((1, SMALL_COLS - LOSS_COL - 1), F32)], axis=1)
    rows = [vals[n] for n in LN_NAMES] + [row4, jnp.zeros((SMALL_ROWS - 5, SMALL_COLS), F32)]
    return jnp.concatenate(rows, axis=0)


def _unpack_small(packed):
    out = {n: packed[i:i + 1] for i, n in enumerate(LN_NAMES)}
    o = RET_WIDTH
    out.update(ret_gn_gain=packed[4:5, 0:o], ret_decay_fwd=packed[4:5, o:o + 8],
               ret_decay_bwd=packed[4:5, o + 8:o + 16], attn_sink=packed[4:5, o + 16:o + 24])
    return out


def _local_step(x, p, tgt, w_in_t, rest, small, pos=None, small_state=None):
    bsz, s, _ = x.shape
    t = bsz * s
    x2d = x.reshape(t, D_MODEL)
    p2d = p.reshape(t, PLE_DIM)
    tgt2d = tgt.reshape(t, D_MODEL)
    dec_f = small["ret_decay_fwd"].reshape(8)
    dec_b = small["ret_decay_bwd"].reshape(8)
    lg_f = jnp.log1p(-jnp.exp2(dec_f))
    lg_b = jnp.log1p(-jnp.exp2(dec_b))
    per_lane = lambda v: jnp.repeat(v, HEAD_DIM).reshape(4, 1, LANES)
    lgf_l, lgb_l = per_lane(lg_f), per_lane(lg_b)
    sink = small["attn_sink"].reshape(8)
    slopes = 2.0 ** (-(jnp.arange(8, dtype=F32) + 1.0))
    gn_gain = small["ret_gn_gain"]
    g1, b1, g2, b2 = (small[n] for n in ("ln1_gain", "ln1_bias", "ln2_gain", "ln2_bias"))

    dist = pos is not None
    chips = lambda names: _gather_chips_rider([rest[REST_NAMES.index(n)] for n in names])
    first, second, third = ("w_ffn_up",), ("w_out", "w_ffn_gate", "w_ple_proj", "w_ple_gate"), ("w_ffn_down",)
    u, *c1 = _inproj(x2d, w_in_t, rider=chips(first) if dist else None)
    u3 = u.reshape(bsz, s, IN_WIDTH)
    y_hat, y_rstd, y_ret, ret_rb, ret_kvf, *o2 = _ret_fwd(u3, lgf_l, lgb_l, gn_gain,
                                 rider=_merge_riders([_gather_pass_rider(c1), chips(second)]) if dist else None)
    y_att, att_p, att_ps, *o3 = _attn_fwd(u3, slopes, sink, rider=_merge_riders(
        [_gather_pass_rider(o2[len(first):]), chips(third)]) if dist else None)
    gathered = dict(zip(first, o2[:len(first)]))
    gathered.update(zip(second, o3[:len(second)]))
    w_out = _assemble_weights({"w_out": gathered["w_out"]})["w_out"] if dist else rest["w_out"]
    zh1, r1, hb, *o4 = _outproj_ln1(y_ret.reshape(t, RET_WIDTH), y_att.reshape(t, ATTN_WIDTH), x2d, w_out, g1, b1,
                                    rider=_gather_pass_rider(o3[len(second):]) if dist else None)
    gathered.update(zip(third, o4))
    wts = _assemble_weights(gathered) if dist else rest
    dz2, dz2b, gs, us, acts, pg, ple, sq, dg2, db2 = _ffn_fwd(
        zh1, hb, p2d, tgt2d, g1, b1, g2, b2, wts["gate4"], wts["up4"], wts["down4"], wts["ple_proj"], wts["ple_gate"])
    dgs, dus, dsp, dple, dz1, dyr, dya, dg1, db1 = _ffn_bwd(dz2, gs, us, pg, ple, zh1, r1, g1, wts["gate4"],
                                                          wts["up4"], wts["down4"], wts["ple_gate"], wts["w_out"])
    ffn_parts = list(_wgrad_ffn(acts, dgs, dus, hb, dz2b))
    d_w_out, d_ple_gate, d_ple_proj, *th_ffn = _wgrad_misc(
        y_ret.reshape(t, RET_WIDTH), y_att.reshape(t, ATTN_WIDTH), dz1, hb, dsp, p2d, dple,
        rider=_exchange_halves_rider(ffn_parts) if dist else None)
    misc_parts = [d_w_out.reshape(N_SHARD, D_MODEL // N_SHARD, D_MODEL),
                  d_ple_proj.reshape(PLE_DIM, N_SHARD, D_MODEL // N_SHARD).transpose(1, 0, 2),
                  d_ple_gate.reshape(N_SHARD, D_MODEL // N_SHARD, D_MODEL)]
    dyr3, dya3 = dyr.reshape(bsz, s, RET_WIDTH), dya.reshape(bsz, s, ATTN_WIDTH)
    if dist:
        s_ffn = _add_halves(ffn_parts, th_ffn, pos)
        drq, drk, drv, drg, rpart, *o5 = _ret_bwd(u3, y_hat, y_rstd, (ret_rb, ret_kvf), dyr3, lgf_l, lgb_l, gn_gain,
                                                  rider=_merge_riders(
            [_exchange_chips_rider(s_ffn[3:5]), _exchange_halves_rider(misc_parts)]))
        s_misc = _add_halves(misc_parts, o5[2:], pos)
        daq, dakv, spart, *o6 = _attn_bwd(u3, dya3, att_p, att_ps,
                                          rider=_exchange_chips_rider([s_ffn[5]] + list(s_misc[3:])))
    else:
        drq, drk, drv, drg, rpart = _ret_bwd(u3, y_hat, y_rstd, (ret_rb, ret_kvf), dyr3, lgf_l, lgb_l, gn_gain)
        daq, dakv, spart = _attn_bwd(u3, dya3, att_p, att_ps)
    pieces = [a.reshape(t, -1) for a in (drq, drk, drv, drg, daq, dakv)]
    kv0 = CB_AK * LANES
    w_kv = jnp.concatenate([w_in_t[kv0 + o:kv0 + o + HEAD_DIM] for o in KV_ORDER], axis=0)
    d_in = _wgrad_in(pieces, x2d).reshape(N_SHARD, FFN_SHARD, D_MODEL)

    rsum = rpart
    lane_heads = lambda row: jnp.sum(row.reshape(4, 2, HEAD_DIM), axis=-1).reshape(8)
    dlg_f = lane_heads(rsum[:, 0, :]) + jnp.stack([jnp.sum(rsum[:, 2, :], -1), jnp.sum(rsum[:, 3, :], -1)], 1).reshape(8)
    dlg_b = lane_heads(rsum[:, 1, :]) + jnp.stack([jnp.sum(rsum[:, 4, :], -1), jnp.sum(rsum[:, 5, :], -1)], 1).reshape(8)
    chain = lambda d: -(math.log(2.0) * jnp.exp2(d)) / (1.0 - jnp.exp2(d))
    grads_small = {
        "ret_decay_fwd": (dlg_f * chain(dec_f)).reshape(1, 8),
        "ret_decay_bwd": (dlg_b * chain(dec_b)).reshape(1, 8),
        "attn_sink": jnp.sum(spart, axis=0)[:, 0:4, 0].reshape(1, 8),
        "ret_gn_gain": rsum[:, 6, :].reshape(1, RET_WIDTH),
        "ln1_gain": dg1, "ln1_bias": db1, "ln2_gain": dg2, "ln2_bias": db2,
    }
    if not dist:
        grad_x, = _inproj_bwd(dz1, pieces, w_in_t[:kv0], w_kv)
        grads_rest = [misc_parts[0]] + ffn_parts + misc_parts[1:]
        return sq[0, 0], grad_x.reshape(bsz, s, D_MODEL), d_in, grads_rest, grads_small
    *small_out, th_in = _small_allreduce_adamw(_pack_small(grads_small, sq[0, 0]), *small_state,
                                               rider=_exchange_halves_rider([d_in]))
    s_in = _add_halves([d_in], [th_in], pos)
    grad_x, chips_in = _inproj_bwd(dz1, pieces, w_in_t[:kv0], w_kv, rider=_exchange_chips_rider([s_in[1]]))
    sums32 = [s_in[0], s_misc[0], s_ffn[0], s_ffn[1], s_ffn[2], s_misc[1], s_misc[2]]
    from_chips = [chips_in, o6[1], o5[0], o5[1], o6[0], o6[2], o6[3]]
    return grad_x.reshape(bsz, s, D_MODEL), sums32, from_chips, small_out


BIG_NAMES = ("w_in", "w_out", "w_ffn_gate", "w_ffn_up", "w_ffn_down", "w_ple_proj", "w_ple_gate")
REST_NAMES = BIG_NAMES[1:]
TRANSPOSED = ("w_in", "w_ffn_gate", "w_ffn_up")
WEIGHT_ORDER = ("w_in", "ret_decay_fwd", "ret_decay_bwd", "ret_gn_gain", "attn_sink", "w_out", "ln1_gain",
                "ln1_bias", "w_ffn_gate", "w_ffn_up", "w_ffn_down", "w_ple_proj", "w_ple_gate", "ln2_gain", "ln2_bias")


def _shard_rows(name, a):
    return jnp.swapaxes(a[0], 0, 1) if name in TRANSPOSED else a[0]


def _unshard_rows(name, a):
    return (jnp.swapaxes(a, 0, 1) if name in TRANSPOSED else a)[None]


def _assemble_weights(gathered):
    cols = lambda a: a.transpose(1, 0, 2).reshape(a.shape[1], N_SHARD * a.shape[2])
    rows = lambda a: a.reshape(N_SHARD * a.shape[1], a.shape[2])
    same = lambda a: a
    layout = {"w_out": ("w_out", rows), "w_ffn_gate": ("gate4", same), "w_ffn_up": ("up4", same),
              "w_ffn_down": ("down4", same), "w_ple_proj": ("ple_proj", cols), "w_ple_gate": ("ple_gate", rows)}
    return {layout[n][0]: layout[n][1](a) for n, a in gathered.items()}


def kernel(x, p, w_in, ret_decay_fwd, ret_decay_bwd, ret_gn_gain, attn_sink, w_out, ln1_gain, ln1_bias, w_ffn_gate, w_ffn_up, w_ffn_down, w_ple_proj, w_ple_gate, ln2_gain, ln2_bias, loss_target, m_w_in, m_ret_decay_fwd, m_ret_decay_bwd, m_ret_gn_gain, m_attn_sink, m_w_out, m_ln1_gain, m_ln1_bias, m_w_ffn_gate, m_w_ffn_up, m_w_ffn_down, m_w_ple_proj, m_w_ple_gate, m_ln2_gain, m_ln2_bias, v_w_in, v_ret_decay_fwd, v_ret_decay_bwd, v_ret_gn_gain, v_attn_sink, v_w_out, v_ln1_gain, v_ln1_bias, v_w_ffn_gate, v_w_ffn_up, v_w_ffn_down, v_w_ple_proj, v_w_ple_gate, v_ln2_gain, v_ln2_bias):
    w = dict(w_in=w_in, ret_decay_fwd=ret_decay_fwd, ret_decay_bwd=ret_decay_bwd, ret_gn_gain=ret_gn_gain,
             attn_sink=attn_sink, w_out=w_out, ln1_gain=ln1_gain, ln1_bias=ln1_bias, w_ffn_gate=w_ffn_gate,
             w_ffn_up=w_ffn_up, w_ffn_down=w_ffn_down, w_ple_proj=w_ple_proj, w_ple_gate=w_ple_gate,
             ln2_gain=ln2_gain, ln2_bias=ln2_bias)
    m = dict(w_in=m_w_in, ret_decay_fwd=m_ret_decay_fwd, ret_decay_bwd=m_ret_decay_bwd, ret_gn_gain=m_ret_gn_gain,
             attn_sink=m_attn_sink, w_out=m_w_out, ln1_gain=m_ln1_gain, ln1_bias=m_ln1_bias, w_ffn_gate=m_w_ffn_gate,
             w_ffn_up=m_w_ffn_up, w_ffn_down=m_w_ffn_down, w_ple_proj=m_w_ple_proj, w_ple_gate=m_w_ple_gate,
             ln2_gain=m_ln2_gain, ln2_bias=m_ln2_bias)
    v = dict(w_in=v_w_in, ret_decay_fwd=v_ret_decay_fwd, ret_decay_bwd=v_ret_decay_bwd, ret_gn_gain=v_ret_gn_gain,
             attn_sink=v_attn_sink, w_out=v_w_out, ln1_gain=v_ln1_gain, ln1_bias=v_ln1_bias, w_ffn_gate=v_w_ffn_gate,
             w_ffn_up=v_w_ffn_up, w_ffn_down=v_w_ffn_down, w_ple_proj=v_w_ple_proj, w_ple_gate=v_w_ple_gate,
             ln2_gain=v_ln2_gain, ln2_bias=v_ln2_bias)
    big = lambda d: [_shard_rows(n, d[n]) for n in BIG_NAMES]
    small = lambda d: {n: d[n] for n in SMALL_NAMES}

    chip = 2 * lax.axis_index("x") + lax.axis_index("y")
    pos = jnp.stack([chip, lax.axis_index("c")]).astype(jnp.int32)

    shards = [a.astype(BF16) for a in big(w)]
    (w_in4,) = _all_gather_weights(shards[:1])
    w_in_t = w_in4.reshape(IN_WIDTH, D_MODEL)
    grad_x, sums32, from_chips, (g_s, d_s, m_s, v_s) = _local_step(
        x, p[0], loss_target, w_in_t, shards[1:], small(w), pos=pos,
        small_state=(_pack_small(small(w)), _pack_small(small(m)), _pack_small(small(v))))
    g_big, d_big, m_big, v_big = _adamw(big(w), _join_halves(_add_chips(sums32, from_chips, pos)), big(m), big(v))
    loss = g_s[4, LOSS_COL] * (0.5 / D_MODEL)

    def tree(bigs, packed):
        out = {n: _unshard_rows(n, a) for n, a in zip(BIG_NAMES, bigs)}
        out.update(_unpack_small(packed))
        return [out[n] for n in WEIGHT_ORDER]

    return (loss, grad_x, *tree(g_big, g_s), *tree(d_big, d_s), *tree(m_big, m_s), *tree(v_big, v_s))
```

```python
import functools
import math

import jax
import jax.numpy as jnp
from jax import lax
from jax.experimental import pallas as pl
from jax.experimental.pallas import tpu as pltpu

F32 = jnp.float32
BF16 = jnp.bfloat16

D_MODEL = 1024
HEAD_DIM = 64
RET_HEADS = 8
ATTN_HEADS = 8
RET_WIDTH = 512
ATTN_WIDTH = 512
KV_WIDTH = 128
IN_WIDTH = 2816
FFN = 2816
N_SHARD = 4
FFN_SHARD = FFN // N_SHARD
PLE_DIM = 256
CHUNK = 128
LANES = 128
ALPHA = 2.0 ** 0.25
LN_EPS = 1e-5
GN_EPS = 1e-5
NEG_INF = -1e30
ADAM_LR = 0.001
ADAM_B1 = 0.9
ADAM_B2 = 0.999
ADAM_EPS = 1e-08
ADAM_WD = 0.01
ADAM_STEP = 10
VMEM_LIMIT = 56 * 1024 * 1024
MESH = pl.DeviceIdType.MESH

CB_RQ, CB_RK, CB_RV, CB_RG, CB_AQ, CB_AK, CB_AV = 0, 4, 8, 12, 16, 20, 21


def _dot(a, b):
    return jnp.dot(a, b, preferred_element_type=F32)


def _dot_nt(a, b):
    return lax.dot_general(a, b, (((1,), (1,)), ((), ())), preferred_element_type=F32)


def _dot_tn(a, b):
    return lax.dot_general(a, b, (((0,), (0,)), ((), ())), preferred_element_type=F32)


def _sigmoid(x):
    return 1.0 / (1.0 + jnp.exp(-x))


def _params(*sem, vmem=None):
    return pltpu.CompilerParams(dimension_semantics=tuple(sem) if sem else None, vmem_limit_bytes=vmem)


class _Rider:
    def __init__(self, ins, out_shapes, sems, start, finish, aliases=None):
        self.ins, self.out_shapes, self.sems = list(ins), list(out_shapes), list(sems)
        self.start, self.finish, self.aliases = start, finish, dict(aliases or {})


def _merge_riders(riders):
    riders = [r for r in riders if r is not None]
    if len(riders) == 1:
        return riders[0]
    bounds, aliases = [], {}
    i0 = o0 = s0 = 0
    for r in riders:
        bounds.append((i0, o0, s0))
        aliases.update({i0 + i: o0 + o for i, o in r.aliases.items()})
        i0, o0, s0 = i0 + len(r.ins), o0 + len(r.out_shapes), s0 + len(r.sems)

    def each(method):
        def run(ins, outs, sems):
            for r, (i, o, s) in zip(riders, bounds):
                getattr(r, method)(ins[i:i + len(r.ins)], outs[o:o + len(r.out_shapes)], sems[s:s + len(r.sems)])
        return run

    return _Rider([a for r in riders for a in r.ins], [a for r in riders for a in r.out_shapes],
                  [a for r in riders for a in r.sems], each("start"), each("finish"), aliases)


def _hosted_call(body, name, grid, in_specs, out_specs, out_shape, scratch_shapes, operands, rider=None,
                 semantics=None):
    n_in, n_out, n_scr = len(in_specs), len(out_specs), len(scratch_shapes)
    if rider is None:
        return pl.pallas_call(
            body, name=name, grid=grid, in_specs=in_specs, out_specs=out_specs, out_shape=out_shape,
            scratch_shapes=scratch_shapes,
            compiler_params=_params(*(semantics or ["parallel"] * len(grid)), vmem=VMEM_LIMIT))(*operands)
    r_in, r_out = len(rider.ins), len(rider.out_shapes)

    def full_body(*refs):
        main_in, rin = refs[:n_in], refs[n_in:n_in + r_in]
        o0 = n_in + r_in
        main_out, rout = refs[o0:o0 + n_out], refs[o0 + n_out:o0 + n_out + r_out]
        s0 = o0 + n_out + r_out
        main_scr, rsem = refs[s0:s0 + n_scr], refs[s0 + n_scr:]
        first = functools.reduce(jnp.logical_and, [pl.program_id(a) == 0 for a in range(len(grid))])
        last = functools.reduce(jnp.logical_and, [pl.program_id(a) == g - 1 for a, g in enumerate(grid)])

        @pl.when(first)
        def _():
            rider.start(rin, rout, rsem)

        body(*main_in, *main_out, *main_scr)

        @pl.when(last)
        def _():
            rider.finish(rin, rout, rsem)

    hbm = pl.BlockSpec(memory_space=pl.ANY)
    return pl.pallas_call(
        full_body, name=name, grid=grid,
        in_specs=list(in_specs) + [hbm] * r_in, out_specs=list(out_specs) + [hbm] * r_out,
        out_shape=list(out_shape) + rider.out_shapes,
        scratch_shapes=list(scratch_shapes) + rider.sems,
        input_output_aliases={n_in + i: n_out + o for i, o in rider.aliases.items()},
        compiler_params=_params(*(["arbitrary"] * len(grid)), vmem=VMEM_LIMIT),
    )(*operands, *rider.ins)


def _head_mean(x, m0):
    s0 = jnp.sum(jnp.where(m0, x, 0.0), axis=1, keepdims=True)
    s1 = jnp.sum(jnp.where(m0, 0.0, x), axis=1, keepdims=True)
    return jnp.where(m0, s0, s1) * (1.0 / HEAD_DIM)


def _inproj(x2d, w_in_t, rider=None):
    t = x2d.shape[0]
    tm = 512
    nb = 256

    def body(x_ref, w_ref, o_ref):
        xb = x_ref[...].astype(BF16)
        for n in range(0, IN_WIDTH, nb):
            o_ref[:, n:n + nb] = _dot_nt(xb, w_ref[n:n + nb, :]).astype(BF16)

    return _hosted_call(
        body, "inproj", (t // tm,),
        in_specs=[pl.BlockSpec((tm, D_MODEL), lambda i: (i, 0)),
                  pl.BlockSpec((IN_WIDTH, D_MODEL), lambda i: (0, 0))],
        out_specs=[pl.BlockSpec((tm, IN_WIDTH), lambda i: (i, 0))],
        out_shape=[jax.ShapeDtypeStruct((t, IN_WIDTH), BF16)],
        scratch_shapes=[], operands=(x2d, w_in_t), rider=rider)


def _outproj_ln1(y_ret, y_att, x2d, w_out, gain, bias, rider=None):
    t = x2d.shape[0]
    tm = 512

    def body(yr_ref, ya_ref, x_ref, w_ref, g_ref, b_ref, zh_ref, r_ref, hb_ref):
        mix = _dot(yr_ref[...], w_ref[0:RET_WIDTH, :]) + _dot(ya_ref[...], w_ref[RET_WIDTH:, :])
        z = ALPHA * x_ref[...] + mix
        mu = jnp.mean(z, axis=1, keepdims=True)
        zc = z - mu
        var = jnp.mean(zc * zc, axis=1, keepdims=True)
        r = lax.rsqrt(var + LN_EPS)
        zh = zc * r
        zh_ref[...] = zh
        r_ref[...] = r
        hb_ref[...] = (zh * g_ref[...] + b_ref[...]).astype(BF16)

    row = lambda w: pl.BlockSpec((tm, w), lambda i: (i, 0))
    const = lambda s: pl.BlockSpec(s, lambda i: (0, 0))
    return _hosted_call(
        body, "outproj_ln1", (t // tm,),
        in_specs=[row(RET_WIDTH), row(ATTN_WIDTH), row(D_MODEL), const((D_MODEL, D_MODEL)),
                  const((1, D_MODEL)), const((1, D_MODEL))],
        out_specs=[row(D_MODEL), row(1), row(D_MODEL)],
        out_shape=[jax.ShapeDtypeStruct((t, D_MODEL), F32), jax.ShapeDtypeStruct((t, 1), F32),
                   jax.ShapeDtypeStruct((t, D_MODEL), BF16)],
        scratch_shapes=[], operands=(y_ret, y_att, x2d, w_out, gain, bias), rider=rider)


def _load_resident(step, pairs, sems):
    copies = [pltpu.make_async_copy(src, dst, sems.at[i]) for i, (src, dst) in enumerate(pairs)]

    @pl.when(step == 0)
    def _():
        for cp in copies:
            cp.start()
        for cp in copies:
            cp.wait()


FFN_CHUNK = 256
N_FFN_CHUNK = FFN // FFN_CHUNK


def _resident_quarters(hbm, vmem):
    q = FFN // N_SHARD
    return [(hbm.at[pl.ds(j * q, q), :], vmem.at[pl.ds(j * q, q), :]) for j in range(N_SHARD)]


def _ffn_fwd(zh1, hb, p2d, tgt, g1, b1, g2, b2, wg4, wu4, wd4, wpe, wpg):
    t = zh1.shape[0]
    tm = 256
    wg_t, wu_t, wd_all = (w.reshape(FFN, D_MODEL) for w in (wg4, wu4, wd4))

    def body(zh_ref, hb_ref, p_ref, t_ref, g1_ref, b1_ref, g2_ref, b2_ref,
             wg_hbm, wu_hbm, wd_hbm, wpe_hbm, wpg_hbm,
             dz_ref, dzb_ref, gs_ref, us_ref, act_ref, pg_ref, ple_ref, loss_ref, dg2_ref, db2_ref,
             wg, wu, wd, wpe, wpg, wsem):
        step = pl.program_id(0)
        loads = _resident_quarters(wg_hbm, wg) + _resident_quarters(wu_hbm, wu) + _resident_quarters(wd_hbm, wd)
        _load_resident(step, loads + [(wpe_hbm, wpe), (wpg_hbm, wpg)], wsem)

        @pl.when(step == 0)
        def _():
            loss_ref[...] = jnp.zeros_like(loss_ref)
            dg2_ref[...] = jnp.zeros_like(dg2_ref)
            db2_ref[...] = jnp.zeros_like(db2_ref)

        h1 = zh_ref[...] * g1_ref[...] + b1_ref[...]
        hbv = hb_ref[...]
        ffn = jnp.zeros((tm, D_MODEL), F32)
        acts = []
        chunks = [slice(n * FFN_CHUNK, (n + 1) * FFN_CHUNK) for n in range(N_FFN_CHUNK)]
        for n in range(N_FFN_CHUNK + 1):
            if n < N_FFN_CHUNK:
                gj = _dot_nt(hbv, wg[chunks[n], :])
                uj = _dot_nt(hbv, wu[chunks[n], :])
                gs_ref[:, chunks[n]] = gj.astype(BF16)
                us_ref[:, chunks[n]] = uj.astype(BF16)
                acts.append((gj * _sigmoid(gj) * uj).astype(BF16))
                act_ref[:, chunks[n]] = acts[n]
            if n > 0:
                ffn = ffn + _dot(acts[n - 1], wd[chunks[n - 1], :])
        ple = _dot(p_ref[...].astype(BF16), wpe[...])
        pg = _sigmoid(_dot(hbv, wpg[...]))
        pg_ref[...] = pg.astype(BF16)
        ple_ref[...] = ple.astype(BF16)
        z2 = ALPHA * h1 + ffn + pg * ple
        mu = jnp.mean(z2, axis=1, keepdims=True)
        zc = z2 - mu
        var = jnp.mean(zc * zc, axis=1, keepdims=True)
        r = lax.rsqrt(var + LN_EPS)
        zh2 = zc * r
        err = zh2 * g2_ref[...] + b2_ref[...] - t_ref[...]
        loss_ref[...] += jnp.sum(err * err)
        dy = err * (1.0 / D_MODEL)
        dg2_ref[...] += jnp.sum(dy * zh2, axis=0, keepdims=True)
        db2_ref[...] += jnp.sum(dy, axis=0, keepdims=True)
        dzh = dy * g2_ref[...]
        m1 = jnp.mean(dzh, axis=1, keepdims=True)
        m2 = jnp.mean(dzh * zh2, axis=1, keepdims=True)
        dz2 = r * (dzh - m1 - zh2 * m2)
        dz_ref[...] = dz2
        dzb_ref[...] = dz2.astype(BF16)

    row = lambda w: pl.BlockSpec((tm, w), lambda i: (i, 0))
    const = lambda s: pl.BlockSpec(s, lambda i: (0, 0))
    hid_shape = jax.ShapeDtypeStruct((t, FFN), BF16)
    hbm = pl.BlockSpec(memory_space=pl.ANY)
    return pl.pallas_call(
        body, name="ffn_fwd", grid=(t // tm,),
        in_specs=[row(D_MODEL), row(D_MODEL), row(PLE_DIM), row(D_MODEL),
                  const((1, D_MODEL)), const((1, D_MODEL)), const((1, D_MODEL)), const((1, D_MODEL)),
                  hbm, hbm, hbm, hbm, hbm],
        out_specs=[row(D_MODEL), row(D_MODEL), row(FFN), row(FFN), row(FFN), row(D_MODEL), row(D_MODEL),
                   const((8, LANES)), const((1, D_MODEL)), const((1, D_MODEL))],
        out_shape=[jax.ShapeDtypeStruct((t, D_MODEL), F32), jax.ShapeDtypeStruct((t, D_MODEL), BF16),
                   hid_shape, hid_shape, hid_shape,
                   jax.ShapeDtypeStruct((t, D_MODEL), BF16), jax.ShapeDtypeStruct((t, D_MODEL), BF16),
                   jax.ShapeDtypeStruct((8, LANES), F32),
                   jax.ShapeDtypeStruct((1, D_MODEL), F32), jax.ShapeDtypeStruct((1, D_MODEL), F32)],
        scratch_shapes=[pltpu.VMEM((FFN, D_MODEL), BF16), pltpu.VMEM((FFN, D_MODEL), BF16),
                        pltpu.VMEM((FFN, D_MODEL), BF16),
                        pltpu.VMEM(wpe.shape, BF16), pltpu.VMEM(wpg.shape, BF16),
                        pltpu.SemaphoreType.DMA((3 * N_SHARD + 2,))],
        compiler_params=_params("arbitrary", vmem=VMEM_LIMIT),
    )(zh1, hb, p2d, tgt, g1, b1, g2, b2, wg_t, wu_t, wd_all, wpe, wpg)


def _ret_tables(lgf, lgb):
    c = CHUNK
    row = lax.broadcasted_iota(jnp.int32, (c, LANES), 0).astype(F32)
    ii = lax.broadcasted_iota(jnp.int32, (c, c), 0).astype(F32)
    jj = lax.broadcasted_iota(jnp.int32, (c, c), 1).astype(F32)
    diff = ii - jj
    dmats = []
    for h in range(2):
        lf = lgf[:, h * HEAD_DIM:h * HEAD_DIM + 1]
        lb = lgb[:, h * HEAD_DIM:h * HEAD_DIM + 1]
        dmats.append(jnp.where(diff > 0, jnp.exp(lf * jnp.maximum(diff, 0.0)),
                               jnp.where(diff < 0, jnp.exp(lb * jnp.maximum(-diff, 0.0)), 2.0)))
    tab = dict(
        qdec_f=jnp.exp(lgf * (row + 1.0)), kdec_f=jnp.exp(lgf * (c - 1.0 - row)),
        qdec_b=jnp.exp(lgb * (c - row)), kdec_b=jnp.exp(lgb * row),
        cdec_f=jnp.exp(lgf * c), cdec_b=jnp.exp(lgb * c),
        d0=dmats[0], d1=dmats[1], row=row, diff=diff)
    r = lax.broadcasted_iota(jnp.int32, (LANES, LANES), 0) < HEAD_DIM
    cc = lax.broadcasted_iota(jnp.int32, (LANES, LANES), 1) < HEAD_DIM
    tab["bd"] = r == cc
    tab["m0"] = lax.broadcasted_iota(jnp.int32, (c, LANES), 1) < HEAD_DIM
    return tab


def _ret_specs(bsz, s):
    blk = lambda cb: pl.BlockSpec((bsz, s, LANES), lambda p, cb=cb: (0, 0, cb + p))
    lane = pl.BlockSpec((None, 1, LANES), lambda p: (p, 0, 0))
    gain = pl.BlockSpec((1, LANES), lambda p: (0, p))
    pair = pl.BlockSpec((bsz, s, LANES), lambda p: (0, 0, p))
    return blk, lane, gain, pair


def _ret_state_spec(bsz, n_chunk):
    spec = pl.BlockSpec((None, bsz, n_chunk, LANES, LANES), lambda p: (p, 0, 0, 0, 0))
    return spec, jax.ShapeDtypeStruct((4, bsz, n_chunk, LANES, LANES), F32)


def _ret_kv_states(tb, k_ref, v_ref, rb_ref, kvf_ref, n_chunk):
    c = CHUNK
    bsz = k_ref.shape[0]
    bd = tb["bd"]

    def contributions(n, carry):
        sl = pl.ds(pl.multiple_of(n * c, c), c)
        kfb = []
        for b in range(bsz):
            k32 = k_ref[b, sl, :].astype(F32)
            kfb.append(jnp.concatenate([k32 * tb["kdec_f"], k32 * tb["kdec_b"]], axis=1).astype(BF16))
        kvs = [_dot_tn(kfb[b], v_ref[b, sl, :]) for b in range(bsz)]
        for b in range(bsz):
            kvf_ref[b, n] = jnp.where(bd, kvs[b][0:LANES], 0.0)
            rb_ref[b, n] = jnp.where(bd, kvs[b][LANES:], 0.0)
        return carry

    lax.fori_loop(0, n_chunk, contributions, 0, unroll=2)

    def recur(i, rbs):
        n = n_chunk - 1 - i
        new = []
        for b in range(bsz):
            own = rb_ref[b, n]
            rb_ref[b, n] = rbs[b]
            new.append(rbs[b] * tb["cdec_b"] + own)
        return tuple(new)

    lax.fori_loop(0, n_chunk, recur, tuple(jnp.zeros((LANES, LANES), F32) for _ in range(bsz)))


def _split_rows(x, m0):
    return jnp.concatenate([jnp.where(m0, x, 0.0), jnp.where(m0, 0.0, x)], axis=0).astype(BF16)


def _ret_fwd(u3, lgf_l, lgb_l, gn_gain, rider=None):
    bsz, s, _ = u3.shape
    n_chunk = s // CHUNK
    c = CHUNK

    def body(q_ref, k_ref, v_ref, g_ref, lgf_ref, lgb_ref, gain_ref, yh_ref, rstd_ref, o_ref, rb_ref, kvf_ref):
        tb = _ret_tables(lgf_ref[...], lgb_ref[...])
        m0 = tb["m0"]
        gain = gain_ref[...]
        rows = range(bsz)
        _ret_kv_states(tb, k_ref, v_ref, rb_ref, kvf_ref, n_chunk)

        def chunk(n, rfs):
            sl = pl.ds(pl.multiple_of(n * c, c), c)
            qs = [q_ref[b, sl, :].astype(F32) * 0.125 for b in rows]
            s01 = [_dot_nt(_split_rows(qs[b], m0), k_ref[b, sl, :]) for b in rows]
            ys = []
            for b in rows:
                lhs = jnp.concatenate([s01[b][0:c] * tb["d0"], s01[b][c:] * tb["d1"],
                                       qs[b] * tb["qdec_f"], qs[b] * tb["qdec_b"]], axis=1).astype(BF16)
                rhs = jnp.concatenate([_split_rows(v_ref[b, sl, :].astype(F32), m0),
                                       rfs[b].astype(BF16), rb_ref[b, n].astype(BF16)], axis=0)
                ys.append(_dot(lhs, rhs))
            new = []
            for b in rows:
                y = ys[b]
                mu = _head_mean(y, m0)
                yc = y - mu
                rstd = lax.rsqrt(_head_mean(yc * yc, m0) + GN_EPS)
                yh = yc * rstd
                g = g_ref[b, sl, :].astype(F32)
                yh_ref[b, sl, :] = yh
                rstd_ref[b, sl, :] = rstd
                o_ref[b, sl, :] = (yh * gain * (g * _sigmoid(g))).astype(BF16)
                new.append(rfs[b] * tb["cdec_f"] + kvf_ref[b, n])
            return tuple(new)

        lax.fori_loop(0, n_chunk, chunk, tuple(jnp.zeros((LANES, LANES), F32) for _ in rows))

    blk, lane, gain, pair = _ret_specs(bsz, s)
    state, state_shape = _ret_state_spec(bsz, n_chunk)
    return _hosted_call(
        body, "ret_fwd", (4,),
        in_specs=[blk(CB_RQ), blk(CB_RK), blk(CB_RV), blk(CB_RG), lane, lane, gain],
        out_specs=[pair, pair, pair, state, state],
        out_shape=[jax.ShapeDtypeStruct((bsz, s, RET_WIDTH), F32), jax.ShapeDtypeStruct((bsz, s, RET_WIDTH), F32),
                   jax.ShapeDtypeStruct((bsz, s, RET_WIDTH), BF16), state_shape, state_shape],
        scratch_shapes=[],
        operands=(u3, u3, u3, u3, lgf_l, lgb_l, gn_gain), rider=rider)


def _ret_bwd(u3, y_hat, y_rstd, states, d_o, lgf_l, lgb_l, gn_gain, rider=None):
    bsz, s, _ = u3.shape
    n_chunk = s // CHUNK
    c = CHUNK

    def body(q_ref, k_ref, v_ref, g_ref, yh_ref, rstd_ref, do_ref, lgf_ref, lgb_ref, gain_ref, rb_ref, kvf_ref,
             dq_ref, dk_ref, dv_ref, dg_ref, part_ref,
             rf_ref, dirf_ref, dy_ref, dk_acc, dv_acc, pa0, pa1, vec_ref):
        tb = _ret_tables(lgf_ref[...], lgb_ref[...])
        m0, bd, row = tb["m0"], tb["bd"], tb["row"]
        gain = gain_ref[...]
        wf = jnp.maximum(tb["diff"], 0.0)
        wb = jnp.maximum(-tb["diff"], 0.0)
        rows = range(bsz)
        zero_states = tuple(jnp.zeros((LANES, LANES), F32) for _ in rows)
        for ref in (pa0, pa1):
            ref[...] = jnp.zeros_like(ref)
        vec_ref[...] = jnp.zeros_like(vec_ref)

        def sweep_fwd(n, carry):
            rfs, gbs = carry
            sl = pl.ds(pl.multiple_of(n * c, c), c)
            qs, ks, vs, dys, dybs, q01, k01, dy01 = [], [], [], [], [], [], [], []
            dgain = jnp.zeros((1, LANES), F32)
            for b in rows:
                q = q_ref[b, sl, :].astype(F32) * 0.125
                k = k_ref[b, sl, :]
                yh = yh_ref[b, sl, :]
                rstd = rstd_ref[b, sl, :]
                do = do_ref[b, sl, :].astype(F32)
                g = g_ref[b, sl, :].astype(F32)
                sg = _sigmoid(g)
                sil = g * sg
                dyh = do * gain * sil
                dg_ref[b, sl, :] = (do * yh * gain * sg * (1.0 + g * (1.0 - sg))).astype(BF16)
                dgain = dgain + jnp.sum(do * yh * sil, axis=0, keepdims=True)
                dy = rstd * (dyh - _head_mean(dyh, m0) - yh * _head_mean(dyh * yh, m0))
                dyb = dy.astype(BF16)
                dy_ref[b, sl, :] = dyb
                rf_ref[b, n] = rfs[b]
                qs.append(q)
                ks.append(k)
                vs.append(v_ref[b, sl, :])
                dys.append(dy)
                dybs.append(dyb)
                q01.append(_split_rows(q, m0))
                k01.append(_split_rows(k.astype(F32), m0))
                dy01.append(_split_rows(dy, m0))
            s01 = [_dot_nt(q01[b], ks[b]) for b in rows]
            da01 = [_dot_nt(dy01[b], vs[b]) for b in rows]
            rbn = [rb_ref[b, n] for b in rows]
            states = [jnp.concatenate([rfs[b], rbn[b]], axis=0).astype(BF16) for b in rows]
            dqc = [_dot_nt(dybs[b], states[b]) for b in rows]
            gbb = [gbs[b].astype(BF16) for b in rows]
            dkb = [_dot_nt(vs[b], gbb[b]) for b in rows]
            qfb = [jnp.concatenate([qs[b] * tb["qdec_f"], qs[b] * tb["qdec_b"]], axis=1) for b in rows]
            direct = [_dot_tn(qfb[b].astype(BF16), dybs[b]) for b in rows]
            ds_cat, ds_rows, a_rows = [], [], []
            for b in rows:
                a0 = s01[b][0:c] * tb["d0"]
                a1 = s01[b][c:] * tb["d1"]
                pa0[...] += da01[b][0:c] * a0
                pa1[...] += da01[b][c:] * a1
                ds0 = da01[b][0:c] * tb["d0"]
                ds1 = da01[b][c:] * tb["d1"]
                ds_cat.append(jnp.concatenate([ds0, ds1], axis=1).astype(BF16))
                ds_rows.append(jnp.concatenate([ds0, ds1], axis=0).astype(BF16))
                a_rows.append(jnp.concatenate([a0, a1], axis=0).astype(BF16))
            kbd = [ks[b].astype(F32) * tb["kdec_b"] for b in rows]
            dq_in = [_dot(ds_cat[b], k01[b]) for b in rows]
            dk_in = [_dot_tn(ds_rows[b], q01[b]) for b in rows]
            dv_in = [_dot_tn(a_rows[b], dy01[b]) for b in rows]
            dv_gb = [_dot(kbd[b].astype(BF16), gbb[b]) for b in rows]
            new_rf, new_gb = [], []
            dlf = jnp.zeros((1, LANES), F32)
            dlb = jnp.zeros((1, LANES), F32)
            for b in rows:
                dqf, dqb = dqc[b][:, 0:LANES], dqc[b][:, LANES:]
                qf, qb = qfb[b][:, 0:LANES], qfb[b][:, LANES:]
                dq = dq_in[b] + dqf * tb["qdec_f"] + dqb * tb["qdec_b"]
                dq_ref[b, sl, :] = (dq * 0.125).astype(BF16)
                dk_acc[b, sl, :] = dk_in[b] + dkb[b] * tb["kdec_b"]
                dv_acc[b, sl, :] = dv_in[b] + dv_gb[b]
                dlf = dlf + jnp.sum((row + 1.0) * qf * dqf, axis=0, keepdims=True)
                dlb = dlb + jnp.sum((c - row) * qb * dqb + row * kbd[b] * dkb[b], axis=0, keepdims=True)
                dlb = dlb + c * tb["cdec_b"] * jnp.sum(gbs[b] * rbn[b], axis=0, keepdims=True)
                dirf_ref[b, n] = jnp.where(bd, direct[b][0:LANES], 0.0)
                new_gb.append(jnp.where(bd, direct[b][LANES:], 0.0) + tb["cdec_b"] * gbs[b])
                new_rf.append(rfs[b] * tb["cdec_f"] + kvf_ref[b, n])
            vec_ref[0:1, :] += dlf
            vec_ref[1:2, :] += dlb
            vec_ref[6:7, :] += dgain
            return tuple(new_rf), tuple(new_gb)

        lax.fori_loop(0, n_chunk, sweep_fwd, (zero_states, zero_states))

        def sweep_bwd(i, gfs):
            n = n_chunk - 1 - i
            sl = pl.ds(pl.multiple_of(n * c, c), c)
            gfb = [gfs[b].astype(BF16) for b in rows]
            kfd = [k_ref[b, sl, :].astype(F32) * tb["kdec_f"] for b in rows]
            dkf = [_dot_nt(v_ref[b, sl, :], gfb[b]) for b in rows]
            dvf = [_dot(kfd[b].astype(BF16), gfb[b]) for b in rows]
            new = []
            dlf = jnp.zeros((1, LANES), F32)
            for b in rows:
                dk_ref[b, sl, :] = (dk_acc[b, sl, :] + dkf[b] * tb["kdec_f"]).astype(BF16)
                dv_ref[b, sl, :] = (dv_acc[b, sl, :] + dvf[b]).astype(BF16)
                dlf = dlf + jnp.sum((c - 1.0 - row) * kfd[b] * dkf[b], axis=0, keepdims=True)
                dlf = dlf + c * tb["cdec_f"] * jnp.sum(gfs[b] * rf_ref[b, n], axis=0, keepdims=True)
                new.append(dirf_ref[b, n] + tb["cdec_f"] * gfs[b])
            vec_ref[0:1, :] += dlf
            return tuple(new)

        lax.fori_loop(0, n_chunk, sweep_bwd, zero_states, unroll=2)
        vec_ref[2:3, :] = jnp.sum(pa0[...] * wf, axis=0, keepdims=True)
        vec_ref[3:4, :] = jnp.sum(pa1[...] * wf, axis=0, keepdims=True)
        vec_ref[4:5, :] = jnp.sum(pa0[...] * wb, axis=0, keepdims=True)
        vec_ref[5:6, :] = jnp.sum(pa1[...] * wb, axis=0, keepdims=True)
        part_ref[...] = vec_ref[...]

    blk, lane, gain, pair = _ret_specs(bsz, s)
    out_bf = jax.ShapeDtypeStruct((bsz, s, RET_WIDTH), BF16)
    state = pltpu.VMEM((bsz, n_chunk, LANES, LANES), F32)
    saved = _ret_state_spec(bsz, n_chunk)[0]
    return _hosted_call(
        body, "ret_bwd", (4,),
        in_specs=[blk(CB_RQ), blk(CB_RK), blk(CB_RV), blk(CB_RG), pair, pair, pair, lane, lane, gain, saved, saved],
        out_specs=[pair, pair, pair, pair, pl.BlockSpec((None, 8, LANES), lambda p: (p, 0, 0))],
        out_shape=[out_bf, out_bf, out_bf, out_bf, jax.ShapeDtypeStruct((4, 8, LANES), F32)],
        scratch_shapes=[state, state,
                        pltpu.VMEM((bsz, s, LANES), BF16), pltpu.VMEM((bsz, s, LANES), F32),
                        pltpu.VMEM((bsz, s, LANES), F32),
                        pltpu.VMEM((c, c), F32), pltpu.VMEM((c, c), F32), pltpu.VMEM((8, LANES), F32)],
        operands=(u3, u3, u3, u3, y_hat, y_rstd, d_o, lgf_l, lgb_l, gn_gain, *states), rider=rider)


def _attn_window_tables(n, s):
    qi = lax.broadcasted_iota(jnp.int32, (CHUNK, 3 * CHUNK), 0)
    kj = lax.broadcasted_iota(jnp.int32, (CHUNK, 3 * CHUNK), 1)
    dist = jnp.abs(kj - CHUNK - qi)
    kpos = n * CHUNK - CHUNK + kj
    valid = (dist <= CHUNK) & (kpos >= 0) & (kpos < s)
    return dist.astype(F32), valid


def _dup_kv_head(x, g):
    lane = lax.broadcasted_iota(jnp.int32, x.shape, 1)
    keep = (lane < HEAD_DIM) == (g == 0)
    xf = x.astype(F32)
    return jnp.where(keep, xf, pltpu.roll(xf, HEAD_DIM, 1))


def _attn_specs(s):
    q = pl.BlockSpec((None, s, 2 * LANES), lambda b, g: (b, 0, CB_AQ // 2 + g))
    k = pl.BlockSpec((None, s, LANES), lambda b, g: (b, 0, CB_AK))
    v = pl.BlockSpec((None, s, LANES), lambda b, g: (b, 0, CB_AV))
    grp = pl.BlockSpec((None, s, 2 * LANES), lambda b, g: (b, 0, g))
    smem = pl.BlockSpec(memory_space=pltpu.SMEM)
    return q, k, v, grp, smem


def _fill_padded(dst_ref, val, s):
    dst_ref[0:CHUNK, :] = jnp.zeros((CHUNK, LANES), dst_ref.dtype)
    dst_ref[CHUNK:CHUNK + s, :] = val.astype(dst_ref.dtype)
    dst_ref[CHUNK + s:2 * CHUNK + s, :] = jnp.zeros((CHUNK, LANES), dst_ref.dtype)


def _attn_probs(sc, slope, snk, dist, valid):
    sc = jnp.where(valid, sc - slope * dist, NEG_INF)
    m = jnp.maximum(jnp.max(sc, axis=1, keepdims=True), snk)
    e = jnp.exp(sc - m)
    es = jnp.exp(snk - m)
    inv = 1.0 / (jnp.sum(e, axis=1, keepdims=True) + es)
    return e * inv, es * inv


def _stack_heads(x2, m0):
    parts = []
    for pr in range(2):
        xp = x2[:, pr * LANES:(pr + 1) * LANES]
        parts += [jnp.where(m0, xp, 0.0), jnp.where(m0, 0.0, xp)]
    return jnp.concatenate(parts, axis=0).astype(BF16)


def _unstack_pair(x_all, pr, m0):
    return jnp.where(m0, x_all[(2 * pr) * CHUNK:(2 * pr + 1) * CHUNK], x_all[(2 * pr + 1) * CHUNK:(2 * pr + 2) * CHUNK])


def _attn_saved_specs(bsz, n_blk):
    specs = [pl.BlockSpec((None, None, n_blk, 4 * CHUNK, w), lambda b, g: (b, g, 0, 0, 0)) for w in (3 * CHUNK, 1)]
    shapes = [jax.ShapeDtypeStruct((bsz, 2, n_blk, 4 * CHUNK, 3 * CHUNK), BF16),
              jax.ShapeDtypeStruct((bsz, 2, n_blk, 4 * CHUNK, 1), F32)]
    return specs, shapes


def _attn_fwd(u3, slopes, sink, rider=None):
    bsz, s, _ = u3.shape
    n_blk = s // CHUNK

    def body(slope_ref, sink_ref, q_ref, k_ref, v_ref, o_ref, p_ref, ps_ref, kp_ref, vp_ref):
        g = pl.program_id(1)
        _fill_padded(kp_ref, _dup_kv_head(k_ref[...], g), s)
        _fill_padded(vp_ref, _dup_kv_head(v_ref[...], g), s)
        m0 = lax.broadcasted_iota(jnp.int32, (CHUNK, LANES), 1) < HEAD_DIM

        def blk(n, carry):
            r0 = pl.multiple_of(n * CHUNK, CHUNK)
            kw = kp_ref[pl.ds(r0, 3 * CHUNK), :]
            vw = vp_ref[pl.ds(r0, 3 * CHUNK), :]
            dist, valid = _attn_window_tables(n, s)
            q_all = _stack_heads(q_ref[pl.ds(r0, CHUNK), :].astype(F32) * 0.125, m0)
            sc_all = _dot_nt(q_all, kw)
            probs, sinks = [], []
            for i in range(4):
                p, ps = _attn_probs(sc_all[i * CHUNK:(i + 1) * CHUNK], slope_ref[g * 4 + i], sink_ref[g * 4 + i],
                                    dist, valid)
                probs.append(p.astype(BF16))
                sinks.append(ps)
            p_all = jnp.concatenate(probs, axis=0)
            p_ref[n] = p_all
            ps_ref[n] = jnp.concatenate(sinks, axis=0)
            out_all = _dot(p_all, vw)
            for pr in range(2):
                o_ref[pl.ds(r0, CHUNK), pr * LANES:(pr + 1) * LANES] = _unstack_pair(out_all, pr, m0).astype(BF16)
            return carry

        lax.fori_loop(0, n_blk, blk, 0)

    q, k, v, grp, smem = _attn_specs(s)
    saved_specs, saved_shapes = _attn_saved_specs(bsz, n_blk)
    return _hosted_call(
        body, "attn_fwd", (bsz, 2),
        in_specs=[smem, smem, q, k, v],
        out_specs=[grp] + saved_specs,
        out_shape=[jax.ShapeDtypeStruct((bsz, s, ATTN_WIDTH), BF16)] + saved_shapes,
        scratch_shapes=[pltpu.VMEM((s + 2 * CHUNK, LANES), BF16), pltpu.VMEM((s + 2 * CHUNK, LANES), BF16)],
        operands=(slopes, sink, u3, u3, u3), rider=rider)


def _attn_bwd(u3, d_o, probs, sink_probs, rider=None):
    bsz, s, _ = u3.shape
    n_blk = s // CHUNK

    def body(q_ref, k_ref, v_ref, do_ref, p_ref, ps_ref, dq_ref, dkv_ref, ds_ref,
             kp_ref, vp_ref, dk_acc, dv_acc):
        g = pl.program_id(1)
        _fill_padded(kp_ref, _dup_kv_head(k_ref[...], g), s)
        _fill_padded(vp_ref, _dup_kv_head(v_ref[...], g), s)
        dk_acc[...] = jnp.zeros_like(dk_acc)
        dv_acc[...] = jnp.zeros_like(dv_acc)
        m0 = lax.broadcasted_iota(jnp.int32, (CHUNK, LANES), 1) < HEAD_DIM

        def blk(n, dsink):
            r0 = pl.multiple_of(n * CHUNK, CHUNK)
            win = pl.ds(r0, 3 * CHUNK)
            kw = kp_ref[win, :]
            vw = vp_ref[win, :]
            q_all = _stack_heads(q_ref[pl.ds(r0, CHUNK), :].astype(F32) * 0.125, m0)
            do_all = _stack_heads(do_ref[pl.ds(r0, CHUNK), :].astype(F32), m0)
            p_all = p_ref[n]
            ps_all = ps_ref[n]
            dp_all = _dot_nt(do_all, vw)
            new_dsink, dscs = [], []
            for i in range(4):
                rows = slice(i * CHUNK, (i + 1) * CHUNK)
                p = p_all[rows].astype(F32)
                dp = dp_all[rows]
                delta = jnp.sum(p * dp, axis=1, keepdims=True)
                dscs.append((p * (dp - delta)).astype(BF16))
                dsh = jnp.sum(ps_all[rows] * delta, axis=0, keepdims=True)
                new_dsink.append(dsink[i] - jnp.broadcast_to(dsh, (1, LANES)))
            dsc_all = jnp.concatenate(dscs, axis=0)
            dq_all = _dot(dsc_all, kw)
            dk_acc[win, :] += _dot_tn(dsc_all, q_all)
            dv_acc[win, :] += _dot_tn(p_all, do_all)
            for pr in range(2):
                dq_ref[pl.ds(r0, CHUNK), pr * LANES:(pr + 1) * LANES] = (
                    _unstack_pair(dq_all, pr, m0) * 0.125).astype(BF16)
            return tuple(new_dsink)

        dsink = lax.fori_loop(0, n_blk, blk, tuple(jnp.zeros((1, LANES), F32) for _ in range(4)))
        dk = dk_acc[CHUNK:CHUNK + s, :]
        dv = dv_acc[CHUNK:CHUNK + s, :]
        lane = lax.broadcasted_iota(jnp.int32, (s, LANES), 1)
        fold = lambda a: a + pltpu.roll(a, HEAD_DIM, 1)
        dkv_ref[...] = jnp.where(lane < HEAD_DIM, fold(dk), fold(dv)).astype(BF16)
        ds_ref[...] = jnp.zeros_like(ds_ref)
        for i in range(4):
            ds_ref[i:i + 1, :] = dsink[i]

    q, k, v, grp, _ = _attn_specs(s)
    return _hosted_call(
        body, "attn_bwd", (bsz, 2),
        in_specs=[q, k, v, grp] + _attn_saved_specs(bsz, n_blk)[0],
        out_specs=[grp, pl.BlockSpec((None, s, LANES), lambda b, g: (b, 0, g)),
                   pl.BlockSpec((None, None, 8, LANES), lambda b, g: (b, g, 0, 0))],
        out_shape=[jax.ShapeDtypeStruct((bsz, s, ATTN_WIDTH), BF16), jax.ShapeDtypeStruct((bsz, s, 2 * LANES), BF16),
                   jax.ShapeDtypeStruct((bsz, 2, 8, LANES), F32)],
        scratch_shapes=[pltpu.VMEM((s + 2 * CHUNK, LANES), BF16), pltpu.VMEM((s + 2 * CHUNK, LANES), BF16),
                        pltpu.VMEM((s + 2 * CHUNK, LANES), F32), pltpu.VMEM((s + 2 * CHUNK, LANES), F32)],
        operands=(u3, u3, u3, d_o, probs, sink_probs), rider=rider)


def _ffn_bwd(dz2, gs, us, pg, ple, zh1, r1, g1, wg4, wu4, wd4, wpg, w_out):
    t = dz2.shape[0]
    tm = 256
    wg_t, wu_t, wd_all = (w.reshape(FFN, D_MODEL) for w in (wg4, wu4, wd4))

    def body(dz_ref, gs_ref, us_ref, pg_ref, ple_ref, zh_ref, r_ref, g1_ref,
             wg_hbm, wu_hbm, wd_hbm, wpg_hbm, wo_hbm,
             dgs_ref, dus_ref, dsp_ref, dple_ref, dz1_ref, dyr_ref, dya_ref, dg1_ref, db1_ref,
             wg, wu, wd, wpg, wo, wsem):
        step = pl.program_id(0)
        loads = _resident_quarters(wd_hbm, wd) + _resident_quarters(wg_hbm, wg) + _resident_quarters(wu_hbm, wu)
        _load_resident(step, loads + [(wpg_hbm, wpg), (wo_hbm, wo)], wsem)

        @pl.when(step == 0)
        def _():
            dg1_ref[...] = jnp.zeros_like(dg1_ref)
            db1_ref[...] = jnp.zeros_like(db1_ref)

        dz = dz_ref[...]
        dzb = dz.astype(BF16)
        dh = ALPHA * dz
        pending = []
        chunks = [slice(n * FFN_CHUNK, (n + 1) * FFN_CHUNK) for n in range(N_FFN_CHUNK)]
        for n in range(N_FFN_CHUNK + 1):
            if n < N_FFN_CHUNK:
                da = _dot_nt(dzb, wd[chunks[n], :])
                gj = gs_ref[:, chunks[n]].astype(F32)
                uj = us_ref[:, chunks[n]].astype(F32)
                sg = _sigmoid(gj)
                dgj = (da * uj * sg * (1.0 + gj * (1.0 - sg))).astype(BF16)
                duj = (da * gj * sg).astype(BF16)
                dgs_ref[:, chunks[n]] = dgj
                dus_ref[:, chunks[n]] = duj
                pending.append((dgj, duj))
            if n > 0:
                dgp, dup = pending[n - 1]
                dh = dh + _dot(dgp, wg[chunks[n - 1], :]) + _dot(dup, wu[chunks[n - 1], :])
        pgv = pg_ref[...].astype(F32)
        plev = ple_ref[...].astype(F32)
        dple_ref[...] = (dz * pgv).astype(BF16)
        dsp = (dz * plev * pgv * (1.0 - pgv)).astype(BF16)
        dsp_ref[...] = dsp
        dh = dh + _dot_nt(dsp, wpg[...])
        zh = zh_ref[...]
        dg1_ref[...] += jnp.sum(dh * zh, axis=0, keepdims=True)
        db1_ref[...] += jnp.sum(dh, axis=0, keepdims=True)
        dzh = dh * g1_ref[...]
        m1 = jnp.mean(dzh, axis=1, keepdims=True)
        m2 = jnp.mean(dzh * zh, axis=1, keepdims=True)
        dz1 = r_ref[...] * (dzh - m1 - zh * m2)
        dz1_ref[...] = dz1
        dyc = _dot_nt(dz1.astype(BF16), wo[...])
        dyr_ref[...] = dyc[:, 0:RET_WIDTH].astype(BF16)
        dya_ref[...] = dyc[:, RET_WIDTH:].astype(BF16)

    row = lambda w: pl.BlockSpec((tm, w), lambda i: (i, 0))
    const = lambda s: pl.BlockSpec(s, lambda i: (0, 0))
    hbm = pl.BlockSpec(memory_space=pl.ANY)
    hid_shape = jax.ShapeDtypeStruct((t, FFN), BF16)
    return pl.pallas_call(
        body, name="ffn_bwd", grid=(t // tm,),
        in_specs=[row(D_MODEL), row(FFN), row(FFN), row(D_MODEL), row(D_MODEL), row(D_MODEL), row(1),
                  const((1, D_MODEL)), hbm, hbm, hbm, hbm, hbm],
        out_specs=[row(FFN), row(FFN), row(D_MODEL), row(D_MODEL), row(D_MODEL), row(RET_WIDTH), row(ATTN_WIDTH),
                   const((1, D_MODEL)), const((1, D_MODEL))],
        out_shape=[hid_shape, hid_shape, jax.ShapeDtypeStruct((t, D_MODEL), BF16),
                   jax.ShapeDtypeStruct((t, D_MODEL), BF16), jax.ShapeDtypeStruct((t, D_MODEL), F32),
                   jax.ShapeDtypeStruct((t, RET_WIDTH), BF16), jax.ShapeDtypeStruct((t, ATTN_WIDTH), BF16),
                   jax.ShapeDtypeStruct((1, D_MODEL), F32), jax.ShapeDtypeStruct((1, D_MODEL), F32)],
        scratch_shapes=[pltpu.VMEM((FFN, D_MODEL), BF16), pltpu.VMEM((FFN, D_MODEL), BF16),
                        pltpu.VMEM((FFN, D_MODEL), BF16),
                        pltpu.VMEM(wpg.shape, BF16), pltpu.VMEM(w_out.shape, BF16),
                        pltpu.SemaphoreType.DMA((3 * N_SHARD + 2,))],
        compiler_params=_params("arbitrary", vmem=VMEM_LIMIT),
    )(dz2, gs, us, pg, ple, zh1, r1, g1, wg_t, wu_t, wd_all, wpg, w_out)


def _wgrad_misc(y_ret, y_att, dz1, hb, dsp, p2d, dple, rider=None):
    t = dz1.shape[0]
    tk = min(t, 512)

    def body(yr_ref, ya_ref, dz_ref, hb_ref, dsp_ref, p_ref, dple_ref, wo_ref, wpg_ref, wpe_ref):
        @pl.when(pl.program_id(0) == 0)
        def _():
            wo_ref[...] = jnp.zeros_like(wo_ref)
            wpg_ref[...] = jnp.zeros_like(wpg_ref)
            wpe_ref[...] = jnp.zeros_like(wpe_ref)

        dzb = dz_ref[...].astype(BF16)
        wo_ref[0:RET_WIDTH, :] += _dot_tn(yr_ref[...], dzb)
        wo_ref[RET_WIDTH:, :] += _dot_tn(ya_ref[...], dzb)
        wpg_ref[...] += _dot_tn(hb_ref[...], dsp_ref[...])
        wpe_ref[...] += _dot_tn(p_ref[...].astype(BF16), dple_ref[...])

    row = lambda w: pl.BlockSpec((tk, w), lambda k: (k, 0))
    const = lambda s: pl.BlockSpec(s, lambda k: (0, 0))
    return _hosted_call(
        body, "wgrad_misc", (t // tk,),
        in_specs=[row(RET_WIDTH), row(ATTN_WIDTH), row(D_MODEL), row(D_MODEL), row(D_MODEL), row(PLE_DIM),
                  row(D_MODEL)],
        out_specs=[const((D_MODEL, D_MODEL)), const((D_MODEL, D_MODEL)), const((PLE_DIM, D_MODEL))],
        out_shape=[jax.ShapeDtypeStruct((D_MODEL, D_MODEL), F32), jax.ShapeDtypeStruct((D_MODEL, D_MODEL), F32),
                   jax.ShapeDtypeStruct((PLE_DIM, D_MODEL), F32)],
        scratch_shapes=[], operands=(y_ret, y_att, dz1, hb, dsp, p2d, dple), rider=rider, semantics=["arbitrary"])


def _wgrad_ffn(acts, dgs, dus, hb, dz2b):
    t = dz2b.shape[0]
    tk = min(t, 512)
    nk = t // tk

    def body(act_ref, dg_ref, du_ref, hb_ref, dz_ref, og_ref, ou_ref, od_ref):
        @pl.when(pl.program_id(1) == 0)
        def _():
            og_ref[...] = jnp.zeros_like(og_ref)
            ou_ref[...] = jnp.zeros_like(ou_ref)
            od_ref[...] = jnp.zeros_like(od_ref)

        hbv = hb_ref[...]
        og_ref[...] += _dot_tn(dg_ref[...], hbv)
        ou_ref[...] += _dot_tn(du_ref[...], hbv)
        od_ref[...] += _dot_tn(act_ref[...], dz_ref[...])

    half = FFN // 2
    a_spec = pl.BlockSpec((tk, half), lambda j, k: (k, j))
    b_spec = pl.BlockSpec((tk, D_MODEL), lambda j, k: (k, 0))
    o_spec = pl.BlockSpec((half, D_MODEL), lambda j, k: (j, 0))
    o_shape = jax.ShapeDtypeStruct((FFN, D_MODEL), F32)
    outs = pl.pallas_call(
        body, name="wgrad_ffn", grid=(2, nk),
        in_specs=[a_spec, a_spec, a_spec, b_spec, b_spec],
        out_specs=[o_spec] * 3, out_shape=[o_shape] * 3,
        compiler_params=_params("parallel", "arbitrary", vmem=VMEM_LIMIT),
    )(acts, dgs, dus, hb, dz2b)
    return [o.reshape(N_SHARD, FFN_SHARD, D_MODEL) for o in outs]


KV_ORDER = (0, 128, 64, 192)


def _wgrad_in(pieces, x2d):
    t = x2d.shape[0]
    tk = min(t, 512)
    nk = t // tk
    kv0 = CB_AK * LANES

    def body(p0, p1, p2, p3, p4, pkv, x_ref, o_ref):
        @pl.when(pl.program_id(0) == 0)
        def _():
            o_ref[...] = jnp.zeros_like(o_ref)

        xb = x_ref[...].astype(BF16)
        for i, ref in enumerate((p0, p1, p2, p3, p4)):
            o_ref[i * 512:(i + 1) * 512, :] += _dot_tn(ref[...], xb)
        dkv = _dot_tn(pkv[...], xb)
        for i, o in enumerate(KV_ORDER):
            o_ref[kv0 + o:kv0 + o + HEAD_DIM, :] += dkv[i * HEAD_DIM:(i + 1) * HEAD_DIM]

    row = lambda w: pl.BlockSpec((tk, w), lambda k: (k, 0))
    return pl.pallas_call(
        body, name="wgrad_in", grid=(nk,),
        in_specs=[row(512)] * 5 + [row(256), row(D_MODEL)],
        out_specs=pl.BlockSpec((IN_WIDTH, D_MODEL), lambda k: (0, 0)),
        out_shape=jax.ShapeDtypeStruct((IN_WIDTH, D_MODEL), F32),
        compiler_params=_params("arbitrary", vmem=VMEM_LIMIT),
    )(*pieces, x2d)


def _inproj_bwd(dz1, pieces, w_main, w_kv, rider=None):
    t = dz1.shape[0]
    tm = 512

    def body(dz_ref, p0, p1, p2, p3, p4, pkv, wm_ref, wkv_ref, o_ref):
        acc = ALPHA * dz_ref[...]
        for i, ref in enumerate((p0, p1, p2, p3, p4)):
            acc = acc + _dot(ref[...], wm_ref[i * 512:(i + 1) * 512, :])
        o_ref[...] = acc + _dot(pkv[...], wkv_ref[...])

    row = lambda w: pl.BlockSpec((tm, w), lambda i: (i, 0))
    const = lambda s: pl.BlockSpec(s, lambda i: (0, 0))
    return _hosted_call(
        body, "inproj_bwd", (t // tm,),
        in_specs=[row(D_MODEL)] + [row(512)] * 5 + [row(256), const(w_main.shape), const(w_kv.shape)],
        out_specs=[row(D_MODEL)],
        out_shape=[jax.ShapeDtypeStruct((t, D_MODEL), F32)],
        scratch_shapes=[], operands=(dz1, *pieces, w_main, w_kv), rider=rider)


def _coords():
    return lax.axis_index("x"), lax.axis_index("y"), lax.axis_index("c")


def _chip_of(x, y, rel):
    return (1 - x if rel & 2 else x), (1 - y if rel & 1 else y)


def _all_gather_weights(shards):
    first = _gather_chips_rider(shards)
    second = _gather_pass_rider([jax.ShapeDtypeStruct((N_SHARD,) + s.shape, s.dtype) for s in shards], chained=True)
    return _run_riders("gather_weights", shards, first.out_shapes, [first, second])


def _run_riders(name, ins, out_shapes, riders):
    n_in, n_out = len(ins), len(out_shapes)

    def body(*refs):
        in_refs, out_refs = refs[:n_in], refs[n_in:n_in + n_out]
        k = n_in + n_out
        for r in riders:
            sems = refs[k:k + len(r.sems)]
            k += len(r.sems)
            r.start(in_refs, out_refs, sems)
            r.finish(in_refs, out_refs, sems)

    hbm = pl.BlockSpec(memory_space=pl.ANY)
    return pl.pallas_call(
        body, name=name, in_specs=[hbm] * n_in, out_specs=[hbm] * n_out, out_shape=list(out_shapes),
        scratch_shapes=[s for r in riders for s in r.sems],
    )(*ins)


def _gather_half(outs, w, chip, cc):
    h = outs[w].shape[1] // 2
    return outs[w].at[chip, pl.ds(cc * h, h), :]


def _gather_chips_rider(shards):
    nw = len(shards)

    def copies(ins, outs, sems, arrivals):
        send, recv, lsend, lrecv = sems
        x, y, c = _coords()
        me = 2 * x + y
        own = [pltpu.make_async_remote_copy(
            src_ref=ins[w], dst_ref=outs[w].at[me], send_sem=lsend.at[w], recv_sem=lrecv.at[w],
            device_id=(x, y, 1 - c), device_id_type=MESH) for w in range(nw)]
        out, arrive = [], []
        for rel in (1, 2, 3):
            kx, ky = _chip_of(x, y, rel)
            for w in range(nw):
                h = shards[w].shape[0] // 2
                sem = dict(send_sem=send.at[w * 3 + rel - 1], recv_sem=recv.at[w * 3 + rel - 1],
                           device_id=(kx, ky, c), device_id_type=MESH)
                out.append(pltpu.make_async_remote_copy(
                    src_ref=ins[w].at[pl.ds(c * h, h), :], dst_ref=_gather_half(outs, w, me, c), **sem))
                if arrivals:
                    theirs = _gather_half(outs, w, 2 * kx + ky, c)
                    arrive.append(pltpu.make_async_remote_copy(src_ref=theirs, dst_ref=theirs, **sem))
        return own, out, arrive

    def start(ins, outs, sems):
        own, out, _ = copies(ins, outs, sems, arrivals=False)
        for cp in own + out:
            cp.start()

    def finish(ins, outs, sems):
        own, out, arrive = copies(ins, outs, sems, arrivals=True)
        for cp in arrive:
            cp.wait_recv()
        for cp in out:
            cp.wait_send()
        for cp in own:
            cp.wait()

    dma = pltpu.SemaphoreType.DMA
    return _Rider(shards, [jax.ShapeDtypeStruct((N_SHARD,) + s.shape, s.dtype) for s in shards],
                  [dma((3 * nw,)), dma((3 * nw,)), dma((nw,)), dma((nw,))], start, finish)


def _gather_pass_rider(gathered, chained=False):
    nw = len(gathered)

    def copies(outs, sems, cc):
        send, recv = sems
        x, y, c = _coords()
        res = []
        for rel in (1, 2, 3):
            kx, ky = _chip_of(x, y, rel)
            for w in range(nw):
                rows = _gather_half(outs, w, 2 * kx + ky, cc)
                res.append(pltpu.make_async_remote_copy(
                    src_ref=rows, dst_ref=rows, send_sem=send.at[w * 3 + rel - 1], recv_sem=recv.at[w * 3 + rel - 1],
                    device_id=(x, y, 1 - c), device_id_type=MESH))
        return res

    def start(ins, outs, sems):
        for cp in copies(outs, sems, lax.axis_index("c")):
            cp.start()

    def finish(ins, outs, sems):
        c = lax.axis_index("c")
        for cp in copies(outs, sems, 1 - c):
            cp.wait_recv()
        for cp in copies(outs, sems, c):
            cp.wait_send()

    dma = pltpu.SemaphoreType.DMA
    shapes = [jax.ShapeDtypeStruct(g.shape, g.dtype) for g in gathered]
    if chained:
        return _Rider([], [], [dma((3 * nw,)), dma((3 * nw,))], start, finish)
    return _Rider(gathered, shapes, [dma((3 * nw,)), dma((3 * nw,))], start, finish,
                  aliases={w: w for w in range(nw)})


def _exchange_halves_rider(parts):
    nw = len(parts)

    def copies(ins, outs, sems):
        send, recv = sems
        x, y, c = _coords()
        res = []
        for w in range(nw):
            h = parts[w].shape[1] // 2
            res.append(pltpu.make_async_remote_copy(
                src_ref=ins[w].at[:, pl.ds((1 - c) * h, h), :], dst_ref=outs[w],
                send_sem=send.at[w], recv_sem=recv.at[w], device_id=(x, y, 1 - c), device_id_type=MESH))
        return res

    def start(ins, outs, sems):
        for cp in copies(ins, outs, sems):
            cp.start()

    def finish(ins, outs, sems):
        for cp in copies(ins, outs, sems):
            cp.wait()

    dma = pltpu.SemaphoreType.DMA
    return _Rider(parts, [jax.ShapeDtypeStruct((N_SHARD, p.shape[1] // 2, p.shape[2]), p.dtype) for p in parts],
                  [dma((nw,)), dma((nw,))], start, finish)


def _add_halves(parts, theirs, pos):
    nw = len(parts)
    split = 2

    def body(pos_ref, *refs):
        ins, oth = refs[:nw], refs[nw:2 * nw]
        o32, o16 = refs[2 * nw:3 * nw], refs[3 * nw:]
        sums = [ins[w][...] + oth[w][...].astype(F32) for w in range(nw)]
        for w in range(nw):
            o16[w][...] = sums[w].astype(BF16)

        @pl.when(pl.program_id(1) == pos_ref[0])
        def _():
            for w in range(nw):
                o32[w][...] = sums[w]

    in_specs, oth_specs, o32_specs, shapes32, shapes16 = [], [], [], [], []
    for p in parts:
        hb = p.shape[1] // 2 // split
        blk = (None, hb, p.shape[2])
        in_specs.append(pl.BlockSpec(blk, lambda i, j, pos_ref: (j, pos_ref[1] * split + i, 0)))
        oth_specs.append(pl.BlockSpec(blk, lambda i, j, pos_ref: (j, i, 0)))
        o32_specs.append(pl.BlockSpec((hb, p.shape[2]), lambda i, j, pos_ref: (i, 0)))
        shapes32.append(jax.ShapeDtypeStruct((p.shape[1] // 2, p.shape[2]), F32))
        shapes16.append(jax.ShapeDtypeStruct((N_SHARD, p.shape[1] // 2, p.shape[2]), BF16))
    return pl.pallas_call(
        body, name="add_halves",
        grid_spec=pltpu.PrefetchScalarGridSpec(
            num_scalar_prefetch=1, grid=(split, N_SHARD),
            in_specs=in_specs + oth_specs, out_specs=o32_specs + oth_specs),
        out_shape=shapes32 + shapes16,
        compiler_params=_params("parallel", "arbitrary", vmem=VMEM_LIMIT),
    )(pos, *parts, *theirs)


def _exchange_chips_rider(sums16):
    nw = len(sums16)

    def copies(ins, outs, sems):
        send, recv = sems
        x, y, c = _coords()
        res = []
        for rel in (1, 2, 3):
            kx, ky = _chip_of(x, y, rel)
            for w in range(nw):
                res.append(pltpu.make_async_remote_copy(
                    src_ref=ins[w].at[2 * kx + ky], dst_ref=outs[w].at[rel - 1],
                    send_sem=send.at[w * 3 + rel - 1], recv_sem=recv.at[w * 3 + rel - 1],
                    device_id=(kx, ky, c), device_id_type=MESH))
        return res

    def start(ins, outs, sems):
        for cp in copies(ins, outs, sems):
            cp.start()

    def finish(ins, outs, sems):
        for cp in copies(ins, outs, sems):
            cp.wait()

    dma = pltpu.SemaphoreType.DMA
    return _Rider(sums16, [jax.ShapeDtypeStruct((3,) + s.shape[1:], BF16) for s in sums16],
                  [dma((3 * nw,)), dma((3 * nw,))], start, finish)


def _add_chips(sums32, theirs, pos):
    nw = len(sums32)
    split = 2

    def body(pos_ref, *refs):
        ins, oth, outs = refs[:nw], refs[nw:2 * nw], refs[2 * nw:]
        for w in range(nw):
            acc = ins[w][...]
            for r in range(3):
                acc = acc + oth[w][r].astype(F32)
            outs[w][...] = acc

    in_specs, oth_specs, out_specs, shapes = [], [], [], []
    for s in sums32:
        hb = s.shape[0] // split
        in_specs.append(pl.BlockSpec((hb, s.shape[1]), lambda i, pos_ref: (i, 0)))
        oth_specs.append(pl.BlockSpec((3, hb, s.shape[1]), lambda i, pos_ref: (0, i, 0)))
        out_specs.append(pl.BlockSpec((hb, s.shape[1]), lambda i, pos_ref: (pos_ref[1] * split + i, 0)))
        shapes.append(jax.ShapeDtypeStruct((2 * s.shape[0], s.shape[1]), F32))
    return pl.pallas_call(
        body, name="add_chips",
        grid_spec=pltpu.PrefetchScalarGridSpec(
            num_scalar_prefetch=1, grid=(split,), in_specs=in_specs + oth_specs, out_specs=out_specs),
        out_shape=shapes,
        compiler_params=_params("parallel", vmem=VMEM_LIMIT),
    )(pos, *sums32, *theirs)


def _join_halves(shards):
    nw = len(shards)

    def body(*refs):
        outs = refs[nw:2 * nw]
        send, recv = refs[2 * nw:]
        x, y, c = _coords()

        def copy(w, cc):
            h = shards[w].shape[0] // 2
            rows = outs[w].at[pl.ds(cc * h, h), :]
            return pltpu.make_async_remote_copy(
                src_ref=rows, dst_ref=rows, send_sem=send.at[w], recv_sem=recv.at[w],
                device_id=(x, y, 1 - c), device_id_type=MESH)

        for w in range(nw):
            copy(w, c).start()
        for w in range(nw):
            copy(w, 1 - c).wait_recv()
            copy(w, c).wait_send()

    hbm = pl.BlockSpec(memory_space=pl.ANY)
    return pl.pallas_call(
        body, name="join_halves",
        in_specs=[hbm] * nw, out_specs=[hbm] * nw,
        out_shape=[jax.ShapeDtypeStruct(s.shape, F32) for s in shards],
        input_output_aliases={w: w for w in range(nw)},
        scratch_shapes=[pltpu.SemaphoreType.DMA((nw,)), pltpu.SemaphoreType.DMA((nw,))],
    )(*shards)


def _adamw_math(w, g, m, v):
    m = ADAM_B1 * m + (1.0 - ADAM_B1) * g
    v = ADAM_B2 * v + (1.0 - ADAM_B2) * (g * g)
    m_hat = m / (1.0 - ADAM_B1 ** ADAM_STEP)
    v_hat = v / (1.0 - ADAM_B2 ** ADAM_STEP)
    delta = -ADAM_LR * (m_hat / (jnp.sqrt(v_hat) + ADAM_EPS) + ADAM_WD * w)
    return delta, m, v


def _adamw(ws, gs, ms, vs):
    nw = len(ws)
    split = 8

    def body(*refs):
        w_r, g_r, m_r, v_r = (refs[i * nw:(i + 1) * nw] for i in range(4))
        g_o, d_o, m_o, v_o = (refs[(4 + i) * nw:(5 + i) * nw] for i in range(4))
        for k in range(nw):
            g = g_r[k][...]
            d, m, v = _adamw_math(w_r[k][...], g, m_r[k][...], v_r[k][...])
            g_o[k][...] = g
            d_o[k][...] = d
            m_o[k][...] = m
            v_o[k][...] = v

    specs = [pl.BlockSpec((w.shape[0] // split, w.shape[1]), lambda i: (i, 0)) for w in ws]
    shapes = [jax.ShapeDtypeStruct(w.shape, F32) for w in ws]
    outs = pl.pallas_call(
        body, name="adamw", grid=(split,),
        in_specs=specs * 4, out_specs=specs * 4, out_shape=shapes * 4,
        compiler_params=_params("parallel", vmem=VMEM_LIMIT),
    )(*ws, *gs, *ms, *vs)
    return outs[:nw], outs[nw:2 * nw], outs[2 * nw:3 * nw], outs[3 * nw:]


SMALL_ROWS = 8
SMALL_COLS = D_MODEL
LOSS_COL = RET_WIDTH + 24


def _small_allreduce_adamw(part, w, m, v, rider=None):
    def body(part_ref, w_ref, m_ref, v_ref, g_out, d_out, m_out, v_out, all_ref, send, recv):
        x, y, c = _coords()
        me = 4 * x + 2 * y + c
        all_ref[me] = part_ref[...]
        copies = []
        for rel in range(1, 8):
            px = 1 - x if rel & 4 else x
            py = 1 - y if rel & 2 else y
            pc = 1 - c if rel & 1 else c
            copies.append(pltpu.make_async_remote_copy(
                src_ref=part_ref, dst_ref=all_ref.at[me],
                send_sem=send.at[rel - 1], recv_sem=recv.at[rel - 1], device_id=(px, py, pc), device_id_type=MESH))
        for cp in copies:
            cp.start()
        for cp in copies:
            cp.wait()
        g = all_ref[0]
        for k in range(1, 8):
            g = g + all_ref[k]
        d, mn, vn = _adamw_math(w_ref[...], g, m_ref[...], v_ref[...])
        g_out[...] = g
        d_out[...] = d
        m_out[...] = mn
        v_out[...] = vn

    vm = pl.BlockSpec(memory_space=pltpu.VMEM)
    shape = jax.ShapeDtypeStruct((SMALL_ROWS, SMALL_COLS), F32)
    return _hosted_call(
        body, "small_allreduce_adamw", (1,),
        in_specs=[vm] * 4, out_specs=[vm] * 4, out_shape=[shape] * 4,
        scratch_shapes=[pltpu.VMEM((8, SMALL_ROWS, SMALL_COLS), F32),
                        pltpu.SemaphoreType.DMA((7,)), pltpu.SemaphoreType.DMA((7,))],
        operands=(part, w, m, v), rider=rider, semantics=["arbitrary"])


SMALL_NAMES = ("ret_decay_fwd", "ret_decay_bwd", "attn_sink", "ret_gn_gain",
               "ln1_gain", "ln1_bias", "ln2_gain", "ln2_bias")


LN_NAMES = ("ln1_gain", "ln1_bias", "ln2_gain", "ln2_bias")


def _pack_small(vals, extra=None):
    tail = jnp.zeros((1, 1), F32) if extra is None else extra.reshape(1, 1)
    row4 = jnp.concatenate([vals["ret_gn_gain"], vals["ret_decay_fwd"], vals["ret_decay_bwd"], vals["attn_sink"],
                            tail, jnp.zeros((1, SMALL_COLS - LOSS_COL - 1), F32)], axis=1)
    rows = [vals[n] for n in LN_NAMES] + [row4, jnp.zeros((SMALL_ROWS - 5, SMALL_COLS), F32)]
    return jnp.concatenate(rows, axis=0)


def _unpack_small(packed):
    out = {n: packed[i:i + 1] for i, n in enumerate(LN_NAMES)}
    o = RET_WIDTH
    out.update(ret_gn_gain=packed[4:5, 0:o], ret_decay_fwd=packed[4:5, o:o + 8],
               ret_decay_bwd=packed[4:5, o + 8:o + 16], attn_sink=packed[4:5, o + 16:o + 24])
    return out


def _local_step(x, p, tgt, w_in_t, rest, small, pos=None, small_state=None):
    bsz, s, _ = x.shape
    t = bsz * s
    x2d = x.reshape(t, D_MODEL)
    p2d = p.reshape(t, PLE_DIM)
    tgt2d = tgt.reshape(t, D_MODEL)
    dec_f = small["ret_decay_fwd"].reshape(8)
    dec_b = small["ret_decay_bwd"].reshape(8)
    lg_f = jnp.log1p(-jnp.exp2(dec_f))
    lg_b = jnp.log1p(-jnp.exp2(dec_b))
    per_lane = lambda v: jnp.repeat(v, HEAD_DIM).reshape(4, 1, LANES)
    lgf_l, lgb_l = per_lane(lg_f), per_lane(lg_b)
    sink = small["attn_sink"].reshape(8)
    slopes = 2.0 ** (-(jnp.arange(8, dtype=F32) + 1.0))
    gn_gain = small["ret_gn_gain"]
    g1, b1, g2, b2 = (small[n] for n in ("ln1_gain", "ln1_bias", "ln2_gain", "ln2_bias"))

    dist = pos is not None
    if dist:
        shard = dict(zip(REST_NAMES, rest))
        half = FFN_SHARD // 2
        shard["up_lo"], shard["up_hi"] = shard["w_ffn_up"][:half], shard["w_ffn_up"][half:]
    chips = lambda names: _gather_chips_rider([shard[n] for n in names])
    first, second, third = ("w_out", "w_ple_gate"), ("w_ffn_gate", "w_ple_proj", "up_lo"), ("up_hi", "w_ffn_down")
    u, *c1 = _inproj(x2d, w_in_t, rider=chips(first) if dist else None)
    u3 = u.reshape(bsz, s, IN_WIDTH)
    y_hat, y_rstd, y_ret, ret_rb, ret_kvf, *o2 = _ret_fwd(u3, lgf_l, lgb_l, gn_gain,
                                 rider=_merge_riders([_gather_pass_rider(c1), chips(second)]) if dist else None)
    y_att, att_p, att_ps, *o3 = _attn_fwd(u3, slopes, sink, rider=_merge_riders(
        [_gather_pass_rider(o2[len(first):]), chips(third)]) if dist else None)
    gathered = dict(zip(first, o2[:len(first)]))
    gathered.update(zip(second, o3[:len(second)]))
    w_out = _assemble_weights({"w_out": gathered["w_out"]})["w_out"] if dist else rest["w_out"]
    zh1, r1, hb, *o4 = _outproj_ln1(y_ret.reshape(t, RET_WIDTH), y_att.reshape(t, ATTN_WIDTH), x2d, w_out, g1, b1,
                                    rider=_gather_pass_rider(o3[len(second):]) if dist else None)
    gathered.update(zip(third, o4))
    if dist:
        gathered["w_ffn_up"] = jnp.concatenate([gathered.pop("up_lo"), gathered.pop("up_hi")], axis=1)
    wts = _assemble_weights(gathered) if dist else rest
    dz2, dz2b, gs, us, acts, pg, ple, sq, dg2, db2 = _ffn_fwd(
        zh1, hb, p2d, tgt2d, g1, b1, g2, b2, wts["gate4"], wts["up4"], wts["down4"], wts["ple_proj"], wts["ple_gate"])
    dgs, dus, dsp, dple, dz1, dyr, dya, dg1, db1 = _ffn_bwd(dz2, gs, us, pg, ple, zh1, r1, g1, wts["gate4"],
                                                          wts["up4"], wts["down4"], wts["ple_gate"], wts["w_out"])
    ffn_parts = list(_wgrad_ffn(acts, dgs, dus, hb, dz2b))
    d_w_out, d_ple_gate, d_ple_proj, *th_ffn = _wgrad_misc(
        y_ret.reshape(t, RET_WIDTH), y_att.reshape(t, ATTN_WIDTH), dz1, hb, dsp, p2d, dple,
        rider=_exchange_halves_rider(ffn_parts) if dist else None)
    misc_parts = [d_w_out.reshape(N_SHARD, D_MODEL // N_SHARD, D_MODEL),
                  d_ple_proj.reshape(PLE_DIM, N_SHARD, D_MODEL // N_SHARD).transpose(1, 0, 2),
                  d_ple_gate.reshape(N_SHARD, D_MODEL // N_SHARD, D_MODEL)]
    dyr3, dya3 = dyr.reshape(bsz, s, RET_WIDTH), dya.reshape(bsz, s, ATTN_WIDTH)
    if dist:
        s_ffn = _add_halves(ffn_parts, th_ffn, pos)
        drq, drk, drv, drg, rpart, *o5 = _ret_bwd(u3, y_hat, y_rstd, (ret_rb, ret_kvf), dyr3, lgf_l, lgb_l, gn_gain,
                                                  rider=_merge_riders(
            [_exchange_chips_rider(s_ffn[3:5]), _exchange_halves_rider(misc_parts)]))
        s_misc = _add_halves(misc_parts, o5[2:], pos)
        daq, dakv, spart, *o6 = _attn_bwd(u3, dya3, att_p, att_ps,
                                          rider=_exchange_chips_rider([s_ffn[5]] + list(s_misc[3:])))
    else:
        drq, drk, drv, drg, rpart = _ret_bwd(u3, y_hat, y_rstd, (ret_rb, ret_kvf), dyr3, lgf_l, lgb_l, gn_gain)
        daq, dakv, spart = _attn_bwd(u3, dya3, att_p, att_ps)
    pieces = [a.reshape(t, -1) for a in (drq, drk, drv, drg, daq, dakv)]
    kv0 = CB_AK * LANES
    w_kv = jnp.concatenate([w_in_t[kv0 + o:kv0 + o + HEAD_DIM] for o in KV_ORDER], axis=0)
    d_in = _wgrad_in(pieces, x2d).reshape(N_SHARD, FFN_SHARD, D_MODEL)

    rsum = rpart
    lane_heads = lambda row: jnp.sum(row.reshape(4, 2, HEAD_DIM), axis=-1).reshape(8)
    dlg_f = lane_heads(rsum[:, 0, :]) + jnp.stack([jnp.sum(rsum[:, 2, :], -1), jnp.sum(rsum[:, 3, :], -1)], 1).reshape(8)
    dlg_b = lane_heads(rsum[:, 1, :]) + jnp.stack([jnp.sum(rsum[:, 4, :], -1), jnp.sum(rsum[:, 5, :], -1)], 1).reshape(8)
    chain = lambda d: -(math.log(2.0) * jnp.exp2(d)) / (1.0 - jnp.exp2(d))
    grads_small = {
        "ret_decay_fwd": (dlg_f * chain(dec_f)).reshape(1, 8),
        "ret_decay_bwd": (dlg_b * chain(dec_b)).reshape(1, 8),
        "attn_sink": jnp.sum(spart, axis=0)[:, 0:4, 0].reshape(1, 8),
        "ret_gn_gain": rsum[:, 6, :].reshape(1, RET_WIDTH),
        "ln1_gain": dg1, "ln1_bias": db1, "ln2_gain": dg2, "ln2_bias": db2,
    }
    if not dist:
        grad_x, = _inproj_bwd(dz1, pieces, w_in_t[:kv0], w_kv)
        grads_rest = [misc_parts[0]] + ffn_parts + misc_parts[1:]
        return sq[0, 0], grad_x.reshape(bsz, s, D_MODEL), d_in, grads_rest, grads_small
    *small_out, th_in = _small_allreduce_adamw(_pack_small(grads_small, sq[0, 0]), *small_state,
                                               rider=_exchange_halves_rider([d_in]))
    s_in = _add_halves([d_in], [th_in], pos)
    grad_x, chips_in = _inproj_bwd(dz1, pieces, w_in_t[:kv0], w_kv, rider=_exchange_chips_rider([s_in[1]]))
    sums32 = [s_in[0], s_misc[0], s_ffn[0], s_ffn[1], s_ffn[2], s_misc[1], s_misc[2]]
    from_chips = [chips_in, o6[1], o5[0], o5[1], o6[0], o6[2], o6[3]]
    return grad_x.reshape(bsz, s, D_MODEL), sums32, from_chips, small_out


BIG_NAMES = ("w_in", "w_out", "w_ffn_gate", "w_ffn_up", "w_ffn_down", "w_ple_proj", "w_ple_gate")
REST_NAMES = BIG_NAMES[1:]
TRANSPOSED = ("w_in", "w_ffn_gate", "w_ffn_up")
WEIGHT_ORDER = ("w_in", "ret_decay_fwd", "ret_decay_bwd", "ret_gn_gain", "attn_sink", "w_out", "ln1_gain",
                "ln1_bias", "w_ffn_gate", "w_ffn_up", "w_ffn_down", "w_ple_proj", "w_ple_gate", "ln2_gain", "ln2_bias")


def _shard_rows(name, a):
    return jnp.swapaxes(a[0], 0, 1) if name in TRANSPOSED else a[0]


def _unshard_rows(name, a):
    return (jnp.swapaxes(a, 0, 1) if name in TRANSPOSED else a)[None]


def _assemble_weights(gathered):
    cols = lambda a: a.transpose(1, 0, 2).reshape(a.shape[1], N_SHARD * a.shape[2])
    rows = lambda a: a.reshape(N_SHARD * a.shape[1], a.shape[2])
    same = lambda a: a
    layout = {"w_out": ("w_out", rows), "w_ffn_gate": ("gate4", same), "w_ffn_up": ("up4", same),
              "w_ffn_down": ("down4", same), "w_ple_proj": ("ple_proj", cols), "w_ple_gate": ("ple_gate", rows)}
    return {layout[n][0]: layout[n][1](a) for n, a in gathered.items()}


def kernel(x, p, w_in, ret_decay_fwd, ret_decay_bwd, ret_gn_gain, attn_sink, w_out, ln1_gain, ln1_bias, w_ffn_gate, w_ffn_up, w_ffn_down, w_ple_proj, w_ple_gate, ln2_gain, ln2_bias, loss_target, m_w_in, m_ret_decay_fwd, m_ret_decay_bwd, m_ret_gn_gain, m_attn_sink, m_w_out, m_ln1_gain, m_ln1_bias, m_w_ffn_gate, m_w_ffn_up, m_w_ffn_down, m_w_ple_proj, m_w_ple_gate, m_ln2_gain, m_ln2_bias, v_w_in, v_ret_decay_fwd, v_ret_decay_bwd, v_ret_gn_gain, v_attn_sink, v_w_out, v_ln1_gain, v_ln1_bias, v_w_ffn_gate, v_w_ffn_up, v_w_ffn_down, v_w_ple_proj, v_w_ple_gate, v_ln2_gain, v_ln2_bias):
    w = dict(w_in=w_in, ret_decay_fwd=ret_decay_fwd, ret_decay_bwd=ret_decay_bwd, ret_gn_gain=ret_gn_gain,
             attn_sink=attn_sink, w_out=w_out, ln1_gain=ln1_gain, ln1_bias=ln1_bias, w_ffn_gate=w_ffn_gate,
             w_ffn_up=w_ffn_up, w_ffn_down=w_ffn_down, w_ple_proj=w_ple_proj, w_ple_gate=w_ple_gate,
             ln2_gain=ln2_gain, ln2_bias=ln2_bias)
    m = dict(w_in=m_w_in, ret_decay_fwd=m_ret_decay_fwd, ret_decay_bwd=m_ret_decay_bwd, ret_gn_gain=m_ret_gn_gain,
             attn_sink=m_attn_sink, w_out=m_w_out, ln1_gain=m_ln1_gain, ln1_bias=m_ln1_bias, w_ffn_gate=m_w_ffn_gate,
             w_ffn_up=m_w_ffn_up, w_ffn_down=m_w_ffn_down, w_ple_proj=m_w_ple_proj, w_ple_gate=m_w_ple_gate,
             ln2_gain=m_ln2_gain, ln2_bias=m_ln2_bias)
    v = dict(w_in=v_w_in, ret_decay_fwd=v_ret_decay_fwd, ret_decay_bwd=v_ret_decay_bwd, ret_gn_gain=v_ret_gn_gain,
             attn_sink=v_attn_sink, w_out=v_w_out, ln1_gain=v_ln1_gain, ln1_bias=v_ln1_bias, w_ffn_gate=v_w_ffn_gate,
             w_ffn_up=v_w_ffn_up, w_ffn_down=v_w_ffn_down, w_ple_proj=v_w_ple_proj, w_ple_gate=v_w_ple_gate,
             ln2_gain=v_ln2_gain, ln2_bias=v_ln2_bias)
    big = lambda d: [_shard_rows(n, d[n]) for n in BIG_NAMES]
    small = lambda d: {n: d[n] for n in SMALL_NAMES}

    chip = 2 * lax.axis_index("x") + lax.axis_index("y")
    pos = jnp.stack([chip, lax.axis_index("c")]).astype(jnp.int32)

    shards = [a.astype(BF16) for a in big(w)]
    (w_in4,) = _all_gather_weights(shards[:1])
    w_in_t = w_in4.reshape(IN_WIDTH, D_MODEL)
    grad_x, sums32, from_chips, (g_s, d_s, m_s, v_s) = _local_step(
        x, p[0], loss_target, w_in_t, shards[1:], small(w), pos=pos,
        small_state=(_pack_small(small(w)), _pack_small(small(m)), _pack_small(small(v))))
    g_big, d_big, m_big, v_big = _adamw(big(w), _join_halves(_add_chips(sums32, from_chips, pos)), big(m), big(v))
    loss = g_s[4, LOSS_COL] * (0.5 / D_MODEL)

    def tree(bigs, packed):
        out = {n: _unshard_rows(n, a) for n, a in zip(BIG_NAMES, bigs)}
        out.update(_unpack_small(packed))
        return [out[n] for n in WEIGHT_ORDER]

    return (loss, grad_x, *tree(g_big, g_s), *tree(d_big, d_s), *tree(m_big, m_s), *tree(v_big, v_s))
```

```python
import functools
import math

import jax
import jax.numpy as jnp
from jax import lax
from jax.experimental import pallas as pl
from jax.experimental.pallas import tpu as pltpu

F32 = jnp.float32
BF16 = jnp.bfloat16

D_MODEL = 1024
HEAD_DIM = 64
RET_HEADS = 8
ATTN_HEADS = 8
RET_WIDTH = 512
ATTN_WIDTH = 512
KV_WIDTH = 128
IN_WIDTH = 2816
FFN = 2816
N_SHARD = 4
FFN_SHARD = FFN // N_SHARD
PLE_DIM = 256
CHUNK = 128
LANES = 128
ALPHA = 2.0 ** 0.25
LN_EPS = 1e-5
GN_EPS = 1e-5
NEG_INF = -1e30
ADAM_LR = 0.001
ADAM_B1 = 0.9
ADAM_B2 = 0.999
ADAM_EPS = 1e-08
ADAM_WD = 0.01
ADAM_STEP = 10
VMEM_LIMIT = 56 * 1024 * 1024
MESH = pl.DeviceIdType.MESH

CB_RQ, CB_RK, CB_RV, CB_RG, CB_AQ, CB_AK, CB_AV = 0, 4, 8, 12, 16, 20, 21


def _dot(a, b):
    return jnp.dot(a, b, preferred_element_type=F32)


def _dot_nt(a, b):
    return lax.dot_general(a, b, (((1,), (1,)), ((), ())), preferred_element_type=F32)


def _dot_tn(a, b):
    return lax.dot_general(a, b, (((0,), (0,)), ((), ())), preferred_element_type=F32)


def _sigmoid(x):
    return 1.0 / (1.0 + jnp.exp(-x))


def _params(*sem, vmem=None):
    return pltpu.CompilerParams(dimension_semantics=tuple(sem) if sem else None, vmem_limit_bytes=vmem)


class _Rider:
    def __init__(self, ins, out_shapes, sems, start, finish, aliases=None):
        self.ins, self.out_shapes, self.sems = list(ins), list(out_shapes), list(sems)
        self.start, self.finish, self.aliases = start, finish, dict(aliases or {})


def _merge_riders(riders):
    riders = [r for r in riders if r is not None]
    if len(riders) == 1:
        return riders[0]
    bounds, aliases = [], {}
    i0 = o0 = s0 = 0
    for r in riders:
        bounds.append((i0, o0, s0))
        aliases.update({i0 + i: o0 + o for i, o in r.aliases.items()})
        i0, o0, s0 = i0 + len(r.ins), o0 + len(r.out_shapes), s0 + len(r.sems)

    def each(method):
        def run(ins, outs, sems):
            for r, (i, o, s) in zip(riders, bounds):
                getattr(r, method)(ins[i:i + len(r.ins)], outs[o:o + len(r.out_shapes)], sems[s:s + len(r.sems)])
        return run

    return _Rider([a for r in riders for a in r.ins], [a for r in riders for a in r.out_shapes],
                  [a for r in riders for a in r.sems], each("start"), each("finish"), aliases)


def _hosted_call(body, name, grid, in_specs, out_specs, out_shape, scratch_shapes, operands, rider=None,
                 semantics=None):
    n_in, n_out, n_scr = len(in_specs), len(out_specs), len(scratch_shapes)
    if rider is None:
        return pl.pallas_call(
            body, name=name, grid=grid, in_specs=in_specs, out_specs=out_specs, out_shape=out_shape,
            scratch_shapes=scratch_shapes,
            compiler_params=_params(*(semantics or ["parallel"] * len(grid)), vmem=VMEM_LIMIT))(*operands)
    r_in, r_out = len(rider.ins), len(rider.out_shapes)

    def full_body(*refs):
        main_in, rin = refs[:n_in], refs[n_in:n_in + r_in]
        o0 = n_in + r_in
        main_out, rout = refs[o0:o0 + n_out], refs[o0 + n_out:o0 + n_out + r_out]
        s0 = o0 + n_out + r_out
        main_scr, rsem = refs[s0:s0 + n_scr], refs[s0 + n_scr:]
        first = functools.reduce(jnp.logical_and, [pl.program_id(a) == 0 for a in range(len(grid))])
        last = functools.reduce(jnp.logical_and, [pl.program_id(a) == g - 1 for a, g in enumerate(grid)])

        @pl.when(first)
        def _():
            rider.start(rin, rout, rsem)

        body(*main_in, *main_out, *main_scr)

        @pl.when(last)
        def _():
            rider.finish(rin, rout, rsem)

    hbm = pl.BlockSpec(memory_space=pl.ANY)
    return pl.pallas_call(
        full_body, name=name, grid=grid,
        in_specs=list(in_specs) + [hbm] * r_in, out_specs=list(out_specs) + [hbm] * r_out,
        out_shape=list(out_shape) + rider.out_shapes,
        scratch_shapes=list(scratch_shapes) + rider.sems,
        input_output_aliases={n_in + i: n_out + o for i, o in rider.aliases.items()},
        compiler_params=_params(*(["arbitrary"] * len(grid)), vmem=VMEM_LIMIT),
    )(*operands, *rider.ins)


def _loop_pairs(n, body, init):
    if n % 2:
        return lax.fori_loop(0, n, body, init)
    return lax.fori_loop(0, n // 2, lambda i, c: body(2 * i + 1, body(2 * i, c)), init)


def _head_mean(x, m0):
    s0 = jnp.sum(jnp.where(m0, x, 0.0), axis=1, keepdims=True)
    s1 = jnp.sum(jnp.where(m0, 0.0, x), axis=1, keepdims=True)
    return jnp.where(m0, s0, s1) * (1.0 / HEAD_DIM)


def _inproj(x2d, w_in_t, rider=None):
    t = x2d.shape[0]
    tm = 512
    nb = 256

    def body(x_ref, w_ref, o_ref):
        xb = x_ref[...].astype(BF16)
        for n in range(0, IN_WIDTH, nb):
            o_ref[:, n:n + nb] = _dot_nt(xb, w_ref[n:n + nb, :]).astype(BF16)

    return _hosted_call(
        body, "inproj", (t // tm,),
        in_specs=[pl.BlockSpec((tm, D_MODEL), lambda i: (i, 0)),
                  pl.BlockSpec((IN_WIDTH, D_MODEL), lambda i: (0, 0))],
        out_specs=[pl.BlockSpec((tm, IN_WIDTH), lambda i: (i, 0))],
        out_shape=[jax.ShapeDtypeStruct((t, IN_WIDTH), BF16)],
        scratch_shapes=[], operands=(x2d, w_in_t), rider=rider)


def _outproj_ln1(y_ret, y_att, x2d, w_out, gain, bias, rider=None):
    t = x2d.shape[0]
    tm = 512

    def body(yr_ref, ya_ref, x_ref, w_ref, g_ref, b_ref, zh_ref, r_ref, hb_ref):
        mix = _dot(yr_ref[...], w_ref[0:RET_WIDTH, :]) + _dot(ya_ref[...], w_ref[RET_WIDTH:, :])
        z = ALPHA * x_ref[...] + mix
        mu = jnp.mean(z, axis=1, keepdims=True)
        zc = z - mu
        var = jnp.mean(zc * zc, axis=1, keepdims=True)
        r = lax.rsqrt(var + LN_EPS)
        zh = zc * r
        zh_ref[...] = zh
        r_ref[...] = r
        hb_ref[...] = (zh * g_ref[...] + b_ref[...]).astype(BF16)

    row = lambda w: pl.BlockSpec((tm, w), lambda i: (i, 0))
    const = lambda s: pl.BlockSpec(s, lambda i: (0, 0))
    return _hosted_call(
        body, "outproj_ln1", (t // tm,),
        in_specs=[row(RET_WIDTH), row(ATTN_WIDTH), row(D_MODEL), const((D_MODEL, D_MODEL)),
                  const((1, D_MODEL)), const((1, D_MODEL))],
        out_specs=[row(D_MODEL), row(1), row(D_MODEL)],
        out_shape=[jax.ShapeDtypeStruct((t, D_MODEL), F32), jax.ShapeDtypeStruct((t, 1), F32),
                   jax.ShapeDtypeStruct((t, D_MODEL), BF16)],
        scratch_shapes=[], operands=(y_ret, y_att, x2d, w_out, gain, bias), rider=rider)


def _load_resident(step, pairs, sems):
    copies = [pltpu.make_async_copy(src, dst, sems.at[i]) for i, (src, dst) in enumerate(pairs)]

    @pl.when(step == 0)
    def _():
        for cp in copies:
            cp.start()
        for cp in copies:
            cp.wait()


FFN_CHUNK = 256
N_FFN_CHUNK = FFN // FFN_CHUNK


def _resident_quarters(hbm, vmem):
    q = FFN // N_SHARD
    return [(hbm.at[pl.ds(j * q, q), :], vmem.at[pl.ds(j * q, q), :]) for j in range(N_SHARD)]


def _ffn_fwd(zh1, hb, p2d, tgt, g1, b1, g2, b2, wg4, wu4, wd4, wpe, wpg):
    t = zh1.shape[0]
    tm = 256
    wg_t, wu_t, wd_all = (w.reshape(FFN, D_MODEL) for w in (wg4, wu4, wd4))

    def body(zh_ref, hb_ref, p_ref, t_ref, g1_ref, b1_ref, g2_ref, b2_ref,
             wg_hbm, wu_hbm, wd_hbm, wpe_hbm, wpg_hbm,
             dz_ref, dzb_ref, gs_ref, us_ref, act_ref, pg_ref, ple_ref, loss_ref, dg2_ref, db2_ref,
             wg, wu, wd, wpe, wpg, wsem):
        step = pl.program_id(0)
        loads = _resident_quarters(wg_hbm, wg) + _resident_quarters(wu_hbm, wu) + _resident_quarters(wd_hbm, wd)
        _load_resident(step, loads + [(wpe_hbm, wpe), (wpg_hbm, wpg)], wsem)

        @pl.when(step == 0)
        def _():
            loss_ref[...] = jnp.zeros_like(loss_ref)
            dg2_ref[...] = jnp.zeros_like(dg2_ref)
            db2_ref[...] = jnp.zeros_like(db2_ref)

        h1 = zh_ref[...] * g1_ref[...] + b1_ref[...]
        hbv = hb_ref[...]
        ffn = jnp.zeros((tm, D_MODEL), F32)
        acts = []
        chunks = [slice(n * FFN_CHUNK, (n + 1) * FFN_CHUNK) for n in range(N_FFN_CHUNK)]
        for n in range(N_FFN_CHUNK + 1):
            if n < N_FFN_CHUNK:
                gj = _dot_nt(hbv, wg[chunks[n], :])
                uj = _dot_nt(hbv, wu[chunks[n], :])
                gs_ref[:, chunks[n]] = gj.astype(BF16)
                us_ref[:, chunks[n]] = uj.astype(BF16)
                acts.append((gj * _sigmoid(gj) * uj).astype(BF16))
                act_ref[:, chunks[n]] = acts[n]
            if n > 0:
                ffn = ffn + _dot(acts[n - 1], wd[chunks[n - 1], :])
        ple = _dot(p_ref[...].astype(BF16), wpe[...])
        pg = _sigmoid(_dot(hbv, wpg[...]))
        pg_ref[...] = pg.astype(BF16)
        ple_ref[...] = ple.astype(BF16)
        z2 = ALPHA * h1 + ffn + pg * ple
        mu = jnp.mean(z2, axis=1, keepdims=True)
        zc = z2 - mu
        var = jnp.mean(zc * zc, axis=1, keepdims=True)
        r = lax.rsqrt(var + LN_EPS)
        zh2 = zc * r
        err = zh2 * g2_ref[...] + b2_ref[...] - t_ref[...]
        loss_ref[...] += jnp.sum(err * err)
        dy = err * (1.0 / D_MODEL)
        dg2_ref[...] += jnp.sum(dy * zh2, axis=0, keepdims=True)
        db2_ref[...] += jnp.sum(dy, axis=0, keepdims=True)
        dzh = dy * g2_ref[...]
        m1 = jnp.mean(dzh, axis=1, keepdims=True)
        m2 = jnp.mean(dzh * zh2, axis=1, keepdims=True)
        dz2 = r * (dzh - m1 - zh2 * m2)
        dz_ref[...] = dz2
        dzb_ref[...] = dz2.astype(BF16)

    row = lambda w: pl.BlockSpec((tm, w), lambda i: (i, 0))
    const = lambda s: pl.BlockSpec(s, lambda i: (0, 0))
    hid_shape = jax.ShapeDtypeStruct((t, FFN), BF16)
    hbm = pl.BlockSpec(memory_space=pl.ANY)
    return pl.pallas_call(
        body, name="ffn_fwd", grid=(t // tm,),
        in_specs=[row(D_MODEL), row(D_MODEL), row(PLE_DIM), row(D_MODEL),
                  const((1, D_MODEL)), const((1, D_MODEL)), const((1, D_MODEL)), const((1, D_MODEL)),
                  hbm, hbm, hbm, hbm, hbm],
        out_specs=[row(D_MODEL), row(D_MODEL), row(FFN), row(FFN), row(FFN), row(D_MODEL), row(D_MODEL),
                   const((8, LANES)), const((1, D_MODEL)), const((1, D_MODEL))],
        out_shape=[jax.ShapeDtypeStruct((t, D_MODEL), F32), jax.ShapeDtypeStruct((t, D_MODEL), BF16),
                   hid_shape, hid_shape, hid_shape,
                   jax.ShapeDtypeStruct((t, D_MODEL), BF16), jax.ShapeDtypeStruct((t, D_MODEL), BF16),
                   jax.ShapeDtypeStruct((8, LANES), F32),
                   jax.ShapeDtypeStruct((1, D_MODEL), F32), jax.ShapeDtypeStruct((1, D_MODEL), F32)],
        scratch_shapes=[pltpu.VMEM((FFN, D_MODEL), BF16), pltpu.VMEM((FFN, D_MODEL), BF16),
                        pltpu.VMEM((FFN, D_MODEL), BF16),
                        pltpu.VMEM(wpe.shape, BF16), pltpu.VMEM(wpg.shape, BF16),
                        pltpu.SemaphoreType.DMA((3 * N_SHARD + 2,))],
        compiler_params=_params("arbitrary", vmem=VMEM_LIMIT),
    )(zh1, hb, p2d, tgt, g1, b1, g2, b2, wg_t, wu_t, wd_all, wpe, wpg)


def _ret_tables(lgf, lgb):
    c = CHUNK
    row = lax.broadcasted_iota(jnp.int32, (c, LANES), 0).astype(F32)
    ii = lax.broadcasted_iota(jnp.int32, (c, c), 0).astype(F32)
    jj = lax.broadcasted_iota(jnp.int32, (c, c), 1).astype(F32)
    diff = ii - jj
    dmats = []
    for h in range(2):
        lf = lgf[:, h * HEAD_DIM:h * HEAD_DIM + 1]
        lb = lgb[:, h * HEAD_DIM:h * HEAD_DIM + 1]
        dmats.append(jnp.where(diff > 0, jnp.exp(lf * jnp.maximum(diff, 0.0)),
                               jnp.where(diff < 0, jnp.exp(lb * jnp.maximum(-diff, 0.0)), 2.0)))
    tab = dict(
        qdec_f=jnp.exp(lgf * (row + 1.0)), kdec_f=jnp.exp(lgf * (c - 1.0 - row)),
        qdec_b=jnp.exp(lgb * (c - row)), kdec_b=jnp.exp(lgb * row),
        cdec_f=jnp.exp(lgf * c), cdec_b=jnp.exp(lgb * c),
        d0=dmats[0], d1=dmats[1], row=row, diff=diff)
    r = lax.broadcasted_iota(jnp.int32, (LANES, LANES), 0) < HEAD_DIM
    cc = lax.broadcasted_iota(jnp.int32, (LANES, LANES), 1) < HEAD_DIM
    tab["bd"] = r == cc
    tab["m0"] = lax.broadcasted_iota(jnp.int32, (c, LANES), 1) < HEAD_DIM
    return tab


def _ret_specs(bsz, s):
    blk = lambda cb: pl.BlockSpec((bsz, s, LANES), lambda p, cb=cb: (0, 0, cb + p))
    lane = pl.BlockSpec((None, 1, LANES), lambda p: (p, 0, 0))
    gain = pl.BlockSpec((1, LANES), lambda p: (0, p))
    pair = pl.BlockSpec((bsz, s, LANES), lambda p: (0, 0, p))
    return blk, lane, gain, pair


def _ret_state_spec(bsz, n_chunk):
    spec = pl.BlockSpec((None, bsz, n_chunk, LANES, LANES), lambda p: (p, 0, 0, 0, 0))
    return spec, jax.ShapeDtypeStruct((4, bsz, n_chunk, LANES, LANES), F32)


def _ret_kv_states(tb, k_ref, v_ref, rb_ref, kvf_ref, n_chunk):
    c = CHUNK
    bsz = k_ref.shape[0]
    bd = tb["bd"]

    def contributions(n, carry):
        sl = pl.ds(pl.multiple_of(n * c, c), c)
        kfb = []
        for b in range(bsz):
            k32 = k_ref[b, sl, :].astype(F32)
            kfb.append(jnp.concatenate([k32 * tb["kdec_f"], k32 * tb["kdec_b"]], axis=1).astype(BF16))
        kvs = [_dot_tn(kfb[b], v_ref[b, sl, :]) for b in range(bsz)]
        for b in range(bsz):
            kvf_ref[b, n] = jnp.where(bd, kvs[b][0:LANES], 0.0)
            rb_ref[b, n] = jnp.where(bd, kvs[b][LANES:], 0.0)
        return carry

    lax.fori_loop(0, n_chunk, contributions, 0, unroll=2)

    def recur(i, rbs):
        n = n_chunk - 1 - i
        new = []
        for b in range(bsz):
            own = rb_ref[b, n]
            rb_ref[b, n] = rbs[b]
            new.append(rbs[b] * tb["cdec_b"] + own)
        return tuple(new)

    lax.fori_loop(0, n_chunk, recur, tuple(jnp.zeros((LANES, LANES), F32) for _ in range(bsz)))


def _split_rows(x, m0):
    return jnp.concatenate([jnp.where(m0, x, 0.0), jnp.where(m0, 0.0, x)], axis=0).astype(BF16)


def _ret_fwd(u3, lgf_l, lgb_l, gn_gain, rider=None):
    bsz, s, _ = u3.shape
    n_chunk = s // CHUNK
    c = CHUNK

    def body(q_ref, k_ref, v_ref, g_ref, lgf_ref, lgb_ref, gain_ref, yh_ref, rstd_ref, o_ref, rb_ref, kvf_ref):
        tb = _ret_tables(lgf_ref[...], lgb_ref[...])
        m0 = tb["m0"]
        gain = gain_ref[...]
        rows = range(bsz)
        _ret_kv_states(tb, k_ref, v_ref, rb_ref, kvf_ref, n_chunk)

        def chunk(n, rfs):
            sl = pl.ds(pl.multiple_of(n * c, c), c)
            qs = [q_ref[b, sl, :].astype(F32) * 0.125 for b in rows]
            s01 = [_dot_nt(_split_rows(qs[b], m0), k_ref[b, sl, :]) for b in rows]
            ys = []
            for b in rows:
                lhs = jnp.concatenate([s01[b][0:c] * tb["d0"], s01[b][c:] * tb["d1"],
                                       qs[b] * tb["qdec_f"], qs[b] * tb["qdec_b"]], axis=1).astype(BF16)
                rhs = jnp.concatenate([_split_rows(v_ref[b, sl, :].astype(F32), m0),
                                       rfs[b].astype(BF16), rb_ref[b, n].astype(BF16)], axis=0)
                ys.append(_dot(lhs, rhs))
            new = []
            for b in rows:
                y = ys[b]
                mu = _head_mean(y, m0)
                yc = y - mu
                rstd = lax.rsqrt(_head_mean(yc * yc, m0) + GN_EPS)
                yh = yc * rstd
                g = g_ref[b, sl, :].astype(F32)
                yh_ref[b, sl, :] = yh
                rstd_ref[b, sl, :] = rstd
                o_ref[b, sl, :] = (yh * gain * (g * _sigmoid(g))).astype(BF16)
                new.append(rfs[b] * tb["cdec_f"] + kvf_ref[b, n])
            return tuple(new)

        _loop_pairs(n_chunk, chunk, tuple(jnp.zeros((LANES, LANES), F32) for _ in rows))

    blk, lane, gain, pair = _ret_specs(bsz, s)
    state, state_shape = _ret_state_spec(bsz, n_chunk)
    return _hosted_call(
        body, "ret_fwd", (4,),
        in_specs=[blk(CB_RQ), blk(CB_RK), blk(CB_RV), blk(CB_RG), lane, lane, gain],
        out_specs=[pair, pair, pair, state, state],
        out_shape=[jax.ShapeDtypeStruct((bsz, s, RET_WIDTH), F32), jax.ShapeDtypeStruct((bsz, s, RET_WIDTH), F32),
                   jax.ShapeDtypeStruct((bsz, s, RET_WIDTH), BF16), state_shape, state_shape],
        scratch_shapes=[],
        operands=(u3, u3, u3, u3, lgf_l, lgb_l, gn_gain), rider=rider)


def _ret_bwd(u3, y_hat, y_rstd, states, d_o, lgf_l, lgb_l, gn_gain, rider=None):
    bsz, s, _ = u3.shape
    n_chunk = s // CHUNK
    c = CHUNK

    def body(q_ref, k_ref, v_ref, g_ref, yh_ref, rstd_ref, do_ref, lgf_ref, lgb_ref, gain_ref, rb_ref, kvf_ref,
             dq_ref, dk_ref, dv_ref, dg_ref, part_ref,
             rf_ref, dirf_ref, dy_ref, dk_acc, dv_acc, pa0, pa1, vec_ref):
        tb = _ret_tables(lgf_ref[...], lgb_ref[...])
        m0, bd, row = tb["m0"], tb["bd"], tb["row"]
        gain = gain_ref[...]
        wf = jnp.maximum(tb["diff"], 0.0)
        wb = jnp.maximum(-tb["diff"], 0.0)
        rows = range(bsz)
        zero_states = tuple(jnp.zeros((LANES, LANES), F32) for _ in rows)
        for ref in (pa0, pa1):
            ref[...] = jnp.zeros_like(ref)
        vec_ref[...] = jnp.zeros_like(vec_ref)

        def sweep_fwd(n, carry):
            rfs, gbs = carry
            sl = pl.ds(pl.multiple_of(n * c, c), c)
            qs, ks, vs, dys, dybs, q01, k01, dy01 = [], [], [], [], [], [], [], []
            dgain = jnp.zeros((1, LANES), F32)
            for b in rows:
                q = q_ref[b, sl, :].astype(F32) * 0.125
                k = k_ref[b, sl, :]
                yh = yh_ref[b, sl, :]
                rstd = rstd_ref[b, sl, :]
                do = do_ref[b, sl, :].astype(F32)
                g = g_ref[b, sl, :].astype(F32)
                sg = _sigmoid(g)
                sil = g * sg
                dyh = do * gain * sil
                dg_ref[b, sl, :] = (do * yh * gain * sg * (1.0 + g * (1.0 - sg))).astype(BF16)
                dgain = dgain + jnp.sum(do * yh * sil, axis=0, keepdims=True)
                dy = rstd * (dyh - _head_mean(dyh, m0) - yh * _head_mean(dyh * yh, m0))
                dyb = dy.astype(BF16)
                dy_ref[b, sl, :] = dyb
                rf_ref[b, n] = rfs[b]
                qs.append(q)
                ks.append(k)
                vs.append(v_ref[b, sl, :])
                dys.append(dy)
                dybs.append(dyb)
                q01.append(_split_rows(q, m0))
                k01.append(_split_rows(k.astype(F32), m0))
                dy01.append(_split_rows(dy, m0))
            s01 = [_dot_nt(q01[b], ks[b]) for b in rows]
            da01 = [_dot_nt(dy01[b], vs[b]) for b in rows]
            rbn = [rb_ref[b, n] for b in rows]
            states = [jnp.concatenate([rfs[b], rbn[b]], axis=0).astype(BF16) for b in rows]
            dqc = [_dot_nt(dybs[b], states[b]) for b in rows]
            gbb = [gbs[b].astype(BF16) for b in rows]
            dkb = [_dot_nt(vs[b], gbb[b]) for b in rows]
            qfb = [jnp.concatenate([qs[b] * tb["qdec_f"], qs[b] * tb["qdec_b"]], axis=1) for b in rows]
            direct = [_dot_tn(qfb[b].astype(BF16), dybs[b]) for b in rows]
            ds_cat, ds_rows, a_rows = [], [], []
            for b in rows:
                a0 = s01[b][0:c] * tb["d0"]
                a1 = s01[b][c:] * tb["d1"]
                pa0[...] += da01[b][0:c] * a0
                pa1[...] += da01[b][c:] * a1
                ds0 = da01[b][0:c] * tb["d0"]
                ds1 = da01[b][c:] * tb["d1"]
                ds_cat.append(jnp.concatenate([ds0, ds1], axis=1).astype(BF16))
                ds_rows.append(jnp.concatenate([ds0, ds1], axis=0).astype(BF16))
                a_rows.append(jnp.concatenate([a0, a1], axis=0).astype(BF16))
            kbd = [ks[b].astype(F32) * tb["kdec_b"] for b in rows]
            dq_in = [_dot(ds_cat[b], k01[b]) for b in rows]
            dk_in = [_dot_tn(ds_rows[b], q01[b]) for b in rows]
            dv_in = [_dot_tn(a_rows[b], dy01[b]) for b in rows]
            dv_gb = [_dot(kbd[b].astype(BF16), gbb[b]) for b in rows]
            new_rf, new_gb = [], []
            dlf = jnp.zeros((1, LANES), F32)
            dlb = jnp.zeros((1, LANES), F32)
            for b in rows:
                dqf, dqb = dqc[b][:, 0:LANES], dqc[b][:, LANES:]
                qf, qb = qfb[b][:, 0:LANES], qfb[b][:, LANES:]
                dq = dq_in[b] + dqf * tb["qdec_f"] + dqb * tb["qdec_b"]
                dq_ref[b, sl, :] = (dq * 0.125).astype(BF16)
                dk_acc[b, sl, :] = dk_in[b] + dkb[b] * tb["kdec_b"]
                dv_acc[b, sl, :] = dv_in[b] + dv_gb[b]
                dlf = dlf + jnp.sum((row + 1.0) * qf * dqf, axis=0, keepdims=True)
                dlb = dlb + jnp.sum((c - row) * qb * dqb + row * kbd[b] * dkb[b], axis=0, keepdims=True)
                dlb = dlb + c * tb["cdec_b"] * jnp.sum(gbs[b] * rbn[b], axis=0, keepdims=True)
                dirf_ref[b, n] = jnp.where(bd, direct[b][0:LANES], 0.0)
                new_gb.append(jnp.where(bd, direct[b][LANES:], 0.0) + tb["cdec_b"] * gbs[b])
                new_rf.append(rfs[b] * tb["cdec_f"] + kvf_ref[b, n])
            vec_ref[0:1, :] += dlf
            vec_ref[1:2, :] += dlb
            vec_ref[6:7, :] += dgain
            return tuple(new_rf), tuple(new_gb)

        _loop_pairs(n_chunk, sweep_fwd, (zero_states, zero_states))

        def sweep_bwd(i, gfs):
            n = n_chunk - 1 - i
            sl = pl.ds(pl.multiple_of(n * c, c), c)
            gfb = [gfs[b].astype(BF16) for b in rows]
            kfd = [k_ref[b, sl, :].astype(F32) * tb["kdec_f"] for b in rows]
            dkf = [_dot_nt(v_ref[b, sl, :], gfb[b]) for b in rows]
            dvf = [_dot(kfd[b].astype(BF16), gfb[b]) for b in rows]
            new = []
            dlf = jnp.zeros((1, LANES), F32)
            for b in rows:
                dk_ref[b, sl, :] = (dk_acc[b, sl, :] + dkf[b] * tb["kdec_f"]).astype(BF16)
                dv_ref[b, sl, :] = (dv_acc[b, sl, :] + dvf[b]).astype(BF16)
                dlf = dlf + jnp.sum((c - 1.0 - row) * kfd[b] * dkf[b], axis=0, keepdims=True)
                dlf = dlf + c * tb["cdec_f"] * jnp.sum(gfs[b] * rf_ref[b, n], axis=0, keepdims=True)
                new.append(dirf_ref[b, n] + tb["cdec_f"] * gfs[b])
            vec_ref[0:1, :] += dlf
            return tuple(new)

        lax.fori_loop(0, n_chunk, sweep_bwd, zero_states, unroll=2)
        vec_ref[2:3, :] = jnp.sum(pa0[...] * wf, axis=0, keepdims=True)
        vec_ref[3:4, :] = jnp.sum(pa1[...] * wf, axis=0, keepdims=True)
        vec_ref[4:5, :] = jnp.sum(pa0[...] * wb, axis=0, keepdims=True)
        vec_ref[5:6, :] = jnp.sum(pa1[...] * wb, axis=0, keepdims=True)
        part_ref[...] = vec_ref[...]

    blk, lane, gain, pair = _ret_specs(bsz, s)
    out_bf = jax.ShapeDtypeStruct((bsz, s, RET_WIDTH), BF16)
    state = pltpu.VMEM((bsz, n_chunk, LANES, LANES), F32)
    saved = _ret_state_spec(bsz, n_chunk)[0]
    return _hosted_call(
        body, "ret_bwd", (4,),
        in_specs=[blk(CB_RQ), blk(CB_RK), blk(CB_RV), blk(CB_RG), pair, pair, pair, lane, lane, gain, saved, saved],
        out_specs=[pair, pair, pair, pair, pl.BlockSpec((None, 8, LANES), lambda p: (p, 0, 0))],
        out_shape=[out_bf, out_bf, out_bf, out_bf, jax.ShapeDtypeStruct((4, 8, LANES), F32)],
        scratch_shapes=[state, state,
                        pltpu.VMEM((bsz, s, LANES), BF16), pltpu.VMEM((bsz, s, LANES), F32),
                        pltpu.VMEM((bsz, s, LANES), F32),
                        pltpu.VMEM((c, c), F32), pltpu.VMEM((c, c), F32), pltpu.VMEM((8, LANES), F32)],
        operands=(u3, u3, u3, u3, y_hat, y_rstd, d_o, lgf_l, lgb_l, gn_gain, *states), rider=rider)


def _attn_window_tables(n, s):
    qi = lax.broadcasted_iota(jnp.int32, (CHUNK, 3 * CHUNK), 0)
    kj = lax.broadcasted_iota(jnp.int32, (CHUNK, 3 * CHUNK), 1)
    dist = jnp.abs(kj - CHUNK - qi)
    kpos = n * CHUNK - CHUNK + kj
    valid = (dist <= CHUNK) & (kpos >= 0) & (kpos < s)
    return dist.astype(F32), valid


def _dup_kv_head(x, g):
    lane = lax.broadcasted_iota(jnp.int32, x.shape, 1)
    keep = (lane < HEAD_DIM) == (g == 0)
    xf = x.astype(F32)
    return jnp.where(keep, xf, pltpu.roll(xf, HEAD_DIM, 1))


def _attn_specs(s):
    q = pl.BlockSpec((None, s, 2 * LANES), lambda b, g: (b, 0, CB_AQ // 2 + g))
    k = pl.BlockSpec((None, s, LANES), lambda b, g: (b, 0, CB_AK))
    v = pl.BlockSpec((None, s, LANES), lambda b, g: (b, 0, CB_AV))
    grp = pl.BlockSpec((None, s, 2 * LANES), lambda b, g: (b, 0, g))
    smem = pl.BlockSpec(memory_space=pltpu.SMEM)
    return q, k, v, grp, smem


def _fill_padded(dst_ref, val, s):
    dst_ref[0:CHUNK, :] = jnp.zeros((CHUNK, LANES), dst_ref.dtype)
    dst_ref[CHUNK:CHUNK + s, :] = val.astype(dst_ref.dtype)
    dst_ref[CHUNK + s:2 * CHUNK + s, :] = jnp.zeros((CHUNK, LANES), dst_ref.dtype)


def _attn_probs(sc, slope, snk, dist, valid):
    sc = jnp.where(valid, sc - slope * dist, NEG_INF)
    m = jnp.maximum(jnp.max(sc, axis=1, keepdims=True), snk)
    e = jnp.exp(sc - m)
    es = jnp.exp(snk - m)
    inv = 1.0 / (jnp.sum(e, axis=1, keepdims=True) + es)
    return e * inv, es * inv


def _stack_heads(x2, m0):
    parts = []
    for pr in range(2):
        xp = x2[:, pr * LANES:(pr + 1) * LANES]
        parts += [jnp.where(m0, xp, 0.0), jnp.where(m0, 0.0, xp)]
    return jnp.concatenate(parts, axis=0).astype(BF16)


def _unstack_pair(x_all, pr, m0):
    return jnp.where(m0, x_all[(2 * pr) * CHUNK:(2 * pr + 1) * CHUNK], x_all[(2 * pr + 1) * CHUNK:(2 * pr + 2) * CHUNK])


def _attn_saved_specs(bsz, n_blk):
    specs = [pl.BlockSpec((None, None, n_blk, 4 * CHUNK, w), lambda b, g: (b, g, 0, 0, 0)) for w in (3 * CHUNK, 1)]
    shapes = [jax.ShapeDtypeStruct((bsz, 2, n_blk, 4 * CHUNK, 3 * CHUNK), BF16),
              jax.ShapeDtypeStruct((bsz, 2, n_blk, 4 * CHUNK, 1), F32)]
    return specs, shapes


def _attn_fwd(u3, slopes, sink, rider=None):
    bsz, s, _ = u3.shape
    n_blk = s // CHUNK

    def body(slope_ref, sink_ref, q_ref, k_ref, v_ref, o_ref, p_ref, ps_ref, kp_ref, vp_ref):
        g = pl.program_id(1)
        _fill_padded(kp_ref, _dup_kv_head(k_ref[...], g), s)
        _fill_padded(vp_ref, _dup_kv_head(v_ref[...], g), s)
        m0 = lax.broadcasted_iota(jnp.int32, (CHUNK, LANES), 1) < HEAD_DIM

        def blk(n, carry):
            r0 = pl.multiple_of(n * CHUNK, CHUNK)
            kw = kp_ref[pl.ds(r0, 3 * CHUNK), :]
            vw = vp_ref[pl.ds(r0, 3 * CHUNK), :]
            dist, valid = _attn_window_tables(n, s)
            q_all = _stack_heads(q_ref[pl.ds(r0, CHUNK), :].astype(F32) * 0.125, m0)
            sc_all = _dot_nt(q_all, kw)
            probs, sinks = [], []
            for i in range(4):
                p, ps = _attn_probs(sc_all[i * CHUNK:(i + 1) * CHUNK], slope_ref[g * 4 + i], sink_ref[g * 4 + i],
                                    dist, valid)
                probs.append(p.astype(BF16))
                sinks.append(ps)
            p_all = jnp.concatenate(probs, axis=0)
            p_ref[n] = p_all
            ps_ref[n] = jnp.concatenate(sinks, axis=0)
            out_all = _dot(p_all, vw)
            for pr in range(2):
                o_ref[pl.ds(r0, CHUNK), pr * LANES:(pr + 1) * LANES] = _unstack_pair(out_all, pr, m0).astype(BF16)
            return carry

        lax.fori_loop(0, n_blk, blk, 0, unroll=2)

    q, k, v, grp, smem = _attn_specs(s)
    saved_specs, saved_shapes = _attn_saved_specs(bsz, n_blk)
    return _hosted_call(
        body, "attn_fwd", (bsz, 2),
        in_specs=[smem, smem, q, k, v],
        out_specs=[grp] + saved_specs,
        out_shape=[jax.ShapeDtypeStruct((bsz, s, ATTN_WIDTH), BF16)] + saved_shapes,
        scratch_shapes=[pltpu.VMEM((s + 2 * CHUNK, LANES), BF16), pltpu.VMEM((s + 2 * CHUNK, LANES), BF16)],
        operands=(slopes, sink, u3, u3, u3), rider=rider)


def _attn_bwd(u3, d_o, probs, sink_probs, rider=None):
    bsz, s, _ = u3.shape
    n_blk = s // CHUNK

    def body(q_ref, k_ref, v_ref, do_ref, p_ref, ps_ref, dq_ref, dkv_ref, ds_ref,
             kp_ref, vp_ref, dk_acc, dv_acc):
        g = pl.program_id(1)
        _fill_padded(kp_ref, _dup_kv_head(k_ref[...], g), s)
        _fill_padded(vp_ref, _dup_kv_head(v_ref[...], g), s)
        dk_acc[...] = jnp.zeros_like(dk_acc)
        dv_acc[...] = jnp.zeros_like(dv_acc)
        m0 = lax.broadcasted_iota(jnp.int32, (CHUNK, LANES), 1) < HEAD_DIM

        def blk(n, dsink):
            r0 = pl.multiple_of(n * CHUNK, CHUNK)
            win = pl.ds(r0, 3 * CHUNK)
            kw = kp_ref[win, :]
            vw = vp_ref[win, :]
            q_all = _stack_heads(q_ref[pl.ds(r0, CHUNK), :].astype(F32) * 0.125, m0)
            do_all = _stack_heads(do_ref[pl.ds(r0, CHUNK), :].astype(F32), m0)
            p_all = p_ref[n]
            ps_all = ps_ref[n]
            dp_all = _dot_nt(do_all, vw)
            new_dsink, dscs = [], []
            for i in range(4):
                rows = slice(i * CHUNK, (i + 1) * CHUNK)
                p = p_all[rows].astype(F32)
                dp = dp_all[rows]
                delta = jnp.sum(p * dp, axis=1, keepdims=True)
                dscs.append((p * (dp - delta)).astype(BF16))
                dsh = jnp.sum(ps_all[rows] * delta, axis=0, keepdims=True)
                new_dsink.append(dsink[i] - jnp.broadcast_to(dsh, (1, LANES)))
            dsc_all = jnp.concatenate(dscs, axis=0)
            dq_all = _dot(dsc_all, kw)
            dk_acc[win, :] += _dot_tn(dsc_all, q_all)
            dv_acc[win, :] += _dot_tn(p_all, do_all)
            for pr in range(2):
                dq_ref[pl.ds(r0, CHUNK), pr * LANES:(pr + 1) * LANES] = (
                    _unstack_pair(dq_all, pr, m0) * 0.125).astype(BF16)
            return tuple(new_dsink)

        dsink = _loop_pairs(n_blk, blk, tuple(jnp.zeros((1, LANES), F32) for _ in range(4)))
        dk = dk_acc[CHUNK:CHUNK + s, :]
        dv = dv_acc[CHUNK:CHUNK + s, :]
        lane = lax.broadcasted_iota(jnp.int32, (s, LANES), 1)
        fold = lambda a: a + pltpu.roll(a, HEAD_DIM, 1)
        dkv_ref[...] = jnp.where(lane < HEAD_DIM, fold(dk), fold(dv)).astype(BF16)
        ds_ref[...] = jnp.zeros_like(ds_ref)
        for i in range(4):
            ds_ref[i:i + 1, :] = dsink[i]

    q, k, v, grp, _ = _attn_specs(s)
    return _hosted_call(
        body, "attn_bwd", (bsz, 2),
        in_specs=[q, k, v, grp] + _attn_saved_specs(bsz, n_blk)[0],
        out_specs=[grp, pl.BlockSpec((None, s, LANES), lambda b, g: (b, 0, g)),
                   pl.BlockSpec((None, None, 8, LANES), lambda b, g: (b, g, 0, 0))],
        out_shape=[jax.ShapeDtypeStruct((bsz, s, ATTN_WIDTH), BF16), jax.ShapeDtypeStruct((bsz, s, 2 * LANES), BF16),
                   jax.ShapeDtypeStruct((bsz, 2, 8, LANES), F32)],
        scratch_shapes=[pltpu.VMEM((s + 2 * CHUNK, LANES), BF16), pltpu.VMEM((s + 2 * CHUNK, LANES), BF16),
                        pltpu.VMEM((s + 2 * CHUNK, LANES), F32), pltpu.VMEM((s + 2 * CHUNK, LANES), F32)],
        operands=(u3, u3, u3, d_o, probs, sink_probs), rider=rider)


def _ffn_bwd(dz2, gs, us, pg, ple, zh1, r1, g1, wg4, wu4, wd4, wpg, w_out):
    t = dz2.shape[0]
    tm = 256
    wg_t, wu_t, wd_all = (w.reshape(FFN, D_MODEL) for w in (wg4, wu4, wd4))

    def body(dz_ref, gs_ref, us_ref, pg_ref, ple_ref, zh_ref, r_ref, g1_ref,
             wg_hbm, wu_hbm, wd_hbm, wpg_hbm, wo_hbm,
             dgs_ref, dus_ref, dsp_ref, dple_ref, dz1_ref, dyr_ref, dya_ref, dg1_ref, db1_ref,
             wg, wu, wd, wpg, wo, wsem):
        step = pl.program_id(0)
        loads = _resident_quarters(wd_hbm, wd) + _resident_quarters(wg_hbm, wg) + _resident_quarters(wu_hbm, wu)
        _load_resident(step, loads + [(wpg_hbm, wpg), (wo_hbm, wo)], wsem)

        @pl.when(step == 0)
        def _():
            dg1_ref[...] = jnp.zeros_like(dg1_ref)
            db1_ref[...] = jnp.zeros_like(db1_ref)

        dz = dz_ref[...]
        dzb = dz.astype(BF16)
        dh = ALPHA * dz
        pending = []
        chunks = [slice(n * FFN_CHUNK, (n + 1) * FFN_CHUNK) for n in range(N_FFN_CHUNK)]
        for n in range(N_FFN_CHUNK + 1):
            if n < N_FFN_CHUNK:
                da = _dot_nt(dzb, wd[chunks[n], :])
                gj = gs_ref[:, chunks[n]].astype(F32)
                uj = us_ref[:, chunks[n]].astype(F32)
                sg = _sigmoid(gj)
                dgj = (da * uj * sg * (1.0 + gj * (1.0 - sg))).astype(BF16)
                duj = (da * gj * sg).astype(BF16)
                dgs_ref[:, chunks[n]] = dgj
                dus_ref[:, chunks[n]] = duj
                pending.append((dgj, duj))
            if n > 0:
                dgp, dup = pending[n - 1]
                dh = dh + _dot(dgp, wg[chunks[n - 1], :]) + _dot(dup, wu[chunks[n - 1], :])
        pgv = pg_ref[...].astype(F32)
        plev = ple_ref[...].astype(F32)
        dple_ref[...] = (dz * pgv).astype(BF16)
        dsp = (dz * plev * pgv * (1.0 - pgv)).astype(BF16)
        dsp_ref[...] = dsp
        dh = dh + _dot_nt(dsp, wpg[...])
        zh = zh_ref[...]
        dg1_ref[...] += jnp.sum(dh * zh, axis=0, keepdims=True)
        db1_ref[...] += jnp.sum(dh, axis=0, keepdims=True)
        dzh = dh * g1_ref[...]
        m1 = jnp.mean(dzh, axis=1, keepdims=True)
        m2 = jnp.mean(dzh * zh, axis=1, keepdims=True)
        dz1 = r_ref[...] * (dzh - m1 - zh * m2)
        dz1_ref[...] = dz1
        dyc = _dot_nt(dz1.astype(BF16), wo[...])
        dyr_ref[...] = dyc[:, 0:RET_WIDTH].astype(BF16)
        dya_ref[...] = dyc[:, RET_WIDTH:].astype(BF16)

    row = lambda w: pl.BlockSpec((tm, w), lambda i: (i, 0))
    const = lambda s: pl.BlockSpec(s, lambda i: (0, 0))
    hbm = pl.BlockSpec(memory_space=pl.ANY)
    hid_shape = jax.ShapeDtypeStruct((t, FFN), BF16)
    return pl.pallas_call(
        body, name="ffn_bwd", grid=(t // tm,),
        in_specs=[row(D_MODEL), row(FFN), row(FFN), row(D_MODEL), row(D_MODEL), row(D_MODEL), row(1),
                  const((1, D_MODEL)), hbm, hbm, hbm, hbm, hbm],
        out_specs=[row(FFN), row(FFN), row(D_MODEL), row(D_MODEL), row(D_MODEL), row(RET_WIDTH), row(ATTN_WIDTH),
                   const((1, D_MODEL)), const((1, D_MODEL))],
        out_shape=[hid_shape, hid_shape, jax.ShapeDtypeStruct((t, D_MODEL), BF16),
                   jax.ShapeDtypeStruct((t, D_MODEL), BF16), jax.ShapeDtypeStruct((t, D_MODEL), F32),
                   jax.ShapeDtypeStruct((t, RET_WIDTH), BF16), jax.ShapeDtypeStruct((t, ATTN_WIDTH), BF16),
                   jax.ShapeDtypeStruct((1, D_MODEL), F32), jax.ShapeDtypeStruct((1, D_MODEL), F32)],
        scratch_shapes=[pltpu.VMEM((FFN, D_MODEL), BF16), pltpu.VMEM((FFN, D_MODEL), BF16),
                        pltpu.VMEM((FFN, D_MODEL), BF16),
                        pltpu.VMEM(wpg.shape, BF16), pltpu.VMEM(w_out.shape, BF16),
                        pltpu.SemaphoreType.DMA((3 * N_SHARD + 2,))],
        compiler_params=_params("arbitrary", vmem=VMEM_LIMIT),
    )(dz2, gs, us, pg, ple, zh1, r1, g1, wg_t, wu_t, wd_all, wpg, w_out)


def _wgrad_misc(y_ret, y_att, dz1, hb, dsp, p2d, dple, rider=None):
    t = dz1.shape[0]
    tk = min(t, 512)

    def body(yr_ref, ya_ref, dz_ref, hb_ref, dsp_ref, p_ref, dple_ref, wo_ref, wpg_ref, wpe_ref):
        @pl.when(pl.program_id(0) == 0)
        def _():
            wo_ref[...] = jnp.zeros_like(wo_ref)
            wpg_ref[...] = jnp.zeros_like(wpg_ref)
            wpe_ref[...] = jnp.zeros_like(wpe_ref)

        dzb = dz_ref[...].astype(BF16)
        wo_ref[0:RET_WIDTH, :] += _dot_tn(yr_ref[...], dzb)
        wo_ref[RET_WIDTH:, :] += _dot_tn(ya_ref[...], dzb)
        wpg_ref[...] += _dot_tn(hb_ref[...], dsp_ref[...])
        wpe_ref[...] += _dot_tn(p_ref[...].astype(BF16), dple_ref[...])

    row = lambda w: pl.BlockSpec((tk, w), lambda k: (k, 0))
    const = lambda s: pl.BlockSpec(s, lambda k: (0, 0))
    return _hosted_call(
        body, "wgrad_misc", (t // tk,),
        in_specs=[row(RET_WIDTH), row(ATTN_WIDTH), row(D_MODEL), row(D_MODEL), row(D_MODEL), row(PLE_DIM),
                  row(D_MODEL)],
        out_specs=[const((D_MODEL, D_MODEL)), const((D_MODEL, D_MODEL)), const((PLE_DIM, D_MODEL))],
        out_shape=[jax.ShapeDtypeStruct((D_MODEL, D_MODEL), F32), jax.ShapeDtypeStruct((D_MODEL, D_MODEL), F32),
                   jax.ShapeDtypeStruct((PLE_DIM, D_MODEL), F32)],
        scratch_shapes=[], operands=(y_ret, y_att, dz1, hb, dsp, p2d, dple), rider=rider, semantics=["arbitrary"])


def _wgrad_ffn(acts, dgs, dus, hb, dz2b):
    t = dz2b.shape[0]
    tk = min(t, 512)
    nk = t // tk

    def body(act_ref, dg_ref, du_ref, hb_ref, dz_ref, og_ref, ou_ref, od_ref):
        @pl.when(pl.program_id(1) == 0)
        def _():
            og_ref[...] = jnp.zeros_like(og_ref)
            ou_ref[...] = jnp.zeros_like(ou_ref)
            od_ref[...] = jnp.zeros_like(od_ref)

        hbv = hb_ref[...]
        og_ref[...] += _dot_tn(dg_ref[...], hbv)
        ou_ref[...] += _dot_tn(du_ref[...], hbv)
        od_ref[...] += _dot_tn(act_ref[...], dz_ref[...])

    half = FFN // 2
    a_spec = pl.BlockSpec((tk, half), lambda j, k: (k, j))
    b_spec = pl.BlockSpec((tk, D_MODEL), lambda j, k: (k, 0))
    o_spec = pl.BlockSpec((half, D_MODEL), lambda j, k: (j, 0))
    o_shape = jax.ShapeDtypeStruct((FFN, D_MODEL), F32)
    outs = pl.pallas_call(
        body, name="wgrad_ffn", grid=(2, nk),
        in_specs=[a_spec, a_spec, a_spec, b_spec, b_spec],
        out_specs=[o_spec] * 3, out_shape=[o_shape] * 3,
        compiler_params=_params("parallel", "arbitrary", vmem=VMEM_LIMIT),
    )(acts, dgs, dus, hb, dz2b)
    return [o.reshape(N_SHARD, FFN_SHARD, D_MODEL) for o in outs]


KV_ORDER = (0, 128, 64, 192)


def _wgrad_in(pieces, x2d, rider=None):
    t = x2d.shape[0]
    tk = min(t, 512)
    nk = t // tk
    kv0 = CB_AK * LANES

    def body(p0, p1, p2, p3, p4, pkv, x_ref, o_ref):
        @pl.when(pl.program_id(0) == 0)
        def _():
            o_ref[...] = jnp.zeros_like(o_ref)

        xb = x_ref[...].astype(BF16)
        for i, ref in enumerate((p0, p1, p2, p3, p4)):
            o_ref[i * 512:(i + 1) * 512, :] += _dot_tn(ref[...], xb)
        dkv = _dot_tn(pkv[...], xb)
        for i, o in enumerate(KV_ORDER):
            o_ref[kv0 + o:kv0 + o + HEAD_DIM, :] += dkv[i * HEAD_DIM:(i + 1) * HEAD_DIM]

    row = lambda w: pl.BlockSpec((tk, w), lambda k: (k, 0))
    return _hosted_call(
        body, "wgrad_in", (nk,),
        in_specs=[row(512)] * 5 + [row(256), row(D_MODEL)],
        out_specs=[pl.BlockSpec((IN_WIDTH, D_MODEL), lambda k: (0, 0))],
        out_shape=[jax.ShapeDtypeStruct((IN_WIDTH, D_MODEL), F32)],
        scratch_shapes=[], operands=(*pieces, x2d), rider=rider, semantics=["arbitrary"])


def _inproj_bwd(dz1, pieces, w_main, w_kv, rider=None):
    t = dz1.shape[0]
    tm = 512

    def body(dz_ref, p0, p1, p2, p3, p4, pkv, wm_ref, wkv_ref, o_ref):
        acc = ALPHA * dz_ref[...]
        for i, ref in enumerate((p0, p1, p2, p3, p4)):
            acc = acc + _dot(ref[...], wm_ref[i * 512:(i + 1) * 512, :])
        o_ref[...] = acc + _dot(pkv[...], wkv_ref[...])

    row = lambda w: pl.BlockSpec((tm, w), lambda i: (i, 0))
    const = lambda s: pl.BlockSpec(s, lambda i: (0, 0))
    return _hosted_call(
        body, "inproj_bwd", (t // tm,),
        in_specs=[row(D_MODEL)] + [row(512)] * 5 + [row(256), const(w_main.shape), const(w_kv.shape)],
        out_specs=[row(D_MODEL)],
        out_shape=[jax.ShapeDtypeStruct((t, D_MODEL), F32)],
        scratch_shapes=[], operands=(dz1, *pieces, w_main, w_kv), rider=rider)


def _coords():
    return lax.axis_index("x"), lax.axis_index("y"), lax.axis_index("c")


def _chip_of(x, y, rel):
    return (1 - x if rel & 2 else x), (1 - y if rel & 1 else y)


def _all_gather_weights(shards):
    first = _gather_chips_rider(shards)
    second = _gather_pass_rider([jax.ShapeDtypeStruct((N_SHARD,) + s.shape, s.dtype) for s in shards], chained=True)
    return _run_riders("gather_weights", shards, first.out_shapes, [first, second])


def _run_riders(name, ins, out_shapes, riders):
    n_in, n_out = len(ins), len(out_shapes)

    def body(*refs):
        in_refs, out_refs = refs[:n_in], refs[n_in:n_in + n_out]
        k = n_in + n_out
        for r in riders:
            sems = refs[k:k + len(r.sems)]
            k += len(r.sems)
            r.start(in_refs, out_refs, sems)
            r.finish(in_refs, out_refs, sems)

    hbm = pl.BlockSpec(memory_space=pl.ANY)
    return pl.pallas_call(
        body, name=name, in_specs=[hbm] * n_in, out_specs=[hbm] * n_out, out_shape=list(out_shapes),
        scratch_shapes=[s for r in riders for s in r.sems],
    )(*ins)


def _gather_half(outs, w, chip, cc):
    h = outs[w].shape[1] // 2
    return outs[w].at[chip, pl.ds(cc * h, h), :]


def _gather_chips_rider(shards):
    nw = len(shards)

    def copies(ins, outs, sems, arrivals):
        send, recv, lsend, lrecv = sems
        x, y, c = _coords()
        me = 2 * x + y
        own = [pltpu.make_async_remote_copy(
            src_ref=ins[w], dst_ref=outs[w].at[me], send_sem=lsend.at[w], recv_sem=lrecv.at[w],
            device_id=(x, y, 1 - c), device_id_type=MESH) for w in range(nw)]
        out, arrive = [], []
        for rel in (1, 2, 3):
            kx, ky = _chip_of(x, y, rel)
            for w in range(nw):
                h = shards[w].shape[0] // 2
                sem = dict(send_sem=send.at[w * 3 + rel - 1], recv_sem=recv.at[w * 3 + rel - 1],
                           device_id=(kx, ky, c), device_id_type=MESH)
                out.append(pltpu.make_async_remote_copy(
                    src_ref=ins[w].at[pl.ds(c * h, h), :], dst_ref=_gather_half(outs, w, me, c), **sem))
                if arrivals:
                    theirs = _gather_half(outs, w, 2 * kx + ky, c)
                    arrive.append(pltpu.make_async_remote_copy(src_ref=theirs, dst_ref=theirs, **sem))
        return own, out, arrive

    def start(ins, outs, sems):
        own, out, _ = copies(ins, outs, sems, arrivals=False)
        for cp in own + out:
            cp.start()

    def finish(ins, outs, sems):
        own, out, arrive = copies(ins, outs, sems, arrivals=True)
        for cp in arrive:
            cp.wait_recv()
        for cp in out:
            cp.wait_send()
        for cp in own:
            cp.wait()

    dma = pltpu.SemaphoreType.DMA
    return _Rider(shards, [jax.ShapeDtypeStruct((N_SHARD,) + s.shape, s.dtype) for s in shards],
                  [dma((3 * nw,)), dma((3 * nw,)), dma((nw,)), dma((nw,))], start, finish)


def _gather_pass_rider(gathered, chained=False):
    nw = len(gathered)

    def copies(outs, sems, cc):
        send, recv = sems
        x, y, c = _coords()
        res = []
        for rel in (1, 2, 3):
            kx, ky = _chip_of(x, y, rel)
            for w in range(nw):
                rows = _gather_half(outs, w, 2 * kx + ky, cc)
                res.append(pltpu.make_async_remote_copy(
                    src_ref=rows, dst_ref=rows, send_sem=send.at[w * 3 + rel - 1], recv_sem=recv.at[w * 3 + rel - 1],
                    device_id=(x, y, 1 - c), device_id_type=MESH))
        return res

    def start(ins, outs, sems):
        for cp in copies(outs, sems, lax.axis_index("c")):
            cp.start()

    def finish(ins, outs, sems):
        c = lax.axis_index("c")
        for cp in copies(outs, sems, 1 - c):
            cp.wait_recv()
        for cp in copies(outs, sems, c):
            cp.wait_send()

    dma = pltpu.SemaphoreType.DMA
    shapes = [jax.ShapeDtypeStruct(g.shape, g.dtype) for g in gathered]
    if chained:
        return _Rider([], [], [dma((3 * nw,)), dma((3 * nw,))], start, finish)
    return _Rider(gathered, shapes, [dma((3 * nw,)), dma((3 * nw,))], start, finish,
                  aliases={w: w for w in range(nw)})


def _exchange_halves_rider(parts):
    nw = len(parts)

    def copies(ins, outs, sems):
        send, recv = sems
        x, y, c = _coords()
        res = []
        for w in range(nw):
            h = parts[w].shape[1] // 2
            res.append(pltpu.make_async_remote_copy(
                src_ref=ins[w].at[:, pl.ds((1 - c) * h, h), :], dst_ref=outs[w],
                send_sem=send.at[w], recv_sem=recv.at[w], device_id=(x, y, 1 - c), device_id_type=MESH))
        return res

    def start(ins, outs, sems):
        for cp in copies(ins, outs, sems):
            cp.start()

    def finish(ins, outs, sems):
        for cp in copies(ins, outs, sems):
            cp.wait()

    dma = pltpu.SemaphoreType.DMA
    return _Rider(parts, [jax.ShapeDtypeStruct((N_SHARD, p.shape[1] // 2, p.shape[2]), p.dtype) for p in parts],
                  [dma((nw,)), dma((nw,))], start, finish)


def _add_halves(parts, theirs, pos):
    nw = len(parts)
    split = 2

    def body(pos_ref, *refs):
        ins, oth = refs[:nw], refs[nw:2 * nw]
        o32, o16 = refs[2 * nw:3 * nw], refs[3 * nw:]
        sums = [ins[w][...] + oth[w][...].astype(F32) for w in range(nw)]
        for w in range(nw):
            o16[w][...] = sums[w].astype(BF16)

        @pl.when(pl.program_id(1) == pos_ref[0])
        def _():
            for w in range(nw):
                o32[w][...] = sums[w]

    in_specs, oth_specs, o32_specs, shapes32, shapes16 = [], [], [], [], []
    for p in parts:
        hb = p.shape[1] // 2 // split
        blk = (None, hb, p.shape[2])
        in_specs.append(pl.BlockSpec(blk, lambda i, j, pos_ref: (j, pos_ref[1] * split + i, 0)))
        oth_specs.append(pl.BlockSpec(blk, lambda i, j, pos_ref: (j, i, 0)))
        o32_specs.append(pl.BlockSpec((hb, p.shape[2]), lambda i, j, pos_ref: (i, 0)))
        shapes32.append(jax.ShapeDtypeStruct((p.shape[1] // 2, p.shape[2]), F32))
        shapes16.append(jax.ShapeDtypeStruct((N_SHARD, p.shape[1] // 2, p.shape[2]), BF16))
    return pl.pallas_call(
        body, name="add_halves",
        grid_spec=pltpu.PrefetchScalarGridSpec(
            num_scalar_prefetch=1, grid=(split, N_SHARD),
            in_specs=in_specs + oth_specs, out_specs=o32_specs + oth_specs),
        out_shape=shapes32 + shapes16,
        compiler_params=_params("parallel", "arbitrary", vmem=VMEM_LIMIT),
    )(pos, *parts, *theirs)


def _exchange_chips_rider(sums16):
    nw = len(sums16)

    def copies(ins, outs, sems):
        send, recv = sems
        x, y, c = _coords()
        res = []
        for rel in (1, 2, 3):
            kx, ky = _chip_of(x, y, rel)
            for w in range(nw):
                res.append(pltpu.make_async_remote_copy(
                    src_ref=ins[w].at[2 * kx + ky], dst_ref=outs[w].at[rel - 1],
                    send_sem=send.at[w * 3 + rel - 1], recv_sem=recv.at[w * 3 + rel - 1],
                    device_id=(kx, ky, c), device_id_type=MESH))
        return res

    def start(ins, outs, sems):
        for cp in copies(ins, outs, sems):
            cp.start()

    def finish(ins, outs, sems):
        for cp in copies(ins, outs, sems):
            cp.wait()

    dma = pltpu.SemaphoreType.DMA
    return _Rider(sums16, [jax.ShapeDtypeStruct((3,) + s.shape[1:], BF16) for s in sums16],
                  [dma((3 * nw,)), dma((3 * nw,))], start, finish)


def _add_chips(sums32, theirs, pos):
    nw = len(sums32)
    split = 2

    def body(pos_ref, *refs):
        ins, oth, outs = refs[:nw], refs[nw:2 * nw], refs[2 * nw:]
        for w in range(nw):
            acc = ins[w][...]
            for r in range(3):
                acc = acc + oth[w][r].astype(F32)
            outs[w][...] = acc

    in_specs, oth_specs, out_specs, shapes = [], [], [], []
    for s in sums32:
        hb = s.shape[0] // split
        in_specs.append(pl.BlockSpec((hb, s.shape[1]), lambda i, pos_ref: (i, 0)))
        oth_specs.append(pl.BlockSpec((3, hb, s.shape[1]), lambda i, pos_ref: (0, i, 0)))
        out_specs.append(pl.BlockSpec((hb, s.shape[1]), lambda i, pos_ref: (pos_ref[1] * split + i, 0)))
        shapes.append(jax.ShapeDtypeStruct((2 * s.shape[0], s.shape[1]), F32))
    return pl.pallas_call(
        body, name="add_chips",
        grid_spec=pltpu.PrefetchScalarGridSpec(
            num_scalar_prefetch=1, grid=(split,), in_specs=in_specs + oth_specs, out_specs=out_specs),
        out_shape=shapes,
        compiler_params=_params("parallel", vmem=VMEM_LIMIT),
    )(pos, *sums32, *theirs)


def _join_halves(shards):
    nw = len(shards)

    def body(*refs):
        outs = refs[nw:2 * nw]
        send, recv = refs[2 * nw:]
        x, y, c = _coords()

        def copy(w, cc):
            h = shards[w].shape[0] // 2
            rows = outs[w].at[pl.ds(cc * h, h), :]
            return pltpu.make_async_remote_copy(
                src_ref=rows, dst_ref=rows, send_sem=send.at[w], recv_sem=recv.at[w],
                device_id=(x, y, 1 - c), device_id_type=MESH)

        for w in range(nw):
            copy(w, c).start()
        for w in range(nw):
            copy(w, 1 - c).wait_recv()
            copy(w, c).wait_send()

    hbm = pl.BlockSpec(memory_space=pl.ANY)
    return pl.pallas_call(
        body, name="join_halves",
        in_specs=[hbm] * nw, out_specs=[hbm] * nw,
        out_shape=[jax.ShapeDtypeStruct(s.shape, F32) for s in shards],
        input_output_aliases={w: w for w in range(nw)},
        scratch_shapes=[pltpu.SemaphoreType.DMA((nw,)), pltpu.SemaphoreType.DMA((nw,))],
    )(*shards)


def _adamw_math(w, g, m, v):
    m = ADAM_B1 * m + (1.0 - ADAM_B1) * g
    v = ADAM_B2 * v + (1.0 - ADAM_B2) * (g * g)
    m_hat = m / (1.0 - ADAM_B1 ** ADAM_STEP)
    v_hat = v / (1.0 - ADAM_B2 ** ADAM_STEP)
    delta = -ADAM_LR * (m_hat / (jnp.sqrt(v_hat) + ADAM_EPS) + ADAM_WD * w)
    return delta, m, v


def _adamw(ws, gs, ms, vs):
    nw = len(ws)
    split = 8

    def body(*refs):
        w_r, g_r, m_r, v_r = (refs[i * nw:(i + 1) * nw] for i in range(4))
        g_o, d_o, m_o, v_o = (refs[(4 + i) * nw:(5 + i) * nw] for i in range(4))
        for k in range(nw):
            g = g_r[k][...]
            d, m, v = _adamw_math(w_r[k][...], g, m_r[k][...], v_r[k][...])
            g_o[k][...] = g
            d_o[k][...] = d
            m_o[k][...] = m
            v_o[k][...] = v

    specs = [pl.BlockSpec((w.shape[0] // split, w.shape[1]), lambda i: (i, 0)) for w in ws]
    shapes = [jax.ShapeDtypeStruct(w.shape, F32) for w in ws]
    outs = pl.pallas_call(
        body, name="adamw", grid=(split,),
        in_specs=specs * 4, out_specs=specs * 4, out_shape=shapes * 4,
        compiler_params=_params("parallel", vmem=VMEM_LIMIT),
    )(*ws, *gs, *ms, *vs)
    return outs[:nw], outs[nw:2 * nw], outs[2 * nw:3 * nw], outs[3 * nw:]


SMALL_ROWS = 8
SMALL_COLS = D_MODEL
LOSS_COL = RET_WIDTH + 24


def _small_allreduce_adamw(part, w, m, v, rider=None):
    def body(part_ref, w_ref, m_ref, v_ref, g_out, d_out, m_out, v_out, all_ref, send, recv):
        x, y, c = _coords()
        me = 4 * x + 2 * y + c
        all_ref[me] = part_ref[...]
        copies = []
        for rel in range(1, 8):
            px = 1 - x if rel & 4 else x
            py = 1 - y if rel & 2 else y
            pc = 1 - c if rel & 1 else c
            copies.append(pltpu.make_async_remote_copy(
                src_ref=part_ref, dst_ref=all_ref.at[me],
                send_sem=send.at[rel - 1], recv_sem=recv.at[rel - 1], device_id=(px, py, pc), device_id_type=MESH))
        for cp in copies:
            cp.start()
        for cp in copies:
            cp.wait()
        g = all_ref[0]
        for k in range(1, 8):
            g = g + all_ref[k]
        d, mn, vn = _adamw_math(w_ref[...], g, m_ref[...], v_ref[...])
        g_out[...] = g
        d_out[...] = d
        m_out[...] = mn
        v_out[...] = vn

    vm = pl.BlockSpec(memory_space=pltpu.VMEM)
    shape = jax.ShapeDtypeStruct((SMALL_ROWS, SMALL_COLS), F32)
    return _hosted_call(
        body, "small_allreduce_adamw", (1,),
        in_specs=[vm] * 4, out_specs=[vm] * 4, out_shape=[shape] * 4,
        scratch_shapes=[pltpu.VMEM((8, SMALL_ROWS, SMALL_COLS), F32),
                        pltpu.SemaphoreType.DMA((7,)), pltpu.SemaphoreType.DMA((7,))],
        operands=(part, w, m, v), rider=rider, semantics=["arbitrary"])


SMALL_NAMES = ("ret_decay_fwd", "ret_decay_bwd", "attn_sink", "ret_gn_gain",
               "ln1_gain", "ln1_bias", "ln2_gain", "ln2_bias")


LN_NAMES = ("ln1_gain", "ln1_bias", "ln2_gain", "ln2_bias")


def _pack_small(vals, extra=None):
    tail = jnp.zeros((1, 1), F32) if extra is None else extra.reshape(1, 1)
    row4 = jnp.concatenate([vals["ret_gn_gain"], vals["ret_decay_fwd"], vals["ret_decay_bwd"], vals["attn_sink"],
                            tail, jnp.zeros((1, SMALL_COLS - LOSS_COL - 1), F32)], axis=1)
    rows = [vals[n] for n in LN_NAMES] + [row4, jnp.zeros((SMALL_ROWS - 5, SMALL_COLS), F32)]
    return jnp.concatenate(rows, axis=0)


def _unpack_small(packed):
    out = {n: packed[i:i + 1] for i, n in enumerate(LN_NAMES)}
    o = RET_WIDTH
    out.update(ret_gn_gain=packed[4:5, 0:o], ret_decay_fwd=packed[4:5, o:o + 8],
               ret_decay_bwd=packed[4:5, o + 8:o + 16], attn_sink=packed[4:5, o + 16:o + 24])
    return out


def _local_step(x, p, tgt, w_in_t, rest, small, pos=None, small_state=None):
    bsz, s, _ = x.shape
    t = bsz * s
    x2d = x.reshape(t, D_MODEL)
    p2d = p.reshape(t, PLE_DIM)
    tgt2d = tgt.reshape(t, D_MODEL)
    dec_f = small["ret_decay_fwd"].reshape(8)
    dec_b = small["ret_decay_bwd"].reshape(8)
    lg_f = jnp.log1p(-jnp.exp2(dec_f))
    lg_b = jnp.log1p(-jnp.exp2(dec_b))
    per_lane = lambda v: jnp.repeat(v, HEAD_DIM).reshape(4, 1, LANES)
    lgf_l, lgb_l = per_lane(lg_f), per_lane(lg_b)
    sink = small["attn_sink"].reshape(8)
    slopes = 2.0 ** (-(jnp.arange(8, dtype=F32) + 1.0))
    gn_gain = small["ret_gn_gain"]
    g1, b1, g2, b2 = (small[n] for n in ("ln1_gain", "ln1_bias", "ln2_gain", "ln2_bias"))

    dist = pos is not None
    if dist:
        shard = dict(zip(REST_NAMES, rest))
        half = FFN_SHARD // 2
        shard["up_lo"], shard["up_hi"] = shard["w_ffn_up"][:half], shard["w_ffn_up"][half:]
    chips = lambda names: _gather_chips_rider([shard[n] for n in names])
    first, second, third = ("w_out", "w_ple_gate"), ("w_ffn_gate", "w_ple_proj", "up_lo"), ("up_hi", "w_ffn_down")
    u, *c1 = _inproj(x2d, w_in_t, rider=chips(first) if dist else None)
    u3 = u.reshape(bsz, s, IN_WIDTH)
    y_hat, y_rstd, y_ret, ret_rb, ret_kvf, *o2 = _ret_fwd(u3, lgf_l, lgb_l, gn_gain,
                                 rider=_merge_riders([_gather_pass_rider(c1), chips(second)]) if dist else None)
    y_att, att_p, att_ps, *o3 = _attn_fwd(u3, slopes, sink, rider=_merge_riders(
        [_gather_pass_rider(o2[len(first):]), chips(third)]) if dist else None)
    gathered = dict(zip(first, o2[:len(first)]))
    gathered.update(zip(second, o3[:len(second)]))
    w_out = _assemble_weights({"w_out": gathered["w_out"]})["w_out"] if dist else rest["w_out"]
    zh1, r1, hb, *o4 = _outproj_ln1(y_ret.reshape(t, RET_WIDTH), y_att.reshape(t, ATTN_WIDTH), x2d, w_out, g1, b1,
                                    rider=_gather_pass_rider(o3[len(second):]) if dist else None)
    gathered.update(zip(third, o4))
    if dist:
        gathered["w_ffn_up"] = jnp.concatenate([gathered.pop("up_lo"), gathered.pop("up_hi")], axis=1)
    wts = _assemble_weights(gathered) if dist else rest
    dz2, dz2b, gs, us, acts, pg, ple, sq, dg2, db2 = _ffn_fwd(
        zh1, hb, p2d, tgt2d, g1, b1, g2, b2, wts["gate4"], wts["up4"], wts["down4"], wts["ple_proj"], wts["ple_gate"])
    dgs, dus, dsp, dple, dz1, dyr, dya, dg1, db1 = _ffn_bwd(dz2, gs, us, pg, ple, zh1, r1, g1, wts["gate4"],
                                                          wts["up4"], wts["down4"], wts["ple_gate"], wts["w_out"])
    ffn_parts = list(_wgrad_ffn(acts, dgs, dus, hb, dz2b))
    d_w_out, d_ple_gate, d_ple_proj, *th_ffn = _wgrad_misc(
        y_ret.reshape(t, RET_WIDTH), y_att.reshape(t, ATTN_WIDTH), dz1, hb, dsp, p2d, dple,
        rider=_exchange_halves_rider(ffn_parts) if dist else None)
    misc_parts = [d_w_out.reshape(N_SHARD, D_MODEL // N_SHARD, D_MODEL),
                  d_ple_proj.reshape(PLE_DIM, N_SHARD, D_MODEL // N_SHARD).transpose(1, 0, 2),
                  d_ple_gate.reshape(N_SHARD, D_MODEL // N_SHARD, D_MODEL)]
    dyr3, dya3 = dyr.reshape(bsz, s, RET_WIDTH), dya.reshape(bsz, s, ATTN_WIDTH)
    if dist:
        s_ffn = _add_halves(ffn_parts, th_ffn, pos)
        quarter = FFN_SHARD // 4
        up_lo, up_hi = s_ffn[4][:, :quarter], s_ffn[4][:, quarter:]
        drq, drk, drv, drg, rpart, *o5 = _ret_bwd(u3, y_hat, y_rstd, (ret_rb, ret_kvf), dyr3, lgf_l, lgb_l, gn_gain,
                                                  rider=_merge_riders(
            [_exchange_chips_rider([s_ffn[3], up_lo]), _exchange_halves_rider(misc_parts)]))
        s_misc = _add_halves(misc_parts, o5[2:], pos)
        daq, dakv, spart, *o6 = _attn_bwd(u3, dya3, att_p, att_ps, rider=_exchange_chips_rider([up_hi, s_ffn[5]]))
    else:
        drq, drk, drv, drg, rpart = _ret_bwd(u3, y_hat, y_rstd, (ret_rb, ret_kvf), dyr3, lgf_l, lgb_l, gn_gain)
        daq, dakv, spart = _attn_bwd(u3, dya3, att_p, att_ps)
    pieces = [a.reshape(t, -1) for a in (drq, drk, drv, drg, daq, dakv)]
    kv0 = CB_AK * LANES
    w_kv = jnp.concatenate([w_in_t[kv0 + o:kv0 + o + HEAD_DIM] for o in KV_ORDER], axis=0)
    d_in, *o7 = _wgrad_in(pieces, x2d, rider=_exchange_chips_rider(list(s_misc[3:])) if dist else None)
    d_in = d_in.reshape(N_SHARD, FFN_SHARD, D_MODEL)

    rsum = rpart
    lane_heads = lambda row: jnp.sum(row.reshape(4, 2, HEAD_DIM), axis=-1).reshape(8)
    dlg_f = lane_heads(rsum[:, 0, :]) + jnp.stack([jnp.sum(rsum[:, 2, :], -1), jnp.sum(rsum[:, 3, :], -1)], 1).reshape(8)
    dlg_b = lane_heads(rsum[:, 1, :]) + jnp.stack([jnp.sum(rsum[:, 4, :], -1), jnp.sum(rsum[:, 5, :], -1)], 1).reshape(8)
    chain = lambda d: -(math.log(2.0) * jnp.exp2(d)) / (1.0 - jnp.exp2(d))
    grads_small = {
        "ret_decay_fwd": (dlg_f * chain(dec_f)).reshape(1, 8),
        "ret_decay_bwd": (dlg_b * chain(dec_b)).reshape(1, 8),
        "attn_sink": jnp.sum(spart, axis=0)[:, 0:4, 0].reshape(1, 8),
        "ret_gn_gain": rsum[:, 6, :].reshape(1, RET_WIDTH),
        "ln1_gain": dg1, "ln1_bias": db1, "ln2_gain": dg2, "ln2_bias": db2,
    }
    if not dist:
        grad_x, = _inproj_bwd(dz1, pieces, w_in_t[:kv0], w_kv)
        grads_rest = [misc_parts[0]] + ffn_parts + misc_parts[1:]
        return sq[0, 0], grad_x.reshape(bsz, s, D_MODEL), d_in, grads_rest, grads_small
    *small_out, th_in = _small_allreduce_adamw(_pack_small(grads_small, sq[0, 0]), *small_state,
                                               rider=_exchange_halves_rider([d_in]))
    s_in = _add_halves([d_in], [th_in], pos)
    grad_x, chips_in = _inproj_bwd(dz1, pieces, w_in_t[:kv0], w_kv, rider=_exchange_chips_rider([s_in[1]]))
    sums32 = [s_in[0], s_misc[0], s_ffn[0], s_ffn[1], s_ffn[2], s_misc[1], s_misc[2]]
    chips_up = jnp.concatenate([o5[1], o6[0]], axis=1)
    from_chips = [chips_in, o7[0], o5[0], chips_up, o6[1], o7[1], o7[2]]
    return grad_x.reshape(bsz, s, D_MODEL), sums32, from_chips, small_out


BIG_NAMES = ("w_in", "w_out", "w_ffn_gate", "w_ffn_up", "w_ffn_down", "w_ple_proj", "w_ple_gate")
REST_NAMES = BIG_NAMES[1:]
TRANSPOSED = ("w_in", "w_ffn_gate", "w_ffn_up")
WEIGHT_ORDER = ("w_in", "ret_decay_fwd", "ret_decay_bwd", "ret_gn_gain", "attn_sink", "w_out", "ln1_gain",
                "ln1_bias", "w_ffn_gate", "w_ffn_up", "w_ffn_down", "w_ple_proj", "w_ple_gate", "ln2_gain", "ln2_bias")


def _shard_rows(name, a):
    return jnp.swapaxes(a[0], 0, 1) if name in TRANSPOSED else a[0]


def _unshard_rows(name, a):
    return (jnp.swapaxes(a, 0, 1) if name in TRANSPOSED else a)[None]


def _assemble_weights(gathered):
    cols = lambda a: a.transpose(1, 0, 2).reshape(a.shape[1], N_SHARD * a.shape[2])
    rows = lambda a: a.reshape(N_SHARD * a.shape[1], a.shape[2])
    same = lambda a: a
    layout = {"w_out": ("w_out", rows), "w_ffn_gate": ("gate4", same), "w_ffn_up": ("up4", same),
              "w_ffn_down": ("down4", same), "w_ple_proj": ("ple_proj", cols), "w_ple_gate": ("ple_gate", rows)}
    return {layout[n][0]: layout[n][1](a) for n, a in gathered.items()}


def kernel(x, p, w_in, ret_decay_fwd, ret_decay_bwd, ret_gn_gain, attn_sink, w_out, ln1_gain, ln1_bias, w_ffn_gate, w_ffn_up, w_ffn_down, w_ple_proj, w_ple_gate, ln2_gain, ln2_bias, loss_target, m_w_in, m_ret_decay_fwd, m_ret_decay_bwd, m_ret_gn_gain, m_attn_sink, m_w_out, m_ln1_gain, m_ln1_bias, m_w_ffn_gate, m_w_ffn_up, m_w_ffn_down, m_w_ple_proj, m_w_ple_gate, m_ln2_gain, m_ln2_bias, v_w_in, v_ret_decay_fwd, v_ret_decay_bwd, v_ret_gn_gain, v_attn_sink, v_w_out, v_ln1_gain, v_ln1_bias, v_w_ffn_gate, v_w_ffn_up, v_w_ffn_down, v_w_ple_proj, v_w_ple_gate, v_ln2_gain, v_ln2_bias):
    w = dict(w_in=w_in, ret_decay_fwd=ret_decay_fwd, ret_decay_bwd=ret_decay_bwd, ret_gn_gain=ret_gn_gain,
             attn_sink=attn_sink, w_out=w_out, ln1_gain=ln1_gain, ln1_bias=ln1_bias, w_ffn_gate=w_ffn_gate,
             w_ffn_up=w_ffn_up, w_ffn_down=w_ffn_down, w_ple_proj=w_ple_proj, w_ple_gate=w_ple_gate,
             ln2_gain=ln2_gain, ln2_bias=ln2_bias)
    m = dict(w_in=m_w_in, ret_decay_fwd=m_ret_decay_fwd, ret_decay_bwd=m_ret_decay_bwd, ret_gn_gain=m_ret_gn_gain,
             attn_sink=m_attn_sink, w_out=m_w_out, ln1_gain=m_ln1_gain, ln1_bias=m_ln1_bias, w_ffn_gate=m_w_ffn_gate,
             w_ffn_up=m_w_ffn_up, w_ffn_down=m_w_ffn_down, w_ple_proj=m_w_ple_proj, w_ple_gate=m_w_ple_gate,
             ln2_gain=m_ln2_gain, ln2_bias=m_ln2_bias)
    v = dict(w_in=v_w_in, ret_decay_fwd=v_ret_decay_fwd, ret_decay_bwd=v_ret_decay_bwd, ret_gn_gain=v_ret_gn_gain,
             attn_sink=v_attn_sink, w_out=v_w_out, ln1_gain=v_ln1_gain, ln1_bias=v_ln1_bias, w_ffn_gate=v_w_ffn_gate,
             w_ffn_up=v_w_ffn_up, w_ffn_down=v_w_ffn_down, w_ple_proj=v_w_ple_proj, w_ple_gate=v_w_ple_gate,
             ln2_gain=v_ln2_gain, ln2_bias=v_ln2_bias)
    big = lambda d: [_shard_rows(n, d[n]) for n in BIG_NAMES]
    small = lambda d: {n: d[n] for n in SMALL_NAMES}

    chip = 2 * lax.axis_index("x") + lax.axis_index("y")
    pos = jnp.stack([chip, lax.axis_index("c")]).astype(jnp.int32)

    shards = [a.astype(BF16) for a in big(w)]
    (w_in4,) = _all_gather_weights(shards[:1])
    w_in_t = w_in4.reshape(IN_WIDTH, D_MODEL)
    grad_x, sums32, from_chips, (g_s, d_s, m_s, v_s) = _local_step(
        x, p[0], loss_target, w_in_t, shards[1:], small(w), pos=pos,
        small_state=(_pack_small(small(w)), _pack_small(small(m)), _pack_small(small(v))))
    g_big, d_big, m_big, v_big = _adamw(big(w), _join_halves(_add_chips(sums32, from_chips, pos)), big(m), big(v))
    loss = g_s[4, LOSS_COL] * (0.5 / D_MODEL)

    def tree(bigs, packed):
        out = {n: _unshard_rows(n, a) for n, a in zip(BIG_NAMES, bigs)}
        out.update(_unpack_small(packed))
        return [out[n] for n in WEIGHT_ORDER]

    return (loss, grad_x, *tree(g_big, g_s), *tree(d_big, d_s), *tree(m_big, m_s), *tree(v_big, v_s))
```

```python
import functools
import math

import jax
import jax.numpy as jnp
from jax import lax
from jax.experimental import pallas as pl
from jax.experimental.pallas import tpu as pltpu

F32 = jnp.float32
BF16 = jnp.bfloat16

D_MODEL = 1024
HEAD_DIM = 64
RET_HEADS = 8
ATTN_HEADS = 8
RET_WIDTH = 512
ATTN_WIDTH = 512
KV_WIDTH = 128
IN_WIDTH = 2816
FFN = 2816
N_SHARD = 4
FFN_SHARD = FFN // N_SHARD
PLE_DIM = 256
CHUNK = 128
LANES = 128
ALPHA = 2.0 ** 0.25
LN_EPS = 1e-5
GN_EPS = 1e-5
NEG_INF = -1e30
ADAM_LR = 0.001
ADAM_B1 = 0.9
ADAM_B2 = 0.999
ADAM_EPS = 1e-08
ADAM_WD = 0.01
ADAM_STEP = 10
VMEM_LIMIT = 56 * 1024 * 1024
MESH = pl.DeviceIdType.MESH

CB_RQ, CB_RK, CB_RV, CB_RG, CB_AQ, CB_AK, CB_AV = 0, 4, 8, 12, 16, 20, 21


def _dot(a, b):
    return jnp.dot(a, b, preferred_element_type=F32)


def _dot_nt(a, b):
    return lax.dot_general(a, b, (((1,), (1,)), ((), ())), preferred_element_type=F32)


def _dot_tn(a, b):
    return lax.dot_general(a, b, (((0,), (0,)), ((), ())), preferred_element_type=F32)


def _sigmoid(x):
    return 1.0 / (1.0 + jnp.exp(-x))


def _params(*sem, vmem=None):
    return pltpu.CompilerParams(dimension_semantics=tuple(sem) if sem else None, vmem_limit_bytes=vmem)


class _Rider:
    def __init__(self, ins, out_shapes, sems, start, finish, aliases=None):
        self.ins, self.out_shapes, self.sems = list(ins), list(out_shapes), list(sems)
        self.start, self.finish, self.aliases = start, finish, dict(aliases or {})


def _merge_riders(riders):
    riders = [r for r in riders if r is not None]
    if len(riders) == 1:
        return riders[0]
    bounds, aliases = [], {}
    i0 = o0 = s0 = 0
    for r in riders:
        bounds.append((i0, o0, s0))
        aliases.update({i0 + i: o0 + o for i, o in r.aliases.items()})
        i0, o0, s0 = i0 + len(r.ins), o0 + len(r.out_shapes), s0 + len(r.sems)

    def each(method):
        def run(ins, outs, sems):
            for r, (i, o, s) in zip(riders, bounds):
                getattr(r, method)(ins[i:i + len(r.ins)], outs[o:o + len(r.out_shapes)], sems[s:s + len(r.sems)])
        return run

    return _Rider([a for r in riders for a in r.ins], [a for r in riders for a in r.out_shapes],
                  [a for r in riders for a in r.sems], each("start"), each("finish"), aliases)


def _hosted_call(body, name, grid, in_specs, out_specs, out_shape, scratch_shapes, operands, rider=None,
                 semantics=None):
    n_in, n_out, n_scr = len(in_specs), len(out_specs), len(scratch_shapes)
    if rider is None:
        return pl.pallas_call(
            body, name=name, grid=grid, in_specs=in_specs, out_specs=out_specs, out_shape=out_shape,
            scratch_shapes=scratch_shapes,
            compiler_params=_params(*(semantics or ["parallel"] * len(grid)), vmem=VMEM_LIMIT))(*operands)
    r_in, r_out = len(rider.ins), len(rider.out_shapes)

    def full_body(*refs):
        main_in, rin = refs[:n_in], refs[n_in:n_in + r_in]
        o0 = n_in + r_in
        main_out, rout = refs[o0:o0 + n_out], refs[o0 + n_out:o0 + n_out + r_out]
        s0 = o0 + n_out + r_out
        main_scr, rsem = refs[s0:s0 + n_scr], refs[s0 + n_scr:]
        first = functools.reduce(jnp.logical_and, [pl.program_id(a) == 0 for a in range(len(grid))])
        last = functools.reduce(jnp.logical_and, [pl.program_id(a) == g - 1 for a, g in enumerate(grid)])

        @pl.when(first)
        def _():
            rider.start(rin, rout, rsem)

        body(*main_in, *main_out, *main_scr)

        @pl.when(last)
        def _():
            rider.finish(rin, rout, rsem)

    hbm = pl.BlockSpec(memory_space=pl.ANY)
    return pl.pallas_call(
        full_body, name=name, grid=grid,
        in_specs=list(in_specs) + [hbm] * r_in, out_specs=list(out_specs) + [hbm] * r_out,
        out_shape=list(out_shape) + rider.out_shapes,
        scratch_shapes=list(scratch_shapes) + rider.sems,
        input_output_aliases={n_in + i: n_out + o for i, o in rider.aliases.items()},
        compiler_params=_params(*(["arbitrary"] * len(grid)), vmem=VMEM_LIMIT),
    )(*operands, *rider.ins)


def _loop_pairs(n, body, init):
    if n % 2:
        return lax.fori_loop(0, n, body, init)
    return lax.fori_loop(0, n // 2, lambda i, c: body(2 * i + 1, body(2 * i, c)), init)


def _head_mean(x, m0):
    s0 = jnp.sum(jnp.where(m0, x, 0.0), axis=1, keepdims=True)
    s1 = jnp.sum(jnp.where(m0, 0.0, x), axis=1, keepdims=True)
    return jnp.where(m0, s0, s1) * (1.0 / HEAD_DIM)


def _inproj(x2d, w_in_t, rider=None):
    t = x2d.shape[0]
    tm = 512
    nb = 256

    def body(x_ref, w_ref, o_ref):
        xb = x_ref[...].astype(BF16)
        for n in range(0, IN_WIDTH, nb):
            o_ref[:, n:n + nb] = _dot_nt(xb, w_ref[n:n + nb, :]).astype(BF16)

    return _hosted_call(
        body, "inproj", (t // tm,),
        in_specs=[pl.BlockSpec((tm, D_MODEL), lambda i: (i, 0)),
                  pl.BlockSpec((IN_WIDTH, D_MODEL), lambda i: (0, 0))],
        out_specs=[pl.BlockSpec((tm, IN_WIDTH), lambda i: (i, 0))],
        out_shape=[jax.ShapeDtypeStruct((t, IN_WIDTH), BF16)],
        scratch_shapes=[], operands=(x2d, w_in_t), rider=rider)


def _outproj_ln1(y_ret, y_att, x2d, w_out, gain, bias, rider=None):
    t = x2d.shape[0]
    tm = 512

    def body(yr_ref, ya_ref, x_ref, w_ref, g_ref, b_ref, zh_ref, r_ref, hb_ref):
        mix = _dot(yr_ref[...], w_ref[0:RET_WIDTH, :]) + _dot(ya_ref[...], w_ref[RET_WIDTH:, :])
        z = ALPHA * x_ref[...] + mix
        mu = jnp.mean(z, axis=1, keepdims=True)
        zc = z - mu
        var = jnp.mean(zc * zc, axis=1, keepdims=True)
        r = lax.rsqrt(var + LN_EPS)
        zh = zc * r
        zh_ref[...] = zh
        r_ref[...] = r
        hb_ref[...] = (zh * g_ref[...] + b_ref[...]).astype(BF16)

    row = lambda w: pl.BlockSpec((tm, w), lambda i: (i, 0))
    const = lambda s: pl.BlockSpec(s, lambda i: (0, 0))
    return _hosted_call(
        body, "outproj_ln1", (t // tm,),
        in_specs=[row(RET_WIDTH), row(ATTN_WIDTH), row(D_MODEL), const((D_MODEL, D_MODEL)),
                  const((1, D_MODEL)), const((1, D_MODEL))],
        out_specs=[row(D_MODEL), row(1), row(D_MODEL)],
        out_shape=[jax.ShapeDtypeStruct((t, D_MODEL), F32), jax.ShapeDtypeStruct((t, 1), F32),
                   jax.ShapeDtypeStruct((t, D_MODEL), BF16)],
        scratch_shapes=[], operands=(y_ret, y_att, x2d, w_out, gain, bias), rider=rider)


def _load_resident(step, pairs, sems):
    copies = [pltpu.make_async_copy(src, dst, sems.at[i]) for i, (src, dst) in enumerate(pairs)]

    @pl.when(step == 0)
    def _():
        for cp in copies:
            cp.start()
        for cp in copies:
            cp.wait()


FFN_CHUNK = 256
N_FFN_CHUNK = FFN // FFN_CHUNK


def _resident_quarters(hbm, vmem):
    q = FFN // N_SHARD
    return [(hbm.at[pl.ds(j * q, q), :], vmem.at[pl.ds(j * q, q), :]) for j in range(N_SHARD)]


def _ffn_fwd(zh1, hb, p2d, tgt, g1, b1, g2, b2, wg4, wu4, wd4, wpe, wpg):
    t = zh1.shape[0]
    tm = 256
    wg_t, wu_t, wd_all = (w.reshape(FFN, D_MODEL) for w in (wg4, wu4, wd4))

    def body(zh_ref, hb_ref, p_ref, t_ref, g1_ref, b1_ref, g2_ref, b2_ref,
             wg_hbm, wu_hbm, wd_hbm, wpe_hbm, wpg_hbm,
             dz_ref, dzb_ref, gs_ref, us_ref, act_ref, pg_ref, ple_ref, loss_ref, dg2_ref, db2_ref,
             wg, wu, wd, wpe, wpg, wsem):
        step = pl.program_id(0)
        loads = _resident_quarters(wg_hbm, wg) + _resident_quarters(wu_hbm, wu) + _resident_quarters(wd_hbm, wd)
        _load_resident(step, loads + [(wpe_hbm, wpe), (wpg_hbm, wpg)], wsem)

        @pl.when(step == 0)
        def _():
            loss_ref[...] = jnp.zeros_like(loss_ref)
            dg2_ref[...] = jnp.zeros_like(dg2_ref)
            db2_ref[...] = jnp.zeros_like(db2_ref)

        h1 = zh_ref[...] * g1_ref[...] + b1_ref[...]
        hbv = hb_ref[...]
        ffn = jnp.zeros((tm, D_MODEL), F32)
        acts = []
        chunks = [slice(n * FFN_CHUNK, (n + 1) * FFN_CHUNK) for n in range(N_FFN_CHUNK)]
        for n in range(N_FFN_CHUNK + 1):
            if n < N_FFN_CHUNK:
                gj = _dot_nt(hbv, wg[chunks[n], :])
                uj = _dot_nt(hbv, wu[chunks[n], :])
                gs_ref[:, chunks[n]] = gj.astype(BF16)
                us_ref[:, chunks[n]] = uj.astype(BF16)
                acts.append((gj * _sigmoid(gj) * uj).astype(BF16))
                act_ref[:, chunks[n]] = acts[n]
            if n > 0:
                ffn = ffn + _dot(acts[n - 1], wd[chunks[n - 1], :])
        ple = _dot(p_ref[...].astype(BF16), wpe[...])
        pg = _sigmoid(_dot(hbv, wpg[...]))
        pg_ref[...] = pg.astype(BF16)
        ple_ref[...] = ple.astype(BF16)
        z2 = ALPHA * h1 + ffn + pg * ple
        mu = jnp.mean(z2, axis=1, keepdims=True)
        zc = z2 - mu
        var = jnp.mean(zc * zc, axis=1, keepdims=True)
        r = lax.rsqrt(var + LN_EPS)
        zh2 = zc * r
        err = zh2 * g2_ref[...] + b2_ref[...] - t_ref[...]
        loss_ref[...] += jnp.sum(err * err)
        dy = err * (1.0 / D_MODEL)
        dg2_ref[...] += jnp.sum(dy * zh2, axis=0, keepdims=True)
        db2_ref[...] += jnp.sum(dy, axis=0, keepdims=True)
        dzh = dy * g2_ref[...]
        m1 = jnp.mean(dzh, axis=1, keepdims=True)
        m2 = jnp.mean(dzh * zh2, axis=1, keepdims=True)
        dz2 = r * (dzh - m1 - zh2 * m2)
        dz_ref[...] = dz2
        dzb_ref[...] = dz2.astype(BF16)

    row = lambda w: pl.BlockSpec((tm, w), lambda i: (i, 0))
    const = lambda s: pl.BlockSpec(s, lambda i: (0, 0))
    hid_shape = jax.ShapeDtypeStruct((t, FFN), BF16)
    hbm = pl.BlockSpec(memory_space=pl.ANY)
    return pl.pallas_call(
        body, name="ffn_fwd", grid=(t // tm,),
        in_specs=[row(D_MODEL), row(D_MODEL), row(PLE_DIM), row(D_MODEL),
                  const((1, D_MODEL)), const((1, D_MODEL)), const((1, D_MODEL)), const((1, D_MODEL)),
                  hbm, hbm, hbm, hbm, hbm],
        out_specs=[row(D_MODEL), row(D_MODEL), row(FFN), row(FFN), row(FFN), row(D_MODEL), row(D_MODEL),
                   const((8, LANES)), const((1, D_MODEL)), const((1, D_MODEL))],
        out_shape=[jax.ShapeDtypeStruct((t, D_MODEL), F32), jax.ShapeDtypeStruct((t, D_MODEL), BF16),
                   hid_shape, hid_shape, hid_shape,
                   jax.ShapeDtypeStruct((t, D_MODEL), BF16), jax.ShapeDtypeStruct((t, D_MODEL), BF16),
                   jax.ShapeDtypeStruct((8, LANES), F32),
                   jax.ShapeDtypeStruct((1, D_MODEL), F32), jax.ShapeDtypeStruct((1, D_MODEL), F32)],
        scratch_shapes=[pltpu.VMEM((FFN, D_MODEL), BF16), pltpu.VMEM((FFN, D_MODEL), BF16),
                        pltpu.VMEM((FFN, D_MODEL), BF16),
                        pltpu.VMEM(wpe.shape, BF16), pltpu.VMEM(wpg.shape, BF16),
                        pltpu.SemaphoreType.DMA((3 * N_SHARD + 2,))],
        compiler_params=_params("arbitrary", vmem=VMEM_LIMIT),
    )(zh1, hb, p2d, tgt, g1, b1, g2, b2, wg_t, wu_t, wd_all, wpe, wpg)


def _ret_tables(lgf, lgb):
    c = CHUNK
    row = lax.broadcasted_iota(jnp.int32, (c, LANES), 0).astype(F32)
    ii = lax.broadcasted_iota(jnp.int32, (c, c), 0).astype(F32)
    jj = lax.broadcasted_iota(jnp.int32, (c, c), 1).astype(F32)
    diff = ii - jj
    dmats = []
    for h in range(2):
        lf = lgf[:, h * HEAD_DIM:h * HEAD_DIM + 1]
        lb = lgb[:, h * HEAD_DIM:h * HEAD_DIM + 1]
        dmats.append(jnp.where(diff > 0, jnp.exp(lf * jnp.maximum(diff, 0.0)),
                               jnp.where(diff < 0, jnp.exp(lb * jnp.maximum(-diff, 0.0)), 2.0)))
    tab = dict(
        qdec_f=jnp.exp(lgf * (row + 1.0)), kdec_f=jnp.exp(lgf * (c - 1.0 - row)),
        qdec_b=jnp.exp(lgb * (c - row)), kdec_b=jnp.exp(lgb * row),
        cdec_f=jnp.exp(lgf * c), cdec_b=jnp.exp(lgb * c),
        d0=dmats[0], d1=dmats[1], row=row, diff=diff)
    r = lax.broadcasted_iota(jnp.int32, (LANES, LANES), 0) < HEAD_DIM
    cc = lax.broadcasted_iota(jnp.int32, (LANES, LANES), 1) < HEAD_DIM
    tab["bd"] = r == cc
    tab["m0"] = lax.broadcasted_iota(jnp.int32, (c, LANES), 1) < HEAD_DIM
    return tab


def _ret_specs(bsz, s):
    blk = lambda cb: pl.BlockSpec((bsz, s, LANES), lambda p, cb=cb: (0, 0, cb + p))
    lane = pl.BlockSpec((None, 1, LANES), lambda p: (p, 0, 0))
    gain = pl.BlockSpec((1, LANES), lambda p: (0, p))
    pair = pl.BlockSpec((bsz, s, LANES), lambda p: (0, 0, p))
    return blk, lane, gain, pair


def _ret_state_spec(bsz, n_chunk):
    spec = pl.BlockSpec((None, bsz, n_chunk, LANES, LANES), lambda p: (p, 0, 0, 0, 0))
    return spec, jax.ShapeDtypeStruct((4, bsz, n_chunk, LANES, LANES), F32)


def _ret_kv_states(tb, k_ref, v_ref, rb_ref, kvf_ref, n_chunk):
    c = CHUNK
    bsz = k_ref.shape[0]
    bd = tb["bd"]

    def contributions(n, carry):
        sl = pl.ds(pl.multiple_of(n * c, c), c)
        kfb = []
        for b in range(bsz):
            k32 = k_ref[b, sl, :].astype(F32)
            kfb.append(jnp.concatenate([k32 * tb["kdec_f"], k32 * tb["kdec_b"]], axis=1).astype(BF16))
        kvs = [_dot_tn(kfb[b], v_ref[b, sl, :]) for b in range(bsz)]
        for b in range(bsz):
            kvf_ref[b, n] = jnp.where(bd, kvs[b][0:LANES], 0.0)
            rb_ref[b, n] = jnp.where(bd, kvs[b][LANES:], 0.0)
        return carry

    lax.fori_loop(0, n_chunk, contributions, 0, unroll=2)

    def recur(i, rbs):
        n = n_chunk - 1 - i
        new = []
        for b in range(bsz):
            own = rb_ref[b, n]
            rb_ref[b, n] = rbs[b]
            new.append(rbs[b] * tb["cdec_b"] + own)
        return tuple(new)

    lax.fori_loop(0, n_chunk, recur, tuple(jnp.zeros((LANES, LANES), F32) for _ in range(bsz)))


def _split_rows(x, m0):
    return jnp.concatenate([jnp.where(m0, x, 0.0), jnp.where(m0, 0.0, x)], axis=0).astype(BF16)


def _ret_fwd(u3, lgf_l, lgb_l, gn_gain, rider=None):
    bsz, s, _ = u3.shape
    n_chunk = s // CHUNK
    c = CHUNK

    def body(q_ref, k_ref, v_ref, g_ref, lgf_ref, lgb_ref, gain_ref, yh_ref, rstd_ref, o_ref, rb_ref, kvf_ref):
        tb = _ret_tables(lgf_ref[...], lgb_ref[...])
        m0 = tb["m0"]
        gain = gain_ref[...]
        rows = range(bsz)
        _ret_kv_states(tb, k_ref, v_ref, rb_ref, kvf_ref, n_chunk)

        def chunk(n, rfs):
            sl = pl.ds(pl.multiple_of(n * c, c), c)
            qs = [q_ref[b, sl, :].astype(F32) * 0.125 for b in rows]
            s01 = [_dot_nt(_split_rows(qs[b], m0), k_ref[b, sl, :]) for b in rows]
            ys = []
            for b in rows:
                lhs = jnp.concatenate([s01[b][0:c] * tb["d0"], s01[b][c:] * tb["d1"],
                                       qs[b] * tb["qdec_f"], qs[b] * tb["qdec_b"]], axis=1).astype(BF16)
                rhs = jnp.concatenate([_split_rows(v_ref[b, sl, :].astype(F32), m0),
                                       rfs[b].astype(BF16), rb_ref[b, n].astype(BF16)], axis=0)
                ys.append(_dot(lhs, rhs))
            new = []
            for b in rows:
                y = ys[b]
                mu = _head_mean(y, m0)
                yc = y - mu
                rstd = lax.rsqrt(_head_mean(yc * yc, m0) + GN_EPS)
                yh = yc * rstd
                g = g_ref[b, sl, :].astype(F32)
                yh_ref[b, sl, :] = yh
                rstd_ref[b, sl, :] = rstd
                o_ref[b, sl, :] = (yh * gain * (g * _sigmoid(g))).astype(BF16)
                new.append(rfs[b] * tb["cdec_f"] + kvf_ref[b, n])
            return tuple(new)

        _loop_pairs(n_chunk, chunk, tuple(jnp.zeros((LANES, LANES), F32) for _ in rows))

    blk, lane, gain, pair = _ret_specs(bsz, s)
    state, state_shape = _ret_state_spec(bsz, n_chunk)
    return _hosted_call(
        body, "ret_fwd", (4,),
        in_specs=[blk(CB_RQ), blk(CB_RK), blk(CB_RV), blk(CB_RG), lane, lane, gain],
        out_specs=[pair, pair, pair, state, state],
        out_shape=[jax.ShapeDtypeStruct((bsz, s, RET_WIDTH), F32), jax.ShapeDtypeStruct((bsz, s, RET_WIDTH), F32),
                   jax.ShapeDtypeStruct((bsz, s, RET_WIDTH), BF16), state_shape, state_shape],
        scratch_shapes=[],
        operands=(u3, u3, u3, u3, lgf_l, lgb_l, gn_gain), rider=rider)


def _ret_bwd(u3, y_hat, y_rstd, states, d_o, lgf_l, lgb_l, gn_gain, rider=None):
    bsz, s, _ = u3.shape
    n_chunk = s // CHUNK
    c = CHUNK

    def body(q_ref, k_ref, v_ref, g_ref, yh_ref, rstd_ref, do_ref, lgf_ref, lgb_ref, gain_ref, rb_ref, kvf_ref,
             dq_ref, dk_ref, dv_ref, dg_ref, part_ref,
             rf_ref, dirf_ref, dy_ref, dk_acc, dv_acc, pa0, pa1, vec_ref):
        tb = _ret_tables(lgf_ref[...], lgb_ref[...])
        m0, bd, row = tb["m0"], tb["bd"], tb["row"]
        gain = gain_ref[...]
        wf = jnp.maximum(tb["diff"], 0.0)
        wb = jnp.maximum(-tb["diff"], 0.0)
        rows = range(bsz)
        zero_states = tuple(jnp.zeros((LANES, LANES), F32) for _ in rows)
        for ref in (pa0, pa1):
            ref[...] = jnp.zeros_like(ref)
        vec_ref[...] = jnp.zeros_like(vec_ref)

        def sweep_fwd(n, carry):
            rfs, gbs = carry
            sl = pl.ds(pl.multiple_of(n * c, c), c)
            qs, ks, vs, dys, dybs, q01, k01, dy01 = [], [], [], [], [], [], [], []
            dgain = jnp.zeros((1, LANES), F32)
            for b in rows:
                q = q_ref[b, sl, :].astype(F32) * 0.125
                k = k_ref[b, sl, :]
                yh = yh_ref[b, sl, :]
                rstd = rstd_ref[b, sl, :]
                do = do_ref[b, sl, :].astype(F32)
                g = g_ref[b, sl, :].astype(F32)
                sg = _sigmoid(g)
                sil = g * sg
                dyh = do * gain * sil
                dg_ref[b, sl, :] = (do * yh * gain * sg * (1.0 + g * (1.0 - sg))).astype(BF16)
                dgain = dgain + jnp.sum(do * yh * sil, axis=0, keepdims=True)
                dy = rstd * (dyh - _head_mean(dyh, m0) - yh * _head_mean(dyh * yh, m0))
                dyb = dy.astype(BF16)
                dy_ref[b, sl, :] = dyb
                rf_ref[b, n] = rfs[b]
                qs.append(q)
                ks.append(k)
                vs.append(v_ref[b, sl, :])
                dys.append(dy)
                dybs.append(dyb)
                q01.append(_split_rows(q, m0))
                k01.append(_split_rows(k.astype(F32), m0))
                dy01.append(_split_rows(dy, m0))
            s01 = [_dot_nt(q01[b], ks[b]) for b in rows]
            da01 = [_dot_nt(dy01[b], vs[b]) for b in rows]
            rbn = [rb_ref[b, n] for b in rows]
            states = [jnp.concatenate([rfs[b], rbn[b]], axis=0).astype(BF16) for b in rows]
            dqc = [_dot_nt(dybs[b], states[b]) for b in rows]
            gbb = [gbs[b].astype(BF16) for b in rows]
            dkb = [_dot_nt(vs[b], gbb[b]) for b in rows]
            qfb = [jnp.concatenate([qs[b] * tb["qdec_f"], qs[b] * tb["qdec_b"]], axis=1) for b in rows]
            direct = [_dot_tn(qfb[b].astype(BF16), dybs[b]) for b in rows]
            ds_cat, ds_rows, a_rows = [], [], []
            for b in rows:
                a0 = s01[b][0:c] * tb["d0"]
                a1 = s01[b][c:] * tb["d1"]
                pa0[...] += da01[b][0:c] * a0
                pa1[...] += da01[b][c:] * a1
                ds0 = da01[b][0:c] * tb["d0"]
                ds1 = da01[b][c:] * tb["d1"]
                ds_cat.append(jnp.concatenate([ds0, ds1], axis=1).astype(BF16))
                ds_rows.append(jnp.concatenate([ds0, ds1], axis=0).astype(BF16))
                a_rows.append(jnp.concatenate([a0, a1], axis=0).astype(BF16))
            kbd = [ks[b].astype(F32) * tb["kdec_b"] for b in rows]
            dq_in = [_dot(ds_cat[b], k01[b]) for b in rows]
            dk_in = [_dot_tn(ds_rows[b], q01[b]) for b in rows]
            dv_in = [_dot_tn(a_rows[b], dy01[b]) for b in rows]
            dv_gb = [_dot(kbd[b].astype(BF16), gbb[b]) for b in rows]
            new_rf, new_gb = [], []
            dlf = jnp.zeros((1, LANES), F32)
            dlb = jnp.zeros((1, LANES), F32)
            for b in rows:
                dqf, dqb = dqc[b][:, 0:LANES], dqc[b][:, LANES:]
                qf, qb = qfb[b][:, 0:LANES], qfb[b][:, LANES:]
                dq = dq_in[b] + dqf * tb["qdec_f"] + dqb * tb["qdec_b"]
                dq_ref[b, sl, :] = (dq * 0.125).astype(BF16)
                dk_acc[b, sl, :] = dk_in[b] + dkb[b] * tb["kdec_b"]
                dv_acc[b, sl, :] = dv_in[b] + dv_gb[b]
                dlf = dlf + jnp.sum((row + 1.0) * qf * dqf, axis=0, keepdims=True)
                dlb = dlb + jnp.sum((c - row) * qb * dqb + row * kbd[b] * dkb[b], axis=0, keepdims=True)
                dlb = dlb + c * tb["cdec_b"] * jnp.sum(gbs[b] * rbn[b], axis=0, keepdims=True)
                dirf_ref[b, n] = jnp.where(bd, direct[b][0:LANES], 0.0)
                new_gb.append(jnp.where(bd, direct[b][LANES:], 0.0) + tb["cdec_b"] * gbs[b])
                new_rf.append(rfs[b] * tb["cdec_f"] + kvf_ref[b, n])
            vec_ref[0:1, :] += dlf
            vec_ref[1:2, :] += dlb
            vec_ref[6:7, :] += dgain
            return tuple(new_rf), tuple(new_gb)

        _loop_pairs(n_chunk, sweep_fwd, (zero_states, zero_states))

        def sweep_bwd(i, gfs):
            n = n_chunk - 1 - i
            sl = pl.ds(pl.multiple_of(n * c, c), c)
            gfb = [gfs[b].astype(BF16) for b in rows]
            kfd = [k_ref[b, sl, :].astype(F32) * tb["kdec_f"] for b in rows]
            dkf = [_dot_nt(v_ref[b, sl, :], gfb[b]) for b in rows]
            dvf = [_dot(kfd[b].astype(BF16), gfb[b]) for b in rows]
            new = []
            dlf = jnp.zeros((1, LANES), F32)
            for b in rows:
                dk_ref[b, sl, :] = (dk_acc[b, sl, :] + dkf[b] * tb["kdec_f"]).astype(BF16)
                dv_ref[b, sl, :] = (dv_acc[b, sl, :] + dvf[b]).astype(BF16)
                dlf = dlf + jnp.sum((c - 1.0 - row) * kfd[b] * dkf[b], axis=0, keepdims=True)
                dlf = dlf + c * tb["cdec_f"] * jnp.sum(gfs[b] * rf_ref[b, n], axis=0, keepdims=True)
                new.append(dirf_ref[b, n] + tb["cdec_f"] * gfs[b])
            vec_ref[0:1, :] += dlf
            return tuple(new)

        lax.fori_loop(0, n_chunk, sweep_bwd, zero_states, unroll=2)
        vec_ref[2:3, :] = jnp.sum(pa0[...] * wf, axis=0, keepdims=True)
        vec_ref[3:4, :] = jnp.sum(pa1[...] * wf, axis=0, keepdims=True)
        vec_ref[4:5, :] = jnp.sum(pa0[...] * wb, axis=0, keepdims=True)
        vec_ref[5:6, :] = jnp.sum(pa1[...] * wb, axis=0, keepdims=True)
        part_ref[...] = vec_ref[...]

    blk, lane, gain, pair = _ret_specs(bsz, s)
    out_bf = jax.ShapeDtypeStruct((bsz, s, RET_WIDTH), BF16)
    state = pltpu.VMEM((bsz, n_chunk, LANES, LANES), F32)
    saved = _ret_state_spec(bsz, n_chunk)[0]
    return _hosted_call(
        body, "ret_bwd", (4,),
        in_specs=[blk(CB_RQ), blk(CB_RK), blk(CB_RV), blk(CB_RG), pair, pair, pair, lane, lane, gain, saved, saved],
        out_specs=[pair, pair, pair, pair, pl.BlockSpec((None, 8, LANES), lambda p: (p, 0, 0))],
        out_shape=[out_bf, out_bf, out_bf, out_bf, jax.ShapeDtypeStruct((4, 8, LANES), F32)],
        scratch_shapes=[state, state,
                        pltpu.VMEM((bsz, s, LANES), BF16), pltpu.VMEM((bsz, s, LANES), F32),
                        pltpu.VMEM((bsz, s, LANES), F32),
                        pltpu.VMEM((c, c), F32), pltpu.VMEM((c, c), F32), pltpu.VMEM((8, LANES), F32)],
        operands=(u3, u3, u3, u3, y_hat, y_rstd, d_o, lgf_l, lgb_l, gn_gain, *states), rider=rider)


def _attn_window_tables(n, s):
    qi = lax.broadcasted_iota(jnp.int32, (CHUNK, 3 * CHUNK), 0)
    kj = lax.broadcasted_iota(jnp.int32, (CHUNK, 3 * CHUNK), 1)
    dist = jnp.abs(kj - CHUNK - qi)
    kpos = n * CHUNK - CHUNK + kj
    valid = (dist <= CHUNK) & (kpos >= 0) & (kpos < s)
    return dist.astype(F32), valid


def _dup_kv_head(x, g):
    lane = lax.broadcasted_iota(jnp.int32, x.shape, 1)
    keep = (lane < HEAD_DIM) == (g == 0)
    xf = x.astype(F32)
    return jnp.where(keep, xf, pltpu.roll(xf, HEAD_DIM, 1))


def _attn_specs(s):
    q = pl.BlockSpec((None, s, 2 * LANES), lambda b, g: (b, 0, CB_AQ // 2 + g))
    k = pl.BlockSpec((None, s, LANES), lambda b, g: (b, 0, CB_AK))
    v = pl.BlockSpec((None, s, LANES), lambda b, g: (b, 0, CB_AV))
    grp = pl.BlockSpec((None, s, 2 * LANES), lambda b, g: (b, 0, g))
    smem = pl.BlockSpec(memory_space=pltpu.SMEM)
    return q, k, v, grp, smem


def _fill_padded(dst_ref, val, s):
    dst_ref[0:CHUNK, :] = jnp.zeros((CHUNK, LANES), dst_ref.dtype)
    dst_ref[CHUNK:CHUNK + s, :] = val.astype(dst_ref.dtype)
    dst_ref[CHUNK + s:2 * CHUNK + s, :] = jnp.zeros((CHUNK, LANES), dst_ref.dtype)


def _attn_probs(sc, slope, snk, dist, valid):
    sc = jnp.where(valid, sc - slope * dist, NEG_INF)
    m = jnp.maximum(jnp.max(sc, axis=1, keepdims=True), snk)
    e = jnp.exp(sc - m)
    es = jnp.exp(snk - m)
    inv = 1.0 / (jnp.sum(e, axis=1, keepdims=True) + es)
    return e * inv, es * inv


def _stack_heads(x2, m0):
    parts = []
    for pr in range(2):
        xp = x2[:, pr * LANES:(pr + 1) * LANES]
        parts += [jnp.where(m0, xp, 0.0), jnp.where(m0, 0.0, xp)]
    return jnp.concatenate(parts, axis=0).astype(BF16)


def _unstack_pair(x_all, pr, m0):
    return jnp.where(m0, x_all[(2 * pr) * CHUNK:(2 * pr + 1) * CHUNK], x_all[(2 * pr + 1) * CHUNK:(2 * pr + 2) * CHUNK])


def _attn_saved_specs(bsz, n_blk):
    specs = [pl.BlockSpec((None, None, n_blk, 4 * CHUNK, w), lambda b, g: (b, g, 0, 0, 0)) for w in (3 * CHUNK, 1)]
    shapes = [jax.ShapeDtypeStruct((bsz, 2, n_blk, 4 * CHUNK, 3 * CHUNK), BF16),
              jax.ShapeDtypeStruct((bsz, 2, n_blk, 4 * CHUNK, 1), F32)]
    return specs, shapes


def _attn_fwd(u3, slopes, sink, rider=None):
    bsz, s, _ = u3.shape
    n_blk = s // CHUNK

    def body(slope_ref, sink_ref, q_ref, k_ref, v_ref, o_ref, p_ref, ps_ref, kp_ref, vp_ref):
        g = pl.program_id(1)
        _fill_padded(kp_ref, _dup_kv_head(k_ref[...], g), s)
        _fill_padded(vp_ref, _dup_kv_head(v_ref[...], g), s)
        m0 = lax.broadcasted_iota(jnp.int32, (CHUNK, LANES), 1) < HEAD_DIM

        def blk(n, carry):
            r0 = pl.multiple_of(n * CHUNK, CHUNK)
            kw = kp_ref[pl.ds(r0, 3 * CHUNK), :]
            vw = vp_ref[pl.ds(r0, 3 * CHUNK), :]
            dist, valid = _attn_window_tables(n, s)
            q_all = _stack_heads(q_ref[pl.ds(r0, CHUNK), :].astype(F32) * 0.125, m0)
            sc_all = _dot_nt(q_all, kw)
            probs, sinks = [], []
            for i in range(4):
                p, ps = _attn_probs(sc_all[i * CHUNK:(i + 1) * CHUNK], slope_ref[g * 4 + i], sink_ref[g * 4 + i],
                                    dist, valid)
                probs.append(p.astype(BF16))
                sinks.append(ps)
            p_all = jnp.concatenate(probs, axis=0)
            p_ref[n] = p_all
            ps_ref[n] = jnp.concatenate(sinks, axis=0)
            out_all = _dot(p_all, vw)
            for pr in range(2):
                o_ref[pl.ds(r0, CHUNK), pr * LANES:(pr + 1) * LANES] = _unstack_pair(out_all, pr, m0).astype(BF16)
            return carry

        lax.fori_loop(0, n_blk, blk, 0, unroll=2)

    q, k, v, grp, smem = _attn_specs(s)
    saved_specs, saved_shapes = _attn_saved_specs(bsz, n_blk)
    return _hosted_call(
        body, "attn_fwd", (bsz, 2),
        in_specs=[smem, smem, q, k, v],
        out_specs=[grp] + saved_specs,
        out_shape=[jax.ShapeDtypeStruct((bsz, s, ATTN_WIDTH), BF16)] + saved_shapes,
        scratch_shapes=[pltpu.VMEM((s + 2 * CHUNK, LANES), BF16), pltpu.VMEM((s + 2 * CHUNK, LANES), BF16)],
        operands=(slopes, sink, u3, u3, u3), rider=rider)


def _attn_bwd(u3, d_o, probs, sink_probs, rider=None):
    bsz, s, _ = u3.shape
    n_blk = s // CHUNK

    def body(q_ref, k_ref, v_ref, do_ref, p_ref, ps_ref, dq_ref, dkv_ref, ds_ref,
             kp_ref, vp_ref, dk_acc, dv_acc):
        g = pl.program_id(1)
        _fill_padded(kp_ref, _dup_kv_head(k_ref[...], g), s)
        _fill_padded(vp_ref, _dup_kv_head(v_ref[...], g), s)
        dk_acc[...] = jnp.zeros_like(dk_acc)
        dv_acc[...] = jnp.zeros_like(dv_acc)
        m0 = lax.broadcasted_iota(jnp.int32, (CHUNK, LANES), 1) < HEAD_DIM

        def blk(n, dsink):
            r0 = pl.multiple_of(n * CHUNK, CHUNK)
            win = pl.ds(r0, 3 * CHUNK)
            kw = kp_ref[win, :]
            vw = vp_ref[win, :]
            q_all = _stack_heads(q_ref[pl.ds(r0, CHUNK), :].astype(F32) * 0.125, m0)
            do_all = _stack_heads(do_ref[pl.ds(r0, CHUNK), :].astype(F32), m0)
            p_all = p_ref[n]
            ps_all = ps_ref[n]
            dp_all = _dot_nt(do_all, vw)
            new_dsink, dscs = [], []
            for i in range(4):
                rows = slice(i * CHUNK, (i + 1) * CHUNK)
                p = p_all[rows].astype(F32)
                dp = dp_all[rows]
                delta = jnp.sum(p * dp, axis=1, keepdims=True)
                dscs.append((p * (dp - delta)).astype(BF16))
                dsh = jnp.sum(ps_all[rows] * delta, axis=0, keepdims=True)
                new_dsink.append(dsink[i] - jnp.broadcast_to(dsh, (1, LANES)))
            dsc_all = jnp.concatenate(dscs, axis=0)
            dq_all = _dot(dsc_all, kw)
            dk_acc[win, :] += _dot_tn(dsc_all, q_all)
            dv_acc[win, :] += _dot_tn(p_all, do_all)
            for pr in range(2):
                dq_ref[pl.ds(r0, CHUNK), pr * LANES:(pr + 1) * LANES] = (
                    _unstack_pair(dq_all, pr, m0) * 0.125).astype(BF16)
            return tuple(new_dsink)

        dsink = _loop_pairs(n_blk, blk, tuple(jnp.zeros((1, LANES), F32) for _ in range(4)))
        dk = dk_acc[CHUNK:CHUNK + s, :]
        dv = dv_acc[CHUNK:CHUNK + s, :]
        lane = lax.broadcasted_iota(jnp.int32, (s, LANES), 1)
        fold = lambda a: a + pltpu.roll(a, HEAD_DIM, 1)
        dkv_ref[...] = jnp.where(lane < HEAD_DIM, fold(dk), fold(dv)).astype(BF16)
        ds_ref[...] = jnp.zeros_like(ds_ref)
        for i in range(4):
            ds_ref[i:i + 1, :] = dsink[i]

    q, k, v, grp, _ = _attn_specs(s)
    return _hosted_call(
        body, "attn_bwd", (bsz, 2),
        in_specs=[q, k, v, grp] + _attn_saved_specs(bsz, n_blk)[0],
        out_specs=[grp, pl.BlockSpec((None, s, LANES), lambda b, g: (b, 0, g)),
                   pl.BlockSpec((None, None, 8, LANES), lambda b, g: (b, g, 0, 0))],
        out_shape=[jax.ShapeDtypeStruct((bsz, s, ATTN_WIDTH), BF16), jax.ShapeDtypeStruct((bsz, s, 2 * LANES), BF16),
                   jax.ShapeDtypeStruct((bsz, 2, 8, LANES), F32)],
        scratch_shapes=[pltpu.VMEM((s + 2 * CHUNK, LANES), BF16), pltpu.VMEM((s + 2 * CHUNK, LANES), BF16),
                        pltpu.VMEM((s + 2 * CHUNK, LANES), F32), pltpu.VMEM((s + 2 * CHUNK, LANES), F32)],
        operands=(u3, u3, u3, d_o, probs, sink_probs), rider=rider)


def _ffn_bwd(dz2, gs, us, pg, ple, zh1, r1, g1, wg4, wu4, wd4, wpg, w_out):
    t = dz2.shape[0]
    tm = 256
    wg_t, wu_t, wd_all = (w.reshape(FFN, D_MODEL) for w in (wg4, wu4, wd4))

    def body(dz_ref, gs_ref, us_ref, pg_ref, ple_ref, zh_ref, r_ref, g1_ref,
             wg_hbm, wu_hbm, wd_hbm, wpg_hbm, wo_hbm,
             dgs_ref, dus_ref, dsp_ref, dple_ref, dz1_ref, dyr_ref, dya_ref, dg1_ref, db1_ref,
             wg, wu, wd, wpg, wo, wsem):
        step = pl.program_id(0)
        loads = _resident_quarters(wd_hbm, wd) + _resident_quarters(wg_hbm, wg) + _resident_quarters(wu_hbm, wu)
        _load_resident(step, loads + [(wpg_hbm, wpg), (wo_hbm, wo)], wsem)

        @pl.when(step == 0)
        def _():
            dg1_ref[...] = jnp.zeros_like(dg1_ref)
            db1_ref[...] = jnp.zeros_like(db1_ref)

        dz = dz_ref[...]
        dzb = dz.astype(BF16)
        dh = ALPHA * dz
        pending = []
        chunks = [slice(n * FFN_CHUNK, (n + 1) * FFN_CHUNK) for n in range(N_FFN_CHUNK)]
        for n in range(N_FFN_CHUNK + 1):
            if n < N_FFN_CHUNK:
                da = _dot_nt(dzb, wd[chunks[n], :])
                gj = gs_ref[:, chunks[n]].astype(F32)
                uj = us_ref[:, chunks[n]].astype(F32)
                sg = _sigmoid(gj)
                dgj = (da * uj * sg * (1.0 + gj * (1.0 - sg))).astype(BF16)
                duj = (da * gj * sg).astype(BF16)
                dgs_ref[:, chunks[n]] = dgj
                dus_ref[:, chunks[n]] = duj
                pending.append((dgj, duj))
            if n > 0:
                dgp, dup = pending[n - 1]
                dh = dh + _dot(dgp, wg[chunks[n - 1], :]) + _dot(dup, wu[chunks[n - 1], :])
        pgv = pg_ref[...].astype(F32)
        plev = ple_ref[...].astype(F32)
        dple_ref[...] = (dz * pgv).astype(BF16)
        dsp = (dz * plev * pgv * (1.0 - pgv)).astype(BF16)
        dsp_ref[...] = dsp
        dh = dh + _dot_nt(dsp, wpg[...])
        zh = zh_ref[...]
        dg1_ref[...] += jnp.sum(dh * zh, axis=0, keepdims=True)
        db1_ref[...] += jnp.sum(dh, axis=0, keepdims=True)
        dzh = dh * g1_ref[...]
        m1 = jnp.mean(dzh, axis=1, keepdims=True)
        m2 = jnp.mean(dzh * zh, axis=1, keepdims=True)
        dz1 = r_ref[...] * (dzh - m1 - zh * m2)
        dz1_ref[...] = dz1
        dyc = _dot_nt(dz1.astype(BF16), wo[...])
        dyr_ref[...] = dyc[:, 0:RET_WIDTH].astype(BF16)
        dya_ref[...] = dyc[:, RET_WIDTH:].astype(BF16)

    row = lambda w: pl.BlockSpec((tm, w), lambda i: (i, 0))
    const = lambda s: pl.BlockSpec(s, lambda i: (0, 0))
    hbm = pl.BlockSpec(memory_space=pl.ANY)
    hid_shape = jax.ShapeDtypeStruct((t, FFN), BF16)
    return pl.pallas_call(
        body, name="ffn_bwd", grid=(t // tm,),
        in_specs=[row(D_MODEL), row(FFN), row(FFN), row(D_MODEL), row(D_MODEL), row(D_MODEL), row(1),
                  const((1, D_MODEL)), hbm, hbm, hbm, hbm, hbm],
        out_specs=[row(FFN), row(FFN), row(D_MODEL), row(D_MODEL), row(D_MODEL), row(RET_WIDTH), row(ATTN_WIDTH),
                   const((1, D_MODEL)), const((1, D_MODEL))],
        out_shape=[hid_shape, hid_shape, jax.ShapeDtypeStruct((t, D_MODEL), BF16),
                   jax.ShapeDtypeStruct((t, D_MODEL), BF16), jax.ShapeDtypeStruct((t, D_MODEL), F32),
                   jax.ShapeDtypeStruct((t, RET_WIDTH), BF16), jax.ShapeDtypeStruct((t, ATTN_WIDTH), BF16),
                   jax.ShapeDtypeStruct((1, D_MODEL), F32), jax.ShapeDtypeStruct((1, D_MODEL), F32)],
        scratch_shapes=[pltpu.VMEM((FFN, D_MODEL), BF16), pltpu.VMEM((FFN, D_MODEL), BF16),
                        pltpu.VMEM((FFN, D_MODEL), BF16),
                        pltpu.VMEM(wpg.shape, BF16), pltpu.VMEM(w_out.shape, BF16),
                        pltpu.SemaphoreType.DMA((3 * N_SHARD + 2,))],
        compiler_params=_params("arbitrary", vmem=VMEM_LIMIT),
    )(dz2, gs, us, pg, ple, zh1, r1, g1, wg_t, wu_t, wd_all, wpg, w_out)


def _wgrad_misc(y_ret, y_att, dz1, hb, dsp, p2d, dple, rider=None):
    t = dz1.shape[0]
    tk = min(t, 512)

    def body(yr_ref, ya_ref, dz_ref, hb_ref, dsp_ref, p_ref, dple_ref, wo_ref, wpg_ref, wpe_ref):
        @pl.when(pl.program_id(0) == 0)
        def _():
            wo_ref[...] = jnp.zeros_like(wo_ref)
            wpg_ref[...] = jnp.zeros_like(wpg_ref)
            wpe_ref[...] = jnp.zeros_like(wpe_ref)

        dzb = dz_ref[...].astype(BF16)
        wo_ref[0:RET_WIDTH, :] += _dot_tn(yr_ref[...], dzb)
        wo_ref[RET_WIDTH:, :] += _dot_tn(ya_ref[...], dzb)
        wpg_ref[...] += _dot_tn(hb_ref[...], dsp_ref[...])
        wpe_ref[...] += _dot_tn(p_ref[...].astype(BF16), dple_ref[...])

    row = lambda w: pl.BlockSpec((tk, w), lambda k: (k, 0))
    const = lambda s: pl.BlockSpec(s, lambda k: (0, 0))
    return _hosted_call(
        body, "wgrad_misc", (t // tk,),
        in_specs=[row(RET_WIDTH), row(ATTN_WIDTH), row(D_MODEL), row(D_MODEL), row(D_MODEL), row(PLE_DIM),
                  row(D_MODEL)],
        out_specs=[const((D_MODEL, D_MODEL)), const((D_MODEL, D_MODEL)), const((PLE_DIM, D_MODEL))],
        out_shape=[jax.ShapeDtypeStruct((D_MODEL, D_MODEL), F32), jax.ShapeDtypeStruct((D_MODEL, D_MODEL), F32),
                   jax.ShapeDtypeStruct((PLE_DIM, D_MODEL), F32)],
        scratch_shapes=[], operands=(y_ret, y_att, dz1, hb, dsp, p2d, dple), rider=rider, semantics=["arbitrary"])


def _wgrad_ffn(acts, dgs, dus, hb, dz2b):
    t = dz2b.shape[0]
    tk = min(t, 512)
    nk = t // tk

    def body(act_ref, dg_ref, du_ref, hb_ref, dz_ref, og_ref, ou_ref, od_ref):
        @pl.when(pl.program_id(1) == 0)
        def _():
            og_ref[...] = jnp.zeros_like(og_ref)
            ou_ref[...] = jnp.zeros_like(ou_ref)
            od_ref[...] = jnp.zeros_like(od_ref)

        hbv = hb_ref[...]
        og_ref[...] += _dot_tn(dg_ref[...], hbv)
        ou_ref[...] += _dot_tn(du_ref[...], hbv)
        od_ref[...] += _dot_tn(act_ref[...], dz_ref[...])

    half = FFN // 2
    a_spec = pl.BlockSpec((tk, half), lambda j, k: (k, j))
    b_spec = pl.BlockSpec((tk, D_MODEL), lambda j, k: (k, 0))
    o_spec = pl.BlockSpec((half, D_MODEL), lambda j, k: (j, 0))
    o_shape = jax.ShapeDtypeStruct((FFN, D_MODEL), F32)
    outs = pl.pallas_call(
        body, name="wgrad_ffn", grid=(2, nk),
        in_specs=[a_spec, a_spec, a_spec, b_spec, b_spec],
        out_specs=[o_spec] * 3, out_shape=[o_shape] * 3,
        compiler_params=_params("parallel", "arbitrary", vmem=VMEM_LIMIT),
    )(acts, dgs, dus, hb, dz2b)
    return [o.reshape(N_SHARD, FFN_SHARD, D_MODEL) for o in outs]


KV_ORDER = (0, 128, 64, 192)


def _wgrad_in(pieces, x2d, rider=None):
    t = x2d.shape[0]
    tk = min(t, 512)
    nk = t // tk
    kv0 = CB_AK * LANES

    def body(p0, p1, p2, p3, p4, pkv, x_ref, o_ref):
        @pl.when(pl.program_id(0) == 0)
        def _():
            o_ref[...] = jnp.zeros_like(o_ref)

        xb = x_ref[...].astype(BF16)
        for i, ref in enumerate((p0, p1, p2, p3, p4)):
            o_ref[i * 512:(i + 1) * 512, :] += _dot_tn(ref[...], xb)
        dkv = _dot_tn(pkv[...], xb)
        for i, o in enumerate(KV_ORDER):
            o_ref[kv0 + o:kv0 + o + HEAD_DIM, :] += dkv[i * HEAD_DIM:(i + 1) * HEAD_DIM]

    row = lambda w: pl.BlockSpec((tk, w), lambda k: (k, 0))
    return _hosted_call(
        body, "wgrad_in", (nk,),
        in_specs=[row(512)] * 5 + [row(256), row(D_MODEL)],
        out_specs=[pl.BlockSpec((IN_WIDTH, D_MODEL), lambda k: (0, 0))],
        out_shape=[jax.ShapeDtypeStruct((IN_WIDTH, D_MODEL), F32)],
        scratch_shapes=[], operands=(*pieces, x2d), rider=rider, semantics=["arbitrary"])


def _inproj_bwd(dz1, pieces, w_main, w_kv, rider=None):
    t = dz1.shape[0]
    tm = 512

    def body(dz_ref, p0, p1, p2, p3, p4, pkv, wm_ref, wkv_ref, o_ref):
        acc = ALPHA * dz_ref[...]
        for i, ref in enumerate((p0, p1, p2, p3, p4)):
            acc = acc + _dot(ref[...], wm_ref[i * 512:(i + 1) * 512, :])
        o_ref[...] = acc + _dot(pkv[...], wkv_ref[...])

    row = lambda w: pl.BlockSpec((tm, w), lambda i: (i, 0))
    const = lambda s: pl.BlockSpec(s, lambda i: (0, 0))
    return _hosted_call(
        body, "inproj_bwd", (t // tm,),
        in_specs=[row(D_MODEL)] + [row(512)] * 5 + [row(256), const(w_main.shape), const(w_kv.shape)],
        out_specs=[row(D_MODEL)],
        out_shape=[jax.ShapeDtypeStruct((t, D_MODEL), F32)],
        scratch_shapes=[], operands=(dz1, *pieces, w_main, w_kv), rider=rider)


def _coords():
    return lax.axis_index("x"), lax.axis_index("y"), lax.axis_index("c")


def _chip_of(x, y, rel):
    return (1 - x if rel & 2 else x), (1 - y if rel & 1 else y)


def _all_gather_weights(shards):
    first = _gather_near_rider(shards)
    later = [f(first.out_shapes, chained=True) for f in (_gather_relay_rider, _gather_pass_rider)]
    return _run_riders("gather_weights", shards, first.out_shapes, [first] + later)


def _run_riders(name, ins, out_shapes, riders):
    n_in, n_out = len(ins), len(out_shapes)

    def body(*refs):
        in_refs, out_refs = refs[:n_in], refs[n_in:n_in + n_out]
        k = n_in + n_out
        for r in riders:
            sems = refs[k:k + len(r.sems)]
            k += len(r.sems)
            r.start(in_refs, out_refs, sems)
            r.finish(in_refs, out_refs, sems)

    hbm = pl.BlockSpec(memory_space=pl.ANY)
    return pl.pallas_call(
        body, name=name, in_specs=[hbm] * n_in, out_specs=[hbm] * n_out, out_shape=list(out_shapes),
        scratch_shapes=[s for r in riders for s in r.sems],
    )(*ins)


def _gather_half(outs, w, chip, cc):
    h = outs[w].shape[1] // 2
    return outs[w].at[chip, pl.ds(cc * h, h), :]


NEAR = (1, 2)


def _gather_near_rider(shards):
    nw = len(shards)

    def copies(ins, outs, sems, arrivals):
        send, recv, lsend, lrecv = sems
        x, y, c = _coords()
        me = 2 * x + y
        own = [pltpu.make_async_remote_copy(
            src_ref=ins[w], dst_ref=outs[w].at[me], send_sem=lsend.at[w], recv_sem=lrecv.at[w],
            device_id=(x, y, 1 - c), device_id_type=MESH) for w in range(nw)]
        out, arrive = [], []
        for rel in NEAR:
            kx, ky = _chip_of(x, y, rel)
            for w in range(nw):
                h = shards[w].shape[0] // 2
                sem = dict(send_sem=send.at[w * 2 + rel - 1], recv_sem=recv.at[w * 2 + rel - 1],
                           device_id=(kx, ky, c), device_id_type=MESH)
                out.append(pltpu.make_async_remote_copy(
                    src_ref=ins[w].at[pl.ds(c * h, h), :], dst_ref=_gather_half(outs, w, me, c), **sem))
                if arrivals:
                    theirs = _gather_half(outs, w, 2 * kx + ky, c)
                    arrive.append(pltpu.make_async_remote_copy(src_ref=theirs, dst_ref=theirs, **sem))
        return own, out, arrive

    def start(ins, outs, sems):
        own, out, _ = copies(ins, outs, sems, arrivals=False)
        for cp in own + out:
            cp.start()

    def finish(ins, outs, sems):
        own, out, arrive = copies(ins, outs, sems, arrivals=True)
        for cp in arrive:
            cp.wait_recv()
        for cp in out:
            cp.wait_send()
        for cp in own:
            cp.wait()

    dma = pltpu.SemaphoreType.DMA
    return _Rider(shards, [jax.ShapeDtypeStruct((N_SHARD,) + s.shape, s.dtype) for s in shards],
                  [dma((2 * nw,)), dma((2 * nw,)), dma((nw,)), dma((nw,))], start, finish)


def _gather_relay_rider(gathered, chained=False):
    nw = len(gathered)

    def quarter(outs, w, chip, c, p):
        q = outs[w].shape[1] // 4
        return outs[w].at[chip, pl.ds(c * 2 * q + p * q, q), :]

    def copies(outs, sems):
        send, recv = sems
        x, y, c = _coords()
        (yx, yy), (xx, xy), (dx, dy) = (_chip_of(x, y, rel) for rel in (1, 2, 3))
        out, arrive = [], []
        for w in range(nw):
            for p, (src_chip, dst) in enumerate(((2 * xx + xy, (yx, yy)), (2 * yx + yy, (xx, xy)))):
                rows = quarter(outs, w, src_chip, c, p)
                sem = dict(send_sem=send.at[w * 2 + p], recv_sem=recv.at[w * 2 + p], device_id_type=MESH)
                out.append(pltpu.make_async_remote_copy(src_ref=rows, dst_ref=rows, device_id=(*dst, c), **sem))
                mine = quarter(outs, w, 2 * dx + dy, c, p)
                arrive.append(pltpu.make_async_remote_copy(src_ref=mine, dst_ref=mine, device_id=(*dst, c), **sem))
        return out, arrive

    def start(ins, outs, sems):
        for cp in copies(outs, sems)[0]:
            cp.start()

    def finish(ins, outs, sems):
        out, arrive = copies(outs, sems)
        for cp in arrive:
            cp.wait_recv()
        for cp in out:
            cp.wait_send()

    dma = pltpu.SemaphoreType.DMA
    shapes = [jax.ShapeDtypeStruct(g.shape, g.dtype) for g in gathered]
    if chained:
        return _Rider([], [], [dma((2 * nw,)), dma((2 * nw,))], start, finish)
    return _Rider(gathered, shapes, [dma((2 * nw,)), dma((2 * nw,))], start, finish,
                  aliases={w: w for w in range(nw)})


def _gather_pass_rider(gathered, chained=False):
    nw = len(gathered)

    def copies(outs, sems, cc):
        send, recv = sems
        x, y, c = _coords()
        res = []
        for rel in (1, 2, 3):
            kx, ky = _chip_of(x, y, rel)
            for w in range(nw):
                rows = _gather_half(outs, w, 2 * kx + ky, cc)
                res.append(pltpu.make_async_remote_copy(
                    src_ref=rows, dst_ref=rows, send_sem=send.at[w * 3 + rel - 1], recv_sem=recv.at[w * 3 + rel - 1],
                    device_id=(x, y, 1 - c), device_id_type=MESH))
        return res

    def start(ins, outs, sems):
        for cp in copies(outs, sems, lax.axis_index("c")):
            cp.start()

    def finish(ins, outs, sems):
        c = lax.axis_index("c")
        for cp in copies(outs, sems, 1 - c):
            cp.wait_recv()
        for cp in copies(outs, sems, c):
            cp.wait_send()

    dma = pltpu.SemaphoreType.DMA
    shapes = [jax.ShapeDtypeStruct(g.shape, g.dtype) for g in gathered]
    if chained:
        return _Rider([], [], [dma((3 * nw,)), dma((3 * nw,))], start, finish)
    return _Rider(gathered, shapes, [dma((3 * nw,)), dma((3 * nw,))], start, finish,
                  aliases={w: w for w in range(nw)})


def _exchange_halves_rider(parts):
    nw = len(parts)

    def copies(ins, outs, sems):
        send, recv = sems
        x, y, c = _coords()
        res = []
        for w in range(nw):
            h = parts[w].shape[1] // 2
            res.append(pltpu.make_async_remote_copy(
                src_ref=ins[w].at[:, pl.ds((1 - c) * h, h), :], dst_ref=outs[w],
                send_sem=send.at[w], recv_sem=recv.at[w], device_id=(x, y, 1 - c), device_id_type=MESH))
        return res

    def start(ins, outs, sems):
        for cp in copies(ins, outs, sems):
            cp.start()

    def finish(ins, outs, sems):
        for cp in copies(ins, outs, sems):
            cp.wait()

    dma = pltpu.SemaphoreType.DMA
    return _Rider(parts, [jax.ShapeDtypeStruct((N_SHARD, p.shape[1] // 2, p.shape[2]), p.dtype) for p in parts],
                  [dma((nw,)), dma((nw,))], start, finish)


def _add_halves(parts, theirs, pos):
    nw = len(parts)
    split = 2

    def body(pos_ref, *refs):
        ins, oth = refs[:nw], refs[nw:2 * nw]
        o32, o16 = refs[2 * nw:3 * nw], refs[3 * nw:]
        sums = [ins[w][...] + oth[w][...].astype(F32) for w in range(nw)]
        for w in range(nw):
            o16[w][...] = sums[w].astype(BF16)

        @pl.when(pl.program_id(1) == pos_ref[0])
        def _():
            for w in range(nw):
                o32[w][...] = sums[w]

    in_specs, oth_specs, o32_specs, shapes32, shapes16 = [], [], [], [], []
    for p in parts:
        hb = p.shape[1] // 2 // split
        blk = (None, hb, p.shape[2])
        in_specs.append(pl.BlockSpec(blk, lambda i, j, pos_ref: (j, pos_ref[1] * split + i, 0)))
        oth_specs.append(pl.BlockSpec(blk, lambda i, j, pos_ref: (j, i, 0)))
        o32_specs.append(pl.BlockSpec((hb, p.shape[2]), lambda i, j, pos_ref: (i, 0)))
        shapes32.append(jax.ShapeDtypeStruct((p.shape[1] // 2, p.shape[2]), F32))
        shapes16.append(jax.ShapeDtypeStruct((N_SHARD, p.shape[1] // 2, p.shape[2]), BF16))
    return pl.pallas_call(
        body, name="add_halves",
        grid_spec=pltpu.PrefetchScalarGridSpec(
            num_scalar_prefetch=1, grid=(split, N_SHARD),
            in_specs=in_specs + oth_specs, out_specs=o32_specs + oth_specs),
        out_shape=shapes32 + shapes16,
        compiler_params=_params("parallel", "arbitrary", vmem=VMEM_LIMIT),
    )(pos, *parts, *theirs)


def _exchange_chips_rider(sums16):
    nw = len(sums16)

    def copies(ins, outs, sems):
        send, recv = sems
        x, y, c = _coords()
        res = []
        for rel in (1, 2, 3):
            kx, ky = _chip_of(x, y, rel)
            for w in range(nw):
                res.append(pltpu.make_async_remote_copy(
                    src_ref=ins[w].at[2 * kx + ky], dst_ref=outs[w].at[rel - 1],
                    send_sem=send.at[w * 3 + rel - 1], recv_sem=recv.at[w * 3 + rel - 1],
                    device_id=(kx, ky, c), device_id_type=MESH))
        return res

    def start(ins, outs, sems):
        for cp in copies(ins, outs, sems):
            cp.start()

    def finish(ins, outs, sems):
        for cp in copies(ins, outs, sems):
            cp.wait()

    dma = pltpu.SemaphoreType.DMA
    return _Rider(sums16, [jax.ShapeDtypeStruct((3,) + s.shape[1:], BF16) for s in sums16],
                  [dma((3 * nw,)), dma((3 * nw,))], start, finish)


def _add_chips(sums32, theirs, pos):
    nw = len(sums32)
    split = 2

    def body(pos_ref, *refs):
        ins, oth, outs = refs[:nw], refs[nw:2 * nw], refs[2 * nw:]
        for w in range(nw):
            acc = ins[w][...]
            for r in range(3):
                acc = acc + oth[w][r].astype(F32)
            outs[w][...] = acc

    in_specs, oth_specs, out_specs, shapes = [], [], [], []
    for s in sums32:
        hb = s.shape[0] // split
        in_specs.append(pl.BlockSpec((hb, s.shape[1]), lambda i, pos_ref: (i, 0)))
        oth_specs.append(pl.BlockSpec((3, hb, s.shape[1]), lambda i, pos_ref: (0, i, 0)))
        out_specs.append(pl.BlockSpec((hb, s.shape[1]), lambda i, pos_ref: (pos_ref[1] * split + i, 0)))
        shapes.append(jax.ShapeDtypeStruct((2 * s.shape[0], s.shape[1]), F32))
    return pl.pallas_call(
        body, name="add_chips",
        grid_spec=pltpu.PrefetchScalarGridSpec(
            num_scalar_prefetch=1, grid=(split,), in_specs=in_specs + oth_specs, out_specs=out_specs),
        out_shape=shapes,
        compiler_params=_params("parallel", vmem=VMEM_LIMIT),
    )(pos, *sums32, *theirs)


def _join_halves(shards):
    nw = len(shards)

    def body(*refs):
        outs = refs[nw:2 * nw]
        send, recv = refs[2 * nw:]
        x, y, c = _coords()

        def copy(w, cc):
            h = shards[w].shape[0] // 2
            rows = outs[w].at[pl.ds(cc * h, h), :]
            return pltpu.make_async_remote_copy(
                src_ref=rows, dst_ref=rows, send_sem=send.at[w], recv_sem=recv.at[w],
                device_id=(x, y, 1 - c), device_id_type=MESH)

        for w in range(nw):
            copy(w, c).start()
        for w in range(nw):
            copy(w, 1 - c).wait_recv()
            copy(w, c).wait_send()

    hbm = pl.BlockSpec(memory_space=pl.ANY)
    return pl.pallas_call(
        body, name="join_halves",
        in_specs=[hbm] * nw, out_specs=[hbm] * nw,
        out_shape=[jax.ShapeDtypeStruct(s.shape, F32) for s in shards],
        input_output_aliases={w: w for w in range(nw)},
        scratch_shapes=[pltpu.SemaphoreType.DMA((nw,)), pltpu.SemaphoreType.DMA((nw,))],
    )(*shards)


def _adamw_math(w, g, m, v):
    m = ADAM_B1 * m + (1.0 - ADAM_B1) * g
    v = ADAM_B2 * v + (1.0 - ADAM_B2) * (g * g)
    m_hat = m / (1.0 - ADAM_B1 ** ADAM_STEP)
    v_hat = v / (1.0 - ADAM_B2 ** ADAM_STEP)
    delta = -ADAM_LR * (m_hat / (jnp.sqrt(v_hat) + ADAM_EPS) + ADAM_WD * w)
    return delta, m, v


def _adamw(ws, gs, ms, vs):
    nw = len(ws)
    split = 8

    def body(*refs):
        w_r, g_r, m_r, v_r = (refs[i * nw:(i + 1) * nw] for i in range(4))
        g_o, d_o, m_o, v_o = (refs[(4 + i) * nw:(5 + i) * nw] for i in range(4))
        for k in range(nw):
            g = g_r[k][...]
            d, m, v = _adamw_math(w_r[k][...], g, m_r[k][...], v_r[k][...])
            g_o[k][...] = g
            d_o[k][...] = d
            m_o[k][...] = m
            v_o[k][...] = v

    specs = [pl.BlockSpec((w.shape[0] // split, w.shape[1]), lambda i: (i, 0)) for w in ws]
    shapes = [jax.ShapeDtypeStruct(w.shape, F32) for w in ws]
    outs = pl.pallas_call(
        body, name="adamw", grid=(split,),
        in_specs=specs * 4, out_specs=specs * 4, out_shape=shapes * 4,
        compiler_params=_params("parallel", vmem=VMEM_LIMIT),
    )(*ws, *gs, *ms, *vs)
    return outs[:nw], outs[nw:2 * nw], outs[2 * nw:3 * nw], outs[3 * nw:]


SMALL_ROWS = 8
SMALL_COLS = D_MODEL
LOSS_COL = RET_WIDTH + 24


def _small_allreduce_adamw(part, w, m, v, rider=None):
    def body(part_ref, w_ref, m_ref, v_ref, g_out, d_out, m_out, v_out, all_ref, send, recv):
        x, y, c = _coords()
        me = 4 * x + 2 * y + c
        all_ref[me] = part_ref[...]
        copies = []
        for rel in range(1, 8):
            px = 1 - x if rel & 4 else x
            py = 1 - y if rel & 2 else y
            pc = 1 - c if rel & 1 else c
            copies.append(pltpu.make_async_remote_copy(
                src_ref=part_ref, dst_ref=all_ref.at[me],
                send_sem=send.at[rel - 1], recv_sem=recv.at[rel - 1], device_id=(px, py, pc), device_id_type=MESH))
        for cp in copies:
            cp.start()
        for cp in copies:
            cp.wait()
        g = all_ref[0]
        for k in range(1, 8):
            g = g + all_ref[k]
        d, mn, vn = _adamw_math(w_ref[...], g, m_ref[...], v_ref[...])
        g_out[...] = g
        d_out[...] = d
        m_out[...] = mn
        v_out[...] = vn

    vm = pl.BlockSpec(memory_space=pltpu.VMEM)
    shape = jax.ShapeDtypeStruct((SMALL_ROWS, SMALL_COLS), F32)
    return _hosted_call(
        body, "small_allreduce_adamw", (1,),
        in_specs=[vm] * 4, out_specs=[vm] * 4, out_shape=[shape] * 4,
        scratch_shapes=[pltpu.VMEM((8, SMALL_ROWS, SMALL_COLS), F32),
                        pltpu.SemaphoreType.DMA((7,)), pltpu.SemaphoreType.DMA((7,))],
        operands=(part, w, m, v), rider=rider, semantics=["arbitrary"])


SMALL_NAMES = ("ret_decay_fwd", "ret_decay_bwd", "attn_sink", "ret_gn_gain",
               "ln1_gain", "ln1_bias", "ln2_gain", "ln2_bias")


LN_NAMES = ("ln1_gain", "ln1_bias", "ln2_gain", "ln2_bias")


def _pack_small(vals, extra=None):
    tail = jnp.zeros((1, 1), F32) if extra is None else extra.reshape(1, 1)
    row4 = jnp.concatenate([vals["ret_gn_gain"], vals["ret_decay_fwd"], vals["ret_decay_bwd"], vals["attn_sink"],
                            tail, jnp.zeros((1, SMALL_COLS - LOSS_COL - 1), F32)], axis=1)
    rows = [vals[n] for n in LN_NAMES] + [row4, jnp.zeros((SMALL_ROWS - 5, SMALL_COLS), F32)]
    return jnp.concatenate(rows, axis=0)


def _unpack_small(packed):
    out = {n: packed[i:i + 1] for i, n in enumerate(LN_NAMES)}
    o = RET_WIDTH
    out.update(ret_gn_gain=packed[4:5, 0:o], ret_decay_fwd=packed[4:5, o:o + 8],
               ret_decay_bwd=packed[4:5, o + 8:o + 16], attn_sink=packed[4:5, o + 16:o + 24])
    return out


def _local_step(x, p, tgt, w_in_t, rest, small, pos=None, small_state=None):
    bsz, s, _ = x.shape
    t = bsz * s
    x2d = x.reshape(t, D_MODEL)
    p2d = p.reshape(t, PLE_DIM)
    tgt2d = tgt.reshape(t, D_MODEL)
    dec_f = small["ret_decay_fwd"].reshape(8)
    dec_b = small["ret_decay_bwd"].reshape(8)
    lg_f = jnp.log1p(-jnp.exp2(dec_f))
    lg_b = jnp.log1p(-jnp.exp2(dec_b))
    per_lane = lambda v: jnp.repeat(v, HEAD_DIM).reshape(4, 1, LANES)
    lgf_l, lgb_l = per_lane(lg_f), per_lane(lg_b)
    sink = small["attn_sink"].reshape(8)
    slopes = 2.0 ** (-(jnp.arange(8, dtype=F32) + 1.0))
    gn_gain = small["ret_gn_gain"]
    g1, b1, g2, b2 = (small[n] for n in ("ln1_gain", "ln1_bias", "ln2_gain", "ln2_bias"))

    dist = pos is not None
    shard = dict(zip(REST_NAMES, rest)) if dist else {}
    near = lambda names: _gather_near_rider([shard[n] for n in names])
    wave1, wave2 = ("w_out", "w_ple_gate", "w_ffn_gate"), ("w_ffn_up", "w_ffn_down", "w_ple_proj")
    n1 = len(wave1)
    u, *o1 = _inproj(x2d, w_in_t, rider=near(wave1) if dist else None)
    u3 = u.reshape(bsz, s, IN_WIDTH)
    y_hat, y_rstd, y_ret, ret_rb, ret_kvf, *o2 = _ret_fwd(u3, lgf_l, lgb_l, gn_gain, rider=_merge_riders(
        [_gather_relay_rider(o1), near(wave2)]) if dist else None)
    y_att, att_p, att_ps, *o3 = _attn_fwd(u3, slopes, sink, rider=_merge_riders(
        [_gather_pass_rider(o2[:n1]), _gather_relay_rider(o2[n1:])]) if dist else None)
    gathered = dict(zip(wave1, o3[:n1]))
    w_out = _assemble_weights({"w_out": gathered["w_out"]})["w_out"] if dist else rest["w_out"]
    zh1, r1, hb, *o4 = _outproj_ln1(y_ret.reshape(t, RET_WIDTH), y_att.reshape(t, ATTN_WIDTH), x2d, w_out, g1, b1,
                                    rider=_gather_pass_rider(o3[n1:]) if dist else None)
    gathered.update(zip(wave2, o4))
    wts = _assemble_weights(gathered) if dist else rest
    dz2, dz2b, gs, us, acts, pg, ple, sq, dg2, db2 = _ffn_fwd(
        zh1, hb, p2d, tgt2d, g1, b1, g2, b2, wts["gate4"], wts["up4"], wts["down4"], wts["ple_proj"], wts["ple_gate"])
    dgs, dus, dsp, dple, dz1, dyr, dya, dg1, db1 = _ffn_bwd(dz2, gs, us, pg, ple, zh1, r1, g1, wts["gate4"],
                                                          wts["up4"], wts["down4"], wts["ple_gate"], wts["w_out"])
    ffn_parts = list(_wgrad_ffn(acts, dgs, dus, hb, dz2b))
    d_w_out, d_ple_gate, d_ple_proj, *th_ffn = _wgrad_misc(
        y_ret.reshape(t, RET_WIDTH), y_att.reshape(t, ATTN_WIDTH), dz1, hb, dsp, p2d, dple,
        rider=_exchange_halves_rider(ffn_parts) if dist else None)
    misc_parts = [d_w_out.reshape(N_SHARD, D_MODEL // N_SHARD, D_MODEL),
                  d_ple_proj.reshape(PLE_DIM, N_SHARD, D_MODEL // N_SHARD).transpose(1, 0, 2),
                  d_ple_gate.reshape(N_SHARD, D_MODEL // N_SHARD, D_MODEL)]
    dyr3, dya3 = dyr.reshape(bsz, s, RET_WIDTH), dya.reshape(bsz, s, ATTN_WIDTH)
    if dist:
        s_ffn = _add_halves(ffn_parts, th_ffn, pos)
        quarter = FFN_SHARD // 4
        up_lo, up_hi = s_ffn[4][:, :quarter], s_ffn[4][:, quarter:]
        drq, drk, drv, drg, rpart, *o5 = _ret_bwd(u3, y_hat, y_rstd, (ret_rb, ret_kvf), dyr3, lgf_l, lgb_l, gn_gain,
                                                  rider=_merge_riders(
            [_exchange_chips_rider([s_ffn[3], up_lo]), _exchange_halves_rider(misc_parts)]))
        s_misc = _add_halves(misc_parts, o5[2:], pos)
        daq, dakv, spart, *o6 = _attn_bwd(u3, dya3, att_p, att_ps, rider=_exchange_chips_rider([up_hi, s_ffn[5]]))
    else:
        drq, drk, drv, drg, rpart = _ret_bwd(u3, y_hat, y_rstd, (ret_rb, ret_kvf), dyr3, lgf_l, lgb_l, gn_gain)
        daq, dakv, spart = _attn_bwd(u3, dya3, att_p, att_ps)
    pieces = [a.reshape(t, -1) for a in (drq, drk, drv, drg, daq, dakv)]
    kv0 = CB_AK * LANES
    w_kv = jnp.concatenate([w_in_t[kv0 + o:kv0 + o + HEAD_DIM] for o in KV_ORDER], axis=0)
    d_in, *o7 = _wgrad_in(pieces, x2d, rider=_exchange_chips_rider(list(s_misc[3:])) if dist else None)
    d_in = d_in.reshape(N_SHARD, FFN_SHARD, D_MODEL)

    rsum = rpart
    lane_heads = lambda row: jnp.sum(row.reshape(4, 2, HEAD_DIM), axis=-1).reshape(8)
    dlg_f = lane_heads(rsum[:, 0, :]) + jnp.stack([jnp.sum(rsum[:, 2, :], -1), jnp.sum(rsum[:, 3, :], -1)], 1).reshape(8)
    dlg_b = lane_heads(rsum[:, 1, :]) + jnp.stack([jnp.sum(rsum[:, 4, :], -1), jnp.sum(rsum[:, 5, :], -1)], 1).reshape(8)
    chain = lambda d: -(math.log(2.0) * jnp.exp2(d)) / (1.0 - jnp.exp2(d))
    grads_small = {
        "ret_decay_fwd": (dlg_f * chain(dec_f)).reshape(1, 8),
        "ret_decay_bwd": (dlg_b * chain(dec_b)).reshape(1, 8),
        "attn_sink": jnp.sum(spart, axis=0)[:, 0:4, 0].reshape(1, 8),
        "ret_gn_gain": rsum[:, 6, :].reshape(1, RET_WIDTH),
        "ln1_gain": dg1, "ln1_bias": db1, "ln2_gain": dg2, "ln2_bias": db2,
    }
    if not dist:
        grad_x, = _inproj_bwd(dz1, pieces, w_in_t[:kv0], w_kv)
        grads_rest = [misc_parts[0]] + ffn_parts + misc_parts[1:]
        return sq[0, 0], grad_x.reshape(bsz, s, D_MODEL), d_in, grads_rest, grads_small
    *small_out, th_in = _small_allreduce_adamw(_pack_small(grads_small, sq[0, 0]), *small_state,
                                               rider=_exchange_halves_rider([d_in]))
    s_in = _add_halves([d_in], [th_in], pos)
    grad_x, chips_in = _inproj_bwd(dz1, pieces, w_in_t[:kv0], w_kv, rider=_exchange_chips_rider([s_in[1]]))
    sums32 = [s_in[0], s_misc[0], s_ffn[0], s_ffn[1], s_ffn[2], s_misc[1], s_misc[2]]
    chips_up = jnp.concatenate([o5[1], o6[0]], axis=1)
    from_chips = [chips_in, o7[0], o5[0], chips_up, o6[1], o7[1], o7[2]]
    return grad_x.reshape(bsz, s, D_MODEL), sums32, from_chips, small_out


BIG_NAMES = ("w_in", "w_out", "w_ffn_gate", "w_ffn_up", "w_ffn_down", "w_ple_proj", "w_ple_gate")
REST_NAMES = BIG_NAMES[1:]
TRANSPOSED = ("w_in", "w_ffn_gate", "w_ffn_up")
WEIGHT_ORDER = ("w_in", "ret_decay_fwd", "ret_decay_bwd", "ret_gn_gain", "attn_sink", "w_out", "ln1_gain",
                "ln1_bias", "w_ffn_gate", "w_ffn_up", "w_ffn_down", "w_ple_proj", "w_ple_gate", "ln2_gain", "ln2_bias")


def _shard_rows(name, a):
    return jnp.swapaxes(a[0], 0, 1) if name in TRANSPOSED else a[0]


def _unshard_rows(name, a):
    return (jnp.swapaxes(a, 0, 1) if name in TRANSPOSED else a)[None]


def _assemble_weights(gathered):
    cols = lambda a: a.transpose(1, 0, 2).reshape(a.shape[1], N_SHARD * a.shape[2])
    rows = lambda a: a.reshape(N_SHARD * a.shape[1], a.shape[2])
    same = lambda a: a
    layout = {"w_out": ("w_out", rows), "w_ffn_gate": ("gate4", same), "w_ffn_up": ("up4", same),
              "w_ffn_down": ("down4", same), "w_ple_proj": ("ple_proj", cols), "w_ple_gate": ("ple_gate", rows)}
    return {layout[n][0]: layout[n][1](a) for n, a in gathered.items()}


def kernel(x, p, w_in, ret_decay_fwd, ret_decay_bwd, ret_gn_gain, attn_sink, w_out, ln1_gain, ln1_bias, w_ffn_gate, w_ffn_up, w_ffn_down, w_ple_proj, w_ple_gate, ln2_gain, ln2_bias, loss_target, m_w_in, m_ret_decay_fwd, m_ret_decay_bwd, m_ret_gn_gain, m_attn_sink, m_w_out, m_ln1_gain, m_ln1_bias, m_w_ffn_gate, m_w_ffn_up, m_w_ffn_down, m_w_ple_proj, m_w_ple_gate, m_ln2_gain, m_ln2_bias, v_w_in, v_ret_decay_fwd, v_ret_decay_bwd, v_ret_gn_gain, v_attn_sink, v_w_out, v_ln1_gain, v_ln1_bias, v_w_ffn_gate, v_w_ffn_up, v_w_ffn_down, v_w_ple_proj, v_w_ple_gate, v_ln2_gain, v_ln2_bias):
    w = dict(w_in=w_in, ret_decay_fwd=ret_decay_fwd, ret_decay_bwd=ret_decay_bwd, ret_gn_gain=ret_gn_gain,
             attn_sink=attn_sink, w_out=w_out, ln1_gain=ln1_gain, ln1_bias=ln1_bias, w_ffn_gate=w_ffn_gate,
             w_ffn_up=w_ffn_up, w_ffn_down=w_ffn_down, w_ple_proj=w_ple_proj, w_ple_gate=w_ple_gate,
             ln2_gain=ln2_gain, ln2_bias=ln2_bias)
    m = dict(w_in=m_w_in, ret_decay_fwd=m_ret_decay_fwd, ret_decay_bwd=m_ret_decay_bwd, ret_gn_gain=m_ret_gn_gain,
             attn_sink=m_attn_sink, w_out=m_w_out, ln1_gain=m_ln1_gain, ln1_bias=m_ln1_bias, w_ffn_gate=m_w_ffn_gate,
             w_ffn_up=m_w_ffn_up, w_ffn_down=m_w_ffn_down, w_ple_proj=m_w_ple_proj, w_ple_gate=m_w_ple_gate,
             ln2_gain=m_ln2_gain, ln2_bias=m_ln2_bias)
    v = dict(w_in=v_w_in, ret_decay_fwd=v_ret_decay_fwd, ret_decay_bwd=v_ret_decay_bwd, ret_gn_gain=v_ret_gn_gain,
             attn_sink=v_attn_sink, w_out=v_w_out, ln1_gain=v_ln1_gain, ln1_bias=v_ln1_bias, w_ffn_gate=v_w_ffn_gate,
             w_ffn_up=v_w_ffn_up, w_ffn_down=v_w_ffn_down, w_ple_proj=v_w_ple_proj, w_ple_gate=v_w_ple_gate,
             ln2_gain=v_ln2_gain, ln2_bias=v_ln2_bias)
    big = lambda d: [_shard_rows(n, d[n]) for n in BIG_NAMES]
    small = lambda d: {n: d[n] for n in SMALL_NAMES}

    chip = 2 * lax.axis_index("x") + lax.axis_index("y")
    pos = jnp.stack([chip, lax.axis_index("c")]).astype(jnp.int32)

    shards = [a.astype(BF16) for a in big(w)]
    (w_in4,) = _all_gather_weights(shards[:1])
    w_in_t = w_in4.reshape(IN_WIDTH, D_MODEL)
    grad_x, sums32, from_chips, (g_s, d_s, m_s, v_s) = _local_step(
        x, p[0], loss_target, w_in_t, shards[1:], small(w), pos=pos,
        small_state=(_pack_small(small(w)), _pack_small(small(m)), _pack_small(small(v))))
    g_big, d_big, m_big, v_big = _adamw(big(w), _join_halves(_add_chips(sums32, from_chips, pos)), big(m), big(v))
    loss = g_s[4, LOSS_COL] * (0.5 / D_MODEL)

    def tree(bigs, packed):
        out = {n: _unshard_rows(n, a) for n, a in zip(BIG_NAMES, bigs)}
        out.update(_unpack_small(packed))
        return [out[n] for n in WEIGHT_ORDER]

    return (loss, grad_x, *tree(g_big, g_s), *tree(d_big, d_s), *tree(m_big, m_s), *tree(v_big, v_s))
```

```python
import functools
import math

import jax
import jax.numpy as jnp
from jax import lax
from jax.experimental import pallas as pl
from jax.experimental.pallas import tpu as pltpu

F32 = jnp.float32
BF16 = jnp.bfloat16

D_MODEL = 1024
HEAD_DIM = 64
RET_HEADS = 8
ATTN_HEADS = 8
RET_WIDTH = 512
ATTN_WIDTH = 512
KV_WIDTH = 128
IN_WIDTH = 2816
FFN = 2816
N_SHARD = 4
FFN_SHARD = FFN // N_SHARD
PLE_DIM = 256
CHUNK = 128
LANES = 128
ALPHA = 2.0 ** 0.25
LN_EPS = 1e-5
GN_EPS = 1e-5
NEG_INF = -1e30
ADAM_LR = 0.001
ADAM_B1 = 0.9
ADAM_B2 = 0.999
ADAM_EPS = 1e-08
ADAM_WD = 0.01
ADAM_STEP = 10
VMEM_LIMIT = 56 * 1024 * 1024
MESH = pl.DeviceIdType.MESH

CB_RQ, CB_RK, CB_RV, CB_RG, CB_AQ, CB_AK, CB_AV = 0, 4, 8, 12, 16, 20, 21


def _dot(a, b):
    return jnp.dot(a, b, preferred_element_type=F32)


def _dot_nt(a, b):
    return lax.dot_general(a, b, (((1,), (1,)), ((), ())), preferred_element_type=F32)


def _dot_tn(a, b):
    return lax.dot_general(a, b, (((0,), (0,)), ((), ())), preferred_element_type=F32)


def _sigmoid(x):
    return 1.0 / (1.0 + jnp.exp(-x))


def _params(*sem, vmem=None):
    return pltpu.CompilerParams(dimension_semantics=tuple(sem) if sem else None, vmem_limit_bytes=vmem)


class _Rider:
    def __init__(self, ins, out_shapes, sems, start, finish, aliases=None):
        self.ins, self.out_shapes, self.sems = list(ins), list(out_shapes), list(sems)
        self.start, self.finish, self.aliases = start, finish, dict(aliases or {})


def _merge_riders(riders):
    riders = [r for r in riders if r is not None]
    if len(riders) == 1:
        return riders[0]
    bounds, aliases = [], {}
    i0 = o0 = s0 = 0
    for r in riders:
        bounds.append((i0, o0, s0))
        aliases.update({i0 + i: o0 + o for i, o in r.aliases.items()})
        i0, o0, s0 = i0 + len(r.ins), o0 + len(r.out_shapes), s0 + len(r.sems)

    def each(method):
        def run(ins, outs, sems):
            for r, (i, o, s) in zip(riders, bounds):
                getattr(r, method)(ins[i:i + len(r.ins)], outs[o:o + len(r.out_shapes)], sems[s:s + len(r.sems)])
        return run

    return _Rider([a for r in riders for a in r.ins], [a for r in riders for a in r.out_shapes],
                  [a for r in riders for a in r.sems], each("start"), each("finish"), aliases)


def _hosted_call(body, name, grid, in_specs, out_specs, out_shape, scratch_shapes, operands, rider=None,
                 semantics=None):
    n_in, n_out, n_scr = len(in_specs), len(out_specs), len(scratch_shapes)
    if rider is None:
        return pl.pallas_call(
            body, name=name, grid=grid, in_specs=in_specs, out_specs=out_specs, out_shape=out_shape,
            scratch_shapes=scratch_shapes,
            compiler_params=_params(*(semantics or ["parallel"] * len(grid)), vmem=VMEM_LIMIT))(*operands)
    r_in, r_out = len(rider.ins), len(rider.out_shapes)

    def full_body(*refs):
        main_in, rin = refs[:n_in], refs[n_in:n_in + r_in]
        o0 = n_in + r_in
        main_out, rout = refs[o0:o0 + n_out], refs[o0 + n_out:o0 + n_out + r_out]
        s0 = o0 + n_out + r_out
        main_scr, rsem = refs[s0:s0 + n_scr], refs[s0 + n_scr:]
        first = functools.reduce(jnp.logical_and, [pl.program_id(a) == 0 for a in range(len(grid))])
        last = functools.reduce(jnp.logical_and, [pl.program_id(a) == g - 1 for a, g in enumerate(grid)])

        @pl.when(first)
        def _():
            rider.start(rin, rout, rsem)

        body(*main_in, *main_out, *main_scr)

        @pl.when(last)
        def _():
            rider.finish(rin, rout, rsem)

    hbm = pl.BlockSpec(memory_space=pl.ANY)
    return pl.pallas_call(
        full_body, name=name, grid=grid,
        in_specs=list(in_specs) + [hbm] * r_in, out_specs=list(out_specs) + [hbm] * r_out,
        out_shape=list(out_shape) + rider.out_shapes,
        scratch_shapes=list(scratch_shapes) + rider.sems,
        input_output_aliases={n_in + i: n_out + o for i, o in rider.aliases.items()},
        compiler_params=_params(*(["arbitrary"] * len(grid)), vmem=VMEM_LIMIT),
    )(*operands, *rider.ins)


def _loop_pairs(n, body, init):
    if n % 2:
        return lax.fori_loop(0, n, body, init)
    return lax.fori_loop(0, n // 2, lambda i, c: body(2 * i + 1, body(2 * i, c)), init)


def _head_mean(x, m0):
    s0 = jnp.sum(jnp.where(m0, x, 0.0), axis=1, keepdims=True)
    s1 = jnp.sum(jnp.where(m0, 0.0, x), axis=1, keepdims=True)
    return jnp.where(m0, s0, s1) * (1.0 / HEAD_DIM)


def _inproj(x2d, w_in_t, rider=None):
    t = x2d.shape[0]
    tm = 512
    nb = 256

    def body(x_ref, w_ref, o_ref):
        xb = x_ref[...].astype(BF16)
        for n in range(0, IN_WIDTH, nb):
            o_ref[:, n:n + nb] = _dot_nt(xb, w_ref[n:n + nb, :]).astype(BF16)

    return _hosted_call(
        body, "inproj", (t // tm,),
        in_specs=[pl.BlockSpec((tm, D_MODEL), lambda i: (i, 0)),
                  pl.BlockSpec((IN_WIDTH, D_MODEL), lambda i: (0, 0))],
        out_specs=[pl.BlockSpec((tm, IN_WIDTH), lambda i: (i, 0))],
        out_shape=[jax.ShapeDtypeStruct((t, IN_WIDTH), BF16)],
        scratch_shapes=[], operands=(x2d, w_in_t), rider=rider)


def _outproj_ln1(y_ret, y_att, x2d, w_out, gain, bias, rider=None):
    t = x2d.shape[0]
    tm = 512

    def body(yr_ref, ya_ref, x_ref, w_ref, g_ref, b_ref, zh_ref, r_ref, hb_ref):
        mix = _dot(yr_ref[...], w_ref[0:RET_WIDTH, :]) + _dot(ya_ref[...], w_ref[RET_WIDTH:, :])
        z = ALPHA * x_ref[...] + mix
        mu = jnp.mean(z, axis=1, keepdims=True)
        zc = z - mu
        var = jnp.mean(zc * zc, axis=1, keepdims=True)
        r = lax.rsqrt(var + LN_EPS)
        zh = zc * r
        zh_ref[...] = zh
        r_ref[...] = r
        hb_ref[...] = (zh * g_ref[...] + b_ref[...]).astype(BF16)

    row = lambda w: pl.BlockSpec((tm, w), lambda i: (i, 0))
    const = lambda s: pl.BlockSpec(s, lambda i: (0, 0))
    return _hosted_call(
        body, "outproj_ln1", (t // tm,),
        in_specs=[row(RET_WIDTH), row(ATTN_WIDTH), row(D_MODEL), const((D_MODEL, D_MODEL)),
                  const((1, D_MODEL)), const((1, D_MODEL))],
        out_specs=[row(D_MODEL), row(1), row(D_MODEL)],
        out_shape=[jax.ShapeDtypeStruct((t, D_MODEL), F32), jax.ShapeDtypeStruct((t, 1), F32),
                   jax.ShapeDtypeStruct((t, D_MODEL), BF16)],
        scratch_shapes=[], operands=(y_ret, y_att, x2d, w_out, gain, bias), rider=rider)


def _load_resident(step, pairs, sems):
    copies = [pltpu.make_async_copy(src, dst, sems.at[i]) for i, (src, dst) in enumerate(pairs)]

    @pl.when(step == 0)
    def _():
        for cp in copies:
            cp.start()
        for cp in copies:
            cp.wait()


FFN_CHUNK = 256
N_FFN_CHUNK = FFN // FFN_CHUNK


def _resident_quarters(hbm, vmem):
    q = FFN // N_SHARD
    return [(hbm.at[pl.ds(j * q, q), :], vmem.at[pl.ds(j * q, q), :]) for j in range(N_SHARD)]


def _ffn_fwd(zh1, hb, p2d, tgt, g1, b1, g2, b2, wg4, wu4, wd4, wpe, wpg):
    t = zh1.shape[0]
    tm = 256
    wg_t, wu_t, wd_all = (w.reshape(FFN, D_MODEL) for w in (wg4, wu4, wd4))

    def body(zh_ref, hb_ref, p_ref, t_ref, g1_ref, b1_ref, g2_ref, b2_ref,
             wg_hbm, wu_hbm, wd_hbm, wpe_hbm, wpg_hbm,
             dz_ref, dzb_ref, gs_ref, us_ref, act_ref, pg_ref, ple_ref, loss_ref, dg2_ref, db2_ref,
             wg, wu, wd, wpe, wpg, wsem):
        step = pl.program_id(0)
        loads = _resident_quarters(wg_hbm, wg) + _resident_quarters(wu_hbm, wu) + _resident_quarters(wd_hbm, wd)
        _load_resident(step, loads + [(wpe_hbm, wpe), (wpg_hbm, wpg)], wsem)

        @pl.when(step == 0)
        def _():
            loss_ref[...] = jnp.zeros_like(loss_ref)
            dg2_ref[...] = jnp.zeros_like(dg2_ref)
            db2_ref[...] = jnp.zeros_like(db2_ref)

        h1 = zh_ref[...] * g1_ref[...] + b1_ref[...]
        hbv = hb_ref[...]
        ffn = jnp.zeros((tm, D_MODEL), F32)
        acts = []
        chunks = [slice(n * FFN_CHUNK, (n + 1) * FFN_CHUNK) for n in range(N_FFN_CHUNK)]
        for n in range(N_FFN_CHUNK + 1):
            if n < N_FFN_CHUNK:
                gj = _dot_nt(hbv, wg[chunks[n], :])
                uj = _dot_nt(hbv, wu[chunks[n], :])
                gs_ref[:, chunks[n]] = gj.astype(BF16)
                us_ref[:, chunks[n]] = uj.astype(BF16)
                acts.append((gj * _sigmoid(gj) * uj).astype(BF16))
                act_ref[:, chunks[n]] = acts[n]
            if n > 0:
                ffn = ffn + _dot(acts[n - 1], wd[chunks[n - 1], :])
        ple = _dot(p_ref[...].astype(BF16), wpe[...])
        pg = _sigmoid(_dot(hbv, wpg[...]))
        pg_ref[...] = pg.astype(BF16)
        ple_ref[...] = ple.astype(BF16)
        z2 = ALPHA * h1 + ffn + pg * ple
        mu = jnp.mean(z2, axis=1, keepdims=True)
        zc = z2 - mu
        var = jnp.mean(zc * zc, axis=1, keepdims=True)
        r = lax.rsqrt(var + LN_EPS)
        zh2 = zc * r
        err = zh2 * g2_ref[...] + b2_ref[...] - t_ref[...]
        loss_ref[...] += jnp.sum(err * err)
        dy = err * (1.0 / D_MODEL)
        dg2_ref[...] += jnp.sum(dy * zh2, axis=0, keepdims=True)
        db2_ref[...] += jnp.sum(dy, axis=0, keepdims=True)
        dzh = dy * g2_ref[...]
        m1 = jnp.mean(dzh, axis=1, keepdims=True)
        m2 = jnp.mean(dzh * zh2, axis=1, keepdims=True)
        dz2 = r * (dzh - m1 - zh2 * m2)
        dz_ref[...] = dz2
        dzb_ref[...] = dz2.astype(BF16)

    row = lambda w: pl.BlockSpec((tm, w), lambda i: (i, 0))
    const = lambda s: pl.BlockSpec(s, lambda i: (0, 0))
    hid_shape = jax.ShapeDtypeStruct((t, FFN), BF16)
    hbm = pl.BlockSpec(memory_space=pl.ANY)
    return pl.pallas_call(
        body, name="ffn_fwd", grid=(t // tm,),
        in_specs=[row(D_MODEL), row(D_MODEL), row(PLE_DIM), row(D_MODEL),
                  const((1, D_MODEL)), const((1, D_MODEL)), const((1, D_MODEL)), const((1, D_MODEL)),
                  hbm, hbm, hbm, hbm, hbm],
        out_specs=[row(D_MODEL), row(D_MODEL), row(FFN), row(FFN), row(FFN), row(D_MODEL), row(D_MODEL),
                   const((8, LANES)), const((1, D_MODEL)), const((1, D_MODEL))],
        out_shape=[jax.ShapeDtypeStruct((t, D_MODEL), F32), jax.ShapeDtypeStruct((t, D_MODEL), BF16),
                   hid_shape, hid_shape, hid_shape,
                   jax.ShapeDtypeStruct((t, D_MODEL), BF16), jax.ShapeDtypeStruct((t, D_MODEL), BF16),
                   jax.ShapeDtypeStruct((8, LANES), F32),
                   jax.ShapeDtypeStruct((1, D_MODEL), F32), jax.ShapeDtypeStruct((1, D_MODEL), F32)],
        scratch_shapes=[pltpu.VMEM((FFN, D_MODEL), BF16), pltpu.VMEM((FFN, D_MODEL), BF16),
                        pltpu.VMEM((FFN, D_MODEL), BF16),
                        pltpu.VMEM(wpe.shape, BF16), pltpu.VMEM(wpg.shape, BF16),
                        pltpu.SemaphoreType.DMA((3 * N_SHARD + 2,))],
        compiler_params=_params("arbitrary", vmem=VMEM_LIMIT),
    )(zh1, hb, p2d, tgt, g1, b1, g2, b2, wg_t, wu_t, wd_all, wpe, wpg)


def _ret_tables(lgf, lgb):
    c = CHUNK
    row = lax.broadcasted_iota(jnp.int32, (c, LANES), 0).astype(F32)
    ii = lax.broadcasted_iota(jnp.int32, (c, c), 0).astype(F32)
    jj = lax.broadcasted_iota(jnp.int32, (c, c), 1).astype(F32)
    diff = ii - jj
    dmats = []
    for h in range(2):
        lf = lgf[:, h * HEAD_DIM:h * HEAD_DIM + 1]
        lb = lgb[:, h * HEAD_DIM:h * HEAD_DIM + 1]
        dmats.append(jnp.where(diff > 0, jnp.exp(lf * jnp.maximum(diff, 0.0)),
                               jnp.where(diff < 0, jnp.exp(lb * jnp.maximum(-diff, 0.0)), 2.0)))
    tab = dict(
        qdec_f=jnp.exp(lgf * (row + 1.0)), kdec_f=jnp.exp(lgf * (c - 1.0 - row)),
        qdec_b=jnp.exp(lgb * (c - row)), kdec_b=jnp.exp(lgb * row),
        cdec_f=jnp.exp(lgf * c), cdec_b=jnp.exp(lgb * c),
        d0=dmats[0], d1=dmats[1], row=row, diff=diff)
    r = lax.broadcasted_iota(jnp.int32, (LANES, LANES), 0) < HEAD_DIM
    cc = lax.broadcasted_iota(jnp.int32, (LANES, LANES), 1) < HEAD_DIM
    tab["bd"] = r == cc
    tab["m0"] = lax.broadcasted_iota(jnp.int32, (c, LANES), 1) < HEAD_DIM
    return tab


def _ret_specs(bsz, s):
    blk = lambda cb: pl.BlockSpec((bsz, s, LANES), lambda p, cb=cb: (0, 0, cb + p))
    lane = pl.BlockSpec((None, 1, LANES), lambda p: (p, 0, 0))
    gain = pl.BlockSpec((1, LANES), lambda p: (0, p))
    pair = pl.BlockSpec((bsz, s, LANES), lambda p: (0, 0, p))
    return blk, lane, gain, pair


def _ret_state_spec(bsz, n_chunk):
    spec = pl.BlockSpec((None, bsz, n_chunk, LANES, LANES), lambda p: (p, 0, 0, 0, 0))
    return spec, jax.ShapeDtypeStruct((4, bsz, n_chunk, LANES, LANES), F32)


def _ret_kv_states(tb, k_ref, v_ref, rb_ref, kvf_ref, n_chunk):
    c = CHUNK
    bsz = k_ref.shape[0]
    bd = tb["bd"]

    def contributions(n, carry):
        sl = pl.ds(pl.multiple_of(n * c, c), c)
        kfb = []
        for b in range(bsz):
            k32 = k_ref[b, sl, :].astype(F32)
            kfb.append(jnp.concatenate([k32 * tb["kdec_f"], k32 * tb["kdec_b"]], axis=1).astype(BF16))
        kvs = [_dot_tn(kfb[b], v_ref[b, sl, :]) for b in range(bsz)]
        for b in range(bsz):
            kvf_ref[b, n] = jnp.where(bd, kvs[b][0:LANES], 0.0)
            rb_ref[b, n] = jnp.where(bd, kvs[b][LANES:], 0.0)
        return carry

    lax.fori_loop(0, n_chunk, contributions, 0, unroll=2)

    def recur(i, rbs):
        n = n_chunk - 1 - i
        new = []
        for b in range(bsz):
            own = rb_ref[b, n]
            rb_ref[b, n] = rbs[b]
            new.append(rbs[b] * tb["cdec_b"] + own)
        return tuple(new)

    lax.fori_loop(0, n_chunk, recur, tuple(jnp.zeros((LANES, LANES), F32) for _ in range(bsz)))


def _split_rows(x, m0):
    return jnp.concatenate([jnp.where(m0, x, 0.0), jnp.where(m0, 0.0, x)], axis=0).astype(BF16)


def _ret_fwd(u3, lgf_l, lgb_l, gn_gain, rider=None):
    bsz, s, _ = u3.shape
    n_chunk = s // CHUNK
    c = CHUNK

    def body(q_ref, k_ref, v_ref, g_ref, lgf_ref, lgb_ref, gain_ref, yh_ref, rstd_ref, o_ref, rb_ref, kvf_ref):
        tb = _ret_tables(lgf_ref[...], lgb_ref[...])
        m0 = tb["m0"]
        gain = gain_ref[...]
        rows = range(bsz)
        _ret_kv_states(tb, k_ref, v_ref, rb_ref, kvf_ref, n_chunk)

        def chunk(n, rfs):
            sl = pl.ds(pl.multiple_of(n * c, c), c)
            qs = [q_ref[b, sl, :].astype(F32) * 0.125 for b in rows]
            s01 = [_dot_nt(_split_rows(qs[b], m0), k_ref[b, sl, :]) for b in rows]
            ys = []
            for b in rows:
                lhs = jnp.concatenate([s01[b][0:c] * tb["d0"], s01[b][c:] * tb["d1"],
                                       qs[b] * tb["qdec_f"], qs[b] * tb["qdec_b"]], axis=1).astype(BF16)
                rhs = jnp.concatenate([_split_rows(v_ref[b, sl, :].astype(F32), m0),
                                       rfs[b].astype(BF16), rb_ref[b, n].astype(BF16)], axis=0)
                ys.append(_dot(lhs, rhs))
            new = []
            for b in rows:
                y = ys[b]
                mu = _head_mean(y, m0)
                yc = y - mu
                rstd = lax.rsqrt(_head_mean(yc * yc, m0) + GN_EPS)
                yh = yc * rstd
                g = g_ref[b, sl, :].astype(F32)
                yh_ref[b, sl, :] = yh
                rstd_ref[b, sl, :] = rstd
                o_ref[b, sl, :] = (yh * gain * (g * _sigmoid(g))).astype(BF16)
                new.append(rfs[b] * tb["cdec_f"] + kvf_ref[b, n])
            return tuple(new)

        _loop_pairs(n_chunk, chunk, tuple(jnp.zeros((LANES, LANES), F32) for _ in rows))

    blk, lane, gain, pair = _ret_specs(bsz, s)
    state, state_shape = _ret_state_spec(bsz, n_chunk)
    return _hosted_call(
        body, "ret_fwd", (4,),
        in_specs=[blk(CB_RQ), blk(CB_RK), blk(CB_RV), blk(CB_RG), lane, lane, gain],
        out_specs=[pair, pair, pair, state, state],
        out_shape=[jax.ShapeDtypeStruct((bsz, s, RET_WIDTH), F32), jax.ShapeDtypeStruct((bsz, s, RET_WIDTH), F32),
                   jax.ShapeDtypeStruct((bsz, s, RET_WIDTH), BF16), state_shape, state_shape],
        scratch_shapes=[],
        operands=(u3, u3, u3, u3, lgf_l, lgb_l, gn_gain), rider=rider)


def _ret_bwd(u3, y_hat, y_rstd, states, d_o, lgf_l, lgb_l, gn_gain, rider=None):
    bsz, s, _ = u3.shape
    n_chunk = s // CHUNK
    c = CHUNK

    def body(q_ref, k_ref, v_ref, g_ref, yh_ref, rstd_ref, do_ref, lgf_ref, lgb_ref, gain_ref, rb_ref, kvf_ref,
             dq_ref, dk_ref, dv_ref, dg_ref, part_ref,
             rf_ref, dirf_ref, dy_ref, dk_acc, dv_acc, pa0, pa1, vec_ref):
        tb = _ret_tables(lgf_ref[...], lgb_ref[...])
        m0, bd, row = tb["m0"], tb["bd"], tb["row"]
        gain = gain_ref[...]
        wf = jnp.maximum(tb["diff"], 0.0)
        wb = jnp.maximum(-tb["diff"], 0.0)
        rows = range(bsz)
        zero_states = tuple(jnp.zeros((LANES, LANES), F32) for _ in rows)
        for ref in (pa0, pa1):
            ref[...] = jnp.zeros_like(ref)
        vec_ref[...] = jnp.zeros_like(vec_ref)

        def sweep_fwd(n, carry):
            rfs, gbs = carry
            sl = pl.ds(pl.multiple_of(n * c, c), c)
            qs, ks, vs, dys, dybs, q01, k01, dy01 = [], [], [], [], [], [], [], []
            dgain = jnp.zeros((1, LANES), F32)
            for b in rows:
                q = q_ref[b, sl, :].astype(F32) * 0.125
                k = k_ref[b, sl, :]
                yh = yh_ref[b, sl, :]
                rstd = rstd_ref[b, sl, :]
                do = do_ref[b, sl, :].astype(F32)
                g = g_ref[b, sl, :].astype(F32)
                sg = _sigmoid(g)
                sil = g * sg
                dyh = do * gain * sil
                dg_ref[b, sl, :] = (do * yh * gain * sg * (1.0 + g * (1.0 - sg))).astype(BF16)
                dgain = dgain + jnp.sum(do * yh * sil, axis=0, keepdims=True)
                dy = rstd * (dyh - _head_mean(dyh, m0) - yh * _head_mean(dyh * yh, m0))
                dyb = dy.astype(BF16)
                dy_ref[b, sl, :] = dyb
                rf_ref[b, n] = rfs[b]
                qs.append(q)
                ks.append(k)
                vs.append(v_ref[b, sl, :])
                dys.append(dy)
                dybs.append(dyb)
                q01.append(_split_rows(q, m0))
                k01.append(_split_rows(k.astype(F32), m0))
                dy01.append(_split_rows(dy, m0))
            s01 = [_dot_nt(q01[b], ks[b]) for b in rows]
            da01 = [_dot_nt(dy01[b], vs[b]) for b in rows]
            rbn = [rb_ref[b, n] for b in rows]
            states = [jnp.concatenate([rfs[b], rbn[b]], axis=0).astype(BF16) for b in rows]
            dqc = [_dot_nt(dybs[b], states[b]) for b in rows]
            gbb = [gbs[b].astype(BF16) for b in rows]
            dkb = [_dot_nt(vs[b], gbb[b]) for b in rows]
            qfb = [jnp.concatenate([qs[b] * tb["qdec_f"], qs[b] * tb["qdec_b"]], axis=1) for b in rows]
            direct = [_dot_tn(qfb[b].astype(BF16), dybs[b]) for b in rows]
            ds_cat, ds_rows, a_rows = [], [], []
            for b in rows:
                a0 = s01[b][0:c] * tb["d0"]
                a1 = s01[b][c:] * tb["d1"]
                pa0[...] += da01[b][0:c] * a0
                pa1[...] += da01[b][c:] * a1
                ds0 = da01[b][0:c] * tb["d0"]
                ds1 = da01[b][c:] * tb["d1"]
                ds_cat.append(jnp.concatenate([ds0, ds1], axis=1).astype(BF16))
                ds_rows.append(jnp.concatenate([ds0, ds1], axis=0).astype(BF16))
                a_rows.append(jnp.concatenate([a0, a1], axis=0).astype(BF16))
            kbd = [ks[b].astype(F32) * tb["kdec_b"] for b in rows]
            dq_in = [_dot(ds_cat[b], k01[b]) for b in rows]
            dk_in = [_dot_tn(ds_rows[b], q01[b]) for b in rows]
            dv_in = [_dot_tn(a_rows[b], dy01[b]) for b in rows]
            dv_gb = [_dot(kbd[b].astype(BF16), gbb[b]) for b in rows]
            new_rf, new_gb = [], []
            dlf = jnp.zeros((1, LANES), F32)
            dlb = jnp.zeros((1, LANES), F32)
            for b in rows:
                dqf, dqb = dqc[b][:, 0:LANES], dqc[b][:, LANES:]
                qf, qb = qfb[b][:, 0:LANES], qfb[b][:, LANES:]
                dq = dq_in[b] + dqf * tb["qdec_f"] + dqb * tb["qdec_b"]
                dq_ref[b, sl, :] = (dq * 0.125).astype(BF16)
                dk_acc[b, sl, :] = dk_in[b] + dkb[b] * tb["kdec_b"]
                dv_acc[b, sl, :] = dv_in[b] + dv_gb[b]
                dlf = dlf + jnp.sum((row + 1.0) * qf * dqf, axis=0, keepdims=True)
                dlb = dlb + jnp.sum((c - row) * qb * dqb + row * kbd[b] * dkb[b], axis=0, keepdims=True)
                dlb = dlb + c * tb["cdec_b"] * jnp.sum(gbs[b] * rbn[b], axis=0, keepdims=True)
                dirf_ref[b, n] = jnp.where(bd, direct[b][0:LANES], 0.0)
                new_gb.append(jnp.where(bd, direct[b][LANES:], 0.0) + tb["cdec_b"] * gbs[b])
                new_rf.append(rfs[b] * tb["cdec_f"] + kvf_ref[b, n])
            vec_ref[0:1, :] += dlf
            vec_ref[1:2, :] += dlb
            vec_ref[6:7, :] += dgain
            return tuple(new_rf), tuple(new_gb)

        _loop_pairs(n_chunk, sweep_fwd, (zero_states, zero_states))

        def sweep_bwd(i, gfs):
            n = n_chunk - 1 - i
            sl = pl.ds(pl.multiple_of(n * c, c), c)
            gfb = [gfs[b].astype(BF16) for b in rows]
            kfd = [k_ref[b, sl, :].astype(F32) * tb["kdec_f"] for b in rows]
            dkf = [_dot_nt(v_ref[b, sl, :], gfb[b]) for b in rows]
            dvf = [_dot(kfd[b].astype(BF16), gfb[b]) for b in rows]
            new = []
            dlf = jnp.zeros((1, LANES), F32)
            for b in rows:
                dk_ref[b, sl, :] = (dk_acc[b, sl, :] + dkf[b] * tb["kdec_f"]).astype(BF16)
                dv_ref[b, sl, :] = (dv_acc[b, sl, :] + dvf[b]).astype(BF16)
                dlf = dlf + jnp.sum((c - 1.0 - row) * kfd[b] * dkf[b], axis=0, keepdims=True)
                dlf = dlf + c * tb["cdec_f"] * jnp.sum(gfs[b] * rf_ref[b, n], axis=0, keepdims=True)
                new.append(dirf_ref[b, n] + tb["cdec_f"] * gfs[b])
            vec_ref[0:1, :] += dlf
            return tuple(new)

        lax.fori_loop(0, n_chunk, sweep_bwd, zero_states, unroll=2)
        vec_ref[2:3, :] = jnp.sum(pa0[...] * wf, axis=0, keepdims=True)
        vec_ref[3:4, :] = jnp.sum(pa1[...] * wf, axis=0, keepdims=True)
        vec_ref[4:5, :] = jnp.sum(pa0[...] * wb, axis=0, keepdims=True)
        vec_ref[5:6, :] = jnp.sum(pa1[...] * wb, axis=0, keepdims=True)
        part_ref[...] = vec_ref[...]

    blk, lane, gain, pair = _ret_specs(bsz, s)
    out_bf = jax.ShapeDtypeStruct((bsz, s, RET_WIDTH), BF16)
    state = pltpu.VMEM((bsz, n_chunk, LANES, LANES), F32)
    saved = _ret_state_spec(bsz, n_chunk)[0]
    return _hosted_call(
        body, "ret_bwd", (4,),
        in_specs=[blk(CB_RQ), blk(CB_RK), blk(CB_RV), blk(CB_RG), pair, pair, pair, lane, lane, gain, saved, saved],
        out_specs=[pair, pair, pair, pair, pl.BlockSpec((None, 8, LANES), lambda p: (p, 0, 0))],
        out_shape=[out_bf, out_bf, out_bf, out_bf, jax.ShapeDtypeStruct((4, 8, LANES), F32)],
        scratch_shapes=[state, state,
                        pltpu.VMEM((bsz, s, LANES), BF16), pltpu.VMEM((bsz, s, LANES), F32),
                        pltpu.VMEM((bsz, s, LANES), F32),
                        pltpu.VMEM((c, c), F32), pltpu.VMEM((c, c), F32), pltpu.VMEM((8, LANES), F32)],
        operands=(u3, u3, u3, u3, y_hat, y_rstd, d_o, lgf_l, lgb_l, gn_gain, *states), rider=rider)


def _attn_window_tables(n, s):
    qi = lax.broadcasted_iota(jnp.int32, (CHUNK, 3 * CHUNK), 0)
    kj = lax.broadcasted_iota(jnp.int32, (CHUNK, 3 * CHUNK), 1)
    dist = jnp.abs(kj - CHUNK - qi)
    kpos = n * CHUNK - CHUNK + kj
    valid = (dist <= CHUNK) & (kpos >= 0) & (kpos < s)
    return dist.astype(F32), valid


def _dup_kv_head(x, g):
    lane = lax.broadcasted_iota(jnp.int32, x.shape, 1)
    keep = (lane < HEAD_DIM) == (g == 0)
    xf = x.astype(F32)
    return jnp.where(keep, xf, pltpu.roll(xf, HEAD_DIM, 1))


def _attn_specs(s):
    q = pl.BlockSpec((None, s, 2 * LANES), lambda b, g: (b, 0, CB_AQ // 2 + g))
    k = pl.BlockSpec((None, s, LANES), lambda b, g: (b, 0, CB_AK))
    v = pl.BlockSpec((None, s, LANES), lambda b, g: (b, 0, CB_AV))
    grp = pl.BlockSpec((None, s, 2 * LANES), lambda b, g: (b, 0, g))
    smem = pl.BlockSpec(memory_space=pltpu.SMEM)
    return q, k, v, grp, smem


def _fill_padded(dst_ref, val, s):
    dst_ref[0:CHUNK, :] = jnp.zeros((CHUNK, LANES), dst_ref.dtype)
    dst_ref[CHUNK:CHUNK + s, :] = val.astype(dst_ref.dtype)
    dst_ref[CHUNK + s:2 * CHUNK + s, :] = jnp.zeros((CHUNK, LANES), dst_ref.dtype)


def _attn_probs(sc, slope, snk, dist, valid):
    sc = jnp.where(valid, sc - slope * dist, NEG_INF)
    m = jnp.maximum(jnp.max(sc, axis=1, keepdims=True), snk)
    e = jnp.exp(sc - m)
    es = jnp.exp(snk - m)
    inv = 1.0 / (jnp.sum(e, axis=1, keepdims=True) + es)
    return e * inv, es * inv


def _stack_heads(x2, m0):
    parts = []
    for pr in range(2):
        xp = x2[:, pr * LANES:(pr + 1) * LANES]
        parts += [jnp.where(m0, xp, 0.0), jnp.where(m0, 0.0, xp)]
    return jnp.concatenate(parts, axis=0).astype(BF16)


def _unstack_pair(x_all, pr, m0):
    return jnp.where(m0, x_all[(2 * pr) * CHUNK:(2 * pr + 1) * CHUNK], x_all[(2 * pr + 1) * CHUNK:(2 * pr + 2) * CHUNK])


def _attn_saved_specs(bsz, n_blk):
    specs = [pl.BlockSpec((None, None, n_blk, 4 * CHUNK, w), lambda b, g: (b, g, 0, 0, 0)) for w in (3 * CHUNK, 1)]
    shapes = [jax.ShapeDtypeStruct((bsz, 2, n_blk, 4 * CHUNK, 3 * CHUNK), BF16),
              jax.ShapeDtypeStruct((bsz, 2, n_blk, 4 * CHUNK, 1), F32)]
    return specs, shapes


def _attn_fwd(u3, slopes, sink, rider=None):
    bsz, s, _ = u3.shape
    n_blk = s // CHUNK

    def body(slope_ref, sink_ref, q_ref, k_ref, v_ref, o_ref, p_ref, ps_ref, kp_ref, vp_ref):
        g = pl.program_id(1)
        _fill_padded(kp_ref, _dup_kv_head(k_ref[...], g), s)
        _fill_padded(vp_ref, _dup_kv_head(v_ref[...], g), s)
        m0 = lax.broadcasted_iota(jnp.int32, (CHUNK, LANES), 1) < HEAD_DIM

        def blk(n, carry):
            r0 = pl.multiple_of(n * CHUNK, CHUNK)
            kw = kp_ref[pl.ds(r0, 3 * CHUNK), :]
            vw = vp_ref[pl.ds(r0, 3 * CHUNK), :]
            dist, valid = _attn_window_tables(n, s)
            q_all = _stack_heads(q_ref[pl.ds(r0, CHUNK), :].astype(F32) * 0.125, m0)
            sc_all = _dot_nt(q_all, kw)
            probs, sinks = [], []
            for i in range(4):
                p, ps = _attn_probs(sc_all[i * CHUNK:(i + 1) * CHUNK], slope_ref[g * 4 + i], sink_ref[g * 4 + i],
                                    dist, valid)
                probs.append(p.astype(BF16))
                sinks.append(ps)
            p_all = jnp.concatenate(probs, axis=0)
            p_ref[n] = p_all
            ps_ref[n] = jnp.concatenate(sinks, axis=0)
            out_all = _dot(p_all, vw)
            for pr in range(2):
                o_ref[pl.ds(r0, CHUNK), pr * LANES:(pr + 1) * LANES] = _unstack_pair(out_all, pr, m0).astype(BF16)
            return carry

        lax.fori_loop(0, n_blk, blk, 0, unroll=2)

    q, k, v, grp, smem = _attn_specs(s)
    saved_specs, saved_shapes = _attn_saved_specs(bsz, n_blk)
    return _hosted_call(
        body, "attn_fwd", (bsz, 2),
        in_specs=[smem, smem, q, k, v],
        out_specs=[grp] + saved_specs,
        out_shape=[jax.ShapeDtypeStruct((bsz, s, ATTN_WIDTH), BF16)] + saved_shapes,
        scratch_shapes=[pltpu.VMEM((s + 2 * CHUNK, LANES), BF16), pltpu.VMEM((s + 2 * CHUNK, LANES), BF16)],
        operands=(slopes, sink, u3, u3, u3), rider=rider)


def _attn_bwd(u3, d_o, probs, sink_probs, rider=None):
    bsz, s, _ = u3.shape
    n_blk = s // CHUNK

    def body(q_ref, k_ref, v_ref, do_ref, p_ref, ps_ref, dq_ref, dkv_ref, ds_ref,
             kp_ref, vp_ref, dk_acc, dv_acc):
        g = pl.program_id(1)
        _fill_padded(kp_ref, _dup_kv_head(k_ref[...], g), s)
        _fill_padded(vp_ref, _dup_kv_head(v_ref[...], g), s)
        dk_acc[...] = jnp.zeros_like(dk_acc)
        dv_acc[...] = jnp.zeros_like(dv_acc)
        m0 = lax.broadcasted_iota(jnp.int32, (CHUNK, LANES), 1) < HEAD_DIM

        def blk(n, dsink):
            r0 = pl.multiple_of(n * CHUNK, CHUNK)
            win = pl.ds(r0, 3 * CHUNK)
            kw = kp_ref[win, :]
            vw = vp_ref[win, :]
            q_all = _stack_heads(q_ref[pl.ds(r0, CHUNK), :].astype(F32) * 0.125, m0)
            do_all = _stack_heads(do_ref[pl.ds(r0, CHUNK), :].astype(F32), m0)
            p_all = p_ref[n]
            ps_all = ps_ref[n]
            dp_all = _dot_nt(do_all, vw)
            new_dsink, dscs = [], []
            for i in range(4):
                rows = slice(i * CHUNK, (i + 1) * CHUNK)
                p = p_all[rows].astype(F32)
                dp = dp_all[rows]
                delta = jnp.sum(p * dp, axis=1, keepdims=True)
                dscs.append((p * (dp - delta)).astype(BF16))
                dsh = jnp.sum(ps_all[rows] * delta, axis=0, keepdims=True)
                new_dsink.append(dsink[i] - jnp.broadcast_to(dsh, (1, LANES)))
            dsc_all = jnp.concatenate(dscs, axis=0)
            dq_all = _dot(dsc_all, kw)
            dk_acc[win, :] += _dot_tn(dsc_all, q_all)
            dv_acc[win, :] += _dot_tn(p_all, do_all)
            for pr in range(2):
                dq_ref[pl.ds(r0, CHUNK), pr * LANES:(pr + 1) * LANES] = (
                    _unstack_pair(dq_all, pr, m0) * 0.125).astype(BF16)
            return tuple(new_dsink)

        dsink = _loop_pairs(n_blk, blk, tuple(jnp.zeros((1, LANES), F32) for _ in range(4)))
        dk = dk_acc[CHUNK:CHUNK + s, :]
        dv = dv_acc[CHUNK:CHUNK + s, :]
        lane = lax.broadcasted_iota(jnp.int32, (s, LANES), 1)
        fold = lambda a: a + pltpu.roll(a, HEAD_DIM, 1)
        dkv_ref[...] = jnp.where(lane < HEAD_DIM, fold(dk), fold(dv)).astype(BF16)
        ds_ref[...] = jnp.zeros_like(ds_ref)
        for i in range(4):
            ds_ref[i:i + 1, :] = dsink[i]

    q, k, v, grp, _ = _attn_specs(s)
    return _hosted_call(
        body, "attn_bwd", (bsz, 2),
        in_specs=[q, k, v, grp] + _attn_saved_specs(bsz, n_blk)[0],
        out_specs=[grp, pl.BlockSpec((None, s, LANES), lambda b, g: (b, 0, g)),
                   pl.BlockSpec((None, None, 8, LANES), lambda b, g: (b, g, 0, 0))],
        out_shape=[jax.ShapeDtypeStruct((bsz, s, ATTN_WIDTH), BF16), jax.ShapeDtypeStruct((bsz, s, 2 * LANES), BF16),
                   jax.ShapeDtypeStruct((bsz, 2, 8, LANES), F32)],
        scratch_shapes=[pltpu.VMEM((s + 2 * CHUNK, LANES), BF16), pltpu.VMEM((s + 2 * CHUNK, LANES), BF16),
                        pltpu.VMEM((s + 2 * CHUNK, LANES), F32), pltpu.VMEM((s + 2 * CHUNK, LANES), F32)],
        operands=(u3, u3, u3, d_o, probs, sink_probs), rider=rider)


def _ffn_bwd(dz2, gs, us, pg, ple, zh1, r1, g1, wg4, wu4, wd4, wpg, w_out):
    t = dz2.shape[0]
    tm = 256
    wg_t, wu_t, wd_all = (w.reshape(FFN, D_MODEL) for w in (wg4, wu4, wd4))

    def body(dz_ref, gs_ref, us_ref, pg_ref, ple_ref, zh_ref, r_ref, g1_ref,
             wg_hbm, wu_hbm, wd_hbm, wpg_hbm, wo_hbm,
             dgs_ref, dus_ref, dsp_ref, dple_ref, dz1_ref, dyr_ref, dya_ref, dg1_ref, db1_ref,
             wg, wu, wd, wpg, wo, wsem):
        step = pl.program_id(0)
        loads = _resident_quarters(wd_hbm, wd) + _resident_quarters(wg_hbm, wg) + _resident_quarters(wu_hbm, wu)
        _load_resident(step, loads + [(wpg_hbm, wpg), (wo_hbm, wo)], wsem)

        @pl.when(step == 0)
        def _():
            dg1_ref[...] = jnp.zeros_like(dg1_ref)
            db1_ref[...] = jnp.zeros_like(db1_ref)

        dz = dz_ref[...]
        dzb = dz.astype(BF16)
        dh = ALPHA * dz
        pending = []
        chunks = [slice(n * FFN_CHUNK, (n + 1) * FFN_CHUNK) for n in range(N_FFN_CHUNK)]
        for n in range(N_FFN_CHUNK + 1):
            if n < N_FFN_CHUNK:
                da = _dot_nt(dzb, wd[chunks[n], :])
                gj = gs_ref[:, chunks[n]].astype(F32)
                uj = us_ref[:, chunks[n]].astype(F32)
                sg = _sigmoid(gj)
                dgj = (da * uj * sg * (1.0 + gj * (1.0 - sg))).astype(BF16)
                duj = (da * gj * sg).astype(BF16)
                dgs_ref[:, chunks[n]] = dgj
                dus_ref[:, chunks[n]] = duj
                pending.append((dgj, duj))
            if n > 0:
                dgp, dup = pending[n - 1]
                dh = dh + _dot(dgp, wg[chunks[n - 1], :]) + _dot(dup, wu[chunks[n - 1], :])
        pgv = pg_ref[...].astype(F32)
        plev = ple_ref[...].astype(F32)
        dple_ref[...] = (dz * pgv).astype(BF16)
        dsp = (dz * plev * pgv * (1.0 - pgv)).astype(BF16)
        dsp_ref[...] = dsp
        dh = dh + _dot_nt(dsp, wpg[...])
        zh = zh_ref[...]
        dg1_ref[...] += jnp.sum(dh * zh, axis=0, keepdims=True)
        db1_ref[...] += jnp.sum(dh, axis=0, keepdims=True)
        dzh = dh * g1_ref[...]
        m1 = jnp.mean(dzh, axis=1, keepdims=True)
        m2 = jnp.mean(dzh * zh, axis=1, keepdims=True)
        dz1 = r_ref[...] * (dzh - m1 - zh * m2)
        dz1_ref[...] = dz1
        dyc = _dot_nt(dz1.astype(BF16), wo[...])
        dyr_ref[...] = dyc[:, 0:RET_WIDTH].astype(BF16)
        dya_ref[...] = dyc[:, RET_WIDTH:].astype(BF16)

    row = lambda w: pl.BlockSpec((tm, w), lambda i: (i, 0))
    const = lambda s: pl.BlockSpec(s, lambda i: (0, 0))
    hbm = pl.BlockSpec(memory_space=pl.ANY)
    hid_shape = jax.ShapeDtypeStruct((t, FFN), BF16)
    return pl.pallas_call(
        body, name="ffn_bwd", grid=(t // tm,),
        in_specs=[row(D_MODEL), row(FFN), row(FFN), row(D_MODEL), row(D_MODEL), row(D_MODEL), row(1),
                  const((1, D_MODEL)), hbm, hbm, hbm, hbm, hbm],
        out_specs=[row(FFN), row(FFN), row(D_MODEL), row(D_MODEL), row(D_MODEL), row(RET_WIDTH), row(ATTN_WIDTH),
                   const((1, D_MODEL)), const((1, D_MODEL))],
        out_shape=[hid_shape, hid_shape, jax.ShapeDtypeStruct((t, D_MODEL), BF16),
                   jax.ShapeDtypeStruct((t, D_MODEL), BF16), jax.ShapeDtypeStruct((t, D_MODEL), F32),
                   jax.ShapeDtypeStruct((t, RET_WIDTH), BF16), jax.ShapeDtypeStruct((t, ATTN_WIDTH), BF16),
                   jax.ShapeDtypeStruct((1, D_MODEL), F32), jax.ShapeDtypeStruct((1, D_MODEL), F32)],
        scratch_shapes=[pltpu.VMEM((FFN, D_MODEL), BF16), pltpu.VMEM((FFN, D_MODEL), BF16),
                        pltpu.VMEM((FFN, D_MODEL), BF16),
                        pltpu.VMEM(wpg.shape, BF16), pltpu.VMEM(w_out.shape, BF16),
                        pltpu.SemaphoreType.DMA((3 * N_SHARD + 2,))],
        compiler_params=_params("arbitrary", vmem=VMEM_LIMIT),
    )(dz2, gs, us, pg, ple, zh1, r1, g1, wg_t, wu_t, wd_all, wpg, w_out)


def _wgrad_misc(y_ret, y_att, dz1, hb, dsp, p2d, dple, rider=None):
    t = dz1.shape[0]
    tk = min(t, 512)

    def body(yr_ref, ya_ref, dz_ref, hb_ref, dsp_ref, p_ref, dple_ref, wo_ref, wpg_ref, wpe_ref):
        @pl.when(pl.program_id(0) == 0)
        def _():
            wo_ref[...] = jnp.zeros_like(wo_ref)
            wpg_ref[...] = jnp.zeros_like(wpg_ref)
            wpe_ref[...] = jnp.zeros_like(wpe_ref)

        dzb = dz_ref[...].astype(BF16)
        wo_ref[0:RET_WIDTH, :] += _dot_tn(yr_ref[...], dzb)
        wo_ref[RET_WIDTH:, :] += _dot_tn(ya_ref[...], dzb)
        wpg_ref[...] += _dot_tn(hb_ref[...], dsp_ref[...])
        wpe_ref[...] += _dot_tn(p_ref[...].astype(BF16), dple_ref[...])

    row = lambda w: pl.BlockSpec((tk, w), lambda k: (k, 0))
    const = lambda s: pl.BlockSpec(s, lambda k: (0, 0))
    return _hosted_call(
        body, "wgrad_misc", (t // tk,),
        in_specs=[row(RET_WIDTH), row(ATTN_WIDTH), row(D_MODEL), row(D_MODEL), row(D_MODEL), row(PLE_DIM),
                  row(D_MODEL)],
        out_specs=[const((D_MODEL, D_MODEL)), const((D_MODEL, D_MODEL)), const((PLE_DIM, D_MODEL))],
        out_shape=[jax.ShapeDtypeStruct((D_MODEL, D_MODEL), F32), jax.ShapeDtypeStruct((D_MODEL, D_MODEL), F32),
                   jax.ShapeDtypeStruct((PLE_DIM, D_MODEL), F32)],
        scratch_shapes=[], operands=(y_ret, y_att, dz1, hb, dsp, p2d, dple), rider=rider, semantics=["arbitrary"])


def _wgrad_ffn(acts, dgs, dus, hb, dz2b):
    t = dz2b.shape[0]
    tk = min(t, 512)
    nk = t // tk

    def body(act_ref, dg_ref, du_ref, hb_ref, dz_ref, og_ref, ou_ref, od_ref):
        @pl.when(pl.program_id(1) == 0)
        def _():
            og_ref[...] = jnp.zeros_like(og_ref)
            ou_ref[...] = jnp.zeros_like(ou_ref)
            od_ref[...] = jnp.zeros_like(od_ref)

        hbv = hb_ref[...]
        og_ref[...] += _dot_tn(dg_ref[...], hbv)
        ou_ref[...] += _dot_tn(du_ref[...], hbv)
        od_ref[...] += _dot_tn(act_ref[...], dz_ref[...])

    half = FFN // 2
    a_spec = pl.BlockSpec((tk, half), lambda j, k: (k, j))
    b_spec = pl.BlockSpec((tk, D_MODEL), lambda j, k: (k, 0))
    o_spec = pl.BlockSpec((half, D_MODEL), lambda j, k: (j, 0))
    o_shape = jax.ShapeDtypeStruct((FFN, D_MODEL), F32)
    outs = pl.pallas_call(
        body, name="wgrad_ffn", grid=(2, nk),
        in_specs=[a_spec, a_spec, a_spec, b_spec, b_spec],
        out_specs=[o_spec] * 3, out_shape=[o_shape] * 3,
        compiler_params=_params("parallel", "arbitrary", vmem=VMEM_LIMIT),
    )(acts, dgs, dus, hb, dz2b)
    return [o.reshape(N_SHARD, FFN_SHARD, D_MODEL) for o in outs]


KV_ORDER = (0, 128, 64, 192)


def _wgrad_in(pieces, x2d, rider=None):
    t = x2d.shape[0]
    tk = min(t, 512)
    nk = t // tk
    kv0 = CB_AK * LANES

    def body(p0, p1, p2, p3, p4, pkv, x_ref, o_ref):
        @pl.when(pl.program_id(0) == 0)
        def _():
            o_ref[...] = jnp.zeros_like(o_ref)

        xb = x_ref[...].astype(BF16)
        for i, ref in enumerate((p0, p1, p2, p3, p4)):
            o_ref[i * 512:(i + 1) * 512, :] += _dot_tn(ref[...], xb)
        dkv = _dot_tn(pkv[...], xb)
        for i, o in enumerate(KV_ORDER):
            o_ref[kv0 + o:kv0 + o + HEAD_DIM, :] += dkv[i * HEAD_DIM:(i + 1) * HEAD_DIM]

    row = lambda w: pl.BlockSpec((tk, w), lambda k: (k, 0))
    return _hosted_call(
        body, "wgrad_in", (nk,),
        in_specs=[row(512)] * 5 + [row(256), row(D_MODEL)],
        out_specs=[pl.BlockSpec((IN_WIDTH, D_MODEL), lambda k: (0, 0))],
        out_shape=[jax.ShapeDtypeStruct((IN_WIDTH, D_MODEL), F32)],
        scratch_shapes=[], operands=(*pieces, x2d), rider=rider, semantics=["arbitrary"])


def _inproj_bwd(dz1, pieces, w_main, w_kv, rider=None):
    t = dz1.shape[0]
    tm = 512

    def body(dz_ref, p0, p1, p2, p3, p4, pkv, wm_ref, wkv_ref, o_ref):
        acc = ALPHA * dz_ref[...]
        for i, ref in enumerate((p0, p1, p2, p3, p4)):
            acc = acc + _dot(ref[...], wm_ref[i * 512:(i + 1) * 512, :])
        o_ref[...] = acc + _dot(pkv[...], wkv_ref[...])

    row = lambda w: pl.BlockSpec((tm, w), lambda i: (i, 0))
    const = lambda s: pl.BlockSpec(s, lambda i: (0, 0))
    return _hosted_call(
        body, "inproj_bwd", (t // tm,),
        in_specs=[row(D_MODEL)] + [row(512)] * 5 + [row(256), const(w_main.shape), const(w_kv.shape)],
        out_specs=[row(D_MODEL)],
        out_shape=[jax.ShapeDtypeStruct((t, D_MODEL), F32)],
        scratch_shapes=[], operands=(dz1, *pieces, w_main, w_kv), rider=rider)


def _coords():
    return lax.axis_index("x"), lax.axis_index("y"), lax.axis_index("c")


def _chip_of(x, y, rel):
    return (1 - x if rel & 2 else x), (1 - y if rel & 1 else y)


def _all_gather_weights(shards):
    first = _gather_near_rider(shards)
    later = [f(first.out_shapes, chained=True) for f in (_gather_relay_rider, _gather_pass_rider)]
    return _run_riders("gather_weights", shards, first.out_shapes, [first] + later)


def _run_riders(name, ins, out_shapes, riders):
    n_in, n_out = len(ins), len(out_shapes)

    def body(*refs):
        in_refs, out_refs = refs[:n_in], refs[n_in:n_in + n_out]
        k = n_in + n_out
        for r in riders:
            sems = refs[k:k + len(r.sems)]
            k += len(r.sems)
            r.start(in_refs, out_refs, sems)
            r.finish(in_refs, out_refs, sems)

    hbm = pl.BlockSpec(memory_space=pl.ANY)
    return pl.pallas_call(
        body, name=name, in_specs=[hbm] * n_in, out_specs=[hbm] * n_out, out_shape=list(out_shapes),
        scratch_shapes=[s for r in riders for s in r.sems],
    )(*ins)


def _gather_half(outs, w, chip, cc):
    h = outs[w].shape[1] // 2
    return outs[w].at[chip, pl.ds(cc * h, h), :]


NEAR = (1, 2)


def _gather_near_rider(shards):
    nw = len(shards)

    def copies(ins, outs, sems, arrivals):
        send, recv, lsend, lrecv = sems
        x, y, c = _coords()
        me = 2 * x + y
        own = [pltpu.make_async_remote_copy(
            src_ref=ins[w], dst_ref=outs[w].at[me], send_sem=lsend.at[w], recv_sem=lrecv.at[w],
            device_id=(x, y, 1 - c), device_id_type=MESH) for w in range(nw)]
        out, arrive = [], []
        for rel in NEAR:
            kx, ky = _chip_of(x, y, rel)
            for w in range(nw):
                h = shards[w].shape[0] // 2
                sem = dict(send_sem=send.at[w * 2 + rel - 1], recv_sem=recv.at[w * 2 + rel - 1],
                           device_id=(kx, ky, c), device_id_type=MESH)
                out.append(pltpu.make_async_remote_copy(
                    src_ref=ins[w].at[pl.ds(c * h, h), :], dst_ref=_gather_half(outs, w, me, c), **sem))
                if arrivals:
                    theirs = _gather_half(outs, w, 2 * kx + ky, c)
                    arrive.append(pltpu.make_async_remote_copy(src_ref=theirs, dst_ref=theirs, **sem))
        return own, out, arrive

    def start(ins, outs, sems):
        own, out, _ = copies(ins, outs, sems, arrivals=False)
        for cp in own + out:
            cp.start()

    def finish(ins, outs, sems):
        own, out, arrive = copies(ins, outs, sems, arrivals=True)
        for cp in arrive:
            cp.wait_recv()
        for cp in out:
            cp.wait_send()
        for cp in own:
            cp.wait()

    dma = pltpu.SemaphoreType.DMA
    return _Rider(shards, [jax.ShapeDtypeStruct((N_SHARD,) + s.shape, s.dtype) for s in shards],
                  [dma((2 * nw,)), dma((2 * nw,)), dma((nw,)), dma((nw,))], start, finish)


def _gather_relay_rider(gathered, chained=False):
    nw = len(gathered)

    def quarter(outs, w, chip, c, p):
        q = outs[w].shape[1] // 4
        return outs[w].at[chip, pl.ds(c * 2 * q + p * q, q), :]

    def copies(outs, sems):
        send, recv = sems
        x, y, c = _coords()
        (yx, yy), (xx, xy), (dx, dy) = (_chip_of(x, y, rel) for rel in (1, 2, 3))
        out, arrive = [], []
        for w in range(nw):
            for p, (src_chip, dst) in enumerate(((2 * xx + xy, (yx, yy)), (2 * yx + yy, (xx, xy)))):
                rows = quarter(outs, w, src_chip, c, p)
                sem = dict(send_sem=send.at[w * 2 + p], recv_sem=recv.at[w * 2 + p], device_id_type=MESH)
                out.append(pltpu.make_async_remote_copy(src_ref=rows, dst_ref=rows, device_id=(*dst, c), **sem))
                mine = quarter(outs, w, 2 * dx + dy, c, p)
                arrive.append(pltpu.make_async_remote_copy(src_ref=mine, dst_ref=mine, device_id=(*dst, c), **sem))
        return out, arrive

    def start(ins, outs, sems):
        for cp in copies(outs, sems)[0]:
            cp.start()

    def finish(ins, outs, sems):
        out, arrive = copies(outs, sems)
        for cp in arrive:
            cp.wait_recv()
        for cp in out:
            cp.wait_send()

    dma = pltpu.SemaphoreType.DMA
    shapes = [jax.ShapeDtypeStruct(g.shape, g.dtype) for g in gathered]
    if chained:
        return _Rider([], [], [dma((2 * nw,)), dma((2 * nw,))], start, finish)
    return _Rider(gathered, shapes, [dma((2 * nw,)), dma((2 * nw,))], start, finish,
                  aliases={w: w for w in range(nw)})


def _gather_pass_rider(gathered, chained=False):
    nw = len(gathered)

    def copies(outs, sems, cc):
        send, recv = sems
        x, y, c = _coords()
        res = []
        for rel in (1, 2, 3):
            kx, ky = _chip_of(x, y, rel)
            for w in range(nw):
                rows = _gather_half(outs, w, 2 * kx + ky, cc)
                res.append(pltpu.make_async_remote_copy(
                    src_ref=rows, dst_ref=rows, send_sem=send.at[w * 3 + rel - 1], recv_sem=recv.at[w * 3 + rel - 1],
                    device_id=(x, y, 1 - c), device_id_type=MESH))
        return res

    def start(ins, outs, sems):
        for cp in copies(outs, sems, lax.axis_index("c")):
            cp.start()

    def finish(ins, outs, sems):
        c = lax.axis_index("c")
        for cp in copies(outs, sems, 1 - c):
            cp.wait_recv()
        for cp in copies(outs, sems, c):
            cp.wait_send()

    dma = pltpu.SemaphoreType.DMA
    shapes = [jax.ShapeDtypeStruct(g.shape, g.dtype) for g in gathered]
    if chained:
        return _Rider([], [], [dma((3 * nw,)), dma((3 * nw,))], start, finish)
    return _Rider(gathered, shapes, [dma((3 * nw,)), dma((3 * nw,))], start, finish,
                  aliases={w: w for w in range(nw)})


def _exchange_halves_rider(parts):
    nw = len(parts)

    def copies(ins, outs, sems):
        send, recv = sems
        x, y, c = _coords()
        res = []
        for w in range(nw):
            h = parts[w].shape[1] // 2
            res.append(pltpu.make_async_remote_copy(
                src_ref=ins[w].at[:, pl.ds((1 - c) * h, h), :], dst_ref=outs[w],
                send_sem=send.at[w], recv_sem=recv.at[w], device_id=(x, y, 1 - c), device_id_type=MESH))
        return res

    def start(ins, outs, sems):
        for cp in copies(ins, outs, sems):
            cp.start()

    def finish(ins, outs, sems):
        for cp in copies(ins, outs, sems):
            cp.wait()

    dma = pltpu.SemaphoreType.DMA
    return _Rider(parts, [jax.ShapeDtypeStruct((N_SHARD, p.shape[1] // 2, p.shape[2]), p.dtype) for p in parts],
                  [dma((nw,)), dma((nw,))], start, finish)


def _add_halves(parts, theirs, pos):
    nw = len(parts)
    split = 2

    def body(pos_ref, *refs):
        ins, oth = refs[:nw], refs[nw:2 * nw]
        o32, o16 = refs[2 * nw:3 * nw], refs[3 * nw:]
        sums = [ins[w][...] + oth[w][...].astype(F32) for w in range(nw)]
        for w in range(nw):
            o16[w][...] = sums[w].astype(BF16)

        @pl.when(pl.program_id(1) == pos_ref[0])
        def _():
            for w in range(nw):
                o32[w][...] = sums[w]

    in_specs, oth_specs, o32_specs, shapes32, shapes16 = [], [], [], [], []
    for p in parts:
        hb = p.shape[1] // 2 // split
        blk = (None, hb, p.shape[2])
        in_specs.append(pl.BlockSpec(blk, lambda i, j, pos_ref: (j, pos_ref[1] * split + i, 0)))
        oth_specs.append(pl.BlockSpec(blk, lambda i, j, pos_ref: (j, i, 0)))
        o32_specs.append(pl.BlockSpec((hb, p.shape[2]), lambda i, j, pos_ref: (i, 0)))
        shapes32.append(jax.ShapeDtypeStruct((p.shape[1] // 2, p.shape[2]), F32))
        shapes16.append(jax.ShapeDtypeStruct((N_SHARD, p.shape[1] // 2, p.shape[2]), BF16))
    return pl.pallas_call(
        body, name="add_halves",
        grid_spec=pltpu.PrefetchScalarGridSpec(
            num_scalar_prefetch=1, grid=(split, N_SHARD),
            in_specs=in_specs + oth_specs, out_specs=o32_specs + oth_specs),
        out_shape=shapes32 + shapes16,
        compiler_params=_params("parallel", "arbitrary", vmem=VMEM_LIMIT),
    )(pos, *parts, *theirs)


def _exchange_chips_rider(sums16):
    nw = len(sums16)

    def copies(ins, outs, sems):
        send, recv = sems
        x, y, c = _coords()
        res = []
        for rel in (1, 2, 3):
            kx, ky = _chip_of(x, y, rel)
            for w in range(nw):
                res.append(pltpu.make_async_remote_copy(
                    src_ref=ins[w].at[2 * kx + ky], dst_ref=outs[w].at[rel - 1],
                    send_sem=send.at[w * 3 + rel - 1], recv_sem=recv.at[w * 3 + rel - 1],
                    device_id=(kx, ky, c), device_id_type=MESH))
        return res

    def start(ins, outs, sems):
        for cp in copies(ins, outs, sems):
            cp.start()

    def finish(ins, outs, sems):
        for cp in copies(ins, outs, sems):
            cp.wait()

    dma = pltpu.SemaphoreType.DMA
    return _Rider(sums16, [jax.ShapeDtypeStruct((3,) + s.shape[1:], BF16) for s in sums16],
                  [dma((3 * nw,)), dma((3 * nw,))], start, finish)


def _add_chips(sums32, theirs, pos):
    nw = len(sums32)
    split = 2

    def body(pos_ref, *refs):
        ins, oth, outs = refs[:nw], refs[nw:2 * nw], refs[2 * nw:]
        for w in range(nw):
            acc = ins[w][...]
            for r in range(3):
                acc = acc + oth[w][r].astype(F32)
            outs[w][...] = acc

    in_specs, oth_specs, out_specs, shapes = [], [], [], []
    for s in sums32:
        hb = s.shape[0] // split
        in_specs.append(pl.BlockSpec((hb, s.shape[1]), lambda i, pos_ref: (i, 0)))
        oth_specs.append(pl.BlockSpec((3, hb, s.shape[1]), lambda i, pos_ref: (0, i, 0)))
        out_specs.append(pl.BlockSpec((hb, s.shape[1]), lambda i, pos_ref: (pos_ref[1] * split + i, 0)))
        shapes.append(jax.ShapeDtypeStruct((2 * s.shape[0], s.shape[1]), F32))
    return pl.pallas_call(
        body, name="add_chips",
        grid_spec=pltpu.PrefetchScalarGridSpec(
            num_scalar_prefetch=1, grid=(split,), in_specs=in_specs + oth_specs, out_specs=out_specs),
        out_shape=shapes,
        compiler_params=_params("parallel", vmem=VMEM_LIMIT),
    )(pos, *sums32, *theirs)


def _join_halves(shards):
    nw = len(shards)

    def body(*refs):
        outs = refs[nw:2 * nw]
        send, recv = refs[2 * nw:]
        x, y, c = _coords()

        def copy(w, cc):
            h = shards[w].shape[0] // 2
            rows = outs[w].at[pl.ds(cc * h, h), :]
            return pltpu.make_async_remote_copy(
                src_ref=rows, dst_ref=rows, send_sem=send.at[w], recv_sem=recv.at[w],
                device_id=(x, y, 1 - c), device_id_type=MESH)

        for w in range(nw):
            copy(w, c).start()
        for w in range(nw):
            copy(w, 1 - c).wait_recv()
            copy(w, c).wait_send()

    hbm = pl.BlockSpec(memory_space=pl.ANY)
    return pl.pallas_call(
        body, name="join_halves",
        in_specs=[hbm] * nw, out_specs=[hbm] * nw,
        out_shape=[jax.ShapeDtypeStruct(s.shape, F32) for s in shards],
        input_output_aliases={w: w for w in range(nw)},
        scratch_shapes=[pltpu.SemaphoreType.DMA((nw,)), pltpu.SemaphoreType.DMA((nw,))],
    )(*shards)


def _adamw_math(w, g, m, v):
    m = ADAM_B1 * m + (1.0 - ADAM_B1) * g
    v = ADAM_B2 * v + (1.0 - ADAM_B2) * (g * g)
    m_hat = m / (1.0 - ADAM_B1 ** ADAM_STEP)
    v_hat = v / (1.0 - ADAM_B2 ** ADAM_STEP)
    delta = -ADAM_LR * (m_hat / (jnp.sqrt(v_hat) + ADAM_EPS) + ADAM_WD * w)
    return delta, m, v


def _adamw(ws, gs, ms, vs):
    nw = len(ws)
    split = 8

    def body(*refs):
        w_r, g_r, m_r, v_r = (refs[i * nw:(i + 1) * nw] for i in range(4))
        g_o, d_o, m_o, v_o = (refs[(4 + i) * nw:(5 + i) * nw] for i in range(4))
        for k in range(nw):
            g = g_r[k][...]
            d, m, v = _adamw_math(w_r[k][...], g, m_r[k][...], v_r[k][...])
            g_o[k][...] = g
            d_o[k][...] = d
            m_o[k][...] = m
            v_o[k][...] = v

    specs = [pl.BlockSpec((w.shape[0] // split, w.shape[1]), lambda i: (i, 0)) for w in ws]
    shapes = [jax.ShapeDtypeStruct(w.shape, F32) for w in ws]
    outs = pl.pallas_call(
        body, name="adamw", grid=(split,),
        in_specs=specs * 4, out_specs=specs * 4, out_shape=shapes * 4,
        compiler_params=_params("parallel", vmem=VMEM_LIMIT),
    )(*ws, *gs, *ms, *vs)
    return outs[:nw], outs[nw:2 * nw], outs[2 * nw:3 * nw], outs[3 * nw:]


SMALL_ROWS = 8
SMALL_COLS = D_MODEL
LOSS_COL = RET_WIDTH + 24


def _small_allreduce_adamw(part, w, m, v, rider=None):
    def body(part_ref, w_ref, m_ref, v_ref, g_out, d_out, m_out, v_out, all_ref, send, recv):
        x, y, c = _coords()
        me = 4 * x + 2 * y + c
        all_ref[me] = part_ref[...]
        copies = []
        for rel in range(1, 8):
            px = 1 - x if rel & 4 else x
            py = 1 - y if rel & 2 else y
            pc = 1 - c if rel & 1 else c
            copies.append(pltpu.make_async_remote_copy(
                src_ref=part_ref, dst_ref=all_ref.at[me],
                send_sem=send.at[rel - 1], recv_sem=recv.at[rel - 1], device_id=(px, py, pc), device_id_type=MESH))
        for cp in copies:
            cp.start()
        for cp in copies:
            cp.wait()
        g = all_ref[0]
        for k in range(1, 8):
            g = g + all_ref[k]
        d, mn, vn = _adamw_math(w_ref[...], g, m_ref[...], v_ref[...])
        g_out[...] = g
        d_out[...] = d
        m_out[...] = mn
        v_out[...] = vn

    vm = pl.BlockSpec(memory_space=pltpu.VMEM)
    shape = jax.ShapeDtypeStruct((SMALL_ROWS, SMALL_COLS), F32)
    return _hosted_call(
        body, "small_allreduce_adamw", (1,),
        in_specs=[vm] * 4, out_specs=[vm] * 4, out_shape=[shape] * 4,
        scratch_shapes=[pltpu.VMEM((8, SMALL_ROWS, SMALL_COLS), F32),
                        pltpu.SemaphoreType.DMA((7,)), pltpu.SemaphoreType.DMA((7,))],
        operands=(part, w, m, v), rider=rider, semantics=["arbitrary"])


SMALL_NAMES = ("ret_decay_fwd", "ret_decay_bwd", "attn_sink", "ret_gn_gain",
               "ln1_gain", "ln1_bias", "ln2_gain", "ln2_bias")


LN_NAMES = ("ln1_gain", "ln1_bias", "ln2_gain", "ln2_bias")


def _pack_small(vals, extra=None):
    tail = jnp.zeros((1, 1), F32) if extra is None else extra.reshape(1, 1)
    row4 = jnp.concatenate([vals["ret_gn_gain"], vals["ret_decay_fwd"], vals["ret_decay_bwd"], vals["attn_sink"],
                            tail, jnp.zeros((1, SMALL_COLS - LOSS_COL - 1), F32)], axis=1)
    rows = [vals[n] for n in LN_NAMES] + [row4, jnp.zeros((SMALL_ROWS - 5, SMALL_COLS), F32)]
    return jnp.concatenate(rows, axis=0)


def _unpack_small(packed):
    out = {n: packed[i:i + 1] for i, n in enumerate(LN_NAMES)}
    o = RET_WIDTH
    out.update(ret_gn_gain=packed[4:5, 0:o], ret_decay_fwd=packed[4:5, o:o + 8],
               ret_decay_bwd=packed[4:5, o + 8:o + 16], attn_sink=packed[4:5, o + 16:o + 24])
    return out


def _local_step(x, p, tgt, w_in_t, rest, small, pos=None, small_state=None):
    bsz, s, _ = x.shape
    t = bsz * s
    x2d = x.reshape(t, D_MODEL)
    p2d = p.reshape(t, PLE_DIM)
    tgt2d = tgt.reshape(t, D_MODEL)
    dec_f = small["ret_decay_fwd"].reshape(8)
    dec_b = small["ret_decay_bwd"].reshape(8)
    lg_f = jnp.log1p(-jnp.exp2(dec_f))
    lg_b = jnp.log1p(-jnp.exp2(dec_b))
    per_lane = lambda v: jnp.repeat(v, HEAD_DIM).reshape(4, 1, LANES)
    lgf_l, lgb_l = per_lane(lg_f), per_lane(lg_b)
    sink = small["attn_sink"].reshape(8)
    slopes = 2.0 ** (-(jnp.arange(8, dtype=F32) + 1.0))
    gn_gain = small["ret_gn_gain"]
    g1, b1, g2, b2 = (small[n] for n in ("ln1_gain", "ln1_bias", "ln2_gain", "ln2_bias"))

    dist = pos is not None
    shard = dict(zip(REST_NAMES, rest)) if dist else {}
    near = lambda names: _gather_near_rider([shard[n] for n in names])
    wave1, wave2 = ("w_out", "w_ple_gate", "w_ffn_gate"), ("w_ffn_up", "w_ffn_down", "w_ple_proj")
    n1 = len(wave1)
    u, *o1 = _inproj(x2d, w_in_t, rider=near(wave1) if dist else None)
    u3 = u.reshape(bsz, s, IN_WIDTH)
    y_hat, y_rstd, y_ret, ret_rb, ret_kvf, *o2 = _ret_fwd(u3, lgf_l, lgb_l, gn_gain, rider=_merge_riders(
        [_gather_relay_rider(o1), near(wave2)]) if dist else None)
    y_att, att_p, att_ps, *o3 = _attn_fwd(u3, slopes, sink, rider=_merge_riders(
        [_gather_pass_rider(o2[:n1]), _gather_relay_rider(o2[n1:])]) if dist else None)
    gathered = dict(zip(wave1, o3[:n1]))
    w_out = _assemble_weights({"w_out": gathered["w_out"]})["w_out"] if dist else rest["w_out"]
    zh1, r1, hb, *o4 = _outproj_ln1(y_ret.reshape(t, RET_WIDTH), y_att.reshape(t, ATTN_WIDTH), x2d, w_out, g1, b1,
                                    rider=_gather_pass_rider(o3[n1:]) if dist else None)
    gathered.update(zip(wave2, o4))
    wts = _assemble_weights(gathered) if dist else rest
    dz2, dz2b, gs, us, acts, pg, ple, sq, dg2, db2 = _ffn_fwd(
        zh1, hb, p2d, tgt2d, g1, b1, g2, b2, wts["gate4"], wts["up4"], wts["down4"], wts["ple_proj"], wts["ple_gate"])
    dgs, dus, dsp, dple, dz1, dyr, dya, dg1, db1 = _ffn_bwd(dz2, gs, us, pg, ple, zh1, r1, g1, wts["gate4"],
                                                          wts["up4"], wts["down4"], wts["ple_gate"], wts["w_out"])
    ffn_parts = list(_wgrad_ffn(acts, dgs, dus, hb, dz2b))
    d_w_out, d_ple_gate, d_ple_proj, *th_ffn = _wgrad_misc(
        y_ret.reshape(t, RET_WIDTH), y_att.reshape(t, ATTN_WIDTH), dz1, hb, dsp, p2d, dple,
        rider=_exchange_halves_rider(ffn_parts[:2]) if dist else None)
    misc_parts = [d_w_out.reshape(N_SHARD, D_MODEL // N_SHARD, D_MODEL),
                  d_ple_proj.reshape(PLE_DIM, N_SHARD, D_MODEL // N_SHARD).transpose(1, 0, 2),
                  d_ple_gate.reshape(N_SHARD, D_MODEL // N_SHARD, D_MODEL)]
    dyr3, dya3 = dyr.reshape(bsz, s, RET_WIDTH), dya.reshape(bsz, s, ATTN_WIDTH)
    if dist:
        s_gu = _add_halves(ffn_parts[:2], th_ffn, pos)
        quarter = FFN_SHARD // 4
        up_lo, up_hi = s_gu[3][:, :quarter], s_gu[3][:, quarter:]
        later_parts = [ffn_parts[2]] + misc_parts
        drq, drk, drv, drg, rpart, *o5 = _ret_bwd(u3, y_hat, y_rstd, (ret_rb, ret_kvf), dyr3, lgf_l, lgb_l, gn_gain,
                                                  rider=_merge_riders(
            [_exchange_chips_rider([s_gu[2], up_lo]), _exchange_halves_rider(later_parts)]))
        s_dm = _add_halves(later_parts, o5[2:], pos)
        daq, dakv, spart, *o6 = _attn_bwd(u3, dya3, att_p, att_ps, rider=_exchange_chips_rider([up_hi, s_dm[4]]))
    else:
        drq, drk, drv, drg, rpart = _ret_bwd(u3, y_hat, y_rstd, (ret_rb, ret_kvf), dyr3, lgf_l, lgb_l, gn_gain)
        daq, dakv, spart = _attn_bwd(u3, dya3, att_p, att_ps)
    pieces = [a.reshape(t, -1) for a in (drq, drk, drv, drg, daq, dakv)]
    kv0 = CB_AK * LANES
    w_kv = jnp.concatenate([w_in_t[kv0 + o:kv0 + o + HEAD_DIM] for o in KV_ORDER], axis=0)
    d_in, *o7 = _wgrad_in(pieces, x2d, rider=_exchange_chips_rider(list(s_dm[5:])) if dist else None)
    d_in = d_in.reshape(N_SHARD, FFN_SHARD, D_MODEL)

    rsum = rpart
    lane_heads = lambda row: jnp.sum(row.reshape(4, 2, HEAD_DIM), axis=-1).reshape(8)
    dlg_f = lane_heads(rsum[:, 0, :]) + jnp.stack([jnp.sum(rsum[:, 2, :], -1), jnp.sum(rsum[:, 3, :], -1)], 1).reshape(8)
    dlg_b = lane_heads(rsum[:, 1, :]) + jnp.stack([jnp.sum(rsum[:, 4, :], -1), jnp.sum(rsum[:, 5, :], -1)], 1).reshape(8)
    chain = lambda d: -(math.log(2.0) * jnp.exp2(d)) / (1.0 - jnp.exp2(d))
    grads_small = {
        "ret_decay_fwd": (dlg_f * chain(dec_f)).reshape(1, 8),
        "ret_decay_bwd": (dlg_b * chain(dec_b)).reshape(1, 8),
        "attn_sink": jnp.sum(spart, axis=0)[:, 0:4, 0].reshape(1, 8),
        "ret_gn_gain": rsum[:, 6, :].reshape(1, RET_WIDTH),
        "ln1_gain": dg1, "ln1_bias": db1, "ln2_gain": dg2, "ln2_bias": db2,
    }
    if not dist:
        grad_x, = _inproj_bwd(dz1, pieces, w_in_t[:kv0], w_kv)
        grads_rest = [misc_parts[0]] + ffn_parts + misc_parts[1:]
        return sq[0, 0], grad_x.reshape(bsz, s, D_MODEL), d_in, grads_rest, grads_small
    *small_out, th_in = _small_allreduce_adamw(_pack_small(grads_small, sq[0, 0]), *small_state,
                                               rider=_exchange_halves_rider([d_in]))
    s_in = _add_halves([d_in], [th_in], pos)
    grad_x, chips_in = _inproj_bwd(dz1, pieces, w_in_t[:kv0], w_kv, rider=_exchange_chips_rider([s_in[1]]))
    sums32 = [s_in[0], s_dm[1], s_gu[0], s_gu[1], s_dm[0], s_dm[2], s_dm[3]]
    chips_up = jnp.concatenate([o5[1], o6[0]], axis=1)
    from_chips = [chips_in, o7[0], o5[0], chips_up, o6[1], o7[1], o7[2]]
    return grad_x.reshape(bsz, s, D_MODEL), sums32, from_chips, small_out


BIG_NAMES = ("w_in", "w_out", "w_ffn_gate", "w_ffn_up", "w_ffn_down", "w_ple_proj", "w_ple_gate")
REST_NAMES = BIG_NAMES[1:]
TRANSPOSED = ("w_in", "w_ffn_gate", "w_ffn_up")
WEIGHT_ORDER = ("w_in", "ret_decay_fwd", "ret_decay_bwd", "ret_gn_gain", "attn_sink", "w_out", "ln1_gain",
                "ln1_bias", "w_ffn_gate", "w_ffn_up", "w_ffn_down", "w_ple_proj", "w_ple_gate", "ln2_gain", "ln2_bias")


def _shard_rows(name, a):
    return jnp.swapaxes(a[0], 0, 1) if name in TRANSPOSED else a[0]


def _unshard_rows(name, a):
    return (jnp.swapaxes(a, 0, 1) if name in TRANSPOSED else a)[None]


def _assemble_weights(gathered):
    cols = lambda a: a.transpose(1, 0, 2).reshape(a.shape[1], N_SHARD * a.shape[2])
    rows = lambda a: a.reshape(N_SHARD * a.shape[1], a.shape[2])
    same = lambda a: a
    layout = {"w_out": ("w_out", rows), "w_ffn_gate": ("gate4", same), "w_ffn_up": ("up4", same),
              "w_ffn_down": ("down4", same), "w_ple_proj": ("ple_proj", cols), "w_ple_gate": ("ple_gate", rows)}
    return {layout[n][0]: layout[n][1](a) for n, a in gathered.items()}


def kernel(x, p, w_in, ret_decay_fwd, ret_decay_bwd, ret_gn_gain, attn_sink, w_out, ln1_gain, ln1_bias, w_ffn_gate, w_ffn_up, w_ffn_down, w_ple_proj, w_ple_gate, ln2_gain, ln2_bias, loss_target, m_w_in, m_ret_decay_fwd, m_ret_decay_bwd, m_ret_gn_gain, m_attn_sink, m_w_out, m_ln1_gain, m_ln1_bias, m_w_ffn_gate, m_w_ffn_up, m_w_ffn_down, m_w_ple_proj, m_w_ple_gate, m_ln2_gain, m_ln2_bias, v_w_in, v_ret_decay_fwd, v_ret_decay_bwd, v_ret_gn_gain, v_attn_sink, v_w_out, v_ln1_gain, v_ln1_bias, v_w_ffn_gate, v_w_ffn_up, v_w_ffn_down, v_w_ple_proj, v_w_ple_gate, v_ln2_gain, v_ln2_bias):
    w = dict(w_in=w_in, ret_decay_fwd=ret_decay_fwd, ret_decay_bwd=ret_decay_bwd, ret_gn_gain=ret_gn_gain,
             attn_sink=attn_sink, w_out=w_out, ln1_gain=ln1_gain, ln1_bias=ln1_bias, w_ffn_gate=w_ffn_gate,
             w_ffn_up=w_ffn_up, w_ffn_down=w_ffn_down, w_ple_proj=w_ple_proj, w_ple_gate=w_ple_gate,
             ln2_gain=ln2_gain, ln2_bias=ln2_bias)
    m = dict(w_in=m_w_in, ret_decay_fwd=m_ret_decay_fwd, ret_decay_bwd=m_ret_decay_bwd, ret_gn_gain=m_ret_gn_gain,
             attn_sink=m_attn_sink, w_out=m_w_out, ln1_gain=m_ln1_gain, ln1_bias=m_ln1_bias, w_ffn_gate=m_w_ffn_gate,
             w_ffn_up=m_w_ffn_up, w_ffn_down=m_w_ffn_down, w_ple_proj=m_w_ple_proj, w_ple_gate=m_w_ple_gate,
             ln2_gain=m_ln2_gain, ln2_bias=m_ln2_bias)
    v = dict(w_in=v_w_in, ret_decay_fwd=v_ret_decay_fwd, ret_decay_bwd=v_ret_decay_bwd, ret_gn_gain=v_ret_gn_gain,
             attn_sink=v_attn_sink, w_out=v_w_out, ln1_gain=v_ln1_gain, ln1_bias=v_ln1_bias, w_ffn_gate=v_w_ffn_gate,
             w_ffn_up=v_w_ffn_up, w_ffn_down=v_w_ffn_down, w_ple_proj=v_w_ple_proj, w_ple_gate=v_w_ple_gate,
             ln2_gain=v_ln2_gain, ln2_bias=v_ln2_bias)
    big = lambda d: [_shard_rows(n, d[n]) for n in BIG_NAMES]
    small = lambda d: {n: d[n] for n in SMALL_NAMES}

    chip = 2 * lax.axis_index("x") + lax.axis_index("y")
    pos = jnp.stack([chip, lax.axis_index("c")]).astype(jnp.int32)

    shards = [a.astype(BF16) for a in big(w)]
    (w_in4,) = _all_gather_weights(shards[:1])
    w_in_t = w_in4.reshape(IN_WIDTH, D_MODEL)
    grad_x, sums32, from_chips, (g_s, d_s, m_s, v_s) = _local_step(
        x, p[0], loss_target, w_in_t, shards[1:], small(w), pos=pos,
        small_state=(_pack_small(small(w)), _pack_small(small(m)), _pack_small(small(v))))
    g_big, d_big, m_big, v_big = _adamw(big(w), _join_halves(_add_chips(sums32, from_chips, pos)), big(m), big(v))
    loss = g_s[4, LOSS_COL] * (0.5 / D_MODEL)

    def tree(bigs, packed):
        out = {n: _unshard_rows(n, a) for n, a in zip(BIG_NAMES, bigs)}
        out.update(_unpack_small(packed))
        return [out[n] for n in WEIGHT_ORDER]

    return (loss, grad_x, *tree(g_big, g_s), *tree(d_big, d_s), *tree(m_big, m_s), *tree(v_big, v_s))
```

```python
import functools
import math

import jax
import jax.numpy as jnp
from jax import lax
from jax.experimental import pallas as pl
from jax.experimental.pallas import tpu as pltpu

F32 = jnp.float32
BF16 = jnp.bfloat16

D_MODEL = 1024
HEAD_DIM = 64
RET_HEADS = 8
ATTN_HEADS = 8
RET_WIDTH = 512
ATTN_WIDTH = 512
KV_WIDTH = 128
IN_WIDTH = 2816
FFN = 2816
N_SHARD = 4
FFN_SHARD = FFN // N_SHARD
PLE_DIM = 256
CHUNK = 128
LANES = 128
ALPHA = 2.0 ** 0.25
LN_EPS = 1e-5
GN_EPS = 1e-5
NEG_INF = -1e30
ADAM_LR = 0.001
ADAM_B1 = 0.9
ADAM_B2 = 0.999
ADAM_EPS = 1e-08
ADAM_WD = 0.01
ADAM_STEP = 10
VMEM_LIMIT = 56 * 1024 * 1024
MESH = pl.DeviceIdType.MESH

CB_RQ, CB_RK, CB_RV, CB_RG, CB_AQ, CB_AK, CB_AV = 0, 4, 8, 12, 16, 20, 21


def _dot(a, b):
    return jnp.dot(a, b, preferred_element_type=F32)


def _dot_nt(a, b):
    return lax.dot_general(a, b, (((1,), (1,)), ((), ())), preferred_element_type=F32)


def _dot_tn(a, b):
    return lax.dot_general(a, b, (((0,), (0,)), ((), ())), preferred_element_type=F32)


def _sigmoid(x):
    return 1.0 / (1.0 + jnp.exp(-x))


def _params(*sem, vmem=None):
    return pltpu.CompilerParams(dimension_semantics=tuple(sem) if sem else None, vmem_limit_bytes=vmem)


class _Rider:
    def __init__(self, ins, out_shapes, sems, start, finish, aliases=None):
        self.ins, self.out_shapes, self.sems = list(ins), list(out_shapes), list(sems)
        self.start, self.finish, self.aliases = start, finish, dict(aliases or {})


def _merge_riders(riders):
    riders = [r for r in riders if r is not None]
    if len(riders) == 1:
        return riders[0]
    bounds, aliases = [], {}
    i0 = o0 = s0 = 0
    for r in riders:
        bounds.append((i0, o0, s0))
        aliases.update({i0 + i: o0 + o for i, o in r.aliases.items()})
        i0, o0, s0 = i0 + len(r.ins), o0 + len(r.out_shapes), s0 + len(r.sems)

    def each(method):
        def run(ins, outs, sems):
            for r, (i, o, s) in zip(riders, bounds):
                getattr(r, method)(ins[i:i + len(r.ins)], outs[o:o + len(r.out_shapes)], sems[s:s + len(r.sems)])
        return run

    return _Rider([a for r in riders for a in r.ins], [a for r in riders for a in r.out_shapes],
                  [a for r in riders for a in r.sems], each("start"), each("finish"), aliases)


def _hosted_call(body, name, grid, in_specs, out_specs, out_shape, scratch_shapes, operands, rider=None,
                 semantics=None):
    n_in, n_out, n_scr = len(in_specs), len(out_specs), len(scratch_shapes)
    if rider is None:
        return pl.pallas_call(
            body, name=name, grid=grid, in_specs=in_specs, out_specs=out_specs, out_shape=out_shape,
            scratch_shapes=scratch_shapes,
            compiler_params=_params(*(semantics or ["parallel"] * len(grid)), vmem=VMEM_LIMIT))(*operands)
    r_in, r_out = len(rider.ins), len(rider.out_shapes)

    def full_body(*refs):
        main_in, rin = refs[:n_in], refs[n_in:n_in + r_in]
        o0 = n_in + r_in
        main_out, rout = refs[o0:o0 + n_out], refs[o0 + n_out:o0 + n_out + r_out]
        s0 = o0 + n_out + r_out
        main_scr, rsem = refs[s0:s0 + n_scr], refs[s0 + n_scr:]
        first = functools.reduce(jnp.logical_and, [pl.program_id(a) == 0 for a in range(len(grid))])
        last = functools.reduce(jnp.logical_and, [pl.program_id(a) == g - 1 for a, g in enumerate(grid)])

        @pl.when(first)
        def _():
            rider.start(rin, rout, rsem)

        body(*main_in, *main_out, *main_scr)

        @pl.when(last)
        def _():
            rider.finish(rin, rout, rsem)

    hbm = pl.BlockSpec(memory_space=pl.ANY)
    return pl.pallas_call(
        full_body, name=name, grid=grid,
        in_specs=list(in_specs) + [hbm] * r_in, out_specs=list(out_specs) + [hbm] * r_out,
        out_shape=list(out_shape) + rider.out_shapes,
        scratch_shapes=list(scratch_shapes) + rider.sems,
        input_output_aliases={n_in + i: n_out + o for i, o in rider.aliases.items()},
        compiler_params=_params(*(["arbitrary"] * len(grid)), vmem=VMEM_LIMIT),
    )(*operands, *rider.ins)


def _loop_pairs(n, body, init, per_trip=2):
    if n % per_trip:
        return lax.fori_loop(0, n, body, init)

    def trip(i, c):
        for j in range(per_trip):
            c = body(per_trip * i + j, c)
        return c

    return lax.fori_loop(0, n // per_trip, trip, init)


def _head_mean(x, m0):
    s0 = jnp.sum(jnp.where(m0, x, 0.0), axis=1, keepdims=True)
    s1 = jnp.sum(jnp.where(m0, 0.0, x), axis=1, keepdims=True)
    return jnp.where(m0, s0, s1) * (1.0 / HEAD_DIM)


def _inproj(x2d, w_in_t, rider=None):
    t = x2d.shape[0]
    tm = 512
    nb = 256

    def body(x_ref, w_ref, o_ref):
        xb = x_ref[...].astype(BF16)
        for n in range(0, IN_WIDTH, nb):
            o_ref[:, n:n + nb] = _dot_nt(xb, w_ref[n:n + nb, :]).astype(BF16)

    return _hosted_call(
        body, "inproj", (t // tm,),
        in_specs=[pl.BlockSpec((tm, D_MODEL), lambda i: (i, 0)),
                  pl.BlockSpec((IN_WIDTH, D_MODEL), lambda i: (0, 0))],
        out_specs=[pl.BlockSpec((tm, IN_WIDTH), lambda i: (i, 0))],
        out_shape=[jax.ShapeDtypeStruct((t, IN_WIDTH), BF16)],
        scratch_shapes=[], operands=(x2d, w_in_t), rider=rider)


def _outproj_ln1(y_ret, y_att, x2d, w_out, gain, bias, rider=None):
    t = x2d.shape[0]
    tm = 512

    def body(yr_ref, ya_ref, x_ref, w_ref, g_ref, b_ref, zh_ref, r_ref, hb_ref):
        mix = _dot(yr_ref[...], w_ref[0:RET_WIDTH, :]) + _dot(ya_ref[...], w_ref[RET_WIDTH:, :])
        z = ALPHA * x_ref[...] + mix
        mu = jnp.mean(z, axis=1, keepdims=True)
        zc = z - mu
        var = jnp.mean(zc * zc, axis=1, keepdims=True)
        r = lax.rsqrt(var + LN_EPS)
        zh = zc * r
        zh_ref[...] = zh
        r_ref[...] = r
        hb_ref[...] = (zh * g_ref[...] + b_ref[...]).astype(BF16)

    row = lambda w: pl.BlockSpec((tm, w), lambda i: (i, 0))
    const = lambda s: pl.BlockSpec(s, lambda i: (0, 0))
    return _hosted_call(
        body, "outproj_ln1", (t // tm,),
        in_specs=[row(RET_WIDTH), row(ATTN_WIDTH), row(D_MODEL), const((D_MODEL, D_MODEL)),
                  const((1, D_MODEL)), const((1, D_MODEL))],
        out_specs=[row(D_MODEL), row(1), row(D_MODEL)],
        out_shape=[jax.ShapeDtypeStruct((t, D_MODEL), F32), jax.ShapeDtypeStruct((t, 1), F32),
                   jax.ShapeDtypeStruct((t, D_MODEL), BF16)],
        scratch_shapes=[], operands=(y_ret, y_att, x2d, w_out, gain, bias), rider=rider)


def _load_resident(step, pairs, sems):
    copies = [pltpu.make_async_copy(src, dst, sems.at[i]) for i, (src, dst) in enumerate(pairs)]

    @pl.when(step == 0)
    def _():
        for cp in copies:
            cp.start()
        for cp in copies:
            cp.wait()


FFN_CHUNK = 256
N_FFN_CHUNK = FFN // FFN_CHUNK


def _resident_quarters(hbm, vmem):
    q = FFN // N_SHARD
    return [(hbm.at[pl.ds(j * q, q), :], vmem.at[pl.ds(j * q, q), :]) for j in range(N_SHARD)]


def _ln2_loss_tail(zh, mixed, tgt, g1, b1, g2, b2):
    z2 = ALPHA * (zh * g1 + b1) + mixed
    mu = jnp.mean(z2, axis=1, keepdims=True)
    zc = z2 - mu
    var = jnp.mean(zc * zc, axis=1, keepdims=True)
    r = lax.rsqrt(var + LN_EPS)
    zh2 = zc * r
    err = zh2 * g2 + b2 - tgt
    dy = err * (1.0 / D_MODEL)
    dzh = dy * g2
    m1 = jnp.mean(dzh, axis=1, keepdims=True)
    m2 = jnp.mean(dzh * zh2, axis=1, keepdims=True)
    dz2 = r * (dzh - m1 - zh2 * m2)
    return dz2, jnp.sum(err * err), jnp.sum(dy * zh2, axis=0, keepdims=True), jnp.sum(dy, axis=0, keepdims=True)


def _ffn_fwd(zh1, hb, p2d, tgt, g1, b1, g2, b2, wg4, wu4, wd4, wpe, wpg):
    t = zh1.shape[0]
    tm = 256
    wg_t, wu_t, wd_all = (w.reshape(FFN, D_MODEL) for w in (wg4, wu4, wd4))

    def body(zh_ref, hb_ref, p_ref, t_ref, g1_ref, b1_ref, g2_ref, b2_ref,
             wg_hbm, wu_hbm, wd_hbm, wpe_hbm, wpg_hbm,
             dz_ref, dzb_ref, gs_ref, us_ref, act_ref, pg_ref, ple_ref, loss_ref, dg2_ref, db2_ref,
             wg, wu, wd, wpe, wpg, wsem):
        step = pl.program_id(0)
        loads = _resident_quarters(wg_hbm, wg) + _resident_quarters(wu_hbm, wu) + _resident_quarters(wd_hbm, wd)
        _load_resident(step, loads + [(wpe_hbm, wpe), (wpg_hbm, wpg)], wsem)

        @pl.when(step == 0)
        def _():
            loss_ref[...] = jnp.zeros_like(loss_ref)
            dg2_ref[...] = jnp.zeros_like(dg2_ref)
            db2_ref[...] = jnp.zeros_like(db2_ref)

        hbv = hb_ref[...]
        ffn = jnp.zeros((tm, D_MODEL), F32)
        acts = []
        chunks = [slice(n * FFN_CHUNK, (n + 1) * FFN_CHUNK) for n in range(N_FFN_CHUNK)]
        for n in range(N_FFN_CHUNK + 1):
            if n < N_FFN_CHUNK:
                gj = _dot_nt(hbv, wg[chunks[n], :])
                uj = _dot_nt(hbv, wu[chunks[n], :])
                gs_ref[:, chunks[n]] = gj.astype(BF16)
                us_ref[:, chunks[n]] = uj.astype(BF16)
                acts.append((gj * _sigmoid(gj) * uj).astype(BF16))
                act_ref[:, chunks[n]] = acts[n]
            if n > 0:
                ffn = ffn + _dot(acts[n - 1], wd[chunks[n - 1], :])
        ple = _dot(p_ref[...].astype(BF16), wpe[...])
        pg = _sigmoid(_dot(hbv, wpg[...]))
        pg_ref[...] = pg.astype(BF16)
        ple_ref[...] = ple.astype(BF16)
        dz2, sq, dg2, db2 = _ln2_loss_tail(zh_ref[...], ffn + pg * ple, t_ref[...], g1_ref[...], b1_ref[...],
                                           g2_ref[...], b2_ref[...])
        dz_ref[...] = dz2
        dzb_ref[...] = dz2.astype(BF16)
        loss_ref[...] += sq
        dg2_ref[...] += dg2
        db2_ref[...] += db2

    row = lambda w: pl.BlockSpec((tm, w), lambda i: (i, 0))
    const = lambda s: pl.BlockSpec(s, lambda i: (0, 0))
    hid_shape = jax.ShapeDtypeStruct((t, FFN), BF16)
    hbm = pl.BlockSpec(memory_space=pl.ANY)
    return pl.pallas_call(
        body, name="ffn_fwd", grid=(t // tm,),
        in_specs=[row(D_MODEL), row(D_MODEL), row(PLE_DIM), row(D_MODEL),
                  const((1, D_MODEL)), const((1, D_MODEL)), const((1, D_MODEL)), const((1, D_MODEL)),
                  hbm, hbm, hbm, hbm, hbm],
        out_specs=[row(D_MODEL), row(D_MODEL), row(FFN), row(FFN), row(FFN), row(D_MODEL), row(D_MODEL),
                   const((8, LANES)), const((1, D_MODEL)), const((1, D_MODEL))],
        out_shape=[jax.ShapeDtypeStruct((t, D_MODEL), F32), jax.ShapeDtypeStruct((t, D_MODEL), BF16),
                   hid_shape, hid_shape, hid_shape,
                   jax.ShapeDtypeStruct((t, D_MODEL), BF16), jax.ShapeDtypeStruct((t, D_MODEL), BF16),
                   jax.ShapeDtypeStruct((8, LANES), F32),
                   jax.ShapeDtypeStruct((1, D_MODEL), F32), jax.ShapeDtypeStruct((1, D_MODEL), F32)],
        scratch_shapes=[pltpu.VMEM((FFN, D_MODEL), BF16), pltpu.VMEM((FFN, D_MODEL), BF16),
                        pltpu.VMEM((FFN, D_MODEL), BF16),
                        pltpu.VMEM(wpe.shape, BF16), pltpu.VMEM(wpg.shape, BF16),
                        pltpu.SemaphoreType.DMA((3 * N_SHARD + 2,))],
        compiler_params=_params("arbitrary", vmem=VMEM_LIMIT),
    )(zh1, hb, p2d, tgt, g1, b1, g2, b2, wg_t, wu_t, wd_all, wpe, wpg)


def _ret_tables(lgf, lgb):
    c = CHUNK
    row = lax.broadcasted_iota(jnp.int32, (c, LANES), 0).astype(F32)
    ii = lax.broadcasted_iota(jnp.int32, (c, c), 0).astype(F32)
    jj = lax.broadcasted_iota(jnp.int32, (c, c), 1).astype(F32)
    diff = ii - jj
    dmats = []
    for h in range(2):
        lf = lgf[:, h * HEAD_DIM:h * HEAD_DIM + 1]
        lb = lgb[:, h * HEAD_DIM:h * HEAD_DIM + 1]
        dmats.append(jnp.where(diff > 0, jnp.exp(lf * jnp.maximum(diff, 0.0)),
                               jnp.where(diff < 0, jnp.exp(lb * jnp.maximum(-diff, 0.0)), 2.0)))
    tab = dict(
        qdec_f=jnp.exp(lgf * (row + 1.0)), kdec_f=jnp.exp(lgf * (c - 1.0 - row)),
        qdec_b=jnp.exp(lgb * (c - row)), kdec_b=jnp.exp(lgb * row),
        cdec_f=jnp.exp(lgf * c), cdec_b=jnp.exp(lgb * c),
        d0=dmats[0], d1=dmats[1], row=row, diff=diff)
    r = lax.broadcasted_iota(jnp.int32, (LANES, LANES), 0) < HEAD_DIM
    cc = lax.broadcasted_iota(jnp.int32, (LANES, LANES), 1) < HEAD_DIM
    tab["bd"] = r == cc
    tab["m0"] = lax.broadcasted_iota(jnp.int32, (c, LANES), 1) < HEAD_DIM
    return tab


def _ret_specs(bsz, s):
    blk = lambda cb: pl.BlockSpec((bsz, s, LANES), lambda p, cb=cb: (0, 0, cb + p))
    lane = pl.BlockSpec((None, 1, LANES), lambda p: (p, 0, 0))
    gain = pl.BlockSpec((1, LANES), lambda p: (0, p))
    pair = pl.BlockSpec((bsz, s, LANES), lambda p: (0, 0, p))
    return blk, lane, gain, pair


def _ret_state_spec(bsz, n_chunk):
    spec = pl.BlockSpec((None, bsz, n_chunk, LANES, LANES), lambda p: (p, 0, 0, 0, 0))
    return spec, jax.ShapeDtypeStruct((4, bsz, n_chunk, LANES, LANES), F32)


def _ret_kv_states(tb, k_ref, v_ref, rb_ref, kvf_ref, n_chunk):
    c = CHUNK
    bsz = k_ref.shape[0]
    bd = tb["bd"]

    def contributions(n, carry):
        sl = pl.ds(pl.multiple_of(n * c, c), c)
        kfb = []
        for b in range(bsz):
            k32 = k_ref[b, sl, :].astype(F32)
            kfb.append(jnp.concatenate([k32 * tb["kdec_f"], k32 * tb["kdec_b"]], axis=1).astype(BF16))
        kvs = [_dot_tn(kfb[b], v_ref[b, sl, :]) for b in range(bsz)]
        for b in range(bsz):
            kvf_ref[b, n] = jnp.where(bd, kvs[b][0:LANES], 0.0)
            rb_ref[b, n] = jnp.where(bd, kvs[b][LANES:], 0.0)
        return carry

    lax.fori_loop(0, n_chunk, contributions, 0, unroll=2)

    def recur(i, rbs):
        n = n_chunk - 1 - i
        new = []
        for b in range(bsz):
            own = rb_ref[b, n]
            rb_ref[b, n] = rbs[b]
            new.append(rbs[b] * tb["cdec_b"] + own)
        return tuple(new)

    lax.fori_loop(0, n_chunk, recur, tuple(jnp.zeros((LANES, LANES), F32) for _ in range(bsz)))


def _split_rows(x, m0):
    return jnp.concatenate([jnp.where(m0, x, 0.0), jnp.where(m0, 0.0, x)], axis=0).astype(BF16)


def _ret_fwd(u3, lgf_l, lgb_l, gn_gain, rider=None):
    bsz, s, _ = u3.shape
    n_chunk = s // CHUNK
    c = CHUNK

    def body(q_ref, k_ref, v_ref, g_ref, lgf_ref, lgb_ref, gain_ref, yh_ref, rstd_ref, o_ref, rb_ref, kvf_ref):
        tb = _ret_tables(lgf_ref[...], lgb_ref[...])
        m0 = tb["m0"]
        gain = gain_ref[...]
        rows = range(bsz)
        _ret_kv_states(tb, k_ref, v_ref, rb_ref, kvf_ref, n_chunk)

        def chunk(n, rfs):
            sl = pl.ds(pl.multiple_of(n * c, c), c)
            qs = [q_ref[b, sl, :].astype(F32) * 0.125 for b in rows]
            s01 = [_dot_nt(_split_rows(qs[b], m0), k_ref[b, sl, :]) for b in rows]
            ys = []
            for b in rows:
                lhs = jnp.concatenate([s01[b][0:c] * tb["d0"], s01[b][c:] * tb["d1"],
                                       qs[b] * tb["qdec_f"], qs[b] * tb["qdec_b"]], axis=1).astype(BF16)
                rhs = jnp.concatenate([_split_rows(v_ref[b, sl, :].astype(F32), m0),
                                       rfs[b].astype(BF16), rb_ref[b, n].astype(BF16)], axis=0)
                ys.append(_dot(lhs, rhs))
            new = []
            for b in rows:
                y = ys[b]
                mu = _head_mean(y, m0)
                yc = y - mu
                rstd = lax.rsqrt(_head_mean(yc * yc, m0) + GN_EPS)
                yh = yc * rstd
                g = g_ref[b, sl, :].astype(F32)
                yh_ref[b, sl, :] = yh
                rstd_ref[b, sl, :] = rstd
                o_ref[b, sl, :] = (yh * gain * (g * _sigmoid(g))).astype(BF16)
                new.append(rfs[b] * tb["cdec_f"] + kvf_ref[b, n])
            return tuple(new)

        _loop_pairs(n_chunk, chunk, tuple(jnp.zeros((LANES, LANES), F32) for _ in rows))

    blk, lane, gain, pair = _ret_specs(bsz, s)
    state, state_shape = _ret_state_spec(bsz, n_chunk)
    return _hosted_call(
        body, "ret_fwd", (4,),
        in_specs=[blk(CB_RQ), blk(CB_RK), blk(CB_RV), blk(CB_RG), lane, lane, gain],
        out_specs=[pair, pair, pair, state, state],
        out_shape=[jax.ShapeDtypeStruct((bsz, s, RET_WIDTH), F32), jax.ShapeDtypeStruct((bsz, s, RET_WIDTH), F32),
                   jax.ShapeDtypeStruct((bsz, s, RET_WIDTH), BF16), state_shape, state_shape],
        scratch_shapes=[],
        operands=(u3, u3, u3, u3, lgf_l, lgb_l, gn_gain), rider=rider)


def _ret_bwd(u3, y_hat, y_rstd, states, d_o, lgf_l, lgb_l, gn_gain, rider=None):
    bsz, s, _ = u3.shape
    n_chunk = s // CHUNK
    c = CHUNK

    def body(q_ref, k_ref, v_ref, g_ref, yh_ref, rstd_ref, do_ref, lgf_ref, lgb_ref, gain_ref, rb_ref, kvf_ref,
             dq_ref, dk_ref, dv_ref, dg_ref, part_ref,
             rf_ref, dirf_ref, dy_ref, dk_acc, dv_acc, pa0, pa1, vec_ref):
        tb = _ret_tables(lgf_ref[...], lgb_ref[...])
        m0, bd, row = tb["m0"], tb["bd"], tb["row"]
        gain = gain_ref[...]
        wf = jnp.maximum(tb["diff"], 0.0)
        wb = jnp.maximum(-tb["diff"], 0.0)
        rows = range(bsz)
        zero_states = tuple(jnp.zeros((LANES, LANES), F32) for _ in rows)
        for ref in (pa0, pa1):
            ref[...] = jnp.zeros_like(ref)
        vec_ref[...] = jnp.zeros_like(vec_ref)

        def sweep_fwd(n, carry):
            rfs, gbs = carry
            sl = pl.ds(pl.multiple_of(n * c, c), c)
            qs, ks, vs, dys, dybs, q01, k01, dy01 = [], [], [], [], [], [], [], []
            dgain = jnp.zeros((1, LANES), F32)
            for b in rows:
                q = q_ref[b, sl, :].astype(F32) * 0.125
                k = k_ref[b, sl, :]
                yh = yh_ref[b, sl, :]
                rstd = rstd_ref[b, sl, :]
                do = do_ref[b, sl, :].astype(F32)
                g = g_ref[b, sl, :].astype(F32)
                sg = _sigmoid(g)
                sil = g * sg
                dyh = do * gain * sil
                dg_ref[b, sl, :] = (do * yh * gain * sg * (1.0 + g * (1.0 - sg))).astype(BF16)
                dgain = dgain + jnp.sum(do * yh * sil, axis=0, keepdims=True)
                dy = rstd * (dyh - _head_mean(dyh, m0) - yh * _head_mean(dyh * yh, m0))
                dyb = dy.astype(BF16)
                dy_ref[b, sl, :] = dyb
                rf_ref[b, n] = rfs[b]
                qs.append(q)
                ks.append(k)
                vs.append(v_ref[b, sl, :])
                dys.append(dy)
                dybs.append(dyb)
                q01.append(_split_rows(q, m0))
                k01.append(_split_rows(k.astype(F32), m0))
                dy01.append(_split_rows(dy, m0))
            s01 = [_dot_nt(q01[b], ks[b]) for b in rows]
            da01 = [_dot_nt(dy01[b], vs[b]) for b in rows]
            rbn = [rb_ref[b, n] for b in rows]
            states = [jnp.concatenate([rfs[b], rbn[b]], axis=0).astype(BF16) for b in rows]
            dqc = [_dot_nt(dybs[b], states[b]) for b in rows]
            gbb = [gbs[b].astype(BF16) for b in rows]
            dkb = [_dot_nt(vs[b], gbb[b]) for b in rows]
            qfb = [jnp.concatenate([qs[b] * tb["qdec_f"], qs[b] * tb["qdec_b"]], axis=1) for b in rows]
            direct = [_dot_tn(qfb[b].astype(BF16), dybs[b]) for b in rows]
            ds_cat, ds_rows, a_rows = [], [], []
            for b in rows:
                a0 = s01[b][0:c] * tb["d0"]
                a1 = s01[b][c:] * tb["d1"]
                pa0[...] += da01[b][0:c] * a0
                pa1[...] += da01[b][c:] * a1
                ds0 = da01[b][0:c] * tb["d0"]
                ds1 = da01[b][c:] * tb["d1"]
                ds_cat.append(jnp.concatenate([ds0, ds1], axis=1).astype(BF16))
                ds_rows.append(jnp.concatenate([ds0, ds1], axis=0).astype(BF16))
                a_rows.append(jnp.concatenate([a0, a1], axis=0).astype(BF16))
            kbd = [ks[b].astype(F32) * tb["kdec_b"] for b in rows]
            dq_in = [_dot(ds_cat[b], k01[b]) for b in rows]
            dk_in = [_dot_tn(ds_rows[b], q01[b]) for b in rows]
            dv_in = [_dot_tn(a_rows[b], dy01[b]) for b in rows]
            dv_gb = [_dot(kbd[b].astype(BF16), gbb[b]) for b in rows]
            new_rf, new_gb = [], []
            dlf = jnp.zeros((1, LANES), F32)
            dlb = jnp.zeros((1, LANES), F32)
            for b in rows:
                dqf, dqb = dqc[b][:, 0:LANES], dqc[b][:, LANES:]
                qf, qb = qfb[b][:, 0:LANES], qfb[b][:, LANES:]
                dq = dq_in[b] + dqf * tb["qdec_f"] + dqb * tb["qdec_b"]
                dq_ref[b, sl, :] = (dq * 0.125).astype(BF16)
                dk_acc[b, sl, :] = dk_in[b] + dkb[b] * tb["kdec_b"]
                dv_acc[b, sl, :] = dv_in[b] + dv_gb[b]
                dlf = dlf + jnp.sum((row + 1.0) * qf * dqf, axis=0, keepdims=True)
                dlb = dlb + jnp.sum((c - row) * qb * dqb + row * kbd[b] * dkb[b], axis=0, keepdims=True)
                dlb = dlb + c * tb["cdec_b"] * jnp.sum(gbs[b] * rbn[b], axis=0, keepdims=True)
                dirf_ref[b, n] = jnp.where(bd, direct[b][0:LANES], 0.0)
                new_gb.append(jnp.where(bd, direct[b][LANES:], 0.0) + tb["cdec_b"] * gbs[b])
                new_rf.append(rfs[b] * tb["cdec_f"] + kvf_ref[b, n])
            vec_ref[0:1, :] += dlf
            vec_ref[1:2, :] += dlb
            vec_ref[6:7, :] += dgain
            return tuple(new_rf), tuple(new_gb)

        _loop_pairs(n_chunk, sweep_fwd, (zero_states, zero_states))

        def sweep_bwd(i, gfs):
            n = n_chunk - 1 - i
            sl = pl.ds(pl.multiple_of(n * c, c), c)
            gfb = [gfs[b].astype(BF16) for b in rows]
            kfd = [k_ref[b, sl, :].astype(F32) * tb["kdec_f"] for b in rows]
            dkf = [_dot_nt(v_ref[b, sl, :], gfb[b]) for b in rows]
            dvf = [_dot(kfd[b].astype(BF16), gfb[b]) for b in rows]
            new = []
            dlf = jnp.zeros((1, LANES), F32)
            for b in rows:
                dk_ref[b, sl, :] = (dk_acc[b, sl, :] + dkf[b] * tb["kdec_f"]).astype(BF16)
                dv_ref[b, sl, :] = (dv_acc[b, sl, :] + dvf[b]).astype(BF16)
                dlf = dlf + jnp.sum((c - 1.0 - row) * kfd[b] * dkf[b], axis=0, keepdims=True)
                dlf = dlf + c * tb["cdec_f"] * jnp.sum(gfs[b] * rf_ref[b, n], axis=0, keepdims=True)
                new.append(dirf_ref[b, n] + tb["cdec_f"] * gfs[b])
            vec_ref[0:1, :] += dlf
            return tuple(new)

        _loop_pairs(n_chunk, sweep_bwd, zero_states)
        vec_ref[2:3, :] = jnp.sum(pa0[...] * wf, axis=0, keepdims=True)
        vec_ref[3:4, :] = jnp.sum(pa1[...] * wf, axis=0, keepdims=True)
        vec_ref[4:5, :] = jnp.sum(pa0[...] * wb, axis=0, keepdims=True)
        vec_ref[5:6, :] = jnp.sum(pa1[...] * wb, axis=0, keepdims=True)
        part_ref[...] = vec_ref[...]

    blk, lane, gain, pair = _ret_specs(bsz, s)
    out_bf = jax.ShapeDtypeStruct((bsz, s, RET_WIDTH), BF16)
    state = pltpu.VMEM((bsz, n_chunk, LANES, LANES), F32)
    saved = _ret_state_spec(bsz, n_chunk)[0]
    return _hosted_call(
        body, "ret_bwd", (4,),
        in_specs=[blk(CB_RQ), blk(CB_RK), blk(CB_RV), blk(CB_RG), pair, pair, pair, lane, lane, gain, saved, saved],
        out_specs=[pair, pair, pair, pair, pl.BlockSpec((None, 8, LANES), lambda p: (p, 0, 0))],
        out_shape=[out_bf, out_bf, out_bf, out_bf, jax.ShapeDtypeStruct((4, 8, LANES), F32)],
        scratch_shapes=[state, state,
                        pltpu.VMEM((bsz, s, LANES), BF16), pltpu.VMEM((bsz, s, LANES), F32),
                        pltpu.VMEM((bsz, s, LANES), F32),
                        pltpu.VMEM((c, c), F32), pltpu.VMEM((c, c), F32), pltpu.VMEM((8, LANES), F32)],
        operands=(u3, u3, u3, u3, y_hat, y_rstd, d_o, lgf_l, lgb_l, gn_gain, *states), rider=rider)


def _attn_window_tables(n, s):
    qi = lax.broadcasted_iota(jnp.int32, (CHUNK, 3 * CHUNK), 0)
    kj = lax.broadcasted_iota(jnp.int32, (CHUNK, 3 * CHUNK), 1)
    dist = jnp.abs(kj - CHUNK - qi)
    kpos = n * CHUNK - CHUNK + kj
    valid = (dist <= CHUNK) & (kpos >= 0) & (kpos < s)
    return dist.astype(F32), valid


def _dup_kv_head(x, g):
    lane = lax.broadcasted_iota(jnp.int32, x.shape, 1)
    keep = (lane < HEAD_DIM) == (g == 0)
    xf = x.astype(F32)
    return jnp.where(keep, xf, pltpu.roll(xf, HEAD_DIM, 1))


def _attn_specs(s):
    q = pl.BlockSpec((None, s, 2 * LANES), lambda b, g: (b, 0, CB_AQ // 2 + g))
    k = pl.BlockSpec((None, s, LANES), lambda b, g: (b, 0, CB_AK))
    v = pl.BlockSpec((None, s, LANES), lambda b, g: (b, 0, CB_AV))
    grp = pl.BlockSpec((None, s, 2 * LANES), lambda b, g: (b, 0, g))
    smem = pl.BlockSpec(memory_space=pltpu.SMEM)
    return q, k, v, grp, smem


def _fill_padded(dst_ref, val, s):
    dst_ref[0:CHUNK, :] = jnp.zeros((CHUNK, LANES), dst_ref.dtype)
    dst_ref[CHUNK:CHUNK + s, :] = val.astype(dst_ref.dtype)
    dst_ref[CHUNK + s:2 * CHUNK + s, :] = jnp.zeros((CHUNK, LANES), dst_ref.dtype)


def _attn_probs(sc, slope, snk, dist, valid):
    sc = jnp.where(valid, sc - slope * dist, NEG_INF)
    m = jnp.maximum(jnp.max(sc, axis=1, keepdims=True), snk)
    e = jnp.exp(sc - m)
    es = jnp.exp(snk - m)
    inv = 1.0 / (jnp.sum(e, axis=1, keepdims=True) + es)
    return e * inv, es * inv


def _stack_heads(x2, m0):
    parts = []
    for pr in range(2):
        xp = x2[:, pr * LANES:(pr + 1) * LANES]
        parts += [jnp.where(m0, xp, 0.0), jnp.where(m0, 0.0, xp)]
    return jnp.concatenate(parts, axis=0).astype(BF16)


def _unstack_pair(x_all, pr, m0):
    return jnp.where(m0, x_all[(2 * pr) * CHUNK:(2 * pr + 1) * CHUNK], x_all[(2 * pr + 1) * CHUNK:(2 * pr + 2) * CHUNK])


def _attn_saved_specs(bsz, n_blk):
    specs = [pl.BlockSpec((None, None, n_blk, 4 * CHUNK, w), lambda b, g: (b, g, 0, 0, 0)) for w in (3 * CHUNK, 1)]
    shapes = [jax.ShapeDtypeStruct((bsz, 2, n_blk, 4 * CHUNK, 3 * CHUNK), BF16),
              jax.ShapeDtypeStruct((bsz, 2, n_blk, 4 * CHUNK, 1), F32)]
    return specs, shapes


def _attn_fwd(u3, slopes, sink, rider=None):
    bsz, s, _ = u3.shape
    n_blk = s // CHUNK

    def body(slope_ref, sink_ref, q_ref, k_ref, v_ref, o_ref, p_ref, ps_ref, kp_ref, vp_ref):
        g = pl.program_id(1)
        _fill_padded(kp_ref, _dup_kv_head(k_ref[...], g), s)
        _fill_padded(vp_ref, _dup_kv_head(v_ref[...], g), s)
        m0 = lax.broadcasted_iota(jnp.int32, (CHUNK, LANES), 1) < HEAD_DIM

        def blk(n, carry):
            r0 = pl.multiple_of(n * CHUNK, CHUNK)
            kw = kp_ref[pl.ds(r0, 3 * CHUNK), :]
            vw = vp_ref[pl.ds(r0, 3 * CHUNK), :]
            dist, valid = _attn_window_tables(n, s)
            q_all = _stack_heads(q_ref[pl.ds(r0, CHUNK), :].astype(F32) * 0.125, m0)
            sc_all = _dot_nt(q_all, kw)
            probs, sinks = [], []
            for i in range(4):
                p, ps = _attn_probs(sc_all[i * CHUNK:(i + 1) * CHUNK], slope_ref[g * 4 + i], sink_ref[g * 4 + i],
                                    dist, valid)
                probs.append(p.astype(BF16))
                sinks.append(ps)
            p_all = jnp.concatenate(probs, axis=0)
            p_ref[n] = p_all
            ps_ref[n] = jnp.concatenate(sinks, axis=0)
            out_all = _dot(p_all, vw)
            for pr in range(2):
                o_ref[pl.ds(r0, CHUNK), pr * LANES:(pr + 1) * LANES] = _unstack_pair(out_all, pr, m0).astype(BF16)
            return carry

        lax.fori_loop(0, n_blk, blk, 0, unroll=4)

    q, k, v, grp, smem = _attn_specs(s)
    saved_specs, saved_shapes = _attn_saved_specs(bsz, n_blk)
    return _hosted_call(
        body, "attn_fwd", (bsz, 2),
        in_specs=[smem, smem, q, k, v],
        out_specs=[grp] + saved_specs,
        out_shape=[jax.ShapeDtypeStruct((bsz, s, ATTN_WIDTH), BF16)] + saved_shapes,
        scratch_shapes=[pltpu.VMEM((s + 2 * CHUNK, LANES), BF16), pltpu.VMEM((s + 2 * CHUNK, LANES), BF16)],
        operands=(slopes, sink, u3, u3, u3), rider=rider)


def _attn_bwd(u3, d_o, probs, sink_probs, rider=None):
    bsz, s, _ = u3.shape
    n_blk = s // CHUNK

    def body(q_ref, k_ref, v_ref, do_ref, p_ref, ps_ref, dq_ref, dkv_ref, ds_ref,
             kp_ref, vp_ref, dk_acc, dv_acc):
        g = pl.program_id(1)
        _fill_padded(kp_ref, _dup_kv_head(k_ref[...], g), s)
        _fill_padded(vp_ref, _dup_kv_head(v_ref[...], g), s)
        dk_acc[...] = jnp.zeros_like(dk_acc)
        dv_acc[...] = jnp.zeros_like(dv_acc)
        m0 = lax.broadcasted_iota(jnp.int32, (CHUNK, LANES), 1) < HEAD_DIM

        def blk(n, dsink):
            r0 = pl.multiple_of(n * CHUNK, CHUNK)
            win = pl.ds(r0, 3 * CHUNK)
            kw = kp_ref[win, :]
            vw = vp_ref[win, :]
            q_all = _stack_heads(q_ref[pl.ds(r0, CHUNK), :].astype(F32) * 0.125, m0)
            do_all = _stack_heads(do_ref[pl.ds(r0, CHUNK), :].astype(F32), m0)
            p_all = p_ref[n]
            ps_all = ps_ref[n]
            dp_all = _dot_nt(do_all, vw)
            new_dsink, dscs = [], []
            for i in range(4):
                rows = slice(i * CHUNK, (i + 1) * CHUNK)
                p = p_all[rows].astype(F32)
                dp = dp_all[rows]
                delta = jnp.sum(p * dp, axis=1, keepdims=True)
                dscs.append((p * (dp - delta)).astype(BF16))
                dsh = jnp.sum(ps_all[rows] * delta, axis=0, keepdims=True)
                new_dsink.append(dsink[i] - jnp.broadcast_to(dsh, (1, LANES)))
            dsc_all = jnp.concatenate(dscs, axis=0)
            dq_all = _dot(dsc_all, kw)
            dk_acc[win, :] += _dot_tn(dsc_all, q_all)
            dv_acc[win, :] += _dot_tn(p_all, do_all)
            for pr in range(2):
                dq_ref[pl.ds(r0, CHUNK), pr * LANES:(pr + 1) * LANES] = (
                    _unstack_pair(dq_all, pr, m0) * 0.125).astype(BF16)
            return tuple(new_dsink)

        dsink = _loop_pairs(n_blk, blk, tuple(jnp.zeros((1, LANES), F32) for _ in range(4)), per_trip=4)
        dk = dk_acc[CHUNK:CHUNK + s, :]
        dv = dv_acc[CHUNK:CHUNK + s, :]
        lane = lax.broadcasted_iota(jnp.int32, (s, LANES), 1)
        fold = lambda a: a + pltpu.roll(a, HEAD_DIM, 1)
        dkv_ref[...] = jnp.where(lane < HEAD_DIM, fold(dk), fold(dv)).astype(BF16)
        ds_ref[...] = jnp.zeros_like(ds_ref)
        for i in range(4):
            ds_ref[i:i + 1, :] = dsink[i]

    q, k, v, grp, _ = _attn_specs(s)
    return _hosted_call(
        body, "attn_bwd", (bsz, 2),
        in_specs=[q, k, v, grp] + _attn_saved_specs(bsz, n_blk)[0],
        out_specs=[grp, pl.BlockSpec((None, s, LANES), lambda b, g: (b, 0, g)),
                   pl.BlockSpec((None, None, 8, LANES), lambda b, g: (b, g, 0, 0))],
        out_shape=[jax.ShapeDtypeStruct((bsz, s, ATTN_WIDTH), BF16), jax.ShapeDtypeStruct((bsz, s, 2 * LANES), BF16),
                   jax.ShapeDtypeStruct((bsz, 2, 8, LANES), F32)],
        scratch_shapes=[pltpu.VMEM((s + 2 * CHUNK, LANES), BF16), pltpu.VMEM((s + 2 * CHUNK, LANES), BF16),
                        pltpu.VMEM((s + 2 * CHUNK, LANES), F32), pltpu.VMEM((s + 2 * CHUNK, LANES), F32)],
        operands=(u3, u3, u3, d_o, probs, sink_probs), rider=rider)


def _ffn_bwd(dz2, gs, us, pg, ple, zh1, r1, g1, wg4, wu4, wd4, wpg, w_out):
    t = dz2.shape[0]
    tm = 256
    wg_t, wu_t, wd_all = (w.reshape(FFN, D_MODEL) for w in (wg4, wu4, wd4))

    def body(dz_ref, gs_ref, us_ref, pg_ref, ple_ref, zh_ref, r_ref, g1_ref,
             wg_hbm, wu_hbm, wd_hbm, wpg_hbm, wo_hbm,
             dgs_ref, dus_ref, dsp_ref, dple_ref, dz1_ref, dyr_ref, dya_ref, dg1_ref, db1_ref,
             wg, wu, wd, wpg, wo, wsem):
        step = pl.program_id(0)
        loads = _resident_quarters(wd_hbm, wd) + _resident_quarters(wg_hbm, wg) + _resident_quarters(wu_hbm, wu)
        _load_resident(step, loads + [(wpg_hbm, wpg), (wo_hbm, wo)], wsem)

        @pl.when(step == 0)
        def _():
            dg1_ref[...] = jnp.zeros_like(dg1_ref)
            db1_ref[...] = jnp.zeros_like(db1_ref)

        dz = dz_ref[...]
        dzb = dz.astype(BF16)
        dh = ALPHA * dz
        pending = []
        chunks = [slice(n * FFN_CHUNK, (n + 1) * FFN_CHUNK) for n in range(N_FFN_CHUNK)]
        for n in range(N_FFN_CHUNK + 1):
            if n < N_FFN_CHUNK:
                da = _dot_nt(dzb, wd[chunks[n], :])
                gj = gs_ref[:, chunks[n]].astype(F32)
                uj = us_ref[:, chunks[n]].astype(F32)
                sg = _sigmoid(gj)
                dgj = (da * uj * sg * (1.0 + gj * (1.0 - sg))).astype(BF16)
                duj = (da * gj * sg).astype(BF16)
                dgs_ref[:, chunks[n]] = dgj
                dus_ref[:, chunks[n]] = duj
                pending.append((dgj, duj))
            if n > 0:
                dgp, dup = pending[n - 1]
                dh = dh + _dot(dgp, wg[chunks[n - 1], :]) + _dot(dup, wu[chunks[n - 1], :])
        pgv = pg_ref[...].astype(F32)
        plev = ple_ref[...].astype(F32)
        dple_ref[...] = (dz * pgv).astype(BF16)
        dsp = (dz * plev * pgv * (1.0 - pgv)).astype(BF16)
        dsp_ref[...] = dsp
        dh = dh + _dot_nt(dsp, wpg[...])
        zh = zh_ref[...]
        dg1_ref[...] += jnp.sum(dh * zh, axis=0, keepdims=True)
        db1_ref[...] += jnp.sum(dh, axis=0, keepdims=True)
        dzh = dh * g1_ref[...]
        m1 = jnp.mean(dzh, axis=1, keepdims=True)
        m2 = jnp.mean(dzh * zh, axis=1, keepdims=True)
        dz1 = r_ref[...] * (dzh - m1 - zh * m2)
        dz1_ref[...] = dz1
        dyc = _dot_nt(dz1.astype(BF16), wo[...])
        dyr_ref[...] = dyc[:, 0:RET_WIDTH].astype(BF16)
        dya_ref[...] = dyc[:, RET_WIDTH:].astype(BF16)

    row = lambda w: pl.BlockSpec((tm, w), lambda i: (i, 0))
    const = lambda s: pl.BlockSpec(s, lambda i: (0, 0))
    hbm = pl.BlockSpec(memory_space=pl.ANY)
    hid_shape = jax.ShapeDtypeStruct((t, FFN), BF16)
    return pl.pallas_call(
        body, name="ffn_bwd", grid=(t // tm,),
        in_specs=[row(D_MODEL), row(FFN), row(FFN), row(D_MODEL), row(D_MODEL), row(D_MODEL), row(1),
                  const((1, D_MODEL)), hbm, hbm, hbm, hbm, hbm],
        out_specs=[row(FFN), row(FFN), row(D_MODEL), row(D_MODEL), row(D_MODEL), row(RET_WIDTH), row(ATTN_WIDTH),
                   const((1, D_MODEL)), const((1, D_MODEL))],
        out_shape=[hid_shape, hid_shape, jax.ShapeDtypeStruct((t, D_MODEL), BF16),
                   jax.ShapeDtypeStruct((t, D_MODEL), BF16), jax.ShapeDtypeStruct((t, D_MODEL), F32),
                   jax.ShapeDtypeStruct((t, RET_WIDTH), BF16), jax.ShapeDtypeStruct((t, ATTN_WIDTH), BF16),
                   jax.ShapeDtypeStruct((1, D_MODEL), F32), jax.ShapeDtypeStruct((1, D_MODEL), F32)],
        scratch_shapes=[pltpu.VMEM((FFN, D_MODEL), BF16), pltpu.VMEM((FFN, D_MODEL), BF16),
                        pltpu.VMEM((FFN, D_MODEL), BF16),
                        pltpu.VMEM(wpg.shape, BF16), pltpu.VMEM(w_out.shape, BF16),
                        pltpu.SemaphoreType.DMA((3 * N_SHARD + 2,))],
        compiler_params=_params("arbitrary", vmem=VMEM_LIMIT),
    )(dz2, gs, us, pg, ple, zh1, r1, g1, wg_t, wu_t, wd_all, wpg, w_out)


def _wgrad_misc(y_ret, y_att, dz1, hb, dsp, p2d, dple, rider=None):
    t = dz1.shape[0]
    tk = min(t, 512)

    def body(yr_ref, ya_ref, dz_ref, hb_ref, dsp_ref, p_ref, dple_ref, wo_ref, wpg_ref, wpe_ref):
        @pl.when(pl.program_id(0) == 0)
        def _():
            wo_ref[...] = jnp.zeros_like(wo_ref)
            wpg_ref[...] = jnp.zeros_like(wpg_ref)
            wpe_ref[...] = jnp.zeros_like(wpe_ref)

        dzb = dz_ref[...].astype(BF16)
        wo_ref[0:RET_WIDTH, :] += _dot_tn(yr_ref[...], dzb)
        wo_ref[RET_WIDTH:, :] += _dot_tn(ya_ref[...], dzb)
        wpg_ref[...] += _dot_tn(hb_ref[...], dsp_ref[...])
        wpe_ref[...] += _dot_tn(p_ref[...].astype(BF16), dple_ref[...])

    row = lambda w: pl.BlockSpec((tk, w), lambda k: (k, 0))
    const = lambda s: pl.BlockSpec(s, lambda k: (0, 0))
    return _hosted_call(
        body, "wgrad_misc", (t // tk,),
        in_specs=[row(RET_WIDTH), row(ATTN_WIDTH), row(D_MODEL), row(D_MODEL), row(D_MODEL), row(PLE_DIM),
                  row(D_MODEL)],
        out_specs=[const((D_MODEL, D_MODEL)), const((D_MODEL, D_MODEL)), const((PLE_DIM, D_MODEL))],
        out_shape=[jax.ShapeDtypeStruct((D_MODEL, D_MODEL), F32), jax.ShapeDtypeStruct((D_MODEL, D_MODEL), F32),
                   jax.ShapeDtypeStruct((PLE_DIM, D_MODEL), F32)],
        scratch_shapes=[], operands=(y_ret, y_att, dz1, hb, dsp, p2d, dple), rider=rider, semantics=["arbitrary"])


def _wgrad_ffn(acts, dgs, dus, hb, dz2b):
    t = dz2b.shape[0]
    tk = min(t, 512)
    nk = t // tk

    def body(act_ref, dg_ref, du_ref, hb_ref, dz_ref, og_ref, ou_ref, od_ref):
        @pl.when(pl.program_id(1) == 0)
        def _():
            og_ref[...] = jnp.zeros_like(og_ref)
            ou_ref[...] = jnp.zeros_like(ou_ref)
            od_ref[...] = jnp.zeros_like(od_ref)

        hbv = hb_ref[...]
        og_ref[...] += _dot_tn(dg_ref[...], hbv)
        ou_ref[...] += _dot_tn(du_ref[...], hbv)
        od_ref[...] += _dot_tn(act_ref[...], dz_ref[...])

    half = FFN // 2
    a_spec = pl.BlockSpec((tk, half), lambda j, k: (k, j))
    b_spec = pl.BlockSpec((tk, D_MODEL), lambda j, k: (k, 0))
    o_spec = pl.BlockSpec((half, D_MODEL), lambda j, k: (j, 0))
    o_shape = jax.ShapeDtypeStruct((FFN, D_MODEL), F32)
    outs = pl.pallas_call(
        body, name="wgrad_ffn", grid=(2, nk),
        in_specs=[a_spec, a_spec, a_spec, b_spec, b_spec],
        out_specs=[o_spec] * 3, out_shape=[o_shape] * 3,
        compiler_params=_params("parallel", "arbitrary", vmem=VMEM_LIMIT),
    )(acts, dgs, dus, hb, dz2b)
    return [o.reshape(N_SHARD, FFN_SHARD, D_MODEL) for o in outs]


KV_ORDER = (0, 128, 64, 192)


def _wgrad_in(pieces, x2d, rider=None):
    t = x2d.shape[0]
    tk = min(t, 512)
    nk = t // tk
    kv0 = CB_AK * LANES

    def body(p0, p1, p2, p3, p4, pkv, x_ref, o_ref):
        @pl.when(pl.program_id(0) == 0)
        def _():
            o_ref[...] = jnp.zeros_like(o_ref)

        xb = x_ref[...].astype(BF16)
        for i, ref in enumerate((p0, p1, p2, p3, p4)):
            o_ref[i * 512:(i + 1) * 512, :] += _dot_tn(ref[...], xb)
        dkv = _dot_tn(pkv[...], xb)
        for i, o in enumerate(KV_ORDER):
            o_ref[kv0 + o:kv0 + o + HEAD_DIM, :] += dkv[i * HEAD_DIM:(i + 1) * HEAD_DIM]

    row = lambda w: pl.BlockSpec((tk, w), lambda k: (k, 0))
    return _hosted_call(
        body, "wgrad_in", (nk,),
        in_specs=[row(512)] * 5 + [row(256), row(D_MODEL)],
        out_specs=[pl.BlockSpec((IN_WIDTH, D_MODEL), lambda k: (0, 0))],
        out_shape=[jax.ShapeDtypeStruct((IN_WIDTH, D_MODEL), F32)],
        scratch_shapes=[], operands=(*pieces, x2d), rider=rider, semantics=["arbitrary"])


def _inproj_bwd(dz1, pieces, w_main, w_kv, rider=None):
    t = dz1.shape[0]
    tm = 512

    def body(dz_ref, p0, p1, p2, p3, p4, pkv, wm_ref, wkv_ref, o_ref):
        acc = ALPHA * dz_ref[...]
        for i, ref in enumerate((p0, p1, p2, p3, p4)):
            acc = acc + _dot(ref[...], wm_ref[i * 512:(i + 1) * 512, :])
        o_ref[...] = acc + _dot(pkv[...], wkv_ref[...])

    row = lambda w: pl.BlockSpec((tm, w), lambda i: (i, 0))
    const = lambda s: pl.BlockSpec(s, lambda i: (0, 0))
    return _hosted_call(
        body, "inproj_bwd", (t // tm,),
        in_specs=[row(D_MODEL)] + [row(512)] * 5 + [row(256), const(w_main.shape), const(w_kv.shape)],
        out_specs=[row(D_MODEL)],
        out_shape=[jax.ShapeDtypeStruct((t, D_MODEL), F32)],
        scratch_shapes=[], operands=(dz1, *pieces, w_main, w_kv), rider=rider)


def _coords():
    return lax.axis_index("x"), lax.axis_index("y"), lax.axis_index("c")


def _chip_of(x, y, rel):
    return (1 - x if rel & 2 else x), (1 - y if rel & 1 else y)


def _all_gather_weights(shards):
    first = _gather_near_rider(shards)
    later = [f(first.out_shapes, chained=True) for f in (_gather_relay_rider, _gather_pass_rider)]
    return _run_riders("gather_weights", shards, first.out_shapes, [first] + later)


def _run_riders(name, ins, out_shapes, riders):
    n_in, n_out = len(ins), len(out_shapes)

    def body(*refs):
        in_refs, out_refs = refs[:n_in], refs[n_in:n_in + n_out]
        k = n_in + n_out
        for r in riders:
            sems = refs[k:k + len(r.sems)]
            k += len(r.sems)
            r.start(in_refs, out_refs, sems)
            r.finish(in_refs, out_refs, sems)

    hbm = pl.BlockSpec(memory_space=pl.ANY)
    return pl.pallas_call(
        body, name=name, in_specs=[hbm] * n_in, out_specs=[hbm] * n_out, out_shape=list(out_shapes),
        scratch_shapes=[s for r in riders for s in r.sems],
    )(*ins)


def _gather_half(outs, w, chip, cc):
    h = outs[w].shape[1] // 2
    return outs[w].at[chip, pl.ds(cc * h, h), :]


NEAR = (1, 2)


def _gather_near_rider(shards):
    nw = len(shards)

    def copies(ins, outs, sems, arrivals):
        send, recv, lsend, lrecv = sems
        x, y, c = _coords()
        me = 2 * x + y
        own = [pltpu.make_async_remote_copy(
            src_ref=ins[w], dst_ref=outs[w].at[me], send_sem=lsend.at[w], recv_sem=lrecv.at[w],
            device_id=(x, y, 1 - c), device_id_type=MESH) for w in range(nw)]
        out, arrive = [], []
        for rel in NEAR:
            kx, ky = _chip_of(x, y, rel)
            for w in range(nw):
                h = shards[w].shape[0] // 2
                sem = dict(send_sem=send.at[w * 2 + rel - 1], recv_sem=recv.at[w * 2 + rel - 1],
                           device_id=(kx, ky, c), device_id_type=MESH)
                out.append(pltpu.make_async_remote_copy(
                    src_ref=ins[w].at[pl.ds(c * h, h), :], dst_ref=_gather_half(outs, w, me, c), **sem))
                if arrivals:
                    theirs = _gather_half(outs, w, 2 * kx + ky, c)
                    arrive.append(pltpu.make_async_remote_copy(src_ref=theirs, dst_ref=theirs, **sem))
        return own, out, arrive

    def start(ins, outs, sems):
        own, out, _ = copies(ins, outs, sems, arrivals=False)
        for cp in own + out:
            cp.start()

    def finish(ins, outs, sems):
        own, out, arrive = copies(ins, outs, sems, arrivals=True)
        for cp in arrive:
            cp.wait_recv()
        for cp in out:
            cp.wait_send()
        for cp in own:
            cp.wait()

    dma = pltpu.SemaphoreType.DMA
    return _Rider(shards, [jax.ShapeDtypeStruct((N_SHARD,) + s.shape, s.dtype) for s in shards],
                  [dma((2 * nw,)), dma((2 * nw,)), dma((nw,)), dma((nw,))], start, finish)


def _gather_relay_rider(gathered, chained=False):
    nw = len(gathered)

    def quarter(outs, w, chip, c, p):
        q = outs[w].shape[1] // 4
        return outs[w].at[chip, pl.ds(c * 2 * q + p * q, q), :]

    def copies(outs, sems):
        send, recv = sems
        x, y, c = _coords()
        (yx, yy), (xx, xy), (dx, dy) = (_chip_of(x, y, rel) for rel in (1, 2, 3))
        out, arrive = [], []
        for w in range(nw):
            for p, (src_chip, dst) in enumerate(((2 * xx + xy, (yx, yy)), (2 * yx + yy, (xx, xy)))):
                rows = quarter(outs, w, src_chip, c, p)
                sem = dict(send_sem=send.at[w * 2 + p], recv_sem=recv.at[w * 2 + p], device_id_type=MESH)
                out.append(pltpu.make_async_remote_copy(src_ref=rows, dst_ref=rows, device_id=(*dst, c), **sem))
                mine = quarter(outs, w, 2 * dx + dy, c, p)
                arrive.append(pltpu.make_async_remote_copy(src_ref=mine, dst_ref=mine, device_id=(*dst, c), **sem))
        return out, arrive

    def start(ins, outs, sems):
        for cp in copies(outs, sems)[0]:
            cp.start()

    def finish(ins, outs, sems):
        out, arrive = copies(outs, sems)
        for cp in arrive:
            cp.wait_recv()
        for cp in out:
            cp.wait_send()

    dma = pltpu.SemaphoreType.DMA
    shapes = [jax.ShapeDtypeStruct(g.shape, g.dtype) for g in gathered]
    if chained:
        return _Rider([], [], [dma((2 * nw,)), dma((2 * nw,))], start, finish)
    return _Rider(gathered, shapes, [dma((2 * nw,)), dma((2 * nw,))], start, finish,
                  aliases={w: w for w in range(nw)})


def _gather_pass_rider(gathered, chained=False):
    nw = len(gathered)

    def copies(outs, sems, cc):
        send, recv = sems
        x, y, c = _coords()
        res = []
        for rel in (1, 2, 3):
            kx, ky = _chip_of(x, y, rel)
            for w in range(nw):
                rows = _gather_half(outs, w, 2 * kx + ky, cc)
                res.append(pltpu.make_async_remote_copy(
                    src_ref=rows, dst_ref=rows, send_sem=send.at[w * 3 + rel - 1], recv_sem=recv.at[w * 3 + rel - 1],
                    device_id=(x, y, 1 - c), device_id_type=MESH))
        return res

    def start(ins, outs, sems):
        for cp in copies(outs, sems, lax.axis_index("c")):
            cp.start()

    def finish(ins, outs, sems):
        c = lax.axis_index("c")
        for cp in copies(outs, sems, 1 - c):
            cp.wait_recv()
        for cp in copies(outs, sems, c):
            cp.wait_send()

    dma = pltpu.SemaphoreType.DMA
    shapes = [jax.ShapeDtypeStruct(g.shape, g.dtype) for g in gathered]
    if chained:
        return _Rider([], [], [dma((3 * nw,)), dma((3 * nw,))], start, finish)
    return _Rider(gathered, shapes, [dma((3 * nw,)), dma((3 * nw,))], start, finish,
                  aliases={w: w for w in range(nw)})


def _exchange_halves_rider(parts):
    nw = len(parts)

    def copies(ins, outs, sems):
        send, recv = sems
        x, y, c = _coords()
        res = []
        for w in range(nw):
            h = parts[w].shape[1] // 2
            res.append(pltpu.make_async_remote_copy(
                src_ref=ins[w].at[:, pl.ds((1 - c) * h, h), :], dst_ref=outs[w],
                send_sem=send.at[w], recv_sem=recv.at[w], device_id=(x, y, 1 - c), device_id_type=MESH))
        return res

    def start(ins, outs, sems):
        for cp in copies(ins, outs, sems):
            cp.start()

    def finish(ins, outs, sems):
        for cp in copies(ins, outs, sems):
            cp.wait()

    dma = pltpu.SemaphoreType.DMA
    return _Rider(parts, [jax.ShapeDtypeStruct((N_SHARD, p.shape[1] // 2, p.shape[2]), p.dtype) for p in parts],
                  [dma((nw,)), dma((nw,))], start, finish)


def _add_halves(parts, theirs, pos):
    nw = len(parts)
    split = 2

    def body(pos_ref, *refs):
        ins, oth = refs[:nw], refs[nw:2 * nw]
        o32, o16 = refs[2 * nw:3 * nw], refs[3 * nw:]
        sums = [ins[w][...] + oth[w][...].astype(F32) for w in range(nw)]
        for w in range(nw):
            o16[w][...] = sums[w].astype(BF16)

        @pl.when(pl.program_id(1) == pos_ref[0])
        def _():
            for w in range(nw):
                o32[w][...] = sums[w]

    in_specs, oth_specs, o32_specs, shapes32, shapes16 = [], [], [], [], []
    for p in parts:
        hb = p.shape[1] // 2 // split
        blk = (None, hb, p.shape[2])
        in_specs.append(pl.BlockSpec(blk, lambda i, j, pos_ref: (j, pos_ref[1] * split + i, 0)))
        oth_specs.append(pl.BlockSpec(blk, lambda i, j, pos_ref: (j, i, 0)))
        o32_specs.append(pl.BlockSpec((hb, p.shape[2]), lambda i, j, pos_ref: (i, 0)))
        shapes32.append(jax.ShapeDtypeStruct((p.shape[1] // 2, p.shape[2]), F32))
        shapes16.append(jax.ShapeDtypeStruct((N_SHARD, p.shape[1] // 2, p.shape[2]), BF16))
    return pl.pallas_call(
        body, name="add_halves",
        grid_spec=pltpu.PrefetchScalarGridSpec(
            num_scalar_prefetch=1, grid=(split, N_SHARD),
            in_specs=in_specs + oth_specs, out_specs=o32_specs + oth_specs),
        out_shape=shapes32 + shapes16,
        compiler_params=_params("parallel", "arbitrary", vmem=VMEM_LIMIT),
    )(pos, *parts, *theirs)


def _exchange_chips_rider(sums16):
    nw = len(sums16)

    def copies(ins, outs, sems):
        send, recv = sems
        x, y, c = _coords()
        res = []
        for rel in (1, 2, 3):
            kx, ky = _chip_of(x, y, rel)
            for w in range(nw):
                res.append(pltpu.make_async_remote_copy(
                    src_ref=ins[w].at[2 * kx + ky], dst_ref=outs[w].at[rel - 1],
                    send_sem=send.at[w * 3 + rel - 1], recv_sem=recv.at[w * 3 + rel - 1],
                    device_id=(kx, ky, c), device_id_type=MESH))
        return res

    def start(ins, outs, sems):
        for cp in copies(ins, outs, sems):
            cp.start()

    def finish(ins, outs, sems):
        for cp in copies(ins, outs, sems):
            cp.wait()

    dma = pltpu.SemaphoreType.DMA
    return _Rider(sums16, [jax.ShapeDtypeStruct((3,) + s.shape[1:], BF16) for s in sums16],
                  [dma((3 * nw,)), dma((3 * nw,))], start, finish)


def _add_chips(sums32, theirs, pos):
    nw = len(sums32)
    split = 2

    def body(pos_ref, *refs):
        ins, oth, outs = refs[:nw], refs[nw:2 * nw], refs[2 * nw:]
        for w in range(nw):
            acc = ins[w][...]
            for r in range(3):
                acc = acc + oth[w][r].astype(F32)
            outs[w][...] = acc

    in_specs, oth_specs, out_specs, shapes = [], [], [], []
    for s in sums32:
        hb = s.shape[0] // split
        in_specs.append(pl.BlockSpec((hb, s.shape[1]), lambda i, pos_ref: (i, 0)))
        oth_specs.append(pl.BlockSpec((3, hb, s.shape[1]), lambda i, pos_ref: (0, i, 0)))
        out_specs.append(pl.BlockSpec((hb, s.shape[1]), lambda i, pos_ref: (pos_ref[1] * split + i, 0)))
        shapes.append(jax.ShapeDtypeStruct((2 * s.shape[0], s.shape[1]), F32))
    return pl.pallas_call(
        body, name="add_chips",
        grid_spec=pltpu.PrefetchScalarGridSpec(
            num_scalar_prefetch=1, grid=(split,), in_specs=in_specs + oth_specs, out_specs=out_specs),
        out_shape=shapes,
        compiler_params=_params("parallel", vmem=VMEM_LIMIT),
    )(pos, *sums32, *theirs)


def _join_halves(shards):
    nw = len(shards)

    def body(*refs):
        outs = refs[nw:2 * nw]
        send, recv = refs[2 * nw:]
        x, y, c = _coords()

        def copy(w, cc):
            h = shards[w].shape[0] // 2
            rows = outs[w].at[pl.ds(cc * h, h), :]
            return pltpu.make_async_remote_copy(
                src_ref=rows, dst_ref=rows, send_sem=send.at[w], recv_sem=recv.at[w],
                device_id=(x, y, 1 - c), device_id_type=MESH)

        for w in range(nw):
            copy(w, c).start()
        for w in range(nw):
            copy(w, 1 - c).wait_recv()
            copy(w, c).wait_send()

    hbm = pl.BlockSpec(memory_space=pl.ANY)
    return pl.pallas_call(
        body, name="join_halves",
        in_specs=[hbm] * nw, out_specs=[hbm] * nw,
        out_shape=[jax.ShapeDtypeStruct(s.shape, F32) for s in shards],
        input_output_aliases={w: w for w in range(nw)},
        scratch_shapes=[pltpu.SemaphoreType.DMA((nw,)), pltpu.SemaphoreType.DMA((nw,))],
    )(*shards)


def _adamw_math(w, g, m, v):
    m = ADAM_B1 * m + (1.0 - ADAM_B1) * g
    v = ADAM_B2 * v + (1.0 - ADAM_B2) * (g * g)
    m_hat = m / (1.0 - ADAM_B1 ** ADAM_STEP)
    v_hat = v / (1.0 - ADAM_B2 ** ADAM_STEP)
    delta = -ADAM_LR * (m_hat / (jnp.sqrt(v_hat) + ADAM_EPS) + ADAM_WD * w)
    return delta, m, v


def _adamw(ws, gs, ms, vs):
    nw = len(ws)
    split = 8

    def body(*refs):
        w_r, g_r, m_r, v_r = (refs[i * nw:(i + 1) * nw] for i in range(4))
        g_o, d_o, m_o, v_o = (refs[(4 + i) * nw:(5 + i) * nw] for i in range(4))
        for k in range(nw):
            g = g_r[k][...]
            d, m, v = _adamw_math(w_r[k][...], g, m_r[k][...], v_r[k][...])
            g_o[k][...] = g
            d_o[k][...] = d
            m_o[k][...] = m
            v_o[k][...] = v

    specs = [pl.BlockSpec((w.shape[0] // split, w.shape[1]), lambda i: (i, 0)) for w in ws]
    shapes = [jax.ShapeDtypeStruct(w.shape, F32) for w in ws]
    outs = pl.pallas_call(
        body, name="adamw", grid=(split,),
        in_specs=specs * 4, out_specs=specs * 4, out_shape=shapes * 4,
        compiler_params=_params("parallel", vmem=VMEM_LIMIT),
    )(*ws, *gs, *ms, *vs)
    return outs[:nw], outs[nw:2 * nw], outs[2 * nw:3 * nw], outs[3 * nw:]


SMALL_ROWS = 8
SMALL_COLS = D_MODEL
LOSS_COL = RET_WIDTH + 24


def _small_allreduce_adamw(part, w, m, v, rider=None):
    def body(part_ref, w_ref, m_ref, v_ref, g_out, d_out, m_out, v_out, all_ref, send, recv):
        x, y, c = _coords()
        me = 4 * x + 2 * y + c
        all_ref[me] = part_ref[...]
        copies = []
        for rel in range(1, 8):
            px = 1 - x if rel & 4 else x
            py = 1 - y if rel & 2 else y
            pc = 1 - c if rel & 1 else c
            copies.append(pltpu.make_async_remote_copy(
                src_ref=part_ref, dst_ref=all_ref.at[me],
                send_sem=send.at[rel - 1], recv_sem=recv.at[rel - 1], device_id=(px, py, pc), device_id_type=MESH))
        for cp in copies:
            cp.start()
        for cp in copies:
            cp.wait()
        g = all_ref[0]
        for k in range(1, 8):
            g = g + all_ref[k]
        d, mn, vn = _adamw_math(w_ref[...], g, m_ref[...], v_ref[...])
        g_out[...] = g
        d_out[...] = d
        m_out[...] = mn
        v_out[...] = vn

    vm = pl.BlockSpec(memory_space=pltpu.VMEM)
    shape = jax.ShapeDtypeStruct((SMALL_ROWS, SMALL_COLS), F32)
    return _hosted_call(
        body, "small_allreduce_adamw", (1,),
        in_specs=[vm] * 4, out_specs=[vm] * 4, out_shape=[shape] * 4,
        scratch_shapes=[pltpu.VMEM((8, SMALL_ROWS, SMALL_COLS), F32),
                        pltpu.SemaphoreType.DMA((7,)), pltpu.SemaphoreType.DMA((7,))],
        operands=(part, w, m, v), rider=rider, semantics=["arbitrary"])


SMALL_NAMES = ("ret_decay_fwd", "ret_decay_bwd", "attn_sink", "ret_gn_gain",
               "ln1_gain", "ln1_bias", "ln2_gain", "ln2_bias")


LN_NAMES = ("ln1_gain", "ln1_bias", "ln2_gain", "ln2_bias")


def _pack_small(vals, extra=None):
    tail = jnp.zeros((1, 1), F32) if extra is None else extra.reshape(1, 1)
    row4 = jnp.concatenate([vals["ret_gn_gain"], vals["ret_decay_fwd"], vals["ret_decay_bwd"], vals["attn_sink"],
                            tail, jnp.zeros((1, SMALL_COLS - LOSS_COL - 1), F32)], axis=1)
    rows = [vals[n] for n in LN_NAMES] + [row4, jnp.zeros((SMALL_ROWS - 5, SMALL_COLS), F32)]
    return jnp.concatenate(rows, axis=0)


def _unpack_small(packed):
    out = {n: packed[i:i + 1] for i, n in enumerate(LN_NAMES)}
    o = RET_WIDTH
    out.update(ret_gn_gain=packed[4:5, 0:o], ret_decay_fwd=packed[4:5, o:o + 8],
               ret_decay_bwd=packed[4:5, o + 8:o + 16], attn_sink=packed[4:5, o + 16:o + 24])
    return out


def _local_step(x, p, tgt, w_in_t, rest, small, pos=None, small_state=None):
    bsz, s, _ = x.shape
    t = bsz * s
    x2d = x.reshape(t, D_MODEL)
    p2d = p.reshape(t, PLE_DIM)
    tgt2d = tgt.reshape(t, D_MODEL)
    dec_f = small["ret_decay_fwd"].reshape(8)
    dec_b = small["ret_decay_bwd"].reshape(8)
    lg_f = jnp.log1p(-jnp.exp2(dec_f))
    lg_b = jnp.log1p(-jnp.exp2(dec_b))
    per_lane = lambda v: jnp.repeat(v, HEAD_DIM).reshape(4, 1, LANES)
    lgf_l, lgb_l = per_lane(lg_f), per_lane(lg_b)
    sink = small["attn_sink"].reshape(8)
    slopes = 2.0 ** (-(jnp.arange(8, dtype=F32) + 1.0))
    gn_gain = small["ret_gn_gain"]
    g1, b1, g2, b2 = (small[n] for n in ("ln1_gain", "ln1_bias", "ln2_gain", "ln2_bias"))

    dist = pos is not None
    shard = dict(zip(REST_NAMES, rest)) if dist else {}
    near = lambda names: _gather_near_rider([shard[n] for n in names])
    wave1, wave2 = ("w_out", "w_ple_gate", "w_ffn_gate"), ("w_ffn_up", "w_ffn_down", "w_ple_proj")
    n1 = len(wave1)
    u, *o1 = _inproj(x2d, w_in_t, rider=near(wave1) if dist else None)
    u3 = u.reshape(bsz, s, IN_WIDTH)
    y_hat, y_rstd, y_ret, ret_rb, ret_kvf, *o2 = _ret_fwd(u3, lgf_l, lgb_l, gn_gain, rider=_merge_riders(
        [_gather_relay_rider(o1), near(wave2)]) if dist else None)
    y_att, att_p, att_ps, *o3 = _attn_fwd(u3, slopes, sink, rider=_merge_riders(
        [_gather_pass_rider(o2[:n1]), _gather_relay_rider(o2[n1:])]) if dist else None)
    gathered = dict(zip(wave1, o3[:n1]))
    w_out = _assemble_weights({"w_out": gathered["w_out"]})["w_out"] if dist else rest["w_out"]
    zh1, r1, hb, *o4 = _outproj_ln1(y_ret.reshape(t, RET_WIDTH), y_att.reshape(t, ATTN_WIDTH), x2d, w_out, g1, b1,
                                    rider=_gather_pass_rider(o3[n1:]) if dist else None)
    gathered.update(zip(wave2, o4))
    wts = _assemble_weights(gathered) if dist else rest
    dz2, dz2b, gs, us, acts, pg, ple, sq, dg2, db2 = _ffn_fwd(
        zh1, hb, p2d, tgt2d, g1, b1, g2, b2, wts["gate4"], wts["up4"], wts["down4"], wts["ple_proj"], wts["ple_gate"])
    dgs, dus, dsp, dple, dz1, dyr, dya, dg1, db1 = _ffn_bwd(dz2, gs, us, pg, ple, zh1, r1, g1, wts["gate4"],
                                                          wts["up4"], wts["down4"], wts["ple_gate"], wts["w_out"])
    ffn_parts = list(_wgrad_ffn(acts, dgs, dus, hb, dz2b))
    d_w_out, d_ple_gate, d_ple_proj, *th_ffn = _wgrad_misc(
        y_ret.reshape(t, RET_WIDTH), y_att.reshape(t, ATTN_WIDTH), dz1, hb, dsp, p2d, dple,
        rider=_exchange_halves_rider(ffn_parts[:2]) if dist else None)
    misc_parts = [d_w_out.reshape(N_SHARD, D_MODEL // N_SHARD, D_MODEL),
                  d_ple_proj.reshape(PLE_DIM, N_SHARD, D_MODEL // N_SHARD).transpose(1, 0, 2),
                  d_ple_gate.reshape(N_SHARD, D_MODEL // N_SHARD, D_MODEL)]
    dyr3, dya3 = dyr.reshape(bsz, s, RET_WIDTH), dya.reshape(bsz, s, ATTN_WIDTH)
    if dist:
        s_gu = _add_halves(ffn_parts[:2], th_ffn, pos)
        quarter = FFN_SHARD // 4
        up_lo, up_hi = s_gu[3][:, :quarter], s_gu[3][:, quarter:]
        later_parts = [ffn_parts[2]] + misc_parts
        drq, drk, drv, drg, rpart, *o5 = _ret_bwd(u3, y_hat, y_rstd, (ret_rb, ret_kvf), dyr3, lgf_l, lgb_l, gn_gain,
                                                  rider=_merge_riders(
            [_exchange_chips_rider([s_gu[2], up_lo]), _exchange_halves_rider(later_parts)]))
        s_dm = _add_halves(later_parts, o5[2:], pos)
        daq, dakv, spart, *o6 = _attn_bwd(u3, dya3, att_p, att_ps, rider=_exchange_chips_rider([up_hi, s_dm[4]]))
    else:
        drq, drk, drv, drg, rpart = _ret_bwd(u3, y_hat, y_rstd, (ret_rb, ret_kvf), dyr3, lgf_l, lgb_l, gn_gain)
        daq, dakv, spart = _attn_bwd(u3, dya3, att_p, att_ps)
    pieces = [a.reshape(t, -1) for a in (drq, drk, drv, drg, daq, dakv)]
    kv0 = CB_AK * LANES
    w_kv = jnp.concatenate([w_in_t[kv0 + o:kv0 + o + HEAD_DIM] for o in KV_ORDER], axis=0)
    d_in, *o7 = _wgrad_in(pieces, x2d, rider=_exchange_chips_rider(list(s_dm[5:])) if dist else None)
    d_in = d_in.reshape(N_SHARD, FFN_SHARD, D_MODEL)

    rsum = rpart
    lane_heads = lambda row: jnp.sum(row.reshape(4, 2, HEAD_DIM), axis=-1).reshape(8)
    dlg_f = lane_heads(rsum[:, 0, :]) + jnp.stack([jnp.sum(rsum[:, 2, :], -1), jnp.sum(rsum[:, 3, :], -1)], 1).reshape(8)
    dlg_b = lane_heads(rsum[:, 1, :]) + jnp.stack([jnp.sum(rsum[:, 4, :], -1), jnp.sum(rsum[:, 5, :], -1)], 1).reshape(8)
    chain = lambda d: -(math.log(2.0) * jnp.exp2(d)) / (1.0 - jnp.exp2(d))
    grads_small = {
        "ret_decay_fwd": (dlg_f * chain(dec_f)).reshape(1, 8),
        "ret_decay_bwd": (dlg_b * chain(dec_b)).reshape(1, 8),
        "attn_sink": jnp.sum(spart, axis=0)[:, 0:4, 0].reshape(1, 8),
        "ret_gn_gain": rsum[:, 6, :].reshape(1, RET_WIDTH),
        "ln1_gain": dg1, "ln1_bias": db1, "ln2_gain": dg2, "ln2_bias": db2,
    }
    if not dist:
        grad_x, = _inproj_bwd(dz1, pieces, w_in_t[:kv0], w_kv)
        grads_rest = [misc_parts[0]] + ffn_parts + misc_parts[1:]
        return sq[0, 0], grad_x.reshape(bsz, s, D_MODEL), d_in, grads_rest, grads_small
    *small_out, th_in = _small_allreduce_adamw(_pack_small(grads_small, sq[0, 0]), *small_state,
                                               rider=_exchange_halves_rider([d_in]))
    s_in = _add_halves([d_in], [th_in], pos)
    grad_x, chips_in = _inproj_bwd(dz1, pieces, w_in_t[:kv0], w_kv, rider=_exchange_chips_rider([s_in[1]]))
    sums32 = [s_in[0], s_dm[1], s_gu[0], s_gu[1], s_dm[0], s_dm[2], s_dm[3]]
    chips_up = jnp.concatenate([o5[1], o6[0]], axis=1)
    from_chips = [chips_in, o7[0], o5[0], chips_up, o6[1], o7[1], o7[2]]
    return grad_x.reshape(bsz, s, D_MODEL), sums32, from_chips, small_out


BIG_NAMES = ("w_in", "w_out", "w_ffn_gate", "w_ffn_up", "w_ffn_down", "w_ple_proj", "w_ple_gate")
REST_NAMES = BIG_NAMES[1:]
TRANSPOSED = ("w_in", "w_ffn_gate", "w_ffn_up")
WEIGHT_ORDER = ("w_in", "ret_decay_fwd", "ret_decay_bwd", "ret_gn_gain", "attn_sink", "w_out", "ln1_gain",
                "ln1_bias", "w_ffn_gate", "w_ffn_up", "w_ffn_down", "w_ple_proj", "w_ple_gate", "ln2_gain", "ln2_bias")


def _shard_rows(name, a):
    return jnp.swapaxes(a[0], 0, 1) if name in TRANSPOSED else a[0]


def _unshard_rows(name, a):
    return (jnp.swapaxes(a, 0, 1) if name in TRANSPOSED else a)[None]


def _assemble_weights(gathered):
    cols = lambda a: a.transpose(1, 0, 2).reshape(a.shape[1], N_SHARD * a.shape[2])
    rows = lambda a: a.reshape(N_SHARD * a.shape[1], a.shape[2])
    same = lambda a: a
    layout = {"w_out": ("w_out", rows), "w_ffn_gate": ("gate4", same), "w_ffn_up": ("up4", same),
              "w_ffn_down": ("down4", same), "w_ple_proj": ("ple_proj", cols), "w_ple_gate": ("ple_gate", rows)}
    return {layout[n][0]: layout[n][1](a) for n, a in gathered.items()}


def kernel(x, p, w_in, ret_decay_fwd, ret_decay_bwd, ret_gn_gain, attn_sink, w_out, ln1_gain, ln1_bias, w_ffn_gate, w_ffn_up, w_ffn_down, w_ple_proj, w_ple_gate, ln2_gain, ln2_bias, loss_target, m_w_in, m_ret_decay_fwd, m_ret_decay_bwd, m_ret_gn_gain, m_attn_sink, m_w_out, m_ln1_gain, m_ln1_bias, m_w_ffn_gate, m_w_ffn_up, m_w_ffn_down, m_w_ple_proj, m_w_ple_gate, m_ln2_gain, m_ln2_bias, v_w_in, v_ret_decay_fwd, v_ret_decay_bwd, v_ret_gn_gain, v_attn_sink, v_w_out, v_ln1_gain, v_ln1_bias, v_w_ffn_gate, v_w_ffn_up, v_w_ffn_down, v_w_ple_proj, v_w_ple_gate, v_ln2_gain, v_ln2_bias):
    w = dict(w_in=w_in, ret_decay_fwd=ret_decay_fwd, ret_decay_bwd=ret_decay_bwd, ret_gn_gain=ret_gn_gain,
             attn_sink=attn_sink, w_out=w_out, ln1_gain=ln1_gain, ln1_bias=ln1_bias, w_ffn_gate=w_ffn_gate,
             w_ffn_up=w_ffn_up, w_ffn_down=w_ffn_down, w_ple_proj=w_ple_proj, w_ple_gate=w_ple_gate,
             ln2_gain=ln2_gain, ln2_bias=ln2_bias)
    m = dict(w_in=m_w_in, ret_decay_fwd=m_ret_decay_fwd, ret_decay_bwd=m_ret_decay_bwd, ret_gn_gain=m_ret_gn_gain,
             attn_sink=m_attn_sink, w_out=m_w_out, ln1_gain=m_ln1_gain, ln1_bias=m_ln1_bias, w_ffn_gate=m_w_ffn_gate,
             w_ffn_up=m_w_ffn_up, w_ffn_down=m_w_ffn_down, w_ple_proj=m_w_ple_proj, w_ple_gate=m_w_ple_gate,
             ln2_gain=m_ln2_gain, ln2_bias=m_ln2_bias)
    v = dict(w_in=v_w_in, ret_decay_fwd=v_ret_decay_fwd, ret_decay_bwd=v_ret_decay_bwd, ret_gn_gain=v_ret_gn_gain,
             attn_sink=v_attn_sink, w_out=v_w_out, ln1_gain=v_ln1_gain, ln1_bias=v_ln1_bias, w_ffn_gate=v_w_ffn_gate,
             w_ffn_up=v_w_ffn_up, w_ffn_down=v_w_ffn_down, w_ple_proj=v_w_ple_proj, w_ple_gate=v_w_ple_gate,
             ln2_gain=v_ln2_gain, ln2_bias=v_ln2_bias)
    big = lambda d: [_shard_rows(n, d[n]) for n in BIG_NAMES]
    small = lambda d: {n: d[n] for n in SMALL_NAMES}

    chip = 2 * lax.axis_index("x") + lax.axis_index("y")
    pos = jnp.stack([chip, lax.axis_index("c")]).astype(jnp.int32)

    shards = [a.astype(BF16) for a in big(w)]
    (w_in4,) = _all_gather_weights(shards[:1])
    w_in_t = w_in4.reshape(IN_WIDTH, D_MODEL)
    grad_x, sums32, from_chips, (g_s, d_s, m_s, v_s) = _local_step(
        x, p[0], loss_target, w_in_t, shards[1:], small(w), pos=pos,
        small_state=(_pack_small(small(w)), _pack_small(small(m)), _pack_small(small(v))))
    g_big, d_big, m_big, v_big = _adamw(big(w), _join_halves(_add_chips(sums32, from_chips, pos)), big(m), big(v))
    loss = g_s[4, LOSS_COL] * (0.5 / D_MODEL)

    def tree(bigs, packed):
        out = {n: _unshard_rows(n, a) for n, a in zip(BIG_NAMES, bigs)}
        out.update(_unpack_small(packed))
        return [out[n] for n in WEIGHT_ORDER]

    return (loss, grad_x, *tree(g_big, g_s), *tree(d_big, d_s), *tree(m_big, m_s), *tree(v_big, v_s))
```

```python
import functools
import math

import jax
import jax.numpy as jnp
from jax import lax
from jax.experimental import pallas as pl
from jax.experimental.pallas import tpu as pltpu

F32 = jnp.float32
BF16 = jnp.bfloat16

D_MODEL = 1024
HEAD_DIM = 64
RET_HEADS = 8
ATTN_HEADS = 8
RET_WIDTH = 512
ATTN_WIDTH = 512
KV_WIDTH = 128
IN_WIDTH = 2816
FFN = 2816
N_SHARD = 4
FFN_SHARD = FFN // N_SHARD
PLE_DIM = 256
CHUNK = 128
LANES = 128
ALPHA = 2.0 ** 0.25
LN_EPS = 1e-5
GN_EPS = 1e-5
NEG_INF = -1e30
ADAM_LR = 0.001
ADAM_B1 = 0.9
ADAM_B2 = 0.999
ADAM_EPS = 1e-08
ADAM_WD = 0.01
ADAM_STEP = 10
VMEM_LIMIT = 56 * 1024 * 1024
MESH = pl.DeviceIdType.MESH

CB_RQ, CB_RK, CB_RV, CB_RG, CB_AQ, CB_AK, CB_AV = 0, 4, 8, 12, 16, 20, 21


def _dot(a, b):
    return jnp.dot(a, b, preferred_element_type=F32)


def _dot_nt(a, b):
    return lax.dot_general(a, b, (((1,), (1,)), ((), ())), preferred_element_type=F32)


def _dot_tn(a, b):
    return lax.dot_general(a, b, (((0,), (0,)), ((), ())), preferred_element_type=F32)


def _sigmoid(x):
    return 1.0 / (1.0 + jnp.exp(-x))


def _params(*sem, vmem=None):
    return pltpu.CompilerParams(dimension_semantics=tuple(sem) if sem else None, vmem_limit_bytes=vmem)


class _Rider:
    def __init__(self, ins, out_shapes, sems, start, finish, aliases=None):
        self.ins, self.out_shapes, self.sems = list(ins), list(out_shapes), list(sems)
        self.start, self.finish, self.aliases = start, finish, dict(aliases or {})


def _merge_riders(riders):
    riders = [r for r in riders if r is not None]
    if len(riders) == 1:
        return riders[0]
    bounds, aliases = [], {}
    i0 = o0 = s0 = 0
    for r in riders:
        bounds.append((i0, o0, s0))
        aliases.update({i0 + i: o0 + o for i, o in r.aliases.items()})
        i0, o0, s0 = i0 + len(r.ins), o0 + len(r.out_shapes), s0 + len(r.sems)

    def each(method):
        def run(ins, outs, sems):
            for r, (i, o, s) in zip(riders, bounds):
                getattr(r, method)(ins[i:i + len(r.ins)], outs[o:o + len(r.out_shapes)], sems[s:s + len(r.sems)])
        return run

    return _Rider([a for r in riders for a in r.ins], [a for r in riders for a in r.out_shapes],
                  [a for r in riders for a in r.sems], each("start"), each("finish"), aliases)


def _hosted_call(body, name, grid, in_specs, out_specs, out_shape, scratch_shapes, operands, rider=None,
                 semantics=None):
    n_in, n_out, n_scr = len(in_specs), len(out_specs), len(scratch_shapes)
    if rider is None:
        return pl.pallas_call(
            body, name=name, grid=grid, in_specs=in_specs, out_specs=out_specs, out_shape=out_shape,
            scratch_shapes=scratch_shapes,
            compiler_params=_params(*(semantics or ["parallel"] * len(grid)), vmem=VMEM_LIMIT))(*operands)
    r_in, r_out = len(rider.ins), len(rider.out_shapes)

    def full_body(*refs):
        main_in, rin = refs[:n_in], refs[n_in:n_in + r_in]
        o0 = n_in + r_in
        main_out, rout = refs[o0:o0 + n_out], refs[o0 + n_out:o0 + n_out + r_out]
        s0 = o0 + n_out + r_out
        main_scr, rsem = refs[s0:s0 + n_scr], refs[s0 + n_scr:]
        first = functools.reduce(jnp.logical_and, [pl.program_id(a) == 0 for a in range(len(grid))])
        last = functools.reduce(jnp.logical_and, [pl.program_id(a) == g - 1 for a, g in enumerate(grid)])

        @pl.when(first)
        def _():
            rider.start(rin, rout, rsem)

        body(*main_in, *main_out, *main_scr)

        @pl.when(last)
        def _():
            rider.finish(rin, rout, rsem)

    hbm = pl.BlockSpec(memory_space=pl.ANY)
    return pl.pallas_call(
        full_body, name=name, grid=grid,
        in_specs=list(in_specs) + [hbm] * r_in, out_specs=list(out_specs) + [hbm] * r_out,
        out_shape=list(out_shape) + rider.out_shapes,
        scratch_shapes=list(scratch_shapes) + rider.sems,
        input_output_aliases={n_in + i: n_out + o for i, o in rider.aliases.items()},
        compiler_params=_params(*(["arbitrary"] * len(grid)), vmem=VMEM_LIMIT),
    )(*operands, *rider.ins)


def _loop_grouped(n, body, init, per_trip=2):
    if n % per_trip:
        return lax.fori_loop(0, n, body, init)

    def trip(i, c):
        for j in range(per_trip):
            c = body(per_trip * i + j, c)
        return c

    return lax.fori_loop(0, n // per_trip, trip, init)


def _head_mean(x, m0):
    s0 = jnp.sum(jnp.where(m0, x, 0.0), axis=1, keepdims=True)
    s1 = jnp.sum(jnp.where(m0, 0.0, x), axis=1, keepdims=True)
    return jnp.where(m0, s0, s1) * (1.0 / HEAD_DIM)


def _inproj(x2d, w_in_t, rider=None):
    t = x2d.shape[0]
    tm = 512
    nb = 256

    def body(x_ref, w_ref, o_ref):
        xb = x_ref[...].astype(BF16)
        for n in range(0, IN_WIDTH, nb):
            o_ref[:, n:n + nb] = _dot_nt(xb, w_ref[n:n + nb, :]).astype(BF16)

    return _hosted_call(
        body, "inproj", (t // tm,),
        in_specs=[pl.BlockSpec((tm, D_MODEL), lambda i: (i, 0)),
                  pl.BlockSpec((IN_WIDTH, D_MODEL), lambda i: (0, 0))],
        out_specs=[pl.BlockSpec((tm, IN_WIDTH), lambda i: (i, 0))],
        out_shape=[jax.ShapeDtypeStruct((t, IN_WIDTH), BF16)],
        scratch_shapes=[], operands=(x2d, w_in_t), rider=rider)


def _outproj_ln1(y_ret, y_att, x2d, w_out, gain, bias, rider=None):
    t = x2d.shape[0]
    tm = 512

    def body(yr_ref, ya_ref, x_ref, w_ref, g_ref, b_ref, zh_ref, r_ref, hb_ref):
        mix = _dot(yr_ref[...], w_ref[0:RET_WIDTH, :]) + _dot(ya_ref[...], w_ref[RET_WIDTH:, :])
        z = ALPHA * x_ref[...] + mix
        mu = jnp.mean(z, axis=1, keepdims=True)
        zc = z - mu
        var = jnp.mean(zc * zc, axis=1, keepdims=True)
        r = lax.rsqrt(var + LN_EPS)
        zh = zc * r
        zh_ref[...] = zh
        r_ref[...] = r
        hb_ref[...] = (zh * g_ref[...] + b_ref[...]).astype(BF16)

    row = lambda w: pl.BlockSpec((tm, w), lambda i: (i, 0))
    const = lambda s: pl.BlockSpec(s, lambda i: (0, 0))
    return _hosted_call(
        body, "outproj_ln1", (t // tm,),
        in_specs=[row(RET_WIDTH), row(ATTN_WIDTH), row(D_MODEL), const((D_MODEL, D_MODEL)),
                  const((1, D_MODEL)), const((1, D_MODEL))],
        out_specs=[row(D_MODEL), row(1), row(D_MODEL)],
        out_shape=[jax.ShapeDtypeStruct((t, D_MODEL), F32), jax.ShapeDtypeStruct((t, 1), F32),
                   jax.ShapeDtypeStruct((t, D_MODEL), BF16)],
        scratch_shapes=[], operands=(y_ret, y_att, x2d, w_out, gain, bias), rider=rider)


def _load_resident(step, pairs, sems):
    copies = [pltpu.make_async_copy(src, dst, sems.at[i]) for i, (src, dst) in enumerate(pairs)]

    @pl.when(step == 0)
    def _():
        for cp in copies:
            cp.start()
        for cp in copies:
            cp.wait()


FFN_CHUNK = 256
N_FFN_CHUNK = FFN // FFN_CHUNK


def _resident_quarters(hbm, vmem):
    q = FFN // N_SHARD
    return [(hbm.at[pl.ds(j * q, q), :], vmem.at[pl.ds(j * q, q), :]) for j in range(N_SHARD)]


def _ln2_loss_tail(zh, mixed, tgt, g1, b1, g2, b2):
    z2 = ALPHA * (zh * g1 + b1) + mixed
    mu = jnp.mean(z2, axis=1, keepdims=True)
    zc = z2 - mu
    var = jnp.mean(zc * zc, axis=1, keepdims=True)
    r = lax.rsqrt(var + LN_EPS)
    zh2 = zc * r
    err = zh2 * g2 + b2 - tgt
    dy = err * (1.0 / D_MODEL)
    dzh = dy * g2
    m1 = jnp.mean(dzh, axis=1, keepdims=True)
    m2 = jnp.mean(dzh * zh2, axis=1, keepdims=True)
    dz2 = r * (dzh - m1 - zh2 * m2)
    return dz2, jnp.sum(err * err), jnp.sum(dy * zh2, axis=0, keepdims=True), jnp.sum(dy, axis=0, keepdims=True)


def _ffn_fwd(zh1, hb, p2d, tgt, g1, b1, g2, b2, wg4, wu4, wd4, wpe, wpg):
    t = zh1.shape[0]
    tm = 256
    wg_t, wu_t, wd_all = (w.reshape(FFN, D_MODEL) for w in (wg4, wu4, wd4))

    def body(zh_ref, hb_ref, p_ref, t_ref, g1_ref, b1_ref, g2_ref, b2_ref,
             wg_hbm, wu_hbm, wd_hbm, wpe_hbm, wpg_hbm,
             dz_ref, dzb_ref, gs_ref, us_ref, act_ref, pg_ref, ple_ref, loss_ref, dg2_ref, db2_ref,
             wg, wu, wd, wpe, wpg, wsem):
        step = pl.program_id(0)
        loads = _resident_quarters(wg_hbm, wg) + _resident_quarters(wu_hbm, wu) + _resident_quarters(wd_hbm, wd)
        _load_resident(step, loads + [(wpe_hbm, wpe), (wpg_hbm, wpg)], wsem)

        @pl.when(step == 0)
        def _():
            loss_ref[...] = jnp.zeros_like(loss_ref)
            dg2_ref[...] = jnp.zeros_like(dg2_ref)
            db2_ref[...] = jnp.zeros_like(db2_ref)

        hbv = hb_ref[...]
        ffn = jnp.zeros((tm, D_MODEL), F32)
        acts = []
        chunks = [slice(n * FFN_CHUNK, (n + 1) * FFN_CHUNK) for n in range(N_FFN_CHUNK)]
        for n in range(N_FFN_CHUNK + 1):
            if n < N_FFN_CHUNK:
                gj = _dot_nt(hbv, wg[chunks[n], :])
                uj = _dot_nt(hbv, wu[chunks[n], :])
                gs_ref[:, chunks[n]] = gj.astype(BF16)
                us_ref[:, chunks[n]] = uj.astype(BF16)
                acts.append((gj * _sigmoid(gj) * uj).astype(BF16))
                act_ref[:, chunks[n]] = acts[n]
            if n > 0:
                ffn = ffn + _dot(acts[n - 1], wd[chunks[n - 1], :])
        ple = _dot(p_ref[...].astype(BF16), wpe[...])
        pg = _sigmoid(_dot(hbv, wpg[...]))
        pg_ref[...] = pg.astype(BF16)
        ple_ref[...] = ple.astype(BF16)
        dz2, sq, dg2, db2 = _ln2_loss_tail(zh_ref[...], ffn + pg * ple, t_ref[...], g1_ref[...], b1_ref[...],
                                           g2_ref[...], b2_ref[...])
        dz_ref[...] = dz2
        dzb_ref[...] = dz2.astype(BF16)
        loss_ref[...] += sq
        dg2_ref[...] += dg2
        db2_ref[...] += db2

    row = lambda w: pl.BlockSpec((tm, w), lambda i: (i, 0))
    const = lambda s: pl.BlockSpec(s, lambda i: (0, 0))
    hid_shape = jax.ShapeDtypeStruct((t, FFN), BF16)
    hbm = pl.BlockSpec(memory_space=pl.ANY)
    return pl.pallas_call(
        body, name="ffn_fwd", grid=(t // tm,),
        in_specs=[row(D_MODEL), row(D_MODEL), row(PLE_DIM), row(D_MODEL),
                  const((1, D_MODEL)), const((1, D_MODEL)), const((1, D_MODEL)), const((1, D_MODEL)),
                  hbm, hbm, hbm, hbm, hbm],
        out_specs=[row(D_MODEL), row(D_MODEL), row(FFN), row(FFN), row(FFN), row(D_MODEL), row(D_MODEL),
                   const((8, LANES)), const((1, D_MODEL)), const((1, D_MODEL))],
        out_shape=[jax.ShapeDtypeStruct((t, D_MODEL), F32), jax.ShapeDtypeStruct((t, D_MODEL), BF16),
                   hid_shape, hid_shape, hid_shape,
                   jax.ShapeDtypeStruct((t, D_MODEL), BF16), jax.ShapeDtypeStruct((t, D_MODEL), BF16),
                   jax.ShapeDtypeStruct((8, LANES), F32),
                   jax.ShapeDtypeStruct((1, D_MODEL), F32), jax.ShapeDtypeStruct((1, D_MODEL), F32)],
        scratch_shapes=[pltpu.VMEM((FFN, D_MODEL), BF16), pltpu.VMEM((FFN, D_MODEL), BF16),
                        pltpu.VMEM((FFN, D_MODEL), BF16),
                        pltpu.VMEM(wpe.shape, BF16), pltpu.VMEM(wpg.shape, BF16),
                        pltpu.SemaphoreType.DMA((3 * N_SHARD + 2,))],
        compiler_params=_params("arbitrary", vmem=VMEM_LIMIT),
    )(zh1, hb, p2d, tgt, g1, b1, g2, b2, wg_t, wu_t, wd_all, wpe, wpg)


def _ret_tables(lgf, lgb):
    c = CHUNK
    row = lax.broadcasted_iota(jnp.int32, (c, LANES), 0).astype(F32)
    ii = lax.broadcasted_iota(jnp.int32, (c, c), 0).astype(F32)
    jj = lax.broadcasted_iota(jnp.int32, (c, c), 1).astype(F32)
    diff = ii - jj
    dmats = []
    for h in range(2):
        lf = lgf[:, h * HEAD_DIM:h * HEAD_DIM + 1]
        lb = lgb[:, h * HEAD_DIM:h * HEAD_DIM + 1]
        dmats.append(jnp.where(diff > 0, jnp.exp(lf * jnp.maximum(diff, 0.0)),
                               jnp.where(diff < 0, jnp.exp(lb * jnp.maximum(-diff, 0.0)), 2.0)))
    tab = dict(
        qdec_f=jnp.exp(lgf * (row + 1.0)), kdec_f=jnp.exp(lgf * (c - 1.0 - row)),
        qdec_b=jnp.exp(lgb * (c - row)), kdec_b=jnp.exp(lgb * row),
        cdec_f=jnp.exp(lgf * c), cdec_b=jnp.exp(lgb * c),
        d0=dmats[0], d1=dmats[1], row=row, diff=diff)
    r = lax.broadcasted_iota(jnp.int32, (LANES, LANES), 0) < HEAD_DIM
    cc = lax.broadcasted_iota(jnp.int32, (LANES, LANES), 1) < HEAD_DIM
    tab["bd"] = r == cc
    tab["m0"] = lax.broadcasted_iota(jnp.int32, (c, LANES), 1) < HEAD_DIM
    return tab


def _ret_specs(bsz, s):
    blk = lambda cb: pl.BlockSpec((bsz, s, LANES), lambda p, cb=cb: (0, 0, cb + p))
    lane = pl.BlockSpec((None, 1, LANES), lambda p: (p, 0, 0))
    gain = pl.BlockSpec((1, LANES), lambda p: (0, p))
    pair = pl.BlockSpec((bsz, s, LANES), lambda p: (0, 0, p))
    return blk, lane, gain, pair


def _ret_state_spec(bsz, n_chunk):
    spec = pl.BlockSpec((None, bsz, n_chunk, LANES, LANES), lambda p: (p, 0, 0, 0, 0))
    return spec, jax.ShapeDtypeStruct((4, bsz, n_chunk, LANES, LANES), F32)


def _ret_kv_states(tb, k_ref, v_ref, rb_ref, kvf_ref, n_chunk):
    c = CHUNK
    bsz = k_ref.shape[0]
    bd = tb["bd"]

    def contributions(n, carry):
        sl = pl.ds(pl.multiple_of(n * c, c), c)
        kfb = []
        for b in range(bsz):
            k32 = k_ref[b, sl, :].astype(F32)
            kfb.append(jnp.concatenate([k32 * tb["kdec_f"], k32 * tb["kdec_b"]], axis=1).astype(BF16))
        kvs = [_dot_tn(kfb[b], v_ref[b, sl, :]) for b in range(bsz)]
        for b in range(bsz):
            kvf_ref[b, n] = jnp.where(bd, kvs[b][0:LANES], 0.0)
            rb_ref[b, n] = jnp.where(bd, kvs[b][LANES:], 0.0)
        return carry

    lax.fori_loop(0, n_chunk, contributions, 0, unroll=2)

    def recur(i, rbs):
        n = n_chunk - 1 - i
        new = []
        for b in range(bsz):
            own = rb_ref[b, n]
            rb_ref[b, n] = rbs[b]
            new.append(rbs[b] * tb["cdec_b"] + own)
        return tuple(new)

    lax.fori_loop(0, n_chunk, recur, tuple(jnp.zeros((LANES, LANES), F32) for _ in range(bsz)))


def _split_rows(x, m0):
    return jnp.concatenate([jnp.where(m0, x, 0.0), jnp.where(m0, 0.0, x)], axis=0).astype(BF16)


def _ret_fwd(u3, lgf_l, lgb_l, gn_gain, rider=None):
    bsz, s, _ = u3.shape
    n_chunk = s // CHUNK
    c = CHUNK

    def body(q_ref, k_ref, v_ref, g_ref, lgf_ref, lgb_ref, gain_ref, yh_ref, rstd_ref, o_ref, rb_ref, kvf_ref):
        tb = _ret_tables(lgf_ref[...], lgb_ref[...])
        m0 = tb["m0"]
        gain = gain_ref[...]
        rows = range(bsz)
        _ret_kv_states(tb, k_ref, v_ref, rb_ref, kvf_ref, n_chunk)

        def chunk(n, rfs):
            sl = pl.ds(pl.multiple_of(n * c, c), c)
            qs = [q_ref[b, sl, :].astype(F32) * 0.125 for b in rows]
            s01 = [_dot_nt(_split_rows(qs[b], m0), k_ref[b, sl, :]) for b in rows]
            ys = []
            for b in rows:
                lhs = jnp.concatenate([s01[b][0:c] * tb["d0"], s01[b][c:] * tb["d1"],
                                       qs[b] * tb["qdec_f"], qs[b] * tb["qdec_b"]], axis=1).astype(BF16)
                rhs = jnp.concatenate([_split_rows(v_ref[b, sl, :].astype(F32), m0),
                                       rfs[b].astype(BF16), rb_ref[b, n].astype(BF16)], axis=0)
                ys.append(_dot(lhs, rhs))
            new = []
            for b in rows:
                y = ys[b]
                mu = _head_mean(y, m0)
                yc = y - mu
                rstd = lax.rsqrt(_head_mean(yc * yc, m0) + GN_EPS)
                yh = yc * rstd
                g = g_ref[b, sl, :].astype(F32)
                yh_ref[b, sl, :] = yh
                rstd_ref[b, sl, :] = rstd
                o_ref[b, sl, :] = (yh * gain * (g * _sigmoid(g))).astype(BF16)
                new.append(rfs[b] * tb["cdec_f"] + kvf_ref[b, n])
            return tuple(new)

        _loop_grouped(n_chunk, chunk, tuple(jnp.zeros((LANES, LANES), F32) for _ in rows))

    blk, lane, gain, pair = _ret_specs(bsz, s)
    state, state_shape = _ret_state_spec(bsz, n_chunk)
    return _hosted_call(
        body, "ret_fwd", (4,),
        in_specs=[blk(CB_RQ), blk(CB_RK), blk(CB_RV), blk(CB_RG), lane, lane, gain],
        out_specs=[pair, pair, pair, state, state],
        out_shape=[jax.ShapeDtypeStruct((bsz, s, RET_WIDTH), F32), jax.ShapeDtypeStruct((bsz, s, RET_WIDTH), F32),
                   jax.ShapeDtypeStruct((bsz, s, RET_WIDTH), BF16), state_shape, state_shape],
        scratch_shapes=[],
        operands=(u3, u3, u3, u3, lgf_l, lgb_l, gn_gain), rider=rider)


def _ret_bwd(u3, y_hat, y_rstd, states, d_o, lgf_l, lgb_l, gn_gain, rider=None):
    bsz, s, _ = u3.shape
    n_chunk = s // CHUNK
    c = CHUNK

    def body(q_ref, k_ref, v_ref, g_ref, yh_ref, rstd_ref, do_ref, lgf_ref, lgb_ref, gain_ref, rb_ref, kvf_ref,
             dq_ref, dk_ref, dv_ref, dg_ref, part_ref,
             rf_ref, dirf_ref, dy_ref, dk_acc, dv_acc, pa0, pa1, vec_ref):
        tb = _ret_tables(lgf_ref[...], lgb_ref[...])
        m0, bd, row = tb["m0"], tb["bd"], tb["row"]
        gain = gain_ref[...]
        wf = jnp.maximum(tb["diff"], 0.0)
        wb = jnp.maximum(-tb["diff"], 0.0)
        rows = range(bsz)
        zero_states = tuple(jnp.zeros((LANES, LANES), F32) for _ in rows)
        for ref in (pa0, pa1):
            ref[...] = jnp.zeros_like(ref)
        vec_ref[...] = jnp.zeros_like(vec_ref)

        def sweep_fwd(n, carry):
            rfs, gbs = carry
            sl = pl.ds(pl.multiple_of(n * c, c), c)
            qs, ks, vs, dys, dybs, q01, k01, dy01 = [], [], [], [], [], [], [], []
            dgain = jnp.zeros((1, LANES), F32)
            for b in rows:
                q = q_ref[b, sl, :].astype(F32) * 0.125
                k = k_ref[b, sl, :]
                yh = yh_ref[b, sl, :]
                rstd = rstd_ref[b, sl, :]
                do = do_ref[b, sl, :].astype(F32)
                g = g_ref[b, sl, :].astype(F32)
                sg = _sigmoid(g)
                sil = g * sg
                dyh = do * gain * sil
                dg_ref[b, sl, :] = (do * yh * gain * sg * (1.0 + g * (1.0 - sg))).astype(BF16)
                dgain = dgain + jnp.sum(do * yh * sil, axis=0, keepdims=True)
                dy = rstd * (dyh - _head_mean(dyh, m0) - yh * _head_mean(dyh * yh, m0))
                dyb = dy.astype(BF16)
                dy_ref[b, sl, :] = dyb
                rf_ref[b, n] = rfs[b]
                qs.append(q)
                ks.append(k)
                vs.append(v_ref[b, sl, :])
                dys.append(dy)
                dybs.append(dyb)
                q01.append(_split_rows(q, m0))
                k01.append(_split_rows(k.astype(F32), m0))
                dy01.append(_split_rows(dy, m0))
            s01 = [_dot_nt(q01[b], ks[b]) for b in rows]
            da01 = [_dot_nt(dy01[b], vs[b]) for b in rows]
            rbn = [rb_ref[b, n] for b in rows]
            states = [jnp.concatenate([rfs[b], rbn[b]], axis=0).astype(BF16) for b in rows]
            dqc = [_dot_nt(dybs[b], states[b]) for b in rows]
            gbb = [gbs[b].astype(BF16) for b in rows]
            dkb = [_dot_nt(vs[b], gbb[b]) for b in rows]
            qfb = [jnp.concatenate([qs[b] * tb["qdec_f"], qs[b] * tb["qdec_b"]], axis=1) for b in rows]
            direct = [_dot_tn(qfb[b].astype(BF16), dybs[b]) for b in rows]
            ds_cat, ds_rows, a_rows = [], [], []
            for b in rows:
                a0 = s01[b][0:c] * tb["d0"]
                a1 = s01[b][c:] * tb["d1"]
                pa0[...] += da01[b][0:c] * a0
                pa1[...] += da01[b][c:] * a1
                ds0 = da01[b][0:c] * tb["d0"]
                ds1 = da01[b][c:] * tb["d1"]
                ds_cat.append(jnp.concatenate([ds0, ds1], axis=1).astype(BF16))
                ds_rows.append(jnp.concatenate([ds0, ds1], axis=0).astype(BF16))
                a_rows.append(jnp.concatenate([a0, a1], axis=0).astype(BF16))
            kbd = [ks[b].astype(F32) * tb["kdec_b"] for b in rows]
            dq_in = [_dot(ds_cat[b], k01[b]) for b in rows]
            dk_in = [_dot_tn(ds_rows[b], q01[b]) for b in rows]
            dv_in = [_dot_tn(a_rows[b], dy01[b]) for b in rows]
            dv_gb = [_dot(kbd[b].astype(BF16), gbb[b]) for b in rows]
            new_rf, new_gb = [], []
            dlf = jnp.zeros((1, LANES), F32)
            dlb = jnp.zeros((1, LANES), F32)
            for b in rows:
                dqf, dqb = dqc[b][:, 0:LANES], dqc[b][:, LANES:]
                qf, qb = qfb[b][:, 0:LANES], qfb[b][:, LANES:]
                dq = dq_in[b] + dqf * tb["qdec_f"] + dqb * tb["qdec_b"]
                dq_ref[b, sl, :] = (dq * 0.125).astype(BF16)
                dk_acc[b, sl, :] = dk_in[b] + dkb[b] * tb["kdec_b"]
                dv_acc[b, sl, :] = dv_in[b] + dv_gb[b]
                dlf = dlf + jnp.sum((row + 1.0) * qf * dqf, axis=0, keepdims=True)
                dlb = dlb + jnp.sum((c - row) * qb * dqb + row * kbd[b] * dkb[b], axis=0, keepdims=True)
                dlb = dlb + c * tb["cdec_b"] * jnp.sum(gbs[b] * rbn[b], axis=0, keepdims=True)
                dirf_ref[b, n] = jnp.where(bd, direct[b][0:LANES], 0.0)
                new_gb.append(jnp.where(bd, direct[b][LANES:], 0.0) + tb["cdec_b"] * gbs[b])
                new_rf.append(rfs[b] * tb["cdec_f"] + kvf_ref[b, n])
            vec_ref[0:1, :] += dlf
            vec_ref[1:2, :] += dlb
            vec_ref[6:7, :] += dgain
            return tuple(new_rf), tuple(new_gb)

        _loop_grouped(n_chunk, sweep_fwd, (zero_states, zero_states), per_trip=4)

        def sweep_bwd(i, gfs):
            n = n_chunk - 1 - i
            sl = pl.ds(pl.multiple_of(n * c, c), c)
            gfb = [gfs[b].astype(BF16) for b in rows]
            kfd = [k_ref[b, sl, :].astype(F32) * tb["kdec_f"] for b in rows]
            dkf = [_dot_nt(v_ref[b, sl, :], gfb[b]) for b in rows]
            dvf = [_dot(kfd[b].astype(BF16), gfb[b]) for b in rows]
            new = []
            dlf = jnp.zeros((1, LANES), F32)
            for b in rows:
                dk_ref[b, sl, :] = (dk_acc[b, sl, :] + dkf[b] * tb["kdec_f"]).astype(BF16)
                dv_ref[b, sl, :] = (dv_acc[b, sl, :] + dvf[b]).astype(BF16)
                dlf = dlf + jnp.sum((c - 1.0 - row) * kfd[b] * dkf[b], axis=0, keepdims=True)
                dlf = dlf + c * tb["cdec_f"] * jnp.sum(gfs[b] * rf_ref[b, n], axis=0, keepdims=True)
                new.append(dirf_ref[b, n] + tb["cdec_f"] * gfs[b])
            vec_ref[0:1, :] += dlf
            return tuple(new)

        _loop_grouped(n_chunk, sweep_bwd, zero_states)
        vec_ref[2:3, :] = jnp.sum(pa0[...] * wf, axis=0, keepdims=True)
        vec_ref[3:4, :] = jnp.sum(pa1[...] * wf, axis=0, keepdims=True)
        vec_ref[4:5, :] = jnp.sum(pa0[...] * wb, axis=0, keepdims=True)
        vec_ref[5:6, :] = jnp.sum(pa1[...] * wb, axis=0, keepdims=True)
        part_ref[...] = vec_ref[...]

    blk, lane, gain, pair = _ret_specs(bsz, s)
    out_bf = jax.ShapeDtypeStruct((bsz, s, RET_WIDTH), BF16)
    state = pltpu.VMEM((bsz, n_chunk, LANES, LANES), F32)
    saved = _ret_state_spec(bsz, n_chunk)[0]
    return _hosted_call(
        body, "ret_bwd", (4,),
        in_specs=[blk(CB_RQ), blk(CB_RK), blk(CB_RV), blk(CB_RG), pair, pair, pair, lane, lane, gain, saved, saved],
        out_specs=[pair, pair, pair, pair, pl.BlockSpec((None, 8, LANES), lambda p: (p, 0, 0))],
        out_shape=[out_bf, out_bf, out_bf, out_bf, jax.ShapeDtypeStruct((4, 8, LANES), F32)],
        scratch_shapes=[state, state,
                        pltpu.VMEM((bsz, s, LANES), BF16), pltpu.VMEM((bsz, s, LANES), F32),
                        pltpu.VMEM((bsz, s, LANES), F32),
                        pltpu.VMEM((c, c), F32), pltpu.VMEM((c, c), F32), pltpu.VMEM((8, LANES), F32)],
        operands=(u3, u3, u3, u3, y_hat, y_rstd, d_o, lgf_l, lgb_l, gn_gain, *states), rider=rider)


def _attn_window_tables(n, s):
    qi = lax.broadcasted_iota(jnp.int32, (CHUNK, 3 * CHUNK), 0)
    kj = lax.broadcasted_iota(jnp.int32, (CHUNK, 3 * CHUNK), 1)
    dist = jnp.abs(kj - CHUNK - qi)
    kpos = n * CHUNK - CHUNK + kj
    valid = (dist <= CHUNK) & (kpos >= 0) & (kpos < s)
    return dist.astype(F32), valid


def _dup_kv_head(x, g):
    lane = lax.broadcasted_iota(jnp.int32, x.shape, 1)
    keep = (lane < HEAD_DIM) == (g == 0)
    xf = x.astype(F32)
    return jnp.where(keep, xf, pltpu.roll(xf, HEAD_DIM, 1))


def _attn_specs(s):
    q = pl.BlockSpec((None, s, 2 * LANES), lambda b, g: (b, 0, CB_AQ // 2 + g))
    k = pl.BlockSpec((None, s, LANES), lambda b, g: (b, 0, CB_AK))
    v = pl.BlockSpec((None, s, LANES), lambda b, g: (b, 0, CB_AV))
    grp = pl.BlockSpec((None, s, 2 * LANES), lambda b, g: (b, 0, g))
    smem = pl.BlockSpec(memory_space=pltpu.SMEM)
    return q, k, v, grp, smem


def _fill_padded(dst_ref, val, s):
    dst_ref[0:CHUNK, :] = jnp.zeros((CHUNK, LANES), dst_ref.dtype)
    dst_ref[CHUNK:CHUNK + s, :] = val.astype(dst_ref.dtype)
    dst_ref[CHUNK + s:2 * CHUNK + s, :] = jnp.zeros((CHUNK, LANES), dst_ref.dtype)


def _attn_probs(sc, slope, snk, dist, valid):
    sc = jnp.where(valid, sc - slope * dist, NEG_INF)
    m = jnp.maximum(jnp.max(sc, axis=1, keepdims=True), snk)
    e = jnp.exp(sc - m)
    es = jnp.exp(snk - m)
    inv = 1.0 / (jnp.sum(e, axis=1, keepdims=True) + es)
    return e * inv, es * inv


def _stack_heads(x2, m0):
    parts = []
    for pr in range(2):
        xp = x2[:, pr * LANES:(pr + 1) * LANES]
        parts += [jnp.where(m0, xp, 0.0), jnp.where(m0, 0.0, xp)]
    return jnp.concatenate(parts, axis=0).astype(BF16)


def _unstack_pair(x_all, pr, m0):
    return jnp.where(m0, x_all[(2 * pr) * CHUNK:(2 * pr + 1) * CHUNK], x_all[(2 * pr + 1) * CHUNK:(2 * pr + 2) * CHUNK])


def _attn_saved_specs(bsz, n_blk):
    specs = [pl.BlockSpec((None, None, n_blk, 4 * CHUNK, w), lambda b, g: (b, g, 0, 0, 0)) for w in (3 * CHUNK, 1)]
    shapes = [jax.ShapeDtypeStruct((bsz, 2, n_blk, 4 * CHUNK, 3 * CHUNK), BF16),
              jax.ShapeDtypeStruct((bsz, 2, n_blk, 4 * CHUNK, 1), F32)]
    return specs, shapes


def _attn_fwd(u3, slopes, sink, rider=None):
    bsz, s, _ = u3.shape
    n_blk = s // CHUNK

    def body(slope_ref, sink_ref, q_ref, k_ref, v_ref, o_ref, p_ref, ps_ref, kp_ref, vp_ref):
        g = pl.program_id(1)
        _fill_padded(kp_ref, _dup_kv_head(k_ref[...], g), s)
        _fill_padded(vp_ref, _dup_kv_head(v_ref[...], g), s)
        m0 = lax.broadcasted_iota(jnp.int32, (CHUNK, LANES), 1) < HEAD_DIM

        def blk(n, carry):
            r0 = pl.multiple_of(n * CHUNK, CHUNK)
            kw = kp_ref[pl.ds(r0, 3 * CHUNK), :]
            vw = vp_ref[pl.ds(r0, 3 * CHUNK), :]
            dist, valid = _attn_window_tables(n, s)
            q_all = _stack_heads(q_ref[pl.ds(r0, CHUNK), :].astype(F32) * 0.125, m0)
            sc_all = _dot_nt(q_all, kw)
            probs, sinks = [], []
            for i in range(4):
                p, ps = _attn_probs(sc_all[i * CHUNK:(i + 1) * CHUNK], slope_ref[g * 4 + i], sink_ref[g * 4 + i],
                                    dist, valid)
                probs.append(p.astype(BF16))
                sinks.append(ps)
            p_all = jnp.concatenate(probs, axis=0)
            p_ref[n] = p_all
            ps_ref[n] = jnp.concatenate(sinks, axis=0)
            out_all = _dot(p_all, vw)
            for pr in range(2):
                o_ref[pl.ds(r0, CHUNK), pr * LANES:(pr + 1) * LANES] = _unstack_pair(out_all, pr, m0).astype(BF16)
            return carry

        lax.fori_loop(0, n_blk, blk, 0, unroll=4)

    q, k, v, grp, smem = _attn_specs(s)
    saved_specs, saved_shapes = _attn_saved_specs(bsz, n_blk)
    return _hosted_call(
        body, "attn_fwd", (bsz, 2),
        in_specs=[smem, smem, q, k, v],
        out_specs=[grp] + saved_specs,
        out_shape=[jax.ShapeDtypeStruct((bsz, s, ATTN_WIDTH), BF16)] + saved_shapes,
        scratch_shapes=[pltpu.VMEM((s + 2 * CHUNK, LANES), BF16), pltpu.VMEM((s + 2 * CHUNK, LANES), BF16)],
        operands=(slopes, sink, u3, u3, u3), rider=rider)


def _attn_bwd(u3, d_o, probs, sink_probs, rider=None):
    bsz, s, _ = u3.shape
    n_blk = s // CHUNK

    def body(q_ref, k_ref, v_ref, do_ref, p_ref, ps_ref, dq_ref, dkv_ref, ds_ref,
             kp_ref, vp_ref, dk_acc, dv_acc):
        g = pl.program_id(1)
        _fill_padded(kp_ref, _dup_kv_head(k_ref[...], g), s)
        _fill_padded(vp_ref, _dup_kv_head(v_ref[...], g), s)
        dk_acc[...] = jnp.zeros_like(dk_acc)
        dv_acc[...] = jnp.zeros_like(dv_acc)
        m0 = lax.broadcasted_iota(jnp.int32, (CHUNK, LANES), 1) < HEAD_DIM

        def blk(n, dsink):
            r0 = pl.multiple_of(n * CHUNK, CHUNK)
            win = pl.ds(r0, 3 * CHUNK)
            kw = kp_ref[win, :]
            vw = vp_ref[win, :]
            q_all = _stack_heads(q_ref[pl.ds(r0, CHUNK), :].astype(F32) * 0.125, m0)
            do_all = _stack_heads(do_ref[pl.ds(r0, CHUNK), :].astype(F32), m0)
            p_all = p_ref[n]
            ps_all = ps_ref[n]
            dp_all = _dot_nt(do_all, vw)
            new_dsink, dscs = [], []
            for i in range(4):
                rows = slice(i * CHUNK, (i + 1) * CHUNK)
                p = p_all[rows].astype(F32)
                dp = dp_all[rows]
                delta = jnp.sum(p * dp, axis=1, keepdims=True)
                dscs.append((p * (dp - delta)).astype(BF16))
                dsh = jnp.sum(ps_all[rows] * delta, axis=0, keepdims=True)
                new_dsink.append(dsink[i] - jnp.broadcast_to(dsh, (1, LANES)))
            dsc_all = jnp.concatenate(dscs, axis=0)
            dq_all = _dot(dsc_all, kw)
            dk_acc[win, :] += _dot_tn(dsc_all, q_all)
            dv_acc[win, :] += _dot_tn(p_all, do_all)
            for pr in range(2):
                dq_ref[pl.ds(r0, CHUNK), pr * LANES:(pr + 1) * LANES] = (
                    _unstack_pair(dq_all, pr, m0) * 0.125).astype(BF16)
            return tuple(new_dsink)

        dsink = _loop_grouped(n_blk, blk, tuple(jnp.zeros((1, LANES), F32) for _ in range(4)), per_trip=4)
        dk = dk_acc[CHUNK:CHUNK + s, :]
        dv = dv_acc[CHUNK:CHUNK + s, :]
        lane = lax.broadcasted_iota(jnp.int32, (s, LANES), 1)
        fold = lambda a: a + pltpu.roll(a, HEAD_DIM, 1)
        dkv_ref[...] = jnp.where(lane < HEAD_DIM, fold(dk), fold(dv)).astype(BF16)
        ds_ref[...] = jnp.zeros_like(ds_ref)
        for i in range(4):
            ds_ref[i:i + 1, :] = dsink[i]

    q, k, v, grp, _ = _attn_specs(s)
    return _hosted_call(
        body, "attn_bwd", (bsz, 2),
        in_specs=[q, k, v, grp] + _attn_saved_specs(bsz, n_blk)[0],
        out_specs=[grp, pl.BlockSpec((None, s, LANES), lambda b, g: (b, 0, g)),
                   pl.BlockSpec((None, None, 8, LANES), lambda b, g: (b, g, 0, 0))],
        out_shape=[jax.ShapeDtypeStruct((bsz, s, ATTN_WIDTH), BF16), jax.ShapeDtypeStruct((bsz, s, 2 * LANES), BF16),
                   jax.ShapeDtypeStruct((bsz, 2, 8, LANES), F32)],
        scratch_shapes=[pltpu.VMEM((s + 2 * CHUNK, LANES), BF16), pltpu.VMEM((s + 2 * CHUNK, LANES), BF16),
                        pltpu.VMEM((s + 2 * CHUNK, LANES), F32), pltpu.VMEM((s + 2 * CHUNK, LANES), F32)],
        operands=(u3, u3, u3, d_o, probs, sink_probs), rider=rider)


def _ffn_bwd(dz2, gs, us, pg, ple, zh1, r1, g1, wg4, wu4, wd4, wpg, w_out):
    t = dz2.shape[0]
    tm = 256
    wg_t, wu_t, wd_all = (w.reshape(FFN, D_MODEL) for w in (wg4, wu4, wd4))

    def body(dz_ref, gs_ref, us_ref, pg_ref, ple_ref, zh_ref, r_ref, g1_ref,
             wg_hbm, wu_hbm, wd_hbm, wpg_hbm, wo_hbm,
             dgs_ref, dus_ref, dsp_ref, dple_ref, dz1_ref, dyr_ref, dya_ref, dg1_ref, db1_ref,
             wg, wu, wd, wpg, wo, wsem):
        step = pl.program_id(0)
        loads = _resident_quarters(wd_hbm, wd) + _resident_quarters(wg_hbm, wg) + _resident_quarters(wu_hbm, wu)
        _load_resident(step, loads + [(wpg_hbm, wpg), (wo_hbm, wo)], wsem)

        @pl.when(step == 0)
        def _():
            dg1_ref[...] = jnp.zeros_like(dg1_ref)
            db1_ref[...] = jnp.zeros_like(db1_ref)

        dz = dz_ref[...]
        dzb = dz.astype(BF16)
        dh = ALPHA * dz
        pending = []
        chunks = [slice(n * FFN_CHUNK, (n + 1) * FFN_CHUNK) for n in range(N_FFN_CHUNK)]
        for n in range(N_FFN_CHUNK + 1):
            if n < N_FFN_CHUNK:
                da = _dot_nt(dzb, wd[chunks[n], :])
                gj = gs_ref[:, chunks[n]].astype(F32)
                uj = us_ref[:, chunks[n]].astype(F32)
                sg = _sigmoid(gj)
                dgj = (da * uj * sg * (1.0 + gj * (1.0 - sg))).astype(BF16)
                duj = (da * gj * sg).astype(BF16)
                dgs_ref[:, chunks[n]] = dgj
                dus_ref[:, chunks[n]] = duj
                pending.append((dgj, duj))
            if n > 0:
                dgp, dup = pending[n - 1]
                dh = dh + _dot(dgp, wg[chunks[n - 1], :]) + _dot(dup, wu[chunks[n - 1], :])
        pgv = pg_ref[...].astype(F32)
        plev = ple_ref[...].astype(F32)
        dple_ref[...] = (dz * pgv).astype(BF16)
        dsp = (dz * plev * pgv * (1.0 - pgv)).astype(BF16)
        dsp_ref[...] = dsp
        dh = dh + _dot_nt(dsp, wpg[...])
        zh = zh_ref[...]
        dg1_ref[...] += jnp.sum(dh * zh, axis=0, keepdims=True)
        db1_ref[...] += jnp.sum(dh, axis=0, keepdims=True)
        dzh = dh * g1_ref[...]
        m1 = jnp.mean(dzh, axis=1, keepdims=True)
        m2 = jnp.mean(dzh * zh, axis=1, keepdims=True)
        dz1 = r_ref[...] * (dzh - m1 - zh * m2)
        dz1_ref[...] = dz1
        dyc = _dot_nt(dz1.astype(BF16), wo[...])
        dyr_ref[...] = dyc[:, 0:RET_WIDTH].astype(BF16)
        dya_ref[...] = dyc[:, RET_WIDTH:].astype(BF16)

    row = lambda w: pl.BlockSpec((tm, w), lambda i: (i, 0))
    const = lambda s: pl.BlockSpec(s, lambda i: (0, 0))
    hbm = pl.BlockSpec(memory_space=pl.ANY)
    hid_shape = jax.ShapeDtypeStruct((t, FFN), BF16)
    return pl.pallas_call(
        body, name="ffn_bwd", grid=(t // tm,),
        in_specs=[row(D_MODEL), row(FFN), row(FFN), row(D_MODEL), row(D_MODEL), row(D_MODEL), row(1),
                  const((1, D_MODEL)), hbm, hbm, hbm, hbm, hbm],
        out_specs=[row(FFN), row(FFN), row(D_MODEL), row(D_MODEL), row(D_MODEL), row(RET_WIDTH), row(ATTN_WIDTH),
                   const((1, D_MODEL)), const((1, D_MODEL))],
        out_shape=[hid_shape, hid_shape, jax.ShapeDtypeStruct((t, D_MODEL), BF16),
                   jax.ShapeDtypeStruct((t, D_MODEL), BF16), jax.ShapeDtypeStruct((t, D_MODEL), F32),
                   jax.ShapeDtypeStruct((t, RET_WIDTH), BF16), jax.ShapeDtypeStruct((t, ATTN_WIDTH), BF16),
                   jax.ShapeDtypeStruct((1, D_MODEL), F32), jax.ShapeDtypeStruct((1, D_MODEL), F32)],
        scratch_shapes=[pltpu.VMEM((FFN, D_MODEL), BF16), pltpu.VMEM((FFN, D_MODEL), BF16),
                        pltpu.VMEM((FFN, D_MODEL), BF16),
                        pltpu.VMEM(wpg.shape, BF16), pltpu.VMEM(w_out.shape, BF16),
                        pltpu.SemaphoreType.DMA((3 * N_SHARD + 2,))],
        compiler_params=_params("arbitrary", vmem=VMEM_LIMIT),
    )(dz2, gs, us, pg, ple, zh1, r1, g1, wg_t, wu_t, wd_all, wpg, w_out)


def _wgrad_misc(y_ret, y_att, dz1, hb, dsp, p2d, dple, rider=None):
    t = dz1.shape[0]
    tk = min(t, 512)

    def body(yr_ref, ya_ref, dz_ref, hb_ref, dsp_ref, p_ref, dple_ref, wo_ref, wpg_ref, wpe_ref):
        @pl.when(pl.program_id(0) == 0)
        def _():
            wo_ref[...] = jnp.zeros_like(wo_ref)
            wpg_ref[...] = jnp.zeros_like(wpg_ref)
            wpe_ref[...] = jnp.zeros_like(wpe_ref)

        dzb = dz_ref[...].astype(BF16)
        wo_ref[0:RET_WIDTH, :] += _dot_tn(yr_ref[...], dzb)
        wo_ref[RET_WIDTH:, :] += _dot_tn(ya_ref[...], dzb)
        wpg_ref[...] += _dot_tn(hb_ref[...], dsp_ref[...])
        wpe_ref[...] += _dot_tn(p_ref[...].astype(BF16), dple_ref[...])

    row = lambda w: pl.BlockSpec((tk, w), lambda k: (k, 0))
    const = lambda s: pl.BlockSpec(s, lambda k: (0, 0))
    return _hosted_call(
        body, "wgrad_misc", (t // tk,),
        in_specs=[row(RET_WIDTH), row(ATTN_WIDTH), row(D_MODEL), row(D_MODEL), row(D_MODEL), row(PLE_DIM),
                  row(D_MODEL)],
        out_specs=[const((D_MODEL, D_MODEL)), const((D_MODEL, D_MODEL)), const((PLE_DIM, D_MODEL))],
        out_shape=[jax.ShapeDtypeStruct((D_MODEL, D_MODEL), F32), jax.ShapeDtypeStruct((D_MODEL, D_MODEL), F32),
                   jax.ShapeDtypeStruct((PLE_DIM, D_MODEL), F32)],
        scratch_shapes=[], operands=(y_ret, y_att, dz1, hb, dsp, p2d, dple), rider=rider, semantics=["arbitrary"])


def _wgrad_ffn(acts, dgs, dus, hb, dz2b):
    t = dz2b.shape[0]
    tk = min(t, 512)
    nk = t // tk

    def body(act_ref, dg_ref, du_ref, hb_ref, dz_ref, og_ref, ou_ref, od_ref):
        @pl.when(pl.program_id(1) == 0)
        def _():
            og_ref[...] = jnp.zeros_like(og_ref)
            ou_ref[...] = jnp.zeros_like(ou_ref)
            od_ref[...] = jnp.zeros_like(od_ref)

        hbv = hb_ref[...]
        og_ref[...] += _dot_tn(dg_ref[...], hbv)
        ou_ref[...] += _dot_tn(du_ref[...], hbv)
        od_ref[...] += _dot_tn(act_ref[...], dz_ref[...])

    half = FFN // 2
    a_spec = pl.BlockSpec((tk, half), lambda j, k: (k, j))
    b_spec = pl.BlockSpec((tk, D_MODEL), lambda j, k: (k, 0))
    o_spec = pl.BlockSpec((half, D_MODEL), lambda j, k: (j, 0))
    o_shape = jax.ShapeDtypeStruct((FFN, D_MODEL), F32)
    outs = pl.pallas_call(
        body, name="wgrad_ffn", grid=(2, nk),
        in_specs=[a_spec, a_spec, a_spec, b_spec, b_spec],
        out_specs=[o_spec] * 3, out_shape=[o_shape] * 3,
        compiler_params=_params("parallel", "arbitrary", vmem=VMEM_LIMIT),
    )(acts, dgs, dus, hb, dz2b)
    return [o.reshape(N_SHARD, FFN_SHARD, D_MODEL) for o in outs]


KV_ORDER = (0, 128, 64, 192)


def _wgrad_in(pieces, x2d, rider=None):
    t = x2d.shape[0]
    tk = min(t, 512)
    nk = t // tk
    kv0 = CB_AK * LANES

    def body(p0, p1, p2, p3, p4, pkv, x_ref, o_ref):
        @pl.when(pl.program_id(0) == 0)
        def _():
            o_ref[...] = jnp.zeros_like(o_ref)

        xb = x_ref[...].astype(BF16)
        for i, ref in enumerate((p0, p1, p2, p3, p4)):
            o_ref[i * 512:(i + 1) * 512, :] += _dot_tn(ref[...], xb)
        dkv = _dot_tn(pkv[...], xb)
        for i, o in enumerate(KV_ORDER):
            o_ref[kv0 + o:kv0 + o + HEAD_DIM, :] += dkv[i * HEAD_DIM:(i + 1) * HEAD_DIM]

    row = lambda w: pl.BlockSpec((tk, w), lambda k: (k, 0))
    return _hosted_call(
        body, "wgrad_in", (nk,),
        in_specs=[row(512)] * 5 + [row(256), row(D_MODEL)],
        out_specs=[pl.BlockSpec((IN_WIDTH, D_MODEL), lambda k: (0, 0))],
        out_shape=[jax.ShapeDtypeStruct((IN_WIDTH, D_MODEL), F32)],
        scratch_shapes=[], operands=(*pieces, x2d), rider=rider, semantics=["arbitrary"])


def _inproj_bwd(dz1, pieces, w_main, w_kv, rider=None):
    t = dz1.shape[0]
    tm = 512

    def body(dz_ref, p0, p1, p2, p3, p4, pkv, wm_ref, wkv_ref, o_ref):
        acc = ALPHA * dz_ref[...]
        for i, ref in enumerate((p0, p1, p2, p3, p4)):
            acc = acc + _dot(ref[...], wm_ref[i * 512:(i + 1) * 512, :])
        o_ref[...] = acc + _dot(pkv[...], wkv_ref[...])

    row = lambda w: pl.BlockSpec((tm, w), lambda i: (i, 0))
    const = lambda s: pl.BlockSpec(s, lambda i: (0, 0))
    return _hosted_call(
        body, "inproj_bwd", (t // tm,),
        in_specs=[row(D_MODEL)] + [row(512)] * 5 + [row(256), const(w_main.shape), const(w_kv.shape)],
        out_specs=[row(D_MODEL)],
        out_shape=[jax.ShapeDtypeStruct((t, D_MODEL), F32)],
        scratch_shapes=[], operands=(dz1, *pieces, w_main, w_kv), rider=rider)


def _coords():
    return lax.axis_index("x"), lax.axis_index("y"), lax.axis_index("c")


def _chip_of(x, y, rel):
    return (1 - x if rel & 2 else x), (1 - y if rel & 1 else y)


def _all_gather_weights(shards):
    first = _gather_near_rider(shards)
    later = [f(first.out_shapes, chained=True) for f in (_gather_relay_rider, _gather_pass_rider)]
    return _run_riders("gather_weights", shards, first.out_shapes, [first] + later)


def _run_riders(name, ins, out_shapes, riders):
    n_in, n_out = len(ins), len(out_shapes)

    def body(*refs):
        in_refs, out_refs = refs[:n_in], refs[n_in:n_in + n_out]
        k = n_in + n_out
        for r in riders:
            sems = refs[k:k + len(r.sems)]
            k += len(r.sems)
            r.start(in_refs, out_refs, sems)
            r.finish(in_refs, out_refs, sems)

    hbm = pl.BlockSpec(memory_space=pl.ANY)
    return pl.pallas_call(
        body, name=name, in_specs=[hbm] * n_in, out_specs=[hbm] * n_out, out_shape=list(out_shapes),
        scratch_shapes=[s for r in riders for s in r.sems],
    )(*ins)


def _gather_half(outs, w, chip, cc):
    h = outs[w].shape[1] // 2
    return outs[w].at[chip, pl.ds(cc * h, h), :]


NEAR = (1, 2)


def _gather_near_rider(shards):
    nw = len(shards)

    def copies(ins, outs, sems, arrivals):
        send, recv, lsend, lrecv = sems
        x, y, c = _coords()
        me = 2 * x + y
        own = [pltpu.make_async_remote_copy(
            src_ref=ins[w], dst_ref=outs[w].at[me], send_sem=lsend.at[w], recv_sem=lrecv.at[w],
            device_id=(x, y, 1 - c), device_id_type=MESH) for w in range(nw)]
        out, arrive = [], []
        for rel in NEAR:
            kx, ky = _chip_of(x, y, rel)
            for w in range(nw):
                h = shards[w].shape[0] // 2
                sem = dict(send_sem=send.at[w * 2 + rel - 1], recv_sem=recv.at[w * 2 + rel - 1],
                           device_id=(kx, ky, c), device_id_type=MESH)
                out.append(pltpu.make_async_remote_copy(
                    src_ref=ins[w].at[pl.ds(c * h, h), :], dst_ref=_gather_half(outs, w, me, c), **sem))
                if arrivals:
                    theirs = _gather_half(outs, w, 2 * kx + ky, c)
                    arrive.append(pltpu.make_async_remote_copy(src_ref=theirs, dst_ref=theirs, **sem))
        return own, out, arrive

    def start(ins, outs, sems):
        own, out, _ = copies(ins, outs, sems, arrivals=False)
        for cp in own + out:
            cp.start()

    def finish(ins, outs, sems):
        own, out, arrive = copies(ins, outs, sems, arrivals=True)
        for cp in arrive:
            cp.wait_recv()
        for cp in out:
            cp.wait_send()
        for cp in own:
            cp.wait()

    dma = pltpu.SemaphoreType.DMA
    return _Rider(shards, [jax.ShapeDtypeStruct((N_SHARD,) + s.shape, s.dtype) for s in shards],
                  [dma((2 * nw,)), dma((2 * nw,)), dma((nw,)), dma((nw,))], start, finish)


def _gather_relay_rider(gathered, chained=False):
    nw = len(gathered)

    def quarter(outs, w, chip, c, p):
        q = outs[w].shape[1] // 4
        return outs[w].at[chip, pl.ds(c * 2 * q + p * q, q), :]

    def copies(outs, sems):
        send, recv = sems
        x, y, c = _coords()
        (yx, yy), (xx, xy), (dx, dy) = (_chip_of(x, y, rel) for rel in (1, 2, 3))
        out, arrive = [], []
        for w in range(nw):
            for p, (src_chip, dst) in enumerate(((2 * xx + xy, (yx, yy)), (2 * yx + yy, (xx, xy)))):
                rows = quarter(outs, w, src_chip, c, p)
                sem = dict(send_sem=send.at[w * 2 + p], recv_sem=recv.at[w * 2 + p], device_id_type=MESH)
                out.append(pltpu.make_async_remote_copy(src_ref=rows, dst_ref=rows, device_id=(*dst, c), **sem))
                mine = quarter(outs, w, 2 * dx + dy, c, p)
                arrive.append(pltpu.make_async_remote_copy(src_ref=mine, dst_ref=mine, device_id=(*dst, c), **sem))
        return out, arrive

    def start(ins, outs, sems):
        for cp in copies(outs, sems)[0]:
            cp.start()

    def finish(ins, outs, sems):
        out, arrive = copies(outs, sems)
        for cp in arrive:
            cp.wait_recv()
        for cp in out:
            cp.wait_send()

    dma = pltpu.SemaphoreType.DMA
    shapes = [jax.ShapeDtypeStruct(g.shape, g.dtype) for g in gathered]
    if chained:
        return _Rider([], [], [dma((2 * nw,)), dma((2 * nw,))], start, finish)
    return _Rider(gathered, shapes, [dma((2 * nw,)), dma((2 * nw,))], start, finish,
                  aliases={w: w for w in range(nw)})


def _gather_pass_rider(gathered, chained=False):
    nw = len(gathered)

    def copies(outs, sems, cc):
        send, recv = sems
        x, y, c = _coords()
        res = []
        for rel in (1, 2, 3):
            kx, ky = _chip_of(x, y, rel)
            for w in range(nw):
                rows = _gather_half(outs, w, 2 * kx + ky, cc)
                res.append(pltpu.make_async_remote_copy(
                    src_ref=rows, dst_ref=rows, send_sem=send.at[w * 3 + rel - 1], recv_sem=recv.at[w * 3 + rel - 1],
                    device_id=(x, y, 1 - c), device_id_type=MESH))
        return res

    def start(ins, outs, sems):
        for cp in copies(outs, sems, lax.axis_index("c")):
            cp.start()

    def finish(ins, outs, sems):
        c = lax.axis_index("c")
        for cp in copies(outs, sems, 1 - c):
            cp.wait_recv()
        for cp in copies(outs, sems, c):
            cp.wait_send()

    dma = pltpu.SemaphoreType.DMA
    shapes = [jax.ShapeDtypeStruct(g.shape, g.dtype) for g in gathered]
    if chained:
        return _Rider([], [], [dma((3 * nw,)), dma((3 * nw,))], start, finish)
    return _Rider(gathered, shapes, [dma((3 * nw,)), dma((3 * nw,))], start, finish,
                  aliases={w: w for w in range(nw)})


def _exchange_halves_rider(parts):
    nw = len(parts)

    def copies(ins, outs, sems):
        send, recv = sems
        x, y, c = _coords()
        res = []
        for w in range(nw):
            h = parts[w].shape[1] // 2
            res.append(pltpu.make_async_remote_copy(
                src_ref=ins[w].at[:, pl.ds((1 - c) * h, h), :], dst_ref=outs[w],
                send_sem=send.at[w], recv_sem=recv.at[w], device_id=(x, y, 1 - c), device_id_type=MESH))
        return res

    def start(ins, outs, sems):
        for cp in copies(ins, outs, sems):
            cp.start()

    def finish(ins, outs, sems):
        for cp in copies(ins, outs, sems):
            cp.wait()

    dma = pltpu.SemaphoreType.DMA
    return _Rider(parts, [jax.ShapeDtypeStruct((N_SHARD, p.shape[1] // 2, p.shape[2]), p.dtype) for p in parts],
                  [dma((nw,)), dma((nw,))], start, finish)


def _add_halves(parts, theirs, pos):
    nw = len(parts)
    split = 2

    def body(pos_ref, *refs):
        ins, oth = refs[:nw], refs[nw:2 * nw]
        o32, o16 = refs[2 * nw:3 * nw], refs[3 * nw:]
        sums = [ins[w][...] + oth[w][...] for w in range(nw)]
        for w in range(nw):
            o16[w][...] = sums[w].astype(BF16)

        @pl.when(pl.program_id(1) == pos_ref[0])
        def _():
            for w in range(nw):
                o32[w][...] = sums[w]

    in_specs, oth_specs, o32_specs, shapes32, shapes16 = [], [], [], [], []
    for p in parts:
        hb = p.shape[1] // 2 // split
        blk = (None, hb, p.shape[2])
        in_specs.append(pl.BlockSpec(blk, lambda i, j, pos_ref: (j, pos_ref[1] * split + i, 0)))
        oth_specs.append(pl.BlockSpec(blk, lambda i, j, pos_ref: (j, i, 0)))
        o32_specs.append(pl.BlockSpec((hb, p.shape[2]), lambda i, j, pos_ref: (i, 0)))
        shapes32.append(jax.ShapeDtypeStruct((p.shape[1] // 2, p.shape[2]), F32))
        shapes16.append(jax.ShapeDtypeStruct((N_SHARD, p.shape[1] // 2, p.shape[2]), BF16))
    return pl.pallas_call(
        body, name="add_halves",
        grid_spec=pltpu.PrefetchScalarGridSpec(
            num_scalar_prefetch=1, grid=(split, N_SHARD),
            in_specs=in_specs + oth_specs, out_specs=o32_specs + oth_specs),
        out_shape=shapes32 + shapes16,
        compiler_params=_params("parallel", "arbitrary", vmem=VMEM_LIMIT),
    )(pos, *parts, *theirs)


def _exchange_chips_rider(sums16):
    nw = len(sums16)

    def copies(ins, outs, sems):
        send, recv = sems
        x, y, c = _coords()
        res = []
        for rel in (1, 2, 3):
            kx, ky = _chip_of(x, y, rel)
            for w in range(nw):
                res.append(pltpu.make_async_remote_copy(
                    src_ref=ins[w].at[2 * kx + ky], dst_ref=outs[w].at[rel - 1],
                    send_sem=send.at[w * 3 + rel - 1], recv_sem=recv.at[w * 3 + rel - 1],
                    device_id=(kx, ky, c), device_id_type=MESH))
        return res

    def start(ins, outs, sems):
        for cp in copies(ins, outs, sems):
            cp.start()

    def finish(ins, outs, sems):
        for cp in copies(ins, outs, sems):
            cp.wait()

    dma = pltpu.SemaphoreType.DMA
    return _Rider(sums16, [jax.ShapeDtypeStruct((3,) + s.shape[1:], BF16) for s in sums16],
                  [dma((3 * nw,)), dma((3 * nw,))], start, finish)


def _add_chips(sums32, theirs, pos):
    nw = len(sums32)
    split = 2

    def body(pos_ref, *refs):
        ins, oth, outs = refs[:nw], refs[nw:2 * nw], refs[2 * nw:]
        for w in range(nw):
            acc = ins[w][...]
            for r in range(3):
                acc = acc + oth[w][r].astype(F32)
            outs[w][...] = acc

    in_specs, oth_specs, out_specs, shapes = [], [], [], []
    for s in sums32:
        hb = s.shape[0] // split
        in_specs.append(pl.BlockSpec((hb, s.shape[1]), lambda i, pos_ref: (i, 0)))
        oth_specs.append(pl.BlockSpec((3, hb, s.shape[1]), lambda i, pos_ref: (0, i, 0)))
        out_specs.append(pl.BlockSpec((hb, s.shape[1]), lambda i, pos_ref: (pos_ref[1] * split + i, 0)))
        shapes.append(jax.ShapeDtypeStruct((2 * s.shape[0], s.shape[1]), F32))
    return pl.pallas_call(
        body, name="add_chips",
        grid_spec=pltpu.PrefetchScalarGridSpec(
            num_scalar_prefetch=1, grid=(split,), in_specs=in_specs + oth_specs, out_specs=out_specs),
        out_shape=shapes,
        compiler_params=_params("parallel", vmem=VMEM_LIMIT),
    )(pos, *sums32, *theirs)


def _join_halves(shards):
    nw = len(shards)

    def body(*refs):
        outs = refs[nw:2 * nw]
        send, recv = refs[2 * nw:]
        x, y, c = _coords()

        def copy(w, cc):
            h = shards[w].shape[0] // 2
            rows = outs[w].at[pl.ds(cc * h, h), :]
            return pltpu.make_async_remote_copy(
                src_ref=rows, dst_ref=rows, send_sem=send.at[w], recv_sem=recv.at[w],
                device_id=(x, y, 1 - c), device_id_type=MESH)

        for w in range(nw):
            copy(w, c).start()
        for w in range(nw):
            copy(w, 1 - c).wait_recv()
            copy(w, c).wait_send()

    hbm = pl.BlockSpec(memory_space=pl.ANY)
    return pl.pallas_call(
        body, name="join_halves",
        in_specs=[hbm] * nw, out_specs=[hbm] * nw,
        out_shape=[jax.ShapeDtypeStruct(s.shape, F32) for s in shards],
        input_output_aliases={w: w for w in range(nw)},
        scratch_shapes=[pltpu.SemaphoreType.DMA((nw,)), pltpu.SemaphoreType.DMA((nw,))],
    )(*shards)


def _adamw_math(w, g, m, v):
    m = ADAM_B1 * m + (1.0 - ADAM_B1) * g
    v = ADAM_B2 * v + (1.0 - ADAM_B2) * (g * g)
    m_hat = m / (1.0 - ADAM_B1 ** ADAM_STEP)
    v_hat = v / (1.0 - ADAM_B2 ** ADAM_STEP)
    delta = -ADAM_LR * (m_hat / (jnp.sqrt(v_hat) + ADAM_EPS) + ADAM_WD * w)
    return delta, m, v


def _adamw(ws, gs, ms, vs):
    nw = len(ws)
    split = 8

    def body(*refs):
        w_r, g_r, m_r, v_r = (refs[i * nw:(i + 1) * nw] for i in range(4))
        g_o, d_o, m_o, v_o = (refs[(4 + i) * nw:(5 + i) * nw] for i in range(4))
        for k in range(nw):
            g = g_r[k][...]
            d, m, v = _adamw_math(w_r[k][...], g, m_r[k][...], v_r[k][...])
            g_o[k][...] = g
            d_o[k][...] = d
            m_o[k][...] = m
            v_o[k][...] = v

    specs = [pl.BlockSpec((w.shape[0] // split, w.shape[1]), lambda i: (i, 0)) for w in ws]
    shapes = [jax.ShapeDtypeStruct(w.shape, F32) for w in ws]
    outs = pl.pallas_call(
        body, name="adamw", grid=(split,),
        in_specs=specs * 4, out_specs=specs * 4, out_shape=shapes * 4,
        compiler_params=_params("parallel", vmem=VMEM_LIMIT),
    )(*ws, *gs, *ms, *vs)
    return outs[:nw], outs[nw:2 * nw], outs[2 * nw:3 * nw], outs[3 * nw:]


SMALL_ROWS = 8
SMALL_COLS = D_MODEL
LOSS_COL = RET_WIDTH + 24


def _small_allreduce_adamw(part, w, m, v, rider=None):
    def body(part_ref, w_ref, m_ref, v_ref, g_out, d_out, m_out, v_out, all_ref, send, recv):
        x, y, c = _coords()
        me = 4 * x + 2 * y + c
        all_ref[me] = part_ref[...]
        copies = []
        for rel in range(1, 8):
            px = 1 - x if rel & 4 else x
            py = 1 - y if rel & 2 else y
            pc = 1 - c if rel & 1 else c
            copies.append(pltpu.make_async_remote_copy(
                src_ref=part_ref, dst_ref=all_ref.at[me],
                send_sem=send.at[rel - 1], recv_sem=recv.at[rel - 1], device_id=(px, py, pc), device_id_type=MESH))
        for cp in copies:
            cp.start()
        for cp in copies:
            cp.wait()
        g = all_ref[0]
        for k in range(1, 8):
            g = g + all_ref[k]
        d, mn, vn = _adamw_math(w_ref[...], g, m_ref[...], v_ref[...])
        g_out[...] = g
        d_out[...] = d
        m_out[...] = mn
        v_out[...] = vn

    vm = pl.BlockSpec(memory_space=pltpu.VMEM)
    shape = jax.ShapeDtypeStruct((SMALL_ROWS, SMALL_COLS), F32)
    return _hosted_call(
        body, "small_allreduce_adamw", (1,),
        in_specs=[vm] * 4, out_specs=[vm] * 4, out_shape=[shape] * 4,
        scratch_shapes=[pltpu.VMEM((8, SMALL_ROWS, SMALL_COLS), F32),
                        pltpu.SemaphoreType.DMA((7,)), pltpu.SemaphoreType.DMA((7,))],
        operands=(part, w, m, v), rider=rider, semantics=["arbitrary"])


SMALL_NAMES = ("ret_decay_fwd", "ret_decay_bwd", "attn_sink", "ret_gn_gain",
               "ln1_gain", "ln1_bias", "ln2_gain", "ln2_bias")


LN_NAMES = ("ln1_gain", "ln1_bias", "ln2_gain", "ln2_bias")


def _pack_small(vals, extra=None):
    tail = jnp.zeros((1, 1), F32) if extra is None else extra.reshape(1, 1)
    row4 = jnp.concatenate([vals["ret_gn_gain"], vals["ret_decay_fwd"], vals["ret_decay_bwd"], vals["attn_sink"],
                            tail, jnp.zeros((1, SMALL_COLS - LOSS_COL - 1), F32)], axis=1)
    rows = [vals[n] for n in LN_NAMES] + [row4, jnp.zeros((SMALL_ROWS - 5, SMALL_COLS), F32)]
    return jnp.concatenate(rows, axis=0)


def _unpack_small(packed):
    out = {n: packed[i:i + 1] for i, n in enumerate(LN_NAMES)}
    o = RET_WIDTH
    out.update(ret_gn_gain=packed[4:5, 0:o], ret_decay_fwd=packed[4:5, o:o + 8],
               ret_decay_bwd=packed[4:5, o + 8:o + 16], attn_sink=packed[4:5, o + 16:o + 24])
    return out


def _local_step(x, p, tgt, w_in_t, rest, small, pos=None, small_state=None):
    bsz, s, _ = x.shape
    t = bsz * s
    x2d = x.reshape(t, D_MODEL)
    p2d = p.reshape(t, PLE_DIM)
    tgt2d = tgt.reshape(t, D_MODEL)
    dec_f = small["ret_decay_fwd"].reshape(8)
    dec_b = small["ret_decay_bwd"].reshape(8)
    lg_f = jnp.log1p(-jnp.exp2(dec_f))
    lg_b = jnp.log1p(-jnp.exp2(dec_b))
    per_lane = lambda v: jnp.repeat(v, HEAD_DIM).reshape(4, 1, LANES)
    lgf_l, lgb_l = per_lane(lg_f), per_lane(lg_b)
    sink = small["attn_sink"].reshape(8)
    slopes = 2.0 ** (-(jnp.arange(8, dtype=F32) + 1.0))
    gn_gain = small["ret_gn_gain"]
    g1, b1, g2, b2 = (small[n] for n in ("ln1_gain", "ln1_bias", "ln2_gain", "ln2_bias"))

    dist = pos is not None
    shard = dict(zip(REST_NAMES, rest)) if dist else {}
    near = lambda names: _gather_near_rider([shard[n] for n in names])
    wave1, wave2 = ("w_out", "w_ple_gate", "w_ffn_gate"), ("w_ffn_up", "w_ffn_down", "w_ple_proj")
    n1 = len(wave1)
    u, *o1 = _inproj(x2d, w_in_t, rider=near(wave1) if dist else None)
    u3 = u.reshape(bsz, s, IN_WIDTH)
    y_hat, y_rstd, y_ret, ret_rb, ret_kvf, *o2 = _ret_fwd(u3, lgf_l, lgb_l, gn_gain, rider=_merge_riders(
        [_gather_relay_rider(o1), near(wave2)]) if dist else None)
    y_att, att_p, att_ps, *o3 = _attn_fwd(u3, slopes, sink, rider=_merge_riders(
        [_gather_pass_rider(o2[:n1]), _gather_relay_rider(o2[n1:])]) if dist else None)
    gathered = dict(zip(wave1, o3[:n1]))
    w_out = _assemble_weights({"w_out": gathered["w_out"]})["w_out"] if dist else rest["w_out"]
    zh1, r1, hb, *o4 = _outproj_ln1(y_ret.reshape(t, RET_WIDTH), y_att.reshape(t, ATTN_WIDTH), x2d, w_out, g1, b1,
                                    rider=_gather_pass_rider(o3[n1:]) if dist else None)
    gathered.update(zip(wave2, o4))
    wts = _assemble_weights(gathered) if dist else rest
    dz2, dz2b, gs, us, acts, pg, ple, sq, dg2, db2 = _ffn_fwd(
        zh1, hb, p2d, tgt2d, g1, b1, g2, b2, wts["gate4"], wts["up4"], wts["down4"], wts["ple_proj"], wts["ple_gate"])
    dgs, dus, dsp, dple, dz1, dyr, dya, dg1, db1 = _ffn_bwd(dz2, gs, us, pg, ple, zh1, r1, g1, wts["gate4"],
                                                          wts["up4"], wts["down4"], wts["ple_gate"], wts["w_out"])
    ffn_parts = list(_wgrad_ffn(acts, dgs, dus, hb, dz2b))
    d_w_out, d_ple_gate, d_ple_proj, *th_ffn = _wgrad_misc(
        y_ret.reshape(t, RET_WIDTH), y_att.reshape(t, ATTN_WIDTH), dz1, hb, dsp, p2d, dple,
        rider=_exchange_halves_rider(ffn_parts[:2]) if dist else None)
    misc_parts = [d_w_out.reshape(N_SHARD, D_MODEL // N_SHARD, D_MODEL),
                  d_ple_proj.reshape(PLE_DIM, N_SHARD, D_MODEL // N_SHARD).transpose(1, 0, 2),
                  d_ple_gate.reshape(N_SHARD, D_MODEL // N_SHARD, D_MODEL)]
    dyr3, dya3 = dyr.reshape(bsz, s, RET_WIDTH), dya.reshape(bsz, s, ATTN_WIDTH)
    if dist:
        s_gu = _add_halves(ffn_parts[:2], th_ffn, pos)
        quarter = FFN_SHARD // 4
        up_lo, up_hi = s_gu[3][:, :quarter], s_gu[3][:, quarter:]
        later_parts = [ffn_parts[2]] + misc_parts
        drq, drk, drv, drg, rpart, *o5 = _ret_bwd(u3, y_hat, y_rstd, (ret_rb, ret_kvf), dyr3, lgf_l, lgb_l, gn_gain,
                                                  rider=_merge_riders(
            [_exchange_chips_rider([s_gu[2], up_lo]), _exchange_halves_rider(later_parts)]))
        s_dm = _add_halves(later_parts, o5[2:], pos)
        daq, dakv, spart, *o6 = _attn_bwd(u3, dya3, att_p, att_ps, rider=_exchange_chips_rider([up_hi, s_dm[4]]))
    else:
        drq, drk, drv, drg, rpart = _ret_bwd(u3, y_hat, y_rstd, (ret_rb, ret_kvf), dyr3, lgf_l, lgb_l, gn_gain)
        daq, dakv, spart = _attn_bwd(u3, dya3, att_p, att_ps)
    pieces = [a.reshape(t, -1) for a in (drq, drk, drv, drg, daq, dakv)]
    kv0 = CB_AK * LANES
    w_kv = jnp.concatenate([w_in_t[kv0 + o:kv0 + o + HEAD_DIM] for o in KV_ORDER], axis=0)
    d_in, *o7 = _wgrad_in(pieces, x2d, rider=_exchange_chips_rider(list(s_dm[5:])) if dist else None)
    d_in = d_in.reshape(N_SHARD, FFN_SHARD, D_MODEL)

    rsum = rpart
    lane_heads = lambda row: jnp.sum(row.reshape(4, 2, HEAD_DIM), axis=-1).reshape(8)
    dlg_f = lane_heads(rsum[:, 0, :]) + jnp.stack([jnp.sum(rsum[:, 2, :], -1), jnp.sum(rsum[:, 3, :], -1)], 1).reshape(8)
    dlg_b = lane_heads(rsum[:, 1, :]) + jnp.stack([jnp.sum(rsum[:, 4, :], -1), jnp.sum(rsum[:, 5, :], -1)], 1).reshape(8)
    chain = lambda d: -(math.log(2.0) * jnp.exp2(d)) / (1.0 - jnp.exp2(d))
    grads_small = {
        "ret_decay_fwd": (dlg_f * chain(dec_f)).reshape(1, 8),
        "ret_decay_bwd": (dlg_b * chain(dec_b)).reshape(1, 8),
        "attn_sink": jnp.sum(spart, axis=0)[:, 0:4, 0].reshape(1, 8),
        "ret_gn_gain": rsum[:, 6, :].reshape(1, RET_WIDTH),
        "ln1_gain": dg1, "ln1_bias": db1, "ln2_gain": dg2, "ln2_bias": db2,
    }
    if not dist:
        grad_x, = _inproj_bwd(dz1, pieces, w_in_t[:kv0], w_kv)
        grads_rest = [misc_parts[0]] + ffn_parts + misc_parts[1:]
        return sq[0, 0], grad_x.reshape(bsz, s, D_MODEL), d_in, grads_rest, grads_small
    *small_out, th_in = _small_allreduce_adamw(_pack_small(grads_small, sq[0, 0]), *small_state,
                                               rider=_exchange_halves_rider([d_in]))
    s_in = _add_halves([d_in], [th_in], pos)
    grad_x, chips_in = _inproj_bwd(dz1, pieces, w_in_t[:kv0], w_kv, rider=_exchange_chips_rider([s_in[1]]))
    sums32 = [s_in[0], s_dm[1], s_gu[0], s_gu[1], s_dm[0], s_dm[2], s_dm[3]]
    chips_up = jnp.concatenate([o5[1], o6[0]], axis=1)
    from_chips = [chips_in, o7[0], o5[0], chips_up, o6[1], o7[1], o7[2]]
    return grad_x.reshape(bsz, s, D_MODEL), sums32, from_chips, small_out


BIG_NAMES = ("w_in", "w_out", "w_ffn_gate", "w_ffn_up", "w_ffn_down", "w_ple_proj", "w_ple_gate")
REST_NAMES = BIG_NAMES[1:]
TRANSPOSED = ("w_in", "w_ffn_gate", "w_ffn_up")
WEIGHT_ORDER = ("w_in", "ret_decay_fwd", "ret_decay_bwd", "ret_gn_gain", "attn_sink", "w_out", "ln1_gain",
                "ln1_bias", "w_ffn_gate", "w_ffn_up", "w_ffn_down", "w_ple_proj", "w_ple_gate", "ln2_gain", "ln2_bias")


def _shard_rows(name, a):
    return jnp.swapaxes(a[0], 0, 1) if name in TRANSPOSED else a[0]


def _unshard_rows(name, a):
    return (jnp.swapaxes(a, 0, 1) if name in TRANSPOSED else a)[None]


def _assemble_weights(gathered):
    cols = lambda a: a.transpose(1, 0, 2).reshape(a.shape[1], N_SHARD * a.shape[2])
    rows = lambda a: a.reshape(N_SHARD * a.shape[1], a.shape[2])
    same = lambda a: a
    layout = {"w_out": ("w_out", rows), "w_ffn_gate": ("gate4", same), "w_ffn_up": ("up4", same),
              "w_ffn_down": ("down4", same), "w_ple_proj": ("ple_proj", cols), "w_ple_gate": ("ple_gate", rows)}
    return {layout[n][0]: layout[n][1](a) for n, a in gathered.items()}


def kernel(x, p, w_in, ret_decay_fwd, ret_decay_bwd, ret_gn_gain, attn_sink, w_out, ln1_gain, ln1_bias, w_ffn_gate, w_ffn_up, w_ffn_down, w_ple_proj, w_ple_gate, ln2_gain, ln2_bias, loss_target, m_w_in, m_ret_decay_fwd, m_ret_decay_bwd, m_ret_gn_gain, m_attn_sink, m_w_out, m_ln1_gain, m_ln1_bias, m_w_ffn_gate, m_w_ffn_up, m_w_ffn_down, m_w_ple_proj, m_w_ple_gate, m_ln2_gain, m_ln2_bias, v_w_in, v_ret_decay_fwd, v_ret_decay_bwd, v_ret_gn_gain, v_attn_sink, v_w_out, v_ln1_gain, v_ln1_bias, v_w_ffn_gate, v_w_ffn_up, v_w_ffn_down, v_w_ple_proj, v_w_ple_gate, v_ln2_gain, v_ln2_bias):
    w = dict(w_in=w_in, ret_decay_fwd=ret_decay_fwd, ret_decay_bwd=ret_decay_bwd, ret_gn_gain=ret_gn_gain,
             attn_sink=attn_sink, w_out=w_out, ln1_gain=ln1_gain, ln1_bias=ln1_bias, w_ffn_gate=w_ffn_gate,
             w_ffn_up=w_ffn_up, w_ffn_down=w_ffn_down, w_ple_proj=w_ple_proj, w_ple_gate=w_ple_gate,
             ln2_gain=ln2_gain, ln2_bias=ln2_bias)
    m = dict(w_in=m_w_in, ret_decay_fwd=m_ret_decay_fwd, ret_decay_bwd=m_ret_decay_bwd, ret_gn_gain=m_ret_gn_gain,
             attn_sink=m_attn_sink, w_out=m_w_out, ln1_gain=m_ln1_gain, ln1_bias=m_ln1_bias, w_ffn_gate=m_w_ffn_gate,
             w_ffn_up=m_w_ffn_up, w_ffn_down=m_w_ffn_down, w_ple_proj=m_w_ple_proj, w_ple_gate=m_w_ple_gate,
             ln2_gain=m_ln2_gain, ln2_bias=m_ln2_bias)
    v = dict(w_in=v_w_in, ret_decay_fwd=v_ret_decay_fwd, ret_decay_bwd=v_ret_decay_bwd, ret_gn_gain=v_ret_gn_gain,
             attn_sink=v_attn_sink, w_out=v_w_out, ln1_gain=v_ln1_gain, ln1_bias=v_ln1_bias, w_ffn_gate=v_w_ffn_gate,
             w_ffn_up=v_w_ffn_up, w_ffn_down=v_w_ffn_down, w_ple_proj=v_w_ple_proj, w_ple_gate=v_w_ple_gate,
             ln2_gain=v_ln2_gain, ln2_bias=v_ln2_bias)
    big = lambda d: [_shard_rows(n, d[n]) for n in BIG_NAMES]
    small = lambda d: {n: d[n] for n in SMALL_NAMES}

    chip = 2 * lax.axis_index("x") + lax.axis_index("y")
    pos = jnp.stack([chip, lax.axis_index("c")]).astype(jnp.int32)

    shards = [a.astype(BF16) for a in big(w)]
    (w_in4,) = _all_gather_weights(shards[:1])
    w_in_t = w_in4.reshape(IN_WIDTH, D_MODEL)
    grad_x, sums32, from_chips, (g_s, d_s, m_s, v_s) = _local_step(
        x, p[0], loss_target, w_in_t, shards[1:], small(w), pos=pos,
        small_state=(_pack_small(small(w)), _pack_small(small(m)), _pack_small(small(v))))
    g_big, d_big, m_big, v_big = _adamw(big(w), _join_halves(_add_chips(sums32, from_chips, pos)), big(m), big(v))
    loss = g_s[4, LOSS_COL] * (0.5 / D_MODEL)

    def tree(bigs, packed):
        out = {n: _unshard_rows(n, a) for n, a in zip(BIG_NAMES, bigs)}
        out.update(_unpack_small(packed))
        return [out[n] for n in WEIGHT_ORDER]

    return (loss, grad_x, *tree(g_big, g_s), *tree(d_big, d_s), *tree(m_big, m_s), *tree(v_big, v_s))
```

```python
import functools
import math

import jax
import jax.numpy as jnp
from jax import lax
from jax.experimental import pallas as pl
from jax.experimental.pallas import tpu as pltpu

F32 = jnp.float32
BF16 = jnp.bfloat16

D_MODEL = 1024
HEAD_DIM = 64
RET_HEADS = 8
ATTN_HEADS = 8
RET_WIDTH = 512
ATTN_WIDTH = 512
KV_WIDTH = 128
IN_WIDTH = 2816
FFN = 2816
N_SHARD = 4
FFN_SHARD = FFN // N_SHARD
PLE_DIM = 256
CHUNK = 128
LANES = 128
ALPHA = 2.0 ** 0.25
LN_EPS = 1e-5
GN_EPS = 1e-5
NEG_INF = -1e30
ADAM_LR = 0.001
ADAM_B1 = 0.9
ADAM_B2 = 0.999
ADAM_EPS = 1e-08
ADAM_WD = 0.01
ADAM_STEP = 10
VMEM_LIMIT = 56 * 1024 * 1024
MESH = pl.DeviceIdType.MESH

CB_RQ, CB_RK, CB_RV, CB_RG, CB_AQ, CB_AK, CB_AV = 0, 4, 8, 12, 16, 20, 21


def _dot(a, b):
    return jnp.dot(a, b, preferred_element_type=F32)


def _dot_nt(a, b):
    return lax.dot_general(a, b, (((1,), (1,)), ((), ())), preferred_element_type=F32)


def _dot_tn(a, b):
    return lax.dot_general(a, b, (((0,), (0,)), ((), ())), preferred_element_type=F32)


def _sigmoid(x):
    return 1.0 / (1.0 + jnp.exp(-x))


def _params(*sem, vmem=None):
    return pltpu.CompilerParams(dimension_semantics=tuple(sem) if sem else None, vmem_limit_bytes=vmem)


class _Rider:
    def __init__(self, ins, out_shapes, sems, start, finish, aliases=None):
        self.ins, self.out_shapes, self.sems = list(ins), list(out_shapes), list(sems)
        self.start, self.finish, self.aliases = start, finish, dict(aliases or {})


def _merge_riders(riders):
    riders = [r for r in riders if r is not None]
    if len(riders) == 1:
        return riders[0]
    bounds, aliases = [], {}
    i0 = o0 = s0 = 0
    for r in riders:
        bounds.append((i0, o0, s0))
        aliases.update({i0 + i: o0 + o for i, o in r.aliases.items()})
        i0, o0, s0 = i0 + len(r.ins), o0 + len(r.out_shapes), s0 + len(r.sems)

    def each(method):
        def run(ins, outs, sems):
            for r, (i, o, s) in zip(riders, bounds):
                getattr(r, method)(ins[i:i + len(r.ins)], outs[o:o + len(r.out_shapes)], sems[s:s + len(r.sems)])
        return run

    return _Rider([a for r in riders for a in r.ins], [a for r in riders for a in r.out_shapes],
                  [a for r in riders for a in r.sems], each("start"), each("finish"), aliases)


def _hosted_call(body, name, grid, in_specs, out_specs, out_shape, scratch_shapes, operands, rider=None,
                 semantics=None):
    n_in, n_out, n_scr = len(in_specs), len(out_specs), len(scratch_shapes)
    if rider is None:
        return pl.pallas_call(
            body, name=name, grid=grid, in_specs=in_specs, out_specs=out_specs, out_shape=out_shape,
            scratch_shapes=scratch_shapes,
            compiler_params=_params(*(semantics or ["parallel"] * len(grid)), vmem=VMEM_LIMIT))(*operands)
    r_in, r_out = len(rider.ins), len(rider.out_shapes)

    def full_body(*refs):
        main_in, rin = refs[:n_in], refs[n_in:n_in + r_in]
        o0 = n_in + r_in
        main_out, rout = refs[o0:o0 + n_out], refs[o0 + n_out:o0 + n_out + r_out]
        s0 = o0 + n_out + r_out
        main_scr, rsem = refs[s0:s0 + n_scr], refs[s0 + n_scr:]
        first = functools.reduce(jnp.logical_and, [pl.program_id(a) == 0 for a in range(len(grid))])
        last = functools.reduce(jnp.logical_and, [pl.program_id(a) == g - 1 for a, g in enumerate(grid)])

        @pl.when(first)
        def _():
            rider.start(rin, rout, rsem)

        body(*main_in, *main_out, *main_scr)

        @pl.when(last)
        def _():
            rider.finish(rin, rout, rsem)

    hbm = pl.BlockSpec(memory_space=pl.ANY)
    return pl.pallas_call(
        full_body, name=name, grid=grid,
        in_specs=list(in_specs) + [hbm] * r_in, out_specs=list(out_specs) + [hbm] * r_out,
        out_shape=list(out_shape) + rider.out_shapes,
        scratch_shapes=list(scratch_shapes) + rider.sems,
        input_output_aliases={n_in + i: n_out + o for i, o in rider.aliases.items()},
        compiler_params=_params(*(["arbitrary"] * len(grid)), vmem=VMEM_LIMIT),
    )(*operands, *rider.ins)


def _loop_grouped(n, body, init, per_trip=2):
    if n % per_trip:
        return lax.fori_loop(0, n, body, init)

    def trip(i, c):
        for j in range(per_trip):
            c = body(per_trip * i + j, c)
        return c

    return lax.fori_loop(0, n // per_trip, trip, init)


def _head_mean(x, m0):
    s0 = jnp.sum(jnp.where(m0, x, 0.0), axis=1, keepdims=True)
    s1 = jnp.sum(jnp.where(m0, 0.0, x), axis=1, keepdims=True)
    return jnp.where(m0, s0, s1) * (1.0 / HEAD_DIM)


def _inproj(x2d, w_in_t, rider=None):
    t = x2d.shape[0]
    tm = 512
    nb = 256

    def body(x_ref, w_ref, o_ref):
        xb = x_ref[...].astype(BF16)
        for n in range(0, IN_WIDTH, nb):
            o_ref[:, n:n + nb] = _dot_nt(xb, w_ref[n:n + nb, :]).astype(BF16)

    return _hosted_call(
        body, "inproj", (t // tm,),
        in_specs=[pl.BlockSpec((tm, D_MODEL), lambda i: (i, 0)),
                  pl.BlockSpec((IN_WIDTH, D_MODEL), lambda i: (0, 0))],
        out_specs=[pl.BlockSpec((tm, IN_WIDTH), lambda i: (i, 0))],
        out_shape=[jax.ShapeDtypeStruct((t, IN_WIDTH), BF16)],
        scratch_shapes=[], operands=(x2d, w_in_t), rider=rider)


def _outproj_ln1(y_ret, y_att, x2d, w_out, gain, bias, rider=None):
    t = x2d.shape[0]
    tm = 512

    def body(yr_ref, ya_ref, x_ref, w_ref, g_ref, b_ref, zh_ref, r_ref, hb_ref):
        mix = _dot(yr_ref[...], w_ref[0:RET_WIDTH, :]) + _dot(ya_ref[...], w_ref[RET_WIDTH:, :])
        z = ALPHA * x_ref[...] + mix
        mu = jnp.mean(z, axis=1, keepdims=True)
        zc = z - mu
        var = jnp.mean(zc * zc, axis=1, keepdims=True)
        r = lax.rsqrt(var + LN_EPS)
        zh = zc * r
        zh_ref[...] = zh
        r_ref[...] = r
        hb_ref[...] = (zh * g_ref[...] + b_ref[...]).astype(BF16)

    row = lambda w: pl.BlockSpec((tm, w), lambda i: (i, 0))
    const = lambda s: pl.BlockSpec(s, lambda i: (0, 0))
    return _hosted_call(
        body, "outproj_ln1", (t // tm,),
        in_specs=[row(RET_WIDTH), row(ATTN_WIDTH), row(D_MODEL), const((D_MODEL, D_MODEL)),
                  const((1, D_MODEL)), const((1, D_MODEL))],
        out_specs=[row(D_MODEL), row(1), row(D_MODEL)],
        out_shape=[jax.ShapeDtypeStruct((t, D_MODEL), F32), jax.ShapeDtypeStruct((t, 1), F32),
                   jax.ShapeDtypeStruct((t, D_MODEL), BF16)],
        scratch_shapes=[], operands=(y_ret, y_att, x2d, w_out, gain, bias), rider=rider)


def _load_resident(step, pairs, sems):
    copies = [pltpu.make_async_copy(src, dst, sems.at[i]) for i, (src, dst) in enumerate(pairs)]

    @pl.when(step == 0)
    def _():
        for cp in copies:
            cp.start()
        for cp in copies:
            cp.wait()


FFN_CHUNK = 256
N_FFN_CHUNK = FFN // FFN_CHUNK


def _resident_quarters(hbm, vmem):
    q = FFN // N_SHARD
    return [(hbm.at[pl.ds(j * q, q), :], vmem.at[pl.ds(j * q, q), :]) for j in range(N_SHARD)]


def _ln2_loss_tail(zh, mixed, tgt, g1, b1, g2, b2):
    z2 = ALPHA * (zh * g1 + b1) + mixed
    mu = jnp.mean(z2, axis=1, keepdims=True)
    zc = z2 - mu
    var = jnp.mean(zc * zc, axis=1, keepdims=True)
    r = lax.rsqrt(var + LN_EPS)
    zh2 = zc * r
    err = zh2 * g2 + b2 - tgt
    dy = err * (1.0 / D_MODEL)
    dzh = dy * g2
    m1 = jnp.mean(dzh, axis=1, keepdims=True)
    m2 = jnp.mean(dzh * zh2, axis=1, keepdims=True)
    dz2 = r * (dzh - m1 - zh2 * m2)
    return dz2, jnp.sum(err * err), jnp.sum(dy * zh2, axis=0, keepdims=True), jnp.sum(dy, axis=0, keepdims=True)


def _ffn_fwd(zh1, hb, p2d, tgt, g1, b1, g2, b2, wg4, wu4, wd4, wpe, wpg):
    t = zh1.shape[0]
    tm = 256
    wg_t, wu_t, wd_all = (w.reshape(FFN, D_MODEL) for w in (wg4, wu4, wd4))

    def body(zh_ref, hb_ref, p_ref, t_ref, g1_ref, b1_ref, g2_ref, b2_ref,
             wg_hbm, wu_hbm, wd_hbm, wpe_hbm, wpg_hbm,
             dz_ref, dzb_ref, gs_ref, us_ref, act_ref, pg_ref, ple_ref, loss_ref, dg2_ref, db2_ref,
             wg, wu, wd, wpe, wpg, wsem):
        step = pl.program_id(0)
        loads = _resident_quarters(wg_hbm, wg) + _resident_quarters(wu_hbm, wu) + _resident_quarters(wd_hbm, wd)
        _load_resident(step, loads + [(wpe_hbm, wpe), (wpg_hbm, wpg)], wsem)

        @pl.when(step == 0)
        def _():
            loss_ref[...] = jnp.zeros_like(loss_ref)
            dg2_ref[...] = jnp.zeros_like(dg2_ref)
            db2_ref[...] = jnp.zeros_like(db2_ref)

        hbv = hb_ref[...]
        ffn = jnp.zeros((tm, D_MODEL), F32)
        acts = []
        chunks = [slice(n * FFN_CHUNK, (n + 1) * FFN_CHUNK) for n in range(N_FFN_CHUNK)]
        for n in range(N_FFN_CHUNK + 1):
            if n < N_FFN_CHUNK:
                gj = _dot_nt(hbv, wg[chunks[n], :])
                uj = _dot_nt(hbv, wu[chunks[n], :])
                gs_ref[:, chunks[n]] = gj.astype(BF16)
                us_ref[:, chunks[n]] = uj.astype(BF16)
                acts.append((gj * _sigmoid(gj) * uj).astype(BF16))
                act_ref[:, chunks[n]] = acts[n]
            if n > 0:
                ffn = ffn + _dot(acts[n - 1], wd[chunks[n - 1], :])
        ple = _dot(p_ref[...].astype(BF16), wpe[...])
        pg = _sigmoid(_dot(hbv, wpg[...]))
        pg_ref[...] = pg.astype(BF16)
        ple_ref[...] = ple.astype(BF16)
        dz2, sq, dg2, db2 = _ln2_loss_tail(zh_ref[...], ffn + pg * ple, t_ref[...], g1_ref[...], b1_ref[...],
                                           g2_ref[...], b2_ref[...])
        dz_ref[...] = dz2
        dzb_ref[...] = dz2.astype(BF16)
        loss_ref[...] += sq
        dg2_ref[...] += dg2
        db2_ref[...] += db2

    row = lambda w: pl.BlockSpec((tm, w), lambda i: (i, 0))
    const = lambda s: pl.BlockSpec(s, lambda i: (0, 0))
    hid_shape = jax.ShapeDtypeStruct((t, FFN), BF16)
    hbm = pl.BlockSpec(memory_space=pl.ANY)
    return pl.pallas_call(
        body, name="ffn_fwd", grid=(t // tm,),
        in_specs=[row(D_MODEL), row(D_MODEL), row(PLE_DIM), row(D_MODEL),
                  const((1, D_MODEL)), const((1, D_MODEL)), const((1, D_MODEL)), const((1, D_MODEL)),
                  hbm, hbm, hbm, hbm, hbm],
        out_specs=[row(D_MODEL), row(D_MODEL), row(FFN), row(FFN), row(FFN), row(D_MODEL), row(D_MODEL),
                   const((8, LANES)), const((1, D_MODEL)), const((1, D_MODEL))],
        out_shape=[jax.ShapeDtypeStruct((t, D_MODEL), F32), jax.ShapeDtypeStruct((t, D_MODEL), BF16),
                   hid_shape, hid_shape, hid_shape,
                   jax.ShapeDtypeStruct((t, D_MODEL), BF16), jax.ShapeDtypeStruct((t, D_MODEL), BF16),
                   jax.ShapeDtypeStruct((8, LANES), F32),
                   jax.ShapeDtypeStruct((1, D_MODEL), F32), jax.ShapeDtypeStruct((1, D_MODEL), F32)],
        scratch_shapes=[pltpu.VMEM((FFN, D_MODEL), BF16), pltpu.VMEM((FFN, D_MODEL), BF16),
                        pltpu.VMEM((FFN, D_MODEL), BF16),
                        pltpu.VMEM(wpe.shape, BF16), pltpu.VMEM(wpg.shape, BF16),
                        pltpu.SemaphoreType.DMA((3 * N_SHARD + 2,))],
        compiler_params=_params("arbitrary", vmem=VMEM_LIMIT),
    )(zh1, hb, p2d, tgt, g1, b1, g2, b2, wg_t, wu_t, wd_all, wpe, wpg)


def _ret_tables(lgf, lgb):
    c = CHUNK
    row = lax.broadcasted_iota(jnp.int32, (c, LANES), 0).astype(F32)
    ii = lax.broadcasted_iota(jnp.int32, (c, c), 0).astype(F32)
    jj = lax.broadcasted_iota(jnp.int32, (c, c), 1).astype(F32)
    diff = ii - jj
    dmats = []
    for h in range(2):
        lf = lgf[:, h * HEAD_DIM:h * HEAD_DIM + 1]
        lb = lgb[:, h * HEAD_DIM:h * HEAD_DIM + 1]
        dmats.append(jnp.where(diff > 0, jnp.exp(lf * jnp.maximum(diff, 0.0)),
                               jnp.where(diff < 0, jnp.exp(lb * jnp.maximum(-diff, 0.0)), 2.0)))
    tab = dict(
        qdec_f=jnp.exp(lgf * (row + 1.0)), kdec_f=jnp.exp(lgf * (c - 1.0 - row)),
        qdec_b=jnp.exp(lgb * (c - row)), kdec_b=jnp.exp(lgb * row),
        cdec_f=jnp.exp(lgf * c), cdec_b=jnp.exp(lgb * c),
        d0=dmats[0], d1=dmats[1], row=row, diff=diff)
    r = lax.broadcasted_iota(jnp.int32, (LANES, LANES), 0) < HEAD_DIM
    cc = lax.broadcasted_iota(jnp.int32, (LANES, LANES), 1) < HEAD_DIM
    tab["bd"] = r == cc
    tab["m0"] = lax.broadcasted_iota(jnp.int32, (c, LANES), 1) < HEAD_DIM
    return tab


def _ret_specs(bsz, s):
    blk = lambda cb: pl.BlockSpec((bsz, s, LANES), lambda p, cb=cb: (0, 0, cb + p))
    lane = pl.BlockSpec((None, 1, LANES), lambda p: (p, 0, 0))
    gain = pl.BlockSpec((1, LANES), lambda p: (0, p))
    pair = pl.BlockSpec((bsz, s, LANES), lambda p: (0, 0, p))
    return blk, lane, gain, pair


def _ret_state_spec(bsz, n_chunk):
    spec = pl.BlockSpec((None, bsz, n_chunk, LANES, LANES), lambda p: (p, 0, 0, 0, 0))
    return spec, jax.ShapeDtypeStruct((4, bsz, n_chunk, LANES, LANES), F32)


def _ret_kv_states(tb, k_ref, v_ref, rb_ref, kvf_ref, n_chunk):
    c = CHUNK
    bsz = k_ref.shape[0]
    bd = tb["bd"]

    def contributions(n, carry):
        sl = pl.ds(pl.multiple_of(n * c, c), c)
        kfb = []
        for b in range(bsz):
            k32 = k_ref[b, sl, :].astype(F32)
            kfb.append(jnp.concatenate([k32 * tb["kdec_f"], k32 * tb["kdec_b"]], axis=1).astype(BF16))
        kvs = [_dot_tn(kfb[b], v_ref[b, sl, :]) for b in range(bsz)]
        for b in range(bsz):
            kvf_ref[b, n] = jnp.where(bd, kvs[b][0:LANES], 0.0)
            rb_ref[b, n] = jnp.where(bd, kvs[b][LANES:], 0.0)
        return carry

    lax.fori_loop(0, n_chunk, contributions, 0, unroll=2)

    def recur(i, rbs):
        n = n_chunk - 1 - i
        new = []
        for b in range(bsz):
            own = rb_ref[b, n]
            rb_ref[b, n] = rbs[b]
            new.append(rbs[b] * tb["cdec_b"] + own)
        return tuple(new)

    lax.fori_loop(0, n_chunk, recur, tuple(jnp.zeros((LANES, LANES), F32) for _ in range(bsz)))


def _split_rows(x, m0):
    return jnp.concatenate([jnp.where(m0, x, 0.0), jnp.where(m0, 0.0, x)], axis=0).astype(BF16)


def _ret_fwd(u3, lgf_l, lgb_l, gn_gain, rider=None):
    bsz, s, _ = u3.shape
    n_chunk = s // CHUNK
    c = CHUNK

    def body(q_ref, k_ref, v_ref, g_ref, lgf_ref, lgb_ref, gain_ref, yh_ref, rstd_ref, o_ref, rb_ref, kvf_ref):
        tb = _ret_tables(lgf_ref[...], lgb_ref[...])
        m0 = tb["m0"]
        gain = gain_ref[...]
        rows = range(bsz)
        _ret_kv_states(tb, k_ref, v_ref, rb_ref, kvf_ref, n_chunk)

        def chunk(n, rfs):
            sl = pl.ds(pl.multiple_of(n * c, c), c)
            qs = [q_ref[b, sl, :].astype(F32) * 0.125 for b in rows]
            s01 = [_dot_nt(_split_rows(qs[b], m0), k_ref[b, sl, :]) for b in rows]
            ys = []
            for b in rows:
                lhs = jnp.concatenate([s01[b][0:c] * tb["d0"], s01[b][c:] * tb["d1"],
                                       qs[b] * tb["qdec_f"], qs[b] * tb["qdec_b"]], axis=1).astype(BF16)
                rhs = jnp.concatenate([_split_rows(v_ref[b, sl, :].astype(F32), m0),
                                       rfs[b].astype(BF16), rb_ref[b, n].astype(BF16)], axis=0)
                ys.append(_dot(lhs, rhs))
            new = []
            for b in rows:
                y = ys[b]
                mu = _head_mean(y, m0)
                yc = y - mu
                rstd = lax.rsqrt(_head_mean(yc * yc, m0) + GN_EPS)
                yh = yc * rstd
                g = g_ref[b, sl, :].astype(F32)
                yh_ref[b, sl, :] = yh
                rstd_ref[b, sl, :] = rstd
                o_ref[b, sl, :] = (yh * gain * (g * _sigmoid(g))).astype(BF16)
                new.append(rfs[b] * tb["cdec_f"] + kvf_ref[b, n])
            return tuple(new)

        _loop_grouped(n_chunk, chunk, tuple(jnp.zeros((LANES, LANES), F32) for _ in rows))

    blk, lane, gain, pair = _ret_specs(bsz, s)
    state, state_shape = _ret_state_spec(bsz, n_chunk)
    return _hosted_call(
        body, "ret_fwd", (4,),
        in_specs=[blk(CB_RQ), blk(CB_RK), blk(CB_RV), blk(CB_RG), lane, lane, gain],
        out_specs=[pair, pair, pair, state, state],
        out_shape=[jax.ShapeDtypeStruct((bsz, s, RET_WIDTH), F32), jax.ShapeDtypeStruct((bsz, s, RET_WIDTH), F32),
                   jax.ShapeDtypeStruct((bsz, s, RET_WIDTH), BF16), state_shape, state_shape],
        scratch_shapes=[],
        operands=(u3, u3, u3, u3, lgf_l, lgb_l, gn_gain), rider=rider)


def _ret_bwd(u3, y_hat, y_rstd, states, d_o, lgf_l, lgb_l, gn_gain, rider=None):
    bsz, s, _ = u3.shape
    n_chunk = s // CHUNK
    c = CHUNK

    def body(q_ref, k_ref, v_ref, g_ref, yh_ref, rstd_ref, do_ref, lgf_ref, lgb_ref, gain_ref, rb_ref, kvf_ref,
             dq_ref, dk_ref, dv_ref, dg_ref, part_ref,
             rf_ref, dirf_ref, dy_ref, dk_acc, dv_acc, pa0, pa1, vec_ref):
        tb = _ret_tables(lgf_ref[...], lgb_ref[...])
        m0, bd, row = tb["m0"], tb["bd"], tb["row"]
        gain = gain_ref[...]
        wf = jnp.maximum(tb["diff"], 0.0)
        wb = jnp.maximum(-tb["diff"], 0.0)
        rows = range(bsz)
        zero_states = tuple(jnp.zeros((LANES, LANES), F32) for _ in rows)
        for ref in (pa0, pa1):
            ref[...] = jnp.zeros_like(ref)
        vec_ref[...] = jnp.zeros_like(vec_ref)

        def sweep_fwd(n, carry):
            rfs, gbs = carry
            sl = pl.ds(pl.multiple_of(n * c, c), c)
            qs, ks, vs, dys, dybs, q01, k01, dy01 = [], [], [], [], [], [], [], []
            dgain = jnp.zeros((1, LANES), F32)
            for b in rows:
                q = q_ref[b, sl, :].astype(F32) * 0.125
                k = k_ref[b, sl, :]
                yh = yh_ref[b, sl, :]
                rstd = rstd_ref[b, sl, :]
                do = do_ref[b, sl, :].astype(F32)
                g = g_ref[b, sl, :].astype(F32)
                sg = _sigmoid(g)
                sil = g * sg
                dyh = do * gain * sil
                dg_ref[b, sl, :] = (do * yh * gain * sg * (1.0 + g * (1.0 - sg))).astype(BF16)
                dgain = dgain + jnp.sum(do * yh * sil, axis=0, keepdims=True)
                dy = rstd * (dyh - _head_mean(dyh, m0) - yh * _head_mean(dyh * yh, m0))
                dyb = dy.astype(BF16)
                dy_ref[b, sl, :] = dyb
                rf_ref[b, n] = rfs[b]
                qs.append(q)
                ks.append(k)
                vs.append(v_ref[b, sl, :])
                dys.append(dy)
                dybs.append(dyb)
                q01.append(_split_rows(q, m0))
                k01.append(_split_rows(k.astype(F32), m0))
                dy01.append(_split_rows(dy, m0))
            s01 = [_dot_nt(q01[b], ks[b]) for b in rows]
            da01 = [_dot_nt(dy01[b], vs[b]) for b in rows]
            rbn = [rb_ref[b, n] for b in rows]
            states = [jnp.concatenate([rfs[b], rbn[b]], axis=0).astype(BF16) for b in rows]
            dqc = [_dot_nt(dybs[b], states[b]) for b in rows]
            gbb = [gbs[b].astype(BF16) for b in rows]
            dkb = [_dot_nt(vs[b], gbb[b]) for b in rows]
            qfb = [jnp.concatenate([qs[b] * tb["qdec_f"], qs[b] * tb["qdec_b"]], axis=1) for b in rows]
            direct = [_dot_tn(qfb[b].astype(BF16), dybs[b]) for b in rows]
            ds_cat, ds_rows, a_rows = [], [], []
            for b in rows:
                a0 = s01[b][0:c] * tb["d0"]
                a1 = s01[b][c:] * tb["d1"]
                pa0[...] += da01[b][0:c] * a0
                pa1[...] += da01[b][c:] * a1
                ds0 = da01[b][0:c] * tb["d0"]
                ds1 = da01[b][c:] * tb["d1"]
                ds_cat.append(jnp.concatenate([ds0, ds1], axis=1).astype(BF16))
                ds_rows.append(jnp.concatenate([ds0, ds1], axis=0).astype(BF16))
                a_rows.append(jnp.concatenate([a0, a1], axis=0).astype(BF16))
            kbd = [ks[b].astype(F32) * tb["kdec_b"] for b in rows]
            dq_in = [_dot(ds_cat[b], k01[b]) for b in rows]
            dk_in = [_dot_tn(ds_rows[b], q01[b]) for b in rows]
            dv_in = [_dot_tn(a_rows[b], dy01[b]) for b in rows]
            dv_gb = [_dot(kbd[b].astype(BF16), gbb[b]) for b in rows]
            new_rf, new_gb = [], []
            dlf = jnp.zeros((1, LANES), F32)
            dlb = jnp.zeros((1, LANES), F32)
            for b in rows:
                dqf, dqb = dqc[b][:, 0:LANES], dqc[b][:, LANES:]
                qf, qb = qfb[b][:, 0:LANES], qfb[b][:, LANES:]
                dq = dq_in[b] + dqf * tb["qdec_f"] + dqb * tb["qdec_b"]
                dq_ref[b, sl, :] = (dq * 0.125).astype(BF16)
                dk_acc[b, sl, :] = dk_in[b] + dkb[b] * tb["kdec_b"]
                dv_acc[b, sl, :] = dv_in[b] + dv_gb[b]
                dlf = dlf + jnp.sum((row + 1.0) * qf * dqf, axis=0, keepdims=True)
                dlb = dlb + jnp.sum((c - row) * qb * dqb + row * kbd[b] * dkb[b], axis=0, keepdims=True)
                dlb = dlb + c * tb["cdec_b"] * jnp.sum(gbs[b] * rbn[b], axis=0, keepdims=True)
                dirf_ref[b, n] = jnp.where(bd, direct[b][0:LANES], 0.0)
                new_gb.append(jnp.where(bd, direct[b][LANES:], 0.0) + tb["cdec_b"] * gbs[b])
                new_rf.append(rfs[b] * tb["cdec_f"] + kvf_ref[b, n])
            vec_ref[0:1, :] += dlf
            vec_ref[1:2, :] += dlb
            vec_ref[6:7, :] += dgain
            return tuple(new_rf), tuple(new_gb)

        _loop_grouped(n_chunk, sweep_fwd, (zero_states, zero_states), per_trip=4)

        def sweep_bwd(i, gfs):
            n = n_chunk - 1 - i
            sl = pl.ds(pl.multiple_of(n * c, c), c)
            gfb = [gfs[b].astype(BF16) for b in rows]
            kfd = [k_ref[b, sl, :].astype(F32) * tb["kdec_f"] for b in rows]
            dkf = [_dot_nt(v_ref[b, sl, :], gfb[b]) for b in rows]
            dvf = [_dot(kfd[b].astype(BF16), gfb[b]) for b in rows]
            new = []
            dlf = jnp.zeros((1, LANES), F32)
            for b in rows:
                dk_ref[b, sl, :] = (dk_acc[b, sl, :] + dkf[b] * tb["kdec_f"]).astype(BF16)
                dv_ref[b, sl, :] = (dv_acc[b, sl, :] + dvf[b]).astype(BF16)
                dlf = dlf + jnp.sum((c - 1.0 - row) * kfd[b] * dkf[b], axis=0, keepdims=True)
                dlf = dlf + c * tb["cdec_f"] * jnp.sum(gfs[b] * rf_ref[b, n], axis=0, keepdims=True)
                new.append(dirf_ref[b, n] + tb["cdec_f"] * gfs[b])
            vec_ref[0:1, :] += dlf
            return tuple(new)

        _loop_grouped(n_chunk, sweep_bwd, zero_states)
        vec_ref[2:3, :] = jnp.sum(pa0[...] * wf, axis=0, keepdims=True)
        vec_ref[3:4, :] = jnp.sum(pa1[...] * wf, axis=0, keepdims=True)
        vec_ref[4:5, :] = jnp.sum(pa0[...] * wb, axis=0, keepdims=True)
        vec_ref[5:6, :] = jnp.sum(pa1[...] * wb, axis=0, keepdims=True)
        part_ref[...] = vec_ref[...]

    blk, lane, gain, pair = _ret_specs(bsz, s)
    out_bf = jax.ShapeDtypeStruct((bsz, s, RET_WIDTH), BF16)
    state = pltpu.VMEM((bsz, n_chunk, LANES, LANES), F32)
    saved = _ret_state_spec(bsz, n_chunk)[0]
    return _hosted_call(
        body, "ret_bwd", (4,),
        in_specs=[blk(CB_RQ), blk(CB_RK), blk(CB_RV), blk(CB_RG), pair, pair, pair, lane, lane, gain, saved, saved],
        out_specs=[pair, pair, pair, pair, pl.BlockSpec((None, 8, LANES), lambda p: (p, 0, 0))],
        out_shape=[out_bf, out_bf, out_bf, out_bf, jax.ShapeDtypeStruct((4, 8, LANES), F32)],
        scratch_shapes=[state, state,
                        pltpu.VMEM((bsz, s, LANES), BF16), pltpu.VMEM((bsz, s, LANES), F32),
                        pltpu.VMEM((bsz, s, LANES), F32),
                        pltpu.VMEM((c, c), F32), pltpu.VMEM((c, c), F32), pltpu.VMEM((8, LANES), F32)],
        operands=(u3, u3, u3, u3, y_hat, y_rstd, d_o, lgf_l, lgb_l, gn_gain, *states), rider=rider)


def _attn_window_tables(n, s):
    qi = lax.broadcasted_iota(jnp.int32, (CHUNK, 3 * CHUNK), 0)
    kj = lax.broadcasted_iota(jnp.int32, (CHUNK, 3 * CHUNK), 1)
    dist = jnp.abs(kj - CHUNK - qi)
    kpos = n * CHUNK - CHUNK + kj
    valid = (dist <= CHUNK) & (kpos >= 0) & (kpos < s)
    return dist.astype(F32), valid


def _dup_kv_head(x, g):
    lane = lax.broadcasted_iota(jnp.int32, x.shape, 1)
    keep = (lane < HEAD_DIM) == (g == 0)
    xf = x.astype(F32)
    return jnp.where(keep, xf, pltpu.roll(xf, HEAD_DIM, 1))


def _attn_specs(s):
    q = pl.BlockSpec((None, s, 2 * LANES), lambda b, g: (b, 0, CB_AQ // 2 + g))
    k = pl.BlockSpec((None, s, LANES), lambda b, g: (b, 0, CB_AK))
    v = pl.BlockSpec((None, s, LANES), lambda b, g: (b, 0, CB_AV))
    grp = pl.BlockSpec((None, s, 2 * LANES), lambda b, g: (b, 0, g))
    smem = pl.BlockSpec(memory_space=pltpu.SMEM)
    return q, k, v, grp, smem


def _fill_padded(dst_ref, val, s):
    dst_ref[0:CHUNK, :] = jnp.zeros((CHUNK, LANES), dst_ref.dtype)
    dst_ref[CHUNK:CHUNK + s, :] = val.astype(dst_ref.dtype)
    dst_ref[CHUNK + s:2 * CHUNK + s, :] = jnp.zeros((CHUNK, LANES), dst_ref.dtype)


def _attn_probs(sc, slope, snk, dist, valid):
    sc = jnp.where(valid, sc - slope * dist, NEG_INF)
    m = jnp.maximum(jnp.max(sc, axis=1, keepdims=True), snk)
    e = jnp.exp(sc - m)
    es = jnp.exp(snk - m)
    inv = 1.0 / (jnp.sum(e, axis=1, keepdims=True) + es)
    return e * inv, es * inv


def _stack_heads(x2, m0):
    parts = []
    for pr in range(2):
        xp = x2[:, pr * LANES:(pr + 1) * LANES]
        parts += [jnp.where(m0, xp, 0.0), jnp.where(m0, 0.0, xp)]
    return jnp.concatenate(parts, axis=0).astype(BF16)


def _unstack_pair(x_all, pr, m0):
    return jnp.where(m0, x_all[(2 * pr) * CHUNK:(2 * pr + 1) * CHUNK], x_all[(2 * pr + 1) * CHUNK:(2 * pr + 2) * CHUNK])


def _attn_saved_specs(bsz, n_blk):
    specs = [pl.BlockSpec((None, None, n_blk, 4 * CHUNK, w), lambda b, g: (b, g, 0, 0, 0)) for w in (3 * CHUNK, 1)]
    shapes = [jax.ShapeDtypeStruct((bsz, 2, n_blk, 4 * CHUNK, 3 * CHUNK), BF16),
              jax.ShapeDtypeStruct((bsz, 2, n_blk, 4 * CHUNK, 1), F32)]
    return specs, shapes


def _attn_fwd(u3, slopes, sink, rider=None):
    bsz, s, _ = u3.shape
    n_blk = s // CHUNK

    def body(slope_ref, sink_ref, q_ref, k_ref, v_ref, o_ref, p_ref, ps_ref, kp_ref, vp_ref):
        g = pl.program_id(1)
        _fill_padded(kp_ref, _dup_kv_head(k_ref[...], g), s)
        _fill_padded(vp_ref, _dup_kv_head(v_ref[...], g), s)
        m0 = lax.broadcasted_iota(jnp.int32, (CHUNK, LANES), 1) < HEAD_DIM

        def blk(n, carry):
            r0 = pl.multiple_of(n * CHUNK, CHUNK)
            kw = kp_ref[pl.ds(r0, 3 * CHUNK), :]
            vw = vp_ref[pl.ds(r0, 3 * CHUNK), :]
            dist, valid = _attn_window_tables(n, s)
            q_all = _stack_heads(q_ref[pl.ds(r0, CHUNK), :].astype(F32) * 0.125, m0)
            sc_all = _dot_nt(q_all, kw)
            probs, sinks = [], []
            for i in range(4):
                p, ps = _attn_probs(sc_all[i * CHUNK:(i + 1) * CHUNK], slope_ref[g * 4 + i], sink_ref[g * 4 + i],
                                    dist, valid)
                probs.append(p.astype(BF16))
                sinks.append(ps)
            p_all = jnp.concatenate(probs, axis=0)
            p_ref[n] = p_all
            ps_ref[n] = jnp.concatenate(sinks, axis=0)
            out_all = _dot(p_all, vw)
            for pr in range(2):
                o_ref[pl.ds(r0, CHUNK), pr * LANES:(pr + 1) * LANES] = _unstack_pair(out_all, pr, m0).astype(BF16)
            return carry

        lax.fori_loop(0, n_blk, blk, 0, unroll=4)

    q, k, v, grp, smem = _attn_specs(s)
    saved_specs, saved_shapes = _attn_saved_specs(bsz, n_blk)
    return _hosted_call(
        body, "attn_fwd", (bsz, 2),
        in_specs=[smem, smem, q, k, v],
        out_specs=[grp] + saved_specs,
        out_shape=[jax.ShapeDtypeStruct((bsz, s, ATTN_WIDTH), BF16)] + saved_shapes,
        scratch_shapes=[pltpu.VMEM((s + 2 * CHUNK, LANES), BF16), pltpu.VMEM((s + 2 * CHUNK, LANES), BF16)],
        operands=(slopes, sink, u3, u3, u3), rider=rider)


def _attn_bwd(u3, d_o, probs, sink_probs, rider=None):
    bsz, s, _ = u3.shape
    n_blk = s // CHUNK

    def body(q_ref, k_ref, v_ref, do_ref, p_ref, ps_ref, dq_ref, dkv_ref, ds_ref,
             kp_ref, vp_ref, dk_acc, dv_acc):
        g = pl.program_id(1)
        _fill_padded(kp_ref, _dup_kv_head(k_ref[...], g), s)
        _fill_padded(vp_ref, _dup_kv_head(v_ref[...], g), s)
        dk_acc[...] = jnp.zeros_like(dk_acc)
        dv_acc[...] = jnp.zeros_like(dv_acc)
        m0 = lax.broadcasted_iota(jnp.int32, (CHUNK, LANES), 1) < HEAD_DIM

        def blk(n, dsink):
            r0 = pl.multiple_of(n * CHUNK, CHUNK)
            win = pl.ds(r0, 3 * CHUNK)
            kw = kp_ref[win, :]
            vw = vp_ref[win, :]
            q_all = _stack_heads(q_ref[pl.ds(r0, CHUNK), :].astype(F32) * 0.125, m0)
            do_all = _stack_heads(do_ref[pl.ds(r0, CHUNK), :].astype(F32), m0)
            p_all = p_ref[n]
            ps_all = ps_ref[n]
            dp_all = _dot_nt(do_all, vw)
            new_dsink, dscs = [], []
            for i in range(4):
                rows = slice(i * CHUNK, (i + 1) * CHUNK)
                p = p_all[rows].astype(F32)
                dp = dp_all[rows]
                delta = jnp.sum(p * dp, axis=1, keepdims=True)
                dscs.append((p * (dp - delta)).astype(BF16))
                dsh = jnp.sum(ps_all[rows] * delta, axis=0, keepdims=True)
                new_dsink.append(dsink[i] - jnp.broadcast_to(dsh, (1, LANES)))
            dsc_all = jnp.concatenate(dscs, axis=0)
            dq_all = _dot(dsc_all, kw)
            dk_acc[win, :] += _dot_tn(dsc_all, q_all)
            dv_acc[win, :] += _dot_tn(p_all, do_all)
            for pr in range(2):
                dq_ref[pl.ds(r0, CHUNK), pr * LANES:(pr + 1) * LANES] = (
                    _unstack_pair(dq_all, pr, m0) * 0.125).astype(BF16)
            return tuple(new_dsink)

        dsink = _loop_grouped(n_blk, blk, tuple(jnp.zeros((1, LANES), F32) for _ in range(4)), per_trip=4)
        dk = dk_acc[CHUNK:CHUNK + s, :]
        dv = dv_acc[CHUNK:CHUNK + s, :]
        lane = lax.broadcasted_iota(jnp.int32, (s, LANES), 1)
        fold = lambda a: a + pltpu.roll(a, HEAD_DIM, 1)
        dkv_ref[...] = jnp.where(lane < HEAD_DIM, fold(dk), fold(dv)).astype(BF16)
        ds_ref[...] = jnp.zeros_like(ds_ref)
        for i in range(4):
            ds_ref[i:i + 1, :] = dsink[i]

    q, k, v, grp, _ = _attn_specs(s)
    return _hosted_call(
        body, "attn_bwd", (bsz, 2),
        in_specs=[q, k, v, grp] + _attn_saved_specs(bsz, n_blk)[0],
        out_specs=[grp, pl.BlockSpec((None, s, LANES), lambda b, g: (b, 0, g)),
                   pl.BlockSpec((None, None, 8, LANES), lambda b, g: (b, g, 0, 0))],
        out_shape=[jax.ShapeDtypeStruct((bsz, s, ATTN_WIDTH), BF16), jax.ShapeDtypeStruct((bsz, s, 2 * LANES), BF16),
                   jax.ShapeDtypeStruct((bsz, 2, 8, LANES), F32)],
        scratch_shapes=[pltpu.VMEM((s + 2 * CHUNK, LANES), BF16), pltpu.VMEM((s + 2 * CHUNK, LANES), BF16),
                        pltpu.VMEM((s + 2 * CHUNK, LANES), F32), pltpu.VMEM((s + 2 * CHUNK, LANES), F32)],
        operands=(u3, u3, u3, d_o, probs, sink_probs), rider=rider)


def _ffn_bwd(dz2, gs, us, pg, ple, zh1, r1, g1, wg4, wu4, wd4, wpg, w_out):
    t = dz2.shape[0]
    tm = 256
    wg_t, wu_t, wd_all = (w.reshape(FFN, D_MODEL) for w in (wg4, wu4, wd4))

    def body(dz_ref, gs_ref, us_ref, pg_ref, ple_ref, zh_ref, r_ref, g1_ref,
             wg_hbm, wu_hbm, wd_hbm, wpg_hbm, wo_hbm,
             dgs_ref, dus_ref, dsp_ref, dple_ref, dz1_ref, dyr_ref, dya_ref, dg1_ref, db1_ref,
             wg, wu, wd, wpg, wo, wsem):
        step = pl.program_id(0)
        loads = _resident_quarters(wd_hbm, wd) + _resident_quarters(wg_hbm, wg) + _resident_quarters(wu_hbm, wu)
        _load_resident(step, loads + [(wpg_hbm, wpg), (wo_hbm, wo)], wsem)

        @pl.when(step == 0)
        def _():
            dg1_ref[...] = jnp.zeros_like(dg1_ref)
            db1_ref[...] = jnp.zeros_like(db1_ref)

        dz = dz_ref[...]
        dzb = dz.astype(BF16)
        dh = ALPHA * dz
        pending = []
        chunks = [slice(n * FFN_CHUNK, (n + 1) * FFN_CHUNK) for n in range(N_FFN_CHUNK)]
        for n in range(N_FFN_CHUNK + 1):
            if n < N_FFN_CHUNK:
                da = _dot_nt(dzb, wd[chunks[n], :])
                gj = gs_ref[:, chunks[n]].astype(F32)
                uj = us_ref[:, chunks[n]].astype(F32)
                sg = _sigmoid(gj)
                dgj = (da * uj * sg * (1.0 + gj * (1.0 - sg))).astype(BF16)
                duj = (da * gj * sg).astype(BF16)
                dgs_ref[:, chunks[n]] = dgj
                dus_ref[:, chunks[n]] = duj
                pending.append((dgj, duj))
            if n > 0:
                dgp, dup = pending[n - 1]
                dh = dh + _dot(dgp, wg[chunks[n - 1], :]) + _dot(dup, wu[chunks[n - 1], :])
        pgv = pg_ref[...].astype(F32)
        plev = ple_ref[...].astype(F32)
        dple_ref[...] = (dz * pgv).astype(BF16)
        dsp = (dz * plev * pgv * (1.0 - pgv)).astype(BF16)
        dsp_ref[...] = dsp
        dh = dh + _dot_nt(dsp, wpg[...])
        zh = zh_ref[...]
        dg1_ref[...] += jnp.sum(dh * zh, axis=0, keepdims=True)
        db1_ref[...] += jnp.sum(dh, axis=0, keepdims=True)
        dzh = dh * g1_ref[...]
        m1 = jnp.mean(dzh, axis=1, keepdims=True)
        m2 = jnp.mean(dzh * zh, axis=1, keepdims=True)
        dz1 = r_ref[...] * (dzh - m1 - zh * m2)
        dz1_ref[...] = dz1
        dyc = _dot_nt(dz1.astype(BF16), wo[...])
        dyr_ref[...] = dyc[:, 0:RET_WIDTH].astype(BF16)
        dya_ref[...] = dyc[:, RET_WIDTH:].astype(BF16)

    row = lambda w: pl.BlockSpec((tm, w), lambda i: (i, 0))
    const = lambda s: pl.BlockSpec(s, lambda i: (0, 0))
    hbm = pl.BlockSpec(memory_space=pl.ANY)
    hid_shape = jax.ShapeDtypeStruct((t, FFN), BF16)
    return pl.pallas_call(
        body, name="ffn_bwd", grid=(t // tm,),
        in_specs=[row(D_MODEL), row(FFN), row(FFN), row(D_MODEL), row(D_MODEL), row(D_MODEL), row(1),
                  const((1, D_MODEL)), hbm, hbm, hbm, hbm, hbm],
        out_specs=[row(FFN), row(FFN), row(D_MODEL), row(D_MODEL), row(D_MODEL), row(RET_WIDTH), row(ATTN_WIDTH),
                   const((1, D_MODEL)), const((1, D_MODEL))],
        out_shape=[hid_shape, hid_shape, jax.ShapeDtypeStruct((t, D_MODEL), BF16),
                   jax.ShapeDtypeStruct((t, D_MODEL), BF16), jax.ShapeDtypeStruct((t, D_MODEL), F32),
                   jax.ShapeDtypeStruct((t, RET_WIDTH), BF16), jax.ShapeDtypeStruct((t, ATTN_WIDTH), BF16),
                   jax.ShapeDtypeStruct((1, D_MODEL), F32), jax.ShapeDtypeStruct((1, D_MODEL), F32)],
        scratch_shapes=[pltpu.VMEM((FFN, D_MODEL), BF16), pltpu.VMEM((FFN, D_MODEL), BF16),
                        pltpu.VMEM((FFN, D_MODEL), BF16),
                        pltpu.VMEM(wpg.shape, BF16), pltpu.VMEM(w_out.shape, BF16),
                        pltpu.SemaphoreType.DMA((3 * N_SHARD + 2,))],
        compiler_params=_params("arbitrary", vmem=VMEM_LIMIT),
    )(dz2, gs, us, pg, ple, zh1, r1, g1, wg_t, wu_t, wd_all, wpg, w_out)


def _wgrad_misc(y_ret, y_att, dz1, hb, dsp, p2d, dple, rider=None):
    t = dz1.shape[0]
    tk = min(t, 512)

    def body(yr_ref, ya_ref, dz_ref, hb_ref, dsp_ref, p_ref, dple_ref, wo_ref, wpg_ref, wpe_ref):
        @pl.when(pl.program_id(0) == 0)
        def _():
            wo_ref[...] = jnp.zeros_like(wo_ref)
            wpg_ref[...] = jnp.zeros_like(wpg_ref)
            wpe_ref[...] = jnp.zeros_like(wpe_ref)

        dzb = dz_ref[...].astype(BF16)
        wo_ref[0:RET_WIDTH, :] += _dot_tn(yr_ref[...], dzb)
        wo_ref[RET_WIDTH:, :] += _dot_tn(ya_ref[...], dzb)
        wpg_ref[...] += _dot_tn(hb_ref[...], dsp_ref[...])
        wpe_ref[...] += _dot_tn(p_ref[...].astype(BF16), dple_ref[...])

    row = lambda w: pl.BlockSpec((tk, w), lambda k: (k, 0))
    const = lambda s: pl.BlockSpec(s, lambda k: (0, 0))
    return _hosted_call(
        body, "wgrad_misc", (t // tk,),
        in_specs=[row(RET_WIDTH), row(ATTN_WIDTH), row(D_MODEL), row(D_MODEL), row(D_MODEL), row(PLE_DIM),
                  row(D_MODEL)],
        out_specs=[const((D_MODEL, D_MODEL)), const((D_MODEL, D_MODEL)), const((PLE_DIM, D_MODEL))],
        out_shape=[jax.ShapeDtypeStruct((D_MODEL, D_MODEL), F32), jax.ShapeDtypeStruct((D_MODEL, D_MODEL), F32),
                   jax.ShapeDtypeStruct((PLE_DIM, D_MODEL), F32)],
        scratch_shapes=[], operands=(y_ret, y_att, dz1, hb, dsp, p2d, dple), rider=rider, semantics=["arbitrary"])


def _wgrad_ffn(name, lhs, rhs, rider=None):
    t = rhs.shape[0]
    tk = min(t, 512)
    nk = t // tk
    nw = len(lhs)

    def body(*refs):
        a_refs, b_ref, o_refs = refs[:nw], refs[nw], refs[nw + 1:]

        @pl.when(pl.program_id(1) == 0)
        def _():
            for o_ref in o_refs:
                o_ref[...] = jnp.zeros_like(o_ref)

        b = b_ref[...]
        for a_ref, o_ref in zip(a_refs, o_refs):
            o_ref[...] += _dot_tn(a_ref[...], b)

    half = FFN // 2
    a_spec = pl.BlockSpec((tk, half), lambda j, k: (k, j))
    b_spec = pl.BlockSpec((tk, D_MODEL), lambda j, k: (k, 0))
    o_spec = pl.BlockSpec((half, D_MODEL), lambda j, k: (j, 0))
    o_shape = jax.ShapeDtypeStruct((FFN, D_MODEL), F32)
    outs = _hosted_call(
        body, name, (2, nk), in_specs=[a_spec] * nw + [b_spec], out_specs=[o_spec] * nw, out_shape=[o_shape] * nw,
        scratch_shapes=[], operands=(*lhs, rhs), rider=rider, semantics=["parallel", "arbitrary"])
    return [o.reshape(N_SHARD, FFN_SHARD, D_MODEL) for o in outs[:nw]] + list(outs[nw:])


KV_ORDER = (0, 128, 64, 192)


def _wgrad_in(pieces, x2d, rider=None):
    t = x2d.shape[0]
    tk = min(t, 512)
    nk = t // tk
    kv0 = CB_AK * LANES

    def body(p0, p1, p2, p3, p4, pkv, x_ref, o_ref):
        @pl.when(pl.program_id(0) == 0)
        def _():
            o_ref[...] = jnp.zeros_like(o_ref)

        xb = x_ref[...].astype(BF16)
        for i, ref in enumerate((p0, p1, p2, p3, p4)):
            o_ref[i * 512:(i + 1) * 512, :] += _dot_tn(ref[...], xb)
        dkv = _dot_tn(pkv[...], xb)
        for i, o in enumerate(KV_ORDER):
            o_ref[kv0 + o:kv0 + o + HEAD_DIM, :] += dkv[i * HEAD_DIM:(i + 1) * HEAD_DIM]

    row = lambda w: pl.BlockSpec((tk, w), lambda k: (k, 0))
    return _hosted_call(
        body, "wgrad_in", (nk,),
        in_specs=[row(512)] * 5 + [row(256), row(D_MODEL)],
        out_specs=[pl.BlockSpec((IN_WIDTH, D_MODEL), lambda k: (0, 0))],
        out_shape=[jax.ShapeDtypeStruct((IN_WIDTH, D_MODEL), F32)],
        scratch_shapes=[], operands=(*pieces, x2d), rider=rider, semantics=["arbitrary"])


def _inproj_bwd(dz1, pieces, w_main, w_kv, rider=None):
    t = dz1.shape[0]
    tm = 512

    def body(dz_ref, p0, p1, p2, p3, p4, pkv, wm_ref, wkv_ref, o_ref):
        acc = ALPHA * dz_ref[...]
        for i, ref in enumerate((p0, p1, p2, p3, p4)):
            acc = acc + _dot(ref[...], wm_ref[i * 512:(i + 1) * 512, :])
        o_ref[...] = acc + _dot(pkv[...], wkv_ref[...])

    row = lambda w: pl.BlockSpec((tm, w), lambda i: (i, 0))
    const = lambda s: pl.BlockSpec(s, lambda i: (0, 0))
    return _hosted_call(
        body, "inproj_bwd", (t // tm,),
        in_specs=[row(D_MODEL)] + [row(512)] * 5 + [row(256), const(w_main.shape), const(w_kv.shape)],
        out_specs=[row(D_MODEL)],
        out_shape=[jax.ShapeDtypeStruct((t, D_MODEL), F32)],
        scratch_shapes=[], operands=(dz1, *pieces, w_main, w_kv), rider=rider)


def _coords():
    return lax.axis_index("x"), lax.axis_index("y"), lax.axis_index("c")


def _chip_of(x, y, rel):
    return (1 - x if rel & 2 else x), (1 - y if rel & 1 else y)


def _all_gather_weights(shards):
    first = _gather_near_rider(shards)
    later = [f(first.out_shapes, chained=True) for f in (_gather_relay_rider, _gather_pass_rider)]
    return _run_riders("gather_weights", shards, first.out_shapes, [first] + later)


def _run_riders(name, ins, out_shapes, riders):
    n_in, n_out = len(ins), len(out_shapes)

    def body(*refs):
        in_refs, out_refs = refs[:n_in], refs[n_in:n_in + n_out]
        k = n_in + n_out
        for r in riders:
            sems = refs[k:k + len(r.sems)]
            k += len(r.sems)
            r.start(in_refs, out_refs, sems)
            r.finish(in_refs, out_refs, sems)

    hbm = pl.BlockSpec(memory_space=pl.ANY)
    return pl.pallas_call(
        body, name=name, in_specs=[hbm] * n_in, out_specs=[hbm] * n_out, out_shape=list(out_shapes),
        scratch_shapes=[s for r in riders for s in r.sems],
    )(*ins)


def _gather_half(outs, w, chip, cc):
    h = outs[w].shape[1] // 2
    return outs[w].at[chip, pl.ds(cc * h, h), :]


UP_SPLIT = 128
NEAR = (1, 2)


def _gather_near_rider(shards):
    nw = len(shards)

    def copies(ins, outs, sems, arrivals):
        send, recv, lsend, lrecv = sems
        x, y, c = _coords()
        me = 2 * x + y
        own = [pltpu.make_async_remote_copy(
            src_ref=ins[w], dst_ref=outs[w].at[me], send_sem=lsend.at[w], recv_sem=lrecv.at[w],
            device_id=(x, y, 1 - c), device_id_type=MESH) for w in range(nw)]
        out, arrive = [], []
        for rel in NEAR:
            kx, ky = _chip_of(x, y, rel)
            for w in range(nw):
                h = shards[w].shape[0] // 2
                sem = dict(send_sem=send.at[w * 2 + rel - 1], recv_sem=recv.at[w * 2 + rel - 1],
                           device_id=(kx, ky, c), device_id_type=MESH)
                out.append(pltpu.make_async_remote_copy(
                    src_ref=ins[w].at[pl.ds(c * h, h), :], dst_ref=_gather_half(outs, w, me, c), **sem))
                if arrivals:
                    theirs = _gather_half(outs, w, 2 * kx + ky, c)
                    arrive.append(pltpu.make_async_remote_copy(src_ref=theirs, dst_ref=theirs, **sem))
        return own, out, arrive

    def start(ins, outs, sems):
        own, out, _ = copies(ins, outs, sems, arrivals=False)
        for cp in own + out:
            cp.start()

    def finish(ins, outs, sems):
        own, out, arrive = copies(ins, outs, sems, arrivals=True)
        for cp in arrive:
            cp.wait_recv()
        for cp in out:
            cp.wait_send()
        for cp in own:
            cp.wait()

    dma = pltpu.SemaphoreType.DMA
    return _Rider(shards, [jax.ShapeDtypeStruct((N_SHARD,) + s.shape, s.dtype) for s in shards],
                  [dma((2 * nw,)), dma((2 * nw,)), dma((nw,)), dma((nw,))], start, finish)


def _gather_relay_rider(gathered, chained=False):
    nw = len(gathered)

    def quarter(outs, w, chip, c, p):
        q = outs[w].shape[1] // 4
        return outs[w].at[chip, pl.ds(c * 2 * q + p * q, q), :]

    def copies(outs, sems):
        send, recv = sems
        x, y, c = _coords()
        (yx, yy), (xx, xy), (dx, dy) = (_chip_of(x, y, rel) for rel in (1, 2, 3))
        out, arrive = [], []
        for w in range(nw):
            for p, (src_chip, dst) in enumerate(((2 * xx + xy, (yx, yy)), (2 * yx + yy, (xx, xy)))):
                rows = quarter(outs, w, src_chip, c, p)
                sem = dict(send_sem=send.at[w * 2 + p], recv_sem=recv.at[w * 2 + p], device_id_type=MESH)
                out.append(pltpu.make_async_remote_copy(src_ref=rows, dst_ref=rows, device_id=(*dst, c), **sem))
                mine = quarter(outs, w, 2 * dx + dy, c, p)
                arrive.append(pltpu.make_async_remote_copy(src_ref=mine, dst_ref=mine, device_id=(*dst, c), **sem))
        return out, arrive

    def start(ins, outs, sems):
        for cp in copies(outs, sems)[0]:
            cp.start()

    def finish(ins, outs, sems):
        out, arrive = copies(outs, sems)
        for cp in arrive:
            cp.wait_recv()
        for cp in out:
            cp.wait_send()

    dma = pltpu.SemaphoreType.DMA
    shapes = [jax.ShapeDtypeStruct(g.shape, g.dtype) for g in gathered]
    if chained:
        return _Rider([], [], [dma((2 * nw,)), dma((2 * nw,))], start, finish)
    return _Rider(gathered, shapes, [dma((2 * nw,)), dma((2 * nw,))], start, finish,
                  aliases={w: w for w in range(nw)})


def _gather_pass_rider(gathered, chained=False):
    nw = len(gathered)

    def copies(outs, sems, cc):
        send, recv = sems
        x, y, c = _coords()
        res = []
        for rel in (1, 2, 3):
            kx, ky = _chip_of(x, y, rel)
            for w in range(nw):
                rows = _gather_half(outs, w, 2 * kx + ky, cc)
                res.append(pltpu.make_async_remote_copy(
                    src_ref=rows, dst_ref=rows, send_sem=send.at[w * 3 + rel - 1], recv_sem=recv.at[w * 3 + rel - 1],
                    device_id=(x, y, 1 - c), device_id_type=MESH))
        return res

    def start(ins, outs, sems):
        for cp in copies(outs, sems, lax.axis_index("c")):
            cp.start()

    def finish(ins, outs, sems):
        c = lax.axis_index("c")
        for cp in copies(outs, sems, 1 - c):
            cp.wait_recv()
        for cp in copies(outs, sems, c):
            cp.wait_send()

    dma = pltpu.SemaphoreType.DMA
    shapes = [jax.ShapeDtypeStruct(g.shape, g.dtype) for g in gathered]
    if chained:
        return _Rider([], [], [dma((3 * nw,)), dma((3 * nw,))], start, finish)
    return _Rider(gathered, shapes, [dma((3 * nw,)), dma((3 * nw,))], start, finish,
                  aliases={w: w for w in range(nw)})


def _exchange_halves_rider(parts):
    nw = len(parts)

    def copies(ins, outs, sems):
        send, recv = sems
        x, y, c = _coords()
        res = []
        for w in range(nw):
            h = parts[w].shape[1] // 2
            res.append(pltpu.make_async_remote_copy(
                src_ref=ins[w].at[:, pl.ds((1 - c) * h, h), :], dst_ref=outs[w],
                send_sem=send.at[w], recv_sem=recv.at[w], device_id=(x, y, 1 - c), device_id_type=MESH))
        return res

    def start(ins, outs, sems):
        for cp in copies(ins, outs, sems):
            cp.start()

    def finish(ins, outs, sems):
        for cp in copies(ins, outs, sems):
            cp.wait()

    dma = pltpu.SemaphoreType.DMA
    return _Rider(parts, [jax.ShapeDtypeStruct((N_SHARD, p.shape[1] // 2, p.shape[2]), p.dtype) for p in parts],
                  [dma((nw,)), dma((nw,))], start, finish)


def _add_halves(parts, theirs, pos):
    nw = len(parts)
    split = 2

    def body(pos_ref, *refs):
        ins, oth = refs[:nw], refs[nw:2 * nw]
        o32, o16 = refs[2 * nw:3 * nw], refs[3 * nw:]
        sums = [ins[w][...] + oth[w][...] for w in range(nw)]
        for w in range(nw):
            o16[w][...] = sums[w].astype(BF16)

        @pl.when(pl.program_id(1) == pos_ref[0])
        def _():
            for w in range(nw):
                o32[w][...] = sums[w]

    in_specs, oth_specs, o32_specs, shapes32, shapes16 = [], [], [], [], []
    for p in parts:
        hb = p.shape[1] // 2 // split
        blk = (None, hb, p.shape[2])
        in_specs.append(pl.BlockSpec(blk, lambda i, j, pos_ref: (j, pos_ref[1] * split + i, 0)))
        oth_specs.append(pl.BlockSpec(blk, lambda i, j, pos_ref: (j, i, 0)))
        o32_specs.append(pl.BlockSpec((hb, p.shape[2]), lambda i, j, pos_ref: (i, 0)))
        shapes32.append(jax.ShapeDtypeStruct((p.shape[1] // 2, p.shape[2]), F32))
        shapes16.append(jax.ShapeDtypeStruct((N_SHARD, p.shape[1] // 2, p.shape[2]), BF16))
    return pl.pallas_call(
        body, name="add_halves",
        grid_spec=pltpu.PrefetchScalarGridSpec(
            num_scalar_prefetch=1, grid=(split, N_SHARD),
            in_specs=in_specs + oth_specs, out_specs=o32_specs + oth_specs),
        out_shape=shapes32 + shapes16,
        compiler_params=_params("parallel", "arbitrary", vmem=VMEM_LIMIT),
    )(pos, *parts, *theirs)


def _exchange_chips_rider(sums16):
    nw = len(sums16)

    def copies(ins, outs, sems):
        send, recv = sems
        x, y, c = _coords()
        res = []
        for rel in (1, 2, 3):
            kx, ky = _chip_of(x, y, rel)
            for w in range(nw):
                res.append(pltpu.make_async_remote_copy(
                    src_ref=ins[w].at[2 * kx + ky], dst_ref=outs[w].at[rel - 1],
                    send_sem=send.at[w * 3 + rel - 1], recv_sem=recv.at[w * 3 + rel - 1],
                    device_id=(kx, ky, c), device_id_type=MESH))
        return res

    def start(ins, outs, sems):
        for cp in copies(ins, outs, sems):
            cp.start()

    def finish(ins, outs, sems):
        for cp in copies(ins, outs, sems):
            cp.wait()

    dma = pltpu.SemaphoreType.DMA
    return _Rider(sums16, [jax.ShapeDtypeStruct((3,) + s.shape[1:], BF16) for s in sums16],
                  [dma((3 * nw,)), dma((3 * nw,))], start, finish)


def _add_chips(sums32, theirs, pos):
    nw = len(sums32)
    split = 2

    def body(pos_ref, *refs):
        ins, oth, outs = refs[:nw], refs[nw:2 * nw], refs[2 * nw:]
        for w in range(nw):
            acc = ins[w][...]
            for r in range(3):
                acc = acc + oth[w][r].astype(F32)
            outs[w][...] = acc

    in_specs, oth_specs, out_specs, shapes = [], [], [], []
    for s in sums32:
        hb = s.shape[0] // split
        in_specs.append(pl.BlockSpec((hb, s.shape[1]), lambda i, pos_ref: (i, 0)))
        oth_specs.append(pl.BlockSpec((3, hb, s.shape[1]), lambda i, pos_ref: (0, i, 0)))
        out_specs.append(pl.BlockSpec((hb, s.shape[1]), lambda i, pos_ref: (pos_ref[1] * split + i, 0)))
        shapes.append(jax.ShapeDtypeStruct((2 * s.shape[0], s.shape[1]), F32))
    return pl.pallas_call(
        body, name="add_chips",
        grid_spec=pltpu.PrefetchScalarGridSpec(
            num_scalar_prefetch=1, grid=(split,), in_specs=in_specs + oth_specs, out_specs=out_specs),
        out_shape=shapes,
        compiler_params=_params("parallel", vmem=VMEM_LIMIT),
    )(pos, *sums32, *theirs)


def _join_halves(shards):
    nw = len(shards)

    def body(*refs):
        outs = refs[nw:2 * nw]
        send, recv = refs[2 * nw:]
        x, y, c = _coords()

        def copy(w, cc):
            h = shards[w].shape[0] // 2
            rows = outs[w].at[pl.ds(cc * h, h), :]
            return pltpu.make_async_remote_copy(
                src_ref=rows, dst_ref=rows, send_sem=send.at[w], recv_sem=recv.at[w],
                device_id=(x, y, 1 - c), device_id_type=MESH)

        for w in range(nw):
            copy(w, c).start()
        for w in range(nw):
            copy(w, 1 - c).wait_recv()
            copy(w, c).wait_send()

    hbm = pl.BlockSpec(memory_space=pl.ANY)
    return pl.pallas_call(
        body, name="join_halves",
        in_specs=[hbm] * nw, out_specs=[hbm] * nw,
        out_shape=[jax.ShapeDtypeStruct(s.shape, F32) for s in shards],
        input_output_aliases={w: w for w in range(nw)},
        scratch_shapes=[pltpu.SemaphoreType.DMA((nw,)), pltpu.SemaphoreType.DMA((nw,))],
    )(*shards)


def _adamw_math(w, g, m, v):
    m = ADAM_B1 * m + (1.0 - ADAM_B1) * g
    v = ADAM_B2 * v + (1.0 - ADAM_B2) * (g * g)
    m_hat = m / (1.0 - ADAM_B1 ** ADAM_STEP)
    v_hat = v / (1.0 - ADAM_B2 ** ADAM_STEP)
    delta = -ADAM_LR * (m_hat / (jnp.sqrt(v_hat) + ADAM_EPS) + ADAM_WD * w)
    return delta, m, v


def _adamw(ws, gs, ms, vs):
    nw = len(ws)
    split = 8

    def body(*refs):
        w_r, g_r, m_r, v_r = (refs[i * nw:(i + 1) * nw] for i in range(4))
        g_o, d_o, m_o, v_o = (refs[(4 + i) * nw:(5 + i) * nw] for i in range(4))
        for k in range(nw):
            g = g_r[k][...]
            d, m, v = _adamw_math(w_r[k][...], g, m_r[k][...], v_r[k][...])
            g_o[k][...] = g
            d_o[k][...] = d
            m_o[k][...] = m
            v_o[k][...] = v

    specs = [pl.BlockSpec((w.shape[0] // split, w.shape[1]), lambda i: (i, 0)) for w in ws]
    shapes = [jax.ShapeDtypeStruct(w.shape, F32) for w in ws]
    outs = pl.pallas_call(
        body, name="adamw", grid=(split,),
        in_specs=specs * 4, out_specs=specs * 4, out_shape=shapes * 4,
        compiler_params=_params("parallel", vmem=VMEM_LIMIT),
    )(*ws, *gs, *ms, *vs)
    return outs[:nw], outs[nw:2 * nw], outs[2 * nw:3 * nw], outs[3 * nw:]


SMALL_ROWS = 8
SMALL_COLS = D_MODEL
LOSS_COL = RET_WIDTH + 24


def _small_allreduce_adamw(part, w, m, v, rider=None):
    def body(part_ref, w_ref, m_ref, v_ref, g_out, d_out, m_out, v_out, all_ref, send, recv):
        x, y, c = _coords()
        me = 4 * x + 2 * y + c
        all_ref[me] = part_ref[...]
        copies = []
        for rel in range(1, 8):
            px = 1 - x if rel & 4 else x
            py = 1 - y if rel & 2 else y
            pc = 1 - c if rel & 1 else c
            copies.append(pltpu.make_async_remote_copy(
                src_ref=part_ref, dst_ref=all_ref.at[me],
                send_sem=send.at[rel - 1], recv_sem=recv.at[rel - 1], device_id=(px, py, pc), device_id_type=MESH))
        for cp in copies:
            cp.start()
        for cp in copies:
            cp.wait()
        g = all_ref[0]
        for k in range(1, 8):
            g = g + all_ref[k]
        d, mn, vn = _adamw_math(w_ref[...], g, m_ref[...], v_ref[...])
        g_out[...] = g
        d_out[...] = d
        m_out[...] = mn
        v_out[...] = vn

    vm = pl.BlockSpec(memory_space=pltpu.VMEM)
    shape = jax.ShapeDtypeStruct((SMALL_ROWS, SMALL_COLS), F32)
    return _hosted_call(
        body, "small_allreduce_adamw", (1,),
        in_specs=[vm] * 4, out_specs=[vm] * 4, out_shape=[shape] * 4,
        scratch_shapes=[pltpu.VMEM((8, SMALL_ROWS, SMALL_COLS), F32),
                        pltpu.SemaphoreType.DMA((7,)), pltpu.SemaphoreType.DMA((7,))],
        operands=(part, w, m, v), rider=rider, semantics=["arbitrary"])


SMALL_NAMES = ("ret_decay_fwd", "ret_decay_bwd", "attn_sink", "ret_gn_gain",
               "ln1_gain", "ln1_bias", "ln2_gain", "ln2_bias")


LN_NAMES = ("ln1_gain", "ln1_bias", "ln2_gain", "ln2_bias")


def _pack_small(vals, extra=None):
    tail = jnp.zeros((1, 1), F32) if extra is None else extra.reshape(1, 1)
    row4 = jnp.concatenate([vals["ret_gn_gain"], vals["ret_decay_fwd"], vals["ret_decay_bwd"], vals["attn_sink"],
                            tail, jnp.zeros((1, SMALL_COLS - LOSS_COL - 1), F32)], axis=1)
    rows = [vals[n] for n in LN_NAMES] + [row4, jnp.zeros((SMALL_ROWS - 5, SMALL_COLS), F32)]
    return jnp.concatenate(rows, axis=0)


def _unpack_small(packed):
    out = {n: packed[i:i + 1] for i, n in enumerate(LN_NAMES)}
    o = RET_WIDTH
    out.update(ret_gn_gain=packed[4:5, 0:o], ret_decay_fwd=packed[4:5, o:o + 8],
               ret_decay_bwd=packed[4:5, o + 8:o + 16], attn_sink=packed[4:5, o + 16:o + 24])
    return out


def _local_step(x, p, tgt, w_in_t, rest, small, pos=None, small_state=None):
    bsz, s, _ = x.shape
    t = bsz * s
    x2d = x.reshape(t, D_MODEL)
    p2d = p.reshape(t, PLE_DIM)
    tgt2d = tgt.reshape(t, D_MODEL)
    dec_f = small["ret_decay_fwd"].reshape(8)
    dec_b = small["ret_decay_bwd"].reshape(8)
    lg_f = jnp.log1p(-jnp.exp2(dec_f))
    lg_b = jnp.log1p(-jnp.exp2(dec_b))
    per_lane = lambda v: jnp.repeat(v, HEAD_DIM).reshape(4, 1, LANES)
    lgf_l, lgb_l = per_lane(lg_f), per_lane(lg_b)
    sink = small["attn_sink"].reshape(8)
    slopes = 2.0 ** (-(jnp.arange(8, dtype=F32) + 1.0))
    gn_gain = small["ret_gn_gain"]
    g1, b1, g2, b2 = (small[n] for n in ("ln1_gain", "ln1_bias", "ln2_gain", "ln2_bias"))

    dist = pos is not None
    shard = dict(zip(REST_NAMES, rest)) if dist else {}
    near = lambda names: _gather_near_rider([shard[n] for n in names])
    wave1, wave2 = ("w_out", "w_ple_gate", "w_ffn_gate"), ("w_ffn_up", "w_ffn_down", "w_ple_proj")
    n1 = len(wave1)
    u, *o1 = _inproj(x2d, w_in_t, rider=near(wave1) if dist else None)
    u3 = u.reshape(bsz, s, IN_WIDTH)
    y_hat, y_rstd, y_ret, ret_rb, ret_kvf, *o2 = _ret_fwd(u3, lgf_l, lgb_l, gn_gain, rider=_merge_riders(
        [_gather_relay_rider(o1), near(wave2)]) if dist else None)
    y_att, att_p, att_ps, *o3 = _attn_fwd(u3, slopes, sink, rider=_merge_riders(
        [_gather_pass_rider(o2[:n1]), _gather_relay_rider(o2[n1:])]) if dist else None)
    gathered = dict(zip(wave1, o3[:n1]))
    w_out = _assemble_weights({"w_out": gathered["w_out"]})["w_out"] if dist else rest["w_out"]
    zh1, r1, hb, *o4 = _outproj_ln1(y_ret.reshape(t, RET_WIDTH), y_att.reshape(t, ATTN_WIDTH), x2d, w_out, g1, b1,
                                    rider=_gather_pass_rider(o3[n1:]) if dist else None)
    gathered.update(zip(wave2, o4))
    wts = _assemble_weights(gathered) if dist else rest
    dz2, dz2b, gs, us, acts, pg, ple, sq, dg2, db2 = _ffn_fwd(
        zh1, hb, p2d, tgt2d, g1, b1, g2, b2, wts["gate4"], wts["up4"], wts["down4"], wts["ple_proj"], wts["ple_gate"])
    dgs, dus, dsp, dple, dz1, dyr, dya, dg1, db1 = _ffn_bwd(dz2, gs, us, pg, ple, zh1, r1, g1, wts["gate4"],
                                                          wts["up4"], wts["down4"], wts["ple_gate"], wts["w_out"])
    d_w_out, d_ple_gate, d_ple_proj = _wgrad_misc(
        y_ret.reshape(t, RET_WIDTH), y_att.reshape(t, ATTN_WIDTH), dz1, hb, dsp, p2d, dple)
    misc_parts = [d_w_out.reshape(N_SHARD, D_MODEL // N_SHARD, D_MODEL),
                  d_ple_proj.reshape(PLE_DIM, N_SHARD, D_MODEL // N_SHARD).transpose(1, 0, 2),
                  d_ple_gate.reshape(N_SHARD, D_MODEL // N_SHARD, D_MODEL)]
    dyr3, dya3 = dyr.reshape(bsz, s, RET_WIDTH), dya.reshape(bsz, s, ATTN_WIDTH)
    if dist:
        d_down, *th_misc = _wgrad_ffn("wgrad_down", [acts], dz2b, rider=_exchange_halves_rider(misc_parts))
        s_m = _add_halves(misc_parts, th_misc, pos)
        d_gate, d_up, *o4b = _wgrad_ffn("wgrad_gate_up", [dgs, dus], hb, rider=_merge_riders(
            [_exchange_chips_rider(list(s_m[3:])), _exchange_halves_rider([d_down])]))
        chips_misc, th_down = o4b[:3], o4b[3:]
        s_d = _add_halves([d_down], th_down, pos)
        drq, drk, drv, drg, rpart, chips_down, *th_gu = _ret_bwd(
            u3, y_hat, y_rstd, (ret_rb, ret_kvf), dyr3, lgf_l, lgb_l, gn_gain,
            rider=_merge_riders([_exchange_chips_rider([s_d[1]]), _exchange_halves_rider([d_gate, d_up])]))
        s_gu = _add_halves([d_gate, d_up], th_gu, pos)
        up_lo, up_hi = s_gu[3][:, :UP_SPLIT], s_gu[3][:, UP_SPLIT:]
        daq, dakv, spart, chips_gate, chips_up_lo = _attn_bwd(u3, dya3, att_p, att_ps,
                                                              rider=_exchange_chips_rider([s_gu[2], up_lo]))
    else:
        d_down, = _wgrad_ffn("wgrad_down", [acts], dz2b)
        d_gate, d_up = _wgrad_ffn("wgrad_gate_up", [dgs, dus], hb)
        drq, drk, drv, drg, rpart = _ret_bwd(u3, y_hat, y_rstd, (ret_rb, ret_kvf), dyr3, lgf_l, lgb_l, gn_gain)
        daq, dakv, spart = _attn_bwd(u3, dya3, att_p, att_ps)
    pieces = [a.reshape(t, -1) for a in (drq, drk, drv, drg, daq, dakv)]
    kv0 = CB_AK * LANES
    w_kv = jnp.concatenate([w_in_t[kv0 + o:kv0 + o + HEAD_DIM] for o in KV_ORDER], axis=0)
    d_in, *o7 = _wgrad_in(pieces, x2d, rider=_exchange_chips_rider([up_hi]) if dist else None)
    d_in = d_in.reshape(N_SHARD, FFN_SHARD, D_MODEL)

    rsum = rpart
    lane_heads = lambda row: jnp.sum(row.reshape(4, 2, HEAD_DIM), axis=-1).reshape(8)
    dlg_f = lane_heads(rsum[:, 0, :]) + jnp.stack([jnp.sum(rsum[:, 2, :], -1), jnp.sum(rsum[:, 3, :], -1)], 1).reshape(8)
    dlg_b = lane_heads(rsum[:, 1, :]) + jnp.stack([jnp.sum(rsum[:, 4, :], -1), jnp.sum(rsum[:, 5, :], -1)], 1).reshape(8)
    chain = lambda d: -(math.log(2.0) * jnp.exp2(d)) / (1.0 - jnp.exp2(d))
    grads_small = {
        "ret_decay_fwd": (dlg_f * chain(dec_f)).reshape(1, 8),
        "ret_decay_bwd": (dlg_b * chain(dec_b)).reshape(1, 8),
        "attn_sink": jnp.sum(spart, axis=0)[:, 0:4, 0].reshape(1, 8),
        "ret_gn_gain": rsum[:, 6, :].reshape(1, RET_WIDTH),
        "ln1_gain": dg1, "ln1_bias": db1, "ln2_gain": dg2, "ln2_bias": db2,
    }
    if not dist:
        grad_x, = _inproj_bwd(dz1, pieces, w_in_t[:kv0], w_kv)
        grads_rest = [misc_parts[0], d_gate, d_up, d_down] + misc_parts[1:]
        return sq[0, 0], grad_x.reshape(bsz, s, D_MODEL), d_in, grads_rest, grads_small
    *small_out, th_in = _small_allreduce_adamw(_pack_small(grads_small, sq[0, 0]), *small_state,
                                               rider=_exchange_halves_rider([d_in]))
    s_in = _add_halves([d_in], [th_in], pos)
    grad_x, chips_in = _inproj_bwd(dz1, pieces, w_in_t[:kv0], w_kv, rider=_exchange_chips_rider([s_in[1]]))
    sums32 = [s_in[0], s_m[0], s_gu[0], s_gu[1], s_d[0], s_m[1], s_m[2]]
    chips_up = jnp.concatenate([chips_up_lo, o7[0]], axis=1)
    from_chips = [chips_in, chips_misc[0], chips_gate, chips_up, chips_down, chips_misc[1], chips_misc[2]]
    return grad_x.reshape(bsz, s, D_MODEL), sums32, from_chips, small_out


BIG_NAMES = ("w_in", "w_out", "w_ffn_gate", "w_ffn_up", "w_ffn_down", "w_ple_proj", "w_ple_gate")
REST_NAMES = BIG_NAMES[1:]
TRANSPOSED = ("w_in", "w_ffn_gate", "w_ffn_up")
WEIGHT_ORDER = ("w_in", "ret_decay_fwd", "ret_decay_bwd", "ret_gn_gain", "attn_sink", "w_out", "ln1_gain",
                "ln1_bias", "w_ffn_gate", "w_ffn_up", "w_ffn_down", "w_ple_proj", "w_ple_gate", "ln2_gain", "ln2_bias")


def _shard_rows(name, a):
    return jnp.swapaxes(a[0], 0, 1) if name in TRANSPOSED else a[0]


def _unshard_rows(name, a):
    return (jnp.swapaxes(a, 0, 1) if name in TRANSPOSED else a)[None]


def _assemble_weights(gathered):
    cols = lambda a: a.transpose(1, 0, 2).reshape(a.shape[1], N_SHARD * a.shape[2])
    rows = lambda a: a.reshape(N_SHARD * a.shape[1], a.shape[2])
    same = lambda a: a
    layout = {"w_out": ("w_out", rows), "w_ffn_gate": ("gate4", same), "w_ffn_up": ("up4", same),
              "w_ffn_down": ("down4", same), "w_ple_proj": ("ple_proj", cols), "w_ple_gate": ("ple_gate", rows)}
    return {layout[n][0]: layout[n][1](a) for n, a in gathered.items()}


def kernel(x, p, w_in, ret_decay_fwd, ret_decay_bwd, ret_gn_gain, attn_sink, w_out, ln1_gain, ln1_bias, w_ffn_gate, w_ffn_up, w_ffn_down, w_ple_proj, w_ple_gate, ln2_gain, ln2_bias, loss_target, m_w_in, m_ret_decay_fwd, m_ret_decay_bwd, m_ret_gn_gain, m_attn_sink, m_w_out, m_ln1_gain, m_ln1_bias, m_w_ffn_gate, m_w_ffn_up, m_w_ffn_down, m_w_ple_proj, m_w_ple_gate, m_ln2_gain, m_ln2_bias, v_w_in, v_ret_decay_fwd, v_ret_decay_bwd, v_ret_gn_gain, v_attn_sink, v_w_out, v_ln1_gain, v_ln1_bias, v_w_ffn_gate, v_w_ffn_up, v_w_ffn_down, v_w_ple_proj, v_w_ple_gate, v_ln2_gain, v_ln2_bias):
    w = dict(w_in=w_in, ret_decay_fwd=ret_decay_fwd, ret_decay_bwd=ret_decay_bwd, ret_gn_gain=ret_gn_gain,
             attn_sink=attn_sink, w_out=w_out, ln1_gain=ln1_gain, ln1_bias=ln1_bias, w_ffn_gate=w_ffn_gate,
             w_ffn_up=w_ffn_up, w_ffn_down=w_ffn_down, w_ple_proj=w_ple_proj, w_ple_gate=w_ple_gate,
             ln2_gain=ln2_gain, ln2_bias=ln2_bias)
    m = dict(w_in=m_w_in, ret_decay_fwd=m_ret_decay_fwd, ret_decay_bwd=m_ret_decay_bwd, ret_gn_gain=m_ret_gn_gain,
             attn_sink=m_attn_sink, w_out=m_w_out, ln1_gain=m_ln1_gain, ln1_bias=m_ln1_bias, w_ffn_gate=m_w_ffn_gate,
             w_ffn_up=m_w_ffn_up, w_ffn_down=m_w_ffn_down, w_ple_proj=m_w_ple_proj, w_ple_gate=m_w_ple_gate,
             ln2_gain=m_ln2_gain, ln2_bias=m_ln2_bias)
    v = dict(w_in=v_w_in, ret_decay_fwd=v_ret_decay_fwd, ret_decay_bwd=v_ret_decay_bwd, ret_gn_gain=v_ret_gn_gain,
             attn_sink=v_attn_sink, w_out=v_w_out, ln1_gain=v_ln1_gain, ln1_bias=v_ln1_bias, w_ffn_gate=v_w_ffn_gate,
             w_ffn_up=v_w_ffn_up, w_ffn_down=v_w_ffn_down, w_ple_proj=v_w_ple_proj, w_ple_gate=v_w_ple_gate,
             ln2_gain=v_ln2_gain, ln2_bias=v_ln2_bias)
    big = lambda d: [_shard_rows(n, d[n]) for n in BIG_NAMES]
    small = lambda d: {n: d[n] for n in SMALL_NAMES}

    chip = 2 * lax.axis_index("x") + lax.axis_index("y")
    pos = jnp.stack([chip, lax.axis_index("c")]).astype(jnp.int32)

    shards = [a.astype(BF16) for a in big(w)]
    (w_in4,) = _all_gather_weights(shards[:1])
    w_in_t = w_in4.reshape(IN_WIDTH, D_MODEL)
    grad_x, sums32, from_chips, (g_s, d_s, m_s, v_s) = _local_step(
        x, p[0], loss_target, w_in_t, shards[1:], small(w), pos=pos,
        small_state=(_pack_small(small(w)), _pack_small(small(m)), _pack_small(small(v))))
    g_big, d_big, m_big, v_big = _adamw(big(w), _join_halves(_add_chips(sums32, from_chips, pos)), big(m), big(v))
    loss = g_s[4, LOSS_COL] * (0.5 / D_MODEL)

    def tree(bigs, packed):
        out = {n: _unshard_rows(n, a) for n, a in zip(BIG_NAMES, bigs)}
        out.update(_unpack_small(packed))
        return [out[n] for n in WEIGHT_ORDER]

    return (loss, grad_x, *tree(g_big, g_s), *tree(d_big, d_s), *tree(m_big, m_s), *tree(v_big, v_s))
```

```python
import functools
import math

import jax
import jax.numpy as jnp
from jax import lax
from jax.experimental import pallas as pl
from jax.experimental.pallas import tpu as pltpu

F32 = jnp.float32
BF16 = jnp.bfloat16

D_MODEL = 1024
HEAD_DIM = 64
RET_HEADS = 8
ATTN_HEADS = 8
RET_WIDTH = 512
ATTN_WIDTH = 512
KV_WIDTH = 128
IN_WIDTH = 2816
FFN = 2816
N_SHARD = 4
FFN_SHARD = FFN // N_SHARD
PLE_DIM = 256
CHUNK = 128
LANES = 128
ALPHA = 2.0 ** 0.25
LN_EPS = 1e-5
GN_EPS = 1e-5
NEG_INF = -1e30
ADAM_LR = 0.001
ADAM_B1 = 0.9
ADAM_B2 = 0.999
ADAM_EPS = 1e-08
ADAM_WD = 0.01
ADAM_STEP = 10
VMEM_LIMIT = 56 * 1024 * 1024
MESH = pl.DeviceIdType.MESH

CB_RQ, CB_RK, CB_RV, CB_RG, CB_AQ, CB_AK, CB_AV = 0, 4, 8, 12, 16, 20, 21


def _dot(a, b):
    return jnp.dot(a, b, preferred_element_type=F32)


def _dot_nt(a, b):
    return lax.dot_general(a, b, (((1,), (1,)), ((), ())), preferred_element_type=F32)


def _dot_tn(a, b):
    return lax.dot_general(a, b, (((0,), (0,)), ((), ())), preferred_element_type=F32)


def _sigmoid(x):
    return 1.0 / (1.0 + jnp.exp(-x))


def _params(*sem, vmem=None):
    return pltpu.CompilerParams(dimension_semantics=tuple(sem) if sem else None, vmem_limit_bytes=vmem)


class _Rider:
    def __init__(self, ins, out_shapes, sems, start, finish, aliases=None):
        self.ins, self.out_shapes, self.sems = list(ins), list(out_shapes), list(sems)
        self.start, self.finish, self.aliases = start, finish, dict(aliases or {})


def _merge_riders(riders):
    riders = [r for r in riders if r is not None]
    if len(riders) == 1:
        return riders[0]
    bounds, aliases = [], {}
    i0 = o0 = s0 = 0
    for r in riders:
        bounds.append((i0, o0, s0))
        aliases.update({i0 + i: o0 + o for i, o in r.aliases.items()})
        i0, o0, s0 = i0 + len(r.ins), o0 + len(r.out_shapes), s0 + len(r.sems)

    def each(method):
        def run(ins, outs, sems):
            for r, (i, o, s) in zip(riders, bounds):
                getattr(r, method)(ins[i:i + len(r.ins)], outs[o:o + len(r.out_shapes)], sems[s:s + len(r.sems)])
        return run

    return _Rider([a for r in riders for a in r.ins], [a for r in riders for a in r.out_shapes],
                  [a for r in riders for a in r.sems], each("start"), each("finish"), aliases)


def _hosted_call(body, name, grid, in_specs, out_specs, out_shape, scratch_shapes, operands, rider=None,
                 semantics=None):
    n_in, n_out, n_scr = len(in_specs), len(out_specs), len(scratch_shapes)
    if rider is None:
        return pl.pallas_call(
            body, name=name, grid=grid, in_specs=in_specs, out_specs=out_specs, out_shape=out_shape,
            scratch_shapes=scratch_shapes,
            compiler_params=_params(*(semantics or ["parallel"] * len(grid)), vmem=VMEM_LIMIT))(*operands)
    r_in, r_out = len(rider.ins), len(rider.out_shapes)

    def full_body(*refs):
        main_in, rin = refs[:n_in], refs[n_in:n_in + r_in]
        o0 = n_in + r_in
        main_out, rout = refs[o0:o0 + n_out], refs[o0 + n_out:o0 + n_out + r_out]
        s0 = o0 + n_out + r_out
        main_scr, rsem = refs[s0:s0 + n_scr], refs[s0 + n_scr:]
        first = functools.reduce(jnp.logical_and, [pl.program_id(a) == 0 for a in range(len(grid))])
        last = functools.reduce(jnp.logical_and, [pl.program_id(a) == g - 1 for a, g in enumerate(grid)])

        @pl.when(first)
        def _():
            rider.start(rin, rout, rsem)

        body(*main_in, *main_out, *main_scr)

        @pl.when(last)
        def _():
            rider.finish(rin, rout, rsem)

    hbm = pl.BlockSpec(memory_space=pl.ANY)
    return pl.pallas_call(
        full_body, name=name, grid=grid,
        in_specs=list(in_specs) + [hbm] * r_in, out_specs=list(out_specs) + [hbm] * r_out,
        out_shape=list(out_shape) + rider.out_shapes,
        scratch_shapes=list(scratch_shapes) + rider.sems,
        input_output_aliases={n_in + i: n_out + o for i, o in rider.aliases.items()},
        compiler_params=_params(*(["arbitrary"] * len(grid)), vmem=VMEM_LIMIT),
    )(*operands, *rider.ins)


def _loop_grouped(n, body, init, per_trip=2):
    if n % per_trip:
        return lax.fori_loop(0, n, body, init)

    def trip(i, c):
        for j in range(per_trip):
            c = body(per_trip * i + j, c)
        return c

    return lax.fori_loop(0, n // per_trip, trip, init)


def _head_mean(x, m0):
    s0 = jnp.sum(jnp.where(m0, x, 0.0), axis=1, keepdims=True)
    s1 = jnp.sum(jnp.where(m0, 0.0, x), axis=1, keepdims=True)
    return jnp.where(m0, s0, s1) * (1.0 / HEAD_DIM)


def _inproj(x2d, w_in_t, rider=None):
    t = x2d.shape[0]
    tm = 512
    nb = 256

    def body(x_ref, w_ref, o_ref):
        xb = x_ref[...].astype(BF16)
        for n in range(0, IN_WIDTH, nb):
            o_ref[:, n:n + nb] = _dot_nt(xb, w_ref[n:n + nb, :]).astype(BF16)

    return _hosted_call(
        body, "inproj", (t // tm,),
        in_specs=[pl.BlockSpec((tm, D_MODEL), lambda i: (i, 0)),
                  pl.BlockSpec((IN_WIDTH, D_MODEL), lambda i: (0, 0))],
        out_specs=[pl.BlockSpec((tm, IN_WIDTH), lambda i: (i, 0))],
        out_shape=[jax.ShapeDtypeStruct((t, IN_WIDTH), BF16)],
        scratch_shapes=[], operands=(x2d, w_in_t), rider=rider)


def _outproj_ln1(y_ret, y_att, x2d, w_out, gain, bias, rider=None):
    t = x2d.shape[0]
    tm = 512

    def body(yr_ref, ya_ref, x_ref, w_ref, g_ref, b_ref, zh_ref, r_ref, hb_ref):
        mix = _dot(yr_ref[...], w_ref[0:RET_WIDTH, :]) + _dot(ya_ref[...], w_ref[RET_WIDTH:, :])
        z = ALPHA * x_ref[...] + mix
        mu = jnp.mean(z, axis=1, keepdims=True)
        zc = z - mu
        var = jnp.mean(zc * zc, axis=1, keepdims=True)
        r = lax.rsqrt(var + LN_EPS)
        zh = zc * r
        zh_ref[...] = zh
        r_ref[...] = r
        hb_ref[...] = (zh * g_ref[...] + b_ref[...]).astype(BF16)

    row = lambda w: pl.BlockSpec((tm, w), lambda i: (i, 0))
    const = lambda s: pl.BlockSpec(s, lambda i: (0, 0))
    return _hosted_call(
        body, "outproj_ln1", (t // tm,),
        in_specs=[row(RET_WIDTH), row(ATTN_WIDTH), row(D_MODEL), const((D_MODEL, D_MODEL)),
                  const((1, D_MODEL)), const((1, D_MODEL))],
        out_specs=[row(D_MODEL), row(1), row(D_MODEL)],
        out_shape=[jax.ShapeDtypeStruct((t, D_MODEL), F32), jax.ShapeDtypeStruct((t, 1), F32),
                   jax.ShapeDtypeStruct((t, D_MODEL), BF16)],
        scratch_shapes=[], operands=(y_ret, y_att, x2d, w_out, gain, bias), rider=rider)


def _load_resident(step, pairs, sems):
    copies = [pltpu.make_async_copy(src, dst, sems.at[i]) for i, (src, dst) in enumerate(pairs)]

    @pl.when(step == 0)
    def _():
        for cp in copies:
            cp.start()
        for cp in copies:
            cp.wait()


FFN_CHUNK = 256
N_FFN_CHUNK = FFN // FFN_CHUNK


def _resident_quarters(hbm, vmem):
    q = FFN // N_SHARD
    return [(hbm.at[pl.ds(j * q, q), :], vmem.at[pl.ds(j * q, q), :]) for j in range(N_SHARD)]


def _ln2_loss_tail(zh, mixed, tgt, g1, b1, g2, b2):
    z2 = ALPHA * (zh * g1 + b1) + mixed
    mu = jnp.mean(z2, axis=1, keepdims=True)
    zc = z2 - mu
    var = jnp.mean(zc * zc, axis=1, keepdims=True)
    r = lax.rsqrt(var + LN_EPS)
    zh2 = zc * r
    err = zh2 * g2 + b2 - tgt
    dy = err * (1.0 / D_MODEL)
    dzh = dy * g2
    m1 = jnp.mean(dzh, axis=1, keepdims=True)
    m2 = jnp.mean(dzh * zh2, axis=1, keepdims=True)
    dz2 = r * (dzh - m1 - zh2 * m2)
    return dz2, jnp.sum(err * err), jnp.sum(dy * zh2, axis=0, keepdims=True), jnp.sum(dy, axis=0, keepdims=True)


def _ffn_fwd(zh1, hb, p2d, tgt, g1, b1, g2, b2, wg4, wu4, wd4, wpe, wpg):
    t = zh1.shape[0]
    tm = 256
    wg_t, wu_t, wd_all = (w.reshape(FFN, D_MODEL) for w in (wg4, wu4, wd4))

    def body(zh_ref, hb_ref, p_ref, t_ref, g1_ref, b1_ref, g2_ref, b2_ref,
             wg_hbm, wu_hbm, wd_hbm, wpe_hbm, wpg_hbm,
             dz_ref, dzb_ref, gs_ref, us_ref, act_ref, pg_ref, ple_ref, loss_ref, dg2_ref, db2_ref,
             wg, wu, wd, wpe, wpg, wsem):
        step = pl.program_id(0)
        loads = _resident_quarters(wg_hbm, wg) + _resident_quarters(wu_hbm, wu) + _resident_quarters(wd_hbm, wd)
        _load_resident(step, loads + [(wpe_hbm, wpe), (wpg_hbm, wpg)], wsem)

        @pl.when(step == 0)
        def _():
            loss_ref[...] = jnp.zeros_like(loss_ref)
            dg2_ref[...] = jnp.zeros_like(dg2_ref)
            db2_ref[...] = jnp.zeros_like(db2_ref)

        hbv = hb_ref[...]
        ffn = jnp.zeros((tm, D_MODEL), F32)
        acts = []
        chunks = [slice(n * FFN_CHUNK, (n + 1) * FFN_CHUNK) for n in range(N_FFN_CHUNK)]
        for n in range(N_FFN_CHUNK + 1):
            if n < N_FFN_CHUNK:
                gj = _dot_nt(hbv, wg[chunks[n], :])
                uj = _dot_nt(hbv, wu[chunks[n], :])
                gs_ref[:, chunks[n]] = gj.astype(BF16)
                us_ref[:, chunks[n]] = uj.astype(BF16)
                acts.append((gj * _sigmoid(gj) * uj).astype(BF16))
                act_ref[:, chunks[n]] = acts[n]
            if n > 0:
                ffn = ffn + _dot(acts[n - 1], wd[chunks[n - 1], :])
        ple = _dot(p_ref[...].astype(BF16), wpe[...])
        pg = _sigmoid(_dot(hbv, wpg[...]))
        pg_ref[...] = pg.astype(BF16)
        ple_ref[...] = ple.astype(BF16)
        dz2, sq, dg2, db2 = _ln2_loss_tail(zh_ref[...], ffn + pg * ple, t_ref[...], g1_ref[...], b1_ref[...],
                                           g2_ref[...], b2_ref[...])
        dz_ref[...] = dz2
        dzb_ref[...] = dz2.astype(BF16)
        loss_ref[...] += sq
        dg2_ref[...] += dg2
        db2_ref[...] += db2

    row = lambda w: pl.BlockSpec((tm, w), lambda i: (i, 0))
    const = lambda s: pl.BlockSpec(s, lambda i: (0, 0))
    hid_shape = jax.ShapeDtypeStruct((t, FFN), BF16)
    hbm = pl.BlockSpec(memory_space=pl.ANY)
    return pl.pallas_call(
        body, name="ffn_fwd", grid=(t // tm,),
        in_specs=[row(D_MODEL), row(D_MODEL), row(PLE_DIM), row(D_MODEL),
                  const((1, D_MODEL)), const((1, D_MODEL)), const((1, D_MODEL)), const((1, D_MODEL)),
                  hbm, hbm, hbm, hbm, hbm],
        out_specs=[row(D_MODEL), row(D_MODEL), row(FFN), row(FFN), row(FFN), row(D_MODEL), row(D_MODEL),
                   const((8, LANES)), const((1, D_MODEL)), const((1, D_MODEL))],
        out_shape=[jax.ShapeDtypeStruct((t, D_MODEL), F32), jax.ShapeDtypeStruct((t, D_MODEL), BF16),
                   hid_shape, hid_shape, hid_shape,
                   jax.ShapeDtypeStruct((t, D_MODEL), BF16), jax.ShapeDtypeStruct((t, D_MODEL), BF16),
                   jax.ShapeDtypeStruct((8, LANES), F32),
                   jax.ShapeDtypeStruct((1, D_MODEL), F32), jax.ShapeDtypeStruct((1, D_MODEL), F32)],
        scratch_shapes=[pltpu.VMEM((FFN, D_MODEL), BF16), pltpu.VMEM((FFN, D_MODEL), BF16),
                        pltpu.VMEM((FFN, D_MODEL), BF16),
                        pltpu.VMEM(wpe.shape, BF16), pltpu.VMEM(wpg.shape, BF16),
                        pltpu.SemaphoreType.DMA((3 * N_SHARD + 2,))],
        compiler_params=_params("arbitrary", vmem=VMEM_LIMIT),
    )(zh1, hb, p2d, tgt, g1, b1, g2, b2, wg_t, wu_t, wd_all, wpe, wpg)


def _ret_tables(lgf, lgb):
    c = CHUNK
    row = lax.broadcasted_iota(jnp.int32, (c, LANES), 0).astype(F32)
    ii = lax.broadcasted_iota(jnp.int32, (c, c), 0).astype(F32)
    jj = lax.broadcasted_iota(jnp.int32, (c, c), 1).astype(F32)
    diff = ii - jj
    dmats = []
    for h in range(2):
        lf = lgf[:, h * HEAD_DIM:h * HEAD_DIM + 1]
        lb = lgb[:, h * HEAD_DIM:h * HEAD_DIM + 1]
        dmats.append(jnp.where(diff > 0, jnp.exp(lf * jnp.maximum(diff, 0.0)),
                               jnp.where(diff < 0, jnp.exp(lb * jnp.maximum(-diff, 0.0)), 2.0)))
    tab = dict(
        qdec_f=jnp.exp(lgf * (row + 1.0)), kdec_f=jnp.exp(lgf * (c - 1.0 - row)),
        qdec_b=jnp.exp(lgb * (c - row)), kdec_b=jnp.exp(lgb * row),
        cdec_f=jnp.exp(lgf * c), cdec_b=jnp.exp(lgb * c),
        d0=dmats[0], d1=dmats[1], row=row, diff=diff)
    r = lax.broadcasted_iota(jnp.int32, (LANES, LANES), 0) < HEAD_DIM
    cc = lax.broadcasted_iota(jnp.int32, (LANES, LANES), 1) < HEAD_DIM
    tab["bd"] = r == cc
    tab["m0"] = lax.broadcasted_iota(jnp.int32, (c, LANES), 1) < HEAD_DIM
    return tab


def _ret_specs(bsz, s):
    blk = lambda cb: pl.BlockSpec((bsz, s, LANES), lambda p, cb=cb: (0, 0, cb + p))
    lane = pl.BlockSpec((None, 1, LANES), lambda p: (p, 0, 0))
    gain = pl.BlockSpec((1, LANES), lambda p: (0, p))
    pair = pl.BlockSpec((bsz, s, LANES), lambda p: (0, 0, p))
    return blk, lane, gain, pair


def _ret_state_spec(bsz, n_chunk):
    spec = pl.BlockSpec((None, bsz, n_chunk, LANES, LANES), lambda p: (p, 0, 0, 0, 0))
    return spec, jax.ShapeDtypeStruct((4, bsz, n_chunk, LANES, LANES), F32)


def _ret_kv_states(tb, k_ref, v_ref, rb_ref, kvf_ref, n_chunk):
    c = CHUNK
    bsz = k_ref.shape[0]
    bd = tb["bd"]

    def contributions(n, carry):
        sl = pl.ds(pl.multiple_of(n * c, c), c)
        kfb = []
        for b in range(bsz):
            k32 = k_ref[b, sl, :].astype(F32)
            kfb.append(jnp.concatenate([k32 * tb["kdec_f"], k32 * tb["kdec_b"]], axis=1).astype(BF16))
        kvs = [_dot_tn(kfb[b], v_ref[b, sl, :]) for b in range(bsz)]
        for b in range(bsz):
            kvf_ref[b, n] = jnp.where(bd, kvs[b][0:LANES], 0.0)
            rb_ref[b, n] = jnp.where(bd, kvs[b][LANES:], 0.0)
        return carry

    lax.fori_loop(0, n_chunk, contributions, 0, unroll=2)

    def recur(i, rbs):
        n = n_chunk - 1 - i
        new = []
        for b in range(bsz):
            own = rb_ref[b, n]
            rb_ref[b, n] = rbs[b]
            new.append(rbs[b] * tb["cdec_b"] + own)
        return tuple(new)

    lax.fori_loop(0, n_chunk, recur, tuple(jnp.zeros((LANES, LANES), F32) for _ in range(bsz)))


def _split_rows(x, m0):
    return jnp.concatenate([jnp.where(m0, x, 0.0), jnp.where(m0, 0.0, x)], axis=0).astype(BF16)


def _ret_fwd(u3, lgf_l, lgb_l, gn_gain, rider=None):
    bsz, s, _ = u3.shape
    n_chunk = s // CHUNK
    c = CHUNK

    def body(q_ref, k_ref, v_ref, g_ref, lgf_ref, lgb_ref, gain_ref, yh_ref, rstd_ref, o_ref, rb_ref, kvf_ref):
        tb = _ret_tables(lgf_ref[...], lgb_ref[...])
        m0 = tb["m0"]
        gain = gain_ref[...]
        rows = range(bsz)
        _ret_kv_states(tb, k_ref, v_ref, rb_ref, kvf_ref, n_chunk)

        def chunk(n, rfs):
            sl = pl.ds(pl.multiple_of(n * c, c), c)
            qs = [q_ref[b, sl, :].astype(F32) * 0.125 for b in rows]
            s01 = [_dot_nt(_split_rows(qs[b], m0), k_ref[b, sl, :]) for b in rows]
            ys = []
            for b in rows:
                lhs = jnp.concatenate([s01[b][0:c] * tb["d0"], s01[b][c:] * tb["d1"],
                                       qs[b] * tb["qdec_f"], qs[b] * tb["qdec_b"]], axis=1).astype(BF16)
                rhs = jnp.concatenate([_split_rows(v_ref[b, sl, :].astype(F32), m0),
                                       rfs[b].astype(BF16), rb_ref[b, n].astype(BF16)], axis=0)
                ys.append(_dot(lhs, rhs))
            new = []
            for b in rows:
                y = ys[b]
                mu = _head_mean(y, m0)
                yc = y - mu
                rstd = lax.rsqrt(_head_mean(yc * yc, m0) + GN_EPS)
                yh = yc * rstd
                g = g_ref[b, sl, :].astype(F32)
                yh_ref[b, sl, :] = yh
                rstd_ref[b, sl, :] = rstd
                o_ref[b, sl, :] = (yh * gain * (g * _sigmoid(g))).astype(BF16)
                new.append(rfs[b] * tb["cdec_f"] + kvf_ref[b, n])
            return tuple(new)

        _loop_grouped(n_chunk, chunk, tuple(jnp.zeros((LANES, LANES), F32) for _ in rows))

    blk, lane, gain, pair = _ret_specs(bsz, s)
    state, state_shape = _ret_state_spec(bsz, n_chunk)
    return _hosted_call(
        body, "ret_fwd", (4,),
        in_specs=[blk(CB_RQ), blk(CB_RK), blk(CB_RV), blk(CB_RG), lane, lane, gain],
        out_specs=[pair, pair, pair, state, state],
        out_shape=[jax.ShapeDtypeStruct((bsz, s, RET_WIDTH), F32), jax.ShapeDtypeStruct((bsz, s, RET_WIDTH), F32),
                   jax.ShapeDtypeStruct((bsz, s, RET_WIDTH), BF16), state_shape, state_shape],
        scratch_shapes=[],
        operands=(u3, u3, u3, u3, lgf_l, lgb_l, gn_gain), rider=rider)


def _ret_bwd(u3, y_hat, y_rstd, states, d_o, lgf_l, lgb_l, gn_gain, rider=None):
    bsz, s, _ = u3.shape
    n_chunk = s // CHUNK
    c = CHUNK

    def body(q_ref, k_ref, v_ref, g_ref, yh_ref, rstd_ref, do_ref, lgf_ref, lgb_ref, gain_ref, rb_ref, kvf_ref,
             dq_ref, dk_ref, dv_ref, dg_ref, part_ref,
             rf_ref, dirf_ref, dy_ref, dk_acc, dv_acc, pa0, pa1, vec_ref):
        tb = _ret_tables(lgf_ref[...], lgb_ref[...])
        m0, bd, row = tb["m0"], tb["bd"], tb["row"]
        gain = gain_ref[...]
        wf = jnp.maximum(tb["diff"], 0.0)
        wb = jnp.maximum(-tb["diff"], 0.0)
        rows = range(bsz)
        zero_states = tuple(jnp.zeros((LANES, LANES), F32) for _ in rows)
        for ref in (pa0, pa1):
            ref[...] = jnp.zeros_like(ref)
        vec_ref[...] = jnp.zeros_like(vec_ref)

        def sweep_fwd(n, carry):
            rfs, gbs = carry
            sl = pl.ds(pl.multiple_of(n * c, c), c)
            qs, ks, vs, dys, dybs, q01, k01, dy01 = [], [], [], [], [], [], [], []
            dgain = jnp.zeros((1, LANES), F32)
            for b in rows:
                q = q_ref[b, sl, :].astype(F32) * 0.125
                k = k_ref[b, sl, :]
                yh = yh_ref[b, sl, :]
                rstd = rstd_ref[b, sl, :]
                do = do_ref[b, sl, :].astype(F32)
                g = g_ref[b, sl, :].astype(F32)
                sg = _sigmoid(g)
                sil = g * sg
                dyh = do * gain * sil
                dg_ref[b, sl, :] = (do * yh * gain * sg * (1.0 + g * (1.0 - sg))).astype(BF16)
                dgain = dgain + jnp.sum(do * yh * sil, axis=0, keepdims=True)
                dy = rstd * (dyh - _head_mean(dyh, m0) - yh * _head_mean(dyh * yh, m0))
                dyb = dy.astype(BF16)
                dy_ref[b, sl, :] = dyb
                rf_ref[b, n] = rfs[b]
                qs.append(q)
                ks.append(k)
                vs.append(v_ref[b, sl, :])
                dys.append(dy)
                dybs.append(dyb)
                q01.append(_split_rows(q, m0))
                k01.append(_split_rows(k.astype(F32), m0))
                dy01.append(_split_rows(dy, m0))
            s01 = [_dot_nt(q01[b], ks[b]) for b in rows]
            da01 = [_dot_nt(dy01[b], vs[b]) for b in rows]
            rbn = [rb_ref[b, n] for b in rows]
            states = [jnp.concatenate([rfs[b], rbn[b]], axis=0).astype(BF16) for b in rows]
            dqc = [_dot_nt(dybs[b], states[b]) for b in rows]
            gbb = [gbs[b].astype(BF16) for b in rows]
            dkb = [_dot_nt(vs[b], gbb[b]) for b in rows]
            qfb = [jnp.concatenate([qs[b] * tb["qdec_f"], qs[b] * tb["qdec_b"]], axis=1) for b in rows]
            direct = [_dot_tn(qfb[b].astype(BF16), dybs[b]) for b in rows]
            ds_cat, ds_rows, a_rows = [], [], []
            for b in rows:
                a0 = s01[b][0:c] * tb["d0"]
                a1 = s01[b][c:] * tb["d1"]
                pa0[...] += da01[b][0:c] * a0
                pa1[...] += da01[b][c:] * a1
                ds0 = da01[b][0:c] * tb["d0"]
                ds1 = da01[b][c:] * tb["d1"]
                ds_cat.append(jnp.concatenate([ds0, ds1], axis=1).astype(BF16))
                ds_rows.append(jnp.concatenate([ds0, ds1], axis=0).astype(BF16))
                a_rows.append(jnp.concatenate([a0, a1], axis=0).astype(BF16))
            kbd = [ks[b].astype(F32) * tb["kdec_b"] for b in rows]
            dq_in = [_dot(ds_cat[b], k01[b]) for b in rows]
            dk_in = [_dot_tn(ds_rows[b], q01[b]) for b in rows]
            dv_in = [_dot_tn(a_rows[b], dy01[b]) for b in rows]
            dv_gb = [_dot(kbd[b].astype(BF16), gbb[b]) for b in rows]
            new_rf, new_gb = [], []
            dlf = jnp.zeros((1, LANES), F32)
            dlb = jnp.zeros((1, LANES), F32)
            for b in rows:
                dqf, dqb = dqc[b][:, 0:LANES], dqc[b][:, LANES:]
                qf, qb = qfb[b][:, 0:LANES], qfb[b][:, LANES:]
                dq = dq_in[b] + dqf * tb["qdec_f"] + dqb * tb["qdec_b"]
                dq_ref[b, sl, :] = (dq * 0.125).astype(BF16)
                dk_acc[b, sl, :] = dk_in[b] + dkb[b] * tb["kdec_b"]
                dv_acc[b, sl, :] = dv_in[b] + dv_gb[b]
                dlf = dlf + jnp.sum((row + 1.0) * qf * dqf, axis=0, keepdims=True)
                dlb = dlb + jnp.sum((c - row) * qb * dqb + row * kbd[b] * dkb[b], axis=0, keepdims=True)
                dlb = dlb + c * tb["cdec_b"] * jnp.sum(gbs[b] * rbn[b], axis=0, keepdims=True)
                dirf_ref[b, n] = jnp.where(bd, direct[b][0:LANES], 0.0)
                new_gb.append(jnp.where(bd, direct[b][LANES:], 0.0) + tb["cdec_b"] * gbs[b])
                new_rf.append(rfs[b] * tb["cdec_f"] + kvf_ref[b, n])
            vec_ref[0:1, :] += dlf
            vec_ref[1:2, :] += dlb
            vec_ref[6:7, :] += dgain
            return tuple(new_rf), tuple(new_gb)

        _loop_grouped(n_chunk, sweep_fwd, (zero_states, zero_states), per_trip=4)

        def sweep_bwd(i, gfs):
            n = n_chunk - 1 - i
            sl = pl.ds(pl.multiple_of(n * c, c), c)
            gfb = [gfs[b].astype(BF16) for b in rows]
            kfd = [k_ref[b, sl, :].astype(F32) * tb["kdec_f"] for b in rows]
            dkf = [_dot_nt(v_ref[b, sl, :], gfb[b]) for b in rows]
            dvf = [_dot(kfd[b].astype(BF16), gfb[b]) for b in rows]
            new = []
            dlf = jnp.zeros((1, LANES), F32)
            for b in rows:
                dk_ref[b, sl, :] = (dk_acc[b, sl, :] + dkf[b] * tb["kdec_f"]).astype(BF16)
                dv_ref[b, sl, :] = (dv_acc[b, sl, :] + dvf[b]).astype(BF16)
                dlf = dlf + jnp.sum((c - 1.0 - row) * kfd[b] * dkf[b], axis=0, keepdims=True)
                dlf = dlf + c * tb["cdec_f"] * jnp.sum(gfs[b] * rf_ref[b, n], axis=0, keepdims=True)
                new.append(dirf_ref[b, n] + tb["cdec_f"] * gfs[b])
            vec_ref[0:1, :] += dlf
            return tuple(new)

        _loop_grouped(n_chunk, sweep_bwd, zero_states)
        vec_ref[2:3, :] = jnp.sum(pa0[...] * wf, axis=0, keepdims=True)
        vec_ref[3:4, :] = jnp.sum(pa1[...] * wf, axis=0, keepdims=True)
        vec_ref[4:5, :] = jnp.sum(pa0[...] * wb, axis=0, keepdims=True)
        vec_ref[5:6, :] = jnp.sum(pa1[...] * wb, axis=0, keepdims=True)
        part_ref[...] = vec_ref[...]

    blk, lane, gain, pair = _ret_specs(bsz, s)
    out_bf = jax.ShapeDtypeStruct((bsz, s, RET_WIDTH), BF16)
    state = pltpu.VMEM((bsz, n_chunk, LANES, LANES), F32)
    saved = _ret_state_spec(bsz, n_chunk)[0]
    return _hosted_call(
        body, "ret_bwd", (4,),
        in_specs=[blk(CB_RQ), blk(CB_RK), blk(CB_RV), blk(CB_RG), pair, pair, pair, lane, lane, gain, saved, saved],
        out_specs=[pair, pair, pair, pair, pl.BlockSpec((None, 8, LANES), lambda p: (p, 0, 0))],
        out_shape=[out_bf, out_bf, out_bf, out_bf, jax.ShapeDtypeStruct((4, 8, LANES), F32)],
        scratch_shapes=[state, state,
                        pltpu.VMEM((bsz, s, LANES), BF16), pltpu.VMEM((bsz, s, LANES), F32),
                        pltpu.VMEM((bsz, s, LANES), F32),
                        pltpu.VMEM((c, c), F32), pltpu.VMEM((c, c), F32), pltpu.VMEM((8, LANES), F32)],
        operands=(u3, u3, u3, u3, y_hat, y_rstd, d_o, lgf_l, lgb_l, gn_gain, *states), rider=rider)


def _attn_window_tables(n, s):
    qi = lax.broadcasted_iota(jnp.int32, (CHUNK, 3 * CHUNK), 0)
    kj = lax.broadcasted_iota(jnp.int32, (CHUNK, 3 * CHUNK), 1)
    dist = jnp.abs(kj - CHUNK - qi)
    kpos = n * CHUNK - CHUNK + kj
    valid = (dist <= CHUNK) & (kpos >= 0) & (kpos < s)
    return dist.astype(F32), valid


def _dup_kv_head(x, g):
    lane = lax.broadcasted_iota(jnp.int32, x.shape, 1)
    keep = (lane < HEAD_DIM) == (g == 0)
    xf = x.astype(F32)
    return jnp.where(keep, xf, pltpu.roll(xf, HEAD_DIM, 1))


def _attn_specs(s):
    q = pl.BlockSpec((None, s, 2 * LANES), lambda b, g: (b, 0, CB_AQ // 2 + g))
    k = pl.BlockSpec((None, s, LANES), lambda b, g: (b, 0, CB_AK))
    v = pl.BlockSpec((None, s, LANES), lambda b, g: (b, 0, CB_AV))
    grp = pl.BlockSpec((None, s, 2 * LANES), lambda b, g: (b, 0, g))
    smem = pl.BlockSpec(memory_space=pltpu.SMEM)
    return q, k, v, grp, smem


def _fill_padded(dst_ref, val, s):
    dst_ref[0:CHUNK, :] = jnp.zeros((CHUNK, LANES), dst_ref.dtype)
    dst_ref[CHUNK:CHUNK + s, :] = val.astype(dst_ref.dtype)
    dst_ref[CHUNK + s:2 * CHUNK + s, :] = jnp.zeros((CHUNK, LANES), dst_ref.dtype)


def _attn_probs(sc, slope, snk, dist, valid):
    sc = jnp.where(valid, sc - slope * dist, NEG_INF)
    m = jnp.maximum(jnp.max(sc, axis=1, keepdims=True), snk)
    e = jnp.exp(sc - m)
    es = jnp.exp(snk - m)
    inv = 1.0 / (jnp.sum(e, axis=1, keepdims=True) + es)
    return e * inv, es * inv


def _stack_heads(x2, m0):
    parts = []
    for pr in range(2):
        xp = x2[:, pr * LANES:(pr + 1) * LANES]
        parts += [jnp.where(m0, xp, 0.0), jnp.where(m0, 0.0, xp)]
    return jnp.concatenate(parts, axis=0).astype(BF16)


def _unstack_pair(x_all, pr, m0):
    return jnp.where(m0, x_all[(2 * pr) * CHUNK:(2 * pr + 1) * CHUNK], x_all[(2 * pr + 1) * CHUNK:(2 * pr + 2) * CHUNK])


def _attn_saved_specs(bsz, n_blk):
    specs = [pl.BlockSpec((None, None, n_blk, 4 * CHUNK, w), lambda b, g: (b, g, 0, 0, 0)) for w in (3 * CHUNK, 1)]
    shapes = [jax.ShapeDtypeStruct((bsz, 2, n_blk, 4 * CHUNK, 3 * CHUNK), BF16),
              jax.ShapeDtypeStruct((bsz, 2, n_blk, 4 * CHUNK, 1), F32)]
    return specs, shapes


def _attn_fwd(u3, slopes, sink, rider=None):
    bsz, s, _ = u3.shape
    n_blk = s // CHUNK

    def body(slope_ref, sink_ref, q_ref, k_ref, v_ref, o_ref, p_ref, ps_ref, kp_ref, vp_ref):
        g = pl.program_id(1)
        _fill_padded(kp_ref, _dup_kv_head(k_ref[...], g), s)
        _fill_padded(vp_ref, _dup_kv_head(v_ref[...], g), s)
        m0 = lax.broadcasted_iota(jnp.int32, (CHUNK, LANES), 1) < HEAD_DIM

        def blk(n, carry):
            r0 = pl.multiple_of(n * CHUNK, CHUNK)
            kw = kp_ref[pl.ds(r0, 3 * CHUNK), :]
            vw = vp_ref[pl.ds(r0, 3 * CHUNK), :]
            dist, valid = _attn_window_tables(n, s)
            q_all = _stack_heads(q_ref[pl.ds(r0, CHUNK), :].astype(F32) * 0.125, m0)
            sc_all = _dot_nt(q_all, kw)
            probs, sinks = [], []
            for i in range(4):
                p, ps = _attn_probs(sc_all[i * CHUNK:(i + 1) * CHUNK], slope_ref[g * 4 + i], sink_ref[g * 4 + i],
                                    dist, valid)
                probs.append(p.astype(BF16))
                sinks.append(ps)
            p_all = jnp.concatenate(probs, axis=0)
            p_ref[n] = p_all
            ps_ref[n] = jnp.concatenate(sinks, axis=0)
            out_all = _dot(p_all, vw)
            for pr in range(2):
                o_ref[pl.ds(r0, CHUNK), pr * LANES:(pr + 1) * LANES] = _unstack_pair(out_all, pr, m0).astype(BF16)
            return carry

        lax.fori_loop(0, n_blk, blk, 0, unroll=4)

    q, k, v, grp, smem = _attn_specs(s)
    saved_specs, saved_shapes = _attn_saved_specs(bsz, n_blk)
    return _hosted_call(
        body, "attn_fwd", (bsz, 2),
        in_specs=[smem, smem, q, k, v],
        out_specs=[grp] + saved_specs,
        out_shape=[jax.ShapeDtypeStruct((bsz, s, ATTN_WIDTH), BF16)] + saved_shapes,
        scratch_shapes=[pltpu.VMEM((s + 2 * CHUNK, LANES), BF16), pltpu.VMEM((s + 2 * CHUNK, LANES), BF16)],
        operands=(slopes, sink, u3, u3, u3), rider=rider)


def _attn_bwd(u3, d_o, probs, sink_probs, rider=None):
    bsz, s, _ = u3.shape
    n_blk = s // CHUNK

    def body(q_ref, k_ref, v_ref, do_ref, p_ref, ps_ref, dq_ref, dkv_ref, ds_ref,
             kp_ref, vp_ref, dk_acc, dv_acc):
        g = pl.program_id(1)
        _fill_padded(kp_ref, _dup_kv_head(k_ref[...], g), s)
        _fill_padded(vp_ref, _dup_kv_head(v_ref[...], g), s)
        dk_acc[...] = jnp.zeros_like(dk_acc)
        dv_acc[...] = jnp.zeros_like(dv_acc)
        m0 = lax.broadcasted_iota(jnp.int32, (CHUNK, LANES), 1) < HEAD_DIM

        def blk(n, dsink):
            r0 = pl.multiple_of(n * CHUNK, CHUNK)
            win = pl.ds(r0, 3 * CHUNK)
            kw = kp_ref[win, :]
            vw = vp_ref[win, :]
            q_all = _stack_heads(q_ref[pl.ds(r0, CHUNK), :].astype(F32) * 0.125, m0)
            do_all = _stack_heads(do_ref[pl.ds(r0, CHUNK), :].astype(F32), m0)
            p_all = p_ref[n]
            ps_all = ps_ref[n]
            dp_all = _dot_nt(do_all, vw)
            new_dsink, dscs = [], []
            for i in range(4):
                rows = slice(i * CHUNK, (i + 1) * CHUNK)
                p = p_all[rows].astype(F32)
                dp = dp_all[rows]
                delta = jnp.sum(p * dp, axis=1, keepdims=True)
                dscs.append((p * (dp - delta)).astype(BF16))
                dsh = jnp.sum(ps_all[rows] * delta, axis=0, keepdims=True)
                new_dsink.append(dsink[i] - jnp.broadcast_to(dsh, (1, LANES)))
            dsc_all = jnp.concatenate(dscs, axis=0)
            dq_all = _dot(dsc_all, kw)
            dk_acc[win, :] += _dot_tn(dsc_all, q_all)
            dv_acc[win, :] += _dot_tn(p_all, do_all)
            for pr in range(2):
                dq_ref[pl.ds(r0, CHUNK), pr * LANES:(pr + 1) * LANES] = (
                    _unstack_pair(dq_all, pr, m0) * 0.125).astype(BF16)
            return tuple(new_dsink)

        dsink = _loop_grouped(n_blk, blk, tuple(jnp.zeros((1, LANES), F32) for _ in range(4)), per_trip=4)
        dk = dk_acc[CHUNK:CHUNK + s, :]
        dv = dv_acc[CHUNK:CHUNK + s, :]
        lane = lax.broadcasted_iota(jnp.int32, (s, LANES), 1)
        fold = lambda a: a + pltpu.roll(a, HEAD_DIM, 1)
        dkv_ref[...] = jnp.where(lane < HEAD_DIM, fold(dk), fold(dv)).astype(BF16)
        ds_ref[...] = jnp.zeros_like(ds_ref)
        for i in range(4):
            ds_ref[i:i + 1, :] = dsink[i]

    q, k, v, grp, _ = _attn_specs(s)
    return _hosted_call(
        body, "attn_bwd", (bsz, 2),
        in_specs=[q, k, v, grp] + _attn_saved_specs(bsz, n_blk)[0],
        out_specs=[grp, pl.BlockSpec((None, s, LANES), lambda b, g: (b, 0, g)),
                   pl.BlockSpec((None, None, 8, LANES), lambda b, g: (b, g, 0, 0))],
        out_shape=[jax.ShapeDtypeStruct((bsz, s, ATTN_WIDTH), BF16), jax.ShapeDtypeStruct((bsz, s, 2 * LANES), BF16),
                   jax.ShapeDtypeStruct((bsz, 2, 8, LANES), F32)],
        scratch_shapes=[pltpu.VMEM((s + 2 * CHUNK, LANES), BF16), pltpu.VMEM((s + 2 * CHUNK, LANES), BF16),
                        pltpu.VMEM((s + 2 * CHUNK, LANES), F32), pltpu.VMEM((s + 2 * CHUNK, LANES), F32)],
        operands=(u3, u3, u3, d_o, probs, sink_probs), rider=rider)


def _ffn_bwd(dz2, gs, us, pg, ple, zh1, r1, g1, wg4, wu4, wd4, wpg, w_out):
    t = dz2.shape[0]
    tm = 256
    wg_t, wu_t, wd_all = (w.reshape(FFN, D_MODEL) for w in (wg4, wu4, wd4))

    def body(dz_ref, gs_ref, us_ref, pg_ref, ple_ref, zh_ref, r_ref, g1_ref,
             wg_hbm, wu_hbm, wd_hbm, wpg_hbm, wo_hbm,
             dgs_ref, dus_ref, dsp_ref, dple_ref, dz1_ref, dyr_ref, dya_ref, dg1_ref, db1_ref,
             wg, wu, wd, wpg, wo, wsem):
        step = pl.program_id(0)
        loads = _resident_quarters(wd_hbm, wd) + _resident_quarters(wg_hbm, wg) + _resident_quarters(wu_hbm, wu)
        _load_resident(step, loads + [(wpg_hbm, wpg), (wo_hbm, wo)], wsem)

        @pl.when(step == 0)
        def _():
            dg1_ref[...] = jnp.zeros_like(dg1_ref)
            db1_ref[...] = jnp.zeros_like(db1_ref)

        dz = dz_ref[...]
        dzb = dz.astype(BF16)
        dh = ALPHA * dz
        pending = []
        chunks = [slice(n * FFN_CHUNK, (n + 1) * FFN_CHUNK) for n in range(N_FFN_CHUNK)]
        for n in range(N_FFN_CHUNK + 1):
            if n < N_FFN_CHUNK:
                da = _dot_nt(dzb, wd[chunks[n], :])
                gj = gs_ref[:, chunks[n]].astype(F32)
                uj = us_ref[:, chunks[n]].astype(F32)
                sg = _sigmoid(gj)
                dgj = (da * uj * sg * (1.0 + gj * (1.0 - sg))).astype(BF16)
                duj = (da * gj * sg).astype(BF16)
                dgs_ref[:, chunks[n]] = dgj
                dus_ref[:, chunks[n]] = duj
                pending.append((dgj, duj))
            if n > 0:
                dgp, dup = pending[n - 1]
                dh = dh + _dot(dgp, wg[chunks[n - 1], :]) + _dot(dup, wu[chunks[n - 1], :])
        pgv = pg_ref[...].astype(F32)
        plev = ple_ref[...].astype(F32)
        dple_ref[...] = (dz * pgv).astype(BF16)
        dsp = (dz * plev * pgv * (1.0 - pgv)).astype(BF16)
        dsp_ref[...] = dsp
        dh = dh + _dot_nt(dsp, wpg[...])
        zh = zh_ref[...]
        dg1_ref[...] += jnp.sum(dh * zh, axis=0, keepdims=True)
        db1_ref[...] += jnp.sum(dh, axis=0, keepdims=True)
        dzh = dh * g1_ref[...]
        m1 = jnp.mean(dzh, axis=1, keepdims=True)
        m2 = jnp.mean(dzh * zh, axis=1, keepdims=True)
        dz1 = r_ref[...] * (dzh - m1 - zh * m2)
        dz1_ref[...] = dz1
        dyc = _dot_nt(dz1.astype(BF16), wo[...])
        dyr_ref[...] = dyc[:, 0:RET_WIDTH].astype(BF16)
        dya_ref[...] = dyc[:, RET_WIDTH:].astype(BF16)

    row = lambda w: pl.BlockSpec((tm, w), lambda i: (i, 0))
    const = lambda s: pl.BlockSpec(s, lambda i: (0, 0))
    hbm = pl.BlockSpec(memory_space=pl.ANY)
    hid_shape = jax.ShapeDtypeStruct((t, FFN), BF16)
    return pl.pallas_call(
        body, name="ffn_bwd", grid=(t // tm,),
        in_specs=[row(D_MODEL), row(FFN), row(FFN), row(D_MODEL), row(D_MODEL), row(D_MODEL), row(1),
                  const((1, D_MODEL)), hbm, hbm, hbm, hbm, hbm],
        out_specs=[row(FFN), row(FFN), row(D_MODEL), row(D_MODEL), row(D_MODEL), row(RET_WIDTH), row(ATTN_WIDTH),
                   const((1, D_MODEL)), const((1, D_MODEL))],
        out_shape=[hid_shape, hid_shape, jax.ShapeDtypeStruct((t, D_MODEL), BF16),
                   jax.ShapeDtypeStruct((t, D_MODEL), BF16), jax.ShapeDtypeStruct((t, D_MODEL), F32),
                   jax.ShapeDtypeStruct((t, RET_WIDTH), BF16), jax.ShapeDtypeStruct((t, ATTN_WIDTH), BF16),
                   jax.ShapeDtypeStruct((1, D_MODEL), F32), jax.ShapeDtypeStruct((1, D_MODEL), F32)],
        scratch_shapes=[pltpu.VMEM((FFN, D_MODEL), BF16), pltpu.VMEM((FFN, D_MODEL), BF16),
                        pltpu.VMEM((FFN, D_MODEL), BF16),
                        pltpu.VMEM(wpg.shape, BF16), pltpu.VMEM(w_out.shape, BF16),
                        pltpu.SemaphoreType.DMA((3 * N_SHARD + 2,))],
        compiler_params=_params("arbitrary", vmem=VMEM_LIMIT),
    )(dz2, gs, us, pg, ple, zh1, r1, g1, wg_t, wu_t, wd_all, wpg, w_out)


def _wgrad_misc(y_ret, y_att, dz1, hb, dsp, p2d, dple, rider=None):
    t = dz1.shape[0]
    tk = min(t, 512)

    def body(yr_ref, ya_ref, dz_ref, hb_ref, dsp_ref, p_ref, dple_ref, wo_ref, wpg_ref, wpe_ref):
        @pl.when(pl.program_id(0) == 0)
        def _():
            wo_ref[...] = jnp.zeros_like(wo_ref)
            wpg_ref[...] = jnp.zeros_like(wpg_ref)
            wpe_ref[...] = jnp.zeros_like(wpe_ref)

        dzb = dz_ref[...].astype(BF16)
        wo_ref[0:RET_WIDTH, :] += _dot_tn(yr_ref[...], dzb)
        wo_ref[RET_WIDTH:, :] += _dot_tn(ya_ref[...], dzb)
        wpg_ref[...] += _dot_tn(hb_ref[...], dsp_ref[...])
        wpe_ref[...] += _dot_tn(p_ref[...].astype(BF16), dple_ref[...])

    row = lambda w: pl.BlockSpec((tk, w), lambda k: (k, 0))
    const = lambda s: pl.BlockSpec(s, lambda k: (0, 0))
    return _hosted_call(
        body, "wgrad_misc", (t // tk,),
        in_specs=[row(RET_WIDTH), row(ATTN_WIDTH), row(D_MODEL), row(D_MODEL), row(D_MODEL), row(PLE_DIM),
                  row(D_MODEL)],
        out_specs=[const((D_MODEL, D_MODEL)), const((D_MODEL, D_MODEL)), const((PLE_DIM, D_MODEL))],
        out_shape=[jax.ShapeDtypeStruct((D_MODEL, D_MODEL), F32), jax.ShapeDtypeStruct((D_MODEL, D_MODEL), F32),
                   jax.ShapeDtypeStruct((PLE_DIM, D_MODEL), F32)],
        scratch_shapes=[], operands=(y_ret, y_att, dz1, hb, dsp, p2d, dple), rider=rider, semantics=["arbitrary"])


def _wgrad_ffn(acts, dgs, dus, hb, dz2b):
    t = dz2b.shape[0]
    tk = min(t, 512)
    nk = t // tk

    def body(act_ref, dg_ref, du_ref, hb_ref, dz_ref, og_ref, ou_ref, od_ref):
        @pl.when(pl.program_id(1) == 0)
        def _():
            og_ref[...] = jnp.zeros_like(og_ref)
            ou_ref[...] = jnp.zeros_like(ou_ref)
            od_ref[...] = jnp.zeros_like(od_ref)

        hbv = hb_ref[...]
        og_ref[...] += _dot_tn(dg_ref[...], hbv)
        ou_ref[...] += _dot_tn(du_ref[...], hbv)
        od_ref[...] += _dot_tn(act_ref[...], dz_ref[...])

    half = FFN // 2
    a_spec = pl.BlockSpec((tk, half), lambda j, k: (k, j))
    b_spec = pl.BlockSpec((tk, D_MODEL), lambda j, k: (k, 0))
    o_spec = pl.BlockSpec((half, D_MODEL), lambda j, k: (j, 0))
    o_shape = jax.ShapeDtypeStruct((FFN, D_MODEL), F32)
    outs = pl.pallas_call(
        body, name="wgrad_ffn", grid=(2, nk),
        in_specs=[a_spec, a_spec, a_spec, b_spec, b_spec],
        out_specs=[o_spec] * 3, out_shape=[o_shape] * 3,
        compiler_params=_params("parallel", "arbitrary", vmem=VMEM_LIMIT),
    )(acts, dgs, dus, hb, dz2b)
    return [o.reshape(N_SHARD, FFN_SHARD, D_MODEL) for o in outs]


KV_ORDER = (0, 128, 64, 192)


def _wgrad_in(pieces, x2d, rider=None):
    t = x2d.shape[0]
    tk = min(t, 512)
    nk = t // tk
    kv0 = CB_AK * LANES

    def body(p0, p1, p2, p3, p4, pkv, x_ref, o_ref):
        @pl.when(pl.program_id(0) == 0)
        def _():
            o_ref[...] = jnp.zeros_like(o_ref)

        xb = x_ref[...].astype(BF16)
        for i, ref in enumerate((p0, p1, p2, p3, p4)):
            o_ref[i * 512:(i + 1) * 512, :] += _dot_tn(ref[...], xb)
        dkv = _dot_tn(pkv[...], xb)
        for i, o in enumerate(KV_ORDER):
            o_ref[kv0 + o:kv0 + o + HEAD_DIM, :] += dkv[i * HEAD_DIM:(i + 1) * HEAD_DIM]

    row = lambda w: pl.BlockSpec((tk, w), lambda k: (k, 0))
    return _hosted_call(
        body, "wgrad_in", (nk,),
        in_specs=[row(512)] * 5 + [row(256), row(D_MODEL)],
        out_specs=[pl.BlockSpec((IN_WIDTH, D_MODEL), lambda k: (0, 0))],
        out_shape=[jax.ShapeDtypeStruct((IN_WIDTH, D_MODEL), F32)],
        scratch_shapes=[], operands=(*pieces, x2d), rider=rider, semantics=["arbitrary"])


def _inproj_bwd(dz1, pieces, w_main, w_kv, rider=None):
    t = dz1.shape[0]
    tm = 512

    def body(dz_ref, p0, p1, p2, p3, p4, pkv, wm_ref, wkv_ref, o_ref):
        acc = ALPHA * dz_ref[...]
        for i, ref in enumerate((p0, p1, p2, p3, p4)):
            acc = acc + _dot(ref[...], wm_ref[i * 512:(i + 1) * 512, :])
        o_ref[...] = acc + _dot(pkv[...], wkv_ref[...])

    row = lambda w: pl.BlockSpec((tm, w), lambda i: (i, 0))
    const = lambda s: pl.BlockSpec(s, lambda i: (0, 0))
    return _hosted_call(
        body, "inproj_bwd", (t // tm,),
        in_specs=[row(D_MODEL)] + [row(512)] * 5 + [row(256), const(w_main.shape), const(w_kv.shape)],
        out_specs=[row(D_MODEL)],
        out_shape=[jax.ShapeDtypeStruct((t, D_MODEL), F32)],
        scratch_shapes=[], operands=(dz1, *pieces, w_main, w_kv), rider=rider)


def _coords():
    return lax.axis_index("x"), lax.axis_index("y"), lax.axis_index("c")


def _chip_of(x, y, rel):
    return (1 - x if rel & 2 else x), (1 - y if rel & 1 else y)


def _all_gather_weights(shards):
    first = _gather_near_rider(shards)
    later = [f(first.out_shapes, chained=True) for f in (_gather_relay_rider, _gather_pass_rider)]
    return _run_riders("gather_weights", shards, first.out_shapes, [first] + later)


def _run_riders(name, ins, out_shapes, riders):
    n_in, n_out = len(ins), len(out_shapes)

    def body(*refs):
        in_refs, out_refs = refs[:n_in], refs[n_in:n_in + n_out]
        k = n_in + n_out
        for r in riders:
            sems = refs[k:k + len(r.sems)]
            k += len(r.sems)
            r.start(in_refs, out_refs, sems)
            r.finish(in_refs, out_refs, sems)

    hbm = pl.BlockSpec(memory_space=pl.ANY)
    return pl.pallas_call(
        body, name=name, in_specs=[hbm] * n_in, out_specs=[hbm] * n_out, out_shape=list(out_shapes),
        scratch_shapes=[s for r in riders for s in r.sems],
    )(*ins)


def _gather_half(outs, w, chip, cc):
    h = outs[w].shape[1] // 2
    return outs[w].at[chip, pl.ds(cc * h, h), :]


NEAR = (1, 2)


def _gather_near_rider(shards, rels=NEAR):
    nw, nr = len(shards), len(rels)

    def copies(ins, outs, sems, arrivals):
        send, recv, lsend, lrecv = sems
        x, y, c = _coords()
        me = 2 * x + y
        own = [pltpu.make_async_remote_copy(
            src_ref=ins[w], dst_ref=outs[w].at[me], send_sem=lsend.at[w], recv_sem=lrecv.at[w],
            device_id=(x, y, 1 - c), device_id_type=MESH) for w in range(nw)]
        out, arrive = [], []
        for i, rel in enumerate(rels):
            kx, ky = _chip_of(x, y, rel)
            for w in range(nw):
                h = shards[w].shape[0] // 2
                sem = dict(send_sem=send.at[w * nr + i], recv_sem=recv.at[w * nr + i],
                           device_id=(kx, ky, c), device_id_type=MESH)
                out.append(pltpu.make_async_remote_copy(
                    src_ref=ins[w].at[pl.ds(c * h, h), :], dst_ref=_gather_half(outs, w, me, c), **sem))
                if arrivals:
                    theirs = _gather_half(outs, w, 2 * kx + ky, c)
                    arrive.append(pltpu.make_async_remote_copy(src_ref=theirs, dst_ref=theirs, **sem))
        return own, out, arrive

    def start(ins, outs, sems):
        own, out, _ = copies(ins, outs, sems, arrivals=False)
        for cp in own + out:
            cp.start()

    def finish(ins, outs, sems):
        own, out, arrive = copies(ins, outs, sems, arrivals=True)
        for cp in arrive:
            cp.wait_recv()
        for cp in out:
            cp.wait_send()
        for cp in own:
            cp.wait()

    dma = pltpu.SemaphoreType.DMA
    return _Rider(shards, [jax.ShapeDtypeStruct((N_SHARD,) + s.shape, s.dtype) for s in shards],
                  [dma((nr * nw,)), dma((nr * nw,)), dma((nw,)), dma((nw,))], start, finish)


def _gather_relay_rider(gathered, chained=False):
    nw = len(gathered)

    def quarter(outs, w, chip, c, p):
        q = outs[w].shape[1] // 4
        return outs[w].at[chip, pl.ds(c * 2 * q + p * q, q), :]

    def copies(outs, sems):
        send, recv = sems
        x, y, c = _coords()
        (yx, yy), (xx, xy), (dx, dy) = (_chip_of(x, y, rel) for rel in (1, 2, 3))
        out, arrive = [], []
        for w in range(nw):
            for p, (src_chip, dst) in enumerate(((2 * xx + xy, (yx, yy)), (2 * yx + yy, (xx, xy)))):
                rows = quarter(outs, w, src_chip, c, p)
                sem = dict(send_sem=send.at[w * 2 + p], recv_sem=recv.at[w * 2 + p], device_id_type=MESH)
                out.append(pltpu.make_async_remote_copy(src_ref=rows, dst_ref=rows, device_id=(*dst, c), **sem))
                mine = quarter(outs, w, 2 * dx + dy, c, p)
                arrive.append(pltpu.make_async_remote_copy(src_ref=mine, dst_ref=mine, device_id=(*dst, c), **sem))
        return out, arrive

    def start(ins, outs, sems):
        for cp in copies(outs, sems)[0]:
            cp.start()

    def finish(ins, outs, sems):
        out, arrive = copies(outs, sems)
        for cp in arrive:
            cp.wait_recv()
        for cp in out:
            cp.wait_send()

    dma = pltpu.SemaphoreType.DMA
    shapes = [jax.ShapeDtypeStruct(g.shape, g.dtype) for g in gathered]
    if chained:
        return _Rider([], [], [dma((2 * nw,)), dma((2 * nw,))], start, finish)
    return _Rider(gathered, shapes, [dma((2 * nw,)), dma((2 * nw,))], start, finish,
                  aliases={w: w for w in range(nw)})


def _gather_pass_rider(gathered, chained=False):
    nw = len(gathered)

    def copies(outs, sems, cc):
        send, recv = sems
        x, y, c = _coords()
        res = []
        for rel in (1, 2, 3):
            kx, ky = _chip_of(x, y, rel)
            for w in range(nw):
                rows = _gather_half(outs, w, 2 * kx + ky, cc)
                res.append(pltpu.make_async_remote_copy(
                    src_ref=rows, dst_ref=rows, send_sem=send.at[w * 3 + rel - 1], recv_sem=recv.at[w * 3 + rel - 1],
                    device_id=(x, y, 1 - c), device_id_type=MESH))
        return res

    def start(ins, outs, sems):
        for cp in copies(outs, sems, lax.axis_index("c")):
            cp.start()

    def finish(ins, outs, sems):
        c = lax.axis_index("c")
        for cp in copies(outs, sems, 1 - c):
            cp.wait_recv()
        for cp in copies(outs, sems, c):
            cp.wait_send()

    dma = pltpu.SemaphoreType.DMA
    shapes = [jax.ShapeDtypeStruct(g.shape, g.dtype) for g in gathered]
    if chained:
        return _Rider([], [], [dma((3 * nw,)), dma((3 * nw,))], start, finish)
    return _Rider(gathered, shapes, [dma((3 * nw,)), dma((3 * nw,))], start, finish,
                  aliases={w: w for w in range(nw)})


def _exchange_halves_rider(parts):
    nw = len(parts)

    def copies(ins, outs, sems):
        send, recv = sems
        x, y, c = _coords()
        res = []
        for w in range(nw):
            h = parts[w].shape[1] // 2
            res.append(pltpu.make_async_remote_copy(
                src_ref=ins[w].at[:, pl.ds((1 - c) * h, h), :], dst_ref=outs[w],
                send_sem=send.at[w], recv_sem=recv.at[w], device_id=(x, y, 1 - c), device_id_type=MESH))
        return res

    def start(ins, outs, sems):
        for cp in copies(ins, outs, sems):
            cp.start()

    def finish(ins, outs, sems):
        for cp in copies(ins, outs, sems):
            cp.wait()

    dma = pltpu.SemaphoreType.DMA
    return _Rider(parts, [jax.ShapeDtypeStruct((N_SHARD, p.shape[1] // 2, p.shape[2]), p.dtype) for p in parts],
                  [dma((nw,)), dma((nw,))], start, finish)


def _add_halves(parts, theirs, pos):
    nw = len(parts)
    split = 2

    def body(pos_ref, *refs):
        ins, oth = refs[:nw], refs[nw:2 * nw]
        o32, o16 = refs[2 * nw:3 * nw], refs[3 * nw:]
        sums = [ins[w][...] + oth[w][...] for w in range(nw)]
        for w in range(nw):
            o16[w][...] = sums[w].astype(BF16)

        @pl.when(pl.program_id(1) == pos_ref[0])
        def _():
            for w in range(nw):
                o32[w][...] = sums[w]

    in_specs, oth_specs, o32_specs, shapes32, shapes16 = [], [], [], [], []
    for p in parts:
        hb = p.shape[1] // 2 // split
        blk = (None, hb, p.shape[2])
        in_specs.append(pl.BlockSpec(blk, lambda i, j, pos_ref: (j, pos_ref[1] * split + i, 0)))
        oth_specs.append(pl.BlockSpec(blk, lambda i, j, pos_ref: (j, i, 0)))
        o32_specs.append(pl.BlockSpec((hb, p.shape[2]), lambda i, j, pos_ref: (i, 0)))
        shapes32.append(jax.ShapeDtypeStruct((p.shape[1] // 2, p.shape[2]), F32))
        shapes16.append(jax.ShapeDtypeStruct((N_SHARD, p.shape[1] // 2, p.shape[2]), BF16))
    return pl.pallas_call(
        body, name="add_halves",
        grid_spec=pltpu.PrefetchScalarGridSpec(
            num_scalar_prefetch=1, grid=(split, N_SHARD),
            in_specs=in_specs + oth_specs, out_specs=o32_specs + oth_specs),
        out_shape=shapes32 + shapes16,
        compiler_params=_params("parallel", "arbitrary", vmem=VMEM_LIMIT),
    )(pos, *parts, *theirs)


def _exchange_chips_rider(sums16):
    nw = len(sums16)

    def copies(ins, outs, sems):
        send, recv = sems
        x, y, c = _coords()
        res = []
        for rel in (1, 2, 3):
            kx, ky = _chip_of(x, y, rel)
            for w in range(nw):
                res.append(pltpu.make_async_remote_copy(
                    src_ref=ins[w].at[2 * kx + ky], dst_ref=outs[w].at[rel - 1],
                    send_sem=send.at[w * 3 + rel - 1], recv_sem=recv.at[w * 3 + rel - 1],
                    device_id=(kx, ky, c), device_id_type=MESH))
        return res

    def start(ins, outs, sems):
        for cp in copies(ins, outs, sems):
            cp.start()

    def finish(ins, outs, sems):
        for cp in copies(ins, outs, sems):
            cp.wait()

    dma = pltpu.SemaphoreType.DMA
    return _Rider(sums16, [jax.ShapeDtypeStruct((3,) + s.shape[1:], BF16) for s in sums16],
                  [dma((3 * nw,)), dma((3 * nw,))], start, finish)


def _add_chips(sums32, theirs, pos):
    nw = len(sums32)
    split = 2

    def body(pos_ref, *refs):
        ins, oth, outs = refs[:nw], refs[nw:2 * nw], refs[2 * nw:]
        for w in range(nw):
            acc = ins[w][...]
            for r in range(3):
                acc = acc + oth[w][r].astype(F32)
            outs[w][...] = acc

    in_specs, oth_specs, out_specs, shapes = [], [], [], []
    for s in sums32:
        hb = s.shape[0] // split
        in_specs.append(pl.BlockSpec((hb, s.shape[1]), lambda i, pos_ref: (i, 0)))
        oth_specs.append(pl.BlockSpec((3, hb, s.shape[1]), lambda i, pos_ref: (0, i, 0)))
        out_specs.append(pl.BlockSpec((hb, s.shape[1]), lambda i, pos_ref: (pos_ref[1] * split + i, 0)))
        shapes.append(jax.ShapeDtypeStruct((2 * s.shape[0], s.shape[1]), F32))
    return pl.pallas_call(
        body, name="add_chips",
        grid_spec=pltpu.PrefetchScalarGridSpec(
            num_scalar_prefetch=1, grid=(split,), in_specs=in_specs + oth_specs, out_specs=out_specs),
        out_shape=shapes,
        compiler_params=_params("parallel", vmem=VMEM_LIMIT),
    )(pos, *sums32, *theirs)


def _join_halves(shards):
    nw = len(shards)

    def body(*refs):
        outs = refs[nw:2 * nw]
        send, recv = refs[2 * nw:]
        x, y, c = _coords()

        def copy(w, cc):
            h = shards[w].shape[0] // 2
            rows = outs[w].at[pl.ds(cc * h, h), :]
            return pltpu.make_async_remote_copy(
                src_ref=rows, dst_ref=rows, send_sem=send.at[w], recv_sem=recv.at[w],
                device_id=(x, y, 1 - c), device_id_type=MESH)

        for w in range(nw):
            copy(w, c).start()
        for w in range(nw):
            copy(w, 1 - c).wait_recv()
            copy(w, c).wait_send()

    hbm = pl.BlockSpec(memory_space=pl.ANY)
    return pl.pallas_call(
        body, name="join_halves",
        in_specs=[hbm] * nw, out_specs=[hbm] * nw,
        out_shape=[jax.ShapeDtypeStruct(s.shape, F32) for s in shards],
        input_output_aliases={w: w for w in range(nw)},
        scratch_shapes=[pltpu.SemaphoreType.DMA((nw,)), pltpu.SemaphoreType.DMA((nw,))],
    )(*shards)


def _adamw_math(w, g, m, v):
    m = ADAM_B1 * m + (1.0 - ADAM_B1) * g
    v = ADAM_B2 * v + (1.0 - ADAM_B2) * (g * g)
    m_hat = m / (1.0 - ADAM_B1 ** ADAM_STEP)
    v_hat = v / (1.0 - ADAM_B2 ** ADAM_STEP)
    delta = -ADAM_LR * (m_hat / (jnp.sqrt(v_hat) + ADAM_EPS) + ADAM_WD * w)
    return delta, m, v


def _adamw(ws, gs, ms, vs):
    nw = len(ws)
    split = 8

    def body(*refs):
        w_r, g_r, m_r, v_r = (refs[i * nw:(i + 1) * nw] for i in range(4))
        g_o, d_o, m_o, v_o = (refs[(4 + i) * nw:(5 + i) * nw] for i in range(4))
        for k in range(nw):
            g = g_r[k][...]
            d, m, v = _adamw_math(w_r[k][...], g, m_r[k][...], v_r[k][...])
            g_o[k][...] = g
            d_o[k][...] = d
            m_o[k][...] = m
            v_o[k][...] = v

    specs = [pl.BlockSpec((w.shape[0] // split, w.shape[1]), lambda i: (i, 0)) for w in ws]
    shapes = [jax.ShapeDtypeStruct(w.shape, F32) for w in ws]
    outs = pl.pallas_call(
        body, name="adamw", grid=(split,),
        in_specs=specs * 4, out_specs=specs * 4, out_shape=shapes * 4,
        compiler_params=_params("parallel", vmem=VMEM_LIMIT),
    )(*ws, *gs, *ms, *vs)
    return outs[:nw], outs[nw:2 * nw], outs[2 * nw:3 * nw], outs[3 * nw:]


SMALL_ROWS = 8
SMALL_COLS = D_MODEL
LOSS_COL = RET_WIDTH + 24


def _small_allreduce_adamw(part, w, m, v, rider=None):
    def body(part_ref, w_ref, m_ref, v_ref, g_out, d_out, m_out, v_out, all_ref, send, recv):
        x, y, c = _coords()
        me = 4 * x + 2 * y + c
        all_ref[me] = part_ref[...]
        copies = []
        for rel in range(1, 8):
            px = 1 - x if rel & 4 else x
            py = 1 - y if rel & 2 else y
            pc = 1 - c if rel & 1 else c
            copies.append(pltpu.make_async_remote_copy(
                src_ref=part_ref, dst_ref=all_ref.at[me],
                send_sem=send.at[rel - 1], recv_sem=recv.at[rel - 1], device_id=(px, py, pc), device_id_type=MESH))
        for cp in copies:
            cp.start()
        for cp in copies:
            cp.wait()
        g = all_ref[0]
        for k in range(1, 8):
            g = g + all_ref[k]
        d, mn, vn = _adamw_math(w_ref[...], g, m_ref[...], v_ref[...])
        g_out[...] = g
        d_out[...] = d
        m_out[...] = mn
        v_out[...] = vn

    vm = pl.BlockSpec(memory_space=pltpu.VMEM)
    shape = jax.ShapeDtypeStruct((SMALL_ROWS, SMALL_COLS), F32)
    return _hosted_call(
        body, "small_allreduce_adamw", (1,),
        in_specs=[vm] * 4, out_specs=[vm] * 4, out_shape=[shape] * 4,
        scratch_shapes=[pltpu.VMEM((8, SMALL_ROWS, SMALL_COLS), F32),
                        pltpu.SemaphoreType.DMA((7,)), pltpu.SemaphoreType.DMA((7,))],
        operands=(part, w, m, v), rider=rider, semantics=["arbitrary"])


SMALL_NAMES = ("ret_decay_fwd", "ret_decay_bwd", "attn_sink", "ret_gn_gain",
               "ln1_gain", "ln1_bias", "ln2_gain", "ln2_bias")


LN_NAMES = ("ln1_gain", "ln1_bias", "ln2_gain", "ln2_bias")


def _pack_small(vals, extra=None):
    tail = jnp.zeros((1, 1), F32) if extra is None else extra.reshape(1, 1)
    row4 = jnp.concatenate([vals["ret_gn_gain"], vals["ret_decay_fwd"], vals["ret_decay_bwd"], vals["attn_sink"],
                            tail, jnp.zeros((1, SMALL_COLS - LOSS_COL - 1), F32)], axis=1)
    rows = [vals[n] for n in LN_NAMES] + [row4, jnp.zeros((SMALL_ROWS - 5, SMALL_COLS), F32)]
    return jnp.concatenate(rows, axis=0)


def _unpack_small(packed):
    out = {n: packed[i:i + 1] for i, n in enumerate(LN_NAMES)}
    o = RET_WIDTH
    out.update(ret_gn_gain=packed[4:5, 0:o], ret_decay_fwd=packed[4:5, o:o + 8],
               ret_decay_bwd=packed[4:5, o + 8:o + 16], attn_sink=packed[4:5, o + 16:o + 24])
    return out


def _local_step(x, p, tgt, w_in_t, rest, small, pos=None, small_state=None):
    bsz, s, _ = x.shape
    t = bsz * s
    x2d = x.reshape(t, D_MODEL)
    p2d = p.reshape(t, PLE_DIM)
    tgt2d = tgt.reshape(t, D_MODEL)
    dec_f = small["ret_decay_fwd"].reshape(8)
    dec_b = small["ret_decay_bwd"].reshape(8)
    lg_f = jnp.log1p(-jnp.exp2(dec_f))
    lg_b = jnp.log1p(-jnp.exp2(dec_b))
    per_lane = lambda v: jnp.repeat(v, HEAD_DIM).reshape(4, 1, LANES)
    lgf_l, lgb_l = per_lane(lg_f), per_lane(lg_b)
    sink = small["attn_sink"].reshape(8)
    slopes = 2.0 ** (-(jnp.arange(8, dtype=F32) + 1.0))
    gn_gain = small["ret_gn_gain"]
    g1, b1, g2, b2 = (small[n] for n in ("ln1_gain", "ln1_bias", "ln2_gain", "ln2_bias"))

    dist = pos is not None
    shard = dict(zip(REST_NAMES, rest)) if dist else {}
    near = lambda names, rels=NEAR: _gather_near_rider([shard[n] for n in names], rels)
    wave1, wave2, wave3 = ("w_out", "w_ple_gate", "w_ffn_gate"), ("w_ffn_up", "w_ple_proj"), ("w_ffn_down",)
    n1 = len(wave1)
    u, *o1 = _inproj(x2d, w_in_t, rider=near(wave1) if dist else None)
    u3 = u.reshape(bsz, s, IN_WIDTH)
    y_hat, y_rstd, y_ret, ret_rb, ret_kvf, *o2 = _ret_fwd(u3, lgf_l, lgb_l, gn_gain, rider=_merge_riders(
        [_gather_relay_rider(o1), near(wave2)]) if dist else None)
    y_att, att_p, att_ps, *o3 = _attn_fwd(u3, slopes, sink, rider=_merge_riders(
        [_gather_pass_rider(o2[:n1]), _gather_relay_rider(o2[n1:]), near(wave3, (1, 2, 3))]) if dist else None)
    gathered = dict(zip(wave1, o3[:n1]))
    w_out = _assemble_weights({"w_out": gathered["w_out"]})["w_out"] if dist else rest["w_out"]
    zh1, r1, hb, *o4 = _outproj_ln1(y_ret.reshape(t, RET_WIDTH), y_att.reshape(t, ATTN_WIDTH), x2d, w_out, g1, b1,
                                    rider=_gather_pass_rider(o3[n1:]) if dist else None)
    gathered.update(zip(wave2 + wave3, o4))
    wts = _assemble_weights(gathered) if dist else rest
    dz2, dz2b, gs, us, acts, pg, ple, sq, dg2, db2 = _ffn_fwd(
        zh1, hb, p2d, tgt2d, g1, b1, g2, b2, wts["gate4"], wts["up4"], wts["down4"], wts["ple_proj"], wts["ple_gate"])
    dgs, dus, dsp, dple, dz1, dyr, dya, dg1, db1 = _ffn_bwd(dz2, gs, us, pg, ple, zh1, r1, g1, wts["gate4"],
                                                          wts["up4"], wts["down4"], wts["ple_gate"], wts["w_out"])
    ffn_parts = list(_wgrad_ffn(acts, dgs, dus, hb, dz2b))
    d_w_out, d_ple_gate, d_ple_proj, *th_ffn = _wgrad_misc(
        y_ret.reshape(t, RET_WIDTH), y_att.reshape(t, ATTN_WIDTH), dz1, hb, dsp, p2d, dple,
        rider=_exchange_halves_rider(ffn_parts[:2]) if dist else None)
    misc_parts = [d_w_out.reshape(N_SHARD, D_MODEL // N_SHARD, D_MODEL),
                  d_ple_proj.reshape(PLE_DIM, N_SHARD, D_MODEL // N_SHARD).transpose(1, 0, 2),
                  d_ple_gate.reshape(N_SHARD, D_MODEL // N_SHARD, D_MODEL)]
    dyr3, dya3 = dyr.reshape(bsz, s, RET_WIDTH), dya.reshape(bsz, s, ATTN_WIDTH)
    if dist:
        s_gu = _add_halves(ffn_parts[:2], th_ffn, pos)
        quarter = FFN_SHARD // 4
        up_lo, up_hi = s_gu[3][:, :quarter], s_gu[3][:, quarter:]
        later_parts = [ffn_parts[2]] + misc_parts
        drq, drk, drv, drg, rpart, *o5 = _ret_bwd(u3, y_hat, y_rstd, (ret_rb, ret_kvf), dyr3, lgf_l, lgb_l, gn_gain,
                                                  rider=_merge_riders(
            [_exchange_chips_rider([s_gu[2], up_lo]), _exchange_halves_rider(later_parts)]))
        s_dm = _add_halves(later_parts, o5[2:], pos)
        daq, dakv, spart, *o6 = _attn_bwd(u3, dya3, att_p, att_ps, rider=_exchange_chips_rider([up_hi, s_dm[4]]))
    else:
        drq, drk, drv, drg, rpart = _ret_bwd(u3, y_hat, y_rstd, (ret_rb, ret_kvf), dyr3, lgf_l, lgb_l, gn_gain)
        daq, dakv, spart = _attn_bwd(u3, dya3, att_p, att_ps)
    pieces = [a.reshape(t, -1) for a in (drq, drk, drv, drg, daq, dakv)]
    kv0 = CB_AK * LANES
    w_kv = jnp.concatenate([w_in_t[kv0 + o:kv0 + o + HEAD_DIM] for o in KV_ORDER], axis=0)
    d_in, *o7 = _wgrad_in(pieces, x2d, rider=_exchange_chips_rider(list(s_dm[5:])) if dist else None)
    d_in = d_in.reshape(N_SHARD, FFN_SHARD, D_MODEL)

    rsum = rpart
    lane_heads = lambda row: jnp.sum(row.reshape(4, 2, HEAD_DIM), axis=-1).reshape(8)
    dlg_f = lane_heads(rsum[:, 0, :]) + jnp.stack([jnp.sum(rsum[:, 2, :], -1), jnp.sum(rsum[:, 3, :], -1)], 1).reshape(8)
    dlg_b = lane_heads(rsum[:, 1, :]) + jnp.stack([jnp.sum(rsum[:, 4, :], -1), jnp.sum(rsum[:, 5, :], -1)], 1).reshape(8)
    chain = lambda d: -(math.log(2.0) * jnp.exp2(d)) / (1.0 - jnp.exp2(d))
    grads_small = {
        "ret_decay_fwd": (dlg_f * chain(dec_f)).reshape(1, 8),
        "ret_decay_bwd": (dlg_b * chain(dec_b)).reshape(1, 8),
        "attn_sink": jnp.sum(spart, axis=0)[:, 0:4, 0].reshape(1, 8),
        "ret_gn_gain": rsum[:, 6, :].reshape(1, RET_WIDTH),
        "ln1_gain": dg1, "ln1_bias": db1, "ln2_gain": dg2, "ln2_bias": db2,
    }
    if not dist:
        grad_x, = _inproj_bwd(dz1, pieces, w_in_t[:kv0], w_kv)
        grads_rest = [misc_parts[0]] + ffn_parts + misc_parts[1:]
        return sq[0, 0], grad_x.reshape(bsz, s, D_MODEL), d_in, grads_rest, grads_small
    *small_out, th_in = _small_allreduce_adamw(_pack_small(grads_small, sq[0, 0]), *small_state,
                                               rider=_exchange_halves_rider([d_in]))
    s_in = _add_halves([d_in], [th_in], pos)
    grad_x, chips_in = _inproj_bwd(dz1, pieces, w_in_t[:kv0], w_kv, rider=_exchange_chips_rider([s_in[1]]))
    sums32 = [s_in[0], s_dm[1], s_gu[0], s_gu[1], s_dm[0], s_dm[2], s_dm[3]]
    chips_up = jnp.concatenate([o5[1], o6[0]], axis=1)
    from_chips = [chips_in, o7[0], o5[0], chips_up, o6[1], o7[1], o7[2]]
    return grad_x.reshape(bsz, s, D_MODEL), sums32, from_chips, small_out


BIG_NAMES = ("w_in", "w_out", "w_ffn_gate", "w_ffn_up", "w_ffn_down", "w_ple_proj", "w_ple_gate")
REST_NAMES = BIG_NAMES[1:]
TRANSPOSED = ("w_in", "w_ffn_gate", "w_ffn_up")
WEIGHT_ORDER = ("w_in", "ret_decay_fwd", "ret_decay_bwd", "ret_gn_gain", "attn_sink", "w_out", "ln1_gain",
                "ln1_bias", "w_ffn_gate", "w_ffn_up", "w_ffn_down", "w_ple_proj", "w_ple_gate", "ln2_gain", "ln2_bias")


def _shard_rows(name, a):
    return jnp.swapaxes(a[0], 0, 1) if name in TRANSPOSED else a[0]


def _unshard_rows(name, a):
    return (jnp.swapaxes(a, 0, 1) if name in TRANSPOSED else a)[None]


def _assemble_weights(gathered):
    cols = lambda a: a.transpose(1, 0, 2).reshape(a.shape[1], N_SHARD * a.shape[2])
    rows = lambda a: a.reshape(N_SHARD * a.shape[1], a.shape[2])
    same = lambda a: a
    layout = {"w_out": ("w_out", rows), "w_ffn_gate": ("gate4", same), "w_ffn_up": ("up4", same),
              "w_ffn_down": ("down4", same), "w_ple_proj": ("ple_proj", cols), "w_ple_gate": ("ple_gate", rows)}
    return {layout[n][0]: layout[n][1](a) for n, a in gathered.items()}


def kernel(x, p, w_in, ret_decay_fwd, ret_decay_bwd, ret_gn_gain, attn_sink, w_out, ln1_gain, ln1_bias, w_ffn_gate, w_ffn_up, w_ffn_down, w_ple_proj, w_ple_gate, ln2_gain, ln2_bias, loss_target, m_w_in, m_ret_decay_fwd, m_ret_decay_bwd, m_ret_gn_gain, m_attn_sink, m_w_out, m_ln1_gain, m_ln1_bias, m_w_ffn_gate, m_w_ffn_up, m_w_ffn_down, m_w_ple_proj, m_w_ple_gate, m_ln2_gain, m_ln2_bias, v_w_in, v_ret_decay_fwd, v_ret_decay_bwd, v_ret_gn_gain, v_attn_sink, v_w_out, v_ln1_gain, v_ln1_bias, v_w_ffn_gate, v_w_ffn_up, v_w_ffn_down, v_w_ple_proj, v_w_ple_gate, v_ln2_gain, v_ln2_bias):
    w = dict(w_in=w_in, ret_decay_fwd=ret_decay_fwd, ret_decay_bwd=ret_decay_bwd, ret_gn_gain=ret_gn_gain,
             attn_sink=attn_sink, w_out=w_out, ln1_gain=ln1_gain, ln1_bias=ln1_bias, w_ffn_gate=w_ffn_gate,
             w_ffn_up=w_ffn_up, w_ffn_down=w_ffn_down, w_ple_proj=w_ple_proj, w_ple_gate=w_ple_gate,
             ln2_gain=ln2_gain, ln2_bias=ln2_bias)
    m = dict(w_in=m_w_in, ret_decay_fwd=m_ret_decay_fwd, ret_decay_bwd=m_ret_decay_bwd, ret_gn_gain=m_ret_gn_gain,
             attn_sink=m_attn_sink, w_out=m_w_out, ln1_gain=m_ln1_gain, ln1_bias=m_ln1_bias, w_ffn_gate=m_w_ffn_gate,
             w_ffn_up=m_w_ffn_up, w_ffn_down=m_w_ffn_down, w_ple_proj=m_w_ple_proj, w_ple_gate=m_w_ple_gate,
             ln2_gain=m_ln2_gain, ln2_bias=m_ln2_bias)
    v = dict(w_in=v_w_in, ret_decay_fwd=v_ret_decay_fwd, ret_decay_bwd=v_ret_decay_bwd, ret_gn_gain=v_ret_gn_gain,
             attn_sink=v_attn_sink, w_out=v_w_out, ln1_gain=v_ln1_gain, ln1_bias=v_ln1_bias, w_ffn_gate=v_w_ffn_gate,
             w_ffn_up=v_w_ffn_up, w_ffn_down=v_w_ffn_down, w_ple_proj=v_w_ple_proj, w_ple_gate=v_w_ple_gate,
             ln2_gain=v_ln2_gain, ln2_bias=v_ln2_bias)
    big = lambda d: [_shard_rows(n, d[n]) for n in BIG_NAMES]
    small = lambda d: {n: d[n] for n in SMALL_NAMES}

    chip = 2 * lax.axis_index("x") + lax.axis_index("y")
    pos = jnp.stack([chip, lax.axis_index("c")]).astype(jnp.int32)

    shards = [a.astype(BF16) for a in big(w)]
    (w_in4,) = _all_gather_weights(shards[:1])
    w_in_t = w_in4.reshape(IN_WIDTH, D_MODEL)
    grad_x, sums32, from_chips, (g_s, d_s, m_s, v_s) = _local_step(
        x, p[0], loss_target, w_in_t, shards[1:], small(w), pos=pos,
        small_state=(_pack_small(small(w)), _pack_small(small(m)), _pack_small(small(v))))
    g_big, d_big, m_big, v_big = _adamw(big(w), _join_halves(_add_chips(sums32, from_chips, pos)), big(m), big(v))
    loss = g_s[4, LOSS_COL] * (0.5 / D_MODEL)

    def tree(bigs, packed):
        out = {n: _unshard_rows(n, a) for n, a in zip(BIG_NAMES, bigs)}
        out.update(_unpack_small(packed))
        return [out[n] for n in WEIGHT_ORDER]

    return (loss, grad_x, *tree(g_big, g_s), *tree(d_big, d_s), *tree(m_big, m_s), *tree(v_big, v_s))
```

```python
import functools
import math

import jax
import jax.numpy as jnp
from jax import lax
from jax.experimental import pallas as pl
from jax.experimental.pallas import tpu as pltpu

F32 = jnp.float32
BF16 = jnp.bfloat16

D_MODEL = 1024
HEAD_DIM = 64
RET_HEADS = 8
ATTN_HEADS = 8
RET_WIDTH = 512
ATTN_WIDTH = 512
KV_WIDTH = 128
IN_WIDTH = 2816
FFN = 2816
N_SHARD = 4
FFN_SHARD = FFN // N_SHARD
PLE_DIM = 256
CHUNK = 128
LANES = 128
ALPHA = 2.0 ** 0.25
LN_EPS = 1e-5
GN_EPS = 1e-5
NEG_INF = -1e30
ADAM_LR = 0.001
ADAM_B1 = 0.9
ADAM_B2 = 0.999
ADAM_EPS = 1e-08
ADAM_WD = 0.01
ADAM_STEP = 10
VMEM_LIMIT = 56 * 1024 * 1024
MESH = pl.DeviceIdType.MESH

CB_RQ, CB_RK, CB_RV, CB_RG, CB_AQ, CB_AK, CB_AV = 0, 4, 8, 12, 16, 20, 21


def _dot(a, b):
    return jnp.dot(a, b, preferred_element_type=F32)


def _dot_nt(a, b):
    return lax.dot_general(a, b, (((1,), (1,)), ((), ())), preferred_element_type=F32)


def _dot_tn(a, b):
    return lax.dot_general(a, b, (((0,), (0,)), ((), ())), preferred_element_type=F32)


def _sigmoid(x):
    return 1.0 / (1.0 + jnp.exp(-x))


def _params(*sem, vmem=None):
    return pltpu.CompilerParams(dimension_semantics=tuple(sem) if sem else None, vmem_limit_bytes=vmem)


class _Rider:
    def __init__(self, ins, out_shapes, sems, start, finish, aliases=None):
        self.ins, self.out_shapes, self.sems = list(ins), list(out_shapes), list(sems)
        self.start, self.finish, self.aliases = start, finish, dict(aliases or {})


def _merge_riders(riders):
    riders = [r for r in riders if r is not None]
    if len(riders) == 1:
        return riders[0]
    bounds, aliases = [], {}
    i0 = o0 = s0 = 0
    for r in riders:
        bounds.append((i0, o0, s0))
        aliases.update({i0 + i: o0 + o for i, o in r.aliases.items()})
        i0, o0, s0 = i0 + len(r.ins), o0 + len(r.out_shapes), s0 + len(r.sems)

    def each(method):
        def run(ins, outs, sems):
            for r, (i, o, s) in zip(riders, bounds):
                getattr(r, method)(ins[i:i + len(r.ins)], outs[o:o + len(r.out_shapes)], sems[s:s + len(r.sems)])
        return run

    return _Rider([a for r in riders for a in r.ins], [a for r in riders for a in r.out_shapes],
                  [a for r in riders for a in r.sems], each("start"), each("finish"), aliases)


def _hosted_call(body, name, grid, in_specs, out_specs, out_shape, scratch_shapes, operands, rider=None,
                 semantics=None):
    n_in, n_out, n_scr = len(in_specs), len(out_specs), len(scratch_shapes)
    if rider is None:
        return pl.pallas_call(
            body, name=name, grid=grid, in_specs=in_specs, out_specs=out_specs, out_shape=out_shape,
            scratch_shapes=scratch_shapes,
            compiler_params=_params(*(semantics or ["parallel"] * len(grid)), vmem=VMEM_LIMIT))(*operands)
    r_in, r_out = len(rider.ins), len(rider.out_shapes)

    def full_body(*refs):
        main_in, rin = refs[:n_in], refs[n_in:n_in + r_in]
        o0 = n_in + r_in
        main_out, rout = refs[o0:o0 + n_out], refs[o0 + n_out:o0 + n_out + r_out]
        s0 = o0 + n_out + r_out
        main_scr, rsem = refs[s0:s0 + n_scr], refs[s0 + n_scr:]
        first = functools.reduce(jnp.logical_and, [pl.program_id(a) == 0 for a in range(len(grid))])
        last = functools.reduce(jnp.logical_and, [pl.program_id(a) == g - 1 for a, g in enumerate(grid)])

        @pl.when(first)
        def _():
            rider.start(rin, rout, rsem)

        body(*main_in, *main_out, *main_scr)

        @pl.when(last)
        def _():
            rider.finish(rin, rout, rsem)

    hbm = pl.BlockSpec(memory_space=pl.ANY)
    return pl.pallas_call(
        full_body, name=name, grid=grid,
        in_specs=list(in_specs) + [hbm] * r_in, out_specs=list(out_specs) + [hbm] * r_out,
        out_shape=list(out_shape) + rider.out_shapes,
        scratch_shapes=list(scratch_shapes) + rider.sems,
        input_output_aliases={n_in + i: n_out + o for i, o in rider.aliases.items()},
        compiler_params=_params(*(["arbitrary"] * len(grid)), vmem=VMEM_LIMIT),
    )(*operands, *rider.ins)


def _loop_grouped(n, body, init, per_trip=2):
    if n % per_trip:
        return lax.fori_loop(0, n, body, init)

    def trip(i, c):
        for j in range(per_trip):
            c = body(per_trip * i + j, c)
        return c

    return lax.fori_loop(0, n // per_trip, trip, init)


def _head_mean(x, m0):
    s0 = jnp.sum(jnp.where(m0, x, 0.0), axis=1, keepdims=True)
    s1 = jnp.sum(jnp.where(m0, 0.0, x), axis=1, keepdims=True)
    return jnp.where(m0, s0, s1) * (1.0 / HEAD_DIM)


def _inproj(x2d, w_in_t, rider=None):
    t = x2d.shape[0]
    tm = 512
    nb = 256

    def body(x_ref, w_ref, o_ref):
        xb = x_ref[...].astype(BF16)
        for n in range(0, IN_WIDTH, nb):
            o_ref[:, n:n + nb] = _dot_nt(xb, w_ref[n:n + nb, :]).astype(BF16)

    return _hosted_call(
        body, "inproj", (t // tm,),
        in_specs=[pl.BlockSpec((tm, D_MODEL), lambda i: (i, 0)),
                  pl.BlockSpec((IN_WIDTH, D_MODEL), lambda i: (0, 0))],
        out_specs=[pl.BlockSpec((tm, IN_WIDTH), lambda i: (i, 0))],
        out_shape=[jax.ShapeDtypeStruct((t, IN_WIDTH), BF16)],
        scratch_shapes=[], operands=(x2d, w_in_t), rider=rider)


def _outproj_ln1(y_ret, y_att, x2d, w_out, gain, bias, rider=None):
    t = x2d.shape[0]
    tm = 512

    def body(yr_ref, ya_ref, x_ref, w_ref, g_ref, b_ref, zh_ref, r_ref, hb_ref):
        mix = _dot(yr_ref[...], w_ref[0:RET_WIDTH, :]) + _dot(ya_ref[...], w_ref[RET_WIDTH:, :])
        z = ALPHA * x_ref[...] + mix
        mu = jnp.mean(z, axis=1, keepdims=True)
        zc = z - mu
        var = jnp.mean(zc * zc, axis=1, keepdims=True)
        r = lax.rsqrt(var + LN_EPS)
        zh = zc * r
        zh_ref[...] = zh
        r_ref[...] = r
        hb_ref[...] = (zh * g_ref[...] + b_ref[...]).astype(BF16)

    row = lambda w: pl.BlockSpec((tm, w), lambda i: (i, 0))
    const = lambda s: pl.BlockSpec(s, lambda i: (0, 0))
    return _hosted_call(
        body, "outproj_ln1", (t // tm,),
        in_specs=[row(RET_WIDTH), row(ATTN_WIDTH), row(D_MODEL), const((D_MODEL, D_MODEL)),
                  const((1, D_MODEL)), const((1, D_MODEL))],
        out_specs=[row(D_MODEL), row(1), row(D_MODEL)],
        out_shape=[jax.ShapeDtypeStruct((t, D_MODEL), F32), jax.ShapeDtypeStruct((t, 1), F32),
                   jax.ShapeDtypeStruct((t, D_MODEL), BF16)],
        scratch_shapes=[], operands=(y_ret, y_att, x2d, w_out, gain, bias), rider=rider)


def _load_resident(step, pairs, sems):
    copies = [pltpu.make_async_copy(src, dst, sems.at[i]) for i, (src, dst) in enumerate(pairs)]

    @pl.when(step == 0)
    def _():
        for cp in copies:
            cp.start()
        for cp in copies:
            cp.wait()


FFN_CHUNK = 256
N_FFN_CHUNK = FFN // FFN_CHUNK


def _resident_quarters(hbm, vmem):
    q = FFN // N_SHARD
    return [(hbm.at[pl.ds(j * q, q), :], vmem.at[pl.ds(j * q, q), :]) for j in range(N_SHARD)]


def _ln2_loss_tail(zh, mixed, tgt, g1, b1, g2, b2):
    z2 = ALPHA * (zh * g1 + b1) + mixed
    mu = jnp.mean(z2, axis=1, keepdims=True)
    zc = z2 - mu
    var = jnp.mean(zc * zc, axis=1, keepdims=True)
    r = lax.rsqrt(var + LN_EPS)
    zh2 = zc * r
    err = zh2 * g2 + b2 - tgt
    dy = err * (1.0 / D_MODEL)
    dzh = dy * g2
    m1 = jnp.mean(dzh, axis=1, keepdims=True)
    m2 = jnp.mean(dzh * zh2, axis=1, keepdims=True)
    dz2 = r * (dzh - m1 - zh2 * m2)
    return dz2, jnp.sum(err * err), jnp.sum(dy * zh2, axis=0, keepdims=True), jnp.sum(dy, axis=0, keepdims=True)


def _ffn_fwd(zh1, hb, p2d, tgt, g1, b1, g2, b2, wg4, wu4, wd4, wpe, wpg):
    t = zh1.shape[0]
    tm = 256
    wg_t, wu_t, wd_all = (w.reshape(FFN, D_MODEL) for w in (wg4, wu4, wd4))

    def body(zh_ref, hb_ref, p_ref, t_ref, g1_ref, b1_ref, g2_ref, b2_ref,
             wg_hbm, wu_hbm, wd_hbm, wpe_hbm, wpg_hbm,
             dz_ref, dzb_ref, gs_ref, us_ref, act_ref, pg_ref, ple_ref, loss_ref, dg2_ref, db2_ref,
             wg, wu, wd, wpe, wpg, wsem):
        step = pl.program_id(0)
        loads = _resident_quarters(wg_hbm, wg) + _resident_quarters(wu_hbm, wu) + _resident_quarters(wd_hbm, wd)
        pc = D_MODEL // N_SHARD
        loads += [(wpe_hbm.at[j], wpe.at[:, pl.ds(j * pc, pc)]) for j in range(N_SHARD)]
        _load_resident(step, loads + [(wpg_hbm, wpg)], wsem)

        @pl.when(step == 0)
        def _():
            loss_ref[...] = jnp.zeros_like(loss_ref)
            dg2_ref[...] = jnp.zeros_like(dg2_ref)
            db2_ref[...] = jnp.zeros_like(db2_ref)

        hbv = hb_ref[...]
        ffn = jnp.zeros((tm, D_MODEL), F32)
        acts = []
        chunks = [slice(n * FFN_CHUNK, (n + 1) * FFN_CHUNK) for n in range(N_FFN_CHUNK)]
        for n in range(N_FFN_CHUNK + 1):
            if n < N_FFN_CHUNK:
                gj = _dot_nt(hbv, wg[chunks[n], :])
                uj = _dot_nt(hbv, wu[chunks[n], :])
                gs_ref[:, chunks[n]] = gj.astype(BF16)
                us_ref[:, chunks[n]] = uj.astype(BF16)
                acts.append((gj * _sigmoid(gj) * uj).astype(BF16))
                act_ref[:, chunks[n]] = acts[n]
            if n > 0:
                ffn = ffn + _dot(acts[n - 1], wd[chunks[n - 1], :])
        ple = _dot(p_ref[...].astype(BF16), wpe[...])
        pg = _sigmoid(_dot(hbv, wpg[...]))
        pg_ref[...] = pg.astype(BF16)
        ple_ref[...] = ple.astype(BF16)
        dz2, sq, dg2, db2 = _ln2_loss_tail(zh_ref[...], ffn + pg * ple, t_ref[...], g1_ref[...], b1_ref[...],
                                           g2_ref[...], b2_ref[...])
        dz_ref[...] = dz2
        dzb_ref[...] = dz2.astype(BF16)
        loss_ref[...] += sq
        dg2_ref[...] += dg2
        db2_ref[...] += db2

    row = lambda w: pl.BlockSpec((tm, w), lambda i: (i, 0))
    const = lambda s: pl.BlockSpec(s, lambda i: (0, 0))
    hid_shape = jax.ShapeDtypeStruct((t, FFN), BF16)
    hbm = pl.BlockSpec(memory_space=pl.ANY)
    return pl.pallas_call(
        body, name="ffn_fwd", grid=(t // tm,),
        in_specs=[row(D_MODEL), row(D_MODEL), row(PLE_DIM), row(D_MODEL),
                  const((1, D_MODEL)), const((1, D_MODEL)), const((1, D_MODEL)), const((1, D_MODEL)),
                  hbm, hbm, hbm, hbm, hbm],
        out_specs=[row(D_MODEL), row(D_MODEL), row(FFN), row(FFN), row(FFN), row(D_MODEL), row(D_MODEL),
                   const((8, LANES)), const((1, D_MODEL)), const((1, D_MODEL))],
        out_shape=[jax.ShapeDtypeStruct((t, D_MODEL), F32), jax.ShapeDtypeStruct((t, D_MODEL), BF16),
                   hid_shape, hid_shape, hid_shape,
                   jax.ShapeDtypeStruct((t, D_MODEL), BF16), jax.ShapeDtypeStruct((t, D_MODEL), BF16),
                   jax.ShapeDtypeStruct((8, LANES), F32),
                   jax.ShapeDtypeStruct((1, D_MODEL), F32), jax.ShapeDtypeStruct((1, D_MODEL), F32)],
        scratch_shapes=[pltpu.VMEM((FFN, D_MODEL), BF16), pltpu.VMEM((FFN, D_MODEL), BF16),
                        pltpu.VMEM((FFN, D_MODEL), BF16),
                        pltpu.VMEM((PLE_DIM, D_MODEL), BF16), pltpu.VMEM(wpg.shape, BF16),
                        pltpu.SemaphoreType.DMA((4 * N_SHARD + 1,))],
        compiler_params=_params("arbitrary", vmem=VMEM_LIMIT),
    )(zh1, hb, p2d, tgt, g1, b1, g2, b2, wg_t, wu_t, wd_all, wpe, wpg)


def _ret_tables(lgf, lgb):
    c = CHUNK
    row = lax.broadcasted_iota(jnp.int32, (c, LANES), 0).astype(F32)
    ii = lax.broadcasted_iota(jnp.int32, (c, c), 0).astype(F32)
    jj = lax.broadcasted_iota(jnp.int32, (c, c), 1).astype(F32)
    diff = ii - jj
    dmats = []
    for h in range(2):
        lf = lgf[:, h * HEAD_DIM:h * HEAD_DIM + 1]
        lb = lgb[:, h * HEAD_DIM:h * HEAD_DIM + 1]
        dmats.append(jnp.where(diff > 0, jnp.exp(lf * jnp.maximum(diff, 0.0)),
                               jnp.where(diff < 0, jnp.exp(lb * jnp.maximum(-diff, 0.0)), 2.0)))
    tab = dict(
        qdec_f=jnp.exp(lgf * (row + 1.0)), kdec_f=jnp.exp(lgf * (c - 1.0 - row)),
        qdec_b=jnp.exp(lgb * (c - row)), kdec_b=jnp.exp(lgb * row),
        cdec_f=jnp.exp(lgf * c), cdec_b=jnp.exp(lgb * c),
        d0=dmats[0], d1=dmats[1], row=row, diff=diff)
    r = lax.broadcasted_iota(jnp.int32, (LANES, LANES), 0) < HEAD_DIM
    cc = lax.broadcasted_iota(jnp.int32, (LANES, LANES), 1) < HEAD_DIM
    tab["bd"] = r == cc
    tab["m0"] = lax.broadcasted_iota(jnp.int32, (c, LANES), 1) < HEAD_DIM
    return tab


def _ret_specs(bsz, s):
    blk = lambda cb: pl.BlockSpec((bsz, s, LANES), lambda p, cb=cb: (0, 0, cb + p))
    lane = pl.BlockSpec((None, 1, LANES), lambda p: (p, 0, 0))
    gain = pl.BlockSpec((1, LANES), lambda p: (0, p))
    pair = pl.BlockSpec((bsz, s, LANES), lambda p: (0, 0, p))
    return blk, lane, gain, pair


def _ret_state_spec(bsz, n_chunk):
    spec = pl.BlockSpec((None, bsz, n_chunk, LANES, LANES), lambda p: (p, 0, 0, 0, 0))
    return spec, jax.ShapeDtypeStruct((4, bsz, n_chunk, LANES, LANES), F32)


def _ret_kv_states(tb, k_ref, v_ref, rb_ref, kvf_ref, n_chunk):
    c = CHUNK
    bsz = k_ref.shape[0]
    bd = tb["bd"]

    def contributions(n, carry):
        sl = pl.ds(pl.multiple_of(n * c, c), c)
        kfb = []
        for b in range(bsz):
            k32 = k_ref[b, sl, :].astype(F32)
            kfb.append(jnp.concatenate([k32 * tb["kdec_f"], k32 * tb["kdec_b"]], axis=1).astype(BF16))
        kvs = [_dot_tn(kfb[b], v_ref[b, sl, :]) for b in range(bsz)]
        for b in range(bsz):
            kvf_ref[b, n] = jnp.where(bd, kvs[b][0:LANES], 0.0)
            rb_ref[b, n] = jnp.where(bd, kvs[b][LANES:], 0.0)
        return carry

    lax.fori_loop(0, n_chunk, contributions, 0, unroll=2)

    def recur(i, rbs):
        n = n_chunk - 1 - i
        new = []
        for b in range(bsz):
            own = rb_ref[b, n]
            rb_ref[b, n] = rbs[b]
            new.append(rbs[b] * tb["cdec_b"] + own)
        return tuple(new)

    lax.fori_loop(0, n_chunk, recur, tuple(jnp.zeros((LANES, LANES), F32) for _ in range(bsz)))


def _split_rows(x, m0):
    return jnp.concatenate([jnp.where(m0, x, 0.0), jnp.where(m0, 0.0, x)], axis=0).astype(BF16)


def _ret_fwd(u3, lgf_l, lgb_l, gn_gain, rider=None):
    bsz, s, _ = u3.shape
    n_chunk = s // CHUNK
    c = CHUNK

    def body(q_ref, k_ref, v_ref, g_ref, lgf_ref, lgb_ref, gain_ref, yh_ref, rstd_ref, o_ref, rb_ref, kvf_ref):
        tb = _ret_tables(lgf_ref[...], lgb_ref[...])
        m0 = tb["m0"]
        gain = gain_ref[...]
        rows = range(bsz)
        _ret_kv_states(tb, k_ref, v_ref, rb_ref, kvf_ref, n_chunk)

        def chunk(n, rfs):
            sl = pl.ds(pl.multiple_of(n * c, c), c)
            qs = [q_ref[b, sl, :].astype(F32) * 0.125 for b in rows]
            s01 = [_dot_nt(_split_rows(qs[b], m0), k_ref[b, sl, :]) for b in rows]
            ys = []
            for b in rows:
                lhs = jnp.concatenate([s01[b][0:c] * tb["d0"], s01[b][c:] * tb["d1"],
                                       qs[b] * tb["qdec_f"], qs[b] * tb["qdec_b"]], axis=1).astype(BF16)
                rhs = jnp.concatenate([_split_rows(v_ref[b, sl, :].astype(F32), m0),
                                       rfs[b].astype(BF16), rb_ref[b, n].astype(BF16)], axis=0)
                ys.append(_dot(lhs, rhs))
            new = []
            for b in rows:
                y = ys[b]
                mu = _head_mean(y, m0)
                yc = y - mu
                rstd = lax.rsqrt(_head_mean(yc * yc, m0) + GN_EPS)
                yh = yc * rstd
                g = g_ref[b, sl, :].astype(F32)
                yh_ref[b, sl, :] = yh
                rstd_ref[b, sl, :] = rstd
                o_ref[b, sl, :] = (yh * gain * (g * _sigmoid(g))).astype(BF16)
                new.append(rfs[b] * tb["cdec_f"] + kvf_ref[b, n])
            return tuple(new)

        _loop_grouped(n_chunk, chunk, tuple(jnp.zeros((LANES, LANES), F32) for _ in rows))

    blk, lane, gain, pair = _ret_specs(bsz, s)
    state, state_shape = _ret_state_spec(bsz, n_chunk)
    return _hosted_call(
        body, "ret_fwd", (4,),
        in_specs=[blk(CB_RQ), blk(CB_RK), blk(CB_RV), blk(CB_RG), lane, lane, gain],
        out_specs=[pair, pair, pair, state, state],
        out_shape=[jax.ShapeDtypeStruct((bsz, s, RET_WIDTH), F32), jax.ShapeDtypeStruct((bsz, s, RET_WIDTH), F32),
                   jax.ShapeDtypeStruct((bsz, s, RET_WIDTH), BF16), state_shape, state_shape],
        scratch_shapes=[],
        operands=(u3, u3, u3, u3, lgf_l, lgb_l, gn_gain), rider=rider)


def _ret_bwd(u3, y_hat, y_rstd, states, d_o, lgf_l, lgb_l, gn_gain, rider=None):
    bsz, s, _ = u3.shape
    n_chunk = s // CHUNK
    c = CHUNK

    def body(q_ref, k_ref, v_ref, g_ref, yh_ref, rstd_ref, do_ref, lgf_ref, lgb_ref, gain_ref, rb_ref, kvf_ref,
             dq_ref, dk_ref, dv_ref, dg_ref, part_ref,
             rf_ref, dirf_ref, dy_ref, dk_acc, dv_acc, pa0, pa1, vec_ref):
        tb = _ret_tables(lgf_ref[...], lgb_ref[...])
        m0, bd, row = tb["m0"], tb["bd"], tb["row"]
        gain = gain_ref[...]
        wf = jnp.maximum(tb["diff"], 0.0)
        wb = jnp.maximum(-tb["diff"], 0.0)
        rows = range(bsz)
        zero_states = tuple(jnp.zeros((LANES, LANES), F32) for _ in rows)
        for ref in (pa0, pa1):
            ref[...] = jnp.zeros_like(ref)
        vec_ref[...] = jnp.zeros_like(vec_ref)

        def sweep_fwd(n, carry):
            rfs, gbs = carry
            sl = pl.ds(pl.multiple_of(n * c, c), c)
            qs, ks, vs, dys, dybs, q01, k01, dy01 = [], [], [], [], [], [], [], []
            dgain = jnp.zeros((1, LANES), F32)
            for b in rows:
                q = q_ref[b, sl, :].astype(F32) * 0.125
                k = k_ref[b, sl, :]
                yh = yh_ref[b, sl, :]
                rstd = rstd_ref[b, sl, :]
                do = do_ref[b, sl, :].astype(F32)
                g = g_ref[b, sl, :].astype(F32)
                sg = _sigmoid(g)
                sil = g * sg
                dyh = do * gain * sil
                dg_ref[b, sl, :] = (do * yh * gain * sg * (1.0 + g * (1.0 - sg))).astype(BF16)
                dgain = dgain + jnp.sum(do * yh * sil, axis=0, keepdims=True)
                dy = rstd * (dyh - _head_mean(dyh, m0) - yh * _head_mean(dyh * yh, m0))
                dyb = dy.astype(BF16)
                dy_ref[b, sl, :] = dyb
                rf_ref[b, n] = rfs[b]
                qs.append(q)
                ks.append(k)
                vs.append(v_ref[b, sl, :])
                dys.append(dy)
                dybs.append(dyb)
                q01.append(_split_rows(q, m0))
                k01.append(_split_rows(k.astype(F32), m0))
                dy01.append(_split_rows(dy, m0))
            s01 = [_dot_nt(q01[b], ks[b]) for b in rows]
            da01 = [_dot_nt(dy01[b], vs[b]) for b in rows]
            rbn = [rb_ref[b, n] for b in rows]
            states = [jnp.concatenate([rfs[b], rbn[b]], axis=0).astype(BF16) for b in rows]
            dqc = [_dot_nt(dybs[b], states[b]) for b in rows]
            gbb = [gbs[b].astype(BF16) for b in rows]
            dkb = [_dot_nt(vs[b], gbb[b]) for b in rows]
            qfb = [jnp.concatenate([qs[b] * tb["qdec_f"], qs[b] * tb["qdec_b"]], axis=1) for b in rows]
            direct = [_dot_tn(qfb[b].astype(BF16), dybs[b]) for b in rows]
            ds_cat, ds_rows, a_rows = [], [], []
            for b in rows:
                a0 = s01[b][0:c] * tb["d0"]
                a1 = s01[b][c:] * tb["d1"]
                pa0[...] += da01[b][0:c] * a0
                pa1[...] += da01[b][c:] * a1
                ds0 = da01[b][0:c] * tb["d0"]
                ds1 = da01[b][c:] * tb["d1"]
                ds_cat.append(jnp.concatenate([ds0, ds1], axis=1).astype(BF16))
                ds_rows.append(jnp.concatenate([ds0, ds1], axis=0).astype(BF16))
                a_rows.append(jnp.concatenate([a0, a1], axis=0).astype(BF16))
            kbd = [ks[b].astype(F32) * tb["kdec_b"] for b in rows]
            dq_in = [_dot(ds_cat[b], k01[b]) for b in rows]
            dk_in = [_dot_tn(ds_rows[b], q01[b]) for b in rows]
            dv_in = [_dot_tn(a_rows[b], dy01[b]) for b in rows]
            dv_gb = [_dot(kbd[b].astype(BF16), gbb[b]) for b in rows]
            new_rf, new_gb = [], []
            dlf = jnp.zeros((1, LANES), F32)
            dlb = jnp.zeros((1, LANES), F32)
            for b in rows:
                dqf, dqb = dqc[b][:, 0:LANES], dqc[b][:, LANES:]
                qf, qb = qfb[b][:, 0:LANES], qfb[b][:, LANES:]
                dq = dq_in[b] + dqf * tb["qdec_f"] + dqb * tb["qdec_b"]
                dq_ref[b, sl, :] = (dq * 0.125).astype(BF16)
                dk_acc[b, sl, :] = dk_in[b] + dkb[b] * tb["kdec_b"]
                dv_acc[b, sl, :] = dv_in[b] + dv_gb[b]
                dlf = dlf + jnp.sum((row + 1.0) * qf * dqf, axis=0, keepdims=True)
                dlb = dlb + jnp.sum((c - row) * qb * dqb + row * kbd[b] * dkb[b], axis=0, keepdims=True)
                dlb = dlb + c * tb["cdec_b"] * jnp.sum(gbs[b] * rbn[b], axis=0, keepdims=True)
                dirf_ref[b, n] = jnp.where(bd, direct[b][0:LANES], 0.0)
                new_gb.append(jnp.where(bd, direct[b][LANES:], 0.0) + tb["cdec_b"] * gbs[b])
                new_rf.append(rfs[b] * tb["cdec_f"] + kvf_ref[b, n])
            vec_ref[0:1, :] += dlf
            vec_ref[1:2, :] += dlb
            vec_ref[6:7, :] += dgain
            return tuple(new_rf), tuple(new_gb)

        _loop_grouped(n_chunk, sweep_fwd, (zero_states, zero_states), per_trip=4)

        def sweep_bwd(i, gfs):
            n = n_chunk - 1 - i
            sl = pl.ds(pl.multiple_of(n * c, c), c)
            gfb = [gfs[b].astype(BF16) for b in rows]
            kfd = [k_ref[b, sl, :].astype(F32) * tb["kdec_f"] for b in rows]
            dkf = [_dot_nt(v_ref[b, sl, :], gfb[b]) for b in rows]
            dvf = [_dot(kfd[b].astype(BF16), gfb[b]) for b in rows]
            new = []
            dlf = jnp.zeros((1, LANES), F32)
            for b in rows:
                dk_ref[b, sl, :] = (dk_acc[b, sl, :] + dkf[b] * tb["kdec_f"]).astype(BF16)
                dv_ref[b, sl, :] = (dv_acc[b, sl, :] + dvf[b]).astype(BF16)
                dlf = dlf + jnp.sum((c - 1.0 - row) * kfd[b] * dkf[b], axis=0, keepdims=True)
                dlf = dlf + c * tb["cdec_f"] * jnp.sum(gfs[b] * rf_ref[b, n], axis=0, keepdims=True)
                new.append(dirf_ref[b, n] + tb["cdec_f"] * gfs[b])
            vec_ref[0:1, :] += dlf
            return tuple(new)

        _loop_grouped(n_chunk, sweep_bwd, zero_states)
        vec_ref[2:3, :] = jnp.sum(pa0[...] * wf, axis=0, keepdims=True)
        vec_ref[3:4, :] = jnp.sum(pa1[...] * wf, axis=0, keepdims=True)
        vec_ref[4:5, :] = jnp.sum(pa0[...] * wb, axis=0, keepdims=True)
        vec_ref[5:6, :] = jnp.sum(pa1[...] * wb, axis=0, keepdims=True)
        part_ref[...] = vec_ref[...]

    blk, lane, gain, pair = _ret_specs(bsz, s)
    out_bf = jax.ShapeDtypeStruct((bsz, s, RET_WIDTH), BF16)
    state = pltpu.VMEM((bsz, n_chunk, LANES, LANES), F32)
    saved = _ret_state_spec(bsz, n_chunk)[0]
    return _hosted_call(
        body, "ret_bwd", (4,),
        in_specs=[blk(CB_RQ), blk(CB_RK), blk(CB_RV), blk(CB_RG), pair, pair, pair, lane, lane, gain, saved, saved],
        out_specs=[pair, pair, pair, pair, pl.BlockSpec((None, 8, LANES), lambda p: (p, 0, 0))],
        out_shape=[out_bf, out_bf, out_bf, out_bf, jax.ShapeDtypeStruct((4, 8, LANES), F32)],
        scratch_shapes=[state, state,
                        pltpu.VMEM((bsz, s, LANES), BF16), pltpu.VMEM((bsz, s, LANES), F32),
                        pltpu.VMEM((bsz, s, LANES), F32),
                        pltpu.VMEM((c, c), F32), pltpu.VMEM((c, c), F32), pltpu.VMEM((8, LANES), F32)],
        operands=(u3, u3, u3, u3, y_hat, y_rstd, d_o, lgf_l, lgb_l, gn_gain, *states), rider=rider)


def _attn_window_tables(n, s):
    qi = lax.broadcasted_iota(jnp.int32, (CHUNK, 3 * CHUNK), 0)
    kj = lax.broadcasted_iota(jnp.int32, (CHUNK, 3 * CHUNK), 1)
    dist = jnp.abs(kj - CHUNK - qi)
    kpos = n * CHUNK - CHUNK + kj
    valid = (dist <= CHUNK) & (kpos >= 0) & (kpos < s)
    return dist.astype(F32), valid


def _dup_kv_head(x, g):
    lane = lax.broadcasted_iota(jnp.int32, x.shape, 1)
    keep = (lane < HEAD_DIM) == (g == 0)
    xf = x.astype(F32)
    return jnp.where(keep, xf, pltpu.roll(xf, HEAD_DIM, 1))


def _attn_specs(s):
    q = pl.BlockSpec((None, s, 2 * LANES), lambda b, g: (b, 0, CB_AQ // 2 + g))
    k = pl.BlockSpec((None, s, LANES), lambda b, g: (b, 0, CB_AK))
    v = pl.BlockSpec((None, s, LANES), lambda b, g: (b, 0, CB_AV))
    grp = pl.BlockSpec((None, s, 2 * LANES), lambda b, g: (b, 0, g))
    smem = pl.BlockSpec(memory_space=pltpu.SMEM)
    return q, k, v, grp, smem


def _fill_padded(dst_ref, val, s):
    dst_ref[0:CHUNK, :] = jnp.zeros((CHUNK, LANES), dst_ref.dtype)
    dst_ref[CHUNK:CHUNK + s, :] = val.astype(dst_ref.dtype)
    dst_ref[CHUNK + s:2 * CHUNK + s, :] = jnp.zeros((CHUNK, LANES), dst_ref.dtype)


def _attn_probs(sc, slope, snk, dist, valid):
    sc = jnp.where(valid, sc - slope * dist, NEG_INF)
    m = jnp.maximum(jnp.max(sc, axis=1, keepdims=True), snk)
    e = jnp.exp(sc - m)
    es = jnp.exp(snk - m)
    inv = 1.0 / (jnp.sum(e, axis=1, keepdims=True) + es)
    return e * inv, es * inv


def _stack_heads(x2, m0):
    parts = []
    for pr in range(2):
        xp = x2[:, pr * LANES:(pr + 1) * LANES]
        parts += [jnp.where(m0, xp, 0.0), jnp.where(m0, 0.0, xp)]
    return jnp.concatenate(parts, axis=0).astype(BF16)


def _unstack_pair(x_all, pr, m0):
    return jnp.where(m0, x_all[(2 * pr) * CHUNK:(2 * pr + 1) * CHUNK], x_all[(2 * pr + 1) * CHUNK:(2 * pr + 2) * CHUNK])


def _attn_saved_specs(bsz, n_blk):
    specs = [pl.BlockSpec((None, None, n_blk, 4 * CHUNK, w), lambda b, g: (b, g, 0, 0, 0)) for w in (3 * CHUNK, 1)]
    shapes = [jax.ShapeDtypeStruct((bsz, 2, n_blk, 4 * CHUNK, 3 * CHUNK), BF16),
              jax.ShapeDtypeStruct((bsz, 2, n_blk, 4 * CHUNK, 1), F32)]
    return specs, shapes


def _attn_fwd(u3, slopes, sink, rider=None):
    bsz, s, _ = u3.shape
    n_blk = s // CHUNK

    def body(slope_ref, sink_ref, q_ref, k_ref, v_ref, o_ref, p_ref, ps_ref, kp_ref, vp_ref):
        g = pl.program_id(1)
        _fill_padded(kp_ref, _dup_kv_head(k_ref[...], g), s)
        _fill_padded(vp_ref, _dup_kv_head(v_ref[...], g), s)
        m0 = lax.broadcasted_iota(jnp.int32, (CHUNK, LANES), 1) < HEAD_DIM

        def blk(n, carry):
            r0 = pl.multiple_of(n * CHUNK, CHUNK)
            kw = kp_ref[pl.ds(r0, 3 * CHUNK), :]
            vw = vp_ref[pl.ds(r0, 3 * CHUNK), :]
            dist, valid = _attn_window_tables(n, s)
            q_all = _stack_heads(q_ref[pl.ds(r0, CHUNK), :].astype(F32) * 0.125, m0)
            sc_all = _dot_nt(q_all, kw)
            probs, sinks = [], []
            for i in range(4):
                p, ps = _attn_probs(sc_all[i * CHUNK:(i + 1) * CHUNK], slope_ref[g * 4 + i], sink_ref[g * 4 + i],
                                    dist, valid)
                probs.append(p.astype(BF16))
                sinks.append(ps)
            p_all = jnp.concatenate(probs, axis=0)
            p_ref[n] = p_all
            ps_ref[n] = jnp.concatenate(sinks, axis=0)
            out_all = _dot(p_all, vw)
            for pr in range(2):
                o_ref[pl.ds(r0, CHUNK), pr * LANES:(pr + 1) * LANES] = _unstack_pair(out_all, pr, m0).astype(BF16)
            return carry

        lax.fori_loop(0, n_blk, blk, 0, unroll=4)

    q, k, v, grp, smem = _attn_specs(s)
    saved_specs, saved_shapes = _attn_saved_specs(bsz, n_blk)
    return _hosted_call(
        body, "attn_fwd", (bsz, 2),
        in_specs=[smem, smem, q, k, v],
        out_specs=[grp] + saved_specs,
        out_shape=[jax.ShapeDtypeStruct((bsz, s, ATTN_WIDTH), BF16)] + saved_shapes,
        scratch_shapes=[pltpu.VMEM((s + 2 * CHUNK, LANES), BF16), pltpu.VMEM((s + 2 * CHUNK, LANES), BF16)],
        operands=(slopes, sink, u3, u3, u3), rider=rider)


def _attn_bwd(u3, d_o, probs, sink_probs, rider=None):
    bsz, s, _ = u3.shape
    n_blk = s // CHUNK

    def body(q_ref, k_ref, v_ref, do_ref, p_ref, ps_ref, dq_ref, dkv_ref, ds_ref,
             kp_ref, vp_ref, dk_acc, dv_acc):
        g = pl.program_id(1)
        _fill_padded(kp_ref, _dup_kv_head(k_ref[...], g), s)
        _fill_padded(vp_ref, _dup_kv_head(v_ref[...], g), s)
        dk_acc[...] = jnp.zeros_like(dk_acc)
        dv_acc[...] = jnp.zeros_like(dv_acc)
        m0 = lax.broadcasted_iota(jnp.int32, (CHUNK, LANES), 1) < HEAD_DIM

        def blk(n, dsink):
            r0 = pl.multiple_of(n * CHUNK, CHUNK)
            win = pl.ds(r0, 3 * CHUNK)
            kw = kp_ref[win, :]
            vw = vp_ref[win, :]
            q_all = _stack_heads(q_ref[pl.ds(r0, CHUNK), :].astype(F32) * 0.125, m0)
            do_all = _stack_heads(do_ref[pl.ds(r0, CHUNK), :].astype(F32), m0)
            p_all = p_ref[n]
            ps_all = ps_ref[n]
            dp_all = _dot_nt(do_all, vw)
            new_dsink, dscs = [], []
            for i in range(4):
                rows = slice(i * CHUNK, (i + 1) * CHUNK)
                p = p_all[rows].astype(F32)
                dp = dp_all[rows]
                delta = jnp.sum(p * dp, axis=1, keepdims=True)
                dscs.append((p * (dp - delta)).astype(BF16))
                dsh = jnp.sum(ps_all[rows] * delta, axis=0, keepdims=True)
                new_dsink.append(dsink[i] - jnp.broadcast_to(dsh, (1, LANES)))
            dsc_all = jnp.concatenate(dscs, axis=0)
            dq_all = _dot(dsc_all, kw)
            dk_acc[win, :] += _dot_tn(dsc_all, q_all)
            dv_acc[win, :] += _dot_tn(p_all, do_all)
            for pr in range(2):
                dq_ref[pl.ds(r0, CHUNK), pr * LANES:(pr + 1) * LANES] = (
                    _unstack_pair(dq_all, pr, m0) * 0.125).astype(BF16)
            return tuple(new_dsink)

        dsink = _loop_grouped(n_blk, blk, tuple(jnp.zeros((1, LANES), F32) for _ in range(4)), per_trip=4)
        dk = dk_acc[CHUNK:CHUNK + s, :]
        dv = dv_acc[CHUNK:CHUNK + s, :]
        lane = lax.broadcasted_iota(jnp.int32, (s, LANES), 1)
        fold = lambda a: a + pltpu.roll(a, HEAD_DIM, 1)
        dkv_ref[...] = jnp.where(lane < HEAD_DIM, fold(dk), fold(dv)).astype(BF16)
        ds_ref[...] = jnp.zeros_like(ds_ref)
        for i in range(4):
            ds_ref[i:i + 1, :] = dsink[i]

    q, k, v, grp, _ = _attn_specs(s)
    return _hosted_call(
        body, "attn_bwd", (bsz, 2),
        in_specs=[q, k, v, grp] + _attn_saved_specs(bsz, n_blk)[0],
        out_specs=[grp, pl.BlockSpec((None, s, LANES), lambda b, g: (b, 0, g)),
                   pl.BlockSpec((None, None, 8, LANES), lambda b, g: (b, g, 0, 0))],
        out_shape=[jax.ShapeDtypeStruct((bsz, s, ATTN_WIDTH), BF16), jax.ShapeDtypeStruct((bsz, s, 2 * LANES), BF16),
                   jax.ShapeDtypeStruct((bsz, 2, 8, LANES), F32)],
        scratch_shapes=[pltpu.VMEM((s + 2 * CHUNK, LANES), BF16), pltpu.VMEM((s + 2 * CHUNK, LANES), BF16),
                        pltpu.VMEM((s + 2 * CHUNK, LANES), F32), pltpu.VMEM((s + 2 * CHUNK, LANES), F32)],
        operands=(u3, u3, u3, d_o, probs, sink_probs), rider=rider)


def _ffn_bwd(dz2, gs, us, pg, ple, zh1, r1, g1, wg4, wu4, wd4, wpg, w_out):
    t = dz2.shape[0]
    tm = 256
    wg_t, wu_t, wd_all = (w.reshape(FFN, D_MODEL) for w in (wg4, wu4, wd4))

    def body(dz_ref, gs_ref, us_ref, pg_ref, ple_ref, zh_ref, r_ref, g1_ref,
             wg_hbm, wu_hbm, wd_hbm, wpg_hbm, wo_hbm,
             dgs_ref, dus_ref, dsp_ref, dple_ref, dz1_ref, dyr_ref, dya_ref, dg1_ref, db1_ref,
             wg, wu, wd, wpg, wo, wsem):
        step = pl.program_id(0)
        loads = _resident_quarters(wd_hbm, wd) + _resident_quarters(wg_hbm, wg) + _resident_quarters(wu_hbm, wu)
        _load_resident(step, loads + [(wpg_hbm, wpg), (wo_hbm, wo)], wsem)

        @pl.when(step == 0)
        def _():
            dg1_ref[...] = jnp.zeros_like(dg1_ref)
            db1_ref[...] = jnp.zeros_like(db1_ref)

        dz = dz_ref[...]
        dzb = dz.astype(BF16)
        dh = ALPHA * dz
        pending = []
        chunks = [slice(n * FFN_CHUNK, (n + 1) * FFN_CHUNK) for n in range(N_FFN_CHUNK)]
        for n in range(N_FFN_CHUNK + 1):
            if n < N_FFN_CHUNK:
                da = _dot_nt(dzb, wd[chunks[n], :])
                gj = gs_ref[:, chunks[n]].astype(F32)
                uj = us_ref[:, chunks[n]].astype(F32)
                sg = _sigmoid(gj)
                dgj = (da * uj * sg * (1.0 + gj * (1.0 - sg))).astype(BF16)
                duj = (da * gj * sg).astype(BF16)
                dgs_ref[:, chunks[n]] = dgj
                dus_ref[:, chunks[n]] = duj
                pending.append((dgj, duj))
            if n > 0:
                dgp, dup = pending[n - 1]
                dh = dh + _dot(dgp, wg[chunks[n - 1], :]) + _dot(dup, wu[chunks[n - 1], :])
        pgv = pg_ref[...].astype(F32)
        plev = ple_ref[...].astype(F32)
        dple_ref[...] = (dz * pgv).astype(BF16)
        dsp = (dz * plev * pgv * (1.0 - pgv)).astype(BF16)
        dsp_ref[...] = dsp
        dh = dh + _dot_nt(dsp, wpg[...])
        zh = zh_ref[...]
        dg1_ref[...] += jnp.sum(dh * zh, axis=0, keepdims=True)
        db1_ref[...] += jnp.sum(dh, axis=0, keepdims=True)
        dzh = dh * g1_ref[...]
        m1 = jnp.mean(dzh, axis=1, keepdims=True)
        m2 = jnp.mean(dzh * zh, axis=1, keepdims=True)
        dz1 = r_ref[...] * (dzh - m1 - zh * m2)
        dz1_ref[...] = dz1
        dyc = _dot_nt(dz1.astype(BF16), wo[...])
        dyr_ref[...] = dyc[:, 0:RET_WIDTH].astype(BF16)
        dya_ref[...] = dyc[:, RET_WIDTH:].astype(BF16)

    row = lambda w: pl.BlockSpec((tm, w), lambda i: (i, 0))
    const = lambda s: pl.BlockSpec(s, lambda i: (0, 0))
    hbm = pl.BlockSpec(memory_space=pl.ANY)
    hid_shape = jax.ShapeDtypeStruct((t, FFN), BF16)
    return pl.pallas_call(
        body, name="ffn_bwd", grid=(t // tm,),
        in_specs=[row(D_MODEL), row(FFN), row(FFN), row(D_MODEL), row(D_MODEL), row(D_MODEL), row(1),
                  const((1, D_MODEL)), hbm, hbm, hbm, hbm, hbm],
        out_specs=[row(FFN), row(FFN), row(D_MODEL), row(D_MODEL), row(D_MODEL), row(RET_WIDTH), row(ATTN_WIDTH),
                   const((1, D_MODEL)), const((1, D_MODEL))],
        out_shape=[hid_shape, hid_shape, jax.ShapeDtypeStruct((t, D_MODEL), BF16),
                   jax.ShapeDtypeStruct((t, D_MODEL), BF16), jax.ShapeDtypeStruct((t, D_MODEL), F32),
                   jax.ShapeDtypeStruct((t, RET_WIDTH), BF16), jax.ShapeDtypeStruct((t, ATTN_WIDTH), BF16),
                   jax.ShapeDtypeStruct((1, D_MODEL), F32), jax.ShapeDtypeStruct((1, D_MODEL), F32)],
        scratch_shapes=[pltpu.VMEM((FFN, D_MODEL), BF16), pltpu.VMEM((FFN, D_MODEL), BF16),
                        pltpu.VMEM((FFN, D_MODEL), BF16),
                        pltpu.VMEM(wpg.shape, BF16), pltpu.VMEM(w_out.shape, BF16),
                        pltpu.SemaphoreType.DMA((3 * N_SHARD + 2,))],
        compiler_params=_params("arbitrary", vmem=VMEM_LIMIT),
    )(dz2, gs, us, pg, ple, zh1, r1, g1, wg_t, wu_t, wd_all, wpg, w_out)


def _wgrad_misc(y_ret, y_att, dz1, hb, dsp, p2d, dple, rider=None):
    t = dz1.shape[0]
    tk = min(t, 512)
    pc = D_MODEL // N_SHARD

    def body(yr_ref, ya_ref, dz_ref, hb_ref, dsp_ref, p_ref, dple_ref, wo_ref, wpg_ref, wpe_ref):
        @pl.when(pl.program_id(0) == 0)
        def _():
            wo_ref[...] = jnp.zeros_like(wo_ref)
            wpg_ref[...] = jnp.zeros_like(wpg_ref)
            wpe_ref[...] = jnp.zeros_like(wpe_ref)

        dzb = dz_ref[...].astype(BF16)
        wo_ref[0:RET_WIDTH, :] += _dot_tn(yr_ref[...], dzb)
        wo_ref[RET_WIDTH:, :] += _dot_tn(ya_ref[...], dzb)
        wpg_ref[...] += _dot_tn(hb_ref[...], dsp_ref[...])
        dpe = _dot_tn(p_ref[...].astype(BF16), dple_ref[...])
        for j in range(N_SHARD):
            wpe_ref[j] += dpe[:, j * pc:(j + 1) * pc]

    row = lambda w: pl.BlockSpec((tk, w), lambda k: (k, 0))
    const = lambda s: pl.BlockSpec(s, lambda k: (0,) * len(s))
    return _hosted_call(
        body, "wgrad_misc", (t // tk,),
        in_specs=[row(RET_WIDTH), row(ATTN_WIDTH), row(D_MODEL), row(D_MODEL), row(D_MODEL), row(PLE_DIM),
                  row(D_MODEL)],
        out_specs=[const((D_MODEL, D_MODEL)), const((D_MODEL, D_MODEL)), const((N_SHARD, PLE_DIM, pc))],
        out_shape=[jax.ShapeDtypeStruct((D_MODEL, D_MODEL), F32), jax.ShapeDtypeStruct((D_MODEL, D_MODEL), F32),
                   jax.ShapeDtypeStruct((N_SHARD, PLE_DIM, pc), F32)],
        scratch_shapes=[], operands=(y_ret, y_att, dz1, hb, dsp, p2d, dple), rider=rider, semantics=["arbitrary"])


def _wgrad_ffn(acts, dgs, dus, hb, dz2b):
    t = dz2b.shape[0]
    tk = min(t, 512)
    nk = t // tk

    def body(act_ref, dg_ref, du_ref, hb_ref, dz_ref, og_ref, ou_ref, od_ref):
        @pl.when(pl.program_id(1) == 0)
        def _():
            og_ref[...] = jnp.zeros_like(og_ref)
            ou_ref[...] = jnp.zeros_like(ou_ref)
            od_ref[...] = jnp.zeros_like(od_ref)

        hbv = hb_ref[...]
        og_ref[...] += _dot_tn(dg_ref[...], hbv)
        ou_ref[...] += _dot_tn(du_ref[...], hbv)
        od_ref[...] += _dot_tn(act_ref[...], dz_ref[...])

    half = FFN // 2
    a_spec = pl.BlockSpec((tk, half), lambda j, k: (k, j))
    b_spec = pl.BlockSpec((tk, D_MODEL), lambda j, k: (k, 0))
    o_spec = pl.BlockSpec((half, D_MODEL), lambda j, k: (j, 0))
    o_shape = jax.ShapeDtypeStruct((FFN, D_MODEL), F32)
    outs = pl.pallas_call(
        body, name="wgrad_ffn", grid=(2, nk),
        in_specs=[a_spec, a_spec, a_spec, b_spec, b_spec],
        out_specs=[o_spec] * 3, out_shape=[o_shape] * 3,
        compiler_params=_params("parallel", "arbitrary", vmem=VMEM_LIMIT),
    )(acts, dgs, dus, hb, dz2b)
    return [o.reshape(N_SHARD, FFN_SHARD, D_MODEL) for o in outs]


KV_ORDER = (0, 128, 64, 192)


def _wgrad_in(pieces, x2d, rider=None):
    t = x2d.shape[0]
    tk = min(t, 512)
    nk = t // tk
    kv0 = CB_AK * LANES

    def body(p0, p1, p2, p3, p4, pkv, x_ref, o_ref):
        @pl.when(pl.program_id(0) == 0)
        def _():
            o_ref[...] = jnp.zeros_like(o_ref)

        xb = x_ref[...].astype(BF16)
        for i, ref in enumerate((p0, p1, p2, p3, p4)):
            o_ref[i * 512:(i + 1) * 512, :] += _dot_tn(ref[...], xb)
        dkv = _dot_tn(pkv[...], xb)
        for i, o in enumerate(KV_ORDER):
            o_ref[kv0 + o:kv0 + o + HEAD_DIM, :] += dkv[i * HEAD_DIM:(i + 1) * HEAD_DIM]

    row = lambda w: pl.BlockSpec((tk, w), lambda k: (k, 0))
    return _hosted_call(
        body, "wgrad_in", (nk,),
        in_specs=[row(512)] * 5 + [row(256), row(D_MODEL)],
        out_specs=[pl.BlockSpec((IN_WIDTH, D_MODEL), lambda k: (0, 0))],
        out_shape=[jax.ShapeDtypeStruct((IN_WIDTH, D_MODEL), F32)],
        scratch_shapes=[], operands=(*pieces, x2d), rider=rider, semantics=["arbitrary"])


def _inproj_bwd(dz1, pieces, w_in_t, w_kv, rider=None):
    t = dz1.shape[0]
    tm = 512
    n_main = 5 * 512

    def body(dz_ref, p0, p1, p2, p3, p4, pkv, wm_ref, wkv_ref, o_ref):
        acc = ALPHA * dz_ref[...]
        for i, ref in enumerate((p0, p1, p2, p3, p4)):
            acc = acc + _dot(ref[...], wm_ref[i * 512:(i + 1) * 512, :])
        o_ref[...] = acc + _dot(pkv[...], wkv_ref[...])

    row = lambda w: pl.BlockSpec((tm, w), lambda i: (i, 0))
    const = lambda s: pl.BlockSpec(s, lambda i: (0, 0))
    return _hosted_call(
        body, "inproj_bwd", (t // tm,),
        in_specs=[row(D_MODEL)] + [row(512)] * 5 + [row(256), const((n_main, D_MODEL)), const(w_kv.shape)],
        out_specs=[row(D_MODEL)],
        out_shape=[jax.ShapeDtypeStruct((t, D_MODEL), F32)],
        scratch_shapes=[], operands=(dz1, *pieces, w_in_t, w_kv), rider=rider)


def _coords():
    return lax.axis_index("x"), lax.axis_index("y"), lax.axis_index("c")


def _chip_of(x, y, rel):
    return (1 - x if rel & 2 else x), (1 - y if rel & 1 else y)


def _all_gather_weights(shards):
    first = _gather_near_rider(shards)
    later = [f(first.out_shapes, chained=True) for f in (_gather_relay_rider, _gather_pass_rider)]
    return _run_riders("gather_weights", shards, first.out_shapes, [first] + later)


def _run_riders(name, ins, out_shapes, riders):
    n_in, n_out = len(ins), len(out_shapes)

    def body(*refs):
        in_refs, out_refs = refs[:n_in], refs[n_in:n_in + n_out]
        k = n_in + n_out
        for r in riders:
            sems = refs[k:k + len(r.sems)]
            k += len(r.sems)
            r.start(in_refs, out_refs, sems)
            r.finish(in_refs, out_refs, sems)

    hbm = pl.BlockSpec(memory_space=pl.ANY)
    return pl.pallas_call(
        body, name=name, in_specs=[hbm] * n_in, out_specs=[hbm] * n_out, out_shape=list(out_shapes),
        scratch_shapes=[s for r in riders for s in r.sems],
    )(*ins)


def _gather_half(outs, w, chip, cc):
    h = outs[w].shape[1] // 2
    return outs[w].at[chip, pl.ds(cc * h, h), :]


NEAR = (1, 2)


def _gather_near_rider(shards, rels=NEAR):
    nw, nr = len(shards), len(rels)

    def copies(ins, outs, sems, arrivals):
        send, recv, lsend, lrecv = sems
        x, y, c = _coords()
        me = 2 * x + y
        own = [pltpu.make_async_remote_copy(
            src_ref=ins[w], dst_ref=outs[w].at[me], send_sem=lsend.at[w], recv_sem=lrecv.at[w],
            device_id=(x, y, 1 - c), device_id_type=MESH) for w in range(nw)]
        out, arrive = [], []
        for i, rel in enumerate(rels):
            kx, ky = _chip_of(x, y, rel)
            for w in range(nw):
                h = shards[w].shape[0] // 2
                sem = dict(send_sem=send.at[w * nr + i], recv_sem=recv.at[w * nr + i],
                           device_id=(kx, ky, c), device_id_type=MESH)
                out.append(pltpu.make_async_remote_copy(
                    src_ref=ins[w].at[pl.ds(c * h, h), :], dst_ref=_gather_half(outs, w, me, c), **sem))
                if arrivals:
                    theirs = _gather_half(outs, w, 2 * kx + ky, c)
                    arrive.append(pltpu.make_async_remote_copy(src_ref=theirs, dst_ref=theirs, **sem))
        return own, out, arrive

    def start(ins, outs, sems):
        own, out, _ = copies(ins, outs, sems, arrivals=False)
        for cp in own + out:
            cp.start()

    def finish(ins, outs, sems):
        own, out, arrive = copies(ins, outs, sems, arrivals=True)
        for cp in arrive:
            cp.wait_recv()
        for cp in out:
            cp.wait_send()
        for cp in own:
            cp.wait()

    dma = pltpu.SemaphoreType.DMA
    return _Rider(shards, [jax.ShapeDtypeStruct((N_SHARD,) + s.shape, s.dtype) for s in shards],
                  [dma((nr * nw,)), dma((nr * nw,)), dma((nw,)), dma((nw,))], start, finish)


def _gather_relay_rider(gathered, chained=False):
    nw = len(gathered)

    def quarter(outs, w, chip, c, p):
        q = outs[w].shape[1] // 4
        return outs[w].at[chip, pl.ds(c * 2 * q + p * q, q), :]

    def copies(outs, sems):
        send, recv = sems
        x, y, c = _coords()
        (yx, yy), (xx, xy), (dx, dy) = (_chip_of(x, y, rel) for rel in (1, 2, 3))
        out, arrive = [], []
        for w in range(nw):
            for p, (src_chip, dst) in enumerate(((2 * xx + xy, (yx, yy)), (2 * yx + yy, (xx, xy)))):
                rows = quarter(outs, w, src_chip, c, p)
                sem = dict(send_sem=send.at[w * 2 + p], recv_sem=recv.at[w * 2 + p], device_id_type=MESH)
                out.append(pltpu.make_async_remote_copy(src_ref=rows, dst_ref=rows, device_id=(*dst, c), **sem))
                mine = quarter(outs, w, 2 * dx + dy, c, p)
                arrive.append(pltpu.make_async_remote_copy(src_ref=mine, dst_ref=mine, device_id=(*dst, c), **sem))
        return out, arrive

    def start(ins, outs, sems):
        for cp in copies(outs, sems)[0]:
            cp.start()

    def finish(ins, outs, sems):
        out, arrive = copies(outs, sems)
        for cp in arrive:
            cp.wait_recv()
        for cp in out:
            cp.wait_send()

    dma = pltpu.SemaphoreType.DMA
    shapes = [jax.ShapeDtypeStruct(g.shape, g.dtype) for g in gathered]
    if chained:
        return _Rider([], [], [dma((2 * nw,)), dma((2 * nw,))], start, finish)
    return _Rider(gathered, shapes, [dma((2 * nw,)), dma((2 * nw,))], start, finish,
                  aliases={w: w for w in range(nw)})


def _gather_pass_rider(gathered, chained=False):
    nw = len(gathered)

    def copies(outs, sems, cc):
        send, recv = sems
        x, y, c = _coords()
        res = []
        for rel in (1, 2, 3):
            kx, ky = _chip_of(x, y, rel)
            for w in range(nw):
                rows = _gather_half(outs, w, 2 * kx + ky, cc)
                res.append(pltpu.make_async_remote_copy(
                    src_ref=rows, dst_ref=rows, send_sem=send.at[w * 3 + rel - 1], recv_sem=recv.at[w * 3 + rel - 1],
                    device_id=(x, y, 1 - c), device_id_type=MESH))
        return res

    def start(ins, outs, sems):
        for cp in copies(outs, sems, lax.axis_index("c")):
            cp.start()

    def finish(ins, outs, sems):
        c = lax.axis_index("c")
        for cp in copies(outs, sems, 1 - c):
            cp.wait_recv()
        for cp in copies(outs, sems, c):
            cp.wait_send()

    dma = pltpu.SemaphoreType.DMA
    shapes = [jax.ShapeDtypeStruct(g.shape, g.dtype) for g in gathered]
    if chained:
        return _Rider([], [], [dma((3 * nw,)), dma((3 * nw,))], start, finish)
    return _Rider(gathered, shapes, [dma((3 * nw,)), dma((3 * nw,))], start, finish,
                  aliases={w: w for w in range(nw)})


def _exchange_halves_rider(parts):
    nw = len(parts)

    def copies(ins, outs, sems):
        send, recv = sems
        x, y, c = _coords()
        res = []
        for w in range(nw):
            h = parts[w].shape[1] // 2
            res.append(pltpu.make_async_remote_copy(
                src_ref=ins[w].at[:, pl.ds((1 - c) * h, h), :], dst_ref=outs[w],
                send_sem=send.at[w], recv_sem=recv.at[w], device_id=(x, y, 1 - c), device_id_type=MESH))
        return res

    def start(ins, outs, sems):
        for cp in copies(ins, outs, sems):
            cp.start()

    def finish(ins, outs, sems):
        for cp in copies(ins, outs, sems):
            cp.wait()

    dma = pltpu.SemaphoreType.DMA
    return _Rider(parts, [jax.ShapeDtypeStruct((N_SHARD, p.shape[1] // 2, p.shape[2]), p.dtype) for p in parts],
                  [dma((nw,)), dma((nw,))], start, finish)


def _add_halves(parts, theirs, pos):
    nw = len(parts)
    split = 2

    def body(pos_ref, *refs):
        ins, oth = refs[:nw], refs[nw:2 * nw]
        o32, o16 = refs[2 * nw:3 * nw], refs[3 * nw:]
        sums = [ins[w][...] + oth[w][...] for w in range(nw)]
        for w in range(nw):
            o16[w][...] = sums[w].astype(BF16)

        @pl.when(pl.program_id(1) == pos_ref[0])
        def _():
            for w in range(nw):
                o32[w][...] = sums[w]

    in_specs, oth_specs, o32_specs, shapes32, shapes16 = [], [], [], [], []
    for p in parts:
        hb = p.shape[1] // 2 // split
        blk = (None, hb, p.shape[2])
        in_specs.append(pl.BlockSpec(blk, lambda i, j, pos_ref: (j, pos_ref[1] * split + i, 0)))
        oth_specs.append(pl.BlockSpec(blk, lambda i, j, pos_ref: (j, i, 0)))
        o32_specs.append(pl.BlockSpec((hb, p.shape[2]), lambda i, j, pos_ref: (i, 0)))
        shapes32.append(jax.ShapeDtypeStruct((p.shape[1] // 2, p.shape[2]), F32))
        shapes16.append(jax.ShapeDtypeStruct((N_SHARD, p.shape[1] // 2, p.shape[2]), BF16))
    return pl.pallas_call(
        body, name="add_halves",
        grid_spec=pltpu.PrefetchScalarGridSpec(
            num_scalar_prefetch=1, grid=(split, N_SHARD),
            in_specs=in_specs + oth_specs, out_specs=o32_specs + oth_specs),
        out_shape=shapes32 + shapes16,
        compiler_params=_params("parallel", "arbitrary", vmem=VMEM_LIMIT),
    )(pos, *parts, *theirs)


def _exchange_chips_rider(sums16, rows=None, into=None):
    nw = len(sums16)
    rows = rows or [(0, s.shape[1]) for s in sums16]
    held = [w for w in range(nw) if into is not None and into[w] is not None]

    def copies(ins, outs, sems):
        send, recv = sems
        x, y, c = _coords()
        res = []
        for rel in (1, 2, 3):
            kx, ky = _chip_of(x, y, rel)
            for w in range(nw):
                r0, n = rows[w]
                res.append(pltpu.make_async_remote_copy(
                    src_ref=ins[w].at[2 * kx + ky, pl.ds(r0, n), :], dst_ref=outs[w].at[rel - 1, pl.ds(r0, n), :],
                    send_sem=send.at[w * 3 + rel - 1], recv_sem=recv.at[w * 3 + rel - 1],
                    device_id=(kx, ky, c), device_id_type=MESH))
        return res

    def start(ins, outs, sems):
        for cp in copies(ins, outs, sems):
            cp.start()

    def finish(ins, outs, sems):
        for cp in copies(ins, outs, sems):
            cp.wait()

    dma = pltpu.SemaphoreType.DMA
    return _Rider(list(sums16) + [into[w] for w in held],
                  [jax.ShapeDtypeStruct((3,) + s.shape[1:], BF16) for s in sums16],
                  [dma((3 * nw,)), dma((3 * nw,))], start, finish, aliases={nw + i: w for i, w in enumerate(held)})


def _add_chips(sums32, theirs, pos):
    nw = len(sums32)
    split = 2

    def body(pos_ref, *refs):
        ins, oth, outs = refs[:nw], refs[nw:2 * nw], refs[2 * nw:]
        for w in range(nw):
            acc = ins[w][...]
            for r in range(3):
                acc = acc + oth[w][r].astype(F32)
            outs[w][...] = acc

    in_specs, oth_specs, out_specs, shapes = [], [], [], []
    for s in sums32:
        hb = s.shape[0] // split
        in_specs.append(pl.BlockSpec((hb, s.shape[1]), lambda i, pos_ref: (i, 0)))
        oth_specs.append(pl.BlockSpec((3, hb, s.shape[1]), lambda i, pos_ref: (0, i, 0)))
        out_specs.append(pl.BlockSpec((hb, s.shape[1]), lambda i, pos_ref: (pos_ref[1] * split + i, 0)))
        shapes.append(jax.ShapeDtypeStruct((2 * s.shape[0], s.shape[1]), F32))
    return pl.pallas_call(
        body, name="add_chips",
        grid_spec=pltpu.PrefetchScalarGridSpec(
            num_scalar_prefetch=1, grid=(split,), in_specs=in_specs + oth_specs, out_specs=out_specs),
        out_shape=shapes,
        compiler_params=_params("parallel", vmem=VMEM_LIMIT),
    )(pos, *sums32, *theirs)


def _join_halves(shards):
    nw = len(shards)

    def body(*refs):
        outs = refs[nw:2 * nw]
        send, recv = refs[2 * nw:]
        x, y, c = _coords()

        def copy(w, cc):
            h = shards[w].shape[0] // 2
            rows = outs[w].at[pl.ds(cc * h, h), :]
            return pltpu.make_async_remote_copy(
                src_ref=rows, dst_ref=rows, send_sem=send.at[w], recv_sem=recv.at[w],
                device_id=(x, y, 1 - c), device_id_type=MESH)

        for w in range(nw):
            copy(w, c).start()
        for w in range(nw):
            copy(w, 1 - c).wait_recv()
            copy(w, c).wait_send()

    hbm = pl.BlockSpec(memory_space=pl.ANY)
    return pl.pallas_call(
        body, name="join_halves",
        in_specs=[hbm] * nw, out_specs=[hbm] * nw,
        out_shape=[jax.ShapeDtypeStruct(s.shape, F32) for s in shards],
        input_output_aliases={w: w for w in range(nw)},
        scratch_shapes=[pltpu.SemaphoreType.DMA((nw,)), pltpu.SemaphoreType.DMA((nw,))],
    )(*shards)


def _adamw_math(w, g, m, v):
    m = ADAM_B1 * m + (1.0 - ADAM_B1) * g
    v = ADAM_B2 * v + (1.0 - ADAM_B2) * (g * g)
    m_hat = m / (1.0 - ADAM_B1 ** ADAM_STEP)
    v_hat = v / (1.0 - ADAM_B2 ** ADAM_STEP)
    delta = -ADAM_LR * (m_hat / (jnp.sqrt(v_hat) + ADAM_EPS) + ADAM_WD * w)
    return delta, m, v


def _adamw(ws, gs, ms, vs):
    nw = len(ws)
    split = 8

    def body(*refs):
        w_r, g_r, m_r, v_r = (refs[i * nw:(i + 1) * nw] for i in range(4))
        g_o, d_o, m_o, v_o = (refs[(4 + i) * nw:(5 + i) * nw] for i in range(4))
        for k in range(nw):
            g = g_r[k][...]
            d, m, v = _adamw_math(w_r[k][...], g, m_r[k][...], v_r[k][...])
            g_o[k][...] = g
            d_o[k][...] = d
            m_o[k][...] = m
            v_o[k][...] = v

    specs = [pl.BlockSpec((w.shape[0] // split, w.shape[1]), lambda i: (i, 0)) for w in ws]
    shapes = [jax.ShapeDtypeStruct(w.shape, F32) for w in ws]
    outs = pl.pallas_call(
        body, name="adamw", grid=(split,),
        in_specs=specs * 4, out_specs=specs * 4, out_shape=shapes * 4,
        compiler_params=_params("parallel", vmem=VMEM_LIMIT),
    )(*ws, *gs, *ms, *vs)
    return outs[:nw], outs[nw:2 * nw], outs[2 * nw:3 * nw], outs[3 * nw:]


SMALL_ROWS = 8
SMALL_COLS = D_MODEL
LOSS_COL = RET_WIDTH + 24


def _small_allreduce_adamw(part, w, m, v, rider=None):
    def body(part_ref, w_ref, m_ref, v_ref, g_out, d_out, m_out, v_out, all_ref, send, recv):
        x, y, c = _coords()
        me = 4 * x + 2 * y + c
        all_ref[me] = part_ref[...]
        copies = []
        for rel in range(1, 8):
            px = 1 - x if rel & 4 else x
            py = 1 - y if rel & 2 else y
            pc = 1 - c if rel & 1 else c
            copies.append(pltpu.make_async_remote_copy(
                src_ref=part_ref, dst_ref=all_ref.at[me],
                send_sem=send.at[rel - 1], recv_sem=recv.at[rel - 1], device_id=(px, py, pc), device_id_type=MESH))
        for cp in copies:
            cp.start()
        for cp in copies:
            cp.wait()
        g = all_ref[0]
        for k in range(1, 8):
            g = g + all_ref[k]
        d, mn, vn = _adamw_math(w_ref[...], g, m_ref[...], v_ref[...])
        g_out[...] = g
        d_out[...] = d
        m_out[...] = mn
        v_out[...] = vn

    vm = pl.BlockSpec(memory_space=pltpu.VMEM)
    shape = jax.ShapeDtypeStruct((SMALL_ROWS, SMALL_COLS), F32)
    return _hosted_call(
        body, "small_allreduce_adamw", (1,),
        in_specs=[vm] * 4, out_specs=[vm] * 4, out_shape=[shape] * 4,
        scratch_shapes=[pltpu.VMEM((8, SMALL_ROWS, SMALL_COLS), F32),
                        pltpu.SemaphoreType.DMA((7,)), pltpu.SemaphoreType.DMA((7,))],
        operands=(part, w, m, v), rider=rider, semantics=["arbitrary"])


SMALL_NAMES = ("ret_decay_fwd", "ret_decay_bwd", "attn_sink", "ret_gn_gain",
               "ln1_gain", "ln1_bias", "ln2_gain", "ln2_bias")


LN_NAMES = ("ln1_gain", "ln1_bias", "ln2_gain", "ln2_bias")


def _pack_small(vals, extra=None):
    tail = jnp.zeros((1, 1), F32) if extra is None else extra.reshape(1, 1)
    row4 = jnp.concatenate([vals["ret_gn_gain"], vals["ret_decay_fwd"], vals["ret_decay_bwd"], vals["attn_sink"],
                            tail, jnp.zeros((1, SMALL_COLS - LOSS_COL - 1), F32)], axis=1)
    rows = [vals[n] for n in LN_NAMES] + [row4, jnp.zeros((SMALL_ROWS - 5, SMALL_COLS), F32)]
    return jnp.concatenate(rows, axis=0)


def _unpack_small(packed):
    out = {n: packed[i:i + 1] for i, n in enumerate(LN_NAMES)}
    o = RET_WIDTH
    out.update(ret_gn_gain=packed[4:5, 0:o], ret_decay_fwd=packed[4:5, o:o + 8],
               ret_decay_bwd=packed[4:5, o + 8:o + 16], attn_sink=packed[4:5, o + 16:o + 24])
    return out


def _local_step(x, p, tgt, w_in_t, rest, small, pos=None, small_state=None):
    bsz, s, _ = x.shape
    t = bsz * s
    x2d = x.reshape(t, D_MODEL)
    p2d = p.reshape(t, PLE_DIM)
    tgt2d = tgt.reshape(t, D_MODEL)
    dec_f = small["ret_decay_fwd"].reshape(8)
    dec_b = small["ret_decay_bwd"].reshape(8)
    lg_f = jnp.log1p(-jnp.exp2(dec_f))
    lg_b = jnp.log1p(-jnp.exp2(dec_b))
    per_lane = lambda v: jnp.repeat(v, HEAD_DIM).reshape(4, 1, LANES)
    lgf_l, lgb_l = per_lane(lg_f), per_lane(lg_b)
    sink = small["attn_sink"].reshape(8)
    slopes = 2.0 ** (-(jnp.arange(8, dtype=F32) + 1.0))
    gn_gain = small["ret_gn_gain"]
    g1, b1, g2, b2 = (small[n] for n in ("ln1_gain", "ln1_bias", "ln2_gain", "ln2_bias"))

    dist = pos is not None
    shard = dict(zip(REST_NAMES, rest)) if dist else {}
    near = lambda names, rels=NEAR: _gather_near_rider([shard[n] for n in names], rels)
    wave1, wave2, wave3 = ("w_out", "w_ple_gate", "w_ffn_gate"), ("w_ffn_up", "w_ple_proj"), ("w_ffn_down",)
    n1 = len(wave1)
    u, *o1 = _inproj(x2d, w_in_t, rider=near(wave1) if dist else None)
    u3 = u.reshape(bsz, s, IN_WIDTH)
    y_hat, y_rstd, y_ret, ret_rb, ret_kvf, *o2 = _ret_fwd(u3, lgf_l, lgb_l, gn_gain, rider=_merge_riders(
        [_gather_relay_rider(o1), near(wave2)]) if dist else None)
    y_att, att_p, att_ps, *o3 = _attn_fwd(u3, slopes, sink, rider=_merge_riders(
        [_gather_pass_rider(o2[:n1]), _gather_relay_rider(o2[n1:]), near(wave3, (1, 2, 3))]) if dist else None)
    gathered = dict(zip(wave1, o3[:n1]))
    w_out = _assemble_weights({"w_out": gathered["w_out"]})["w_out"] if dist else rest["w_out"]
    zh1, r1, hb, *o4 = _outproj_ln1(y_ret.reshape(t, RET_WIDTH), y_att.reshape(t, ATTN_WIDTH), x2d, w_out, g1, b1,
                                    rider=_gather_pass_rider(o3[n1:]) if dist else None)
    gathered.update(zip(wave2 + wave3, o4))
    wts = _assemble_weights(gathered) if dist else rest
    dz2, dz2b, gs, us, acts, pg, ple, sq, dg2, db2 = _ffn_fwd(
        zh1, hb, p2d, tgt2d, g1, b1, g2, b2, wts["gate4"], wts["up4"], wts["down4"], wts["ple_proj"], wts["ple_gate"])
    dgs, dus, dsp, dple, dz1, dyr, dya, dg1, db1 = _ffn_bwd(dz2, gs, us, pg, ple, zh1, r1, g1, wts["gate4"],
                                                          wts["up4"], wts["down4"], wts["ple_gate"], wts["w_out"])
    ffn_parts = list(_wgrad_ffn(acts, dgs, dus, hb, dz2b))
    d_w_out, d_ple_gate, d_ple_proj, *th_ffn = _wgrad_misc(
        y_ret.reshape(t, RET_WIDTH), y_att.reshape(t, ATTN_WIDTH), dz1, hb, dsp, p2d, dple,
        rider=_exchange_halves_rider(ffn_parts[:2]) if dist else None)
    misc_parts = [d_w_out.reshape(N_SHARD, D_MODEL // N_SHARD, D_MODEL), d_ple_proj,
                  d_ple_gate.reshape(N_SHARD, D_MODEL // N_SHARD, D_MODEL)]
    dyr3, dya3 = dyr.reshape(bsz, s, RET_WIDTH), dya.reshape(bsz, s, ATTN_WIDTH)
    if dist:
        s_gu = _add_halves(ffn_parts[:2], th_ffn, pos)
        half = FFN_SHARD // 2
        quarter = half // 2
        later_parts = [ffn_parts[2]] + misc_parts
        drq, drk, drv, drg, rpart, *o5 = _ret_bwd(u3, y_hat, y_rstd, (ret_rb, ret_kvf), dyr3, lgf_l, lgb_l, gn_gain,
                                                  rider=_merge_riders(
            [_exchange_chips_rider(s_gu[2:], rows=[(0, half), (0, quarter)]), _exchange_halves_rider(later_parts)]))
        s_dm = _add_halves(later_parts, o5[2:], pos)
        daq, dakv, spart, *o6 = _attn_bwd(u3, dya3, att_p, att_ps, rider=_exchange_chips_rider(
            [s_gu[3], s_dm[4]], rows=[(quarter, half - quarter), (0, half)], into=[o5[1], None]))
    else:
        drq, drk, drv, drg, rpart = _ret_bwd(u3, y_hat, y_rstd, (ret_rb, ret_kvf), dyr3, lgf_l, lgb_l, gn_gain)
        daq, dakv, spart = _attn_bwd(u3, dya3, att_p, att_ps)
    pieces = [a.reshape(t, -1) for a in (drq, drk, drv, drg, daq, dakv)]
    kv0 = CB_AK * LANES
    w_kv = jnp.concatenate([w_in_t[kv0 + o:kv0 + o + HEAD_DIM] for o in KV_ORDER], axis=0)
    d_in, *o7 = _wgrad_in(pieces, x2d, rider=_exchange_chips_rider(list(s_dm[5:])) if dist else None)
    d_in = d_in.reshape(N_SHARD, FFN_SHARD, D_MODEL)

    rsum = rpart
    lane_heads = lambda row: jnp.sum(row.reshape(4, 2, HEAD_DIM), axis=-1).reshape(8)
    dlg_f = lane_heads(rsum[:, 0, :]) + jnp.stack([jnp.sum(rsum[:, 2, :], -1), jnp.sum(rsum[:, 3, :], -1)], 1).reshape(8)
    dlg_b = lane_heads(rsum[:, 1, :]) + jnp.stack([jnp.sum(rsum[:, 4, :], -1), jnp.sum(rsum[:, 5, :], -1)], 1).reshape(8)
    chain = lambda d: -(math.log(2.0) * jnp.exp2(d)) / (1.0 - jnp.exp2(d))
    grads_small = {
        "ret_decay_fwd": (dlg_f * chain(dec_f)).reshape(1, 8),
        "ret_decay_bwd": (dlg_b * chain(dec_b)).reshape(1, 8),
        "attn_sink": jnp.sum(spart, axis=0)[:, 0:4, 0].reshape(1, 8),
        "ret_gn_gain": rsum[:, 6, :].reshape(1, RET_WIDTH),
        "ln1_gain": dg1, "ln1_bias": db1, "ln2_gain": dg2, "ln2_bias": db2,
    }
    if not dist:
        grad_x, = _inproj_bwd(dz1, pieces, w_in_t, w_kv)
        grads_rest = [misc_parts[0]] + ffn_parts + misc_parts[1:]
        return sq[0, 0], grad_x.reshape(bsz, s, D_MODEL), d_in, grads_rest, grads_small
    *small_out, th_in = _small_allreduce_adamw(_pack_small(grads_small, sq[0, 0]), *small_state,
                                               rider=_exchange_halves_rider([d_in]))
    s_in = _add_halves([d_in], [th_in], pos)
    grad_x, chips_in = _inproj_bwd(dz1, pieces, w_in_t, w_kv, rider=_exchange_chips_rider([s_in[1]]))
    sums32 = [s_in[0], s_dm[1], s_gu[0], s_gu[1], s_dm[0], s_dm[2], s_dm[3]]
    from_chips = [chips_in, o7[0], o5[0], o6[0], o6[1], o7[1], o7[2]]
    return grad_x.reshape(bsz, s, D_MODEL), sums32, from_chips, small_out


BIG_NAMES = ("w_in", "w_out", "w_ffn_gate", "w_ffn_up", "w_ffn_down", "w_ple_proj", "w_ple_gate")
REST_NAMES = BIG_NAMES[1:]
TRANSPOSED = ("w_in", "w_ffn_gate", "w_ffn_up")
WEIGHT_ORDER = ("w_in", "ret_decay_fwd", "ret_decay_bwd", "ret_gn_gain", "attn_sink", "w_out", "ln1_gain",
                "ln1_bias", "w_ffn_gate", "w_ffn_up", "w_ffn_down", "w_ple_proj", "w_ple_gate", "ln2_gain", "ln2_bias")


def _shard_rows(name, a):
    return jnp.swapaxes(a[0], 0, 1) if name in TRANSPOSED else a[0]


def _unshard_rows(name, a):
    return (jnp.swapaxes(a, 0, 1) if name in TRANSPOSED else a)[None]


def _assemble_weights(gathered):
    rows = lambda a: a.reshape(N_SHARD * a.shape[1], a.shape[2])
    same = lambda a: a
    layout = {"w_out": ("w_out", rows), "w_ffn_gate": ("gate4", same), "w_ffn_up": ("up4", same),
              "w_ffn_down": ("down4", same), "w_ple_proj": ("ple_proj", same), "w_ple_gate": ("ple_gate", rows)}
    return {layout[n][0]: layout[n][1](a) for n, a in gathered.items()}


def kernel(x, p, w_in, ret_decay_fwd, ret_decay_bwd, ret_gn_gain, attn_sink, w_out, ln1_gain, ln1_bias, w_ffn_gate, w_ffn_up, w_ffn_down, w_ple_proj, w_ple_gate, ln2_gain, ln2_bias, loss_target, m_w_in, m_ret_decay_fwd, m_ret_decay_bwd, m_ret_gn_gain, m_attn_sink, m_w_out, m_ln1_gain, m_ln1_bias, m_w_ffn_gate, m_w_ffn_up, m_w_ffn_down, m_w_ple_proj, m_w_ple_gate, m_ln2_gain, m_ln2_bias, v_w_in, v_ret_decay_fwd, v_ret_decay_bwd, v_ret_gn_gain, v_attn_sink, v_w_out, v_ln1_gain, v_ln1_bias, v_w_ffn_gate, v_w_ffn_up, v_w_ffn_down, v_w_ple_proj, v_w_ple_gate, v_ln2_gain, v_ln2_bias):
    w = dict(w_in=w_in, ret_decay_fwd=ret_decay_fwd, ret_decay_bwd=ret_decay_bwd, ret_gn_gain=ret_gn_gain,
             attn_sink=attn_sink, w_out=w_out, ln1_gain=ln1_gain, ln1_bias=ln1_bias, w_ffn_gate=w_ffn_gate,
             w_ffn_up=w_ffn_up, w_ffn_down=w_ffn_down, w_ple_proj=w_ple_proj, w_ple_gate=w_ple_gate,
             ln2_gain=ln2_gain, ln2_bias=ln2_bias)
    m = dict(w_in=m_w_in, ret_decay_fwd=m_ret_decay_fwd, ret_decay_bwd=m_ret_decay_bwd, ret_gn_gain=m_ret_gn_gain,
             attn_sink=m_attn_sink, w_out=m_w_out, ln1_gain=m_ln1_gain, ln1_bias=m_ln1_bias, w_ffn_gate=m_w_ffn_gate,
             w_ffn_up=m_w_ffn_up, w_ffn_down=m_w_ffn_down, w_ple_proj=m_w_ple_proj, w_ple_gate=m_w_ple_gate,
             ln2_gain=m_ln2_gain, ln2_bias=m_ln2_bias)
    v = dict(w_in=v_w_in, ret_decay_fwd=v_ret_decay_fwd, ret_decay_bwd=v_ret_decay_bwd, ret_gn_gain=v_ret_gn_gain,
             attn_sink=v_attn_sink, w_out=v_w_out, ln1_gain=v_ln1_gain, ln1_bias=v_ln1_bias, w_ffn_gate=v_w_ffn_gate,
             w_ffn_up=v_w_ffn_up, w_ffn_down=v_w_ffn_down, w_ple_proj=v_w_ple_proj, w_ple_gate=v_w_ple_gate,
             ln2_gain=v_ln2_gain, ln2_bias=v_ln2_bias)
    big = lambda d: [_shard_rows(n, d[n]) for n in BIG_NAMES]
    small = lambda d: {n: d[n] for n in SMALL_NAMES}

    chip = 2 * lax.axis_index("x") + lax.axis_index("y")
    pos = jnp.stack([chip, lax.axis_index("c")]).astype(jnp.int32)

    shards = [a.astype(BF16) for a in big(w)]
    (w_in4,) = _all_gather_weights(shards[:1])
    w_in_t = w_in4.reshape(IN_WIDTH, D_MODEL)
    grad_x, sums32, from_chips, (g_s, d_s, m_s, v_s) = _local_step(
        x, p[0], loss_target, w_in_t, shards[1:], small(w), pos=pos,
        small_state=(_pack_small(small(w)), _pack_small(small(m)), _pack_small(small(v))))
    g_big, d_big, m_big, v_big = _adamw(big(w), _join_halves(_add_chips(sums32, from_chips, pos)), big(m), big(v))
    loss = g_s[4, LOSS_COL] * (0.5 / D_MODEL)

    def tree(bigs, packed):
        out = {n: _unshard_rows(n, a) for n, a in zip(BIG_NAMES, bigs)}
        out.update(_unpack_small(packed))
        return [out[n] for n in WEIGHT_ORDER]

    return (loss, grad_x, *tree(g_big, g_s), *tree(d_big, d_s), *tree(m_big, m_s), *tree(v_big, v_s))
```

```python
import functools
import math

import jax
import jax.numpy as jnp
from jax import lax
from jax.experimental import pallas as pl
from jax.experimental.pallas import tpu as pltpu

F32 = jnp.float32
BF16 = jnp.bfloat16

D_MODEL = 1024
HEAD_DIM = 64
RET_HEADS = 8
ATTN_HEADS = 8
RET_WIDTH = 512
ATTN_WIDTH = 512
KV_WIDTH = 128
IN_WIDTH = 2816
FFN = 2816
N_SHARD = 4
FFN_SHARD = FFN // N_SHARD
PLE_DIM = 256
CHUNK = 128
LANES = 128
ALPHA = 2.0 ** 0.25
LN_EPS = 1e-5
GN_EPS = 1e-5
NEG_INF = -1e30
ADAM_LR = 0.001
ADAM_B1 = 0.9
ADAM_B2 = 0.999
ADAM_EPS = 1e-08
ADAM_WD = 0.01
ADAM_STEP = 10
VMEM_LIMIT = 56 * 1024 * 1024
MESH = pl.DeviceIdType.MESH

CB_RQ, CB_RK, CB_RV, CB_RG, CB_AQ, CB_AK, CB_AV = 0, 4, 8, 12, 16, 20, 21


def _dot(a, b):
    return jnp.dot(a, b, preferred_element_type=F32)


def _dot_nt(a, b):
    return lax.dot_general(a, b, (((1,), (1,)), ((), ())), preferred_element_type=F32)


def _dot_tn(a, b):
    return lax.dot_general(a, b, (((0,), (0,)), ((), ())), preferred_element_type=F32)


def _sigmoid(x):
    return 1.0 / (1.0 + jnp.exp(-x))


def _params(*sem, vmem=None):
    return pltpu.CompilerParams(dimension_semantics=tuple(sem) if sem else None, vmem_limit_bytes=vmem)


class _Rider:
    def __init__(self, ins, out_shapes, sems, start, finish, aliases=None):
        self.ins, self.out_shapes, self.sems = list(ins), list(out_shapes), list(sems)
        self.start, self.finish, self.aliases = start, finish, dict(aliases or {})


def _merge_riders(riders):
    riders = [r for r in riders if r is not None]
    if len(riders) == 1:
        return riders[0]
    bounds, aliases = [], {}
    i0 = o0 = s0 = 0
    for r in riders:
        bounds.append((i0, o0, s0))
        aliases.update({i0 + i: o0 + o for i, o in r.aliases.items()})
        i0, o0, s0 = i0 + len(r.ins), o0 + len(r.out_shapes), s0 + len(r.sems)

    def each(method):
        def run(ins, outs, sems):
            for r, (i, o, s) in zip(riders, bounds):
                getattr(r, method)(ins[i:i + len(r.ins)], outs[o:o + len(r.out_shapes)], sems[s:s + len(r.sems)])
        return run

    return _Rider([a for r in riders for a in r.ins], [a for r in riders for a in r.out_shapes],
                  [a for r in riders for a in r.sems], each("start"), each("finish"), aliases)


def _hosted_call(body, name, grid, in_specs, out_specs, out_shape, scratch_shapes, operands, rider=None,
                 semantics=None):
    n_in, n_out, n_scr = len(in_specs), len(out_specs), len(scratch_shapes)
    if rider is None:
        return pl.pallas_call(
            body, name=name, grid=grid, in_specs=in_specs, out_specs=out_specs, out_shape=out_shape,
            scratch_shapes=scratch_shapes,
            compiler_params=_params(*(semantics or ["parallel"] * len(grid)), vmem=VMEM_LIMIT))(*operands)
    r_in, r_out = len(rider.ins), len(rider.out_shapes)

    def full_body(*refs):
        main_in, rin = refs[:n_in], refs[n_in:n_in + r_in]
        o0 = n_in + r_in
        main_out, rout = refs[o0:o0 + n_out], refs[o0 + n_out:o0 + n_out + r_out]
        s0 = o0 + n_out + r_out
        main_scr, rsem = refs[s0:s0 + n_scr], refs[s0 + n_scr:]
        first = functools.reduce(jnp.logical_and, [pl.program_id(a) == 0 for a in range(len(grid))])
        last = functools.reduce(jnp.logical_and, [pl.program_id(a) == g - 1 for a, g in enumerate(grid)])

        @pl.when(first)
        def _():
            rider.start(rin, rout, rsem)

        body(*main_in, *main_out, *main_scr)

        @pl.when(last)
        def _():
            rider.finish(rin, rout, rsem)

    hbm = pl.BlockSpec(memory_space=pl.ANY)
    return pl.pallas_call(
        full_body, name=name, grid=grid,
        in_specs=list(in_specs) + [hbm] * r_in, out_specs=list(out_specs) + [hbm] * r_out,
        out_shape=list(out_shape) + rider.out_shapes,
        scratch_shapes=list(scratch_shapes) + rider.sems,
        input_output_aliases={n_in + i: n_out + o for i, o in rider.aliases.items()},
        compiler_params=_params(*(["arbitrary"] * len(grid)), vmem=VMEM_LIMIT),
    )(*operands, *rider.ins)


def _loop_grouped(n, body, init, per_trip=2):
    if n % per_trip:
        return lax.fori_loop(0, n, body, init)

    def trip(i, c):
        for j in range(per_trip):
            c = body(per_trip * i + j, c)
        return c

    return lax.fori_loop(0, n // per_trip, trip, init)


def _head_mean(x, m0):
    s0 = jnp.sum(jnp.where(m0, x, 0.0), axis=1, keepdims=True)
    s1 = jnp.sum(jnp.where(m0, 0.0, x), axis=1, keepdims=True)
    return jnp.where(m0, s0, s1) * (1.0 / HEAD_DIM)


def _inproj(x2d, w_in_t, rider=None):
    t = x2d.shape[0]
    tm = 512
    nb = 256

    def body(x_ref, w_ref, o_ref):
        xb = x_ref[...].astype(BF16)
        for n in range(0, IN_WIDTH, nb):
            o_ref[:, n:n + nb] = _dot_nt(xb, w_ref[n:n + nb, :]).astype(BF16)

    return _hosted_call(
        body, "inproj", (t // tm,),
        in_specs=[pl.BlockSpec((tm, D_MODEL), lambda i: (i, 0)),
                  pl.BlockSpec((IN_WIDTH, D_MODEL), lambda i: (0, 0))],
        out_specs=[pl.BlockSpec((tm, IN_WIDTH), lambda i: (i, 0))],
        out_shape=[jax.ShapeDtypeStruct((t, IN_WIDTH), BF16)],
        scratch_shapes=[], operands=(x2d, w_in_t), rider=rider)


def _outproj_ln1(y_ret, y_att, x2d, w_out, gain, bias, rider=None):
    t = x2d.shape[0]
    tm = 512

    def body(yr_ref, ya_ref, x_ref, w_ref, g_ref, b_ref, zh_ref, r_ref, hb_ref):
        mix = _dot(yr_ref[...], w_ref[0:RET_WIDTH, :]) + _dot(ya_ref[...], w_ref[RET_WIDTH:, :])
        z = ALPHA * x_ref[...] + mix
        mu = jnp.mean(z, axis=1, keepdims=True)
        zc = z - mu
        var = jnp.mean(zc * zc, axis=1, keepdims=True)
        r = lax.rsqrt(var + LN_EPS)
        zh = zc * r
        zh_ref[...] = zh
        r_ref[...] = r
        hb_ref[...] = (zh * g_ref[...] + b_ref[...]).astype(BF16)

    row = lambda w: pl.BlockSpec((tm, w), lambda i: (i, 0))
    const = lambda s: pl.BlockSpec(s, lambda i: (0, 0))
    return _hosted_call(
        body, "outproj_ln1", (t // tm,),
        in_specs=[row(RET_WIDTH), row(ATTN_WIDTH), row(D_MODEL), const((D_MODEL, D_MODEL)),
                  const((1, D_MODEL)), const((1, D_MODEL))],
        out_specs=[row(D_MODEL), row(1), row(D_MODEL)],
        out_shape=[jax.ShapeDtypeStruct((t, D_MODEL), F32), jax.ShapeDtypeStruct((t, 1), F32),
                   jax.ShapeDtypeStruct((t, D_MODEL), BF16)],
        scratch_shapes=[], operands=(y_ret, y_att, x2d, w_out, gain, bias), rider=rider)


def _load_resident(step, pairs, sems):
    copies = [pltpu.make_async_copy(src, dst, sems.at[i]) for i, (src, dst) in enumerate(pairs)]

    @pl.when(step == 0)
    def _():
        for cp in copies:
            cp.start()
        for cp in copies:
            cp.wait()


FFN_CHUNK = 256
N_FFN_CHUNK = FFN // FFN_CHUNK


def _resident_quarters(hbm, vmem):
    q = FFN // N_SHARD
    return [(hbm.at[pl.ds(j * q, q), :], vmem.at[pl.ds(j * q, q), :]) for j in range(N_SHARD)]


def _ln2_loss_tail(zh, mixed, tgt, g1, b1, g2, b2):
    z2 = ALPHA * (zh * g1 + b1) + mixed
    mu = jnp.mean(z2, axis=1, keepdims=True)
    zc = z2 - mu
    var = jnp.mean(zc * zc, axis=1, keepdims=True)
    r = lax.rsqrt(var + LN_EPS)
    zh2 = zc * r
    err = zh2 * g2 + b2 - tgt
    dy = err * (1.0 / D_MODEL)
    dzh = dy * g2
    m1 = jnp.mean(dzh, axis=1, keepdims=True)
    m2 = jnp.mean(dzh * zh2, axis=1, keepdims=True)
    dz2 = r * (dzh - m1 - zh2 * m2)
    return dz2, jnp.sum(err * err), jnp.sum(dy * zh2, axis=0, keepdims=True), jnp.sum(dy, axis=0, keepdims=True)


def _ffn_fwd(zh1, hb, p2d, tgt, g1, b1, g2, b2, wg4, wu4, wd4, wpe, wpg):
    t = zh1.shape[0]
    tm = 256
    wg_t, wu_t, wd_all = (w.reshape(FFN, D_MODEL) for w in (wg4, wu4, wd4))

    def body(zh_ref, hb_ref, p_ref, t_ref, g1_ref, b1_ref, g2_ref, b2_ref,
             wg_hbm, wu_hbm, wd_hbm, wpe_hbm, wpg_hbm,
             dz_ref, dzb_ref, gs_ref, us_ref, act_ref, pg_ref, ple_ref, loss_ref, dg2_ref, db2_ref,
             wg, wu, wd, wpe, wpg, wsem):
        step = pl.program_id(0)
        loads = _resident_quarters(wg_hbm, wg) + _resident_quarters(wu_hbm, wu) + _resident_quarters(wd_hbm, wd)
        pc = D_MODEL // N_SHARD
        loads += [(wpe_hbm.at[j], wpe.at[:, pl.ds(j * pc, pc)]) for j in range(N_SHARD)]
        _load_resident(step, loads + [(wpg_hbm, wpg)], wsem)

        @pl.when(step == 0)
        def _():
            loss_ref[...] = jnp.zeros_like(loss_ref)
            dg2_ref[...] = jnp.zeros_like(dg2_ref)
            db2_ref[...] = jnp.zeros_like(db2_ref)

        hbv = hb_ref[...]
        ffn = jnp.zeros((tm, D_MODEL), F32)
        acts = []
        chunks = [slice(n * FFN_CHUNK, (n + 1) * FFN_CHUNK) for n in range(N_FFN_CHUNK)]
        for n in range(N_FFN_CHUNK + 1):
            if n < N_FFN_CHUNK:
                gj = _dot_nt(hbv, wg[chunks[n], :])
                uj = _dot_nt(hbv, wu[chunks[n], :])
                gs_ref[:, chunks[n]] = gj.astype(BF16)
                us_ref[:, chunks[n]] = uj.astype(BF16)
                acts.append((gj * _sigmoid(gj) * uj).astype(BF16))
                act_ref[:, chunks[n]] = acts[n]
            if n > 0:
                ffn = ffn + _dot(acts[n - 1], wd[chunks[n - 1], :])
        ple = _dot(p_ref[...].astype(BF16), wpe[...])
        pg = _sigmoid(_dot(hbv, wpg[...]))
        pg_ref[...] = pg.astype(BF16)
        ple_ref[...] = ple.astype(BF16)
        dz2, sq, dg2, db2 = _ln2_loss_tail(zh_ref[...], ffn + pg * ple, t_ref[...], g1_ref[...], b1_ref[...],
                                           g2_ref[...], b2_ref[...])
        dz_ref[...] = dz2
        dzb_ref[...] = dz2.astype(BF16)
        loss_ref[...] += sq
        dg2_ref[...] += dg2
        db2_ref[...] += db2

    row = lambda w: pl.BlockSpec((tm, w), lambda i: (i, 0))
    const = lambda s: pl.BlockSpec(s, lambda i: (0, 0))
    hid_shape = jax.ShapeDtypeStruct((t, FFN), BF16)
    hbm = pl.BlockSpec(memory_space=pl.ANY)
    return pl.pallas_call(
        body, name="ffn_fwd", grid=(t // tm,),
        in_specs=[row(D_MODEL), row(D_MODEL), row(PLE_DIM), row(D_MODEL),
                  const((1, D_MODEL)), const((1, D_MODEL)), const((1, D_MODEL)), const((1, D_MODEL)),
                  hbm, hbm, hbm, hbm, hbm],
        out_specs=[row(D_MODEL), row(D_MODEL), row(FFN), row(FFN), row(FFN), row(D_MODEL), row(D_MODEL),
                   const((8, LANES)), const((1, D_MODEL)), const((1, D_MODEL))],
        out_shape=[jax.ShapeDtypeStruct((t, D_MODEL), F32), jax.ShapeDtypeStruct((t, D_MODEL), BF16),
                   hid_shape, hid_shape, hid_shape,
                   jax.ShapeDtypeStruct((t, D_MODEL), BF16), jax.ShapeDtypeStruct((t, D_MODEL), BF16),
                   jax.ShapeDtypeStruct((8, LANES), F32),
                   jax.ShapeDtypeStruct((1, D_MODEL), F32), jax.ShapeDtypeStruct((1, D_MODEL), F32)],
        scratch_shapes=[pltpu.VMEM((FFN, D_MODEL), BF16), pltpu.VMEM((FFN, D_MODEL), BF16),
                        pltpu.VMEM((FFN, D_MODEL), BF16),
                        pltpu.VMEM((PLE_DIM, D_MODEL), BF16), pltpu.VMEM(wpg.shape, BF16),
                        pltpu.SemaphoreType.DMA((4 * N_SHARD + 1,))],
        compiler_params=_params("arbitrary", vmem=VMEM_LIMIT),
    )(zh1, hb, p2d, tgt, g1, b1, g2, b2, wg_t, wu_t, wd_all, wpe, wpg)


def _ret_tables(lgf, lgb):
    c = CHUNK
    row = lax.broadcasted_iota(jnp.int32, (c, LANES), 0).astype(F32)
    ii = lax.broadcasted_iota(jnp.int32, (c, c), 0).astype(F32)
    jj = lax.broadcasted_iota(jnp.int32, (c, c), 1).astype(F32)
    diff = ii - jj
    dmats = []
    for h in range(2):
        lf = lgf[:, h * HEAD_DIM:h * HEAD_DIM + 1]
        lb = lgb[:, h * HEAD_DIM:h * HEAD_DIM + 1]
        dmats.append(jnp.where(diff > 0, jnp.exp(lf * jnp.maximum(diff, 0.0)),
                               jnp.where(diff < 0, jnp.exp(lb * jnp.maximum(-diff, 0.0)), 2.0)))
    tab = dict(
        qdec_f=jnp.exp(lgf * (row + 1.0)), kdec_f=jnp.exp(lgf * (c - 1.0 - row)),
        qdec_b=jnp.exp(lgb * (c - row)), kdec_b=jnp.exp(lgb * row),
        cdec_f=jnp.exp(lgf * c), cdec_b=jnp.exp(lgb * c),
        d0=dmats[0], d1=dmats[1], row=row, diff=diff)
    r = lax.broadcasted_iota(jnp.int32, (LANES, LANES), 0) < HEAD_DIM
    cc = lax.broadcasted_iota(jnp.int32, (LANES, LANES), 1) < HEAD_DIM
    tab["bd"] = r == cc
    tab["m0"] = lax.broadcasted_iota(jnp.int32, (c, LANES), 1) < HEAD_DIM
    return tab


def _ret_specs(bsz, s):
    blk = lambda cb: pl.BlockSpec((bsz, s, LANES), lambda p, cb=cb: (0, 0, cb + p))
    lane = pl.BlockSpec((None, 1, LANES), lambda p: (p, 0, 0))
    gain = pl.BlockSpec((1, LANES), lambda p: (0, p))
    pair = pl.BlockSpec((bsz, s, LANES), lambda p: (0, 0, p))
    return blk, lane, gain, pair


def _ret_state_spec(bsz, n_chunk):
    spec = pl.BlockSpec((None, bsz, n_chunk, LANES, LANES), lambda p: (p, 0, 0, 0, 0))
    return spec, jax.ShapeDtypeStruct((4, bsz, n_chunk, LANES, LANES), F32)


def _ret_kv_states(tb, k_ref, v_ref, rb_ref, kvf_ref, n_chunk):
    c = CHUNK
    bsz = k_ref.shape[0]
    bd = tb["bd"]

    def contributions(n, carry):
        sl = pl.ds(pl.multiple_of(n * c, c), c)
        kfb = []
        for b in range(bsz):
            k32 = k_ref[b, sl, :].astype(F32)
            kfb.append(jnp.concatenate([k32 * tb["kdec_f"], k32 * tb["kdec_b"]], axis=1).astype(BF16))
        kvs = [_dot_tn(kfb[b], v_ref[b, sl, :]) for b in range(bsz)]
        for b in range(bsz):
            kvf_ref[b, n] = jnp.where(bd, kvs[b][0:LANES], 0.0)
            rb_ref[b, n] = jnp.where(bd, kvs[b][LANES:], 0.0)
        return carry

    lax.fori_loop(0, n_chunk, contributions, 0, unroll=2)

    def recur(i, rbs):
        n = n_chunk - 1 - i
        new = []
        for b in range(bsz):
            own = rb_ref[b, n]
            rb_ref[b, n] = rbs[b]
            new.append(rbs[b] * tb["cdec_b"] + own)
        return tuple(new)

    lax.fori_loop(0, n_chunk, recur, tuple(jnp.zeros((LANES, LANES), F32) for _ in range(bsz)))


def _split_rows(x, m0):
    return jnp.concatenate([jnp.where(m0, x, 0.0), jnp.where(m0, 0.0, x)], axis=0).astype(BF16)


def _ret_fwd(u3, lgf_l, lgb_l, gn_gain, rider=None):
    bsz, s, _ = u3.shape
    n_chunk = s // CHUNK
    c = CHUNK

    def body(q_ref, k_ref, v_ref, g_ref, lgf_ref, lgb_ref, gain_ref, yh_ref, rstd_ref, o_ref, rb_ref, kvf_ref):
        tb = _ret_tables(lgf_ref[...], lgb_ref[...])
        m0 = tb["m0"]
        gain = gain_ref[...]
        rows = range(bsz)
        _ret_kv_states(tb, k_ref, v_ref, rb_ref, kvf_ref, n_chunk)

        def chunk(n, rfs):
            sl = pl.ds(pl.multiple_of(n * c, c), c)
            qs = [q_ref[b, sl, :].astype(F32) * 0.125 for b in rows]
            s01 = [_dot_nt(_split_rows(qs[b], m0), k_ref[b, sl, :]) for b in rows]
            ys = []
            for b in rows:
                lhs = jnp.concatenate([s01[b][0:c] * tb["d0"], s01[b][c:] * tb["d1"],
                                       qs[b] * tb["qdec_f"], qs[b] * tb["qdec_b"]], axis=1).astype(BF16)
                rhs = jnp.concatenate([_split_rows(v_ref[b, sl, :].astype(F32), m0),
                                       rfs[b].astype(BF16), rb_ref[b, n].astype(BF16)], axis=0)
                ys.append(_dot(lhs, rhs))
            new = []
            for b in rows:
                y = ys[b]
                mu = _head_mean(y, m0)
                yc = y - mu
                rstd = lax.rsqrt(_head_mean(yc * yc, m0) + GN_EPS)
                yh = yc * rstd
                g = g_ref[b, sl, :].astype(F32)
                yh_ref[b, sl, :] = yh
                rstd_ref[b, sl, :] = rstd
                o_ref[b, sl, :] = (yh * gain * (g * _sigmoid(g))).astype(BF16)
                new.append(rfs[b] * tb["cdec_f"] + kvf_ref[b, n])
            return tuple(new)

        _loop_grouped(n_chunk, chunk, tuple(jnp.zeros((LANES, LANES), F32) for _ in rows))

    blk, lane, gain, pair = _ret_specs(bsz, s)
    state, state_shape = _ret_state_spec(bsz, n_chunk)
    return _hosted_call(
        body, "ret_fwd", (4,),
        in_specs=[blk(CB_RQ), blk(CB_RK), blk(CB_RV), blk(CB_RG), lane, lane, gain],
        out_specs=[pair, pair, pair, state, state],
        out_shape=[jax.ShapeDtypeStruct((bsz, s, RET_WIDTH), F32), jax.ShapeDtypeStruct((bsz, s, RET_WIDTH), F32),
                   jax.ShapeDtypeStruct((bsz, s, RET_WIDTH), BF16), state_shape, state_shape],
        scratch_shapes=[],
        operands=(u3, u3, u3, u3, lgf_l, lgb_l, gn_gain), rider=rider)


def _ret_bwd(u3, y_hat, y_rstd, states, d_o, lgf_l, lgb_l, gn_gain, rider=None):
    bsz, s, _ = u3.shape
    n_chunk = s // CHUNK
    c = CHUNK

    def body(q_ref, k_ref, v_ref, g_ref, yh_ref, rstd_ref, do_ref, lgf_ref, lgb_ref, gain_ref, rb_ref, kvf_ref,
             dq_ref, dk_ref, dv_ref, dg_ref, part_ref,
             rf_ref, dirf_ref, dy_ref, dk_acc, dv_acc, pa0, pa1, vec_ref):
        tb = _ret_tables(lgf_ref[...], lgb_ref[...])
        m0, bd, row = tb["m0"], tb["bd"], tb["row"]
        gain = gain_ref[...]
        wf = jnp.maximum(tb["diff"], 0.0)
        wb = jnp.maximum(-tb["diff"], 0.0)
        rows = range(bsz)
        zero_states = tuple(jnp.zeros((LANES, LANES), F32) for _ in rows)
        for ref in (pa0, pa1):
            ref[...] = jnp.zeros_like(ref)
        vec_ref[...] = jnp.zeros_like(vec_ref)

        def sweep_fwd(n, carry):
            rfs, gbs = carry
            sl = pl.ds(pl.multiple_of(n * c, c), c)
            qs, ks, vs, dys, dybs, q01, k01, dy01 = [], [], [], [], [], [], [], []
            dgain = jnp.zeros((1, LANES), F32)
            for b in rows:
                q = q_ref[b, sl, :].astype(F32) * 0.125
                k = k_ref[b, sl, :]
                yh = yh_ref[b, sl, :]
                rstd = rstd_ref[b, sl, :]
                do = do_ref[b, sl, :].astype(F32)
                g = g_ref[b, sl, :].astype(F32)
                sg = _sigmoid(g)
                sil = g * sg
                dyh = do * gain * sil
                dg_ref[b, sl, :] = (do * yh * gain * sg * (1.0 + g * (1.0 - sg))).astype(BF16)
                dgain = dgain + jnp.sum(do * yh * sil, axis=0, keepdims=True)
                dy = rstd * (dyh - _head_mean(dyh, m0) - yh * _head_mean(dyh * yh, m0))
                dyb = dy.astype(BF16)
                dy_ref[b, sl, :] = dyb
                rf_ref[b, n] = rfs[b]
                qs.append(q)
                ks.append(k)
                vs.append(v_ref[b, sl, :])
                dys.append(dy)
                dybs.append(dyb)
                q01.append(_split_rows(q, m0))
                k01.append(_split_rows(k.astype(F32), m0))
                dy01.append(_split_rows(dy, m0))
            s01 = [_dot_nt(q01[b], ks[b]) for b in rows]
            da01 = [_dot_nt(dy01[b], vs[b]) for b in rows]
            rbn = [rb_ref[b, n] for b in rows]
            states = [jnp.concatenate([rfs[b], rbn[b]], axis=0).astype(BF16) for b in rows]
            dqc = [_dot_nt(dybs[b], states[b]) for b in rows]
            gbb = [gbs[b].astype(BF16) for b in rows]
            dkb = [_dot_nt(vs[b], gbb[b]) for b in rows]
            qfb = [jnp.concatenate([qs[b] * tb["qdec_f"], qs[b] * tb["qdec_b"]], axis=1) for b in rows]
            direct = [_dot_tn(qfb[b].astype(BF16), dybs[b]) for b in rows]
            ds_cat, ds_rows, a_rows = [], [], []
            for b in rows:
                a0 = s01[b][0:c] * tb["d0"]
                a1 = s01[b][c:] * tb["d1"]
                pa0[...] += da01[b][0:c] * a0
                pa1[...] += da01[b][c:] * a1
                ds0 = da01[b][0:c] * tb["d0"]
                ds1 = da01[b][c:] * tb["d1"]
                ds_cat.append(jnp.concatenate([ds0, ds1], axis=1).astype(BF16))
                ds_rows.append(jnp.concatenate([ds0, ds1], axis=0).astype(BF16))
                a_rows.append(jnp.concatenate([a0, a1], axis=0).astype(BF16))
            kbd = [ks[b].astype(F32) * tb["kdec_b"] for b in rows]
            dq_in = [_dot(ds_cat[b], k01[b]) for b in rows]
            dk_in = [_dot_tn(ds_rows[b], q01[b]) for b in rows]
            dv_in = [_dot_tn(a_rows[b], dy01[b]) for b in rows]
            dv_gb = [_dot(kbd[b].astype(BF16), gbb[b]) for b in rows]
            new_rf, new_gb = [], []
            dlf = jnp.zeros((1, LANES), F32)
            dlb = jnp.zeros((1, LANES), F32)
            for b in rows:
                dqf, dqb = dqc[b][:, 0:LANES], dqc[b][:, LANES:]
                qf, qb = qfb[b][:, 0:LANES], qfb[b][:, LANES:]
                dq = dq_in[b] + dqf * tb["qdec_f"] + dqb * tb["qdec_b"]
                dq_ref[b, sl, :] = (dq * 0.125).astype(BF16)
                dk_acc[b, sl, :] = dk_in[b] + dkb[b] * tb["kdec_b"]
                dv_acc[b, sl, :] = dv_in[b] + dv_gb[b]
                dlf = dlf + jnp.sum((row + 1.0) * qf * dqf, axis=0, keepdims=True)
                dlb = dlb + jnp.sum((c - row) * qb * dqb + row * kbd[b] * dkb[b], axis=0, keepdims=True)
                dlb = dlb + c * tb["cdec_b"] * jnp.sum(gbs[b] * rbn[b], axis=0, keepdims=True)
                dirf_ref[b, n] = jnp.where(bd, direct[b][0:LANES], 0.0)
                new_gb.append(jnp.where(bd, direct[b][LANES:], 0.0) + tb["cdec_b"] * gbs[b])
                new_rf.append(rfs[b] * tb["cdec_f"] + kvf_ref[b, n])
            vec_ref[0:1, :] += dlf
            vec_ref[1:2, :] += dlb
            vec_ref[6:7, :] += dgain
            return tuple(new_rf), tuple(new_gb)

        _loop_grouped(n_chunk, sweep_fwd, (zero_states, zero_states), per_trip=4)

        def sweep_bwd(i, gfs):
            n = n_chunk - 1 - i
            sl = pl.ds(pl.multiple_of(n * c, c), c)
            gfb = [gfs[b].astype(BF16) for b in rows]
            kfd = [k_ref[b, sl, :].astype(F32) * tb["kdec_f"] for b in rows]
            dkf = [_dot_nt(v_ref[b, sl, :], gfb[b]) for b in rows]
            dvf = [_dot(kfd[b].astype(BF16), gfb[b]) for b in rows]
            new = []
            dlf = jnp.zeros((1, LANES), F32)
            for b in rows:
                dk_ref[b, sl, :] = (dk_acc[b, sl, :] + dkf[b] * tb["kdec_f"]).astype(BF16)
                dv_ref[b, sl, :] = (dv_acc[b, sl, :] + dvf[b]).astype(BF16)
                dlf = dlf + jnp.sum((c - 1.0 - row) * kfd[b] * dkf[b], axis=0, keepdims=True)
                dlf = dlf + c * tb["cdec_f"] * jnp.sum(gfs[b] * rf_ref[b, n], axis=0, keepdims=True)
                new.append(dirf_ref[b, n] + tb["cdec_f"] * gfs[b])
            vec_ref[0:1, :] += dlf
            return tuple(new)

        _loop_grouped(n_chunk, sweep_bwd, zero_states)
        vec_ref[2:3, :] = jnp.sum(pa0[...] * wf, axis=0, keepdims=True)
        vec_ref[3:4, :] = jnp.sum(pa1[...] * wf, axis=0, keepdims=True)
        vec_ref[4:5, :] = jnp.sum(pa0[...] * wb, axis=0, keepdims=True)
        vec_ref[5:6, :] = jnp.sum(pa1[...] * wb, axis=0, keepdims=True)
        part_ref[...] = vec_ref[...]

    blk, lane, gain, pair = _ret_specs(bsz, s)
    out_bf = jax.ShapeDtypeStruct((bsz, s, RET_WIDTH), BF16)
    state = pltpu.VMEM((bsz, n_chunk, LANES, LANES), F32)
    saved = _ret_state_spec(bsz, n_chunk)[0]
    return _hosted_call(
        body, "ret_bwd", (4,),
        in_specs=[blk(CB_RQ), blk(CB_RK), blk(CB_RV), blk(CB_RG), pair, pair, pair, lane, lane, gain, saved, saved],
        out_specs=[pair, pair, pair, pair, pl.BlockSpec((None, 8, LANES), lambda p: (p, 0, 0))],
        out_shape=[out_bf, out_bf, out_bf, out_bf, jax.ShapeDtypeStruct((4, 8, LANES), F32)],
        scratch_shapes=[state, state,
                        pltpu.VMEM((bsz, s, LANES), BF16), pltpu.VMEM((bsz, s, LANES), F32),
                        pltpu.VMEM((bsz, s, LANES), F32),
                        pltpu.VMEM((c, c), F32), pltpu.VMEM((c, c), F32), pltpu.VMEM((8, LANES), F32)],
        operands=(u3, u3, u3, u3, y_hat, y_rstd, d_o, lgf_l, lgb_l, gn_gain, *states), rider=rider)


def _attn_window_tables(n, s):
    qi = lax.broadcasted_iota(jnp.int32, (CHUNK, 3 * CHUNK), 0)
    kj = lax.broadcasted_iota(jnp.int32, (CHUNK, 3 * CHUNK), 1)
    dist = jnp.abs(kj - CHUNK - qi)
    kpos = n * CHUNK - CHUNK + kj
    valid = (dist <= CHUNK) & (kpos >= 0) & (kpos < s)
    return dist.astype(F32), valid


def _dup_kv_head(x, g):
    lane = lax.broadcasted_iota(jnp.int32, x.shape, 1)
    keep = (lane < HEAD_DIM) == (g == 0)
    xf = x.astype(F32)
    return jnp.where(keep, xf, pltpu.roll(xf, HEAD_DIM, 1))


def _attn_specs(s):
    q = pl.BlockSpec((None, s, 2 * LANES), lambda b, g: (b, 0, CB_AQ // 2 + g))
    k = pl.BlockSpec((None, s, LANES), lambda b, g: (b, 0, CB_AK))
    v = pl.BlockSpec((None, s, LANES), lambda b, g: (b, 0, CB_AV))
    grp = pl.BlockSpec((None, s, 2 * LANES), lambda b, g: (b, 0, g))
    smem = pl.BlockSpec(memory_space=pltpu.SMEM)
    return q, k, v, grp, smem


def _fill_padded(dst_ref, val, s):
    dst_ref[0:CHUNK, :] = jnp.zeros((CHUNK, LANES), dst_ref.dtype)
    dst_ref[CHUNK:CHUNK + s, :] = val.astype(dst_ref.dtype)
    dst_ref[CHUNK + s:2 * CHUNK + s, :] = jnp.zeros((CHUNK, LANES), dst_ref.dtype)


def _attn_probs(sc, slope, snk, dist, valid):
    sc = jnp.where(valid, sc - slope * dist, NEG_INF)
    m = jnp.maximum(jnp.max(sc, axis=1, keepdims=True), snk)
    e = jnp.exp(sc - m)
    es = jnp.exp(snk - m)
    inv = 1.0 / (jnp.sum(e, axis=1, keepdims=True) + es)
    return e * inv, es * inv


def _stack_heads(x2, m0):
    parts = []
    for pr in range(2):
        xp = x2[:, pr * LANES:(pr + 1) * LANES]
        parts += [jnp.where(m0, xp, 0.0), jnp.where(m0, 0.0, xp)]
    return jnp.concatenate(parts, axis=0).astype(BF16)


def _unstack_pair(x_all, pr, m0):
    return jnp.where(m0, x_all[(2 * pr) * CHUNK:(2 * pr + 1) * CHUNK], x_all[(2 * pr + 1) * CHUNK:(2 * pr + 2) * CHUNK])


def _attn_saved_specs(bsz, n_blk):
    specs = [pl.BlockSpec((None, None, n_blk, 4 * CHUNK, w), lambda b, g: (b, g, 0, 0, 0)) for w in (3 * CHUNK, 1)]
    shapes = [jax.ShapeDtypeStruct((bsz, 2, n_blk, 4 * CHUNK, 3 * CHUNK), BF16),
              jax.ShapeDtypeStruct((bsz, 2, n_blk, 4 * CHUNK, 1), F32)]
    return specs, shapes


def _attn_fwd(u3, slopes, sink, rider=None):
    bsz, s, _ = u3.shape
    n_blk = s // CHUNK

    def body(slope_ref, sink_ref, q_ref, k_ref, v_ref, o_ref, p_ref, ps_ref, kp_ref, vp_ref):
        g = pl.program_id(1)
        _fill_padded(kp_ref, _dup_kv_head(k_ref[...], g), s)
        _fill_padded(vp_ref, _dup_kv_head(v_ref[...], g), s)
        m0 = lax.broadcasted_iota(jnp.int32, (CHUNK, LANES), 1) < HEAD_DIM

        def blk(n, carry):
            r0 = pl.multiple_of(n * CHUNK, CHUNK)
            kw = kp_ref[pl.ds(r0, 3 * CHUNK), :]
            vw = vp_ref[pl.ds(r0, 3 * CHUNK), :]
            dist, valid = _attn_window_tables(n, s)
            q_all = _stack_heads(q_ref[pl.ds(r0, CHUNK), :].astype(F32) * 0.125, m0)
            sc_all = _dot_nt(q_all, kw)
            probs, sinks = [], []
            for i in range(4):
                p, ps = _attn_probs(sc_all[i * CHUNK:(i + 1) * CHUNK], slope_ref[g * 4 + i], sink_ref[g * 4 + i],
                                    dist, valid)
                probs.append(p.astype(BF16))
                sinks.append(ps)
            p_all = jnp.concatenate(probs, axis=0)
            p_ref[n] = p_all
            ps_ref[n] = jnp.concatenate(sinks, axis=0)
            out_all = _dot(p_all, vw)
            for pr in range(2):
                o_ref[pl.ds(r0, CHUNK), pr * LANES:(pr + 1) * LANES] = _unstack_pair(out_all, pr, m0).astype(BF16)
            return carry

        lax.fori_loop(0, n_blk, blk, 0, unroll=4)

    q, k, v, grp, smem = _attn_specs(s)
    saved_specs, saved_shapes = _attn_saved_specs(bsz, n_blk)
    return _hosted_call(
        body, "attn_fwd", (bsz, 2),
        in_specs=[smem, smem, q, k, v],
        out_specs=[grp] + saved_specs,
        out_shape=[jax.ShapeDtypeStruct((bsz, s, ATTN_WIDTH), BF16)] + saved_shapes,
        scratch_shapes=[pltpu.VMEM((s + 2 * CHUNK, LANES), BF16), pltpu.VMEM((s + 2 * CHUNK, LANES), BF16)],
        operands=(slopes, sink, u3, u3, u3), rider=rider)


def _attn_bwd(u3, d_o, probs, sink_probs, rider=None):
    bsz, s, _ = u3.shape
    n_blk = s // CHUNK

    def body(q_ref, k_ref, v_ref, do_ref, p_ref, ps_ref, dq_ref, dkv_ref, ds_ref,
             kp_ref, vp_ref, dk_acc, dv_acc):
        g = pl.program_id(1)
        _fill_padded(kp_ref, _dup_kv_head(k_ref[...], g), s)
        _fill_padded(vp_ref, _dup_kv_head(v_ref[...], g), s)
        dk_acc[...] = jnp.zeros_like(dk_acc)
        dv_acc[...] = jnp.zeros_like(dv_acc)
        m0 = lax.broadcasted_iota(jnp.int32, (CHUNK, LANES), 1) < HEAD_DIM

        def blk(n, dsink):
            r0 = pl.multiple_of(n * CHUNK, CHUNK)
            win = pl.ds(r0, 3 * CHUNK)
            kw = kp_ref[win, :]
            vw = vp_ref[win, :]
            q_all = _stack_heads(q_ref[pl.ds(r0, CHUNK), :].astype(F32) * 0.125, m0)
            do_all = _stack_heads(do_ref[pl.ds(r0, CHUNK), :].astype(F32), m0)
            p_all = p_ref[n]
            ps_all = ps_ref[n]
            dp_all = _dot_nt(do_all, vw)
            new_dsink, dscs = [], []
            for i in range(4):
                rows = slice(i * CHUNK, (i + 1) * CHUNK)
                p = p_all[rows].astype(F32)
                dp = dp_all[rows]
                delta = jnp.sum(p * dp, axis=1, keepdims=True)
                dscs.append((p * (dp - delta)).astype(BF16))
                dsh = jnp.sum(ps_all[rows] * delta, axis=0, keepdims=True)
                new_dsink.append(dsink[i] - jnp.broadcast_to(dsh, (1, LANES)))
            dsc_all = jnp.concatenate(dscs, axis=0)
            dq_all = _dot(dsc_all, kw)
            dk_acc[win, :] += _dot_tn(dsc_all, q_all)
            dv_acc[win, :] += _dot_tn(p_all, do_all)
            for pr in range(2):
                dq_ref[pl.ds(r0, CHUNK), pr * LANES:(pr + 1) * LANES] = (
                    _unstack_pair(dq_all, pr, m0) * 0.125).astype(BF16)
            return tuple(new_dsink)

        dsink = _loop_grouped(n_blk, blk, tuple(jnp.zeros((1, LANES), F32) for _ in range(4)), per_trip=4)
        dk = dk_acc[CHUNK:CHUNK + s, :]
        dv = dv_acc[CHUNK:CHUNK + s, :]
        lane = lax.broadcasted_iota(jnp.int32, (s, LANES), 1)
        fold = lambda a: a + pltpu.roll(a, HEAD_DIM, 1)
        dkv_ref[...] = jnp.where(lane < HEAD_DIM, fold(dk), fold(dv)).astype(BF16)
        ds_ref[...] = jnp.zeros_like(ds_ref)
        for i in range(4):
            ds_ref[i:i + 1, :] = dsink[i]

    q, k, v, grp, _ = _attn_specs(s)
    return _hosted_call(
        body, "attn_bwd", (bsz, 2),
        in_specs=[q, k, v, grp] + _attn_saved_specs(bsz, n_blk)[0],
        out_specs=[grp, pl.BlockSpec((None, s, LANES), lambda b, g: (b, 0, g)),
                   pl.BlockSpec((None, None, 8, LANES), lambda b, g: (b, g, 0, 0))],
        out_shape=[jax.ShapeDtypeStruct((bsz, s, ATTN_WIDTH), BF16), jax.ShapeDtypeStruct((bsz, s, 2 * LANES), BF16),
                   jax.ShapeDtypeStruct((bsz, 2, 8, LANES), F32)],
        scratch_shapes=[pltpu.VMEM((s + 2 * CHUNK, LANES), BF16), pltpu.VMEM((s + 2 * CHUNK, LANES), BF16),
                        pltpu.VMEM((s + 2 * CHUNK, LANES), F32), pltpu.VMEM((s + 2 * CHUNK, LANES), F32)],
        operands=(u3, u3, u3, d_o, probs, sink_probs), rider=rider)


def _ffn_bwd(dz2, gs, us, pg, ple, zh1, r1, g1, wg4, wu4, wd4, wpg, w_out):
    t = dz2.shape[0]
    tm = 256
    wg_t, wu_t, wd_all = (w.reshape(FFN, D_MODEL) for w in (wg4, wu4, wd4))

    def body(dz_ref, gs_ref, us_ref, pg_ref, ple_ref, zh_ref, r_ref, g1_ref,
             wg_hbm, wu_hbm, wd_hbm, wpg_hbm, wo_hbm,
             dgs_ref, dus_ref, dsp_ref, dple_ref, dz1_ref, dyr_ref, dya_ref, dg1_ref, db1_ref,
             wg, wu, wd, wpg, wo, wsem):
        step = pl.program_id(0)
        loads = _resident_quarters(wd_hbm, wd) + _resident_quarters(wg_hbm, wg) + _resident_quarters(wu_hbm, wu)
        _load_resident(step, loads + [(wpg_hbm, wpg), (wo_hbm, wo)], wsem)

        @pl.when(step == 0)
        def _():
            dg1_ref[...] = jnp.zeros_like(dg1_ref)
            db1_ref[...] = jnp.zeros_like(db1_ref)

        dz = dz_ref[...]
        dzb = dz.astype(BF16)
        dh = ALPHA * dz
        pending = []
        chunks = [slice(n * FFN_CHUNK, (n + 1) * FFN_CHUNK) for n in range(N_FFN_CHUNK)]
        for n in range(N_FFN_CHUNK + 1):
            if n < N_FFN_CHUNK:
                da = _dot_nt(dzb, wd[chunks[n], :])
                gj = gs_ref[:, chunks[n]].astype(F32)
                uj = us_ref[:, chunks[n]].astype(F32)
                sg = _sigmoid(gj)
                dgj = (da * uj * sg * (1.0 + gj * (1.0 - sg))).astype(BF16)
                duj = (da * gj * sg).astype(BF16)
                dgs_ref[:, chunks[n]] = dgj
                dus_ref[:, chunks[n]] = duj
                pending.append((dgj, duj))
            if n > 0:
                dgp, dup = pending[n - 1]
                dh = dh + _dot(dgp, wg[chunks[n - 1], :]) + _dot(dup, wu[chunks[n - 1], :])
        pgv = pg_ref[...].astype(F32)
        plev = ple_ref[...].astype(F32)
        dple_ref[...] = (dz * pgv).astype(BF16)
        dsp = (dz * plev * pgv * (1.0 - pgv)).astype(BF16)
        dsp_ref[...] = dsp
        dh = dh + _dot_nt(dsp, wpg[...])
        zh = zh_ref[...]
        dg1_ref[...] += jnp.sum(dh * zh, axis=0, keepdims=True)
        db1_ref[...] += jnp.sum(dh, axis=0, keepdims=True)
        dzh = dh * g1_ref[...]
        m1 = jnp.mean(dzh, axis=1, keepdims=True)
        m2 = jnp.mean(dzh * zh, axis=1, keepdims=True)
        dz1 = r_ref[...] * (dzh - m1 - zh * m2)
        dz1_ref[...] = dz1
        dyc = _dot_nt(dz1.astype(BF16), wo[...])
        dyr_ref[...] = dyc[:, 0:RET_WIDTH].astype(BF16)
        dya_ref[...] = dyc[:, RET_WIDTH:].astype(BF16)

    row = lambda w: pl.BlockSpec((tm, w), lambda i: (i, 0))
    const = lambda s: pl.BlockSpec(s, lambda i: (0, 0))
    hbm = pl.BlockSpec(memory_space=pl.ANY)
    hid_shape = jax.ShapeDtypeStruct((t, FFN), BF16)
    return pl.pallas_call(
        body, name="ffn_bwd", grid=(t // tm,),
        in_specs=[row(D_MODEL), row(FFN), row(FFN), row(D_MODEL), row(D_MODEL), row(D_MODEL), row(1),
                  const((1, D_MODEL)), hbm, hbm, hbm, hbm, hbm],
        out_specs=[row(FFN), row(FFN), row(D_MODEL), row(D_MODEL), row(D_MODEL), row(RET_WIDTH), row(ATTN_WIDTH),
                   const((1, D_MODEL)), const((1, D_MODEL))],
        out_shape=[hid_shape, hid_shape, jax.ShapeDtypeStruct((t, D_MODEL), BF16),
                   jax.ShapeDtypeStruct((t, D_MODEL), BF16), jax.ShapeDtypeStruct((t, D_MODEL), F32),
                   jax.ShapeDtypeStruct((t, RET_WIDTH), BF16), jax.ShapeDtypeStruct((t, ATTN_WIDTH), BF16),
                   jax.ShapeDtypeStruct((1, D_MODEL), F32), jax.ShapeDtypeStruct((1, D_MODEL), F32)],
        scratch_shapes=[pltpu.VMEM((FFN, D_MODEL), BF16), pltpu.VMEM((FFN, D_MODEL), BF16),
                        pltpu.VMEM((FFN, D_MODEL), BF16),
                        pltpu.VMEM(wpg.shape, BF16), pltpu.VMEM(w_out.shape, BF16),
                        pltpu.SemaphoreType.DMA((3 * N_SHARD + 2,))],
        compiler_params=_params("arbitrary", vmem=VMEM_LIMIT),
    )(dz2, gs, us, pg, ple, zh1, r1, g1, wg_t, wu_t, wd_all, wpg, w_out)


def _wgrad_misc(y_ret, y_att, dz1, hb, dsp, p2d, dple, rider=None):
    t = dz1.shape[0]
    tk = min(t, 512)
    pc = D_MODEL // N_SHARD

    def body(yr_ref, ya_ref, dz_ref, hb_ref, dsp_ref, p_ref, dple_ref, wo_ref, wpg_ref, wpe_ref):
        @pl.when(pl.program_id(0) == 0)
        def _():
            wo_ref[...] = jnp.zeros_like(wo_ref)
            wpg_ref[...] = jnp.zeros_like(wpg_ref)
            wpe_ref[...] = jnp.zeros_like(wpe_ref)

        dzb = dz_ref[...].astype(BF16)
        wo_ref[0:RET_WIDTH, :] += _dot_tn(yr_ref[...], dzb)
        wo_ref[RET_WIDTH:, :] += _dot_tn(ya_ref[...], dzb)
        wpg_ref[...] += _dot_tn(hb_ref[...], dsp_ref[...])
        dpe = _dot_tn(p_ref[...].astype(BF16), dple_ref[...])
        for j in range(N_SHARD):
            wpe_ref[j] += dpe[:, j * pc:(j + 1) * pc]

    row = lambda w: pl.BlockSpec((tk, w), lambda k: (k, 0))
    const = lambda s: pl.BlockSpec(s, lambda k: (0,) * len(s))
    return _hosted_call(
        body, "wgrad_misc", (t // tk,),
        in_specs=[row(RET_WIDTH), row(ATTN_WIDTH), row(D_MODEL), row(D_MODEL), row(D_MODEL), row(PLE_DIM),
                  row(D_MODEL)],
        out_specs=[const((D_MODEL, D_MODEL)), const((D_MODEL, D_MODEL)), const((N_SHARD, PLE_DIM, pc))],
        out_shape=[jax.ShapeDtypeStruct((D_MODEL, D_MODEL), F32), jax.ShapeDtypeStruct((D_MODEL, D_MODEL), F32),
                   jax.ShapeDtypeStruct((N_SHARD, PLE_DIM, pc), F32)],
        scratch_shapes=[], operands=(y_ret, y_att, dz1, hb, dsp, p2d, dple), rider=rider, semantics=["arbitrary"])


def _wgrad_ffn(acts, dgs, dus, hb, dz2b):
    t = dz2b.shape[0]
    tk = min(t, 512)
    nk = t // tk

    def body(act_ref, dg_ref, du_ref, hb_ref, dz_ref, og_ref, ou_ref, od_ref):
        @pl.when(pl.program_id(1) == 0)
        def _():
            og_ref[...] = jnp.zeros_like(og_ref)
            ou_ref[...] = jnp.zeros_like(ou_ref)
            od_ref[...] = jnp.zeros_like(od_ref)

        hbv = hb_ref[...]
        og_ref[...] += _dot_tn(dg_ref[...], hbv)
        ou_ref[...] += _dot_tn(du_ref[...], hbv)
        od_ref[...] += _dot_tn(act_ref[...], dz_ref[...])

    half = FFN // 2
    a_spec = pl.BlockSpec((tk, half), lambda j, k: (k, j))
    b_spec = pl.BlockSpec((tk, D_MODEL), lambda j, k: (k, 0))
    o_spec = pl.BlockSpec((half, D_MODEL), lambda j, k: (j, 0))
    o_shape = jax.ShapeDtypeStruct((FFN, D_MODEL), F32)
    outs = pl.pallas_call(
        body, name="wgrad_ffn", grid=(2, nk),
        in_specs=[a_spec, a_spec, a_spec, b_spec, b_spec],
        out_specs=[o_spec] * 3, out_shape=[o_shape] * 3,
        compiler_params=_params("parallel", "arbitrary", vmem=VMEM_LIMIT),
    )(acts, dgs, dus, hb, dz2b)
    return [o.reshape(N_SHARD, FFN_SHARD, D_MODEL) for o in outs]


KV_ORDER = (0, 128, 64, 192)


def _wgrad_in(pieces, x2d, rider=None):
    t = x2d.shape[0]
    tk = min(t, 512)
    nk = t // tk
    kv0 = CB_AK * LANES

    def body(p0, p1, p2, p3, p4, pkv, x_ref, o_ref):
        @pl.when(pl.program_id(0) == 0)
        def _():
            o_ref[...] = jnp.zeros_like(o_ref)

        xb = x_ref[...].astype(BF16)
        for i, ref in enumerate((p0, p1, p2, p3, p4)):
            o_ref[i * 512:(i + 1) * 512, :] += _dot_tn(ref[...], xb)
        dkv = _dot_tn(pkv[...], xb)
        for i, o in enumerate(KV_ORDER):
            o_ref[kv0 + o:kv0 + o + HEAD_DIM, :] += dkv[i * HEAD_DIM:(i + 1) * HEAD_DIM]

    row = lambda w: pl.BlockSpec((tk, w), lambda k: (k, 0))
    return _hosted_call(
        body, "wgrad_in", (nk,),
        in_specs=[row(512)] * 5 + [row(256), row(D_MODEL)],
        out_specs=[pl.BlockSpec((IN_WIDTH, D_MODEL), lambda k: (0, 0))],
        out_shape=[jax.ShapeDtypeStruct((IN_WIDTH, D_MODEL), F32)],
        scratch_shapes=[], operands=(*pieces, x2d), rider=rider, semantics=["arbitrary"])


def _inproj_bwd(dz1, pieces, w_in_t, w_kv, rider=None):
    t = dz1.shape[0]
    tm = 512
    n_main = 5 * 512

    def body(dz_ref, p0, p1, p2, p3, p4, pkv, wm_ref, wkv_ref, o_ref):
        acc = ALPHA * dz_ref[...]
        for i, ref in enumerate((p0, p1, p2, p3, p4)):
            acc = acc + _dot(ref[...], wm_ref[i * 512:(i + 1) * 512, :])
        o_ref[...] = acc + _dot(pkv[...], wkv_ref[...])

    row = lambda w: pl.BlockSpec((tm, w), lambda i: (i, 0))
    const = lambda s: pl.BlockSpec(s, lambda i: (0, 0))
    return _hosted_call(
        body, "inproj_bwd", (t // tm,),
        in_specs=[row(D_MODEL)] + [row(512)] * 5 + [row(256), const((n_main, D_MODEL)), const(w_kv.shape)],
        out_specs=[row(D_MODEL)],
        out_shape=[jax.ShapeDtypeStruct((t, D_MODEL), F32)],
        scratch_shapes=[], operands=(dz1, *pieces, w_in_t, w_kv), rider=rider)


def _coords():
    return lax.axis_index("x"), lax.axis_index("y"), lax.axis_index("c")


def _chip_of(x, y, rel):
    return (1 - x if rel & 2 else x), (1 - y if rel & 1 else y)


def _all_gather_weights(shards):
    first = _gather_near_rider(shards)
    later = [f(first.out_shapes, chained=True) for f in (_gather_relay_rider, _gather_pass_rider)]
    return _run_riders("gather_weights", shards, first.out_shapes, [first] + later)


def _run_riders(name, ins, out_shapes, riders):
    n_in, n_out = len(ins), len(out_shapes)

    def body(*refs):
        in_refs, out_refs = refs[:n_in], refs[n_in:n_in + n_out]
        k = n_in + n_out
        for r in riders:
            sems = refs[k:k + len(r.sems)]
            k += len(r.sems)
            r.start(in_refs, out_refs, sems)
            r.finish(in_refs, out_refs, sems)

    hbm = pl.BlockSpec(memory_space=pl.ANY)
    return pl.pallas_call(
        body, name=name, in_specs=[hbm] * n_in, out_specs=[hbm] * n_out, out_shape=list(out_shapes),
        scratch_shapes=[s for r in riders for s in r.sems],
    )(*ins)


def _gather_half(outs, w, chip, cc):
    h = outs[w].shape[1] // 2
    return outs[w].at[chip, pl.ds(cc * h, h), :]


NEAR = (1, 2)


def _gather_near_rider(shards, rels=NEAR):
    nw, nr = len(shards), len(rels)

    def copies(ins, outs, sems, arrivals):
        send, recv, lsend, lrecv = sems
        x, y, c = _coords()
        me = 2 * x + y
        own = [pltpu.make_async_remote_copy(
            src_ref=ins[w], dst_ref=outs[w].at[me], send_sem=lsend.at[w], recv_sem=lrecv.at[w],
            device_id=(x, y, 1 - c), device_id_type=MESH) for w in range(nw)]
        out, arrive = [], []
        for i, rel in enumerate(rels):
            kx, ky = _chip_of(x, y, rel)
            for w in range(nw):
                h = shards[w].shape[0] // 2
                sem = dict(send_sem=send.at[w * nr + i], recv_sem=recv.at[w * nr + i],
                           device_id=(kx, ky, c), device_id_type=MESH)
                out.append(pltpu.make_async_remote_copy(
                    src_ref=ins[w].at[pl.ds(c * h, h), :], dst_ref=_gather_half(outs, w, me, c), **sem))
                if arrivals:
                    theirs = _gather_half(outs, w, 2 * kx + ky, c)
                    arrive.append(pltpu.make_async_remote_copy(src_ref=theirs, dst_ref=theirs, **sem))
        return own, out, arrive

    def start(ins, outs, sems):
        own, out, _ = copies(ins, outs, sems, arrivals=False)
        for cp in own + out:
            cp.start()

    def finish(ins, outs, sems):
        own, out, arrive = copies(ins, outs, sems, arrivals=True)
        for cp in arrive:
            cp.wait_recv()
        for cp in out:
            cp.wait_send()
        for cp in own:
            cp.wait()

    dma = pltpu.SemaphoreType.DMA
    return _Rider(shards, [jax.ShapeDtypeStruct((N_SHARD,) + s.shape, s.dtype) for s in shards],
                  [dma((nr * nw,)), dma((nr * nw,)), dma((nw,)), dma((nw,))], start, finish)


def _gather_relay_rider(gathered, chained=False):
    nw = len(gathered)

    def quarter(outs, w, chip, c, p):
        q = outs[w].shape[1] // 4
        return outs[w].at[chip, pl.ds(c * 2 * q + p * q, q), :]

    def copies(outs, sems):
        send, recv = sems
        x, y, c = _coords()
        (yx, yy), (xx, xy), (dx, dy) = (_chip_of(x, y, rel) for rel in (1, 2, 3))
        out, arrive = [], []
        for w in range(nw):
            for p, (src_chip, dst) in enumerate(((2 * xx + xy, (yx, yy)), (2 * yx + yy, (xx, xy)))):
                rows = quarter(outs, w, src_chip, c, p)
                sem = dict(send_sem=send.at[w * 2 + p], recv_sem=recv.at[w * 2 + p], device_id_type=MESH)
                out.append(pltpu.make_async_remote_copy(src_ref=rows, dst_ref=rows, device_id=(*dst, c), **sem))
                mine = quarter(outs, w, 2 * dx + dy, c, p)
                arrive.append(pltpu.make_async_remote_copy(src_ref=mine, dst_ref=mine, device_id=(*dst, c), **sem))
        return out, arrive

    def start(ins, outs, sems):
        for cp in copies(outs, sems)[0]:
            cp.start()

    def finish(ins, outs, sems):
        out, arrive = copies(outs, sems)
        for cp in arrive:
            cp.wait_recv()
        for cp in out:
            cp.wait_send()

    dma = pltpu.SemaphoreType.DMA
    shapes = [jax.ShapeDtypeStruct(g.shape, g.dtype) for g in gathered]
    if chained:
        return _Rider([], [], [dma((2 * nw,)), dma((2 * nw,))], start, finish)
    return _Rider(gathered, shapes, [dma((2 * nw,)), dma((2 * nw,))], start, finish,
                  aliases={w: w for w in range(nw)})


def _gather_pass_rider(gathered, chained=False):
    nw = len(gathered)

    def copies(outs, sems, cc):
        send, recv = sems
        x, y, c = _coords()
        res = []
        for rel in (1, 2, 3):
            kx, ky = _chip_of(x, y, rel)
            for w in range(nw):
                rows = _gather_half(outs, w, 2 * kx + ky, cc)
                res.append(pltpu.make_async_remote_copy(
                    src_ref=rows, dst_ref=rows, send_sem=send.at[w * 3 + rel - 1], recv_sem=recv.at[w * 3 + rel - 1],
                    device_id=(x, y, 1 - c), device_id_type=MESH))
        return res

    def start(ins, outs, sems):
        for cp in copies(outs, sems, lax.axis_index("c")):
            cp.start()

    def finish(ins, outs, sems):
        c = lax.axis_index("c")
        for cp in copies(outs, sems, 1 - c):
            cp.wait_recv()
        for cp in copies(outs, sems, c):
            cp.wait_send()

    dma = pltpu.SemaphoreType.DMA
    shapes = [jax.ShapeDtypeStruct(g.shape, g.dtype) for g in gathered]
    if chained:
        return _Rider([], [], [dma((3 * nw,)), dma((3 * nw,))], start, finish)
    return _Rider(gathered, shapes, [dma((3 * nw,)), dma((3 * nw,))], start, finish,
                  aliases={w: w for w in range(nw)})


def _exchange_halves_rider(parts):
    nw = len(parts)

    def copies(ins, outs, sems):
        send, recv = sems
        x, y, c = _coords()
        res = []
        for w in range(nw):
            h = parts[w].shape[1] // 2
            res.append(pltpu.make_async_remote_copy(
                src_ref=ins[w].at[:, pl.ds((1 - c) * h, h), :], dst_ref=outs[w],
                send_sem=send.at[w], recv_sem=recv.at[w], device_id=(x, y, 1 - c), device_id_type=MESH))
        return res

    def start(ins, outs, sems):
        for cp in copies(ins, outs, sems):
            cp.start()

    def finish(ins, outs, sems):
        for cp in copies(ins, outs, sems):
            cp.wait()

    dma = pltpu.SemaphoreType.DMA
    return _Rider(parts, [jax.ShapeDtypeStruct((N_SHARD, p.shape[1] // 2, p.shape[2]), p.dtype) for p in parts],
                  [dma((nw,)), dma((nw,))], start, finish)


def _add_halves(parts, theirs, pos):
    nw = len(parts)
    split = 2

    def body(pos_ref, *refs):
        ins, oth = refs[:nw], refs[nw:2 * nw]
        o32, o16 = refs[2 * nw:3 * nw], refs[3 * nw:]
        sums = [ins[w][...] + oth[w][...] for w in range(nw)]
        for w in range(nw):
            o16[w][...] = sums[w].astype(BF16)

        @pl.when(pl.program_id(1) == pos_ref[0])
        def _():
            for w in range(nw):
                o32[w][...] = sums[w]

    in_specs, oth_specs, o32_specs, shapes32, shapes16 = [], [], [], [], []
    for p in parts:
        hb = p.shape[1] // 2 // split
        blk = (None, hb, p.shape[2])
        in_specs.append(pl.BlockSpec(blk, lambda i, j, pos_ref: (j, pos_ref[1] * split + i, 0)))
        oth_specs.append(pl.BlockSpec(blk, lambda i, j, pos_ref: (j, i, 0)))
        o32_specs.append(pl.BlockSpec((hb, p.shape[2]), lambda i, j, pos_ref: (i, 0)))
        shapes32.append(jax.ShapeDtypeStruct((p.shape[1] // 2, p.shape[2]), F32))
        shapes16.append(jax.ShapeDtypeStruct((N_SHARD, p.shape[1] // 2, p.shape[2]), BF16))
    return pl.pallas_call(
        body, name="add_halves",
        grid_spec=pltpu.PrefetchScalarGridSpec(
            num_scalar_prefetch=1, grid=(split, N_SHARD),
            in_specs=in_specs + oth_specs, out_specs=o32_specs + oth_specs),
        out_shape=shapes32 + shapes16,
        compiler_params=_params("parallel", "arbitrary", vmem=VMEM_LIMIT),
    )(pos, *parts, *theirs)


def _exchange_chips_rider(sums16, rows=None, into=None):
    nw = len(sums16)
    rows = rows or [(0, s.shape[1]) for s in sums16]
    held = [w for w in range(nw) if into is not None and into[w] is not None]

    def copies(ins, outs, sems):
        send, recv = sems
        x, y, c = _coords()
        res = []
        for rel in (1, 2, 3):
            kx, ky = _chip_of(x, y, rel)
            for w in range(nw):
                r0, n = rows[w]
                res.append(pltpu.make_async_remote_copy(
                    src_ref=ins[w].at[2 * kx + ky, pl.ds(r0, n), :], dst_ref=outs[w].at[rel - 1, pl.ds(r0, n), :],
                    send_sem=send.at[w * 3 + rel - 1], recv_sem=recv.at[w * 3 + rel - 1],
                    device_id=(kx, ky, c), device_id_type=MESH))
        return res

    def start(ins, outs, sems):
        for cp in copies(ins, outs, sems):
            cp.start()

    def finish(ins, outs, sems):
        for cp in copies(ins, outs, sems):
            cp.wait()

    dma = pltpu.SemaphoreType.DMA
    return _Rider(list(sums16) + [into[w] for w in held],
                  [jax.ShapeDtypeStruct((3,) + s.shape[1:], BF16) for s in sums16],
                  [dma((3 * nw,)), dma((3 * nw,))], start, finish, aliases={nw + i: w for i, w in enumerate(held)})


def _add_chips(sums32, theirs, pos):
    nw = len(sums32)
    split = 2
    hbs = [s.shape[0] // split for s in sums32]

    def body(pos_ref, *refs):
        ins, oth, outs, bufs = (refs[k * nw:(k + 1) * nw] for k in range(4))
        lsem, ssem, rsem = refs[4 * nw:]
        i = pl.program_id(0)
        x, y, c = _coords()

        def copies(w, j):
            rows = pl.ds(pl.multiple_of((pos_ref[1] * split + j) * hbs[w], 8), hbs[w])
            return (pltpu.make_async_copy(bufs[w].at[j], outs[w].at[rows, :], lsem.at[w, j]),
                    pltpu.make_async_remote_copy(
                        src_ref=bufs[w].at[j], dst_ref=outs[w].at[rows, :], send_sem=ssem.at[w, j],
                        recv_sem=rsem.at[w, j], device_id=(x, y, 1 - c), device_id_type=MESH))

        for w in range(nw):
            acc = ins[w][...]
            for r in range(3):
                acc = acc + oth[w][r].astype(F32)
            bufs[w][i] = acc
            for cp in copies(w, i):
                cp.start()

        @pl.when(i == split - 1)
        def _():
            for w in range(nw):
                for j in range(split):
                    local, remote = copies(w, j)
                    local.wait()
                    remote.wait()

    in_specs, oth_specs, shapes, scratch = [], [], [], []
    for s, hb in zip(sums32, hbs):
        in_specs.append(pl.BlockSpec((hb, s.shape[1]), lambda i, pos_ref: (i, 0)))
        oth_specs.append(pl.BlockSpec((3, hb, s.shape[1]), lambda i, pos_ref: (0, i, 0)))
        shapes.append(jax.ShapeDtypeStruct((2 * s.shape[0], s.shape[1]), F32))
        scratch.append(pltpu.VMEM((split, hb, s.shape[1]), F32))
    dma = pltpu.SemaphoreType.DMA
    return pl.pallas_call(
        body, name="add_chips",
        grid_spec=pltpu.PrefetchScalarGridSpec(
            num_scalar_prefetch=1, grid=(split,), in_specs=in_specs + oth_specs,
            out_specs=[pl.BlockSpec(memory_space=pl.ANY)] * nw,
            scratch_shapes=scratch + [dma((nw, split)), dma((nw, split)), dma((nw, split))]),
        out_shape=shapes,
        compiler_params=_params("arbitrary", vmem=VMEM_LIMIT),
    )(pos, *sums32, *theirs)


def _adamw_math(w, g, m, v):
    m = ADAM_B1 * m + (1.0 - ADAM_B1) * g
    v = ADAM_B2 * v + (1.0 - ADAM_B2) * (g * g)
    m_hat = m / (1.0 - ADAM_B1 ** ADAM_STEP)
    v_hat = v / (1.0 - ADAM_B2 ** ADAM_STEP)
    delta = -ADAM_LR * (m_hat / (jnp.sqrt(v_hat) + ADAM_EPS) + ADAM_WD * w)
    return delta, m, v


def _adamw(ws, gs, ms, vs):
    nw = len(ws)
    split = 8

    def body(*refs):
        w_r, g_r, m_r, v_r = (refs[i * nw:(i + 1) * nw] for i in range(4))
        g_o, d_o, m_o, v_o = (refs[(4 + i) * nw:(5 + i) * nw] for i in range(4))
        for k in range(nw):
            g = g_r[k][...]
            d, m, v = _adamw_math(w_r[k][...], g, m_r[k][...], v_r[k][...])
            g_o[k][...] = g
            d_o[k][...] = d
            m_o[k][...] = m
            v_o[k][...] = v

    specs = [pl.BlockSpec((w.shape[0] // split, w.shape[1]), lambda i: (i, 0)) for w in ws]
    shapes = [jax.ShapeDtypeStruct(w.shape, F32) for w in ws]
    outs = pl.pallas_call(
        body, name="adamw", grid=(split,),
        in_specs=specs * 4, out_specs=specs * 4, out_shape=shapes * 4,
        compiler_params=_params("parallel", vmem=VMEM_LIMIT),
    )(*ws, *gs, *ms, *vs)
    return outs[:nw], outs[nw:2 * nw], outs[2 * nw:3 * nw], outs[3 * nw:]


SMALL_ROWS = 8
SMALL_COLS = D_MODEL
LOSS_COL = RET_WIDTH + 24


def _small_allreduce_adamw(part, w, m, v, rider=None):
    def body(part_ref, w_ref, m_ref, v_ref, g_out, d_out, m_out, v_out, all_ref, send, recv):
        x, y, c = _coords()
        me = 4 * x + 2 * y + c
        all_ref[me] = part_ref[...]
        copies = []
        for rel in range(1, 8):
            px = 1 - x if rel & 4 else x
            py = 1 - y if rel & 2 else y
            pc = 1 - c if rel & 1 else c
            copies.append(pltpu.make_async_remote_copy(
                src_ref=part_ref, dst_ref=all_ref.at[me],
                send_sem=send.at[rel - 1], recv_sem=recv.at[rel - 1], device_id=(px, py, pc), device_id_type=MESH))
        for cp in copies:
            cp.start()
        for cp in copies:
            cp.wait()
        g = all_ref[0]
        for k in range(1, 8):
            g = g + all_ref[k]
        d, mn, vn = _adamw_math(w_ref[...], g, m_ref[...], v_ref[...])
        g_out[...] = g
        d_out[...] = d
        m_out[...] = mn
        v_out[...] = vn

    vm = pl.BlockSpec(memory_space=pltpu.VMEM)
    shape = jax.ShapeDtypeStruct((SMALL_ROWS, SMALL_COLS), F32)
    return _hosted_call(
        body, "small_allreduce_adamw", (1,),
        in_specs=[vm] * 4, out_specs=[vm] * 4, out_shape=[shape] * 4,
        scratch_shapes=[pltpu.VMEM((8, SMALL_ROWS, SMALL_COLS), F32),
                        pltpu.SemaphoreType.DMA((7,)), pltpu.SemaphoreType.DMA((7,))],
        operands=(part, w, m, v), rider=rider, semantics=["arbitrary"])


SMALL_NAMES = ("ret_decay_fwd", "ret_decay_bwd", "attn_sink", "ret_gn_gain",
               "ln1_gain", "ln1_bias", "ln2_gain", "ln2_bias")


LN_NAMES = ("ln1_gain", "ln1_bias", "ln2_gain", "ln2_bias")


def _pack_small(vals, extra=None):
    tail = jnp.zeros((1, 1), F32) if extra is None else extra.reshape(1, 1)
    row4 = jnp.concatenate([vals["ret_gn_gain"], vals["ret_decay_fwd"], vals["ret_decay_bwd"], vals["attn_sink"],
                            tail, jnp.zeros((1, SMALL_COLS - LOSS_COL - 1), F32)], axis=1)
    rows = [vals[n] for n in LN_NAMES] + [row4, jnp.zeros((SMALL_ROWS - 5, SMALL_COLS), F32)]
    return jnp.concatenate(rows, axis=0)


def _unpack_small(packed):
    out = {n: packed[i:i + 1] for i, n in enumerate(LN_NAMES)}
    o = RET_WIDTH
    out.update(ret_gn_gain=packed[4:5, 0:o], ret_decay_fwd=packed[4:5, o:o + 8],
               ret_decay_bwd=packed[4:5, o + 8:o + 16], attn_sink=packed[4:5, o + 16:o + 24])
    return out


def _local_step(x, p, tgt, w_in_t, rest, small, pos=None, small_state=None):
    bsz, s, _ = x.shape
    t = bsz * s
    x2d = x.reshape(t, D_MODEL)
    p2d = p.reshape(t, PLE_DIM)
    tgt2d = tgt.reshape(t, D_MODEL)
    dec_f = small["ret_decay_fwd"].reshape(8)
    dec_b = small["ret_decay_bwd"].reshape(8)
    lg_f = jnp.log1p(-jnp.exp2(dec_f))
    lg_b = jnp.log1p(-jnp.exp2(dec_b))
    per_lane = lambda v: jnp.repeat(v, HEAD_DIM).reshape(4, 1, LANES)
    lgf_l, lgb_l = per_lane(lg_f), per_lane(lg_b)
    sink = small["attn_sink"].reshape(8)
    slopes = 2.0 ** (-(jnp.arange(8, dtype=F32) + 1.0))
    gn_gain = small["ret_gn_gain"]
    g1, b1, g2, b2 = (small[n] for n in ("ln1_gain", "ln1_bias", "ln2_gain", "ln2_bias"))

    dist = pos is not None
    shard = dict(zip(REST_NAMES, rest)) if dist else {}
    near = lambda names, rels=NEAR: _gather_near_rider([shard[n] for n in names], rels)
    wave1, wave2, wave3 = ("w_out", "w_ple_gate", "w_ffn_gate"), ("w_ffn_up", "w_ple_proj"), ("w_ffn_down",)
    n1 = len(wave1)
    u, *o1 = _inproj(x2d, w_in_t, rider=near(wave1) if dist else None)
    u3 = u.reshape(bsz, s, IN_WIDTH)
    y_hat, y_rstd, y_ret, ret_rb, ret_kvf, *o2 = _ret_fwd(u3, lgf_l, lgb_l, gn_gain, rider=_merge_riders(
        [_gather_relay_rider(o1), near(wave2)]) if dist else None)
    y_att, att_p, att_ps, *o3 = _attn_fwd(u3, slopes, sink, rider=_merge_riders(
        [_gather_pass_rider(o2[:n1]), _gather_relay_rider(o2[n1:]), near(wave3, (1, 2, 3))]) if dist else None)
    gathered = dict(zip(wave1, o3[:n1]))
    w_out = _assemble_weights({"w_out": gathered["w_out"]})["w_out"] if dist else rest["w_out"]
    zh1, r1, hb, *o4 = _outproj_ln1(y_ret.reshape(t, RET_WIDTH), y_att.reshape(t, ATTN_WIDTH), x2d, w_out, g1, b1,
                                    rider=_gather_pass_rider(o3[n1:]) if dist else None)
    gathered.update(zip(wave2 + wave3, o4))
    wts = _assemble_weights(gathered) if dist else rest
    dz2, dz2b, gs, us, acts, pg, ple, sq, dg2, db2 = _ffn_fwd(
        zh1, hb, p2d, tgt2d, g1, b1, g2, b2, wts["gate4"], wts["up4"], wts["down4"], wts["ple_proj"], wts["ple_gate"])
    dgs, dus, dsp, dple, dz1, dyr, dya, dg1, db1 = _ffn_bwd(dz2, gs, us, pg, ple, zh1, r1, g1, wts["gate4"],
                                                          wts["up4"], wts["down4"], wts["ple_gate"], wts["w_out"])
    ffn_parts = list(_wgrad_ffn(acts, dgs, dus, hb, dz2b))
    d_w_out, d_ple_gate, d_ple_proj, *th_ffn = _wgrad_misc(
        y_ret.reshape(t, RET_WIDTH), y_att.reshape(t, ATTN_WIDTH), dz1, hb, dsp, p2d, dple,
        rider=_exchange_halves_rider(ffn_parts[:2]) if dist else None)
    misc_parts = [d_w_out.reshape(N_SHARD, D_MODEL // N_SHARD, D_MODEL), d_ple_proj,
                  d_ple_gate.reshape(N_SHARD, D_MODEL // N_SHARD, D_MODEL)]
    dyr3, dya3 = dyr.reshape(bsz, s, RET_WIDTH), dya.reshape(bsz, s, ATTN_WIDTH)
    if dist:
        s_gu = _add_halves(ffn_parts[:2], th_ffn, pos)
        half = FFN_SHARD // 2
        quarter = half // 2
        later_parts = [ffn_parts[2]] + misc_parts
        drq, drk, drv, drg, rpart, *o5 = _ret_bwd(u3, y_hat, y_rstd, (ret_rb, ret_kvf), dyr3, lgf_l, lgb_l, gn_gain,
                                                  rider=_merge_riders(
            [_exchange_chips_rider(s_gu[2:], rows=[(0, half), (0, quarter)]), _exchange_halves_rider(later_parts)]))
        s_dm = _add_halves(later_parts, o5[2:], pos)
        daq, dakv, spart, *o6 = _attn_bwd(u3, dya3, att_p, att_ps, rider=_exchange_chips_rider(
            [s_gu[3], s_dm[4]], rows=[(quarter, half - quarter), (0, half)], into=[o5[1], None]))
    else:
        drq, drk, drv, drg, rpart = _ret_bwd(u3, y_hat, y_rstd, (ret_rb, ret_kvf), dyr3, lgf_l, lgb_l, gn_gain)
        daq, dakv, spart = _attn_bwd(u3, dya3, att_p, att_ps)
    pieces = [a.reshape(t, -1) for a in (drq, drk, drv, drg, daq, dakv)]
    kv0 = CB_AK * LANES
    w_kv = jnp.concatenate([w_in_t[kv0 + o:kv0 + o + HEAD_DIM] for o in KV_ORDER], axis=0)
    d_in, *o7 = _wgrad_in(pieces, x2d, rider=_exchange_chips_rider(list(s_dm[5:])) if dist else None)
    d_in = d_in.reshape(N_SHARD, FFN_SHARD, D_MODEL)

    rsum = rpart
    lane_heads = lambda row: jnp.sum(row.reshape(4, 2, HEAD_DIM), axis=-1).reshape(8)
    dlg_f = lane_heads(rsum[:, 0, :]) + jnp.stack([jnp.sum(rsum[:, 2, :], -1), jnp.sum(rsum[:, 3, :], -1)], 1).reshape(8)
    dlg_b = lane_heads(rsum[:, 1, :]) + jnp.stack([jnp.sum(rsum[:, 4, :], -1), jnp.sum(rsum[:, 5, :], -1)], 1).reshape(8)
    chain = lambda d: -(math.log(2.0) * jnp.exp2(d)) / (1.0 - jnp.exp2(d))
    grads_small = {
        "ret_decay_fwd": (dlg_f * chain(dec_f)).reshape(1, 8),
        "ret_decay_bwd": (dlg_b * chain(dec_b)).reshape(1, 8),
        "attn_sink": jnp.sum(spart, axis=0)[:, 0:4, 0].reshape(1, 8),
        "ret_gn_gain": rsum[:, 6, :].reshape(1, RET_WIDTH),
        "ln1_gain": dg1, "ln1_bias": db1, "ln2_gain": dg2, "ln2_bias": db2,
    }
    if not dist:
        grad_x, = _inproj_bwd(dz1, pieces, w_in_t, w_kv)
        grads_rest = [misc_parts[0]] + ffn_parts + misc_parts[1:]
        return sq[0, 0], grad_x.reshape(bsz, s, D_MODEL), d_in, grads_rest, grads_small
    *small_out, th_in = _small_allreduce_adamw(_pack_small(grads_small, sq[0, 0]), *small_state,
                                               rider=_exchange_halves_rider([d_in]))
    s_in = _add_halves([d_in], [th_in], pos)
    grad_x, chips_in = _inproj_bwd(dz1, pieces, w_in_t, w_kv, rider=_exchange_chips_rider([s_in[1]]))
    sums32 = [s_in[0], s_dm[1], s_gu[0], s_gu[1], s_dm[0], s_dm[2], s_dm[3]]
    from_chips = [chips_in, o7[0], o5[0], o6[0], o6[1], o7[1], o7[2]]
    return grad_x.reshape(bsz, s, D_MODEL), sums32, from_chips, small_out


BIG_NAMES = ("w_in", "w_out", "w_ffn_gate", "w_ffn_up", "w_ffn_down", "w_ple_proj", "w_ple_gate")
REST_NAMES = BIG_NAMES[1:]
TRANSPOSED = ("w_in", "w_ffn_gate", "w_ffn_up")
WEIGHT_ORDER = ("w_in", "ret_decay_fwd", "ret_decay_bwd", "ret_gn_gain", "attn_sink", "w_out", "ln1_gain",
                "ln1_bias", "w_ffn_gate", "w_ffn_up", "w_ffn_down", "w_ple_proj", "w_ple_gate", "ln2_gain", "ln2_bias")


def _shard_rows(name, a):
    return jnp.swapaxes(a[0], 0, 1) if name in TRANSPOSED else a[0]


def _unshard_rows(name, a):
    return (jnp.swapaxes(a, 0, 1) if name in TRANSPOSED else a)[None]


def _assemble_weights(gathered):
    rows = lambda a: a.reshape(N_SHARD * a.shape[1], a.shape[2])
    same = lambda a: a
    layout = {"w_out": ("w_out", rows), "w_ffn_gate": ("gate4", same), "w_ffn_up": ("up4", same),
              "w_ffn_down": ("down4", same), "w_ple_proj": ("ple_proj", same), "w_ple_gate": ("ple_gate", rows)}
    return {layout[n][0]: layout[n][1](a) for n, a in gathered.items()}


def kernel(x, p, w_in, ret_decay_fwd, ret_decay_bwd, ret_gn_gain, attn_sink, w_out, ln1_gain, ln1_bias, w_ffn_gate, w_ffn_up, w_ffn_down, w_ple_proj, w_ple_gate, ln2_gain, ln2_bias, loss_target, m_w_in, m_ret_decay_fwd, m_ret_decay_bwd, m_ret_gn_gain, m_attn_sink, m_w_out, m_ln1_gain, m_ln1_bias, m_w_ffn_gate, m_w_ffn_up, m_w_ffn_down, m_w_ple_proj, m_w_ple_gate, m_ln2_gain, m_ln2_bias, v_w_in, v_ret_decay_fwd, v_ret_decay_bwd, v_ret_gn_gain, v_attn_sink, v_w_out, v_ln1_gain, v_ln1_bias, v_w_ffn_gate, v_w_ffn_up, v_w_ffn_down, v_w_ple_proj, v_w_ple_gate, v_ln2_gain, v_ln2_bias):
    w = dict(w_in=w_in, ret_decay_fwd=ret_decay_fwd, ret_decay_bwd=ret_decay_bwd, ret_gn_gain=ret_gn_gain,
             attn_sink=attn_sink, w_out=w_out, ln1_gain=ln1_gain, ln1_bias=ln1_bias, w_ffn_gate=w_ffn_gate,
             w_ffn_up=w_ffn_up, w_ffn_down=w_ffn_down, w_ple_proj=w_ple_proj, w_ple_gate=w_ple_gate,
             ln2_gain=ln2_gain, ln2_bias=ln2_bias)
    m = dict(w_in=m_w_in, ret_decay_fwd=m_ret_decay_fwd, ret_decay_bwd=m_ret_decay_bwd, ret_gn_gain=m_ret_gn_gain,
             attn_sink=m_attn_sink, w_out=m_w_out, ln1_gain=m_ln1_gain, ln1_bias=m_ln1_bias, w_ffn_gate=m_w_ffn_gate,
             w_ffn_up=m_w_ffn_up, w_ffn_down=m_w_ffn_down, w_ple_proj=m_w_ple_proj, w_ple_gate=m_w_ple_gate,
             ln2_gain=m_ln2_gain, ln2_bias=m_ln2_bias)
    v = dict(w_in=v_w_in, ret_decay_fwd=v_ret_decay_fwd, ret_decay_bwd=v_ret_decay_bwd, ret_gn_gain=v_ret_gn_gain,
             attn_sink=v_attn_sink, w_out=v_w_out, ln1_gain=v_ln1_gain, ln1_bias=v_ln1_bias, w_ffn_gate=v_w_ffn_gate,
             w_ffn_up=v_w_ffn_up, w_ffn_down=v_w_ffn_down, w_ple_proj=v_w_ple_proj, w_ple_gate=v_w_ple_gate,
             ln2_gain=v_ln2_gain, ln2_bias=v_ln2_bias)
    big = lambda d: [_shard_rows(n, d[n]) for n in BIG_NAMES]
    small = lambda d: {n: d[n] for n in SMALL_NAMES}

    chip = 2 * lax.axis_index("x") + lax.axis_index("y")
    pos = jnp.stack([chip, lax.axis_index("c")]).astype(jnp.int32)

    shards = [a.astype(BF16) for a in big(w)]
    (w_in4,) = _all_gather_weights(shards[:1])
    w_in_t = w_in4.reshape(IN_WIDTH, D_MODEL)
    grad_x, sums32, from_chips, (g_s, d_s, m_s, v_s) = _local_step(
        x, p[0], loss_target, w_in_t, shards[1:], small(w), pos=pos,
        small_state=(_pack_small(small(w)), _pack_small(small(m)), _pack_small(small(v))))
    g_big, d_big, m_big, v_big = _adamw(big(w), _add_chips(sums32, from_chips, pos), big(m), big(v))
    loss = g_s[4, LOSS_COL] * (0.5 / D_MODEL)

    def tree(bigs, packed):
        out = {n: _unshard_rows(n, a) for n, a in zip(BIG_NAMES, bigs)}
        out.update(_unpack_small(packed))
        return [out[n] for n in WEIGHT_ORDER]

    return (loss, grad_x, *tree(g_big, g_s), *tree(d_big, d_s), *tree(m_big, m_s), *tree(v_big, v_s))
```

```python
import functools
import math

import jax
import jax.numpy as jnp
from jax import lax
from jax.experimental import pallas as pl
from jax.experimental.pallas import tpu as pltpu

F32 = jnp.float32
BF16 = jnp.bfloat16

D_MODEL = 1024
HEAD_DIM = 64
RET_HEADS = 8
ATTN_HEADS = 8
RET_WIDTH = 512
ATTN_WIDTH = 512
KV_WIDTH = 128
IN_WIDTH = 2816
FFN = 2816
N_SHARD = 4
FFN_SHARD = FFN // N_SHARD
PLE_DIM = 256
CHUNK = 128
LANES = 128
ALPHA = 2.0 ** 0.25
LN_EPS = 1e-5
GN_EPS = 1e-5
NEG_INF = -1e30
ADAM_LR = 0.001
ADAM_B1 = 0.9
ADAM_B2 = 0.999
ADAM_EPS = 1e-08
ADAM_WD = 0.01
ADAM_STEP = 10
VMEM_LIMIT = 56 * 1024 * 1024
MESH = pl.DeviceIdType.MESH

CB_RQ, CB_RK, CB_RV, CB_RG, CB_AQ, CB_AK, CB_AV = 0, 4, 8, 12, 16, 20, 21


def _dot(a, b):
    return jnp.dot(a, b, preferred_element_type=F32)


def _dot_nt(a, b):
    return lax.dot_general(a, b, (((1,), (1,)), ((), ())), preferred_element_type=F32)


def _dot_tn(a, b):
    return lax.dot_general(a, b, (((0,), (0,)), ((), ())), preferred_element_type=F32)


def _sigmoid(x):
    return 1.0 / (1.0 + jnp.exp(-x))


def _params(*sem, vmem=None):
    return pltpu.CompilerParams(dimension_semantics=tuple(sem) if sem else None, vmem_limit_bytes=vmem)


class _Rider:
    def __init__(self, ins, out_shapes, sems, start, finish, aliases=None):
        self.ins, self.out_shapes, self.sems = list(ins), list(out_shapes), list(sems)
        self.start, self.finish, self.aliases = start, finish, dict(aliases or {})


def _merge_riders(riders):
    riders = [r for r in riders if r is not None]
    if len(riders) == 1:
        return riders[0]
    bounds, aliases = [], {}
    i0 = o0 = s0 = 0
    for r in riders:
        bounds.append((i0, o0, s0))
        aliases.update({i0 + i: o0 + o for i, o in r.aliases.items()})
        i0, o0, s0 = i0 + len(r.ins), o0 + len(r.out_shapes), s0 + len(r.sems)

    def each(method):
        def run(ins, outs, sems):
            for r, (i, o, s) in zip(riders, bounds):
                getattr(r, method)(ins[i:i + len(r.ins)], outs[o:o + len(r.out_shapes)], sems[s:s + len(r.sems)])
        return run

    return _Rider([a for r in riders for a in r.ins], [a for r in riders for a in r.out_shapes],
                  [a for r in riders for a in r.sems], each("start"), each("finish"), aliases)


def _hosted_call(body, name, grid, in_specs, out_specs, out_shape, scratch_shapes, operands, rider=None,
                 semantics=None):
    n_in, n_out, n_scr = len(in_specs), len(out_specs), len(scratch_shapes)
    if rider is None:
        return pl.pallas_call(
            body, name=name, grid=grid, in_specs=in_specs, out_specs=out_specs, out_shape=out_shape,
            scratch_shapes=scratch_shapes,
            compiler_params=_params(*(semantics or ["parallel"] * len(grid)), vmem=VMEM_LIMIT))(*operands)
    r_in, r_out = len(rider.ins), len(rider.out_shapes)

    def full_body(*refs):
        main_in, rin = refs[:n_in], refs[n_in:n_in + r_in]
        o0 = n_in + r_in
        main_out, rout = refs[o0:o0 + n_out], refs[o0 + n_out:o0 + n_out + r_out]
        s0 = o0 + n_out + r_out
        main_scr, rsem = refs[s0:s0 + n_scr], refs[s0 + n_scr:]
        first = functools.reduce(jnp.logical_and, [pl.program_id(a) == 0 for a in range(len(grid))])
        last = functools.reduce(jnp.logical_and, [pl.program_id(a) == g - 1 for a, g in enumerate(grid)])

        @pl.when(first)
        def _():
            rider.start(rin, rout, rsem)

        body(*main_in, *main_out, *main_scr)

        @pl.when(last)
        def _():
            rider.finish(rin, rout, rsem)

    hbm = pl.BlockSpec(memory_space=pl.ANY)
    return pl.pallas_call(
        full_body, name=name, grid=grid,
        in_specs=list(in_specs) + [hbm] * r_in, out_specs=list(out_specs) + [hbm] * r_out,
        out_shape=list(out_shape) + rider.out_shapes,
        scratch_shapes=list(scratch_shapes) + rider.sems,
        input_output_aliases={n_in + i: n_out + o for i, o in rider.aliases.items()},
        compiler_params=_params(*(["arbitrary"] * len(grid)), vmem=VMEM_LIMIT),
    )(*operands, *rider.ins)


def _loop_grouped(n, body, init, per_trip=2):
    if n % per_trip:
        return lax.fori_loop(0, n, body, init)

    def trip(i, c):
        for j in range(per_trip):
            c = body(per_trip * i + j, c)
        return c

    return lax.fori_loop(0, n // per_trip, trip, init)


def _head_mean(x, m0):
    s0 = jnp.sum(jnp.where(m0, x, 0.0), axis=1, keepdims=True)
    s1 = jnp.sum(jnp.where(m0, 0.0, x), axis=1, keepdims=True)
    return jnp.where(m0, s0, s1) * (1.0 / HEAD_DIM)


def _inproj(x2d, w_in_t, rider=None):
    t = x2d.shape[0]
    tm = 512
    nb = 256

    def body(x_ref, w_ref, o_ref):
        xb = x_ref[...].astype(BF16)
        for n in range(0, IN_WIDTH, nb):
            o_ref[:, n:n + nb] = _dot_nt(xb, w_ref[n:n + nb, :]).astype(BF16)

    return _hosted_call(
        body, "inproj", (t // tm,),
        in_specs=[pl.BlockSpec((tm, D_MODEL), lambda i: (i, 0)),
                  pl.BlockSpec((IN_WIDTH, D_MODEL), lambda i: (0, 0))],
        out_specs=[pl.BlockSpec((tm, IN_WIDTH), lambda i: (i, 0))],
        out_shape=[jax.ShapeDtypeStruct((t, IN_WIDTH), BF16)],
        scratch_shapes=[], operands=(x2d, w_in_t), rider=rider)


def _outproj_ln1(y_ret, y_att, x2d, w_out, gain, bias, rider=None):
    t = x2d.shape[0]
    tm = 512

    def body(yr_ref, ya_ref, x_ref, w_ref, g_ref, b_ref, zh_ref, r_ref, hb_ref):
        mix = _dot(yr_ref[...], w_ref[0:RET_WIDTH, :]) + _dot(ya_ref[...], w_ref[RET_WIDTH:, :])
        z = ALPHA * x_ref[...] + mix
        mu = jnp.mean(z, axis=1, keepdims=True)
        zc = z - mu
        var = jnp.mean(zc * zc, axis=1, keepdims=True)
        r = lax.rsqrt(var + LN_EPS)
        zh = zc * r
        zh_ref[...] = zh
        r_ref[...] = r
        hb_ref[...] = (zh * g_ref[...] + b_ref[...]).astype(BF16)

    row = lambda w: pl.BlockSpec((tm, w), lambda i: (i, 0))
    const = lambda s: pl.BlockSpec(s, lambda i: (0, 0))
    return _hosted_call(
        body, "outproj_ln1", (t // tm,),
        in_specs=[row(RET_WIDTH), row(ATTN_WIDTH), row(D_MODEL), const((D_MODEL, D_MODEL)),
                  const((1, D_MODEL)), const((1, D_MODEL))],
        out_specs=[row(D_MODEL), row(1), row(D_MODEL)],
        out_shape=[jax.ShapeDtypeStruct((t, D_MODEL), F32), jax.ShapeDtypeStruct((t, 1), F32),
                   jax.ShapeDtypeStruct((t, D_MODEL), BF16)],
        scratch_shapes=[], operands=(y_ret, y_att, x2d, w_out, gain, bias), rider=rider)


def _load_resident(step, pairs, sems):
    copies = [pltpu.make_async_copy(src, dst, sems.at[i]) for i, (src, dst) in enumerate(pairs)]

    @pl.when(step == 0)
    def _():
        for cp in copies:
            cp.start()
        for cp in copies:
            cp.wait()


FFN_CHUNK = 256
N_FFN_CHUNK = FFN // FFN_CHUNK


def _resident_quarters(hbm, vmem):
    q = FFN // N_SHARD
    return [(hbm.at[pl.ds(j * q, q), :], vmem.at[pl.ds(j * q, q), :]) for j in range(N_SHARD)]


def _ln2_loss_tail(zh, mixed, tgt, g1, b1, g2, b2):
    z2 = ALPHA * (zh * g1 + b1) + mixed
    mu = jnp.mean(z2, axis=1, keepdims=True)
    zc = z2 - mu
    var = jnp.mean(zc * zc, axis=1, keepdims=True)
    r = lax.rsqrt(var + LN_EPS)
    zh2 = zc * r
    err = zh2 * g2 + b2 - tgt
    dy = err * (1.0 / D_MODEL)
    dzh = dy * g2
    m1 = jnp.mean(dzh, axis=1, keepdims=True)
    m2 = jnp.mean(dzh * zh2, axis=1, keepdims=True)
    dz2 = r * (dzh - m1 - zh2 * m2)
    return dz2, jnp.sum(err * err), jnp.sum(dy * zh2, axis=0, keepdims=True), jnp.sum(dy, axis=0, keepdims=True)


def _ffn_fwd(zh1, hb, p2d, tgt, g1, b1, g2, b2, wg4, wu4, wd4, wpe, wpg):
    t = zh1.shape[0]
    tm = 256
    wg_t, wu_t, wd_all = (w.reshape(FFN, D_MODEL) for w in (wg4, wu4, wd4))

    def body(zh_ref, hb_ref, p_ref, t_ref, g1_ref, b1_ref, g2_ref, b2_ref,
             wg_hbm, wu_hbm, wd_hbm, wpe_hbm, wpg_hbm,
             dz_ref, dzb_ref, gs_ref, us_ref, act_ref, pg_ref, ple_ref, loss_ref, dg2_ref, db2_ref,
             wg, wu, wd, wpe, wpg, wsem):
        step = pl.program_id(0)
        loads = _resident_quarters(wg_hbm, wg) + _resident_quarters(wu_hbm, wu) + _resident_quarters(wd_hbm, wd)
        pc = D_MODEL // N_SHARD
        loads += [(wpe_hbm.at[j], wpe.at[:, pl.ds(j * pc, pc)]) for j in range(N_SHARD)]
        _load_resident(step, loads + [(wpg_hbm, wpg)], wsem)

        @pl.when(step == 0)
        def _():
            loss_ref[...] = jnp.zeros_like(loss_ref)
            dg2_ref[...] = jnp.zeros_like(dg2_ref)
            db2_ref[...] = jnp.zeros_like(db2_ref)

        hbv = hb_ref[...]
        ffn = jnp.zeros((tm, D_MODEL), F32)
        acts = []
        chunks = [slice(n * FFN_CHUNK, (n + 1) * FFN_CHUNK) for n in range(N_FFN_CHUNK)]
        for n in range(N_FFN_CHUNK + 1):
            if n < N_FFN_CHUNK:
                gj = _dot_nt(hbv, wg[chunks[n], :])
                uj = _dot_nt(hbv, wu[chunks[n], :])
                gs_ref[:, chunks[n]] = gj.astype(BF16)
                us_ref[:, chunks[n]] = uj.astype(BF16)
                acts.append((gj * _sigmoid(gj) * uj).astype(BF16))
                act_ref[:, chunks[n]] = acts[n]
            if n > 0:
                ffn = ffn + _dot(acts[n - 1], wd[chunks[n - 1], :])
        ple = _dot(p_ref[...].astype(BF16), wpe[...])
        pg = _sigmoid(_dot(hbv, wpg[...]))
        pg_ref[...] = pg.astype(BF16)
        ple_ref[...] = ple.astype(BF16)
        dz2, sq, dg2, db2 = _ln2_loss_tail(zh_ref[...], ffn + pg * ple, t_ref[...], g1_ref[...], b1_ref[...],
                                           g2_ref[...], b2_ref[...])
        dz_ref[...] = dz2
        dzb_ref[...] = dz2.astype(BF16)
        loss_ref[...] += sq
        dg2_ref[...] += dg2
        db2_ref[...] += db2

    row = lambda w: pl.BlockSpec((tm, w), lambda i: (i, 0))
    const = lambda s: pl.BlockSpec(s, lambda i: (0, 0))
    hid_shape = jax.ShapeDtypeStruct((t, FFN), BF16)
    hbm = pl.BlockSpec(memory_space=pl.ANY)
    return pl.pallas_call(
        body, name="ffn_fwd", grid=(t // tm,),
        in_specs=[row(D_MODEL), row(D_MODEL), row(PLE_DIM), row(D_MODEL),
                  const((1, D_MODEL)), const((1, D_MODEL)), const((1, D_MODEL)), const((1, D_MODEL)),
                  hbm, hbm, hbm, hbm, hbm],
        out_specs=[row(D_MODEL), row(D_MODEL), row(FFN), row(FFN), row(FFN), row(D_MODEL), row(D_MODEL),
                   const((8, LANES)), const((1, D_MODEL)), const((1, D_MODEL))],
        out_shape=[jax.ShapeDtypeStruct((t, D_MODEL), F32), jax.ShapeDtypeStruct((t, D_MODEL), BF16),
                   hid_shape, hid_shape, hid_shape,
                   jax.ShapeDtypeStruct((t, D_MODEL), BF16), jax.ShapeDtypeStruct((t, D_MODEL), BF16),
                   jax.ShapeDtypeStruct((8, LANES), F32),
                   jax.ShapeDtypeStruct((1, D_MODEL), F32), jax.ShapeDtypeStruct((1, D_MODEL), F32)],
        scratch_shapes=[pltpu.VMEM((FFN, D_MODEL), BF16), pltpu.VMEM((FFN, D_MODEL), BF16),
                        pltpu.VMEM((FFN, D_MODEL), BF16),
                        pltpu.VMEM((PLE_DIM, D_MODEL), BF16), pltpu.VMEM(wpg.shape, BF16),
                        pltpu.SemaphoreType.DMA((4 * N_SHARD + 1,))],
        compiler_params=_params("arbitrary", vmem=VMEM_LIMIT),
    )(zh1, hb, p2d, tgt, g1, b1, g2, b2, wg_t, wu_t, wd_all, wpe, wpg)


def _ret_tables(lgf, lgb):
    c = CHUNK
    row = lax.broadcasted_iota(jnp.int32, (c, LANES), 0).astype(F32)
    ii = lax.broadcasted_iota(jnp.int32, (c, c), 0).astype(F32)
    jj = lax.broadcasted_iota(jnp.int32, (c, c), 1).astype(F32)
    diff = ii - jj
    dmats = []
    for h in range(2):
        lf = lgf[:, h * HEAD_DIM:h * HEAD_DIM + 1]
        lb = lgb[:, h * HEAD_DIM:h * HEAD_DIM + 1]
        dmats.append(jnp.where(diff > 0, jnp.exp(lf * jnp.maximum(diff, 0.0)),
                               jnp.where(diff < 0, jnp.exp(lb * jnp.maximum(-diff, 0.0)), 2.0)))
    tab = dict(
        qdec_f=jnp.exp(lgf * (row + 1.0)), kdec_f=jnp.exp(lgf * (c - 1.0 - row)),
        qdec_b=jnp.exp(lgb * (c - row)), kdec_b=jnp.exp(lgb * row),
        cdec_f=jnp.exp(lgf * c), cdec_b=jnp.exp(lgb * c),
        d0=dmats[0], d1=dmats[1], row=row, diff=diff)
    r = lax.broadcasted_iota(jnp.int32, (LANES, LANES), 0) < HEAD_DIM
    cc = lax.broadcasted_iota(jnp.int32, (LANES, LANES), 1) < HEAD_DIM
    tab["bd"] = r == cc
    tab["m0"] = lax.broadcasted_iota(jnp.int32, (c, LANES), 1) < HEAD_DIM
    return tab


def _ret_specs(bsz, s):
    blk = lambda cb: pl.BlockSpec((bsz, s, LANES), lambda p, cb=cb: (0, 0, cb + p))
    lane = pl.BlockSpec((None, 1, LANES), lambda p: (p, 0, 0))
    gain = pl.BlockSpec((1, LANES), lambda p: (0, p))
    pair = pl.BlockSpec((bsz, s, LANES), lambda p: (0, 0, p))
    return blk, lane, gain, pair


def _ret_state_spec(bsz, n_chunk):
    spec = pl.BlockSpec((None, bsz, n_chunk, LANES, LANES), lambda p: (p, 0, 0, 0, 0))
    return spec, jax.ShapeDtypeStruct((4, bsz, n_chunk, LANES, LANES), F32)


def _ret_kv_states(tb, k_ref, v_ref, rb_ref, kvf_ref, n_chunk):
    c = CHUNK
    bsz = k_ref.shape[0]
    bd = tb["bd"]

    def contributions(n, carry):
        sl = pl.ds(pl.multiple_of(n * c, c), c)
        kfb = []
        for b in range(bsz):
            k32 = k_ref[b, sl, :].astype(F32)
            kfb.append(jnp.concatenate([k32 * tb["kdec_f"], k32 * tb["kdec_b"]], axis=1).astype(BF16))
        kvs = [_dot_tn(kfb[b], v_ref[b, sl, :]) for b in range(bsz)]
        for b in range(bsz):
            kvf_ref[b, n] = jnp.where(bd, kvs[b][0:LANES], 0.0)
            rb_ref[b, n] = jnp.where(bd, kvs[b][LANES:], 0.0)
        return carry

    lax.fori_loop(0, n_chunk, contributions, 0, unroll=2)

    def recur(i, rbs):
        n = n_chunk - 1 - i
        new = []
        for b in range(bsz):
            own = rb_ref[b, n]
            rb_ref[b, n] = rbs[b]
            new.append(rbs[b] * tb["cdec_b"] + own)
        return tuple(new)

    lax.fori_loop(0, n_chunk, recur, tuple(jnp.zeros((LANES, LANES), F32) for _ in range(bsz)))


def _split_rows(x, m0):
    return jnp.concatenate([jnp.where(m0, x, 0.0), jnp.where(m0, 0.0, x)], axis=0).astype(BF16)


def _ret_fwd(u3, lgf_l, lgb_l, gn_gain, rider=None):
    bsz, s, _ = u3.shape
    n_chunk = s // CHUNK
    c = CHUNK

    def body(q_ref, k_ref, v_ref, g_ref, lgf_ref, lgb_ref, gain_ref, yh_ref, rstd_ref, o_ref, rb_ref, kvf_ref):
        tb = _ret_tables(lgf_ref[...], lgb_ref[...])
        m0 = tb["m0"]
        gain = gain_ref[...]
        rows = range(bsz)
        _ret_kv_states(tb, k_ref, v_ref, rb_ref, kvf_ref, n_chunk)

        def chunk(n, rfs):
            sl = pl.ds(pl.multiple_of(n * c, c), c)
            qs = [q_ref[b, sl, :].astype(F32) * 0.125 for b in rows]
            s01 = [_dot_nt(_split_rows(qs[b], m0), k_ref[b, sl, :]) for b in rows]
            ys = []
            for b in rows:
                lhs = jnp.concatenate([s01[b][0:c] * tb["d0"], s01[b][c:] * tb["d1"],
                                       qs[b] * tb["qdec_f"], qs[b] * tb["qdec_b"]], axis=1).astype(BF16)
                rhs = jnp.concatenate([_split_rows(v_ref[b, sl, :].astype(F32), m0),
                                       rfs[b].astype(BF16), rb_ref[b, n].astype(BF16)], axis=0)
                ys.append(_dot(lhs, rhs))
            new = []
            for b in rows:
                y = ys[b]
                mu = _head_mean(y, m0)
                yc = y - mu
                rstd = lax.rsqrt(_head_mean(yc * yc, m0) + GN_EPS)
                yh = yc * rstd
                g = g_ref[b, sl, :].astype(F32)
                yh_ref[b, sl, :] = yh
                rstd_ref[b, sl, :] = rstd
                o_ref[b, sl, :] = (yh * gain * (g * _sigmoid(g))).astype(BF16)
                new.append(rfs[b] * tb["cdec_f"] + kvf_ref[b, n])
            return tuple(new)

        _loop_grouped(n_chunk, chunk, tuple(jnp.zeros((LANES, LANES), F32) for _ in rows))

    blk, lane, gain, pair = _ret_specs(bsz, s)
    state, state_shape = _ret_state_spec(bsz, n_chunk)
    return _hosted_call(
        body, "ret_fwd", (4,),
        in_specs=[blk(CB_RQ), blk(CB_RK), blk(CB_RV), blk(CB_RG), lane, lane, gain],
        out_specs=[pair, pair, pair, state, state],
        out_shape=[jax.ShapeDtypeStruct((bsz, s, RET_WIDTH), F32), jax.ShapeDtypeStruct((bsz, s, RET_WIDTH), F32),
                   jax.ShapeDtypeStruct((bsz, s, RET_WIDTH), BF16), state_shape, state_shape],
        scratch_shapes=[],
        operands=(u3, u3, u3, u3, lgf_l, lgb_l, gn_gain), rider=rider)


def _ret_bwd(u3, y_hat, y_rstd, states, d_o, lgf_l, lgb_l, gn_gain, rider=None):
    bsz, s, _ = u3.shape
    n_chunk = s // CHUNK
    c = CHUNK

    def body(q_ref, k_ref, v_ref, g_ref, yh_ref, rstd_ref, do_ref, lgf_ref, lgb_ref, gain_ref, rb_ref, kvf_ref,
             dq_ref, dk_ref, dv_ref, dg_ref, part_ref,
             rf_ref, dirf_ref, dy_ref, dk_acc, dv_acc, pa0, pa1, vec_ref):
        tb = _ret_tables(lgf_ref[...], lgb_ref[...])
        m0, bd, row = tb["m0"], tb["bd"], tb["row"]
        gain = gain_ref[...]
        wf = jnp.maximum(tb["diff"], 0.0)
        wb = jnp.maximum(-tb["diff"], 0.0)
        rows = range(bsz)
        zero_states = tuple(jnp.zeros((LANES, LANES), F32) for _ in rows)
        for ref in (pa0, pa1):
            ref[...] = jnp.zeros_like(ref)
        vec_ref[...] = jnp.zeros_like(vec_ref)

        def sweep_fwd(n, carry):
            rfs, gbs = carry
            sl = pl.ds(pl.multiple_of(n * c, c), c)
            qs, ks, vs, dys, dybs, q01, k01, dy01 = [], [], [], [], [], [], [], []
            dgain = jnp.zeros((1, LANES), F32)
            for b in rows:
                q = q_ref[b, sl, :].astype(F32) * 0.125
                k = k_ref[b, sl, :]
                yh = yh_ref[b, sl, :]
                rstd = rstd_ref[b, sl, :]
                do = do_ref[b, sl, :].astype(F32)
                g = g_ref[b, sl, :].astype(F32)
                sg = _sigmoid(g)
                sil = g * sg
                dyh = do * gain * sil
                dg_ref[b, sl, :] = (do * yh * gain * sg * (1.0 + g * (1.0 - sg))).astype(BF16)
                dgain = dgain + jnp.sum(do * yh * sil, axis=0, keepdims=True)
                dy = rstd * (dyh - _head_mean(dyh, m0) - yh * _head_mean(dyh * yh, m0))
                dyb = dy.astype(BF16)
                dy_ref[b, sl, :] = dyb
                rf_ref[b, n] = rfs[b]
                qs.append(q)
                ks.append(k)
                vs.append(v_ref[b, sl, :])
                dys.append(dy)
                dybs.append(dyb)
                q01.append(_split_rows(q, m0))
                k01.append(_split_rows(k.astype(F32), m0))
                dy01.append(_split_rows(dy, m0))
            s01 = [_dot_nt(q01[b], ks[b]) for b in rows]
            da01 = [_dot_nt(dy01[b], vs[b]) for b in rows]
            rbn = [rb_ref[b, n] for b in rows]
            states = [jnp.concatenate([rfs[b], rbn[b]], axis=0).astype(BF16) for b in rows]
            dqc = [_dot_nt(dybs[b], states[b]) for b in rows]
            gbb = [gbs[b].astype(BF16) for b in rows]
            dkb = [_dot_nt(vs[b], gbb[b]) for b in rows]
            qfb = [jnp.concatenate([qs[b] * tb["qdec_f"], qs[b] * tb["qdec_b"]], axis=1) for b in rows]
            direct = [_dot_tn(qfb[b].astype(BF16), dybs[b]) for b in rows]
            ds_cat, ds_rows, a_rows = [], [], []
            for b in rows:
                a0 = s01[b][0:c] * tb["d0"]
                a1 = s01[b][c:] * tb["d1"]
                pa0[...] += da01[b][0:c] * a0
                pa1[...] += da01[b][c:] * a1
                ds0 = da01[b][0:c] * tb["d0"]
                ds1 = da01[b][c:] * tb["d1"]
                ds_cat.append(jnp.concatenate([ds0, ds1], axis=1).astype(BF16))
                ds_rows.append(jnp.concatenate([ds0, ds1], axis=0).astype(BF16))
                a_rows.append(jnp.concatenate([a0, a1], axis=0).astype(BF16))
            kbd = [ks[b].astype(F32) * tb["kdec_b"] for b in rows]
            dq_in = [_dot(ds_cat[b], k01[b]) for b in rows]
            dk_in = [_dot_tn(ds_rows[b], q01[b]) for b in rows]
            dv_in = [_dot_tn(a_rows[b], dy01[b]) for b in rows]
            dv_gb = [_dot(kbd[b].astype(BF16), gbb[b]) for b in rows]
            new_rf, new_gb = [], []
            dlf = jnp.zeros((1, LANES), F32)
            dlb = jnp.zeros((1, LANES), F32)
            for b in rows:
                dqf, dqb = dqc[b][:, 0:LANES], dqc[b][:, LANES:]
                qf, qb = qfb[b][:, 0:LANES], qfb[b][:, LANES:]
                dq = dq_in[b] + dqf * tb["qdec_f"] + dqb * tb["qdec_b"]
                dq_ref[b, sl, :] = (dq * 0.125).astype(BF16)
                dk_acc[b, sl, :] = dk_in[b] + dkb[b] * tb["kdec_b"]
                dv_acc[b, sl, :] = dv_in[b] + dv_gb[b]
                dlf = dlf + jnp.sum((row + 1.0) * qf * dqf, axis=0, keepdims=True)
                dlb = dlb + jnp.sum((c - row) * qb * dqb + row * kbd[b] * dkb[b], axis=0, keepdims=True)
                dlb = dlb + c * tb["cdec_b"] * jnp.sum(gbs[b] * rbn[b], axis=0, keepdims=True)
                dirf_ref[b, n] = jnp.where(bd, direct[b][0:LANES], 0.0)
                new_gb.append(jnp.where(bd, direct[b][LANES:], 0.0) + tb["cdec_b"] * gbs[b])
                new_rf.append(rfs[b] * tb["cdec_f"] + kvf_ref[b, n])
            vec_ref[0:1, :] += dlf
            vec_ref[1:2, :] += dlb
            vec_ref[6:7, :] += dgain
            return tuple(new_rf), tuple(new_gb)

        _loop_grouped(n_chunk, sweep_fwd, (zero_states, zero_states), per_trip=4)

        def sweep_bwd(i, gfs):
            n = n_chunk - 1 - i
            sl = pl.ds(pl.multiple_of(n * c, c), c)
            gfb = [gfs[b].astype(BF16) for b in rows]
            kfd = [k_ref[b, sl, :].astype(F32) * tb["kdec_f"] for b in rows]
            dkf = [_dot_nt(v_ref[b, sl, :], gfb[b]) for b in rows]
            dvf = [_dot(kfd[b].astype(BF16), gfb[b]) for b in rows]
            new = []
            dlf = jnp.zeros((1, LANES), F32)
            for b in rows:
                dk_ref[b, sl, :] = (dk_acc[b, sl, :] + dkf[b] * tb["kdec_f"]).astype(BF16)
                dv_ref[b, sl, :] = (dv_acc[b, sl, :] + dvf[b]).astype(BF16)
                dlf = dlf + jnp.sum((c - 1.0 - row) * kfd[b] * dkf[b], axis=0, keepdims=True)
                dlf = dlf + c * tb["cdec_f"] * jnp.sum(gfs[b] * rf_ref[b, n], axis=0, keepdims=True)
                new.append(dirf_ref[b, n] + tb["cdec_f"] * gfs[b])
            vec_ref[0:1, :] += dlf
            return tuple(new)

        _loop_grouped(n_chunk, sweep_bwd, zero_states)
        vec_ref[2:3, :] = jnp.sum(pa0[...] * wf, axis=0, keepdims=True)
        vec_ref[3:4, :] = jnp.sum(pa1[...] * wf, axis=0, keepdims=True)
        vec_ref[4:5, :] = jnp.sum(pa0[...] * wb, axis=0, keepdims=True)
        vec_ref[5:6, :] = jnp.sum(pa1[...] * wb, axis=0, keepdims=True)
        part_ref[...] = vec_ref[...]

    blk, lane, gain, pair = _ret_specs(bsz, s)
    out_bf = jax.ShapeDtypeStruct((bsz, s, RET_WIDTH), BF16)
    state = pltpu.VMEM((bsz, n_chunk, LANES, LANES), F32)
    saved = _ret_state_spec(bsz, n_chunk)[0]
    return _hosted_call(
        body, "ret_bwd", (4,),
        in_specs=[blk(CB_RQ), blk(CB_RK), blk(CB_RV), blk(CB_RG), pair, pair, pair, lane, lane, gain, saved, saved],
        out_specs=[pair, pair, pair, pair, pl.BlockSpec((None, 8, LANES), lambda p: (p, 0, 0))],
        out_shape=[out_bf, out_bf, out_bf, out_bf, jax.ShapeDtypeStruct((4, 8, LANES), F32)],
        scratch_shapes=[state, state,
                        pltpu.VMEM((bsz, s, LANES), BF16), pltpu.VMEM((bsz, s, LANES), F32),
                        pltpu.VMEM((bsz, s, LANES), F32),
                        pltpu.VMEM((c, c), F32), pltpu.VMEM((c, c), F32), pltpu.VMEM((8, LANES), F32)],
        operands=(u3, u3, u3, u3, y_hat, y_rstd, d_o, lgf_l, lgb_l, gn_gain, *states), rider=rider)


def _attn_window_tables(n, s):
    qi = lax.broadcasted_iota(jnp.int32, (CHUNK, 3 * CHUNK), 0)
    kj = lax.broadcasted_iota(jnp.int32, (CHUNK, 3 * CHUNK), 1)
    dist = jnp.abs(kj - CHUNK - qi)
    kpos = n * CHUNK - CHUNK + kj
    valid = (dist <= CHUNK) & (kpos >= 0) & (kpos < s)
    return dist.astype(F32), valid


def _dup_kv_head(x, g):
    lane = lax.broadcasted_iota(jnp.int32, x.shape, 1)
    keep = (lane < HEAD_DIM) == (g == 0)
    xf = x.astype(F32)
    return jnp.where(keep, xf, pltpu.roll(xf, HEAD_DIM, 1))


def _attn_specs(s):
    q = pl.BlockSpec((None, s, 2 * LANES), lambda b, g: (b, 0, CB_AQ // 2 + g))
    k = pl.BlockSpec((None, s, LANES), lambda b, g: (b, 0, CB_AK))
    v = pl.BlockSpec((None, s, LANES), lambda b, g: (b, 0, CB_AV))
    grp = pl.BlockSpec((None, s, 2 * LANES), lambda b, g: (b, 0, g))
    smem = pl.BlockSpec(memory_space=pltpu.SMEM)
    return q, k, v, grp, smem


def _fill_padded(dst_ref, val, s):
    dst_ref[0:CHUNK, :] = jnp.zeros((CHUNK, LANES), dst_ref.dtype)
    dst_ref[CHUNK:CHUNK + s, :] = val.astype(dst_ref.dtype)
    dst_ref[CHUNK + s:2 * CHUNK + s, :] = jnp.zeros((CHUNK, LANES), dst_ref.dtype)


def _attn_probs(sc, slope, snk, dist, valid):
    sc = jnp.where(valid, sc - slope * dist, NEG_INF)
    m = jnp.maximum(jnp.max(sc, axis=1, keepdims=True), snk)
    e = jnp.exp(sc - m)
    es = jnp.exp(snk - m)
    inv = 1.0 / (jnp.sum(e, axis=1, keepdims=True) + es)
    return e * inv, es * inv


def _stack_heads(x2, m0):
    parts = []
    for pr in range(2):
        xp = x2[:, pr * LANES:(pr + 1) * LANES]
        parts += [jnp.where(m0, xp, 0.0), jnp.where(m0, 0.0, xp)]
    return jnp.concatenate(parts, axis=0).astype(BF16)


def _unstack_pair(x_all, pr, m0):
    return jnp.where(m0, x_all[(2 * pr) * CHUNK:(2 * pr + 1) * CHUNK], x_all[(2 * pr + 1) * CHUNK:(2 * pr + 2) * CHUNK])


def _attn_saved_specs(bsz, n_blk):
    specs = [pl.BlockSpec((None, None, n_blk, 4 * CHUNK, w), lambda b, g: (b, g, 0, 0, 0)) for w in (3 * CHUNK, 1)]
    shapes = [jax.ShapeDtypeStruct((bsz, 2, n_blk, 4 * CHUNK, 3 * CHUNK), BF16),
              jax.ShapeDtypeStruct((bsz, 2, n_blk, 4 * CHUNK, 1), F32)]
    return specs, shapes


def _attn_fwd(u3, slopes, sink, rider=None):
    bsz, s, _ = u3.shape
    n_blk = s // CHUNK

    def body(slope_ref, sink_ref, q_ref, k_ref, v_ref, o_ref, p_ref, ps_ref, kp_ref, vp_ref):
        g = pl.program_id(1)
        _fill_padded(kp_ref, _dup_kv_head(k_ref[...], g), s)
        _fill_padded(vp_ref, _dup_kv_head(v_ref[...], g), s)
        m0 = lax.broadcasted_iota(jnp.int32, (CHUNK, LANES), 1) < HEAD_DIM

        def blk(n, carry):
            r0 = pl.multiple_of(n * CHUNK, CHUNK)
            kw = kp_ref[pl.ds(r0, 3 * CHUNK), :]
            vw = vp_ref[pl.ds(r0, 3 * CHUNK), :]
            dist, valid = _attn_window_tables(n, s)
            q_all = _stack_heads(q_ref[pl.ds(r0, CHUNK), :].astype(F32) * 0.125, m0)
            sc_all = _dot_nt(q_all, kw)
            probs, sinks = [], []
            for i in range(4):
                p, ps = _attn_probs(sc_all[i * CHUNK:(i + 1) * CHUNK], slope_ref[g * 4 + i], sink_ref[g * 4 + i],
                                    dist, valid)
                probs.append(p.astype(BF16))
                sinks.append(ps)
            p_all = jnp.concatenate(probs, axis=0)
            p_ref[n] = p_all
            ps_ref[n] = jnp.concatenate(sinks, axis=0)
            out_all = _dot(p_all, vw)
            for pr in range(2):
                o_ref[pl.ds(r0, CHUNK), pr * LANES:(pr + 1) * LANES] = _unstack_pair(out_all, pr, m0).astype(BF16)
            return carry

        lax.fori_loop(0, n_blk, blk, 0, unroll=4)

    q, k, v, grp, smem = _attn_specs(s)
    saved_specs, saved_shapes = _attn_saved_specs(bsz, n_blk)
    return _hosted_call(
        body, "attn_fwd", (bsz, 2),
        in_specs=[smem, smem, q, k, v],
        out_specs=[grp] + saved_specs,
        out_shape=[jax.ShapeDtypeStruct((bsz, s, ATTN_WIDTH), BF16)] + saved_shapes,
        scratch_shapes=[pltpu.VMEM((s + 2 * CHUNK, LANES), BF16), pltpu.VMEM((s + 2 * CHUNK, LANES), BF16)],
        operands=(slopes, sink, u3, u3, u3), rider=rider)


def _attn_bwd(u3, d_o, probs, sink_probs, rider=None):
    bsz, s, _ = u3.shape
    n_blk = s // CHUNK

    def body(q_ref, k_ref, v_ref, do_ref, p_ref, ps_ref, dq_ref, dkv_ref, ds_ref,
             kp_ref, vp_ref, dk_acc, dv_acc):
        g = pl.program_id(1)
        _fill_padded(kp_ref, _dup_kv_head(k_ref[...], g), s)
        _fill_padded(vp_ref, _dup_kv_head(v_ref[...], g), s)
        dk_acc[...] = jnp.zeros_like(dk_acc)
        dv_acc[...] = jnp.zeros_like(dv_acc)
        m0 = lax.broadcasted_iota(jnp.int32, (CHUNK, LANES), 1) < HEAD_DIM

        def blk(n, dsink):
            r0 = pl.multiple_of(n * CHUNK, CHUNK)
            win = pl.ds(r0, 3 * CHUNK)
            kw = kp_ref[win, :]
            vw = vp_ref[win, :]
            q_all = _stack_heads(q_ref[pl.ds(r0, CHUNK), :].astype(F32) * 0.125, m0)
            do_all = _stack_heads(do_ref[pl.ds(r0, CHUNK), :].astype(F32), m0)
            p_all = p_ref[n]
            ps_all = ps_ref[n]
            dp_all = _dot_nt(do_all, vw)
            new_dsink, dscs = [], []
            for i in range(4):
                rows = slice(i * CHUNK, (i + 1) * CHUNK)
                p = p_all[rows].astype(F32)
                dp = dp_all[rows]
                delta = jnp.sum(p * dp, axis=1, keepdims=True)
                dscs.append((p * (dp - delta)).astype(BF16))
                dsh = jnp.sum(ps_all[rows] * delta, axis=0, keepdims=True)
                new_dsink.append(dsink[i] - jnp.broadcast_to(dsh, (1, LANES)))
            dsc_all = jnp.concatenate(dscs, axis=0)
            dq_all = _dot(dsc_all, kw)
            dk_acc[win, :] += _dot_tn(dsc_all, q_all)
            dv_acc[win, :] += _dot_tn(p_all, do_all)
            for pr in range(2):
                dq_ref[pl.ds(r0, CHUNK), pr * LANES:(pr + 1) * LANES] = (
                    _unstack_pair(dq_all, pr, m0) * 0.125).astype(BF16)
            return tuple(new_dsink)

        dsink = _loop_grouped(n_blk, blk, tuple(jnp.zeros((1, LANES), F32) for _ in range(4)), per_trip=4)
        dk = dk_acc[CHUNK:CHUNK + s, :]
        dv = dv_acc[CHUNK:CHUNK + s, :]
        lane = lax.broadcasted_iota(jnp.int32, (s, LANES), 1)
        fold = lambda a: a + pltpu.roll(a, HEAD_DIM, 1)
        dkv_ref[...] = jnp.where(lane < HEAD_DIM, fold(dk), fold(dv)).astype(BF16)
        ds_ref[...] = jnp.zeros_like(ds_ref)
        for i in range(4):
            ds_ref[i:i + 1, :] = dsink[i]

    q, k, v, grp, _ = _attn_specs(s)
    return _hosted_call(
        body, "attn_bwd", (bsz, 2),
        in_specs=[q, k, v, grp] + _attn_saved_specs(bsz, n_blk)[0],
        out_specs=[grp, pl.BlockSpec((None, s, LANES), lambda b, g: (b, 0, g)),
                   pl.BlockSpec((None, None, 8, LANES), lambda b, g: (b, g, 0, 0))],
        out_shape=[jax.ShapeDtypeStruct((bsz, s, ATTN_WIDTH), BF16), jax.ShapeDtypeStruct((bsz, s, 2 * LANES), BF16),
                   jax.ShapeDtypeStruct((bsz, 2, 8, LANES), F32)],
        scratch_shapes=[pltpu.VMEM((s + 2 * CHUNK, LANES), BF16), pltpu.VMEM((s + 2 * CHUNK, LANES), BF16),
                        pltpu.VMEM((s + 2 * CHUNK, LANES), F32), pltpu.VMEM((s + 2 * CHUNK, LANES), F32)],
        operands=(u3, u3, u3, d_o, probs, sink_probs), rider=rider)


def _ffn_bwd(dz2, gs, us, pg, ple, zh1, r1, g1, wg4, wu4, wd4, wpg, w_out):
    t = dz2.shape[0]
    tm = 256
    wg_t, wu_t, wd_all = (w.reshape(FFN, D_MODEL) for w in (wg4, wu4, wd4))

    def body(dz_ref, gs_ref, us_ref, pg_ref, ple_ref, zh_ref, r_ref, g1_ref,
             wg_hbm, wu_hbm, wd_hbm, wpg_hbm, wo_hbm,
             dgs_ref, dus_ref, dsp_ref, dple_ref, dz1_ref, dyr_ref, dya_ref, dg1_ref, db1_ref,
             wg, wu, wd, wpg, wo, wsem):
        step = pl.program_id(0)
        loads = _resident_quarters(wd_hbm, wd) + _resident_quarters(wg_hbm, wg) + _resident_quarters(wu_hbm, wu)
        _load_resident(step, loads + [(wpg_hbm, wpg), (wo_hbm, wo)], wsem)

        @pl.when(step == 0)
        def _():
            dg1_ref[...] = jnp.zeros_like(dg1_ref)
            db1_ref[...] = jnp.zeros_like(db1_ref)

        dz = dz_ref[...]
        dzb = dz.astype(BF16)
        dh = ALPHA * dz
        pending = []
        chunks = [slice(n * FFN_CHUNK, (n + 1) * FFN_CHUNK) for n in range(N_FFN_CHUNK)]
        for n in range(N_FFN_CHUNK + 1):
            if n < N_FFN_CHUNK:
                da = _dot_nt(dzb, wd[chunks[n], :])
                gj = gs_ref[:, chunks[n]].astype(F32)
                uj = us_ref[:, chunks[n]].astype(F32)
                sg = _sigmoid(gj)
                dgj = (da * uj * sg * (1.0 + gj * (1.0 - sg))).astype(BF16)
                duj = (da * gj * sg).astype(BF16)
                dgs_ref[:, chunks[n]] = dgj
                dus_ref[:, chunks[n]] = duj
                pending.append((dgj, duj))
            if n > 0:
                dgp, dup = pending[n - 1]
                dh = dh + _dot(dgp, wg[chunks[n - 1], :]) + _dot(dup, wu[chunks[n - 1], :])
        pgv = pg_ref[...].astype(F32)
        plev = ple_ref[...].astype(F32)
        dple_ref[...] = (dz * pgv).astype(BF16)
        dsp = (dz * plev * pgv * (1.0 - pgv)).astype(BF16)
        dsp_ref[...] = dsp
        dh = dh + _dot_nt(dsp, wpg[...])
        zh = zh_ref[...]
        dg1_ref[...] += jnp.sum(dh * zh, axis=0, keepdims=True)
        db1_ref[...] += jnp.sum(dh, axis=0, keepdims=True)
        dzh = dh * g1_ref[...]
        m1 = jnp.mean(dzh, axis=1, keepdims=True)
        m2 = jnp.mean(dzh * zh, axis=1, keepdims=True)
        dz1 = r_ref[...] * (dzh - m1 - zh * m2)
        dz1_ref[...] = dz1
        dyc = _dot_nt(dz1.astype(BF16), wo[...])
        dyr_ref[...] = dyc[:, 0:RET_WIDTH].astype(BF16)
        dya_ref[...] = dyc[:, RET_WIDTH:].astype(BF16)

    row = lambda w: pl.BlockSpec((tm, w), lambda i: (i, 0))
    const = lambda s: pl.BlockSpec(s, lambda i: (0, 0))
    hbm = pl.BlockSpec(memory_space=pl.ANY)
    hid_shape = jax.ShapeDtypeStruct((t, FFN), BF16)
    return pl.pallas_call(
        body, name="ffn_bwd", grid=(t // tm,),
        in_specs=[row(D_MODEL), row(FFN), row(FFN), row(D_MODEL), row(D_MODEL), row(D_MODEL), row(1),
                  const((1, D_MODEL)), hbm, hbm, hbm, hbm, hbm],
        out_specs=[row(FFN), row(FFN), row(D_MODEL), row(D_MODEL), row(D_MODEL), row(RET_WIDTH), row(ATTN_WIDTH),
                   const((1, D_MODEL)), const((1, D_MODEL))],
        out_shape=[hid_shape, hid_shape, jax.ShapeDtypeStruct((t, D_MODEL), BF16),
                   jax.ShapeDtypeStruct((t, D_MODEL), BF16), jax.ShapeDtypeStruct((t, D_MODEL), F32),
                   jax.ShapeDtypeStruct((t, RET_WIDTH), BF16), jax.ShapeDtypeStruct((t, ATTN_WIDTH), BF16),
                   jax.ShapeDtypeStruct((1, D_MODEL), F32), jax.ShapeDtypeStruct((1, D_MODEL), F32)],
        scratch_shapes=[pltpu.VMEM((FFN, D_MODEL), BF16), pltpu.VMEM((FFN, D_MODEL), BF16),
                        pltpu.VMEM((FFN, D_MODEL), BF16),
                        pltpu.VMEM(wpg.shape, BF16), pltpu.VMEM(w_out.shape, BF16),
                        pltpu.SemaphoreType.DMA((3 * N_SHARD + 2,))],
        compiler_params=_params("arbitrary", vmem=VMEM_LIMIT),
    )(dz2, gs, us, pg, ple, zh1, r1, g1, wg_t, wu_t, wd_all, wpg, w_out)


def _wgrad_misc(y_ret, y_att, dz1, hb, dsp, p2d, dple, rider=None):
    t = dz1.shape[0]
    tk = min(t, 512)
    pc = D_MODEL // N_SHARD

    def body(yr_ref, ya_ref, dz_ref, hb_ref, dsp_ref, p_ref, dple_ref, wo_ref, wpg_ref, wpe_ref):
        @pl.when(pl.program_id(0) == 0)
        def _():
            wo_ref[...] = jnp.zeros_like(wo_ref)
            wpg_ref[...] = jnp.zeros_like(wpg_ref)
            wpe_ref[...] = jnp.zeros_like(wpe_ref)

        dzb = dz_ref[...].astype(BF16)
        wo_ref[0:RET_WIDTH, :] += _dot_tn(yr_ref[...], dzb)
        wo_ref[RET_WIDTH:, :] += _dot_tn(ya_ref[...], dzb)
        wpg_ref[...] += _dot_tn(hb_ref[...], dsp_ref[...])
        dpe = _dot_tn(p_ref[...].astype(BF16), dple_ref[...])
        for j in range(N_SHARD):
            wpe_ref[j] += dpe[:, j * pc:(j + 1) * pc]

    row = lambda w: pl.BlockSpec((tk, w), lambda k: (k, 0))
    const = lambda s: pl.BlockSpec(s, lambda k: (0,) * len(s))
    return _hosted_call(
        body, "wgrad_misc", (t // tk,),
        in_specs=[row(RET_WIDTH), row(ATTN_WIDTH), row(D_MODEL), row(D_MODEL), row(D_MODEL), row(PLE_DIM),
                  row(D_MODEL)],
        out_specs=[const((D_MODEL, D_MODEL)), const((D_MODEL, D_MODEL)), const((N_SHARD, PLE_DIM, pc))],
        out_shape=[jax.ShapeDtypeStruct((D_MODEL, D_MODEL), F32), jax.ShapeDtypeStruct((D_MODEL, D_MODEL), F32),
                   jax.ShapeDtypeStruct((N_SHARD, PLE_DIM, pc), F32)],
        scratch_shapes=[], operands=(y_ret, y_att, dz1, hb, dsp, p2d, dple), rider=rider, semantics=["arbitrary"])


def _wgrad_ffn(acts, dgs, dus, hb, dz2b):
    t = dz2b.shape[0]
    tk = min(t, 512)
    nk = t // tk

    def body(act_ref, dg_ref, du_ref, hb_ref, dz_ref, og_ref, ou_ref, od_ref):
        @pl.when(pl.program_id(1) == 0)
        def _():
            og_ref[...] = jnp.zeros_like(og_ref)
            ou_ref[...] = jnp.zeros_like(ou_ref)
            od_ref[...] = jnp.zeros_like(od_ref)

        hbv = hb_ref[...]
        og_ref[...] += _dot_tn(dg_ref[...], hbv)
        ou_ref[...] += _dot_tn(du_ref[...], hbv)
        od_ref[...] += _dot_tn(act_ref[...], dz_ref[...])

    half = FFN // 2
    a_spec = pl.BlockSpec((tk, half), lambda j, k: (k, j))
    b_spec = pl.BlockSpec((tk, D_MODEL), lambda j, k: (k, 0))
    o_spec = pl.BlockSpec((half, D_MODEL), lambda j, k: (j, 0))
    o_shape = jax.ShapeDtypeStruct((FFN, D_MODEL), F32)
    outs = pl.pallas_call(
        body, name="wgrad_ffn", grid=(2, nk),
        in_specs=[a_spec, a_spec, a_spec, b_spec, b_spec],
        out_specs=[o_spec] * 3, out_shape=[o_shape] * 3,
        compiler_params=_params("parallel", "arbitrary", vmem=VMEM_LIMIT),
    )(acts, dgs, dus, hb, dz2b)
    return [o.reshape(N_SHARD, FFN_SHARD, D_MODEL) for o in outs]


KV_ORDER = (0, 128, 64, 192)


def _wgrad_in(pieces, x2d, rider=None):
    t = x2d.shape[0]
    tk = min(t, 512)
    nk = t // tk
    kv0 = CB_AK * LANES

    def body(p0, p1, p2, p3, p4, pkv, x_ref, o_ref):
        @pl.when(pl.program_id(0) == 0)
        def _():
            o_ref[...] = jnp.zeros_like(o_ref)

        xb = x_ref[...].astype(BF16)
        for i, ref in enumerate((p0, p1, p2, p3, p4)):
            o_ref[i * 512:(i + 1) * 512, :] += _dot_tn(ref[...], xb)
        dkv = _dot_tn(pkv[...], xb)
        for i, o in enumerate(KV_ORDER):
            o_ref[kv0 + o:kv0 + o + HEAD_DIM, :] += dkv[i * HEAD_DIM:(i + 1) * HEAD_DIM]

    row = lambda w: pl.BlockSpec((tk, w), lambda k: (k, 0))
    return _hosted_call(
        body, "wgrad_in", (nk,),
        in_specs=[row(512)] * 5 + [row(256), row(D_MODEL)],
        out_specs=[pl.BlockSpec((IN_WIDTH, D_MODEL), lambda k: (0, 0))],
        out_shape=[jax.ShapeDtypeStruct((IN_WIDTH, D_MODEL), F32)],
        scratch_shapes=[], operands=(*pieces, x2d), rider=rider, semantics=["arbitrary"])


def _inproj_bwd(dz1, pieces, w_in_t, w_kv, rider=None):
    t = dz1.shape[0]
    tm = 512
    n_main = 5 * 512

    def body(dz_ref, p0, p1, p2, p3, p4, pkv, wm_ref, wkv_ref, o_ref):
        acc = ALPHA * dz_ref[...]
        for i, ref in enumerate((p0, p1, p2, p3, p4)):
            acc = acc + _dot(ref[...], wm_ref[i * 512:(i + 1) * 512, :])
        o_ref[...] = acc + _dot(pkv[...], wkv_ref[...])

    row = lambda w: pl.BlockSpec((tm, w), lambda i: (i, 0))
    const = lambda s: pl.BlockSpec(s, lambda i: (0, 0))
    return _hosted_call(
        body, "inproj_bwd", (t // tm,),
        in_specs=[row(D_MODEL)] + [row(512)] * 5 + [row(256), const((n_main, D_MODEL)), const(w_kv.shape)],
        out_specs=[row(D_MODEL)],
        out_shape=[jax.ShapeDtypeStruct((t, D_MODEL), F32)],
        scratch_shapes=[], operands=(dz1, *pieces, w_in_t, w_kv), rider=rider)


def _coords():
    return lax.axis_index("x"), lax.axis_index("y"), lax.axis_index("c")


def _chip_of(x, y, rel):
    return (1 - x if rel & 2 else x), (1 - y if rel & 1 else y)


def _all_gather_weights(shards):
    near = _gather_near_rider(shards)
    relay = _gather_relay_rider(near.out_shapes, chained=True)
    pass_near = _gather_pass_rider(near.out_shapes, chained=True, rels=NEAR)
    pass_far = _gather_pass_rider(near.out_shapes, chained=True, rels=(3,))
    steps = [(near, "start"), (near, "finish"), (relay, "start"), (pass_near, "start"), (relay, "finish"),
             (pass_far, "start"), (pass_near, "finish"), (pass_far, "finish")]
    return _run_riders("gather_weights", shards, near.out_shapes, [near, relay, pass_near, pass_far], steps)


def _run_riders(name, ins, out_shapes, riders, steps):
    n_in, n_out = len(ins), len(out_shapes)

    def body(*refs):
        in_refs, out_refs = refs[:n_in], refs[n_in:n_in + n_out]
        k, sems = n_in + n_out, {}
        for r in riders:
            sems[id(r)] = refs[k:k + len(r.sems)]
            k += len(r.sems)
        for r, method in steps:
            getattr(r, method)(in_refs, out_refs, sems[id(r)])

    hbm = pl.BlockSpec(memory_space=pl.ANY)
    return pl.pallas_call(
        body, name=name, in_specs=[hbm] * n_in, out_specs=[hbm] * n_out, out_shape=list(out_shapes),
        scratch_shapes=[s for r in riders for s in r.sems],
    )(*ins)


def _gather_half(outs, w, chip, cc):
    h = outs[w].shape[1] // 2
    return outs[w].at[chip, pl.ds(cc * h, h), :]


NEAR = (1, 2)


def _gather_near_rider(shards, rels=NEAR):
    nw, nr = len(shards), len(rels)

    def copies(ins, outs, sems, arrivals):
        send, recv, lsend, lrecv = sems
        x, y, c = _coords()
        me = 2 * x + y
        own = [pltpu.make_async_remote_copy(
            src_ref=ins[w], dst_ref=outs[w].at[me], send_sem=lsend.at[w], recv_sem=lrecv.at[w],
            device_id=(x, y, 1 - c), device_id_type=MESH) for w in range(nw)]
        out, arrive = [], []
        for i, rel in enumerate(rels):
            kx, ky = _chip_of(x, y, rel)
            for w in range(nw):
                h = shards[w].shape[0] // 2
                sem = dict(send_sem=send.at[w * nr + i], recv_sem=recv.at[w * nr + i],
                           device_id=(kx, ky, c), device_id_type=MESH)
                out.append(pltpu.make_async_remote_copy(
                    src_ref=ins[w].at[pl.ds(c * h, h), :], dst_ref=_gather_half(outs, w, me, c), **sem))
                if arrivals:
                    theirs = _gather_half(outs, w, 2 * kx + ky, c)
                    arrive.append(pltpu.make_async_remote_copy(src_ref=theirs, dst_ref=theirs, **sem))
        return own, out, arrive

    def start(ins, outs, sems):
        own, out, _ = copies(ins, outs, sems, arrivals=False)
        for cp in own + out:
            cp.start()

    def finish(ins, outs, sems):
        own, out, arrive = copies(ins, outs, sems, arrivals=True)
        for cp in arrive:
            cp.wait_recv()
        for cp in out:
            cp.wait_send()
        for cp in own:
            cp.wait()

    dma = pltpu.SemaphoreType.DMA
    return _Rider(shards, [jax.ShapeDtypeStruct((N_SHARD,) + s.shape, s.dtype) for s in shards],
                  [dma((nr * nw,)), dma((nr * nw,)), dma((nw,)), dma((nw,))], start, finish)


def _gather_relay_rider(gathered, chained=False):
    nw = len(gathered)

    def quarter(outs, w, chip, c, p):
        q = outs[w].shape[1] // 4
        return outs[w].at[chip, pl.ds(c * 2 * q + p * q, q), :]

    def copies(outs, sems):
        send, recv = sems
        x, y, c = _coords()
        (yx, yy), (xx, xy), (dx, dy) = (_chip_of(x, y, rel) for rel in (1, 2, 3))
        out, arrive = [], []
        for w in range(nw):
            for p, (src_chip, dst) in enumerate(((2 * xx + xy, (yx, yy)), (2 * yx + yy, (xx, xy)))):
                rows = quarter(outs, w, src_chip, c, p)
                sem = dict(send_sem=send.at[w * 2 + p], recv_sem=recv.at[w * 2 + p], device_id_type=MESH)
                out.append(pltpu.make_async_remote_copy(src_ref=rows, dst_ref=rows, device_id=(*dst, c), **sem))
                mine = quarter(outs, w, 2 * dx + dy, c, p)
                arrive.append(pltpu.make_async_remote_copy(src_ref=mine, dst_ref=mine, device_id=(*dst, c), **sem))
        return out, arrive

    def start(ins, outs, sems):
        for cp in copies(outs, sems)[0]:
            cp.start()

    def finish(ins, outs, sems):
        out, arrive = copies(outs, sems)
        for cp in arrive:
            cp.wait_recv()
        for cp in out:
            cp.wait_send()

    dma = pltpu.SemaphoreType.DMA
    shapes = [jax.ShapeDtypeStruct(g.shape, g.dtype) for g in gathered]
    if chained:
        return _Rider([], [], [dma((2 * nw,)), dma((2 * nw,))], start, finish)
    return _Rider(gathered, shapes, [dma((2 * nw,)), dma((2 * nw,))], start, finish,
                  aliases={w: w for w in range(nw)})


def _gather_pass_rider(gathered, chained=False, rels=(1, 2, 3)):
    nw, nr = len(gathered), len(rels)

    def copies(outs, sems, cc):
        send, recv = sems
        x, y, c = _coords()
        res = []
        for i, rel in enumerate(rels):
            kx, ky = _chip_of(x, y, rel)
            for w in range(nw):
                rows = _gather_half(outs, w, 2 * kx + ky, cc)
                res.append(pltpu.make_async_remote_copy(
                    src_ref=rows, dst_ref=rows, send_sem=send.at[w * nr + i], recv_sem=recv.at[w * nr + i],
                    device_id=(x, y, 1 - c), device_id_type=MESH))
        return res

    def start(ins, outs, sems):
        for cp in copies(outs, sems, lax.axis_index("c")):
            cp.start()

    def finish(ins, outs, sems):
        c = lax.axis_index("c")
        for cp in copies(outs, sems, 1 - c):
            cp.wait_recv()
        for cp in copies(outs, sems, c):
            cp.wait_send()

    dma = pltpu.SemaphoreType.DMA
    shapes = [jax.ShapeDtypeStruct(g.shape, g.dtype) for g in gathered]
    if chained:
        return _Rider([], [], [dma((nr * nw,)), dma((nr * nw,))], start, finish)
    return _Rider(gathered, shapes, [dma((nr * nw,)), dma((nr * nw,))], start, finish,
                  aliases={w: w for w in range(nw)})


def _exchange_halves_rider(parts):
    nw = len(parts)

    def copies(ins, outs, sems):
        send, recv = sems
        x, y, c = _coords()
        res = []
        for w in range(nw):
            h = parts[w].shape[1] // 2
            res.append(pltpu.make_async_remote_copy(
                src_ref=ins[w].at[:, pl.ds((1 - c) * h, h), :], dst_ref=outs[w],
                send_sem=send.at[w], recv_sem=recv.at[w], device_id=(x, y, 1 - c), device_id_type=MESH))
        return res

    def start(ins, outs, sems):
        for cp in copies(ins, outs, sems):
            cp.start()

    def finish(ins, outs, sems):
        for cp in copies(ins, outs, sems):
            cp.wait()

    dma = pltpu.SemaphoreType.DMA
    return _Rider(parts, [jax.ShapeDtypeStruct((N_SHARD, p.shape[1] // 2, p.shape[2]), p.dtype) for p in parts],
                  [dma((nw,)), dma((nw,))], start, finish)


def _add_halves(parts, theirs, pos):
    nw = len(parts)
    split = 2

    def body(pos_ref, *refs):
        ins, oth = refs[:nw], refs[nw:2 * nw]
        o32, o16 = refs[2 * nw:3 * nw], refs[3 * nw:]
        sums = [ins[w][...] + oth[w][...] for w in range(nw)]
        for w in range(nw):
            o16[w][...] = sums[w].astype(BF16)

        @pl.when(pl.program_id(1) == pos_ref[0])
        def _():
            for w in range(nw):
                o32[w][...] = sums[w]

    in_specs, oth_specs, o32_specs, shapes32, shapes16 = [], [], [], [], []
    for p in parts:
        hb = p.shape[1] // 2 // split
        blk = (None, hb, p.shape[2])
        in_specs.append(pl.BlockSpec(blk, lambda i, j, pos_ref: (j, pos_ref[1] * split + i, 0)))
        oth_specs.append(pl.BlockSpec(blk, lambda i, j, pos_ref: (j, i, 0)))
        o32_specs.append(pl.BlockSpec((hb, p.shape[2]), lambda i, j, pos_ref: (i, 0)))
        shapes32.append(jax.ShapeDtypeStruct((p.shape[1] // 2, p.shape[2]), F32))
        shapes16.append(jax.ShapeDtypeStruct((N_SHARD, p.shape[1] // 2, p.shape[2]), BF16))
    return pl.pallas_call(
        body, name="add_halves",
        grid_spec=pltpu.PrefetchScalarGridSpec(
            num_scalar_prefetch=1, grid=(split, N_SHARD),
            in_specs=in_specs + oth_specs, out_specs=o32_specs + oth_specs),
        out_shape=shapes32 + shapes16,
        compiler_params=_params("parallel", "arbitrary", vmem=VMEM_LIMIT),
    )(pos, *parts, *theirs)


def _exchange_chips_rider(sums16, rows=None, into=None):
    nw = len(sums16)
    rows = rows or [(0, s.shape[1]) for s in sums16]
    held = [w for w in range(nw) if into is not None and into[w] is not None]

    def copies(ins, outs, sems):
        send, recv = sems
        x, y, c = _coords()
        res = []
        for rel in (1, 2, 3):
            kx, ky = _chip_of(x, y, rel)
            for w in range(nw):
                r0, n = rows[w]
                res.append(pltpu.make_async_remote_copy(
                    src_ref=ins[w].at[2 * kx + ky, pl.ds(r0, n), :], dst_ref=outs[w].at[rel - 1, pl.ds(r0, n), :],
                    send_sem=send.at[w * 3 + rel - 1], recv_sem=recv.at[w * 3 + rel - 1],
                    device_id=(kx, ky, c), device_id_type=MESH))
        return res

    def start(ins, outs, sems):
        for cp in copies(ins, outs, sems):
            cp.start()

    def finish(ins, outs, sems):
        for cp in copies(ins, outs, sems):
            cp.wait()

    dma = pltpu.SemaphoreType.DMA
    return _Rider(list(sums16) + [into[w] for w in held],
                  [jax.ShapeDtypeStruct((3,) + s.shape[1:], BF16) for s in sums16],
                  [dma((3 * nw,)), dma((3 * nw,))], start, finish, aliases={nw + i: w for i, w in enumerate(held)})


def _add_chips(sums32, theirs, pos):
    nw = len(sums32)
    split = 2
    hbs = [s.shape[0] // split for s in sums32]

    def body(pos_ref, *refs):
        ins, oth, outs, bufs = (refs[k * nw:(k + 1) * nw] for k in range(4))
        lsem, ssem, rsem = refs[4 * nw:]
        i = pl.program_id(0)
        x, y, c = _coords()

        def copies(w, j):
            rows = pl.ds(pl.multiple_of((pos_ref[1] * split + j) * hbs[w], 8), hbs[w])
            return (pltpu.make_async_copy(bufs[w].at[j], outs[w].at[rows, :], lsem.at[w, j]),
                    pltpu.make_async_remote_copy(
                        src_ref=bufs[w].at[j], dst_ref=outs[w].at[rows, :], send_sem=ssem.at[w, j],
                        recv_sem=rsem.at[w, j], device_id=(x, y, 1 - c), device_id_type=MESH))

        for w in range(nw):
            acc = ins[w][...]
            for r in range(3):
                acc = acc + oth[w][r].astype(F32)
            bufs[w][i] = acc
            for cp in copies(w, i):
                cp.start()

        @pl.when(i == split - 1)
        def _():
            for w in range(nw):
                for j in range(split):
                    local, remote = copies(w, j)
                    local.wait()
                    remote.wait()

    in_specs, oth_specs, shapes, scratch = [], [], [], []
    for s, hb in zip(sums32, hbs):
        in_specs.append(pl.BlockSpec((hb, s.shape[1]), lambda i, pos_ref: (i, 0)))
        oth_specs.append(pl.BlockSpec((3, hb, s.shape[1]), lambda i, pos_ref: (0, i, 0)))
        shapes.append(jax.ShapeDtypeStruct((2 * s.shape[0], s.shape[1]), F32))
        scratch.append(pltpu.VMEM((split, hb, s.shape[1]), F32))
    dma = pltpu.SemaphoreType.DMA
    return pl.pallas_call(
        body, name="add_chips",
        grid_spec=pltpu.PrefetchScalarGridSpec(
            num_scalar_prefetch=1, grid=(split,), in_specs=in_specs + oth_specs,
            out_specs=[pl.BlockSpec(memory_space=pl.ANY)] * nw,
            scratch_shapes=scratch + [dma((nw, split)), dma((nw, split)), dma((nw, split))]),
        out_shape=shapes,
        compiler_params=_params("arbitrary", vmem=VMEM_LIMIT),
    )(pos, *sums32, *theirs)


def _adamw_math(w, g, m, v):
    m = ADAM_B1 * m + (1.0 - ADAM_B1) * g
    v = ADAM_B2 * v + (1.0 - ADAM_B2) * (g * g)
    m_hat = m / (1.0 - ADAM_B1 ** ADAM_STEP)
    v_hat = v / (1.0 - ADAM_B2 ** ADAM_STEP)
    delta = -ADAM_LR * (m_hat / (jnp.sqrt(v_hat) + ADAM_EPS) + ADAM_WD * w)
    return delta, m, v


def _adamw(ws, gs, ms, vs):
    nw = len(ws)
    split = 8

    def body(*refs):
        w_r, g_r, m_r, v_r = (refs[i * nw:(i + 1) * nw] for i in range(4))
        g_o, d_o, m_o, v_o = (refs[(4 + i) * nw:(5 + i) * nw] for i in range(4))
        for k in range(nw):
            g = g_r[k][...]
            d, m, v = _adamw_math(w_r[k][...], g, m_r[k][...], v_r[k][...])
            g_o[k][...] = g
            d_o[k][...] = d
            m_o[k][...] = m
            v_o[k][...] = v

    specs = [pl.BlockSpec((w.shape[0] // split, w.shape[1]), lambda i: (i, 0)) for w in ws]
    shapes = [jax.ShapeDtypeStruct(w.shape, F32) for w in ws]
    outs = pl.pallas_call(
        body, name="adamw", grid=(split,),
        in_specs=specs * 4, out_specs=specs * 4, out_shape=shapes * 4,
        compiler_params=_params("parallel", vmem=VMEM_LIMIT),
    )(*ws, *gs, *ms, *vs)
    return outs[:nw], outs[nw:2 * nw], outs[2 * nw:3 * nw], outs[3 * nw:]


SMALL_ROWS = 8
SMALL_COLS = D_MODEL
LOSS_COL = RET_WIDTH + 24


def _small_allreduce_adamw(part, w, m, v, rider=None):
    def body(part_ref, w_ref, m_ref, v_ref, g_out, d_out, m_out, v_out, all_ref, send, recv):
        x, y, c = _coords()
        me = 4 * x + 2 * y + c
        all_ref[me] = part_ref[...]
        copies = []
        for rel in range(1, 8):
            px = 1 - x if rel & 4 else x
            py = 1 - y if rel & 2 else y
            pc = 1 - c if rel & 1 else c
            copies.append(pltpu.make_async_remote_copy(
                src_ref=part_ref, dst_ref=all_ref.at[me],
                send_sem=send.at[rel - 1], recv_sem=recv.at[rel - 1], device_id=(px, py, pc), device_id_type=MESH))
        for cp in copies:
            cp.start()
        for cp in copies:
            cp.wait()
        g = all_ref[0]
        for k in range(1, 8):
            g = g + all_ref[k]
        d, mn, vn = _adamw_math(w_ref[...], g, m_ref[...], v_ref[...])
        g_out[...] = g
        d_out[...] = d
        m_out[...] = mn
        v_out[...] = vn

    vm = pl.BlockSpec(memory_space=pltpu.VMEM)
    shape = jax.ShapeDtypeStruct((SMALL_ROWS, SMALL_COLS), F32)
    return _hosted_call(
        body, "small_allreduce_adamw", (1,),
        in_specs=[vm] * 4, out_specs=[vm] * 4, out_shape=[shape] * 4,
        scratch_shapes=[pltpu.VMEM((8, SMALL_ROWS, SMALL_COLS), F32),
                        pltpu.SemaphoreType.DMA((7,)), pltpu.SemaphoreType.DMA((7,))],
        operands=(part, w, m, v), rider=rider, semantics=["arbitrary"])


SMALL_NAMES = ("ret_decay_fwd", "ret_decay_bwd", "attn_sink", "ret_gn_gain",
               "ln1_gain", "ln1_bias", "ln2_gain", "ln2_bias")


LN_NAMES = ("ln1_gain", "ln1_bias", "ln2_gain", "ln2_bias")


def _pack_small(vals, extra=None):
    tail = jnp.zeros((1, 1), F32) if extra is None else extra.reshape(1, 1)
    row4 = jnp.concatenate([vals["ret_gn_gain"], vals["ret_decay_fwd"], vals["ret_decay_bwd"], vals["attn_sink"],
                            tail, jnp.zeros((1, SMALL_COLS - LOSS_COL - 1), F32)], axis=1)
    rows = [vals[n] for n in LN_NAMES] + [row4, jnp.zeros((SMALL_ROWS - 5, SMALL_COLS), F32)]
    return jnp.concatenate(rows, axis=0)


def _unpack_small(packed):
    o = RET_WIDTH
    where = [(n, i, 0, SMALL_COLS) for i, n in enumerate(LN_NAMES)] + [
        ("ret_gn_gain", 4, 0, o), ("ret_decay_fwd", 4, o, 8), ("ret_decay_bwd", 4, o + 8, 8),
        ("attn_sink", 4, o + 16, 8)]
    na, k = len(packed), len(where)

    def body(*refs):
        for a in range(na):
            rows = refs[a][...]
            for b, (_, row, col, n) in enumerate(where):
                refs[na + a * k + b][...] = rows[row:row + 1, col:col + n]

    vmem = pl.BlockSpec(memory_space=pltpu.VMEM)
    outs = pl.pallas_call(
        body, name="unpack_small", in_specs=[vmem] * na, out_specs=[vmem] * (na * k),
        out_shape=[jax.ShapeDtypeStruct((1, n), F32) for _ in range(na) for (_, _, _, n) in where])(*packed)
    return [{where[b][0]: outs[a * k + b] for b in range(k)} for a in range(na)]


def _local_step(x, p, tgt, w_in_t, rest, small, pos=None, small_state=None):
    bsz, s, _ = x.shape
    t = bsz * s
    x2d = x.reshape(t, D_MODEL)
    p2d = p.reshape(t, PLE_DIM)
    tgt2d = tgt.reshape(t, D_MODEL)
    dec_f = small["ret_decay_fwd"].reshape(8)
    dec_b = small["ret_decay_bwd"].reshape(8)
    lg_f = jnp.log1p(-jnp.exp2(dec_f))
    lg_b = jnp.log1p(-jnp.exp2(dec_b))
    per_lane = lambda v: jnp.repeat(v, HEAD_DIM).reshape(4, 1, LANES)
    lgf_l, lgb_l = per_lane(lg_f), per_lane(lg_b)
    sink = small["attn_sink"].reshape(8)
    slopes = 2.0 ** (-(jnp.arange(8, dtype=F32) + 1.0))
    gn_gain = small["ret_gn_gain"]
    g1, b1, g2, b2 = (small[n] for n in ("ln1_gain", "ln1_bias", "ln2_gain", "ln2_bias"))

    dist = pos is not None
    shard = dict(zip(REST_NAMES, rest)) if dist else {}
    near = lambda names, rels=NEAR: _gather_near_rider([shard[n] for n in names], rels)
    wave1, wave2, wave3 = ("w_out", "w_ple_gate", "w_ffn_gate"), ("w_ffn_up", "w_ple_proj"), ("w_ffn_down",)
    n1 = len(wave1)
    u, *o1 = _inproj(x2d, w_in_t, rider=near(wave1) if dist else None)
    u3 = u.reshape(bsz, s, IN_WIDTH)
    y_hat, y_rstd, y_ret, ret_rb, ret_kvf, *o2 = _ret_fwd(u3, lgf_l, lgb_l, gn_gain, rider=_merge_riders(
        [_gather_relay_rider(o1), near(wave2)]) if dist else None)
    y_att, att_p, att_ps, *o3 = _attn_fwd(u3, slopes, sink, rider=_merge_riders(
        [_gather_pass_rider(o2[:n1]), _gather_relay_rider(o2[n1:]), near(wave3, (1, 2, 3))]) if dist else None)
    gathered = dict(zip(wave1, o3[:n1]))
    w_out = _assemble_weights({"w_out": gathered["w_out"]})["w_out"] if dist else rest["w_out"]
    zh1, r1, hb, *o4 = _outproj_ln1(y_ret.reshape(t, RET_WIDTH), y_att.reshape(t, ATTN_WIDTH), x2d, w_out, g1, b1,
                                    rider=_gather_pass_rider(o3[n1:]) if dist else None)
    gathered.update(zip(wave2 + wave3, o4))
    wts = _assemble_weights(gathered) if dist else rest
    dz2, dz2b, gs, us, acts, pg, ple, sq, dg2, db2 = _ffn_fwd(
        zh1, hb, p2d, tgt2d, g1, b1, g2, b2, wts["gate4"], wts["up4"], wts["down4"], wts["ple_proj"], wts["ple_gate"])
    dgs, dus, dsp, dple, dz1, dyr, dya, dg1, db1 = _ffn_bwd(dz2, gs, us, pg, ple, zh1, r1, g1, wts["gate4"],
                                                          wts["up4"], wts["down4"], wts["ple_gate"], wts["w_out"])
    ffn_parts = list(_wgrad_ffn(acts, dgs, dus, hb, dz2b))
    d_w_out, d_ple_gate, d_ple_proj, *th_ffn = _wgrad_misc(
        y_ret.reshape(t, RET_WIDTH), y_att.reshape(t, ATTN_WIDTH), dz1, hb, dsp, p2d, dple,
        rider=_exchange_halves_rider(ffn_parts[:2]) if dist else None)
    misc_parts = [d_w_out.reshape(N_SHARD, D_MODEL // N_SHARD, D_MODEL), d_ple_proj,
                  d_ple_gate.reshape(N_SHARD, D_MODEL // N_SHARD, D_MODEL)]
    dyr3, dya3 = dyr.reshape(bsz, s, RET_WIDTH), dya.reshape(bsz, s, ATTN_WIDTH)
    if dist:
        s_gu = _add_halves(ffn_parts[:2], th_ffn, pos)
        half = FFN_SHARD // 2
        quarter = half // 2
        later_parts = [ffn_parts[2]] + misc_parts
        drq, drk, drv, drg, rpart, *o5 = _ret_bwd(u3, y_hat, y_rstd, (ret_rb, ret_kvf), dyr3, lgf_l, lgb_l, gn_gain,
                                                  rider=_merge_riders(
            [_exchange_chips_rider(s_gu[2:], rows=[(0, half), (0, quarter)]), _exchange_halves_rider(later_parts)]))
        s_dm = _add_halves(later_parts, o5[2:], pos)
        daq, dakv, spart, *o6 = _attn_bwd(u3, dya3, att_p, att_ps, rider=_exchange_chips_rider(
            [s_gu[3], s_dm[4]], rows=[(quarter, half - quarter), (0, half)], into=[o5[1], None]))
    else:
        drq, drk, drv, drg, rpart = _ret_bwd(u3, y_hat, y_rstd, (ret_rb, ret_kvf), dyr3, lgf_l, lgb_l, gn_gain)
        daq, dakv, spart = _attn_bwd(u3, dya3, att_p, att_ps)
    pieces = [a.reshape(t, -1) for a in (drq, drk, drv, drg, daq, dakv)]
    kv0 = CB_AK * LANES
    w_kv = jnp.concatenate([w_in_t[kv0 + o:kv0 + o + HEAD_DIM] for o in KV_ORDER], axis=0)
    d_in, *o7 = _wgrad_in(pieces, x2d, rider=_exchange_chips_rider(list(s_dm[5:])) if dist else None)
    d_in = d_in.reshape(N_SHARD, FFN_SHARD, D_MODEL)

    rsum = rpart
    lane_heads = lambda row: jnp.sum(row.reshape(4, 2, HEAD_DIM), axis=-1).reshape(8)
    dlg_f = lane_heads(rsum[:, 0, :]) + jnp.stack([jnp.sum(rsum[:, 2, :], -1), jnp.sum(rsum[:, 3, :], -1)], 1).reshape(8)
    dlg_b = lane_heads(rsum[:, 1, :]) + jnp.stack([jnp.sum(rsum[:, 4, :], -1), jnp.sum(rsum[:, 5, :], -1)], 1).reshape(8)
    chain = lambda d: -(math.log(2.0) * jnp.exp2(d)) / (1.0 - jnp.exp2(d))
    grads_small = {
        "ret_decay_fwd": (dlg_f * chain(dec_f)).reshape(1, 8),
        "ret_decay_bwd": (dlg_b * chain(dec_b)).reshape(1, 8),
        "attn_sink": jnp.sum(spart, axis=0)[:, 0:4, 0].reshape(1, 8),
        "ret_gn_gain": rsum[:, 6, :].reshape(1, RET_WIDTH),
        "ln1_gain": dg1, "ln1_bias": db1, "ln2_gain": dg2, "ln2_bias": db2,
    }
    if not dist:
        grad_x, = _inproj_bwd(dz1, pieces, w_in_t, w_kv)
        grads_rest = [misc_parts[0]] + ffn_parts + misc_parts[1:]
        return sq[0, 0], grad_x.reshape(bsz, s, D_MODEL), d_in, grads_rest, grads_small
    *small_out, th_in = _small_allreduce_adamw(_pack_small(grads_small, sq[0, 0]), *small_state,
                                               rider=_exchange_halves_rider([d_in]))
    s_in = _add_halves([d_in], [th_in], pos)
    grad_x, chips_in = _inproj_bwd(dz1, pieces, w_in_t, w_kv, rider=_exchange_chips_rider([s_in[1]]))
    sums32 = [s_in[0], s_dm[1], s_gu[0], s_gu[1], s_dm[0], s_dm[2], s_dm[3]]
    from_chips = [chips_in, o7[0], o5[0], o6[0], o6[1], o7[1], o7[2]]
    return grad_x.reshape(bsz, s, D_MODEL), sums32, from_chips, small_out


BIG_NAMES = ("w_in", "w_out", "w_ffn_gate", "w_ffn_up", "w_ffn_down", "w_ple_proj", "w_ple_gate")
REST_NAMES = BIG_NAMES[1:]
TRANSPOSED = ("w_in", "w_ffn_gate", "w_ffn_up")
WEIGHT_ORDER = ("w_in", "ret_decay_fwd", "ret_decay_bwd", "ret_gn_gain", "attn_sink", "w_out", "ln1_gain",
                "ln1_bias", "w_ffn_gate", "w_ffn_up", "w_ffn_down", "w_ple_proj", "w_ple_gate", "ln2_gain", "ln2_bias")


def _shard_rows(name, a):
    return jnp.swapaxes(a[0], 0, 1) if name in TRANSPOSED else a[0]


def _unshard_rows(name, a):
    return (jnp.swapaxes(a, 0, 1) if name in TRANSPOSED else a)[None]


def _assemble_weights(gathered):
    rows = lambda a: a.reshape(N_SHARD * a.shape[1], a.shape[2])
    same = lambda a: a
    layout = {"w_out": ("w_out", rows), "w_ffn_gate": ("gate4", same), "w_ffn_up": ("up4", same),
              "w_ffn_down": ("down4", same), "w_ple_proj": ("ple_proj", same), "w_ple_gate": ("ple_gate", rows)}
    return {layout[n][0]: layout[n][1](a) for n, a in gathered.items()}


def kernel(x, p, w_in, ret_decay_fwd, ret_decay_bwd, ret_gn_gain, attn_sink, w_out, ln1_gain, ln1_bias, w_ffn_gate, w_ffn_up, w_ffn_down, w_ple_proj, w_ple_gate, ln2_gain, ln2_bias, loss_target, m_w_in, m_ret_decay_fwd, m_ret_decay_bwd, m_ret_gn_gain, m_attn_sink, m_w_out, m_ln1_gain, m_ln1_bias, m_w_ffn_gate, m_w_ffn_up, m_w_ffn_down, m_w_ple_proj, m_w_ple_gate, m_ln2_gain, m_ln2_bias, v_w_in, v_ret_decay_fwd, v_ret_decay_bwd, v_ret_gn_gain, v_attn_sink, v_w_out, v_ln1_gain, v_ln1_bias, v_w_ffn_gate, v_w_ffn_up, v_w_ffn_down, v_w_ple_proj, v_w_ple_gate, v_ln2_gain, v_ln2_bias):
    w = dict(w_in=w_in, ret_decay_fwd=ret_decay_fwd, ret_decay_bwd=ret_decay_bwd, ret_gn_gain=ret_gn_gain,
             attn_sink=attn_sink, w_out=w_out, ln1_gain=ln1_gain, ln1_bias=ln1_bias, w_ffn_gate=w_ffn_gate,
             w_ffn_up=w_ffn_up, w_ffn_down=w_ffn_down, w_ple_proj=w_ple_proj, w_ple_gate=w_ple_gate,
             ln2_gain=ln2_gain, ln2_bias=ln2_bias)
    m = dict(w_in=m_w_in, ret_decay_fwd=m_ret_decay_fwd, ret_decay_bwd=m_ret_decay_bwd, ret_gn_gain=m_ret_gn_gain,
             attn_sink=m_attn_sink, w_out=m_w_out, ln1_gain=m_ln1_gain, ln1_bias=m_ln1_bias, w_ffn_gate=m_w_ffn_gate,
             w_ffn_up=m_w_ffn_up, w_ffn_down=m_w_ffn_down, w_ple_proj=m_w_ple_proj, w_ple_gate=m_w_ple_gate,
             ln2_gain=m_ln2_gain, ln2_bias=m_ln2_bias)
    v = dict(w_in=v_w_in, ret_decay_fwd=v_ret_decay_fwd, ret_decay_bwd=v_ret_decay_bwd, ret_gn_gain=v_ret_gn_gain,
             attn_sink=v_attn_sink, w_out=v_w_out, ln1_gain=v_ln1_gain, ln1_bias=v_ln1_bias, w_ffn_gate=v_w_ffn_gate,
             w_ffn_up=v_w_ffn_up, w_ffn_down=v_w_ffn_down, w_ple_proj=v_w_ple_proj, w_ple_gate=v_w_ple_gate,
             ln2_gain=v_ln2_gain, ln2_bias=v_ln2_bias)
    big = lambda d: [_shard_rows(n, d[n]) for n in BIG_NAMES]
    small = lambda d: {n: d[n] for n in SMALL_NAMES}

    chip = 2 * lax.axis_index("x") + lax.axis_index("y")
    pos = jnp.stack([chip, lax.axis_index("c")]).astype(jnp.int32)

    shards = [a.astype(BF16) for a in big(w)]
    (w_in4,) = _all_gather_weights(shards[:1])
    w_in_t = w_in4.reshape(IN_WIDTH, D_MODEL)
    grad_x, sums32, from_chips, (g_s, d_s, m_s, v_s) = _local_step(
        x, p[0], loss_target, w_in_t, shards[1:], small(w), pos=pos,
        small_state=(_pack_small(small(w)), _pack_small(small(m)), _pack_small(small(v))))
    g_big, d_big, m_big, v_big = _adamw(big(w), _add_chips(sums32, from_chips, pos), big(m), big(v))
    loss = g_s[4, LOSS_COL] * (0.5 / D_MODEL)

    def tree(bigs, smalls):
        out = {n: _unshard_rows(n, a) for n, a in zip(BIG_NAMES, bigs)}
        out.update(smalls)
        return [out[n] for n in WEIGHT_ORDER]

    smalls = _unpack_small([g_s, d_s, m_s, v_s])
    return (loss, grad_x, *(a for bigs, s in zip((g_big, d_big, m_big, v_big), smalls) for a in tree(bigs, s)))
```

```python
import functools
import math

import jax
import jax.numpy as jnp
from jax import lax
from jax.experimental import pallas as pl
from jax.experimental.pallas import tpu as pltpu

F32 = jnp.float32
BF16 = jnp.bfloat16

D_MODEL = 1024
HEAD_DIM = 64
RET_HEADS = 8
ATTN_HEADS = 8
RET_WIDTH = 512
ATTN_WIDTH = 512
KV_WIDTH = 128
IN_WIDTH = 2816
FFN = 2816
N_SHARD = 4
FFN_SHARD = FFN // N_SHARD
PLE_DIM = 256
CHUNK = 128
LANES = 128
ALPHA = 2.0 ** 0.25
LN_EPS = 1e-5
GN_EPS = 1e-5
NEG_INF = -1e30
ADAM_LR = 0.001
ADAM_B1 = 0.9
ADAM_B2 = 0.999
ADAM_EPS = 1e-08
ADAM_WD = 0.01
ADAM_STEP = 10
VMEM_LIMIT = 56 * 1024 * 1024
MESH = pl.DeviceIdType.MESH

CB_RQ, CB_RK, CB_RV, CB_RG, CB_AQ, CB_AK, CB_AV = 0, 4, 8, 12, 16, 20, 21


def _dot(a, b):
    return jnp.dot(a, b, preferred_element_type=F32)


def _dot_nt(a, b):
    return lax.dot_general(a, b, (((1,), (1,)), ((), ())), preferred_element_type=F32)


def _dot_tn(a, b):
    return lax.dot_general(a, b, (((0,), (0,)), ((), ())), preferred_element_type=F32)


def _sigmoid(x):
    return 1.0 / (1.0 + jnp.exp(-x))


def _params(*sem, vmem=None):
    return pltpu.CompilerParams(dimension_semantics=tuple(sem) if sem else None, vmem_limit_bytes=vmem)


class _Rider:
    def __init__(self, ins, out_shapes, sems, start, finish, aliases=None):
        self.ins, self.out_shapes, self.sems = list(ins), list(out_shapes), list(sems)
        self.start, self.finish, self.aliases = start, finish, dict(aliases or {})


def _merge_riders(riders):
    riders = [r for r in riders if r is not None]
    if len(riders) == 1:
        return riders[0]
    bounds, aliases = [], {}
    i0 = o0 = s0 = 0
    for r in riders:
        bounds.append((i0, o0, s0))
        aliases.update({i0 + i: o0 + o for i, o in r.aliases.items()})
        i0, o0, s0 = i0 + len(r.ins), o0 + len(r.out_shapes), s0 + len(r.sems)

    def each(method):
        def run(ins, outs, sems):
            for r, (i, o, s) in zip(riders, bounds):
                getattr(r, method)(ins[i:i + len(r.ins)], outs[o:o + len(r.out_shapes)], sems[s:s + len(r.sems)])
        return run

    return _Rider([a for r in riders for a in r.ins], [a for r in riders for a in r.out_shapes],
                  [a for r in riders for a in r.sems], each("start"), each("finish"), aliases)


def _hosted_call(body, name, grid, in_specs, out_specs, out_shape, scratch_shapes, operands, rider=None,
                 semantics=None):
    n_in, n_out, n_scr = len(in_specs), len(out_specs), len(scratch_shapes)
    if rider is None:
        return pl.pallas_call(
            body, name=name, grid=grid, in_specs=in_specs, out_specs=out_specs, out_shape=out_shape,
            scratch_shapes=scratch_shapes,
            compiler_params=_params(*(semantics or ["parallel"] * len(grid)), vmem=VMEM_LIMIT))(*operands)
    r_in, r_out = len(rider.ins), len(rider.out_shapes)

    def full_body(*refs):
        main_in, rin = refs[:n_in], refs[n_in:n_in + r_in]
        o0 = n_in + r_in
        main_out, rout = refs[o0:o0 + n_out], refs[o0 + n_out:o0 + n_out + r_out]
        s0 = o0 + n_out + r_out
        main_scr, rsem = refs[s0:s0 + n_scr], refs[s0 + n_scr:]
        first = functools.reduce(jnp.logical_and, [pl.program_id(a) == 0 for a in range(len(grid))])
        last = functools.reduce(jnp.logical_and, [pl.program_id(a) == g - 1 for a, g in enumerate(grid)])

        @pl.when(first)
        def _():
            rider.start(rin, rout, rsem)

        body(*main_in, *main_out, *main_scr)

        @pl.when(last)
        def _():
            rider.finish(rin, rout, rsem)

    hbm = pl.BlockSpec(memory_space=pl.ANY)
    return pl.pallas_call(
        full_body, name=name, grid=grid,
        in_specs=list(in_specs) + [hbm] * r_in, out_specs=list(out_specs) + [hbm] * r_out,
        out_shape=list(out_shape) + rider.out_shapes,
        scratch_shapes=list(scratch_shapes) + rider.sems,
        input_output_aliases={n_in + i: n_out + o for i, o in rider.aliases.items()},
        compiler_params=_params(*(["arbitrary"] * len(grid)), vmem=VMEM_LIMIT),
    )(*operands, *rider.ins)


def _loop_grouped(n, body, init, per_trip=2):
    if n % per_trip:
        return lax.fori_loop(0, n, body, init)

    def trip(i, c):
        for j in range(per_trip):
            c = body(per_trip * i + j, c)
        return c

    return lax.fori_loop(0, n // per_trip, trip, init)


def _head_mean(x, m0):
    s0 = jnp.sum(jnp.where(m0, x, 0.0), axis=1, keepdims=True)
    s1 = jnp.sum(jnp.where(m0, 0.0, x), axis=1, keepdims=True)
    return jnp.where(m0, s0, s1) * (1.0 / HEAD_DIM)


def _inproj(x2d, w_in_t, rider=None):
    t = x2d.shape[0]
    tm = 512
    nb = 256

    def body(x_ref, w_ref, o_ref):
        xb = x_ref[...].astype(BF16)
        for n in range(0, IN_WIDTH, nb):
            o_ref[:, n:n + nb] = _dot_nt(xb, w_ref[n:n + nb, :]).astype(BF16)

    return _hosted_call(
        body, "inproj", (t // tm,),
        in_specs=[pl.BlockSpec((tm, D_MODEL), lambda i: (i, 0)),
                  pl.BlockSpec((IN_WIDTH, D_MODEL), lambda i: (0, 0))],
        out_specs=[pl.BlockSpec((tm, IN_WIDTH), lambda i: (i, 0))],
        out_shape=[jax.ShapeDtypeStruct((t, IN_WIDTH), BF16)],
        scratch_shapes=[], operands=(x2d, w_in_t), rider=rider)


def _outproj_ln1(y_ret, y_att, x2d, w_out, gain, bias, rider=None):
    t = x2d.shape[0]
    tm = 512

    def body(yr_ref, ya_ref, x_ref, w_ref, g_ref, b_ref, zh_ref, r_ref, hb_ref):
        mix = _dot(yr_ref[...], w_ref[0:RET_WIDTH, :]) + _dot(ya_ref[...], w_ref[RET_WIDTH:, :])
        z = ALPHA * x_ref[...] + mix
        mu = jnp.mean(z, axis=1, keepdims=True)
        zc = z - mu
        var = jnp.mean(zc * zc, axis=1, keepdims=True)
        r = lax.rsqrt(var + LN_EPS)
        zh = zc * r
        zh_ref[...] = zh
        r_ref[...] = r
        hb_ref[...] = (zh * g_ref[...] + b_ref[...]).astype(BF16)

    row = lambda w: pl.BlockSpec((tm, w), lambda i: (i, 0))
    const = lambda s: pl.BlockSpec(s, lambda i: (0, 0))
    return _hosted_call(
        body, "outproj_ln1", (t // tm,),
        in_specs=[row(RET_WIDTH), row(ATTN_WIDTH), row(D_MODEL), const((D_MODEL, D_MODEL)),
                  const((1, D_MODEL)), const((1, D_MODEL))],
        out_specs=[row(D_MODEL), row(1), row(D_MODEL)],
        out_shape=[jax.ShapeDtypeStruct((t, D_MODEL), F32), jax.ShapeDtypeStruct((t, 1), F32),
                   jax.ShapeDtypeStruct((t, D_MODEL), BF16)],
        scratch_shapes=[], operands=(y_ret, y_att, x2d, w_out, gain, bias), rider=rider)


def _load_resident(step, pairs, sems):
    copies = [pltpu.make_async_copy(src, dst, sems.at[i]) for i, (src, dst) in enumerate(pairs)]

    @pl.when(step == 0)
    def _():
        for cp in copies:
            cp.start()
        for cp in copies:
            cp.wait()


FFN_CHUNK = 256
N_FFN_CHUNK = FFN // FFN_CHUNK


def _resident_quarters(hbm, vmem):
    q = FFN // N_SHARD
    return [(hbm.at[pl.ds(j * q, q), :], vmem.at[pl.ds(j * q, q), :]) for j in range(N_SHARD)]


def _ln2_loss_tail(zh, mixed, tgt, g1, b1, g2, b2):
    z2 = ALPHA * (zh * g1 + b1) + mixed
    mu = jnp.mean(z2, axis=1, keepdims=True)
    zc = z2 - mu
    var = jnp.mean(zc * zc, axis=1, keepdims=True)
    r = lax.rsqrt(var + LN_EPS)
    zh2 = zc * r
    err = zh2 * g2 + b2 - tgt
    dy = err * (1.0 / D_MODEL)
    dzh = dy * g2
    m1 = jnp.mean(dzh, axis=1, keepdims=True)
    m2 = jnp.mean(dzh * zh2, axis=1, keepdims=True)
    dz2 = r * (dzh - m1 - zh2 * m2)
    return dz2, jnp.sum(err * err), jnp.sum(dy * zh2, axis=0, keepdims=True), jnp.sum(dy, axis=0, keepdims=True)


def _ffn_fwd(zh1, hb, p2d, tgt, g1, b1, g2, b2, wg4, wu4, wd4, wpe, wpg):
    t = zh1.shape[0]
    tm = 256
    wg_t, wu_t, wd_all = (w.reshape(FFN, D_MODEL) for w in (wg4, wu4, wd4))

    def body(zh_ref, hb_ref, p_ref, t_ref, g1_ref, b1_ref, g2_ref, b2_ref,
             wg_hbm, wu_hbm, wd_hbm, wpe_hbm, wpg_hbm,
             dz_ref, dzb_ref, gs_ref, us_ref, act_ref, pg_ref, ple_ref, loss_ref, dg2_ref, db2_ref,
             wg, wu, wd, wpe, wpg, wsem):
        step = pl.program_id(0)
        loads = _resident_quarters(wg_hbm, wg) + _resident_quarters(wu_hbm, wu) + _resident_quarters(wd_hbm, wd)
        pc = D_MODEL // N_SHARD
        loads += [(wpe_hbm.at[j], wpe.at[:, pl.ds(j * pc, pc)]) for j in range(N_SHARD)]
        _load_resident(step, loads + [(wpg_hbm, wpg)], wsem)

        @pl.when(step == 0)
        def _():
            loss_ref[...] = jnp.zeros_like(loss_ref)
            dg2_ref[...] = jnp.zeros_like(dg2_ref)
            db2_ref[...] = jnp.zeros_like(db2_ref)

        hbv = hb_ref[...]
        ffn = jnp.zeros((tm, D_MODEL), F32)
        acts = []
        chunks = [slice(n * FFN_CHUNK, (n + 1) * FFN_CHUNK) for n in range(N_FFN_CHUNK)]
        for n in range(N_FFN_CHUNK + 1):
            if n < N_FFN_CHUNK:
                gj = _dot_nt(hbv, wg[chunks[n], :])
                uj = _dot_nt(hbv, wu[chunks[n], :])
                gs_ref[:, chunks[n]] = gj.astype(BF16)
                us_ref[:, chunks[n]] = uj.astype(BF16)
                acts.append((gj * _sigmoid(gj) * uj).astype(BF16))
                act_ref[:, chunks[n]] = acts[n]
            if n > 0:
                ffn = ffn + _dot(acts[n - 1], wd[chunks[n - 1], :])
        ple = _dot(p_ref[...].astype(BF16), wpe[...])
        pg = _sigmoid(_dot(hbv, wpg[...]))
        pg_ref[...] = pg.astype(BF16)
        ple_ref[...] = ple.astype(BF16)
        dz2, sq, dg2, db2 = _ln2_loss_tail(zh_ref[...], ffn + pg * ple, t_ref[...], g1_ref[...], b1_ref[...],
                                           g2_ref[...], b2_ref[...])
        dz_ref[...] = dz2
        dzb_ref[...] = dz2.astype(BF16)
        loss_ref[...] += sq
        dg2_ref[...] += dg2
        db2_ref[...] += db2

    row = lambda w: pl.BlockSpec((tm, w), lambda i: (i, 0))
    const = lambda s: pl.BlockSpec(s, lambda i: (0, 0))
    hid_shape = jax.ShapeDtypeStruct((t, FFN), BF16)
    hbm = pl.BlockSpec(memory_space=pl.ANY)
    return pl.pallas_call(
        body, name="ffn_fwd", grid=(t // tm,),
        in_specs=[row(D_MODEL), row(D_MODEL), row(PLE_DIM), row(D_MODEL),
                  const((1, D_MODEL)), const((1, D_MODEL)), const((1, D_MODEL)), const((1, D_MODEL)),
                  hbm, hbm, hbm, hbm, hbm],
        out_specs=[row(D_MODEL), row(D_MODEL), row(FFN), row(FFN), row(FFN), row(D_MODEL), row(D_MODEL),
                   const((8, LANES)), const((1, D_MODEL)), const((1, D_MODEL))],
        out_shape=[jax.ShapeDtypeStruct((t, D_MODEL), F32), jax.ShapeDtypeStruct((t, D_MODEL), BF16),
                   hid_shape, hid_shape, hid_shape,
                   jax.ShapeDtypeStruct((t, D_MODEL), BF16), jax.ShapeDtypeStruct((t, D_MODEL), BF16),
                   jax.ShapeDtypeStruct((8, LANES), F32),
                   jax.ShapeDtypeStruct((1, D_MODEL), F32), jax.ShapeDtypeStruct((1, D_MODEL), F32)],
        scratch_shapes=[pltpu.VMEM((FFN, D_MODEL), BF16), pltpu.VMEM((FFN, D_MODEL), BF16),
                        pltpu.VMEM((FFN, D_MODEL), BF16),
                        pltpu.VMEM((PLE_DIM, D_MODEL), BF16), pltpu.VMEM(wpg.shape, BF16),
                        pltpu.SemaphoreType.DMA((4 * N_SHARD + 1,))],
        compiler_params=_params("arbitrary", vmem=VMEM_LIMIT),
    )(zh1, hb, p2d, tgt, g1, b1, g2, b2, wg_t, wu_t, wd_all, wpe, wpg)


def _ret_tables(lgf, lgb):
    c = CHUNK
    row = lax.broadcasted_iota(jnp.int32, (c, LANES), 0).astype(F32)
    ii = lax.broadcasted_iota(jnp.int32, (c, c), 0).astype(F32)
    jj = lax.broadcasted_iota(jnp.int32, (c, c), 1).astype(F32)
    diff = ii - jj
    dmats = []
    for h in range(2):
        lf = lgf[:, h * HEAD_DIM:h * HEAD_DIM + 1]
        lb = lgb[:, h * HEAD_DIM:h * HEAD_DIM + 1]
        dmats.append(jnp.where(diff > 0, jnp.exp(lf * jnp.maximum(diff, 0.0)),
                               jnp.where(diff < 0, jnp.exp(lb * jnp.maximum(-diff, 0.0)), 2.0)))
    tab = dict(
        qdec_f=jnp.exp(lgf * (row + 1.0)), kdec_f=jnp.exp(lgf * (c - 1.0 - row)),
        qdec_b=jnp.exp(lgb * (c - row)), kdec_b=jnp.exp(lgb * row),
        cdec_f=jnp.exp(lgf * c), cdec_b=jnp.exp(lgb * c),
        d0=dmats[0], d1=dmats[1], row=row, diff=diff)
    r = lax.broadcasted_iota(jnp.int32, (LANES, LANES), 0) < HEAD_DIM
    cc = lax.broadcasted_iota(jnp.int32, (LANES, LANES), 1) < HEAD_DIM
    tab["bd"] = r == cc
    tab["m0"] = lax.broadcasted_iota(jnp.int32, (c, LANES), 1) < HEAD_DIM
    return tab


def _ret_specs(bsz, s):
    blk = lambda cb: pl.BlockSpec((bsz, s, LANES), lambda p, cb=cb: (0, 0, cb + p))
    lane = pl.BlockSpec((None, 1, LANES), lambda p: (p, 0, 0))
    gain = pl.BlockSpec((1, LANES), lambda p: (0, p))
    pair = pl.BlockSpec((bsz, s, LANES), lambda p: (0, 0, p))
    return blk, lane, gain, pair


def _ret_state_spec(bsz, n_chunk):
    spec = pl.BlockSpec((None, bsz, n_chunk, LANES, LANES), lambda p: (p, 0, 0, 0, 0))
    return spec, jax.ShapeDtypeStruct((4, bsz, n_chunk, LANES, LANES), F32)


def _ret_kv_states(tb, k_ref, v_ref, rb_ref, kvf_ref, n_chunk):
    c = CHUNK
    bsz = k_ref.shape[0]
    bd = tb["bd"]

    def contributions(n, carry):
        sl = pl.ds(pl.multiple_of(n * c, c), c)
        kfb = []
        for b in range(bsz):
            k32 = k_ref[b, sl, :].astype(F32)
            kfb.append(jnp.concatenate([k32 * tb["kdec_f"], k32 * tb["kdec_b"]], axis=1).astype(BF16))
        kvs = [_dot_tn(kfb[b], v_ref[b, sl, :]) for b in range(bsz)]
        for b in range(bsz):
            kvf_ref[b, n] = jnp.where(bd, kvs[b][0:LANES], 0.0)
            rb_ref[b, n] = jnp.where(bd, kvs[b][LANES:], 0.0)
        return carry

    lax.fori_loop(0, n_chunk, contributions, 0, unroll=2)

    def recur(i, rbs):
        n = n_chunk - 1 - i
        new = []
        for b in range(bsz):
            own = rb_ref[b, n]
            rb_ref[b, n] = rbs[b]
            new.append(rbs[b] * tb["cdec_b"] + own)
        return tuple(new)

    lax.fori_loop(0, n_chunk, recur, tuple(jnp.zeros((LANES, LANES), F32) for _ in range(bsz)))


def _split_rows(x, m0):
    return jnp.concatenate([jnp.where(m0, x, 0.0), jnp.where(m0, 0.0, x)], axis=0).astype(BF16)


def _ret_fwd(u3, lgf_l, lgb_l, gn_gain, rider=None):
    bsz, s, _ = u3.shape
    n_chunk = s // CHUNK
    c = CHUNK

    def body(q_ref, k_ref, v_ref, g_ref, lgf_ref, lgb_ref, gain_ref, yh_ref, rstd_ref, o_ref, rb_ref, kvf_ref):
        tb = _ret_tables(lgf_ref[...], lgb_ref[...])
        m0 = tb["m0"]
        gain = gain_ref[...]
        rows = range(bsz)
        _ret_kv_states(tb, k_ref, v_ref, rb_ref, kvf_ref, n_chunk)

        def chunk(n, rfs):
            sl = pl.ds(pl.multiple_of(n * c, c), c)
            qs = [q_ref[b, sl, :].astype(F32) * 0.125 for b in rows]
            s01 = [_dot_nt(_split_rows(qs[b], m0), k_ref[b, sl, :]) for b in rows]
            ys = []
            for b in rows:
                lhs = jnp.concatenate([s01[b][0:c] * tb["d0"], s01[b][c:] * tb["d1"],
                                       qs[b] * tb["qdec_f"], qs[b] * tb["qdec_b"]], axis=1).astype(BF16)
                rhs = jnp.concatenate([_split_rows(v_ref[b, sl, :].astype(F32), m0),
                                       rfs[b].astype(BF16), rb_ref[b, n].astype(BF16)], axis=0)
                ys.append(_dot(lhs, rhs))
            new = []
            for b in rows:
                y = ys[b]
                mu = _head_mean(y, m0)
                yc = y - mu
                rstd = lax.rsqrt(_head_mean(yc * yc, m0) + GN_EPS)
                yh = yc * rstd
                g = g_ref[b, sl, :].astype(F32)
                yh_ref[b, sl, :] = yh
                rstd_ref[b, sl, :] = rstd
                o_ref[b, sl, :] = (yh * gain * (g * _sigmoid(g))).astype(BF16)
                new.append(rfs[b] * tb["cdec_f"] + kvf_ref[b, n])
            return tuple(new)

        _loop_grouped(n_chunk, chunk, tuple(jnp.zeros((LANES, LANES), F32) for _ in rows))

    blk, lane, gain, pair = _ret_specs(bsz, s)
    state, state_shape = _ret_state_spec(bsz, n_chunk)
    return _hosted_call(
        body, "ret_fwd", (4,),
        in_specs=[blk(CB_RQ), blk(CB_RK), blk(CB_RV), blk(CB_RG), lane, lane, gain],
        out_specs=[pair, pair, pair, state, state],
        out_shape=[jax.ShapeDtypeStruct((bsz, s, RET_WIDTH), F32), jax.ShapeDtypeStruct((bsz, s, RET_WIDTH), F32),
                   jax.ShapeDtypeStruct((bsz, s, RET_WIDTH), BF16), state_shape, state_shape],
        scratch_shapes=[],
        operands=(u3, u3, u3, u3, lgf_l, lgb_l, gn_gain), rider=rider)


def _ret_bwd(u3, y_hat, y_rstd, states, d_o, lgf_l, lgb_l, gn_gain, rider=None):
    bsz, s, _ = u3.shape
    n_chunk = s // CHUNK
    c = CHUNK

    def body(q_ref, k_ref, v_ref, g_ref, yh_ref, rstd_ref, do_ref, lgf_ref, lgb_ref, gain_ref, rb_ref, kvf_ref,
             dq_ref, dk_ref, dv_ref, dg_ref, part_ref,
             rf_ref, dirf_ref, dy_ref, dk_acc, dv_acc, pa0, pa1, vec_ref):
        tb = _ret_tables(lgf_ref[...], lgb_ref[...])
        m0, bd, row = tb["m0"], tb["bd"], tb["row"]
        gain = gain_ref[...]
        wf = jnp.maximum(tb["diff"], 0.0)
        wb = jnp.maximum(-tb["diff"], 0.0)
        rows = range(bsz)
        zero_states = tuple(jnp.zeros((LANES, LANES), F32) for _ in rows)
        for ref in (pa0, pa1):
            ref[...] = jnp.zeros_like(ref)
        vec_ref[...] = jnp.zeros_like(vec_ref)

        def sweep_fwd(n, carry):
            rfs, gbs = carry
            sl = pl.ds(pl.multiple_of(n * c, c), c)
            qs, ks, vs, dys, dybs, q01, k01, dy01 = [], [], [], [], [], [], [], []
            dgain = jnp.zeros((1, LANES), F32)
            for b in rows:
                q = q_ref[b, sl, :].astype(F32) * 0.125
                k = k_ref[b, sl, :]
                yh = yh_ref[b, sl, :]
                rstd = rstd_ref[b, sl, :]
                do = do_ref[b, sl, :].astype(F32)
                g = g_ref[b, sl, :].astype(F32)
                sg = _sigmoid(g)
                sil = g * sg
                dyh = do * gain * sil
                dg_ref[b, sl, :] = (do * yh * gain * sg * (1.0 + g * (1.0 - sg))).astype(BF16)
                dgain = dgain + jnp.sum(do * yh * sil, axis=0, keepdims=True)
                dy = rstd * (dyh - _head_mean(dyh, m0) - yh * _head_mean(dyh * yh, m0))
                dyb = dy.astype(BF16)
                dy_ref[b, sl, :] = dyb
                rf_ref[b, n] = rfs[b]
                qs.append(q)
                ks.append(k)
                vs.append(v_ref[b, sl, :])
                dys.append(dy)
                dybs.append(dyb)
                q01.append(_split_rows(q, m0))
                k01.append(_split_rows(k.astype(F32), m0))
                dy01.append(_split_rows(dy, m0))
            s01 = [_dot_nt(q01[b], ks[b]) for b in rows]
            da01 = [_dot_nt(dy01[b], vs[b]) for b in rows]
            rbn = [rb_ref[b, n] for b in rows]
            states = [jnp.concatenate([rfs[b], rbn[b]], axis=0).astype(BF16) for b in rows]
            dqc = [_dot_nt(dybs[b], states[b]) for b in rows]
            gbb = [gbs[b].astype(BF16) for b in rows]
            dkb = [_dot_nt(vs[b], gbb[b]) for b in rows]
            qfb = [jnp.concatenate([qs[b] * tb["qdec_f"], qs[b] * tb["qdec_b"]], axis=1) for b in rows]
            direct = [_dot_tn(qfb[b].astype(BF16), dybs[b]) for b in rows]
            ds_cat, ds_rows, a_rows = [], [], []
            for b in rows:
                a0 = s01[b][0:c] * tb["d0"]
                a1 = s01[b][c:] * tb["d1"]
                pa0[...] += da01[b][0:c] * a0
                pa1[...] += da01[b][c:] * a1
                ds0 = da01[b][0:c] * tb["d0"]
                ds1 = da01[b][c:] * tb["d1"]
                ds_cat.append(jnp.concatenate([ds0, ds1], axis=1).astype(BF16))
                ds_rows.append(jnp.concatenate([ds0, ds1], axis=0).astype(BF16))
                a_rows.append(jnp.concatenate([a0, a1], axis=0).astype(BF16))
            kbd = [ks[b].astype(F32) * tb["kdec_b"] for b in rows]
            dq_in = [_dot(ds_cat[b], k01[b]) for b in rows]
            dk_in = [_dot_tn(ds_rows[b], q01[b]) for b in rows]
            dv_in = [_dot_tn(a_rows[b], dy01[b]) for b in rows]
            dv_gb = [_dot(kbd[b].astype(BF16), gbb[b]) for b in rows]
            new_rf, new_gb = [], []
            dlf = jnp.zeros((1, LANES), F32)
            dlb = jnp.zeros((1, LANES), F32)
            for b in rows:
                dqf, dqb = dqc[b][:, 0:LANES], dqc[b][:, LANES:]
                qf, qb = qfb[b][:, 0:LANES], qfb[b][:, LANES:]
                dq = dq_in[b] + dqf * tb["qdec_f"] + dqb * tb["qdec_b"]
                dq_ref[b, sl, :] = (dq * 0.125).astype(BF16)
                dk_acc[b, sl, :] = dk_in[b] + dkb[b] * tb["kdec_b"]
                dv_acc[b, sl, :] = dv_in[b] + dv_gb[b]
                dlf = dlf + jnp.sum((row + 1.0) * qf * dqf, axis=0, keepdims=True)
                dlb = dlb + jnp.sum((c - row) * qb * dqb + row * kbd[b] * dkb[b], axis=0, keepdims=True)
                dlb = dlb + c * tb["cdec_b"] * jnp.sum(gbs[b] * rbn[b], axis=0, keepdims=True)
                dirf_ref[b, n] = jnp.where(bd, direct[b][0:LANES], 0.0)
                new_gb.append(jnp.where(bd, direct[b][LANES:], 0.0) + tb["cdec_b"] * gbs[b])
                new_rf.append(rfs[b] * tb["cdec_f"] + kvf_ref[b, n])
            vec_ref[0:1, :] += dlf
            vec_ref[1:2, :] += dlb
            vec_ref[6:7, :] += dgain
            return tuple(new_rf), tuple(new_gb)

        _loop_grouped(n_chunk, sweep_fwd, (zero_states, zero_states), per_trip=4)

        def sweep_bwd(i, gfs):
            n = n_chunk - 1 - i
            sl = pl.ds(pl.multiple_of(n * c, c), c)
            gfb = [gfs[b].astype(BF16) for b in rows]
            kfd = [k_ref[b, sl, :].astype(F32) * tb["kdec_f"] for b in rows]
            dkf = [_dot_nt(v_ref[b, sl, :], gfb[b]) for b in rows]
            dvf = [_dot(kfd[b].astype(BF16), gfb[b]) for b in rows]
            new = []
            dlf = jnp.zeros((1, LANES), F32)
            for b in rows:
                dk_ref[b, sl, :] = (dk_acc[b, sl, :] + dkf[b] * tb["kdec_f"]).astype(BF16)
                dv_ref[b, sl, :] = (dv_acc[b, sl, :] + dvf[b]).astype(BF16)
                dlf = dlf + jnp.sum((c - 1.0 - row) * kfd[b] * dkf[b], axis=0, keepdims=True)
                dlf = dlf + c * tb["cdec_f"] * jnp.sum(gfs[b] * rf_ref[b, n], axis=0, keepdims=True)
                new.append(dirf_ref[b, n] + tb["cdec_f"] * gfs[b])
            vec_ref[0:1, :] += dlf
            return tuple(new)

        _loop_grouped(n_chunk, sweep_bwd, zero_states)
        vec_ref[2:3, :] = jnp.sum(pa0[...] * wf, axis=0, keepdims=True)
        vec_ref[3:4, :] = jnp.sum(pa1[...] * wf, axis=0, keepdims=True)
        vec_ref[4:5, :] = jnp.sum(pa0[...] * wb, axis=0, keepdims=True)
        vec_ref[5:6, :] = jnp.sum(pa1[...] * wb, axis=0, keepdims=True)
        part_ref[...] = vec_ref[...]

    blk, lane, gain, pair = _ret_specs(bsz, s)
    out_bf = jax.ShapeDtypeStruct((bsz, s, RET_WIDTH), BF16)
    state = pltpu.VMEM((bsz, n_chunk, LANES, LANES), F32)
    saved = _ret_state_spec(bsz, n_chunk)[0]
    return _hosted_call(
        body, "ret_bwd", (4,),
        in_specs=[blk(CB_RQ), blk(CB_RK), blk(CB_RV), blk(CB_RG), pair, pair, pair, lane, lane, gain, saved, saved],
        out_specs=[pair, pair, pair, pair, pl.BlockSpec((None, 8, LANES), lambda p: (p, 0, 0))],
        out_shape=[out_bf, out_bf, out_bf, out_bf, jax.ShapeDtypeStruct((4, 8, LANES), F32)],
        scratch_shapes=[state, state,
                        pltpu.VMEM((bsz, s, LANES), BF16), pltpu.VMEM((bsz, s, LANES), F32),
                        pltpu.VMEM((bsz, s, LANES), F32),
                        pltpu.VMEM((c, c), F32), pltpu.VMEM((c, c), F32), pltpu.VMEM((8, LANES), F32)],
        operands=(u3, u3, u3, u3, y_hat, y_rstd, d_o, lgf_l, lgb_l, gn_gain, *states), rider=rider)


def _attn_window_tables(n, s):
    qi = lax.broadcasted_iota(jnp.int32, (CHUNK, 3 * CHUNK), 0)
    kj = lax.broadcasted_iota(jnp.int32, (CHUNK, 3 * CHUNK), 1)
    dist = jnp.abs(kj - CHUNK - qi)
    kpos = n * CHUNK - CHUNK + kj
    valid = (dist <= CHUNK) & (kpos >= 0) & (kpos < s)
    return dist.astype(F32), valid


def _dup_kv_head(x, g):
    lane = lax.broadcasted_iota(jnp.int32, x.shape, 1)
    keep = (lane < HEAD_DIM) == (g == 0)
    xf = x.astype(F32)
    return jnp.where(keep, xf, pltpu.roll(xf, HEAD_DIM, 1))


def _attn_specs(s):
    q = pl.BlockSpec((None, s, 2 * LANES), lambda b, g: (b, 0, CB_AQ // 2 + g))
    k = pl.BlockSpec((None, s, LANES), lambda b, g: (b, 0, CB_AK))
    v = pl.BlockSpec((None, s, LANES), lambda b, g: (b, 0, CB_AV))
    grp = pl.BlockSpec((None, s, 2 * LANES), lambda b, g: (b, 0, g))
    smem = pl.BlockSpec(memory_space=pltpu.SMEM)
    return q, k, v, grp, smem


def _fill_padded(dst_ref, val, s):
    dst_ref[0:CHUNK, :] = jnp.zeros((CHUNK, LANES), dst_ref.dtype)
    dst_ref[CHUNK:CHUNK + s, :] = val.astype(dst_ref.dtype)
    dst_ref[CHUNK + s:2 * CHUNK + s, :] = jnp.zeros((CHUNK, LANES), dst_ref.dtype)


def _attn_probs(sc, slope, snk, dist, valid):
    sc = jnp.where(valid, sc - slope * dist, NEG_INF)
    m = jnp.maximum(jnp.max(sc, axis=1, keepdims=True), snk)
    e = jnp.exp(sc - m)
    es = jnp.exp(snk - m)
    inv = 1.0 / (jnp.sum(e, axis=1, keepdims=True) + es)
    return e * inv, es * inv


def _stack_heads(x2, m0):
    parts = []
    for pr in range(2):
        xp = x2[:, pr * LANES:(pr + 1) * LANES]
        parts += [jnp.where(m0, xp, 0.0), jnp.where(m0, 0.0, xp)]
    return jnp.concatenate(parts, axis=0).astype(BF16)


def _unstack_pair(x_all, pr, m0):
    return jnp.where(m0, x_all[(2 * pr) * CHUNK:(2 * pr + 1) * CHUNK], x_all[(2 * pr + 1) * CHUNK:(2 * pr + 2) * CHUNK])


def _attn_saved_specs(bsz, n_blk):
    specs = [pl.BlockSpec((None, None, n_blk, 4 * CHUNK, w), lambda b, g: (b, g, 0, 0, 0)) for w in (3 * CHUNK, 1)]
    shapes = [jax.ShapeDtypeStruct((bsz, 2, n_blk, 4 * CHUNK, 3 * CHUNK), BF16),
              jax.ShapeDtypeStruct((bsz, 2, n_blk, 4 * CHUNK, 1), F32)]
    return specs, shapes


def _attn_fwd(u3, slopes, sink, rider=None):
    bsz, s, _ = u3.shape
    n_blk = s // CHUNK

    def body(slope_ref, sink_ref, q_ref, k_ref, v_ref, o_ref, p_ref, ps_ref, kp_ref, vp_ref):
        g = pl.program_id(1)
        _fill_padded(kp_ref, _dup_kv_head(k_ref[...], g), s)
        _fill_padded(vp_ref, _dup_kv_head(v_ref[...], g), s)
        m0 = lax.broadcasted_iota(jnp.int32, (CHUNK, LANES), 1) < HEAD_DIM

        def blk(n, carry):
            r0 = pl.multiple_of(n * CHUNK, CHUNK)
            kw = kp_ref[pl.ds(r0, 3 * CHUNK), :]
            vw = vp_ref[pl.ds(r0, 3 * CHUNK), :]
            dist, valid = _attn_window_tables(n, s)
            q_all = _stack_heads(q_ref[pl.ds(r0, CHUNK), :].astype(F32) * 0.125, m0)
            sc_all = _dot_nt(q_all, kw)
            probs, sinks = [], []
            for i in range(4):
                p, ps = _attn_probs(sc_all[i * CHUNK:(i + 1) * CHUNK], slope_ref[g * 4 + i], sink_ref[g * 4 + i],
                                    dist, valid)
                probs.append(p.astype(BF16))
                sinks.append(ps)
            p_all = jnp.concatenate(probs, axis=0)
            p_ref[n] = p_all
            ps_ref[n] = jnp.concatenate(sinks, axis=0)
            out_all = _dot(p_all, vw)
            for pr in range(2):
                o_ref[pl.ds(r0, CHUNK), pr * LANES:(pr + 1) * LANES] = _unstack_pair(out_all, pr, m0).astype(BF16)
            return carry

        lax.fori_loop(0, n_blk, blk, 0, unroll=4)

    q, k, v, grp, smem = _attn_specs(s)
    saved_specs, saved_shapes = _attn_saved_specs(bsz, n_blk)
    return _hosted_call(
        body, "attn_fwd", (bsz, 2),
        in_specs=[smem, smem, q, k, v],
        out_specs=[grp] + saved_specs,
        out_shape=[jax.ShapeDtypeStruct((bsz, s, ATTN_WIDTH), BF16)] + saved_shapes,
        scratch_shapes=[pltpu.VMEM((s + 2 * CHUNK, LANES), BF16), pltpu.VMEM((s + 2 * CHUNK, LANES), BF16)],
        operands=(slopes, sink, u3, u3, u3), rider=rider)


def _attn_bwd(u3, d_o, probs, sink_probs, rider=None):
    bsz, s, _ = u3.shape
    n_blk = s // CHUNK

    def body(q_ref, k_ref, v_ref, do_ref, p_ref, ps_ref, dq_ref, dkv_ref, ds_ref,
             kp_ref, vp_ref, dk_acc, dv_acc):
        g = pl.program_id(1)
        _fill_padded(kp_ref, _dup_kv_head(k_ref[...], g), s)
        _fill_padded(vp_ref, _dup_kv_head(v_ref[...], g), s)
        dk_acc[...] = jnp.zeros_like(dk_acc)
        dv_acc[...] = jnp.zeros_like(dv_acc)
        m0 = lax.broadcasted_iota(jnp.int32, (CHUNK, LANES), 1) < HEAD_DIM

        def blk(n, dsink):
            r0 = pl.multiple_of(n * CHUNK, CHUNK)
            win = pl.ds(r0, 3 * CHUNK)
            kw = kp_ref[win, :]
            vw = vp_ref[win, :]
            q_all = _stack_heads(q_ref[pl.ds(r0, CHUNK), :].astype(F32) * 0.125, m0)
            do_all = _stack_heads(do_ref[pl.ds(r0, CHUNK), :].astype(F32), m0)
            p_all = p_ref[n]
            ps_all = ps_ref[n]
            dp_all = _dot_nt(do_all, vw)
            new_dsink, dscs = [], []
            for i in range(4):
                rows = slice(i * CHUNK, (i + 1) * CHUNK)
                p = p_all[rows].astype(F32)
                dp = dp_all[rows]
                delta = jnp.sum(p * dp, axis=1, keepdims=True)
                dscs.append((p * (dp - delta)).astype(BF16))
                dsh = jnp.sum(ps_all[rows] * delta, axis=0, keepdims=True)
                new_dsink.append(dsink[i] - jnp.broadcast_to(dsh, (1, LANES)))
            dsc_all = jnp.concatenate(dscs, axis=0)
            dq_all = _dot(dsc_all, kw)
            dk_acc[win, :] += _dot_tn(dsc_all, q_all)
            dv_acc[win, :] += _dot_tn(p_all, do_all)
            for pr in range(2):
                dq_ref[pl.ds(r0, CHUNK), pr * LANES:(pr + 1) * LANES] = (
                    _unstack_pair(dq_all, pr, m0) * 0.125).astype(BF16)
            return tuple(new_dsink)

        dsink = _loop_grouped(n_blk, blk, tuple(jnp.zeros((1, LANES), F32) for _ in range(4)), per_trip=4)
        dk = dk_acc[CHUNK:CHUNK + s, :]
        dv = dv_acc[CHUNK:CHUNK + s, :]
        lane = lax.broadcasted_iota(jnp.int32, (s, LANES), 1)
        fold = lambda a: a + pltpu.roll(a, HEAD_DIM, 1)
        dkv_ref[...] = jnp.where(lane < HEAD_DIM, fold(dk), fold(dv)).astype(BF16)
        ds_ref[...] = jnp.zeros_like(ds_ref)
        for i in range(4):
            ds_ref[i:i + 1, :] = dsink[i]

    q, k, v, grp, _ = _attn_specs(s)
    return _hosted_call(
        body, "attn_bwd", (bsz, 2),
        in_specs=[q, k, v, grp] + _attn_saved_specs(bsz, n_blk)[0],
        out_specs=[grp, pl.BlockSpec((None, s, LANES), lambda b, g: (b, 0, g)),
                   pl.BlockSpec((None, None, 8, LANES), lambda b, g: (b, g, 0, 0))],
        out_shape=[jax.ShapeDtypeStruct((bsz, s, ATTN_WIDTH), BF16), jax.ShapeDtypeStruct((bsz, s, 2 * LANES), BF16),
                   jax.ShapeDtypeStruct((bsz, 2, 8, LANES), F32)],
        scratch_shapes=[pltpu.VMEM((s + 2 * CHUNK, LANES), BF16), pltpu.VMEM((s + 2 * CHUNK, LANES), BF16),
                        pltpu.VMEM((s + 2 * CHUNK, LANES), F32), pltpu.VMEM((s + 2 * CHUNK, LANES), F32)],
        operands=(u3, u3, u3, d_o, probs, sink_probs), rider=rider)


def _ffn_bwd(dz2, gs, us, pg, ple, zh1, r1, g1, wg4, wu4, wd4, wpg, w_out):
    t = dz2.shape[0]
    tm = 256
    wg_t, wu_t, wd_all = (w.reshape(FFN, D_MODEL) for w in (wg4, wu4, wd4))

    def body(dz_ref, gs_ref, us_ref, pg_ref, ple_ref, zh_ref, r_ref, g1_ref,
             wg_hbm, wu_hbm, wd_hbm, wpg_hbm, wo_hbm,
             dgs_ref, dus_ref, dsp_ref, dple_ref, dz1_ref, dyr_ref, dya_ref, dg1_ref, db1_ref,
             wg, wu, wd, wpg, wo, wsem):
        step = pl.program_id(0)
        loads = _resident_quarters(wd_hbm, wd) + _resident_quarters(wg_hbm, wg) + _resident_quarters(wu_hbm, wu)
        _load_resident(step, loads + [(wpg_hbm, wpg), (wo_hbm, wo)], wsem)

        @pl.when(step == 0)
        def _():
            dg1_ref[...] = jnp.zeros_like(dg1_ref)
            db1_ref[...] = jnp.zeros_like(db1_ref)

        dz = dz_ref[...]
        dzb = dz.astype(BF16)
        dh = ALPHA * dz
        pending = []
        chunks = [slice(n * FFN_CHUNK, (n + 1) * FFN_CHUNK) for n in range(N_FFN_CHUNK)]
        for n in range(N_FFN_CHUNK + 1):
            if n < N_FFN_CHUNK:
                da = _dot_nt(dzb, wd[chunks[n], :])
                gj = gs_ref[:, chunks[n]].astype(F32)
                uj = us_ref[:, chunks[n]].astype(F32)
                sg = _sigmoid(gj)
                dgj = (da * uj * sg * (1.0 + gj * (1.0 - sg))).astype(BF16)
                duj = (da * gj * sg).astype(BF16)
                dgs_ref[:, chunks[n]] = dgj
                dus_ref[:, chunks[n]] = duj
                pending.append((dgj, duj))
            if n > 0:
                dgp, dup = pending[n - 1]
                dh = dh + _dot(dgp, wg[chunks[n - 1], :]) + _dot(dup, wu[chunks[n - 1], :])
        pgv = pg_ref[...].astype(F32)
        plev = ple_ref[...].astype(F32)
        dple_ref[...] = (dz * pgv).astype(BF16)
        dsp = (dz * plev * pgv * (1.0 - pgv)).astype(BF16)
        dsp_ref[...] = dsp
        dh = dh + _dot_nt(dsp, wpg[...])
        zh = zh_ref[...]
        dg1_ref[...] += jnp.sum(dh * zh, axis=0, keepdims=True)
        db1_ref[...] += jnp.sum(dh, axis=0, keepdims=True)
        dzh = dh * g1_ref[...]
        m1 = jnp.mean(dzh, axis=1, keepdims=True)
        m2 = jnp.mean(dzh * zh, axis=1, keepdims=True)
        dz1 = r_ref[...] * (dzh - m1 - zh * m2)
        dz1_ref[...] = dz1
        dyc = _dot_nt(dz1.astype(BF16), wo[...])
        dyr_ref[...] = dyc[:, 0:RET_WIDTH].astype(BF16)
        dya_ref[...] = dyc[:, RET_WIDTH:].astype(BF16)

    row = lambda w: pl.BlockSpec((tm, w), lambda i: (i, 0))
    const = lambda s: pl.BlockSpec(s, lambda i: (0, 0))
    hbm = pl.BlockSpec(memory_space=pl.ANY)
    hid_shape = jax.ShapeDtypeStruct((t, FFN), BF16)
    return pl.pallas_call(
        body, name="ffn_bwd", grid=(t // tm,),
        in_specs=[row(D_MODEL), row(FFN), row(FFN), row(D_MODEL), row(D_MODEL), row(D_MODEL), row(1),
                  const((1, D_MODEL)), hbm, hbm, hbm, hbm, hbm],
        out_specs=[row(FFN), row(FFN), row(D_MODEL), row(D_MODEL), row(D_MODEL), row(RET_WIDTH), row(ATTN_WIDTH),
                   const((1, D_MODEL)), const((1, D_MODEL))],
        out_shape=[hid_shape, hid_shape, jax.ShapeDtypeStruct((t, D_MODEL), BF16),
                   jax.ShapeDtypeStruct((t, D_MODEL), BF16), jax.ShapeDtypeStruct((t, D_MODEL), F32),
                   jax.ShapeDtypeStruct((t, RET_WIDTH), BF16), jax.ShapeDtypeStruct((t, ATTN_WIDTH), BF16),
                   jax.ShapeDtypeStruct((1, D_MODEL), F32), jax.ShapeDtypeStruct((1, D_MODEL), F32)],
        scratch_shapes=[pltpu.VMEM((FFN, D_MODEL), BF16), pltpu.VMEM((FFN, D_MODEL), BF16),
                        pltpu.VMEM((FFN, D_MODEL), BF16),
                        pltpu.VMEM(wpg.shape, BF16), pltpu.VMEM(w_out.shape, BF16),
                        pltpu.SemaphoreType.DMA((3 * N_SHARD + 2,))],
        compiler_params=_params("arbitrary", vmem=VMEM_LIMIT),
    )(dz2, gs, us, pg, ple, zh1, r1, g1, wg_t, wu_t, wd_all, wpg, w_out)


def _wgrad_misc(y_ret, y_att, dz1, hb, dsp, p2d, dple, rider=None):
    t = dz1.shape[0]
    tk = min(t, 512)
    pc = D_MODEL // N_SHARD

    def body(yr_ref, ya_ref, dz_ref, hb_ref, dsp_ref, p_ref, dple_ref, wo_ref, wpg_ref, wpe_ref):
        @pl.when(pl.program_id(0) == 0)
        def _():
            wo_ref[...] = jnp.zeros_like(wo_ref)
            wpg_ref[...] = jnp.zeros_like(wpg_ref)
            wpe_ref[...] = jnp.zeros_like(wpe_ref)

        dzb = dz_ref[...].astype(BF16)
        wo_ref[0:RET_WIDTH, :] += _dot_tn(yr_ref[...], dzb)
        wo_ref[RET_WIDTH:, :] += _dot_tn(ya_ref[...], dzb)
        wpg_ref[...] += _dot_tn(hb_ref[...], dsp_ref[...])
        dpe = _dot_tn(p_ref[...].astype(BF16), dple_ref[...])
        for j in range(N_SHARD):
            wpe_ref[j] += dpe[:, j * pc:(j + 1) * pc]

    row = lambda w: pl.BlockSpec((tk, w), lambda k: (k, 0))
    const = lambda s: pl.BlockSpec(s, lambda k: (0,) * len(s))
    return _hosted_call(
        body, "wgrad_misc", (t // tk,),
        in_specs=[row(RET_WIDTH), row(ATTN_WIDTH), row(D_MODEL), row(D_MODEL), row(D_MODEL), row(PLE_DIM),
                  row(D_MODEL)],
        out_specs=[const((D_MODEL, D_MODEL)), const((D_MODEL, D_MODEL)), const((N_SHARD, PLE_DIM, pc))],
        out_shape=[jax.ShapeDtypeStruct((D_MODEL, D_MODEL), F32), jax.ShapeDtypeStruct((D_MODEL, D_MODEL), F32),
                   jax.ShapeDtypeStruct((N_SHARD, PLE_DIM, pc), F32)],
        scratch_shapes=[], operands=(y_ret, y_att, dz1, hb, dsp, p2d, dple), rider=rider, semantics=["arbitrary"])


def _wgrad_ffn(acts, dgs, dus, hb, dz2b):
    t = dz2b.shape[0]
    tk = min(t, 512)
    nk = t // tk

    def body(act_ref, dg_ref, du_ref, hb_ref, dz_ref, og_ref, ou_ref, od_ref):
        @pl.when(pl.program_id(1) == 0)
        def _():
            og_ref[...] = jnp.zeros_like(og_ref)
            ou_ref[...] = jnp.zeros_like(ou_ref)
            od_ref[...] = jnp.zeros_like(od_ref)

        hbv = hb_ref[...]
        og_ref[...] += _dot_tn(dg_ref[...], hbv)
        ou_ref[...] += _dot_tn(du_ref[...], hbv)
        od_ref[...] += _dot_tn(act_ref[...], dz_ref[...])

    half = FFN // 2
    a_spec = pl.BlockSpec((tk, half), lambda j, k: (k, j))
    b_spec = pl.BlockSpec((tk, D_MODEL), lambda j, k: (k, 0))
    o_spec = pl.BlockSpec((half, D_MODEL), lambda j, k: (j, 0))
    o_shape = jax.ShapeDtypeStruct((FFN, D_MODEL), F32)
    outs = pl.pallas_call(
        body, name="wgrad_ffn", grid=(2, nk),
        in_specs=[a_spec, a_spec, a_spec, b_spec, b_spec],
        out_specs=[o_spec] * 3, out_shape=[o_shape] * 3,
        compiler_params=_params("parallel", "arbitrary", vmem=VMEM_LIMIT),
    )(acts, dgs, dus, hb, dz2b)
    return [o.reshape(N_SHARD, FFN_SHARD, D_MODEL) for o in outs]


KV_ORDER = (0, 128, 64, 192)


def _wgrad_in(pieces, x2d, rider=None):
    t = x2d.shape[0]
    tk = min(t, 512)
    nk = t // tk
    kv0 = CB_AK * LANES

    def body(p0, p1, p2, p3, p4, pkv, x_ref, o_ref):
        @pl.when(pl.program_id(0) == 0)
        def _():
            o_ref[...] = jnp.zeros_like(o_ref)

        xb = x_ref[...].astype(BF16)
        for i, ref in enumerate((p0, p1, p2, p3, p4)):
            o_ref[i * 512:(i + 1) * 512, :] += _dot_tn(ref[...], xb)
        dkv = _dot_tn(pkv[...], xb)
        for i, o in enumerate(KV_ORDER):
            o_ref[kv0 + o:kv0 + o + HEAD_DIM, :] += dkv[i * HEAD_DIM:(i + 1) * HEAD_DIM]

    row = lambda w: pl.BlockSpec((tk, w), lambda k: (k, 0))
    return _hosted_call(
        body, "wgrad_in", (nk,),
        in_specs=[row(512)] * 5 + [row(256), row(D_MODEL)],
        out_specs=[pl.BlockSpec((IN_WIDTH, D_MODEL), lambda k: (0, 0))],
        out_shape=[jax.ShapeDtypeStruct((IN_WIDTH, D_MODEL), F32)],
        scratch_shapes=[], operands=(*pieces, x2d), rider=rider, semantics=["arbitrary"])


def _inproj_bwd(dz1, pieces, w_in_t, w_kv, rider=None):
    t = dz1.shape[0]
    tm = 512
    n_main = 5 * 512

    def body(dz_ref, p0, p1, p2, p3, p4, pkv, wm_ref, wkv_ref, o_ref):
        acc = ALPHA * dz_ref[...]
        for i, ref in enumerate((p0, p1, p2, p3, p4)):
            acc = acc + _dot(ref[...], wm_ref[i * 512:(i + 1) * 512, :])
        o_ref[...] = acc + _dot(pkv[...], wkv_ref[...])

    row = lambda w: pl.BlockSpec((tm, w), lambda i: (i, 0))
    const = lambda s: pl.BlockSpec(s, lambda i: (0, 0))
    return _hosted_call(
        body, "inproj_bwd", (t // tm,),
        in_specs=[row(D_MODEL)] + [row(512)] * 5 + [row(256), const((n_main, D_MODEL)), const(w_kv.shape)],
        out_specs=[row(D_MODEL)],
        out_shape=[jax.ShapeDtypeStruct((t, D_MODEL), F32)],
        scratch_shapes=[], operands=(dz1, *pieces, w_in_t, w_kv), rider=rider)


def _coords():
    return lax.axis_index("x"), lax.axis_index("y"), lax.axis_index("c")


def _chip_of(x, y, rel):
    return (1 - x if rel & 2 else x), (1 - y if rel & 1 else y)


def _gather_and_cast(shard, others):
    near = _gather_near_rider([shard])
    relay = _gather_relay_rider(near.out_shapes, chained=True)
    pass_near = _gather_pass_rider(near.out_shapes, chained=True, rels=NEAR)
    pass_far = _gather_pass_rider(near.out_shapes, chained=True, rels=(3,))
    riders = [near, relay, pass_near, pass_far]
    no = len(others)

    def body(*refs):
        shard_ref, wide = refs[0], refs[1:1 + no]
        out_ref, narrow = refs[1 + no], refs[2 + no:2 + 2 * no]
        k = 2 + 2 * no
        vin, vout, (lsem, ssem) = refs[k:k + no], refs[k + no:k + 2 * no], refs[k + 2 * no:k + 2 * no + 2]
        k += 2 * no + 2
        sems = {}
        for r in riders:
            sems[id(r)] = refs[k:k + len(r.sems)]
            k += len(r.sems)
        run = lambda r, method: getattr(r, method)([shard_ref], [out_ref], sems[id(r)])
        loads = [pltpu.make_async_copy(wide[w], vin[w], lsem.at[w]) for w in range(no)]
        stores = [pltpu.make_async_copy(vout[w], narrow[w], ssem.at[w]) for w in range(no)]

        run(near, "start")
        for cp in loads:
            cp.start()
        for w in range(no):
            loads[w].wait()
            vout[w][...] = vin[w][...].astype(BF16)
            stores[w].start()
        run(near, "finish")
        run(relay, "start")
        run(pass_near, "start")
        run(relay, "finish")
        run(pass_far, "start")
        run(pass_near, "finish")
        run(pass_far, "finish")
        for cp in stores:
            cp.wait()

    hbm = pl.BlockSpec(memory_space=pl.ANY)
    dma = pltpu.SemaphoreType.DMA
    gathered, *cast = pl.pallas_call(
        body, name="gather_weights", in_specs=[hbm] * (1 + no), out_specs=[hbm] * (1 + no),
        out_shape=near.out_shapes + [jax.ShapeDtypeStruct(a.shape, BF16) for a in others],
        scratch_shapes=[pltpu.VMEM(a.shape, F32) for a in others] + [pltpu.VMEM(a.shape, BF16) for a in others]
        + [dma((no,)), dma((no,))] + [s for r in riders for s in r.sems],
        compiler_params=_params(vmem=VMEM_LIMIT),
    )(shard, *others)
    return gathered, cast


def _gather_half(outs, w, chip, cc):
    h = outs[w].shape[1] // 2
    return outs[w].at[chip, pl.ds(cc * h, h), :]


NEAR = (1, 2)


def _gather_near_rider(shards, rels=NEAR):
    nw, nr = len(shards), len(rels)

    def copies(ins, outs, sems, arrivals):
        send, recv, lsend, lrecv = sems
        x, y, c = _coords()
        me = 2 * x + y
        own = [pltpu.make_async_remote_copy(
            src_ref=ins[w], dst_ref=outs[w].at[me], send_sem=lsend.at[w], recv_sem=lrecv.at[w],
            device_id=(x, y, 1 - c), device_id_type=MESH) for w in range(nw)]
        out, arrive = [], []
        for i, rel in enumerate(rels):
            kx, ky = _chip_of(x, y, rel)
            for w in range(nw):
                h = shards[w].shape[0] // 2
                sem = dict(send_sem=send.at[w * nr + i], recv_sem=recv.at[w * nr + i],
                           device_id=(kx, ky, c), device_id_type=MESH)
                out.append(pltpu.make_async_remote_copy(
                    src_ref=ins[w].at[pl.ds(c * h, h), :], dst_ref=_gather_half(outs, w, me, c), **sem))
                if arrivals:
                    theirs = _gather_half(outs, w, 2 * kx + ky, c)
                    arrive.append(pltpu.make_async_remote_copy(src_ref=theirs, dst_ref=theirs, **sem))
        return own, out, arrive

    def start(ins, outs, sems):
        own, out, _ = copies(ins, outs, sems, arrivals=False)
        for cp in own + out:
            cp.start()

    def finish(ins, outs, sems):
        own, out, arrive = copies(ins, outs, sems, arrivals=True)
        for cp in arrive:
            cp.wait_recv()
        for cp in out:
            cp.wait_send()
        for cp in own:
            cp.wait()

    dma = pltpu.SemaphoreType.DMA
    return _Rider(shards, [jax.ShapeDtypeStruct((N_SHARD,) + s.shape, s.dtype) for s in shards],
                  [dma((nr * nw,)), dma((nr * nw,)), dma((nw,)), dma((nw,))], start, finish)


def _gather_relay_rider(gathered, chained=False):
    nw = len(gathered)

    def quarter(outs, w, chip, c, p):
        q = outs[w].shape[1] // 4
        return outs[w].at[chip, pl.ds(c * 2 * q + p * q, q), :]

    def copies(outs, sems):
        send, recv = sems
        x, y, c = _coords()
        (yx, yy), (xx, xy), (dx, dy) = (_chip_of(x, y, rel) for rel in (1, 2, 3))
        out, arrive = [], []
        for w in range(nw):
            for p, (src_chip, dst) in enumerate(((2 * xx + xy, (yx, yy)), (2 * yx + yy, (xx, xy)))):
                rows = quarter(outs, w, src_chip, c, p)
                sem = dict(send_sem=send.at[w * 2 + p], recv_sem=recv.at[w * 2 + p], device_id_type=MESH)
                out.append(pltpu.make_async_remote_copy(src_ref=rows, dst_ref=rows, device_id=(*dst, c), **sem))
                mine = quarter(outs, w, 2 * dx + dy, c, p)
                arrive.append(pltpu.make_async_remote_copy(src_ref=mine, dst_ref=mine, device_id=(*dst, c), **sem))
        return out, arrive

    def start(ins, outs, sems):
        for cp in copies(outs, sems)[0]:
            cp.start()

    def finish(ins, outs, sems):
        out, arrive = copies(outs, sems)
        for cp in arrive:
            cp.wait_recv()
        for cp in out:
            cp.wait_send()

    dma = pltpu.SemaphoreType.DMA
    shapes = [jax.ShapeDtypeStruct(g.shape, g.dtype) for g in gathered]
    if chained:
        return _Rider([], [], [dma((2 * nw,)), dma((2 * nw,))], start, finish)
    return _Rider(gathered, shapes, [dma((2 * nw,)), dma((2 * nw,))], start, finish,
                  aliases={w: w for w in range(nw)})


def _gather_pass_rider(gathered, chained=False, rels=(1, 2, 3)):
    nw, nr = len(gathered), len(rels)

    def copies(outs, sems, cc):
        send, recv = sems
        x, y, c = _coords()
        res = []
        for i, rel in enumerate(rels):
            kx, ky = _chip_of(x, y, rel)
            for w in range(nw):
                rows = _gather_half(outs, w, 2 * kx + ky, cc)
                res.append(pltpu.make_async_remote_copy(
                    src_ref=rows, dst_ref=rows, send_sem=send.at[w * nr + i], recv_sem=recv.at[w * nr + i],
                    device_id=(x, y, 1 - c), device_id_type=MESH))
        return res

    def start(ins, outs, sems):
        for cp in copies(outs, sems, lax.axis_index("c")):
            cp.start()

    def finish(ins, outs, sems):
        c = lax.axis_index("c")
        for cp in copies(outs, sems, 1 - c):
            cp.wait_recv()
        for cp in copies(outs, sems, c):
            cp.wait_send()

    dma = pltpu.SemaphoreType.DMA
    shapes = [jax.ShapeDtypeStruct(g.shape, g.dtype) for g in gathered]
    if chained:
        return _Rider([], [], [dma((nr * nw,)), dma((nr * nw,))], start, finish)
    return _Rider(gathered, shapes, [dma((nr * nw,)), dma((nr * nw,))], start, finish,
                  aliases={w: w for w in range(nw)})


def _exchange_halves_rider(parts):
    nw = len(parts)

    def copies(ins, outs, sems):
        send, recv = sems
        x, y, c = _coords()
        res = []
        for w in range(nw):
            h = parts[w].shape[1] // 2
            res.append(pltpu.make_async_remote_copy(
                src_ref=ins[w].at[:, pl.ds((1 - c) * h, h), :], dst_ref=outs[w],
                send_sem=send.at[w], recv_sem=recv.at[w], device_id=(x, y, 1 - c), device_id_type=MESH))
        return res

    def start(ins, outs, sems):
        for cp in copies(ins, outs, sems):
            cp.start()

    def finish(ins, outs, sems):
        for cp in copies(ins, outs, sems):
            cp.wait()

    dma = pltpu.SemaphoreType.DMA
    return _Rider(parts, [jax.ShapeDtypeStruct((N_SHARD, p.shape[1] // 2, p.shape[2]), p.dtype) for p in parts],
                  [dma((nw,)), dma((nw,))], start, finish)


def _add_halves(parts, theirs, pos):
    nw = len(parts)
    split = 2

    def body(pos_ref, *refs):
        ins, oth = refs[:nw], refs[nw:2 * nw]
        o32, o16 = refs[2 * nw:3 * nw], refs[3 * nw:]
        sums = [ins[w][...] + oth[w][...] for w in range(nw)]
        for w in range(nw):
            o16[w][...] = sums[w].astype(BF16)

        @pl.when(pl.program_id(1) == pos_ref[0])
        def _():
            for w in range(nw):
                o32[w][...] = sums[w]

    in_specs, oth_specs, o32_specs, shapes32, shapes16 = [], [], [], [], []
    for p in parts:
        hb = p.shape[1] // 2 // split
        blk = (None, hb, p.shape[2])
        in_specs.append(pl.BlockSpec(blk, lambda i, j, pos_ref: (j, pos_ref[1] * split + i, 0)))
        oth_specs.append(pl.BlockSpec(blk, lambda i, j, pos_ref: (j, i, 0)))
        o32_specs.append(pl.BlockSpec((hb, p.shape[2]), lambda i, j, pos_ref: (i, 0)))
        shapes32.append(jax.ShapeDtypeStruct((p.shape[1] // 2, p.shape[2]), F32))
        shapes16.append(jax.ShapeDtypeStruct((N_SHARD, p.shape[1] // 2, p.shape[2]), BF16))
    return pl.pallas_call(
        body, name="add_halves",
        grid_spec=pltpu.PrefetchScalarGridSpec(
            num_scalar_prefetch=1, grid=(split, N_SHARD),
            in_specs=in_specs + oth_specs, out_specs=o32_specs + oth_specs),
        out_shape=shapes32 + shapes16,
        compiler_params=_params("parallel", "arbitrary", vmem=VMEM_LIMIT),
    )(pos, *parts, *theirs)


def _exchange_chips_rider(sums16, rows=None, into=None):
    nw = len(sums16)
    rows = rows or [(0, s.shape[1]) for s in sums16]
    held = [w for w in range(nw) if into is not None and into[w] is not None]

    def copies(ins, outs, sems):
        send, recv = sems
        x, y, c = _coords()
        res = []
        for rel in (1, 2, 3):
            kx, ky = _chip_of(x, y, rel)
            for w in range(nw):
                r0, n = rows[w]
                res.append(pltpu.make_async_remote_copy(
                    src_ref=ins[w].at[2 * kx + ky, pl.ds(r0, n), :], dst_ref=outs[w].at[rel - 1, pl.ds(r0, n), :],
                    send_sem=send.at[w * 3 + rel - 1], recv_sem=recv.at[w * 3 + rel - 1],
                    device_id=(kx, ky, c), device_id_type=MESH))
        return res

    def start(ins, outs, sems):
        for cp in copies(ins, outs, sems):
            cp.start()

    def finish(ins, outs, sems):
        for cp in copies(ins, outs, sems):
            cp.wait()

    dma = pltpu.SemaphoreType.DMA
    return _Rider(list(sums16) + [into[w] for w in held],
                  [jax.ShapeDtypeStruct((3,) + s.shape[1:], BF16) for s in sums16],
                  [dma((3 * nw,)), dma((3 * nw,))], start, finish, aliases={nw + i: w for i, w in enumerate(held)})


def _add_chips(sums32, theirs, pos):
    nw = len(sums32)
    split = 2
    hbs = [s.shape[0] // split for s in sums32]

    def body(pos_ref, *refs):
        ins, oth, outs, bufs = (refs[k * nw:(k + 1) * nw] for k in range(4))
        lsem, ssem, rsem = refs[4 * nw:]
        i = pl.program_id(0)
        x, y, c = _coords()

        def copies(w, j):
            rows = pl.ds(pl.multiple_of((pos_ref[1] * split + j) * hbs[w], 8), hbs[w])
            return (pltpu.make_async_copy(bufs[w].at[j], outs[w].at[rows, :], lsem.at[w, j]),
                    pltpu.make_async_remote_copy(
                        src_ref=bufs[w].at[j], dst_ref=outs[w].at[rows, :], send_sem=ssem.at[w, j],
                        recv_sem=rsem.at[w, j], device_id=(x, y, 1 - c), device_id_type=MESH))

        for w in range(nw):
            acc = ins[w][...]
            for r in range(3):
                acc = acc + oth[w][r].astype(F32)
            bufs[w][i] = acc
            for cp in copies(w, i):
                cp.start()

        @pl.when(i == split - 1)
        def _():
            for w in range(nw):
                for j in range(split):
                    local, remote = copies(w, j)
                    local.wait()
                    remote.wait()

    in_specs, oth_specs, shapes, scratch = [], [], [], []
    for s, hb in zip(sums32, hbs):
        in_specs.append(pl.BlockSpec((hb, s.shape[1]), lambda i, pos_ref: (i, 0)))
        oth_specs.append(pl.BlockSpec((3, hb, s.shape[1]), lambda i, pos_ref: (0, i, 0)))
        shapes.append(jax.ShapeDtypeStruct((2 * s.shape[0], s.shape[1]), F32))
        scratch.append(pltpu.VMEM((split, hb, s.shape[1]), F32))
    dma = pltpu.SemaphoreType.DMA
    return pl.pallas_call(
        body, name="add_chips",
        grid_spec=pltpu.PrefetchScalarGridSpec(
            num_scalar_prefetch=1, grid=(split,), in_specs=in_specs + oth_specs,
            out_specs=[pl.BlockSpec(memory_space=pl.ANY)] * nw,
            scratch_shapes=scratch + [dma((nw, split)), dma((nw, split)), dma((nw, split))]),
        out_shape=shapes,
        compiler_params=_params("arbitrary", vmem=VMEM_LIMIT),
    )(pos, *sums32, *theirs)


def _adamw_math(w, g, m, v):
    m = ADAM_B1 * m + (1.0 - ADAM_B1) * g
    v = ADAM_B2 * v + (1.0 - ADAM_B2) * (g * g)
    m_hat = m / (1.0 - ADAM_B1 ** ADAM_STEP)
    v_hat = v / (1.0 - ADAM_B2 ** ADAM_STEP)
    delta = -ADAM_LR * (m_hat / (jnp.sqrt(v_hat) + ADAM_EPS) + ADAM_WD * w)
    return delta, m, v


def _adamw(ws, gs, ms, vs):
    nw = len(ws)
    split = 8

    def body(*refs):
        w_r, g_r, m_r, v_r = (refs[i * nw:(i + 1) * nw] for i in range(4))
        g_o, d_o, m_o, v_o = (refs[(4 + i) * nw:(5 + i) * nw] for i in range(4))
        for k in range(nw):
            g = g_r[k][...]
            d, m, v = _adamw_math(w_r[k][...], g, m_r[k][...], v_r[k][...])
            g_o[k][...] = g
            d_o[k][...] = d
            m_o[k][...] = m
            v_o[k][...] = v

    specs = [pl.BlockSpec((w.shape[0] // split, w.shape[1]), lambda i: (i, 0)) for w in ws]
    shapes = [jax.ShapeDtypeStruct(w.shape, F32) for w in ws]
    outs = pl.pallas_call(
        body, name="adamw", grid=(split,),
        in_specs=specs * 4, out_specs=specs * 4, out_shape=shapes * 4,
        compiler_params=_params("parallel", vmem=VMEM_LIMIT),
    )(*ws, *gs, *ms, *vs)
    return outs[:nw], outs[nw:2 * nw], outs[2 * nw:3 * nw], outs[3 * nw:]


SMALL_ROWS = 8
SMALL_COLS = D_MODEL
LOSS_COL = RET_WIDTH + 24


def _small_allreduce_adamw(part, w, m, v, rider=None):
    def body(part_ref, w_ref, m_ref, v_ref, g_out, d_out, m_out, v_out, all_ref, send, recv):
        x, y, c = _coords()
        me = 4 * x + 2 * y + c
        all_ref[me] = part_ref[...]
        copies = []
        for rel in range(1, 8):
            px = 1 - x if rel & 4 else x
            py = 1 - y if rel & 2 else y
            pc = 1 - c if rel & 1 else c
            copies.append(pltpu.make_async_remote_copy(
                src_ref=part_ref, dst_ref=all_ref.at[me],
                send_sem=send.at[rel - 1], recv_sem=recv.at[rel - 1], device_id=(px, py, pc), device_id_type=MESH))
        for cp in copies:
            cp.start()
        for cp in copies:
            cp.wait()
        g = all_ref[0]
        for k in range(1, 8):
            g = g + all_ref[k]
        d, mn, vn = _adamw_math(w_ref[...], g, m_ref[...], v_ref[...])
        g_out[...] = g
        d_out[...] = d
        m_out[...] = mn
        v_out[...] = vn

    vm = pl.BlockSpec(memory_space=pltpu.VMEM)
    shape = jax.ShapeDtypeStruct((SMALL_ROWS, SMALL_COLS), F32)
    return _hosted_call(
        body, "small_allreduce_adamw", (1,),
        in_specs=[vm] * 4, out_specs=[vm] * 4, out_shape=[shape] * 4,
        scratch_shapes=[pltpu.VMEM((8, SMALL_ROWS, SMALL_COLS), F32),
                        pltpu.SemaphoreType.DMA((7,)), pltpu.SemaphoreType.DMA((7,))],
        operands=(part, w, m, v), rider=rider, semantics=["arbitrary"])


SMALL_NAMES = ("ret_decay_fwd", "ret_decay_bwd", "attn_sink", "ret_gn_gain",
               "ln1_gain", "ln1_bias", "ln2_gain", "ln2_bias")


LN_NAMES = ("ln1_gain", "ln1_bias", "ln2_gain", "ln2_bias")


def _pack_small(vals, extra=None):
    tail = jnp.zeros((1, 1), F32) if extra is None else extra.reshape(1, 1)
    row4 = jnp.concatenate([vals["ret_gn_gain"], vals["ret_decay_fwd"], vals["ret_decay_bwd"], vals["attn_sink"],
                            tail, jnp.zeros((1, SMALL_COLS - LOSS_COL - 1), F32)], axis=1)
    rows = [vals[n] for n in LN_NAMES] + [row4, jnp.zeros((SMALL_ROWS - 5, SMALL_COLS), F32)]
    return jnp.concatenate(rows, axis=0)


def _unpack_small(packed):
    o = RET_WIDTH
    where = [(n, i, 0, SMALL_COLS) for i, n in enumerate(LN_NAMES)] + [
        ("ret_gn_gain", 4, 0, o), ("ret_decay_fwd", 4, o, 8), ("ret_decay_bwd", 4, o + 8, 8),
        ("attn_sink", 4, o + 16, 8)]
    na, k = len(packed), len(where)

    def body(*refs):
        for a in range(na):
            rows = refs[a][...]
            for b, (_, row, col, n) in enumerate(where):
                refs[na + a * k + b][...] = rows[row:row + 1, col:col + n]

    vmem = pl.BlockSpec(memory_space=pltpu.VMEM)
    outs = pl.pallas_call(
        body, name="unpack_small", in_specs=[vmem] * na, out_specs=[vmem] * (na * k),
        out_shape=[jax.ShapeDtypeStruct((1, n), F32) for _ in range(na) for (_, _, _, n) in where])(*packed)
    return [{where[b][0]: outs[a * k + b] for b in range(k)} for a in range(na)]


def _local_step(x, p, tgt, w_in_t, rest, small, pos=None, small_state=None):
    bsz, s, _ = x.shape
    t = bsz * s
    x2d = x.reshape(t, D_MODEL)
    p2d = p.reshape(t, PLE_DIM)
    tgt2d = tgt.reshape(t, D_MODEL)
    dec_f = small["ret_decay_fwd"].reshape(8)
    dec_b = small["ret_decay_bwd"].reshape(8)
    lg_f = jnp.log1p(-jnp.exp2(dec_f))
    lg_b = jnp.log1p(-jnp.exp2(dec_b))
    per_lane = lambda v: jnp.repeat(v, HEAD_DIM).reshape(4, 1, LANES)
    lgf_l, lgb_l = per_lane(lg_f), per_lane(lg_b)
    sink = small["attn_sink"].reshape(8)
    slopes = 2.0 ** (-(jnp.arange(8, dtype=F32) + 1.0))
    gn_gain = small["ret_gn_gain"]
    g1, b1, g2, b2 = (small[n] for n in ("ln1_gain", "ln1_bias", "ln2_gain", "ln2_bias"))

    dist = pos is not None
    shard = dict(zip(REST_NAMES, rest)) if dist else {}
    near = lambda names, rels=NEAR: _gather_near_rider([shard[n] for n in names], rels)
    wave1, wave2, wave3 = ("w_out", "w_ple_gate", "w_ffn_gate"), ("w_ffn_up", "w_ple_proj"), ("w_ffn_down",)
    n1 = len(wave1)
    u, *o1 = _inproj(x2d, w_in_t, rider=near(wave1) if dist else None)
    u3 = u.reshape(bsz, s, IN_WIDTH)
    y_hat, y_rstd, y_ret, ret_rb, ret_kvf, *o2 = _ret_fwd(u3, lgf_l, lgb_l, gn_gain, rider=_merge_riders(
        [_gather_relay_rider(o1), near(wave2)]) if dist else None)
    y_att, att_p, att_ps, *o3 = _attn_fwd(u3, slopes, sink, rider=_merge_riders(
        [_gather_pass_rider(o2[:n1]), _gather_relay_rider(o2[n1:]), near(wave3, (1, 2, 3))]) if dist else None)
    gathered = dict(zip(wave1, o3[:n1]))
    w_out = _assemble_weights({"w_out": gathered["w_out"]})["w_out"] if dist else rest["w_out"]
    zh1, r1, hb, *o4 = _outproj_ln1(y_ret.reshape(t, RET_WIDTH), y_att.reshape(t, ATTN_WIDTH), x2d, w_out, g1, b1,
                                    rider=_gather_pass_rider(o3[n1:]) if dist else None)
    gathered.update(zip(wave2 + wave3, o4))
    wts = _assemble_weights(gathered) if dist else rest
    dz2, dz2b, gs, us, acts, pg, ple, sq, dg2, db2 = _ffn_fwd(
        zh1, hb, p2d, tgt2d, g1, b1, g2, b2, wts["gate4"], wts["up4"], wts["down4"], wts["ple_proj"], wts["ple_gate"])
    dgs, dus, dsp, dple, dz1, dyr, dya, dg1, db1 = _ffn_bwd(dz2, gs, us, pg, ple, zh1, r1, g1, wts["gate4"],
                                                          wts["up4"], wts["down4"], wts["ple_gate"], wts["w_out"])
    ffn_parts = list(_wgrad_ffn(acts, dgs, dus, hb, dz2b))
    d_w_out, d_ple_gate, d_ple_proj, *th_ffn = _wgrad_misc(
        y_ret.reshape(t, RET_WIDTH), y_att.reshape(t, ATTN_WIDTH), dz1, hb, dsp, p2d, dple,
        rider=_exchange_halves_rider(ffn_parts[:2]) if dist else None)
    misc_parts = [d_w_out.reshape(N_SHARD, D_MODEL // N_SHARD, D_MODEL), d_ple_proj,
                  d_ple_gate.reshape(N_SHARD, D_MODEL // N_SHARD, D_MODEL)]
    dyr3, dya3 = dyr.reshape(bsz, s, RET_WIDTH), dya.reshape(bsz, s, ATTN_WIDTH)
    if dist:
        s_gu = _add_halves(ffn_parts[:2], th_ffn, pos)
        half = FFN_SHARD // 2
        quarter = half // 2
        later_parts = [ffn_parts[2]] + misc_parts
        drq, drk, drv, drg, rpart, *o5 = _ret_bwd(u3, y_hat, y_rstd, (ret_rb, ret_kvf), dyr3, lgf_l, lgb_l, gn_gain,
                                                  rider=_merge_riders(
            [_exchange_chips_rider(s_gu[2:], rows=[(0, half), (0, quarter)]), _exchange_halves_rider(later_parts)]))
        s_dm = _add_halves(later_parts, o5[2:], pos)
        daq, dakv, spart, *o6 = _attn_bwd(u3, dya3, att_p, att_ps, rider=_exchange_chips_rider(
            [s_gu[3], s_dm[4]], rows=[(quarter, half - quarter), (0, half)], into=[o5[1], None]))
    else:
        drq, drk, drv, drg, rpart = _ret_bwd(u3, y_hat, y_rstd, (ret_rb, ret_kvf), dyr3, lgf_l, lgb_l, gn_gain)
        daq, dakv, spart = _attn_bwd(u3, dya3, att_p, att_ps)
    pieces = [a.reshape(t, -1) for a in (drq, drk, drv, drg, daq, dakv)]
    kv0 = CB_AK * LANES
    w_kv = jnp.concatenate([w_in_t[kv0 + o:kv0 + o + HEAD_DIM] for o in KV_ORDER], axis=0)
    d_in, *o7 = _wgrad_in(pieces, x2d, rider=_exchange_chips_rider(list(s_dm[5:])) if dist else None)
    d_in = d_in.reshape(N_SHARD, FFN_SHARD, D_MODEL)

    rsum = rpart
    lane_heads = lambda row: jnp.sum(row.reshape(4, 2, HEAD_DIM), axis=-1).reshape(8)
    dlg_f = lane_heads(rsum[:, 0, :]) + jnp.stack([jnp.sum(rsum[:, 2, :], -1), jnp.sum(rsum[:, 3, :], -1)], 1).reshape(8)
    dlg_b = lane_heads(rsum[:, 1, :]) + jnp.stack([jnp.sum(rsum[:, 4, :], -1), jnp.sum(rsum[:, 5, :], -1)], 1).reshape(8)
    chain = lambda d: -(math.log(2.0) * jnp.exp2(d)) / (1.0 - jnp.exp2(d))
    grads_small = {
        "ret_decay_fwd": (dlg_f * chain(dec_f)).reshape(1, 8),
        "ret_decay_bwd": (dlg_b * chain(dec_b)).reshape(1, 8),
        "attn_sink": jnp.sum(spart, axis=0)[:, 0:4, 0].reshape(1, 8),
        "ret_gn_gain": rsum[:, 6, :].reshape(1, RET_WIDTH),
        "ln1_gain": dg1, "ln1_bias": db1, "ln2_gain": dg2, "ln2_bias": db2,
    }
    if not dist:
        grad_x, = _inproj_bwd(dz1, pieces, w_in_t, w_kv)
        grads_rest = [misc_parts[0]] + ffn_parts + misc_parts[1:]
        return sq[0, 0], grad_x.reshape(bsz, s, D_MODEL), d_in, grads_rest, grads_small
    *small_out, th_in = _small_allreduce_adamw(_pack_small(grads_small, sq[0, 0]), *small_state,
                                               rider=_exchange_halves_rider([d_in]))
    s_in = _add_halves([d_in], [th_in], pos)
    grad_x, chips_in = _inproj_bwd(dz1, pieces, w_in_t, w_kv, rider=_exchange_chips_rider([s_in[1]]))
    sums32 = [s_in[0], s_dm[1], s_gu[0], s_gu[1], s_dm[0], s_dm[2], s_dm[3]]
    from_chips = [chips_in, o7[0], o5[0], o6[0], o6[1], o7[1], o7[2]]
    return grad_x.reshape(bsz, s, D_MODEL), sums32, from_chips, small_out


BIG_NAMES = ("w_in", "w_out", "w_ffn_gate", "w_ffn_up", "w_ffn_down", "w_ple_proj", "w_ple_gate")
REST_NAMES = BIG_NAMES[1:]
TRANSPOSED = ("w_in", "w_ffn_gate", "w_ffn_up")
WEIGHT_ORDER = ("w_in", "ret_decay_fwd", "ret_decay_bwd", "ret_gn_gain", "attn_sink", "w_out", "ln1_gain",
                "ln1_bias", "w_ffn_gate", "w_ffn_up", "w_ffn_down", "w_ple_proj", "w_ple_gate", "ln2_gain", "ln2_bias")


def _shard_rows(name, a):
    return jnp.swapaxes(a[0], 0, 1) if name in TRANSPOSED else a[0]


def _unshard_rows(name, a):
    return (jnp.swapaxes(a, 0, 1) if name in TRANSPOSED else a)[None]


def _assemble_weights(gathered):
    rows = lambda a: a.reshape(N_SHARD * a.shape[1], a.shape[2])
    same = lambda a: a
    layout = {"w_out": ("w_out", rows), "w_ffn_gate": ("gate4", same), "w_ffn_up": ("up4", same),
              "w_ffn_down": ("down4", same), "w_ple_proj": ("ple_proj", same), "w_ple_gate": ("ple_gate", rows)}
    return {layout[n][0]: layout[n][1](a) for n, a in gathered.items()}


def kernel(x, p, w_in, ret_decay_fwd, ret_decay_bwd, ret_gn_gain, attn_sink, w_out, ln1_gain, ln1_bias, w_ffn_gate, w_ffn_up, w_ffn_down, w_ple_proj, w_ple_gate, ln2_gain, ln2_bias, loss_target, m_w_in, m_ret_decay_fwd, m_ret_decay_bwd, m_ret_gn_gain, m_attn_sink, m_w_out, m_ln1_gain, m_ln1_bias, m_w_ffn_gate, m_w_ffn_up, m_w_ffn_down, m_w_ple_proj, m_w_ple_gate, m_ln2_gain, m_ln2_bias, v_w_in, v_ret_decay_fwd, v_ret_decay_bwd, v_ret_gn_gain, v_attn_sink, v_w_out, v_ln1_gain, v_ln1_bias, v_w_ffn_gate, v_w_ffn_up, v_w_ffn_down, v_w_ple_proj, v_w_ple_gate, v_ln2_gain, v_ln2_bias):
    w = dict(w_in=w_in, ret_decay_fwd=ret_decay_fwd, ret_decay_bwd=ret_decay_bwd, ret_gn_gain=ret_gn_gain,
             attn_sink=attn_sink, w_out=w_out, ln1_gain=ln1_gain, ln1_bias=ln1_bias, w_ffn_gate=w_ffn_gate,
             w_ffn_up=w_ffn_up, w_ffn_down=w_ffn_down, w_ple_proj=w_ple_proj, w_ple_gate=w_ple_gate,
             ln2_gain=ln2_gain, ln2_bias=ln2_bias)
    m = dict(w_in=m_w_in, ret_decay_fwd=m_ret_decay_fwd, ret_decay_bwd=m_ret_decay_bwd, ret_gn_gain=m_ret_gn_gain,
             attn_sink=m_attn_sink, w_out=m_w_out, ln1_gain=m_ln1_gain, ln1_bias=m_ln1_bias, w_ffn_gate=m_w_ffn_gate,
             w_ffn_up=m_w_ffn_up, w_ffn_down=m_w_ffn_down, w_ple_proj=m_w_ple_proj, w_ple_gate=m_w_ple_gate,
             ln2_gain=m_ln2_gain, ln2_bias=m_ln2_bias)
    v = dict(w_in=v_w_in, ret_decay_fwd=v_ret_decay_fwd, ret_decay_bwd=v_ret_decay_bwd, ret_gn_gain=v_ret_gn_gain,
             attn_sink=v_attn_sink, w_out=v_w_out, ln1_gain=v_ln1_gain, ln1_bias=v_ln1_bias, w_ffn_gate=v_w_ffn_gate,
             w_ffn_up=v_w_ffn_up, w_ffn_down=v_w_ffn_down, w_ple_proj=v_w_ple_proj, w_ple_gate=v_w_ple_gate,
             ln2_gain=v_ln2_gain, ln2_bias=v_ln2_bias)
    big = lambda d: [_shard_rows(n, d[n]) for n in BIG_NAMES]
    small = lambda d: {n: d[n] for n in SMALL_NAMES}

    chip = 2 * lax.axis_index("x") + lax.axis_index("y")
    pos = jnp.stack([chip, lax.axis_index("c")]).astype(jnp.int32)

    shards = big(w)
    w_in4, rest16 = _gather_and_cast(shards[0].astype(BF16), shards[1:])
    w_in_t = w_in4.reshape(IN_WIDTH, D_MODEL)
    grad_x, sums32, from_chips, (g_s, d_s, m_s, v_s) = _local_step(
        x, p[0], loss_target, w_in_t, rest16, small(w), pos=pos,
        small_state=(_pack_small(small(w)), _pack_small(small(m)), _pack_small(small(v))))
    g_big, d_big, m_big, v_big = _adamw(big(w), _add_chips(sums32, from_chips, pos), big(m), big(v))
    loss = g_s[4, LOSS_COL] * (0.5 / D_MODEL)

    def tree(bigs, smalls):
        out = {n: _unshard_rows(n, a) for n, a in zip(BIG_NAMES, bigs)}
        out.update(smalls)
        return [out[n] for n in WEIGHT_ORDER]

    smalls = _unpack_small([g_s, d_s, m_s, v_s])
    return (loss, grad_x, *(a for bigs, s in zip((g_big, d_big, m_big, v_big), smalls) for a in tree(bigs, s)))
```

```python
import functools
import math

import jax
import jax.numpy as jnp
from jax import lax
from jax.experimental import pallas as pl
from jax.experimental.pallas import tpu as pltpu

F32 = jnp.float32
BF16 = jnp.bfloat16

D_MODEL = 1024
HEAD_DIM = 64
RET_HEADS = 8
ATTN_HEADS = 8
RET_WIDTH = 512
ATTN_WIDTH = 512
KV_WIDTH = 128
IN_WIDTH = 2816
FFN = 2816
N_SHARD = 4
FFN_SHARD = FFN // N_SHARD
PLE_DIM = 256
CHUNK = 128
LANES = 128
ALPHA = 2.0 ** 0.25
LN_EPS = 1e-5
GN_EPS = 1e-5
NEG_INF = -1e30
ADAM_LR = 0.001
ADAM_B1 = 0.9
ADAM_B2 = 0.999
ADAM_EPS = 1e-08
ADAM_WD = 0.01
ADAM_STEP = 10
VMEM_LIMIT = 56 * 1024 * 1024
MESH = pl.DeviceIdType.MESH

CB_RQ, CB_RK, CB_RV, CB_RG, CB_AQ, CB_AK, CB_AV = 0, 4, 8, 12, 16, 20, 21


def _dot(a, b):
    return jnp.dot(a, b, preferred_element_type=F32)


def _dot_nt(a, b):
    return lax.dot_general(a, b, (((1,), (1,)), ((), ())), preferred_element_type=F32)


def _dot_tn(a, b):
    return lax.dot_general(a, b, (((0,), (0,)), ((), ())), preferred_element_type=F32)


def _sigmoid(x):
    return 1.0 / (1.0 + jnp.exp(-x))


def _params(*sem, vmem=None):
    return pltpu.CompilerParams(dimension_semantics=tuple(sem) if sem else None, vmem_limit_bytes=vmem)


class _Rider:
    def __init__(self, ins, out_shapes, sems, start, finish, aliases=None):
        self.ins, self.out_shapes, self.sems = list(ins), list(out_shapes), list(sems)
        self.start, self.finish, self.aliases = start, finish, dict(aliases or {})


def _merge_riders(riders):
    riders = [r for r in riders if r is not None]
    if len(riders) == 1:
        return riders[0]
    bounds, aliases = [], {}
    i0 = o0 = s0 = 0
    for r in riders:
        bounds.append((i0, o0, s0))
        aliases.update({i0 + i: o0 + o for i, o in r.aliases.items()})
        i0, o0, s0 = i0 + len(r.ins), o0 + len(r.out_shapes), s0 + len(r.sems)

    def each(method):
        def run(ins, outs, sems):
            for r, (i, o, s) in zip(riders, bounds):
                getattr(r, method)(ins[i:i + len(r.ins)], outs[o:o + len(r.out_shapes)], sems[s:s + len(r.sems)])
        return run

    return _Rider([a for r in riders for a in r.ins], [a for r in riders for a in r.out_shapes],
                  [a for r in riders for a in r.sems], each("start"), each("finish"), aliases)


def _hosted_call(body, name, grid, in_specs, out_specs, out_shape, scratch_shapes, operands, rider=None,
                 semantics=None):
    n_in, n_out, n_scr = len(in_specs), len(out_specs), len(scratch_shapes)
    if rider is None:
        return pl.pallas_call(
            body, name=name, grid=grid, in_specs=in_specs, out_specs=out_specs, out_shape=out_shape,
            scratch_shapes=scratch_shapes,
            compiler_params=_params(*(semantics or ["parallel"] * len(grid)), vmem=VMEM_LIMIT))(*operands)
    r_in, r_out = len(rider.ins), len(rider.out_shapes)

    def full_body(*refs):
        main_in, rin = refs[:n_in], refs[n_in:n_in + r_in]
        o0 = n_in + r_in
        main_out, rout = refs[o0:o0 + n_out], refs[o0 + n_out:o0 + n_out + r_out]
        s0 = o0 + n_out + r_out
        main_scr, rsem = refs[s0:s0 + n_scr], refs[s0 + n_scr:]
        first = functools.reduce(jnp.logical_and, [pl.program_id(a) == 0 for a in range(len(grid))])
        last = functools.reduce(jnp.logical_and, [pl.program_id(a) == g - 1 for a, g in enumerate(grid)])

        @pl.when(first)
        def _():
            rider.start(rin, rout, rsem)

        body(*main_in, *main_out, *main_scr)

        @pl.when(last)
        def _():
            rider.finish(rin, rout, rsem)

    hbm = pl.BlockSpec(memory_space=pl.ANY)
    return pl.pallas_call(
        full_body, name=name, grid=grid,
        in_specs=list(in_specs) + [hbm] * r_in, out_specs=list(out_specs) + [hbm] * r_out,
        out_shape=list(out_shape) + rider.out_shapes,
        scratch_shapes=list(scratch_shapes) + rider.sems,
        input_output_aliases={n_in + i: n_out + o for i, o in rider.aliases.items()},
        compiler_params=_params(*(["arbitrary"] * len(grid)), vmem=VMEM_LIMIT),
    )(*operands, *rider.ins)


def _loop_grouped(n, body, init, per_trip=2):
    if n % per_trip:
        return lax.fori_loop(0, n, body, init)

    def trip(i, c):
        for j in range(per_trip):
            c = body(per_trip * i + j, c)
        return c

    return lax.fori_loop(0, n // per_trip, trip, init)


def _head_mean(x, m0):
    s0 = jnp.sum(jnp.where(m0, x, 0.0), axis=1, keepdims=True)
    s1 = jnp.sum(jnp.where(m0, 0.0, x), axis=1, keepdims=True)
    return jnp.where(m0, s0, s1) * (1.0 / HEAD_DIM)


def _inproj(x2d, w_in_t, rider=None):
    t = x2d.shape[0]
    tm = 512
    nb = 256

    def body(x_ref, w_ref, o_ref):
        xb = x_ref[...].astype(BF16)
        for n in range(0, IN_WIDTH, nb):
            o_ref[:, n:n + nb] = _dot_nt(xb, w_ref[n:n + nb, :]).astype(BF16)

    return _hosted_call(
        body, "inproj", (t // tm,),
        in_specs=[pl.BlockSpec((tm, D_MODEL), lambda i: (i, 0)),
                  pl.BlockSpec((IN_WIDTH, D_MODEL), lambda i: (0, 0))],
        out_specs=[pl.BlockSpec((tm, IN_WIDTH), lambda i: (i, 0))],
        out_shape=[jax.ShapeDtypeStruct((t, IN_WIDTH), BF16)],
        scratch_shapes=[], operands=(x2d, w_in_t), rider=rider)


def _outproj_ln1(y_ret, y_att, x2d, w_out, gain, bias, rider=None):
    t = x2d.shape[0]
    tm = 512

    def body(yr_ref, ya_ref, x_ref, w_ref, g_ref, b_ref, zh_ref, r_ref, hb_ref):
        mix = _dot(yr_ref[...], w_ref[0:RET_WIDTH, :]) + _dot(ya_ref[...], w_ref[RET_WIDTH:, :])
        z = ALPHA * x_ref[...] + mix
        mu = jnp.mean(z, axis=1, keepdims=True)
        zc = z - mu
        var = jnp.mean(zc * zc, axis=1, keepdims=True)
        r = lax.rsqrt(var + LN_EPS)
        zh = zc * r
        zh_ref[...] = zh
        r_ref[...] = r
        hb_ref[...] = (zh * g_ref[...] + b_ref[...]).astype(BF16)

    row = lambda w: pl.BlockSpec((tm, w), lambda i: (i, 0))
    const = lambda s: pl.BlockSpec(s, lambda i: (0, 0))
    return _hosted_call(
        body, "outproj_ln1", (t // tm,),
        in_specs=[row(RET_WIDTH), row(ATTN_WIDTH), row(D_MODEL), const((D_MODEL, D_MODEL)),
                  const((1, D_MODEL)), const((1, D_MODEL))],
        out_specs=[row(D_MODEL), row(1), row(D_MODEL)],
        out_shape=[jax.ShapeDtypeStruct((t, D_MODEL), F32), jax.ShapeDtypeStruct((t, 1), F32),
                   jax.ShapeDtypeStruct((t, D_MODEL), BF16)],
        scratch_shapes=[], operands=(y_ret, y_att, x2d, w_out, gain, bias), rider=rider)


def _load_resident(step, pairs, sems):
    copies = [pltpu.make_async_copy(src, dst, sems.at[i]) for i, (src, dst) in enumerate(pairs)]

    @pl.when(step == 0)
    def _():
        for cp in copies:
            cp.start()
        for cp in copies:
            cp.wait()


FFN_CHUNK = 256
N_FFN_CHUNK = FFN // FFN_CHUNK


def _resident_quarters(hbm, vmem):
    q = FFN // N_SHARD
    return [(hbm.at[pl.ds(j * q, q), :], vmem.at[pl.ds(j * q, q), :]) for j in range(N_SHARD)]


def _ln2_loss_tail(zh, mixed, tgt, g1, b1, g2, b2):
    z2 = ALPHA * (zh * g1 + b1) + mixed
    mu = jnp.mean(z2, axis=1, keepdims=True)
    zc = z2 - mu
    var = jnp.mean(zc * zc, axis=1, keepdims=True)
    r = lax.rsqrt(var + LN_EPS)
    zh2 = zc * r
    err = zh2 * g2 + b2 - tgt
    dy = err * (1.0 / D_MODEL)
    dzh = dy * g2
    m1 = jnp.mean(dzh, axis=1, keepdims=True)
    m2 = jnp.mean(dzh * zh2, axis=1, keepdims=True)
    dz2 = r * (dzh - m1 - zh2 * m2)
    return dz2, jnp.sum(err * err), jnp.sum(dy * zh2, axis=0, keepdims=True), jnp.sum(dy, axis=0, keepdims=True)


def _ffn_fwd(zh1, hb, p2d, tgt, g1, b1, g2, b2, wg4, wu4, wd4, wpe, wpg):
    t = zh1.shape[0]
    tm = 256
    wg_t, wu_t, wd_all = (w.reshape(FFN, D_MODEL) for w in (wg4, wu4, wd4))

    def body(zh_ref, hb_ref, p_ref, t_ref, g1_ref, b1_ref, g2_ref, b2_ref,
             wg_hbm, wu_hbm, wd_hbm, wpe_hbm, wpg_hbm,
             dz_ref, dzb_ref, gs_ref, us_ref, act_ref, pg_ref, ple_ref, loss_ref, dg2_ref, db2_ref,
             wg, wu, wd, wpe, wpg, wsem):
        step = pl.program_id(0)
        loads = _resident_quarters(wg_hbm, wg) + _resident_quarters(wu_hbm, wu) + _resident_quarters(wd_hbm, wd)
        pc = D_MODEL // N_SHARD
        loads += [(wpe_hbm.at[j], wpe.at[:, pl.ds(j * pc, pc)]) for j in range(N_SHARD)]
        _load_resident(step, loads + [(wpg_hbm, wpg)], wsem)

        @pl.when(step == 0)
        def _():
            loss_ref[...] = jnp.zeros_like(loss_ref)
            dg2_ref[...] = jnp.zeros_like(dg2_ref)
            db2_ref[...] = jnp.zeros_like(db2_ref)

        hbv = hb_ref[...]
        ffn = jnp.zeros((tm, D_MODEL), F32)
        acts = []
        chunks = [slice(n * FFN_CHUNK, (n + 1) * FFN_CHUNK) for n in range(N_FFN_CHUNK)]
        for n in range(N_FFN_CHUNK + 1):
            if n < N_FFN_CHUNK:
                gj = _dot_nt(hbv, wg[chunks[n], :])
                uj = _dot_nt(hbv, wu[chunks[n], :])
                gs_ref[:, chunks[n]] = gj.astype(BF16)
                us_ref[:, chunks[n]] = uj.astype(BF16)
                acts.append((gj * _sigmoid(gj) * uj).astype(BF16))
                act_ref[:, chunks[n]] = acts[n]
            if n > 0:
                ffn = ffn + _dot(acts[n - 1], wd[chunks[n - 1], :])
        ple = _dot(p_ref[...].astype(BF16), wpe[...])
        pg = _sigmoid(_dot(hbv, wpg[...]))
        pg_ref[...] = pg.astype(BF16)
        ple_ref[...] = ple.astype(BF16)
        dz2, sq, dg2, db2 = _ln2_loss_tail(zh_ref[...], ffn + pg * ple, t_ref[...], g1_ref[...], b1_ref[...],
                                           g2_ref[...], b2_ref[...])
        dz_ref[...] = dz2
        dzb_ref[...] = dz2.astype(BF16)
        loss_ref[...] += sq
        dg2_ref[...] += dg2
        db2_ref[...] += db2

    row = lambda w: pl.BlockSpec((tm, w), lambda i: (i, 0))
    const = lambda s: pl.BlockSpec(s, lambda i: (0, 0))
    hid_shape = jax.ShapeDtypeStruct((t, FFN), BF16)
    hbm = pl.BlockSpec(memory_space=pl.ANY)
    return pl.pallas_call(
        body, name="ffn_fwd", grid=(t // tm,),
        in_specs=[row(D_MODEL), row(D_MODEL), row(PLE_DIM), row(D_MODEL),
                  const((1, D_MODEL)), const((1, D_MODEL)), const((1, D_MODEL)), const((1, D_MODEL)),
                  hbm, hbm, hbm, hbm, hbm],
        out_specs=[row(D_MODEL), row(D_MODEL), row(FFN), row(FFN), row(FFN), row(D_MODEL), row(D_MODEL),
                   const((8, LANES)), const((1, D_MODEL)), const((1, D_MODEL))],
        out_shape=[jax.ShapeDtypeStruct((t, D_MODEL), F32), jax.ShapeDtypeStruct((t, D_MODEL), BF16),
                   hid_shape, hid_shape, hid_shape,
                   jax.ShapeDtypeStruct((t, D_MODEL), BF16), jax.ShapeDtypeStruct((t, D_MODEL), BF16),
                   jax.ShapeDtypeStruct((8, LANES), F32),
                   jax.ShapeDtypeStruct((1, D_MODEL), F32), jax.ShapeDtypeStruct((1, D_MODEL), F32)],
        scratch_shapes=[pltpu.VMEM((FFN, D_MODEL), BF16), pltpu.VMEM((FFN, D_MODEL), BF16),
                        pltpu.VMEM((FFN, D_MODEL), BF16),
                        pltpu.VMEM((PLE_DIM, D_MODEL), BF16), pltpu.VMEM(wpg.shape, BF16),
                        pltpu.SemaphoreType.DMA((4 * N_SHARD + 1,))],
        compiler_params=_params("arbitrary", vmem=VMEM_LIMIT),
    )(zh1, hb, p2d, tgt, g1, b1, g2, b2, wg_t, wu_t, wd_all, wpe, wpg)


def _ret_tables(lgf, lgb):
    c = CHUNK
    row = lax.broadcasted_iota(jnp.int32, (c, LANES), 0).astype(F32)
    ii = lax.broadcasted_iota(jnp.int32, (c, c), 0).astype(F32)
    jj = lax.broadcasted_iota(jnp.int32, (c, c), 1).astype(F32)
    diff = ii - jj
    dmats = []
    for h in range(2):
        lf = lgf[:, h * HEAD_DIM:h * HEAD_DIM + 1]
        lb = lgb[:, h * HEAD_DIM:h * HEAD_DIM + 1]
        dmats.append(jnp.where(diff > 0, jnp.exp(lf * jnp.maximum(diff, 0.0)),
                               jnp.where(diff < 0, jnp.exp(lb * jnp.maximum(-diff, 0.0)), 2.0)))
    tab = dict(
        qdec_f=jnp.exp(lgf * (row + 1.0)), kdec_f=jnp.exp(lgf * (c - 1.0 - row)),
        qdec_b=jnp.exp(lgb * (c - row)), kdec_b=jnp.exp(lgb * row),
        cdec_f=jnp.exp(lgf * c), cdec_b=jnp.exp(lgb * c),
        d0=dmats[0], d1=dmats[1], row=row, diff=diff)
    r = lax.broadcasted_iota(jnp.int32, (LANES, LANES), 0) < HEAD_DIM
    cc = lax.broadcasted_iota(jnp.int32, (LANES, LANES), 1) < HEAD_DIM
    tab["bd"] = r == cc
    tab["m0"] = lax.broadcasted_iota(jnp.int32, (c, LANES), 1) < HEAD_DIM
    return tab


def _ret_specs(bsz, s):
    blk = lambda cb: pl.BlockSpec((bsz, s, LANES), lambda p, cb=cb: (0, 0, cb + p))
    lane = pl.BlockSpec((None, 1, LANES), lambda p: (p, 0, 0))
    gain = pl.BlockSpec((1, LANES), lambda p: (0, p))
    pair = pl.BlockSpec((bsz, s, LANES), lambda p: (0, 0, p))
    return blk, lane, gain, pair


def _ret_state_spec(bsz, n_chunk):
    spec = pl.BlockSpec((None, bsz, n_chunk, LANES, LANES), lambda p: (p, 0, 0, 0, 0))
    return spec, jax.ShapeDtypeStruct((4, bsz, n_chunk, LANES, LANES), F32)


def _ret_kv_states(tb, k_ref, v_ref, rb_ref, kvf_ref, n_chunk):
    c = CHUNK
    bsz = k_ref.shape[0]
    bd = tb["bd"]

    def contributions(n, carry):
        sl = pl.ds(pl.multiple_of(n * c, c), c)
        kfb = []
        for b in range(bsz):
            k32 = k_ref[b, sl, :].astype(F32)
            kfb.append(jnp.concatenate([k32 * tb["kdec_f"], k32 * tb["kdec_b"]], axis=1).astype(BF16))
        kvs = [_dot_tn(kfb[b], v_ref[b, sl, :]) for b in range(bsz)]
        for b in range(bsz):
            kvf_ref[b, n] = jnp.where(bd, kvs[b][0:LANES], 0.0)
            rb_ref[b, n] = jnp.where(bd, kvs[b][LANES:], 0.0)
        return carry

    lax.fori_loop(0, n_chunk, contributions, 0, unroll=2)

    def recur(i, rbs):
        n = n_chunk - 1 - i
        new = []
        for b in range(bsz):
            own = rb_ref[b, n]
            rb_ref[b, n] = rbs[b]
            new.append(rbs[b] * tb["cdec_b"] + own)
        return tuple(new)

    lax.fori_loop(0, n_chunk, recur, tuple(jnp.zeros((LANES, LANES), F32) for _ in range(bsz)))


def _split_rows(x, m0):
    return jnp.concatenate([jnp.where(m0, x, 0.0), jnp.where(m0, 0.0, x)], axis=0).astype(BF16)


def _ret_fwd(u3, lgf_l, lgb_l, gn_gain, rider=None):
    bsz, s, _ = u3.shape
    n_chunk = s // CHUNK
    c = CHUNK

    def body(q_ref, k_ref, v_ref, g_ref, lgf_ref, lgb_ref, gain_ref, yh_ref, rstd_ref, o_ref, rb_ref, kvf_ref):
        tb = _ret_tables(lgf_ref[...], lgb_ref[...])
        m0 = tb["m0"]
        gain = gain_ref[...]
        rows = range(bsz)
        _ret_kv_states(tb, k_ref, v_ref, rb_ref, kvf_ref, n_chunk)

        def chunk(n, rfs):
            sl = pl.ds(pl.multiple_of(n * c, c), c)
            qs = [q_ref[b, sl, :].astype(F32) * 0.125 for b in rows]
            s01 = [_dot_nt(_split_rows(qs[b], m0), k_ref[b, sl, :]) for b in rows]
            ys = []
            for b in rows:
                lhs = jnp.concatenate([s01[b][0:c] * tb["d0"], s01[b][c:] * tb["d1"],
                                       qs[b] * tb["qdec_f"], qs[b] * tb["qdec_b"]], axis=1).astype(BF16)
                rhs = jnp.concatenate([_split_rows(v_ref[b, sl, :].astype(F32), m0),
                                       rfs[b].astype(BF16), rb_ref[b, n].astype(BF16)], axis=0)
                ys.append(_dot(lhs, rhs))
            new = []
            for b in rows:
                y = ys[b]
                mu = _head_mean(y, m0)
                yc = y - mu
                rstd = lax.rsqrt(_head_mean(yc * yc, m0) + GN_EPS)
                yh = yc * rstd
                g = g_ref[b, sl, :].astype(F32)
                yh_ref[b, sl, :] = yh
                rstd_ref[b, sl, :] = rstd
                o_ref[b, sl, :] = (yh * gain * (g * _sigmoid(g))).astype(BF16)
                new.append(rfs[b] * tb["cdec_f"] + kvf_ref[b, n])
            return tuple(new)

        _loop_grouped(n_chunk, chunk, tuple(jnp.zeros((LANES, LANES), F32) for _ in rows))

    blk, lane, gain, pair = _ret_specs(bsz, s)
    state, state_shape = _ret_state_spec(bsz, n_chunk)
    return _hosted_call(
        body, "ret_fwd", (4,),
        in_specs=[blk(CB_RQ), blk(CB_RK), blk(CB_RV), blk(CB_RG), lane, lane, gain],
        out_specs=[pair, pair, pair, state, state],
        out_shape=[jax.ShapeDtypeStruct((bsz, s, RET_WIDTH), F32), jax.ShapeDtypeStruct((bsz, s, RET_WIDTH), F32),
                   jax.ShapeDtypeStruct((bsz, s, RET_WIDTH), BF16), state_shape, state_shape],
        scratch_shapes=[],
        operands=(u3, u3, u3, u3, lgf_l, lgb_l, gn_gain), rider=rider)


def _ret_bwd(u3, y_hat, y_rstd, states, d_o, lgf_l, lgb_l, gn_gain, rider=None):
    bsz, s, _ = u3.shape
    n_chunk = s // CHUNK
    c = CHUNK

    def body(q_ref, k_ref, v_ref, g_ref, yh_ref, rstd_ref, do_ref, lgf_ref, lgb_ref, gain_ref, rb_ref, kvf_ref,
             dq_ref, dk_ref, dv_ref, dg_ref, part_ref,
             rf_ref, dirf_ref, dy_ref, dk_acc, dv_acc, pa0, pa1, vec_ref):
        tb = _ret_tables(lgf_ref[...], lgb_ref[...])
        m0, bd, row = tb["m0"], tb["bd"], tb["row"]
        gain = gain_ref[...]
        wf = jnp.maximum(tb["diff"], 0.0)
        wb = jnp.maximum(-tb["diff"], 0.0)
        rows = range(bsz)
        zero_states = tuple(jnp.zeros((LANES, LANES), F32) for _ in rows)
        for ref in (pa0, pa1):
            ref[...] = jnp.zeros_like(ref)
        vec_ref[...] = jnp.zeros_like(vec_ref)

        def sweep_fwd(n, carry):
            rfs, gbs = carry
            sl = pl.ds(pl.multiple_of(n * c, c), c)
            qs, ks, vs, dys, dybs, q01, k01, dy01 = [], [], [], [], [], [], [], []
            dgain = jnp.zeros((1, LANES), F32)
            for b in rows:
                q = q_ref[b, sl, :].astype(F32) * 0.125
                k = k_ref[b, sl, :]
                yh = yh_ref[b, sl, :]
                rstd = rstd_ref[b, sl, :]
                do = do_ref[b, sl, :].astype(F32)
                g = g_ref[b, sl, :].astype(F32)
                sg = _sigmoid(g)
                sil = g * sg
                dyh = do * gain * sil
                dg_ref[b, sl, :] = (do * yh * gain * sg * (1.0 + g * (1.0 - sg))).astype(BF16)
                dgain = dgain + jnp.sum(do * yh * sil, axis=0, keepdims=True)
                dy = rstd * (dyh - _head_mean(dyh, m0) - yh * _head_mean(dyh * yh, m0))
                dyb = dy.astype(BF16)
                dy_ref[b, sl, :] = dyb
                rf_ref[b, n] = rfs[b]
                qs.append(q)
                ks.append(k)
                vs.append(v_ref[b, sl, :])
                dys.append(dy)
                dybs.append(dyb)
                q01.append(_split_rows(q, m0))
                k01.append(_split_rows(k.astype(F32), m0))
                dy01.append(_split_rows(dy, m0))
            s01 = [_dot_nt(q01[b], ks[b]) for b in rows]
            da01 = [_dot_nt(dy01[b], vs[b]) for b in rows]
            rbn = [rb_ref[b, n] for b in rows]
            states = [jnp.concatenate([rfs[b], rbn[b]], axis=0).astype(BF16) for b in rows]
            dqc = [_dot_nt(dybs[b], states[b]) for b in rows]
            gbb = [gbs[b].astype(BF16) for b in rows]
            dkb = [_dot_nt(vs[b], gbb[b]) for b in rows]
            qfb = [jnp.concatenate([qs[b] * tb["qdec_f"], qs[b] * tb["qdec_b"]], axis=1) for b in rows]
            direct = [_dot_tn(qfb[b].astype(BF16), dybs[b]) for b in rows]
            ds_cat, ds_rows, a_rows = [], [], []
            for b in rows:
                a0 = s01[b][0:c] * tb["d0"]
                a1 = s01[b][c:] * tb["d1"]
                pa0[...] += da01[b][0:c] * a0
                pa1[...] += da01[b][c:] * a1
                ds0 = da01[b][0:c] * tb["d0"]
                ds1 = da01[b][c:] * tb["d1"]
                ds_cat.append(jnp.concatenate([ds0, ds1], axis=1).astype(BF16))
                ds_rows.append(jnp.concatenate([ds0, ds1], axis=0).astype(BF16))
                a_rows.append(jnp.concatenate([a0, a1], axis=0).astype(BF16))
            kbd = [ks[b].astype(F32) * tb["kdec_b"] for b in rows]
            dq_in = [_dot(ds_cat[b], k01[b]) for b in rows]
            dk_in = [_dot_tn(ds_rows[b], q01[b]) for b in rows]
            dv_in = [_dot_tn(a_rows[b], dy01[b]) for b in rows]
            dv_gb = [_dot(kbd[b].astype(BF16), gbb[b]) for b in rows]
            new_rf, new_gb = [], []
            dlf = jnp.zeros((1, LANES), F32)
            dlb = jnp.zeros((1, LANES), F32)
            for b in rows:
                dqf, dqb = dqc[b][:, 0:LANES], dqc[b][:, LANES:]
                qf, qb = qfb[b][:, 0:LANES], qfb[b][:, LANES:]
                dq = dq_in[b] + dqf * tb["qdec_f"] + dqb * tb["qdec_b"]
                dq_ref[b, sl, :] = (dq * 0.125).astype(BF16)
                dk_acc[b, sl, :] = dk_in[b] + dkb[b] * tb["kdec_b"]
                dv_acc[b, sl, :] = dv_in[b] + dv_gb[b]
                dlf = dlf + jnp.sum((row + 1.0) * qf * dqf, axis=0, keepdims=True)
                dlb = dlb + jnp.sum((c - row) * qb * dqb + row * kbd[b] * dkb[b], axis=0, keepdims=True)
                dlb = dlb + c * tb["cdec_b"] * jnp.sum(gbs[b] * rbn[b], axis=0, keepdims=True)
                dirf_ref[b, n] = jnp.where(bd, direct[b][0:LANES], 0.0)
                new_gb.append(jnp.where(bd, direct[b][LANES:], 0.0) + tb["cdec_b"] * gbs[b])
                new_rf.append(rfs[b] * tb["cdec_f"] + kvf_ref[b, n])
            vec_ref[0:1, :] += dlf
            vec_ref[1:2, :] += dlb
            vec_ref[6:7, :] += dgain
            return tuple(new_rf), tuple(new_gb)

        _loop_grouped(n_chunk, sweep_fwd, (zero_states, zero_states), per_trip=4)

        def sweep_bwd(i, gfs):
            n = n_chunk - 1 - i
            sl = pl.ds(pl.multiple_of(n * c, c), c)
            gfb = [gfs[b].astype(BF16) for b in rows]
            kfd = [k_ref[b, sl, :].astype(F32) * tb["kdec_f"] for b in rows]
            dkf = [_dot_nt(v_ref[b, sl, :], gfb[b]) for b in rows]
            dvf = [_dot(kfd[b].astype(BF16), gfb[b]) for b in rows]
            new = []
            dlf = jnp.zeros((1, LANES), F32)
            for b in rows:
                dk_ref[b, sl, :] = (dk_acc[b, sl, :] + dkf[b] * tb["kdec_f"]).astype(BF16)
                dv_ref[b, sl, :] = (dv_acc[b, sl, :] + dvf[b]).astype(BF16)
                dlf = dlf + jnp.sum((c - 1.0 - row) * kfd[b] * dkf[b], axis=0, keepdims=True)
                dlf = dlf + c * tb["cdec_f"] * jnp.sum(gfs[b] * rf_ref[b, n], axis=0, keepdims=True)
                new.append(dirf_ref[b, n] + tb["cdec_f"] * gfs[b])
            vec_ref[0:1, :] += dlf
            return tuple(new)

        _loop_grouped(n_chunk, sweep_bwd, zero_states)
        vec_ref[2:3, :] = jnp.sum(pa0[...] * wf, axis=0, keepdims=True)
        vec_ref[3:4, :] = jnp.sum(pa1[...] * wf, axis=0, keepdims=True)
        vec_ref[4:5, :] = jnp.sum(pa0[...] * wb, axis=0, keepdims=True)
        vec_ref[5:6, :] = jnp.sum(pa1[...] * wb, axis=0, keepdims=True)
        part_ref[...] = vec_ref[...]

    blk, lane, gain, pair = _ret_specs(bsz, s)
    out_bf = jax.ShapeDtypeStruct((bsz, s, RET_WIDTH), BF16)
    state = pltpu.VMEM((bsz, n_chunk, LANES, LANES), F32)
    saved = _ret_state_spec(bsz, n_chunk)[0]
    return _hosted_call(
        body, "ret_bwd", (4,),
        in_specs=[blk(CB_RQ), blk(CB_RK), blk(CB_RV), blk(CB_RG), pair, pair, pair, lane, lane, gain, saved, saved],
        out_specs=[pair, pair, pair, pair, pl.BlockSpec((None, 8, LANES), lambda p: (p, 0, 0))],
        out_shape=[out_bf, out_bf, out_bf, out_bf, jax.ShapeDtypeStruct((4, 8, LANES), F32)],
        scratch_shapes=[state, state,
                        pltpu.VMEM((bsz, s, LANES), BF16), pltpu.VMEM((bsz, s, LANES), F32),
                        pltpu.VMEM((bsz, s, LANES), F32),
                        pltpu.VMEM((c, c), F32), pltpu.VMEM((c, c), F32), pltpu.VMEM((8, LANES), F32)],
        operands=(u3, u3, u3, u3, y_hat, y_rstd, d_o, lgf_l, lgb_l, gn_gain, *states), rider=rider)


def _attn_window_tables(n, s):
    qi = lax.broadcasted_iota(jnp.int32, (CHUNK, 3 * CHUNK), 0)
    kj = lax.broadcasted_iota(jnp.int32, (CHUNK, 3 * CHUNK), 1)
    dist = jnp.abs(kj - CHUNK - qi)
    kpos = n * CHUNK - CHUNK + kj
    valid = (dist <= CHUNK) & (kpos >= 0) & (kpos < s)
    return dist.astype(F32), valid


def _dup_kv_head(x, g):
    lane = lax.broadcasted_iota(jnp.int32, x.shape, 1)
    keep = (lane < HEAD_DIM) == (g == 0)
    xf = x.astype(F32)
    return jnp.where(keep, xf, pltpu.roll(xf, HEAD_DIM, 1))


def _attn_specs(s):
    q = pl.BlockSpec((None, s, 2 * LANES), lambda b, g: (b, 0, CB_AQ // 2 + g))
    k = pl.BlockSpec((None, s, LANES), lambda b, g: (b, 0, CB_AK))
    v = pl.BlockSpec((None, s, LANES), lambda b, g: (b, 0, CB_AV))
    grp = pl.BlockSpec((None, s, 2 * LANES), lambda b, g: (b, 0, g))
    smem = pl.BlockSpec(memory_space=pltpu.SMEM)
    return q, k, v, grp, smem


def _fill_padded(dst_ref, val, s):
    dst_ref[0:CHUNK, :] = jnp.zeros((CHUNK, LANES), dst_ref.dtype)
    dst_ref[CHUNK:CHUNK + s, :] = val.astype(dst_ref.dtype)
    dst_ref[CHUNK + s:2 * CHUNK + s, :] = jnp.zeros((CHUNK, LANES), dst_ref.dtype)


def _attn_probs(sc, slope, snk, dist, valid):
    sc = jnp.where(valid, sc - slope * dist, NEG_INF)
    m = jnp.maximum(jnp.max(sc, axis=1, keepdims=True), snk)
    e = jnp.exp(sc - m)
    es = jnp.exp(snk - m)
    inv = 1.0 / (jnp.sum(e, axis=1, keepdims=True) + es)
    return e * inv, es * inv


def _stack_heads(x2, m0):
    parts = []
    for pr in range(2):
        xp = x2[:, pr * LANES:(pr + 1) * LANES]
        parts += [jnp.where(m0, xp, 0.0), jnp.where(m0, 0.0, xp)]
    return jnp.concatenate(parts, axis=0).astype(BF16)


def _unstack_pair(x_all, pr, m0):
    return jnp.where(m0, x_all[(2 * pr) * CHUNK:(2 * pr + 1) * CHUNK], x_all[(2 * pr + 1) * CHUNK:(2 * pr + 2) * CHUNK])


def _attn_saved_specs(bsz, n_blk):
    specs = [pl.BlockSpec((None, None, n_blk, 4 * CHUNK, w), lambda b, g: (b, g, 0, 0, 0)) for w in (3 * CHUNK, 1)]
    shapes = [jax.ShapeDtypeStruct((bsz, 2, n_blk, 4 * CHUNK, 3 * CHUNK), BF16),
              jax.ShapeDtypeStruct((bsz, 2, n_blk, 4 * CHUNK, 1), F32)]
    return specs, shapes


def _attn_fwd(u3, slopes, sink, rider=None):
    bsz, s, _ = u3.shape
    n_blk = s // CHUNK

    def body(slope_ref, sink_ref, q_ref, k_ref, v_ref, o_ref, p_ref, ps_ref, kp_ref, vp_ref):
        g = pl.program_id(1)
        _fill_padded(kp_ref, _dup_kv_head(k_ref[...], g), s)
        _fill_padded(vp_ref, _dup_kv_head(v_ref[...], g), s)
        m0 = lax.broadcasted_iota(jnp.int32, (CHUNK, LANES), 1) < HEAD_DIM

        def blk(n, carry):
            r0 = pl.multiple_of(n * CHUNK, CHUNK)
            kw = kp_ref[pl.ds(r0, 3 * CHUNK), :]
            vw = vp_ref[pl.ds(r0, 3 * CHUNK), :]
            dist, valid = _attn_window_tables(n, s)
            q_all = _stack_heads(q_ref[pl.ds(r0, CHUNK), :].astype(F32) * 0.125, m0)
            sc_all = _dot_nt(q_all, kw)
            probs, sinks = [], []
            for i in range(4):
                p, ps = _attn_probs(sc_all[i * CHUNK:(i + 1) * CHUNK], slope_ref[g * 4 + i], sink_ref[g * 4 + i],
                                    dist, valid)
                probs.append(p.astype(BF16))
                sinks.append(ps)
            p_all = jnp.concatenate(probs, axis=0)
            p_ref[n] = p_all
            ps_ref[n] = jnp.concatenate(sinks, axis=0)
            out_all = _dot(p_all, vw)
            for pr in range(2):
                o_ref[pl.ds(r0, CHUNK), pr * LANES:(pr + 1) * LANES] = _unstack_pair(out_all, pr, m0).astype(BF16)
            return carry

        lax.fori_loop(0, n_blk, blk, 0, unroll=4)

    q, k, v, grp, smem = _attn_specs(s)
    saved_specs, saved_shapes = _attn_saved_specs(bsz, n_blk)
    return _hosted_call(
        body, "attn_fwd", (bsz, 2),
        in_specs=[smem, smem, q, k, v],
        out_specs=[grp] + saved_specs,
        out_shape=[jax.ShapeDtypeStruct((bsz, s, ATTN_WIDTH), BF16)] + saved_shapes,
        scratch_shapes=[pltpu.VMEM((s + 2 * CHUNK, LANES), BF16), pltpu.VMEM((s + 2 * CHUNK, LANES), BF16)],
        operands=(slopes, sink, u3, u3, u3), rider=rider)


def _attn_bwd(u3, d_o, probs, sink_probs, rider=None):
    bsz, s, _ = u3.shape
    n_blk = s // CHUNK

    def body(q_ref, k_ref, v_ref, do_ref, p_ref, ps_ref, dq_ref, dkv_ref, ds_ref,
             kp_ref, vp_ref, dk_acc, dv_acc):
        g = pl.program_id(1)
        _fill_padded(kp_ref, _dup_kv_head(k_ref[...], g), s)
        _fill_padded(vp_ref, _dup_kv_head(v_ref[...], g), s)
        dk_acc[...] = jnp.zeros_like(dk_acc)
        dv_acc[...] = jnp.zeros_like(dv_acc)
        m0 = lax.broadcasted_iota(jnp.int32, (CHUNK, LANES), 1) < HEAD_DIM

        def blk(n, dsink):
            r0 = pl.multiple_of(n * CHUNK, CHUNK)
            win = pl.ds(r0, 3 * CHUNK)
            kw = kp_ref[win, :]
            vw = vp_ref[win, :]
            q_all = _stack_heads(q_ref[pl.ds(r0, CHUNK), :].astype(F32) * 0.125, m0)
            do_all = _stack_heads(do_ref[pl.ds(r0, CHUNK), :].astype(F32), m0)
            p_all = p_ref[n]
            ps_all = ps_ref[n]
            dp_all = _dot_nt(do_all, vw)
            new_dsink, dscs = [], []
            for i in range(4):
                rows = slice(i * CHUNK, (i + 1) * CHUNK)
                p = p_all[rows].astype(F32)
                dp = dp_all[rows]
                delta = jnp.sum(p * dp, axis=1, keepdims=True)
                dscs.append((p * (dp - delta)).astype(BF16))
                dsh = jnp.sum(ps_all[rows] * delta, axis=0, keepdims=True)
                new_dsink.append(dsink[i] - jnp.broadcast_to(dsh, (1, LANES)))
            dsc_all = jnp.concatenate(dscs, axis=0)
            dq_all = _dot(dsc_all, kw)
            dk_acc[win, :] += _dot_tn(dsc_all, q_all)
            dv_acc[win, :] += _dot_tn(p_all, do_all)
            for pr in range(2):
                dq_ref[pl.ds(r0, CHUNK), pr * LANES:(pr + 1) * LANES] = (
                    _unstack_pair(dq_all, pr, m0) * 0.125).astype(BF16)
            return tuple(new_dsink)

        dsink = _loop_grouped(n_blk, blk, tuple(jnp.zeros((1, LANES), F32) for _ in range(4)), per_trip=4)
        dk = dk_acc[CHUNK:CHUNK + s, :]
        dv = dv_acc[CHUNK:CHUNK + s, :]
        lane = lax.broadcasted_iota(jnp.int32, (s, LANES), 1)
        fold = lambda a: a + pltpu.roll(a, HEAD_DIM, 1)
        dkv_ref[...] = jnp.where(lane < HEAD_DIM, fold(dk), fold(dv)).astype(BF16)
        ds_ref[...] = jnp.zeros_like(ds_ref)
        for i in range(4):
            ds_ref[i:i + 1, :] = dsink[i]

    q, k, v, grp, _ = _attn_specs(s)
    return _hosted_call(
        body, "attn_bwd", (bsz, 2),
        in_specs=[q, k, v, grp] + _attn_saved_specs(bsz, n_blk)[0],
        out_specs=[grp, pl.BlockSpec((None, s, LANES), lambda b, g: (b, 0, g)),
                   pl.BlockSpec((None, None, 8, LANES), lambda b, g: (b, g, 0, 0))],
        out_shape=[jax.ShapeDtypeStruct((bsz, s, ATTN_WIDTH), BF16), jax.ShapeDtypeStruct((bsz, s, 2 * LANES), BF16),
                   jax.ShapeDtypeStruct((bsz, 2, 8, LANES), F32)],
        scratch_shapes=[pltpu.VMEM((s + 2 * CHUNK, LANES), BF16), pltpu.VMEM((s + 2 * CHUNK, LANES), BF16),
                        pltpu.VMEM((s + 2 * CHUNK, LANES), F32), pltpu.VMEM((s + 2 * CHUNK, LANES), F32)],
        operands=(u3, u3, u3, d_o, probs, sink_probs), rider=rider)


def _ffn_bwd(dz2, gs, us, pg, ple, zh1, r1, g1, wg4, wu4, wd4, wpg, w_out):
    t = dz2.shape[0]
    tm = 256
    wg_t, wu_t, wd_all = (w.reshape(FFN, D_MODEL) for w in (wg4, wu4, wd4))

    def body(dz_ref, gs_ref, us_ref, pg_ref, ple_ref, zh_ref, r_ref, g1_ref,
             wg_hbm, wu_hbm, wd_hbm, wpg_hbm, wo_hbm,
             dgs_ref, dus_ref, dsp_ref, dple_ref, dz1_ref, dyr_ref, dya_ref, dg1_ref, db1_ref,
             wg, wu, wd, wpg, wo, wsem):
        step = pl.program_id(0)
        loads = _resident_quarters(wd_hbm, wd) + _resident_quarters(wg_hbm, wg) + _resident_quarters(wu_hbm, wu)
        _load_resident(step, loads + [(wpg_hbm, wpg), (wo_hbm, wo)], wsem)

        @pl.when(step == 0)
        def _():
            dg1_ref[...] = jnp.zeros_like(dg1_ref)
            db1_ref[...] = jnp.zeros_like(db1_ref)

        dz = dz_ref[...]
        dzb = dz.astype(BF16)
        dh = ALPHA * dz
        pending = []
        chunks = [slice(n * FFN_CHUNK, (n + 1) * FFN_CHUNK) for n in range(N_FFN_CHUNK)]
        for n in range(N_FFN_CHUNK + 1):
            if n < N_FFN_CHUNK:
                da = _dot_nt(dzb, wd[chunks[n], :])
                gj = gs_ref[:, chunks[n]].astype(F32)
                uj = us_ref[:, chunks[n]].astype(F32)
                sg = _sigmoid(gj)
                dgj = (da * uj * sg * (1.0 + gj * (1.0 - sg))).astype(BF16)
                duj = (da * gj * sg).astype(BF16)
                dgs_ref[:, chunks[n]] = dgj
                dus_ref[:, chunks[n]] = duj
                pending.append((dgj, duj))
            if n > 0:
                dgp, dup = pending[n - 1]
                dh = dh + _dot(dgp, wg[chunks[n - 1], :]) + _dot(dup, wu[chunks[n - 1], :])
        pgv = pg_ref[...].astype(F32)
        plev = ple_ref[...].astype(F32)
        dple_ref[...] = (dz * pgv).astype(BF16)
        dsp = (dz * plev * pgv * (1.0 - pgv)).astype(BF16)
        dsp_ref[...] = dsp
        dh = dh + _dot_nt(dsp, wpg[...])
        zh = zh_ref[...]
        dg1_ref[...] += jnp.sum(dh * zh, axis=0, keepdims=True)
        db1_ref[...] += jnp.sum(dh, axis=0, keepdims=True)
        dzh = dh * g1_ref[...]
        m1 = jnp.mean(dzh, axis=1, keepdims=True)
        m2 = jnp.mean(dzh * zh, axis=1, keepdims=True)
        dz1 = r_ref[...] * (dzh - m1 - zh * m2)
        dz1_ref[...] = dz1
        dyc = _dot_nt(dz1.astype(BF16), wo[...])
        dyr_ref[...] = dyc[:, 0:RET_WIDTH].astype(BF16)
        dya_ref[...] = dyc[:, RET_WIDTH:].astype(BF16)

    row = lambda w: pl.BlockSpec((tm, w), lambda i: (i, 0))
    const = lambda s: pl.BlockSpec(s, lambda i: (0, 0))
    hbm = pl.BlockSpec(memory_space=pl.ANY)
    hid_shape = jax.ShapeDtypeStruct((t, FFN), BF16)
    return pl.pallas_call(
        body, name="ffn_bwd", grid=(t // tm,),
        in_specs=[row(D_MODEL), row(FFN), row(FFN), row(D_MODEL), row(D_MODEL), row(D_MODEL), row(1),
                  const((1, D_MODEL)), hbm, hbm, hbm, hbm, hbm],
        out_specs=[row(FFN), row(FFN), row(D_MODEL), row(D_MODEL), row(D_MODEL), row(RET_WIDTH), row(ATTN_WIDTH),
                   const((1, D_MODEL)), const((1, D_MODEL))],
        out_shape=[hid_shape, hid_shape, jax.ShapeDtypeStruct((t, D_MODEL), BF16),
                   jax.ShapeDtypeStruct((t, D_MODEL), BF16), jax.ShapeDtypeStruct((t, D_MODEL), F32),
                   jax.ShapeDtypeStruct((t, RET_WIDTH), BF16), jax.ShapeDtypeStruct((t, ATTN_WIDTH), BF16),
                   jax.ShapeDtypeStruct((1, D_MODEL), F32), jax.ShapeDtypeStruct((1, D_MODEL), F32)],
        scratch_shapes=[pltpu.VMEM((FFN, D_MODEL), BF16), pltpu.VMEM((FFN, D_MODEL), BF16),
                        pltpu.VMEM((FFN, D_MODEL), BF16),
                        pltpu.VMEM(wpg.shape, BF16), pltpu.VMEM(w_out.shape, BF16),
                        pltpu.SemaphoreType.DMA((3 * N_SHARD + 2,))],
        compiler_params=_params("arbitrary", vmem=VMEM_LIMIT),
    )(dz2, gs, us, pg, ple, zh1, r1, g1, wg_t, wu_t, wd_all, wpg, w_out)


def _wgrad_misc(y_ret, y_att, dz1, hb, dsp, p2d, dple, rider=None):
    t = dz1.shape[0]
    tk = min(t, 512)
    pc = D_MODEL // N_SHARD

    def body(yr_ref, ya_ref, dz_ref, hb_ref, dsp_ref, p_ref, dple_ref, wo_ref, wpg_ref, wpe_ref):
        @pl.when(pl.program_id(0) == 0)
        def _():
            wo_ref[...] = jnp.zeros_like(wo_ref)
            wpg_ref[...] = jnp.zeros_like(wpg_ref)
            wpe_ref[...] = jnp.zeros_like(wpe_ref)

        dzb = dz_ref[...].astype(BF16)
        wo_ref[0:RET_WIDTH, :] += _dot_tn(yr_ref[...], dzb)
        wo_ref[RET_WIDTH:, :] += _dot_tn(ya_ref[...], dzb)
        wpg_ref[...] += _dot_tn(hb_ref[...], dsp_ref[...])
        dpe = _dot_tn(p_ref[...].astype(BF16), dple_ref[...])
        for j in range(N_SHARD):
            wpe_ref[j] += dpe[:, j * pc:(j + 1) * pc]

    row = lambda w: pl.BlockSpec((tk, w), lambda k: (k, 0))
    const = lambda s: pl.BlockSpec(s, lambda k: (0,) * len(s))
    return _hosted_call(
        body, "wgrad_misc", (t // tk,),
        in_specs=[row(RET_WIDTH), row(ATTN_WIDTH), row(D_MODEL), row(D_MODEL), row(D_MODEL), row(PLE_DIM),
                  row(D_MODEL)],
        out_specs=[const((D_MODEL, D_MODEL)), const((D_MODEL, D_MODEL)), const((N_SHARD, PLE_DIM, pc))],
        out_shape=[jax.ShapeDtypeStruct((D_MODEL, D_MODEL), F32), jax.ShapeDtypeStruct((D_MODEL, D_MODEL), F32),
                   jax.ShapeDtypeStruct((N_SHARD, PLE_DIM, pc), F32)],
        scratch_shapes=[], operands=(y_ret, y_att, dz1, hb, dsp, p2d, dple), rider=rider, semantics=["arbitrary"])


def _wgrad_ffn(acts, dgs, dus, hb, dz2b):
    t = dz2b.shape[0]
    tk = min(t, 512)
    nk = t // tk

    def body(act_ref, dg_ref, du_ref, hb_ref, dz_ref, og_ref, ou_ref, od_ref):
        @pl.when(pl.program_id(1) == 0)
        def _():
            og_ref[...] = jnp.zeros_like(og_ref)
            ou_ref[...] = jnp.zeros_like(ou_ref)
            od_ref[...] = jnp.zeros_like(od_ref)

        hbv = hb_ref[...]
        og_ref[...] += _dot_tn(dg_ref[...], hbv)
        ou_ref[...] += _dot_tn(du_ref[...], hbv)
        od_ref[...] += _dot_tn(act_ref[...], dz_ref[...])

    half = FFN // 2
    a_spec = pl.BlockSpec((tk, half), lambda j, k: (k, j))
    b_spec = pl.BlockSpec((tk, D_MODEL), lambda j, k: (k, 0))
    o_spec = pl.BlockSpec((half, D_MODEL), lambda j, k: (j, 0))
    o_shape = jax.ShapeDtypeStruct((FFN, D_MODEL), F32)
    outs = pl.pallas_call(
        body, name="wgrad_ffn", grid=(2, nk),
        in_specs=[a_spec, a_spec, a_spec, b_spec, b_spec],
        out_specs=[o_spec] * 3, out_shape=[o_shape] * 3,
        compiler_params=_params("parallel", "arbitrary", vmem=VMEM_LIMIT),
    )(acts, dgs, dus, hb, dz2b)
    return [o.reshape(N_SHARD, FFN_SHARD, D_MODEL) for o in outs]


KV_ORDER = (0, 128, 64, 192)


def _wgrad_in(pieces, x2d, rider=None):
    t = x2d.shape[0]
    tk = min(t, 512)
    nk = t // tk
    kv0 = CB_AK * LANES

    def body(p0, p1, p2, p3, p4, pkv, x_ref, o_ref):
        @pl.when(pl.program_id(0) == 0)
        def _():
            o_ref[...] = jnp.zeros_like(o_ref)

        xb = x_ref[...].astype(BF16)
        for i, ref in enumerate((p0, p1, p2, p3, p4)):
            o_ref[i * 512:(i + 1) * 512, :] += _dot_tn(ref[...], xb)
        dkv = _dot_tn(pkv[...], xb)
        for i, o in enumerate(KV_ORDER):
            o_ref[kv0 + o:kv0 + o + HEAD_DIM, :] += dkv[i * HEAD_DIM:(i + 1) * HEAD_DIM]

    row = lambda w: pl.BlockSpec((tk, w), lambda k: (k, 0))
    return _hosted_call(
        body, "wgrad_in", (nk,),
        in_specs=[row(512)] * 5 + [row(256), row(D_MODEL)],
        out_specs=[pl.BlockSpec((IN_WIDTH, D_MODEL), lambda k: (0, 0))],
        out_shape=[jax.ShapeDtypeStruct((IN_WIDTH, D_MODEL), F32)],
        scratch_shapes=[], operands=(*pieces, x2d), rider=rider, semantics=["arbitrary"])


def _inproj_bwd(dz1, pieces, w_in_t, rider=None):
    t = dz1.shape[0]
    tm = 512
    kv0 = CB_AK * LANES

    def body(dz_ref, p0, p1, p2, p3, p4, pkv, w_ref, o_ref):
        acc = ALPHA * dz_ref[...]
        for i, ref in enumerate((p0, p1, p2, p3, p4)):
            acc = acc + _dot(ref[...], w_ref[i * 512:(i + 1) * 512, :])
        w_kv = jnp.concatenate([w_ref[kv0 + o:kv0 + o + HEAD_DIM, :] for o in KV_ORDER], axis=0)
        o_ref[...] = acc + _dot(pkv[...], w_kv)

    row = lambda w: pl.BlockSpec((tm, w), lambda i: (i, 0))
    return _hosted_call(
        body, "inproj_bwd", (t // tm,),
        in_specs=[row(D_MODEL)] + [row(512)] * 5 + [row(256), pl.BlockSpec((IN_WIDTH, D_MODEL), lambda i: (0, 0))],
        out_specs=[row(D_MODEL)],
        out_shape=[jax.ShapeDtypeStruct((t, D_MODEL), F32)],
        scratch_shapes=[], operands=(dz1, *pieces, w_in_t), rider=rider)


def _coords():
    return lax.axis_index("x"), lax.axis_index("y"), lax.axis_index("c")


def _chip_of(x, y, rel):
    return (1 - x if rel & 2 else x), (1 - y if rel & 1 else y)


def _gather_and_cast(shard, others):
    near = _gather_near_rider([shard])
    relay = _gather_relay_rider(near.out_shapes, chained=True)
    pass_near = _gather_pass_rider(near.out_shapes, chained=True, rels=NEAR)
    pass_far = _gather_pass_rider(near.out_shapes, chained=True, rels=(3,))
    riders = [near, relay, pass_near, pass_far]
    no = len(others)

    def body(*refs):
        shard_ref, wide = refs[0], refs[1:1 + no]
        out_ref, narrow = refs[1 + no], refs[2 + no:2 + 2 * no]
        k = 2 + 2 * no
        vin, vout, (lsem, ssem) = refs[k:k + no], refs[k + no:k + 2 * no], refs[k + 2 * no:k + 2 * no + 2]
        k += 2 * no + 2
        sems = {}
        for r in riders:
            sems[id(r)] = refs[k:k + len(r.sems)]
            k += len(r.sems)
        run = lambda r, method: getattr(r, method)([shard_ref], [out_ref], sems[id(r)])
        loads = [pltpu.make_async_copy(wide[w], vin[w], lsem.at[w]) for w in range(no)]
        stores = [pltpu.make_async_copy(vout[w], narrow[w], ssem.at[w]) for w in range(no)]

        run(near, "start")
        for cp in loads:
            cp.start()
        for w in range(no):
            loads[w].wait()
            vout[w][...] = vin[w][...].astype(BF16)
            stores[w].start()
        run(near, "finish")
        run(relay, "start")
        run(pass_near, "start")
        run(relay, "finish")
        run(pass_far, "start")
        run(pass_near, "finish")
        run(pass_far, "finish")
        for cp in stores:
            cp.wait()

    hbm = pl.BlockSpec(memory_space=pl.ANY)
    dma = pltpu.SemaphoreType.DMA
    gathered, *cast = pl.pallas_call(
        body, name="gather_weights", in_specs=[hbm] * (1 + no), out_specs=[hbm] * (1 + no),
        out_shape=near.out_shapes + [jax.ShapeDtypeStruct(a.shape, BF16) for a in others],
        scratch_shapes=[pltpu.VMEM(a.shape, F32) for a in others] + [pltpu.VMEM(a.shape, BF16) for a in others]
        + [dma((no,)), dma((no,))] + [s for r in riders for s in r.sems],
        compiler_params=_params(vmem=VMEM_LIMIT),
    )(shard, *others)
    return gathered, cast


def _gather_half(outs, w, chip, cc):
    h = outs[w].shape[1] // 2
    return outs[w].at[chip, pl.ds(cc * h, h), :]


NEAR = (1, 2)


def _gather_near_rider(shards, rels=NEAR):
    nw, nr = len(shards), len(rels)

    def copies(ins, outs, sems, arrivals):
        send, recv, lsend, lrecv = sems
        x, y, c = _coords()
        me = 2 * x + y
        own = [pltpu.make_async_remote_copy(
            src_ref=ins[w], dst_ref=outs[w].at[me], send_sem=lsend.at[w], recv_sem=lrecv.at[w],
            device_id=(x, y, 1 - c), device_id_type=MESH) for w in range(nw)]
        out, arrive = [], []
        for i, rel in enumerate(rels):
            kx, ky = _chip_of(x, y, rel)
            for w in range(nw):
                h = shards[w].shape[0] // 2
                sem = dict(send_sem=send.at[w * nr + i], recv_sem=recv.at[w * nr + i],
                           device_id=(kx, ky, c), device_id_type=MESH)
                out.append(pltpu.make_async_remote_copy(
                    src_ref=ins[w].at[pl.ds(c * h, h), :], dst_ref=_gather_half(outs, w, me, c), **sem))
                if arrivals:
                    theirs = _gather_half(outs, w, 2 * kx + ky, c)
                    arrive.append(pltpu.make_async_remote_copy(src_ref=theirs, dst_ref=theirs, **sem))
        return own, out, arrive

    def start(ins, outs, sems):
        own, out, _ = copies(ins, outs, sems, arrivals=False)
        for cp in own + out:
            cp.start()

    def finish(ins, outs, sems):
        own, out, arrive = copies(ins, outs, sems, arrivals=True)
        for cp in arrive:
            cp.wait_recv()
        for cp in out:
            cp.wait_send()
        for cp in own:
            cp.wait()

    dma = pltpu.SemaphoreType.DMA
    return _Rider(shards, [jax.ShapeDtypeStruct((N_SHARD,) + s.shape, s.dtype) for s in shards],
                  [dma((nr * nw,)), dma((nr * nw,)), dma((nw,)), dma((nw,))], start, finish)


def _gather_relay_rider(gathered, chained=False):
    nw = len(gathered)

    def quarter(outs, w, chip, c, p):
        q = outs[w].shape[1] // 4
        return outs[w].at[chip, pl.ds(c * 2 * q + p * q, q), :]

    def copies(outs, sems):
        send, recv = sems
        x, y, c = _coords()
        (yx, yy), (xx, xy), (dx, dy) = (_chip_of(x, y, rel) for rel in (1, 2, 3))
        out, arrive = [], []
        for w in range(nw):
            for p, (src_chip, dst) in enumerate(((2 * xx + xy, (yx, yy)), (2 * yx + yy, (xx, xy)))):
                rows = quarter(outs, w, src_chip, c, p)
                sem = dict(send_sem=send.at[w * 2 + p], recv_sem=recv.at[w * 2 + p], device_id_type=MESH)
                out.append(pltpu.make_async_remote_copy(src_ref=rows, dst_ref=rows, device_id=(*dst, c), **sem))
                mine = quarter(outs, w, 2 * dx + dy, c, p)
                arrive.append(pltpu.make_async_remote_copy(src_ref=mine, dst_ref=mine, device_id=(*dst, c), **sem))
        return out, arrive

    def start(ins, outs, sems):
        for cp in copies(outs, sems)[0]:
            cp.start()

    def finish(ins, outs, sems):
        out, arrive = copies(outs, sems)
        for cp in arrive:
            cp.wait_recv()
        for cp in out:
            cp.wait_send()

    dma = pltpu.SemaphoreType.DMA
    shapes = [jax.ShapeDtypeStruct(g.shape, g.dtype) for g in gathered]
    if chained:
        return _Rider([], [], [dma((2 * nw,)), dma((2 * nw,))], start, finish)
    return _Rider(gathered, shapes, [dma((2 * nw,)), dma((2 * nw,))], start, finish,
                  aliases={w: w for w in range(nw)})


def _gather_pass_rider(gathered, chained=False, rels=(1, 2, 3)):
    nw, nr = len(gathered), len(rels)

    def copies(outs, sems, cc):
        send, recv = sems
        x, y, c = _coords()
        res = []
        for i, rel in enumerate(rels):
            kx, ky = _chip_of(x, y, rel)
            for w in range(nw):
                rows = _gather_half(outs, w, 2 * kx + ky, cc)
                res.append(pltpu.make_async_remote_copy(
                    src_ref=rows, dst_ref=rows, send_sem=send.at[w * nr + i], recv_sem=recv.at[w * nr + i],
                    device_id=(x, y, 1 - c), device_id_type=MESH))
        return res

    def start(ins, outs, sems):
        for cp in copies(outs, sems, lax.axis_index("c")):
            cp.start()

    def finish(ins, outs, sems):
        c = lax.axis_index("c")
        for cp in copies(outs, sems, 1 - c):
            cp.wait_recv()
        for cp in copies(outs, sems, c):
            cp.wait_send()

    dma = pltpu.SemaphoreType.DMA
    shapes = [jax.ShapeDtypeStruct(g.shape, g.dtype) for g in gathered]
    if chained:
        return _Rider([], [], [dma((nr * nw,)), dma((nr * nw,))], start, finish)
    return _Rider(gathered, shapes, [dma((nr * nw,)), dma((nr * nw,))], start, finish,
                  aliases={w: w for w in range(nw)})


def _exchange_halves_rider(parts):
    nw = len(parts)

    def copies(ins, outs, sems):
        send, recv = sems
        x, y, c = _coords()
        res = []
        for w in range(nw):
            h = parts[w].shape[1] // 2
            res.append(pltpu.make_async_remote_copy(
                src_ref=ins[w].at[:, pl.ds((1 - c) * h, h), :], dst_ref=outs[w],
                send_sem=send.at[w], recv_sem=recv.at[w], device_id=(x, y, 1 - c), device_id_type=MESH))
        return res

    def start(ins, outs, sems):
        for cp in copies(ins, outs, sems):
            cp.start()

    def finish(ins, outs, sems):
        for cp in copies(ins, outs, sems):
            cp.wait()

    dma = pltpu.SemaphoreType.DMA
    return _Rider(parts, [jax.ShapeDtypeStruct((N_SHARD, p.shape[1] // 2, p.shape[2]), p.dtype) for p in parts],
                  [dma((nw,)), dma((nw,))], start, finish)


def _add_halves(parts, theirs, pos):
    nw = len(parts)
    split = 2

    def body(pos_ref, *refs):
        ins, oth = refs[:nw], refs[nw:2 * nw]
        o32, o16 = refs[2 * nw:3 * nw], refs[3 * nw:]
        sums = [ins[w][...] + oth[w][...] for w in range(nw)]
        for w in range(nw):
            o16[w][...] = sums[w].astype(BF16)

        @pl.when(pl.program_id(1) == pos_ref[0])
        def _():
            for w in range(nw):
                o32[w][...] = sums[w]

    in_specs, oth_specs, o32_specs, shapes32, shapes16 = [], [], [], [], []
    for p in parts:
        hb = p.shape[1] // 2 // split
        blk = (None, hb, p.shape[2])
        in_specs.append(pl.BlockSpec(blk, lambda i, j, pos_ref: (j, pos_ref[1] * split + i, 0)))
        oth_specs.append(pl.BlockSpec(blk, lambda i, j, pos_ref: (j, i, 0)))
        o32_specs.append(pl.BlockSpec((hb, p.shape[2]), lambda i, j, pos_ref: (i, 0)))
        shapes32.append(jax.ShapeDtypeStruct((p.shape[1] // 2, p.shape[2]), F32))
        shapes16.append(jax.ShapeDtypeStruct((N_SHARD, p.shape[1] // 2, p.shape[2]), BF16))
    return pl.pallas_call(
        body, name="add_halves",
        grid_spec=pltpu.PrefetchScalarGridSpec(
            num_scalar_prefetch=1, grid=(split, N_SHARD),
            in_specs=in_specs + oth_specs, out_specs=o32_specs + oth_specs),
        out_shape=shapes32 + shapes16,
        compiler_params=_params("parallel", "arbitrary", vmem=VMEM_LIMIT),
    )(pos, *parts, *theirs)


def _exchange_chips_rider(sums16, rows=None, into=None):
    nw = len(sums16)
    rows = rows or [(0, s.shape[1]) for s in sums16]
    held = [w for w in range(nw) if into is not None and into[w] is not None]

    def copies(ins, outs, sems):
        send, recv = sems
        x, y, c = _coords()
        res = []
        for rel in (1, 2, 3):
            kx, ky = _chip_of(x, y, rel)
            for w in range(nw):
                r0, n = rows[w]
                res.append(pltpu.make_async_remote_copy(
                    src_ref=ins[w].at[2 * kx + ky, pl.ds(r0, n), :], dst_ref=outs[w].at[rel - 1, pl.ds(r0, n), :],
                    send_sem=send.at[w * 3 + rel - 1], recv_sem=recv.at[w * 3 + rel - 1],
                    device_id=(kx, ky, c), device_id_type=MESH))
        return res

    def start(ins, outs, sems):
        for cp in copies(ins, outs, sems):
            cp.start()

    def finish(ins, outs, sems):
        for cp in copies(ins, outs, sems):
            cp.wait()

    dma = pltpu.SemaphoreType.DMA
    return _Rider(list(sums16) + [into[w] for w in held],
                  [jax.ShapeDtypeStruct((3,) + s.shape[1:], BF16) for s in sums16],
                  [dma((3 * nw,)), dma((3 * nw,))], start, finish, aliases={nw + i: w for i, w in enumerate(held)})


def _add_chips(sums32, theirs, pos):
    nw = len(sums32)
    split = 2
    hbs = [s.shape[0] // split for s in sums32]

    def body(pos_ref, *refs):
        ins, oth, outs, bufs = (refs[k * nw:(k + 1) * nw] for k in range(4))
        lsem, ssem, rsem = refs[4 * nw:]
        i = pl.program_id(0)
        x, y, c = _coords()

        def copies(w, j):
            rows = pl.ds(pl.multiple_of((pos_ref[1] * split + j) * hbs[w], 8), hbs[w])
            return (pltpu.make_async_copy(bufs[w].at[j], outs[w].at[rows, :], lsem.at[w, j]),
                    pltpu.make_async_remote_copy(
                        src_ref=bufs[w].at[j], dst_ref=outs[w].at[rows, :], send_sem=ssem.at[w, j],
                        recv_sem=rsem.at[w, j], device_id=(x, y, 1 - c), device_id_type=MESH))

        for w in range(nw):
            acc = ins[w][...]
            for r in range(3):
                acc = acc + oth[w][r].astype(F32)
            bufs[w][i] = acc
            for cp in copies(w, i):
                cp.start()

        @pl.when(i == split - 1)
        def _():
            for w in range(nw):
                for j in range(split):
                    local, remote = copies(w, j)
                    local.wait()
                    remote.wait()

    in_specs, oth_specs, shapes, scratch = [], [], [], []
    for s, hb in zip(sums32, hbs):
        in_specs.append(pl.BlockSpec((hb, s.shape[1]), lambda i, pos_ref: (i, 0)))
        oth_specs.append(pl.BlockSpec((3, hb, s.shape[1]), lambda i, pos_ref: (0, i, 0)))
        shapes.append(jax.ShapeDtypeStruct((2 * s.shape[0], s.shape[1]), F32))
        scratch.append(pltpu.VMEM((split, hb, s.shape[1]), F32))
    dma = pltpu.SemaphoreType.DMA
    return pl.pallas_call(
        body, name="add_chips",
        grid_spec=pltpu.PrefetchScalarGridSpec(
            num_scalar_prefetch=1, grid=(split,), in_specs=in_specs + oth_specs,
            out_specs=[pl.BlockSpec(memory_space=pl.ANY)] * nw,
            scratch_shapes=scratch + [dma((nw, split)), dma((nw, split)), dma((nw, split))]),
        out_shape=shapes,
        compiler_params=_params("arbitrary", vmem=VMEM_LIMIT),
    )(pos, *sums32, *theirs)


def _adamw_math(w, g, m, v):
    m = ADAM_B1 * m + (1.0 - ADAM_B1) * g
    v = ADAM_B2 * v + (1.0 - ADAM_B2) * (g * g)
    m_hat = m / (1.0 - ADAM_B1 ** ADAM_STEP)
    v_hat = v / (1.0 - ADAM_B2 ** ADAM_STEP)
    delta = -ADAM_LR * (m_hat / (jnp.sqrt(v_hat) + ADAM_EPS) + ADAM_WD * w)
    return delta, m, v


def _adamw(ws, gs, ms, vs):
    nw = len(ws)
    split = 8

    def body(*refs):
        w_r, g_r, m_r, v_r = (refs[i * nw:(i + 1) * nw] for i in range(4))
        g_o, d_o, m_o, v_o = (refs[(4 + i) * nw:(5 + i) * nw] for i in range(4))
        for k in range(nw):
            g = g_r[k][...]
            d, m, v = _adamw_math(w_r[k][...], g, m_r[k][...], v_r[k][...])
            g_o[k][...] = g
            d_o[k][...] = d
            m_o[k][...] = m
            v_o[k][...] = v

    specs = [pl.BlockSpec((w.shape[0] // split, w.shape[1]), lambda i: (i, 0)) for w in ws]
    shapes = [jax.ShapeDtypeStruct(w.shape, F32) for w in ws]
    outs = pl.pallas_call(
        body, name="adamw", grid=(split,),
        in_specs=specs * 4, out_specs=specs * 4, out_shape=shapes * 4,
        compiler_params=_params("parallel", vmem=VMEM_LIMIT),
    )(*ws, *gs, *ms, *vs)
    return outs[:nw], outs[nw:2 * nw], outs[2 * nw:3 * nw], outs[3 * nw:]


SMALL_ROWS = 8
SMALL_COLS = D_MODEL
LOSS_COL = RET_WIDTH + 24


def _small_allreduce_adamw(part, w, m, v, rider=None):
    def body(part_ref, w_ref, m_ref, v_ref, g_out, d_out, m_out, v_out, all_ref, send, recv):
        x, y, c = _coords()
        me = 4 * x + 2 * y + c
        all_ref[me] = part_ref[...]
        copies = []
        for rel in range(1, 8):
            px = 1 - x if rel & 4 else x
            py = 1 - y if rel & 2 else y
            pc = 1 - c if rel & 1 else c
            copies.append(pltpu.make_async_remote_copy(
                src_ref=part_ref, dst_ref=all_ref.at[me],
                send_sem=send.at[rel - 1], recv_sem=recv.at[rel - 1], device_id=(px, py, pc), device_id_type=MESH))
        for cp in copies:
            cp.start()
        for cp in copies:
            cp.wait()
        g = all_ref[0]
        for k in range(1, 8):
            g = g + all_ref[k]
        d, mn, vn = _adamw_math(w_ref[...], g, m_ref[...], v_ref[...])
        g_out[...] = g
        d_out[...] = d
        m_out[...] = mn
        v_out[...] = vn

    vm = pl.BlockSpec(memory_space=pltpu.VMEM)
    shape = jax.ShapeDtypeStruct((SMALL_ROWS, SMALL_COLS), F32)
    return _hosted_call(
        body, "small_allreduce_adamw", (1,),
        in_specs=[vm] * 4, out_specs=[vm] * 4, out_shape=[shape] * 4,
        scratch_shapes=[pltpu.VMEM((8, SMALL_ROWS, SMALL_COLS), F32),
                        pltpu.SemaphoreType.DMA((7,)), pltpu.SemaphoreType.DMA((7,))],
        operands=(part, w, m, v), rider=rider, semantics=["arbitrary"])


SMALL_NAMES = ("ret_decay_fwd", "ret_decay_bwd", "attn_sink", "ret_gn_gain",
               "ln1_gain", "ln1_bias", "ln2_gain", "ln2_bias")


LN_NAMES = ("ln1_gain", "ln1_bias", "ln2_gain", "ln2_bias")


def _pack_small(vals, extra=None):
    tail = jnp.zeros((1, 1), F32) if extra is None else extra.reshape(1, 1)
    row4 = jnp.concatenate([vals["ret_gn_gain"], vals["ret_decay_fwd"], vals["ret_decay_bwd"], vals["attn_sink"],
                            tail, jnp.zeros((1, SMALL_COLS - LOSS_COL - 1), F32)], axis=1)
    rows = [vals[n] for n in LN_NAMES] + [row4, jnp.zeros((SMALL_ROWS - 5, SMALL_COLS), F32)]
    return jnp.concatenate(rows, axis=0)


def _unpack_small(packed):
    o = RET_WIDTH
    where = [(n, i, 0, SMALL_COLS) for i, n in enumerate(LN_NAMES)] + [
        ("ret_gn_gain", 4, 0, o), ("ret_decay_fwd", 4, o, 8), ("ret_decay_bwd", 4, o + 8, 8),
        ("attn_sink", 4, o + 16, 8)]
    na, k = len(packed), len(where)

    def body(*refs):
        for a in range(na):
            rows = refs[a][...]
            for b, (_, row, col, n) in enumerate(where):
                refs[na + a * k + b][...] = rows[row:row + 1, col:col + n]
            if a == 0:
                refs[na + na * k][...] = rows[4:5, LOSS_COL:LOSS_COL + 1] * (0.5 / D_MODEL)

    vmem = pl.BlockSpec(memory_space=pltpu.VMEM)
    outs = pl.pallas_call(
        body, name="unpack_small", in_specs=[vmem] * na, out_specs=[vmem] * (na * k + 1),
        out_shape=[jax.ShapeDtypeStruct((1, n), F32) for _ in range(na) for (_, _, _, n) in where]
        + [jax.ShapeDtypeStruct((1, 1), F32)])(*packed)
    return [{where[b][0]: outs[a * k + b] for b in range(k)} for a in range(na)], outs[na * k]


def _local_step(x, p, tgt, w_in_t, rest, small, pos=None, small_state=None):
    bsz, s, _ = x.shape
    t = bsz * s
    x2d = x.reshape(t, D_MODEL)
    p2d = p.reshape(t, PLE_DIM)
    tgt2d = tgt.reshape(t, D_MODEL)
    dec_f = small["ret_decay_fwd"].reshape(8)
    dec_b = small["ret_decay_bwd"].reshape(8)
    lg_f = jnp.log1p(-jnp.exp2(dec_f))
    lg_b = jnp.log1p(-jnp.exp2(dec_b))
    per_lane = lambda v: jnp.repeat(v, HEAD_DIM).reshape(4, 1, LANES)
    lgf_l, lgb_l = per_lane(lg_f), per_lane(lg_b)
    sink = small["attn_sink"].reshape(8)
    slopes = 2.0 ** (-(jnp.arange(8, dtype=F32) + 1.0))
    gn_gain = small["ret_gn_gain"]
    g1, b1, g2, b2 = (small[n] for n in ("ln1_gain", "ln1_bias", "ln2_gain", "ln2_bias"))

    dist = pos is not None
    shard = dict(zip(REST_NAMES, rest)) if dist else {}
    near = lambda names, rels=NEAR: _gather_near_rider([shard[n] for n in names], rels)
    wave1, wave2, wave3 = ("w_out", "w_ple_gate", "w_ffn_gate"), ("w_ffn_up", "w_ple_proj"), ("w_ffn_down",)
    n1 = len(wave1)
    u, *o1 = _inproj(x2d, w_in_t, rider=near(wave1) if dist else None)
    u3 = u.reshape(bsz, s, IN_WIDTH)
    y_hat, y_rstd, y_ret, ret_rb, ret_kvf, *o2 = _ret_fwd(u3, lgf_l, lgb_l, gn_gain, rider=_merge_riders(
        [_gather_relay_rider(o1), near(wave2)]) if dist else None)
    y_att, att_p, att_ps, *o3 = _attn_fwd(u3, slopes, sink, rider=_merge_riders(
        [_gather_pass_rider(o2[:n1]), _gather_relay_rider(o2[n1:]), near(wave3, (1, 2, 3))]) if dist else None)
    gathered = dict(zip(wave1, o3[:n1]))
    w_out = _assemble_weights({"w_out": gathered["w_out"]})["w_out"] if dist else rest["w_out"]
    zh1, r1, hb, *o4 = _outproj_ln1(y_ret.reshape(t, RET_WIDTH), y_att.reshape(t, ATTN_WIDTH), x2d, w_out, g1, b1,
                                    rider=_gather_pass_rider(o3[n1:]) if dist else None)
    gathered.update(zip(wave2 + wave3, o4))
    wts = _assemble_weights(gathered) if dist else rest
    dz2, dz2b, gs, us, acts, pg, ple, sq, dg2, db2 = _ffn_fwd(
        zh1, hb, p2d, tgt2d, g1, b1, g2, b2, wts["gate4"], wts["up4"], wts["down4"], wts["ple_proj"], wts["ple_gate"])
    dgs, dus, dsp, dple, dz1, dyr, dya, dg1, db1 = _ffn_bwd(dz2, gs, us, pg, ple, zh1, r1, g1, wts["gate4"],
                                                          wts["up4"], wts["down4"], wts["ple_gate"], wts["w_out"])
    ffn_parts = list(_wgrad_ffn(acts, dgs, dus, hb, dz2b))
    d_w_out, d_ple_gate, d_ple_proj, *th_ffn = _wgrad_misc(
        y_ret.reshape(t, RET_WIDTH), y_att.reshape(t, ATTN_WIDTH), dz1, hb, dsp, p2d, dple,
        rider=_exchange_halves_rider(ffn_parts[:2]) if dist else None)
    misc_parts = [d_w_out.reshape(N_SHARD, D_MODEL // N_SHARD, D_MODEL), d_ple_proj,
                  d_ple_gate.reshape(N_SHARD, D_MODEL // N_SHARD, D_MODEL)]
    dyr3, dya3 = dyr.reshape(bsz, s, RET_WIDTH), dya.reshape(bsz, s, ATTN_WIDTH)
    if dist:
        s_gu = _add_halves(ffn_parts[:2], th_ffn, pos)
        half = FFN_SHARD // 2
        quarter = half // 2
        later_parts = [ffn_parts[2]] + misc_parts
        drq, drk, drv, drg, rpart, *o5 = _ret_bwd(u3, y_hat, y_rstd, (ret_rb, ret_kvf), dyr3, lgf_l, lgb_l, gn_gain,
                                                  rider=_merge_riders(
            [_exchange_chips_rider(s_gu[2:], rows=[(0, half), (0, quarter)]), _exchange_halves_rider(later_parts)]))
        s_dm = _add_halves(later_parts, o5[2:], pos)
        daq, dakv, spart, *o6 = _attn_bwd(u3, dya3, att_p, att_ps, rider=_exchange_chips_rider(
            [s_gu[3], s_dm[4]], rows=[(quarter, half - quarter), (0, half)], into=[o5[1], None]))
    else:
        drq, drk, drv, drg, rpart = _ret_bwd(u3, y_hat, y_rstd, (ret_rb, ret_kvf), dyr3, lgf_l, lgb_l, gn_gain)
        daq, dakv, spart = _attn_bwd(u3, dya3, att_p, att_ps)
    pieces = [a.reshape(t, -1) for a in (drq, drk, drv, drg, daq, dakv)]
    d_in, *o7 = _wgrad_in(pieces, x2d, rider=_exchange_chips_rider(list(s_dm[5:])) if dist else None)
    d_in = d_in.reshape(N_SHARD, FFN_SHARD, D_MODEL)

    rsum = rpart
    lane_heads = lambda row: jnp.sum(row.reshape(4, 2, HEAD_DIM), axis=-1).reshape(8)
    dlg_f = lane_heads(rsum[:, 0, :]) + jnp.stack([jnp.sum(rsum[:, 2, :], -1), jnp.sum(rsum[:, 3, :], -1)], 1).reshape(8)
    dlg_b = lane_heads(rsum[:, 1, :]) + jnp.stack([jnp.sum(rsum[:, 4, :], -1), jnp.sum(rsum[:, 5, :], -1)], 1).reshape(8)
    chain = lambda d: -(math.log(2.0) * jnp.exp2(d)) / (1.0 - jnp.exp2(d))
    grads_small = {
        "ret_decay_fwd": (dlg_f * chain(dec_f)).reshape(1, 8),
        "ret_decay_bwd": (dlg_b * chain(dec_b)).reshape(1, 8),
        "attn_sink": jnp.sum(spart, axis=0)[:, 0:4, 0].reshape(1, 8),
        "ret_gn_gain": rsum[:, 6, :].reshape(1, RET_WIDTH),
        "ln1_gain": dg1, "ln1_bias": db1, "ln2_gain": dg2, "ln2_bias": db2,
    }
    if not dist:
        grad_x, = _inproj_bwd(dz1, pieces, w_in_t)
        grads_rest = [misc_parts[0]] + ffn_parts + misc_parts[1:]
        return sq[0, 0], grad_x.reshape(bsz, s, D_MODEL), d_in, grads_rest, grads_small
    *small_out, th_in = _small_allreduce_adamw(_pack_small(grads_small, sq[0, 0]), *small_state,
                                               rider=_exchange_halves_rider([d_in]))
    s_in = _add_halves([d_in], [th_in], pos)
    grad_x, chips_in = _inproj_bwd(dz1, pieces, w_in_t, rider=_exchange_chips_rider([s_in[1]]))
    sums32 = [s_in[0], s_dm[1], s_gu[0], s_gu[1], s_dm[0], s_dm[2], s_dm[3]]
    from_chips = [chips_in, o7[0], o5[0], o6[0], o6[1], o7[1], o7[2]]
    return grad_x.reshape(bsz, s, D_MODEL), sums32, from_chips, small_out


BIG_NAMES = ("w_in", "w_out", "w_ffn_gate", "w_ffn_up", "w_ffn_down", "w_ple_proj", "w_ple_gate")
REST_NAMES = BIG_NAMES[1:]
TRANSPOSED = ("w_in", "w_ffn_gate", "w_ffn_up")
WEIGHT_ORDER = ("w_in", "ret_decay_fwd", "ret_decay_bwd", "ret_gn_gain", "attn_sink", "w_out", "ln1_gain",
                "ln1_bias", "w_ffn_gate", "w_ffn_up", "w_ffn_down", "w_ple_proj", "w_ple_gate", "ln2_gain", "ln2_bias")


def _shard_rows(name, a):
    return jnp.swapaxes(a[0], 0, 1) if name in TRANSPOSED else a[0]


def _unshard_rows(name, a):
    return (jnp.swapaxes(a, 0, 1) if name in TRANSPOSED else a)[None]


def _assemble_weights(gathered):
    rows = lambda a: a.reshape(N_SHARD * a.shape[1], a.shape[2])
    same = lambda a: a
    layout = {"w_out": ("w_out", rows), "w_ffn_gate": ("gate4", same), "w_ffn_up": ("up4", same),
              "w_ffn_down": ("down4", same), "w_ple_proj": ("ple_proj", same), "w_ple_gate": ("ple_gate", rows)}
    return {layout[n][0]: layout[n][1](a) for n, a in gathered.items()}


def kernel(x, p, w_in, ret_decay_fwd, ret_decay_bwd, ret_gn_gain, attn_sink, w_out, ln1_gain, ln1_bias, w_ffn_gate, w_ffn_up, w_ffn_down, w_ple_proj, w_ple_gate, ln2_gain, ln2_bias, loss_target, m_w_in, m_ret_decay_fwd, m_ret_decay_bwd, m_ret_gn_gain, m_attn_sink, m_w_out, m_ln1_gain, m_ln1_bias, m_w_ffn_gate, m_w_ffn_up, m_w_ffn_down, m_w_ple_proj, m_w_ple_gate, m_ln2_gain, m_ln2_bias, v_w_in, v_ret_decay_fwd, v_ret_decay_bwd, v_ret_gn_gain, v_attn_sink, v_w_out, v_ln1_gain, v_ln1_bias, v_w_ffn_gate, v_w_ffn_up, v_w_ffn_down, v_w_ple_proj, v_w_ple_gate, v_ln2_gain, v_ln2_bias):
    w = dict(w_in=w_in, ret_decay_fwd=ret_decay_fwd, ret_decay_bwd=ret_decay_bwd, ret_gn_gain=ret_gn_gain,
             attn_sink=attn_sink, w_out=w_out, ln1_gain=ln1_gain, ln1_bias=ln1_bias, w_ffn_gate=w_ffn_gate,
             w_ffn_up=w_ffn_up, w_ffn_down=w_ffn_down, w_ple_proj=w_ple_proj, w_ple_gate=w_ple_gate,
             ln2_gain=ln2_gain, ln2_bias=ln2_bias)
    m = dict(w_in=m_w_in, ret_decay_fwd=m_ret_decay_fwd, ret_decay_bwd=m_ret_decay_bwd, ret_gn_gain=m_ret_gn_gain,
             attn_sink=m_attn_sink, w_out=m_w_out, ln1_gain=m_ln1_gain, ln1_bias=m_ln1_bias, w_ffn_gate=m_w_ffn_gate,
             w_ffn_up=m_w_ffn_up, w_ffn_down=m_w_ffn_down, w_ple_proj=m_w_ple_proj, w_ple_gate=m_w_ple_gate,
             ln2_gain=m_ln2_gain, ln2_bias=m_ln2_bias)
    v = dict(w_in=v_w_in, ret_decay_fwd=v_ret_decay_fwd, ret_decay_bwd=v_ret_decay_bwd, ret_gn_gain=v_ret_gn_gain,
             attn_sink=v_attn_sink, w_out=v_w_out, ln1_gain=v_ln1_gain, ln1_bias=v_ln1_bias, w_ffn_gate=v_w_ffn_gate,
             w_ffn_up=v_w_ffn_up, w_ffn_down=v_w_ffn_down, w_ple_proj=v_w_ple_proj, w_ple_gate=v_w_ple_gate,
             ln2_gain=v_ln2_gain, ln2_bias=v_ln2_bias)
    big = lambda d: [_shard_rows(n, d[n]) for n in BIG_NAMES]
    small = lambda d: {n: d[n] for n in SMALL_NAMES}

    chip = 2 * lax.axis_index("x") + lax.axis_index("y")
    pos = jnp.stack([chip, lax.axis_index("c")]).astype(jnp.int32)

    shards = big(w)
    w_in4, rest16 = _gather_and_cast(shards[0].astype(BF16), shards[1:])
    w_in_t = w_in4.reshape(IN_WIDTH, D_MODEL)
    grad_x, sums32, from_chips, (g_s, d_s, m_s, v_s) = _local_step(
        x, p[0], loss_target, w_in_t, rest16, small(w), pos=pos,
        small_state=(_pack_small(small(w)), _pack_small(small(m)), _pack_small(small(v))))
    g_big, d_big, m_big, v_big = _adamw(big(w), _add_chips(sums32, from_chips, pos), big(m), big(v))

    def tree(bigs, smalls):
        out = {n: _unshard_rows(n, a) for n, a in zip(BIG_NAMES, bigs)}
        out.update(smalls)
        return [out[n] for n in WEIGHT_ORDER]

    smalls, loss = _unpack_small([g_s, d_s, m_s, v_s])
    return (loss.reshape(()), grad_x,
            *(a for bigs, s in zip((g_big, d_big, m_big, v_big), smalls) for a in tree(bigs, s)))
```

```python
import functools
import math

import jax
import jax.numpy as jnp
from jax import lax
from jax.experimental import pallas as pl
from jax.experimental.pallas import tpu as pltpu

F32 = jnp.float32
BF16 = jnp.bfloat16

D_MODEL = 1024
HEAD_DIM = 64
RET_HEADS = 8
ATTN_HEADS = 8
RET_WIDTH = 512
ATTN_WIDTH = 512
KV_WIDTH = 128
IN_WIDTH = 2816
FFN = 2816
N_SHARD = 4
FFN_SHARD = FFN // N_SHARD
PLE_DIM = 256
CHUNK = 128
LANES = 128
ALPHA = 2.0 ** 0.25
LN_EPS = 1e-5
GN_EPS = 1e-5
NEG_INF = -1e30
ADAM_LR = 0.001
ADAM_B1 = 0.9
ADAM_B2 = 0.999
ADAM_EPS = 1e-08
ADAM_WD = 0.01
ADAM_STEP = 10
VMEM_LIMIT = 56 * 1024 * 1024
MESH = pl.DeviceIdType.MESH

CB_RQ, CB_RK, CB_RV, CB_RG, CB_AQ, CB_AK, CB_AV = 0, 4, 8, 12, 16, 20, 21


def _dot(a, b):
    return jnp.dot(a, b, preferred_element_type=F32)


def _dot_nt(a, b):
    return lax.dot_general(a, b, (((1,), (1,)), ((), ())), preferred_element_type=F32)


def _dot_tn(a, b):
    return lax.dot_general(a, b, (((0,), (0,)), ((), ())), preferred_element_type=F32)


def _sigmoid(x):
    return 1.0 / (1.0 + jnp.exp(-x))


def _params(*sem, vmem=None):
    return pltpu.CompilerParams(dimension_semantics=tuple(sem) if sem else None, vmem_limit_bytes=vmem)


class _Rider:
    def __init__(self, ins, out_shapes, sems, start, finish, aliases=None):
        self.ins, self.out_shapes, self.sems = list(ins), list(out_shapes), list(sems)
        self.start, self.finish, self.aliases = start, finish, dict(aliases or {})


def _merge_riders(riders):
    riders = [r for r in riders if r is not None]
    if len(riders) == 1:
        return riders[0]
    bounds, aliases = [], {}
    i0 = o0 = s0 = 0
    for r in riders:
        bounds.append((i0, o0, s0))
        aliases.update({i0 + i: o0 + o for i, o in r.aliases.items()})
        i0, o0, s0 = i0 + len(r.ins), o0 + len(r.out_shapes), s0 + len(r.sems)

    def each(method):
        def run(ins, outs, sems):
            for r, (i, o, s) in zip(riders, bounds):
                getattr(r, method)(ins[i:i + len(r.ins)], outs[o:o + len(r.out_shapes)], sems[s:s + len(r.sems)])
        return run

    return _Rider([a for r in riders for a in r.ins], [a for r in riders for a in r.out_shapes],
                  [a for r in riders for a in r.sems], each("start"), each("finish"), aliases)


def _hosted_call(body, name, grid, in_specs, out_specs, out_shape, scratch_shapes, operands, rider=None,
                 semantics=None):
    n_in, n_out, n_scr = len(in_specs), len(out_specs), len(scratch_shapes)
    if rider is None:
        return pl.pallas_call(
            body, name=name, grid=grid, in_specs=in_specs, out_specs=out_specs, out_shape=out_shape,
            scratch_shapes=scratch_shapes,
            compiler_params=_params(*(semantics or ["parallel"] * len(grid)), vmem=VMEM_LIMIT))(*operands)
    r_in, r_out = len(rider.ins), len(rider.out_shapes)

    def full_body(*refs):
        main_in, rin = refs[:n_in], refs[n_in:n_in + r_in]
        o0 = n_in + r_in
        main_out, rout = refs[o0:o0 + n_out], refs[o0 + n_out:o0 + n_out + r_out]
        s0 = o0 + n_out + r_out
        main_scr, rsem = refs[s0:s0 + n_scr], refs[s0 + n_scr:]
        first = functools.reduce(jnp.logical_and, [pl.program_id(a) == 0 for a in range(len(grid))])
        last = functools.reduce(jnp.logical_and, [pl.program_id(a) == g - 1 for a, g in enumerate(grid)])

        @pl.when(first)
        def _():
            rider.start(rin, rout, rsem)

        body(*main_in, *main_out, *main_scr)

        @pl.when(last)
        def _():
            rider.finish(rin, rout, rsem)

    hbm = pl.BlockSpec(memory_space=pl.ANY)
    return pl.pallas_call(
        full_body, name=name, grid=grid,
        in_specs=list(in_specs) + [hbm] * r_in, out_specs=list(out_specs) + [hbm] * r_out,
        out_shape=list(out_shape) + rider.out_shapes,
        scratch_shapes=list(scratch_shapes) + rider.sems,
        input_output_aliases={n_in + i: n_out + o for i, o in rider.aliases.items()},
        compiler_params=_params(*(["arbitrary"] * len(grid)), vmem=VMEM_LIMIT),
    )(*operands, *rider.ins)


def _loop_grouped(n, body, init, per_trip=2):
    if n % per_trip:
        return lax.fori_loop(0, n, body, init)

    def trip(i, c):
        for j in range(per_trip):
            c = body(per_trip * i + j, c)
        return c

    return lax.fori_loop(0, n // per_trip, trip, init)


def _head_mean(x, m0):
    s0 = jnp.sum(jnp.where(m0, x, 0.0), axis=1, keepdims=True)
    s1 = jnp.sum(jnp.where(m0, 0.0, x), axis=1, keepdims=True)
    return jnp.where(m0, s0, s1) * (1.0 / HEAD_DIM)


def _inproj(x2d, w_in_t, rider=None):
    t = x2d.shape[0]
    tm = 512
    nb = 256

    def body(x_ref, w_ref, o_ref):
        xb = x_ref[...].astype(BF16)
        for n in range(0, IN_WIDTH, nb):
            o_ref[:, n:n + nb] = _dot_nt(xb, w_ref[n:n + nb, :]).astype(BF16)

    return _hosted_call(
        body, "inproj", (t // tm,),
        in_specs=[pl.BlockSpec((tm, D_MODEL), lambda i: (i, 0)),
                  pl.BlockSpec((IN_WIDTH, D_MODEL), lambda i: (0, 0))],
        out_specs=[pl.BlockSpec((tm, IN_WIDTH), lambda i: (i, 0))],
        out_shape=[jax.ShapeDtypeStruct((t, IN_WIDTH), BF16)],
        scratch_shapes=[], operands=(x2d, w_in_t), rider=rider)


def _outproj_ln1(y_ret, y_att, x2d, w_out, gain, bias, rider=None):
    t = x2d.shape[0]
    tm = 512

    def body(yr_ref, ya_ref, x_ref, w_ref, g_ref, b_ref, zh_ref, r_ref, hb_ref):
        mix = _dot(yr_ref[...], w_ref[0:RET_WIDTH, :]) + _dot(ya_ref[...], w_ref[RET_WIDTH:, :])
        z = ALPHA * x_ref[...] + mix
        mu = jnp.mean(z, axis=1, keepdims=True)
        zc = z - mu
        var = jnp.mean(zc * zc, axis=1, keepdims=True)
        r = lax.rsqrt(var + LN_EPS)
        zh = zc * r
        zh_ref[...] = zh
        r_ref[...] = r
        hb_ref[...] = (zh * g_ref[...] + b_ref[...]).astype(BF16)

    row = lambda w: pl.BlockSpec((tm, w), lambda i: (i, 0))
    const = lambda s: pl.BlockSpec(s, lambda i: (0, 0))
    return _hosted_call(
        body, "outproj_ln1", (t // tm,),
        in_specs=[row(RET_WIDTH), row(ATTN_WIDTH), row(D_MODEL), const((D_MODEL, D_MODEL)),
                  const((1, D_MODEL)), const((1, D_MODEL))],
        out_specs=[row(D_MODEL), row(1), row(D_MODEL)],
        out_shape=[jax.ShapeDtypeStruct((t, D_MODEL), F32), jax.ShapeDtypeStruct((t, 1), F32),
                   jax.ShapeDtypeStruct((t, D_MODEL), BF16)],
        scratch_shapes=[], operands=(y_ret, y_att, x2d, w_out, gain, bias), rider=rider)


def _load_resident(step, pairs, sems):
    copies = [pltpu.make_async_copy(src, dst, sems.at[i]) for i, (src, dst) in enumerate(pairs)]

    @pl.when(step == 0)
    def _():
        for cp in copies:
            cp.start()
        for cp in copies:
            cp.wait()


FFN_CHUNK = 256
N_FFN_CHUNK = FFN // FFN_CHUNK


def _resident_quarters(hbm, vmem):
    q = FFN // N_SHARD
    return [(hbm.at[pl.ds(j * q, q), :], vmem.at[pl.ds(j * q, q), :]) for j in range(N_SHARD)]


def _ln2_loss_tail(zh, mixed, tgt, g1, b1, g2, b2):
    z2 = ALPHA * (zh * g1 + b1) + mixed
    mu = jnp.mean(z2, axis=1, keepdims=True)
    zc = z2 - mu
    var = jnp.mean(zc * zc, axis=1, keepdims=True)
    r = lax.rsqrt(var + LN_EPS)
    zh2 = zc * r
    err = zh2 * g2 + b2 - tgt
    dy = err * (1.0 / D_MODEL)
    dzh = dy * g2
    m1 = jnp.mean(dzh, axis=1, keepdims=True)
    m2 = jnp.mean(dzh * zh2, axis=1, keepdims=True)
    dz2 = r * (dzh - m1 - zh2 * m2)
    return dz2, jnp.sum(err * err), jnp.sum(dy * zh2, axis=0, keepdims=True), jnp.sum(dy, axis=0, keepdims=True)


def _ffn_fwd(zh1, hb, p2d, tgt, g1, b1, g2, b2, wg4, wu4, wd4, wpe, wpg):
    t = zh1.shape[0]
    tm = 256
    wg_t, wu_t, wd_all = (w.reshape(FFN, D_MODEL) for w in (wg4, wu4, wd4))

    def body(zh_ref, hb_ref, p_ref, t_ref, g1_ref, b1_ref, g2_ref, b2_ref,
             wg_hbm, wu_hbm, wd_hbm, wpe_hbm, wpg_hbm,
             dz_ref, dzb_ref, gs_ref, us_ref, act_ref, pg_ref, ple_ref, loss_ref, dg2_ref, db2_ref,
             wg, wu, wd, wpe, wpg, wsem):
        step = pl.program_id(0)
        loads = _resident_quarters(wg_hbm, wg) + _resident_quarters(wu_hbm, wu) + _resident_quarters(wd_hbm, wd)
        pc = D_MODEL // N_SHARD
        loads += [(wpe_hbm.at[j], wpe.at[:, pl.ds(j * pc, pc)]) for j in range(N_SHARD)]
        _load_resident(step, loads + [(wpg_hbm, wpg)], wsem)

        @pl.when(step == 0)
        def _():
            loss_ref[...] = jnp.zeros_like(loss_ref)
            dg2_ref[...] = jnp.zeros_like(dg2_ref)
            db2_ref[...] = jnp.zeros_like(db2_ref)

        hbv = hb_ref[...]
        ffn = jnp.zeros((tm, D_MODEL), F32)
        acts = []
        chunks = [slice(n * FFN_CHUNK, (n + 1) * FFN_CHUNK) for n in range(N_FFN_CHUNK)]
        for n in range(N_FFN_CHUNK + 1):
            if n < N_FFN_CHUNK:
                gj = _dot_nt(hbv, wg[chunks[n], :])
                uj = _dot_nt(hbv, wu[chunks[n], :])
                gs_ref[:, chunks[n]] = gj.astype(BF16)
                us_ref[:, chunks[n]] = uj.astype(BF16)
                acts.append((gj * _sigmoid(gj) * uj).astype(BF16))
                act_ref[:, chunks[n]] = acts[n]
            if n > 0:
                ffn = ffn + _dot(acts[n - 1], wd[chunks[n - 1], :])
        ple = _dot(p_ref[...].astype(BF16), wpe[...])
        pg = _sigmoid(_dot(hbv, wpg[...]))
        pg_ref[...] = pg.astype(BF16)
        ple_ref[...] = ple.astype(BF16)
        dz2, sq, dg2, db2 = _ln2_loss_tail(zh_ref[...], ffn + pg * ple, t_ref[...], g1_ref[...], b1_ref[...],
                                           g2_ref[...], b2_ref[...])
        dz_ref[...] = dz2
        dzb_ref[...] = dz2.astype(BF16)
        loss_ref[...] += sq
        dg2_ref[...] += dg2
        db2_ref[...] += db2

    row = lambda w: pl.BlockSpec((tm, w), lambda i: (i, 0))
    const = lambda s: pl.BlockSpec(s, lambda i: (0, 0))
    hid_shape = jax.ShapeDtypeStruct((t, FFN), BF16)
    hbm = pl.BlockSpec(memory_space=pl.ANY)
    return pl.pallas_call(
        body, name="ffn_fwd", grid=(t // tm,),
        in_specs=[row(D_MODEL), row(D_MODEL), row(PLE_DIM), row(D_MODEL),
                  const((1, D_MODEL)), const((1, D_MODEL)), const((1, D_MODEL)), const((1, D_MODEL)),
                  hbm, hbm, hbm, hbm, hbm],
        out_specs=[row(D_MODEL), row(D_MODEL), row(FFN), row(FFN), row(FFN), row(D_MODEL), row(D_MODEL),
                   const((8, LANES)), const((1, D_MODEL)), const((1, D_MODEL))],
        out_shape=[jax.ShapeDtypeStruct((t, D_MODEL), F32), jax.ShapeDtypeStruct((t, D_MODEL), BF16),
                   hid_shape, hid_shape, hid_shape,
                   jax.ShapeDtypeStruct((t, D_MODEL), BF16), jax.ShapeDtypeStruct((t, D_MODEL), BF16),
                   jax.ShapeDtypeStruct((8, LANES), F32),
                   jax.ShapeDtypeStruct((1, D_MODEL), F32), jax.ShapeDtypeStruct((1, D_MODEL), F32)],
        scratch_shapes=[pltpu.VMEM((FFN, D_MODEL), BF16), pltpu.VMEM((FFN, D_MODEL), BF16),
                        pltpu.VMEM((FFN, D_MODEL), BF16),
                        pltpu.VMEM((PLE_DIM, D_MODEL), BF16), pltpu.VMEM(wpg.shape, BF16),
                        pltpu.SemaphoreType.DMA((4 * N_SHARD + 1,))],
        compiler_params=_params("arbitrary", vmem=VMEM_LIMIT),
    )(zh1, hb, p2d, tgt, g1, b1, g2, b2, wg_t, wu_t, wd_all, wpe, wpg)


def _ret_tables(lgf, lgb):
    c = CHUNK
    row = lax.broadcasted_iota(jnp.int32, (c, LANES), 0).astype(F32)
    ii = lax.broadcasted_iota(jnp.int32, (c, c), 0).astype(F32)
    jj = lax.broadcasted_iota(jnp.int32, (c, c), 1).astype(F32)
    diff = ii - jj
    dmats = []
    for h in range(2):
        lf = lgf[:, h * HEAD_DIM:h * HEAD_DIM + 1]
        lb = lgb[:, h * HEAD_DIM:h * HEAD_DIM + 1]
        dmats.append(jnp.where(diff > 0, jnp.exp(lf * jnp.maximum(diff, 0.0)),
                               jnp.where(diff < 0, jnp.exp(lb * jnp.maximum(-diff, 0.0)), 2.0)))
    tab = dict(
        qdec_f=jnp.exp(lgf * (row + 1.0)), kdec_f=jnp.exp(lgf * (c - 1.0 - row)),
        qdec_b=jnp.exp(lgb * (c - row)), kdec_b=jnp.exp(lgb * row),
        cdec_f=jnp.exp(lgf * c), cdec_b=jnp.exp(lgb * c),
        d0=dmats[0], d1=dmats[1], row=row, diff=diff)
    r = lax.broadcasted_iota(jnp.int32, (LANES, LANES), 0) < HEAD_DIM
    cc = lax.broadcasted_iota(jnp.int32, (LANES, LANES), 1) < HEAD_DIM
    tab["bd"] = r == cc
    tab["m0"] = lax.broadcasted_iota(jnp.int32, (c, LANES), 1) < HEAD_DIM
    return tab


def _ret_specs(bsz, s):
    blk = lambda cb: pl.BlockSpec((bsz, s, LANES), lambda p, cb=cb: (0, 0, cb + p))
    lane = pl.BlockSpec((None, 1, LANES), lambda p: (p, 0, 0))
    gain = pl.BlockSpec((1, LANES), lambda p: (0, p))
    pair = pl.BlockSpec((bsz, s, LANES), lambda p: (0, 0, p))
    return blk, lane, gain, pair


def _ret_state_spec(bsz, n_chunk):
    spec = pl.BlockSpec((None, bsz, n_chunk, LANES, LANES), lambda p: (p, 0, 0, 0, 0))
    return spec, jax.ShapeDtypeStruct((4, bsz, n_chunk, LANES, LANES), F32)


def _ret_kv_states(tb, k_ref, v_ref, rb_ref, kvf_ref, n_chunk):
    c = CHUNK
    bsz = k_ref.shape[0]
    bd = tb["bd"]

    def contributions(n, carry):
        sl = pl.ds(pl.multiple_of(n * c, c), c)
        kfb = []
        for b in range(bsz):
            k32 = k_ref[b, sl, :].astype(F32)
            kfb.append(jnp.concatenate([k32 * tb["kdec_f"], k32 * tb["kdec_b"]], axis=1).astype(BF16))
        kvs = [_dot_tn(kfb[b], v_ref[b, sl, :]) for b in range(bsz)]
        for b in range(bsz):
            kvf_ref[b, n] = jnp.where(bd, kvs[b][0:LANES], 0.0)
            rb_ref[b, n] = jnp.where(bd, kvs[b][LANES:], 0.0)
        return carry

    lax.fori_loop(0, n_chunk, contributions, 0, unroll=2)

    def recur(i, rbs):
        n = n_chunk - 1 - i
        new = []
        for b in range(bsz):
            own = rb_ref[b, n]
            rb_ref[b, n] = rbs[b]
            new.append(rbs[b] * tb["cdec_b"] + own)
        return tuple(new)

    lax.fori_loop(0, n_chunk, recur, tuple(jnp.zeros((LANES, LANES), F32) for _ in range(bsz)))


def _split_rows(x, m0):
    return jnp.concatenate([jnp.where(m0, x, 0.0), jnp.where(m0, 0.0, x)], axis=0).astype(BF16)


def _ret_fwd(u3, lgf_l, lgb_l, gn_gain, rider=None):
    bsz, s, _ = u3.shape
    n_chunk = s // CHUNK
    c = CHUNK

    def body(q_ref, k_ref, v_ref, g_ref, lgf_ref, lgb_ref, gain_ref, yh_ref, rstd_ref, o_ref, rb_ref, kvf_ref):
        tb = _ret_tables(lgf_ref[...], lgb_ref[...])
        m0 = tb["m0"]
        gain = gain_ref[...]
        rows = range(bsz)
        _ret_kv_states(tb, k_ref, v_ref, rb_ref, kvf_ref, n_chunk)

        def chunk(n, rfs):
            sl = pl.ds(pl.multiple_of(n * c, c), c)
            qs = [q_ref[b, sl, :].astype(F32) * 0.125 for b in rows]
            s01 = [_dot_nt(_split_rows(qs[b], m0), k_ref[b, sl, :]) for b in rows]
            ys = []
            for b in rows:
                lhs = jnp.concatenate([s01[b][0:c] * tb["d0"], s01[b][c:] * tb["d1"],
                                       qs[b] * tb["qdec_f"], qs[b] * tb["qdec_b"]], axis=1).astype(BF16)
                rhs = jnp.concatenate([_split_rows(v_ref[b, sl, :].astype(F32), m0),
                                       rfs[b].astype(BF16), rb_ref[b, n].astype(BF16)], axis=0)
                ys.append(_dot(lhs, rhs))
            new = []
            for b in rows:
                y = ys[b]
                mu = _head_mean(y, m0)
                yc = y - mu
                rstd = lax.rsqrt(_head_mean(yc * yc, m0) + GN_EPS)
                yh = yc * rstd
                g = g_ref[b, sl, :].astype(F32)
                yh_ref[b, sl, :] = yh
                rstd_ref[b, sl, :] = rstd
                o_ref[b, sl, :] = (yh * gain * (g * _sigmoid(g))).astype(BF16)
                new.append(rfs[b] * tb["cdec_f"] + kvf_ref[b, n])
            return tuple(new)

        _loop_grouped(n_chunk, chunk, tuple(jnp.zeros((LANES, LANES), F32) for _ in rows))

    blk, lane, gain, pair = _ret_specs(bsz, s)
    state, state_shape = _ret_state_spec(bsz, n_chunk)
    return _hosted_call(
        body, "ret_fwd", (4,),
        in_specs=[blk(CB_RQ), blk(CB_RK), blk(CB_RV), blk(CB_RG), lane, lane, gain],
        out_specs=[pair, pair, pair, state, state],
        out_shape=[jax.ShapeDtypeStruct((bsz, s, RET_WIDTH), F32), jax.ShapeDtypeStruct((bsz, s, RET_WIDTH), F32),
                   jax.ShapeDtypeStruct((bsz, s, RET_WIDTH), BF16), state_shape, state_shape],
        scratch_shapes=[],
        operands=(u3, u3, u3, u3, lgf_l, lgb_l, gn_gain), rider=rider)


def _ret_bwd(u3, y_hat, y_rstd, states, d_o, lgf_l, lgb_l, gn_gain, rider=None):
    bsz, s, _ = u3.shape
    n_chunk = s // CHUNK
    c = CHUNK

    def body(q_ref, k_ref, v_ref, g_ref, yh_ref, rstd_ref, do_ref, lgf_ref, lgb_ref, gain_ref, rb_ref, kvf_ref,
             dq_ref, dk_ref, dv_ref, dg_ref, part_ref,
             rf_ref, dirf_ref, dy_ref, dk_acc, dv_acc, pa0, pa1, vec_ref):
        tb = _ret_tables(lgf_ref[...], lgb_ref[...])
        m0, bd, row = tb["m0"], tb["bd"], tb["row"]
        gain = gain_ref[...]
        wf = jnp.maximum(tb["diff"], 0.0)
        wb = jnp.maximum(-tb["diff"], 0.0)
        rows = range(bsz)
        zero_states = tuple(jnp.zeros((LANES, LANES), F32) for _ in rows)
        for ref in (pa0, pa1):
            ref[...] = jnp.zeros_like(ref)
        vec_ref[...] = jnp.zeros_like(vec_ref)

        def sweep_fwd(n, carry):
            rfs, gbs = carry
            sl = pl.ds(pl.multiple_of(n * c, c), c)
            qs, ks, vs, dys, dybs, q01, k01, dy01 = [], [], [], [], [], [], [], []
            dgain = jnp.zeros((1, LANES), F32)
            for b in rows:
                q = q_ref[b, sl, :].astype(F32) * 0.125
                k = k_ref[b, sl, :]
                yh = yh_ref[b, sl, :]
                rstd = rstd_ref[b, sl, :]
                do = do_ref[b, sl, :].astype(F32)
                g = g_ref[b, sl, :].astype(F32)
                sg = _sigmoid(g)
                sil = g * sg
                dyh = do * gain * sil
                dg_ref[b, sl, :] = (do * yh * gain * sg * (1.0 + g * (1.0 - sg))).astype(BF16)
                dgain = dgain + jnp.sum(do * yh * sil, axis=0, keepdims=True)
                dy = rstd * (dyh - _head_mean(dyh, m0) - yh * _head_mean(dyh * yh, m0))
                dyb = dy.astype(BF16)
                dy_ref[b, sl, :] = dyb
                rf_ref[b, n] = rfs[b]
                qs.append(q)
                ks.append(k)
                vs.append(v_ref[b, sl, :])
                dys.append(dy)
                dybs.append(dyb)
                q01.append(_split_rows(q, m0))
                k01.append(_split_rows(k.astype(F32), m0))
                dy01.append(_split_rows(dy, m0))
            s01 = [_dot_nt(q01[b], ks[b]) for b in rows]
            da01 = [_dot_nt(dy01[b], vs[b]) for b in rows]
            rbn = [rb_ref[b, n] for b in rows]
            states = [jnp.concatenate([rfs[b], rbn[b]], axis=0).astype(BF16) for b in rows]
            dqc = [_dot_nt(dybs[b], states[b]) for b in rows]
            gbb = [gbs[b].astype(BF16) for b in rows]
            dkb = [_dot_nt(vs[b], gbb[b]) for b in rows]
            qfb = [jnp.concatenate([qs[b] * tb["qdec_f"], qs[b] * tb["qdec_b"]], axis=1) for b in rows]
            direct = [_dot_tn(qfb[b].astype(BF16), dybs[b]) for b in rows]
            ds_cat, ds_rows, a_rows = [], [], []
            for b in rows:
                a0 = s01[b][0:c] * tb["d0"]
                a1 = s01[b][c:] * tb["d1"]
                pa0[...] += da01[b][0:c] * a0
                pa1[...] += da01[b][c:] * a1
                ds0 = da01[b][0:c] * tb["d0"]
                ds1 = da01[b][c:] * tb["d1"]
                ds_cat.append(jnp.concatenate([ds0, ds1], axis=1).astype(BF16))
                ds_rows.append(jnp.concatenate([ds0, ds1], axis=0).astype(BF16))
                a_rows.append(jnp.concatenate([a0, a1], axis=0).astype(BF16))
            kbd = [ks[b].astype(F32) * tb["kdec_b"] for b in rows]
            dq_in = [_dot(ds_cat[b], k01[b]) for b in rows]
            dk_in = [_dot_tn(ds_rows[b], q01[b]) for b in rows]
            dv_in = [_dot_tn(a_rows[b], dy01[b]) for b in rows]
            dv_gb = [_dot(kbd[b].astype(BF16), gbb[b]) for b in rows]
            new_rf, new_gb = [], []
            dlf = jnp.zeros((1, LANES), F32)
            dlb = jnp.zeros((1, LANES), F32)
            for b in rows:
                dqf, dqb = dqc[b][:, 0:LANES], dqc[b][:, LANES:]
                qf, qb = qfb[b][:, 0:LANES], qfb[b][:, LANES:]
                dq = dq_in[b] + dqf * tb["qdec_f"] + dqb * tb["qdec_b"]
                dq_ref[b, sl, :] = (dq * 0.125).astype(BF16)
                dk_acc[b, sl, :] = dk_in[b] + dkb[b] * tb["kdec_b"]
                dv_acc[b, sl, :] = dv_in[b] + dv_gb[b]
                dlf = dlf + jnp.sum((row + 1.0) * qf * dqf, axis=0, keepdims=True)
                dlb = dlb + jnp.sum((c - row) * qb * dqb + row * kbd[b] * dkb[b], axis=0, keepdims=True)
                dlb = dlb + c * tb["cdec_b"] * jnp.sum(gbs[b] * rbn[b], axis=0, keepdims=True)
                dirf_ref[b, n] = jnp.where(bd, direct[b][0:LANES], 0.0)
                new_gb.append(jnp.where(bd, direct[b][LANES:], 0.0) + tb["cdec_b"] * gbs[b])
                new_rf.append(rfs[b] * tb["cdec_f"] + kvf_ref[b, n])
            vec_ref[0:1, :] += dlf
            vec_ref[1:2, :] += dlb
            vec_ref[6:7, :] += dgain
            return tuple(new_rf), tuple(new_gb)

        _loop_grouped(n_chunk, sweep_fwd, (zero_states, zero_states), per_trip=4)

        def sweep_bwd(i, gfs):
            n = n_chunk - 1 - i
            sl = pl.ds(pl.multiple_of(n * c, c), c)
            gfb = [gfs[b].astype(BF16) for b in rows]
            kfd = [k_ref[b, sl, :].astype(F32) * tb["kdec_f"] for b in rows]
            dkf = [_dot_nt(v_ref[b, sl, :], gfb[b]) for b in rows]
            dvf = [_dot(kfd[b].astype(BF16), gfb[b]) for b in rows]
            new = []
            dlf = jnp.zeros((1, LANES), F32)
            for b in rows:
                dk_ref[b, sl, :] = (dk_acc[b, sl, :] + dkf[b] * tb["kdec_f"]).astype(BF16)
                dv_ref[b, sl, :] = (dv_acc[b, sl, :] + dvf[b]).astype(BF16)
                dlf = dlf + jnp.sum((c - 1.0 - row) * kfd[b] * dkf[b], axis=0, keepdims=True)
                dlf = dlf + c * tb["cdec_f"] * jnp.sum(gfs[b] * rf_ref[b, n], axis=0, keepdims=True)
                new.append(dirf_ref[b, n] + tb["cdec_f"] * gfs[b])
            vec_ref[0:1, :] += dlf
            return tuple(new)

        _loop_grouped(n_chunk, sweep_bwd, zero_states)
        vec_ref[2:3, :] = jnp.sum(pa0[...] * wf, axis=0, keepdims=True)
        vec_ref[3:4, :] = jnp.sum(pa1[...] * wf, axis=0, keepdims=True)
        vec_ref[4:5, :] = jnp.sum(pa0[...] * wb, axis=0, keepdims=True)
        vec_ref[5:6, :] = jnp.sum(pa1[...] * wb, axis=0, keepdims=True)
        part_ref[...] = vec_ref[...]

    blk, lane, gain, pair = _ret_specs(bsz, s)
    out_bf = jax.ShapeDtypeStruct((bsz, s, RET_WIDTH), BF16)
    state = pltpu.VMEM((bsz, n_chunk, LANES, LANES), F32)
    saved = _ret_state_spec(bsz, n_chunk)[0]
    return _hosted_call(
        body, "ret_bwd", (4,),
        in_specs=[blk(CB_RQ), blk(CB_RK), blk(CB_RV), blk(CB_RG), pair, pair, pair, lane, lane, gain, saved, saved],
        out_specs=[pair, pair, pair, pair, pl.BlockSpec((None, 8, LANES), lambda p: (p, 0, 0))],
        out_shape=[out_bf, out_bf, out_bf, out_bf, jax.ShapeDtypeStruct((4, 8, LANES), F32)],
        scratch_shapes=[state, state,
                        pltpu.VMEM((bsz, s, LANES), BF16), pltpu.VMEM((bsz, s, LANES), F32),
                        pltpu.VMEM((bsz, s, LANES), F32),
                        pltpu.VMEM((c, c), F32), pltpu.VMEM((c, c), F32), pltpu.VMEM((8, LANES), F32)],
        operands=(u3, u3, u3, u3, y_hat, y_rstd, d_o, lgf_l, lgb_l, gn_gain, *states), rider=rider)


def _attn_window_tables(n, s):
    qi = lax.broadcasted_iota(jnp.int32, (CHUNK, 3 * CHUNK), 0)
    kj = lax.broadcasted_iota(jnp.int32, (CHUNK, 3 * CHUNK), 1)
    dist = jnp.abs(kj - CHUNK - qi)
    kpos = n * CHUNK - CHUNK + kj
    valid = (dist <= CHUNK) & (kpos >= 0) & (kpos < s)
    return dist.astype(F32), valid


def _dup_kv_head(x, g):
    lane = lax.broadcasted_iota(jnp.int32, x.shape, 1)
    keep = (lane < HEAD_DIM) == (g == 0)
    xf = x.astype(F32)
    return jnp.where(keep, xf, pltpu.roll(xf, HEAD_DIM, 1))


def _attn_specs(s):
    q = pl.BlockSpec((None, s, 2 * LANES), lambda b, g: (b, 0, CB_AQ // 2 + g))
    k = pl.BlockSpec((None, s, LANES), lambda b, g: (b, 0, CB_AK))
    v = pl.BlockSpec((None, s, LANES), lambda b, g: (b, 0, CB_AV))
    grp = pl.BlockSpec((None, s, 2 * LANES), lambda b, g: (b, 0, g))
    smem = pl.BlockSpec(memory_space=pltpu.SMEM)
    return q, k, v, grp, smem


def _fill_padded(dst_ref, val, s):
    dst_ref[0:CHUNK, :] = jnp.zeros((CHUNK, LANES), dst_ref.dtype)
    dst_ref[CHUNK:CHUNK + s, :] = val.astype(dst_ref.dtype)
    dst_ref[CHUNK + s:2 * CHUNK + s, :] = jnp.zeros((CHUNK, LANES), dst_ref.dtype)


def _attn_probs(sc, slope, snk, dist, valid):
    sc = jnp.where(valid, sc - slope * dist, NEG_INF)
    m = jnp.maximum(jnp.max(sc, axis=1, keepdims=True), snk)
    e = jnp.exp(sc - m)
    es = jnp.exp(snk - m)
    inv = 1.0 / (jnp.sum(e, axis=1, keepdims=True) + es)
    return e * inv, es * inv


def _stack_heads(x2, m0):
    parts = []
    for pr in range(2):
        xp = x2[:, pr * LANES:(pr + 1) * LANES]
        parts += [jnp.where(m0, xp, 0.0), jnp.where(m0, 0.0, xp)]
    return jnp.concatenate(parts, axis=0).astype(BF16)


def _unstack_pair(x_all, pr, m0):
    return jnp.where(m0, x_all[(2 * pr) * CHUNK:(2 * pr + 1) * CHUNK], x_all[(2 * pr + 1) * CHUNK:(2 * pr + 2) * CHUNK])


def _attn_saved_specs(bsz, n_blk):
    specs = [pl.BlockSpec((None, None, n_blk, 4 * CHUNK, w), lambda b, g: (b, g, 0, 0, 0)) for w in (3 * CHUNK, 1)]
    shapes = [jax.ShapeDtypeStruct((bsz, 2, n_blk, 4 * CHUNK, 3 * CHUNK), BF16),
              jax.ShapeDtypeStruct((bsz, 2, n_blk, 4 * CHUNK, 1), F32)]
    return specs, shapes


def _attn_fwd(u3, slopes, sink, rider=None):
    bsz, s, _ = u3.shape
    n_blk = s // CHUNK

    def body(slope_ref, sink_ref, q_ref, k_ref, v_ref, o_ref, p_ref, ps_ref, kp_ref, vp_ref):
        g = pl.program_id(1)
        _fill_padded(kp_ref, _dup_kv_head(k_ref[...], g), s)
        _fill_padded(vp_ref, _dup_kv_head(v_ref[...], g), s)
        m0 = lax.broadcasted_iota(jnp.int32, (CHUNK, LANES), 1) < HEAD_DIM

        def blk(n, carry):
            r0 = pl.multiple_of(n * CHUNK, CHUNK)
            kw = kp_ref[pl.ds(r0, 3 * CHUNK), :]
            vw = vp_ref[pl.ds(r0, 3 * CHUNK), :]
            dist, valid = _attn_window_tables(n, s)
            q_all = _stack_heads(q_ref[pl.ds(r0, CHUNK), :].astype(F32) * 0.125, m0)
            sc_all = _dot_nt(q_all, kw)
            probs, sinks = [], []
            for i in range(4):
                p, ps = _attn_probs(sc_all[i * CHUNK:(i + 1) * CHUNK], slope_ref[g * 4 + i], sink_ref[g * 4 + i],
                                    dist, valid)
                probs.append(p.astype(BF16))
                sinks.append(ps)
            p_all = jnp.concatenate(probs, axis=0)
            p_ref[n] = p_all
            ps_ref[n] = jnp.concatenate(sinks, axis=0)
            out_all = _dot(p_all, vw)
            for pr in range(2):
                o_ref[pl.ds(r0, CHUNK), pr * LANES:(pr + 1) * LANES] = _unstack_pair(out_all, pr, m0).astype(BF16)
            return carry

        lax.fori_loop(0, n_blk, blk, 0, unroll=4)

    q, k, v, grp, smem = _attn_specs(s)
    saved_specs, saved_shapes = _attn_saved_specs(bsz, n_blk)
    return _hosted_call(
        body, "attn_fwd", (bsz, 2),
        in_specs=[smem, smem, q, k, v],
        out_specs=[grp] + saved_specs,
        out_shape=[jax.ShapeDtypeStruct((bsz, s, ATTN_WIDTH), BF16)] + saved_shapes,
        scratch_shapes=[pltpu.VMEM((s + 2 * CHUNK, LANES), BF16), pltpu.VMEM((s + 2 * CHUNK, LANES), BF16)],
        operands=(slopes, sink, u3, u3, u3), rider=rider)


def _attn_bwd(u3, d_o, probs, sink_probs, rider=None):
    bsz, s, _ = u3.shape
    n_blk = s // CHUNK

    def body(q_ref, k_ref, v_ref, do_ref, p_ref, ps_ref, dq_ref, dkv_ref, ds_ref,
             kp_ref, vp_ref, dk_acc, dv_acc):
        g = pl.program_id(1)
        _fill_padded(kp_ref, _dup_kv_head(k_ref[...], g), s)
        _fill_padded(vp_ref, _dup_kv_head(v_ref[...], g), s)
        dk_acc[...] = jnp.zeros_like(dk_acc)
        dv_acc[...] = jnp.zeros_like(dv_acc)
        m0 = lax.broadcasted_iota(jnp.int32, (CHUNK, LANES), 1) < HEAD_DIM

        def blk(n, dsink):
            r0 = pl.multiple_of(n * CHUNK, CHUNK)
            win = pl.ds(r0, 3 * CHUNK)
            kw = kp_ref[win, :]
            vw = vp_ref[win, :]
            q_all = _stack_heads(q_ref[pl.ds(r0, CHUNK), :].astype(F32) * 0.125, m0)
            do_all = _stack_heads(do_ref[pl.ds(r0, CHUNK), :].astype(F32), m0)
            p_all = p_ref[n]
            ps_all = ps_ref[n]
            dp_all = _dot_nt(do_all, vw)
            new_dsink, dscs = [], []
            for i in range(4):
                rows = slice(i * CHUNK, (i + 1) * CHUNK)
                p = p_all[rows].astype(F32)
                dp = dp_all[rows]
                delta = jnp.sum(p * dp, axis=1, keepdims=True)
                dscs.append((p * (dp - delta)).astype(BF16))
                dsh = jnp.sum(ps_all[rows] * delta, axis=0, keepdims=True)
                new_dsink.append(dsink[i] - jnp.broadcast_to(dsh, (1, LANES)))
            dsc_all = jnp.concatenate(dscs, axis=0)
            dq_all = _dot(dsc_all, kw)
            dk_acc[win, :] += _dot_tn(dsc_all, q_all)
            dv_acc[win, :] += _dot_tn(p_all, do_all)
            for pr in range(2):
                dq_ref[pl.ds(r0, CHUNK), pr * LANES:(pr + 1) * LANES] = (
                    _unstack_pair(dq_all, pr, m0) * 0.125).astype(BF16)
            return tuple(new_dsink)

        dsink = _loop_grouped(n_blk, blk, tuple(jnp.zeros((1, LANES), F32) for _ in range(4)), per_trip=4)
        dk = dk_acc[CHUNK:CHUNK + s, :]
        dv = dv_acc[CHUNK:CHUNK + s, :]
        lane = lax.broadcasted_iota(jnp.int32, (s, LANES), 1)
        fold = lambda a: a + pltpu.roll(a, HEAD_DIM, 1)
        dkv_ref[...] = jnp.where(lane < HEAD_DIM, fold(dk), fold(dv)).astype(BF16)
        ds_ref[...] = jnp.zeros_like(ds_ref)
        for i in range(4):
            ds_ref[i:i + 1, :] = dsink[i]

    q, k, v, grp, _ = _attn_specs(s)
    return _hosted_call(
        body, "attn_bwd", (bsz, 2),
        in_specs=[q, k, v, grp] + _attn_saved_specs(bsz, n_blk)[0],
        out_specs=[grp, pl.BlockSpec((None, s, LANES), lambda b, g: (b, 0, g)),
                   pl.BlockSpec((None, None, 8, LANES), lambda b, g: (b, g, 0, 0))],
        out_shape=[jax.ShapeDtypeStruct((bsz, s, ATTN_WIDTH), BF16), jax.ShapeDtypeStruct((bsz, s, 2 * LANES), BF16),
                   jax.ShapeDtypeStruct((bsz, 2, 8, LANES), F32)],
        scratch_shapes=[pltpu.VMEM((s + 2 * CHUNK, LANES), BF16), pltpu.VMEM((s + 2 * CHUNK, LANES), BF16),
                        pltpu.VMEM((s + 2 * CHUNK, LANES), F32), pltpu.VMEM((s + 2 * CHUNK, LANES), F32)],
        operands=(u3, u3, u3, d_o, probs, sink_probs), rider=rider)


def _ffn_bwd(dz2, gs, us, pg, ple, zh1, r1, g1, wg4, wu4, wd4, wpg, w_out):
    t = dz2.shape[0]
    tm = 256
    wg_t, wu_t, wd_all = (w.reshape(FFN, D_MODEL) for w in (wg4, wu4, wd4))

    def body(dz_ref, gs_ref, us_ref, pg_ref, ple_ref, zh_ref, r_ref, g1_ref,
             wg_hbm, wu_hbm, wd_hbm, wpg_hbm, wo_hbm,
             dgs_ref, dus_ref, dsp_ref, dple_ref, dz1_ref, dyr_ref, dya_ref, dg1_ref, db1_ref,
             wg, wu, wd, wpg, wo, wsem):
        step = pl.program_id(0)
        loads = _resident_quarters(wd_hbm, wd) + _resident_quarters(wg_hbm, wg) + _resident_quarters(wu_hbm, wu)
        _load_resident(step, loads + [(wpg_hbm, wpg), (wo_hbm, wo)], wsem)

        @pl.when(step == 0)
        def _():
            dg1_ref[...] = jnp.zeros_like(dg1_ref)
            db1_ref[...] = jnp.zeros_like(db1_ref)

        dz = dz_ref[...]
        dzb = dz.astype(BF16)
        dh = ALPHA * dz
        pending = []
        chunks = [slice(n * FFN_CHUNK, (n + 1) * FFN_CHUNK) for n in range(N_FFN_CHUNK)]
        for n in range(N_FFN_CHUNK + 1):
            if n < N_FFN_CHUNK:
                da = _dot_nt(dzb, wd[chunks[n], :])
                gj = gs_ref[:, chunks[n]].astype(F32)
                uj = us_ref[:, chunks[n]].astype(F32)
                sg = _sigmoid(gj)
                dgj = (da * uj * sg * (1.0 + gj * (1.0 - sg))).astype(BF16)
                duj = (da * gj * sg).astype(BF16)
                dgs_ref[:, chunks[n]] = dgj
                dus_ref[:, chunks[n]] = duj
                pending.append((dgj, duj))
            if n > 0:
                dgp, dup = pending[n - 1]
                dh = dh + _dot(dgp, wg[chunks[n - 1], :]) + _dot(dup, wu[chunks[n - 1], :])
        pgv = pg_ref[...].astype(F32)
        plev = ple_ref[...].astype(F32)
        dple_ref[...] = (dz * pgv).astype(BF16)
        dsp = (dz * plev * pgv * (1.0 - pgv)).astype(BF16)
        dsp_ref[...] = dsp
        dh = dh + _dot_nt(dsp, wpg[...])
        zh = zh_ref[...]
        dg1_ref[...] += jnp.sum(dh * zh, axis=0, keepdims=True)
        db1_ref[...] += jnp.sum(dh, axis=0, keepdims=True)
        dzh = dh * g1_ref[...]
        m1 = jnp.mean(dzh, axis=1, keepdims=True)
        m2 = jnp.mean(dzh * zh, axis=1, keepdims=True)
        dz1 = r_ref[...] * (dzh - m1 - zh * m2)
        dz1_ref[...] = dz1
        dyc = _dot_nt(dz1.astype(BF16), wo[...])
        dyr_ref[...] = dyc[:, 0:RET_WIDTH].astype(BF16)
        dya_ref[...] = dyc[:, RET_WIDTH:].astype(BF16)

    row = lambda w: pl.BlockSpec((tm, w), lambda i: (i, 0))
    const = lambda s: pl.BlockSpec(s, lambda i: (0, 0))
    hbm = pl.BlockSpec(memory_space=pl.ANY)
    hid_shape = jax.ShapeDtypeStruct((t, FFN), BF16)
    return pl.pallas_call(
        body, name="ffn_bwd", grid=(t // tm,),
        in_specs=[row(D_MODEL), row(FFN), row(FFN), row(D_MODEL), row(D_MODEL), row(D_MODEL), row(1),
                  const((1, D_MODEL)), hbm, hbm, hbm, hbm, hbm],
        out_specs=[row(FFN), row(FFN), row(D_MODEL), row(D_MODEL), row(D_MODEL), row(RET_WIDTH), row(ATTN_WIDTH),
                   const((1, D_MODEL)), const((1, D_MODEL))],
        out_shape=[hid_shape, hid_shape, jax.ShapeDtypeStruct((t, D_MODEL), BF16),
                   jax.ShapeDtypeStruct((t, D_MODEL), BF16), jax.ShapeDtypeStruct((t, D_MODEL), F32),
                   jax.ShapeDtypeStruct((t, RET_WIDTH), BF16), jax.ShapeDtypeStruct((t, ATTN_WIDTH), BF16),
                   jax.ShapeDtypeStruct((1, D_MODEL), F32), jax.ShapeDtypeStruct((1, D_MODEL), F32)],
        scratch_shapes=[pltpu.VMEM((FFN, D_MODEL), BF16), pltpu.VMEM((FFN, D_MODEL), BF16),
                        pltpu.VMEM((FFN, D_MODEL), BF16),
                        pltpu.VMEM(wpg.shape, BF16), pltpu.VMEM(w_out.shape, BF16),
                        pltpu.SemaphoreType.DMA((3 * N_SHARD + 2,))],
        compiler_params=_params("arbitrary", vmem=VMEM_LIMIT),
    )(dz2, gs, us, pg, ple, zh1, r1, g1, wg_t, wu_t, wd_all, wpg, w_out)


def _wgrad_misc(y_ret, y_att, dz1, hb, dsp, p2d, dple, rider=None):
    t = dz1.shape[0]
    tk = min(t, 512)
    pc = D_MODEL // N_SHARD

    def body(yr_ref, ya_ref, dz_ref, hb_ref, dsp_ref, p_ref, dple_ref, wo_ref, wpg_ref, wpe_ref):
        @pl.when(pl.program_id(0) == 0)
        def _():
            wo_ref[...] = jnp.zeros_like(wo_ref)
            wpg_ref[...] = jnp.zeros_like(wpg_ref)
            wpe_ref[...] = jnp.zeros_like(wpe_ref)

        dzb = dz_ref[...].astype(BF16)
        wo_ref[0:RET_WIDTH, :] += _dot_tn(yr_ref[...], dzb)
        wo_ref[RET_WIDTH:, :] += _dot_tn(ya_ref[...], dzb)
        wpg_ref[...] += _dot_tn(hb_ref[...], dsp_ref[...])
        dpe = _dot_tn(p_ref[...].astype(BF16), dple_ref[...])
        for j in range(N_SHARD):
            wpe_ref[j] += dpe[:, j * pc:(j + 1) * pc]

    row = lambda w: pl.BlockSpec((tk, w), lambda k: (k, 0))
    const = lambda s: pl.BlockSpec(s, lambda k: (0,) * len(s))
    return _hosted_call(
        body, "wgrad_misc", (t // tk,),
        in_specs=[row(RET_WIDTH), row(ATTN_WIDTH), row(D_MODEL), row(D_MODEL), row(D_MODEL), row(PLE_DIM),
                  row(D_MODEL)],
        out_specs=[const((D_MODEL, D_MODEL)), const((D_MODEL, D_MODEL)), const((N_SHARD, PLE_DIM, pc))],
        out_shape=[jax.ShapeDtypeStruct((D_MODEL, D_MODEL), F32), jax.ShapeDtypeStruct((D_MODEL, D_MODEL), F32),
                   jax.ShapeDtypeStruct((N_SHARD, PLE_DIM, pc), F32)],
        scratch_shapes=[], operands=(y_ret, y_att, dz1, hb, dsp, p2d, dple), rider=rider, semantics=["arbitrary"])


def _wgrad_ffn(acts, dgs, dus, hb, dz2b):
    t = dz2b.shape[0]
    tk = min(t, 512)
    nk = t // tk

    def body(act_ref, dg_ref, du_ref, hb_ref, dz_ref, og_ref, ou_ref, od_ref):
        @pl.when(pl.program_id(1) == 0)
        def _():
            og_ref[...] = jnp.zeros_like(og_ref)
            ou_ref[...] = jnp.zeros_like(ou_ref)
            od_ref[...] = jnp.zeros_like(od_ref)

        hbv = hb_ref[...]
        og_ref[...] += _dot_tn(dg_ref[...], hbv)
        ou_ref[...] += _dot_tn(du_ref[...], hbv)
        od_ref[...] += _dot_tn(act_ref[...], dz_ref[...])

    half = FFN // 2
    a_spec = pl.BlockSpec((tk, half), lambda j, k: (k, j))
    b_spec = pl.BlockSpec((tk, D_MODEL), lambda j, k: (k, 0))
    o_spec = pl.BlockSpec((half, D_MODEL), lambda j, k: (j, 0))
    o_shape = jax.ShapeDtypeStruct((FFN, D_MODEL), F32)
    outs = pl.pallas_call(
        body, name="wgrad_ffn", grid=(2, nk),
        in_specs=[a_spec, a_spec, a_spec, b_spec, b_spec],
        out_specs=[o_spec] * 3, out_shape=[o_shape] * 3,
        compiler_params=_params("parallel", "arbitrary", vmem=VMEM_LIMIT),
    )(acts, dgs, dus, hb, dz2b)
    return [o.reshape(N_SHARD, FFN_SHARD, D_MODEL) for o in outs]


KV_ORDER = (0, 128, 64, 192)


def _wgrad_in(pieces, x2d, rider=None):
    t = x2d.shape[0]
    tk = min(t, 512)
    nk = t // tk
    kv0 = CB_AK * LANES

    def body(p0, p1, p2, p3, p4, pkv, x_ref, o_ref):
        @pl.when(pl.program_id(0) == 0)
        def _():
            o_ref[...] = jnp.zeros_like(o_ref)

        xb = x_ref[...].astype(BF16)
        for i, ref in enumerate((p0, p1, p2, p3, p4)):
            o_ref[i * 512:(i + 1) * 512, :] += _dot_tn(ref[...], xb)
        dkv = _dot_tn(pkv[...], xb)
        for i, o in enumerate(KV_ORDER):
            o_ref[kv0 + o:kv0 + o + HEAD_DIM, :] += dkv[i * HEAD_DIM:(i + 1) * HEAD_DIM]

    row = lambda w: pl.BlockSpec((tk, w), lambda k: (k, 0))
    return _hosted_call(
        body, "wgrad_in", (nk,),
        in_specs=[row(512)] * 5 + [row(256), row(D_MODEL)],
        out_specs=[pl.BlockSpec((IN_WIDTH, D_MODEL), lambda k: (0, 0))],
        out_shape=[jax.ShapeDtypeStruct((IN_WIDTH, D_MODEL), F32)],
        scratch_shapes=[], operands=(*pieces, x2d), rider=rider, semantics=["arbitrary"])


def _inproj_bwd(dz1, pieces, w_in_t, rider=None):
    t = dz1.shape[0]
    tm = 512
    kv0 = CB_AK * LANES

    def body(dz_ref, p0, p1, p2, p3, p4, pkv, w_ref, o_ref):
        acc = ALPHA * dz_ref[...]
        for i, ref in enumerate((p0, p1, p2, p3, p4)):
            acc = acc + _dot(ref[...], w_ref[i * 512:(i + 1) * 512, :])
        w_kv = jnp.concatenate([w_ref[kv0 + o:kv0 + o + HEAD_DIM, :] for o in KV_ORDER], axis=0)
        o_ref[...] = acc + _dot(pkv[...], w_kv)

    row = lambda w: pl.BlockSpec((tm, w), lambda i: (i, 0))
    return _hosted_call(
        body, "inproj_bwd", (t // tm,),
        in_specs=[row(D_MODEL)] + [row(512)] * 5 + [row(256), pl.BlockSpec((IN_WIDTH, D_MODEL), lambda i: (0, 0))],
        out_specs=[row(D_MODEL)],
        out_shape=[jax.ShapeDtypeStruct((t, D_MODEL), F32)],
        scratch_shapes=[], operands=(dz1, *pieces, w_in_t), rider=rider)


def _coords():
    return lax.axis_index("x"), lax.axis_index("y"), lax.axis_index("c")


def _chip_of(x, y, rel):
    return (1 - x if rel & 2 else x), (1 - y if rel & 1 else y)


def _gather_and_cast(shard, others):
    near = _gather_near_rider([shard])
    relay = _gather_relay_rider(near.out_shapes, chained=True)
    pass_near = _gather_pass_rider(near.out_shapes, chained=True, rels=NEAR)
    pass_far = _gather_pass_rider(near.out_shapes, chained=True, rels=(3,))
    riders = [near, relay, pass_near, pass_far]
    no = len(others)

    def body(*refs):
        shard_ref, wide = refs[0], refs[1:1 + no]
        out_ref, narrow = refs[1 + no], refs[2 + no:2 + 2 * no]
        k = 2 + 2 * no
        vin, vout, (lsem, ssem) = refs[k:k + no], refs[k + no:k + 2 * no], refs[k + 2 * no:k + 2 * no + 2]
        k += 2 * no + 2
        sems = {}
        for r in riders:
            sems[id(r)] = refs[k:k + len(r.sems)]
            k += len(r.sems)
        run = lambda r, method: getattr(r, method)([shard_ref], [out_ref], sems[id(r)])
        loads = [pltpu.make_async_copy(wide[w], vin[w], lsem.at[w]) for w in range(no)]
        stores = [pltpu.make_async_copy(vout[w], narrow[w], ssem.at[w]) for w in range(no)]

        run(near, "start")
        for cp in loads:
            cp.start()
        for w in range(no):
            loads[w].wait()
            vout[w][...] = vin[w][...].astype(BF16)
            stores[w].start()
        run(near, "finish")
        run(relay, "start")
        run(pass_near, "start")
        run(relay, "finish")
        run(pass_far, "start")
        run(pass_near, "finish")
        run(pass_far, "finish")
        for cp in stores:
            cp.wait()

    hbm = pl.BlockSpec(memory_space=pl.ANY)
    dma = pltpu.SemaphoreType.DMA
    gathered, *cast = pl.pallas_call(
        body, name="gather_weights", in_specs=[hbm] * (1 + no), out_specs=[hbm] * (1 + no),
        out_shape=near.out_shapes + [jax.ShapeDtypeStruct(a.shape, BF16) for a in others],
        scratch_shapes=[pltpu.VMEM(a.shape, F32) for a in others] + [pltpu.VMEM(a.shape, BF16) for a in others]
        + [dma((no,)), dma((no,))] + [s for r in riders for s in r.sems],
        compiler_params=_params(vmem=VMEM_LIMIT),
    )(shard, *others)
    return gathered, cast


def _gather_half(outs, w, chip, cc):
    h = outs[w].shape[1] // 2
    return outs[w].at[chip, pl.ds(cc * h, h), :]


NEAR = (1, 2)


def _gather_near_rider(shards, rels=NEAR):
    nw, nr = len(shards), len(rels)

    def copies(ins, outs, sems, arrivals):
        send, recv, lsend, lrecv = sems
        x, y, c = _coords()
        me = 2 * x + y
        own = [pltpu.make_async_remote_copy(
            src_ref=ins[w], dst_ref=outs[w].at[me], send_sem=lsend.at[w], recv_sem=lrecv.at[w],
            device_id=(x, y, 1 - c), device_id_type=MESH) for w in range(nw)]
        out, arrive = [], []
        for i, rel in enumerate(rels):
            kx, ky = _chip_of(x, y, rel)
            for w in range(nw):
                h = shards[w].shape[0] // 2
                sem = dict(send_sem=send.at[w * nr + i], recv_sem=recv.at[w * nr + i],
                           device_id=(kx, ky, c), device_id_type=MESH)
                out.append(pltpu.make_async_remote_copy(
                    src_ref=ins[w].at[pl.ds(c * h, h), :], dst_ref=_gather_half(outs, w, me, c), **sem))
                if arrivals:
                    theirs = _gather_half(outs, w, 2 * kx + ky, c)
                    arrive.append(pltpu.make_async_remote_copy(src_ref=theirs, dst_ref=theirs, **sem))
        return own, out, arrive

    def start(ins, outs, sems):
        own, out, _ = copies(ins, outs, sems, arrivals=False)
        for cp in own + out:
            cp.start()

    def finish(ins, outs, sems):
        own, out, arrive = copies(ins, outs, sems, arrivals=True)
        for cp in arrive:
            cp.wait_recv()
        for cp in out:
            cp.wait_send()
        for cp in own:
            cp.wait()

    dma = pltpu.SemaphoreType.DMA
    return _Rider(shards, [jax.ShapeDtypeStruct((N_SHARD,) + s.shape, s.dtype) for s in shards],
                  [dma((nr * nw,)), dma((nr * nw,)), dma((nw,)), dma((nw,))], start, finish)


def _gather_relay_rider(gathered, chained=False):
    nw = len(gathered)

    def quarter(outs, w, chip, c, p):
        q = outs[w].shape[1] // 4
        return outs[w].at[chip, pl.ds(c * 2 * q + p * q, q), :]

    def copies(outs, sems):
        send, recv = sems
        x, y, c = _coords()
        (yx, yy), (xx, xy), (dx, dy) = (_chip_of(x, y, rel) for rel in (1, 2, 3))
        out, arrive = [], []
        for w in range(nw):
            for p, (src_chip, dst) in enumerate(((2 * xx + xy, (yx, yy)), (2 * yx + yy, (xx, xy)))):
                rows = quarter(outs, w, src_chip, c, p)
                sem = dict(send_sem=send.at[w * 2 + p], recv_sem=recv.at[w * 2 + p], device_id_type=MESH)
                out.append(pltpu.make_async_remote_copy(src_ref=rows, dst_ref=rows, device_id=(*dst, c), **sem))
                mine = quarter(outs, w, 2 * dx + dy, c, p)
                arrive.append(pltpu.make_async_remote_copy(src_ref=mine, dst_ref=mine, device_id=(*dst, c), **sem))
        return out, arrive

    def start(ins, outs, sems):
        for cp in copies(outs, sems)[0]:
            cp.start()

    def finish(ins, outs, sems):
        out, arrive = copies(outs, sems)
        for cp in arrive:
            cp.wait_recv()
        for cp in out:
            cp.wait_send()

    dma = pltpu.SemaphoreType.DMA
    shapes = [jax.ShapeDtypeStruct(g.shape, g.dtype) for g in gathered]
    if chained:
        return _Rider([], [], [dma((2 * nw,)), dma((2 * nw,))], start, finish)
    return _Rider(gathered, shapes, [dma((2 * nw,)), dma((2 * nw,))], start, finish,
                  aliases={w: w for w in range(nw)})


def _gather_pass_rider(gathered, chained=False, rels=(1, 2, 3)):
    nw, nr = len(gathered), len(rels)

    def copies(outs, sems, cc):
        send, recv = sems
        x, y, c = _coords()
        res = []
        for i, rel in enumerate(rels):
            kx, ky = _chip_of(x, y, rel)
            for w in range(nw):
                rows = _gather_half(outs, w, 2 * kx + ky, cc)
                res.append(pltpu.make_async_remote_copy(
                    src_ref=rows, dst_ref=rows, send_sem=send.at[w * nr + i], recv_sem=recv.at[w * nr + i],
                    device_id=(x, y, 1 - c), device_id_type=MESH))
        return res

    def start(ins, outs, sems):
        for cp in copies(outs, sems, lax.axis_index("c")):
            cp.start()

    def finish(ins, outs, sems):
        c = lax.axis_index("c")
        for cp in copies(outs, sems, 1 - c):
            cp.wait_recv()
        for cp in copies(outs, sems, c):
            cp.wait_send()

    dma = pltpu.SemaphoreType.DMA
    shapes = [jax.ShapeDtypeStruct(g.shape, g.dtype) for g in gathered]
    if chained:
        return _Rider([], [], [dma((nr * nw,)), dma((nr * nw,))], start, finish)
    return _Rider(gathered, shapes, [dma((nr * nw,)), dma((nr * nw,))], start, finish,
                  aliases={w: w for w in range(nw)})


def _exchange_halves_rider(parts):
    nw = len(parts)

    def copies(ins, outs, sems):
        send, recv = sems
        x, y, c = _coords()
        res = []
        for w in range(nw):
            h = parts[w].shape[1] // 2
            res.append(pltpu.make_async_remote_copy(
                src_ref=ins[w].at[:, pl.ds((1 - c) * h, h), :], dst_ref=outs[w],
                send_sem=send.at[w], recv_sem=recv.at[w], device_id=(x, y, 1 - c), device_id_type=MESH))
        return res

    def start(ins, outs, sems):
        for cp in copies(ins, outs, sems):
            cp.start()

    def finish(ins, outs, sems):
        for cp in copies(ins, outs, sems):
            cp.wait()

    dma = pltpu.SemaphoreType.DMA
    return _Rider(parts, [jax.ShapeDtypeStruct((N_SHARD, p.shape[1] // 2, p.shape[2]), p.dtype) for p in parts],
                  [dma((nw,)), dma((nw,))], start, finish)


def _add_halves(parts, theirs, pos):
    nw = len(parts)
    split = 1

    def body(pos_ref, *refs):
        ins, oth = refs[:nw], refs[nw:2 * nw]
        o32, o16 = refs[2 * nw:3 * nw], refs[3 * nw:]
        sums = [ins[w][...] + oth[w][...] for w in range(nw)]
        for w in range(nw):
            o16[w][...] = sums[w].astype(BF16)

        @pl.when(pl.program_id(1) == pos_ref[0])
        def _():
            for w in range(nw):
                o32[w][...] = sums[w]

    in_specs, oth_specs, o32_specs, shapes32, shapes16 = [], [], [], [], []
    for p in parts:
        hb = p.shape[1] // 2 // split
        blk = (None, hb, p.shape[2])
        in_specs.append(pl.BlockSpec(blk, lambda i, j, pos_ref: (j, pos_ref[1] * split + i, 0)))
        oth_specs.append(pl.BlockSpec(blk, lambda i, j, pos_ref: (j, i, 0)))
        o32_specs.append(pl.BlockSpec((hb, p.shape[2]), lambda i, j, pos_ref: (i, 0)))
        shapes32.append(jax.ShapeDtypeStruct((p.shape[1] // 2, p.shape[2]), F32))
        shapes16.append(jax.ShapeDtypeStruct((N_SHARD, p.shape[1] // 2, p.shape[2]), BF16))
    return pl.pallas_call(
        body, name="add_halves",
        grid_spec=pltpu.PrefetchScalarGridSpec(
            num_scalar_prefetch=1, grid=(split, N_SHARD),
            in_specs=in_specs + oth_specs, out_specs=o32_specs + oth_specs),
        out_shape=shapes32 + shapes16,
        compiler_params=_params("parallel", "arbitrary", vmem=VMEM_LIMIT),
    )(pos, *parts, *theirs)


def _exchange_chips_rider(sums16, rows=None, into=None):
    nw = len(sums16)
    rows = rows or [(0, s.shape[1]) for s in sums16]
    held = [w for w in range(nw) if into is not None and into[w] is not None]

    def copies(ins, outs, sems):
        send, recv = sems
        x, y, c = _coords()
        res = []
        for rel in (1, 2, 3):
            kx, ky = _chip_of(x, y, rel)
            for w in range(nw):
                r0, n = rows[w]
                res.append(pltpu.make_async_remote_copy(
                    src_ref=ins[w].at[2 * kx + ky, pl.ds(r0, n), :], dst_ref=outs[w].at[rel - 1, pl.ds(r0, n), :],
                    send_sem=send.at[w * 3 + rel - 1], recv_sem=recv.at[w * 3 + rel - 1],
                    device_id=(kx, ky, c), device_id_type=MESH))
        return res

    def start(ins, outs, sems):
        for cp in copies(ins, outs, sems):
            cp.start()

    def finish(ins, outs, sems):
        for cp in copies(ins, outs, sems):
            cp.wait()

    dma = pltpu.SemaphoreType.DMA
    return _Rider(list(sums16) + [into[w] for w in held],
                  [jax.ShapeDtypeStruct((3,) + s.shape[1:], BF16) for s in sums16],
                  [dma((3 * nw,)), dma((3 * nw,))], start, finish, aliases={nw + i: w for i, w in enumerate(held)})


def _add_chips(sums32, theirs, pos):
    nw = len(sums32)
    split = 2
    hbs = [s.shape[0] // split for s in sums32]

    def body(pos_ref, *refs):
        ins, oth, outs, bufs = (refs[k * nw:(k + 1) * nw] for k in range(4))
        lsem, ssem, rsem = refs[4 * nw:]
        i = pl.program_id(0)
        x, y, c = _coords()

        def copies(w, j):
            rows = pl.ds(pl.multiple_of((pos_ref[1] * split + j) * hbs[w], 8), hbs[w])
            return (pltpu.make_async_copy(bufs[w].at[j], outs[w].at[rows, :], lsem.at[w, j]),
                    pltpu.make_async_remote_copy(
                        src_ref=bufs[w].at[j], dst_ref=outs[w].at[rows, :], send_sem=ssem.at[w, j],
                        recv_sem=rsem.at[w, j], device_id=(x, y, 1 - c), device_id_type=MESH))

        for w in range(nw):
            acc = ins[w][...]
            for r in range(3):
                acc = acc + oth[w][r].astype(F32)
            bufs[w][i] = acc
            for cp in copies(w, i):
                cp.start()

        @pl.when(i == split - 1)
        def _():
            for w in range(nw):
                for j in range(split):
                    local, remote = copies(w, j)
                    local.wait()
                    remote.wait()

    in_specs, oth_specs, shapes, scratch = [], [], [], []
    for s, hb in zip(sums32, hbs):
        in_specs.append(pl.BlockSpec((hb, s.shape[1]), lambda i, pos_ref: (i, 0)))
        oth_specs.append(pl.BlockSpec((3, hb, s.shape[1]), lambda i, pos_ref: (0, i, 0)))
        shapes.append(jax.ShapeDtypeStruct((2 * s.shape[0], s.shape[1]), F32))
        scratch.append(pltpu.VMEM((split, hb, s.shape[1]), F32))
    dma = pltpu.SemaphoreType.DMA
    return pl.pallas_call(
        body, name="add_chips",
        grid_spec=pltpu.PrefetchScalarGridSpec(
            num_scalar_prefetch=1, grid=(split,), in_specs=in_specs + oth_specs,
            out_specs=[pl.BlockSpec(memory_space=pl.ANY)] * nw,
            scratch_shapes=scratch + [dma((nw, split)), dma((nw, split)), dma((nw, split))]),
        out_shape=shapes,
        compiler_params=_params("arbitrary", vmem=VMEM_LIMIT),
    )(pos, *sums32, *theirs)


def _adamw_math(w, g, m, v):
    m = ADAM_B1 * m + (1.0 - ADAM_B1) * g
    v = ADAM_B2 * v + (1.0 - ADAM_B2) * (g * g)
    m_hat = m / (1.0 - ADAM_B1 ** ADAM_STEP)
    v_hat = v / (1.0 - ADAM_B2 ** ADAM_STEP)
    delta = -ADAM_LR * (m_hat / (jnp.sqrt(v_hat) + ADAM_EPS) + ADAM_WD * w)
    return delta, m, v


def _adamw(ws, gs, ms, vs):
    nw = len(ws)
    split = 8

    def body(*refs):
        w_r, g_r, m_r, v_r = (refs[i * nw:(i + 1) * nw] for i in range(4))
        g_o, d_o, m_o, v_o = (refs[(4 + i) * nw:(5 + i) * nw] for i in range(4))
        for k in range(nw):
            g = g_r[k][...]
            d, m, v = _adamw_math(w_r[k][...], g, m_r[k][...], v_r[k][...])
            g_o[k][...] = g
            d_o[k][...] = d
            m_o[k][...] = m
            v_o[k][...] = v

    specs = [pl.BlockSpec((w.shape[0] // split, w.shape[1]), lambda i: (i, 0)) for w in ws]
    shapes = [jax.ShapeDtypeStruct(w.shape, F32) for w in ws]
    outs = pl.pallas_call(
        body, name="adamw", grid=(split,),
        in_specs=specs * 4, out_specs=specs * 4, out_shape=shapes * 4,
        compiler_params=_params("parallel", vmem=VMEM_LIMIT),
    )(*ws, *gs, *ms, *vs)
    return outs[:nw], outs[nw:2 * nw], outs[2 * nw:3 * nw], outs[3 * nw:]


SMALL_ROWS = 8
SMALL_COLS = D_MODEL
LOSS_COL = RET_WIDTH + 24


def _small_allreduce_adamw(part, w, m, v, rider=None):
    def body(part_ref, w_ref, m_ref, v_ref, g_out, d_out, m_out, v_out, all_ref, send, recv):
        x, y, c = _coords()
        me = 4 * x + 2 * y + c
        all_ref[me] = part_ref[...]
        copies = []
        for rel in range(1, 8):
            px = 1 - x if rel & 4 else x
            py = 1 - y if rel & 2 else y
            pc = 1 - c if rel & 1 else c
            copies.append(pltpu.make_async_remote_copy(
                src_ref=part_ref, dst_ref=all_ref.at[me],
                send_sem=send.at[rel - 1], recv_sem=recv.at[rel - 1], device_id=(px, py, pc), device_id_type=MESH))
        for cp in copies:
            cp.start()
        for cp in copies:
            cp.wait()
        g = all_ref[0]
        for k in range(1, 8):
            g = g + all_ref[k]
        d, mn, vn = _adamw_math(w_ref[...], g, m_ref[...], v_ref[...])
        g_out[...] = g
        d_out[...] = d
        m_out[...] = mn
        v_out[...] = vn

    vm = pl.BlockSpec(memory_space=pltpu.VMEM)
    shape = jax.ShapeDtypeStruct((SMALL_ROWS, SMALL_COLS), F32)
    return _hosted_call(
        body, "small_allreduce_adamw", (1,),
        in_specs=[vm] * 4, out_specs=[vm] * 4, out_shape=[shape] * 4,
        scratch_shapes=[pltpu.VMEM((8, SMALL_ROWS, SMALL_COLS), F32),
                        pltpu.SemaphoreType.DMA((7,)), pltpu.SemaphoreType.DMA((7,))],
        operands=(part, w, m, v), rider=rider, semantics=["arbitrary"])


SMALL_NAMES = ("ret_decay_fwd", "ret_decay_bwd", "attn_sink", "ret_gn_gain",
               "ln1_gain", "ln1_bias", "ln2_gain", "ln2_bias")


LN_NAMES = ("ln1_gain", "ln1_bias", "ln2_gain", "ln2_bias")


def _pack_small(vals, extra=None):
    tail = jnp.zeros((1, 1), F32) if extra is None else extra.reshape(1, 1)
    row4 = jnp.concatenate([vals["ret_gn_gain"], vals["ret_decay_fwd"], vals["ret_decay_bwd"], vals["attn_sink"],
                            tail, jnp.zeros((1, SMALL_COLS - LOSS_COL - 1), F32)], axis=1)
    rows = [vals[n] for n in LN_NAMES] + [row4, jnp.zeros((SMALL_ROWS - 5, SMALL_COLS), F32)]
    return jnp.concatenate(rows, axis=0)


def _unpack_small(packed):
    o = RET_WIDTH
    where = [(n, i, 0, SMALL_COLS) for i, n in enumerate(LN_NAMES)] + [
        ("ret_gn_gain", 4, 0, o), ("ret_decay_fwd", 4, o, 8), ("ret_decay_bwd", 4, o + 8, 8),
        ("attn_sink", 4, o + 16, 8)]
    na, k = len(packed), len(where)

    def body(*refs):
        for a in range(na):
            rows = refs[a][...]
            for b, (_, row, col, n) in enumerate(where):
                refs[na + a * k + b][...] = rows[row:row + 1, col:col + n]
            if a == 0:
                refs[na + na * k][...] = rows[4:5, LOSS_COL:LOSS_COL + 1] * (0.5 / D_MODEL)

    vmem = pl.BlockSpec(memory_space=pltpu.VMEM)
    outs = pl.pallas_call(
        body, name="unpack_small", in_specs=[vmem] * na, out_specs=[vmem] * (na * k + 1),
        out_shape=[jax.ShapeDtypeStruct((1, n), F32) for _ in range(na) for (_, _, _, n) in where]
        + [jax.ShapeDtypeStruct((1, 1), F32)])(*packed)
    return [{where[b][0]: outs[a * k + b] for b in range(k)} for a in range(na)], outs[na * k]


def _local_step(x, p, tgt, w_in_t, rest, small, pos=None, small_state=None):
    bsz, s, _ = x.shape
    t = bsz * s
    x2d = x.reshape(t, D_MODEL)
    p2d = p.reshape(t, PLE_DIM)
    tgt2d = tgt.reshape(t, D_MODEL)
    dec_f = small["ret_decay_fwd"].reshape(8)
    dec_b = small["ret_decay_bwd"].reshape(8)
    lg_f = jnp.log1p(-jnp.exp2(dec_f))
    lg_b = jnp.log1p(-jnp.exp2(dec_b))
    per_lane = lambda v: jnp.repeat(v, HEAD_DIM).reshape(4, 1, LANES)
    lgf_l, lgb_l = per_lane(lg_f), per_lane(lg_b)
    sink = small["attn_sink"].reshape(8)
    slopes = 2.0 ** (-(jnp.arange(8, dtype=F32) + 1.0))
    gn_gain = small["ret_gn_gain"]
    g1, b1, g2, b2 = (small[n] for n in ("ln1_gain", "ln1_bias", "ln2_gain", "ln2_bias"))

    dist = pos is not None
    shard = dict(zip(REST_NAMES, rest)) if dist else {}
    near = lambda names, rels=NEAR: _gather_near_rider([shard[n] for n in names], rels)
    wave1, wave2, wave3 = ("w_out", "w_ple_gate", "w_ffn_gate"), ("w_ffn_up", "w_ple_proj"), ("w_ffn_down",)
    n1 = len(wave1)
    u, *o1 = _inproj(x2d, w_in_t, rider=near(wave1) if dist else None)
    u3 = u.reshape(bsz, s, IN_WIDTH)
    y_hat, y_rstd, y_ret, ret_rb, ret_kvf, *o2 = _ret_fwd(u3, lgf_l, lgb_l, gn_gain, rider=_merge_riders(
        [_gather_relay_rider(o1), near(wave2)]) if dist else None)
    y_att, att_p, att_ps, *o3 = _attn_fwd(u3, slopes, sink, rider=_merge_riders(
        [_gather_pass_rider(o2[:n1]), _gather_relay_rider(o2[n1:]), near(wave3, (1, 2, 3))]) if dist else None)
    gathered = dict(zip(wave1, o3[:n1]))
    w_out = _assemble_weights({"w_out": gathered["w_out"]})["w_out"] if dist else rest["w_out"]
    zh1, r1, hb, *o4 = _outproj_ln1(y_ret.reshape(t, RET_WIDTH), y_att.reshape(t, ATTN_WIDTH), x2d, w_out, g1, b1,
                                    rider=_gather_pass_rider(o3[n1:]) if dist else None)
    gathered.update(zip(wave2 + wave3, o4))
    wts = _assemble_weights(gathered) if dist else rest
    dz2, dz2b, gs, us, acts, pg, ple, sq, dg2, db2 = _ffn_fwd(
        zh1, hb, p2d, tgt2d, g1, b1, g2, b2, wts["gate4"], wts["up4"], wts["down4"], wts["ple_proj"], wts["ple_gate"])
    dgs, dus, dsp, dple, dz1, dyr, dya, dg1, db1 = _ffn_bwd(dz2, gs, us, pg, ple, zh1, r1, g1, wts["gate4"],
                                                          wts["up4"], wts["down4"], wts["ple_gate"], wts["w_out"])
    ffn_parts = list(_wgrad_ffn(acts, dgs, dus, hb, dz2b))
    d_w_out, d_ple_gate, d_ple_proj, *th_ffn = _wgrad_misc(
        y_ret.reshape(t, RET_WIDTH), y_att.reshape(t, ATTN_WIDTH), dz1, hb, dsp, p2d, dple,
        rider=_exchange_halves_rider(ffn_parts[:2]) if dist else None)
    misc_parts = [d_w_out.reshape(N_SHARD, D_MODEL // N_SHARD, D_MODEL), d_ple_proj,
                  d_ple_gate.reshape(N_SHARD, D_MODEL // N_SHARD, D_MODEL)]
    dyr3, dya3 = dyr.reshape(bsz, s, RET_WIDTH), dya.reshape(bsz, s, ATTN_WIDTH)
    if dist:
        s_gu = _add_halves(ffn_parts[:2], th_ffn, pos)
        half = FFN_SHARD // 2
        quarter = half // 2
        later_parts = [ffn_parts[2]] + misc_parts
        drq, drk, drv, drg, rpart, *o5 = _ret_bwd(u3, y_hat, y_rstd, (ret_rb, ret_kvf), dyr3, lgf_l, lgb_l, gn_gain,
                                                  rider=_merge_riders(
            [_exchange_chips_rider(s_gu[2:], rows=[(0, half), (0, quarter)]), _exchange_halves_rider(later_parts)]))
        s_dm = _add_halves(later_parts, o5[2:], pos)
        daq, dakv, spart, *o6 = _attn_bwd(u3, dya3, att_p, att_ps, rider=_exchange_chips_rider(
            [s_gu[3], s_dm[4]], rows=[(quarter, half - quarter), (0, half)], into=[o5[1], None]))
    else:
        drq, drk, drv, drg, rpart = _ret_bwd(u3, y_hat, y_rstd, (ret_rb, ret_kvf), dyr3, lgf_l, lgb_l, gn_gain)
        daq, dakv, spart = _attn_bwd(u3, dya3, att_p, att_ps)
    pieces = [a.reshape(t, -1) for a in (drq, drk, drv, drg, daq, dakv)]
    d_in, *o7 = _wgrad_in(pieces, x2d, rider=_exchange_chips_rider(list(s_dm[5:])) if dist else None)
    d_in = d_in.reshape(N_SHARD, FFN_SHARD, D_MODEL)

    rsum = rpart
    lane_heads = lambda row: jnp.sum(row.reshape(4, 2, HEAD_DIM), axis=-1).reshape(8)
    dlg_f = lane_heads(rsum[:, 0, :]) + jnp.stack([jnp.sum(rsum[:, 2, :], -1), jnp.sum(rsum[:, 3, :], -1)], 1).reshape(8)
    dlg_b = lane_heads(rsum[:, 1, :]) + jnp.stack([jnp.sum(rsum[:, 4, :], -1), jnp.sum(rsum[:, 5, :], -1)], 1).reshape(8)
    chain = lambda d: -(math.log(2.0) * jnp.exp2(d)) / (1.0 - jnp.exp2(d))
    grads_small = {
        "ret_decay_fwd": (dlg_f * chain(dec_f)).reshape(1, 8),
        "ret_decay_bwd": (dlg_b * chain(dec_b)).reshape(1, 8),
        "attn_sink": jnp.sum(spart, axis=0)[:, 0:4, 0].reshape(1, 8),
        "ret_gn_gain": rsum[:, 6, :].reshape(1, RET_WIDTH),
        "ln1_gain": dg1, "ln1_bias": db1, "ln2_gain": dg2, "ln2_bias": db2,
    }
    if not dist:
        grad_x, = _inproj_bwd(dz1, pieces, w_in_t)
        grads_rest = [misc_parts[0]] + ffn_parts + misc_parts[1:]
        return sq[0, 0], grad_x.reshape(bsz, s, D_MODEL), d_in, grads_rest, grads_small
    *small_out, th_in = _small_allreduce_adamw(_pack_small(grads_small, sq[0, 0]), *small_state,
                                               rider=_exchange_halves_rider([d_in]))
    s_in = _add_halves([d_in], [th_in], pos)
    grad_x, chips_in = _inproj_bwd(dz1, pieces, w_in_t, rider=_exchange_chips_rider([s_in[1]]))
    sums32 = [s_in[0], s_dm[1], s_gu[0], s_gu[1], s_dm[0], s_dm[2], s_dm[3]]
    from_chips = [chips_in, o7[0], o5[0], o6[0], o6[1], o7[1], o7[2]]
    return grad_x.reshape(bsz, s, D_MODEL), sums32, from_chips, small_out


BIG_NAMES = ("w_in", "w_out", "w_ffn_gate", "w_ffn_up", "w_ffn_down", "w_ple_proj", "w_ple_gate")
REST_NAMES = BIG_NAMES[1:]
TRANSPOSED = ("w_in", "w_ffn_gate", "w_ffn_up")
WEIGHT_ORDER = ("w_in", "ret_decay_fwd", "ret_decay_bwd", "ret_gn_gain", "attn_sink", "w_out", "ln1_gain",
                "ln1_bias", "w_ffn_gate", "w_ffn_up", "w_ffn_down", "w_ple_proj", "w_ple_gate", "ln2_gain", "ln2_bias")


def _shard_rows(name, a):
    return jnp.swapaxes(a[0], 0, 1) if name in TRANSPOSED else a[0]


def _unshard_rows(name, a):
    return (jnp.swapaxes(a, 0, 1) if name in TRANSPOSED else a)[None]


def _assemble_weights(gathered):
    rows = lambda a: a.reshape(N_SHARD * a.shape[1], a.shape[2])
    same = lambda a: a
    layout = {"w_out": ("w_out", rows), "w_ffn_gate": ("gate4", same), "w_ffn_up": ("up4", same),
              "w_ffn_down": ("down4", same), "w_ple_proj": ("ple_proj", same), "w_ple_gate": ("ple_gate", rows)}
    return {layout[n][0]: layout[n][1](a) for n, a in gathered.items()}


def kernel(x, p, w_in, ret_decay_fwd, ret_decay_bwd, ret_gn_gain, attn_sink, w_out, ln1_gain, ln1_bias, w_ffn_gate, w_ffn_up, w_ffn_down, w_ple_proj, w_ple_gate, ln2_gain, ln2_bias, loss_target, m_w_in, m_ret_decay_fwd, m_ret_decay_bwd, m_ret_gn_gain, m_attn_sink, m_w_out, m_ln1_gain, m_ln1_bias, m_w_ffn_gate, m_w_ffn_up, m_w_ffn_down, m_w_ple_proj, m_w_ple_gate, m_ln2_gain, m_ln2_bias, v_w_in, v_ret_decay_fwd, v_ret_decay_bwd, v_ret_gn_gain, v_attn_sink, v_w_out, v_ln1_gain, v_ln1_bias, v_w_ffn_gate, v_w_ffn_up, v_w_ffn_down, v_w_ple_proj, v_w_ple_gate, v_ln2_gain, v_ln2_bias):
    w = dict(w_in=w_in, ret_decay_fwd=ret_decay_fwd, ret_decay_bwd=ret_decay_bwd, ret_gn_gain=ret_gn_gain,
             attn_sink=attn_sink, w_out=w_out, ln1_gain=ln1_gain, ln1_bias=ln1_bias, w_ffn_gate=w_ffn_gate,
             w_ffn_up=w_ffn_up, w_ffn_down=w_ffn_down, w_ple_proj=w_ple_proj, w_ple_gate=w_ple_gate,
             ln2_gain=ln2_gain, ln2_bias=ln2_bias)
    m = dict(w_in=m_w_in, ret_decay_fwd=m_ret_decay_fwd, ret_decay_bwd=m_ret_decay_bwd, ret_gn_gain=m_ret_gn_gain,
             attn_sink=m_attn_sink, w_out=m_w_out, ln1_gain=m_ln1_gain, ln1_bias=m_ln1_bias, w_ffn_gate=m_w_ffn_gate,
             w_ffn_up=m_w_ffn_up, w_ffn_down=m_w_ffn_down, w_ple_proj=m_w_ple_proj, w_ple_gate=m_w_ple_gate,
             ln2_gain=m_ln2_gain, ln2_bias=m_ln2_bias)
    v = dict(w_in=v_w_in, ret_decay_fwd=v_ret_decay_fwd, ret_decay_bwd=v_ret_decay_bwd, ret_gn_gain=v_ret_gn_gain,
             attn_sink=v_attn_sink, w_out=v_w_out, ln1_gain=v_ln1_gain, ln1_bias=v_ln1_bias, w_ffn_gate=v_w_ffn_gate,
             w_ffn_up=v_w_ffn_up, w_ffn_down=v_w_ffn_down, w_ple_proj=v_w_ple_proj, w_ple_gate=v_w_ple_gate,
             ln2_gain=v_ln2_gain, ln2_bias=v_ln2_bias)
    big = lambda d: [_shard_rows(n, d[n]) for n in BIG_NAMES]
    small = lambda d: {n: d[n] for n in SMALL_NAMES}

    chip = 2 * lax.axis_index("x") + lax.axis_index("y")
    pos = jnp.stack([chip, lax.axis_index("c")]).astype(jnp.int32)

    shards = big(w)
    w_in4, rest16 = _gather_and_cast(shards[0].astype(BF16), shards[1:])
    w_in_t = w_in4.reshape(IN_WIDTH, D_MODEL)
    grad_x, sums32, from_chips, (g_s, d_s, m_s, v_s) = _local_step(
        x, p[0], loss_target, w_in_t, rest16, small(w), pos=pos,
        small_state=(_pack_small(small(w)), _pack_small(small(m)), _pack_small(small(v))))
    g_big, d_big, m_big, v_big = _adamw(big(w), _add_chips(sums32, from_chips, pos), big(m), big(v))

    def tree(bigs, smalls):
        out = {n: _unshard_rows(n, a) for n, a in zip(BIG_NAMES, bigs)}
        out.update(smalls)
        return [out[n] for n in WEIGHT_ORDER]

    smalls, loss = _unpack_small([g_s, d_s, m_s, v_s])
    return (loss.reshape(()), grad_x,
            *(a for bigs, s in zip((g_big, d_big, m_big, v_big), smalls) for a in tree(bigs, s)))
```

```python
import functools
import math

import jax
import jax.numpy as jnp
from jax import lax
from jax.experimental import pallas as pl
from jax.experimental.pallas import tpu as pltpu

F32 = jnp.float32
BF16 = jnp.bfloat16

D_MODEL = 1024
HEAD_DIM = 64
RET_HEADS = 8
ATTN_HEADS = 8
RET_WIDTH = 512
ATTN_WIDTH = 512
KV_WIDTH = 128
IN_WIDTH = 2816
FFN = 2816
N_SHARD = 4
FFN_SHARD = FFN // N_SHARD
PLE_DIM = 256
CHUNK = 128
LANES = 128
ALPHA = 2.0 ** 0.25
LN_EPS = 1e-5
GN_EPS = 1e-5
NEG_INF = -1e30
ADAM_LR = 0.001
ADAM_B1 = 0.9
ADAM_B2 = 0.999
ADAM_EPS = 1e-08
ADAM_WD = 0.01
ADAM_STEP = 10
VMEM_LIMIT = 56 * 1024 * 1024
MESH = pl.DeviceIdType.MESH

CB_RQ, CB_RK, CB_RV, CB_RG, CB_AQ, CB_AK, CB_AV = 0, 4, 8, 12, 16, 20, 21


def _dot(a, b):
    return jnp.dot(a, b, preferred_element_type=F32)


def _dot_nt(a, b):
    return lax.dot_general(a, b, (((1,), (1,)), ((), ())), preferred_element_type=F32)


def _dot_tn(a, b):
    return lax.dot_general(a, b, (((0,), (0,)), ((), ())), preferred_element_type=F32)


def _sigmoid(x):
    return 1.0 / (1.0 + jnp.exp(-x))


def _params(*sem, vmem=None):
    return pltpu.CompilerParams(dimension_semantics=tuple(sem) if sem else None, vmem_limit_bytes=vmem)


class _Rider:
    def __init__(self, ins, out_shapes, sems, start, finish, aliases=None):
        self.ins, self.out_shapes, self.sems = list(ins), list(out_shapes), list(sems)
        self.start, self.finish, self.aliases = start, finish, dict(aliases or {})


def _merge_riders(riders):
    riders = [r for r in riders if r is not None]
    if len(riders) == 1:
        return riders[0]
    bounds, aliases = [], {}
    i0 = o0 = s0 = 0
    for r in riders:
        bounds.append((i0, o0, s0))
        aliases.update({i0 + i: o0 + o for i, o in r.aliases.items()})
        i0, o0, s0 = i0 + len(r.ins), o0 + len(r.out_shapes), s0 + len(r.sems)

    def each(method):
        def run(ins, outs, sems):
            for r, (i, o, s) in zip(riders, bounds):
                getattr(r, method)(ins[i:i + len(r.ins)], outs[o:o + len(r.out_shapes)], sems[s:s + len(r.sems)])
        return run

    return _Rider([a for r in riders for a in r.ins], [a for r in riders for a in r.out_shapes],
                  [a for r in riders for a in r.sems], each("start"), each("finish"), aliases)


def _hosted_call(body, name, grid, in_specs, out_specs, out_shape, scratch_shapes, operands, rider=None,
                 semantics=None):
    n_in, n_out, n_scr = len(in_specs), len(out_specs), len(scratch_shapes)
    if rider is None:
        return pl.pallas_call(
            body, name=name, grid=grid, in_specs=in_specs, out_specs=out_specs, out_shape=out_shape,
            scratch_shapes=scratch_shapes,
            compiler_params=_params(*(semantics or ["parallel"] * len(grid)), vmem=VMEM_LIMIT))(*operands)
    r_in, r_out = len(rider.ins), len(rider.out_shapes)

    def full_body(*refs):
        main_in, rin = refs[:n_in], refs[n_in:n_in + r_in]
        o0 = n_in + r_in
        main_out, rout = refs[o0:o0 + n_out], refs[o0 + n_out:o0 + n_out + r_out]
        s0 = o0 + n_out + r_out
        main_scr, rsem = refs[s0:s0 + n_scr], refs[s0 + n_scr:]
        first = functools.reduce(jnp.logical_and, [pl.program_id(a) == 0 for a in range(len(grid))])
        last = functools.reduce(jnp.logical_and, [pl.program_id(a) == g - 1 for a, g in enumerate(grid)])

        @pl.when(first)
        def _():
            rider.start(rin, rout, rsem)

        body(*main_in, *main_out, *main_scr)

        @pl.when(last)
        def _():
            rider.finish(rin, rout, rsem)

    hbm = pl.BlockSpec(memory_space=pl.ANY)
    return pl.pallas_call(
        full_body, name=name, grid=grid,
        in_specs=list(in_specs) + [hbm] * r_in, out_specs=list(out_specs) + [hbm] * r_out,
        out_shape=list(out_shape) + rider.out_shapes,
        scratch_shapes=list(scratch_shapes) + rider.sems,
        input_output_aliases={n_in + i: n_out + o for i, o in rider.aliases.items()},
        compiler_params=_params(*(["arbitrary"] * len(grid)), vmem=VMEM_LIMIT),
    )(*operands, *rider.ins)


def _loop_grouped(n, body, init, per_trip=2):
    if n % per_trip:
        return lax.fori_loop(0, n, body, init)

    def trip(i, c):
        for j in range(per_trip):
            c = body(per_trip * i + j, c)
        return c

    return lax.fori_loop(0, n // per_trip, trip, init)


def _head_mean(x, m0):
    s0 = jnp.sum(jnp.where(m0, x, 0.0), axis=1, keepdims=True)
    s1 = jnp.sum(jnp.where(m0, 0.0, x), axis=1, keepdims=True)
    return jnp.where(m0, s0, s1) * (1.0 / HEAD_DIM)


def _inproj(x2d, w_in_t, rider=None):
    t = x2d.shape[0]
    tm = 512
    nb = 256

    def body(x_ref, w_ref, o_ref):
        xb = x_ref[...].astype(BF16)
        for n in range(0, IN_WIDTH, nb):
            o_ref[:, n:n + nb] = _dot_nt(xb, w_ref[n:n + nb, :]).astype(BF16)

    return _hosted_call(
        body, "inproj", (t // tm,),
        in_specs=[pl.BlockSpec((tm, D_MODEL), lambda i: (i, 0)),
                  pl.BlockSpec((IN_WIDTH, D_MODEL), lambda i: (0, 0))],
        out_specs=[pl.BlockSpec((tm, IN_WIDTH), lambda i: (i, 0))],
        out_shape=[jax.ShapeDtypeStruct((t, IN_WIDTH), BF16)],
        scratch_shapes=[], operands=(x2d, w_in_t), rider=rider)


def _outproj_ln1(y_ret, y_att, x2d, w_out, gain, bias, rider=None):
    t = x2d.shape[0]
    tm = 512

    def body(yr_ref, ya_ref, x_ref, w_ref, g_ref, b_ref, zh_ref, r_ref, hb_ref):
        mix = _dot(yr_ref[...], w_ref[0:RET_WIDTH, :]) + _dot(ya_ref[...], w_ref[RET_WIDTH:, :])
        z = ALPHA * x_ref[...] + mix
        mu = jnp.mean(z, axis=1, keepdims=True)
        zc = z - mu
        var = jnp.mean(zc * zc, axis=1, keepdims=True)
        r = lax.rsqrt(var + LN_EPS)
        zh = zc * r
        zh_ref[...] = zh
        r_ref[...] = r
        hb_ref[...] = (zh * g_ref[...] + b_ref[...]).astype(BF16)

    row = lambda w: pl.BlockSpec((tm, w), lambda i: (i, 0))
    const = lambda s: pl.BlockSpec(s, lambda i: (0, 0))
    return _hosted_call(
        body, "outproj_ln1", (t // tm,),
        in_specs=[row(RET_WIDTH), row(ATTN_WIDTH), row(D_MODEL), const((D_MODEL, D_MODEL)),
                  const((1, D_MODEL)), const((1, D_MODEL))],
        out_specs=[row(D_MODEL), row(1), row(D_MODEL)],
        out_shape=[jax.ShapeDtypeStruct((t, D_MODEL), F32), jax.ShapeDtypeStruct((t, 1), F32),
                   jax.ShapeDtypeStruct((t, D_MODEL), BF16)],
        scratch_shapes=[], operands=(y_ret, y_att, x2d, w_out, gain, bias), rider=rider)


FFN_CHUNK = 256
N_FFN_CHUNK = FFN // FFN_CHUNK


def _ln2_loss_tail(zh, mixed, tgt, g1, b1, g2, b2):
    z2 = ALPHA * (zh * g1 + b1) + mixed
    mu = jnp.mean(z2, axis=1, keepdims=True)
    zc = z2 - mu
    var = jnp.mean(zc * zc, axis=1, keepdims=True)
    r = lax.rsqrt(var + LN_EPS)
    zh2 = zc * r
    err = zh2 * g2 + b2 - tgt
    dy = err * (1.0 / D_MODEL)
    dzh = dy * g2
    m1 = jnp.mean(dzh, axis=1, keepdims=True)
    m2 = jnp.mean(dzh * zh2, axis=1, keepdims=True)
    dz2 = r * (dzh - m1 - zh2 * m2)
    return dz2, jnp.sum(err * err), jnp.sum(dy * zh2, axis=0, keepdims=True), jnp.sum(dy, axis=0, keepdims=True)


def _ffn_fwd(zh1, hb, p2d, tgt, g1, b1, g2, b2, wg4, wu4, wd4, wpe, wpg):
    t = zh1.shape[0]
    tm = 256
    wg_t, wu_t, wd_all = (w.reshape(FFN, D_MODEL) for w in (wg4, wu4, wd4))

    def body(zh_ref, hb_ref, p_ref, t_ref, g1_ref, b1_ref, g2_ref, b2_ref,
             wg_hbm, wu_hbm, wd_hbm, wpe_hbm, wpg_hbm,
             dz_ref, dzb_ref, gs_ref, us_ref, act_ref, pg_ref, ple_ref, loss_ref, dg2_ref, db2_ref,
             wg, wu, wd, wpe, wpg, wsem):
        step = pl.program_id(0)
        pc = D_MODEL // N_SHARD
        chunks = [slice(n * FFN_CHUNK, (n + 1) * FFN_CHUNK) for n in range(N_FFN_CHUNK)]
        rows = lambda ref, n: ref.at[pl.ds(n * FFN_CHUNK, FFN_CHUNK), :]
        loads = [(rows(src, n), rows(dst, n)) for n in range(N_FFN_CHUNK)
                 for src, dst in ((wg_hbm, wg), (wu_hbm, wu), (wd_hbm, wd))]
        loads += [(wpe_hbm.at[j], wpe.at[:, pl.ds(j * pc, pc)]) for j in range(N_SHARD)]
        loads += [(wpg_hbm, wpg)]
        copies = [pltpu.make_async_copy(src, dst, wsem.at[i]) for i, (src, dst) in enumerate(loads)]

        def tile(first):
            def arrived(lo, hi):
                if first:
                    for cp in copies[lo:hi]:
                        cp.wait()

            hbv = hb_ref[...]
            ffn = jnp.zeros((tm, D_MODEL), F32)
            acts = []
            for n in range(N_FFN_CHUNK + 1):
                if n < N_FFN_CHUNK:
                    arrived(3 * n, 3 * n + 2)
                    gj = _dot_nt(hbv, wg[chunks[n], :])
                    uj = _dot_nt(hbv, wu[chunks[n], :])
                    gs_ref[:, chunks[n]] = gj.astype(BF16)
                    us_ref[:, chunks[n]] = uj.astype(BF16)
                    acts.append((gj * _sigmoid(gj) * uj).astype(BF16))
                    act_ref[:, chunks[n]] = acts[n]
                if n > 0:
                    arrived(3 * n - 1, 3 * n)
                    ffn = ffn + _dot(acts[n - 1], wd[chunks[n - 1], :])
            arrived(3 * N_FFN_CHUNK, len(copies))
            ple = _dot(p_ref[...].astype(BF16), wpe[...])
            pg = _sigmoid(_dot(hbv, wpg[...]))
            pg_ref[...] = pg.astype(BF16)
            ple_ref[...] = ple.astype(BF16)
            dz2, sq, dg2, db2 = _ln2_loss_tail(zh_ref[...], ffn + pg * ple, t_ref[...], g1_ref[...], b1_ref[...],
                                               g2_ref[...], b2_ref[...])
            dz_ref[...] = dz2
            dzb_ref[...] = dz2.astype(BF16)
            if first:
                loss_ref[...] = jnp.zeros_like(loss_ref) + sq
                dg2_ref[...] = dg2
                db2_ref[...] = db2
            else:
                loss_ref[...] += sq
                dg2_ref[...] += dg2
                db2_ref[...] += db2

        @pl.when(step == 0)
        def _():
            for cp in copies:
                cp.start()
            tile(True)

        @pl.when(step != 0)
        def _():
            tile(False)

    row = lambda w: pl.BlockSpec((tm, w), lambda i: (i, 0))
    const = lambda s: pl.BlockSpec(s, lambda i: (0, 0))
    hid_shape = jax.ShapeDtypeStruct((t, FFN), BF16)
    hbm = pl.BlockSpec(memory_space=pl.ANY)
    return pl.pallas_call(
        body, name="ffn_fwd", grid=(t // tm,),
        in_specs=[row(D_MODEL), row(D_MODEL), row(PLE_DIM), row(D_MODEL),
                  const((1, D_MODEL)), const((1, D_MODEL)), const((1, D_MODEL)), const((1, D_MODEL)),
                  hbm, hbm, hbm, hbm, hbm],
        out_specs=[row(D_MODEL), row(D_MODEL), row(FFN), row(FFN), row(FFN), row(D_MODEL), row(D_MODEL),
                   const((8, LANES)), const((1, D_MODEL)), const((1, D_MODEL))],
        out_shape=[jax.ShapeDtypeStruct((t, D_MODEL), F32), jax.ShapeDtypeStruct((t, D_MODEL), BF16),
                   hid_shape, hid_shape, hid_shape,
                   jax.ShapeDtypeStruct((t, D_MODEL), BF16), jax.ShapeDtypeStruct((t, D_MODEL), BF16),
                   jax.ShapeDtypeStruct((8, LANES), F32),
                   jax.ShapeDtypeStruct((1, D_MODEL), F32), jax.ShapeDtypeStruct((1, D_MODEL), F32)],
        scratch_shapes=[pltpu.VMEM((FFN, D_MODEL), BF16), pltpu.VMEM((FFN, D_MODEL), BF16),
                        pltpu.VMEM((FFN, D_MODEL), BF16),
                        pltpu.VMEM((PLE_DIM, D_MODEL), BF16), pltpu.VMEM(wpg.shape, BF16),
                        pltpu.SemaphoreType.DMA((3 * N_FFN_CHUNK + N_SHARD + 1,))],
        compiler_params=_params("arbitrary", vmem=VMEM_LIMIT),
    )(zh1, hb, p2d, tgt, g1, b1, g2, b2, wg_t, wu_t, wd_all, wpe, wpg)


def _ret_tables(lgf, lgb):
    c = CHUNK
    row = lax.broadcasted_iota(jnp.int32, (c, LANES), 0).astype(F32)
    ii = lax.broadcasted_iota(jnp.int32, (c, c), 0).astype(F32)
    jj = lax.broadcasted_iota(jnp.int32, (c, c), 1).astype(F32)
    diff = ii - jj
    dmats = []
    for h in range(2):
        lf = lgf[:, h * HEAD_DIM:h * HEAD_DIM + 1]
        lb = lgb[:, h * HEAD_DIM:h * HEAD_DIM + 1]
        dmats.append(jnp.where(diff > 0, jnp.exp(lf * jnp.maximum(diff, 0.0)),
                               jnp.where(diff < 0, jnp.exp(lb * jnp.maximum(-diff, 0.0)), 2.0)))
    tab = dict(
        qdec_f=jnp.exp(lgf * (row + 1.0)), kdec_f=jnp.exp(lgf * (c - 1.0 - row)),
        qdec_b=jnp.exp(lgb * (c - row)), kdec_b=jnp.exp(lgb * row),
        cdec_f=jnp.exp(lgf * c), cdec_b=jnp.exp(lgb * c),
        d0=dmats[0], d1=dmats[1], row=row, diff=diff)
    r = lax.broadcasted_iota(jnp.int32, (LANES, LANES), 0) < HEAD_DIM
    cc = lax.broadcasted_iota(jnp.int32, (LANES, LANES), 1) < HEAD_DIM
    tab["bd"] = r == cc
    tab["m0"] = lax.broadcasted_iota(jnp.int32, (c, LANES), 1) < HEAD_DIM
    return tab


def _ret_specs(bsz, s):
    blk = lambda cb: pl.BlockSpec((bsz, s, LANES), lambda p, cb=cb: (0, 0, cb + p))
    lane = pl.BlockSpec((None, 1, LANES), lambda p: (p, 0, 0))
    gain = pl.BlockSpec((1, LANES), lambda p: (0, p))
    pair = pl.BlockSpec((bsz, s, LANES), lambda p: (0, 0, p))
    return blk, lane, gain, pair


def _ret_state_spec(bsz, n_chunk):
    spec = pl.BlockSpec((None, bsz, n_chunk, LANES, LANES), lambda p: (p, 0, 0, 0, 0))
    return spec, jax.ShapeDtypeStruct((4, bsz, n_chunk, LANES, LANES), F32)


def _ret_kv_states(tb, k_ref, v_ref, rb_ref, kvf_ref, n_chunk):
    c = CHUNK
    bsz = k_ref.shape[0]
    bd = tb["bd"]

    def contributions(n, carry):
        sl = pl.ds(pl.multiple_of(n * c, c), c)
        kfb = []
        for b in range(bsz):
            k32 = k_ref[b, sl, :].astype(F32)
            kfb.append(jnp.concatenate([k32 * tb["kdec_f"], k32 * tb["kdec_b"]], axis=1).astype(BF16))
        kvs = [_dot_tn(kfb[b], v_ref[b, sl, :]) for b in range(bsz)]
        for b in range(bsz):
            kvf_ref[b, n] = jnp.where(bd, kvs[b][0:LANES], 0.0)
            rb_ref[b, n] = jnp.where(bd, kvs[b][LANES:], 0.0)
        return carry

    lax.fori_loop(0, n_chunk, contributions, 0, unroll=2)

    def recur(i, rbs):
        n = n_chunk - 1 - i
        new = []
        for b in range(bsz):
            own = rb_ref[b, n]
            rb_ref[b, n] = rbs[b]
            new.append(rbs[b] * tb["cdec_b"] + own)
        return tuple(new)

    lax.fori_loop(0, n_chunk, recur, tuple(jnp.zeros((LANES, LANES), F32) for _ in range(bsz)))


def _split_rows(x, m0):
    return jnp.concatenate([jnp.where(m0, x, 0.0), jnp.where(m0, 0.0, x)], axis=0).astype(BF16)


def _ret_fwd(u3, lgf_l, lgb_l, gn_gain, rider=None):
    bsz, s, _ = u3.shape
    n_chunk = s // CHUNK
    c = CHUNK

    def body(q_ref, k_ref, v_ref, g_ref, lgf_ref, lgb_ref, gain_ref, yh_ref, rstd_ref, o_ref, rb_ref, kvf_ref):
        tb = _ret_tables(lgf_ref[...], lgb_ref[...])
        m0 = tb["m0"]
        gain = gain_ref[...]
        rows = range(bsz)
        _ret_kv_states(tb, k_ref, v_ref, rb_ref, kvf_ref, n_chunk)

        def chunk(n, rfs):
            sl = pl.ds(pl.multiple_of(n * c, c), c)
            qs = [q_ref[b, sl, :].astype(F32) * 0.125 for b in rows]
            s01 = [_dot_nt(_split_rows(qs[b], m0), k_ref[b, sl, :]) for b in rows]
            ys = []
            for b in rows:
                lhs = jnp.concatenate([s01[b][0:c] * tb["d0"], s01[b][c:] * tb["d1"],
                                       qs[b] * tb["qdec_f"], qs[b] * tb["qdec_b"]], axis=1).astype(BF16)
                rhs = jnp.concatenate([_split_rows(v_ref[b, sl, :].astype(F32), m0),
                                       rfs[b].astype(BF16), rb_ref[b, n].astype(BF16)], axis=0)
                ys.append(_dot(lhs, rhs))
            new = []
            for b in rows:
                y = ys[b]
                mu = _head_mean(y, m0)
                yc = y - mu
                rstd = lax.rsqrt(_head_mean(yc * yc, m0) + GN_EPS)
                yh = yc * rstd
                g = g_ref[b, sl, :].astype(F32)
                yh_ref[b, sl, :] = yh
                rstd_ref[b, sl, :] = rstd
                o_ref[b, sl, :] = (yh * gain * (g * _sigmoid(g))).astype(BF16)
                new.append(rfs[b] * tb["cdec_f"] + kvf_ref[b, n])
            return tuple(new)

        _loop_grouped(n_chunk, chunk, tuple(jnp.zeros((LANES, LANES), F32) for _ in rows))

    blk, lane, gain, pair = _ret_specs(bsz, s)
    state, state_shape = _ret_state_spec(bsz, n_chunk)
    return _hosted_call(
        body, "ret_fwd", (4,),
        in_specs=[blk(CB_RQ), blk(CB_RK), blk(CB_RV), blk(CB_RG), lane, lane, gain],
        out_specs=[pair, pair, pair, state, state],
        out_shape=[jax.ShapeDtypeStruct((bsz, s, RET_WIDTH), F32), jax.ShapeDtypeStruct((bsz, s, RET_WIDTH), F32),
                   jax.ShapeDtypeStruct((bsz, s, RET_WIDTH), BF16), state_shape, state_shape],
        scratch_shapes=[],
        operands=(u3, u3, u3, u3, lgf_l, lgb_l, gn_gain), rider=rider)


def _ret_bwd(u3, y_hat, y_rstd, states, d_o, lgf_l, lgb_l, gn_gain, rider=None):
    bsz, s, _ = u3.shape
    n_chunk = s // CHUNK
    c = CHUNK

    def body(q_ref, k_ref, v_ref, g_ref, yh_ref, rstd_ref, do_ref, lgf_ref, lgb_ref, gain_ref, rb_ref, kvf_ref,
             dq_ref, dk_ref, dv_ref, dg_ref, part_ref,
             rf_ref, dirf_ref, dy_ref, dk_acc, dv_acc, pa0, pa1, vec_ref):
        tb = _ret_tables(lgf_ref[...], lgb_ref[...])
        m0, bd, row = tb["m0"], tb["bd"], tb["row"]
        gain = gain_ref[...]
        wf = jnp.maximum(tb["diff"], 0.0)
        wb = jnp.maximum(-tb["diff"], 0.0)
        rows = range(bsz)
        zero_states = tuple(jnp.zeros((LANES, LANES), F32) for _ in rows)
        for ref in (pa0, pa1):
            ref[...] = jnp.zeros_like(ref)
        vec_ref[...] = jnp.zeros_like(vec_ref)

        def sweep_fwd(n, carry):
            rfs, gbs = carry
            sl = pl.ds(pl.multiple_of(n * c, c), c)
            qs, ks, vs, dys, dybs, q01, k01, dy01 = [], [], [], [], [], [], [], []
            dgain = jnp.zeros((1, LANES), F32)
            for b in rows:
                q = q_ref[b, sl, :].astype(F32) * 0.125
                k = k_ref[b, sl, :]
                yh = yh_ref[b, sl, :]
                rstd = rstd_ref[b, sl, :]
                do = do_ref[b, sl, :].astype(F32)
                g = g_ref[b, sl, :].astype(F32)
                sg = _sigmoid(g)
                sil = g * sg
                dyh = do * gain * sil
                dg_ref[b, sl, :] = (do * yh * gain * sg * (1.0 + g * (1.0 - sg))).astype(BF16)
                dgain = dgain + jnp.sum(do * yh * sil, axis=0, keepdims=True)
                dy = rstd * (dyh - _head_mean(dyh, m0) - yh * _head_mean(dyh * yh, m0))
                dyb = dy.astype(BF16)
                dy_ref[b, sl, :] = dyb
                rf_ref[b, n] = rfs[b]
                qs.append(q)
                ks.append(k)
                vs.append(v_ref[b, sl, :])
                dys.append(dy)
                dybs.append(dyb)
                q01.append(_split_rows(q, m0))
                k01.append(_split_rows(k.astype(F32), m0))
                dy01.append(_split_rows(dy, m0))
            s01 = [_dot_nt(q01[b], ks[b]) for b in rows]
            da01 = [_dot_nt(dy01[b], vs[b]) for b in rows]
            rbn = [rb_ref[b, n] for b in rows]
            states = [jnp.concatenate([rfs[b], rbn[b]], axis=0).astype(BF16) for b in rows]
            dqc = [_dot_nt(dybs[b], states[b]) for b in rows]
            gbb = [gbs[b].astype(BF16) for b in rows]
            dkb = [_dot_nt(vs[b], gbb[b]) for b in rows]
            qfb = [jnp.concatenate([qs[b] * tb["qdec_f"], qs[b] * tb["qdec_b"]], axis=1) for b in rows]
            direct = [_dot_tn(qfb[b].astype(BF16), dybs[b]) for b in rows]
            ds_cat, ds_rows, a_rows = [], [], []
            for b in rows:
                a0 = s01[b][0:c] * tb["d0"]
                a1 = s01[b][c:] * tb["d1"]
                pa0[...] += da01[b][0:c] * a0
                pa1[...] += da01[b][c:] * a1
                ds0 = da01[b][0:c] * tb["d0"]
                ds1 = da01[b][c:] * tb["d1"]
                ds_cat.append(jnp.concatenate([ds0, ds1], axis=1).astype(BF16))
                ds_rows.append(jnp.concatenate([ds0, ds1], axis=0).astype(BF16))
                a_rows.append(jnp.concatenate([a0, a1], axis=0).astype(BF16))
            kbd = [ks[b].astype(F32) * tb["kdec_b"] for b in rows]
            dq_in = [_dot(ds_cat[b], k01[b]) for b in rows]
            dk_in = [_dot_tn(ds_rows[b], q01[b]) for b in rows]
            dv_in = [_dot_tn(a_rows[b], dy01[b]) for b in rows]
            dv_gb = [_dot(kbd[b].astype(BF16), gbb[b]) for b in rows]
            new_rf, new_gb = [], []
            dlf = jnp.zeros((1, LANES), F32)
            dlb = jnp.zeros((1, LANES), F32)
            for b in rows:
                dqf, dqb = dqc[b][:, 0:LANES], dqc[b][:, LANES:]
                qf, qb = qfb[b][:, 0:LANES], qfb[b][:, LANES:]
                dq = dq_in[b] + dqf * tb["qdec_f"] + dqb * tb["qdec_b"]
                dq_ref[b, sl, :] = (dq * 0.125).astype(BF16)
                dk_acc[b, sl, :] = dk_in[b] + dkb[b] * tb["kdec_b"]
                dv_acc[b, sl, :] = dv_in[b] + dv_gb[b]
                dlf = dlf + jnp.sum((row + 1.0) * qf * dqf, axis=0, keepdims=True)
                dlb = dlb + jnp.sum((c - row) * qb * dqb + row * kbd[b] * dkb[b], axis=0, keepdims=True)
                dlb = dlb + c * tb["cdec_b"] * jnp.sum(gbs[b] * rbn[b], axis=0, keepdims=True)
                dirf_ref[b, n] = jnp.where(bd, direct[b][0:LANES], 0.0)
                new_gb.append(jnp.where(bd, direct[b][LANES:], 0.0) + tb["cdec_b"] * gbs[b])
                new_rf.append(rfs[b] * tb["cdec_f"] + kvf_ref[b, n])
            vec_ref[0:1, :] += dlf
            vec_ref[1:2, :] += dlb
            vec_ref[6:7, :] += dgain
            return tuple(new_rf), tuple(new_gb)

        _loop_grouped(n_chunk, sweep_fwd, (zero_states, zero_states), per_trip=4)

        def sweep_bwd(i, gfs):
            n = n_chunk - 1 - i
            sl = pl.ds(pl.multiple_of(n * c, c), c)
            gfb = [gfs[b].astype(BF16) for b in rows]
            kfd = [k_ref[b, sl, :].astype(F32) * tb["kdec_f"] for b in rows]
            dkf = [_dot_nt(v_ref[b, sl, :], gfb[b]) for b in rows]
            dvf = [_dot(kfd[b].astype(BF16), gfb[b]) for b in rows]
            new = []
            dlf = jnp.zeros((1, LANES), F32)
            for b in rows:
                dk_ref[b, sl, :] = (dk_acc[b, sl, :] + dkf[b] * tb["kdec_f"]).astype(BF16)
                dv_ref[b, sl, :] = (dv_acc[b, sl, :] + dvf[b]).astype(BF16)
                dlf = dlf + jnp.sum((c - 1.0 - row) * kfd[b] * dkf[b], axis=0, keepdims=True)
                dlf = dlf + c * tb["cdec_f"] * jnp.sum(gfs[b] * rf_ref[b, n], axis=0, keepdims=True)
                new.append(dirf_ref[b, n] + tb["cdec_f"] * gfs[b])
            vec_ref[0:1, :] += dlf
            return tuple(new)

        _loop_grouped(n_chunk, sweep_bwd, zero_states)
        vec_ref[2:3, :] = jnp.sum(pa0[...] * wf, axis=0, keepdims=True)
        vec_ref[3:4, :] = jnp.sum(pa1[...] * wf, axis=0, keepdims=True)
        vec_ref[4:5, :] = jnp.sum(pa0[...] * wb, axis=0, keepdims=True)
        vec_ref[5:6, :] = jnp.sum(pa1[...] * wb, axis=0, keepdims=True)
        part_ref[...] = vec_ref[...]

    blk, lane, gain, pair = _ret_specs(bsz, s)
    out_bf = jax.ShapeDtypeStruct((bsz, s, RET_WIDTH), BF16)
    state = pltpu.VMEM((bsz, n_chunk, LANES, LANES), F32)
    saved = _ret_state_spec(bsz, n_chunk)[0]
    return _hosted_call(
        body, "ret_bwd", (4,),
        in_specs=[blk(CB_RQ), blk(CB_RK), blk(CB_RV), blk(CB_RG), pair, pair, pair, lane, lane, gain, saved, saved],
        out_specs=[pair, pair, pair, pair, pl.BlockSpec((None, 8, LANES), lambda p: (p, 0, 0))],
        out_shape=[out_bf, out_bf, out_bf, out_bf, jax.ShapeDtypeStruct((4, 8, LANES), F32)],
        scratch_shapes=[state, state,
                        pltpu.VMEM((bsz, s, LANES), BF16), pltpu.VMEM((bsz, s, LANES), F32),
                        pltpu.VMEM((bsz, s, LANES), F32),
                        pltpu.VMEM((c, c), F32), pltpu.VMEM((c, c), F32), pltpu.VMEM((8, LANES), F32)],
        operands=(u3, u3, u3, u3, y_hat, y_rstd, d_o, lgf_l, lgb_l, gn_gain, *states), rider=rider)


def _attn_window_tables(n, s):
    qi = lax.broadcasted_iota(jnp.int32, (CHUNK, 3 * CHUNK), 0)
    kj = lax.broadcasted_iota(jnp.int32, (CHUNK, 3 * CHUNK), 1)
    dist = jnp.abs(kj - CHUNK - qi)
    kpos = n * CHUNK - CHUNK + kj
    valid = (dist <= CHUNK) & (kpos >= 0) & (kpos < s)
    return dist.astype(F32), valid


def _dup_kv_head(x, g):
    lane = lax.broadcasted_iota(jnp.int32, x.shape, 1)
    keep = (lane < HEAD_DIM) == (g == 0)
    xf = x.astype(F32)
    return jnp.where(keep, xf, pltpu.roll(xf, HEAD_DIM, 1))


def _attn_specs(s):
    q = pl.BlockSpec((None, s, 2 * LANES), lambda b, g: (b, 0, CB_AQ // 2 + g))
    k = pl.BlockSpec((None, s, LANES), lambda b, g: (b, 0, CB_AK))
    v = pl.BlockSpec((None, s, LANES), lambda b, g: (b, 0, CB_AV))
    grp = pl.BlockSpec((None, s, 2 * LANES), lambda b, g: (b, 0, g))
    smem = pl.BlockSpec(memory_space=pltpu.SMEM)
    return q, k, v, grp, smem


def _fill_padded(dst_ref, val, s):
    dst_ref[0:CHUNK, :] = jnp.zeros((CHUNK, LANES), dst_ref.dtype)
    dst_ref[CHUNK:CHUNK + s, :] = val.astype(dst_ref.dtype)
    dst_ref[CHUNK + s:2 * CHUNK + s, :] = jnp.zeros((CHUNK, LANES), dst_ref.dtype)


def _attn_probs(sc, slope, snk, dist, valid):
    sc = jnp.where(valid, sc - slope * dist, NEG_INF)
    m = jnp.maximum(jnp.max(sc, axis=1, keepdims=True), snk)
    e = jnp.exp(sc - m)
    es = jnp.exp(snk - m)
    inv = 1.0 / (jnp.sum(e, axis=1, keepdims=True) + es)
    return e * inv, es * inv


def _stack_heads(x2, m0):
    parts = []
    for pr in range(2):
        xp = x2[:, pr * LANES:(pr + 1) * LANES]
        parts += [jnp.where(m0, xp, 0.0), jnp.where(m0, 0.0, xp)]
    return jnp.concatenate(parts, axis=0).astype(BF16)


def _unstack_pair(x_all, pr, m0):
    return jnp.where(m0, x_all[(2 * pr) * CHUNK:(2 * pr + 1) * CHUNK], x_all[(2 * pr + 1) * CHUNK:(2 * pr + 2) * CHUNK])


def _attn_saved_specs(bsz, n_blk):
    specs = [pl.BlockSpec((None, None, n_blk, 4 * CHUNK, w), lambda b, g: (b, g, 0, 0, 0)) for w in (3 * CHUNK, 1)]
    shapes = [jax.ShapeDtypeStruct((bsz, 2, n_blk, 4 * CHUNK, 3 * CHUNK), BF16),
              jax.ShapeDtypeStruct((bsz, 2, n_blk, 4 * CHUNK, 1), F32)]
    return specs, shapes


def _attn_fwd(u3, slopes, sink, rider=None):
    bsz, s, _ = u3.shape
    n_blk = s // CHUNK

    def body(slope_ref, sink_ref, q_ref, k_ref, v_ref, o_ref, p_ref, ps_ref, kp_ref, vp_ref):
        g = pl.program_id(1)
        _fill_padded(kp_ref, _dup_kv_head(k_ref[...], g), s)
        _fill_padded(vp_ref, _dup_kv_head(v_ref[...], g), s)
        m0 = lax.broadcasted_iota(jnp.int32, (CHUNK, LANES), 1) < HEAD_DIM

        def blk(n, carry):
            r0 = pl.multiple_of(n * CHUNK, CHUNK)
            kw = kp_ref[pl.ds(r0, 3 * CHUNK), :]
            vw = vp_ref[pl.ds(r0, 3 * CHUNK), :]
            dist, valid = _attn_window_tables(n, s)
            q_all = _stack_heads(q_ref[pl.ds(r0, CHUNK), :].astype(F32) * 0.125, m0)
            sc_all = _dot_nt(q_all, kw)
            probs, sinks = [], []
            for i in range(4):
                p, ps = _attn_probs(sc_all[i * CHUNK:(i + 1) * CHUNK], slope_ref[g * 4 + i], sink_ref[g * 4 + i],
                                    dist, valid)
                probs.append(p.astype(BF16))
                sinks.append(ps)
            p_all = jnp.concatenate(probs, axis=0)
            p_ref[n] = p_all
            ps_ref[n] = jnp.concatenate(sinks, axis=0)
            out_all = _dot(p_all, vw)
            for pr in range(2):
                o_ref[pl.ds(r0, CHUNK), pr * LANES:(pr + 1) * LANES] = _unstack_pair(out_all, pr, m0).astype(BF16)
            return carry

        lax.fori_loop(0, n_blk, blk, 0, unroll=4)

    q, k, v, grp, smem = _attn_specs(s)
    saved_specs, saved_shapes = _attn_saved_specs(bsz, n_blk)
    return _hosted_call(
        body, "attn_fwd", (bsz, 2),
        in_specs=[smem, smem, q, k, v],
        out_specs=[grp] + saved_specs,
        out_shape=[jax.ShapeDtypeStruct((bsz, s, ATTN_WIDTH), BF16)] + saved_shapes,
        scratch_shapes=[pltpu.VMEM((s + 2 * CHUNK, LANES), BF16), pltpu.VMEM((s + 2 * CHUNK, LANES), BF16)],
        operands=(slopes, sink, u3, u3, u3), rider=rider)


def _attn_bwd(u3, d_o, probs, sink_probs, rider=None):
    bsz, s, _ = u3.shape
    n_blk = s // CHUNK

    def body(q_ref, k_ref, v_ref, do_ref, p_ref, ps_ref, dq_ref, dkv_ref, ds_ref,
             kp_ref, vp_ref, dk_acc, dv_acc):
        g = pl.program_id(1)
        _fill_padded(kp_ref, _dup_kv_head(k_ref[...], g), s)
        _fill_padded(vp_ref, _dup_kv_head(v_ref[...], g), s)
        dk_acc[...] = jnp.zeros_like(dk_acc)
        dv_acc[...] = jnp.zeros_like(dv_acc)
        m0 = lax.broadcasted_iota(jnp.int32, (CHUNK, LANES), 1) < HEAD_DIM

        def blk(n, dsink):
            r0 = pl.multiple_of(n * CHUNK, CHUNK)
            win = pl.ds(r0, 3 * CHUNK)
            kw = kp_ref[win, :]
            vw = vp_ref[win, :]
            q_all = _stack_heads(q_ref[pl.ds(r0, CHUNK), :].astype(F32) * 0.125, m0)
            do_all = _stack_heads(do_ref[pl.ds(r0, CHUNK), :].astype(F32), m0)
            p_all = p_ref[n]
            ps_all = ps_ref[n]
            dp_all = _dot_nt(do_all, vw)
            new_dsink, dscs = [], []
            for i in range(4):
                rows = slice(i * CHUNK, (i + 1) * CHUNK)
                p = p_all[rows].astype(F32)
                dp = dp_all[rows]
                delta = jnp.sum(p * dp, axis=1, keepdims=True)
                dscs.append((p * (dp - delta)).astype(BF16))
                dsh = jnp.sum(ps_all[rows] * delta, axis=0, keepdims=True)
                new_dsink.append(dsink[i] - jnp.broadcast_to(dsh, (1, LANES)))
            dsc_all = jnp.concatenate(dscs, axis=0)
            dq_all = _dot(dsc_all, kw)
            dk_acc[win, :] += _dot_tn(dsc_all, q_all)
            dv_acc[win, :] += _dot_tn(p_all, do_all)
            for pr in range(2):
                dq_ref[pl.ds(r0, CHUNK), pr * LANES:(pr + 1) * LANES] = (
                    _unstack_pair(dq_all, pr, m0) * 0.125).astype(BF16)
            return tuple(new_dsink)

        dsink = _loop_grouped(n_blk, blk, tuple(jnp.zeros((1, LANES), F32) for _ in range(4)), per_trip=4)
        dk = dk_acc[CHUNK:CHUNK + s, :]
        dv = dv_acc[CHUNK:CHUNK + s, :]
        lane = lax.broadcasted_iota(jnp.int32, (s, LANES), 1)
        fold = lambda a: a + pltpu.roll(a, HEAD_DIM, 1)
        dkv_ref[...] = jnp.where(lane < HEAD_DIM, fold(dk), fold(dv)).astype(BF16)
        ds_ref[...] = jnp.zeros_like(ds_ref)
        for i in range(4):
            ds_ref[i:i + 1, :] = dsink[i]

    q, k, v, grp, _ = _attn_specs(s)
    return _hosted_call(
        body, "attn_bwd", (bsz, 2),
        in_specs=[q, k, v, grp] + _attn_saved_specs(bsz, n_blk)[0],
        out_specs=[grp, pl.BlockSpec((None, s, LANES), lambda b, g: (b, 0, g)),
                   pl.BlockSpec((None, None, 8, LANES), lambda b, g: (b, g, 0, 0))],
        out_shape=[jax.ShapeDtypeStruct((bsz, s, ATTN_WIDTH), BF16), jax.ShapeDtypeStruct((bsz, s, 2 * LANES), BF16),
                   jax.ShapeDtypeStruct((bsz, 2, 8, LANES), F32)],
        scratch_shapes=[pltpu.VMEM((s + 2 * CHUNK, LANES), BF16), pltpu.VMEM((s + 2 * CHUNK, LANES), BF16),
                        pltpu.VMEM((s + 2 * CHUNK, LANES), F32), pltpu.VMEM((s + 2 * CHUNK, LANES), F32)],
        operands=(u3, u3, u3, d_o, probs, sink_probs), rider=rider)


def _ffn_bwd(dz2, gs, us, pg, ple, zh1, r1, g1, wg4, wu4, wd4, wpg, w_out):
    t = dz2.shape[0]
    tm = 256
    wg_t, wu_t, wd_all = (w.reshape(FFN, D_MODEL) for w in (wg4, wu4, wd4))

    def body(dz_ref, gs_ref, us_ref, pg_ref, ple_ref, zh_ref, r_ref, g1_ref,
             wg_hbm, wu_hbm, wd_hbm, wpg_hbm, wo_hbm,
             dgs_ref, dus_ref, dsp_ref, dple_ref, dz1_ref, dyr_ref, dya_ref, dg1_ref, db1_ref,
             wg, wu, wd, wpg, wo, wsem):
        step = pl.program_id(0)
        chunks = [slice(n * FFN_CHUNK, (n + 1) * FFN_CHUNK) for n in range(N_FFN_CHUNK)]
        rows = lambda ref, n: ref.at[pl.ds(n * FFN_CHUNK, FFN_CHUNK), :]
        loads = [(rows(src, n), rows(dst, n)) for n in range(N_FFN_CHUNK)
                 for src, dst in ((wd_hbm, wd), (wg_hbm, wg), (wu_hbm, wu))]
        loads += [(wpg_hbm, wpg), (wo_hbm, wo)]
        copies = [pltpu.make_async_copy(src, dst, wsem.at[i]) for i, (src, dst) in enumerate(loads)]

        def tile(first):
            def arrived(lo, hi):
                if first:
                    for cp in copies[lo:hi]:
                        cp.wait()

            dz = dz_ref[...]
            dzb = dz.astype(BF16)
            dh = ALPHA * dz
            pending = []
            for n in range(N_FFN_CHUNK + 1):
                if n < N_FFN_CHUNK:
                    arrived(3 * n, 3 * n + 1)
                    da = _dot_nt(dzb, wd[chunks[n], :])
                    gj = gs_ref[:, chunks[n]].astype(F32)
                    uj = us_ref[:, chunks[n]].astype(F32)
                    sg = _sigmoid(gj)
                    dgj = (da * uj * sg * (1.0 + gj * (1.0 - sg))).astype(BF16)
                    duj = (da * gj * sg).astype(BF16)
                    dgs_ref[:, chunks[n]] = dgj
                    dus_ref[:, chunks[n]] = duj
                    pending.append((dgj, duj))
                if n > 0:
                    arrived(3 * n - 2, 3 * n)
                    dgp, dup = pending[n - 1]
                    dh = dh + _dot(dgp, wg[chunks[n - 1], :]) + _dot(dup, wu[chunks[n - 1], :])
            pgv = pg_ref[...].astype(F32)
            plev = ple_ref[...].astype(F32)
            dple_ref[...] = (dz * pgv).astype(BF16)
            dsp = (dz * plev * pgv * (1.0 - pgv)).astype(BF16)
            dsp_ref[...] = dsp
            arrived(3 * N_FFN_CHUNK, len(copies))
            dh = dh + _dot_nt(dsp, wpg[...])
            zh = zh_ref[...]
            dg1, db1 = jnp.sum(dh * zh, axis=0, keepdims=True), jnp.sum(dh, axis=0, keepdims=True)
            if first:
                dg1_ref[...] = dg1
                db1_ref[...] = db1
            else:
                dg1_ref[...] += dg1
                db1_ref[...] += db1
            dzh = dh * g1_ref[...]
            m1 = jnp.mean(dzh, axis=1, keepdims=True)
            m2 = jnp.mean(dzh * zh, axis=1, keepdims=True)
            dz1 = r_ref[...] * (dzh - m1 - zh * m2)
            dz1_ref[...] = dz1
            dyc = _dot_nt(dz1.astype(BF16), wo[...])
            dyr_ref[...] = dyc[:, 0:RET_WIDTH].astype(BF16)
            dya_ref[...] = dyc[:, RET_WIDTH:].astype(BF16)

        @pl.when(step == 0)
        def _():
            for cp in copies:
                cp.start()
            tile(True)

        @pl.when(step != 0)
        def _():
            tile(False)

    row = lambda w: pl.BlockSpec((tm, w), lambda i: (i, 0))
    const = lambda s: pl.BlockSpec(s, lambda i: (0, 0))
    hbm = pl.BlockSpec(memory_space=pl.ANY)
    hid_shape = jax.ShapeDtypeStruct((t, FFN), BF16)
    return pl.pallas_call(
        body, name="ffn_bwd", grid=(t // tm,),
        in_specs=[row(D_MODEL), row(FFN), row(FFN), row(D_MODEL), row(D_MODEL), row(D_MODEL), row(1),
                  const((1, D_MODEL)), hbm, hbm, hbm, hbm, hbm],
        out_specs=[row(FFN), row(FFN), row(D_MODEL), row(D_MODEL), row(D_MODEL), row(RET_WIDTH), row(ATTN_WIDTH),
                   const((1, D_MODEL)), const((1, D_MODEL))],
        out_shape=[hid_shape, hid_shape, jax.ShapeDtypeStruct((t, D_MODEL), BF16),
                   jax.ShapeDtypeStruct((t, D_MODEL), BF16), jax.ShapeDtypeStruct((t, D_MODEL), F32),
                   jax.ShapeDtypeStruct((t, RET_WIDTH), BF16), jax.ShapeDtypeStruct((t, ATTN_WIDTH), BF16),
                   jax.ShapeDtypeStruct((1, D_MODEL), F32), jax.ShapeDtypeStruct((1, D_MODEL), F32)],
        scratch_shapes=[pltpu.VMEM((FFN, D_MODEL), BF16), pltpu.VMEM((FFN, D_MODEL), BF16),
                        pltpu.VMEM((FFN, D_MODEL), BF16),
                        pltpu.VMEM(wpg.shape, BF16), pltpu.VMEM(w_out.shape, BF16),
                        pltpu.SemaphoreType.DMA((3 * N_FFN_CHUNK + 2,))],
        compiler_params=_params("arbitrary", vmem=VMEM_LIMIT),
    )(dz2, gs, us, pg, ple, zh1, r1, g1, wg_t, wu_t, wd_all, wpg, w_out)


def _wgrad_misc(y_ret, y_att, dz1, hb, dsp, p2d, dple, rider=None):
    t = dz1.shape[0]
    tk = min(t, 512)
    pc = D_MODEL // N_SHARD

    def body(yr_ref, ya_ref, dz_ref, hb_ref, dsp_ref, p_ref, dple_ref, wo_ref, wpg_ref, wpe_ref):
        @pl.when(pl.program_id(0) == 0)
        def _():
            wo_ref[...] = jnp.zeros_like(wo_ref)
            wpg_ref[...] = jnp.zeros_like(wpg_ref)
            wpe_ref[...] = jnp.zeros_like(wpe_ref)

        dzb = dz_ref[...].astype(BF16)
        wo_ref[0:RET_WIDTH, :] += _dot_tn(yr_ref[...], dzb)
        wo_ref[RET_WIDTH:, :] += _dot_tn(ya_ref[...], dzb)
        wpg_ref[...] += _dot_tn(hb_ref[...], dsp_ref[...])
        dpe = _dot_tn(p_ref[...].astype(BF16), dple_ref[...])
        for j in range(N_SHARD):
            wpe_ref[j] += dpe[:, j * pc:(j + 1) * pc]

    row = lambda w: pl.BlockSpec((tk, w), lambda k: (k, 0))
    const = lambda s: pl.BlockSpec(s, lambda k: (0,) * len(s))
    return _hosted_call(
        body, "wgrad_misc", (t // tk,),
        in_specs=[row(RET_WIDTH), row(ATTN_WIDTH), row(D_MODEL), row(D_MODEL), row(D_MODEL), row(PLE_DIM),
                  row(D_MODEL)],
        out_specs=[const((D_MODEL, D_MODEL)), const((D_MODEL, D_MODEL)), const((N_SHARD, PLE_DIM, pc))],
        out_shape=[jax.ShapeDtypeStruct((D_MODEL, D_MODEL), F32), jax.ShapeDtypeStruct((D_MODEL, D_MODEL), F32),
                   jax.ShapeDtypeStruct((N_SHARD, PLE_DIM, pc), F32)],
        scratch_shapes=[], operands=(y_ret, y_att, dz1, hb, dsp, p2d, dple), rider=rider, semantics=["arbitrary"])


def _wgrad_ffn(acts, dgs, dus, hb, dz2b):
    t = dz2b.shape[0]
    tk = min(t, 512)
    nk = t // tk

    def body(act_ref, dg_ref, du_ref, hb_ref, dz_ref, og_ref, ou_ref, od_ref):
        @pl.when(pl.program_id(1) == 0)
        def _():
            og_ref[...] = jnp.zeros_like(og_ref)
            ou_ref[...] = jnp.zeros_like(ou_ref)
            od_ref[...] = jnp.zeros_like(od_ref)

        hbv = hb_ref[...]
        og_ref[...] += _dot_tn(dg_ref[...], hbv)
        ou_ref[...] += _dot_tn(du_ref[...], hbv)
        od_ref[...] += _dot_tn(act_ref[...], dz_ref[...])

    half = FFN // 2
    a_spec = pl.BlockSpec((tk, half), lambda j, k: (k, j))
    b_spec = pl.BlockSpec((tk, D_MODEL), lambda j, k: (k, 0))
    o_spec = pl.BlockSpec((half, D_MODEL), lambda j, k: (j, 0))
    o_shape = jax.ShapeDtypeStruct((FFN, D_MODEL), F32)
    outs = pl.pallas_call(
        body, name="wgrad_ffn", grid=(2, nk),
        in_specs=[a_spec, a_spec, a_spec, b_spec, b_spec],
        out_specs=[o_spec] * 3, out_shape=[o_shape] * 3,
        compiler_params=_params("parallel", "arbitrary", vmem=VMEM_LIMIT),
    )(acts, dgs, dus, hb, dz2b)
    return [o.reshape(N_SHARD, FFN_SHARD, D_MODEL) for o in outs]


KV_ORDER = (0, 128, 64, 192)


def _wgrad_in(pieces, x2d, rider=None):
    t = x2d.shape[0]
    tk = min(t, 512)
    nk = t // tk
    kv0 = CB_AK * LANES

    def body(p0, p1, p2, p3, p4, pkv, x_ref, o_ref):
        @pl.when(pl.program_id(0) == 0)
        def _():
            o_ref[...] = jnp.zeros_like(o_ref)

        xb = x_ref[...].astype(BF16)
        for i, ref in enumerate((p0, p1, p2, p3, p4)):
            o_ref[i * 512:(i + 1) * 512, :] += _dot_tn(ref[...], xb)
        dkv = _dot_tn(pkv[...], xb)
        for i, o in enumerate(KV_ORDER):
            o_ref[kv0 + o:kv0 + o + HEAD_DIM, :] += dkv[i * HEAD_DIM:(i + 1) * HEAD_DIM]

    row = lambda w: pl.BlockSpec((tk, w), lambda k: (k, 0))
    return _hosted_call(
        body, "wgrad_in", (nk,),
        in_specs=[row(512)] * 5 + [row(256), row(D_MODEL)],
        out_specs=[pl.BlockSpec((IN_WIDTH, D_MODEL), lambda k: (0, 0))],
        out_shape=[jax.ShapeDtypeStruct((IN_WIDTH, D_MODEL), F32)],
        scratch_shapes=[], operands=(*pieces, x2d), rider=rider, semantics=["arbitrary"])


def _inproj_bwd(dz1, pieces, w_in_t, rider=None):
    t = dz1.shape[0]
    tm = 512
    kv0 = CB_AK * LANES

    def body(dz_ref, p0, p1, p2, p3, p4, pkv, w_ref, o_ref):
        acc = ALPHA * dz_ref[...]
        for i, ref in enumerate((p0, p1, p2, p3, p4)):
            acc = acc + _dot(ref[...], w_ref[i * 512:(i + 1) * 512, :])
        w_kv = jnp.concatenate([w_ref[kv0 + o:kv0 + o + HEAD_DIM, :] for o in KV_ORDER], axis=0)
        o_ref[...] = acc + _dot(pkv[...], w_kv)

    row = lambda w: pl.BlockSpec((tm, w), lambda i: (i, 0))
    return _hosted_call(
        body, "inproj_bwd", (t // tm,),
        in_specs=[row(D_MODEL)] + [row(512)] * 5 + [row(256), pl.BlockSpec((IN_WIDTH, D_MODEL), lambda i: (0, 0))],
        out_specs=[row(D_MODEL)],
        out_shape=[jax.ShapeDtypeStruct((t, D_MODEL), F32)],
        scratch_shapes=[], operands=(dz1, *pieces, w_in_t), rider=rider)


def _coords():
    return lax.axis_index("x"), lax.axis_index("y"), lax.axis_index("c")


def _chip_of(x, y, rel):
    return (1 - x if rel & 2 else x), (1 - y if rel & 1 else y)


def _gather_and_cast(shard, others):
    near = _gather_near_rider([shard])
    relay = _gather_relay_rider(near.out_shapes, chained=True)
    pass_near = _gather_pass_rider(near.out_shapes, chained=True, rels=NEAR)
    pass_far = _gather_pass_rider(near.out_shapes, chained=True, rels=(3,))
    riders = [near, relay, pass_near, pass_far]
    no = len(others)

    def body(*refs):
        shard_ref, wide = refs[0], refs[1:1 + no]
        out_ref, narrow = refs[1 + no], refs[2 + no:2 + 2 * no]
        k = 2 + 2 * no
        vin, vout, (lsem, ssem) = refs[k:k + no], refs[k + no:k + 2 * no], refs[k + 2 * no:k + 2 * no + 2]
        k += 2 * no + 2
        sems = {}
        for r in riders:
            sems[id(r)] = refs[k:k + len(r.sems)]
            k += len(r.sems)
        run = lambda r, method: getattr(r, method)([shard_ref], [out_ref], sems[id(r)])
        loads = [pltpu.make_async_copy(wide[w], vin[w], lsem.at[w]) for w in range(no)]
        stores = [pltpu.make_async_copy(vout[w], narrow[w], ssem.at[w]) for w in range(no)]

        run(near, "start")
        for cp in loads:
            cp.start()
        for w in range(no):
            loads[w].wait()
            vout[w][...] = vin[w][...].astype(BF16)
            stores[w].start()
        run(near, "finish")
        run(relay, "start")
        run(pass_near, "start")
        run(relay, "finish")
        run(pass_far, "start")
        run(pass_near, "finish")
        run(pass_far, "finish")
        for cp in stores:
            cp.wait()

    hbm = pl.BlockSpec(memory_space=pl.ANY)
    dma = pltpu.SemaphoreType.DMA
    gathered, *cast = pl.pallas_call(
        body, name="gather_weights", in_specs=[hbm] * (1 + no), out_specs=[hbm] * (1 + no),
        out_shape=near.out_shapes + [jax.ShapeDtypeStruct(a.shape, BF16) for a in others],
        scratch_shapes=[pltpu.VMEM(a.shape, F32) for a in others] + [pltpu.VMEM(a.shape, BF16) for a in others]
        + [dma((no,)), dma((no,))] + [s for r in riders for s in r.sems],
        compiler_params=_params(vmem=VMEM_LIMIT),
    )(shard, *others)
    return gathered, cast


def _gather_half(outs, w, chip, cc):
    h = outs[w].shape[1] // 2
    return outs[w].at[chip, pl.ds(cc * h, h), :]


NEAR = (1, 2)


def _gather_near_rider(shards, rels=NEAR):
    nw, nr = len(shards), len(rels)

    def copies(ins, outs, sems, arrivals):
        send, recv, lsend, lrecv = sems
        x, y, c = _coords()
        me = 2 * x + y
        own = [pltpu.make_async_remote_copy(
            src_ref=ins[w], dst_ref=outs[w].at[me], send_sem=lsend.at[w], recv_sem=lrecv.at[w],
            device_id=(x, y, 1 - c), device_id_type=MESH) for w in range(nw)]
        out, arrive = [], []
        for i, rel in enumerate(rels):
            kx, ky = _chip_of(x, y, rel)
            for w in range(nw):
                h = shards[w].shape[0] // 2
                sem = dict(send_sem=send.at[w * nr + i], recv_sem=recv.at[w * nr + i],
                           device_id=(kx, ky, c), device_id_type=MESH)
                out.append(pltpu.make_async_remote_copy(
                    src_ref=ins[w].at[pl.ds(c * h, h), :], dst_ref=_gather_half(outs, w, me, c), **sem))
                if arrivals:
                    theirs = _gather_half(outs, w, 2 * kx + ky, c)
                    arrive.append(pltpu.make_async_remote_copy(src_ref=theirs, dst_ref=theirs, **sem))
        return own, out, arrive

    def start(ins, outs, sems):
        own, out, _ = copies(ins, outs, sems, arrivals=False)
        for cp in own + out:
            cp.start()

    def finish(ins, outs, sems):
        own, out, arrive = copies(ins, outs, sems, arrivals=True)
        for cp in arrive:
            cp.wait_recv()
        for cp in out:
            cp.wait_send()
        for cp in own:
            cp.wait()

    dma = pltpu.SemaphoreType.DMA
    return _Rider(shards, [jax.ShapeDtypeStruct((N_SHARD,) + s.shape, s.dtype) for s in shards],
                  [dma((nr * nw,)), dma((nr * nw,)), dma((nw,)), dma((nw,))], start, finish)


def _gather_relay_rider(gathered, chained=False):
    nw = len(gathered)

    def quarter(outs, w, chip, c, p):
        q = outs[w].shape[1] // 4
        return outs[w].at[chip, pl.ds(c * 2 * q + p * q, q), :]

    def copies(outs, sems):
        send, recv = sems
        x, y, c = _coords()
        (yx, yy), (xx, xy), (dx, dy) = (_chip_of(x, y, rel) for rel in (1, 2, 3))
        out, arrive = [], []
        for w in range(nw):
            for p, (src_chip, dst) in enumerate(((2 * xx + xy, (yx, yy)), (2 * yx + yy, (xx, xy)))):
                rows = quarter(outs, w, src_chip, c, p)
                sem = dict(send_sem=send.at[w * 2 + p], recv_sem=recv.at[w * 2 + p], device_id_type=MESH)
                out.append(pltpu.make_async_remote_copy(src_ref=rows, dst_ref=rows, device_id=(*dst, c), **sem))
                mine = quarter(outs, w, 2 * dx + dy, c, p)
                arrive.append(pltpu.make_async_remote_copy(src_ref=mine, dst_ref=mine, device_id=(*dst, c), **sem))
        return out, arrive

    def start(ins, outs, sems):
        for cp in copies(outs, sems)[0]:
            cp.start()

    def finish(ins, outs, sems):
        out, arrive = copies(outs, sems)
        for cp in arrive:
            cp.wait_recv()
        for cp in out:
            cp.wait_send()

    dma = pltpu.SemaphoreType.DMA
    shapes = [jax.ShapeDtypeStruct(g.shape, g.dtype) for g in gathered]
    if chained:
        return _Rider([], [], [dma((2 * nw,)), dma((2 * nw,))], start, finish)
    return _Rider(gathered, shapes, [dma((2 * nw,)), dma((2 * nw,))], start, finish,
                  aliases={w: w for w in range(nw)})


def _gather_pass_rider(gathered, chained=False, rels=(1, 2, 3)):
    nw, nr = len(gathered), len(rels)

    def copies(outs, sems, cc):
        send, recv = sems
        x, y, c = _coords()
        res = []
        for i, rel in enumerate(rels):
            kx, ky = _chip_of(x, y, rel)
            for w in range(nw):
                rows = _gather_half(outs, w, 2 * kx + ky, cc)
                res.append(pltpu.make_async_remote_copy(
                    src_ref=rows, dst_ref=rows, send_sem=send.at[w * nr + i], recv_sem=recv.at[w * nr + i],
                    device_id=(x, y, 1 - c), device_id_type=MESH))
        return res

    def start(ins, outs, sems):
        for cp in copies(outs, sems, lax.axis_index("c")):
            cp.start()

    def finish(ins, outs, sems):
        c = lax.axis_index("c")
        for cp in copies(outs, sems, 1 - c):
            cp.wait_recv()
        for cp in copies(outs, sems, c):
            cp.wait_send()

    dma = pltpu.SemaphoreType.DMA
    shapes = [jax.ShapeDtypeStruct(g.shape, g.dtype) for g in gathered]
    if chained:
        return _Rider([], [], [dma((nr * nw,)), dma((nr * nw,))], start, finish)
    return _Rider(gathered, shapes, [dma((nr * nw,)), dma((nr * nw,))], start, finish,
                  aliases={w: w for w in range(nw)})


def _exchange_halves_rider(parts):
    nw = len(parts)

    def copies(ins, outs, sems):
        send, recv = sems
        x, y, c = _coords()
        res = []
        for w in range(nw):
            h = parts[w].shape[1] // 2
            res.append(pltpu.make_async_remote_copy(
                src_ref=ins[w].at[:, pl.ds((1 - c) * h, h), :], dst_ref=outs[w],
                send_sem=send.at[w], recv_sem=recv.at[w], device_id=(x, y, 1 - c), device_id_type=MESH))
        return res

    def start(ins, outs, sems):
        for cp in copies(ins, outs, sems):
            cp.start()

    def finish(ins, outs, sems):
        for cp in copies(ins, outs, sems):
            cp.wait()

    dma = pltpu.SemaphoreType.DMA
    return _Rider(parts, [jax.ShapeDtypeStruct((N_SHARD, p.shape[1] // 2, p.shape[2]), p.dtype) for p in parts],
                  [dma((nw,)), dma((nw,))], start, finish)


def _add_halves(parts, theirs, pos):
    nw = len(parts)
    split = 1

    def body(pos_ref, *refs):
        ins, oth = refs[:nw], refs[nw:2 * nw]
        o32, o16 = refs[2 * nw:3 * nw], refs[3 * nw:]
        sums = [ins[w][...] + oth[w][...] for w in range(nw)]
        for w in range(nw):
            o16[w][...] = sums[w].astype(BF16)

        @pl.when(pl.program_id(1) == pos_ref[0])
        def _():
            for w in range(nw):
                o32[w][...] = sums[w]

    in_specs, oth_specs, o32_specs, shapes32, shapes16 = [], [], [], [], []
    for p in parts:
        hb = p.shape[1] // 2 // split
        blk = (None, hb, p.shape[2])
        in_specs.append(pl.BlockSpec(blk, lambda i, j, pos_ref: (j, pos_ref[1] * split + i, 0)))
        oth_specs.append(pl.BlockSpec(blk, lambda i, j, pos_ref: (j, i, 0)))
        o32_specs.append(pl.BlockSpec((hb, p.shape[2]), lambda i, j, pos_ref: (i, 0)))
        shapes32.append(jax.ShapeDtypeStruct((p.shape[1] // 2, p.shape[2]), F32))
        shapes16.append(jax.ShapeDtypeStruct((N_SHARD, p.shape[1] // 2, p.shape[2]), BF16))
    return pl.pallas_call(
        body, name="add_halves",
        grid_spec=pltpu.PrefetchScalarGridSpec(
            num_scalar_prefetch=1, grid=(split, N_SHARD),
            in_specs=in_specs + oth_specs, out_specs=o32_specs + oth_specs),
        out_shape=shapes32 + shapes16,
        compiler_params=_params("parallel", "arbitrary", vmem=VMEM_LIMIT),
    )(pos, *parts, *theirs)


def _exchange_chips_rider(sums16, rows=None, into=None):
    nw = len(sums16)
    rows = rows or [(0, s.shape[1]) for s in sums16]
    held = [w for w in range(nw) if into is not None and into[w] is not None]

    def copies(ins, outs, sems):
        send, recv = sems
        x, y, c = _coords()
        res = []
        for rel in (1, 2, 3):
            kx, ky = _chip_of(x, y, rel)
            for w in range(nw):
                r0, n = rows[w]
                res.append(pltpu.make_async_remote_copy(
                    src_ref=ins[w].at[2 * kx + ky, pl.ds(r0, n), :], dst_ref=outs[w].at[rel - 1, pl.ds(r0, n), :],
                    send_sem=send.at[w * 3 + rel - 1], recv_sem=recv.at[w * 3 + rel - 1],
                    device_id=(kx, ky, c), device_id_type=MESH))
        return res

    def start(ins, outs, sems):
        for cp in copies(ins, outs, sems):
            cp.start()

    def finish(ins, outs, sems):
        for cp in copies(ins, outs, sems):
            cp.wait()

    dma = pltpu.SemaphoreType.DMA
    return _Rider(list(sums16) + [into[w] for w in held],
                  [jax.ShapeDtypeStruct((3,) + s.shape[1:], BF16) for s in sums16],
                  [dma((3 * nw,)), dma((3 * nw,))], start, finish, aliases={nw + i: w for i, w in enumerate(held)})


def _add_chips(sums32, theirs, pos):
    nw = len(sums32)
    split = 2
    hbs = [s.shape[0] // split for s in sums32]

    def body(pos_ref, *refs):
        ins, oth, outs, bufs = (refs[k * nw:(k + 1) * nw] for k in range(4))
        lsem, ssem, rsem = refs[4 * nw:]
        i = pl.program_id(0)
        x, y, c = _coords()

        def copies(w, j):
            rows = pl.ds(pl.multiple_of((pos_ref[1] * split + j) * hbs[w], 8), hbs[w])
            return (pltpu.make_async_copy(bufs[w].at[j], outs[w].at[rows, :], lsem.at[w, j]),
                    pltpu.make_async_remote_copy(
                        src_ref=bufs[w].at[j], dst_ref=outs[w].at[rows, :], send_sem=ssem.at[w, j],
                        recv_sem=rsem.at[w, j], device_id=(x, y, 1 - c), device_id_type=MESH))

        for w in range(nw):
            acc = ins[w][...]
            for r in range(3):
                acc = acc + oth[w][r].astype(F32)
            bufs[w][i] = acc
            for cp in copies(w, i):
                cp.start()

        @pl.when(i == split - 1)
        def _():
            for w in range(nw):
                for j in range(split):
                    local, remote = copies(w, j)
                    local.wait()
                    remote.wait()

    in_specs, oth_specs, shapes, scratch = [], [], [], []
    for s, hb in zip(sums32, hbs):
        in_specs.append(pl.BlockSpec((hb, s.shape[1]), lambda i, pos_ref: (i, 0)))
        oth_specs.append(pl.BlockSpec((3, hb, s.shape[1]), lambda i, pos_ref: (0, i, 0)))
        shapes.append(jax.ShapeDtypeStruct((2 * s.shape[0], s.shape[1]), F32))
        scratch.append(pltpu.VMEM((split, hb, s.shape[1]), F32))
    dma = pltpu.SemaphoreType.DMA
    return pl.pallas_call(
        body, name="add_chips",
        grid_spec=pltpu.PrefetchScalarGridSpec(
            num_scalar_prefetch=1, grid=(split,), in_specs=in_specs + oth_specs,
            out_specs=[pl.BlockSpec(memory_space=pl.ANY)] * nw,
            scratch_shapes=scratch + [dma((nw, split)), dma((nw, split)), dma((nw, split))]),
        out_shape=shapes,
        compiler_params=_params("arbitrary", vmem=VMEM_LIMIT),
    )(pos, *sums32, *theirs)


def _adamw_math(w, g, m, v):
    m = ADAM_B1 * m + (1.0 - ADAM_B1) * g
    v = ADAM_B2 * v + (1.0 - ADAM_B2) * (g * g)
    m_hat = m / (1.0 - ADAM_B1 ** ADAM_STEP)
    v_hat = v / (1.0 - ADAM_B2 ** ADAM_STEP)
    delta = -ADAM_LR * (m_hat / (jnp.sqrt(v_hat) + ADAM_EPS) + ADAM_WD * w)
    return delta, m, v


def _adamw(ws, gs, ms, vs):
    nw = len(ws)
    split = 8

    def body(*refs):
        w_r, g_r, m_r, v_r = (refs[i * nw:(i + 1) * nw] for i in range(4))
        g_o, d_o, m_o, v_o = (refs[(4 + i) * nw:(5 + i) * nw] for i in range(4))
        for k in range(nw):
            g = g_r[k][...]
            d, m, v = _adamw_math(w_r[k][...], g, m_r[k][...], v_r[k][...])
            g_o[k][...] = g
            d_o[k][...] = d
            m_o[k][...] = m
            v_o[k][...] = v

    specs = [pl.BlockSpec((w.shape[0] // split, w.shape[1]), lambda i: (i, 0)) for w in ws]
    shapes = [jax.ShapeDtypeStruct(w.shape, F32) for w in ws]
    outs = pl.pallas_call(
        body, name="adamw", grid=(split,),
        in_specs=specs * 4, out_specs=specs * 4, out_shape=shapes * 4,
        compiler_params=_params("parallel", vmem=VMEM_LIMIT),
    )(*ws, *gs, *ms, *vs)
    return outs[:nw], outs[nw:2 * nw], outs[2 * nw:3 * nw], outs[3 * nw:]


SMALL_ROWS = 8
SMALL_COLS = D_MODEL
LOSS_COL = RET_WIDTH + 24


def _small_allreduce_adamw(part, w, m, v, rider=None):
    def body(part_ref, w_ref, m_ref, v_ref, g_out, d_out, m_out, v_out, all_ref, send, recv):
        x, y, c = _coords()
        me = 4 * x + 2 * y + c
        all_ref[me] = part_ref[...]
        copies = []
        for rel in range(1, 8):
            px = 1 - x if rel & 4 else x
            py = 1 - y if rel & 2 else y
            pc = 1 - c if rel & 1 else c
            copies.append(pltpu.make_async_remote_copy(
                src_ref=part_ref, dst_ref=all_ref.at[me],
                send_sem=send.at[rel - 1], recv_sem=recv.at[rel - 1], device_id=(px, py, pc), device_id_type=MESH))
        for cp in copies:
            cp.start()
        for cp in copies:
            cp.wait()
        g = all_ref[0]
        for k in range(1, 8):
            g = g + all_ref[k]
        d, mn, vn = _adamw_math(w_ref[...], g, m_ref[...], v_ref[...])
        g_out[...] = g
        d_out[...] = d
        m_out[...] = mn
        v_out[...] = vn

    vm = pl.BlockSpec(memory_space=pltpu.VMEM)
    shape = jax.ShapeDtypeStruct((SMALL_ROWS, SMALL_COLS), F32)
    return _hosted_call(
        body, "small_allreduce_adamw", (1,),
        in_specs=[vm] * 4, out_specs=[vm] * 4, out_shape=[shape] * 4,
        scratch_shapes=[pltpu.VMEM((8, SMALL_ROWS, SMALL_COLS), F32),
                        pltpu.SemaphoreType.DMA((7,)), pltpu.SemaphoreType.DMA((7,))],
        operands=(part, w, m, v), rider=rider, semantics=["arbitrary"])


SMALL_NAMES = ("ret_decay_fwd", "ret_decay_bwd", "attn_sink", "ret_gn_gain",
               "ln1_gain", "ln1_bias", "ln2_gain", "ln2_bias")


LN_NAMES = ("ln1_gain", "ln1_bias", "ln2_gain", "ln2_bias")


def _pack_small(vals, extra=None):
    tail = jnp.zeros((1, 1), F32) if extra is None else extra.reshape(1, 1)
    row4 = jnp.concatenate([vals["ret_gn_gain"], vals["ret_decay_fwd"], vals["ret_decay_bwd"], vals["attn_sink"],
                            tail, jnp.zeros((1, SMALL_COLS - LOSS_COL - 1), F32)], axis=1)
    rows = [vals[n] for n in LN_NAMES] + [row4, jnp.zeros((SMALL_ROWS - 5, SMALL_COLS), F32)]
    return jnp.concatenate(rows, axis=0)


def _unpack_small(packed):
    o = RET_WIDTH
    where = [(n, i, 0, SMALL_COLS) for i, n in enumerate(LN_NAMES)] + [
        ("ret_gn_gain", 4, 0, o), ("ret_decay_fwd", 4, o, 8), ("ret_decay_bwd", 4, o + 8, 8),
        ("attn_sink", 4, o + 16, 8)]
    na, k = len(packed), len(where)

    def body(*refs):
        for a in range(na):
            rows = refs[a][...]
            for b, (_, row, col, n) in enumerate(where):
                refs[na + a * k + b][...] = rows[row:row + 1, col:col + n]
            if a == 0:
                refs[na + na * k][...] = rows[4:5, LOSS_COL:LOSS_COL + 1] * (0.5 / D_MODEL)

    vmem = pl.BlockSpec(memory_space=pltpu.VMEM)
    outs = pl.pallas_call(
        body, name="unpack_small", in_specs=[vmem] * na, out_specs=[vmem] * (na * k + 1),
        out_shape=[jax.ShapeDtypeStruct((1, n), F32) for _ in range(na) for (_, _, _, n) in where]
        + [jax.ShapeDtypeStruct((1, 1), F32)])(*packed)
    return [{where[b][0]: outs[a * k + b] for b in range(k)} for a in range(na)], outs[na * k]


def _local_step(x, p, tgt, w_in_t, rest, small, pos=None, small_state=None):
    bsz, s, _ = x.shape
    t = bsz * s
    x2d = x.reshape(t, D_MODEL)
    p2d = p.reshape(t, PLE_DIM)
    tgt2d = tgt.reshape(t, D_MODEL)
    dec_f = small["ret_decay_fwd"].reshape(8)
    dec_b = small["ret_decay_bwd"].reshape(8)
    lg_f = jnp.log1p(-jnp.exp2(dec_f))
    lg_b = jnp.log1p(-jnp.exp2(dec_b))
    per_lane = lambda v: jnp.repeat(v, HEAD_DIM).reshape(4, 1, LANES)
    lgf_l, lgb_l = per_lane(lg_f), per_lane(lg_b)
    sink = small["attn_sink"].reshape(8)
    slopes = 2.0 ** (-(jnp.arange(8, dtype=F32) + 1.0))
    gn_gain = small["ret_gn_gain"]
    g1, b1, g2, b2 = (small[n] for n in ("ln1_gain", "ln1_bias", "ln2_gain", "ln2_bias"))

    dist = pos is not None
    shard = dict(zip(REST_NAMES, rest)) if dist else {}
    near = lambda names, rels=NEAR: _gather_near_rider([shard[n] for n in names], rels)
    wave1, wave2, wave3 = ("w_out", "w_ple_gate", "w_ffn_gate"), ("w_ffn_up", "w_ple_proj"), ("w_ffn_down",)
    n1 = len(wave1)
    u, *o1 = _inproj(x2d, w_in_t, rider=near(wave1) if dist else None)
    u3 = u.reshape(bsz, s, IN_WIDTH)
    y_hat, y_rstd, y_ret, ret_rb, ret_kvf, *o2 = _ret_fwd(u3, lgf_l, lgb_l, gn_gain, rider=_merge_riders(
        [_gather_relay_rider(o1), near(wave2)]) if dist else None)
    y_att, att_p, att_ps, *o3 = _attn_fwd(u3, slopes, sink, rider=_merge_riders(
        [_gather_pass_rider(o2[:n1]), _gather_relay_rider(o2[n1:]), near(wave3, (1, 2, 3))]) if dist else None)
    gathered = dict(zip(wave1, o3[:n1]))
    w_out = _assemble_weights({"w_out": gathered["w_out"]})["w_out"] if dist else rest["w_out"]
    zh1, r1, hb, *o4 = _outproj_ln1(y_ret.reshape(t, RET_WIDTH), y_att.reshape(t, ATTN_WIDTH), x2d, w_out, g1, b1,
                                    rider=_gather_pass_rider(o3[n1:]) if dist else None)
    gathered.update(zip(wave2 + wave3, o4))
    wts = _assemble_weights(gathered) if dist else rest
    dz2, dz2b, gs, us, acts, pg, ple, sq, dg2, db2 = _ffn_fwd(
        zh1, hb, p2d, tgt2d, g1, b1, g2, b2, wts["gate4"], wts["up4"], wts["down4"], wts["ple_proj"], wts["ple_gate"])
    dgs, dus, dsp, dple, dz1, dyr, dya, dg1, db1 = _ffn_bwd(dz2, gs, us, pg, ple, zh1, r1, g1, wts["gate4"],
                                                          wts["up4"], wts["down4"], wts["ple_gate"], wts["w_out"])
    ffn_parts = list(_wgrad_ffn(acts, dgs, dus, hb, dz2b))
    d_w_out, d_ple_gate, d_ple_proj, *th_ffn = _wgrad_misc(
        y_ret.reshape(t, RET_WIDTH), y_att.reshape(t, ATTN_WIDTH), dz1, hb, dsp, p2d, dple,
        rider=_exchange_halves_rider(ffn_parts[:2]) if dist else None)
    misc_parts = [d_w_out.reshape(N_SHARD, D_MODEL // N_SHARD, D_MODEL), d_ple_proj,
                  d_ple_gate.reshape(N_SHARD, D_MODEL // N_SHARD, D_MODEL)]
    dyr3, dya3 = dyr.reshape(bsz, s, RET_WIDTH), dya.reshape(bsz, s, ATTN_WIDTH)
    if dist:
        s_gu = _add_halves(ffn_parts[:2], th_ffn, pos)
        half = FFN_SHARD // 2
        quarter = half // 2
        later_parts = [ffn_parts[2]] + misc_parts
        drq, drk, drv, drg, rpart, *o5 = _ret_bwd(u3, y_hat, y_rstd, (ret_rb, ret_kvf), dyr3, lgf_l, lgb_l, gn_gain,
                                                  rider=_merge_riders(
            [_exchange_chips_rider(s_gu[2:], rows=[(0, half), (0, quarter)]), _exchange_halves_rider(later_parts)]))
        s_dm = _add_halves(later_parts, o5[2:], pos)
        daq, dakv, spart, *o6 = _attn_bwd(u3, dya3, att_p, att_ps, rider=_exchange_chips_rider(
            [s_gu[3], s_dm[4]], rows=[(quarter, half - quarter), (0, half)], into=[o5[1], None]))
    else:
        drq, drk, drv, drg, rpart = _ret_bwd(u3, y_hat, y_rstd, (ret_rb, ret_kvf), dyr3, lgf_l, lgb_l, gn_gain)
        daq, dakv, spart = _attn_bwd(u3, dya3, att_p, att_ps)
    pieces = [a.reshape(t, -1) for a in (drq, drk, drv, drg, daq, dakv)]
    d_in, *o7 = _wgrad_in(pieces, x2d, rider=_exchange_chips_rider(list(s_dm[5:])) if dist else None)
    d_in = d_in.reshape(N_SHARD, FFN_SHARD, D_MODEL)

    rsum = rpart
    lane_heads = lambda row: jnp.sum(row.reshape(4, 2, HEAD_DIM), axis=-1).reshape(8)
    dlg_f = lane_heads(rsum[:, 0, :]) + jnp.stack([jnp.sum(rsum[:, 2, :], -1), jnp.sum(rsum[:, 3, :], -1)], 1).reshape(8)
    dlg_b = lane_heads(rsum[:, 1, :]) + jnp.stack([jnp.sum(rsum[:, 4, :], -1), jnp.sum(rsum[:, 5, :], -1)], 1).reshape(8)
    chain = lambda d: -(math.log(2.0) * jnp.exp2(d)) / (1.0 - jnp.exp2(d))
    grads_small = {
        "ret_decay_fwd": (dlg_f * chain(dec_f)).reshape(1, 8),
        "ret_decay_bwd": (dlg_b * chain(dec_b)).reshape(1, 8),
        "attn_sink": jnp.sum(spart, axis=0)[:, 0:4, 0].reshape(1, 8),
        "ret_gn_gain": rsum[:, 6, :].reshape(1, RET_WIDTH),
        "ln1_gain": dg1, "ln1_bias": db1, "ln2_gain": dg2, "ln2_bias": db2,
    }
    if not dist:
        grad_x, = _inproj_bwd(dz1, pieces, w_in_t)
        grads_rest = [misc_parts[0]] + ffn_parts + misc_parts[1:]
        return sq[0, 0], grad_x.reshape(bsz, s, D_MODEL), d_in, grads_rest, grads_small
    *small_out, th_in = _small_allreduce_adamw(_pack_small(grads_small, sq[0, 0]), *small_state,
                                               rider=_exchange_halves_rider([d_in]))
    s_in = _add_halves([d_in], [th_in], pos)
    grad_x, chips_in = _inproj_bwd(dz1, pieces, w_in_t, rider=_exchange_chips_rider([s_in[1]]))
    sums32 = [s_in[0], s_dm[1], s_gu[0], s_gu[1], s_dm[0], s_dm[2], s_dm[3]]
    from_chips = [chips_in, o7[0], o5[0], o6[0], o6[1], o7[1], o7[2]]
    return grad_x.reshape(bsz, s, D_MODEL), sums32, from_chips, small_out


BIG_NAMES = ("w_in", "w_out", "w_ffn_gate", "w_ffn_up", "w_ffn_down", "w_ple_proj", "w_ple_gate")
REST_NAMES = BIG_NAMES[1:]
TRANSPOSED = ("w_in", "w_ffn_gate", "w_ffn_up")
WEIGHT_ORDER = ("w_in", "ret_decay_fwd", "ret_decay_bwd", "ret_gn_gain", "attn_sink", "w_out", "ln1_gain",
                "ln1_bias", "w_ffn_gate", "w_ffn_up", "w_ffn_down", "w_ple_proj", "w_ple_gate", "ln2_gain", "ln2_bias")


def _shard_rows(name, a):
    return jnp.swapaxes(a[0], 0, 1) if name in TRANSPOSED else a[0]


def _unshard_rows(name, a):
    return (jnp.swapaxes(a, 0, 1) if name in TRANSPOSED else a)[None]


def _assemble_weights(gathered):
    rows = lambda a: a.reshape(N_SHARD * a.shape[1], a.shape[2])
    same = lambda a: a
    layout = {"w_out": ("w_out", rows), "w_ffn_gate": ("gate4", same), "w_ffn_up": ("up4", same),
              "w_ffn_down": ("down4", same), "w_ple_proj": ("ple_proj", same), "w_ple_gate": ("ple_gate", rows)}
    return {layout[n][0]: layout[n][1](a) for n, a in gathered.items()}


def kernel(x, p, w_in, ret_decay_fwd, ret_decay_bwd, ret_gn_gain, attn_sink, w_out, ln1_gain, ln1_bias, w_ffn_gate, w_ffn_up, w_ffn_down, w_ple_proj, w_ple_gate, ln2_gain, ln2_bias, loss_target, m_w_in, m_ret_decay_fwd, m_ret_decay_bwd, m_ret_gn_gain, m_attn_sink, m_w_out, m_ln1_gain, m_ln1_bias, m_w_ffn_gate, m_w_ffn_up, m_w_ffn_down, m_w_ple_proj, m_w_ple_gate, m_ln2_gain, m_ln2_bias, v_w_in, v_ret_decay_fwd, v_ret_decay_bwd, v_ret_gn_gain, v_attn_sink, v_w_out, v_ln1_gain, v_ln1_bias, v_w_ffn_gate, v_w_ffn_up, v_w_ffn_down, v_w_ple_proj, v_w_ple_gate, v_ln2_gain, v_ln2_bias):
    w = dict(w_in=w_in, ret_decay_fwd=ret_decay_fwd, ret_decay_bwd=ret_decay_bwd, ret_gn_gain=ret_gn_gain,
             attn_sink=attn_sink, w_out=w_out, ln1_gain=ln1_gain, ln1_bias=ln1_bias, w_ffn_gate=w_ffn_gate,
             w_ffn_up=w_ffn_up, w_ffn_down=w_ffn_down, w_ple_proj=w_ple_proj, w_ple_gate=w_ple_gate,
             ln2_gain=ln2_gain, ln2_bias=ln2_bias)
    m = dict(w_in=m_w_in, ret_decay_fwd=m_ret_decay_fwd, ret_decay_bwd=m_ret_decay_bwd, ret_gn_gain=m_ret_gn_gain,
             attn_sink=m_attn_sink, w_out=m_w_out, ln1_gain=m_ln1_gain, ln1_bias=m_ln1_bias, w_ffn_gate=m_w_ffn_gate,
             w_ffn_up=m_w_ffn_up, w_ffn_down=m_w_ffn_down, w_ple_proj=m_w_ple_proj, w_ple_gate=m_w_ple_gate,
             ln2_gain=m_ln2_gain, ln2_bias=m_ln2_bias)
    v = dict(w_in=v_w_in, ret_decay_fwd=v_ret_decay_fwd, ret_decay_bwd=v_ret_decay_bwd, ret_gn_gain=v_ret_gn_gain,
             attn_sink=v_attn_sink, w_out=v_w_out, ln1_gain=v_ln1_gain, ln1_bias=v_ln1_bias, w_ffn_gate=v_w_ffn_gate,
             w_ffn_up=v_w_ffn_up, w_ffn_down=v_w_ffn_down, w_ple_proj=v_w_ple_proj, w_ple_gate=v_w_ple_gate,
             ln2_gain=v_ln2_gain, ln2_bias=v_ln2_bias)
    big = lambda d: [_shard_rows(n, d[n]) for n in BIG_NAMES]
    small = lambda d: {n: d[n] for n in SMALL_NAMES}

    chip = 2 * lax.axis_index("x") + lax.axis_index("y")
    pos = jnp.stack([chip, lax.axis_index("c")]).astype(jnp.int32)

    shards = big(w)
    w_in4, rest16 = _gather_and_cast(shards[0].astype(BF16), shards[1:])
    w_in_t = w_in4.reshape(IN_WIDTH, D_MODEL)
    grad_x, sums32, from_chips, (g_s, d_s, m_s, v_s) = _local_step(
        x, p[0], loss_target, w_in_t, rest16, small(w), pos=pos,
        small_state=(_pack_small(small(w)), _pack_small(small(m)), _pack_small(small(v))))
    g_big, d_big, m_big, v_big = _adamw(big(w), _add_chips(sums32, from_chips, pos), big(m), big(v))

    def tree(bigs, smalls):
        out = {n: _unshard_rows(n, a) for n, a in zip(BIG_NAMES, bigs)}
        out.update(smalls)
        return [out[n] for n in WEIGHT_ORDER]

    smalls, loss = _unpack_small([g_s, d_s, m_s, v_s])
    return (loss.reshape(()), grad_x,
            *(a for bigs, s in zip((g_big, d_big, m_big, v_big), smalls) for a in tree(bigs, s)))
```

```python
import functools
import math

import jax
import jax.numpy as jnp
from jax import lax
from jax.experimental import pallas as pl
from jax.experimental.pallas import tpu as pltpu

F32 = jnp.float32
BF16 = jnp.bfloat16

D_MODEL = 1024
HEAD_DIM = 64
RET_HEADS = 8
ATTN_HEADS = 8
RET_WIDTH = 512
ATTN_WIDTH = 512
KV_WIDTH = 128
IN_WIDTH = 2816
FFN = 2816
N_SHARD = 4
FFN_SHARD = FFN // N_SHARD
PLE_DIM = 256
CHUNK = 128
LANES = 128
ALPHA = 2.0 ** 0.25
LN_EPS = 1e-5
GN_EPS = 1e-5
NEG_INF = -1e30
ADAM_LR = 0.001
ADAM_B1 = 0.9
ADAM_B2 = 0.999
ADAM_EPS = 1e-08
ADAM_WD = 0.01
ADAM_STEP = 10
VMEM_LIMIT = 56 * 1024 * 1024
MESH = pl.DeviceIdType.MESH

CB_RQ, CB_RK, CB_RV, CB_RG, CB_AQ, CB_AK, CB_AV = 0, 4, 8, 12, 16, 20, 21


def _dot(a, b):
    return jnp.dot(a, b, preferred_element_type=F32)


def _dot_nt(a, b):
    return lax.dot_general(a, b, (((1,), (1,)), ((), ())), preferred_element_type=F32)


def _dot_tn(a, b):
    return lax.dot_general(a, b, (((0,), (0,)), ((), ())), preferred_element_type=F32)


def _sigmoid(x):
    return 1.0 / (1.0 + jnp.exp(-x))


def _params(*sem, vmem=None, collective_id=None):
    return pltpu.CompilerParams(dimension_semantics=tuple(sem) if sem else None, vmem_limit_bytes=vmem,
                                collective_id=collective_id)


def _meet_sibling():
    x, y, c = lax.axis_index("x"), lax.axis_index("y"), lax.axis_index("c")
    sem = pltpu.get_barrier_semaphore()
    pl.semaphore_signal(sem, inc=1, device_id=(x, y, 1 - c), device_id_type=MESH)
    pl.semaphore_wait(sem, 1)


class _Rider:
    def __init__(self, ins, out_shapes, sems, start, finish, aliases=None, sibling_id=None):
        self.ins, self.out_shapes, self.sems = list(ins), list(out_shapes), list(sems)
        self.start, self.finish, self.aliases, self.sibling_id = start, finish, dict(aliases or {}), sibling_id


def _merge_riders(riders):
    riders = [r for r in riders if r is not None]
    if len(riders) == 1:
        return riders[0]
    bounds, aliases = [], {}
    i0 = o0 = s0 = 0
    for r in riders:
        bounds.append((i0, o0, s0))
        aliases.update({i0 + i: o0 + o for i, o in r.aliases.items()})
        i0, o0, s0 = i0 + len(r.ins), o0 + len(r.out_shapes), s0 + len(r.sems)

    def each(method):
        def run(ins, outs, sems):
            for r, (i, o, s) in zip(riders, bounds):
                getattr(r, method)(ins[i:i + len(r.ins)], outs[o:o + len(r.out_shapes)], sems[s:s + len(r.sems)])
        return run

    return _Rider([a for r in riders for a in r.ins], [a for r in riders for a in r.out_shapes],
                  [a for r in riders for a in r.sems], each("start"), each("finish"), aliases)


def _hosted_call(body, name, grid, in_specs, out_specs, out_shape, scratch_shapes, operands, rider=None,
                 semantics=None):
    n_in, n_out, n_scr = len(in_specs), len(out_specs), len(scratch_shapes)
    if rider is None:
        return pl.pallas_call(
            body, name=name, grid=grid, in_specs=in_specs, out_specs=out_specs, out_shape=out_shape,
            scratch_shapes=scratch_shapes,
            compiler_params=_params(*(semantics or ["parallel"] * len(grid)), vmem=VMEM_LIMIT))(*operands)
    r_in, r_out = len(rider.ins), len(rider.out_shapes)

    def full_body(*refs):
        main_in, rin = refs[:n_in], refs[n_in:n_in + r_in]
        o0 = n_in + r_in
        main_out, rout = refs[o0:o0 + n_out], refs[o0 + n_out:o0 + n_out + r_out]
        s0 = o0 + n_out + r_out
        main_scr, rsem = refs[s0:s0 + n_scr], refs[s0 + n_scr:]
        first = functools.reduce(jnp.logical_and, [pl.program_id(a) == 0 for a in range(len(grid))])
        last = functools.reduce(jnp.logical_and, [pl.program_id(a) == g - 1 for a, g in enumerate(grid)])

        @pl.when(first)
        def _():
            if rider.sibling_id is not None:
                _meet_sibling()
            rider.start(rin, rout, rsem)

        body(*main_in, *main_out, *main_scr)

        @pl.when(last)
        def _():
            rider.finish(rin, rout, rsem)

    hbm = pl.BlockSpec(memory_space=pl.ANY)
    return pl.pallas_call(
        full_body, name=name, grid=grid,
        in_specs=list(in_specs) + [hbm] * r_in, out_specs=list(out_specs) + [hbm] * r_out,
        out_shape=list(out_shape) + rider.out_shapes,
        scratch_shapes=list(scratch_shapes) + rider.sems,
        input_output_aliases={n_in + i: n_out + o for i, o in rider.aliases.items()},
        compiler_params=_params(*(["arbitrary"] * len(grid)), vmem=VMEM_LIMIT, collective_id=rider.sibling_id),
    )(*operands, *rider.ins)


def _loop_grouped(n, body, init, per_trip=2):
    if n % per_trip:
        return lax.fori_loop(0, n, body, init)

    def trip(i, c):
        for j in range(per_trip):
            c = body(per_trip * i + j, c)
        return c

    return lax.fori_loop(0, n // per_trip, trip, init)


def _head_mean(x, m0):
    s0 = jnp.sum(jnp.where(m0, x, 0.0), axis=1, keepdims=True)
    s1 = jnp.sum(jnp.where(m0, 0.0, x), axis=1, keepdims=True)
    return jnp.where(m0, s0, s1) * (1.0 / HEAD_DIM)


def _inproj(x2d, w_in_t, rider=None):
    t = x2d.shape[0]
    tm = 512
    nb = 256

    def body(x_ref, w_ref, o_ref):
        xb = x_ref[...].astype(BF16)
        for n in range(0, IN_WIDTH, nb):
            o_ref[:, n:n + nb] = _dot_nt(xb, w_ref[n:n + nb, :]).astype(BF16)

    return _hosted_call(
        body, "inproj", (t // tm,),
        in_specs=[pl.BlockSpec((tm, D_MODEL), lambda i: (i, 0)),
                  pl.BlockSpec((IN_WIDTH, D_MODEL), lambda i: (0, 0))],
        out_specs=[pl.BlockSpec((tm, IN_WIDTH), lambda i: (i, 0))],
        out_shape=[jax.ShapeDtypeStruct((t, IN_WIDTH), BF16)],
        scratch_shapes=[], operands=(x2d, w_in_t), rider=rider)


def _outproj_ln1(y_ret, y_att, x2d, w_out, gain, bias, rider=None):
    t = x2d.shape[0]
    tm = 512

    def body(yr_ref, ya_ref, x_ref, w_ref, g_ref, b_ref, zh_ref, r_ref, hb_ref):
        mix = _dot(yr_ref[...], w_ref[0:RET_WIDTH, :]) + _dot(ya_ref[...], w_ref[RET_WIDTH:, :])
        z = ALPHA * x_ref[...] + mix
        mu = jnp.mean(z, axis=1, keepdims=True)
        zc = z - mu
        var = jnp.mean(zc * zc, axis=1, keepdims=True)
        r = lax.rsqrt(var + LN_EPS)
        zh = zc * r
        zh_ref[...] = zh
        r_ref[...] = r
        hb_ref[...] = (zh * g_ref[...] + b_ref[...]).astype(BF16)

    row = lambda w: pl.BlockSpec((tm, w), lambda i: (i, 0))
    const = lambda s: pl.BlockSpec(s, lambda i: (0, 0))
    return _hosted_call(
        body, "outproj_ln1", (t // tm,),
        in_specs=[row(RET_WIDTH), row(ATTN_WIDTH), row(D_MODEL), const((D_MODEL, D_MODEL)),
                  const((1, D_MODEL)), const((1, D_MODEL))],
        out_specs=[row(D_MODEL), row(1), row(D_MODEL)],
        out_shape=[jax.ShapeDtypeStruct((t, D_MODEL), F32), jax.ShapeDtypeStruct((t, 1), F32),
                   jax.ShapeDtypeStruct((t, D_MODEL), BF16)],
        scratch_shapes=[], operands=(y_ret, y_att, x2d, w_out, gain, bias), rider=rider)


def _load_resident(step, pairs, sems):
    copies = [pltpu.make_async_copy(src, dst, sems.at[i]) for i, (src, dst) in enumerate(pairs)]

    @pl.when(step == 0)
    def _():
        for cp in copies:
            cp.start()
        for cp in copies:
            cp.wait()


FFN_CHUNK = 256
N_FFN_CHUNK = FFN // FFN_CHUNK


def _resident_quarters(hbm, vmem):
    q = FFN // N_SHARD
    return [(hbm.at[pl.ds(j * q, q), :], vmem.at[pl.ds(j * q, q), :]) for j in range(N_SHARD)]


def _ln2_loss_tail(zh, mixed, tgt, g1, b1, g2, b2):
    z2 = ALPHA * (zh * g1 + b1) + mixed
    mu = jnp.mean(z2, axis=1, keepdims=True)
    zc = z2 - mu
    var = jnp.mean(zc * zc, axis=1, keepdims=True)
    r = lax.rsqrt(var + LN_EPS)
    zh2 = zc * r
    err = zh2 * g2 + b2 - tgt
    dy = err * (1.0 / D_MODEL)
    dzh = dy * g2
    m1 = jnp.mean(dzh, axis=1, keepdims=True)
    m2 = jnp.mean(dzh * zh2, axis=1, keepdims=True)
    dz2 = r * (dzh - m1 - zh2 * m2)
    return dz2, jnp.sum(err * err), jnp.sum(dy * zh2, axis=0, keepdims=True), jnp.sum(dy, axis=0, keepdims=True)


def _ffn_fwd(zh1, hb, p2d, tgt, g1, b1, g2, b2, wg4, wu4, wd4, wpe, wpg):
    t = zh1.shape[0]
    tm = 256
    wg_t, wu_t, wd_all = (w.reshape(FFN, D_MODEL) for w in (wg4, wu4, wd4))

    def body(zh_ref, hb_ref, p_ref, t_ref, g1_ref, b1_ref, g2_ref, b2_ref,
             wg_hbm, wu_hbm, wd_hbm, wpe_hbm, wpg_hbm,
             dz_ref, dzb_ref, gs_ref, us_ref, act_ref, pg_ref, ple_ref, loss_ref, dg2_ref, db2_ref,
             wg, wu, wd, wpe, wpg, wsem):
        step = pl.program_id(0)
        loads = _resident_quarters(wg_hbm, wg) + _resident_quarters(wu_hbm, wu) + _resident_quarters(wd_hbm, wd)
        pc = D_MODEL // N_SHARD
        loads += [(wpe_hbm.at[j], wpe.at[:, pl.ds(j * pc, pc)]) for j in range(N_SHARD)]
        _load_resident(step, loads + [(wpg_hbm, wpg)], wsem)

        @pl.when(step == 0)
        def _():
            loss_ref[...] = jnp.zeros_like(loss_ref)
            dg2_ref[...] = jnp.zeros_like(dg2_ref)
            db2_ref[...] = jnp.zeros_like(db2_ref)

        hbv = hb_ref[...]
        ffn = jnp.zeros((tm, D_MODEL), F32)
        acts = []
        chunks = [slice(n * FFN_CHUNK, (n + 1) * FFN_CHUNK) for n in range(N_FFN_CHUNK)]
        for n in range(N_FFN_CHUNK + 1):
            if n < N_FFN_CHUNK:
                gj = _dot_nt(hbv, wg[chunks[n], :])
                uj = _dot_nt(hbv, wu[chunks[n], :])
                gs_ref[:, chunks[n]] = gj.astype(BF16)
                us_ref[:, chunks[n]] = uj.astype(BF16)
                acts.append((gj * _sigmoid(gj) * uj).astype(BF16))
                act_ref[:, chunks[n]] = acts[n]
            if n > 0:
                ffn = ffn + _dot(acts[n - 1], wd[chunks[n - 1], :])
        ple = _dot(p_ref[...].astype(BF16), wpe[...])
        pg = _sigmoid(_dot(hbv, wpg[...]))
        pg_ref[...] = pg.astype(BF16)
        ple_ref[...] = ple.astype(BF16)
        dz2, sq, dg2, db2 = _ln2_loss_tail(zh_ref[...], ffn + pg * ple, t_ref[...], g1_ref[...], b1_ref[...],
                                           g2_ref[...], b2_ref[...])
        dz_ref[...] = dz2
        dzb_ref[...] = dz2.astype(BF16)
        loss_ref[...] += sq
        dg2_ref[...] += dg2
        db2_ref[...] += db2

    row = lambda w: pl.BlockSpec((tm, w), lambda i: (i, 0))
    const = lambda s: pl.BlockSpec(s, lambda i: (0, 0))
    hid_shape = jax.ShapeDtypeStruct((t, FFN), BF16)
    hbm = pl.BlockSpec(memory_space=pl.ANY)
    return pl.pallas_call(
        body, name="ffn_fwd", grid=(t // tm,),
        in_specs=[row(D_MODEL), row(D_MODEL), row(PLE_DIM), row(D_MODEL),
                  const((1, D_MODEL)), const((1, D_MODEL)), const((1, D_MODEL)), const((1, D_MODEL)),
                  hbm, hbm, hbm, hbm, hbm],
        out_specs=[row(D_MODEL), row(D_MODEL), row(FFN), row(FFN), row(FFN), row(D_MODEL), row(D_MODEL),
                   const((8, LANES)), const((1, D_MODEL)), const((1, D_MODEL))],
        out_shape=[jax.ShapeDtypeStruct((t, D_MODEL), F32), jax.ShapeDtypeStruct((t, D_MODEL), BF16),
                   hid_shape, hid_shape, hid_shape,
                   jax.ShapeDtypeStruct((t, D_MODEL), BF16), jax.ShapeDtypeStruct((t, D_MODEL), BF16),
                   jax.ShapeDtypeStruct((8, LANES), F32),
                   jax.ShapeDtypeStruct((1, D_MODEL), F32), jax.ShapeDtypeStruct((1, D_MODEL), F32)],
        scratch_shapes=[pltpu.VMEM((FFN, D_MODEL), BF16), pltpu.VMEM((FFN, D_MODEL), BF16),
                        pltpu.VMEM((FFN, D_MODEL), BF16),
                        pltpu.VMEM((PLE_DIM, D_MODEL), BF16), pltpu.VMEM(wpg.shape, BF16),
                        pltpu.SemaphoreType.DMA((4 * N_SHARD + 1,))],
        compiler_params=_params("arbitrary", vmem=VMEM_LIMIT),
    )(zh1, hb, p2d, tgt, g1, b1, g2, b2, wg_t, wu_t, wd_all, wpe, wpg)


def _ret_tables(lgf, lgb):
    c = CHUNK
    row = lax.broadcasted_iota(jnp.int32, (c, LANES), 0).astype(F32)
    ii = lax.broadcasted_iota(jnp.int32, (c, c), 0).astype(F32)
    jj = lax.broadcasted_iota(jnp.int32, (c, c), 1).astype(F32)
    diff = ii - jj
    dmats = []
    for h in range(2):
        lf = lgf[:, h * HEAD_DIM:h * HEAD_DIM + 1]
        lb = lgb[:, h * HEAD_DIM:h * HEAD_DIM + 1]
        dmats.append(jnp.where(diff > 0, jnp.exp(lf * jnp.maximum(diff, 0.0)),
                               jnp.where(diff < 0, jnp.exp(lb * jnp.maximum(-diff, 0.0)), 2.0)))
    tab = dict(
        qdec_f=jnp.exp(lgf * (row + 1.0)), kdec_f=jnp.exp(lgf * (c - 1.0 - row)),
        qdec_b=jnp.exp(lgb * (c - row)), kdec_b=jnp.exp(lgb * row),
        cdec_f=jnp.exp(lgf * c), cdec_b=jnp.exp(lgb * c),
        d0=dmats[0], d1=dmats[1], row=row, diff=diff)
    r = lax.broadcasted_iota(jnp.int32, (LANES, LANES), 0) < HEAD_DIM
    cc = lax.broadcasted_iota(jnp.int32, (LANES, LANES), 1) < HEAD_DIM
    tab["bd"] = r == cc
    tab["m0"] = lax.broadcasted_iota(jnp.int32, (c, LANES), 1) < HEAD_DIM
    return tab


def _ret_specs(bsz, s):
    blk = lambda cb: pl.BlockSpec((bsz, s, LANES), lambda p, cb=cb: (0, 0, cb + p))
    lane = pl.BlockSpec((None, 1, LANES), lambda p: (p, 0, 0))
    gain = pl.BlockSpec((1, LANES), lambda p: (0, p))
    pair = pl.BlockSpec((bsz, s, LANES), lambda p: (0, 0, p))
    return blk, lane, gain, pair


def _ret_state_spec(bsz, n_chunk):
    spec = pl.BlockSpec((None, bsz, n_chunk, LANES, LANES), lambda p: (p, 0, 0, 0, 0))
    return spec, jax.ShapeDtypeStruct((4, bsz, n_chunk, LANES, LANES), F32)


def _ret_kv_states(tb, k_ref, v_ref, rb_ref, kvf_ref, n_chunk):
    c = CHUNK
    bsz = k_ref.shape[0]
    bd = tb["bd"]

    def contributions(n, carry):
        sl = pl.ds(pl.multiple_of(n * c, c), c)
        kfb = []
        for b in range(bsz):
            k32 = k_ref[b, sl, :].astype(F32)
            kfb.append(jnp.concatenate([k32 * tb["kdec_f"], k32 * tb["kdec_b"]], axis=1).astype(BF16))
        kvs = [_dot_tn(kfb[b], v_ref[b, sl, :]) for b in range(bsz)]
        for b in range(bsz):
            kvf_ref[b, n] = jnp.where(bd, kvs[b][0:LANES], 0.0)
            rb_ref[b, n] = jnp.where(bd, kvs[b][LANES:], 0.0)
        return carry

    lax.fori_loop(0, n_chunk, contributions, 0, unroll=2)

    def recur(i, rbs):
        n = n_chunk - 1 - i
        new = []
        for b in range(bsz):
            own = rb_ref[b, n]
            rb_ref[b, n] = rbs[b]
            new.append(rbs[b] * tb["cdec_b"] + own)
        return tuple(new)

    lax.fori_loop(0, n_chunk, recur, tuple(jnp.zeros((LANES, LANES), F32) for _ in range(bsz)))


def _split_rows(x, m0):
    return jnp.concatenate([jnp.where(m0, x, 0.0), jnp.where(m0, 0.0, x)], axis=0).astype(BF16)


def _ret_fwd(u3, lgf_l, lgb_l, gn_gain, rider=None):
    bsz, s, _ = u3.shape
    n_chunk = s // CHUNK
    c = CHUNK

    def body(q_ref, k_ref, v_ref, g_ref, lgf_ref, lgb_ref, gain_ref, yh_ref, rstd_ref, o_ref, rb_ref, kvf_ref):
        tb = _ret_tables(lgf_ref[...], lgb_ref[...])
        m0 = tb["m0"]
        gain = gain_ref[...]
        rows = range(bsz)
        _ret_kv_states(tb, k_ref, v_ref, rb_ref, kvf_ref, n_chunk)

        def chunk(n, rfs):
            sl = pl.ds(pl.multiple_of(n * c, c), c)
            qs = [q_ref[b, sl, :].astype(F32) * 0.125 for b in rows]
            s01 = [_dot_nt(_split_rows(qs[b], m0), k_ref[b, sl, :]) for b in rows]
            ys = []
            for b in rows:
                lhs = jnp.concatenate([s01[b][0:c] * tb["d0"], s01[b][c:] * tb["d1"],
                                       qs[b] * tb["qdec_f"], qs[b] * tb["qdec_b"]], axis=1).astype(BF16)
                rhs = jnp.concatenate([_split_rows(v_ref[b, sl, :].astype(F32), m0),
                                       rfs[b].astype(BF16), rb_ref[b, n].astype(BF16)], axis=0)
                ys.append(_dot(lhs, rhs))
            new = []
            for b in rows:
                y = ys[b]
                mu = _head_mean(y, m0)
                yc = y - mu
                rstd = lax.rsqrt(_head_mean(yc * yc, m0) + GN_EPS)
                yh = yc * rstd
                g = g_ref[b, sl, :].astype(F32)
                yh_ref[b, sl, :] = yh
                rstd_ref[b, sl, :] = rstd
                o_ref[b, sl, :] = (yh * gain * (g * _sigmoid(g))).astype(BF16)
                new.append(rfs[b] * tb["cdec_f"] + kvf_ref[b, n])
            return tuple(new)

        _loop_grouped(n_chunk, chunk, tuple(jnp.zeros((LANES, LANES), F32) for _ in rows))

    blk, lane, gain, pair = _ret_specs(bsz, s)
    state, state_shape = _ret_state_spec(bsz, n_chunk)
    return _hosted_call(
        body, "ret_fwd", (4,),
        in_specs=[blk(CB_RQ), blk(CB_RK), blk(CB_RV), blk(CB_RG), lane, lane, gain],
        out_specs=[pair, pair, pair, state, state],
        out_shape=[jax.ShapeDtypeStruct((bsz, s, RET_WIDTH), F32), jax.ShapeDtypeStruct((bsz, s, RET_WIDTH), F32),
                   jax.ShapeDtypeStruct((bsz, s, RET_WIDTH), BF16), state_shape, state_shape],
        scratch_shapes=[],
        operands=(u3, u3, u3, u3, lgf_l, lgb_l, gn_gain), rider=rider)


def _ret_bwd(u3, y_hat, y_rstd, states, d_o, lgf_l, lgb_l, gn_gain, rider=None):
    bsz, s, _ = u3.shape
    n_chunk = s // CHUNK
    c = CHUNK

    def body(q_ref, k_ref, v_ref, g_ref, yh_ref, rstd_ref, do_ref, lgf_ref, lgb_ref, gain_ref, rb_ref, kvf_ref,
             dq_ref, dk_ref, dv_ref, dg_ref, part_ref,
             rf_ref, dirf_ref, dy_ref, dk_acc, dv_acc, pa0, pa1, vec_ref):
        tb = _ret_tables(lgf_ref[...], lgb_ref[...])
        m0, bd, row = tb["m0"], tb["bd"], tb["row"]
        gain = gain_ref[...]
        wf = jnp.maximum(tb["diff"], 0.0)
        wb = jnp.maximum(-tb["diff"], 0.0)
        rows = range(bsz)
        zero_states = tuple(jnp.zeros((LANES, LANES), F32) for _ in rows)
        for ref in (pa0, pa1):
            ref[...] = jnp.zeros_like(ref)
        vec_ref[...] = jnp.zeros_like(vec_ref)

        def sweep_fwd(n, carry):
            rfs, gbs = carry
            sl = pl.ds(pl.multiple_of(n * c, c), c)
            qs, ks, vs, dys, dybs, q01, k01, dy01 = [], [], [], [], [], [], [], []
            dgain = jnp.zeros((1, LANES), F32)
            for b in rows:
                q = q_ref[b, sl, :].astype(F32) * 0.125
                k = k_ref[b, sl, :]
                yh = yh_ref[b, sl, :]
                rstd = rstd_ref[b, sl, :]
                do = do_ref[b, sl, :].astype(F32)
                g = g_ref[b, sl, :].astype(F32)
                sg = _sigmoid(g)
                sil = g * sg
                dyh = do * gain * sil
                dg_ref[b, sl, :] = (do * yh * gain * sg * (1.0 + g * (1.0 - sg))).astype(BF16)
                dgain = dgain + jnp.sum(do * yh * sil, axis=0, keepdims=True)
                dy = rstd * (dyh - _head_mean(dyh, m0) - yh * _head_mean(dyh * yh, m0))
                dyb = dy.astype(BF16)
                dy_ref[b, sl, :] = dyb
                rf_ref[b, n] = rfs[b]
                qs.append(q)
                ks.append(k)
                vs.append(v_ref[b, sl, :])
                dys.append(dy)
                dybs.append(dyb)
                q01.append(_split_rows(q, m0))
                k01.append(_split_rows(k.astype(F32), m0))
                dy01.append(_split_rows(dy, m0))
            s01 = [_dot_nt(q01[b], ks[b]) for b in rows]
            da01 = [_dot_nt(dy01[b], vs[b]) for b in rows]
            rbn = [rb_ref[b, n] for b in rows]
            states = [jnp.concatenate([rfs[b], rbn[b]], axis=0).astype(BF16) for b in rows]
            dqc = [_dot_nt(dybs[b], states[b]) for b in rows]
            gbb = [gbs[b].astype(BF16) for b in rows]
            dkb = [_dot_nt(vs[b], gbb[b]) for b in rows]
            qfb = [jnp.concatenate([qs[b] * tb["qdec_f"], qs[b] * tb["qdec_b"]], axis=1) for b in rows]
            direct = [_dot_tn(qfb[b].astype(BF16), dybs[b]) for b in rows]
            ds_cat, ds_rows, a_rows = [], [], []
            for b in rows:
                a0 = s01[b][0:c] * tb["d0"]
                a1 = s01[b][c:] * tb["d1"]
                pa0[...] += da01[b][0:c] * a0
                pa1[...] += da01[b][c:] * a1
                ds0 = da01[b][0:c] * tb["d0"]
                ds1 = da01[b][c:] * tb["d1"]
                ds_cat.append(jnp.concatenate([ds0, ds1], axis=1).astype(BF16))
                ds_rows.append(jnp.concatenate([ds0, ds1], axis=0).astype(BF16))
                a_rows.append(jnp.concatenate([a0, a1], axis=0).astype(BF16))
            kbd = [ks[b].astype(F32) * tb["kdec_b"] for b in rows]
            dq_in = [_dot(ds_cat[b], k01[b]) for b in rows]
            dk_in = [_dot_tn(ds_rows[b], q01[b]) for b in rows]
            dv_in = [_dot_tn(a_rows[b], dy01[b]) for b in rows]
            dv_gb = [_dot(kbd[b].astype(BF16), gbb[b]) for b in rows]
            new_rf, new_gb = [], []
            dlf = jnp.zeros((1, LANES), F32)
            dlb = jnp.zeros((1, LANES), F32)
            for b in rows:
                dqf, dqb = dqc[b][:, 0:LANES], dqc[b][:, LANES:]
                qf, qb = qfb[b][:, 0:LANES], qfb[b][:, LANES:]
                dq = dq_in[b] + dqf * tb["qdec_f"] + dqb * tb["qdec_b"]
                dq_ref[b, sl, :] = (dq * 0.125).astype(BF16)
                dk_acc[b, sl, :] = dk_in[b] + dkb[b] * tb["kdec_b"]
                dv_acc[b, sl, :] = dv_in[b] + dv_gb[b]
                dlf = dlf + jnp.sum((row + 1.0) * qf * dqf, axis=0, keepdims=True)
                dlb = dlb + jnp.sum((c - row) * qb * dqb + row * kbd[b] * dkb[b], axis=0, keepdims=True)
                dlb = dlb + c * tb["cdec_b"] * jnp.sum(gbs[b] * rbn[b], axis=0, keepdims=True)
                dirf_ref[b, n] = jnp.where(bd, direct[b][0:LANES], 0.0)
                new_gb.append(jnp.where(bd, direct[b][LANES:], 0.0) + tb["cdec_b"] * gbs[b])
                new_rf.append(rfs[b] * tb["cdec_f"] + kvf_ref[b, n])
            vec_ref[0:1, :] += dlf
            vec_ref[1:2, :] += dlb
            vec_ref[6:7, :] += dgain
            return tuple(new_rf), tuple(new_gb)

        _loop_grouped(n_chunk, sweep_fwd, (zero_states, zero_states), per_trip=4)

        def sweep_bwd(i, gfs):
            n = n_chunk - 1 - i
            sl = pl.ds(pl.multiple_of(n * c, c), c)
            gfb = [gfs[b].astype(BF16) for b in rows]
            kfd = [k_ref[b, sl, :].astype(F32) * tb["kdec_f"] for b in rows]
            dkf = [_dot_nt(v_ref[b, sl, :], gfb[b]) for b in rows]
            dvf = [_dot(kfd[b].astype(BF16), gfb[b]) for b in rows]
            new = []
            dlf = jnp.zeros((1, LANES), F32)
            for b in rows:
                dk_ref[b, sl, :] = (dk_acc[b, sl, :] + dkf[b] * tb["kdec_f"]).astype(BF16)
                dv_ref[b, sl, :] = (dv_acc[b, sl, :] + dvf[b]).astype(BF16)
                dlf = dlf + jnp.sum((c - 1.0 - row) * kfd[b] * dkf[b], axis=0, keepdims=True)
                dlf = dlf + c * tb["cdec_f"] * jnp.sum(gfs[b] * rf_ref[b, n], axis=0, keepdims=True)
                new.append(dirf_ref[b, n] + tb["cdec_f"] * gfs[b])
            vec_ref[0:1, :] += dlf
            return tuple(new)

        _loop_grouped(n_chunk, sweep_bwd, zero_states)
        vec_ref[2:3, :] = jnp.sum(pa0[...] * wf, axis=0, keepdims=True)
        vec_ref[3:4, :] = jnp.sum(pa1[...] * wf, axis=0, keepdims=True)
        vec_ref[4:5, :] = jnp.sum(pa0[...] * wb, axis=0, keepdims=True)
        vec_ref[5:6, :] = jnp.sum(pa1[...] * wb, axis=0, keepdims=True)
        part_ref[...] = vec_ref[...]

    blk, lane, gain, pair = _ret_specs(bsz, s)
    out_bf = jax.ShapeDtypeStruct((bsz, s, RET_WIDTH), BF16)
    state = pltpu.VMEM((bsz, n_chunk, LANES, LANES), F32)
    saved = _ret_state_spec(bsz, n_chunk)[0]
    return _hosted_call(
        body, "ret_bwd", (4,),
        in_specs=[blk(CB_RQ), blk(CB_RK), blk(CB_RV), blk(CB_RG), pair, pair, pair, lane, lane, gain, saved, saved],
        out_specs=[pair, pair, pair, pair, pl.BlockSpec((None, 8, LANES), lambda p: (p, 0, 0))],
        out_shape=[out_bf, out_bf, out_bf, out_bf, jax.ShapeDtypeStruct((4, 8, LANES), F32)],
        scratch_shapes=[state, state,
                        pltpu.VMEM((bsz, s, LANES), BF16), pltpu.VMEM((bsz, s, LANES), F32),
                        pltpu.VMEM((bsz, s, LANES), F32),
                        pltpu.VMEM((c, c), F32), pltpu.VMEM((c, c), F32), pltpu.VMEM((8, LANES), F32)],
        operands=(u3, u3, u3, u3, y_hat, y_rstd, d_o, lgf_l, lgb_l, gn_gain, *states), rider=rider)


def _attn_window_tables(n, s):
    qi = lax.broadcasted_iota(jnp.int32, (CHUNK, 3 * CHUNK), 0)
    kj = lax.broadcasted_iota(jnp.int32, (CHUNK, 3 * CHUNK), 1)
    dist = jnp.abs(kj - CHUNK - qi)
    kpos = n * CHUNK - CHUNK + kj
    valid = (dist <= CHUNK) & (kpos >= 0) & (kpos < s)
    return dist.astype(F32), valid


def _dup_kv_head(x, g):
    lane = lax.broadcasted_iota(jnp.int32, x.shape, 1)
    keep = (lane < HEAD_DIM) == (g == 0)
    xf = x.astype(F32)
    return jnp.where(keep, xf, pltpu.roll(xf, HEAD_DIM, 1))


def _attn_specs(s):
    q = pl.BlockSpec((None, s, 2 * LANES), lambda b, g: (b, 0, CB_AQ // 2 + g))
    k = pl.BlockSpec((None, s, LANES), lambda b, g: (b, 0, CB_AK))
    v = pl.BlockSpec((None, s, LANES), lambda b, g: (b, 0, CB_AV))
    grp = pl.BlockSpec((None, s, 2 * LANES), lambda b, g: (b, 0, g))
    smem = pl.BlockSpec(memory_space=pltpu.SMEM)
    return q, k, v, grp, smem


def _fill_padded(dst_ref, val, s):
    dst_ref[0:CHUNK, :] = jnp.zeros((CHUNK, LANES), dst_ref.dtype)
    dst_ref[CHUNK:CHUNK + s, :] = val.astype(dst_ref.dtype)
    dst_ref[CHUNK + s:2 * CHUNK + s, :] = jnp.zeros((CHUNK, LANES), dst_ref.dtype)


def _attn_probs(sc, slope, snk, dist, valid):
    sc = jnp.where(valid, sc - slope * dist, NEG_INF)
    m = jnp.maximum(jnp.max(sc, axis=1, keepdims=True), snk)
    e = jnp.exp(sc - m)
    es = jnp.exp(snk - m)
    inv = 1.0 / (jnp.sum(e, axis=1, keepdims=True) + es)
    return e * inv, es * inv


def _stack_heads(x2, m0):
    parts = []
    for pr in range(2):
        xp = x2[:, pr * LANES:(pr + 1) * LANES]
        parts += [jnp.where(m0, xp, 0.0), jnp.where(m0, 0.0, xp)]
    return jnp.concatenate(parts, axis=0).astype(BF16)


def _unstack_pair(x_all, pr, m0):
    return jnp.where(m0, x_all[(2 * pr) * CHUNK:(2 * pr + 1) * CHUNK], x_all[(2 * pr + 1) * CHUNK:(2 * pr + 2) * CHUNK])


def _attn_saved_specs(bsz, n_blk):
    specs = [pl.BlockSpec((None, None, n_blk, 4 * CHUNK, w), lambda b, g: (b, g, 0, 0, 0)) for w in (3 * CHUNK, 1)]
    shapes = [jax.ShapeDtypeStruct((bsz, 2, n_blk, 4 * CHUNK, 3 * CHUNK), BF16),
              jax.ShapeDtypeStruct((bsz, 2, n_blk, 4 * CHUNK, 1), F32)]
    return specs, shapes


def _attn_fwd(u3, slopes, sink, rider=None):
    bsz, s, _ = u3.shape
    n_blk = s // CHUNK

    def body(slope_ref, sink_ref, q_ref, k_ref, v_ref, o_ref, p_ref, ps_ref, kp_ref, vp_ref):
        g = pl.program_id(1)
        _fill_padded(kp_ref, _dup_kv_head(k_ref[...], g), s)
        _fill_padded(vp_ref, _dup_kv_head(v_ref[...], g), s)
        m0 = lax.broadcasted_iota(jnp.int32, (CHUNK, LANES), 1) < HEAD_DIM

        def blk(n, carry):
            r0 = pl.multiple_of(n * CHUNK, CHUNK)
            kw = kp_ref[pl.ds(r0, 3 * CHUNK), :]
            vw = vp_ref[pl.ds(r0, 3 * CHUNK), :]
            dist, valid = _attn_window_tables(n, s)
            q_all = _stack_heads(q_ref[pl.ds(r0, CHUNK), :].astype(F32) * 0.125, m0)
            sc_all = _dot_nt(q_all, kw)
            probs, sinks = [], []
            for i in range(4):
                p, ps = _attn_probs(sc_all[i * CHUNK:(i + 1) * CHUNK], slope_ref[g * 4 + i], sink_ref[g * 4 + i],
                                    dist, valid)
                probs.append(p.astype(BF16))
                sinks.append(ps)
            p_all = jnp.concatenate(probs, axis=0)
            p_ref[n] = p_all
            ps_ref[n] = jnp.concatenate(sinks, axis=0)
            out_all = _dot(p_all, vw)
            for pr in range(2):
                o_ref[pl.ds(r0, CHUNK), pr * LANES:(pr + 1) * LANES] = _unstack_pair(out_all, pr, m0).astype(BF16)
            return carry

        lax.fori_loop(0, n_blk, blk, 0, unroll=4)

    q, k, v, grp, smem = _attn_specs(s)
    saved_specs, saved_shapes = _attn_saved_specs(bsz, n_blk)
    return _hosted_call(
        body, "attn_fwd", (bsz, 2),
        in_specs=[smem, smem, q, k, v],
        out_specs=[grp] + saved_specs,
        out_shape=[jax.ShapeDtypeStruct((bsz, s, ATTN_WIDTH), BF16)] + saved_shapes,
        scratch_shapes=[pltpu.VMEM((s + 2 * CHUNK, LANES), BF16), pltpu.VMEM((s + 2 * CHUNK, LANES), BF16)],
        operands=(slopes, sink, u3, u3, u3), rider=rider)


def _attn_bwd(u3, d_o, probs, sink_probs, rider=None):
    bsz, s, _ = u3.shape
    n_blk = s // CHUNK

    def body(q_ref, k_ref, v_ref, do_ref, p_ref, ps_ref, dq_ref, dkv_ref, ds_ref,
             kp_ref, vp_ref, dk_acc, dv_acc):
        g = pl.program_id(1)
        _fill_padded(kp_ref, _dup_kv_head(k_ref[...], g), s)
        _fill_padded(vp_ref, _dup_kv_head(v_ref[...], g), s)
        dk_acc[...] = jnp.zeros_like(dk_acc)
        dv_acc[...] = jnp.zeros_like(dv_acc)
        m0 = lax.broadcasted_iota(jnp.int32, (CHUNK, LANES), 1) < HEAD_DIM

        def blk(n, dsink):
            r0 = pl.multiple_of(n * CHUNK, CHUNK)
            win = pl.ds(r0, 3 * CHUNK)
            kw = kp_ref[win, :]
            vw = vp_ref[win, :]
            q_all = _stack_heads(q_ref[pl.ds(r0, CHUNK), :].astype(F32) * 0.125, m0)
            do_all = _stack_heads(do_ref[pl.ds(r0, CHUNK), :].astype(F32), m0)
            p_all = p_ref[n]
            ps_all = ps_ref[n]
            dp_all = _dot_nt(do_all, vw)
            new_dsink, dscs = [], []
            for i in range(4):
                rows = slice(i * CHUNK, (i + 1) * CHUNK)
                p = p_all[rows].astype(F32)
                dp = dp_all[rows]
                delta = jnp.sum(p * dp, axis=1, keepdims=True)
                dscs.append((p * (dp - delta)).astype(BF16))
                dsh = jnp.sum(ps_all[rows] * delta, axis=0, keepdims=True)
                new_dsink.append(dsink[i] - jnp.broadcast_to(dsh, (1, LANES)))
            dsc_all = jnp.concatenate(dscs, axis=0)
            dq_all = _dot(dsc_all, kw)
            dk_acc[win, :] += _dot_tn(dsc_all, q_all)
            dv_acc[win, :] += _dot_tn(p_all, do_all)
            for pr in range(2):
                dq_ref[pl.ds(r0, CHUNK), pr * LANES:(pr + 1) * LANES] = (
                    _unstack_pair(dq_all, pr, m0) * 0.125).astype(BF16)
            return tuple(new_dsink)

        dsink = _loop_grouped(n_blk, blk, tuple(jnp.zeros((1, LANES), F32) for _ in range(4)), per_trip=4)
        dk = dk_acc[CHUNK:CHUNK + s, :]
        dv = dv_acc[CHUNK:CHUNK + s, :]
        lane = lax.broadcasted_iota(jnp.int32, (s, LANES), 1)
        fold = lambda a: a + pltpu.roll(a, HEAD_DIM, 1)
        dkv_ref[...] = jnp.where(lane < HEAD_DIM, fold(dk), fold(dv)).astype(BF16)
        ds_ref[...] = jnp.zeros_like(ds_ref)
        for i in range(4):
            ds_ref[i:i + 1, :] = dsink[i]

    q, k, v, grp, _ = _attn_specs(s)
    return _hosted_call(
        body, "attn_bwd", (bsz, 2),
        in_specs=[q, k, v, grp] + _attn_saved_specs(bsz, n_blk)[0],
        out_specs=[grp, pl.BlockSpec((None, s, LANES), lambda b, g: (b, 0, g)),
                   pl.BlockSpec((None, None, 8, LANES), lambda b, g: (b, g, 0, 0))],
        out_shape=[jax.ShapeDtypeStruct((bsz, s, ATTN_WIDTH), BF16), jax.ShapeDtypeStruct((bsz, s, 2 * LANES), BF16),
                   jax.ShapeDtypeStruct((bsz, 2, 8, LANES), F32)],
        scratch_shapes=[pltpu.VMEM((s + 2 * CHUNK, LANES), BF16), pltpu.VMEM((s + 2 * CHUNK, LANES), BF16),
                        pltpu.VMEM((s + 2 * CHUNK, LANES), F32), pltpu.VMEM((s + 2 * CHUNK, LANES), F32)],
        operands=(u3, u3, u3, d_o, probs, sink_probs), rider=rider)


def _ffn_bwd(dz2, gs, us, pg, ple, zh1, r1, g1, wg4, wu4, wd4, wpg, w_out):
    t = dz2.shape[0]
    tm = 256
    wg_t, wu_t, wd_all = (w.reshape(FFN, D_MODEL) for w in (wg4, wu4, wd4))

    def body(dz_ref, gs_ref, us_ref, pg_ref, ple_ref, zh_ref, r_ref, g1_ref,
             wg_hbm, wu_hbm, wd_hbm, wpg_hbm, wo_hbm,
             dgs_ref, dus_ref, dsp_ref, dple_ref, dz1_ref, dyr_ref, dya_ref, dg1_ref, db1_ref,
             wg, wu, wd, wpg, wo, wsem):
        step = pl.program_id(0)
        loads = _resident_quarters(wd_hbm, wd) + _resident_quarters(wg_hbm, wg) + _resident_quarters(wu_hbm, wu)
        _load_resident(step, loads + [(wpg_hbm, wpg), (wo_hbm, wo)], wsem)

        @pl.when(step == 0)
        def _():
            dg1_ref[...] = jnp.zeros_like(dg1_ref)
            db1_ref[...] = jnp.zeros_like(db1_ref)

        dz = dz_ref[...]
        dzb = dz.astype(BF16)
        dh = ALPHA * dz
        pending = []
        chunks = [slice(n * FFN_CHUNK, (n + 1) * FFN_CHUNK) for n in range(N_FFN_CHUNK)]
        for n in range(N_FFN_CHUNK + 1):
            if n < N_FFN_CHUNK:
                da = _dot_nt(dzb, wd[chunks[n], :])
                gj = gs_ref[:, chunks[n]].astype(F32)
                uj = us_ref[:, chunks[n]].astype(F32)
                sg = _sigmoid(gj)
                dgj = (da * uj * sg * (1.0 + gj * (1.0 - sg))).astype(BF16)
                duj = (da * gj * sg).astype(BF16)
                dgs_ref[:, chunks[n]] = dgj
                dus_ref[:, chunks[n]] = duj
                pending.append((dgj, duj))
            if n > 0:
                dgp, dup = pending[n - 1]
                dh = dh + _dot(dgp, wg[chunks[n - 1], :]) + _dot(dup, wu[chunks[n - 1], :])
        pgv = pg_ref[...].astype(F32)
        plev = ple_ref[...].astype(F32)
        dple_ref[...] = (dz * pgv).astype(BF16)
        dsp = (dz * plev * pgv * (1.0 - pgv)).astype(BF16)
        dsp_ref[...] = dsp
        dh = dh + _dot_nt(dsp, wpg[...])
        zh = zh_ref[...]
        dg1_ref[...] += jnp.sum(dh * zh, axis=0, keepdims=True)
        db1_ref[...] += jnp.sum(dh, axis=0, keepdims=True)
        dzh = dh * g1_ref[...]
        m1 = jnp.mean(dzh, axis=1, keepdims=True)
        m2 = jnp.mean(dzh * zh, axis=1, keepdims=True)
        dz1 = r_ref[...] * (dzh - m1 - zh * m2)
        dz1_ref[...] = dz1
        dyc = _dot_nt(dz1.astype(BF16), wo[...])
        dyr_ref[...] = dyc[:, 0:RET_WIDTH].astype(BF16)
        dya_ref[...] = dyc[:, RET_WIDTH:].astype(BF16)

    row = lambda w: pl.BlockSpec((tm, w), lambda i: (i, 0))
    const = lambda s: pl.BlockSpec(s, lambda i: (0, 0))
    hbm = pl.BlockSpec(memory_space=pl.ANY)
    hid_shape = jax.ShapeDtypeStruct((t, FFN), BF16)
    return pl.pallas_call(
        body, name="ffn_bwd", grid=(t // tm,),
        in_specs=[row(D_MODEL), row(FFN), row(FFN), row(D_MODEL), row(D_MODEL), row(D_MODEL), row(1),
                  const((1, D_MODEL)), hbm, hbm, hbm, hbm, hbm],
        out_specs=[row(FFN), row(FFN), row(D_MODEL), row(D_MODEL), row(D_MODEL), row(RET_WIDTH), row(ATTN_WIDTH),
                   const((1, D_MODEL)), const((1, D_MODEL))],
        out_shape=[hid_shape, hid_shape, jax.ShapeDtypeStruct((t, D_MODEL), BF16),
                   jax.ShapeDtypeStruct((t, D_MODEL), BF16), jax.ShapeDtypeStruct((t, D_MODEL), F32),
                   jax.ShapeDtypeStruct((t, RET_WIDTH), BF16), jax.ShapeDtypeStruct((t, ATTN_WIDTH), BF16),
                   jax.ShapeDtypeStruct((1, D_MODEL), F32), jax.ShapeDtypeStruct((1, D_MODEL), F32)],
        scratch_shapes=[pltpu.VMEM((FFN, D_MODEL), BF16), pltpu.VMEM((FFN, D_MODEL), BF16),
                        pltpu.VMEM((FFN, D_MODEL), BF16),
                        pltpu.VMEM(wpg.shape, BF16), pltpu.VMEM(w_out.shape, BF16),
                        pltpu.SemaphoreType.DMA((3 * N_SHARD + 2,))],
        compiler_params=_params("arbitrary", vmem=VMEM_LIMIT),
    )(dz2, gs, us, pg, ple, zh1, r1, g1, wg_t, wu_t, wd_all, wpg, w_out)


def _wgrad_misc(y_ret, y_att, dz1, hb, dsp, p2d, dple, rider=None):
    t = dz1.shape[0]
    tk = min(t, 512)
    pc = D_MODEL // N_SHARD

    def body(yr_ref, ya_ref, dz_ref, hb_ref, dsp_ref, p_ref, dple_ref, wo_ref, wpg_ref, wpe_ref):
        @pl.when(pl.program_id(0) == 0)
        def _():
            wo_ref[...] = jnp.zeros_like(wo_ref)
            wpg_ref[...] = jnp.zeros_like(wpg_ref)
            wpe_ref[...] = jnp.zeros_like(wpe_ref)

        dzb = dz_ref[...].astype(BF16)
        wo_ref[0:RET_WIDTH, :] += _dot_tn(yr_ref[...], dzb)
        wo_ref[RET_WIDTH:, :] += _dot_tn(ya_ref[...], dzb)
        wpg_ref[...] += _dot_tn(hb_ref[...], dsp_ref[...])
        dpe = _dot_tn(p_ref[...].astype(BF16), dple_ref[...])
        for j in range(N_SHARD):
            wpe_ref[j] += dpe[:, j * pc:(j + 1) * pc]

    row = lambda w: pl.BlockSpec((tk, w), lambda k: (k, 0))
    const = lambda s: pl.BlockSpec(s, lambda k: (0,) * len(s))
    return _hosted_call(
        body, "wgrad_misc", (t // tk,),
        in_specs=[row(RET_WIDTH), row(ATTN_WIDTH), row(D_MODEL), row(D_MODEL), row(D_MODEL), row(PLE_DIM),
                  row(D_MODEL)],
        out_specs=[const((D_MODEL, D_MODEL)), const((D_MODEL, D_MODEL)), const((N_SHARD, PLE_DIM, pc))],
        out_shape=[jax.ShapeDtypeStruct((D_MODEL, D_MODEL), F32), jax.ShapeDtypeStruct((D_MODEL, D_MODEL), F32),
                   jax.ShapeDtypeStruct((N_SHARD, PLE_DIM, pc), F32)],
        scratch_shapes=[], operands=(y_ret, y_att, dz1, hb, dsp, p2d, dple), rider=rider, semantics=["arbitrary"])


def _wgrad_ffn(acts, dgs, dus, hb, dz2b):
    t = dz2b.shape[0]
    tk = min(t, 512)
    nk = t // tk

    def body(act_ref, dg_ref, du_ref, hb_ref, dz_ref, og_ref, ou_ref, od_ref):
        @pl.when(pl.program_id(1) == 0)
        def _():
            og_ref[...] = jnp.zeros_like(og_ref)
            ou_ref[...] = jnp.zeros_like(ou_ref)
            od_ref[...] = jnp.zeros_like(od_ref)

        hbv = hb_ref[...]
        og_ref[...] += _dot_tn(dg_ref[...], hbv)
        ou_ref[...] += _dot_tn(du_ref[...], hbv)
        od_ref[...] += _dot_tn(act_ref[...], dz_ref[...])

    half = FFN // 2
    a_spec = pl.BlockSpec((tk, half), lambda j, k: (k, j))
    b_spec = pl.BlockSpec((tk, D_MODEL), lambda j, k: (k, 0))
    o_spec = pl.BlockSpec((half, D_MODEL), lambda j, k: (j, 0))
    o_shape = jax.ShapeDtypeStruct((FFN, D_MODEL), F32)
    outs = pl.pallas_call(
        body, name="wgrad_ffn", grid=(2, nk),
        in_specs=[a_spec, a_spec, a_spec, b_spec, b_spec],
        out_specs=[o_spec] * 3, out_shape=[o_shape] * 3,
        compiler_params=_params("parallel", "arbitrary", vmem=VMEM_LIMIT),
    )(acts, dgs, dus, hb, dz2b)
    return [o.reshape(N_SHARD, FFN_SHARD, D_MODEL) for o in outs]


KV_ORDER = (0, 128, 64, 192)


def _wgrad_in(pieces, x2d, rider=None):
    t = x2d.shape[0]
    tk = min(t, 512)
    nk = t // tk
    kv0 = CB_AK * LANES

    def body(p0, p1, p2, p3, p4, pkv, x_ref, o_ref):
        @pl.when(pl.program_id(0) == 0)
        def _():
            o_ref[...] = jnp.zeros_like(o_ref)

        xb = x_ref[...].astype(BF16)
        for i, ref in enumerate((p0, p1, p2, p3, p4)):
            o_ref[i * 512:(i + 1) * 512, :] += _dot_tn(ref[...], xb)
        dkv = _dot_tn(pkv[...], xb)
        for i, o in enumerate(KV_ORDER):
            o_ref[kv0 + o:kv0 + o + HEAD_DIM, :] += dkv[i * HEAD_DIM:(i + 1) * HEAD_DIM]

    row = lambda w: pl.BlockSpec((tk, w), lambda k: (k, 0))
    return _hosted_call(
        body, "wgrad_in", (nk,),
        in_specs=[row(512)] * 5 + [row(256), row(D_MODEL)],
        out_specs=[pl.BlockSpec((IN_WIDTH, D_MODEL), lambda k: (0, 0))],
        out_shape=[jax.ShapeDtypeStruct((IN_WIDTH, D_MODEL), F32)],
        scratch_shapes=[], operands=(*pieces, x2d), rider=rider, semantics=["arbitrary"])


def _inproj_bwd(dz1, pieces, w_in_t, rider=None):
    t = dz1.shape[0]
    tm = 512
    kv0 = CB_AK * LANES

    def body(dz_ref, p0, p1, p2, p3, p4, pkv, w_ref, o_ref):
        acc = ALPHA * dz_ref[...]
        for i, ref in enumerate((p0, p1, p2, p3, p4)):
            acc = acc + _dot(ref[...], w_ref[i * 512:(i + 1) * 512, :])
        w_kv = jnp.concatenate([w_ref[kv0 + o:kv0 + o + HEAD_DIM, :] for o in KV_ORDER], axis=0)
        o_ref[...] = acc + _dot(pkv[...], w_kv)

    row = lambda w: pl.BlockSpec((tm, w), lambda i: (i, 0))
    return _hosted_call(
        body, "inproj_bwd", (t // tm,),
        in_specs=[row(D_MODEL)] + [row(512)] * 5 + [row(256), pl.BlockSpec((IN_WIDTH, D_MODEL), lambda i: (0, 0))],
        out_specs=[row(D_MODEL)],
        out_shape=[jax.ShapeDtypeStruct((t, D_MODEL), F32)],
        scratch_shapes=[], operands=(dz1, *pieces, w_in_t), rider=rider)


def _coords():
    return lax.axis_index("x"), lax.axis_index("y"), lax.axis_index("c")


def _chip_of(x, y, rel):
    return (1 - x if rel & 2 else x), (1 - y if rel & 1 else y)


def _gather_and_cast(shard, others):
    near = _gather_near_rider([shard])
    relay = _gather_relay_rider(near.out_shapes, chained=True)
    pass_near = _gather_pass_rider(near.out_shapes, chained=True, rels=NEAR)
    pass_far = _gather_pass_rider(near.out_shapes, chained=True, rels=(3,))
    riders = [near, relay, pass_near, pass_far]
    no = len(others)

    def body(*refs):
        shard_ref, wide = refs[0], refs[1:1 + no]
        out_ref, narrow = refs[1 + no], refs[2 + no:2 + 2 * no]
        k = 2 + 2 * no
        vin, vout, (lsem, ssem) = refs[k:k + no], refs[k + no:k + 2 * no], refs[k + 2 * no:k + 2 * no + 2]
        k += 2 * no + 2
        sems = {}
        for r in riders:
            sems[id(r)] = refs[k:k + len(r.sems)]
            k += len(r.sems)
        run = lambda r, method: getattr(r, method)([shard_ref], [out_ref], sems[id(r)])
        loads = [pltpu.make_async_copy(wide[w], vin[w], lsem.at[w]) for w in range(no)]
        stores = [pltpu.make_async_copy(vout[w], narrow[w], ssem.at[w]) for w in range(no)]

        run(near, "start")
        for cp in loads:
            cp.start()
        for w in range(no):
            loads[w].wait()
            vout[w][...] = vin[w][...].astype(BF16)
            stores[w].start()
        run(near, "finish")
        run(relay, "start")
        run(pass_near, "start")
        run(relay, "finish")
        run(pass_far, "start")
        run(pass_near, "finish")
        run(pass_far, "finish")
        for cp in stores:
            cp.wait()

    hbm = pl.BlockSpec(memory_space=pl.ANY)
    dma = pltpu.SemaphoreType.DMA
    gathered, *cast = pl.pallas_call(
        body, name="gather_weights", in_specs=[hbm] * (1 + no), out_specs=[hbm] * (1 + no),
        out_shape=near.out_shapes + [jax.ShapeDtypeStruct(a.shape, BF16) for a in others],
        scratch_shapes=[pltpu.VMEM(a.shape, F32) for a in others] + [pltpu.VMEM(a.shape, BF16) for a in others]
        + [dma((no,)), dma((no,))] + [s for r in riders for s in r.sems],
        compiler_params=_params(vmem=VMEM_LIMIT),
    )(shard, *others)
    return gathered, cast


def _gather_half(outs, w, chip, cc):
    h = outs[w].shape[1] // 2
    return outs[w].at[chip, pl.ds(cc * h, h), :]


NEAR = (1, 2)


def _gather_near_rider(shards, rels=NEAR):
    nw, nr = len(shards), len(rels)

    def copies(ins, outs, sems, arrivals):
        send, recv, lsend, lrecv = sems
        x, y, c = _coords()
        me = 2 * x + y
        own = [pltpu.make_async_remote_copy(
            src_ref=ins[w], dst_ref=outs[w].at[me], send_sem=lsend.at[w], recv_sem=lrecv.at[w],
            device_id=(x, y, 1 - c), device_id_type=MESH) for w in range(nw)]
        out, arrive = [], []
        for i, rel in enumerate(rels):
            kx, ky = _chip_of(x, y, rel)
            for w in range(nw):
                h = shards[w].shape[0] // 2
                sem = dict(send_sem=send.at[w * nr + i], recv_sem=recv.at[w * nr + i],
                           device_id=(kx, ky, c), device_id_type=MESH)
                out.append(pltpu.make_async_remote_copy(
                    src_ref=ins[w].at[pl.ds(c * h, h), :], dst_ref=_gather_half(outs, w, me, c), **sem))
                if arrivals:
                    theirs = _gather_half(outs, w, 2 * kx + ky, c)
                    arrive.append(pltpu.make_async_remote_copy(src_ref=theirs, dst_ref=theirs, **sem))
        return own, out, arrive

    def start(ins, outs, sems):
        own, out, _ = copies(ins, outs, sems, arrivals=False)
        for cp in own + out:
            cp.start()

    def finish(ins, outs, sems):
        own, out, arrive = copies(ins, outs, sems, arrivals=True)
        for cp in arrive:
            cp.wait_recv()
        for cp in out:
            cp.wait_send()
        for cp in own:
            cp.wait()

    dma = pltpu.SemaphoreType.DMA
    return _Rider(shards, [jax.ShapeDtypeStruct((N_SHARD,) + s.shape, s.dtype) for s in shards],
                  [dma((nr * nw,)), dma((nr * nw,)), dma((nw,)), dma((nw,))], start, finish)


def _gather_relay_rider(gathered, chained=False):
    nw = len(gathered)

    def quarter(outs, w, chip, c, p):
        q = outs[w].shape[1] // 4
        return outs[w].at[chip, pl.ds(c * 2 * q + p * q, q), :]

    def copies(outs, sems):
        send, recv = sems
        x, y, c = _coords()
        (yx, yy), (xx, xy), (dx, dy) = (_chip_of(x, y, rel) for rel in (1, 2, 3))
        out, arrive = [], []
        for w in range(nw):
            for p, (src_chip, dst) in enumerate(((2 * xx + xy, (yx, yy)), (2 * yx + yy, (xx, xy)))):
                rows = quarter(outs, w, src_chip, c, p)
                sem = dict(send_sem=send.at[w * 2 + p], recv_sem=recv.at[w * 2 + p], device_id_type=MESH)
                out.append(pltpu.make_async_remote_copy(src_ref=rows, dst_ref=rows, device_id=(*dst, c), **sem))
                mine = quarter(outs, w, 2 * dx + dy, c, p)
                arrive.append(pltpu.make_async_remote_copy(src_ref=mine, dst_ref=mine, device_id=(*dst, c), **sem))
        return out, arrive

    def start(ins, outs, sems):
        for cp in copies(outs, sems)[0]:
            cp.start()

    def finish(ins, outs, sems):
        out, arrive = copies(outs, sems)
        for cp in arrive:
            cp.wait_recv()
        for cp in out:
            cp.wait_send()

    dma = pltpu.SemaphoreType.DMA
    shapes = [jax.ShapeDtypeStruct(g.shape, g.dtype) for g in gathered]
    if chained:
        return _Rider([], [], [dma((2 * nw,)), dma((2 * nw,))], start, finish)
    return _Rider(gathered, shapes, [dma((2 * nw,)), dma((2 * nw,))], start, finish,
                  aliases={w: w for w in range(nw)})


def _gather_pass_rider(gathered, chained=False, rels=(1, 2, 3), sibling_id=None):
    nw, nr = len(gathered), len(rels)

    def copies(outs, sems, cc):
        send, recv = sems
        x, y, c = _coords()
        res = []
        for i, rel in enumerate(rels):
            kx, ky = _chip_of(x, y, rel)
            for w in range(nw):
                rows = _gather_half(outs, w, 2 * kx + ky, cc)
                res.append(pltpu.make_async_remote_copy(
                    src_ref=rows, dst_ref=rows, send_sem=send.at[w * nr + i], recv_sem=recv.at[w * nr + i],
                    device_id=(x, y, 1 - c), device_id_type=MESH))
        return res

    def start(ins, outs, sems):
        for cp in copies(outs, sems, lax.axis_index("c")):
            cp.start()

    def finish(ins, outs, sems):
        c = lax.axis_index("c")
        for cp in copies(outs, sems, 1 - c):
            cp.wait_recv()
        for cp in copies(outs, sems, c):
            cp.wait_send()

    dma = pltpu.SemaphoreType.DMA
    shapes = [jax.ShapeDtypeStruct(g.shape, g.dtype) for g in gathered]
    if chained:
        return _Rider([], [], [dma((nr * nw,)), dma((nr * nw,))], start, finish)
    return _Rider(gathered, shapes, [dma((nr * nw,)), dma((nr * nw,))], start, finish,
                  aliases={w: w for w in range(nw)}, sibling_id=sibling_id)


def _exchange_halves_rider(parts, sibling_id=None):
    nw = len(parts)

    def copies(ins, outs, sems):
        send, recv = sems
        x, y, c = _coords()
        res = []
        for w in range(nw):
            h = parts[w].shape[1] // 2
            res.append(pltpu.make_async_remote_copy(
                src_ref=ins[w].at[:, pl.ds((1 - c) * h, h), :], dst_ref=outs[w],
                send_sem=send.at[w], recv_sem=recv.at[w], device_id=(x, y, 1 - c), device_id_type=MESH))
        return res

    def start(ins, outs, sems):
        for cp in copies(ins, outs, sems):
            cp.start()

    def finish(ins, outs, sems):
        for cp in copies(ins, outs, sems):
            cp.wait()

    dma = pltpu.SemaphoreType.DMA
    return _Rider(parts, [jax.ShapeDtypeStruct((N_SHARD, p.shape[1] // 2, p.shape[2]), p.dtype) for p in parts],
                  [dma((nw,)), dma((nw,))], start, finish, sibling_id=sibling_id)


def _add_halves(parts, theirs, pos):
    nw = len(parts)
    split = 1

    def body(pos_ref, *refs):
        ins, oth = refs[:nw], refs[nw:2 * nw]
        o32, o16 = refs[2 * nw:3 * nw], refs[3 * nw:]
        sums = [ins[w][...] + oth[w][...] for w in range(nw)]
        for w in range(nw):
            o16[w][...] = sums[w].astype(BF16)

        @pl.when(pl.program_id(1) == pos_ref[0])
        def _():
            for w in range(nw):
                o32[w][...] = sums[w]

    in_specs, oth_specs, o32_specs, shapes32, shapes16 = [], [], [], [], []
    for p in parts:
        hb = p.shape[1] // 2 // split
        blk = (None, hb, p.shape[2])
        in_specs.append(pl.BlockSpec(blk, lambda i, j, pos_ref: (j, pos_ref[1] * split + i, 0)))
        oth_specs.append(pl.BlockSpec(blk, lambda i, j, pos_ref: (j, i, 0)))
        o32_specs.append(pl.BlockSpec((hb, p.shape[2]), lambda i, j, pos_ref: (i, 0)))
        shapes32.append(jax.ShapeDtypeStruct((p.shape[1] // 2, p.shape[2]), F32))
        shapes16.append(jax.ShapeDtypeStruct((N_SHARD, p.shape[1] // 2, p.shape[2]), BF16))
    return pl.pallas_call(
        body, name="add_halves",
        grid_spec=pltpu.PrefetchScalarGridSpec(
            num_scalar_prefetch=1, grid=(split, N_SHARD),
            in_specs=in_specs + oth_specs, out_specs=o32_specs + oth_specs),
        out_shape=shapes32 + shapes16,
        compiler_params=_params("parallel", "arbitrary", vmem=VMEM_LIMIT),
    )(pos, *parts, *theirs)


def _exchange_chips_rider(sums16, rows=None, into=None):
    nw = len(sums16)
    rows = rows or [(0, s.shape[1]) for s in sums16]
    held = [w for w in range(nw) if into is not None and into[w] is not None]

    def copies(ins, outs, sems):
        send, recv = sems
        x, y, c = _coords()
        res = []
        for rel in (1, 2, 3):
            kx, ky = _chip_of(x, y, rel)
            for w in range(nw):
                r0, n = rows[w]
                res.append(pltpu.make_async_remote_copy(
                    src_ref=ins[w].at[2 * kx + ky, pl.ds(r0, n), :], dst_ref=outs[w].at[rel - 1, pl.ds(r0, n), :],
                    send_sem=send.at[w * 3 + rel - 1], recv_sem=recv.at[w * 3 + rel - 1],
                    device_id=(kx, ky, c), device_id_type=MESH))
        return res

    def start(ins, outs, sems):
        for cp in copies(ins, outs, sems):
            cp.start()

    def finish(ins, outs, sems):
        for cp in copies(ins, outs, sems):
            cp.wait()

    dma = pltpu.SemaphoreType.DMA
    return _Rider(list(sums16) + [into[w] for w in held],
                  [jax.ShapeDtypeStruct((3,) + s.shape[1:], BF16) for s in sums16],
                  [dma((3 * nw,)), dma((3 * nw,))], start, finish, aliases={nw + i: w for i, w in enumerate(held)})


def _add_chips(sums32, theirs, pos):
    nw = len(sums32)
    split = 2
    hbs = [s.shape[0] // split for s in sums32]

    def body(pos_ref, *refs):
        ins, oth, outs, bufs = (refs[k * nw:(k + 1) * nw] for k in range(4))
        lsem, ssem, rsem = refs[4 * nw:]
        i = pl.program_id(0)
        x, y, c = _coords()

        def copies(w, j):
            rows = pl.ds(pl.multiple_of((pos_ref[1] * split + j) * hbs[w], 8), hbs[w])
            return (pltpu.make_async_copy(bufs[w].at[j], outs[w].at[rows, :], lsem.at[w, j]),
                    pltpu.make_async_remote_copy(
                        src_ref=bufs[w].at[j], dst_ref=outs[w].at[rows, :], send_sem=ssem.at[w, j],
                        recv_sem=rsem.at[w, j], device_id=(x, y, 1 - c), device_id_type=MESH))

        @pl.when(i == 0)
        def _():
            _meet_sibling()

        for w in range(nw):
            acc = ins[w][...]
            for r in range(3):
                acc = acc + oth[w][r].astype(F32)
            bufs[w][i] = acc
            for cp in copies(w, i):
                cp.start()

        @pl.when(i == split - 1)
        def _():
            for w in range(nw):
                for j in range(split):
                    local, remote = copies(w, j)
                    local.wait()
                    remote.wait()

    in_specs, oth_specs, shapes, scratch = [], [], [], []
    for s, hb in zip(sums32, hbs):
        in_specs.append(pl.BlockSpec((hb, s.shape[1]), lambda i, pos_ref: (i, 0)))
        oth_specs.append(pl.BlockSpec((3, hb, s.shape[1]), lambda i, pos_ref: (0, i, 0)))
        shapes.append(jax.ShapeDtypeStruct((2 * s.shape[0], s.shape[1]), F32))
        scratch.append(pltpu.VMEM((split, hb, s.shape[1]), F32))
    dma = pltpu.SemaphoreType.DMA
    return pl.pallas_call(
        body, name="add_chips",
        grid_spec=pltpu.PrefetchScalarGridSpec(
            num_scalar_prefetch=1, grid=(split,), in_specs=in_specs + oth_specs,
            out_specs=[pl.BlockSpec(memory_space=pl.ANY)] * nw,
            scratch_shapes=scratch + [dma((nw, split)), dma((nw, split)), dma((nw, split))]),
        out_shape=shapes,
        compiler_params=_params("arbitrary", vmem=VMEM_LIMIT, collective_id=2),
    )(pos, *sums32, *theirs)


def _adamw_math(w, g, m, v):
    m = ADAM_B1 * m + (1.0 - ADAM_B1) * g
    v = ADAM_B2 * v + (1.0 - ADAM_B2) * (g * g)
    m_hat = m / (1.0 - ADAM_B1 ** ADAM_STEP)
    v_hat = v / (1.0 - ADAM_B2 ** ADAM_STEP)
    delta = -ADAM_LR * (m_hat / (jnp.sqrt(v_hat) + ADAM_EPS) + ADAM_WD * w)
    return delta, m, v


def _adamw(ws, gs, ms, vs):
    nw = len(ws)
    split = 8

    def body(*refs):
        w_r, g_r, m_r, v_r = (refs[i * nw:(i + 1) * nw] for i in range(4))
        g_o, d_o, m_o, v_o = (refs[(4 + i) * nw:(5 + i) * nw] for i in range(4))
        for k in range(nw):
            g = g_r[k][...]
            d, m, v = _adamw_math(w_r[k][...], g, m_r[k][...], v_r[k][...])
            g_o[k][...] = g
            d_o[k][...] = d
            m_o[k][...] = m
            v_o[k][...] = v

    specs = [pl.BlockSpec((w.shape[0] // split, w.shape[1]), lambda i: (i, 0)) for w in ws]
    shapes = [jax.ShapeDtypeStruct(w.shape, F32) for w in ws]
    outs = pl.pallas_call(
        body, name="adamw", grid=(split,),
        in_specs=specs * 4, out_specs=specs * 4, out_shape=shapes * 4,
        compiler_params=_params("parallel", vmem=VMEM_LIMIT),
    )(*ws, *gs, *ms, *vs)
    return outs[:nw], outs[nw:2 * nw], outs[2 * nw:3 * nw], outs[3 * nw:]


SMALL_ROWS = 8
SMALL_COLS = D_MODEL
LOSS_COL = RET_WIDTH + 24


def _small_allreduce_adamw(part, w, m, v, rider=None):
    def body(part_ref, w_ref, m_ref, v_ref, g_out, d_out, m_out, v_out, all_ref, send, recv):
        x, y, c = _coords()
        me = 4 * x + 2 * y + c
        all_ref[me] = part_ref[...]
        copies = []
        for rel in range(1, 8):
            px = 1 - x if rel & 4 else x
            py = 1 - y if rel & 2 else y
            pc = 1 - c if rel & 1 else c
            copies.append(pltpu.make_async_remote_copy(
                src_ref=part_ref, dst_ref=all_ref.at[me],
                send_sem=send.at[rel - 1], recv_sem=recv.at[rel - 1], device_id=(px, py, pc), device_id_type=MESH))
        for cp in copies:
            cp.start()
        for cp in copies:
            cp.wait()
        g = all_ref[0]
        for k in range(1, 8):
            g = g + all_ref[k]
        d, mn, vn = _adamw_math(w_ref[...], g, m_ref[...], v_ref[...])
        g_out[...] = g
        d_out[...] = d
        m_out[...] = mn
        v_out[...] = vn

    vm = pl.BlockSpec(memory_space=pltpu.VMEM)
    shape = jax.ShapeDtypeStruct((SMALL_ROWS, SMALL_COLS), F32)
    return _hosted_call(
        body, "small_allreduce_adamw", (1,),
        in_specs=[vm] * 4, out_specs=[vm] * 4, out_shape=[shape] * 4,
        scratch_shapes=[pltpu.VMEM((8, SMALL_ROWS, SMALL_COLS), F32),
                        pltpu.SemaphoreType.DMA((7,)), pltpu.SemaphoreType.DMA((7,))],
        operands=(part, w, m, v), rider=rider, semantics=["arbitrary"])


SMALL_NAMES = ("ret_decay_fwd", "ret_decay_bwd", "attn_sink", "ret_gn_gain",
               "ln1_gain", "ln1_bias", "ln2_gain", "ln2_bias")


LN_NAMES = ("ln1_gain", "ln1_bias", "ln2_gain", "ln2_bias")


def _pack_small(vals, extra=None):
    tail = jnp.zeros((1, 1), F32) if extra is None else extra.reshape(1, 1)
    row4 = jnp.concatenate([vals["ret_gn_gain"], vals["ret_decay_fwd"], vals["ret_decay_bwd"], vals["attn_sink"],
                            tail, jnp.zeros((1, SMALL_COLS - LOSS_COL - 1), F32)], axis=1)
    rows = [vals[n] for n in LN_NAMES] + [row4, jnp.zeros((SMALL_ROWS - 5, SMALL_COLS), F32)]
    return jnp.concatenate(rows, axis=0)


def _unpack_small(packed):
    o = RET_WIDTH
    where = [(n, i, 0, SMALL_COLS) for i, n in enumerate(LN_NAMES)] + [
        ("ret_gn_gain", 4, 0, o), ("ret_decay_fwd", 4, o, 8), ("ret_decay_bwd", 4, o + 8, 8),
        ("attn_sink", 4, o + 16, 8)]
    na, k = len(packed), len(where)

    def body(*refs):
        for a in range(na):
            rows = refs[a][...]
            for b, (_, row, col, n) in enumerate(where):
                refs[na + a * k + b][...] = rows[row:row + 1, col:col + n]
            if a == 0:
                refs[na + na * k][...] = rows[4:5, LOSS_COL:LOSS_COL + 1] * (0.5 / D_MODEL)

    vmem = pl.BlockSpec(memory_space=pltpu.VMEM)
    outs = pl.pallas_call(
        body, name="unpack_small", in_specs=[vmem] * na, out_specs=[vmem] * (na * k + 1),
        out_shape=[jax.ShapeDtypeStruct((1, n), F32) for _ in range(na) for (_, _, _, n) in where]
        + [jax.ShapeDtypeStruct((1, 1), F32)])(*packed)
    return [{where[b][0]: outs[a * k + b] for b in range(k)} for a in range(na)], outs[na * k]


def _local_step(x, p, tgt, w_in_t, rest, small, pos=None, small_state=None):
    bsz, s, _ = x.shape
    t = bsz * s
    x2d = x.reshape(t, D_MODEL)
    p2d = p.reshape(t, PLE_DIM)
    tgt2d = tgt.reshape(t, D_MODEL)
    dec_f = small["ret_decay_fwd"].reshape(8)
    dec_b = small["ret_decay_bwd"].reshape(8)
    lg_f = jnp.log1p(-jnp.exp2(dec_f))
    lg_b = jnp.log1p(-jnp.exp2(dec_b))
    per_lane = lambda v: jnp.repeat(v, HEAD_DIM).reshape(4, 1, LANES)
    lgf_l, lgb_l = per_lane(lg_f), per_lane(lg_b)
    sink = small["attn_sink"].reshape(8)
    slopes = 2.0 ** (-(jnp.arange(8, dtype=F32) + 1.0))
    gn_gain = small["ret_gn_gain"]
    g1, b1, g2, b2 = (small[n] for n in ("ln1_gain", "ln1_bias", "ln2_gain", "ln2_bias"))

    dist = pos is not None
    shard = dict(zip(REST_NAMES, rest)) if dist else {}
    near = lambda names, rels=NEAR: _gather_near_rider([shard[n] for n in names], rels)
    wave1, wave2, wave3 = ("w_out", "w_ple_gate", "w_ffn_gate"), ("w_ffn_up", "w_ple_proj"), ("w_ffn_down",)
    n1 = len(wave1)
    u, *o1 = _inproj(x2d, w_in_t, rider=near(wave1) if dist else None)
    u3 = u.reshape(bsz, s, IN_WIDTH)
    y_hat, y_rstd, y_ret, ret_rb, ret_kvf, *o2 = _ret_fwd(u3, lgf_l, lgb_l, gn_gain, rider=_merge_riders(
        [_gather_relay_rider(o1), near(wave2)]) if dist else None)
    y_att, att_p, att_ps, *o3 = _attn_fwd(u3, slopes, sink, rider=_merge_riders(
        [_gather_pass_rider(o2[:n1]), _gather_relay_rider(o2[n1:]), near(wave3, (1, 2, 3))]) if dist else None)
    gathered = dict(zip(wave1, o3[:n1]))
    w_out = _assemble_weights({"w_out": gathered["w_out"]})["w_out"] if dist else rest["w_out"]
    zh1, r1, hb, *o4 = _outproj_ln1(y_ret.reshape(t, RET_WIDTH), y_att.reshape(t, ATTN_WIDTH), x2d, w_out, g1, b1,
                                    rider=_gather_pass_rider(o3[n1:], sibling_id=0) if dist else None)
    gathered.update(zip(wave2 + wave3, o4))
    wts = _assemble_weights(gathered) if dist else rest
    dz2, dz2b, gs, us, acts, pg, ple, sq, dg2, db2 = _ffn_fwd(
        zh1, hb, p2d, tgt2d, g1, b1, g2, b2, wts["gate4"], wts["up4"], wts["down4"], wts["ple_proj"], wts["ple_gate"])
    dgs, dus, dsp, dple, dz1, dyr, dya, dg1, db1 = _ffn_bwd(dz2, gs, us, pg, ple, zh1, r1, g1, wts["gate4"],
                                                          wts["up4"], wts["down4"], wts["ple_gate"], wts["w_out"])
    ffn_parts = list(_wgrad_ffn(acts, dgs, dus, hb, dz2b))
    d_w_out, d_ple_gate, d_ple_proj, *th_ffn = _wgrad_misc(
        y_ret.reshape(t, RET_WIDTH), y_att.reshape(t, ATTN_WIDTH), dz1, hb, dsp, p2d, dple,
        rider=_exchange_halves_rider(ffn_parts[:2], sibling_id=1) if dist else None)
    misc_parts = [d_w_out.reshape(N_SHARD, D_MODEL // N_SHARD, D_MODEL), d_ple_proj,
                  d_ple_gate.reshape(N_SHARD, D_MODEL // N_SHARD, D_MODEL)]
    dyr3, dya3 = dyr.reshape(bsz, s, RET_WIDTH), dya.reshape(bsz, s, ATTN_WIDTH)
    if dist:
        s_gu = _add_halves(ffn_parts[:2], th_ffn, pos)
        half = FFN_SHARD // 2
        quarter = half // 2
        later_parts = [ffn_parts[2]] + misc_parts
        drq, drk, drv, drg, rpart, *o5 = _ret_bwd(u3, y_hat, y_rstd, (ret_rb, ret_kvf), dyr3, lgf_l, lgb_l, gn_gain,
                                                  rider=_merge_riders(
            [_exchange_chips_rider(s_gu[2:], rows=[(0, half), (0, quarter)]), _exchange_halves_rider(later_parts)]))
        s_dm = _add_halves(later_parts, o5[2:], pos)
        daq, dakv, spart, *o6 = _attn_bwd(u3, dya3, att_p, att_ps, rider=_exchange_chips_rider(
            [s_gu[3], s_dm[4]], rows=[(quarter, half - quarter), (0, half)], into=[o5[1], None]))
    else:
        drq, drk, drv, drg, rpart = _ret_bwd(u3, y_hat, y_rstd, (ret_rb, ret_kvf), dyr3, lgf_l, lgb_l, gn_gain)
        daq, dakv, spart = _attn_bwd(u3, dya3, att_p, att_ps)
    pieces = [a.reshape(t, -1) for a in (drq, drk, drv, drg, daq, dakv)]
    d_in, *o7 = _wgrad_in(pieces, x2d, rider=_exchange_chips_rider(list(s_dm[5:])) if dist else None)
    d_in = d_in.reshape(N_SHARD, FFN_SHARD, D_MODEL)

    rsum = rpart
    lane_heads = lambda row: jnp.sum(row.reshape(4, 2, HEAD_DIM), axis=-1).reshape(8)
    dlg_f = lane_heads(rsum[:, 0, :]) + jnp.stack([jnp.sum(rsum[:, 2, :], -1), jnp.sum(rsum[:, 3, :], -1)], 1).reshape(8)
    dlg_b = lane_heads(rsum[:, 1, :]) + jnp.stack([jnp.sum(rsum[:, 4, :], -1), jnp.sum(rsum[:, 5, :], -1)], 1).reshape(8)
    chain = lambda d: -(math.log(2.0) * jnp.exp2(d)) / (1.0 - jnp.exp2(d))
    grads_small = {
        "ret_decay_fwd": (dlg_f * chain(dec_f)).reshape(1, 8),
        "ret_decay_bwd": (dlg_b * chain(dec_b)).reshape(1, 8),
        "attn_sink": jnp.sum(spart, axis=0)[:, 0:4, 0].reshape(1, 8),
        "ret_gn_gain": rsum[:, 6, :].reshape(1, RET_WIDTH),
        "ln1_gain": dg1, "ln1_bias": db1, "ln2_gain": dg2, "ln2_bias": db2,
    }
    if not dist:
        grad_x, = _inproj_bwd(dz1, pieces, w_in_t)
        grads_rest = [misc_parts[0]] + ffn_parts + misc_parts[1:]
        return sq[0, 0], grad_x.reshape(bsz, s, D_MODEL), d_in, grads_rest, grads_small
    *small_out, th_in = _small_allreduce_adamw(_pack_small(grads_small, sq[0, 0]), *small_state,
                                               rider=_exchange_halves_rider([d_in]))
    s_in = _add_halves([d_in], [th_in], pos)
    grad_x, chips_in = _inproj_bwd(dz1, pieces, w_in_t, rider=_exchange_chips_rider([s_in[1]]))
    sums32 = [s_in[0], s_dm[1], s_gu[0], s_gu[1], s_dm[0], s_dm[2], s_dm[3]]
    from_chips = [chips_in, o7[0], o5[0], o6[0], o6[1], o7[1], o7[2]]
    return grad_x.reshape(bsz, s, D_MODEL), sums32, from_chips, small_out


BIG_NAMES = ("w_in", "w_out", "w_ffn_gate", "w_ffn_up", "w_ffn_down", "w_ple_proj", "w_ple_gate")
REST_NAMES = BIG_NAMES[1:]
TRANSPOSED = ("w_in", "w_ffn_gate", "w_ffn_up")
WEIGHT_ORDER = ("w_in", "ret_decay_fwd", "ret_decay_bwd", "ret_gn_gain", "attn_sink", "w_out", "ln1_gain",
                "ln1_bias", "w_ffn_gate", "w_ffn_up", "w_ffn_down", "w_ple_proj", "w_ple_gate", "ln2_gain", "ln2_bias")


def _shard_rows(name, a):
    return jnp.swapaxes(a[0], 0, 1) if name in TRANSPOSED else a[0]


def _unshard_rows(name, a):
    return (jnp.swapaxes(a, 0, 1) if name in TRANSPOSED else a)[None]


def _assemble_weights(gathered):
    rows = lambda a: a.reshape(N_SHARD * a.shape[1], a.shape[2])
    same = lambda a: a
    layout = {"w_out": ("w_out", rows), "w_ffn_gate": ("gate4", same), "w_ffn_up": ("up4", same),
              "w_ffn_down": ("down4", same), "w_ple_proj": ("ple_proj", same), "w_ple_gate": ("ple_gate", rows)}
    return {layout[n][0]: layout[n][1](a) for n, a in gathered.items()}


def kernel(x, p, w_in, ret_decay_fwd, ret_decay_bwd, ret_gn_gain, attn_sink, w_out, ln1_gain, ln1_bias, w_ffn_gate, w_ffn_up, w_ffn_down, w_ple_proj, w_ple_gate, ln2_gain, ln2_bias, loss_target, m_w_in, m_ret_decay_fwd, m_ret_decay_bwd, m_ret_gn_gain, m_attn_sink, m_w_out, m_ln1_gain, m_ln1_bias, m_w_ffn_gate, m_w_ffn_up, m_w_ffn_down, m_w_ple_proj, m_w_ple_gate, m_ln2_gain, m_ln2_bias, v_w_in, v_ret_decay_fwd, v_ret_decay_bwd, v_ret_gn_gain, v_attn_sink, v_w_out, v_ln1_gain, v_ln1_bias, v_w_ffn_gate, v_w_ffn_up, v_w_ffn_down, v_w_ple_proj, v_w_ple_gate, v_ln2_gain, v_ln2_bias):
    w = dict(w_in=w_in, ret_decay_fwd=ret_decay_fwd, ret_decay_bwd=ret_decay_bwd, ret_gn_gain=ret_gn_gain,
             attn_sink=attn_sink, w_out=w_out, ln1_gain=ln1_gain, ln1_bias=ln1_bias, w_ffn_gate=w_ffn_gate,
             w_ffn_up=w_ffn_up, w_ffn_down=w_ffn_down, w_ple_proj=w_ple_proj, w_ple_gate=w_ple_gate,
             ln2_gain=ln2_gain, ln2_bias=ln2_bias)
    m = dict(w_in=m_w_in, ret_decay_fwd=m_ret_decay_fwd, ret_decay_bwd=m_ret_decay_bwd, ret_gn_gain=m_ret_gn_gain,
             attn_sink=m_attn_sink, w_out=m_w_out, ln1_gain=m_ln1_gain, ln1_bias=m_ln1_bias, w_ffn_gate=m_w_ffn_gate,
             w_ffn_up=m_w_ffn_up, w_ffn_down=m_w_ffn_down, w_ple_proj=m_w_ple_proj, w_ple_gate=m_w_ple_gate,
             ln2_gain=m_ln2_gain, ln2_bias=m_ln2_bias)
    v = dict(w_in=v_w_in, ret_decay_fwd=v_ret_decay_fwd, ret_decay_bwd=v_ret_decay_bwd, ret_gn_gain=v_ret_gn_gain,
             attn_sink=v_attn_sink, w_out=v_w_out, ln1_gain=v_ln1_gain, ln1_bias=v_ln1_bias, w_ffn_gate=v_w_ffn_gate,
             w_ffn_up=v_w_ffn_up, w_ffn_down=v_w_ffn_down, w_ple_proj=v_w_ple_proj, w_ple_gate=v_w_ple_gate,
             ln2_gain=v_ln2_gain, ln2_bias=v_ln2_bias)
    big = lambda d: [_shard_rows(n, d[n]) for n in BIG_NAMES]
    small = lambda d: {n: d[n] for n in SMALL_NAMES}

    chip = 2 * lax.axis_index("x") + lax.axis_index("y")
    pos = jnp.stack([chip, lax.axis_index("c")]).astype(jnp.int32)

    shards = big(w)
    w_in4, rest16 = _gather_and_cast(shards[0].astype(BF16), shards[1:])
    w_in_t = w_in4.reshape(IN_WIDTH, D_MODEL)
    grad_x, sums32, from_chips, (g_s, d_s, m_s, v_s) = _local_step(
        x, p[0], loss_target, w_in_t, rest16, small(w), pos=pos,
        small_state=(_pack_small(small(w)), _pack_small(small(m)), _pack_small(small(v))))
    g_big, d_big, m_big, v_big = _adamw(big(w), _add_chips(sums32, from_chips, pos), big(m), big(v))

    def tree(bigs, smalls):
        out = {n: _unshard_rows(n, a) for n, a in zip(BIG_NAMES, bigs)}
        out.update(smalls)
        return [out[n] for n in WEIGHT_ORDER]

    smalls, loss = _unpack_small([g_s, d_s, m_s, v_s])
    return (loss.reshape(()), grad_x,
            *(a for bigs, s in zip((g_big, d_big, m_big, v_big), smalls) for a in tree(bigs, s)))
```

```python
import functools
import math

import jax
import jax.numpy as jnp
from jax import lax
from jax.experimental import pallas as pl
from jax.experimental.pallas import tpu as pltpu

F32 = jnp.float32
BF16 = jnp.bfloat16

D_MODEL = 1024
HEAD_DIM = 64
RET_HEADS = 8
ATTN_HEADS = 8
RET_WIDTH = 512
ATTN_WIDTH = 512
KV_WIDTH = 128
IN_WIDTH = 2816
FFN = 2816
N_SHARD = 4
FFN_SHARD = FFN // N_SHARD
PLE_DIM = 256
CHUNK = 128
LANES = 128
ALPHA = 2.0 ** 0.25
LN_EPS = 1e-5
GN_EPS = 1e-5
NEG_INF = -1e30
ADAM_LR = 0.001
ADAM_B1 = 0.9
ADAM_B2 = 0.999
ADAM_EPS = 1e-08
ADAM_WD = 0.01
ADAM_STEP = 10
VMEM_LIMIT = 56 * 1024 * 1024
MESH = pl.DeviceIdType.MESH

CB_RQ, CB_RK, CB_RV, CB_RG, CB_AQ, CB_AK, CB_AV = 0, 4, 8, 12, 16, 20, 21


def _dot(a, b):
    return jnp.dot(a, b, preferred_element_type=F32)


def _dot_nt(a, b):
    return lax.dot_general(a, b, (((1,), (1,)), ((), ())), preferred_element_type=F32)


def _dot_tn(a, b):
    return lax.dot_general(a, b, (((0,), (0,)), ((), ())), preferred_element_type=F32)


def _sigmoid(x):
    return 1.0 / (1.0 + jnp.exp(-x))


def _params(*sem, vmem=None, collective_id=None):
    return pltpu.CompilerParams(dimension_semantics=tuple(sem) if sem else None, vmem_limit_bytes=vmem,
                                collective_id=collective_id)


def _meet_sibling():
    x, y, c = lax.axis_index("x"), lax.axis_index("y"), lax.axis_index("c")
    sem = pltpu.get_barrier_semaphore()
    pl.semaphore_signal(sem, inc=1, device_id=(x, y, 1 - c), device_id_type=MESH)
    pl.semaphore_wait(sem, 1)


def _meet_chips():
    x, y, c = lax.axis_index("x"), lax.axis_index("y"), lax.axis_index("c")
    sem = pltpu.get_barrier_semaphore()
    for peer in ((1 - x, y, c), (x, 1 - y, c), (1 - x, 1 - y, c)):
        pl.semaphore_signal(sem, inc=1, device_id=peer, device_id_type=MESH)
    pl.semaphore_wait(sem, 3)


class _Rider:
    def __init__(self, ins, out_shapes, sems, start, finish, aliases=None, sibling_id=None, chips_id=None):
        self.ins, self.out_shapes, self.sems = list(ins), list(out_shapes), list(sems)
        self.start, self.finish, self.aliases = start, finish, dict(aliases or {})
        self.sibling_id, self.chips_id = sibling_id, chips_id


def _merge_riders(riders):
    riders = [r for r in riders if r is not None]
    if len(riders) == 1:
        return riders[0]
    bounds, aliases = [], {}
    i0 = o0 = s0 = 0
    for r in riders:
        bounds.append((i0, o0, s0))
        aliases.update({i0 + i: o0 + o for i, o in r.aliases.items()})
        i0, o0, s0 = i0 + len(r.ins), o0 + len(r.out_shapes), s0 + len(r.sems)

    def each(method):
        def run(ins, outs, sems):
            for r, (i, o, s) in zip(riders, bounds):
                getattr(r, method)(ins[i:i + len(r.ins)], outs[o:o + len(r.out_shapes)], sems[s:s + len(r.sems)])
        return run

    return _Rider([a for r in riders for a in r.ins], [a for r in riders for a in r.out_shapes],
                  [a for r in riders for a in r.sems], each("start"), each("finish"), aliases)


def _hosted_call(body, name, grid, in_specs, out_specs, out_shape, scratch_shapes, operands, rider=None,
                 semantics=None):
    n_in, n_out, n_scr = len(in_specs), len(out_specs), len(scratch_shapes)
    if rider is None:
        return pl.pallas_call(
            body, name=name, grid=grid, in_specs=in_specs, out_specs=out_specs, out_shape=out_shape,
            scratch_shapes=scratch_shapes,
            compiler_params=_params(*(semantics or ["parallel"] * len(grid)), vmem=VMEM_LIMIT))(*operands)
    r_in, r_out = len(rider.ins), len(rider.out_shapes)

    def full_body(*refs):
        main_in, rin = refs[:n_in], refs[n_in:n_in + r_in]
        o0 = n_in + r_in
        main_out, rout = refs[o0:o0 + n_out], refs[o0 + n_out:o0 + n_out + r_out]
        s0 = o0 + n_out + r_out
        main_scr, rsem = refs[s0:s0 + n_scr], refs[s0 + n_scr:]
        first = functools.reduce(jnp.logical_and, [pl.program_id(a) == 0 for a in range(len(grid))])
        last = functools.reduce(jnp.logical_and, [pl.program_id(a) == g - 1 for a, g in enumerate(grid)])

        @pl.when(first)
        def _():
            if rider.sibling_id is not None:
                _meet_sibling()
            elif rider.chips_id is not None:
                _meet_chips()
            rider.start(rin, rout, rsem)

        body(*main_in, *main_out, *main_scr)

        @pl.when(last)
        def _():
            rider.finish(rin, rout, rsem)

    hbm = pl.BlockSpec(memory_space=pl.ANY)
    return pl.pallas_call(
        full_body, name=name, grid=grid,
        in_specs=list(in_specs) + [hbm] * r_in, out_specs=list(out_specs) + [hbm] * r_out,
        out_shape=list(out_shape) + rider.out_shapes,
        scratch_shapes=list(scratch_shapes) + rider.sems,
        input_output_aliases={n_in + i: n_out + o for i, o in rider.aliases.items()},
        compiler_params=_params(*(["arbitrary"] * len(grid)), vmem=VMEM_LIMIT,
                                collective_id=rider.chips_id if rider.sibling_id is None else rider.sibling_id),
    )(*operands, *rider.ins)


def _loop_grouped(n, body, init, per_trip=2):
    if n % per_trip:
        return lax.fori_loop(0, n, body, init)

    def trip(i, c):
        for j in range(per_trip):
            c = body(per_trip * i + j, c)
        return c

    return lax.fori_loop(0, n // per_trip, trip, init)


def _head_mean(x, m0):
    s0 = jnp.sum(jnp.where(m0, x, 0.0), axis=1, keepdims=True)
    s1 = jnp.sum(jnp.where(m0, 0.0, x), axis=1, keepdims=True)
    return jnp.where(m0, s0, s1) * (1.0 / HEAD_DIM)


def _inproj(x2d, w_in_t, rider=None):
    t = x2d.shape[0]
    tm = 512
    nb = 256

    def body(x_ref, w_ref, o_ref):
        xb = x_ref[...].astype(BF16)
        for n in range(0, IN_WIDTH, nb):
            o_ref[:, n:n + nb] = _dot_nt(xb, w_ref[n:n + nb, :]).astype(BF16)

    return _hosted_call(
        body, "inproj", (t // tm,),
        in_specs=[pl.BlockSpec((tm, D_MODEL), lambda i: (i, 0)),
                  pl.BlockSpec((IN_WIDTH, D_MODEL), lambda i: (0, 0))],
        out_specs=[pl.BlockSpec((tm, IN_WIDTH), lambda i: (i, 0))],
        out_shape=[jax.ShapeDtypeStruct((t, IN_WIDTH), BF16)],
        scratch_shapes=[], operands=(x2d, w_in_t), rider=rider)


def _outproj_ln1(y_ret, y_att, x2d, w_out, gain, bias, rider=None):
    t = x2d.shape[0]
    tm = 512

    def body(yr_ref, ya_ref, x_ref, w_ref, g_ref, b_ref, zh_ref, r_ref, hb_ref):
        mix = _dot(yr_ref[...], w_ref[0:RET_WIDTH, :]) + _dot(ya_ref[...], w_ref[RET_WIDTH:, :])
        z = ALPHA * x_ref[...] + mix
        mu = jnp.mean(z, axis=1, keepdims=True)
        zc = z - mu
        var = jnp.mean(zc * zc, axis=1, keepdims=True)
        r = lax.rsqrt(var + LN_EPS)
        zh = zc * r
        zh_ref[...] = zh
        r_ref[...] = r
        hb_ref[...] = (zh * g_ref[...] + b_ref[...]).astype(BF16)

    row = lambda w: pl.BlockSpec((tm, w), lambda i: (i, 0))
    const = lambda s: pl.BlockSpec(s, lambda i: (0, 0))
    return _hosted_call(
        body, "outproj_ln1", (t // tm,),
        in_specs=[row(RET_WIDTH), row(ATTN_WIDTH), row(D_MODEL), const((D_MODEL, D_MODEL)),
                  const((1, D_MODEL)), const((1, D_MODEL))],
        out_specs=[row(D_MODEL), row(1), row(D_MODEL)],
        out_shape=[jax.ShapeDtypeStruct((t, D_MODEL), F32), jax.ShapeDtypeStruct((t, 1), F32),
                   jax.ShapeDtypeStruct((t, D_MODEL), BF16)],
        scratch_shapes=[], operands=(y_ret, y_att, x2d, w_out, gain, bias), rider=rider)


def _load_resident(step, pairs, sems):
    copies = [pltpu.make_async_copy(src, dst, sems.at[i]) for i, (src, dst) in enumerate(pairs)]

    @pl.when(step == 0)
    def _():
        for cp in copies:
            cp.start()
        for cp in copies:
            cp.wait()


FFN_CHUNK = 256
N_FFN_CHUNK = FFN // FFN_CHUNK


def _resident_quarters(hbm, vmem):
    q = FFN // N_SHARD
    return [(hbm.at[pl.ds(j * q, q), :], vmem.at[pl.ds(j * q, q), :]) for j in range(N_SHARD)]


def _ln2_loss_tail(zh, mixed, tgt, g1, b1, g2, b2):
    z2 = ALPHA * (zh * g1 + b1) + mixed
    mu = jnp.mean(z2, axis=1, keepdims=True)
    zc = z2 - mu
    var = jnp.mean(zc * zc, axis=1, keepdims=True)
    r = lax.rsqrt(var + LN_EPS)
    zh2 = zc * r
    err = zh2 * g2 + b2 - tgt
    dy = err * (1.0 / D_MODEL)
    dzh = dy * g2
    m1 = jnp.mean(dzh, axis=1, keepdims=True)
    m2 = jnp.mean(dzh * zh2, axis=1, keepdims=True)
    dz2 = r * (dzh - m1 - zh2 * m2)
    return dz2, jnp.sum(err * err), jnp.sum(dy * zh2, axis=0, keepdims=True), jnp.sum(dy, axis=0, keepdims=True)


def _ffn_fwd(zh1, hb, p2d, tgt, g1, b1, g2, b2, wg4, wu4, wd4, wpe, wpg):
    t = zh1.shape[0]
    tm = 256
    wg_t, wu_t, wd_all = (w.reshape(FFN, D_MODEL) for w in (wg4, wu4, wd4))

    def body(zh_ref, hb_ref, p_ref, t_ref, g1_ref, b1_ref, g2_ref, b2_ref,
             wg_hbm, wu_hbm, wd_hbm, wpe_hbm, wpg_hbm,
             dz_ref, dzb_ref, gs_ref, us_ref, act_ref, pg_ref, ple_ref, loss_ref, dg2_ref, db2_ref,
             wg, wu, wd, wpe, wpg, wsem):
        step = pl.program_id(0)
        loads = _resident_quarters(wg_hbm, wg) + _resident_quarters(wu_hbm, wu) + _resident_quarters(wd_hbm, wd)
        pc = D_MODEL // N_SHARD
        loads += [(wpe_hbm.at[j], wpe.at[:, pl.ds(j * pc, pc)]) for j in range(N_SHARD)]
        _load_resident(step, loads + [(wpg_hbm, wpg)], wsem)

        @pl.when(step == 0)
        def _():
            loss_ref[...] = jnp.zeros_like(loss_ref)
            dg2_ref[...] = jnp.zeros_like(dg2_ref)
            db2_ref[...] = jnp.zeros_like(db2_ref)

        hbv = hb_ref[...]
        ffn = jnp.zeros((tm, D_MODEL), F32)
        acts = []
        chunks = [slice(n * FFN_CHUNK, (n + 1) * FFN_CHUNK) for n in range(N_FFN_CHUNK)]
        for n in range(N_FFN_CHUNK + 1):
            if n < N_FFN_CHUNK:
                gj = _dot_nt(hbv, wg[chunks[n], :])
                uj = _dot_nt(hbv, wu[chunks[n], :])
                gs_ref[:, chunks[n]] = gj.astype(BF16)
                us_ref[:, chunks[n]] = uj.astype(BF16)
                acts.append((gj * _sigmoid(gj) * uj).astype(BF16))
                act_ref[:, chunks[n]] = acts[n]
            if n > 0:
                ffn = ffn + _dot(acts[n - 1], wd[chunks[n - 1], :])
        ple = _dot(p_ref[...].astype(BF16), wpe[...])
        pg = _sigmoid(_dot(hbv, wpg[...]))
        pg_ref[...] = pg.astype(BF16)
        ple_ref[...] = ple.astype(BF16)
        dz2, sq, dg2, db2 = _ln2_loss_tail(zh_ref[...], ffn + pg * ple, t_ref[...], g1_ref[...], b1_ref[...],
                                           g2_ref[...], b2_ref[...])
        dz_ref[...] = dz2
        dzb_ref[...] = dz2.astype(BF16)
        loss_ref[...] += sq
        dg2_ref[...] += dg2
        db2_ref[...] += db2

    row = lambda w: pl.BlockSpec((tm, w), lambda i: (i, 0))
    const = lambda s: pl.BlockSpec(s, lambda i: (0, 0))
    hid_shape = jax.ShapeDtypeStruct((t, FFN), BF16)
    hbm = pl.BlockSpec(memory_space=pl.ANY)
    return pl.pallas_call(
        body, name="ffn_fwd", grid=(t // tm,),
        in_specs=[row(D_MODEL), row(D_MODEL), row(PLE_DIM), row(D_MODEL),
                  const((1, D_MODEL)), const((1, D_MODEL)), const((1, D_MODEL)), const((1, D_MODEL)),
                  hbm, hbm, hbm, hbm, hbm],
        out_specs=[row(D_MODEL), row(D_MODEL), row(FFN), row(FFN), row(FFN), row(D_MODEL), row(D_MODEL),
                   const((8, LANES)), const((1, D_MODEL)), const((1, D_MODEL))],
        out_shape=[jax.ShapeDtypeStruct((t, D_MODEL), F32), jax.ShapeDtypeStruct((t, D_MODEL), BF16),
                   hid_shape, hid_shape, hid_shape,
                   jax.ShapeDtypeStruct((t, D_MODEL), BF16), jax.ShapeDtypeStruct((t, D_MODEL), BF16),
                   jax.ShapeDtypeStruct((8, LANES), F32),
                   jax.ShapeDtypeStruct((1, D_MODEL), F32), jax.ShapeDtypeStruct((1, D_MODEL), F32)],
        scratch_shapes=[pltpu.VMEM((FFN, D_MODEL), BF16), pltpu.VMEM((FFN, D_MODEL), BF16),
                        pltpu.VMEM((FFN, D_MODEL), BF16),
                        pltpu.VMEM((PLE_DIM, D_MODEL), BF16), pltpu.VMEM(wpg.shape, BF16),
                        pltpu.SemaphoreType.DMA((4 * N_SHARD + 1,))],
        compiler_params=_params("arbitrary", vmem=VMEM_LIMIT),
    )(zh1, hb, p2d, tgt, g1, b1, g2, b2, wg_t, wu_t, wd_all, wpe, wpg)


def _ret_tables(lgf, lgb):
    c = CHUNK
    row = lax.broadcasted_iota(jnp.int32, (c, LANES), 0).astype(F32)
    ii = lax.broadcasted_iota(jnp.int32, (c, c), 0).astype(F32)
    jj = lax.broadcasted_iota(jnp.int32, (c, c), 1).astype(F32)
    diff = ii - jj
    dmats = []
    for h in range(2):
        lf = lgf[:, h * HEAD_DIM:h * HEAD_DIM + 1]
        lb = lgb[:, h * HEAD_DIM:h * HEAD_DIM + 1]
        dmats.append(jnp.where(diff > 0, jnp.exp(lf * jnp.maximum(diff, 0.0)),
                               jnp.where(diff < 0, jnp.exp(lb * jnp.maximum(-diff, 0.0)), 2.0)))
    tab = dict(
        qdec_f=jnp.exp(lgf * (row + 1.0)), kdec_f=jnp.exp(lgf * (c - 1.0 - row)),
        qdec_b=jnp.exp(lgb * (c - row)), kdec_b=jnp.exp(lgb * row),
        cdec_f=jnp.exp(lgf * c), cdec_b=jnp.exp(lgb * c),
        d0=dmats[0], d1=dmats[1], row=row, diff=diff)
    r = lax.broadcasted_iota(jnp.int32, (LANES, LANES), 0) < HEAD_DIM
    cc = lax.broadcasted_iota(jnp.int32, (LANES, LANES), 1) < HEAD_DIM
    tab["bd"] = r == cc
    tab["m0"] = lax.broadcasted_iota(jnp.int32, (c, LANES), 1) < HEAD_DIM
    return tab


def _ret_specs(bsz, s):
    blk = lambda cb: pl.BlockSpec((bsz, s, LANES), lambda p, cb=cb: (0, 0, cb + p))
    lane = pl.BlockSpec((None, 1, LANES), lambda p: (p, 0, 0))
    gain = pl.BlockSpec((1, LANES), lambda p: (0, p))
    pair = pl.BlockSpec((bsz, s, LANES), lambda p: (0, 0, p))
    return blk, lane, gain, pair


def _ret_state_spec(bsz, n_chunk):
    spec = pl.BlockSpec((None, bsz, n_chunk, LANES, LANES), lambda p: (p, 0, 0, 0, 0))
    return spec, jax.ShapeDtypeStruct((4, bsz, n_chunk, LANES, LANES), F32)


def _ret_kv_states(tb, k_ref, v_ref, rb_ref, kvf_ref, n_chunk):
    c = CHUNK
    bsz = k_ref.shape[0]
    bd = tb["bd"]

    def contributions(n, carry):
        sl = pl.ds(pl.multiple_of(n * c, c), c)
        kfb = []
        for b in range(bsz):
            k32 = k_ref[b, sl, :].astype(F32)
            kfb.append(jnp.concatenate([k32 * tb["kdec_f"], k32 * tb["kdec_b"]], axis=1).astype(BF16))
        kvs = [_dot_tn(kfb[b], v_ref[b, sl, :]) for b in range(bsz)]
        for b in range(bsz):
            kvf_ref[b, n] = jnp.where(bd, kvs[b][0:LANES], 0.0)
            rb_ref[b, n] = jnp.where(bd, kvs[b][LANES:], 0.0)
        return carry

    lax.fori_loop(0, n_chunk, contributions, 0, unroll=2)

    def recur(i, rbs):
        n = n_chunk - 1 - i
        new = []
        for b in range(bsz):
            own = rb_ref[b, n]
            rb_ref[b, n] = rbs[b]
            new.append(rbs[b] * tb["cdec_b"] + own)
        return tuple(new)

    lax.fori_loop(0, n_chunk, recur, tuple(jnp.zeros((LANES, LANES), F32) for _ in range(bsz)))


def _split_rows(x, m0):
    return jnp.concatenate([jnp.where(m0, x, 0.0), jnp.where(m0, 0.0, x)], axis=0).astype(BF16)


def _ret_fwd(u3, lgf_l, lgb_l, gn_gain, rider=None):
    bsz, s, _ = u3.shape
    n_chunk = s // CHUNK
    c = CHUNK

    def body(q_ref, k_ref, v_ref, g_ref, lgf_ref, lgb_ref, gain_ref, yh_ref, rstd_ref, o_ref, rb_ref, kvf_ref):
        tb = _ret_tables(lgf_ref[...], lgb_ref[...])
        m0 = tb["m0"]
        gain = gain_ref[...]
        rows = range(bsz)
        _ret_kv_states(tb, k_ref, v_ref, rb_ref, kvf_ref, n_chunk)

        def chunk(n, rfs):
            sl = pl.ds(pl.multiple_of(n * c, c), c)
            qs = [q_ref[b, sl, :].astype(F32) * 0.125 for b in rows]
            s01 = [_dot_nt(_split_rows(qs[b], m0), k_ref[b, sl, :]) for b in rows]
            ys = []
            for b in rows:
                lhs = jnp.concatenate([s01[b][0:c] * tb["d0"], s01[b][c:] * tb["d1"],
                                       qs[b] * tb["qdec_f"], qs[b] * tb["qdec_b"]], axis=1).astype(BF16)
                rhs = jnp.concatenate([_split_rows(v_ref[b, sl, :].astype(F32), m0),
                                       rfs[b].astype(BF16), rb_ref[b, n].astype(BF16)], axis=0)
                ys.append(_dot(lhs, rhs))
            new = []
            for b in rows:
                y = ys[b]
                mu = _head_mean(y, m0)
                yc = y - mu
                rstd = lax.rsqrt(_head_mean(yc * yc, m0) + GN_EPS)
                yh = yc * rstd
                g = g_ref[b, sl, :].astype(F32)
                yh_ref[b, sl, :] = yh
                rstd_ref[b, sl, :] = rstd
                o_ref[b, sl, :] = (yh * gain * (g * _sigmoid(g))).astype(BF16)
                new.append(rfs[b] * tb["cdec_f"] + kvf_ref[b, n])
            return tuple(new)

        _loop_grouped(n_chunk, chunk, tuple(jnp.zeros((LANES, LANES), F32) for _ in rows))

    blk, lane, gain, pair = _ret_specs(bsz, s)
    state, state_shape = _ret_state_spec(bsz, n_chunk)
    return _hosted_call(
        body, "ret_fwd", (4,),
        in_specs=[blk(CB_RQ), blk(CB_RK), blk(CB_RV), blk(CB_RG), lane, lane, gain],
        out_specs=[pair, pair, pair, state, state],
        out_shape=[jax.ShapeDtypeStruct((bsz, s, RET_WIDTH), F32), jax.ShapeDtypeStruct((bsz, s, RET_WIDTH), F32),
                   jax.ShapeDtypeStruct((bsz, s, RET_WIDTH), BF16), state_shape, state_shape],
        scratch_shapes=[],
        operands=(u3, u3, u3, u3, lgf_l, lgb_l, gn_gain), rider=rider)


def _ret_bwd(u3, y_hat, y_rstd, states, d_o, lgf_l, lgb_l, gn_gain, rider=None):
    bsz, s, _ = u3.shape
    n_chunk = s // CHUNK
    c = CHUNK

    def body(q_ref, k_ref, v_ref, g_ref, yh_ref, rstd_ref, do_ref, lgf_ref, lgb_ref, gain_ref, rb_ref, kvf_ref,
             dq_ref, dk_ref, dv_ref, dg_ref, part_ref,
             rf_ref, dirf_ref, dy_ref, dk_acc, dv_acc, pa0, pa1, vec_ref):
        tb = _ret_tables(lgf_ref[...], lgb_ref[...])
        m0, bd, row = tb["m0"], tb["bd"], tb["row"]
        gain = gain_ref[...]
        wf = jnp.maximum(tb["diff"], 0.0)
        wb = jnp.maximum(-tb["diff"], 0.0)
        rows = range(bsz)
        zero_states = tuple(jnp.zeros((LANES, LANES), F32) for _ in rows)
        for ref in (pa0, pa1):
            ref[...] = jnp.zeros_like(ref)
        vec_ref[...] = jnp.zeros_like(vec_ref)

        def sweep_fwd(n, carry):
            rfs, gbs = carry
            sl = pl.ds(pl.multiple_of(n * c, c), c)
            qs, ks, vs, dys, dybs, q01, k01, dy01 = [], [], [], [], [], [], [], []
            dgain = jnp.zeros((1, LANES), F32)
            for b in rows:
                q = q_ref[b, sl, :].astype(F32) * 0.125
                k = k_ref[b, sl, :]
                yh = yh_ref[b, sl, :]
                rstd = rstd_ref[b, sl, :]
                do = do_ref[b, sl, :].astype(F32)
                g = g_ref[b, sl, :].astype(F32)
                sg = _sigmoid(g)
                sil = g * sg
                dyh = do * gain * sil
                dg_ref[b, sl, :] = (do * yh * gain * sg * (1.0 + g * (1.0 - sg))).astype(BF16)
                dgain = dgain + jnp.sum(do * yh * sil, axis=0, keepdims=True)
                dy = rstd * (dyh - _head_mean(dyh, m0) - yh * _head_mean(dyh * yh, m0))
                dyb = dy.astype(BF16)
                dy_ref[b, sl, :] = dyb
                rf_ref[b, n] = rfs[b]
                qs.append(q)
                ks.append(k)
                vs.append(v_ref[b, sl, :])
                dys.append(dy)
                dybs.append(dyb)
                q01.append(_split_rows(q, m0))
                k01.append(_split_rows(k.astype(F32), m0))
                dy01.append(_split_rows(dy, m0))
            s01 = [_dot_nt(q01[b], ks[b]) for b in rows]
            da01 = [_dot_nt(dy01[b], vs[b]) for b in rows]
            rbn = [rb_ref[b, n] for b in rows]
            states = [jnp.concatenate([rfs[b], rbn[b]], axis=0).astype(BF16) for b in rows]
            dqc = [_dot_nt(dybs[b], states[b]) for b in rows]
            gbb = [gbs[b].astype(BF16) for b in rows]
            dkb = [_dot_nt(vs[b], gbb[b]) for b in rows]
            qfb = [jnp.concatenate([qs[b] * tb["qdec_f"], qs[b] * tb["qdec_b"]], axis=1) for b in rows]
            direct = [_dot_tn(qfb[b].astype(BF16), dybs[b]) for b in rows]
            ds_cat, ds_rows, a_rows = [], [], []
            for b in rows:
                a0 = s01[b][0:c] * tb["d0"]
                a1 = s01[b][c:] * tb["d1"]
                pa0[...] += da01[b][0:c] * a0
                pa1[...] += da01[b][c:] * a1
                ds0 = da01[b][0:c] * tb["d0"]
                ds1 = da01[b][c:] * tb["d1"]
                ds_cat.append(jnp.concatenate([ds0, ds1], axis=1).astype(BF16))
                ds_rows.append(jnp.concatenate([ds0, ds1], axis=0).astype(BF16))
                a_rows.append(jnp.concatenate([a0, a1], axis=0).astype(BF16))
            kbd = [ks[b].astype(F32) * tb["kdec_b"] for b in rows]
            dq_in = [_dot(ds_cat[b], k01[b]) for b in rows]
            dk_in = [_dot_tn(ds_rows[b], q01[b]) for b in rows]
            dv_in = [_dot_tn(a_rows[b], dy01[b]) for b in rows]
            dv_gb = [_dot(kbd[b].astype(BF16), gbb[b]) for b in rows]
            new_rf, new_gb = [], []
            dlf = jnp.zeros((1, LANES), F32)
            dlb = jnp.zeros((1, LANES), F32)
            for b in rows:
                dqf, dqb = dqc[b][:, 0:LANES], dqc[b][:, LANES:]
                qf, qb = qfb[b][:, 0:LANES], qfb[b][:, LANES:]
                dq = dq_in[b] + dqf * tb["qdec_f"] + dqb * tb["qdec_b"]
                dq_ref[b, sl, :] = (dq * 0.125).astype(BF16)
                dk_acc[b, sl, :] = dk_in[b] + dkb[b] * tb["kdec_b"]
                dv_acc[b, sl, :] = dv_in[b] + dv_gb[b]
                dlf = dlf + jnp.sum((row + 1.0) * qf * dqf, axis=0, keepdims=True)
                dlb = dlb + jnp.sum((c - row) * qb * dqb + row * kbd[b] * dkb[b], axis=0, keepdims=True)
                dlb = dlb + c * tb["cdec_b"] * jnp.sum(gbs[b] * rbn[b], axis=0, keepdims=True)
                dirf_ref[b, n] = jnp.where(bd, direct[b][0:LANES], 0.0)
                new_gb.append(jnp.where(bd, direct[b][LANES:], 0.0) + tb["cdec_b"] * gbs[b])
                new_rf.append(rfs[b] * tb["cdec_f"] + kvf_ref[b, n])
            vec_ref[0:1, :] += dlf
            vec_ref[1:2, :] += dlb
            vec_ref[6:7, :] += dgain
            return tuple(new_rf), tuple(new_gb)

        _loop_grouped(n_chunk, sweep_fwd, (zero_states, zero_states), per_trip=4)

        def sweep_bwd(i, gfs):
            n = n_chunk - 1 - i
            sl = pl.ds(pl.multiple_of(n * c, c), c)
            gfb = [gfs[b].astype(BF16) for b in rows]
            kfd = [k_ref[b, sl, :].astype(F32) * tb["kdec_f"] for b in rows]
            dkf = [_dot_nt(v_ref[b, sl, :], gfb[b]) for b in rows]
            dvf = [_dot(kfd[b].astype(BF16), gfb[b]) for b in rows]
            new = []
            dlf = jnp.zeros((1, LANES), F32)
            for b in rows:
                dk_ref[b, sl, :] = (dk_acc[b, sl, :] + dkf[b] * tb["kdec_f"]).astype(BF16)
                dv_ref[b, sl, :] = (dv_acc[b, sl, :] + dvf[b]).astype(BF16)
                dlf = dlf + jnp.sum((c - 1.0 - row) * kfd[b] * dkf[b], axis=0, keepdims=True)
                dlf = dlf + c * tb["cdec_f"] * jnp.sum(gfs[b] * rf_ref[b, n], axis=0, keepdims=True)
                new.append(dirf_ref[b, n] + tb["cdec_f"] * gfs[b])
            vec_ref[0:1, :] += dlf
            return tuple(new)

        _loop_grouped(n_chunk, sweep_bwd, zero_states)
        vec_ref[2:3, :] = jnp.sum(pa0[...] * wf, axis=0, keepdims=True)
        vec_ref[3:4, :] = jnp.sum(pa1[...] * wf, axis=0, keepdims=True)
        vec_ref[4:5, :] = jnp.sum(pa0[...] * wb, axis=0, keepdims=True)
        vec_ref[5:6, :] = jnp.sum(pa1[...] * wb, axis=0, keepdims=True)
        part_ref[...] = vec_ref[...]

    blk, lane, gain, pair = _ret_specs(bsz, s)
    out_bf = jax.ShapeDtypeStruct((bsz, s, RET_WIDTH), BF16)
    state = pltpu.VMEM((bsz, n_chunk, LANES, LANES), F32)
    saved = _ret_state_spec(bsz, n_chunk)[0]
    return _hosted_call(
        body, "ret_bwd", (4,),
        in_specs=[blk(CB_RQ), blk(CB_RK), blk(CB_RV), blk(CB_RG), pair, pair, pair, lane, lane, gain, saved, saved],
        out_specs=[pair, pair, pair, pair, pl.BlockSpec((None, 8, LANES), lambda p: (p, 0, 0))],
        out_shape=[out_bf, out_bf, out_bf, out_bf, jax.ShapeDtypeStruct((4, 8, LANES), F32)],
        scratch_shapes=[state, state,
                        pltpu.VMEM((bsz, s, LANES), BF16), pltpu.VMEM((bsz, s, LANES), F32),
                        pltpu.VMEM((bsz, s, LANES), F32),
                        pltpu.VMEM((c, c), F32), pltpu.VMEM((c, c), F32), pltpu.VMEM((8, LANES), F32)],
        operands=(u3, u3, u3, u3, y_hat, y_rstd, d_o, lgf_l, lgb_l, gn_gain, *states), rider=rider)


def _attn_window_tables(n, s):
    qi = lax.broadcasted_iota(jnp.int32, (CHUNK, 3 * CHUNK), 0)
    kj = lax.broadcasted_iota(jnp.int32, (CHUNK, 3 * CHUNK), 1)
    dist = jnp.abs(kj - CHUNK - qi)
    kpos = n * CHUNK - CHUNK + kj
    valid = (dist <= CHUNK) & (kpos >= 0) & (kpos < s)
    return dist.astype(F32), valid


def _dup_kv_head(x, g):
    lane = lax.broadcasted_iota(jnp.int32, x.shape, 1)
    keep = (lane < HEAD_DIM) == (g == 0)
    xf = x.astype(F32)
    return jnp.where(keep, xf, pltpu.roll(xf, HEAD_DIM, 1))


def _attn_specs(s):
    q = pl.BlockSpec((None, s, 2 * LANES), lambda b, g: (b, 0, CB_AQ // 2 + g))
    k = pl.BlockSpec((None, s, LANES), lambda b, g: (b, 0, CB_AK))
    v = pl.BlockSpec((None, s, LANES), lambda b, g: (b, 0, CB_AV))
    grp = pl.BlockSpec((None, s, 2 * LANES), lambda b, g: (b, 0, g))
    smem = pl.BlockSpec(memory_space=pltpu.SMEM)
    return q, k, v, grp, smem


def _fill_padded(dst_ref, val, s):
    dst_ref[0:CHUNK, :] = jnp.zeros((CHUNK, LANES), dst_ref.dtype)
    dst_ref[CHUNK:CHUNK + s, :] = val.astype(dst_ref.dtype)
    dst_ref[CHUNK + s:2 * CHUNK + s, :] = jnp.zeros((CHUNK, LANES), dst_ref.dtype)


def _attn_probs(sc, slope, snk, dist, valid):
    sc = jnp.where(valid, sc - slope * dist, NEG_INF)
    m = jnp.maximum(jnp.max(sc, axis=1, keepdims=True), snk)
    e = jnp.exp(sc - m)
    es = jnp.exp(snk - m)
    inv = 1.0 / (jnp.sum(e, axis=1, keepdims=True) + es)
    return e * inv, es * inv


def _stack_heads(x2, m0):
    parts = []
    for pr in range(2):
        xp = x2[:, pr * LANES:(pr + 1) * LANES]
        parts += [jnp.where(m0, xp, 0.0), jnp.where(m0, 0.0, xp)]
    return jnp.concatenate(parts, axis=0).astype(BF16)


def _unstack_pair(x_all, pr, m0):
    return jnp.where(m0, x_all[(2 * pr) * CHUNK:(2 * pr + 1) * CHUNK], x_all[(2 * pr + 1) * CHUNK:(2 * pr + 2) * CHUNK])


def _attn_saved_specs(bsz, n_blk):
    specs = [pl.BlockSpec((None, None, n_blk, 4 * CHUNK, w), lambda b, g: (b, g, 0, 0, 0)) for w in (3 * CHUNK, 1)]
    shapes = [jax.ShapeDtypeStruct((bsz, 2, n_blk, 4 * CHUNK, 3 * CHUNK), BF16),
              jax.ShapeDtypeStruct((bsz, 2, n_blk, 4 * CHUNK, 1), F32)]
    return specs, shapes


def _attn_fwd(u3, slopes, sink, rider=None):
    bsz, s, _ = u3.shape
    n_blk = s // CHUNK

    def body(slope_ref, sink_ref, q_ref, k_ref, v_ref, o_ref, p_ref, ps_ref, kp_ref, vp_ref):
        g = pl.program_id(1)
        _fill_padded(kp_ref, _dup_kv_head(k_ref[...], g), s)
        _fill_padded(vp_ref, _dup_kv_head(v_ref[...], g), s)
        m0 = lax.broadcasted_iota(jnp.int32, (CHUNK, LANES), 1) < HEAD_DIM

        def blk(n, carry):
            r0 = pl.multiple_of(n * CHUNK, CHUNK)
            kw = kp_ref[pl.ds(r0, 3 * CHUNK), :]
            vw = vp_ref[pl.ds(r0, 3 * CHUNK), :]
            dist, valid = _attn_window_tables(n, s)
            q_all = _stack_heads(q_ref[pl.ds(r0, CHUNK), :].astype(F32) * 0.125, m0)
            sc_all = _dot_nt(q_all, kw)
            probs, sinks = [], []
            for i in range(4):
                p, ps = _attn_probs(sc_all[i * CHUNK:(i + 1) * CHUNK], slope_ref[g * 4 + i], sink_ref[g * 4 + i],
                                    dist, valid)
                probs.append(p.astype(BF16))
                sinks.append(ps)
            p_all = jnp.concatenate(probs, axis=0)
            p_ref[n] = p_all
            ps_ref[n] = jnp.concatenate(sinks, axis=0)
            out_all = _dot(p_all, vw)
            for pr in range(2):
                o_ref[pl.ds(r0, CHUNK), pr * LANES:(pr + 1) * LANES] = _unstack_pair(out_all, pr, m0).astype(BF16)
            return carry

        lax.fori_loop(0, n_blk, blk, 0, unroll=4)

    q, k, v, grp, smem = _attn_specs(s)
    saved_specs, saved_shapes = _attn_saved_specs(bsz, n_blk)
    return _hosted_call(
        body, "attn_fwd", (bsz, 2),
        in_specs=[smem, smem, q, k, v],
        out_specs=[grp] + saved_specs,
        out_shape=[jax.ShapeDtypeStruct((bsz, s, ATTN_WIDTH), BF16)] + saved_shapes,
        scratch_shapes=[pltpu.VMEM((s + 2 * CHUNK, LANES), BF16), pltpu.VMEM((s + 2 * CHUNK, LANES), BF16)],
        operands=(slopes, sink, u3, u3, u3), rider=rider)


def _attn_bwd(u3, d_o, probs, sink_probs, rider=None):
    bsz, s, _ = u3.shape
    n_blk = s // CHUNK

    def body(q_ref, k_ref, v_ref, do_ref, p_ref, ps_ref, dq_ref, dkv_ref, ds_ref,
             kp_ref, vp_ref, dk_acc, dv_acc):
        g = pl.program_id(1)
        _fill_padded(kp_ref, _dup_kv_head(k_ref[...], g), s)
        _fill_padded(vp_ref, _dup_kv_head(v_ref[...], g), s)
        dk_acc[...] = jnp.zeros_like(dk_acc)
        dv_acc[...] = jnp.zeros_like(dv_acc)
        m0 = lax.broadcasted_iota(jnp.int32, (CHUNK, LANES), 1) < HEAD_DIM

        def blk(n, dsink):
            r0 = pl.multiple_of(n * CHUNK, CHUNK)
            win = pl.ds(r0, 3 * CHUNK)
            kw = kp_ref[win, :]
            vw = vp_ref[win, :]
            q_all = _stack_heads(q_ref[pl.ds(r0, CHUNK), :].astype(F32) * 0.125, m0)
            do_all = _stack_heads(do_ref[pl.ds(r0, CHUNK), :].astype(F32), m0)
            p_all = p_ref[n]
            ps_all = ps_ref[n]
            dp_all = _dot_nt(do_all, vw)
            new_dsink, dscs = [], []
            for i in range(4):
                rows = slice(i * CHUNK, (i + 1) * CHUNK)
                p = p_all[rows].astype(F32)
                dp = dp_all[rows]
                delta = jnp.sum(p * dp, axis=1, keepdims=True)
                dscs.append((p * (dp - delta)).astype(BF16))
                dsh = jnp.sum(ps_all[rows] * delta, axis=0, keepdims=True)
                new_dsink.append(dsink[i] - jnp.broadcast_to(dsh, (1, LANES)))
            dsc_all = jnp.concatenate(dscs, axis=0)
            dq_all = _dot(dsc_all, kw)
            dk_acc[win, :] += _dot_tn(dsc_all, q_all)
            dv_acc[win, :] += _dot_tn(p_all, do_all)
            for pr in range(2):
                dq_ref[pl.ds(r0, CHUNK), pr * LANES:(pr + 1) * LANES] = (
                    _unstack_pair(dq_all, pr, m0) * 0.125).astype(BF16)
            return tuple(new_dsink)

        dsink = _loop_grouped(n_blk, blk, tuple(jnp.zeros((1, LANES), F32) for _ in range(4)), per_trip=4)
        dk = dk_acc[CHUNK:CHUNK + s, :]
        dv = dv_acc[CHUNK:CHUNK + s, :]
        lane = lax.broadcasted_iota(jnp.int32, (s, LANES), 1)
        fold = lambda a: a + pltpu.roll(a, HEAD_DIM, 1)
        dkv_ref[...] = jnp.where(lane < HEAD_DIM, fold(dk), fold(dv)).astype(BF16)
        ds_ref[...] = jnp.zeros_like(ds_ref)
        for i in range(4):
            ds_ref[i:i + 1, :] = dsink[i]

    q, k, v, grp, _ = _attn_specs(s)
    return _hosted_call(
        body, "attn_bwd", (bsz, 2),
        in_specs=[q, k, v, grp] + _attn_saved_specs(bsz, n_blk)[0],
        out_specs=[grp, pl.BlockSpec((None, s, LANES), lambda b, g: (b, 0, g)),
                   pl.BlockSpec((None, None, 8, LANES), lambda b, g: (b, g, 0, 0))],
        out_shape=[jax.ShapeDtypeStruct((bsz, s, ATTN_WIDTH), BF16), jax.ShapeDtypeStruct((bsz, s, 2 * LANES), BF16),
                   jax.ShapeDtypeStruct((bsz, 2, 8, LANES), F32)],
        scratch_shapes=[pltpu.VMEM((s + 2 * CHUNK, LANES), BF16), pltpu.VMEM((s + 2 * CHUNK, LANES), BF16),
                        pltpu.VMEM((s + 2 * CHUNK, LANES), F32), pltpu.VMEM((s + 2 * CHUNK, LANES), F32)],
        operands=(u3, u3, u3, d_o, probs, sink_probs), rider=rider)


def _ffn_bwd(dz2, gs, us, pg, ple, zh1, r1, g1, wg4, wu4, wd4, wpg, w_out):
    t = dz2.shape[0]
    tm = 256
    wg_t, wu_t, wd_all = (w.reshape(FFN, D_MODEL) for w in (wg4, wu4, wd4))

    def body(dz_ref, gs_ref, us_ref, pg_ref, ple_ref, zh_ref, r_ref, g1_ref,
             wg_hbm, wu_hbm, wd_hbm, wpg_hbm, wo_hbm,
             dgs_ref, dus_ref, dsp_ref, dple_ref, dz1_ref, dyr_ref, dya_ref, dg1_ref, db1_ref,
             wg, wu, wd, wpg, wo, wsem):
        step = pl.program_id(0)
        loads = _resident_quarters(wd_hbm, wd) + _resident_quarters(wg_hbm, wg) + _resident_quarters(wu_hbm, wu)
        _load_resident(step, loads + [(wpg_hbm, wpg), (wo_hbm, wo)], wsem)

        @pl.when(step == 0)
        def _():
            dg1_ref[...] = jnp.zeros_like(dg1_ref)
            db1_ref[...] = jnp.zeros_like(db1_ref)

        dz = dz_ref[...]
        dzb = dz.astype(BF16)
        dh = ALPHA * dz
        pending = []
        chunks = [slice(n * FFN_CHUNK, (n + 1) * FFN_CHUNK) for n in range(N_FFN_CHUNK)]
        for n in range(N_FFN_CHUNK + 1):
            if n < N_FFN_CHUNK:
                da = _dot_nt(dzb, wd[chunks[n], :])
                gj = gs_ref[:, chunks[n]].astype(F32)
                uj = us_ref[:, chunks[n]].astype(F32)
                sg = _sigmoid(gj)
                dgj = (da * uj * sg * (1.0 + gj * (1.0 - sg))).astype(BF16)
                duj = (da * gj * sg).astype(BF16)
                dgs_ref[:, chunks[n]] = dgj
                dus_ref[:, chunks[n]] = duj
                pending.append((dgj, duj))
            if n > 0:
                dgp, dup = pending[n - 1]
                dh = dh + _dot(dgp, wg[chunks[n - 1], :]) + _dot(dup, wu[chunks[n - 1], :])
        pgv = pg_ref[...].astype(F32)
        plev = ple_ref[...].astype(F32)
        dple_ref[...] = (dz * pgv).astype(BF16)
        dsp = (dz * plev * pgv * (1.0 - pgv)).astype(BF16)
        dsp_ref[...] = dsp
        dh = dh + _dot_nt(dsp, wpg[...])
        zh = zh_ref[...]
        dg1_ref[...] += jnp.sum(dh * zh, axis=0, keepdims=True)
        db1_ref[...] += jnp.sum(dh, axis=0, keepdims=True)
        dzh = dh * g1_ref[...]
        m1 = jnp.mean(dzh, axis=1, keepdims=True)
        m2 = jnp.mean(dzh * zh, axis=1, keepdims=True)
        dz1 = r_ref[...] * (dzh - m1 - zh * m2)
        dz1_ref[...] = dz1
        dyc = _dot_nt(dz1.astype(BF16), wo[...])
        dyr_ref[...] = dyc[:, 0:RET_WIDTH].astype(BF16)
        dya_ref[...] = dyc[:, RET_WIDTH:].astype(BF16)

    row = lambda w: pl.BlockSpec((tm, w), lambda i: (i, 0))
    const = lambda s: pl.BlockSpec(s, lambda i: (0, 0))
    hbm = pl.BlockSpec(memory_space=pl.ANY)
    hid_shape = jax.ShapeDtypeStruct((t, FFN), BF16)
    return pl.pallas_call(
        body, name="ffn_bwd", grid=(t // tm,),
        in_specs=[row(D_MODEL), row(FFN), row(FFN), row(D_MODEL), row(D_MODEL), row(D_MODEL), row(1),
                  const((1, D_MODEL)), hbm, hbm, hbm, hbm, hbm],
        out_specs=[row(FFN), row(FFN), row(D_MODEL), row(D_MODEL), row(D_MODEL), row(RET_WIDTH), row(ATTN_WIDTH),
                   const((1, D_MODEL)), const((1, D_MODEL))],
        out_shape=[hid_shape, hid_shape, jax.ShapeDtypeStruct((t, D_MODEL), BF16),
                   jax.ShapeDtypeStruct((t, D_MODEL), BF16), jax.ShapeDtypeStruct((t, D_MODEL), F32),
                   jax.ShapeDtypeStruct((t, RET_WIDTH), BF16), jax.ShapeDtypeStruct((t, ATTN_WIDTH), BF16),
                   jax.ShapeDtypeStruct((1, D_MODEL), F32), jax.ShapeDtypeStruct((1, D_MODEL), F32)],
        scratch_shapes=[pltpu.VMEM((FFN, D_MODEL), BF16), pltpu.VMEM((FFN, D_MODEL), BF16),
                        pltpu.VMEM((FFN, D_MODEL), BF16),
                        pltpu.VMEM(wpg.shape, BF16), pltpu.VMEM(w_out.shape, BF16),
                        pltpu.SemaphoreType.DMA((3 * N_SHARD + 2,))],
        compiler_params=_params("arbitrary", vmem=VMEM_LIMIT),
    )(dz2, gs, us, pg, ple, zh1, r1, g1, wg_t, wu_t, wd_all, wpg, w_out)


def _wgrad_misc(y_ret, y_att, dz1, hb, dsp, p2d, dple, rider=None):
    t = dz1.shape[0]
    tk = min(t, 512)
    pc = D_MODEL // N_SHARD

    def body(yr_ref, ya_ref, dz_ref, hb_ref, dsp_ref, p_ref, dple_ref, wo_ref, wpg_ref, wpe_ref):
        @pl.when(pl.program_id(0) == 0)
        def _():
            wo_ref[...] = jnp.zeros_like(wo_ref)
            wpg_ref[...] = jnp.zeros_like(wpg_ref)
            wpe_ref[...] = jnp.zeros_like(wpe_ref)

        dzb = dz_ref[...].astype(BF16)
        wo_ref[0:RET_WIDTH, :] += _dot_tn(yr_ref[...], dzb)
        wo_ref[RET_WIDTH:, :] += _dot_tn(ya_ref[...], dzb)
        wpg_ref[...] += _dot_tn(hb_ref[...], dsp_ref[...])
        dpe = _dot_tn(p_ref[...].astype(BF16), dple_ref[...])
        for j in range(N_SHARD):
            wpe_ref[j] += dpe[:, j * pc:(j + 1) * pc]

    row = lambda w: pl.BlockSpec((tk, w), lambda k: (k, 0))
    const = lambda s: pl.BlockSpec(s, lambda k: (0,) * len(s))
    return _hosted_call(
        body, "wgrad_misc", (t // tk,),
        in_specs=[row(RET_WIDTH), row(ATTN_WIDTH), row(D_MODEL), row(D_MODEL), row(D_MODEL), row(PLE_DIM),
                  row(D_MODEL)],
        out_specs=[const((D_MODEL, D_MODEL)), const((D_MODEL, D_MODEL)), const((N_SHARD, PLE_DIM, pc))],
        out_shape=[jax.ShapeDtypeStruct((D_MODEL, D_MODEL), F32), jax.ShapeDtypeStruct((D_MODEL, D_MODEL), F32),
                   jax.ShapeDtypeStruct((N_SHARD, PLE_DIM, pc), F32)],
        scratch_shapes=[], operands=(y_ret, y_att, dz1, hb, dsp, p2d, dple), rider=rider, semantics=["arbitrary"])


def _wgrad_ffn(acts, dgs, dus, hb, dz2b):
    t = dz2b.shape[0]
    tk = min(t, 512)
    nk = t // tk

    def body(act_ref, dg_ref, du_ref, hb_ref, dz_ref, og_ref, ou_ref, od_ref):
        @pl.when(pl.program_id(1) == 0)
        def _():
            og_ref[...] = jnp.zeros_like(og_ref)
            ou_ref[...] = jnp.zeros_like(ou_ref)
            od_ref[...] = jnp.zeros_like(od_ref)

        hbv = hb_ref[...]
        og_ref[...] += _dot_tn(dg_ref[...], hbv)
        ou_ref[...] += _dot_tn(du_ref[...], hbv)
        od_ref[...] += _dot_tn(act_ref[...], dz_ref[...])

    half = FFN // 2
    a_spec = pl.BlockSpec((tk, half), lambda j, k: (k, j))
    b_spec = pl.BlockSpec((tk, D_MODEL), lambda j, k: (k, 0))
    o_spec = pl.BlockSpec((half, D_MODEL), lambda j, k: (j, 0))
    o_shape = jax.ShapeDtypeStruct((FFN, D_MODEL), F32)
    outs = pl.pallas_call(
        body, name="wgrad_ffn", grid=(2, nk),
        in_specs=[a_spec, a_spec, a_spec, b_spec, b_spec],
        out_specs=[o_spec] * 3, out_shape=[o_shape] * 3,
        compiler_params=_params("parallel", "arbitrary", vmem=VMEM_LIMIT),
    )(acts, dgs, dus, hb, dz2b)
    return [o.reshape(N_SHARD, FFN_SHARD, D_MODEL) for o in outs]


KV_ORDER = (0, 128, 64, 192)


def _wgrad_in(pieces, x2d, rider=None):
    t = x2d.shape[0]
    tk = min(t, 512)
    nk = t // tk
    kv0 = CB_AK * LANES

    def body(p0, p1, p2, p3, p4, pkv, x_ref, o_ref):
        @pl.when(pl.program_id(0) == 0)
        def _():
            o_ref[...] = jnp.zeros_like(o_ref)

        xb = x_ref[...].astype(BF16)
        for i, ref in enumerate((p0, p1, p2, p3, p4)):
            o_ref[i * 512:(i + 1) * 512, :] += _dot_tn(ref[...], xb)
        dkv = _dot_tn(pkv[...], xb)
        for i, o in enumerate(KV_ORDER):
            o_ref[kv0 + o:kv0 + o + HEAD_DIM, :] += dkv[i * HEAD_DIM:(i + 1) * HEAD_DIM]

    row = lambda w: pl.BlockSpec((tk, w), lambda k: (k, 0))
    return _hosted_call(
        body, "wgrad_in", (nk,),
        in_specs=[row(512)] * 5 + [row(256), row(D_MODEL)],
        out_specs=[pl.BlockSpec((IN_WIDTH, D_MODEL), lambda k: (0, 0))],
        out_shape=[jax.ShapeDtypeStruct((IN_WIDTH, D_MODEL), F32)],
        scratch_shapes=[], operands=(*pieces, x2d), rider=rider, semantics=["arbitrary"])


def _inproj_bwd(dz1, pieces, w_in_t, rider=None):
    t = dz1.shape[0]
    tm = 512
    kv0 = CB_AK * LANES

    def body(dz_ref, p0, p1, p2, p3, p4, pkv, w_ref, o_ref):
        acc = ALPHA * dz_ref[...]
        for i, ref in enumerate((p0, p1, p2, p3, p4)):
            acc = acc + _dot(ref[...], w_ref[i * 512:(i + 1) * 512, :])
        w_kv = jnp.concatenate([w_ref[kv0 + o:kv0 + o + HEAD_DIM, :] for o in KV_ORDER], axis=0)
        o_ref[...] = acc + _dot(pkv[...], w_kv)

    row = lambda w: pl.BlockSpec((tm, w), lambda i: (i, 0))
    return _hosted_call(
        body, "inproj_bwd", (t // tm,),
        in_specs=[row(D_MODEL)] + [row(512)] * 5 + [row(256), pl.BlockSpec((IN_WIDTH, D_MODEL), lambda i: (0, 0))],
        out_specs=[row(D_MODEL)],
        out_shape=[jax.ShapeDtypeStruct((t, D_MODEL), F32)],
        scratch_shapes=[], operands=(dz1, *pieces, w_in_t), rider=rider)


def _coords():
    return lax.axis_index("x"), lax.axis_index("y"), lax.axis_index("c")


def _chip_of(x, y, rel):
    return (1 - x if rel & 2 else x), (1 - y if rel & 1 else y)


def _gather_and_cast(shard, others):
    near = _gather_near_rider([shard])
    relay = _gather_relay_rider(near.out_shapes, chained=True)
    pass_near = _gather_pass_rider(near.out_shapes, chained=True, rels=NEAR)
    pass_far = _gather_pass_rider(near.out_shapes, chained=True, rels=(3,))
    riders = [near, relay, pass_near, pass_far]
    no = len(others)

    def body(*refs):
        shard_ref, wide = refs[0], refs[1:1 + no]
        out_ref, narrow = refs[1 + no], refs[2 + no:2 + 2 * no]
        k = 2 + 2 * no
        vin, vout, (lsem, ssem) = refs[k:k + no], refs[k + no:k + 2 * no], refs[k + 2 * no:k + 2 * no + 2]
        k += 2 * no + 2
        sems = {}
        for r in riders:
            sems[id(r)] = refs[k:k + len(r.sems)]
            k += len(r.sems)
        run = lambda r, method: getattr(r, method)([shard_ref], [out_ref], sems[id(r)])
        loads = [pltpu.make_async_copy(wide[w], vin[w], lsem.at[w]) for w in range(no)]
        stores = [pltpu.make_async_copy(vout[w], narrow[w], ssem.at[w]) for w in range(no)]

        run(near, "start")
        for cp in loads:
            cp.start()
        for w in range(no):
            loads[w].wait()
            vout[w][...] = vin[w][...].astype(BF16)
            stores[w].start()
        run(near, "finish")
        run(relay, "start")
        run(pass_near, "start")
        run(relay, "finish")
        run(pass_far, "start")
        run(pass_near, "finish")
        run(pass_far, "finish")
        for cp in stores:
            cp.wait()

    hbm = pl.BlockSpec(memory_space=pl.ANY)
    dma = pltpu.SemaphoreType.DMA
    gathered, *cast = pl.pallas_call(
        body, name="gather_weights", in_specs=[hbm] * (1 + no), out_specs=[hbm] * (1 + no),
        out_shape=near.out_shapes + [jax.ShapeDtypeStruct(a.shape, BF16) for a in others],
        scratch_shapes=[pltpu.VMEM(a.shape, F32) for a in others] + [pltpu.VMEM(a.shape, BF16) for a in others]
        + [dma((no,)), dma((no,))] + [s for r in riders for s in r.sems],
        compiler_params=_params(vmem=VMEM_LIMIT),
    )(shard, *others)
    return gathered, cast


def _gather_half(outs, w, chip, cc):
    h = outs[w].shape[1] // 2
    return outs[w].at[chip, pl.ds(cc * h, h), :]


NEAR = (1, 2)


def _gather_near_rider(shards, rels=NEAR):
    nw, nr = len(shards), len(rels)

    def copies(ins, outs, sems, arrivals):
        send, recv, lsend, lrecv = sems
        x, y, c = _coords()
        me = 2 * x + y
        own = [pltpu.make_async_remote_copy(
            src_ref=ins[w], dst_ref=outs[w].at[me], send_sem=lsend.at[w], recv_sem=lrecv.at[w],
            device_id=(x, y, 1 - c), device_id_type=MESH) for w in range(nw)]
        out, arrive = [], []
        for i, rel in enumerate(rels):
            kx, ky = _chip_of(x, y, rel)
            for w in range(nw):
                h = shards[w].shape[0] // 2
                sem = dict(send_sem=send.at[w * nr + i], recv_sem=recv.at[w * nr + i],
                           device_id=(kx, ky, c), device_id_type=MESH)
                out.append(pltpu.make_async_remote_copy(
                    src_ref=ins[w].at[pl.ds(c * h, h), :], dst_ref=_gather_half(outs, w, me, c), **sem))
                if arrivals:
                    theirs = _gather_half(outs, w, 2 * kx + ky, c)
                    arrive.append(pltpu.make_async_remote_copy(src_ref=theirs, dst_ref=theirs, **sem))
        return own, out, arrive

    def start(ins, outs, sems):
        own, out, _ = copies(ins, outs, sems, arrivals=False)
        for cp in own + out:
            cp.start()

    def finish(ins, outs, sems):
        own, out, arrive = copies(ins, outs, sems, arrivals=True)
        for cp in arrive:
            cp.wait_recv()
        for cp in out:
            cp.wait_send()
        for cp in own:
            cp.wait()

    dma = pltpu.SemaphoreType.DMA
    return _Rider(shards, [jax.ShapeDtypeStruct((N_SHARD,) + s.shape, s.dtype) for s in shards],
                  [dma((nr * nw,)), dma((nr * nw,)), dma((nw,)), dma((nw,))], start, finish)


def _gather_relay_rider(gathered, chained=False):
    nw = len(gathered)

    def quarter(outs, w, chip, c, p):
        q = outs[w].shape[1] // 4
        return outs[w].at[chip, pl.ds(c * 2 * q + p * q, q), :]

    def copies(outs, sems):
        send, recv = sems
        x, y, c = _coords()
        (yx, yy), (xx, xy), (dx, dy) = (_chip_of(x, y, rel) for rel in (1, 2, 3))
        out, arrive = [], []
        for w in range(nw):
            for p, (src_chip, dst) in enumerate(((2 * xx + xy, (yx, yy)), (2 * yx + yy, (xx, xy)))):
                rows = quarter(outs, w, src_chip, c, p)
                sem = dict(send_sem=send.at[w * 2 + p], recv_sem=recv.at[w * 2 + p], device_id_type=MESH)
                out.append(pltpu.make_async_remote_copy(src_ref=rows, dst_ref=rows, device_id=(*dst, c), **sem))
                mine = quarter(outs, w, 2 * dx + dy, c, p)
                arrive.append(pltpu.make_async_remote_copy(src_ref=mine, dst_ref=mine, device_id=(*dst, c), **sem))
        return out, arrive

    def start(ins, outs, sems):
        for cp in copies(outs, sems)[0]:
            cp.start()

    def finish(ins, outs, sems):
        out, arrive = copies(outs, sems)
        for cp in arrive:
            cp.wait_recv()
        for cp in out:
            cp.wait_send()

    dma = pltpu.SemaphoreType.DMA
    shapes = [jax.ShapeDtypeStruct(g.shape, g.dtype) for g in gathered]
    if chained:
        return _Rider([], [], [dma((2 * nw,)), dma((2 * nw,))], start, finish)
    return _Rider(gathered, shapes, [dma((2 * nw,)), dma((2 * nw,))], start, finish,
                  aliases={w: w for w in range(nw)})


def _gather_pass_rider(gathered, chained=False, rels=(1, 2, 3), sibling_id=None):
    nw, nr = len(gathered), len(rels)

    def copies(outs, sems, cc):
        send, recv = sems
        x, y, c = _coords()
        res = []
        for i, rel in enumerate(rels):
            kx, ky = _chip_of(x, y, rel)
            for w in range(nw):
                rows = _gather_half(outs, w, 2 * kx + ky, cc)
                res.append(pltpu.make_async_remote_copy(
                    src_ref=rows, dst_ref=rows, send_sem=send.at[w * nr + i], recv_sem=recv.at[w * nr + i],
                    device_id=(x, y, 1 - c), device_id_type=MESH))
        return res

    def start(ins, outs, sems):
        for cp in copies(outs, sems, lax.axis_index("c")):
            cp.start()

    def finish(ins, outs, sems):
        c = lax.axis_index("c")
        for cp in copies(outs, sems, 1 - c):
            cp.wait_recv()
        for cp in copies(outs, sems, c):
            cp.wait_send()

    dma = pltpu.SemaphoreType.DMA
    shapes = [jax.ShapeDtypeStruct(g.shape, g.dtype) for g in gathered]
    if chained:
        return _Rider([], [], [dma((nr * nw,)), dma((nr * nw,))], start, finish)
    return _Rider(gathered, shapes, [dma((nr * nw,)), dma((nr * nw,))], start, finish,
                  aliases={w: w for w in range(nw)}, sibling_id=sibling_id)


def _exchange_halves_rider(parts, sibling_id=None):
    nw = len(parts)

    def copies(ins, outs, sems):
        send, recv = sems
        x, y, c = _coords()
        res = []
        for w in range(nw):
            h = parts[w].shape[1] // 2
            res.append(pltpu.make_async_remote_copy(
                src_ref=ins[w].at[:, pl.ds((1 - c) * h, h), :], dst_ref=outs[w],
                send_sem=send.at[w], recv_sem=recv.at[w], device_id=(x, y, 1 - c), device_id_type=MESH))
        return res

    def start(ins, outs, sems):
        for cp in copies(ins, outs, sems):
            cp.start()

    def finish(ins, outs, sems):
        for cp in copies(ins, outs, sems):
            cp.wait()

    dma = pltpu.SemaphoreType.DMA
    return _Rider(parts, [jax.ShapeDtypeStruct((N_SHARD, p.shape[1] // 2, p.shape[2]), p.dtype) for p in parts],
                  [dma((nw,)), dma((nw,))], start, finish, sibling_id=sibling_id)


def _add_halves(parts, theirs, pos):
    nw = len(parts)
    split = 1

    def body(pos_ref, *refs):
        ins, oth = refs[:nw], refs[nw:2 * nw]
        o32, o16 = refs[2 * nw:3 * nw], refs[3 * nw:]
        sums = [ins[w][...] + oth[w][...] for w in range(nw)]
        for w in range(nw):
            o16[w][...] = sums[w].astype(BF16)

        @pl.when(pl.program_id(1) == pos_ref[0])
        def _():
            for w in range(nw):
                o32[w][...] = sums[w]

    in_specs, oth_specs, o32_specs, shapes32, shapes16 = [], [], [], [], []
    for p in parts:
        hb = p.shape[1] // 2 // split
        blk = (None, hb, p.shape[2])
        in_specs.append(pl.BlockSpec(blk, lambda i, j, pos_ref: (j, pos_ref[1] * split + i, 0)))
        oth_specs.append(pl.BlockSpec(blk, lambda i, j, pos_ref: (j, i, 0)))
        o32_specs.append(pl.BlockSpec((hb, p.shape[2]), lambda i, j, pos_ref: (i, 0)))
        shapes32.append(jax.ShapeDtypeStruct((p.shape[1] // 2, p.shape[2]), F32))
        shapes16.append(jax.ShapeDtypeStruct((N_SHARD, p.shape[1] // 2, p.shape[2]), BF16))
    return pl.pallas_call(
        body, name="add_halves",
        grid_spec=pltpu.PrefetchScalarGridSpec(
            num_scalar_prefetch=1, grid=(split, N_SHARD),
            in_specs=in_specs + oth_specs, out_specs=o32_specs + oth_specs),
        out_shape=shapes32 + shapes16,
        compiler_params=_params("parallel", "arbitrary", vmem=VMEM_LIMIT),
    )(pos, *parts, *theirs)


def _exchange_chips_rider(sums16, rows=None, into=None, chips_id=None):
    nw = len(sums16)
    rows = rows or [(0, s.shape[1]) for s in sums16]
    held = [w for w in range(nw) if into is not None and into[w] is not None]

    def copies(ins, outs, sems):
        send, recv = sems
        x, y, c = _coords()
        res = []
        for rel in (1, 2, 3):
            kx, ky = _chip_of(x, y, rel)
            for w in range(nw):
                r0, n = rows[w]
                res.append(pltpu.make_async_remote_copy(
                    src_ref=ins[w].at[2 * kx + ky, pl.ds(r0, n), :], dst_ref=outs[w].at[rel - 1, pl.ds(r0, n), :],
                    send_sem=send.at[w * 3 + rel - 1], recv_sem=recv.at[w * 3 + rel - 1],
                    device_id=(kx, ky, c), device_id_type=MESH))
        return res

    def start(ins, outs, sems):
        for cp in copies(ins, outs, sems):
            cp.start()

    def finish(ins, outs, sems):
        for cp in copies(ins, outs, sems):
            cp.wait()

    dma = pltpu.SemaphoreType.DMA
    return _Rider(list(sums16) + [into[w] for w in held],
                  [jax.ShapeDtypeStruct((3,) + s.shape[1:], BF16) for s in sums16],
                  [dma((3 * nw,)), dma((3 * nw,))], start, finish, aliases={nw + i: w for i, w in enumerate(held)},
                  chips_id=chips_id)


def _add_chips(sums32, theirs, pos):
    nw = len(sums32)
    split = 2
    hbs = [s.shape[0] // split for s in sums32]

    def body(pos_ref, *refs):
        ins, oth, outs, bufs = (refs[k * nw:(k + 1) * nw] for k in range(4))
        lsem, ssem, rsem = refs[4 * nw:]
        i = pl.program_id(0)
        x, y, c = _coords()

        def copies(w, j):
            rows = pl.ds(pl.multiple_of((pos_ref[1] * split + j) * hbs[w], 8), hbs[w])
            return (pltpu.make_async_copy(bufs[w].at[j], outs[w].at[rows, :], lsem.at[w, j]),
                    pltpu.make_async_remote_copy(
                        src_ref=bufs[w].at[j], dst_ref=outs[w].at[rows, :], send_sem=ssem.at[w, j],
                        recv_sem=rsem.at[w, j], device_id=(x, y, 1 - c), device_id_type=MESH))

        @pl.when(i == 0)
        def _():
            _meet_sibling()

        for w in range(nw):
            acc = ins[w][...]
            for r in range(3):
                acc = acc + oth[w][r].astype(F32)
            bufs[w][i] = acc
            for cp in copies(w, i):
                cp.start()

        @pl.when(i == split - 1)
        def _():
            for w in range(nw):
                for j in range(split):
                    local, remote = copies(w, j)
                    local.wait()
                    remote.wait()

    in_specs, oth_specs, shapes, scratch = [], [], [], []
    for s, hb in zip(sums32, hbs):
        in_specs.append(pl.BlockSpec((hb, s.shape[1]), lambda i, pos_ref: (i, 0)))
        oth_specs.append(pl.BlockSpec((3, hb, s.shape[1]), lambda i, pos_ref: (0, i, 0)))
        shapes.append(jax.ShapeDtypeStruct((2 * s.shape[0], s.shape[1]), F32))
        scratch.append(pltpu.VMEM((split, hb, s.shape[1]), F32))
    dma = pltpu.SemaphoreType.DMA
    return pl.pallas_call(
        body, name="add_chips",
        grid_spec=pltpu.PrefetchScalarGridSpec(
            num_scalar_prefetch=1, grid=(split,), in_specs=in_specs + oth_specs,
            out_specs=[pl.BlockSpec(memory_space=pl.ANY)] * nw,
            scratch_shapes=scratch + [dma((nw, split)), dma((nw, split)), dma((nw, split))]),
        out_shape=shapes,
        compiler_params=_params("arbitrary", vmem=VMEM_LIMIT, collective_id=2),
    )(pos, *sums32, *theirs)


def _adamw_math(w, g, m, v):
    m = ADAM_B1 * m + (1.0 - ADAM_B1) * g
    v = ADAM_B2 * v + (1.0 - ADAM_B2) * (g * g)
    m_hat = m / (1.0 - ADAM_B1 ** ADAM_STEP)
    v_hat = v / (1.0 - ADAM_B2 ** ADAM_STEP)
    delta = -ADAM_LR * (m_hat / (jnp.sqrt(v_hat) + ADAM_EPS) + ADAM_WD * w)
    return delta, m, v


def _adamw(ws, gs, ms, vs):
    nw = len(ws)
    split = 8

    def body(*refs):
        w_r, g_r, m_r, v_r = (refs[i * nw:(i + 1) * nw] for i in range(4))
        g_o, d_o, m_o, v_o = (refs[(4 + i) * nw:(5 + i) * nw] for i in range(4))
        for k in range(nw):
            g = g_r[k][...]
            d, m, v = _adamw_math(w_r[k][...], g, m_r[k][...], v_r[k][...])
            g_o[k][...] = g
            d_o[k][...] = d
            m_o[k][...] = m
            v_o[k][...] = v

    specs = [pl.BlockSpec((w.shape[0] // split, w.shape[1]), lambda i: (i, 0)) for w in ws]
    shapes = [jax.ShapeDtypeStruct(w.shape, F32) for w in ws]
    outs = pl.pallas_call(
        body, name="adamw", grid=(split,),
        in_specs=specs * 4, out_specs=specs * 4, out_shape=shapes * 4,
        compiler_params=_params("parallel", vmem=VMEM_LIMIT),
    )(*ws, *gs, *ms, *vs)
    return outs[:nw], outs[nw:2 * nw], outs[2 * nw:3 * nw], outs[3 * nw:]


SMALL_ROWS = 8
SMALL_COLS = D_MODEL
LOSS_COL = RET_WIDTH + 24


def _small_allreduce_adamw(part, w, m, v, rider=None):
    def body(part_ref, w_ref, m_ref, v_ref, g_out, d_out, m_out, v_out, all_ref, send, recv):
        x, y, c = _coords()
        me = 4 * x + 2 * y + c
        all_ref[me] = part_ref[...]
        copies = []
        for rel in range(1, 8):
            px = 1 - x if rel & 4 else x
            py = 1 - y if rel & 2 else y
            pc = 1 - c if rel & 1 else c
            copies.append(pltpu.make_async_remote_copy(
                src_ref=part_ref, dst_ref=all_ref.at[me],
                send_sem=send.at[rel - 1], recv_sem=recv.at[rel - 1], device_id=(px, py, pc), device_id_type=MESH))
        for cp in copies:
            cp.start()
        for cp in copies:
            cp.wait()
        g = all_ref[0]
        for k in range(1, 8):
            g = g + all_ref[k]
        d, mn, vn = _adamw_math(w_ref[...], g, m_ref[...], v_ref[...])
        g_out[...] = g
        d_out[...] = d
        m_out[...] = mn
        v_out[...] = vn

    vm = pl.BlockSpec(memory_space=pltpu.VMEM)
    shape = jax.ShapeDtypeStruct((SMALL_ROWS, SMALL_COLS), F32)
    return _hosted_call(
        body, "small_allreduce_adamw", (1,),
        in_specs=[vm] * 4, out_specs=[vm] * 4, out_shape=[shape] * 4,
        scratch_shapes=[pltpu.VMEM((8, SMALL_ROWS, SMALL_COLS), F32),
                        pltpu.SemaphoreType.DMA((7,)), pltpu.SemaphoreType.DMA((7,))],
        operands=(part, w, m, v), rider=rider, semantics=["arbitrary"])


SMALL_NAMES = ("ret_decay_fwd", "ret_decay_bwd", "attn_sink", "ret_gn_gain",
               "ln1_gain", "ln1_bias", "ln2_gain", "ln2_bias")


LN_NAMES = ("ln1_gain", "ln1_bias", "ln2_gain", "ln2_bias")


def _pack_small(vals, extra=None):
    tail = jnp.zeros((1, 1), F32) if extra is None else extra.reshape(1, 1)
    row4 = jnp.concatenate([vals["ret_gn_gain"], vals["ret_decay_fwd"], vals["ret_decay_bwd"], vals["attn_sink"],
                            tail, jnp.zeros((1, SMALL_COLS - LOSS_COL - 1), F32)], axis=1)
    rows = [vals[n] for n in LN_NAMES] + [row4, jnp.zeros((SMALL_ROWS - 5, SMALL_COLS), F32)]
    return jnp.concatenate(rows, axis=0)


def _unpack_small(packed):
    o = RET_WIDTH
    where = [(n, i, 0, SMALL_COLS) for i, n in enumerate(LN_NAMES)] + [
        ("ret_gn_gain", 4, 0, o), ("ret_decay_fwd", 4, o, 8), ("ret_decay_bwd", 4, o + 8, 8),
        ("attn_sink", 4, o + 16, 8)]
    na, k = len(packed), len(where)

    def body(*refs):
        for a in range(na):
            rows = refs[a][...]
            for b, (_, row, col, n) in enumerate(where):
                refs[na + a * k + b][...] = rows[row:row + 1, col:col + n]
            if a == 0:
                refs[na + na * k][...] = rows[4:5, LOSS_COL:LOSS_COL + 1] * (0.5 / D_MODEL)

    vmem = pl.BlockSpec(memory_space=pltpu.VMEM)
    outs = pl.pallas_call(
        body, name="unpack_small", in_specs=[vmem] * na, out_specs=[vmem] * (na * k + 1),
        out_shape=[jax.ShapeDtypeStruct((1, n), F32) for _ in range(na) for (_, _, _, n) in where]
        + [jax.ShapeDtypeStruct((1, 1), F32)])(*packed)
    return [{where[b][0]: outs[a * k + b] for b in range(k)} for a in range(na)], outs[na * k]


def _local_step(x, p, tgt, w_in_t, rest, small, pos=None, small_state=None):
    bsz, s, _ = x.shape
    t = bsz * s
    x2d = x.reshape(t, D_MODEL)
    p2d = p.reshape(t, PLE_DIM)
    tgt2d = tgt.reshape(t, D_MODEL)
    dec_f = small["ret_decay_fwd"].reshape(8)
    dec_b = small["ret_decay_bwd"].reshape(8)
    lg_f = jnp.log1p(-jnp.exp2(dec_f))
    lg_b = jnp.log1p(-jnp.exp2(dec_b))
    per_lane = lambda v: jnp.repeat(v, HEAD_DIM).reshape(4, 1, LANES)
    lgf_l, lgb_l = per_lane(lg_f), per_lane(lg_b)
    sink = small["attn_sink"].reshape(8)
    slopes = 2.0 ** (-(jnp.arange(8, dtype=F32) + 1.0))
    gn_gain = small["ret_gn_gain"]
    g1, b1, g2, b2 = (small[n] for n in ("ln1_gain", "ln1_bias", "ln2_gain", "ln2_bias"))

    dist = pos is not None
    shard = dict(zip(REST_NAMES, rest)) if dist else {}
    near = lambda names, rels=NEAR: _gather_near_rider([shard[n] for n in names], rels)
    wave1, wave2, wave3 = ("w_out", "w_ple_gate", "w_ffn_gate"), ("w_ffn_up", "w_ple_proj"), ("w_ffn_down",)
    n1 = len(wave1)
    u, *o1 = _inproj(x2d, w_in_t, rider=near(wave1) if dist else None)
    u3 = u.reshape(bsz, s, IN_WIDTH)
    y_hat, y_rstd, y_ret, ret_rb, ret_kvf, *o2 = _ret_fwd(u3, lgf_l, lgb_l, gn_gain, rider=_merge_riders(
        [_gather_relay_rider(o1), near(wave2)]) if dist else None)
    y_att, att_p, att_ps, *o3 = _attn_fwd(u3, slopes, sink, rider=_merge_riders(
        [_gather_pass_rider(o2[:n1]), _gather_relay_rider(o2[n1:]), near(wave3, (1, 2, 3))]) if dist else None)
    gathered = dict(zip(wave1, o3[:n1]))
    w_out = _assemble_weights({"w_out": gathered["w_out"]})["w_out"] if dist else rest["w_out"]
    zh1, r1, hb, *o4 = _outproj_ln1(y_ret.reshape(t, RET_WIDTH), y_att.reshape(t, ATTN_WIDTH), x2d, w_out, g1, b1,
                                    rider=_gather_pass_rider(o3[n1:], sibling_id=0) if dist else None)
    gathered.update(zip(wave2 + wave3, o4))
    wts = _assemble_weights(gathered) if dist else rest
    dz2, dz2b, gs, us, acts, pg, ple, sq, dg2, db2 = _ffn_fwd(
        zh1, hb, p2d, tgt2d, g1, b1, g2, b2, wts["gate4"], wts["up4"], wts["down4"], wts["ple_proj"], wts["ple_gate"])
    dgs, dus, dsp, dple, dz1, dyr, dya, dg1, db1 = _ffn_bwd(dz2, gs, us, pg, ple, zh1, r1, g1, wts["gate4"],
                                                          wts["up4"], wts["down4"], wts["ple_gate"], wts["w_out"])
    ffn_parts = list(_wgrad_ffn(acts, dgs, dus, hb, dz2b))
    d_w_out, d_ple_gate, d_ple_proj, *th_ffn = _wgrad_misc(
        y_ret.reshape(t, RET_WIDTH), y_att.reshape(t, ATTN_WIDTH), dz1, hb, dsp, p2d, dple,
        rider=_exchange_halves_rider(ffn_parts[:2], sibling_id=1) if dist else None)
    misc_parts = [d_w_out.reshape(N_SHARD, D_MODEL // N_SHARD, D_MODEL), d_ple_proj,
                  d_ple_gate.reshape(N_SHARD, D_MODEL // N_SHARD, D_MODEL)]
    dyr3, dya3 = dyr.reshape(bsz, s, RET_WIDTH), dya.reshape(bsz, s, ATTN_WIDTH)
    if dist:
        s_gu = _add_halves(ffn_parts[:2], th_ffn, pos)
        half = FFN_SHARD // 2
        quarter = half // 2
        later_parts = [ffn_parts[2]] + misc_parts
        drq, drk, drv, drg, rpart, *o5 = _ret_bwd(u3, y_hat, y_rstd, (ret_rb, ret_kvf), dyr3, lgf_l, lgb_l, gn_gain,
                                                  rider=_merge_riders(
            [_exchange_chips_rider(s_gu[2:], rows=[(0, half), (0, quarter)]), _exchange_halves_rider(later_parts)]))
        s_dm = _add_halves(later_parts, o5[2:], pos)
        daq, dakv, spart, *o6 = _attn_bwd(u3, dya3, att_p, att_ps, rider=_exchange_chips_rider(
            [s_gu[3], s_dm[4]], rows=[(quarter, half - quarter), (0, half)], into=[o5[1], None], chips_id=3))
    else:
        drq, drk, drv, drg, rpart = _ret_bwd(u3, y_hat, y_rstd, (ret_rb, ret_kvf), dyr3, lgf_l, lgb_l, gn_gain)
        daq, dakv, spart = _attn_bwd(u3, dya3, att_p, att_ps)
    pieces = [a.reshape(t, -1) for a in (drq, drk, drv, drg, daq, dakv)]
    d_in, *o7 = _wgrad_in(pieces, x2d, rider=_exchange_chips_rider(list(s_dm[5:]), chips_id=4) if dist else None)
    d_in = d_in.reshape(N_SHARD, FFN_SHARD, D_MODEL)

    rsum = rpart
    lane_heads = lambda row: jnp.sum(row.reshape(4, 2, HEAD_DIM), axis=-1).reshape(8)
    dlg_f = lane_heads(rsum[:, 0, :]) + jnp.stack([jnp.sum(rsum[:, 2, :], -1), jnp.sum(rsum[:, 3, :], -1)], 1).reshape(8)
    dlg_b = lane_heads(rsum[:, 1, :]) + jnp.stack([jnp.sum(rsum[:, 4, :], -1), jnp.sum(rsum[:, 5, :], -1)], 1).reshape(8)
    chain = lambda d: -(math.log(2.0) * jnp.exp2(d)) / (1.0 - jnp.exp2(d))
    grads_small = {
        "ret_decay_fwd": (dlg_f * chain(dec_f)).reshape(1, 8),
        "ret_decay_bwd": (dlg_b * chain(dec_b)).reshape(1, 8),
        "attn_sink": jnp.sum(spart, axis=0)[:, 0:4, 0].reshape(1, 8),
        "ret_gn_gain": rsum[:, 6, :].reshape(1, RET_WIDTH),
        "ln1_gain": dg1, "ln1_bias": db1, "ln2_gain": dg2, "ln2_bias": db2,
    }
    if not dist:
        grad_x, = _inproj_bwd(dz1, pieces, w_in_t)
        grads_rest = [misc_parts[0]] + ffn_parts + misc_parts[1:]
        return sq[0, 0], grad_x.reshape(bsz, s, D_MODEL), d_in, grads_rest, grads_small
    *small_out, th_in = _small_allreduce_adamw(_pack_small(grads_small, sq[0, 0]), *small_state,
                                               rider=_exchange_halves_rider([d_in]))
    s_in = _add_halves([d_in], [th_in], pos)
    grad_x, chips_in = _inproj_bwd(dz1, pieces, w_in_t, rider=_exchange_chips_rider([s_in[1]], chips_id=5))
    sums32 = [s_in[0], s_dm[1], s_gu[0], s_gu[1], s_dm[0], s_dm[2], s_dm[3]]
    from_chips = [chips_in, o7[0], o5[0], o6[0], o6[1], o7[1], o7[2]]
    return grad_x.reshape(bsz, s, D_MODEL), sums32, from_chips, small_out


BIG_NAMES = ("w_in", "w_out", "w_ffn_gate", "w_ffn_up", "w_ffn_down", "w_ple_proj", "w_ple_gate")
REST_NAMES = BIG_NAMES[1:]
TRANSPOSED = ("w_in", "w_ffn_gate", "w_ffn_up")
WEIGHT_ORDER = ("w_in", "ret_decay_fwd", "ret_decay_bwd", "ret_gn_gain", "attn_sink", "w_out", "ln1_gain",
                "ln1_bias", "w_ffn_gate", "w_ffn_up", "w_ffn_down", "w_ple_proj", "w_ple_gate", "ln2_gain", "ln2_bias")


def _shard_rows(name, a):
    return jnp.swapaxes(a[0], 0, 1) if name in TRANSPOSED else a[0]


def _unshard_rows(name, a):
    return (jnp.swapaxes(a, 0, 1) if name in TRANSPOSED else a)[None]


def _assemble_weights(gathered):
    rows = lambda a: a.reshape(N_SHARD * a.shape[1], a.shape[2])
    same = lambda a: a
    layout = {"w_out": ("w_out", rows), "w_ffn_gate": ("gate4", same), "w_ffn_up": ("up4", same),
              "w_ffn_down": ("down4", same), "w_ple_proj": ("ple_proj", same), "w_ple_gate": ("ple_gate", rows)}
    return {layout[n][0]: layout[n][1](a) for n, a in gathered.items()}


def kernel(x, p, w_in, ret_decay_fwd, ret_decay_bwd, ret_gn_gain, attn_sink, w_out, ln1_gain, ln1_bias, w_ffn_gate, w_ffn_up, w_ffn_down, w_ple_proj, w_ple_gate, ln2_gain, ln2_bias, loss_target, m_w_in, m_ret_decay_fwd, m_ret_decay_bwd, m_ret_gn_gain, m_attn_sink, m_w_out, m_ln1_gain, m_ln1_bias, m_w_ffn_gate, m_w_ffn_up, m_w_ffn_down, m_w_ple_proj, m_w_ple_gate, m_ln2_gain, m_ln2_bias, v_w_in, v_ret_decay_fwd, v_ret_decay_bwd, v_ret_gn_gain, v_attn_sink, v_w_out, v_ln1_gain, v_ln1_bias, v_w_ffn_gate, v_w_ffn_up, v_w_ffn_down, v_w_ple_proj, v_w_ple_gate, v_ln2_gain, v_ln2_bias):
    w = dict(w_in=w_in, ret_decay_fwd=ret_decay_fwd, ret_decay_bwd=ret_decay_bwd, ret_gn_gain=ret_gn_gain,
             attn_sink=attn_sink, w_out=w_out, ln1_gain=ln1_gain, ln1_bias=ln1_bias, w_ffn_gate=w_ffn_gate,
             w_ffn_up=w_ffn_up, w_ffn_down=w_ffn_down, w_ple_proj=w_ple_proj, w_ple_gate=w_ple_gate,
             ln2_gain=ln2_gain, ln2_bias=ln2_bias)
    m = dict(w_in=m_w_in, ret_decay_fwd=m_ret_decay_fwd, ret_decay_bwd=m_ret_decay_bwd, ret_gn_gain=m_ret_gn_gain,
             attn_sink=m_attn_sink, w_out=m_w_out, ln1_gain=m_ln1_gain, ln1_bias=m_ln1_bias, w_ffn_gate=m_w_ffn_gate,
             w_ffn_up=m_w_ffn_up, w_ffn_down=m_w_ffn_down, w_ple_proj=m_w_ple_proj, w_ple_gate=m_w_ple_gate,
             ln2_gain=m_ln2_gain, ln2_bias=m_ln2_bias)
    v = dict(w_in=v_w_in, ret_decay_fwd=v_ret_decay_fwd, ret_decay_bwd=v_ret_decay_bwd, ret_gn_gain=v_ret_gn_gain,
             attn_sink=v_attn_sink, w_out=v_w_out, ln1_gain=v_ln1_gain, ln1_bias=v_ln1_bias, w_ffn_gate=v_w_ffn_gate,
             w_ffn_up=v_w_ffn_up, w_ffn_down=v_w_ffn_down, w_ple_proj=v_w_ple_proj, w_ple_gate=v_w_ple_gate,
             ln2_gain=v_ln2_gain, ln2_bias=v_ln2_bias)
    big = lambda d: [_shard_rows(n, d[n]) for n in BIG_NAMES]
    small = lambda d: {n: d[n] for n in SMALL_NAMES}

    chip = 2 * lax.axis_index("x") + lax.axis_index("y")
    pos = jnp.stack([chip, lax.axis_index("c")]).astype(jnp.int32)

    shards = big(w)
    w_in4, rest16 = _gather_and_cast(shards[0].astype(BF16), shards[1:])
    w_in_t = w_in4.reshape(IN_WIDTH, D_MODEL)
    grad_x, sums32, from_chips, (g_s, d_s, m_s, v_s) = _local_step(
        x, p[0], loss_target, w_in_t, rest16, small(w), pos=pos,
        small_state=(_pack_small(small(w)), _pack_small(small(m)), _pack_small(small(v))))
    g_big, d_big, m_big, v_big = _adamw(big(w), _add_chips(sums32, from_chips, pos), big(m), big(v))

    def tree(bigs, smalls):
        out = {n: _unshard_rows(n, a) for n, a in zip(BIG_NAMES, bigs)}
        out.update(smalls)
        return [out[n] for n in WEIGHT_ORDER]

    smalls, loss = _unpack_small([g_s, d_s, m_s, v_s])
    return (loss.reshape(()), grad_x,
            *(a for bigs, s in zip((g_big, d_big, m_big, v_big), smalls) for a in tree(bigs, s)))
```
